```python
import math
import jax, jax.numpy as jnp
from jax import lax
import numpy as np

D_MODEL = 1024
BATCH = 8
SEQ = 2048
DEPTH = 2

HEAD_DIM = 64
BLK = 128
H_SB = 4
DIL_PATTERNS = ((128, 1), (512, 4), (2048, 16))
H_PER_DIL = 2
H_DIL = H_PER_DIL * len(DIL_PATTERNS)
H_SWA_Q = 6
H_SWA_KV = 2
SWA_WINDOW = 128
N_BUCKETS = 32
MAX_REL_DIST = 2048
N_SOFT_HEADS = H_DIL + H_SWA_Q
D_FF = 2816
RMS_EPS = 1e-6
ADA_CHUNKS = 9
IN_SPLITS = (H_SB * HEAD_DIM, H_SB * HEAD_DIM, H_SB * HEAD_DIM,
             H_DIL * HEAD_DIM, H_DIL * HEAD_DIM, H_DIL * HEAD_DIM,
             H_SWA_Q * HEAD_DIM, H_SWA_KV * HEAD_DIM, H_SWA_KV * HEAD_DIM,
             D_MODEL, D_MODEL, D_MODEL)
D_IN = sum(IN_SPLITS)

kernel_name = 'hybrid_sb_dilated_swa_macaron'


def rmsnorm(x, g):
    xf = x.astype(jnp.float32)
    y = xf * lax.rsqrt(jnp.mean(xf * xf, axis=-1, keepdims=True) + RMS_EPS)
    return (y * g.astype(jnp.float32)).astype(x.dtype)


def swiglu(h, wg, wu, wd):
    return (jax.nn.silu(h @ wg) * (h @ wu)) @ wd


def t5_bucket(n):
    max_exact = N_BUCKETS // 2
    nf = jnp.maximum(n, 1).astype(jnp.float32)
    large = max_exact + (jnp.log(nf / max_exact) / math.log(MAX_REL_DIST / max_exact)
                         * (N_BUCKETS - max_exact)).astype(jnp.int32)
    large = jnp.minimum(large, N_BUCKETS - 1)
    return jnp.where(n < max_exact, n, large)


def band_bias(table_cols, dilation):
    rel = jnp.arange(BLK)[:, None] + BLK - jnp.arange(2 * BLK)[None, :]
    b = t5_bucket(jnp.maximum(rel, 0) * dilation)
    return jnp.transpose(table_cols[b], (2, 0, 1)).astype(jnp.float32)


def banded_attention(q, k, v, bias, max_dist, sinks=None):
    N, L, Hk, G, Dh = q.shape
    nb = L // BLK
    qb = q.reshape(N, nb, BLK, Hk, G, Dh).astype(jnp.float32)

    def band(t):
        tb = t.reshape(N, nb, BLK, Hk, Dh).astype(jnp.float32)
        prev = jnp.pad(tb, ((0, 0), (1, 0), (0, 0), (0, 0), (0, 0)))[:, :-1]
        return jnp.concatenate([prev, tb], axis=2)

    kk, vv = band(k), band(v)
    s = jnp.einsum('nbqhgd,nbkhd->nbhgqk', qb, kk) * (Dh ** -0.5) + bias
    rel = jnp.arange(BLK)[:, None] + BLK - jnp.arange(2 * BLK)[None, :]
    in_band = (rel >= 0) & (rel <= max_dist)
    key_pos = jnp.arange(nb)[:, None] * BLK - BLK + jnp.arange(2 * BLK)[None, :]
    mask = in_band[None] & (key_pos >= 0)[:, None, :]
    s = jnp.where(mask[None, :, None, None], s, -jnp.inf)
    m = jnp.max(s, axis=-1)
    if sinks is not None:
        sk = sinks.astype(jnp.float32)[None, None, :, :, None]
        m = jnp.maximum(m, sk)
    p = jnp.exp(s - m[..., None])
    denom = jnp.sum(p, axis=-1)
    if sinks is not None:
        denom = denom + jnp.exp(sk - m)
    o = jnp.einsum('nbhgqk,nbkhd->nbqhgd', p, vv) / jnp.transpose(denom, (0, 1, 4, 2, 3))[..., None]
    lse = jnp.transpose(m + jnp.log(denom), (0, 1, 4, 2, 3))
    return o.reshape(N, L, Hk, G, Dh), lse.reshape(N, L, Hk, G)


def stick_breaking_mixer(q, k, v):
    Bn, S, H, Dh = q.shape
    nb = S // BLK
    kf, vf = k.astype(jnp.float32), v.astype(jnp.float32)
    qb = q.astype(jnp.float32).reshape(Bn, nb, BLK, H, Dh).transpose(1, 0, 2, 3, 4)
    key_pos = jnp.arange(S)

    def block(args):
        qblk, i = args
        z = jnp.einsum('bqhd,bkhd->bhqk', qblk, kf) * (Dh ** -0.5)
        q_pos = i * BLK + jnp.arange(BLK)
        before = key_pos[None, :] < q_pos[:, None]
        log_fail = jnp.where(before, jax.nn.log_sigmoid(-z), 0.0)
        between = lax.cumsum(log_fail, axis=3, reverse=True) - log_fail
        w = jnp.where(before, jnp.exp(jax.nn.log_sigmoid(z) + between), 0.0)
        return jnp.einsum('bhqk,bkhd->bqhd', w, vf)

    o = lax.map(block, (qb, jnp.arange(nb)))
    return o.transpose(1, 0, 2, 3, 4).reshape(Bn, S, H, Dh)


def dilated_mixer(q, k, v, rel_table):
    Bn, S = q.shape[:2]
    outs, lses = [], []
    for g, (w, d) in enumerate(DIL_PATTERNS):
        hs = slice(g * H_PER_DIL, (g + 1) * H_PER_DIL)
        Ls = S // d
        Lp = -(-Ls // BLK) * BLK

        def sub(t):
            t = t[:, :, hs].reshape(Bn, Ls, d, H_PER_DIL, HEAD_DIM).transpose(0, 2, 1, 3, 4)
            t = t.reshape(Bn * d, Ls, H_PER_DIL, HEAD_DIM)
            return jnp.pad(t, ((0, 0), (0, Lp - Ls), (0, 0), (0, 0)))

        bias = band_bias(rel_table[:, hs], d)[:, None]
        o, lse = banded_attention(sub(q)[:, :, :, None], sub(k), sub(v), bias, w // d)
        o = o[:, :Ls, :, 0].reshape(Bn, d, Ls, H_PER_DIL, HEAD_DIM).transpose(0, 2, 1, 3, 4)
        lse = lse[:, :Ls, :, 0].reshape(Bn, d, Ls, H_PER_DIL).transpose(0, 2, 1, 3)
        outs.append(o.reshape(Bn, S, H_PER_DIL, HEAD_DIM))
        lses.append(lse.reshape(Bn, S, H_PER_DIL))
    alpha = jax.nn.softmax(jnp.stack(lses), axis=0)
    return jnp.sum(alpha[..., None] * jnp.stack(outs), axis=0)


def swa_mixer(q, k, v, rel_table, sinks):
    Bn, S = q.shape[:2]
    G = H_SWA_Q // H_SWA_KV
    bias = band_bias(rel_table[:, H_DIL:], 1).reshape(H_SWA_KV, G, BLK, 2 * BLK)
    o, _ = banded_attention(q.reshape(Bn, S, H_SWA_KV, G, HEAD_DIM), k, v, bias,
                            SWA_WINDOW - 1, sinks.reshape(H_SWA_KV, G))
    return o.reshape(Bn, S, H_SWA_Q, HEAD_DIM)


def _fwd_setup_inputs(seed: int = 0) -> dict:
    key = jax.random.key(seed)
    ks = jax.random.split(key, 18)
    f32 = jnp.float32
    nrm = lambda k, shape, scale: jax.random.normal(k, shape, f32) * scale
    return {
        'x': nrm(ks[0], (BATCH, SEQ, D_MODEL), 1.0),
        'c': nrm(ks[1], (BATCH, D_MODEL), 1.0),
        'w_ada': nrm(ks[2], (DEPTH, D_MODEL, ADA_CHUNKS * D_MODEL), 0.5 * D_MODEL ** -0.5),
        'b_ada': nrm(ks[3], (DEPTH, ADA_CHUNKS * D_MODEL), 0.02),
        'norm_gain': 1.0 + nrm(ks[4], (DEPTH, 3, D_MODEL), 0.05),
        'w_ffn_gate': nrm(ks[5], (DEPTH, 2, D_MODEL, D_FF), D_MODEL ** -0.5),
        'w_ffn_up': nrm(ks[6], (DEPTH, 2, D_MODEL, D_FF), D_MODEL ** -0.5),
        'w_ffn_down': nrm(ks[7], (DEPTH, 2, D_FF, D_MODEL), D_FF ** -0.5),
        'w_in': nrm(ks[8], (DEPTH, D_MODEL, D_IN), D_MODEL ** -0.5),
        'w_br_sb': nrm(ks[9], (DEPTH, H_SB * HEAD_DIM, D_MODEL), (H_SB * HEAD_DIM) ** -0.5),
        'w_br_dil': nrm(ks[10], (DEPTH, H_PER_DIL * HEAD_DIM, D_MODEL), (H_PER_DIL * HEAD_DIM) ** -0.5),
        'w_br_swa': nrm(ks[11], (DEPTH, H_SWA_Q * HEAD_DIM, D_MODEL), (H_SWA_Q * HEAD_DIM) ** -0.5),
        'w_out': nrm(ks[12], (DEPTH, D_MODEL, D_MODEL), D_MODEL ** -0.5),
        'sinks': nrm(ks[13], (DEPTH, H_SWA_Q), 0.5),
        'rel_bias': nrm(ks[14], (N_BUCKETS, N_SOFT_HEADS), 0.5),
        'final_gain': 1.0 + nrm(ks[15], (D_MODEL,), 0.05),
    }


def _fwd_reference(x, c, w_ada, b_ada, norm_gain, w_ffn_gate, w_ffn_up, w_ffn_down, w_in,
              w_br_sb, w_br_dil, w_br_swa, w_out, sinks, rel_bias, final_gain):
    Bn, S, _ = x.shape
    split_idx = np.cumsum(IN_SPLITS)[:-1].tolist()
    heads = lambda t: t.reshape(Bn, S, -1, HEAD_DIM)
    for l in range(DEPTH):
        mod = (jax.nn.silu(c) @ w_ada[l] + b_ada[l]).reshape(Bn, 3, 3, D_MODEL)[:, :, :, None, :]

        def pre(xx, j):
            return rmsnorm(xx, norm_gain[l, j]) * (1 + mod[:, j, 1]) + mod[:, j, 0]

        h = pre(x, 0)
        x = x + 0.5 * mod[:, 0, 2] * swiglu(h, w_ffn_gate[l, 0], w_ffn_up[l, 0], w_ffn_down[l, 0])

        h = pre(x, 1)
        (q_sb, k_sb, v_sb, q_dil, k_dil, v_dil, q_swa, k_swa, v_swa,
         g_sb, g_dil, g_swa) = jnp.split(h @ w_in[l], split_idx, axis=-1)
        o_sb = stick_breaking_mixer(heads(q_sb), heads(k_sb), heads(v_sb)).reshape(Bn, S, -1).astype(x.dtype)
        o_dil = dilated_mixer(heads(q_dil), heads(k_dil), heads(v_dil), rel_bias).reshape(Bn, S, -1).astype(x.dtype)
        o_swa = swa_mixer(heads(q_swa), heads(k_swa), heads(v_swa), rel_bias, sinks[l]).reshape(Bn, S, -1).astype(x.dtype)
        merged = (jax.nn.sigmoid(g_sb) * (o_sb @ w_br_sb[l])
                  + jax.nn.sigmoid(g_dil) * (o_dil @ w_br_dil[l])
                  + jax.nn.sigmoid(g_swa) * (o_swa @ w_br_swa[l]))
        x = x + mod[:, 1, 2] * (merged @ w_out[l])

        h = pre(x, 2)
        x = x + 0.5 * mod[:, 2, 2] * swiglu(h, w_ffn_gate[l, 1], w_ffn_up[l, 1], w_ffn_down[l, 1])
    return rmsnorm(x, final_gain)


import jax as _jax
import jax.numpy as _jnp

TWIN_FORMAT = 'train_step'
FWD_PARAMS = ['x', 'c', 'w_ada', 'b_ada', 'norm_gain', 'w_ffn_gate', 'w_ffn_up', 'w_ffn_down', 'w_in', 'w_br_sb', 'w_br_dil', 'w_br_swa', 'w_out', 'sinks', 'rel_bias', 'final_gain']
TWIN_WEIGHTS = ['w_ada', 'b_ada', 'norm_gain', 'w_ffn_gate', 'w_ffn_up', 'w_ffn_down', 'w_in', 'w_br_sb', 'w_br_dil', 'w_br_swa', 'w_out', 'sinks', 'rel_bias', 'final_gain']
TWIN_DIFF_INPUT = 'x'
TWIN_INPUTS = ['x', 'c', 'w_ada', 'b_ada', 'norm_gain', 'w_ffn_gate', 'w_ffn_up', 'w_ffn_down', 'w_in', 'w_br_sb', 'w_br_dil', 'w_br_swa', 'w_out', 'sinks', 'rel_bias', 'final_gain', 'loss_target', 'm_w_ada', 'm_b_ada', 'm_norm_gain', 'm_w_ffn_gate', 'm_w_ffn_up', 'm_w_ffn_down', 'm_w_in', 'm_w_br_sb', 'm_w_br_dil', 'm_w_br_swa', 'm_w_out', 'm_sinks', 'm_rel_bias', 'm_final_gain', 'v_w_ada', 'v_b_ada', 'v_norm_gain', 'v_w_ffn_gate', 'v_w_ffn_up', 'v_w_ffn_down', 'v_w_in', 'v_w_br_sb', 'v_w_br_dil', 'v_w_br_swa', 'v_w_out', 'v_sinks', 'v_rel_bias', 'v_final_gain']
TWIN_OUTPUTS = ['loss', 'grad_x', 'grad_w_ada', 'grad_b_ada', 'grad_norm_gain', 'grad_w_ffn_gate', 'grad_w_ffn_up', 'grad_w_ffn_down', 'grad_w_in', 'grad_w_br_sb', 'grad_w_br_dil', 'grad_w_br_swa', 'grad_w_out', 'grad_sinks', 'grad_rel_bias', 'grad_final_gain', 'delta_w_ada', 'delta_b_ada', 'delta_norm_gain', 'delta_w_ffn_gate', 'delta_w_ffn_up', 'delta_w_ffn_down', 'delta_w_in', 'delta_w_br_sb', 'delta_w_br_dil', 'delta_w_br_swa', 'delta_w_out', 'delta_sinks', 'delta_rel_bias', 'delta_final_gain', 'new_m_w_ada', 'new_m_b_ada', 'new_m_norm_gain', 'new_m_w_ffn_gate', 'new_m_w_ffn_up', 'new_m_w_ffn_down', 'new_m_w_in', 'new_m_w_br_sb', 'new_m_w_br_dil', 'new_m_w_br_swa', 'new_m_w_out', 'new_m_sinks', 'new_m_rel_bias', 'new_m_final_gain', 'new_v_w_ada', 'new_v_b_ada', 'new_v_norm_gain', 'new_v_w_ffn_gate', 'new_v_w_ffn_up', 'new_v_w_ffn_down', 'new_v_w_in', 'new_v_w_br_sb', 'new_v_w_br_dil', 'new_v_w_br_swa', 'new_v_w_out', 'new_v_sinks', 'new_v_rel_bias', 'new_v_final_gain']
TWIN_LEAF_KINDS = {'loss': 'loss', 'grad_x': 'grad_x', 'grad_w_ada': 'grad_w', 'grad_b_ada': 'grad_w', 'grad_norm_gain': 'grad_w', 'grad_w_ffn_gate': 'grad_w', 'grad_w_ffn_up': 'grad_w', 'grad_w_ffn_down': 'grad_w', 'grad_w_in': 'grad_w', 'grad_w_br_sb': 'grad_w', 'grad_w_br_dil': 'grad_w', 'grad_w_br_swa': 'grad_w', 'grad_w_out': 'grad_w', 'grad_sinks': 'grad_w', 'grad_rel_bias': 'grad_w', 'grad_final_gain': 'grad_w', 'delta_w_ada': 'delta_w', 'delta_b_ada': 'delta_w', 'delta_norm_gain': 'delta_w', 'delta_w_ffn_gate': 'delta_w', 'delta_w_ffn_up': 'delta_w', 'delta_w_ffn_down': 'delta_w', 'delta_w_in': 'delta_w', 'delta_w_br_sb': 'delta_w', 'delta_w_br_dil': 'delta_w', 'delta_w_br_swa': 'delta_w', 'delta_w_out': 'delta_w', 'delta_sinks': 'delta_w', 'delta_rel_bias': 'delta_w', 'delta_final_gain': 'delta_w', 'new_m_w_ada': 'new_m', 'new_m_b_ada': 'new_m', 'new_m_norm_gain': 'new_m', 'new_m_w_ffn_gate': 'new_m', 'new_m_w_ffn_up': 'new_m', 'new_m_w_ffn_down': 'new_m', 'new_m_w_in': 'new_m', 'new_m_w_br_sb': 'new_m', 'new_m_w_br_dil': 'new_m', 'new_m_w_br_swa': 'new_m', 'new_m_w_out': 'new_m', 'new_m_sinks': 'new_m', 'new_m_rel_bias': 'new_m', 'new_m_final_gain': 'new_m', 'new_v_w_ada': 'new_v', 'new_v_b_ada': 'new_v', 'new_v_norm_gain': 'new_v', 'new_v_w_ffn_gate': 'new_v', 'new_v_w_ffn_up': 'new_v', 'new_v_w_ffn_down': 'new_v', 'new_v_w_in': 'new_v', 'new_v_w_br_sb': 'new_v', 'new_v_w_br_dil': 'new_v', 'new_v_w_br_swa': 'new_v', 'new_v_w_out': 'new_v', 'new_v_sinks': 'new_v', 'new_v_rel_bias': 'new_v', 'new_v_final_gain': 'new_v'}


def _forward(args):
    return _fwd_reference(*[args[k] for k in FWD_PARAMS])


def _output_shape():
    out = _jax.eval_shape(lambda: _forward(_fwd_setup_inputs(0)))
    return out.shape, out.dtype

N_MICROBATCH = 1
ADAM_LR = 0.001
ADAM_B1 = 0.9
ADAM_B2 = 0.999
ADAM_EPS = 1e-08
ADAM_WD = 0.01
ADAM_STEP = 10
PER_EXAMPLE_BATCH_AXIS = {'x': 0, 'c': 0, 'loss_target': 0}
SHARED_INPUTS = []
_WEIGHT_DTYPES = {'w_ada': _jnp.float32, 'b_ada': _jnp.float32, 'norm_gain': _jnp.float32, 'w_ffn_gate': _jnp.float32, 'w_ffn_up': _jnp.float32, 'w_ffn_down': _jnp.float32, 'w_in': _jnp.float32, 'w_br_sb': _jnp.float32, 'w_br_dil': _jnp.float32, 'w_br_swa': _jnp.float32, 'w_out': _jnp.float32, 'sinks': _jnp.float32, 'rel_bias': _jnp.float32, 'final_gain': _jnp.float32}
MOMENT_SCALE = {'w_ada': 2.374981e-02, 'b_ada': 3.841357e-02, 'norm_gain': 2.041682e-02, 'w_ffn_gate': 8.647720e-03, 'w_ffn_up': 8.390952e-03, 'w_ffn_down': 1.390826e-02, 'w_in': 1.074553e-02, 'w_br_sb': 1.634600e-02, 'w_br_dil': 7.163349e-03, 'w_br_swa': 8.369482e-03, 'w_out': 1.917748e-02, 'sinks': 7.036596e-03, 'rel_bias': 1.294843e-02, 'final_gain': 1.601058e+01}


def _to_microbatches(a, axis):
    t = _jnp.moveaxis(a, axis, 0)
    t = t.reshape((N_MICROBATCH, t.shape[0] // N_MICROBATCH) + t.shape[1:])
    return _jnp.moveaxis(t, 1, axis + 1)


def setup_inputs(seed: int = 0) -> dict:
    inp = _fwd_setup_inputs(seed)
    key = _jax.random.fold_in(_jax.random.key(seed), 7919)
    shape, _ = _output_shape()
    out = dict(inp)
    out["loss_target"] = _jax.random.normal(_jax.random.fold_in(key, 0), shape, _jnp.float32)
    for i, name in enumerate(TWIN_WEIGHTS):
        w = inp[name].astype(_jnp.float32)
        if MOMENT_SCALE is None:
            s = _jnp.sqrt(_jnp.mean(_jnp.square(w)) + 1e-30)
        else:
            s = MOMENT_SCALE[name]
        km, kv = _jax.random.split(_jax.random.fold_in(key, i + 1))
        out[name] = w
        out["m_" + name] = s * _jax.random.normal(km, w.shape, _jnp.float32)
        out["v_" + name] = (s * s) * _jax.random.uniform(kv, w.shape, _jnp.float32, 0.5, 1.5)
    if N_MICROBATCH > 1:
        for name, axis in PER_EXAMPLE_BATCH_AXIS.items():
            out[name] = _to_microbatches(out[name], axis)
    return {'x': out['x'], 'c': out['c'], 'w_ada': out['w_ada'], 'b_ada': out['b_ada'], 'norm_gain': out['norm_gain'], 'w_ffn_gate': out['w_ffn_gate'], 'w_ffn_up': out['w_ffn_up'], 'w_ffn_down': out['w_ffn_down'], 'w_in': out['w_in'], 'w_br_sb': out['w_br_sb'], 'w_br_dil': out['w_br_dil'], 'w_br_swa': out['w_br_swa'], 'w_out': out['w_out'], 'sinks': out['sinks'], 'rel_bias': out['rel_bias'], 'final_gain': out['final_gain'], 'loss_target': out['loss_target'], 'm_w_ada': out['m_w_ada'], 'm_b_ada': out['m_b_ada'], 'm_norm_gain': out['m_norm_gain'], 'm_w_ffn_gate': out['m_w_ffn_gate'], 'm_w_ffn_up': out['m_w_ffn_up'], 'm_w_ffn_down': out['m_w_ffn_down'], 'm_w_in': out['m_w_in'], 'm_w_br_sb': out['m_w_br_sb'], 'm_w_br_dil': out['m_w_br_dil'], 'm_w_br_swa': out['m_w_br_swa'], 'm_w_out': out['m_w_out'], 'm_sinks': out['m_sinks'], 'm_rel_bias': out['m_rel_bias'], 'm_final_gain': out['m_final_gain'], 'v_w_ada': out['v_w_ada'], 'v_b_ada': out['v_b_ada'], 'v_norm_gain': out['v_norm_gain'], 'v_w_ffn_gate': out['v_w_ffn_gate'], 'v_w_ffn_up': out['v_w_ffn_up'], 'v_w_ffn_down': out['v_w_ffn_down'], 'v_w_in': out['v_w_in'], 'v_w_br_sb': out['v_w_br_sb'], 'v_w_br_dil': out['v_w_br_dil'], 'v_w_br_swa': out['v_w_br_swa'], 'v_w_out': out['v_w_out'], 'v_sinks': out['v_sinks'], 'v_rel_bias': out['v_rel_bias'], 'v_final_gain': out['v_final_gain']}


def _loss(weights, diff, rest, loss_target):
    with _jax.named_scope("forward"):
        args = {**rest, TWIN_DIFF_INPUT: diff, **{k: w.astype(_WEIGHT_DTYPES[k]) for k, w in weights.items()}}
        y = _forward(args)
    with _jax.named_scope("loss_head"):
        err = _jnp.square(y.astype(_jnp.float32) - loss_target)
        return 0.5 * _jnp.sum(_jnp.mean(err, axis=-1)) if err.ndim else 0.5 * err


def _adamw(w, g, m, v):
    m = ADAM_B1 * m + (1.0 - ADAM_B1) * g
    v = ADAM_B2 * v + (1.0 - ADAM_B2) * _jnp.square(g)
    m_hat = m / (1.0 - ADAM_B1 ** ADAM_STEP)
    v_hat = v / (1.0 - ADAM_B2 ** ADAM_STEP)
    delta = -ADAM_LR * (m_hat / (_jnp.sqrt(v_hat) + ADAM_EPS) + ADAM_WD * w)
    return delta, m, v


def reference(x, c, w_ada, b_ada, norm_gain, w_ffn_gate, w_ffn_up, w_ffn_down, w_in, w_br_sb, w_br_dil, w_br_swa, w_out, sinks, rel_bias, final_gain, loss_target, m_w_ada, m_b_ada, m_norm_gain, m_w_ffn_gate, m_w_ffn_up, m_w_ffn_down, m_w_in, m_w_br_sb, m_w_br_dil, m_w_br_swa, m_w_out, m_sinks, m_rel_bias, m_final_gain, v_w_ada, v_b_ada, v_norm_gain, v_w_ffn_gate, v_w_ffn_up, v_w_ffn_down, v_w_in, v_w_br_sb, v_w_br_dil, v_w_br_swa, v_w_out, v_sinks, v_rel_bias, v_final_gain):
    given = dict(x=x, c=c, w_ada=w_ada, b_ada=b_ada, norm_gain=norm_gain, w_ffn_gate=w_ffn_gate, w_ffn_up=w_ffn_up, w_ffn_down=w_ffn_down, w_in=w_in, w_br_sb=w_br_sb, w_br_dil=w_br_dil, w_br_swa=w_br_swa, w_out=w_out, sinks=sinks, rel_bias=rel_bias, final_gain=final_gain, loss_target=loss_target, m_w_ada=m_w_ada, m_b_ada=m_b_ada, m_norm_gain=m_norm_gain, m_w_ffn_gate=m_w_ffn_gate, m_w_ffn_up=m_w_ffn_up, m_w_ffn_down=m_w_ffn_down, m_w_in=m_w_in, m_w_br_sb=m_w_br_sb, m_w_br_dil=m_w_br_dil, m_w_br_swa=m_w_br_swa, m_w_out=m_w_out, m_sinks=m_sinks, m_rel_bias=m_rel_bias, m_final_gain=m_final_gain, v_w_ada=v_w_ada, v_b_ada=v_b_ada, v_norm_gain=v_norm_gain, v_w_ffn_gate=v_w_ffn_gate, v_w_ffn_up=v_w_ffn_up, v_w_ffn_down=v_w_ffn_down, v_w_in=v_w_in, v_w_br_sb=v_w_br_sb, v_w_br_dil=v_w_br_dil, v_w_br_swa=v_w_br_swa, v_w_out=v_w_out, v_sinks=v_sinks, v_rel_bias=v_rel_bias, v_final_gain=v_final_gain)
    weights = {n: given[n] for n in TWIN_WEIGHTS}
    shared = {n: given[n] for n in SHARED_INPUTS}
    per_example = {n: given[n] for n in ['x', 'c']}
    grad_fn = _jax.value_and_grad(_loss, argnums=(0, 1))

    def one_microbatch(ex, loss_target):
        ex = dict(ex)
        diff = ex.pop(TWIN_DIFF_INPUT)
        return grad_fn(weights, diff, {**shared, **ex}, loss_target)

    if N_MICROBATCH == 1:
        loss, (grad_w, grad_x) = one_microbatch(per_example, given["loss_target"])
    else:
        def body(carry, xs):
            loss_sum, grad_sum = carry
            l_k, (gw_k, gx_k) = one_microbatch(xs[0], xs[1])
            with _jax.named_scope("update"):
                return (loss_sum + l_k, _jax.tree.map(_jnp.add, grad_sum, gw_k)), gx_k

        init = (_jnp.zeros((), _jnp.float32), _jax.tree.map(_jnp.zeros_like, weights))
        (loss, grad_w), grad_x = _jax.lax.scan(body, init, (per_example, given["loss_target"]))
    with _jax.named_scope("update"):
        delta_w, new_m, new_v = {}, {}, {}
        for n in TWIN_WEIGHTS:
            delta_w[n], new_m[n], new_v[n] = _adamw(weights[n], grad_w[n], given["m_" + n], given["v_" + n])
    return (loss, grad_x, *[grad_w[n] for n in TWIN_WEIGHTS], *[delta_w[n] for n in TWIN_WEIGHTS],
            *[new_m[n] for n in TWIN_WEIGHTS], *[new_v[n] for n in TWIN_WEIGHTS])
```

```python
import math

import numpy as np
import jax
import jax.numpy as jnp
from jax import lax
from jax.experimental import pallas as pl
from jax.experimental.pallas import tpu as pltpu

F32, BF16 = jnp.float32, jnp.bfloat16

SEQ, D_MODEL, D_FF, HEAD_DIM = 2048, 1024, 2816, 64
DEPTH = 2
BLK = 128
H_SB, H_DIL, H_SWA_Q, H_SWA_KV = 4, 6, 6, 2
DIL_PATTERNS = ((128, 1), (512, 4), (2048, 16))
SWA_WINDOW = 128
N_BUCKETS, MAX_REL_DIST = 32, 2048
RMS_EPS = 1e-6
D_QKV = 2560
D_GATES = 3 * D_MODEL
QKV_SPLITS = (256, 256, 256, 384, 384, 384, 384, 128, 128)
ADAM_LR, ADAM_B1, ADAM_B2, ADAM_EPS, ADAM_WD, ADAM_STEP = 0.001, 0.9, 0.999, 1e-08, 0.01, 10

N_DEV = 8
LANES = 128
NEG = -1e30
SB_TILE = 256
VMEM_LIMIT_BYTES = 48 * 1024 * 1024
HBM = pl.BlockSpec(memory_space=pltpu.HBM)
MESH = pl.DeviceIdType.MESH


def _tile(n, target):
    t = (min(n, target) // LANES) * LANES
    while t >= LANES:
        if n % t == 0:
            return t
        t -= LANES
    return n


def _params(semantics=None):
    return pltpu.CompilerParams(dimension_semantics=semantics, vmem_limit_bytes=VMEM_LIMIT_BYTES)


def _dot(a, b, ca, cb):
    return lax.dot_general(a, b, (((ca,), (cb,)), ((), ())), preferred_element_type=F32)


def _sigmoid(a):
    return 1.0 / (1.0 + jnp.exp(-a))


def _row(v):
    return v.reshape(1, -1)


def _all_gather(arrs, name):
    n = len(arrs)

    def body(*refs):
        x_refs, out_refs = refs[:n], refs[n:2 * n]
        send_sems, recv_sems, local_sems = refs[2 * n:]
        x, y, c = lax.axis_index("x"), lax.axis_index("y"), lax.axis_index("c")
        me, sibling = (x, y, c), (x, y, 1 - c)
        chips = [(1 - x, y), (x, 1 - y), (1 - x, 1 - y)]

        def slot(t, px, py, pc):
            return out_refs[t].at[4 * px + 2 * py + pc]

        def copy(t, k, block, to, src=None):
            return pltpu.make_async_remote_copy(
                src_ref=slot(t, *block) if src is None else src, dst_ref=slot(t, *block),
                send_sem=send_sems.at[7 * t + k], recv_sem=recv_sems.at[7 * t + k], device_id=to, device_id_type=MESH)

        mine = [pltpu.make_async_copy(x_refs[t], slot(t, *me), local_sems.at[t]) for t in range(n)]
        for cp in mine:
            cp.start()
        first = []
        for t in range(n):
            first.append(copy(t, 0, me, sibling, src=x_refs[t]))
            first += [copy(t, 1 + j, me, (*chip, c), src=x_refs[t]) for j, chip in enumerate(chips)]
        for cp in first:
            cp.start()
        passed = []
        for j, chip in enumerate(chips):
            for t in range(n):
                copy(t, 1 + j, (*chip, c), me).wait_recv()
                passed.append(copy(t, 4 + j, (*chip, c), sibling))
                passed[-1].start()
        for t in range(n):
            copy(t, 0, sibling, me).wait_recv()
        for j, chip in enumerate(chips):
            for t in range(n):
                copy(t, 4 + j, (*chip, 1 - c), me).wait_recv()
        for cp in first + passed:
            cp.wait_send()
        for cp in mine:
            cp.wait()

    return pl.pallas_call(
        body, name=name, out_shape=[jax.ShapeDtypeStruct((N_DEV,) + a.shape, a.dtype) for a in arrs],
        in_specs=[HBM] * n, out_specs=[HBM] * n,
        scratch_shapes=[pltpu.SemaphoreType.DMA((7 * n,)), pltpu.SemaphoreType.DMA((7 * n,)), pltpu.SemaphoreType.DMA((n,))],
    )(*arrs)


def _all_to_all(arrs, name):
    n = len(arrs)

    def body(*refs):
        x_refs, out_refs = refs[:n], refs[n:2 * n]
        send_sems, recv_sems, local_sems = refs[2 * n:]
        x, y, c = lax.axis_index("x"), lax.axis_index("y"), lax.axis_index("c")
        me = 4 * x + 2 * y + c
        mine = [pltpu.make_async_copy(x_refs[t].at[me], out_refs[t].at[me], local_sems.at[t]) for t in range(n)]
        for cp in mine:
            cp.start()
        sends, recvs = [], []
        for k in range(1, N_DEV):
            px = 1 - x if (k >> 2) & 1 else x
            py = 1 - y if (k >> 1) & 1 else y
            pc = 1 - c if k & 1 else c
            peer = 4 * px + 2 * py + pc
            for t in range(n):
                sem = 7 * t + k - 1
                sends.append(pltpu.make_async_remote_copy(
                    src_ref=x_refs[t].at[peer], dst_ref=out_refs[t].at[me], send_sem=send_sems.at[sem],
                    recv_sem=recv_sems.at[sem], device_id=(px, py, pc), device_id_type=MESH))
                recvs.append(pltpu.make_async_remote_copy(
                    src_ref=x_refs[t].at[me], dst_ref=out_refs[t].at[peer], send_sem=send_sems.at[sem],
                    recv_sem=recv_sems.at[sem], device_id=(px, py, pc), device_id_type=MESH))
        for cp in sends:
            cp.start()
        for cp in recvs:
            cp.wait_recv()
        for cp in sends:
            cp.wait_send()
        for cp in mine:
            cp.wait()

    return pl.pallas_call(
        body, name=name, out_shape=[jax.ShapeDtypeStruct(a.shape, a.dtype) for a in arrs],
        in_specs=[HBM] * n, out_specs=[HBM] * n,
        scratch_shapes=[pltpu.SemaphoreType.DMA((7 * n,)), pltpu.SemaphoreType.DMA((7 * n,)), pltpu.SemaphoreType.DMA((n,))],
    )(*arrs)


def _col_pieces(w, widths):
    pieces, lo = [], 0
    for part, width in enumerate(widths):
        for p in range(N_DEV):
            a, b = max(lo, p * w), min(lo + width, (p + 1) * w)
            if a < b:
                pieces.append((p, part, a - p * w, a - lo, b - a))
        lo += width
    return pieces


def _unshard_cols(g, widths, name):
    _, r, w = g.shape
    tr = 256
    pieces = _col_pieces(w, widths)

    def body(g_ref, *o_refs):
        for p, part, s0, d0, size in pieces:
            o_refs[part][:, d0:d0 + size] = g_ref[p, :, s0:s0 + size]

    return pl.pallas_call(
        body, name=name, out_shape=[jax.ShapeDtypeStruct((r, wd), g.dtype) for wd in widths], grid=(r // tr,),
        in_specs=[pl.BlockSpec((N_DEV, tr, w), lambda i: (0, i, 0))],
        out_specs=[pl.BlockSpec((tr, wd), lambda i: (i, 0)) for wd in widths],
        compiler_params=_params(("parallel",)),
    )(g)


def _shard_cols(groups, name):
    widths = [a.shape[1] for a in groups[0]]
    r = groups[0][0].shape[0]
    w = sum(widths) // N_DEV
    tr = 256
    steps = r // tr
    pieces = _col_pieces(w, widths)
    nparts = len(widths)
    dtype = groups[0][0].dtype

    def body(*refs):
        o_ref = refs[-1]
        gg = pl.program_id(0)
        for gi in range(len(groups)):
            @pl.when(gg == gi)
            def _(gi=gi):
                for p, part, s0, d0, size in pieces:
                    o_ref[p, :, s0:s0 + size] = refs[gi * nparts + part][:, d0:d0 + size]

    def in_spec(gi, wd):
        return pl.BlockSpec((tr, wd), lambda gg, i: (jnp.where(gg == gi, i, 0), 0))

    return pl.pallas_call(
        body, name=name, out_shape=jax.ShapeDtypeStruct((N_DEV, len(groups) * r, w), dtype), grid=(len(groups), steps),
        in_specs=[in_spec(gi, wd) for gi in range(len(groups)) for wd in widths],
        out_specs=pl.BlockSpec((N_DEV, tr, w), lambda gg, i: (0, gg * steps + i, 0)),
        compiler_params=_params(("parallel", "parallel")),
    )(*[a for grp in groups for a in grp])


def _sum_parts(parts, name):
    n, r, cdim = parts.shape
    tr = r
    for cand in (512, 256, 128, 64, 32, 16):
        if r % cand == 0 and r > cand:
            tr = cand
            break

    def body(p_ref, o_ref):
        acc = p_ref[0].astype(F32)
        for k in range(1, n):
            acc = acc + p_ref[k].astype(F32)
        o_ref[...] = acc

    return pl.pallas_call(
        body, name=name, out_shape=jax.ShapeDtypeStruct((r, cdim), F32), grid=(r // tr,),
        in_specs=[pl.BlockSpec((n, tr, cdim), lambda i: (0, i, 0))],
        out_specs=pl.BlockSpec((tr, cdim), lambda i: (i, 0)),
        compiler_params=_params(("parallel",)),
    )(parts)


def _mm(a, b, *, name, ta=False, tb=False, res=None, colscale=None, emit_acc=False,
        out_dtype=F32, tm=512, tn=512):
    m, k = (a.shape[1], a.shape[0]) if ta else a.shape
    n = b.shape[0] if tb else b.shape[1]
    tm, tn = _tile(m, tm), _tile(n, tn)
    ca, cb = (0 if ta else 1), (1 if tb else 0)
    a_spec = pl.BlockSpec((k, tm), lambda i, j: (0, i)) if ta else pl.BlockSpec((tm, k), lambda i, j: (i, 0))
    b_spec = pl.BlockSpec((tn, k), lambda i, j: (j, 0)) if tb else pl.BlockSpec((k, tn), lambda i, j: (0, j))
    tile = pl.BlockSpec((tm, tn), lambda i, j: (i, j))
    ins, in_specs = [a, b], [a_spec, b_spec]
    if res is not None:
        ins.append(res)
        in_specs.append(tile)
    if colscale is not None:
        ins.append(colscale)
        in_specs.append(pl.BlockSpec((1, tn), lambda i, j: (0, j)))
    n_in = len(ins)

    def body(*refs):
        outs = refs[n_in:]
        acc = _dot(refs[0][...].astype(BF16), refs[1][...].astype(BF16), ca, cb)
        val, p = acc, 2
        if res is not None:
            r_val, p = refs[p][...], p + 1
        if colscale is not None:
            val = val * refs[p][...]
        if res is not None:
            val = r_val + val
        if emit_acc:
            outs[0][...] = acc
        outs[-1][...] = val.astype(out_dtype)

    out_shape = [jax.ShapeDtypeStruct((m, n), out_dtype)]
    out_specs = [tile]
    if emit_acc:
        out_shape.insert(0, jax.ShapeDtypeStruct((m, n), F32))
        out_specs.insert(0, tile)
    out = pl.pallas_call(
        body, name=name, out_shape=out_shape, grid=(m // tm, n // tn), in_specs=in_specs, out_specs=out_specs,
        compiler_params=_params(("parallel", "parallel")),
    )(*ins)
    return out if emit_acc else out[0]


def _norm_fwd(x, g, scale, shift, name):
    s, d = x.shape
    tr = 256

    def body(x_ref, g_ref, sc_ref, sh_ref, h_ref):
        xv = x_ref[...]
        rstd = lax.rsqrt(jnp.mean(xv * xv, axis=-1, keepdims=True) + RMS_EPS)
        h_ref[...] = (xv * rstd * g_ref[...] * (1.0 + sc_ref[...]) + sh_ref[...]).astype(BF16)

    rowspec = pl.BlockSpec((1, d), lambda i: (0, 0))
    return pl.pallas_call(
        body, name=name, out_shape=jax.ShapeDtypeStruct((s, d), BF16), grid=(s // tr,),
        in_specs=[pl.BlockSpec((tr, d), lambda i: (i, 0)), rowspec, rowspec, rowspec],
        out_specs=pl.BlockSpec((tr, d), lambda i: (i, 0)),
        compiler_params=_params(("parallel",)),
    )(x, g, scale, shift)


def _norm_bwd(x, dh, dres, g, scale, name):
    s, d = x.shape
    tr = 256

    def body(x_ref, dh_ref, dr_ref, g_ref, sc_ref, dx_ref, a_ref, b_ref):
        @pl.when(pl.program_id(0) == 0)
        def _():
            a_ref[...] = jnp.zeros_like(a_ref)
            b_ref[...] = jnp.zeros_like(b_ref)

        xv = x_ref[...]
        rstd = lax.rsqrt(jnp.mean(xv * xv, axis=-1, keepdims=True) + RMS_EPS)
        xhat = xv * rstd
        dhv = dh_ref[...]
        dxhat = dhv * (g_ref[...] * (1.0 + sc_ref[...]))
        mean_term = jnp.mean(dxhat * xhat, axis=-1, keepdims=True)
        dx_ref[...] = dr_ref[...] + rstd * (dxhat - xhat * mean_term)
        a_ref[...] += jnp.sum(dhv, axis=0, keepdims=True)
        b_ref[...] += jnp.sum(dhv * xhat, axis=0, keepdims=True)

    rowspec = pl.BlockSpec((1, d), lambda i: (0, 0))
    tile = pl.BlockSpec((tr, d), lambda i: (i, 0))
    return pl.pallas_call(
        body, name=name,
        out_shape=[jax.ShapeDtypeStruct((s, d), F32), jax.ShapeDtypeStruct((1, d), F32), jax.ShapeDtypeStruct((1, d), F32)],
        grid=(s // tr,), in_specs=[tile, tile, tile, rowspec, rowspec], out_specs=[tile, rowspec, rowspec],
        compiler_params=_params(("arbitrary",)),
    )(x, dh, dres, g, scale)


def _gate_bwd(dxn, f, colscale, coef, name):
    s, d = dxn.shape
    tr = 256

    def body(dx_ref, f_ref, cs_ref, df_ref, dg_ref):
        @pl.when(pl.program_id(0) == 0)
        def _():
            dg_ref[...] = jnp.zeros_like(dg_ref)

        dxv = dx_ref[...]
        df_ref[...] = (dxv * cs_ref[...]).astype(BF16)
        dg_ref[...] += coef * jnp.sum(dxv * f_ref[...], axis=0, keepdims=True)

    rowspec = pl.BlockSpec((1, d), lambda i: (0, 0))
    tile = pl.BlockSpec((tr, d), lambda i: (i, 0))
    return pl.pallas_call(
        body, name=name, out_shape=[jax.ShapeDtypeStruct((s, d), BF16), jax.ShapeDtypeStruct((1, d), F32)],
        grid=(s // tr,), in_specs=[tile, tile, rowspec], out_specs=[tile, rowspec],
        compiler_params=_params(("arbitrary",)),
    )(dxn, f, colscale)


def _ffn_up(h, wg, wu, name):
    s, d = h.shape
    f = wg.shape[1]
    tm, tn = 512, _tile(f, 256)

    def body(h_ref, wg_ref, wu_ref, a_ref, u_ref, s_ref):
        hv = h_ref[...]
        a = _dot(hv, wg_ref[...], 1, 0)
        u = _dot(hv, wu_ref[...], 1, 0)
        a_ref[...] = a
        u_ref[...] = u
        s_ref[...] = (a * _sigmoid(a) * u).astype(BF16)

    tile = pl.BlockSpec((tm, tn), lambda i, j: (i, j))
    wspec = pl.BlockSpec((d, tn), lambda i, j: (0, j))
    return pl.pallas_call(
        body, name=name,
        out_shape=[jax.ShapeDtypeStruct((s, f), F32), jax.ShapeDtypeStruct((s, f), F32), jax.ShapeDtypeStruct((s, f), BF16)],
        grid=(s // tm, f // tn), in_specs=[pl.BlockSpec((tm, d), lambda i, j: (i, 0)), wspec, wspec],
        out_specs=[tile, tile, tile], compiler_params=_params(("parallel", "parallel")),
    )(h, wg, wu)


def _ffn_bwd_ds(df, wd, a, u, name):
    s, d = df.shape
    f = wd.shape[0]
    tm, tn = 512, _tile(f, 256)

    def body(df_ref, wd_ref, a_ref, u_ref, da_ref, du_ref):
        ds = _dot(df_ref[...], wd_ref[...], 1, 1)
        av = a_ref[...]
        sg = _sigmoid(av)
        da_ref[...] = (ds * u_ref[...] * (sg * (1.0 + av * (1.0 - sg)))).astype(BF16)
        du_ref[...] = (ds * (av * sg)).astype(BF16)

    tile = pl.BlockSpec((tm, tn), lambda i, j: (i, j))
    return pl.pallas_call(
        body, name=name, out_shape=[jax.ShapeDtypeStruct((s, f), BF16), jax.ShapeDtypeStruct((s, f), BF16)],
        grid=(s // tm, f // tn),
        in_specs=[pl.BlockSpec((tm, d), lambda i, j: (i, 0)), pl.BlockSpec((tn, d), lambda i, j: (j, 0)), tile, tile],
        out_specs=[tile, tile], compiler_params=_params(("parallel", "parallel")),
    )(df, wd, a, u)


def _merge_fwd(o_sb, o_dil, o_swa, gates, wb_sb, wb_dil, wb_swa, name):
    s = o_sb.shape[0]
    d = D_MODEL
    tm = 256

    def body(osb_ref, odl_ref, osw_ref, g_ref, wsb_ref, wdl_ref, wsw_ref, m_ref):
        acc = _sigmoid(g_ref[:, 0:d]) * _dot(osb_ref[...].astype(BF16), wsb_ref[...], 1, 0)
        acc += _sigmoid(g_ref[:, d:2 * d]) * _dot(odl_ref[...].astype(BF16), wdl_ref[...], 1, 0)
        acc += _sigmoid(g_ref[:, 2 * d:3 * d]) * _dot(osw_ref[...].astype(BF16), wsw_ref[...], 1, 0)
        m_ref[...] = acc.astype(BF16)

    def rows(w):
        return pl.BlockSpec((tm, w), lambda i: (i, 0))

    def whole(w):
        return pl.BlockSpec((w, d), lambda i: (0, 0))

    return pl.pallas_call(
        body, name=name, out_shape=jax.ShapeDtypeStruct((s, d), BF16), grid=(s // tm,),
        in_specs=[rows(256), rows(128), rows(384), rows(3 * d), whole(256), whole(128), whole(384)],
        out_specs=rows(d), compiler_params=_params(("parallel",)),
    )(o_sb, o_dil, o_swa, gates, wb_sb, wb_dil, wb_swa)


def _merge_bwd(dmerged, o_sb, o_dil, o_swa, gates, wb_sb, wb_dil, wb_swa, name):
    s = o_sb.shape[0]
    d = D_MODEL
    tm = 256

    def body(dm_ref, osb_ref, odl_ref, osw_ref, g_ref, wsb_ref, wdl_ref, wsw_ref,
             dg_ref, dosb_ref, dodl_ref, dosw_ref, dbsb_ref, dbdl_ref, dbsw_ref):
        dm = dm_ref[...]
        for idx, (o_ref, w_ref, do_ref, db_ref) in enumerate((
                (osb_ref, wsb_ref, dosb_ref, dbsb_ref), (odl_ref, wdl_ref, dodl_ref, dbdl_ref),
                (osw_ref, wsw_ref, dosw_ref, dbsw_ref))):
            w = w_ref[...]
            br = _dot(o_ref[...].astype(BF16), w, 1, 0)
            sg = _sigmoid(g_ref[:, idx * d:(idx + 1) * d])
            dbr = (dm * sg).astype(BF16)
            dg_ref[:, idx * d:(idx + 1) * d] = dm * br * (sg * (1.0 - sg))
            db_ref[...] = dbr
            do_ref[...] = _dot(dbr, w, 1, 1)

    def rows(w):
        return pl.BlockSpec((tm, w), lambda i: (i, 0))

    def whole(w):
        return pl.BlockSpec((w, d), lambda i: (0, 0))

    def shp(w, dt):
        return jax.ShapeDtypeStruct((s, w), dt)

    return pl.pallas_call(
        body, name=name,
        out_shape=[shp(3 * d, F32), shp(256, F32), shp(128, F32), shp(384, F32), shp(d, BF16), shp(d, BF16), shp(d, BF16)],
        grid=(s // tm,),
        in_specs=[rows(d), rows(256), rows(128), rows(384), rows(3 * d), whole(256), whole(128), whole(384)],
        out_specs=[rows(3 * d), rows(256), rows(128), rows(384), rows(d), rows(d), rows(d)],
        compiler_params=_params(("parallel",)),
    )(dmerged, o_sb, o_dil, o_swa, gates, wb_sb, wb_dil, wb_swa)


def _final_loss(x, target, g, name):
    s, d = x.shape
    tr = 256

    def body(x_ref, t_ref, g_ref, loss_ref, dx_ref, dg_ref):
        @pl.when(pl.program_id(0) == 0)
        def _():
            loss_ref[...] = jnp.zeros_like(loss_ref)
            dg_ref[...] = jnp.zeros_like(dg_ref)

        xv = x_ref[...]
        gv = g_ref[...]
        rstd = lax.rsqrt(jnp.mean(xv * xv, axis=-1, keepdims=True) + RMS_EPS)
        xhat = xv * rstd
        err = xhat * gv - t_ref[...]
        loss_ref[...] += 0.5 * jnp.sum(jnp.mean(err * err, axis=-1, keepdims=True))
        dy = err * (1.0 / d)
        dxhat = dy * gv
        mean_term = jnp.mean(dxhat * xhat, axis=-1, keepdims=True)
        dx_ref[...] = rstd * (dxhat - xhat * mean_term)
        dg_ref[...] += jnp.sum(dy * xhat, axis=0, keepdims=True)

    rowspec = pl.BlockSpec((1, d), lambda i: (0, 0))
    tile = pl.BlockSpec((tr, d), lambda i: (i, 0))
    return pl.pallas_call(
        body, name=name,
        out_shape=[jax.ShapeDtypeStruct((1, LANES), F32), jax.ShapeDtypeStruct((s, d), F32), jax.ShapeDtypeStruct((1, d), F32)],
        grid=(s // tr,), in_specs=[tile, tile, rowspec],
        out_specs=[pl.BlockSpec((1, LANES), lambda i: (0, 0)), tile, rowspec],
        compiler_params=_params(("arbitrary",)),
    )(x, target, g)


def _adamw(w, g, m, v, name):
    shape = w.shape
    cols = shape[-1]
    rows = int(np.prod(shape[:-1])) if len(shape) > 1 else 1
    tr = rows
    for cand in (1024, 512, 256, 128, 64, 32, 16, 8):
        if rows % cand == 0 and rows > cand and cand * cols * 4 <= (1 << 21):
            tr = cand
            break

    def body(w_ref, g_ref, m_ref, v_ref, d_ref, nm_ref, nv_ref):
        gv = g_ref[...]
        nm = ADAM_B1 * m_ref[...] + (1.0 - ADAM_B1) * gv
        nv = ADAM_B2 * v_ref[...] + (1.0 - ADAM_B2) * (gv * gv)
        m_hat = nm / (1.0 - ADAM_B1 ** ADAM_STEP)
        v_hat = nv / (1.0 - ADAM_B2 ** ADAM_STEP)
        d_ref[...] = -ADAM_LR * (m_hat / (jnp.sqrt(v_hat) + ADAM_EPS) + ADAM_WD * w_ref[...])
        nm_ref[...] = nm
        nv_ref[...] = nv

    tile = pl.BlockSpec((tr, cols), lambda i: (i, 0))
    flat = [t.reshape(rows, cols) for t in (w, g, m, v)]
    out = pl.pallas_call(
        body, name=name, out_shape=[jax.ShapeDtypeStruct((rows, cols), F32)] * 3, grid=(rows // tr,),
        in_specs=[tile] * 4, out_specs=[tile] * 3, compiler_params=_params(("parallel",)),
    )(*flat)
    return tuple(t.reshape(shape) for t in out)


def _ada_fwd(c_all, w, name):
    n = w.shape[1]

    def body(c_ref, w_ref, o_ref):
        cv = c_ref[...]
        o_ref[...] = jnp.dot(cv * _sigmoid(cv), w_ref[...], preferred_element_type=F32, precision=lax.Precision.HIGHEST)

    return pl.pallas_call(body, name=name, out_shape=jax.ShapeDtypeStruct((N_DEV, n), F32), compiler_params=_params())(c_all, w)


def _ada_bwd(c_all_t, dmod, name):
    n = dmod.shape[1]

    def body(c_ref, d_ref, o_ref):
        cv = c_ref[...]
        o_ref[...] = jnp.dot(cv * _sigmoid(cv), d_ref[...], preferred_element_type=F32, precision=lax.Precision.HIGHEST)

    return pl.pallas_call(body, name=name, out_shape=jax.ShapeDtypeStruct((D_MODEL, n), F32), compiler_params=_params())(c_all_t, dmod)


def _bucket_tables():
    rel = np.arange(BLK)[:, None] + BLK - np.arange(2 * BLK)[None, :]
    max_exact = N_BUCKETS // 2

    def bucket(n):
        nf = np.maximum(n, 1).astype(np.float32)
        large = max_exact + (np.log(nf / np.float32(max_exact)) / np.float32(math.log(MAX_REL_DIST / max_exact))
                             * np.float32(N_BUCKETS - max_exact)).astype(np.int32)
        return np.where(n < max_exact, n, np.minimum(large, N_BUCKETS - 1))

    tabs = []
    for dil, max_dist in ((1, 128), (4, 128), (16, 128), (1, SWA_WINDOW - 1)):
        in_band = (rel >= 0) & (rel <= max_dist)
        tabs.append(np.where(in_band, bucket(np.maximum(rel, 0) * dil), -1))
    return np.stack(tabs).astype(np.int32)


N_SOFT = H_DIL + H_SWA_Q


def _table_of_head(h):
    return jnp.minimum(h // 2, 3)


def _bias_build(rel_bias, tables, name):
    def body(rel_ref, t_ref, o_ref):
        h = pl.program_id(0)
        tb = t_ref[0]
        out = jnp.full((BLK, 2 * BLK), NEG, F32)
        for b in range(N_BUCKETS):
            out = jnp.where(tb == b, rel_ref[b, h], out)
        o_ref[0] = out

    return pl.pallas_call(
        body, name=name, out_shape=jax.ShapeDtypeStruct((N_SOFT, BLK, 2 * BLK), F32), grid=(N_SOFT,),
        in_specs=[pl.BlockSpec(memory_space=pltpu.SMEM),
                  pl.BlockSpec((1, BLK, 2 * BLK), lambda h: (_table_of_head(h), 0, 0))],
        out_specs=pl.BlockSpec((1, BLK, 2 * BLK), lambda h: (h, 0, 0)),
        compiler_params=_params(("parallel",)),
    )(rel_bias, tables)


def _bias_grad(dbias, tables, name):
    def body(d_ref, t_ref, o_ref):
        tb = t_ref[0]
        dv = d_ref[0]
        lane = lax.broadcasted_iota(jnp.int32, (1, LANES), 1)
        out = jnp.zeros((1, LANES), F32)
        for b in range(N_BUCKETS):
            out = jnp.where(lane == b, jnp.sum(jnp.where(tb == b, dv, 0.0)), out)
        o_ref[0] = out

    return pl.pallas_call(
        body, name=name, out_shape=jax.ShapeDtypeStruct((N_SOFT, 1, LANES), F32), grid=(N_SOFT,),
        in_specs=[pl.BlockSpec((1, BLK, 2 * BLK), lambda h: (h, 0, 0)),
                  pl.BlockSpec((1, BLK, 2 * BLK), lambda h: (_table_of_head(h), 0, 0))],
        out_specs=pl.BlockSpec((1, 1, LANES), lambda h: (h, 0, 0)),
        compiler_params=_params(("parallel",)),
    )(dbias, tables)


def _band_specs(g, bias_div):
    qspec = pl.BlockSpec((1, BLK, HEAD_DIM), lambda n, i: (n, i, 0))
    prev = pl.BlockSpec((1, BLK, HEAD_DIM), lambda n, i: (n // g, jnp.maximum(i - 1, 0), 0))
    cur = pl.BlockSpec((1, BLK, HEAD_DIM), lambda n, i: (n // g, i, 0))
    bspec = pl.BlockSpec((1, BLK, 2 * BLK), lambda n, i: (n // bias_div, 0, 0))
    sspec = pl.BlockSpec((1, 1, LANES), lambda n, i: (n, 0, 0))
    colspec = pl.BlockSpec((1, BLK, 1), lambda n, i: (n, i, 0))
    return qspec, prev, cur, bspec, sspec, colspec


def _band_scores(q_ref, kp_ref, kc_ref, b_ref, first):
    qv = q_ref[0]
    bv = b_ref[0]
    sp = _dot(qv, kp_ref[0], 1, 1) + bv[:, :BLK]
    sp = jnp.where(first, NEG, sp)
    sc = _dot(qv, kc_ref[0], 1, 1) + bv[:, BLK:]
    return sp, sc


def _band_fwd(q, k, v, bias, sink, *, g, bias_div, has_sink, name):
    nq, length, _ = q.shape

    def body(q_ref, kp_ref, kc_ref, vp_ref, vc_ref, b_ref, s_ref, o_ref, lse_ref):
        sp, sc = _band_scores(q_ref, kp_ref, kc_ref, b_ref, pl.program_id(1) == 0)
        m = jnp.maximum(jnp.max(sp, axis=1, keepdims=True), jnp.max(sc, axis=1, keepdims=True))
        if has_sink:
            sk = s_ref[0][:, :1]
            m = jnp.maximum(m, sk)
        pp, pc = jnp.exp(sp - m), jnp.exp(sc - m)
        den = jnp.sum(pp, axis=1, keepdims=True) + jnp.sum(pc, axis=1, keepdims=True)
        if has_sink:
            den = den + jnp.exp(sk - m)
        acc = _dot(pp.astype(BF16), vp_ref[0], 1, 0) + _dot(pc.astype(BF16), vc_ref[0], 1, 0)
        o_ref[0] = acc / den
        lse_ref[0] = m + jnp.log(den)

    qspec, prev, cur, bspec, sspec, colspec = _band_specs(g, bias_div)
    return pl.pallas_call(
        body, name=name,
        out_shape=[jax.ShapeDtypeStruct((nq, length, HEAD_DIM), F32), jax.ShapeDtypeStruct((nq, length, 1), F32)],
        grid=(nq, length // BLK), in_specs=[qspec, prev, cur, prev, cur, bspec, sspec],
        out_specs=[qspec, colspec], compiler_params=_params(("parallel", "parallel")),
    )(q, k, k, v, v, bias, sink)


def _band_bwd(q, k, v, bias, sink, o, lse, do, dlse, *, g, bias_div, has_sink, name):
    nq, length, _ = q.shape
    nk, nbias = nq // g, nq // bias_div

    def body(q_ref, kp_ref, kc_ref, vp_ref, vc_ref, b_ref, s_ref, o_ref, lse_ref, do_ref, dlse_ref,
             dq_ref, dk_ref, dv_ref, db_ref, dsk_ref):
        n, i = pl.program_id(0), pl.program_id(1)

        @pl.when((n % g == 0) & (i == 0))
        def _():
            dk_ref[...] = jnp.zeros_like(dk_ref)
            dv_ref[...] = jnp.zeros_like(dv_ref)

        @pl.when((n % bias_div == 0) & (i == 0))
        def _():
            db_ref[...] = jnp.zeros_like(db_ref)

        @pl.when(i == 0)
        def _():
            dsk_ref[...] = jnp.zeros_like(dsk_ref)

        sp, sc = _band_scores(q_ref, kp_ref, kc_ref, b_ref, i == 0)
        lse_v = lse_ref[0]
        pp, pc = jnp.exp(sp - lse_v), jnp.exp(sc - lse_v)
        dov = do_ref[0]
        dob = dov.astype(BF16)
        coef = dlse_ref[0] - jnp.sum(dov * o_ref[0], axis=1, keepdims=True)
        dsp = pp * (_dot(dob, vp_ref[0], 1, 1) + coef)
        dsc = pc * (_dot(dob, vc_ref[0], 1, 1) + coef)
        dspb, dscb = dsp.astype(BF16), dsc.astype(BF16)
        dq_ref[0] = (_dot(dspb, kp_ref[0], 1, 0) + _dot(dscb, kc_ref[0], 1, 0)) * (HEAD_DIM ** -0.5)
        qv = q_ref[0]
        cur = pl.ds(pl.multiple_of(i * BLK, BLK), BLK)
        prv = pl.ds(pl.multiple_of(jnp.maximum(i - 1, 0) * BLK, BLK), BLK)
        dk_ref[0, cur, :] += _dot(dscb, qv, 0, 0)
        dk_ref[0, prv, :] += _dot(dspb, qv, 0, 0)
        dv_ref[0, cur, :] += _dot(pc.astype(BF16), dob, 0, 0)
        dv_ref[0, prv, :] += _dot(pp.astype(BF16), dob, 0, 0)
        db_ref[0, :, :BLK] += dsp
        db_ref[0, :, BLK:] += dsc
        if has_sink:
            dsk_ref[0] += jnp.sum(jnp.exp(s_ref[0][:, :1] - lse_v) * coef)

    qspec, prev, cur, bspec, sspec, colspec = _band_specs(g, bias_div)
    kvfull = pl.BlockSpec((1, length, HEAD_DIM), lambda n, i: (n // g, 0, 0))
    return pl.pallas_call(
        body, name=name,
        out_shape=[jax.ShapeDtypeStruct((nq, length, HEAD_DIM), F32), jax.ShapeDtypeStruct((nk, length, HEAD_DIM), F32),
                   jax.ShapeDtypeStruct((nk, length, HEAD_DIM), F32), jax.ShapeDtypeStruct((nbias, BLK, 2 * BLK), F32),
                   jax.ShapeDtypeStruct((nq, 1, LANES), F32)],
        grid=(nq, length // BLK),
        in_specs=[qspec, prev, cur, prev, cur, bspec, sspec, qspec, colspec, qspec, colspec],
        out_specs=[qspec, kvfull, kvfull, bspec, sspec], compiler_params=_params(("arbitrary", "arbitrary")),
    )(q, k, k, v, v, bias, sink, o, lse, do, dlse)


def _dil_merge(os_, lses, dout, name):
    tr = 512
    n = len(os_)
    tile = pl.BlockSpec((1, tr, HEAD_DIM), lambda h, i: (h, i, 0))
    col = pl.BlockSpec((1, tr, 1), lambda h, i: (h, i, 0))

    def weights(l_refs):
        ls = [r[0] for r in l_refs]
        m = ls[0]
        for lv in ls[1:]:
            m = jnp.maximum(m, lv)
        es = [jnp.exp(lv - m) for lv in ls]
        den = es[0]
        for e in es[1:]:
            den = den + e
        return [e / den for e in es]

    if dout is None:
        def body(*refs):
            alphas = weights(refs[n:2 * n])
            acc = alphas[0] * refs[0][0]
            for gi in range(1, n):
                acc = acc + alphas[gi] * refs[gi][0]
            refs[2 * n][0] = acc

        return pl.pallas_call(
            body, name=name, out_shape=jax.ShapeDtypeStruct(os_[0].shape, F32), grid=(2, SEQ // tr),
            in_specs=[tile] * n + [col] * n, out_specs=tile, compiler_params=_params(("parallel", "parallel")),
        )(*os_, *lses)

    def body(*refs):
        alphas = weights(refs[n:2 * n])
        dov = refs[2 * n][0]
        outs = refs[2 * n + 1:]
        das = [jnp.sum(dov * refs[gi][0], axis=1, keepdims=True) for gi in range(n)]
        dbar = alphas[0] * das[0]
        for gi in range(1, n):
            dbar = dbar + alphas[gi] * das[gi]
        for gi in range(n):
            outs[gi][0] = alphas[gi] * dov
            outs[n + gi][0] = alphas[gi] * (das[gi] - dbar)

    return pl.pallas_call(
        body, name=name,
        out_shape=[jax.ShapeDtypeStruct(os_[0].shape, F32)] * n + [jax.ShapeDtypeStruct(lses[0].shape, F32)] * n,
        grid=(2, SEQ // tr), in_specs=[tile] * n + [col] * n + [tile], out_specs=[tile] * n + [col] * n,
        compiler_params=_params(("parallel", "parallel")),
    )(*os_, *lses, dout)


def _tri(cmp):
    r = lax.broadcasted_iota(jnp.int32, (SB_TILE, SB_TILE), 0)
    c = lax.broadcasted_iota(jnp.int32, (SB_TILE, SB_TILE), 1)
    return cmp(r, c).astype(BF16)


def _cum(x, tri, terms):
    acc, rest = None, x
    for _ in range(terms):
        part = rest.astype(BF16)
        rest = rest - part.astype(F32)
        d = _dot(part, tri, 1, 0)
        acc = d if acc is None else acc + d
    return acc


def _sb_logits(q, k_ref, j, i):
    t = SB_TILE
    ks = k_ref[0, pl.ds(pl.multiple_of(j * t, t), t), :]
    z = _dot(q, ks, 1, 1)
    rows = i * t + lax.broadcasted_iota(jnp.int32, (t, t), 0)
    cols = j * t + lax.broadcasted_iota(jnp.int32, (t, t), 1)
    mask = cols < rows
    e = jnp.exp(-jnp.abs(z))
    lf = jnp.where(mask, -(jnp.maximum(z, 0.0) + jnp.log(1.0 + e)), 0.0)
    return ks, z, e, lf, mask


def _sb_fwd(q, k, v, name):
    h, s, _ = q.shape
    t = SB_TILE

    def body(q_ref, k_ref, v_ref, o_ref, tot_ref):
        i = pl.program_id(1)
        qv = q_ref[0]
        after = _tri(lambda r, c: r > c)

        def step(jj, carry):
            right, acc = carry
            j = i - jj
            _, z, _, lf, mask = _sb_logits(qv, k_ref, j, i)
            between = right + _cum(lf, after, 3)
            w = jnp.where(mask, jnp.exp(z + lf + between), 0.0)
            vs = v_ref[0, pl.ds(pl.multiple_of(j * t, t), t), :]
            return right + jnp.sum(lf, axis=1, keepdims=True), acc + _dot(w.astype(BF16), vs, 1, 0)

        right, acc = lax.fori_loop(0, i + 1, step, (jnp.zeros((t, 1), F32), jnp.zeros((t, HEAD_DIM), F32)))
        o_ref[0] = acc
        tot_ref[0] = right

    tile = pl.BlockSpec((1, t, HEAD_DIM), lambda hh, i: (hh, i, 0))
    full = pl.BlockSpec((1, s, HEAD_DIM), lambda hh, i: (hh, 0, 0))
    return pl.pallas_call(
        body, name=name, out_shape=[jax.ShapeDtypeStruct((h, s, HEAD_DIM), F32), jax.ShapeDtypeStruct((h, s, 1), F32)],
        grid=(h, s // t), in_specs=[tile, full, full],
        out_specs=[tile, pl.BlockSpec((1, t, 1), lambda hh, i: (hh, i, 0))],
        compiler_params=_params(("parallel", "parallel")),
    )(q, k, v)


def _sb_bwd(q, k, v, tot, do, name):
    h, s, _ = q.shape
    t = SB_TILE

    def body(q_ref, k_ref, v_ref, tot_ref, do_ref, dq_ref, dk_ref, dv_ref):
        i = pl.program_id(1)

        @pl.when(i == 0)
        def _():
            dk_ref[...] = jnp.zeros_like(dk_ref)
            dv_ref[...] = jnp.zeros_like(dv_ref)

        qv = q_ref[0]
        dob = do_ref[0].astype(BF16)
        total = tot_ref[0]
        upto = _tri(lambda r, c: r <= c)
        before = _tri(lambda r, c: r < c)

        def step(j, carry):
            left, cleft, dq = carry
            ks, z, e, lf, mask = _sb_logits(qv, k_ref, j, i)
            rows = pl.ds(pl.multiple_of(j * t, t), t)
            vs = v_ref[0, rows, :]
            between = total - (left + _cum(lf, upto, 3))
            w = jnp.where(mask, jnp.exp(z + lf + between), 0.0)
            dlog = w * _dot(dob, vs, 1, 1)
            cfail = cleft + _cum(dlog, before, 2)
            sig = jnp.where(z >= 0.0, 1.0, e) / (1.0 + e)
            dz = jnp.where(mask, dlog * (1.0 - sig) - sig * cfail, 0.0).astype(BF16)
            dk_ref[0, rows, :] += _dot(dz, qv, 0, 0)
            dv_ref[0, rows, :] += _dot(w.astype(BF16), dob, 0, 0)
            return (left + jnp.sum(lf, axis=1, keepdims=True), cleft + jnp.sum(dlog, axis=1, keepdims=True),
                    dq + _dot(dz, ks, 1, 0))

        zero = jnp.zeros((t, 1), F32)
        _, _, dq = lax.fori_loop(0, i + 1, step, (zero, zero, jnp.zeros((t, HEAD_DIM), F32)))
        dq_ref[0] = dq * (HEAD_DIM ** -0.5)

    tile = pl.BlockSpec((1, t, HEAD_DIM), lambda hh, i: (hh, i, 0))
    full = pl.BlockSpec((1, s, HEAD_DIM), lambda hh, i: (hh, 0, 0))
    shp = jax.ShapeDtypeStruct((h, s, HEAD_DIM), F32)
    return pl.pallas_call(
        body, name=name, out_shape=[shp, shp, shp], grid=(h, s // t),
        in_specs=[tile, full, full, pl.BlockSpec((1, t, 1), lambda hh, i: (hh, i, 0)), tile],
        out_specs=[tile, full, full], compiler_params=_params(("arbitrary", "arbitrary")),
    )(q, k, v, tot, do)


def _heads(t):
    return t.reshape(SEQ, -1, HEAD_DIM).transpose(1, 0, 2)


def _unheads(t):
    return t.transpose(1, 0, 2).reshape(SEQ, -1)


def _to_dil(t, d):
    xdim = t.shape[-1]
    return t.reshape(2, SEQ // d, d, xdim).transpose(0, 2, 1, 3).reshape(2 * d, SEQ // d, xdim)


def _from_dil(t, d):
    xdim = t.shape[-1]
    return t.reshape(2, d, SEQ // d, xdim).transpose(0, 2, 1, 3).reshape(2, SEQ, xdim)


def _split_qkv(qkv):
    parts, off = [], 0
    for w in QKV_SPLITS:
        parts.append(qkv[:, off:off + w])
        off += w
    return parts


def _mixer_fwd(qkv, bias, sinks_l, tag):
    scale = HEAD_DIM ** -0.5
    q_sb, k_sb, v_sb, q_dl, k_dl, v_dl, q_sw, k_sw, v_sw = _split_qkv(qkv)
    hq = lambda t: _heads((t * scale).astype(BF16))
    hk = lambda t: _heads(t.astype(BF16))
    st = {}
    st["sb"] = (hq(q_sb), hk(k_sb), hk(v_sb))
    o_sb, st["sb_tot"] = _sb_fwd(*st["sb"], name=f"sb_fwd_{tag}")

    qd, kd, vd = hq(q_dl), hk(k_dl), hk(v_dl)
    no_sink = jnp.zeros((1, 1, LANES), F32)
    st["dil"], outs, lses = [], [], []
    for gi, (_, d) in enumerate(DIL_PATTERNS):
        hs = slice(2 * gi, 2 * gi + 2)
        qg, kg, vg = _to_dil(qd[hs], d), _to_dil(kd[hs], d), _to_dil(vd[hs], d)
        sink = jnp.broadcast_to(no_sink, (2 * d, 1, LANES))
        og, lg = _band_fwd(qg, kg, vg, bias[hs], sink, g=1, bias_div=d, has_sink=False, name=f"dil{gi}_fwd_{tag}")
        st["dil"].append((qg, kg, vg, sink, og, lg))
        outs.append(_from_dil(og, d))
        lses.append(_from_dil(lg, d))
    st["dil_outs"], st["dil_lses"] = outs, lses
    o_dil = _dil_merge(outs, lses, None, name=f"dil_merge_fwd_{tag}")

    sink = jnp.broadcast_to(sinks_l.reshape(H_SWA_Q, 1, 1), (H_SWA_Q, 1, LANES))
    st["swa"] = (hq(q_sw), hk(k_sw), hk(v_sw), sink)
    o_sw, l_sw = _band_fwd(*st["swa"][:3], bias[H_DIL:], sink, g=H_SWA_Q // H_SWA_KV, bias_div=1, has_sink=True,
                           name=f"swa_fwd_{tag}")
    st["swa_out"] = (o_sw, l_sw)
    return (_unheads(o_sb), _unheads(o_dil), _unheads(o_sw)), st


def _mixer_bwd(st, bias, do_sb, do_dil, do_swa, tag):
    dq_sb, dk_sb, dv_sb = _sb_bwd(*st["sb"], st["sb_tot"], _heads(do_sb), name=f"sb_bwd_{tag}")

    dmerge = _dil_merge(st["dil_outs"], st["dil_lses"], _heads(do_dil), name=f"dil_merge_bwd_{tag}")
    dqs, dks, dvs, dbs = [], [], [], []
    for gi, (_, d) in enumerate(DIL_PATTERNS):
        qg, kg, vg, sink, og, lg = st["dil"][gi]
        hs = slice(2 * gi, 2 * gi + 2)
        dq, dk, dv, db, _ = _band_bwd(qg, kg, vg, bias[hs], sink, og, lg, _to_dil(dmerge[gi], d), _to_dil(dmerge[3 + gi], d),
                                      g=1, bias_div=d, has_sink=False, name=f"dil{gi}_bwd_{tag}")
        dqs.append(_from_dil(dq, d))
        dks.append(_from_dil(dk, d))
        dvs.append(_from_dil(dv, d))
        dbs.append(db)

    q_sw, k_sw, v_sw, sink = st["swa"]
    o_sw, l_sw = st["swa_out"]
    dq_sw, dk_sw, dv_sw, db_sw, dsink = _band_bwd(q_sw, k_sw, v_sw, bias[H_DIL:], sink, o_sw, l_sw, _heads(do_swa),
                                                  jnp.zeros_like(l_sw), g=H_SWA_Q // H_SWA_KV, bias_div=1, has_sink=True,
                                                  name=f"swa_bwd_{tag}")
    dqkv = jnp.concatenate(
        [_unheads(dq_sb), _unheads(dk_sb), _unheads(dv_sb),
         _unheads(jnp.concatenate(dqs, 0)), _unheads(jnp.concatenate(dks, 0)), _unheads(jnp.concatenate(dvs, 0)),
         _unheads(dq_sw), _unheads(dk_sw), _unheads(dv_sw)], axis=1)
    return dqkv, jnp.concatenate(dbs + [db_sw], 0), dsink[:, 0, 0]


def _layer_fwd(x0, mod, gains, w, bias, sinks_l, tag):
    st = {"x0": x0}
    st["h0"] = _norm_fwd(x0, _row(gains[0]), _row(mod[0, 1]), _row(mod[0, 0]), name=f"norm0_fwd_{tag}")
    st["a0"], st["u0"], st["s0"] = _ffn_up(st["h0"], w["gate"][0], w["up"][0], name=f"ffn0_up_{tag}")
    st["f0"], x1 = _mm(st["s0"], w["down"][0], res=x0, colscale=_row(0.5 * mod[0, 2]), emit_acc=True, name=f"ffn0_down_{tag}")
    st["x1"] = x1
    st["h1"] = _norm_fwd(x1, _row(gains[1]), _row(mod[1, 1]), _row(mod[1, 0]), name=f"norm1_fwd_{tag}")
    qkv = _mm(st["h1"], w["qkv"], name=f"qkv_{tag}")
    st["gates"] = _mm(st["h1"], w["gates"], name=f"gates_{tag}")
    st["o"], st["mix"] = _mixer_fwd(qkv, bias, sinks_l, tag)
    st["merged"] = _merge_fwd(*st["o"], st["gates"], w["br_sb"], w["br_dil"], w["br_swa"], name=f"merge_fwd_{tag}")
    st["f1"], x2 = _mm(st["merged"], w["out"], res=x1, colscale=_row(mod[1, 2]), emit_acc=True, name=f"out_{tag}")
    st["x2"] = x2
    st["h2"] = _norm_fwd(x2, _row(gains[2]), _row(mod[2, 1]), _row(mod[2, 0]), name=f"norm2_fwd_{tag}")
    st["a2"], st["u2"], st["s2"] = _ffn_up(st["h2"], w["gate"][1], w["up"][1], name=f"ffn1_up_{tag}")
    st["f2"], x3 = _mm(st["s2"], w["down"][1], res=x2, colscale=_row(0.5 * mod[2, 2]), emit_acc=True, name=f"ffn1_down_{tag}")
    return x3, st


def _ffn_bwd(dx_out, x_in, h, a, u, s_act, f, wg, wu, wd, gain, mod_j, tag):
    df, dgate = _gate_bwd(dx_out, f, _row(0.5 * mod_j[2]), 0.5, name=f"gate_bwd_{tag}")
    dwd = _mm(s_act, df, ta=True, out_dtype=BF16, name=f"dwd_{tag}")
    da, du = _ffn_bwd_ds(df, wd, a, u, name=f"ds_{tag}")
    dwg = _mm(h, da, ta=True, out_dtype=BF16, name=f"dwg_{tag}")
    dwu = _mm(h, du, ta=True, out_dtype=BF16, name=f"dwu_{tag}")
    dh = _mm(da, wg, tb=True, name=f"dh_a_{tag}")
    dh = _mm(du, wu, tb=True, res=dh, name=f"dh_u_{tag}")
    dx_in, sum_dh, sum_dhx = _norm_bwd(x_in, dh, dx_out, _row(gain), _row(mod_j[1]), name=f"norm_bwd_{tag}")
    dmod = jnp.concatenate([sum_dh, gain * sum_dhx, dgate], 0)
    return dx_in, (dwg, dwu, dwd), dmod, (1.0 + mod_j[1]) * sum_dhx[0]


def _layer_bwd(dx3, st, mod, gains, w, bias, tag):
    g = {}
    dx2, (dwg1, dwu1, dwd1), dmod2, dgain2 = _ffn_bwd(
        dx3, st["x2"], st["h2"], st["a2"], st["u2"], st["s2"], st["f2"], w["gate"][1], w["up"][1], w["down"][1],
        gains[2], mod[2], f"ffn1_{tag}")
    df1, dgate1 = _gate_bwd(dx2, st["f1"], _row(mod[1, 2]), 1.0, name=f"gate_bwd_mix_{tag}")
    g["out"] = _mm(st["merged"], df1, ta=True, out_dtype=BF16, name=f"dw_out_{tag}")
    dmerged = _mm(df1, w["out"], tb=True, name=f"dmerged_{tag}")
    dgates, do_sb, do_dil, do_swa, dbr_sb, dbr_dil, dbr_swa = _merge_bwd(
        dmerged, *st["o"], st["gates"], w["br_sb"], w["br_dil"], w["br_swa"], name=f"merge_bwd_{tag}")
    g["br_sb"] = _mm(st["o"][0], dbr_sb, ta=True, out_dtype=BF16, name=f"dw_br_sb_{tag}")
    g["br_dil"] = _mm(st["o"][1], dbr_dil, ta=True, out_dtype=BF16, name=f"dw_br_dil_{tag}")
    g["br_swa"] = _mm(st["o"][2], dbr_swa, ta=True, out_dtype=BF16, name=f"dw_br_swa_{tag}")
    dqkv, dbias, dsinks = _mixer_bwd(st["mix"], bias, do_sb, do_dil, do_swa, tag)
    g["qkv"] = _mm(st["h1"], dqkv, ta=True, out_dtype=BF16, name=f"dw_qkv_{tag}")
    g["gates"] = _mm(st["h1"], dgates, ta=True, out_dtype=BF16, name=f"dw_gates_{tag}")
    dh1 = _mm(dqkv, w["qkv"], tb=True, name=f"dh1_qkv_{tag}")
    dh1 = _mm(dgates, w["gates"], tb=True, res=dh1, name=f"dh1_gates_{tag}")
    dx1, sum_dh, sum_dhx = _norm_bwd(st["x1"], dh1, dx2, _row(gains[1]), _row(mod[1, 1]), name=f"norm_bwd_mix_{tag}")
    dmod1 = jnp.concatenate([sum_dh, gains[1] * sum_dhx, dgate1], 0)
    dgain1 = (1.0 + mod[1, 1]) * sum_dhx[0]
    dx0, (dwg0, dwu0, dwd0), dmod0, dgain0 = _ffn_bwd(
        dx1, st["x0"], st["h0"], st["a0"], st["u0"], st["s0"], st["f0"], w["gate"][0], w["up"][0], w["down"][0],
        gains[0], mod[0], f"ffn0_{tag}")
    g["gate"], g["up"], g["down"] = [dwg0, dwg1], [dwu0, dwu1], [dwd0, dwd1]
    dmod = jnp.stack([dmod0, dmod1, dmod2])
    dgain = jnp.stack([dgain0, dgain1, dgain2])
    return dx0, g, dmod, dgain, dbias, dsinks


def _local_step(x, target, mod, gains, weights, rel_bias, sinks, final_gain):
    tables = jnp.asarray(_bucket_tables())
    bias = _bias_build(rel_bias, tables, name="bias_build")
    states, h = [], x
    for l in range(DEPTH):
        h, st = _layer_fwd(h, mod[l], gains[l], weights[l], bias, sinks[l], f"l{l}")
        states.append(st)
    loss, dx, dfinal = _final_loss(h, target, _row(final_gain), name="final_loss")
    grads, dmods, dgains, dsinks = [None] * DEPTH, [None] * DEPTH, [None] * DEPTH, [None] * DEPTH
    dbias = None
    for l in reversed(range(DEPTH)):
        dx, grads[l], dmods[l], dgains[l], db, dsinks[l] = _layer_bwd(dx, states[l], mod[l], gains[l], weights[l], bias, f"l{l}")
        dbias = db if dbias is None else dbias + db
    drel = _bias_grad(dbias, tables, name="bias_grad")[:, 0, :N_BUCKETS].T
    return loss, dx, grads, jnp.stack(dmods), jnp.stack(dgains), dfinal[0], drel, jnp.stack(dsinks)


BR_ROWS = (H_SB * HEAD_DIM, 2 * HEAD_DIM, H_SWA_Q * HEAD_DIM)


def _rows_unshard(g, lead):
    _, rows, cdim = g.shape
    r = rows // lead
    return jnp.moveaxis(g.reshape(N_DEV, lead, r, cdim), 0, 1).reshape(lead, N_DEV * r, cdim)


def _rows_shard(full):
    lead, rows, cdim = full.shape
    r = rows // N_DEV
    return jnp.moveaxis(full.reshape(lead, N_DEV, r, cdim), 1, 0).reshape(N_DEV, lead * r, cdim)


def _lanes_unshard(g, lead):
    _, rows, _ = g.shape
    r = rows // lead
    return g.reshape(N_DEV, lead, r, LANES).transpose(1, 2, 0, 3).reshape(lead, r, N_DEV * LANES)


def _lanes_shard(full):
    lead, r, _ = full.shape
    return full.reshape(lead, r, N_DEV, LANES).transpose(2, 0, 1, 3).reshape(N_DEV, lead * r, LANES)


def _pack_rows(parts, dtype):
    flat = jnp.concatenate([p.astype(dtype).reshape(-1) for p in parts])
    pad = (-flat.shape[0]) % (16 * LANES)
    if pad:
        flat = jnp.concatenate([flat, jnp.zeros((pad,), dtype)])
    return flat.reshape(-1, LANES)


def _unshard(gathered, axis):
    moved = jnp.moveaxis(gathered, 0, axis)
    shape = list(moved.shape)
    shape[axis:axis + 2] = [shape[axis] * shape[axis + 1]]
    return moved.reshape(shape)


def kernel(x, c, w_ada, b_ada, norm_gain, w_ffn_gate, w_ffn_up, w_ffn_down, w_in, w_br_sb, w_br_dil, w_br_swa, w_out, sinks, rel_bias, final_gain, loss_target, m_w_ada, m_b_ada, m_norm_gain, m_w_ffn_gate, m_w_ffn_up, m_w_ffn_down, m_w_in, m_w_br_sb, m_w_br_dil, m_w_br_swa, m_w_out, m_sinks, m_rel_bias, m_final_gain, v_w_ada, v_b_ada, v_norm_gain, v_w_ffn_gate, v_w_ffn_up, v_w_ffn_down, v_w_in, v_w_br_sb, v_w_br_dil, v_w_br_swa, v_w_out, v_sinks, v_rel_bias, v_final_gain):
    me = 4 * lax.axis_index("x") + 2 * lax.axis_index("y") + lax.axis_index("c")
    d = D_MODEL
    small, = _all_gather([_pack_rows([c, norm_gain], F32)], name="gather_cond")
    c_all = small[:, :d // LANES].reshape(N_DEV, d)
    gains = _unshard(small[:, d // LANES:d // LANES + 6].reshape(N_DEV, DEPTH, 3, LANES), 2)

    cols = w_ada.shape[2]
    mod_cols = jnp.stack([_ada_fwd(c_all, w_ada[l], name=f"ada_fwd_l{l}") for l in range(DEPTH)])
    mod_all, = _all_gather([_pack_rows([mod_cols], F32)], name="gather_mod")
    mod_all = mod_all.reshape(N_DEV, -1)[:, :DEPTH * N_DEV * cols].reshape(N_DEV, DEPTH, N_DEV, cols)
    mod_mine = lax.dynamic_index_in_dim(mod_all, me, axis=2, keepdims=False)
    mod = (mod_mine.transpose(1, 0, 2).reshape(DEPTH, N_DEV * cols) + b_ada).reshape(DEPTH, 3, 3, d)

    def rows2d(t):
        return t.astype(BF16).reshape(-1, t.shape[-1])

    br_shard = jnp.concatenate([rows2d(w_br_sb), rows2d(w_br_dil), rows2d(w_br_swa)], 0)
    g_gate, g_up, g_down, g_in, g_br, g_out = _all_gather(
        [rows2d(w_ffn_gate), rows2d(w_ffn_up), rows2d(w_ffn_down), rows2d(w_in), br_shard, rows2d(w_out)],
        name="gather_weights")
    f_gate, = _unshard_cols(g_gate, [D_FF], name="unshard_gate")
    f_up, = _unshard_cols(g_up, [D_FF], name="unshard_up")
    f_qkv, f_gates = _unshard_cols(g_in, [D_QKV, D_GATES], name="unshard_in")
    f_gate, f_up = f_gate.reshape(DEPTH, 2, d, D_FF), f_up.reshape(DEPTH, 2, d, D_FF)
    f_qkv, f_gates = f_qkv.reshape(DEPTH, d, D_QKV), f_gates.reshape(DEPTH, d, D_GATES)
    f_down = _rows_unshard(g_down, 2 * DEPTH).reshape(DEPTH, 2, D_FF, d)
    f_out = _rows_unshard(g_out, DEPTH)
    br_off = np.concatenate([[0], np.cumsum([DEPTH * r for r in BR_ROWS])])
    f_br = [_lanes_unshard(g_br[:, br_off[k]:br_off[k + 1]], DEPTH) for k in range(3)]
    weights = [{"gate": f_gate[l], "up": f_up[l], "down": f_down[l], "qkv": f_qkv[l], "gates": f_gates[l],
                "br_sb": f_br[0][l], "br_dil": f_br[1][l], "br_swa": f_br[2][l], "out": f_out[l]} for l in range(DEPTH)]

    loss, dx, grads, dmod, dgains, dfinal, drel, dsinks = _local_step(
        x[0], loss_target[0], mod, gains, weights, rel_bias, sinks, final_gain)

    s_gate = _shard_cols([[g["gate"][i]] for g in grads for i in range(2)], name="shard_gate")
    s_up = _shard_cols([[g["up"][i]] for g in grads for i in range(2)], name="shard_up")
    s_in = _shard_cols([[g["qkv"], g["gates"]] for g in grads], name="shard_in")
    s_down = _rows_shard(jnp.stack([g["down"][i] for g in grads for i in range(2)]))
    s_out = _rows_shard(jnp.stack([g["out"] for g in grads]))
    s_br = jnp.concatenate([_lanes_shard(jnp.stack([g[n] for g in grads])) for n in ("br_sb", "br_dil", "br_swa")], 1)
    parts = _all_to_all([s_gate, s_up, s_down, s_in, s_br, s_out], name="exchange_grads")
    sums = [_sum_parts(p, name=f"sum_grads_{n}") for p, n in zip(parts, ("gate", "up", "down", "in", "br", "out"))]
    gshard = {"gate": sums[0].reshape(w_ffn_gate.shape), "up": sums[1].reshape(w_ffn_up.shape),
              "down": sums[2].reshape(w_ffn_down.shape), "in": sums[3].reshape(w_in.shape),
              "br_sb": sums[4][br_off[0]:br_off[1]].reshape(w_br_sb.shape),
              "br_dil": sums[4][br_off[1]:br_off[2]].reshape(w_br_dil.shape),
              "br_swa": sums[4][br_off[2]:br_off[3]].reshape(w_br_swa.shape), "out": sums[5].reshape(w_out.shape)}

    small_parts = [dmod, dgains, dfinal, drel.T, dsinks, loss[0, :1]]
    small_sizes = [int(np.prod(p.shape)) for p in small_parts]
    small_all, = _all_gather([_pack_rows(small_parts, F32)], name="gather_small")
    small_sum = _sum_parts(small_all, name="sum_small").reshape(-1)
    offs = np.concatenate([[0], np.cumsum(small_sizes)])
    g_b_ada = small_sum[offs[0]:offs[1]].reshape(DEPTH, 9 * d)
    g_gain_full = small_sum[offs[1]:offs[2]].reshape(DEPTH, 3, d)
    g_norm_gain = lax.dynamic_slice_in_dim(g_gain_full, me * LANES, LANES, axis=2)
    g_final = small_sum[offs[2]:offs[3]]
    g_rel = small_sum[offs[3]:offs[4]].reshape(N_SOFT, N_BUCKETS).T
    g_sinks = small_sum[offs[4]:offs[5]].reshape(DEPTH, H_SWA_Q)
    loss_total = small_sum[offs[5]]

    dmod_all = small_all.reshape(N_DEV, -1)[:, :DEPTH * 9 * d].reshape(N_DEV, DEPTH, 9 * d)
    dmod_cols = lax.dynamic_slice_in_dim(dmod_all, me * cols, cols, axis=2)
    g_w_ada = jnp.stack([_ada_bwd(c_all.T, dmod_cols[:, l], name=f"ada_bwd_l{l}") for l in range(DEPTH)])

    names = ["w_ada", "b_ada", "norm_gain", "w_ffn_gate", "w_ffn_up", "w_ffn_down", "w_in", "w_br_sb", "w_br_dil",
             "w_br_swa", "w_out", "sinks", "rel_bias", "final_gain"]
    ws = [w_ada, b_ada, norm_gain, w_ffn_gate, w_ffn_up, w_ffn_down, w_in, w_br_sb, w_br_dil, w_br_swa, w_out, sinks,
          rel_bias, final_gain]
    gs = [g_w_ada, g_b_ada, g_norm_gain, gshard["gate"], gshard["up"], gshard["down"], gshard["in"], gshard["br_sb"],
          gshard["br_dil"], gshard["br_swa"], gshard["out"], g_sinks, g_rel, g_final]
    ms = [m_w_ada, m_b_ada, m_norm_gain, m_w_ffn_gate, m_w_ffn_up, m_w_ffn_down, m_w_in, m_w_br_sb, m_w_br_dil,
          m_w_br_swa, m_w_out, m_sinks, m_rel_bias, m_final_gain]
    vs = [v_w_ada, v_b_ada, v_norm_gain, v_w_ffn_gate, v_w_ffn_up, v_w_ffn_down, v_w_in, v_w_br_sb, v_w_br_dil,
          v_w_br_swa, v_w_out, v_sinks, v_rel_bias, v_final_gain]
    deltas, new_ms, new_vs = [], [], []
    for n, w, g, m, v in zip(names, ws, gs, ms, vs):
        if w.ndim == 1:
            dl, nm, nv = (t.reshape(w.shape) for t in _adamw(_row(w), _row(g), _row(m), _row(v), name=f"adamw_{n}"))
        else:
            dl, nm, nv = _adamw(w, g, m, v, name=f"adamw_{n}")
        deltas.append(dl)
        new_ms.append(nm)
        new_vs.append(nv)
    return (loss_total, dx[None], *gs, *deltas, *new_ms, *new_vs)
```

```python
import math

import numpy as np
import jax
import jax.numpy as jnp
from jax import lax
from jax.experimental import pallas as pl
from jax.experimental.pallas import tpu as pltpu

F32, BF16 = jnp.float32, jnp.bfloat16

SEQ, D_MODEL, D_FF, HEAD_DIM = 2048, 1024, 2816, 64
DEPTH = 2
BLK = 128
H_SB, H_DIL, H_SWA_Q, H_SWA_KV = 4, 6, 6, 2
DIL_PATTERNS = ((128, 1), (512, 4), (2048, 16))
SWA_WINDOW = 128
N_BUCKETS, MAX_REL_DIST = 32, 2048
RMS_EPS = 1e-6
D_QKV = 2560
D_GATES = 3 * D_MODEL
QKV_SPLITS = (256, 256, 256, 384, 384, 384, 384, 128, 128)
ADAM_LR, ADAM_B1, ADAM_B2, ADAM_EPS, ADAM_WD, ADAM_STEP = 0.001, 0.9, 0.999, 1e-08, 0.01, 10

N_DEV = 8
LANES = 128
NEG = -1e30
SB_TILE = 256
VMEM_LIMIT_BYTES = 48 * 1024 * 1024
HBM = pl.BlockSpec(memory_space=pltpu.HBM)
MESH = pl.DeviceIdType.MESH


def _tile(n, target):
    t = (min(n, target) // LANES) * LANES
    while t >= LANES:
        if n % t == 0:
            return t
        t -= LANES
    return n


def _row_tile(r, cap):
    t = (min(r, cap) // 16) * 16
    while t > 16 and r % t:
        t -= 16
    return t


def _params(semantics=None):
    return pltpu.CompilerParams(dimension_semantics=semantics, vmem_limit_bytes=VMEM_LIMIT_BYTES)


def _dot(a, b, ca, cb):
    return lax.dot_general(a, b, (((ca,), (cb,)), ((), ())), preferred_element_type=F32)


def _sigmoid(a):
    return 1.0 / (1.0 + jnp.exp(-a))


def _row(v):
    return v.reshape(1, -1)


def _all_gather(arrs, name):
    n = len(arrs)

    def body(*refs):
        x_refs, out_refs = refs[:n], refs[n:2 * n]
        send_sems, recv_sems, local_sems = refs[2 * n:]
        x, y, c = lax.axis_index("x"), lax.axis_index("y"), lax.axis_index("c")
        me, sibling = (x, y, c), (x, y, 1 - c)
        chips = [(1 - x, y), (x, 1 - y), (1 - x, 1 - y)]

        def slot(t, px, py, pc):
            return out_refs[t].at[4 * px + 2 * py + pc]

        def copy(t, k, block, to, src=None):
            return pltpu.make_async_remote_copy(
                src_ref=slot(t, *block) if src is None else src, dst_ref=slot(t, *block),
                send_sem=send_sems.at[7 * t + k], recv_sem=recv_sems.at[7 * t + k], device_id=to, device_id_type=MESH)

        mine = [pltpu.make_async_copy(x_refs[t], slot(t, *me), local_sems.at[t]) for t in range(n)]
        for cp in mine:
            cp.start()
        first = []
        for t in range(n):
            first.append(copy(t, 0, me, sibling, src=x_refs[t]))
            first += [copy(t, 1 + j, me, (*chip, c), src=x_refs[t]) for j, chip in enumerate(chips)]
        for cp in first:
            cp.start()
        passed = []
        for j, chip in enumerate(chips):
            for t in range(n):
                copy(t, 1 + j, (*chip, c), me).wait_recv()
                passed.append(copy(t, 4 + j, (*chip, c), sibling))
                passed[-1].start()
        for t in range(n):
            copy(t, 0, sibling, me).wait_recv()
        for j, chip in enumerate(chips):
            for t in range(n):
                copy(t, 4 + j, (*chip, 1 - c), me).wait_recv()
        for cp in first + passed:
            cp.wait_send()
        for cp in mine:
            cp.wait()

    return pl.pallas_call(
        body, name=name, out_shape=[jax.ShapeDtypeStruct((N_DEV,) + a.shape, a.dtype) for a in arrs],
        in_specs=[HBM] * n, out_specs=[HBM] * n,
        scratch_shapes=[pltpu.SemaphoreType.DMA((7 * n,)), pltpu.SemaphoreType.DMA((7 * n,)), pltpu.SemaphoreType.DMA((n,))],
    )(*arrs)


def _all_to_all(arrs, name):
    n = len(arrs)

    def body(*refs):
        x_refs, out_refs = refs[:n], refs[n:2 * n]
        send_sems, recv_sems, local_sems = refs[2 * n:]
        x, y, c = lax.axis_index("x"), lax.axis_index("y"), lax.axis_index("c")
        me = 4 * x + 2 * y + c
        mine = [pltpu.make_async_copy(x_refs[t].at[me], out_refs[t].at[me], local_sems.at[t]) for t in range(n)]
        for cp in mine:
            cp.start()
        sends, recvs = [], []
        for k in range(1, N_DEV):
            px = 1 - x if (k >> 2) & 1 else x
            py = 1 - y if (k >> 1) & 1 else y
            pc = 1 - c if k & 1 else c
            peer = 4 * px + 2 * py + pc
            for t in range(n):
                sem = 7 * t + k - 1
                sends.append(pltpu.make_async_remote_copy(
                    src_ref=x_refs[t].at[peer], dst_ref=out_refs[t].at[me], send_sem=send_sems.at[sem],
                    recv_sem=recv_sems.at[sem], device_id=(px, py, pc), device_id_type=MESH))
                recvs.append(pltpu.make_async_remote_copy(
                    src_ref=x_refs[t].at[me], dst_ref=out_refs[t].at[peer], send_sem=send_sems.at[sem],
                    recv_sem=recv_sems.at[sem], device_id=(px, py, pc), device_id_type=MESH))
        for cp in sends:
            cp.start()
        for cp in recvs:
            cp.wait_recv()
        for cp in sends:
            cp.wait_send()
        for cp in mine:
            cp.wait()

    return pl.pallas_call(
        body, name=name, out_shape=[jax.ShapeDtypeStruct(a.shape, a.dtype) for a in arrs],
        in_specs=[HBM] * n, out_specs=[HBM] * n,
        scratch_shapes=[pltpu.SemaphoreType.DMA((7 * n,)), pltpu.SemaphoreType.DMA((7 * n,)), pltpu.SemaphoreType.DMA((n,))],
    )(*arrs)


def _direct_copies(x_refs, land_refs, send_sems, recv_sems, gather):
    x, y, c = lax.axis_index("x"), lax.axis_index("y"), lax.axis_index("c")
    me = 4 * x + 2 * y + c
    sends, recvs = [], []
    for k in range(1, N_DEV):
        px = 1 - x if (k >> 2) & 1 else x
        py = 1 - y if (k >> 1) & 1 else y
        pc = 1 - c if k & 1 else c
        peer = 4 * px + 2 * py + pc
        for t, (x_ref, land_ref) in enumerate(zip(x_refs, land_refs)):
            sem = 7 * t + k - 1
            for out, src, slot in ((sends, x_ref if gather else x_ref.at[peer], me),
                                   (recvs, x_ref if gather else x_ref.at[me], peer)):
                out.append(pltpu.make_async_remote_copy(
                    src_ref=src, dst_ref=land_ref.at[slot], send_sem=send_sems.at[sem], recv_sem=recv_sems.at[sem],
                    device_id=(px, py, pc), device_id_type=MESH))
    return sends, recvs


SEM = pl.BlockSpec(memory_space=pltpu.SEMAPHORE)
ANY = pl.BlockSpec(memory_space=pl.ANY)
SIDE_EFFECT = pltpu.SideEffectType.DATAFLOW_SIDE_EFFECTING


def _exchange_start(arrs, lands, after, *, gather, name):
    n = len(arrs)

    def body(*refs):
        sends, _ = _direct_copies(refs[:n], refs[n:2 * n], refs[2 * n + 1], refs[2 * n + 2], gather)
        for cp in sends:
            cp.start()
        refs[-1][...] = jnp.zeros_like(refs[-1])

    ops = [pltpu.with_memory_space_constraint(a, pltpu.HBM) for a in list(arrs) + list(lands)]
    out = pl.pallas_call(
        body, name=name,
        out_shape=(pltpu.SemaphoreType.DMA((7 * n,)), pltpu.SemaphoreType.DMA((7 * n,)),
                   *[pltpu.HBM(a.shape, a.dtype) for a in ops], jax.ShapeDtypeStruct((8, LANES), F32)),
        in_specs=[HBM] * (2 * n) + [ANY],
        out_specs=(SEM, SEM, *[HBM] * (2 * n), pl.BlockSpec(memory_space=pltpu.VMEM)),
        input_output_aliases={t: 2 + t for t in range(2 * n)},
        compiler_params=pltpu.CompilerParams(has_side_effects=SIDE_EFFECT),
    )(*ops, after)
    return out[0], out[1], out[2:2 + n], out[2 + n:2 + 2 * n], out[-1]


def _exchange_wait(send_sems, recv_sems, arrs, lands, after, *, gather, name):
    n = len(arrs)

    def body(*refs):
        sends, recvs = _direct_copies(refs[:n], refs[n:2 * n], refs[2 * n], refs[2 * n + 1], gather)
        for cp in sends:
            cp.wait_send()
        for cp in recvs:
            cp.wait_recv()

    out = pl.pallas_call(
        body, name=name, out_shape=tuple(pltpu.HBM(a.shape, a.dtype) for a in list(arrs) + list(lands)),
        in_specs=[HBM] * (2 * n) + [SEM, SEM, ANY], out_specs=tuple([HBM] * (2 * n)),
        input_output_aliases={t: t for t in range(2 * n)},
        compiler_params=pltpu.CompilerParams(has_side_effects=SIDE_EFFECT),
    )(*arrs, *lands, send_sems, recv_sems, after)
    return out[n:]


def _put_own(me, src, name):
    r, cdim = src.shape[-2:]
    tr = _row_tile(r, 512)

    def body(me_ref, s_ref, o_ref):
        o_ref[...] = s_ref[...]

    if src.ndim == 2:
        s_spec = pl.BlockSpec((tr, cdim), lambda i, me_ref: (i, 0))
    else:
        s_spec = pl.BlockSpec((None, tr, cdim), lambda i, me_ref: (me_ref[0], i, 0))
    return pl.pallas_call(
        body, name=name, out_shape=jax.ShapeDtypeStruct((N_DEV, r, cdim), src.dtype),
        grid_spec=pltpu.PrefetchScalarGridSpec(
            num_scalar_prefetch=1, grid=(r // tr,), in_specs=[s_spec],
            out_specs=pl.BlockSpec((None, tr, cdim), lambda i, me_ref: (me_ref[0], i, 0))),
        compiler_params=_params(("parallel",)),
    )(me.reshape(1).astype(jnp.int32), src)


def _col_pieces(w, widths):
    pieces, lo = [], 0
    for part, width in enumerate(widths):
        for p in range(N_DEV):
            a, b = max(lo, p * w), min(lo + width, (p + 1) * w)
            if a < b:
                pieces.append((p, part, a - p * w, a - lo, b - a))
        lo += width
    return pieces


def _unshard_cols(g, widths, name):
    _, r, w = g.shape
    tr = 256
    pieces = _col_pieces(w, widths)

    def body(g_ref, *o_refs):
        for p, part, s0, d0, size in pieces:
            o_refs[part][:, d0:d0 + size] = g_ref[p, :, s0:s0 + size]

    return pl.pallas_call(
        body, name=name, out_shape=[jax.ShapeDtypeStruct((r, wd), g.dtype) for wd in widths], grid=(r // tr,),
        in_specs=[pl.BlockSpec((N_DEV, tr, w), lambda i: (0, i, 0))],
        out_specs=[pl.BlockSpec((tr, wd), lambda i: (i, 0)) for wd in widths],
        compiler_params=_params(("parallel",)),
    )(g)


def _shard_cols(groups, name):
    widths = [a.shape[1] for a in groups[0]]
    r = groups[0][0].shape[0]
    w = sum(widths) // N_DEV
    tr = 256
    steps = r // tr
    pieces = _col_pieces(w, widths)
    nparts = len(widths)
    dtype = groups[0][0].dtype

    def body(*refs):
        o_ref = refs[-1]
        gg = pl.program_id(0)
        for gi in range(len(groups)):
            @pl.when(gg == gi)
            def _(gi=gi):
                for p, part, s0, d0, size in pieces:
                    o_ref[p, :, s0:s0 + size] = refs[gi * nparts + part][:, d0:d0 + size]

    def in_spec(gi, wd):
        return pl.BlockSpec((tr, wd), lambda gg, i: (jnp.where(gg == gi, i, 0), 0))

    return pl.pallas_call(
        body, name=name, out_shape=jax.ShapeDtypeStruct((N_DEV, len(groups) * r, w), dtype), grid=(len(groups), steps),
        in_specs=[in_spec(gi, wd) for gi in range(len(groups)) for wd in widths],
        out_specs=pl.BlockSpec((N_DEV, tr, w), lambda gg, i: (0, gg * steps + i, 0)),
        compiler_params=_params(("parallel", "parallel")),
    )(*[a for grp in groups for a in grp])


def _sum_parts(groups, name):
    n, r, cdim = groups[0].shape
    tr = _row_tile(r, max(16, (1 << 21) // (n * cdim * groups[0].dtype.itemsize)))
    steps = r // tr

    def body(*refs):
        o_ref = refs[-1]
        gg = pl.program_id(0)
        for gi in range(len(groups)):
            @pl.when(gg == gi)
            def _(gi=gi):
                acc = refs[gi][0].astype(F32)
                for k in range(1, n):
                    acc = acc + refs[gi][k].astype(F32)
                o_ref[...] = acc

    def in_spec(gi):
        return pl.BlockSpec((n, tr, cdim), lambda gg, i: (0, jnp.where(gg == gi, i, 0), 0))

    return pl.pallas_call(
        body, name=name, out_shape=jax.ShapeDtypeStruct((len(groups) * r, cdim), F32), grid=(len(groups), steps),
        in_specs=[in_spec(gi) for gi in range(len(groups))],
        out_specs=pl.BlockSpec((tr, cdim), lambda gg, i: (gg * steps + i, 0)),
        compiler_params=_params(("parallel", "parallel")),
    )(*groups)


def _mm(a, b, *, name, ta=False, tb=False, res=None, colscale=None, emit_acc=False,
        out_dtype=F32, tm=512, tn=512):
    m, k = (a.shape[1], a.shape[0]) if ta else a.shape
    n = b.shape[0] if tb else b.shape[1]
    tm, tn = _tile(m, tm), _tile(n, tn)
    ca, cb = (0 if ta else 1), (1 if tb else 0)
    a_spec = pl.BlockSpec((k, tm), lambda i, j: (0, i)) if ta else pl.BlockSpec((tm, k), lambda i, j: (i, 0))
    b_spec = pl.BlockSpec((tn, k), lambda i, j: (j, 0)) if tb else pl.BlockSpec((k, tn), lambda i, j: (0, j))
    tile = pl.BlockSpec((tm, tn), lambda i, j: (i, j))
    ins, in_specs = [a, b], [a_spec, b_spec]
    if res is not None:
        ins.append(res)
        in_specs.append(tile)
    if colscale is not None:
        ins.append(colscale)
        in_specs.append(pl.BlockSpec((1, tn), lambda i, j: (0, j)))
    n_in = len(ins)

    def body(*refs):
        outs = refs[n_in:]
        acc = _dot(refs[0][...].astype(BF16), refs[1][...].astype(BF16), ca, cb)
        val, p = acc, 2
        if res is not None:
            r_val, p = refs[p][...], p + 1
        if colscale is not None:
            val = val * refs[p][...]
        if res is not None:
            val = r_val + val
        if emit_acc:
            outs[0][...] = acc
        outs[-1][...] = val.astype(out_dtype)

    out_shape = [jax.ShapeDtypeStruct((m, n), out_dtype)]
    out_specs = [tile]
    if emit_acc:
        out_shape.insert(0, jax.ShapeDtypeStruct((m, n), F32))
        out_specs.insert(0, tile)
    out = pl.pallas_call(
        body, name=name, out_shape=out_shape, grid=(m // tm, n // tn), in_specs=in_specs, out_specs=out_specs,
        compiler_params=_params(("parallel", "parallel")),
    )(*ins)
    return out if emit_acc else out[0]


def _norm_fwd(x, g, scale, shift, name):
    s, d = x.shape
    tr = 256

    def body(x_ref, g_ref, sc_ref, sh_ref, h_ref):
        xv = x_ref[...]
        rstd = lax.rsqrt(jnp.mean(xv * xv, axis=-1, keepdims=True) + RMS_EPS)
        h_ref[...] = (xv * rstd * g_ref[...] * (1.0 + sc_ref[...]) + sh_ref[...]).astype(BF16)

    rowspec = pl.BlockSpec((1, d), lambda i: (0, 0))
    return pl.pallas_call(
        body, name=name, out_shape=jax.ShapeDtypeStruct((s, d), BF16), grid=(s // tr,),
        in_specs=[pl.BlockSpec((tr, d), lambda i: (i, 0)), rowspec, rowspec, rowspec],
        out_specs=pl.BlockSpec((tr, d), lambda i: (i, 0)),
        compiler_params=_params(("parallel",)),
    )(x, g, scale, shift)


def _norm_bwd(x, dh, dres, g, scale, name):
    s, d = x.shape
    tr = 256

    def body(x_ref, dh_ref, dr_ref, g_ref, sc_ref, dx_ref, a_ref, b_ref):
        @pl.when(pl.program_id(0) == 0)
        def _():
            a_ref[...] = jnp.zeros_like(a_ref)
            b_ref[...] = jnp.zeros_like(b_ref)

        xv = x_ref[...]
        rstd = lax.rsqrt(jnp.mean(xv * xv, axis=-1, keepdims=True) + RMS_EPS)
        xhat = xv * rstd
        dhv = dh_ref[...]
        dxhat = dhv * (g_ref[...] * (1.0 + sc_ref[...]))
        mean_term = jnp.mean(dxhat * xhat, axis=-1, keepdims=True)
        dx_ref[...] = dr_ref[...] + rstd * (dxhat - xhat * mean_term)
        a_ref[...] += jnp.sum(dhv, axis=0, keepdims=True)
        b_ref[...] += jnp.sum(dhv * xhat, axis=0, keepdims=True)

    rowspec = pl.BlockSpec((1, d), lambda i: (0, 0))
    tile = pl.BlockSpec((tr, d), lambda i: (i, 0))
    return pl.pallas_call(
        body, name=name,
        out_shape=[jax.ShapeDtypeStruct((s, d), F32), jax.ShapeDtypeStruct((1, d), F32), jax.ShapeDtypeStruct((1, d), F32)],
        grid=(s // tr,), in_specs=[tile, tile, tile, rowspec, rowspec], out_specs=[tile, rowspec, rowspec],
        compiler_params=_params(("arbitrary",)),
    )(x, dh, dres, g, scale)


def _gate_bwd(dxn, f, colscale, coef, name):
    s, d = dxn.shape
    tr = 256

    def body(dx_ref, f_ref, cs_ref, df_ref, dg_ref):
        @pl.when(pl.program_id(0) == 0)
        def _():
            dg_ref[...] = jnp.zeros_like(dg_ref)

        dxv = dx_ref[...]
        df_ref[...] = (dxv * cs_ref[...]).astype(BF16)
        dg_ref[...] += coef * jnp.sum(dxv * f_ref[...], axis=0, keepdims=True)

    rowspec = pl.BlockSpec((1, d), lambda i: (0, 0))
    tile = pl.BlockSpec((tr, d), lambda i: (i, 0))
    return pl.pallas_call(
        body, name=name, out_shape=[jax.ShapeDtypeStruct((s, d), BF16), jax.ShapeDtypeStruct((1, d), F32)],
        grid=(s // tr,), in_specs=[tile, tile, rowspec], out_specs=[tile, rowspec],
        compiler_params=_params(("arbitrary",)),
    )(dxn, f, colscale)


def _ffn_up(h, wg, wu, name):
    s, d = h.shape
    f = wg.shape[1]
    tm, tn = 512, _tile(f, 256)

    def body(h_ref, wg_ref, wu_ref, a_ref, u_ref, s_ref):
        hv = h_ref[...]
        a = _dot(hv, wg_ref[...], 1, 0)
        u = _dot(hv, wu_ref[...], 1, 0)
        a_ref[...] = a
        u_ref[...] = u
        s_ref[...] = (a * _sigmoid(a) * u).astype(BF16)

    tile = pl.BlockSpec((tm, tn), lambda i, j: (i, j))
    wspec = pl.BlockSpec((d, tn), lambda i, j: (0, j))
    return pl.pallas_call(
        body, name=name,
        out_shape=[jax.ShapeDtypeStruct((s, f), F32), jax.ShapeDtypeStruct((s, f), F32), jax.ShapeDtypeStruct((s, f), BF16)],
        grid=(s // tm, f // tn), in_specs=[pl.BlockSpec((tm, d), lambda i, j: (i, 0)), wspec, wspec],
        out_specs=[tile, tile, tile], compiler_params=_params(("parallel", "parallel")),
    )(h, wg, wu)


def _ffn_bwd_ds(df, wd, a, u, name):
    s, d = df.shape
    f = wd.shape[0]
    tm, tn = 512, _tile(f, 256)

    def body(df_ref, wd_ref, a_ref, u_ref, da_ref, du_ref):
        ds = _dot(df_ref[...], wd_ref[...], 1, 1)
        av = a_ref[...]
        sg = _sigmoid(av)
        da_ref[...] = (ds * u_ref[...] * (sg * (1.0 + av * (1.0 - sg)))).astype(BF16)
        du_ref[...] = (ds * (av * sg)).astype(BF16)

    tile = pl.BlockSpec((tm, tn), lambda i, j: (i, j))
    return pl.pallas_call(
        body, name=name, out_shape=[jax.ShapeDtypeStruct((s, f), BF16), jax.ShapeDtypeStruct((s, f), BF16)],
        grid=(s // tm, f // tn),
        in_specs=[pl.BlockSpec((tm, d), lambda i, j: (i, 0)), pl.BlockSpec((tn, d), lambda i, j: (j, 0)), tile, tile],
        out_specs=[tile, tile], compiler_params=_params(("parallel", "parallel")),
    )(df, wd, a, u)


def _merge_fwd(o_sb, o_dil, o_swa, gates, wb_sb, wb_dil, wb_swa, name):
    s = o_sb.shape[0]
    d = D_MODEL
    tm = 256

    def body(osb_ref, odl_ref, osw_ref, g_ref, wsb_ref, wdl_ref, wsw_ref, m_ref):
        acc = _sigmoid(g_ref[:, 0:d]) * _dot(osb_ref[...].astype(BF16), wsb_ref[...], 1, 0)
        acc += _sigmoid(g_ref[:, d:2 * d]) * _dot(odl_ref[...].astype(BF16), wdl_ref[...], 1, 0)
        acc += _sigmoid(g_ref[:, 2 * d:3 * d]) * _dot(osw_ref[...].astype(BF16), wsw_ref[...], 1, 0)
        m_ref[...] = acc.astype(BF16)

    def rows(w):
        return pl.BlockSpec((tm, w), lambda i: (i, 0))

    def whole(w):
        return pl.BlockSpec((w, d), lambda i: (0, 0))

    return pl.pallas_call(
        body, name=name, out_shape=jax.ShapeDtypeStruct((s, d), BF16), grid=(s // tm,),
        in_specs=[rows(256), rows(128), rows(384), rows(3 * d), whole(256), whole(128), whole(384)],
        out_specs=rows(d), compiler_params=_params(("parallel",)),
    )(o_sb, o_dil, o_swa, gates, wb_sb, wb_dil, wb_swa)


def _merge_bwd(dmerged, o_sb, o_dil, o_swa, gates, wb_sb, wb_dil, wb_swa, name):
    s = o_sb.shape[0]
    d = D_MODEL
    tm = 256

    def body(dm_ref, osb_ref, odl_ref, osw_ref, g_ref, wsb_ref, wdl_ref, wsw_ref,
             dg_ref, dosb_ref, dodl_ref, dosw_ref, dbsb_ref, dbdl_ref, dbsw_ref):
        dm = dm_ref[...]
        for idx, (o_ref, w_ref, do_ref, db_ref) in enumerate((
                (osb_ref, wsb_ref, dosb_ref, dbsb_ref), (odl_ref, wdl_ref, dodl_ref, dbdl_ref),
                (osw_ref, wsw_ref, dosw_ref, dbsw_ref))):
            w = w_ref[...]
            br = _dot(o_ref[...].astype(BF16), w, 1, 0)
            sg = _sigmoid(g_ref[:, idx * d:(idx + 1) * d])
            dbr = (dm * sg).astype(BF16)
            dg_ref[:, idx * d:(idx + 1) * d] = dm * br * (sg * (1.0 - sg))
            db_ref[...] = dbr
            do_ref[...] = _dot(dbr, w, 1, 1)

    def rows(w):
        return pl.BlockSpec((tm, w), lambda i: (i, 0))

    def whole(w):
        return pl.BlockSpec((w, d), lambda i: (0, 0))

    def shp(w, dt):
        return jax.ShapeDtypeStruct((s, w), dt)

    return pl.pallas_call(
        body, name=name,
        out_shape=[shp(3 * d, F32), shp(256, F32), shp(128, F32), shp(384, F32), shp(d, BF16), shp(d, BF16), shp(d, BF16)],
        grid=(s // tm,),
        in_specs=[rows(d), rows(256), rows(128), rows(384), rows(3 * d), whole(256), whole(128), whole(384)],
        out_specs=[rows(3 * d), rows(256), rows(128), rows(384), rows(d), rows(d), rows(d)],
        compiler_params=_params(("parallel",)),
    )(dmerged, o_sb, o_dil, o_swa, gates, wb_sb, wb_dil, wb_swa)


def _final_loss(x, target, g, name):
    s, d = x.shape
    tr = 256

    def body(x_ref, t_ref, g_ref, loss_ref, dx_ref, dg_ref):
        @pl.when(pl.program_id(0) == 0)
        def _():
            loss_ref[...] = jnp.zeros_like(loss_ref)
            dg_ref[...] = jnp.zeros_like(dg_ref)

        xv = x_ref[...]
        gv = g_ref[...]
        rstd = lax.rsqrt(jnp.mean(xv * xv, axis=-1, keepdims=True) + RMS_EPS)
        xhat = xv * rstd
        err = xhat * gv - t_ref[...]
        loss_ref[...] += 0.5 * jnp.sum(jnp.mean(err * err, axis=-1, keepdims=True))
        dy = err * (1.0 / d)
        dxhat = dy * gv
        mean_term = jnp.mean(dxhat * xhat, axis=-1, keepdims=True)
        dx_ref[...] = rstd * (dxhat - xhat * mean_term)
        dg_ref[...] += jnp.sum(dy * xhat, axis=0, keepdims=True)

    rowspec = pl.BlockSpec((1, d), lambda i: (0, 0))
    tile = pl.BlockSpec((tr, d), lambda i: (i, 0))
    return pl.pallas_call(
        body, name=name,
        out_shape=[jax.ShapeDtypeStruct((1, LANES), F32), jax.ShapeDtypeStruct((s, d), F32), jax.ShapeDtypeStruct((1, d), F32)],
        grid=(s // tr,), in_specs=[tile, tile, rowspec],
        out_specs=[pl.BlockSpec((1, LANES), lambda i: (0, 0)), tile, rowspec],
        compiler_params=_params(("arbitrary",)),
    )(x, target, g)


def _adamw(w, g, m, v, name):
    shape = w.shape
    cols = shape[-1]
    rows = int(np.prod(shape[:-1])) if len(shape) > 1 else 1
    tr = rows
    for cand in (1024, 512, 256, 128, 64, 32, 16, 8):
        if rows % cand == 0 and rows > cand and cand * cols * 4 <= (1 << 21):
            tr = cand
            break

    def body(w_ref, g_ref, m_ref, v_ref, d_ref, nm_ref, nv_ref):
        gv = g_ref[...]
        nm = ADAM_B1 * m_ref[...] + (1.0 - ADAM_B1) * gv
        nv = ADAM_B2 * v_ref[...] + (1.0 - ADAM_B2) * (gv * gv)
        m_hat = nm / (1.0 - ADAM_B1 ** ADAM_STEP)
        v_hat = nv / (1.0 - ADAM_B2 ** ADAM_STEP)
        d_ref[...] = -ADAM_LR * (m_hat / (jnp.sqrt(v_hat) + ADAM_EPS) + ADAM_WD * w_ref[...])
        nm_ref[...] = nm
        nv_ref[...] = nv

    tile = pl.BlockSpec((tr, cols), lambda i: (i, 0))
    flat = [t.reshape(rows, cols) for t in (w, g, m, v)]
    out = pl.pallas_call(
        body, name=name, out_shape=[jax.ShapeDtypeStruct((rows, cols), F32)] * 3, grid=(rows // tr,),
        in_specs=[tile] * 4, out_specs=[tile] * 3, compiler_params=_params(("parallel",)),
    )(*flat)
    return tuple(t.reshape(shape) for t in out)


def _ada_fwd(c_all, w, name):
    n = w.shape[1]

    def body(c_ref, w_ref, o_ref):
        cv = c_ref[...]
        o_ref[...] = jnp.dot(cv * _sigmoid(cv), w_ref[...], preferred_element_type=F32, precision=lax.Precision.HIGHEST)

    return pl.pallas_call(body, name=name, out_shape=jax.ShapeDtypeStruct((N_DEV, n), F32), compiler_params=_params())(c_all, w)


def _ada_bwd(c_all_t, dmod, name):
    n = dmod.shape[1]

    def body(c_ref, d_ref, o_ref):
        cv = c_ref[...]
        o_ref[...] = jnp.dot(cv * _sigmoid(cv), d_ref[...], preferred_element_type=F32, precision=lax.Precision.HIGHEST)

    return pl.pallas_call(body, name=name, out_shape=jax.ShapeDtypeStruct((D_MODEL, n), F32), compiler_params=_params())(c_all_t, dmod)


def _bucket_tables():
    rel = np.arange(BLK)[:, None] + BLK - np.arange(2 * BLK)[None, :]
    max_exact = N_BUCKETS // 2

    def bucket(n):
        nf = np.maximum(n, 1).astype(np.float32)
        large = max_exact + (np.log(nf / np.float32(max_exact)) / np.float32(math.log(MAX_REL_DIST / max_exact))
                             * np.float32(N_BUCKETS - max_exact)).astype(np.int32)
        return np.where(n < max_exact, n, np.minimum(large, N_BUCKETS - 1))

    tabs = []
    for dil, max_dist in ((1, 128), (4, 128), (16, 128), (1, SWA_WINDOW - 1)):
        in_band = (rel >= 0) & (rel <= max_dist)
        tabs.append(np.where(in_band, bucket(np.maximum(rel, 0) * dil), -1))
    return np.stack(tabs).astype(np.int32)


N_SOFT = H_DIL + H_SWA_Q


def _table_of_head(h):
    return jnp.minimum(h // 2, 3)


def _bias_build(rel_bias, tables, name):
    def body(rel_ref, t_ref, o_ref):
        h = pl.program_id(0)
        tb = t_ref[0]
        out = jnp.full((BLK, 2 * BLK), NEG, F32)
        for b in range(N_BUCKETS):
            out = jnp.where(tb == b, rel_ref[b, h], out)
        o_ref[0] = out

    return pl.pallas_call(
        body, name=name, out_shape=jax.ShapeDtypeStruct((N_SOFT, BLK, 2 * BLK), F32), grid=(N_SOFT,),
        in_specs=[pl.BlockSpec(memory_space=pltpu.SMEM),
                  pl.BlockSpec((1, BLK, 2 * BLK), lambda h: (_table_of_head(h), 0, 0))],
        out_specs=pl.BlockSpec((1, BLK, 2 * BLK), lambda h: (h, 0, 0)),
        compiler_params=_params(("parallel",)),
    )(rel_bias, tables)


def _bias_grad(dbias, tables, name):
    def body(d_ref, t_ref, o_ref):
        tb = t_ref[0]
        dv = d_ref[0]
        lane = lax.broadcasted_iota(jnp.int32, (1, LANES), 1)
        out = jnp.zeros((1, LANES), F32)
        for b in range(N_BUCKETS):
            out = jnp.where(lane == b, jnp.sum(jnp.where(tb == b, dv, 0.0)), out)
        o_ref[0] = out

    return pl.pallas_call(
        body, name=name, out_shape=jax.ShapeDtypeStruct((N_SOFT, 1, LANES), F32), grid=(N_SOFT,),
        in_specs=[pl.BlockSpec((1, BLK, 2 * BLK), lambda h: (h, 0, 0)),
                  pl.BlockSpec((1, BLK, 2 * BLK), lambda h: (_table_of_head(h), 0, 0))],
        out_specs=pl.BlockSpec((1, 1, LANES), lambda h: (h, 0, 0)),
        compiler_params=_params(("parallel",)),
    )(dbias, tables)


def _band_specs(g, bias_div):
    qspec = pl.BlockSpec((1, BLK, HEAD_DIM), lambda n, i: (n, i, 0))
    prev = pl.BlockSpec((1, BLK, HEAD_DIM), lambda n, i: (n // g, jnp.maximum(i - 1, 0), 0))
    cur = pl.BlockSpec((1, BLK, HEAD_DIM), lambda n, i: (n // g, i, 0))
    bspec = pl.BlockSpec((1, BLK, 2 * BLK), lambda n, i: (n // bias_div, 0, 0))
    sspec = pl.BlockSpec((1, 1, LANES), lambda n, i: (n, 0, 0))
    colspec = pl.BlockSpec((1, BLK, 1), lambda n, i: (n, i, 0))
    return qspec, prev, cur, bspec, sspec, colspec


def _band_scores(q_ref, kp_ref, kc_ref, b_ref, first):
    qv = q_ref[0]
    bv = b_ref[0]
    sp = _dot(qv, kp_ref[0], 1, 1) + bv[:, :BLK]
    sp = jnp.where(first, NEG, sp)
    sc = _dot(qv, kc_ref[0], 1, 1) + bv[:, BLK:]
    return sp, sc


def _band_fwd(q, k, v, bias, sink, *, g, bias_div, has_sink, name):
    nq, length, _ = q.shape

    def body(q_ref, kp_ref, kc_ref, vp_ref, vc_ref, b_ref, s_ref, o_ref, lse_ref):
        sp, sc = _band_scores(q_ref, kp_ref, kc_ref, b_ref, pl.program_id(1) == 0)
        m = jnp.maximum(jnp.max(sp, axis=1, keepdims=True), jnp.max(sc, axis=1, keepdims=True))
        if has_sink:
            sk = s_ref[0][:, :1]
            m = jnp.maximum(m, sk)
        pp, pc = jnp.exp(sp - m), jnp.exp(sc - m)
        den = jnp.sum(pp, axis=1, keepdims=True) + jnp.sum(pc, axis=1, keepdims=True)
        if has_sink:
            den = den + jnp.exp(sk - m)
        acc = _dot(pp.astype(BF16), vp_ref[0], 1, 0) + _dot(pc.astype(BF16), vc_ref[0], 1, 0)
        o_ref[0] = acc / den
        lse_ref[0] = m + jnp.log(den)

    qspec, prev, cur, bspec, sspec, colspec = _band_specs(g, bias_div)
    return pl.pallas_call(
        body, name=name,
        out_shape=[jax.ShapeDtypeStruct((nq, length, HEAD_DIM), F32), jax.ShapeDtypeStruct((nq, length, 1), F32)],
        grid=(nq, length // BLK), in_specs=[qspec, prev, cur, prev, cur, bspec, sspec],
        out_specs=[qspec, colspec], compiler_params=_params(("parallel", "parallel")),
    )(q, k, k, v, v, bias, sink)


def _band_bwd(q, k, v, bias, sink, o, lse, do, dlse, *, g, bias_div, has_sink, name):
    nq, length, _ = q.shape
    nk, nbias = nq // g, nq // bias_div

    def body(q_ref, kp_ref, kc_ref, vp_ref, vc_ref, b_ref, s_ref, o_ref, lse_ref, do_ref, dlse_ref,
             dq_ref, dk_ref, dv_ref, db_ref, dsk_ref):
        n, i = pl.program_id(0), pl.program_id(1)

        @pl.when((n % g == 0) & (i == 0))
        def _():
            dk_ref[...] = jnp.zeros_like(dk_ref)
            dv_ref[...] = jnp.zeros_like(dv_ref)

        @pl.when((n % bias_div == 0) & (i == 0))
        def _():
            db_ref[...] = jnp.zeros_like(db_ref)

        @pl.when(i == 0)
        def _():
            dsk_ref[...] = jnp.zeros_like(dsk_ref)

        sp, sc = _band_scores(q_ref, kp_ref, kc_ref, b_ref, i == 0)
        lse_v = lse_ref[0]
        pp, pc = jnp.exp(sp - lse_v), jnp.exp(sc - lse_v)
        dov = do_ref[0]
        dob = dov.astype(BF16)
        coef = dlse_ref[0] - jnp.sum(dov * o_ref[0], axis=1, keepdims=True)
        dsp = pp * (_dot(dob, vp_ref[0], 1, 1) + coef)
        dsc = pc * (_dot(dob, vc_ref[0], 1, 1) + coef)
        dspb, dscb = dsp.astype(BF16), dsc.astype(BF16)
        dq_ref[0] = (_dot(dspb, kp_ref[0], 1, 0) + _dot(dscb, kc_ref[0], 1, 0)) * (HEAD_DIM ** -0.5)
        qv = q_ref[0]
        cur = pl.ds(pl.multiple_of(i * BLK, BLK), BLK)
        prv = pl.ds(pl.multiple_of(jnp.maximum(i - 1, 0) * BLK, BLK), BLK)
        dk_ref[0, cur, :] += _dot(dscb, qv, 0, 0)
        dk_ref[0, prv, :] += _dot(dspb, qv, 0, 0)
        dv_ref[0, cur, :] += _dot(pc.astype(BF16), dob, 0, 0)
        dv_ref[0, prv, :] += _dot(pp.astype(BF16), dob, 0, 0)
        db_ref[0, :, :BLK] += dsp
        db_ref[0, :, BLK:] += dsc
        if has_sink:
            dsk_ref[0] += jnp.sum(jnp.exp(s_ref[0][:, :1] - lse_v) * coef)

    qspec, prev, cur, bspec, sspec, colspec = _band_specs(g, bias_div)
    kvfull = pl.BlockSpec((1, length, HEAD_DIM), lambda n, i: (n // g, 0, 0))
    return pl.pallas_call(
        body, name=name,
        out_shape=[jax.ShapeDtypeStruct((nq, length, HEAD_DIM), F32), jax.ShapeDtypeStruct((nk, length, HEAD_DIM), F32),
                   jax.ShapeDtypeStruct((nk, length, HEAD_DIM), F32), jax.ShapeDtypeStruct((nbias, BLK, 2 * BLK), F32),
                   jax.ShapeDtypeStruct((nq, 1, LANES), F32)],
        grid=(nq, length // BLK),
        in_specs=[qspec, prev, cur, prev, cur, bspec, sspec, qspec, colspec, qspec, colspec],
        out_specs=[qspec, kvfull, kvfull, bspec, sspec], compiler_params=_params(("arbitrary", "arbitrary")),
    )(q, k, k, v, v, bias, sink, o, lse, do, dlse)


def _dil_merge(os_, lses, dout, name):
    tr = 512
    n = len(os_)
    tile = pl.BlockSpec((1, tr, HEAD_DIM), lambda h, i: (h, i, 0))
    col = pl.BlockSpec((1, tr, 1), lambda h, i: (h, i, 0))

    def weights(l_refs):
        ls = [r[0] for r in l_refs]
        m = ls[0]
        for lv in ls[1:]:
            m = jnp.maximum(m, lv)
        es = [jnp.exp(lv - m) for lv in ls]
        den = es[0]
        for e in es[1:]:
            den = den + e
        return [e / den for e in es]

    if dout is None:
        def body(*refs):
            alphas = weights(refs[n:2 * n])
            acc = alphas[0] * refs[0][0]
            for gi in range(1, n):
                acc = acc + alphas[gi] * refs[gi][0]
            refs[2 * n][0] = acc

        return pl.pallas_call(
            body, name=name, out_shape=jax.ShapeDtypeStruct(os_[0].shape, F32), grid=(2, SEQ // tr),
            in_specs=[tile] * n + [col] * n, out_specs=tile, compiler_params=_params(("parallel", "parallel")),
        )(*os_, *lses)

    def body(*refs):
        alphas = weights(refs[n:2 * n])
        dov = refs[2 * n][0]
        outs = refs[2 * n + 1:]
        das = [jnp.sum(dov * refs[gi][0], axis=1, keepdims=True) for gi in range(n)]
        dbar = alphas[0] * das[0]
        for gi in range(1, n):
            dbar = dbar + alphas[gi] * das[gi]
        for gi in range(n):
            outs[gi][0] = alphas[gi] * dov
            outs[n + gi][0] = alphas[gi] * (das[gi] - dbar)

    return pl.pallas_call(
        body, name=name,
        out_shape=[jax.ShapeDtypeStruct(os_[0].shape, F32)] * n + [jax.ShapeDtypeStruct(lses[0].shape, F32)] * n,
        grid=(2, SEQ // tr), in_specs=[tile] * n + [col] * n + [tile], out_specs=[tile] * n + [col] * n,
        compiler_params=_params(("parallel", "parallel")),
    )(*os_, *lses, dout)


def _tri(cmp):
    r = lax.broadcasted_iota(jnp.int32, (SB_TILE, SB_TILE), 0)
    c = lax.broadcasted_iota(jnp.int32, (SB_TILE, SB_TILE), 1)
    return cmp(r, c).astype(BF16)


def _cum(x, tri, terms):
    acc, rest = None, x
    for _ in range(terms):
        part = rest.astype(BF16)
        rest = rest - part.astype(F32)
        d = _dot(part, tri, 1, 0)
        acc = d if acc is None else acc + d
    return acc


def _sb_logits(q, k_ref, j, i):
    t = SB_TILE
    ks = k_ref[0, pl.ds(pl.multiple_of(j * t, t), t), :]
    z = _dot(q, ks, 1, 1)
    rows = i * t + lax.broadcasted_iota(jnp.int32, (t, t), 0)
    cols = j * t + lax.broadcasted_iota(jnp.int32, (t, t), 1)
    mask = cols < rows
    e = jnp.exp(-jnp.abs(z))
    lf = jnp.where(mask, -(jnp.maximum(z, 0.0) + jnp.log(1.0 + e)), 0.0)
    return ks, z, e, lf, mask


def _sb_fwd(q, k, v, name):
    h, s, _ = q.shape
    t = SB_TILE

    def body(q_ref, k_ref, v_ref, o_ref, tot_ref):
        i = pl.program_id(1)
        qv = q_ref[0]
        after = _tri(lambda r, c: r > c)

        def step(jj, carry):
            right, acc = carry
            j = i - jj
            _, z, _, lf, mask = _sb_logits(qv, k_ref, j, i)
            between = right + _cum(lf, after, 3)
            w = jnp.where(mask, jnp.exp(z + lf + between), 0.0)
            vs = v_ref[0, pl.ds(pl.multiple_of(j * t, t), t), :]
            return right + jnp.sum(lf, axis=1, keepdims=True), acc + _dot(w.astype(BF16), vs, 1, 0)

        right, acc = lax.fori_loop(0, i + 1, step, (jnp.zeros((t, 1), F32), jnp.zeros((t, HEAD_DIM), F32)))
        o_ref[0] = acc
        tot_ref[0] = right

    tile = pl.BlockSpec((1, t, HEAD_DIM), lambda hh, i: (hh, i, 0))
    full = pl.BlockSpec((1, s, HEAD_DIM), lambda hh, i: (hh, 0, 0))
    return pl.pallas_call(
        body, name=name, out_shape=[jax.ShapeDtypeStruct((h, s, HEAD_DIM), F32), jax.ShapeDtypeStruct((h, s, 1), F32)],
        grid=(h, s // t), in_specs=[tile, full, full],
        out_specs=[tile, pl.BlockSpec((1, t, 1), lambda hh, i: (hh, i, 0))],
        compiler_params=_params(("parallel", "parallel")),
    )(q, k, v)


def _sb_bwd(q, k, v, tot, do, name):
    h, s, _ = q.shape
    t = SB_TILE

    def body(q_ref, k_ref, v_ref, tot_ref, do_ref, dq_ref, dk_ref, dv_ref):
        i = pl.program_id(1)

        @pl.when(i == 0)
        def _():
            dk_ref[...] = jnp.zeros_like(dk_ref)
            dv_ref[...] = jnp.zeros_like(dv_ref)

        qv = q_ref[0]
        dob = do_ref[0].astype(BF16)
        total = tot_ref[0]
        upto = _tri(lambda r, c: r <= c)
        before = _tri(lambda r, c: r < c)

        def step(j, carry):
            left, cleft, dq = carry
            ks, z, e, lf, mask = _sb_logits(qv, k_ref, j, i)
            rows = pl.ds(pl.multiple_of(j * t, t), t)
            vs = v_ref[0, rows, :]
            between = total - (left + _cum(lf, upto, 3))
            w = jnp.where(mask, jnp.exp(z + lf + between), 0.0)
            dlog = w * _dot(dob, vs, 1, 1)
            cfail = cleft + _cum(dlog, before, 2)
            sig = jnp.where(z >= 0.0, 1.0, e) / (1.0 + e)
            dz = jnp.where(mask, dlog * (1.0 - sig) - sig * cfail, 0.0).astype(BF16)
            dk_ref[0, rows, :] += _dot(dz, qv, 0, 0)
            dv_ref[0, rows, :] += _dot(w.astype(BF16), dob, 0, 0)
            return (left + jnp.sum(lf, axis=1, keepdims=True), cleft + jnp.sum(dlog, axis=1, keepdims=True),
                    dq + _dot(dz, ks, 1, 0))

        zero = jnp.zeros((t, 1), F32)
        _, _, dq = lax.fori_loop(0, i + 1, step, (zero, zero, jnp.zeros((t, HEAD_DIM), F32)))
        dq_ref[0] = dq * (HEAD_DIM ** -0.5)

    tile = pl.BlockSpec((1, t, HEAD_DIM), lambda hh, i: (hh, i, 0))
    full = pl.BlockSpec((1, s, HEAD_DIM), lambda hh, i: (hh, 0, 0))
    shp = jax.ShapeDtypeStruct((h, s, HEAD_DIM), F32)
    return pl.pallas_call(
        body, name=name, out_shape=[shp, shp, shp], grid=(h, s // t),
        in_specs=[tile, full, full, pl.BlockSpec((1, t, 1), lambda hh, i: (hh, i, 0)), tile],
        out_specs=[tile, full, full], compiler_params=_params(("arbitrary", "arbitrary")),
    )(q, k, v, tot, do)


def _heads(t):
    return t.reshape(SEQ, -1, HEAD_DIM).transpose(1, 0, 2)


def _unheads(t):
    return t.transpose(1, 0, 2).reshape(SEQ, -1)


def _to_dil(t, d):
    xdim = t.shape[-1]
    return t.reshape(2, SEQ // d, d, xdim).transpose(0, 2, 1, 3).reshape(2 * d, SEQ // d, xdim)


def _from_dil(t, d):
    xdim = t.shape[-1]
    return t.reshape(2, d, SEQ // d, xdim).transpose(0, 2, 1, 3).reshape(2, SEQ, xdim)


def _split_qkv(qkv):
    parts, off = [], 0
    for w in QKV_SPLITS:
        parts.append(qkv[:, off:off + w])
        off += w
    return parts


def _mixer_fwd(qkv, bias, sinks_l, tag):
    scale = HEAD_DIM ** -0.5
    q_sb, k_sb, v_sb, q_dl, k_dl, v_dl, q_sw, k_sw, v_sw = _split_qkv(qkv)
    hq = lambda t: _heads((t * scale).astype(BF16))
    hk = lambda t: _heads(t.astype(BF16))
    st = {}
    st["sb"] = (hq(q_sb), hk(k_sb), hk(v_sb))
    o_sb, st["sb_tot"] = _sb_fwd(*st["sb"], name=f"sb_fwd_{tag}")

    qd, kd, vd = hq(q_dl), hk(k_dl), hk(v_dl)
    no_sink = jnp.zeros((1, 1, LANES), F32)
    st["dil"], outs, lses = [], [], []
    for gi, (_, d) in enumerate(DIL_PATTERNS):
        hs = slice(2 * gi, 2 * gi + 2)
        qg, kg, vg = _to_dil(qd[hs], d), _to_dil(kd[hs], d), _to_dil(vd[hs], d)
        sink = jnp.broadcast_to(no_sink, (2 * d, 1, LANES))
        og, lg = _band_fwd(qg, kg, vg, bias[hs], sink, g=1, bias_div=d, has_sink=False, name=f"dil{gi}_fwd_{tag}")
        st["dil"].append((qg, kg, vg, sink, og, lg))
        outs.append(_from_dil(og, d))
        lses.append(_from_dil(lg, d))
    st["dil_outs"], st["dil_lses"] = outs, lses
    o_dil = _dil_merge(outs, lses, None, name=f"dil_merge_fwd_{tag}")

    sink = jnp.broadcast_to(sinks_l.reshape(H_SWA_Q, 1, 1), (H_SWA_Q, 1, LANES))
    st["swa"] = (hq(q_sw), hk(k_sw), hk(v_sw), sink)
    o_sw, l_sw = _band_fwd(*st["swa"][:3], bias[H_DIL:], sink, g=H_SWA_Q // H_SWA_KV, bias_div=1, has_sink=True,
                           name=f"swa_fwd_{tag}")
    st["swa_out"] = (o_sw, l_sw)
    return (_unheads(o_sb), _unheads(o_dil), _unheads(o_sw)), st


def _mixer_bwd(st, bias, do_sb, do_dil, do_swa, tag):
    dq_sb, dk_sb, dv_sb = _sb_bwd(*st["sb"], st["sb_tot"], _heads(do_sb), name=f"sb_bwd_{tag}")

    dmerge = _dil_merge(st["dil_outs"], st["dil_lses"], _heads(do_dil), name=f"dil_merge_bwd_{tag}")
    dqs, dks, dvs, dbs = [], [], [], []
    for gi, (_, d) in enumerate(DIL_PATTERNS):
        qg, kg, vg, sink, og, lg = st["dil"][gi]
        hs = slice(2 * gi, 2 * gi + 2)
        dq, dk, dv, db, _ = _band_bwd(qg, kg, vg, bias[hs], sink, og, lg, _to_dil(dmerge[gi], d), _to_dil(dmerge[3 + gi], d),
                                      g=1, bias_div=d, has_sink=False, name=f"dil{gi}_bwd_{tag}")
        dqs.append(_from_dil(dq, d))
        dks.append(_from_dil(dk, d))
        dvs.append(_from_dil(dv, d))
        dbs.append(db)

    q_sw, k_sw, v_sw, sink = st["swa"]
    o_sw, l_sw = st["swa_out"]
    dq_sw, dk_sw, dv_sw, db_sw, dsink = _band_bwd(q_sw, k_sw, v_sw, bias[H_DIL:], sink, o_sw, l_sw, _heads(do_swa),
                                                  jnp.zeros_like(l_sw), g=H_SWA_Q // H_SWA_KV, bias_div=1, has_sink=True,
                                                  name=f"swa_bwd_{tag}")
    dqkv = jnp.concatenate(
        [_unheads(dq_sb), _unheads(dk_sb), _unheads(dv_sb),
         _unheads(jnp.concatenate(dqs, 0)), _unheads(jnp.concatenate(dks, 0)), _unheads(jnp.concatenate(dvs, 0)),
         _unheads(dq_sw), _unheads(dk_sw), _unheads(dv_sw)], axis=1)
    return dqkv, jnp.concatenate(dbs + [db_sw], 0), dsink[:, 0, 0]


def _layer_fwd(x0, mod, gains, w, bias, sinks_l, tag):
    st = {"x0": x0}
    st["h0"] = _norm_fwd(x0, _row(gains[0]), _row(mod[0, 1]), _row(mod[0, 0]), name=f"norm0_fwd_{tag}")
    st["a0"], st["u0"], st["s0"] = _ffn_up(st["h0"], w["gate"][0], w["up"][0], name=f"ffn0_up_{tag}")
    st["f0"], x1 = _mm(st["s0"], w["down"][0], res=x0, colscale=_row(0.5 * mod[0, 2]), emit_acc=True, name=f"ffn0_down_{tag}")
    st["x1"] = x1
    st["h1"] = _norm_fwd(x1, _row(gains[1]), _row(mod[1, 1]), _row(mod[1, 0]), name=f"norm1_fwd_{tag}")
    qkv = _mm(st["h1"], w["qkv"], name=f"qkv_{tag}")
    st["gates"] = _mm(st["h1"], w["gates"], name=f"gates_{tag}")
    st["o"], st["mix"] = _mixer_fwd(qkv, bias, sinks_l, tag)
    st["merged"] = _merge_fwd(*st["o"], st["gates"], w["br_sb"], w["br_dil"], w["br_swa"], name=f"merge_fwd_{tag}")
    st["f1"], x2 = _mm(st["merged"], w["out"], res=x1, colscale=_row(mod[1, 2]), emit_acc=True, name=f"out_{tag}")
    st["x2"] = x2
    st["h2"] = _norm_fwd(x2, _row(gains[2]), _row(mod[2, 1]), _row(mod[2, 0]), name=f"norm2_fwd_{tag}")
    st["a2"], st["u2"], st["s2"] = _ffn_up(st["h2"], w["gate"][1], w["up"][1], name=f"ffn1_up_{tag}")
    st["f2"], x3 = _mm(st["s2"], w["down"][1], res=x2, colscale=_row(0.5 * mod[2, 2]), emit_acc=True, name=f"ffn1_down_{tag}")
    return x3, st


def _ffn_bwd(dx_out, x_in, h, a, u, s_act, f, wg, wu, wd, gain, mod_j, tag):
    df, dgate = _gate_bwd(dx_out, f, _row(0.5 * mod_j[2]), 0.5, name=f"gate_bwd_{tag}")
    dwd = _mm(s_act, df, ta=True, out_dtype=BF16, name=f"dwd_{tag}")
    da, du = _ffn_bwd_ds(df, wd, a, u, name=f"ds_{tag}")
    dwg = _mm(h, da, ta=True, out_dtype=BF16, name=f"dwg_{tag}")
    dwu = _mm(h, du, ta=True, out_dtype=BF16, name=f"dwu_{tag}")
    dh = _mm(da, wg, tb=True, name=f"dh_a_{tag}")
    dh = _mm(du, wu, tb=True, res=dh, name=f"dh_u_{tag}")
    dx_in, sum_dh, sum_dhx = _norm_bwd(x_in, dh, dx_out, _row(gain), _row(mod_j[1]), name=f"norm_bwd_{tag}")
    dmod = jnp.concatenate([sum_dh, gain * sum_dhx, dgate], 0)
    return dx_in, (dwg, dwu, dwd), dmod, (1.0 + mod_j[1]) * sum_dhx[0]


def _layer_bwd(dx3, st, mod, gains, w, bias, tag):
    g = {}
    dx2, (dwg1, dwu1, dwd1), dmod2, dgain2 = _ffn_bwd(
        dx3, st["x2"], st["h2"], st["a2"], st["u2"], st["s2"], st["f2"], w["gate"][1], w["up"][1], w["down"][1],
        gains[2], mod[2], f"ffn1_{tag}")
    df1, dgate1 = _gate_bwd(dx2, st["f1"], _row(mod[1, 2]), 1.0, name=f"gate_bwd_mix_{tag}")
    g["out"] = _mm(st["merged"], df1, ta=True, out_dtype=BF16, name=f"dw_out_{tag}")
    dmerged = _mm(df1, w["out"], tb=True, name=f"dmerged_{tag}")
    dgates, do_sb, do_dil, do_swa, dbr_sb, dbr_dil, dbr_swa = _merge_bwd(
        dmerged, *st["o"], st["gates"], w["br_sb"], w["br_dil"], w["br_swa"], name=f"merge_bwd_{tag}")
    g["br_sb"] = _mm(st["o"][0], dbr_sb, ta=True, out_dtype=BF16, name=f"dw_br_sb_{tag}")
    g["br_dil"] = _mm(st["o"][1], dbr_dil, ta=True, out_dtype=BF16, name=f"dw_br_dil_{tag}")
    g["br_swa"] = _mm(st["o"][2], dbr_swa, ta=True, out_dtype=BF16, name=f"dw_br_swa_{tag}")
    dqkv, dbias, dsinks = _mixer_bwd(st["mix"], bias, do_sb, do_dil, do_swa, tag)
    g["qkv"] = _mm(st["h1"], dqkv, ta=True, out_dtype=BF16, name=f"dw_qkv_{tag}")
    g["gates"] = _mm(st["h1"], dgates, ta=True, out_dtype=BF16, name=f"dw_gates_{tag}")
    dh1 = _mm(dqkv, w["qkv"], tb=True, name=f"dh1_qkv_{tag}")
    dh1 = _mm(dgates, w["gates"], tb=True, res=dh1, name=f"dh1_gates_{tag}")
    dx1, sum_dh, sum_dhx = _norm_bwd(st["x1"], dh1, dx2, _row(gains[1]), _row(mod[1, 1]), name=f"norm_bwd_mix_{tag}")
    dmod1 = jnp.concatenate([sum_dh, gains[1] * sum_dhx, dgate1], 0)
    dgain1 = (1.0 + mod[1, 1]) * sum_dhx[0]
    dx0, (dwg0, dwu0, dwd0), dmod0, dgain0 = _ffn_bwd(
        dx1, st["x0"], st["h0"], st["a0"], st["u0"], st["s0"], st["f0"], w["gate"][0], w["up"][0], w["down"][0],
        gains[0], mod[0], f"ffn0_{tag}")
    g["gate"], g["up"], g["down"] = [dwg0, dwg1], [dwu0, dwu1], [dwd0, dwd1]
    dmod = jnp.stack([dmod0, dmod1, dmod2])
    dgain = jnp.stack([dgain0, dgain1, dgain2])
    return dx0, g, dmod, dgain, dbias, dsinks


def _local_step(x, target, mod, gains, weights_of, rel_bias, sinks, final_gain, grads_done):
    tables = jnp.asarray(_bucket_tables())
    bias = _bias_build(rel_bias, tables, name="bias_build")
    states, weights, h = [], [], x
    for l in range(DEPTH):
        weights.append(weights_of(l, h))
        h, st = _layer_fwd(h, mod[l], gains[l], weights[l], bias, sinks[l], f"l{l}")
        states.append(st)
    loss, dx, dfinal = _final_loss(h, target, _row(final_gain), name="final_loss")
    dmods, dgains, dsinks = [None] * DEPTH, [None] * DEPTH, [None] * DEPTH
    dbias, zero = None, 0.0
    for l in reversed(range(DEPTH)):
        dx, grads, dmods[l], dgains[l], db, dsinks[l] = _layer_bwd(dx, states[l], mod[l] + zero, gains[l], weights[l], bias, f"l{l}")
        zero = grads_done(l, grads)
        dbias = db if dbias is None else dbias + db
    drel = _bias_grad(dbias, tables, name="bias_grad")[:, 0, :N_BUCKETS].T
    return loss, dx, jnp.stack(dmods), jnp.stack(dgains), dfinal[0], drel, jnp.stack(dsinks)


BR_ROWS = (H_SB * HEAD_DIM, 2 * HEAD_DIM, H_SWA_Q * HEAD_DIM)


def _rows_unshard(g, lead):
    _, rows, cdim = g.shape
    r = rows // lead
    return jnp.moveaxis(g.reshape(N_DEV, lead, r, cdim), 0, 1).reshape(lead, N_DEV * r, cdim)


def _rows_shard(full):
    lead, rows, cdim = full.shape
    r = rows // N_DEV
    return jnp.moveaxis(full.reshape(lead, N_DEV, r, cdim), 1, 0).reshape(N_DEV, lead * r, cdim)


def _lanes_unshard(g, lead):
    _, rows, _ = g.shape
    r = rows // lead
    return g.reshape(N_DEV, lead, r, LANES).transpose(1, 2, 0, 3).reshape(lead, r, N_DEV * LANES)


def _lanes_shard(full):
    lead, r, _ = full.shape
    return full.reshape(lead, r, N_DEV, LANES).transpose(2, 0, 1, 3).reshape(N_DEV, lead * r, LANES)


def _pack_rows(parts, dtype):
    flat = jnp.concatenate([p.astype(dtype).reshape(-1) for p in parts])
    pad = (-flat.shape[0]) % (16 * LANES)
    if pad:
        flat = jnp.concatenate([flat, jnp.zeros((pad,), dtype)])
    return flat.reshape(-1, LANES)


def _unshard(gathered, axis):
    moved = jnp.moveaxis(gathered, 0, axis)
    shape = list(moved.shape)
    shape[axis:axis + 2] = [shape[axis] * shape[axis + 1]]
    return moved.reshape(shape)


def kernel(x, c, w_ada, b_ada, norm_gain, w_ffn_gate, w_ffn_up, w_ffn_down, w_in, w_br_sb, w_br_dil, w_br_swa, w_out, sinks, rel_bias, final_gain, loss_target, m_w_ada, m_b_ada, m_norm_gain, m_w_ffn_gate, m_w_ffn_up, m_w_ffn_down, m_w_in, m_w_br_sb, m_w_br_dil, m_w_br_swa, m_w_out, m_sinks, m_rel_bias, m_final_gain, v_w_ada, v_b_ada, v_norm_gain, v_w_ffn_gate, v_w_ffn_up, v_w_ffn_down, v_w_in, v_w_br_sb, v_w_br_dil, v_w_br_swa, v_w_out, v_sinks, v_rel_bias, v_final_gain):
    me = 4 * lax.axis_index("x") + 2 * lax.axis_index("y") + lax.axis_index("c")
    d = D_MODEL
    small, = _all_gather([_pack_rows([c, norm_gain], F32)], name="gather_cond")
    c_all = small[:, :d // LANES].reshape(N_DEV, d)
    gains = _unshard(small[:, d // LANES:d // LANES + 6].reshape(N_DEV, DEPTH, 3, LANES), 2)

    cols = w_ada.shape[2]
    mod_cols = jnp.stack([_ada_fwd(c_all, w_ada[l], name=f"ada_fwd_l{l}") for l in range(DEPTH)])
    mod_all, = _all_gather([_pack_rows([mod_cols], F32)], name="gather_mod")
    mod_all = mod_all.reshape(N_DEV, -1)[:, :DEPTH * N_DEV * cols].reshape(N_DEV, DEPTH, N_DEV, cols)
    mod_mine = lax.dynamic_index_in_dim(mod_all, me, axis=2, keepdims=False)
    mod = (mod_mine.transpose(1, 0, 2).reshape(DEPTH, N_DEV * cols) + b_ada).reshape(DEPTH, 3, 3, d)

    def layer_shards(l):
        rows2d = lambda t: t[l].astype(BF16).reshape(-1, t.shape[-1])
        br = jnp.concatenate([rows2d(w_br_sb), rows2d(w_br_dil), rows2d(w_br_swa)], 0)
        return [rows2d(w_ffn_gate), rows2d(w_ffn_up), rows2d(w_ffn_down), rows2d(w_in), br, rows2d(w_out)]

    br_off = np.concatenate([[0], np.cumsum(BR_ROWS)])

    def layer_weights(gathered, l):
        g_gate, g_up, g_down, g_in, g_br, g_out = gathered
        f_gate, = _unshard_cols(g_gate, [D_FF], name=f"unshard_gate_l{l}")
        f_up, = _unshard_cols(g_up, [D_FF], name=f"unshard_up_l{l}")
        f_qkv, f_gates = _unshard_cols(g_in, [D_QKV, D_GATES], name=f"unshard_in_l{l}")
        f_br = [_lanes_unshard(g_br[:, br_off[k]:br_off[k + 1]], 1)[0] for k in range(3)]
        return {"gate": f_gate.reshape(2, d, D_FF), "up": f_up.reshape(2, d, D_FF), "down": _rows_unshard(g_down, 2),
                "qkv": f_qkv, "gates": f_gates, "br_sb": f_br[0], "br_dil": f_br[1], "br_swa": f_br[2],
                "out": _rows_unshard(g_out, 1)[0]}

    gathered0 = _all_gather(layer_shards(0), name="gather_weights_l0")
    shards1 = layer_shards(1)
    lands1 = [_put_own(me, s, name=f"own_weights_{k}") for k, s in enumerate(shards1)]
    w_send, w_recv, shards1, lands1, w_token = _exchange_start(shards1, lands1, gathered0[0], gather=True,
                                                               name="gather_weights_l1_start")
    mod = mod + w_token[0, 0]

    def weights_of(l, h):
        if l == 0:
            return layer_weights(gathered0, 0)
        return layer_weights(_exchange_wait(w_send, w_recv, shards1, lands1, h, gather=True, name="gather_weights_l1_wait"), 1)

    def shard_grads(g, l):
        s_br = jnp.concatenate([_lanes_shard(g[n][None]) for n in ("br_sb", "br_dil", "br_swa")], 1)
        return [_shard_cols([[g["gate"][0]], [g["gate"][1]]], name=f"shard_gate_l{l}"),
                _shard_cols([[g["up"][0]], [g["up"][1]]], name=f"shard_up_l{l}"),
                _rows_shard(jnp.stack(g["down"])), _shard_cols([[g["qkv"], g["gates"]]], name=f"shard_in_l{l}"),
                s_br, _rows_shard(g["out"][None])]

    pending = {}

    def grads_done(l, g):
        if l == 0:
            pending["l0"] = shard_grads(g, 0)
            return 0.0
        sg = shard_grads(g, 1)
        lands = [_put_own(me, s, name=f"own_grads_{k}") for k, s in enumerate(sg)]
        pending["l1"] = _exchange_start(sg, lands, sg[0], gather=False, name="exchange_grads_l1_start")
        return pending["l1"][4][0, 0]

    loss, dx, dmod, dgains, dfinal, drel, dsinks = _local_step(
        x[0], loss_target[0], mod, gains, weights_of, rel_bias, sinks, final_gain, grads_done)

    g_send, g_recv, sg1, lands1g, _ = pending["l1"]
    parts1 = _exchange_wait(g_send, g_recv, sg1, lands1g, dx, gather=False, name="exchange_grads_l1_wait")
    parts0 = _all_to_all(pending["l0"], name="exchange_grads_l0")
    sums = [_sum_parts([p0, p1], name=f"sum_grads_{n}")
            for p0, p1, n in zip(parts0, parts1, ("gate", "up", "down", "in", "br", "out"))]
    br_sums = sums[4].reshape(DEPTH, -1, LANES)
    gshard = {"gate": sums[0].reshape(w_ffn_gate.shape), "up": sums[1].reshape(w_ffn_up.shape),
              "down": sums[2].reshape(w_ffn_down.shape), "in": sums[3].reshape(w_in.shape),
              "br_sb": br_sums[:, br_off[0]:br_off[1]], "br_dil": br_sums[:, br_off[1]:br_off[2]],
              "br_swa": br_sums[:, br_off[2]:br_off[3]], "out": sums[5].reshape(w_out.shape)}

    small_parts = [dmod, dgains, dfinal, drel.T, dsinks, loss[0, :1]]
    small_sizes = [int(np.prod(p.shape)) for p in small_parts]
    small_all, = _all_gather([_pack_rows(small_parts, F32)], name="gather_small")
    small_sum = _sum_parts([small_all], name="sum_small").reshape(-1)
    offs = np.concatenate([[0], np.cumsum(small_sizes)])
    g_b_ada = small_sum[offs[0]:offs[1]].reshape(DEPTH, 9 * d)
    g_gain_full = small_sum[offs[1]:offs[2]].reshape(DEPTH, 3, d)
    g_norm_gain = lax.dynamic_slice_in_dim(g_gain_full, me * LANES, LANES, axis=2)
    g_final = small_sum[offs[2]:offs[3]]
    g_rel = small_sum[offs[3]:offs[4]].reshape(N_SOFT, N_BUCKETS).T
    g_sinks = small_sum[offs[4]:offs[5]].reshape(DEPTH, H_SWA_Q)
    loss_total = small_sum[offs[5]]

    dmod_all = small_all.reshape(N_DEV, -1)[:, :DEPTH * 9 * d].reshape(N_DEV, DEPTH, 9 * d)
    dmod_cols = lax.dynamic_slice_in_dim(dmod_all, me * cols, cols, axis=2)
    g_w_ada = jnp.stack([_ada_bwd(c_all.T, dmod_cols[:, l], name=f"ada_bwd_l{l}") for l in range(DEPTH)])

    names = ["w_ada", "b_ada", "norm_gain", "w_ffn_gate", "w_ffn_up", "w_ffn_down", "w_in", "w_br_sb", "w_br_dil",
             "w_br_swa", "w_out", "sinks", "rel_bias", "final_gain"]
    ws = [w_ada, b_ada, norm_gain, w_ffn_gate, w_ffn_up, w_ffn_down, w_in, w_br_sb, w_br_dil, w_br_swa, w_out, sinks,
          rel_bias, final_gain]
    gs = [g_w_ada, g_b_ada, g_norm_gain, gshard["gate"], gshard["up"], gshard["down"], gshard["in"], gshard["br_sb"],
          gshard["br_dil"], gshard["br_swa"], gshard["out"], g_sinks, g_rel, g_final]
    ms = [m_w_ada, m_b_ada, m_norm_gain, m_w_ffn_gate, m_w_ffn_up, m_w_ffn_down, m_w_in, m_w_br_sb, m_w_br_dil,
          m_w_br_swa, m_w_out, m_sinks, m_rel_bias, m_final_gain]
    vs = [v_w_ada, v_b_ada, v_norm_gain, v_w_ffn_gate, v_w_ffn_up, v_w_ffn_down, v_w_in, v_w_br_sb, v_w_br_dil,
          v_w_br_swa, v_w_out, v_sinks, v_rel_bias, v_final_gain]
    deltas, new_ms, new_vs = [], [], []
    for n, w, g, m, v in zip(names, ws, gs, ms, vs):
        if w.ndim == 1:
            dl, nm, nv = (t.reshape(w.shape) for t in _adamw(_row(w), _row(g), _row(m), _row(v), name=f"adamw_{n}"))
        else:
            dl, nm, nv = _adamw(w, g, m, v, name=f"adamw_{n}")
        deltas.append(dl)
        new_ms.append(nm)
        new_vs.append(nv)
    return (loss_total, dx[None], *gs, *deltas, *new_ms, *new_vs)
```

```python
import math

import numpy as np
import jax
import jax.numpy as jnp
from jax import lax
from jax.experimental import pallas as pl
from jax.experimental.pallas import tpu as pltpu

F32, BF16 = jnp.float32, jnp.bfloat16

SEQ, D_MODEL, D_FF, HEAD_DIM = 2048, 1024, 2816, 64
DEPTH = 2
BLK = 128
H_SB, H_DIL, H_SWA_Q, H_SWA_KV = 4, 6, 6, 2
DIL_PATTERNS = ((128, 1), (512, 4), (2048, 16))
SWA_WINDOW = 128
N_BUCKETS, MAX_REL_DIST = 32, 2048
RMS_EPS = 1e-6
D_QKV = 2560
D_GATES = 3 * D_MODEL
QKV_SPLITS = (256, 256, 256, 384, 384, 384, 384, 128, 128)
ADAM_LR, ADAM_B1, ADAM_B2, ADAM_EPS, ADAM_WD, ADAM_STEP = 0.001, 0.9, 0.999, 1e-08, 0.01, 10

N_DEV = 8
LANES = 128
NEG = -1e30
SB_TILE = 256
VMEM_LIMIT_BYTES = 48 * 1024 * 1024
HBM = pl.BlockSpec(memory_space=pltpu.HBM)
MESH = pl.DeviceIdType.MESH


def _tile(n, target):
    t = (min(n, target) // LANES) * LANES
    while t >= LANES:
        if n % t == 0:
            return t
        t -= LANES
    return n


def _row_tile(r, cap):
    t = (min(r, cap) // 16) * 16
    while t > 16 and r % t:
        t -= 16
    return t


def _params(semantics=None):
    return pltpu.CompilerParams(dimension_semantics=semantics, vmem_limit_bytes=VMEM_LIMIT_BYTES)


def _dot(a, b, ca, cb):
    return lax.dot_general(a, b, (((ca,), (cb,)), ((), ())), preferred_element_type=F32)


def _sigmoid(a):
    return 1.0 / (1.0 + jnp.exp(-a))


def _row(v):
    return v.reshape(1, -1)


def _all_gather(arrs, name):
    n = len(arrs)

    def body(*refs):
        x_refs, out_refs = refs[:n], refs[n:2 * n]
        send_sems, recv_sems, local_sems = refs[2 * n:]
        x, y, c = lax.axis_index("x"), lax.axis_index("y"), lax.axis_index("c")
        me, sibling = (x, y, c), (x, y, 1 - c)
        chips = [(1 - x, y), (x, 1 - y), (1 - x, 1 - y)]

        def slot(t, px, py, pc):
            return out_refs[t].at[4 * px + 2 * py + pc]

        def copy(t, k, block, to, src=None):
            return pltpu.make_async_remote_copy(
                src_ref=slot(t, *block) if src is None else src, dst_ref=slot(t, *block),
                send_sem=send_sems.at[7 * t + k], recv_sem=recv_sems.at[7 * t + k], device_id=to, device_id_type=MESH)

        mine = [pltpu.make_async_copy(x_refs[t], slot(t, *me), local_sems.at[t]) for t in range(n)]
        for cp in mine:
            cp.start()
        first = []
        for t in range(n):
            first.append(copy(t, 0, me, sibling, src=x_refs[t]))
            first += [copy(t, 1 + j, me, (*chip, c), src=x_refs[t]) for j, chip in enumerate(chips)]
        for cp in first:
            cp.start()
        passed = []
        for j, chip in enumerate(chips):
            for t in range(n):
                copy(t, 1 + j, (*chip, c), me).wait_recv()
                passed.append(copy(t, 4 + j, (*chip, c), sibling))
                passed[-1].start()
        for t in range(n):
            copy(t, 0, sibling, me).wait_recv()
        for j, chip in enumerate(chips):
            for t in range(n):
                copy(t, 4 + j, (*chip, 1 - c), me).wait_recv()
        for cp in first + passed:
            cp.wait_send()
        for cp in mine:
            cp.wait()

    return pl.pallas_call(
        body, name=name, out_shape=[jax.ShapeDtypeStruct((N_DEV,) + a.shape, a.dtype) for a in arrs],
        in_specs=[HBM] * n, out_specs=[HBM] * n,
        scratch_shapes=[pltpu.SemaphoreType.DMA((7 * n,)), pltpu.SemaphoreType.DMA((7 * n,)), pltpu.SemaphoreType.DMA((n,))],
    )(*arrs)


def _all_to_all(arrs, name):
    n = len(arrs)

    def body(*refs):
        x_refs, out_refs = refs[:n], refs[n:2 * n]
        send_sems, recv_sems, local_sems = refs[2 * n:]
        x, y, c = lax.axis_index("x"), lax.axis_index("y"), lax.axis_index("c")
        me = 4 * x + 2 * y + c
        mine = [pltpu.make_async_copy(x_refs[t].at[me], out_refs[t].at[me], local_sems.at[t]) for t in range(n)]
        for cp in mine:
            cp.start()
        sends, recvs = [], []
        for k in range(1, N_DEV):
            px = 1 - x if (k >> 2) & 1 else x
            py = 1 - y if (k >> 1) & 1 else y
            pc = 1 - c if k & 1 else c
            peer = 4 * px + 2 * py + pc
            for t in range(n):
                sem = 7 * t + k - 1
                sends.append(pltpu.make_async_remote_copy(
                    src_ref=x_refs[t].at[peer], dst_ref=out_refs[t].at[me], send_sem=send_sems.at[sem],
                    recv_sem=recv_sems.at[sem], device_id=(px, py, pc), device_id_type=MESH))
                recvs.append(pltpu.make_async_remote_copy(
                    src_ref=x_refs[t].at[me], dst_ref=out_refs[t].at[peer], send_sem=send_sems.at[sem],
                    recv_sem=recv_sems.at[sem], device_id=(px, py, pc), device_id_type=MESH))
        for cp in sends:
            cp.start()
        for cp in recvs:
            cp.wait_recv()
        for cp in sends:
            cp.wait_send()
        for cp in mine:
            cp.wait()

    return pl.pallas_call(
        body, name=name, out_shape=[jax.ShapeDtypeStruct(a.shape, a.dtype) for a in arrs],
        in_specs=[HBM] * n, out_specs=[HBM] * n,
        scratch_shapes=[pltpu.SemaphoreType.DMA((7 * n,)), pltpu.SemaphoreType.DMA((7 * n,)), pltpu.SemaphoreType.DMA((n,))],
    )(*arrs)


def _direct_copies(x_refs, land_refs, send_sems, recv_sems, local_sems, gather):
    x, y, c = lax.axis_index("x"), lax.axis_index("y"), lax.axis_index("c")
    me = 4 * x + 2 * y + c
    sends, recvs = [], []
    for k in range(1, N_DEV):
        px = 1 - x if (k >> 2) & 1 else x
        py = 1 - y if (k >> 1) & 1 else y
        pc = 1 - c if k & 1 else c
        peer = 4 * px + 2 * py + pc
        for t, (x_ref, land_ref) in enumerate(zip(x_refs, land_refs)):
            sem = 7 * t + k - 1
            for out, src, slot in ((sends, x_ref if gather else x_ref.at[peer], me),
                                   (recvs, x_ref if gather else x_ref.at[me], peer)):
                out.append(pltpu.make_async_remote_copy(
                    src_ref=src, dst_ref=land_ref.at[slot], send_sem=send_sems.at[sem], recv_sem=recv_sems.at[sem],
                    device_id=(px, py, pc), device_id_type=MESH))
    own = [pltpu.make_async_copy(x_ref if gather else x_ref.at[me], land_ref.at[me], local_sems.at[t])
           for t, (x_ref, land_ref) in enumerate(zip(x_refs, land_refs))]
    return sends, recvs, own


SEM =pl.BlockSpec(memory_space=pltpu.SEMAPHORE)
ANY = pl.BlockSpec(memory_space=pl.ANY)
SIDE_EFFECT = pltpu.SideEffectType.DATAFLOW_SIDE_EFFECTING


def _exchange_start(arrs, after, *, gather, name):
    n = len(arrs)
    lands = [lax.empty(((N_DEV,) + a.shape) if gather else a.shape, a.dtype) for a in arrs]

    def body(*refs):
        sends, _, own = _direct_copies(refs[:n], refs[n:2 * n], *refs[2 * n + 1:2 * n + 4], gather)
        for cp in own + sends:
            cp.start()
        refs[-1][...] = jnp.zeros_like(refs[-1])

    ops = [pltpu.with_memory_space_constraint(a, pltpu.HBM) for a in list(arrs) + lands]
    out = pl.pallas_call(
        body, name=name,
        out_shape=(pltpu.SemaphoreType.DMA((7 * n,)), pltpu.SemaphoreType.DMA((7 * n,)), pltpu.SemaphoreType.DMA((n,)),
                   *[pltpu.HBM(a.shape, a.dtype) for a in ops], jax.ShapeDtypeStruct((8, LANES), F32)),
        in_specs=[HBM] * (2 * n) + [ANY],
        out_specs=(SEM, SEM, SEM, *[HBM] * (2 * n), pl.BlockSpec(memory_space=pltpu.VMEM)),
        input_output_aliases={t: 3 + t for t in range(2 * n)},
        compiler_params=pltpu.CompilerParams(has_side_effects=SIDE_EFFECT),
    )(*ops, after)
    return (out[:3], out[3:3 + n], out[3 + n:3 + 2 * n]), out[-1]


def _exchange_wait(state, after, *, gather, name):
    sems, arrs, lands = state
    n = len(arrs)

    def body(*refs):
        sends, recvs, own = _direct_copies(refs[:n], refs[n:2 * n], *refs[2 * n:2 * n + 3], gather)
        for cp in own:
            cp.wait()
        for cp in sends:
            cp.wait_send()
        for cp in recvs:
            cp.wait_recv()

    out = pl.pallas_call(
        body, name=name, out_shape=tuple(pltpu.HBM(a.shape, a.dtype) for a in list(arrs) + list(lands)),
        in_specs=[HBM] * (2 * n) + [SEM, SEM, SEM, ANY], out_specs=tuple([HBM] * (2 * n)),
        input_output_aliases={t: t for t in range(2 * n)},
        compiler_params=pltpu.CompilerParams(has_side_effects=SIDE_EFFECT),
    )(*arrs, *lands, *sems, after)
    return out[n:]


def _col_pieces(w, widths):
    pieces, lo = [], 0
    for part, width in enumerate(widths):
        for p in range(N_DEV):
            a, b = max(lo, p * w), min(lo + width, (p + 1) * w)
            if a < b:
                pieces.append((p, part, a - p * w, a - lo, b - a))
        lo += width
    return pieces


def _unshard_cols(g, widths, name):
    _, r, w = g.shape
    tr = 256
    pieces = _col_pieces(w, widths)

    def body(g_ref, *o_refs):
        for p, part, s0, d0, size in pieces:
            o_refs[part][:, d0:d0 + size] = g_ref[p, :, s0:s0 + size]

    return pl.pallas_call(
        body, name=name, out_shape=[jax.ShapeDtypeStruct((r, wd), g.dtype) for wd in widths], grid=(r // tr,),
        in_specs=[pl.BlockSpec((N_DEV, tr, w), lambda i: (0, i, 0))],
        out_specs=[pl.BlockSpec((tr, wd), lambda i: (i, 0)) for wd in widths],
        compiler_params=_params(("parallel",)),
    )(g)


def _shard_cols(groups, name):
    widths = [a.shape[1] for a in groups[0]]
    r = groups[0][0].shape[0]
    w = sum(widths) // N_DEV
    tr = 256
    steps = r // tr
    pieces = _col_pieces(w, widths)
    nparts = len(widths)
    dtype = groups[0][0].dtype

    def body(*refs):
        o_ref = refs[-1]
        gg = pl.program_id(0)
        for gi in range(len(groups)):
            @pl.when(gg == gi)
            def _(gi=gi):
                for p, part, s0, d0, size in pieces:
                    o_ref[p, :, s0:s0 + size] = refs[gi * nparts + part][:, d0:d0 + size]

    def in_spec(gi, wd):
        return pl.BlockSpec((tr, wd), lambda gg, i: (jnp.where(gg == gi, i, 0), 0))

    return pl.pallas_call(
        body, name=name, out_shape=jax.ShapeDtypeStruct((N_DEV, len(groups) * r, w), dtype), grid=(len(groups), steps),
        in_specs=[in_spec(gi, wd) for gi in range(len(groups)) for wd in widths],
        out_specs=pl.BlockSpec((N_DEV, tr, w), lambda gg, i: (0, gg * steps + i, 0)),
        compiler_params=_params(("parallel", "parallel")),
    )(*[a for grp in groups for a in grp])


def _sum_parts(groups, name):
    n, r, cdim = groups[0].shape
    tr = _row_tile(r, max(16, (1 << 21) // (n * cdim * groups[0].dtype.itemsize)))
    steps = r // tr

    def body(*refs):
        o_ref = refs[-1]
        gg = pl.program_id(0)
        for gi in range(len(groups)):
            @pl.when(gg == gi)
            def _(gi=gi):
                acc = refs[gi][0].astype(F32)
                for k in range(1, n):
                    acc = acc + refs[gi][k].astype(F32)
                o_ref[...] = acc

    def in_spec(gi):
        return pl.BlockSpec((n, tr, cdim), lambda gg, i: (0, jnp.where(gg == gi, i, 0), 0))

    return pl.pallas_call(
        body, name=name, out_shape=jax.ShapeDtypeStruct((len(groups) * r, cdim), F32), grid=(len(groups), steps),
        in_specs=[in_spec(gi) for gi in range(len(groups))],
        out_specs=pl.BlockSpec((tr, cdim), lambda gg, i: (gg * steps + i, 0)),
        compiler_params=_params(("parallel", "parallel")),
    )(*groups)


def _mm(a, b, *, name, ta=False, tb=False, res=None, colscale=None, emit_acc=False,
        out_dtype=F32, tm=512, tn=512):
    m, k = (a.shape[1], a.shape[0]) if ta else a.shape
    n = b.shape[0] if tb else b.shape[1]
    tm, tn = _tile(m, tm), _tile(n, tn)
    ca, cb = (0 if ta else 1), (1 if tb else 0)
    a_spec = pl.BlockSpec((k, tm), lambda i, j: (0, i)) if ta else pl.BlockSpec((tm, k), lambda i, j: (i, 0))
    b_spec = pl.BlockSpec((tn, k), lambda i, j: (j, 0)) if tb else pl.BlockSpec((k, tn), lambda i, j: (0, j))
    tile = pl.BlockSpec((tm, tn), lambda i, j: (i, j))
    ins, in_specs = [a, b], [a_spec, b_spec]
    if res is not None:
        ins.append(res)
        in_specs.append(tile)
    if colscale is not None:
        ins.append(colscale)
        in_specs.append(pl.BlockSpec((1, tn), lambda i, j: (0, j)))
    n_in = len(ins)

    def body(*refs):
        outs = refs[n_in:]
        acc = _dot(refs[0][...].astype(BF16), refs[1][...].astype(BF16), ca, cb)
        val, p = acc, 2
        if res is not None:
            r_val, p = refs[p][...], p + 1
        if colscale is not None:
            val = val * refs[p][...]
        if res is not None:
            val = r_val + val
        if emit_acc:
            outs[0][...] = acc
        outs[-1][...] = val.astype(out_dtype)

    out_shape = [jax.ShapeDtypeStruct((m, n), out_dtype)]
    out_specs = [tile]
    if emit_acc:
        out_shape.insert(0, jax.ShapeDtypeStruct((m, n), F32))
        out_specs.insert(0, tile)
    out = pl.pallas_call(
        body, name=name, out_shape=out_shape, grid=(m // tm, n // tn), in_specs=in_specs, out_specs=out_specs,
        compiler_params=_params(("parallel", "parallel")),
    )(*ins)
    return out if emit_acc else out[0]


def _norm_fwd(x, g, scale, shift, name):
    s, d = x.shape
    tr = 256

    def body(x_ref, g_ref, sc_ref, sh_ref, h_ref):
        xv = x_ref[...]
        rstd = lax.rsqrt(jnp.mean(xv * xv, axis=-1, keepdims=True) + RMS_EPS)
        h_ref[...] = (xv * rstd * g_ref[...] * (1.0 + sc_ref[...]) + sh_ref[...]).astype(BF16)

    rowspec = pl.BlockSpec((1, d), lambda i: (0, 0))
    return pl.pallas_call(
        body, name=name, out_shape=jax.ShapeDtypeStruct((s, d), BF16), grid=(s // tr,),
        in_specs=[pl.BlockSpec((tr, d), lambda i: (i, 0)), rowspec, rowspec, rowspec],
        out_specs=pl.BlockSpec((tr, d), lambda i: (i, 0)),
        compiler_params=_params(("parallel",)),
    )(x, g, scale, shift)


def _norm_bwd(x, dh, dres, g, scale, name):
    s, d = x.shape
    tr = 256

    def body(x_ref, dh_ref, dr_ref, g_ref, sc_ref, dx_ref, a_ref, b_ref):
        @pl.when(pl.program_id(0) == 0)
        def _():
            a_ref[...] = jnp.zeros_like(a_ref)
            b_ref[...] = jnp.zeros_like(b_ref)

        xv = x_ref[...]
        rstd = lax.rsqrt(jnp.mean(xv * xv, axis=-1, keepdims=True) + RMS_EPS)
        xhat = xv * rstd
        dhv = dh_ref[...]
        dxhat = dhv * (g_ref[...] * (1.0 + sc_ref[...]))
        mean_term = jnp.mean(dxhat * xhat, axis=-1, keepdims=True)
        dx_ref[...] = dr_ref[...] + rstd * (dxhat - xhat * mean_term)
        a_ref[...] += jnp.sum(dhv, axis=0, keepdims=True)
        b_ref[...] += jnp.sum(dhv * xhat, axis=0, keepdims=True)

    rowspec = pl.BlockSpec((1, d), lambda i: (0, 0))
    tile = pl.BlockSpec((tr, d), lambda i: (i, 0))
    return pl.pallas_call(
        body, name=name,
        out_shape=[jax.ShapeDtypeStruct((s, d), F32), jax.ShapeDtypeStruct((1, d), F32), jax.ShapeDtypeStruct((1, d), F32)],
        grid=(s // tr,), in_specs=[tile, tile, tile, rowspec, rowspec], out_specs=[tile, rowspec, rowspec],
        compiler_params=_params(("arbitrary",)),
    )(x, dh, dres, g, scale)


def _gate_bwd(dxn, f, colscale, coef, name):
    s, d = dxn.shape
    tr = 256

    def body(dx_ref, f_ref, cs_ref, df_ref, dg_ref):
        @pl.when(pl.program_id(0) == 0)
        def _():
            dg_ref[...] = jnp.zeros_like(dg_ref)

        dxv = dx_ref[...]
        df_ref[...] = (dxv * cs_ref[...]).astype(BF16)
        dg_ref[...] += coef * jnp.sum(dxv * f_ref[...], axis=0, keepdims=True)

    rowspec = pl.BlockSpec((1, d), lambda i: (0, 0))
    tile = pl.BlockSpec((tr, d), lambda i: (i, 0))
    return pl.pallas_call(
        body, name=name, out_shape=[jax.ShapeDtypeStruct((s, d), BF16), jax.ShapeDtypeStruct((1, d), F32)],
        grid=(s // tr,), in_specs=[tile, tile, rowspec], out_specs=[tile, rowspec],
        compiler_params=_params(("arbitrary",)),
    )(dxn, f, colscale)


def _ffn_up(h, wg, wu, name):
    s, d = h.shape
    f = wg.shape[1]
    tm, tn = 512, _tile(f, 256)

    def body(h_ref, wg_ref, wu_ref, a_ref, u_ref, s_ref):
        hv = h_ref[...]
        a = _dot(hv, wg_ref[...], 1, 0)
        u = _dot(hv, wu_ref[...], 1, 0)
        a_ref[...] = a
        u_ref[...] = u
        s_ref[...] = (a * _sigmoid(a) * u).astype(BF16)

    tile = pl.BlockSpec((tm, tn), lambda i, j: (i, j))
    wspec = pl.BlockSpec((d, tn), lambda i, j: (0, j))
    return pl.pallas_call(
        body, name=name,
        out_shape=[jax.ShapeDtypeStruct((s, f), F32), jax.ShapeDtypeStruct((s, f), F32), jax.ShapeDtypeStruct((s, f), BF16)],
        grid=(s // tm, f // tn), in_specs=[pl.BlockSpec((tm, d), lambda i, j: (i, 0)), wspec, wspec],
        out_specs=[tile, tile, tile], compiler_params=_params(("parallel", "parallel")),
    )(h, wg, wu)


def _ffn_bwd_ds(df, wd, a, u, name):
    s, d = df.shape
    f = wd.shape[0]
    tm, tn = 512, _tile(f, 256)

    def body(df_ref, wd_ref, a_ref, u_ref, da_ref, du_ref):
        ds = _dot(df_ref[...], wd_ref[...], 1, 1)
        av = a_ref[...]
        sg = _sigmoid(av)
        da_ref[...] = (ds * u_ref[...] * (sg * (1.0 + av * (1.0 - sg)))).astype(BF16)
        du_ref[...] = (ds * (av * sg)).astype(BF16)

    tile = pl.BlockSpec((tm, tn), lambda i, j: (i, j))
    return pl.pallas_call(
        body, name=name, out_shape=[jax.ShapeDtypeStruct((s, f), BF16), jax.ShapeDtypeStruct((s, f), BF16)],
        grid=(s // tm, f // tn),
        in_specs=[pl.BlockSpec((tm, d), lambda i, j: (i, 0)), pl.BlockSpec((tn, d), lambda i, j: (j, 0)), tile, tile],
        out_specs=[tile, tile], compiler_params=_params(("parallel", "parallel")),
    )(df, wd, a, u)


def _merge_fwd(o_sb, o_dil, o_swa, gates, wb_sb, wb_dil, wb_swa, name):
    s = o_sb.shape[0]
    d = D_MODEL
    tm = 256

    def body(osb_ref, odl_ref, osw_ref, g_ref, wsb_ref, wdl_ref, wsw_ref, m_ref):
        acc = _sigmoid(g_ref[:, 0:d]) * _dot(osb_ref[...].astype(BF16), wsb_ref[...], 1, 0)
        acc += _sigmoid(g_ref[:, d:2 * d]) * _dot(odl_ref[...].astype(BF16), wdl_ref[...], 1, 0)
        acc += _sigmoid(g_ref[:, 2 * d:3 * d]) * _dot(osw_ref[...].astype(BF16), wsw_ref[...], 1, 0)
        m_ref[...] = acc.astype(BF16)

    def rows(w):
        return pl.BlockSpec((tm, w), lambda i: (i, 0))

    def whole(w):
        return pl.BlockSpec((w, d), lambda i: (0, 0))

    return pl.pallas_call(
        body, name=name, out_shape=jax.ShapeDtypeStruct((s, d), BF16), grid=(s // tm,),
        in_specs=[rows(256), rows(128), rows(384), rows(3 * d), whole(256), whole(128), whole(384)],
        out_specs=rows(d), compiler_params=_params(("parallel",)),
    )(o_sb, o_dil, o_swa, gates, wb_sb, wb_dil, wb_swa)


def _merge_bwd(dmerged, o_sb, o_dil, o_swa, gates, wb_sb, wb_dil, wb_swa, name):
    s = o_sb.shape[0]
    d = D_MODEL
    tm = 256

    def body(dm_ref, osb_ref, odl_ref, osw_ref, g_ref, wsb_ref, wdl_ref, wsw_ref,
             dg_ref, dosb_ref, dodl_ref, dosw_ref, dbsb_ref, dbdl_ref, dbsw_ref):
        dm = dm_ref[...]
        for idx, (o_ref, w_ref, do_ref, db_ref) in enumerate((
                (osb_ref, wsb_ref, dosb_ref, dbsb_ref), (odl_ref, wdl_ref, dodl_ref, dbdl_ref),
                (osw_ref, wsw_ref, dosw_ref, dbsw_ref))):
            w = w_ref[...]
            br = _dot(o_ref[...].astype(BF16), w, 1, 0)
            sg = _sigmoid(g_ref[:, idx * d:(idx + 1) * d])
            dbr = (dm * sg).astype(BF16)
            dg_ref[:, idx * d:(idx + 1) * d] = dm * br * (sg * (1.0 - sg))
            db_ref[...] = dbr
            do_ref[...] = _dot(dbr, w, 1, 1)

    def rows(w):
        return pl.BlockSpec((tm, w), lambda i: (i, 0))

    def whole(w):
        return pl.BlockSpec((w, d), lambda i: (0, 0))

    def shp(w, dt):
        return jax.ShapeDtypeStruct((s, w), dt)

    return pl.pallas_call(
        body, name=name,
        out_shape=[shp(3 * d, F32), shp(256, F32), shp(128, F32), shp(384, F32), shp(d, BF16), shp(d, BF16), shp(d, BF16)],
        grid=(s // tm,),
        in_specs=[rows(d), rows(256), rows(128), rows(384), rows(3 * d), whole(256), whole(128), whole(384)],
        out_specs=[rows(3 * d), rows(256), rows(128), rows(384), rows(d), rows(d), rows(d)],
        compiler_params=_params(("parallel",)),
    )(dmerged, o_sb, o_dil, o_swa, gates, wb_sb, wb_dil, wb_swa)


def _final_loss(x, target, g, name):
    s, d = x.shape
    tr = 256

    def body(x_ref, t_ref, g_ref, loss_ref, dx_ref, dg_ref):
        @pl.when(pl.program_id(0) == 0)
        def _():
            loss_ref[...] = jnp.zeros_like(loss_ref)
            dg_ref[...] = jnp.zeros_like(dg_ref)

        xv = x_ref[...]
        gv = g_ref[...]
        rstd = lax.rsqrt(jnp.mean(xv * xv, axis=-1, keepdims=True) + RMS_EPS)
        xhat = xv * rstd
        err = xhat * gv - t_ref[...]
        loss_ref[...] += 0.5 * jnp.sum(jnp.mean(err * err, axis=-1, keepdims=True))
        dy = err * (1.0 / d)
        dxhat = dy * gv
        mean_term = jnp.mean(dxhat * xhat, axis=-1, keepdims=True)
        dx_ref[...] = rstd * (dxhat - xhat * mean_term)
        dg_ref[...] += jnp.sum(dy * xhat, axis=0, keepdims=True)

    rowspec = pl.BlockSpec((1, d), lambda i: (0, 0))
    tile = pl.BlockSpec((tr, d), lambda i: (i, 0))
    return pl.pallas_call(
        body, name=name,
        out_shape=[jax.ShapeDtypeStruct((1, LANES), F32), jax.ShapeDtypeStruct((s, d), F32), jax.ShapeDtypeStruct((1, d), F32)],
        grid=(s // tr,), in_specs=[tile, tile, rowspec],
        out_specs=[pl.BlockSpec((1, LANES), lambda i: (0, 0)), tile, rowspec],
        compiler_params=_params(("arbitrary",)),
    )(x, target, g)


def _adamw(w, g, m, v, name):
    shape = w.shape
    cols = shape[-1]
    rows = int(np.prod(shape[:-1])) if len(shape) > 1 else 1
    tr = rows
    for cand in (1024, 512, 256, 128, 64, 32, 16, 8):
        if rows % cand == 0 and rows > cand and cand * cols * 4 <= (1 << 21):
            tr = cand
            break

    def body(w_ref, g_ref, m_ref, v_ref, d_ref, nm_ref, nv_ref):
        gv = g_ref[...]
        nm = ADAM_B1 * m_ref[...] + (1.0 - ADAM_B1) * gv
        nv = ADAM_B2 * v_ref[...] + (1.0 - ADAM_B2) * (gv * gv)
        m_hat = nm / (1.0 - ADAM_B1 ** ADAM_STEP)
        v_hat = nv / (1.0 - ADAM_B2 ** ADAM_STEP)
        d_ref[...] = -ADAM_LR * (m_hat / (jnp.sqrt(v_hat) + ADAM_EPS) + ADAM_WD * w_ref[...])
        nm_ref[...] = nm
        nv_ref[...] = nv

    tile = pl.BlockSpec((tr, cols), lambda i: (i, 0))
    flat = [t.reshape(rows, cols) for t in (w, g, m, v)]
    out = pl.pallas_call(
        body, name=name, out_shape=[jax.ShapeDtypeStruct((rows, cols), F32)] * 3, grid=(rows // tr,),
        in_specs=[tile] * 4, out_specs=[tile] * 3, compiler_params=_params(("parallel",)),
    )(*flat)
    return tuple(t.reshape(shape) for t in out)


def _ada_fwd(c_all, w, name):
    n = w.shape[1]

    def body(c_ref, w_ref, o_ref):
        cv = c_ref[...]
        o_ref[...] = jnp.dot(cv * _sigmoid(cv), w_ref[...], preferred_element_type=F32, precision=lax.Precision.HIGHEST)

    return pl.pallas_call(body, name=name, out_shape=jax.ShapeDtypeStruct((N_DEV, n), F32), compiler_params=_params())(c_all, w)


def _ada_bwd(c_all_t, dmod, name):
    n = dmod.shape[1]

    def body(c_ref, d_ref, o_ref):
        cv = c_ref[...]
        o_ref[...] = jnp.dot(cv * _sigmoid(cv), d_ref[...], preferred_element_type=F32, precision=lax.Precision.HIGHEST)

    return pl.pallas_call(body, name=name, out_shape=jax.ShapeDtypeStruct((D_MODEL, n), F32), compiler_params=_params())(c_all_t, dmod)


def _bucket_tables():
    rel = np.arange(BLK)[:, None] + BLK - np.arange(2 * BLK)[None, :]
    max_exact = N_BUCKETS // 2

    def bucket(n):
        nf = np.maximum(n, 1).astype(np.float32)
        large = max_exact + (np.log(nf / np.float32(max_exact)) / np.float32(math.log(MAX_REL_DIST / max_exact))
                             * np.float32(N_BUCKETS - max_exact)).astype(np.int32)
        return np.where(n < max_exact, n, np.minimum(large, N_BUCKETS - 1))

    tabs = []
    for dil, max_dist in ((1, 128), (4, 128), (16, 128), (1, SWA_WINDOW - 1)):
        in_band = (rel >= 0) & (rel <= max_dist)
        tabs.append(np.where(in_band, bucket(np.maximum(rel, 0) * dil), -1))
    return np.stack(tabs).astype(np.int32)


N_SOFT = H_DIL + H_SWA_Q


def _table_of_head(h):
    return jnp.minimum(h // 2, 3)


def _bias_build(rel_bias, tables, name):
    def body(rel_ref, t_ref, o_ref):
        h = pl.program_id(0)
        tb = t_ref[0]
        out = jnp.full((BLK, 2 * BLK), NEG, F32)
        for b in range(N_BUCKETS):
            out = jnp.where(tb == b, rel_ref[b, h], out)
        o_ref[0] = out

    return pl.pallas_call(
        body, name=name, out_shape=jax.ShapeDtypeStruct((N_SOFT, BLK, 2 * BLK), F32), grid=(N_SOFT,),
        in_specs=[pl.BlockSpec(memory_space=pltpu.SMEM),
                  pl.BlockSpec((1, BLK, 2 * BLK), lambda h: (_table_of_head(h), 0, 0))],
        out_specs=pl.BlockSpec((1, BLK, 2 * BLK), lambda h: (h, 0, 0)),
        compiler_params=_params(("parallel",)),
    )(rel_bias, tables)


def _bias_grad(dbias, tables, name):
    def body(d_ref, t_ref, o_ref):
        tb = t_ref[0]
        dv = d_ref[0]
        lane = lax.broadcasted_iota(jnp.int32, (1, LANES), 1)
        out = jnp.zeros((1, LANES), F32)
        for b in range(N_BUCKETS):
            out = jnp.where(lane == b, jnp.sum(jnp.where(tb == b, dv, 0.0)), out)
        o_ref[0] = out

    return pl.pallas_call(
        body, name=name, out_shape=jax.ShapeDtypeStruct((N_SOFT, 1, LANES), F32), grid=(N_SOFT,),
        in_specs=[pl.BlockSpec((1, BLK, 2 * BLK), lambda h: (h, 0, 0)),
                  pl.BlockSpec((1, BLK, 2 * BLK), lambda h: (_table_of_head(h), 0, 0))],
        out_specs=pl.BlockSpec((1, 1, LANES), lambda h: (h, 0, 0)),
        compiler_params=_params(("parallel",)),
    )(dbias, tables)


def _band_specs(g, bias_div):
    qspec = pl.BlockSpec((1, BLK, HEAD_DIM), lambda n, i: (n, i, 0))
    prev = pl.BlockSpec((1, BLK, HEAD_DIM), lambda n, i: (n // g, jnp.maximum(i - 1, 0), 0))
    cur = pl.BlockSpec((1, BLK, HEAD_DIM), lambda n, i: (n // g, i, 0))
    bspec = pl.BlockSpec((1, BLK, 2 * BLK), lambda n, i: (n // bias_div, 0, 0))
    sspec = pl.BlockSpec((1, 1, LANES), lambda n, i: (n, 0, 0))
    colspec = pl.BlockSpec((1, BLK, 1), lambda n, i: (n, i, 0))
    return qspec, prev, cur, bspec, sspec, colspec


def _band_scores(q_ref, kp_ref, kc_ref, b_ref, first):
    qv = q_ref[0]
    bv = b_ref[0]
    sp = _dot(qv, kp_ref[0], 1, 1) + bv[:, :BLK]
    sp = jnp.where(first, NEG, sp)
    sc = _dot(qv, kc_ref[0], 1, 1) + bv[:, BLK:]
    return sp, sc


def _band_fwd(q, k, v, bias, sink, *, g, bias_div, has_sink, name):
    nq, length, _ = q.shape

    def body(q_ref, kp_ref, kc_ref, vp_ref, vc_ref, b_ref, s_ref, o_ref, lse_ref):
        sp, sc = _band_scores(q_ref, kp_ref, kc_ref, b_ref, pl.program_id(1) == 0)
        m = jnp.maximum(jnp.max(sp, axis=1, keepdims=True), jnp.max(sc, axis=1, keepdims=True))
        if has_sink:
            sk = s_ref[0][:, :1]
            m = jnp.maximum(m, sk)
        pp, pc = jnp.exp(sp - m), jnp.exp(sc - m)
        den = jnp.sum(pp, axis=1, keepdims=True) + jnp.sum(pc, axis=1, keepdims=True)
        if has_sink:
            den = den + jnp.exp(sk - m)
        acc = _dot(pp.astype(BF16), vp_ref[0], 1, 0) + _dot(pc.astype(BF16), vc_ref[0], 1, 0)
        o_ref[0] = acc / den
        lse_ref[0] = m + jnp.log(den)

    qspec, prev, cur, bspec, sspec, colspec = _band_specs(g, bias_div)
    return pl.pallas_call(
        body, name=name,
        out_shape=[jax.ShapeDtypeStruct((nq, length, HEAD_DIM), F32), jax.ShapeDtypeStruct((nq, length, 1), F32)],
        grid=(nq, length // BLK), in_specs=[qspec, prev, cur, prev, cur, bspec, sspec],
        out_specs=[qspec, colspec], compiler_params=_params(("parallel", "parallel")),
    )(q, k, k, v, v, bias, sink)


def _band_bwd(q, k, v, bias, sink, o, lse, do, dlse, *, g, bias_div, has_sink, name):
    nq, length, _ = q.shape
    nk, nbias = nq // g, nq // bias_div

    def body(q_ref, kp_ref, kc_ref, vp_ref, vc_ref, b_ref, s_ref, o_ref, lse_ref, do_ref, dlse_ref,
             dq_ref, dk_ref, dv_ref, db_ref, dsk_ref):
        n, i = pl.program_id(0), pl.program_id(1)

        @pl.when((n % g == 0) & (i == 0))
        def _():
            dk_ref[...] = jnp.zeros_like(dk_ref)
            dv_ref[...] = jnp.zeros_like(dv_ref)

        @pl.when((n % bias_div == 0) & (i == 0))
        def _():
            db_ref[...] = jnp.zeros_like(db_ref)

        @pl.when(i == 0)
        def _():
            dsk_ref[...] = jnp.zeros_like(dsk_ref)

        sp, sc = _band_scores(q_ref, kp_ref, kc_ref, b_ref, i == 0)
        lse_v = lse_ref[0]
        pp, pc = jnp.exp(sp - lse_v), jnp.exp(sc - lse_v)
        dov = do_ref[0]
        dob = dov.astype(BF16)
        coef = dlse_ref[0] - jnp.sum(dov * o_ref[0], axis=1, keepdims=True)
        dsp = pp * (_dot(dob, vp_ref[0], 1, 1) + coef)
        dsc = pc * (_dot(dob, vc_ref[0], 1, 1) + coef)
        dspb, dscb = dsp.astype(BF16), dsc.astype(BF16)
        dq_ref[0] = (_dot(dspb, kp_ref[0], 1, 0) + _dot(dscb, kc_ref[0], 1, 0)) * (HEAD_DIM ** -0.5)
        qv = q_ref[0]
        cur = pl.ds(pl.multiple_of(i * BLK, BLK), BLK)
        prv = pl.ds(pl.multiple_of(jnp.maximum(i - 1, 0) * BLK, BLK), BLK)
        dk_ref[0, cur, :] += _dot(dscb, qv, 0, 0)
        dk_ref[0, prv, :] += _dot(dspb, qv, 0, 0)
        dv_ref[0, cur, :] += _dot(pc.astype(BF16), dob, 0, 0)
        dv_ref[0, prv, :] += _dot(pp.astype(BF16), dob, 0, 0)
        db_ref[0, :, :BLK] += dsp
        db_ref[0, :, BLK:] += dsc
        if has_sink:
            dsk_ref[0] += jnp.sum(jnp.exp(s_ref[0][:, :1] - lse_v) * coef)

    qspec, prev, cur, bspec, sspec, colspec = _band_specs(g, bias_div)
    kvfull = pl.BlockSpec((1, length, HEAD_DIM), lambda n, i: (n // g, 0, 0))
    return pl.pallas_call(
        body, name=name,
        out_shape=[jax.ShapeDtypeStruct((nq, length, HEAD_DIM), F32), jax.ShapeDtypeStruct((nk, length, HEAD_DIM), F32),
                   jax.ShapeDtypeStruct((nk, length, HEAD_DIM), F32), jax.ShapeDtypeStruct((nbias, BLK, 2 * BLK), F32),
                   jax.ShapeDtypeStruct((nq, 1, LANES), F32)],
        grid=(nq, length // BLK),
        in_specs=[qspec, prev, cur, prev, cur, bspec, sspec, qspec, colspec, qspec, colspec],
        out_specs=[qspec, kvfull, kvfull, bspec, sspec], compiler_params=_params(("arbitrary", "arbitrary")),
    )(q, k, k, v, v, bias, sink, o, lse, do, dlse)


def _dil_merge(os_, lses, dout, name):
    tr = 512
    n = len(os_)
    tile = pl.BlockSpec((1, tr, HEAD_DIM), lambda h, i: (h, i, 0))
    col = pl.BlockSpec((1, tr, 1), lambda h, i: (h, i, 0))

    def weights(l_refs):
        ls = [r[0] for r in l_refs]
        m = ls[0]
        for lv in ls[1:]:
            m = jnp.maximum(m, lv)
        es = [jnp.exp(lv - m) for lv in ls]
        den = es[0]
        for e in es[1:]:
            den = den + e
        return [e / den for e in es]

    if dout is None:
        def body(*refs):
            alphas = weights(refs[n:2 * n])
            acc = alphas[0] * refs[0][0]
            for gi in range(1, n):
                acc = acc + alphas[gi] * refs[gi][0]
            refs[2 * n][0] = acc

        return pl.pallas_call(
            body, name=name, out_shape=jax.ShapeDtypeStruct(os_[0].shape, F32), grid=(2, SEQ // tr),
            in_specs=[tile] * n + [col] * n, out_specs=tile, compiler_params=_params(("parallel", "parallel")),
        )(*os_, *lses)

    def body(*refs):
        alphas = weights(refs[n:2 * n])
        dov = refs[2 * n][0]
        outs = refs[2 * n + 1:]
        das = [jnp.sum(dov * refs[gi][0], axis=1, keepdims=True) for gi in range(n)]
        dbar = alphas[0] * das[0]
        for gi in range(1, n):
            dbar = dbar + alphas[gi] * das[gi]
        for gi in range(n):
            outs[gi][0] = alphas[gi] * dov
            outs[n + gi][0] = alphas[gi] * (das[gi] - dbar)

    return pl.pallas_call(
        body, name=name,
        out_shape=[jax.ShapeDtypeStruct(os_[0].shape, F32)] * n + [jax.ShapeDtypeStruct(lses[0].shape, F32)] * n,
        grid=(2, SEQ // tr), in_specs=[tile] * n + [col] * n + [tile], out_specs=[tile] * n + [col] * n,
        compiler_params=_params(("parallel", "parallel")),
    )(*os_, *lses, dout)


def _tri(cmp):
    r = lax.broadcasted_iota(jnp.int32, (SB_TILE, SB_TILE), 0)
    c = lax.broadcasted_iota(jnp.int32, (SB_TILE, SB_TILE), 1)
    return cmp(r, c).astype(BF16)


def _cum(x, tri, terms):
    acc, rest = None, x
    for _ in range(terms):
        part = rest.astype(BF16)
        rest = rest - part.astype(F32)
        d = _dot(part, tri, 1, 0)
        acc = d if acc is None else acc + d
    return acc


def _sb_logits(q, k_ref, j, i):
    t = SB_TILE
    ks = k_ref[0, pl.ds(pl.multiple_of(j * t, t), t), :]
    z = _dot(q, ks, 1, 1)
    rows = i * t + lax.broadcasted_iota(jnp.int32, (t, t), 0)
    cols = j * t + lax.broadcasted_iota(jnp.int32, (t, t), 1)
    mask = cols < rows
    e = jnp.exp(-jnp.abs(z))
    lf = jnp.where(mask, -(jnp.maximum(z, 0.0) + jnp.log(1.0 + e)), 0.0)
    return ks, z, e, lf, mask


def _sb_fwd(q, k, v, name):
    h, s, _ = q.shape
    t = SB_TILE

    def body(q_ref, k_ref, v_ref, o_ref, tot_ref):
        i = pl.program_id(1)
        qv = q_ref[0]
        after = _tri(lambda r, c: r > c)

        def step(jj, carry):
            right, acc = carry
            j = i - jj
            _, z, _, lf, mask = _sb_logits(qv, k_ref, j, i)
            between = right + _cum(lf, after, 3)
            w = jnp.where(mask, jnp.exp(z + lf + between), 0.0)
            vs = v_ref[0, pl.ds(pl.multiple_of(j * t, t), t), :]
            return right + jnp.sum(lf, axis=1, keepdims=True), acc + _dot(w.astype(BF16), vs, 1, 0)

        right, acc = lax.fori_loop(0, i + 1, step, (jnp.zeros((t, 1), F32), jnp.zeros((t, HEAD_DIM), F32)))
        o_ref[0] = acc
        tot_ref[0] = right

    tile = pl.BlockSpec((1, t, HEAD_DIM), lambda hh, i: (hh, i, 0))
    full = pl.BlockSpec((1, s, HEAD_DIM), lambda hh, i: (hh, 0, 0))
    return pl.pallas_call(
        body, name=name, out_shape=[jax.ShapeDtypeStruct((h, s, HEAD_DIM), F32), jax.ShapeDtypeStruct((h, s, 1), F32)],
        grid=(h, s // t), in_specs=[tile, full, full],
        out_specs=[tile, pl.BlockSpec((1, t, 1), lambda hh, i: (hh, i, 0))],
        compiler_params=_params(("parallel", "parallel")),
    )(q, k, v)


def _sb_bwd(q, k, v, tot, do, name):
    h, s, _ = q.shape
    t = SB_TILE

    def body(q_ref, k_ref, v_ref, tot_ref, do_ref, dq_ref, dk_ref, dv_ref):
        i = pl.program_id(1)

        @pl.when(i == 0)
        def _():
            dk_ref[...] = jnp.zeros_like(dk_ref)
            dv_ref[...] = jnp.zeros_like(dv_ref)

        qv = q_ref[0]
        dob = do_ref[0].astype(BF16)
        total = tot_ref[0]
        upto = _tri(lambda r, c: r <= c)
        before = _tri(lambda r, c: r < c)

        def step(j, carry):
            left, cleft, dq = carry
            ks, z, e, lf, mask = _sb_logits(qv, k_ref, j, i)
            rows = pl.ds(pl.multiple_of(j * t, t), t)
            vs = v_ref[0, rows, :]
            between = total - (left + _cum(lf, upto, 3))
            w = jnp.where(mask, jnp.exp(z + lf + between), 0.0)
            dlog = w * _dot(dob, vs, 1, 1)
            cfail = cleft + _cum(dlog, before, 2)
            sig = jnp.where(z >= 0.0, 1.0, e) / (1.0 + e)
            dz = jnp.where(mask, dlog * (1.0 - sig) - sig * cfail, 0.0).astype(BF16)
            dk_ref[0, rows, :] += _dot(dz, qv, 0, 0)
            dv_ref[0, rows, :] += _dot(w.astype(BF16), dob, 0, 0)
            return (left + jnp.sum(lf, axis=1, keepdims=True), cleft + jnp.sum(dlog, axis=1, keepdims=True),
                    dq + _dot(dz, ks, 1, 0))

        zero = jnp.zeros((t, 1), F32)
        _, _, dq = lax.fori_loop(0, i + 1, step, (zero, zero, jnp.zeros((t, HEAD_DIM), F32)))
        dq_ref[0] = dq * (HEAD_DIM ** -0.5)

    tile = pl.BlockSpec((1, t, HEAD_DIM), lambda hh, i: (hh, i, 0))
    full = pl.BlockSpec((1, s, HEAD_DIM), lambda hh, i: (hh, 0, 0))
    shp = jax.ShapeDtypeStruct((h, s, HEAD_DIM), F32)
    return pl.pallas_call(
        body, name=name, out_shape=[shp, shp, shp], grid=(h, s // t),
        in_specs=[tile, full, full, pl.BlockSpec((1, t, 1), lambda hh, i: (hh, i, 0)), tile],
        out_specs=[tile, full, full], compiler_params=_params(("arbitrary", "arbitrary")),
    )(q, k, v, tot, do)


def _heads(t):
    return t.reshape(SEQ, -1, HEAD_DIM).transpose(1, 0, 2)


def _unheads(t):
    return t.transpose(1, 0, 2).reshape(SEQ, -1)


def _to_dil(t, d):
    xdim = t.shape[-1]
    return t.reshape(2, SEQ // d, d, xdim).transpose(0, 2, 1, 3).reshape(2 * d, SEQ // d, xdim)


def _from_dil(t, d):
    xdim = t.shape[-1]
    return t.reshape(2, d, SEQ // d, xdim).transpose(0, 2, 1, 3).reshape(2, SEQ, xdim)


def _split_qkv(qkv):
    parts, off = [], 0
    for w in QKV_SPLITS:
        parts.append(qkv[:, off:off + w])
        off += w
    return parts


def _mixer_fwd(qkv, bias, sinks_l, tag):
    scale = HEAD_DIM ** -0.5
    q_sb, k_sb, v_sb, q_dl, k_dl, v_dl, q_sw, k_sw, v_sw = _split_qkv(qkv)
    hq = lambda t: _heads((t * scale).astype(BF16))
    hk = lambda t: _heads(t.astype(BF16))
    st = {}
    st["sb"] = (hq(q_sb), hk(k_sb), hk(v_sb))
    o_sb, st["sb_tot"] = _sb_fwd(*st["sb"], name=f"sb_fwd_{tag}")

    qd, kd, vd = hq(q_dl), hk(k_dl), hk(v_dl)
    no_sink = jnp.zeros((1, 1, LANES), F32)
    st["dil"], outs, lses = [], [], []
    for gi, (_, d) in enumerate(DIL_PATTERNS):
        hs = slice(2 * gi, 2 * gi + 2)
        qg, kg, vg = _to_dil(qd[hs], d), _to_dil(kd[hs], d), _to_dil(vd[hs], d)
        sink = jnp.broadcast_to(no_sink, (2 * d, 1, LANES))
        og, lg = _band_fwd(qg, kg, vg, bias[hs], sink, g=1, bias_div=d, has_sink=False, name=f"dil{gi}_fwd_{tag}")
        st["dil"].append((qg, kg, vg, sink, og, lg))
        outs.append(_from_dil(og, d))
        lses.append(_from_dil(lg, d))
    st["dil_outs"], st["dil_lses"] = outs, lses
    o_dil = _dil_merge(outs, lses, None, name=f"dil_merge_fwd_{tag}")

    sink = jnp.broadcast_to(sinks_l.reshape(H_SWA_Q, 1, 1), (H_SWA_Q, 1, LANES))
    st["swa"] = (hq(q_sw), hk(k_sw), hk(v_sw), sink)
    o_sw, l_sw = _band_fwd(*st["swa"][:3], bias[H_DIL:], sink, g=H_SWA_Q // H_SWA_KV, bias_div=1, has_sink=True,
                           name=f"swa_fwd_{tag}")
    st["swa_out"] = (o_sw, l_sw)
    return (_unheads(o_sb), _unheads(o_dil), _unheads(o_sw)), st


def _mixer_bwd(st, bias, do_sb, do_dil, do_swa, tag):
    dq_sb, dk_sb, dv_sb = _sb_bwd(*st["sb"], st["sb_tot"], _heads(do_sb), name=f"sb_bwd_{tag}")

    dmerge = _dil_merge(st["dil_outs"], st["dil_lses"], _heads(do_dil), name=f"dil_merge_bwd_{tag}")
    dqs, dks, dvs, dbs = [], [], [], []
    for gi, (_, d) in enumerate(DIL_PATTERNS):
        qg, kg, vg, sink, og, lg = st["dil"][gi]
        hs = slice(2 * gi, 2 * gi + 2)
        dq, dk, dv, db, _ = _band_bwd(qg, kg, vg, bias[hs], sink, og, lg, _to_dil(dmerge[gi], d), _to_dil(dmerge[3 + gi], d),
                                      g=1, bias_div=d, has_sink=False, name=f"dil{gi}_bwd_{tag}")
        dqs.append(_from_dil(dq, d))
        dks.append(_from_dil(dk, d))
        dvs.append(_from_dil(dv, d))
        dbs.append(db)

    q_sw, k_sw, v_sw, sink = st["swa"]
    o_sw, l_sw = st["swa_out"]
    dq_sw, dk_sw, dv_sw, db_sw, dsink = _band_bwd(q_sw, k_sw, v_sw, bias[H_DIL:], sink, o_sw, l_sw, _heads(do_swa),
                                                  jnp.zeros_like(l_sw), g=H_SWA_Q // H_SWA_KV, bias_div=1, has_sink=True,
                                                  name=f"swa_bwd_{tag}")
    dqkv = jnp.concatenate(
        [_unheads(dq_sb), _unheads(dk_sb), _unheads(dv_sb),
         _unheads(jnp.concatenate(dqs, 0)), _unheads(jnp.concatenate(dks, 0)), _unheads(jnp.concatenate(dvs, 0)),
         _unheads(dq_sw), _unheads(dk_sw), _unheads(dv_sw)], axis=1)
    return dqkv, jnp.concatenate(dbs + [db_sw], 0), dsink[:, 0, 0]


PIECES = ("ffn0", "mix", "ffn1")


def _ffn_fwd(x_in, w, gain, mod_j, tag):
    st = {"x": x_in, "w": w}
    st["h"] = _norm_fwd(x_in, _row(gain), _row(mod_j[1]), _row(mod_j[0]), name=f"norm_fwd_{tag}")
    st["a"], st["u"], st["s"] = _ffn_up(st["h"], w["gate"], w["up"], name=f"up_{tag}")
    st["f"], x_out = _mm(st["s"], w["down"], res=x_in, colscale=_row(0.5 * mod_j[2]), emit_acc=True, name=f"down_{tag}")
    return x_out, st


def _ffn_bwd(dx_out, st, gain, mod_j, tag):
    w = st["w"]
    df, dgate = _gate_bwd(dx_out, st["f"], _row(0.5 * mod_j[2]), 0.5, name=f"gate_bwd_{tag}")
    grads = {"down": _mm(st["s"], df, ta=True, out_dtype=BF16, name=f"dwd_{tag}")}
    da, du = _ffn_bwd_ds(df, w["down"], st["a"], st["u"], name=f"ds_{tag}")
    grads["gate"] = _mm(st["h"], da, ta=True, out_dtype=BF16, name=f"dwg_{tag}")
    grads["up"] = _mm(st["h"], du, ta=True, out_dtype=BF16, name=f"dwu_{tag}")
    dh = _mm(da, w["gate"], tb=True, name=f"dh_a_{tag}")
    dh = _mm(du, w["up"], tb=True, res=dh, name=f"dh_u_{tag}")
    dx_in, sum_dh, sum_dhx = _norm_bwd(st["x"], dh, dx_out, _row(gain), _row(mod_j[1]), name=f"norm_bwd_{tag}")
    dmod = jnp.concatenate([sum_dh, gain * sum_dhx, dgate], 0)
    return dx_in, grads, dmod, (1.0 + mod_j[1]) * sum_dhx[0]


def _mix_fwd(x_in, w, gain, mod_j, bias, sinks_l, tag):
    st = {"x": x_in, "w": w}
    st["h"] = _norm_fwd(x_in, _row(gain), _row(mod_j[1]), _row(mod_j[0]), name=f"norm_fwd_mix_{tag}")
    qkv = _mm(st["h"], w["qkv"], name=f"qkv_{tag}")
    st["gates"] = _mm(st["h"], w["gates"], name=f"gates_{tag}")
    st["o"], st["mix"] = _mixer_fwd(qkv, bias, sinks_l, tag)
    st["merged"] = _merge_fwd(*st["o"], st["gates"], w["br_sb"], w["br_dil"], w["br_swa"], name=f"merge_fwd_{tag}")
    st["f"], x_out = _mm(st["merged"], w["out"], res=x_in, colscale=_row(mod_j[2]), emit_acc=True, name=f"out_{tag}")
    return x_out, st


def _mix_bwd(dx_out, st, gain, mod_j, bias, tag):
    w = st["w"]
    df, dgate = _gate_bwd(dx_out, st["f"], _row(mod_j[2]), 1.0, name=f"gate_bwd_mix_{tag}")
    g = {"out": _mm(st["merged"], df, ta=True, out_dtype=BF16, name=f"dw_out_{tag}")}
    dmerged = _mm(df, w["out"], tb=True, name=f"dmerged_{tag}")
    dgates, do_sb, do_dil, do_swa, dbr_sb, dbr_dil, dbr_swa = _merge_bwd(
        dmerged, *st["o"], st["gates"], w["br_sb"], w["br_dil"], w["br_swa"], name=f"merge_bwd_{tag}")
    g["br_sb"] = _mm(st["o"][0], dbr_sb, ta=True, out_dtype=BF16, name=f"dw_br_sb_{tag}")
    g["br_dil"] = _mm(st["o"][1], dbr_dil, ta=True, out_dtype=BF16, name=f"dw_br_dil_{tag}")
    g["br_swa"] = _mm(st["o"][2], dbr_swa, ta=True, out_dtype=BF16, name=f"dw_br_swa_{tag}")
    dqkv, dbias, dsinks = _mixer_bwd(st["mix"], bias, do_sb, do_dil, do_swa, tag)
    g["qkv"] = _mm(st["h"], dqkv, ta=True, out_dtype=BF16, name=f"dw_qkv_{tag}")
    g["gates"] = _mm(st["h"], dgates, ta=True, out_dtype=BF16, name=f"dw_gates_{tag}")
    dh = _mm(dqkv, w["qkv"], tb=True, name=f"dh_qkv_{tag}")
    dh = _mm(dgates, w["gates"], tb=True, res=dh, name=f"dh_gates_{tag}")
    dx_in, sum_dh, sum_dhx = _norm_bwd(st["x"], dh, dx_out, _row(gain), _row(mod_j[1]), name=f"norm_bwd_mix_{tag}")
    dmod = jnp.concatenate([sum_dh, gain * sum_dhx, dgate], 0)
    return dx_in, g, dmod, (1.0 + mod_j[1]) * sum_dhx[0], dbias, dsinks


def _local_step(x, target, mod, gains, weights_of, rel_bias, sinks, final_gain, grads_done):
    tables = jnp.asarray(_bucket_tables())
    bias = _bias_build(rel_bias, tables, name="bias_build")
    states, h = [], x
    for l in range(DEPTH):
        st = {}
        for j, piece in enumerate(PIECES):
            w = weights_of(l, piece, h)
            if piece == "mix":
                h, st[piece] = _mix_fwd(h, w, gains[l, j], mod[l, j], bias, sinks[l], f"l{l}")
            else:
                h, st[piece] = _ffn_fwd(h, w, gains[l, j], mod[l, j], f"{piece}_l{l}")
        states.append(st)
    loss, dx, dfinal = _final_loss(h, target, _row(final_gain), name="final_loss")
    dmods = [[None] * 3 for _ in range(DEPTH)]
    dgains = [[None] * 3 for _ in range(DEPTH)]
    dsinks = [None] * DEPTH
    dbias, zero = None, 0.0
    for l in reversed(range(DEPTH)):
        for j in reversed(range(3)):
            piece, mod_j = PIECES[j], mod[l, j] + zero
            if piece == "mix":
                dx, grads, dmods[l][j], dgains[l][j], db, dsinks[l] = _mix_bwd(dx, states[l][piece], gains[l, j], mod_j, bias, f"l{l}")
                dbias = db if dbias is None else dbias + db
            else:
                dx, grads, dmods[l][j], dgains[l][j] = _ffn_bwd(dx, states[l][piece], gains[l, j], mod_j, f"{piece}_l{l}")
            zero = grads_done(l, piece, grads)
    drel = _bias_grad(dbias, tables, name="bias_grad")[:, 0, :N_BUCKETS].T
    dmod = jnp.stack([jnp.stack(m) for m in dmods])
    dgain = jnp.stack([jnp.stack(g) for g in dgains])
    return loss, dx, dmod, dgain, dfinal[0], drel, jnp.stack(dsinks)


BR_ROWS = (H_SB * HEAD_DIM, 2 * HEAD_DIM, H_SWA_Q * HEAD_DIM)


def _rows_unshard(g, lead):
    _, rows, cdim = g.shape
    r = rows // lead
    return jnp.moveaxis(g.reshape(N_DEV, lead, r, cdim), 0, 1).reshape(lead, N_DEV * r, cdim)


def _rows_shard(full):
    lead, rows, cdim = full.shape
    r = rows // N_DEV
    return jnp.moveaxis(full.reshape(lead, N_DEV, r, cdim), 1, 0).reshape(N_DEV, lead * r, cdim)


def _lanes_unshard(g, lead):
    _, rows, _ = g.shape
    r = rows // lead
    return g.reshape(N_DEV, lead, r, LANES).transpose(1, 2, 0, 3).reshape(lead, r, N_DEV * LANES)


def _lanes_shard(full):
    lead, r, _ = full.shape
    return full.reshape(lead, r, N_DEV, LANES).transpose(2, 0, 1, 3).reshape(N_DEV, lead * r, LANES)


def _pack_rows(parts, dtype):
    flat = jnp.concatenate([p.astype(dtype).reshape(-1) for p in parts])
    pad = (-flat.shape[0]) % (16 * LANES)
    if pad:
        flat = jnp.concatenate([flat, jnp.zeros((pad,), dtype)])
    return flat.reshape(-1, LANES)


def _unshard(gathered, axis):
    moved = jnp.moveaxis(gathered, 0, axis)
    shape = list(moved.shape)
    shape[axis:axis + 2] = [shape[axis] * shape[axis + 1]]
    return moved.reshape(shape)


def kernel(x, c, w_ada, b_ada, norm_gain, w_ffn_gate, w_ffn_up, w_ffn_down, w_in, w_br_sb, w_br_dil, w_br_swa, w_out, sinks, rel_bias, final_gain, loss_target, m_w_ada, m_b_ada, m_norm_gain, m_w_ffn_gate, m_w_ffn_up, m_w_ffn_down, m_w_in, m_w_br_sb, m_w_br_dil, m_w_br_swa, m_w_out, m_sinks, m_rel_bias, m_final_gain, v_w_ada, v_b_ada, v_norm_gain, v_w_ffn_gate, v_w_ffn_up, v_w_ffn_down, v_w_in, v_w_br_sb, v_w_br_dil, v_w_br_swa, v_w_out, v_sinks, v_rel_bias, v_final_gain):
    me = 4 * lax.axis_index("x") + 2 * lax.axis_index("y") + lax.axis_index("c")
    d = D_MODEL
    small, = _all_gather([_pack_rows([c, norm_gain], F32)], name="gather_cond")
    c_all = small[:, :d // LANES].reshape(N_DEV, d)
    gains = _unshard(small[:, d // LANES:d // LANES + 6].reshape(N_DEV, DEPTH, 3, LANES), 2)

    cols = w_ada.shape[2]
    mod_cols = jnp.stack([_ada_fwd(c_all, w_ada[l], name=f"ada_fwd_l{l}") for l in range(DEPTH)])
    mod_all, = _all_gather([_pack_rows([mod_cols], F32)], name="gather_mod")
    mod_all = mod_all.reshape(N_DEV, -1)[:, :DEPTH * N_DEV * cols].reshape(N_DEV, DEPTH, N_DEV, cols)
    mod_mine = lax.dynamic_index_in_dim(mod_all, me, axis=2, keepdims=False)
    mod = (mod_mine.transpose(1, 0, 2).reshape(DEPTH, N_DEV * cols) + b_ada).reshape(DEPTH, 3, 3, d)

    def piece_shards(l, piece):
        bf = lambda t: t.astype(BF16)
        if piece == "mix":
            return [bf(w_in[l]), jnp.concatenate([bf(w_br_sb[l]), bf(w_br_dil[l]), bf(w_br_swa[l])], 0), bf(w_out[l])]
        i = PIECES.index(piece) // 2
        return [bf(w_ffn_gate[l, i]), bf(w_ffn_up[l, i]), bf(w_ffn_down[l, i])]

    br_off = np.concatenate([[0], np.cumsum(BR_ROWS)])

    def piece_weights(gathered, l, piece):
        if piece == "mix":
            g_in, g_br, g_out = gathered
            f_qkv, f_gates = _unshard_cols(g_in, [D_QKV, D_GATES], name=f"unshard_in_l{l}")
            f_br = [_lanes_unshard(g_br[:, br_off[k]:br_off[k + 1]], 1)[0] for k in range(3)]
            return {"qkv": f_qkv, "gates": f_gates, "br_sb": f_br[0], "br_dil": f_br[1], "br_swa": f_br[2],
                    "out": g_out.reshape(d, d)}
        g_gate, g_up, g_down = gathered
        gate, = _unshard_cols(g_gate, [D_FF], name=f"unshard_gate_{piece}_l{l}")
        up, = _unshard_cols(g_up, [D_FF], name=f"unshard_up_{piece}_l{l}")
        return {"gate": gate, "up": up, "down": g_down.reshape(D_FF, d)}

    order = [(l, piece) for l in range(DEPTH) for piece in PIECES]
    ahead = 2
    in_flight = {}
    first = _all_gather(piece_shards(*order[0]), name="gather_first")

    def start_gather(k, after):
        l, piece = order[k]
        in_flight[k], token = _exchange_start(piece_shards(l, piece), after, gather=True, name=f"gather_{piece}_l{l}_start")
        return token

    token = first[0]
    for k in range(1, 1 + ahead):
        token = start_gather(k, token)
    mod = mod + token[0, 0]

    def weights_of(l, piece, h):
        k = order.index((l, piece))
        if k == 0:
            return piece_weights(first, l, piece)
        after = start_gather(k + ahead, h) if k + ahead < len(order) else h
        return piece_weights(_exchange_wait(in_flight[k], after, gather=True, name=f"gather_{piece}_l{l}_wait"), l, piece)

    exchanges = {}

    def grads_done(l, piece, g):
        if piece == "mix":
            s_br = jnp.concatenate([_lanes_shard(g[n][None]) for n in ("br_sb", "br_dil", "br_swa")], 1)
            sg = [_shard_cols([[g["qkv"], g["gates"]]], name=f"shard_in_l{l}"), s_br, g["out"].reshape(N_DEV, -1, d)]
        else:
            sg = [_shard_cols([[g["gate"]]], name=f"shard_gate_{piece}_l{l}"),
                  _shard_cols([[g["up"]]], name=f"shard_up_{piece}_l{l}"), g["down"].reshape(N_DEV, -1, d)]
        exchanges[(l, piece)], token = _exchange_start(sg, sg[0], gather=False, name=f"exchange_{piece}_l{l}_start")
        return token[0, 0]

    loss, dx, dmod, dgains, dfinal, drel, dsinks = _local_step(
        x[0], loss_target[0], mod, gains, weights_of, rel_bias, sinks, final_gain, grads_done)

    parts = {key: _exchange_wait(exchanges[key], dx, gather=False, name=f"exchange_{key[1]}_l{key[0]}_wait")
             for key in reversed(order)}
    ffn_keys = [key for key in order if key[1] != "mix"]
    mix_keys = [key for key in order if key[1] == "mix"]
    sums = [_sum_parts([parts[key][t] for key in ffn_keys], name=f"sum_grads_{n}") for t, n in enumerate(("gate", "up", "down"))]
    sums += [_sum_parts([parts[key][t] for key in mix_keys], name=f"sum_grads_{n}") for t, n in enumerate(("in", "br", "out"))]
    br_sums = sums[4].reshape(DEPTH, -1, LANES)
    gshard = {"gate": sums[0].reshape(w_ffn_gate.shape), "up": sums[1].reshape(w_ffn_up.shape),
              "down": sums[2].reshape(w_ffn_down.shape), "in": sums[3].reshape(w_in.shape),
              "br_sb": br_sums[:, br_off[0]:br_off[1]], "br_dil": br_sums[:, br_off[1]:br_off[2]],
              "br_swa": br_sums[:, br_off[2]:br_off[3]], "out": sums[5].reshape(w_out.shape)}

    small_parts = [dmod, dgains, dfinal, drel.T, dsinks, loss[0, :1]]
    small_sizes = [int(np.prod(p.shape)) for p in small_parts]
    small_all, = _all_gather([_pack_rows(small_parts, F32)], name="gather_small")
    small_sum = _sum_parts([small_all], name="sum_small").reshape(-1)
    offs = np.concatenate([[0], np.cumsum(small_sizes)])
    g_b_ada = small_sum[offs[0]:offs[1]].reshape(DEPTH, 9 * d)
    g_gain_full = small_sum[offs[1]:offs[2]].reshape(DEPTH, 3, d)
    g_norm_gain = lax.dynamic_slice_in_dim(g_gain_full, me * LANES, LANES, axis=2)
    g_final = small_sum[offs[2]:offs[3]]
    g_rel = small_sum[offs[3]:offs[4]].reshape(N_SOFT, N_BUCKETS).T
    g_sinks = small_sum[offs[4]:offs[5]].reshape(DEPTH, H_SWA_Q)
    loss_total = small_sum[offs[5]]

    dmod_all = small_all.reshape(N_DEV, -1)[:, :DEPTH * 9 * d].reshape(N_DEV, DEPTH, 9 * d)
    dmod_cols = lax.dynamic_slice_in_dim(dmod_all, me * cols, cols, axis=2)
    g_w_ada = jnp.stack([_ada_bwd(c_all.T, dmod_cols[:, l], name=f"ada_bwd_l{l}") for l in range(DEPTH)])

    names = ["w_ada", "b_ada", "norm_gain", "w_ffn_gate", "w_ffn_up", "w_ffn_down", "w_in", "w_br_sb", "w_br_dil",
             "w_br_swa", "w_out", "sinks", "rel_bias", "final_gain"]
    ws = [w_ada, b_ada, norm_gain, w_ffn_gate, w_ffn_up, w_ffn_down, w_in, w_br_sb, w_br_dil, w_br_swa, w_out, sinks,
          rel_bias, final_gain]
    gs = [g_w_ada, g_b_ada, g_norm_gain, gshard["gate"], gshard["up"], gshard["down"], gshard["in"], gshard["br_sb"],
          gshard["br_dil"], gshard["br_swa"], gshard["out"], g_sinks, g_rel, g_final]
    ms = [m_w_ada, m_b_ada, m_norm_gain, m_w_ffn_gate, m_w_ffn_up, m_w_ffn_down, m_w_in, m_w_br_sb, m_w_br_dil,
          m_w_br_swa, m_w_out, m_sinks, m_rel_bias, m_final_gain]
    vs = [v_w_ada, v_b_ada, v_norm_gain, v_w_ffn_gate, v_w_ffn_up, v_w_ffn_down, v_w_in, v_w_br_sb, v_w_br_dil,
          v_w_br_swa, v_w_out, v_sinks, v_rel_bias, v_final_gain]
    deltas, new_ms, new_vs = [], [], []
    for n, w, g, m, v in zip(names, ws, gs, ms, vs):
        if w.ndim == 1:
            dl, nm, nv = (t.reshape(w.shape) for t in _adamw(_row(w), _row(g), _row(m), _row(v), name=f"adamw_{n}"))
        else:
            dl, nm, nv = _adamw(w, g, m, v, name=f"adamw_{n}")
        deltas.append(dl)
        new_ms.append(nm)
        new_vs.append(nv)
    return (loss_total, dx[None], *gs, *deltas, *new_ms, *new_vs)
```

```python
import math

import numpy as np
import jax
import jax.numpy as jnp
from jax import lax
from jax.experimental import pallas as pl
from jax.experimental.pallas import tpu as pltpu

F32, BF16 = jnp.float32, jnp.bfloat16

SEQ, D_MODEL, D_FF, HEAD_DIM = 2048, 1024, 2816, 64
DEPTH = 2
BLK = 128
H_SB, H_DIL, H_SWA_Q, H_SWA_KV = 4, 6, 6, 2
DIL_PATTERNS = ((128, 1), (512, 4), (2048, 16))
SWA_WINDOW = 128
N_BUCKETS, MAX_REL_DIST = 32, 2048
RMS_EPS = 1e-6
D_QKV = 2560
D_GATES = 3 * D_MODEL
QKV_SPLITS = (256, 256, 256, 384, 384, 384, 384, 128, 128)
ADAM_LR, ADAM_B1, ADAM_B2, ADAM_EPS, ADAM_WD, ADAM_STEP = 0.001, 0.9, 0.999, 1e-08, 0.01, 10

N_DEV = 8
LANES = 128
NEG = -1e30
SB_TILE = 256
VMEM_LIMIT_BYTES = 48 * 1024 * 1024
HBM = pl.BlockSpec(memory_space=pltpu.HBM)
MESH = pl.DeviceIdType.MESH


def _tile(n, target):
    t = (min(n, target) // LANES) * LANES
    while t >= LANES:
        if n % t == 0:
            return t
        t -= LANES
    return n


def _row_tile(r, cap):
    t = (min(r, cap) // 16) * 16
    while t > 16 and r % t:
        t -= 16
    return t


def _params(semantics=None):
    return pltpu.CompilerParams(dimension_semantics=semantics, vmem_limit_bytes=VMEM_LIMIT_BYTES)


def _dot(a, b, ca, cb):
    return lax.dot_general(a, b, (((ca,), (cb,)), ((), ())), preferred_element_type=F32)


def _sigmoid(a):
    return 1.0 / (1.0 + jnp.exp(-a))


def _row(v):
    return v.reshape(1, -1)


def _all_gather(arrs, name):
    n = len(arrs)

    def body(*refs):
        x_refs, out_refs = refs[:n], refs[n:2 * n]
        send_sems, recv_sems, local_sems = refs[2 * n:]
        x, y, c = lax.axis_index("x"), lax.axis_index("y"), lax.axis_index("c")
        me, sibling = (x, y, c), (x, y, 1 - c)
        chips = [(1 - x, y), (x, 1 - y), (1 - x, 1 - y)]

        def slot(t, px, py, pc):
            return out_refs[t].at[4 * px + 2 * py + pc]

        def copy(t, k, block, to, src=None):
            return pltpu.make_async_remote_copy(
                src_ref=slot(t, *block) if src is None else src, dst_ref=slot(t, *block),
                send_sem=send_sems.at[7 * t + k], recv_sem=recv_sems.at[7 * t + k], device_id=to, device_id_type=MESH)

        mine = [pltpu.make_async_copy(x_refs[t], slot(t, *me), local_sems.at[t]) for t in range(n)]
        for cp in mine:
            cp.start()
        first = []
        for t in range(n):
            first.append(copy(t, 0, me, sibling, src=x_refs[t]))
            first += [copy(t, 1 + j, me, (*chip, c), src=x_refs[t]) for j, chip in enumerate(chips)]
        for cp in first:
            cp.start()
        passed = []
        for j, chip in enumerate(chips):
            for t in range(n):
                copy(t, 1 + j, (*chip, c), me).wait_recv()
                passed.append(copy(t, 4 + j, (*chip, c), sibling))
                passed[-1].start()
        for t in range(n):
            copy(t, 0, sibling, me).wait_recv()
        for j, chip in enumerate(chips):
            for t in range(n):
                copy(t, 4 + j, (*chip, 1 - c), me).wait_recv()
        for cp in first + passed:
            cp.wait_send()
        for cp in mine:
            cp.wait()

    return pl.pallas_call(
        body, name=name, out_shape=[jax.ShapeDtypeStruct((N_DEV,) + a.shape, a.dtype) for a in arrs],
        in_specs=[HBM] * n, out_specs=[HBM] * n,
        scratch_shapes=[pltpu.SemaphoreType.DMA((7 * n,)), pltpu.SemaphoreType.DMA((7 * n,)), pltpu.SemaphoreType.DMA((n,))],
    )(*arrs)


def _all_to_all(arrs, name):
    n = len(arrs)

    def body(*refs):
        x_refs, out_refs = refs[:n], refs[n:2 * n]
        send_sems, recv_sems, local_sems = refs[2 * n:]
        x, y, c = lax.axis_index("x"), lax.axis_index("y"), lax.axis_index("c")
        me = 4 * x + 2 * y + c
        mine = [pltpu.make_async_copy(x_refs[t].at[me], out_refs[t].at[me], local_sems.at[t]) for t in range(n)]
        for cp in mine:
            cp.start()
        sends, recvs = [], []
        for k in range(1, N_DEV):
            px = 1 - x if (k >> 2) & 1 else x
            py = 1 - y if (k >> 1) & 1 else y
            pc = 1 - c if k & 1 else c
            peer = 4 * px + 2 * py + pc
            for t in range(n):
                sem = 7 * t + k - 1
                sends.append(pltpu.make_async_remote_copy(
                    src_ref=x_refs[t].at[peer], dst_ref=out_refs[t].at[me], send_sem=send_sems.at[sem],
                    recv_sem=recv_sems.at[sem], device_id=(px, py, pc), device_id_type=MESH))
                recvs.append(pltpu.make_async_remote_copy(
                    src_ref=x_refs[t].at[me], dst_ref=out_refs[t].at[peer], send_sem=send_sems.at[sem],
                    recv_sem=recv_sems.at[sem], device_id=(px, py, pc), device_id_type=MESH))
        for cp in sends:
            cp.start()
        for cp in recvs:
            cp.wait_recv()
        for cp in sends:
            cp.wait_send()
        for cp in mine:
            cp.wait()

    return pl.pallas_call(
        body, name=name, out_shape=[jax.ShapeDtypeStruct(a.shape, a.dtype) for a in arrs],
        in_specs=[HBM] * n, out_specs=[HBM] * n,
        scratch_shapes=[pltpu.SemaphoreType.DMA((7 * n,)), pltpu.SemaphoreType.DMA((7 * n,)), pltpu.SemaphoreType.DMA((n,))],
    )(*arrs)


def _direct_copies(x_refs, land_refs, send_sems, recv_sems, local_sems, gather):
    x, y, c = lax.axis_index("x"), lax.axis_index("y"), lax.axis_index("c")
    me = 4 * x + 2 * y + c
    sends, recvs = [], []
    for k in range(1, N_DEV):
        px = 1 - x if (k >> 2) & 1 else x
        py = 1 - y if (k >> 1) & 1 else y
        pc = 1 - c if k & 1 else c
        peer = 4 * px + 2 * py + pc
        for t, (x_ref, land_ref) in enumerate(zip(x_refs, land_refs)):
            sem = 7 * t + k - 1
            for out, src, slot in ((sends, x_ref if gather else x_ref.at[peer], me),
                                   (recvs, x_ref if gather else x_ref.at[me], peer)):
                out.append(pltpu.make_async_remote_copy(
                    src_ref=src, dst_ref=land_ref.at[slot], send_sem=send_sems.at[sem], recv_sem=recv_sems.at[sem],
                    device_id=(px, py, pc), device_id_type=MESH))
    own = [pltpu.make_async_copy(x_ref if gather else x_ref.at[me], land_ref.at[me], local_sems.at[t])
           for t, (x_ref, land_ref) in enumerate(zip(x_refs, land_refs))]
    return sends, recvs, own


SEM =pl.BlockSpec(memory_space=pltpu.SEMAPHORE)
ANY = pl.BlockSpec(memory_space=pl.ANY)
SIDE_EFFECT = pltpu.SideEffectType.DATAFLOW_SIDE_EFFECTING


def _exchange_start(arrs, after, *, gather, name):
    n = len(arrs)
    lands = [lax.empty(((N_DEV,) + a.shape) if gather else a.shape, a.dtype) for a in arrs]

    def body(*refs):
        sends, _, own = _direct_copies(refs[:n], refs[n:2 * n], *refs[2 * n + 1:2 * n + 4], gather)
        for cp in own + sends:
            cp.start()
        refs[-1][...] = jnp.zeros_like(refs[-1])

    ops = [pltpu.with_memory_space_constraint(a, pltpu.HBM) for a in list(arrs) + lands]
    out = pl.pallas_call(
        body, name=name,
        out_shape=(pltpu.SemaphoreType.DMA((7 * n,)), pltpu.SemaphoreType.DMA((7 * n,)), pltpu.SemaphoreType.DMA((n,)),
                   *[pltpu.HBM(a.shape, a.dtype) for a in ops], jax.ShapeDtypeStruct((8, LANES), F32)),
        in_specs=[HBM] * (2 * n) + [ANY],
        out_specs=(SEM, SEM, SEM, *[HBM] * (2 * n), pl.BlockSpec(memory_space=pltpu.VMEM)),
        input_output_aliases={t: 3 + t for t in range(2 * n)},
        compiler_params=pltpu.CompilerParams(has_side_effects=SIDE_EFFECT),
    )(*ops, after)
    return (out[:3], out[3:3 + n], out[3 + n:3 + 2 * n]), out[-1]


def _exchange_wait(state, after, *, gather, name):
    sems, arrs, lands = state
    n = len(arrs)

    def body(*refs):
        sends, recvs, own = _direct_copies(refs[:n], refs[n:2 * n], *refs[2 * n:2 * n + 3], gather)
        for cp in own:
            cp.wait()
        for cp in sends:
            cp.wait_send()
        for cp in recvs:
            cp.wait_recv()

    out = pl.pallas_call(
        body, name=name, out_shape=tuple(pltpu.HBM(a.shape, a.dtype) for a in list(arrs) + list(lands)),
        in_specs=[HBM] * (2 * n) + [SEM, SEM, SEM, ANY], out_specs=tuple([HBM] * (2 * n)),
        input_output_aliases={t: t for t in range(2 * n)},
        compiler_params=pltpu.CompilerParams(has_side_effects=SIDE_EFFECT),
    )(*arrs, *lands, *sems, after)
    return out[n:]


def _col_pieces(w, widths):
    pieces, lo = [], 0
    for part, width in enumerate(widths):
        for p in range(N_DEV):
            a, b = max(lo, p * w), min(lo + width, (p + 1) * w)
            if a < b:
                pieces.append((p, part, a - p * w, a - lo, b - a))
        lo += width
    return pieces


def _unshard_cols(g, widths, name):
    _, r, w = g.shape
    tr = 256
    pieces = _col_pieces(w, widths)

    def body(g_ref, *o_refs):
        for p, part, s0, d0, size in pieces:
            o_refs[part][:, d0:d0 + size] = g_ref[p, :, s0:s0 + size]

    return pl.pallas_call(
        body, name=name, out_shape=[jax.ShapeDtypeStruct((r, wd), g.dtype) for wd in widths], grid=(r // tr,),
        in_specs=[pl.BlockSpec((N_DEV, tr, w), lambda i: (0, i, 0))],
        out_specs=[pl.BlockSpec((tr, wd), lambda i: (i, 0)) for wd in widths],
        compiler_params=_params(("parallel",)),
    )(g)


def _shard_cols(groups, name):
    widths = [a.shape[1] for a in groups[0]]
    r = groups[0][0].shape[0]
    w = sum(widths) // N_DEV
    tr = 256
    steps = r // tr
    pieces = _col_pieces(w, widths)
    nparts = len(widths)
    dtype = groups[0][0].dtype

    def body(*refs):
        o_ref = refs[-1]
        gg = pl.program_id(0)
        for gi in range(len(groups)):
            @pl.when(gg == gi)
            def _(gi=gi):
                for p, part, s0, d0, size in pieces:
                    o_ref[p, :, s0:s0 + size] = refs[gi * nparts + part][:, d0:d0 + size]

    def in_spec(gi, wd):
        return pl.BlockSpec((tr, wd), lambda gg, i: (jnp.where(gg == gi, i, 0), 0))

    return pl.pallas_call(
        body, name=name, out_shape=jax.ShapeDtypeStruct((N_DEV, len(groups) * r, w), dtype), grid=(len(groups), steps),
        in_specs=[in_spec(gi, wd) for gi in range(len(groups)) for wd in widths],
        out_specs=pl.BlockSpec((N_DEV, tr, w), lambda gg, i: (0, gg * steps + i, 0)),
        compiler_params=_params(("parallel", "parallel")),
    )(*[a for grp in groups for a in grp])


def _sum_parts(groups, name):
    n, r, cdim = groups[0].shape
    tr = _row_tile(r, max(16, (1 << 21) // (n * cdim * groups[0].dtype.itemsize)))
    steps = r // tr

    def body(*refs):
        o_ref = refs[-1]
        gg = pl.program_id(0)
        for gi in range(len(groups)):
            @pl.when(gg == gi)
            def _(gi=gi):
                acc = refs[gi][0].astype(F32)
                for k in range(1, n):
                    acc = acc + refs[gi][k].astype(F32)
                o_ref[...] = acc

    def in_spec(gi):
        return pl.BlockSpec((n, tr, cdim), lambda gg, i: (0, jnp.where(gg == gi, i, 0), 0))

    return pl.pallas_call(
        body, name=name, out_shape=jax.ShapeDtypeStruct((len(groups) * r, cdim), F32), grid=(len(groups), steps),
        in_specs=[in_spec(gi) for gi in range(len(groups))],
        out_specs=pl.BlockSpec((tr, cdim), lambda gg, i: (gg * steps + i, 0)),
        compiler_params=_params(("parallel", "parallel")),
    )(*groups)


def _mm_tn(a, b, *, name, after=None, tm=512, tn=512):
    k, m = a.shape
    n = b.shape[1]
    tm, tn = _tile(m, tm), _tile(n, tn)

    def body(a_ref, b_ref, *rest):
        o_ref, at_ref = rest[-2], rest[-1]

        @pl.when(pl.program_id(1) == 0)
        def _():
            at_ref[...] = a_ref[...].astype(BF16).T

        o_ref[...] = _dot(at_ref[...], b_ref[...].astype(BF16), 1, 0).astype(BF16)

    ins = [a, b] + ([] if after is None else [after])
    return pl.pallas_call(
        body, name=name, out_shape=jax.ShapeDtypeStruct((m, n), BF16), grid=(m // tm, n // tn),
        in_specs=[pl.BlockSpec((k, tm), lambda i, j: (0, i)), pl.BlockSpec((k, tn), lambda i, j: (0, j))] + [ANY] * (len(ins) - 2),
        out_specs=pl.BlockSpec((tm, tn), lambda i, j: (i, j)),
        scratch_shapes=[pltpu.VMEM((tm, k), BF16)], compiler_params=_params(("parallel", "arbitrary")),
    )(*ins)


def _mm_nt2(a1, b1, a2, b2, *, name, after=None, tm=512, tn=512):
    m, k = a1.shape
    n = b1.shape[0]
    tm, tn = _tile(m, tm), _tile(n, tn)

    def body(a1_ref, b1_ref, a2_ref, b2_ref, *rest):
        rest[-1][...] = (_dot(a1_ref[...].astype(BF16), b1_ref[...], 1, 1)
                         + _dot(a2_ref[...].astype(BF16), b2_ref[...], 1, 1))

    ins = [a1, b1, a2, b2] + ([] if after is None else [after])

    def a_spec(t):
        return pl.BlockSpec((tm, t.shape[1]), lambda i, j: (i, 0))

    def b_spec(t):
        return pl.BlockSpec((tn, t.shape[1]), lambda i, j: (j, 0))

    return pl.pallas_call(
        body, name=name, out_shape=jax.ShapeDtypeStruct((m, n), F32), grid=(m // tm, n // tn),
        in_specs=[a_spec(a1), b_spec(b1), a_spec(a2), b_spec(b2)] + [ANY] * (len(ins) - 4),
        out_specs=pl.BlockSpec((tm, tn), lambda i, j: (i, j)), compiler_params=_params(("parallel", "parallel")),
    )(*ins)


def _mm(a, b, *, name, ta=False, tb=False, res=None, colscale=None, emit_acc=False,
        out_dtype=F32, tm=512, tn=512):
    m, k = (a.shape[1], a.shape[0]) if ta else a.shape
    n = b.shape[0] if tb else b.shape[1]
    tm, tn = _tile(m, tm), _tile(n, tn)
    ca, cb = (0 if ta else 1), (1 if tb else 0)
    a_spec = pl.BlockSpec((k, tm), lambda i, j: (0, i)) if ta else pl.BlockSpec((tm, k), lambda i, j: (i, 0))
    b_spec = pl.BlockSpec((tn, k), lambda i, j: (j, 0)) if tb else pl.BlockSpec((k, tn), lambda i, j: (0, j))
    tile = pl.BlockSpec((tm, tn), lambda i, j: (i, j))
    ins, in_specs = [a, b], [a_spec, b_spec]
    if res is not None:
        ins.append(res)
        in_specs.append(tile)
    if colscale is not None:
        ins.append(colscale)
        in_specs.append(pl.BlockSpec((1, tn), lambda i, j: (0, j)))
    n_in = len(ins)

    def body(*refs):
        outs = refs[n_in:]
        acc = _dot(refs[0][...].astype(BF16), refs[1][...].astype(BF16), ca, cb)
        val, p = acc, 2
        if res is not None:
            r_val, p = refs[p][...], p + 1
        if colscale is not None:
            val = val * refs[p][...]
        if res is not None:
            val = r_val + val
        if emit_acc:
            outs[0][...] = acc
        outs[-1][...] = val.astype(out_dtype)

    out_shape = [jax.ShapeDtypeStruct((m, n), out_dtype)]
    out_specs = [tile]
    if emit_acc:
        out_shape.insert(0, jax.ShapeDtypeStruct((m, n), F32))
        out_specs.insert(0, tile)
    out = pl.pallas_call(
        body, name=name, out_shape=out_shape, grid=(m // tm, n // tn), in_specs=in_specs, out_specs=out_specs,
        compiler_params=_params(("parallel", "parallel")),
    )(*ins)
    return out if emit_acc else out[0]


def _norm_fwd(x, g, scale, shift, name):
    s, d = x.shape
    tr = 256

    def body(x_ref, g_ref, sc_ref, sh_ref, h_ref):
        xv = x_ref[...]
        rstd = lax.rsqrt(jnp.mean(xv * xv, axis=-1, keepdims=True) + RMS_EPS)
        h_ref[...] = (xv * rstd * g_ref[...] * (1.0 + sc_ref[...]) + sh_ref[...]).astype(BF16)

    rowspec = pl.BlockSpec((1, d), lambda i: (0, 0))
    return pl.pallas_call(
        body, name=name, out_shape=jax.ShapeDtypeStruct((s, d), BF16), grid=(s // tr,),
        in_specs=[pl.BlockSpec((tr, d), lambda i: (i, 0)), rowspec, rowspec, rowspec],
        out_specs=pl.BlockSpec((tr, d), lambda i: (i, 0)),
        compiler_params=_params(("parallel",)),
    )(x, g, scale, shift)


def _norm_bwd(x, dh, dres, g, scale, name):
    s, d = x.shape
    tr = 256

    def body(x_ref, dh_ref, dr_ref, g_ref, sc_ref, dx_ref, a_ref, b_ref):
        @pl.when(pl.program_id(0) == 0)
        def _():
            a_ref[...] = jnp.zeros_like(a_ref)
            b_ref[...] = jnp.zeros_like(b_ref)

        xv = x_ref[...]
        rstd = lax.rsqrt(jnp.mean(xv * xv, axis=-1, keepdims=True) + RMS_EPS)
        xhat = xv * rstd
        dhv = dh_ref[...]
        dxhat = dhv * (g_ref[...] * (1.0 + sc_ref[...]))
        mean_term = jnp.mean(dxhat * xhat, axis=-1, keepdims=True)
        dx_ref[...] = dr_ref[...] + rstd * (dxhat - xhat * mean_term)
        a_ref[...] += jnp.sum(dhv, axis=0, keepdims=True)
        b_ref[...] += jnp.sum(dhv * xhat, axis=0, keepdims=True)

    rowspec = pl.BlockSpec((1, d), lambda i: (0, 0))
    tile = pl.BlockSpec((tr, d), lambda i: (i, 0))
    return pl.pallas_call(
        body, name=name,
        out_shape=[jax.ShapeDtypeStruct((s, d), F32), jax.ShapeDtypeStruct((1, d), F32), jax.ShapeDtypeStruct((1, d), F32)],
        grid=(s // tr,), in_specs=[tile, tile, tile, rowspec, rowspec], out_specs=[tile, rowspec, rowspec],
        compiler_params=_params(("arbitrary",)),
    )(x, dh, dres, g, scale)


def _gate_bwd(dxn, f, colscale, coef, name):
    s, d = dxn.shape
    tr = 256

    def body(dx_ref, f_ref, cs_ref, df_ref, dg_ref):
        @pl.when(pl.program_id(0) == 0)
        def _():
            dg_ref[...] = jnp.zeros_like(dg_ref)

        dxv = dx_ref[...]
        df_ref[...] = (dxv * cs_ref[...]).astype(BF16)
        dg_ref[...] += coef * jnp.sum(dxv * f_ref[...], axis=0, keepdims=True)

    rowspec = pl.BlockSpec((1, d), lambda i: (0, 0))
    tile = pl.BlockSpec((tr, d), lambda i: (i, 0))
    return pl.pallas_call(
        body, name=name, out_shape=[jax.ShapeDtypeStruct((s, d), BF16), jax.ShapeDtypeStruct((1, d), F32)],
        grid=(s // tr,), in_specs=[tile, tile, rowspec], out_specs=[tile, rowspec],
        compiler_params=_params(("arbitrary",)),
    )(dxn, f, colscale)


def _ffn_up(h, wg, wu, name):
    s, d = h.shape
    f = wg.shape[1]
    tm, tn = s, _tile(f, 256)

    def body(h_ref, wg_ref, wu_ref, a_ref, u_ref, s_ref):
        hv = h_ref[...]
        a = _dot(hv, wg_ref[...], 1, 0)
        u = _dot(hv, wu_ref[...], 1, 0)
        a_ref[...] = a.astype(BF16)
        u_ref[...] = u.astype(BF16)
        s_ref[...] = (a * _sigmoid(a) * u).astype(BF16)

    tile = pl.BlockSpec((tm, tn), lambda i, j: (i, j))
    wspec = pl.BlockSpec((d, tn), lambda i, j: (0, j))
    return pl.pallas_call(
        body, name=name,
        out_shape=[jax.ShapeDtypeStruct((s, f), BF16), jax.ShapeDtypeStruct((s, f), BF16), jax.ShapeDtypeStruct((s, f), BF16)],
        grid=(s // tm, f // tn), in_specs=[pl.BlockSpec((tm, d), lambda i, j: (i, 0)), wspec, wspec],
        out_specs=[tile, tile, tile], compiler_params=_params(("parallel", "parallel")),
    )(h, wg, wu)


def _ffn_bwd_ds(df, wd, a, u, name):
    s, d = df.shape
    f = wd.shape[0]
    tm, tn = 1024, _tile(f, 256)

    def body(df_ref, wd_ref, a_ref, u_ref, da_ref, du_ref):
        ds = _dot(df_ref[...], wd_ref[...], 1, 1)
        av = a_ref[...].astype(F32)
        sg = _sigmoid(av)
        da_ref[...] = (ds * u_ref[...].astype(F32) * (sg * (1.0 + av * (1.0 - sg)))).astype(BF16)
        du_ref[...] = (ds * (av * sg)).astype(BF16)

    tile = pl.BlockSpec((tm, tn), lambda i, j: (i, j))
    return pl.pallas_call(
        body, name=name, out_shape=[jax.ShapeDtypeStruct((s, f), BF16), jax.ShapeDtypeStruct((s, f), BF16)],
        grid=(s // tm, f // tn),
        in_specs=[pl.BlockSpec((tm, d), lambda i, j: (i, 0)), pl.BlockSpec((tn, d), lambda i, j: (j, 0)), tile, tile],
        out_specs=[tile, tile], compiler_params=_params(("parallel", "parallel")),
    )(df, wd, a, u)


def _merge_fwd(o_sb, o_dil, o_swa, gates, wb_sb, wb_dil, wb_swa, name):
    s = o_sb.shape[0]
    d = D_MODEL
    tm = 256

    def body(osb_ref, odl_ref, osw_ref, g_ref, wsb_ref, wdl_ref, wsw_ref, m_ref):
        acc = _sigmoid(g_ref[:, 0:d]) * _dot(osb_ref[...].astype(BF16), wsb_ref[...], 1, 0)
        acc += _sigmoid(g_ref[:, d:2 * d]) * _dot(odl_ref[...].astype(BF16), wdl_ref[...], 1, 0)
        acc += _sigmoid(g_ref[:, 2 * d:3 * d]) * _dot(osw_ref[...].astype(BF16), wsw_ref[...], 1, 0)
        m_ref[...] = acc.astype(BF16)

    def rows(w):
        return pl.BlockSpec((tm, w), lambda i: (i, 0))

    def whole(w):
        return pl.BlockSpec((w, d), lambda i: (0, 0))

    return pl.pallas_call(
        body, name=name, out_shape=jax.ShapeDtypeStruct((s, d), BF16), grid=(s // tm,),
        in_specs=[rows(256), rows(128), rows(384), rows(3 * d), whole(256), whole(128), whole(384)],
        out_specs=rows(d), compiler_params=_params(("parallel",)),
    )(o_sb, o_dil, o_swa, gates, wb_sb, wb_dil, wb_swa)


def _merge_bwd(dmerged, o_sb, o_dil, o_swa, gates, wb_sb, wb_dil, wb_swa, name):
    s = o_sb.shape[0]
    d = D_MODEL
    tm = 256

    def body(dm_ref, osb_ref, odl_ref, osw_ref, g_ref, wsb_ref, wdl_ref, wsw_ref,
             dg_ref, dosb_ref, dodl_ref, dosw_ref, dbsb_ref, dbdl_ref, dbsw_ref):
        dm = dm_ref[...]
        for idx, (o_ref, w_ref, do_ref, db_ref) in enumerate((
                (osb_ref, wsb_ref, dosb_ref, dbsb_ref), (odl_ref, wdl_ref, dodl_ref, dbdl_ref),
                (osw_ref, wsw_ref, dosw_ref, dbsw_ref))):
            w = w_ref[...]
            br = _dot(o_ref[...].astype(BF16), w, 1, 0)
            sg = _sigmoid(g_ref[:, idx * d:(idx + 1) * d])
            dbr = (dm * sg).astype(BF16)
            dg_ref[:, idx * d:(idx + 1) * d] = dm * br * (sg * (1.0 - sg))
            db_ref[...] = dbr
            do_ref[...] = _dot(dbr, w, 1, 1)

    def rows(w):
        return pl.BlockSpec((tm, w), lambda i: (i, 0))

    def whole(w):
        return pl.BlockSpec((w, d), lambda i: (0, 0))

    def shp(w, dt):
        return jax.ShapeDtypeStruct((s, w), dt)

    return pl.pallas_call(
        body, name=name,
        out_shape=[shp(3 * d, F32), shp(256, F32), shp(128, F32), shp(384, F32), shp(d, BF16), shp(d, BF16), shp(d, BF16)],
        grid=(s // tm,),
        in_specs=[rows(d), rows(256), rows(128), rows(384), rows(3 * d), whole(256), whole(128), whole(384)],
        out_specs=[rows(3 * d), rows(256), rows(128), rows(384), rows(d), rows(d), rows(d)],
        compiler_params=_params(("parallel",)),
    )(dmerged, o_sb, o_dil, o_swa, gates, wb_sb, wb_dil, wb_swa)


def _final_loss(x, target, g, name):
    s, d = x.shape
    tr = 256

    def body(x_ref, t_ref, g_ref, loss_ref, dx_ref, dg_ref):
        @pl.when(pl.program_id(0) == 0)
        def _():
            loss_ref[...] = jnp.zeros_like(loss_ref)
            dg_ref[...] = jnp.zeros_like(dg_ref)

        xv = x_ref[...]
        gv = g_ref[...]
        rstd = lax.rsqrt(jnp.mean(xv * xv, axis=-1, keepdims=True) + RMS_EPS)
        xhat = xv * rstd
        err = xhat * gv - t_ref[...]
        loss_ref[...] += 0.5 * jnp.sum(jnp.mean(err * err, axis=-1, keepdims=True))
        dy = err * (1.0 / d)
        dxhat = dy * gv
        mean_term = jnp.mean(dxhat * xhat, axis=-1, keepdims=True)
        dx_ref[...] = rstd * (dxhat - xhat * mean_term)
        dg_ref[...] += jnp.sum(dy * xhat, axis=0, keepdims=True)

    rowspec = pl.BlockSpec((1, d), lambda i: (0, 0))
    tile = pl.BlockSpec((tr, d), lambda i: (i, 0))
    return pl.pallas_call(
        body, name=name,
        out_shape=[jax.ShapeDtypeStruct((1, LANES), F32), jax.ShapeDtypeStruct((s, d), F32), jax.ShapeDtypeStruct((1, d), F32)],
        grid=(s // tr,), in_specs=[tile, tile, rowspec],
        out_specs=[pl.BlockSpec((1, LANES), lambda i: (0, 0)), tile, rowspec],
        compiler_params=_params(("arbitrary",)),
    )(x, target, g)


def _adamw(w, g, m, v, name):
    shape = w.shape
    cols = shape[-1]
    rows = int(np.prod(shape[:-1])) if len(shape) > 1 else 1
    tr = rows
    for cand in (1024, 512, 256, 128, 64, 32, 16, 8):
        if rows % cand == 0 and rows > cand and cand * cols * 4 <= (1 << 21):
            tr = cand
            break

    def body(w_ref, g_ref, m_ref, v_ref, d_ref, nm_ref, nv_ref):
        gv = g_ref[...]
        nm = ADAM_B1 * m_ref[...] + (1.0 - ADAM_B1) * gv
        nv = ADAM_B2 * v_ref[...] + (1.0 - ADAM_B2) * (gv * gv)
        m_hat = nm / (1.0 - ADAM_B1 ** ADAM_STEP)
        v_hat = nv / (1.0 - ADAM_B2 ** ADAM_STEP)
        d_ref[...] = -ADAM_LR * (m_hat / (jnp.sqrt(v_hat) + ADAM_EPS) + ADAM_WD * w_ref[...])
        nm_ref[...] = nm
        nv_ref[...] = nv

    tile = pl.BlockSpec((tr, cols), lambda i: (i, 0))
    flat = [t.reshape(rows, cols) for t in (w, g, m, v)]
    out = pl.pallas_call(
        body, name=name, out_shape=[jax.ShapeDtypeStruct((rows, cols), F32)] * 3, grid=(rows // tr,),
        in_specs=[tile] * 4, out_specs=[tile] * 3, compiler_params=_params(("parallel",)),
    )(*flat)
    return tuple(t.reshape(shape) for t in out)


def _ada_fwd(c_all, w, name):
    n = w.shape[1]

    def body(c_ref, w_ref, o_ref):
        cv = c_ref[...]
        o_ref[...] = jnp.dot(cv * _sigmoid(cv), w_ref[...], preferred_element_type=F32, precision=lax.Precision.HIGHEST)

    return pl.pallas_call(body, name=name, out_shape=jax.ShapeDtypeStruct((N_DEV, n), F32), compiler_params=_params())(c_all, w)


def _ada_bwd(c_all_t, dmod, name):
    n = dmod.shape[1]

    def body(c_ref, d_ref, o_ref):
        cv = c_ref[...]
        o_ref[...] = jnp.dot(cv * _sigmoid(cv), d_ref[...], preferred_element_type=F32, precision=lax.Precision.HIGHEST)

    return pl.pallas_call(body, name=name, out_shape=jax.ShapeDtypeStruct((D_MODEL, n), F32), compiler_params=_params())(c_all_t, dmod)


def _bucket_tables():
    rel = np.arange(BLK)[:, None] + BLK - np.arange(2 * BLK)[None, :]
    max_exact = N_BUCKETS // 2

    def bucket(n):
        nf = np.maximum(n, 1).astype(np.float32)
        large = max_exact + (np.log(nf / np.float32(max_exact)) / np.float32(math.log(MAX_REL_DIST / max_exact))
                             * np.float32(N_BUCKETS - max_exact)).astype(np.int32)
        return np.where(n < max_exact, n, np.minimum(large, N_BUCKETS - 1))

    tabs = []
    for dil, max_dist in ((1, 128), (4, 128), (16, 128), (1, SWA_WINDOW - 1)):
        in_band = (rel >= 0) & (rel <= max_dist)
        tabs.append(np.where(in_band, bucket(np.maximum(rel, 0) * dil), -1))
    return np.stack(tabs).astype(np.int32)


N_SOFT = H_DIL + H_SWA_Q


def _table_of_head(h):
    return jnp.minimum(h // 2, 3)


def _bias_build(rel_bias, tables, name):
    def body(rel_ref, t_ref, o_ref):
        h = pl.program_id(0)
        tb = t_ref[0]
        out = jnp.full((BLK, 2 * BLK), NEG, F32)
        for b in range(N_BUCKETS):
            out = jnp.where(tb == b, rel_ref[b, h], out)
        o_ref[0] = out

    return pl.pallas_call(
        body, name=name, out_shape=jax.ShapeDtypeStruct((N_SOFT, BLK, 2 * BLK), F32), grid=(N_SOFT,),
        in_specs=[pl.BlockSpec(memory_space=pltpu.SMEM),
                  pl.BlockSpec((1, BLK, 2 * BLK), lambda h: (_table_of_head(h), 0, 0))],
        out_specs=pl.BlockSpec((1, BLK, 2 * BLK), lambda h: (h, 0, 0)),
        compiler_params=_params(("parallel",)),
    )(rel_bias, tables)


def _bias_grad(dbias, tables, name):
    def body(d_ref, t_ref, o_ref):
        tb = t_ref[0]
        dv = d_ref[0]
        lane = lax.broadcasted_iota(jnp.int32, (1, LANES), 1)
        out = jnp.zeros((1, LANES), F32)
        for b in range(N_BUCKETS):
            out = jnp.where(lane == b, jnp.sum(jnp.where(tb == b, dv, 0.0)), out)
        o_ref[0] = out

    return pl.pallas_call(
        body, name=name, out_shape=jax.ShapeDtypeStruct((N_SOFT, 1, LANES), F32), grid=(N_SOFT,),
        in_specs=[pl.BlockSpec((1, BLK, 2 * BLK), lambda h: (h, 0, 0)),
                  pl.BlockSpec((1, BLK, 2 * BLK), lambda h: (_table_of_head(h), 0, 0))],
        out_specs=pl.BlockSpec((1, 1, LANES), lambda h: (h, 0, 0)),
        compiler_params=_params(("parallel",)),
    )(dbias, tables)


def _band_specs(g, bias_div):
    qspec = pl.BlockSpec((1, BLK, HEAD_DIM), lambda n, i: (n, i, 0))
    prev = pl.BlockSpec((1, BLK, HEAD_DIM), lambda n, i: (n // g, jnp.maximum(i - 1, 0), 0))
    cur = pl.BlockSpec((1, BLK, HEAD_DIM), lambda n, i: (n // g, i, 0))
    bspec = pl.BlockSpec((1, BLK, 2 * BLK), lambda n, i: (n // bias_div, 0, 0))
    sspec = pl.BlockSpec((1, 1, LANES), lambda n, i: (n, 0, 0))
    colspec = pl.BlockSpec((1, BLK, 1), lambda n, i: (n, i, 0))
    return qspec, prev, cur, bspec, sspec, colspec


def _band_scores(q_ref, kp_ref, kc_ref, b_ref, first):
    qv = q_ref[0]
    bv = b_ref[0]
    sp = _dot(qv, kp_ref[0], 1, 1) + bv[:, :BLK]
    sp = jnp.where(first, NEG, sp)
    sc = _dot(qv, kc_ref[0], 1, 1) + bv[:, BLK:]
    return sp, sc


def _band_fwd(q, k, v, bias, sink, *, g, bias_div, has_sink, name):
    nq, length, _ = q.shape

    def body(q_ref, kp_ref, kc_ref, vp_ref, vc_ref, b_ref, s_ref, o_ref, lse_ref):
        sp, sc = _band_scores(q_ref, kp_ref, kc_ref, b_ref, pl.program_id(1) == 0)
        m = jnp.maximum(jnp.max(sp, axis=1, keepdims=True), jnp.max(sc, axis=1, keepdims=True))
        if has_sink:
            sk = s_ref[0][:, :1]
            m = jnp.maximum(m, sk)
        pp, pc = jnp.exp(sp - m), jnp.exp(sc - m)
        den = jnp.sum(pp, axis=1, keepdims=True) + jnp.sum(pc, axis=1, keepdims=True)
        if has_sink:
            den = den + jnp.exp(sk - m)
        acc = _dot(pp.astype(BF16), vp_ref[0], 1, 0) + _dot(pc.astype(BF16), vc_ref[0], 1, 0)
        o_ref[0] = acc / den
        lse_ref[0] = m + jnp.log(den)

    qspec, prev, cur, bspec, sspec, colspec = _band_specs(g, bias_div)
    return pl.pallas_call(
        body, name=name,
        out_shape=[jax.ShapeDtypeStruct((nq, length, HEAD_DIM), F32), jax.ShapeDtypeStruct((nq, length, 1), F32)],
        grid=(nq, length // BLK), in_specs=[qspec, prev, cur, prev, cur, bspec, sspec],
        out_specs=[qspec, colspec], compiler_params=_params(("parallel", "parallel")),
    )(q, k, k, v, v, bias, sink)


def _band_bwd(q, k, v, bias, sink, o, lse, do, dlse, *, g, bias_div, has_sink, name):
    nq, length, _ = q.shape
    nk, nbias = nq // g, nq // bias_div

    def body(q_ref, kp_ref, kc_ref, vp_ref, vc_ref, b_ref, s_ref, o_ref, lse_ref, do_ref, dlse_ref,
             dq_ref, dk_ref, dv_ref, db_ref, dsk_ref):
        n, i = pl.program_id(0), pl.program_id(1)

        @pl.when((n % g == 0) & (i == 0))
        def _():
            dk_ref[...] = jnp.zeros_like(dk_ref)
            dv_ref[...] = jnp.zeros_like(dv_ref)

        @pl.when((n % bias_div == 0) & (i == 0))
        def _():
            db_ref[...] = jnp.zeros_like(db_ref)

        @pl.when(i == 0)
        def _():
            dsk_ref[...] = jnp.zeros_like(dsk_ref)

        sp, sc = _band_scores(q_ref, kp_ref, kc_ref, b_ref, i == 0)
        lse_v = lse_ref[0]
        pp, pc = jnp.exp(sp - lse_v), jnp.exp(sc - lse_v)
        dov = do_ref[0]
        dob = dov.astype(BF16)
        coef = dlse_ref[0] - jnp.sum(dov * o_ref[0], axis=1, keepdims=True)
        dsp = pp * (_dot(dob, vp_ref[0], 1, 1) + coef)
        dsc = pc * (_dot(dob, vc_ref[0], 1, 1) + coef)
        dspb, dscb = dsp.astype(BF16), dsc.astype(BF16)
        dq_ref[0] = (_dot(dspb, kp_ref[0], 1, 0) + _dot(dscb, kc_ref[0], 1, 0)) * (HEAD_DIM ** -0.5)
        qv = q_ref[0]
        cur = pl.ds(pl.multiple_of(i * BLK, BLK), BLK)
        prv = pl.ds(pl.multiple_of(jnp.maximum(i - 1, 0) * BLK, BLK), BLK)
        dk_ref[0, cur, :] += _dot(dscb, qv, 0, 0)
        dk_ref[0, prv, :] += _dot(dspb, qv, 0, 0)
        dv_ref[0, cur, :] += _dot(pc.astype(BF16), dob, 0, 0)
        dv_ref[0, prv, :] += _dot(pp.astype(BF16), dob, 0, 0)
        db_ref[0, :, :BLK] += dsp
        db_ref[0, :, BLK:] += dsc
        if has_sink:
            dsk_ref[0] += jnp.sum(jnp.exp(s_ref[0][:, :1] - lse_v) * coef)

    qspec, prev, cur, bspec, sspec, colspec = _band_specs(g, bias_div)
    kvfull = pl.BlockSpec((1, length, HEAD_DIM), lambda n, i: (n // g, 0, 0))
    return pl.pallas_call(
        body, name=name,
        out_shape=[jax.ShapeDtypeStruct((nq, length, HEAD_DIM), F32), jax.ShapeDtypeStruct((nk, length, HEAD_DIM), F32),
                   jax.ShapeDtypeStruct((nk, length, HEAD_DIM), F32), jax.ShapeDtypeStruct((nbias, BLK, 2 * BLK), F32),
                   jax.ShapeDtypeStruct((nq, 1, LANES), F32)],
        grid=(nq, length // BLK),
        in_specs=[qspec, prev, cur, prev, cur, bspec, sspec, qspec, colspec, qspec, colspec],
        out_specs=[qspec, kvfull, kvfull, bspec, sspec], compiler_params=_params(("arbitrary", "arbitrary")),
    )(q, k, k, v, v, bias, sink, o, lse, do, dlse)


def _dil_merge(os_, lses, dout, name):
    tr = 512
    n = len(os_)
    tile = pl.BlockSpec((1, tr, HEAD_DIM), lambda h, i: (h, i, 0))
    col = pl.BlockSpec((1, tr, 1), lambda h, i: (h, i, 0))

    def weights(l_refs):
        ls = [r[0] for r in l_refs]
        m = ls[0]
        for lv in ls[1:]:
            m = jnp.maximum(m, lv)
        es = [jnp.exp(lv - m) for lv in ls]
        den = es[0]
        for e in es[1:]:
            den = den + e
        return [e / den for e in es]

    if dout is None:
        def body(*refs):
            alphas = weights(refs[n:2 * n])
            acc = alphas[0] * refs[0][0]
            for gi in range(1, n):
                acc = acc + alphas[gi] * refs[gi][0]
            refs[2 * n][0] = acc

        return pl.pallas_call(
            body, name=name, out_shape=jax.ShapeDtypeStruct(os_[0].shape, F32), grid=(2, SEQ // tr),
            in_specs=[tile] * n + [col] * n, out_specs=tile, compiler_params=_params(("parallel", "parallel")),
        )(*os_, *lses)

    def body(*refs):
        alphas = weights(refs[n:2 * n])
        dov = refs[2 * n][0]
        outs = refs[2 * n + 1:]
        das = [jnp.sum(dov * refs[gi][0], axis=1, keepdims=True) for gi in range(n)]
        dbar = alphas[0] * das[0]
        for gi in range(1, n):
            dbar = dbar + alphas[gi] * das[gi]
        for gi in range(n):
            outs[gi][0] = alphas[gi] * dov
            outs[n + gi][0] = alphas[gi] * (das[gi] - dbar)

    return pl.pallas_call(
        body, name=name,
        out_shape=[jax.ShapeDtypeStruct(os_[0].shape, F32)] * n + [jax.ShapeDtypeStruct(lses[0].shape, F32)] * n,
        grid=(2, SEQ // tr), in_specs=[tile] * n + [col] * n + [tile], out_specs=[tile] * n + [col] * n,
        compiler_params=_params(("parallel", "parallel")),
    )(*os_, *lses, dout)


def _tri(cmp):
    r = lax.broadcasted_iota(jnp.int32, (SB_TILE, SB_TILE), 0)
    c = lax.broadcasted_iota(jnp.int32, (SB_TILE, SB_TILE), 1)
    return cmp(r, c).astype(BF16)


def _cum(x, tri, terms):
    acc, rest = None, x
    for _ in range(terms):
        part = rest.astype(BF16)
        rest = rest - part.astype(F32)
        d = _dot(part, tri, 1, 0)
        acc = d if acc is None else acc + d
    return acc


def _sb_logits(q, k_ref, j, i):
    t = SB_TILE
    ks = k_ref[0, pl.ds(pl.multiple_of(j * t, t), t), :]
    z = _dot(q, ks, 1, 1)
    rows = i * t + lax.broadcasted_iota(jnp.int32, (t, t), 0)
    cols = j * t + lax.broadcasted_iota(jnp.int32, (t, t), 1)
    mask = cols < rows
    e = jnp.exp(-jnp.abs(z))
    lf = jnp.where(mask, -(jnp.maximum(z, 0.0) + jnp.log(1.0 + e)), 0.0)
    return ks, z, e, lf, mask


def _sb_fwd(q, k, v, name):
    h, s, _ = q.shape
    t = SB_TILE

    def body(q_ref, k_ref, v_ref, o_ref, tot_ref):
        i = pl.program_id(1)
        qv = q_ref[0]
        after = _tri(lambda r, c: r > c)

        def step(jj, carry):
            right, acc = carry
            j = i - jj
            _, z, _, lf, mask = _sb_logits(qv, k_ref, j, i)
            between = right + _cum(lf, after, 3)
            w = jnp.where(mask, jnp.exp(z + lf + between), 0.0)
            vs = v_ref[0, pl.ds(pl.multiple_of(j * t, t), t), :]
            return right + jnp.sum(lf, axis=1, keepdims=True), acc + _dot(w.astype(BF16), vs, 1, 0)

        right, acc = lax.fori_loop(0, i + 1, step, (jnp.zeros((t, 1), F32), jnp.zeros((t, HEAD_DIM), F32)))
        o_ref[0] = acc
        tot_ref[0] = right

    tile = pl.BlockSpec((1, t, HEAD_DIM), lambda hh, i: (hh, i, 0))
    full = pl.BlockSpec((1, s, HEAD_DIM), lambda hh, i: (hh, 0, 0))
    return pl.pallas_call(
        body, name=name, out_shape=[jax.ShapeDtypeStruct((h, s, HEAD_DIM), F32), jax.ShapeDtypeStruct((h, s, 1), F32)],
        grid=(h, s // t), in_specs=[tile, full, full],
        out_specs=[tile, pl.BlockSpec((1, t, 1), lambda hh, i: (hh, i, 0))],
        compiler_params=_params(("parallel", "parallel")),
    )(q, k, v)


def _sb_bwd(q, k, v, tot, do, name):
    h, s, _ = q.shape
    t = SB_TILE

    def body(q_ref, k_ref, v_ref, tot_ref, do_ref, dq_ref, dk_ref, dv_ref):
        i = pl.program_id(1)

        @pl.when(i == 0)
        def _():
            dk_ref[...] = jnp.zeros_like(dk_ref)
            dv_ref[...] = jnp.zeros_like(dv_ref)

        qv = q_ref[0]
        dob = do_ref[0].astype(BF16)
        total = tot_ref[0]
        upto = _tri(lambda r, c: r <= c)
        before = _tri(lambda r, c: r < c)

        def step(j, carry):
            left, cleft, dq = carry
            ks, z, e, lf, mask = _sb_logits(qv, k_ref, j, i)
            rows = pl.ds(pl.multiple_of(j * t, t), t)
            vs = v_ref[0, rows, :]
            between = total - (left + _cum(lf, upto, 3))
            w = jnp.where(mask, jnp.exp(z + lf + between), 0.0)
            dlog = w * _dot(dob, vs, 1, 1)
            cfail = cleft + _cum(dlog, before, 2)
            sig = jnp.where(z >= 0.0, 1.0, e) / (1.0 + e)
            dz = jnp.where(mask, dlog * (1.0 - sig) - sig * cfail, 0.0).astype(BF16)
            dk_ref[0, rows, :] += _dot(dz, qv, 0, 0)
            dv_ref[0, rows, :] += _dot(w.astype(BF16), dob, 0, 0)
            return (left + jnp.sum(lf, axis=1, keepdims=True), cleft + jnp.sum(dlog, axis=1, keepdims=True),
                    dq + _dot(dz, ks, 1, 0))

        zero = jnp.zeros((t, 1), F32)
        _, _, dq = lax.fori_loop(0, i + 1, step, (zero, zero, jnp.zeros((t, HEAD_DIM), F32)))
        dq_ref[0] = dq * (HEAD_DIM ** -0.5)

    tile = pl.BlockSpec((1, t, HEAD_DIM), lambda hh, i: (hh, i, 0))
    full = pl.BlockSpec((1, s, HEAD_DIM), lambda hh, i: (hh, 0, 0))
    shp = jax.ShapeDtypeStruct((h, s, HEAD_DIM), F32)
    return pl.pallas_call(
        body, name=name, out_shape=[shp, shp, shp], grid=(h, s // t),
        in_specs=[tile, full, full, pl.BlockSpec((1, t, 1), lambda hh, i: (hh, i, 0)), tile],
        out_specs=[tile, full, full], compiler_params=_params(("arbitrary", "arbitrary")),
    )(q, k, v, tot, do)


def _heads(t):
    return t.reshape(SEQ, -1, HEAD_DIM).transpose(1, 0, 2)


def _unheads(t):
    return t.transpose(1, 0, 2).reshape(SEQ, -1)


def _to_dil(t, d):
    xdim = t.shape[-1]
    return t.reshape(2, SEQ // d, d, xdim).transpose(0, 2, 1, 3).reshape(2 * d, SEQ // d, xdim)


def _from_dil(t, d):
    xdim = t.shape[-1]
    return t.reshape(2, d, SEQ // d, xdim).transpose(0, 2, 1, 3).reshape(2, SEQ, xdim)


def _split_qkv(qkv):
    parts, off = [], 0
    for w in QKV_SPLITS:
        parts.append(qkv[:, off:off + w])
        off += w
    return parts


def _mixer_fwd(qkv, bias, sinks_l, tag):
    scale = HEAD_DIM ** -0.5
    q_sb, k_sb, v_sb, q_dl, k_dl, v_dl, q_sw, k_sw, v_sw = _split_qkv(qkv)
    hq = lambda t: _heads((t * scale).astype(BF16))
    hk = lambda t: _heads(t.astype(BF16))
    st = {}
    st["sb"] = (hq(q_sb), hk(k_sb), hk(v_sb))
    o_sb, st["sb_tot"] = _sb_fwd(*st["sb"], name=f"sb_fwd_{tag}")

    qd, kd, vd = hq(q_dl), hk(k_dl), hk(v_dl)
    no_sink = jnp.zeros((1, 1, LANES), F32)
    st["dil"], outs, lses = [], [], []
    for gi, (_, d) in enumerate(DIL_PATTERNS):
        hs = slice(2 * gi, 2 * gi + 2)
        qg, kg, vg = _to_dil(qd[hs], d), _to_dil(kd[hs], d), _to_dil(vd[hs], d)
        sink = jnp.broadcast_to(no_sink, (2 * d, 1, LANES))
        og, lg = _band_fwd(qg, kg, vg, bias[hs], sink, g=1, bias_div=d, has_sink=False, name=f"dil{gi}_fwd_{tag}")
        st["dil"].append((qg, kg, vg, sink, og, lg))
        outs.append(_from_dil(og, d))
        lses.append(_from_dil(lg, d))
    st["dil_outs"], st["dil_lses"] = outs, lses
    o_dil = _dil_merge(outs, lses, None, name=f"dil_merge_fwd_{tag}")

    sink = jnp.broadcast_to(sinks_l.reshape(H_SWA_Q, 1, 1), (H_SWA_Q, 1, LANES))
    st["swa"] = (hq(q_sw), hk(k_sw), hk(v_sw), sink)
    o_sw, l_sw = _band_fwd(*st["swa"][:3], bias[H_DIL:], sink, g=H_SWA_Q // H_SWA_KV, bias_div=1, has_sink=True,
                           name=f"swa_fwd_{tag}")
    st["swa_out"] = (o_sw, l_sw)
    return (_unheads(o_sb), _unheads(o_dil), _unheads(o_sw)), st


def _mixer_bwd(st, bias, do_sb, do_dil, do_swa, tag):
    dq_sb, dk_sb, dv_sb = _sb_bwd(*st["sb"], st["sb_tot"], _heads(do_sb), name=f"sb_bwd_{tag}")

    dmerge = _dil_merge(st["dil_outs"], st["dil_lses"], _heads(do_dil), name=f"dil_merge_bwd_{tag}")
    dqs, dks, dvs, dbs = [], [], [], []
    for gi, (_, d) in enumerate(DIL_PATTERNS):
        qg, kg, vg, sink, og, lg = st["dil"][gi]
        hs = slice(2 * gi, 2 * gi + 2)
        dq, dk, dv, db, _ = _band_bwd(qg, kg, vg, bias[hs], sink, og, lg, _to_dil(dmerge[gi], d), _to_dil(dmerge[3 + gi], d),
                                      g=1, bias_div=d, has_sink=False, name=f"dil{gi}_bwd_{tag}")
        dqs.append(_from_dil(dq, d))
        dks.append(_from_dil(dk, d))
        dvs.append(_from_dil(dv, d))
        dbs.append(db)

    q_sw, k_sw, v_sw, sink = st["swa"]
    o_sw, l_sw = st["swa_out"]
    dq_sw, dk_sw, dv_sw, db_sw, dsink = _band_bwd(q_sw, k_sw, v_sw, bias[H_DIL:], sink, o_sw, l_sw, _heads(do_swa),
                                                  jnp.zeros_like(l_sw), g=H_SWA_Q // H_SWA_KV, bias_div=1, has_sink=True,
                                                  name=f"swa_bwd_{tag}")
    dqkv = jnp.concatenate(
        [_unheads(dq_sb), _unheads(dk_sb), _unheads(dv_sb),
         _unheads(jnp.concatenate(dqs, 0)), _unheads(jnp.concatenate(dks, 0)), _unheads(jnp.concatenate(dvs, 0)),
         _unheads(dq_sw), _unheads(dk_sw), _unheads(dv_sw)], axis=1)
    return dqkv, jnp.concatenate(dbs + [db_sw], 0), dsink[:, 0, 0]


PIECES = ("ffn0", "mix", "ffn1")


def _ffn_fwd(x_in, w, gain, mod_j, tag):
    st = {"x": x_in, "w": w}
    st["h"] = _norm_fwd(x_in, _row(gain), _row(mod_j[1]), _row(mod_j[0]), name=f"norm_fwd_{tag}")
    st["a"], st["u"], st["s"] = _ffn_up(st["h"], w["gate"], w["up"], name=f"up_{tag}")
    st["f"], x_out = _mm(st["s"], w["down"], res=x_in, colscale=_row(0.5 * mod_j[2]), emit_acc=True, name=f"down_{tag}")
    return x_out, st


def _ffn_bwd(dx_out, st, gain, mod_j, tag, done):
    w = st["w"]
    df, dgate = _gate_bwd(dx_out, st["f"], _row(0.5 * mod_j[2]), 0.5, name=f"gate_bwd_{tag}")
    grads = {"down": _mm_tn(st["s"], df, name=f"dwd_{tag}")}
    da, du = _ffn_bwd_ds(df, w["down"], st["a"], st["u"], name=f"ds_{tag}")
    grads["gate"] = _mm_tn(st["h"], da, name=f"dwg_{tag}")
    grads["up"] = _mm_tn(st["h"], du, name=f"dwu_{tag}")
    dh = _mm_nt2(da, w["gate"], du, w["up"], after=done(grads), name=f"dh_{tag}")
    dx_in, sum_dh, sum_dhx = _norm_bwd(st["x"], dh, dx_out, _row(gain), _row(mod_j[1]), name=f"norm_bwd_{tag}")
    dmod = jnp.concatenate([sum_dh, gain * sum_dhx, dgate], 0)
    return dx_in, dmod, (1.0 + mod_j[1]) * sum_dhx[0]


def _mix_fwd(x_in, w, gain, mod_j, bias, sinks_l, tag):
    st = {"x": x_in, "w": w}
    st["h"] = _norm_fwd(x_in, _row(gain), _row(mod_j[1]), _row(mod_j[0]), name=f"norm_fwd_mix_{tag}")
    qkv = _mm(st["h"], w["qkv"], name=f"qkv_{tag}")
    st["gates"] = _mm(st["h"], w["gates"], name=f"gates_{tag}")
    st["o"], st["mix"] = _mixer_fwd(qkv, bias, sinks_l, tag)
    st["merged"] = _merge_fwd(*st["o"], st["gates"], w["br_sb"], w["br_dil"], w["br_swa"], name=f"merge_fwd_{tag}")
    st["f"], x_out = _mm(st["merged"], w["out"], res=x_in, colscale=_row(mod_j[2]), emit_acc=True, name=f"out_{tag}")
    return x_out, st


def _mix_bwd(dx_out, st, gain, mod_j, bias, tag, done):
    w = st["w"]
    df, dgate = _gate_bwd(dx_out, st["f"], _row(mod_j[2]), 1.0, name=f"gate_bwd_mix_{tag}")
    g = {"out": _mm_tn(st["merged"], df, name=f"dw_out_{tag}")}
    dmerged = _mm(df, w["out"], tb=True, name=f"dmerged_{tag}")
    dgates, do_sb, do_dil, do_swa, dbr_sb, dbr_dil, dbr_swa = _merge_bwd(
        dmerged, *st["o"], st["gates"], w["br_sb"], w["br_dil"], w["br_swa"], name=f"merge_bwd_{tag}")
    g["br_sb"] = _mm_tn(st["o"][0], dbr_sb, name=f"dw_br_sb_{tag}")
    g["br_dil"] = _mm_tn(st["o"][1], dbr_dil, name=f"dw_br_dil_{tag}")
    g["br_swa"] = _mm_tn(st["o"][2], dbr_swa, name=f"dw_br_swa_{tag}")
    dqkv, dbias, dsinks = _mixer_bwd(st["mix"], bias, do_sb, do_dil, do_swa, tag)
    g["qkv"] = _mm_tn(st["h"], dqkv, name=f"dw_qkv_{tag}")
    g["gates"] = _mm_tn(st["h"], dgates, name=f"dw_gates_{tag}")
    dh = _mm_nt2(dqkv, w["qkv"], dgates, w["gates"], after=done(g), tm=256, name=f"dh_mix_{tag}")
    dx_in, sum_dh, sum_dhx = _norm_bwd(st["x"], dh, dx_out, _row(gain), _row(mod_j[1]), name=f"norm_bwd_mix_{tag}")
    dmod = jnp.concatenate([sum_dh, gain * sum_dhx, dgate], 0)
    return dx_in, dmod, (1.0 + mod_j[1]) * sum_dhx[0], dbias, dsinks


def _local_step(x, target, mod, gains, weights_of, rel_bias, sinks, final_gain, grads_done):
    tables = jnp.asarray(_bucket_tables())
    bias = _bias_build(rel_bias, tables, name="bias_build")
    states, h = [], x
    for l in range(DEPTH):
        st = {}
        for j, piece in enumerate(PIECES):
            w = weights_of(l, piece, h)
            if piece == "mix":
                h, st[piece] = _mix_fwd(h, w, gains[l, j], mod[l, j], bias, sinks[l], f"l{l}")
            else:
                h, st[piece] = _ffn_fwd(h, w, gains[l, j], mod[l, j], f"{piece}_l{l}")
        states.append(st)
    loss, dx, dfinal = _final_loss(h, target, _row(final_gain), name="final_loss")
    dmods = [[None] * 3 for _ in range(DEPTH)]
    dgains = [[None] * 3 for _ in range(DEPTH)]
    dsinks = [None] * DEPTH
    dbias = None
    for l in reversed(range(DEPTH)):
        for j in reversed(range(3)):
            piece = PIECES[j]
            done = lambda grads, l=l, piece=piece: grads_done(l, piece, grads)
            if piece == "mix":
                dx, dmods[l][j], dgains[l][j], db, dsinks[l] = _mix_bwd(dx, states[l][piece], gains[l, j], mod[l, j], bias, f"l{l}", done)
                dbias = db if dbias is None else dbias + db
            else:
                dx, dmods[l][j], dgains[l][j] = _ffn_bwd(dx, states[l][piece], gains[l, j], mod[l, j], f"{piece}_l{l}", done)
    drel = _bias_grad(dbias, tables, name="bias_grad")[:, 0, :N_BUCKETS].T
    dmod = jnp.stack([jnp.stack(m) for m in dmods])
    dgain = jnp.stack([jnp.stack(g) for g in dgains])
    return loss, dx, dmod, dgain, dfinal[0], drel, jnp.stack(dsinks)


BR_ROWS = (H_SB * HEAD_DIM, 2 * HEAD_DIM, H_SWA_Q * HEAD_DIM)


def _rows_unshard(g, lead):
    _, rows, cdim = g.shape
    r = rows // lead
    return jnp.moveaxis(g.reshape(N_DEV, lead, r, cdim), 0, 1).reshape(lead, N_DEV * r, cdim)


def _rows_shard(full):
    lead, rows, cdim = full.shape
    r = rows // N_DEV
    return jnp.moveaxis(full.reshape(lead, N_DEV, r, cdim), 1, 0).reshape(N_DEV, lead * r, cdim)


def _lanes_unshard(g, lead):
    _, rows, _ = g.shape
    r = rows // lead
    return g.reshape(N_DEV, lead, r, LANES).transpose(1, 2, 0, 3).reshape(lead, r, N_DEV * LANES)


def _lanes_shard(full):
    lead, r, _ = full.shape
    return full.reshape(lead, r, N_DEV, LANES).transpose(2, 0, 1, 3).reshape(N_DEV, lead * r, LANES)


def _pack_rows(parts, dtype):
    flat = jnp.concatenate([p.astype(dtype).reshape(-1) for p in parts])
    pad = (-flat.shape[0]) % (16 * LANES)
    if pad:
        flat = jnp.concatenate([flat, jnp.zeros((pad,), dtype)])
    return flat.reshape(-1, LANES)


def _unshard(gathered, axis):
    moved = jnp.moveaxis(gathered, 0, axis)
    shape = list(moved.shape)
    shape[axis:axis + 2] = [shape[axis] * shape[axis + 1]]
    return moved.reshape(shape)


def kernel(x, c, w_ada, b_ada, norm_gain, w_ffn_gate, w_ffn_up, w_ffn_down, w_in, w_br_sb, w_br_dil, w_br_swa, w_out, sinks, rel_bias, final_gain, loss_target, m_w_ada, m_b_ada, m_norm_gain, m_w_ffn_gate, m_w_ffn_up, m_w_ffn_down, m_w_in, m_w_br_sb, m_w_br_dil, m_w_br_swa, m_w_out, m_sinks, m_rel_bias, m_final_gain, v_w_ada, v_b_ada, v_norm_gain, v_w_ffn_gate, v_w_ffn_up, v_w_ffn_down, v_w_in, v_w_br_sb, v_w_br_dil, v_w_br_swa, v_w_out, v_sinks, v_rel_bias, v_final_gain):
    me = 4 * lax.axis_index("x") + 2 * lax.axis_index("y") + lax.axis_index("c")
    d = D_MODEL
    small, = _all_gather([_pack_rows([c, norm_gain], F32)], name="gather_cond")
    c_all = small[:, :d // LANES].reshape(N_DEV, d)
    gains = _unshard(small[:, d // LANES:d // LANES + 6].reshape(N_DEV, DEPTH, 3, LANES), 2)

    cols = w_ada.shape[2]
    mod_cols = jnp.stack([_ada_fwd(c_all, w_ada[l], name=f"ada_fwd_l{l}") for l in range(DEPTH)])
    mod_all, = _all_gather([_pack_rows([mod_cols], F32)], name="gather_mod")
    mod_all = mod_all.reshape(N_DEV, -1)[:, :DEPTH * N_DEV * cols].reshape(N_DEV, DEPTH, N_DEV, cols)
    mod_mine = lax.dynamic_index_in_dim(mod_all, me, axis=2, keepdims=False)
    mod = (mod_mine.transpose(1, 0, 2).reshape(DEPTH, N_DEV * cols) + b_ada).reshape(DEPTH, 3, 3, d)

    def piece_shards(l, piece):
        bf = lambda t: t.astype(BF16)
        if piece == "mix":
            return [bf(w_in[l]), jnp.concatenate([bf(w_br_sb[l]), bf(w_br_dil[l]), bf(w_br_swa[l])], 0), bf(w_out[l])]
        i = PIECES.index(piece) // 2
        return [bf(w_ffn_gate[l, i]), bf(w_ffn_up[l, i]), bf(w_ffn_down[l, i])]

    br_off = np.concatenate([[0], np.cumsum(BR_ROWS)])

    def piece_weights(gathered, l, piece):
        if piece == "mix":
            g_in, g_br, g_out = gathered
            f_qkv, f_gates = _unshard_cols(g_in, [D_QKV, D_GATES], name=f"unshard_in_l{l}")
            f_br = [_lanes_unshard(g_br[:, br_off[k]:br_off[k + 1]], 1)[0] for k in range(3)]
            return {"qkv": f_qkv, "gates": f_gates, "br_sb": f_br[0], "br_dil": f_br[1], "br_swa": f_br[2],
                    "out": g_out.reshape(d, d)}
        g_gate, g_up, g_down = gathered
        gate, = _unshard_cols(g_gate, [D_FF], name=f"unshard_gate_{piece}_l{l}")
        up, = _unshard_cols(g_up, [D_FF], name=f"unshard_up_{piece}_l{l}")
        return {"gate": gate, "up": up, "down": g_down.reshape(D_FF, d)}

    order = [(l, piece) for l in range(DEPTH) for piece in PIECES]
    ahead = 2
    in_flight = {}
    first = _all_gather(piece_shards(*order[0]), name="gather_first")

    def start_gather(k, after):
        l, piece = order[k]
        in_flight[k], token = _exchange_start(piece_shards(l, piece), after, gather=True, name=f"gather_{piece}_l{l}_start")
        return token

    token = first[0]
    for k in range(1, 1 + ahead):
        token = start_gather(k, token)
    mod = mod + token[0, 0]

    def weights_of(l, piece, h):
        k = order.index((l, piece))
        if k == 0:
            return piece_weights(first, l, piece)
        after = start_gather(k + ahead, h) if k + ahead < len(order) else h
        return piece_weights(_exchange_wait(in_flight[k], after, gather=True, name=f"gather_{piece}_l{l}_wait"), l, piece)

    exchanges = {}

    def grads_done(l, piece, g):
        if piece == "mix":
            s_br = jnp.concatenate([_lanes_shard(g[n][None]) for n in ("br_sb", "br_dil", "br_swa")], 1)
            sg = [_shard_cols([[g["qkv"], g["gates"]]], name=f"shard_in_l{l}"), s_br, g["out"].reshape(N_DEV, -1, d)]
        else:
            sg = [_shard_cols([[g["gate"]]], name=f"shard_gate_{piece}_l{l}"),
                  _shard_cols([[g["up"]]], name=f"shard_up_{piece}_l{l}"), g["down"].reshape(N_DEV, -1, d)]
        exchanges[(l, piece)], token = _exchange_start(sg, sg[0], gather=False, name=f"exchange_{piece}_l{l}_start")
        return token

    loss, dx, dmod, dgains, dfinal, drel, dsinks = _local_step(
        x[0], loss_target[0], mod, gains, weights_of, rel_bias, sinks, final_gain, grads_done)

    small_parts = [dmod, dgains, dfinal, drel.T, dsinks, loss[0, :1]]
    small_sizes = [int(np.prod(p.shape)) for p in small_parts]
    small_all, = _all_gather([_pack_rows(small_parts, F32)], name="gather_small")
    small_sum = _sum_parts([small_all], name="sum_small").reshape(-1)
    offs = np.concatenate([[0], np.cumsum(small_sizes)])
    g_b_ada = small_sum[offs[0]:offs[1]].reshape(DEPTH, 9 * d)
    g_gain_full = small_sum[offs[1]:offs[2]].reshape(DEPTH, 3, d)
    g_norm_gain = lax.dynamic_slice_in_dim(g_gain_full, me * LANES, LANES, axis=2)
    g_final = small_sum[offs[2]:offs[3]]
    g_rel = small_sum[offs[3]:offs[4]].reshape(N_SOFT, N_BUCKETS).T
    g_sinks = small_sum[offs[4]:offs[5]].reshape(DEPTH, H_SWA_Q)
    loss_total = small_sum[offs[5]]

    dmod_all = small_all.reshape(N_DEV, -1)[:, :DEPTH * 9 * d].reshape(N_DEV, DEPTH, 9 * d)
    dmod_cols = lax.dynamic_slice_in_dim(dmod_all, me * cols, cols, axis=2)
    g_w_ada = jnp.stack([_ada_bwd(c_all.T, dmod_cols[:, l], name=f"ada_bwd_l{l}") for l in range(DEPTH)])

    state = {"w_ada": (w_ada, m_w_ada, v_w_ada), "b_ada": (b_ada, m_b_ada, v_b_ada),
             "norm_gain": (norm_gain, m_norm_gain, v_norm_gain), "w_ffn_gate": (w_ffn_gate, m_w_ffn_gate, v_w_ffn_gate),
             "w_ffn_up": (w_ffn_up, m_w_ffn_up, v_w_ffn_up), "w_ffn_down": (w_ffn_down, m_w_ffn_down, v_w_ffn_down),
             "w_in": (w_in, m_w_in, v_w_in), "w_br_sb": (w_br_sb, m_w_br_sb, v_w_br_sb),
             "w_br_dil": (w_br_dil, m_w_br_dil, v_w_br_dil), "w_br_swa": (w_br_swa, m_w_br_swa, v_w_br_swa),
             "w_out": (w_out, m_w_out, v_w_out), "sinks": (sinks, m_sinks, v_sinks),
             "rel_bias": (rel_bias, m_rel_bias, v_rel_bias), "final_gain": (final_gain, m_final_gain, v_final_gain)}
    grad, update = {}, {}

    def adamw(n, g):
        w, m, v = state[n]
        grad[n] = g
        if w.ndim == 1:
            update[n] = tuple(t.reshape(w.shape) for t in _adamw(_row(w), _row(g), _row(m), _row(v), name=f"adamw_{n}"))
        else:
            update[n] = _adamw(w, g, m, v, name=f"adamw_{n}")

    for n, g in (("w_ada", g_w_ada), ("b_ada", g_b_ada), ("norm_gain", g_norm_gain), ("sinks", g_sinks),
                 ("rel_bias", g_rel), ("final_gain", g_final)):
        adamw(n, g)

    after = update["w_ada"][0]
    parts = {}
    for key in reversed(order):
        parts[key] = _exchange_wait(exchanges[key], after, gather=False, name=f"exchange_{key[1]}_l{key[0]}_wait")
        after = parts[key][0]
    ffn_keys = [key for key in order if key[1] != "mix"]
    mix_keys = [key for key in order if key[1] == "mix"]
    sums = [_sum_parts([parts[key][t] for key in ffn_keys], name=f"sum_grads_{n}") for t, n in enumerate(("gate", "up", "down"))]
    sums += [_sum_parts([parts[key][t] for key in mix_keys], name=f"sum_grads_{n}") for t, n in enumerate(("in", "br", "out"))]
    br_sums = sums[4].reshape(DEPTH, -1, LANES)
    adamw("w_ffn_gate", sums[0].reshape(w_ffn_gate.shape))
    adamw("w_ffn_up", sums[1].reshape(w_ffn_up.shape))
    adamw("w_ffn_down", sums[2].reshape(w_ffn_down.shape))
    adamw("w_in", sums[3].reshape(w_in.shape))
    adamw("w_br_sb", br_sums[:, br_off[0]:br_off[1]])
    adamw("w_br_dil", br_sums[:, br_off[1]:br_off[2]])
    adamw("w_br_swa", br_sums[:, br_off[2]:br_off[3]])
    adamw("w_out", sums[5].reshape(w_out.shape))

    names = ["w_ada", "b_ada", "norm_gain", "w_ffn_gate", "w_ffn_up", "w_ffn_down", "w_in", "w_br_sb", "w_br_dil",
             "w_br_swa", "w_out", "sinks", "rel_bias", "final_gain"]
    return (loss_total, dx[None], *[grad[n] for n in names], *[update[n][0] for n in names],
            *[update[n][1] for n in names], *[update[n][2] for n in names])
```

```python
import math

import numpy as np
import jax
import jax.numpy as jnp
from jax import lax
from jax.experimental import pallas as pl
from jax.experimental.pallas import tpu as pltpu

F32, BF16 = jnp.float32, jnp.bfloat16

SEQ, D_MODEL, D_FF, HEAD_DIM = 2048, 1024, 2816, 64
DEPTH = 2
BLK = 128
H_SB, H_DIL, H_SWA_Q, H_SWA_KV = 4, 6, 6, 2
DIL_PATTERNS = ((128, 1), (512, 4), (2048, 16))
SWA_WINDOW = 128
N_BUCKETS, MAX_REL_DIST = 32, 2048
RMS_EPS = 1e-6
D_QKV = 2560
D_GATES = 3 * D_MODEL
QKV_SPLITS = (256, 256, 256, 384, 384, 384, 384, 128, 128)
ADAM_LR, ADAM_B1, ADAM_B2, ADAM_EPS, ADAM_WD, ADAM_STEP = 0.001, 0.9, 0.999, 1e-08, 0.01, 10

N_DEV = 8
LANES = 128
NEG = -1e30
SB_TILE = 256
VMEM_LIMIT_BYTES = 48 * 1024 * 1024
HBM = pl.BlockSpec(memory_space=pltpu.HBM)
MESH = pl.DeviceIdType.MESH


def _tile(n, target):
    t = (min(n, target) // LANES) * LANES
    while t >= LANES:
        if n % t == 0:
            return t
        t -= LANES
    return n


def _row_tile(r, cap):
    t = (min(r, cap) // 16) * 16
    while t > 16 and r % t:
        t -= 16
    return t


def _params(semantics=None):
    return pltpu.CompilerParams(dimension_semantics=semantics, vmem_limit_bytes=VMEM_LIMIT_BYTES)


def _dot(a, b, ca, cb):
    return lax.dot_general(a, b, (((ca,), (cb,)), ((), ())), preferred_element_type=F32)


def _sigmoid(a):
    return 1.0 / (1.0 + jnp.exp(-a))


def _row(v):
    return v.reshape(1, -1)


def _all_gather(arrs, name):
    n = len(arrs)

    def body(*refs):
        x_refs, out_refs = refs[:n], refs[n:2 * n]
        send_sems, recv_sems, local_sems = refs[2 * n:]
        x, y, c = lax.axis_index("x"), lax.axis_index("y"), lax.axis_index("c")
        me, sibling = (x, y, c), (x, y, 1 - c)
        chips = [(1 - x, y), (x, 1 - y), (1 - x, 1 - y)]

        def slot(t, px, py, pc):
            return out_refs[t].at[4 * px + 2 * py + pc]

        def copy(t, k, block, to, src=None):
            return pltpu.make_async_remote_copy(
                src_ref=slot(t, *block) if src is None else src, dst_ref=slot(t, *block),
                send_sem=send_sems.at[7 * t + k], recv_sem=recv_sems.at[7 * t + k], device_id=to, device_id_type=MESH)

        mine = [pltpu.make_async_copy(x_refs[t], slot(t, *me), local_sems.at[t]) for t in range(n)]
        for cp in mine:
            cp.start()
        first = []
        for t in range(n):
            first.append(copy(t, 0, me, sibling, src=x_refs[t]))
            first += [copy(t, 1 + j, me, (*chip, c), src=x_refs[t]) for j, chip in enumerate(chips)]
        for cp in first:
            cp.start()
        passed = []
        for j, chip in enumerate(chips):
            for t in range(n):
                copy(t, 1 + j, (*chip, c), me).wait_recv()
                passed.append(copy(t, 4 + j, (*chip, c), sibling))
                passed[-1].start()
        for t in range(n):
            copy(t, 0, sibling, me).wait_recv()
        for j, chip in enumerate(chips):
            for t in range(n):
                copy(t, 4 + j, (*chip, 1 - c), me).wait_recv()
        for cp in first + passed:
            cp.wait_send()
        for cp in mine:
            cp.wait()

    return pl.pallas_call(
        body, name=name, out_shape=[jax.ShapeDtypeStruct((N_DEV,) + a.shape, a.dtype) for a in arrs],
        in_specs=[HBM] * n, out_specs=[HBM] * n,
        scratch_shapes=[pltpu.SemaphoreType.DMA((7 * n,)), pltpu.SemaphoreType.DMA((7 * n,)), pltpu.SemaphoreType.DMA((n,))],
    )(*arrs)


def _direct_copies(x_refs, land_refs, send_sems, recv_sems, local_sems, gather):
    x, y, c = lax.axis_index("x"), lax.axis_index("y"), lax.axis_index("c")
    me = 4 * x + 2 * y + c
    sends, recvs = [], []
    for k in range(1, N_DEV):
        px = 1 - x if (k >> 2) & 1 else x
        py = 1 - y if (k >> 1) & 1 else y
        pc = 1 - c if k & 1 else c
        peer = 4 * px + 2 * py + pc
        for t, (x_ref, land_ref) in enumerate(zip(x_refs, land_refs)):
            sem = 7 * t + k - 1
            for out, src, slot in ((sends, x_ref if gather else x_ref.at[peer], me),
                                   (recvs, x_ref if gather else x_ref.at[me], peer)):
                out.append(pltpu.make_async_remote_copy(
                    src_ref=src, dst_ref=land_ref.at[slot], send_sem=send_sems.at[sem], recv_sem=recv_sems.at[sem],
                    device_id=(px, py, pc), device_id_type=MESH))
    own = [pltpu.make_async_copy(x_ref if gather else x_ref.at[me], land_ref.at[me], local_sems.at[t])
           for t, (x_ref, land_ref) in enumerate(zip(x_refs, land_refs))]
    return sends, recvs, own


SEM =pl.BlockSpec(memory_space=pltpu.SEMAPHORE)
ANY = pl.BlockSpec(memory_space=pl.ANY)
SIDE_EFFECT = pltpu.SideEffectType.DATAFLOW_SIDE_EFFECTING


def _exchange_start(arrs, after, *, gather, name):
    n = len(arrs)
    lands = [lax.empty(((N_DEV,) + a.shape) if gather else a.shape, a.dtype) for a in arrs]

    def body(*refs):
        sends, _, own = _direct_copies(refs[:n], refs[n:2 * n], *refs[2 * n + 1:2 * n + 4], gather)
        for cp in own + sends:
            cp.start()
        refs[-1][...] = jnp.zeros_like(refs[-1])

    ops = [pltpu.with_memory_space_constraint(a, pltpu.HBM) for a in list(arrs) + lands]
    out = pl.pallas_call(
        body, name=name,
        out_shape=(pltpu.SemaphoreType.DMA((7 * n,)), pltpu.SemaphoreType.DMA((7 * n,)), pltpu.SemaphoreType.DMA((n,)),
                   *[pltpu.HBM(a.shape, a.dtype) for a in ops], jax.ShapeDtypeStruct((8, LANES), F32)),
        in_specs=[HBM] * (2 * n) + [ANY],
        out_specs=(SEM, SEM, SEM, *[HBM] * (2 * n), pl.BlockSpec(memory_space=pltpu.VMEM)),
        input_output_aliases={t: 3 + t for t in range(2 * n)},
        compiler_params=pltpu.CompilerParams(has_side_effects=SIDE_EFFECT),
    )(*ops, after)
    return (out[:3], out[3:3 + n], out[3 + n:3 + 2 * n]), out[-1]


def _exchange_wait(state, after, *, gather, name):
    sems, arrs, lands = state
    n = len(arrs)

    def body(*refs):
        sends, recvs, own = _direct_copies(refs[:n], refs[n:2 * n], *refs[2 * n:2 * n + 3], gather)
        for cp in own:
            cp.wait()
        for cp in sends:
            cp.wait_send()
        for cp in recvs:
            cp.wait_recv()

    out = pl.pallas_call(
        body, name=name, out_shape=tuple(pltpu.HBM(a.shape, a.dtype) for a in list(arrs) + list(lands)),
        in_specs=[HBM] * (2 * n) + [SEM, SEM, SEM, ANY], out_specs=tuple([HBM] * (2 * n)),
        input_output_aliases={t: t for t in range(2 * n)},
        compiler_params=pltpu.CompilerParams(has_side_effects=SIDE_EFFECT),
    )(*arrs, *lands, *sems, after)
    return out[n:]


def _sum_parts(groups, name):
    n, r, cdim = groups[0].shape
    tr = _row_tile(r, max(16, (1 << 21) // (n * cdim * groups[0].dtype.itemsize)))
    steps = r // tr

    def body(*refs):
        o_ref = refs[-1]
        gg = pl.program_id(0)
        for gi in range(len(groups)):
            @pl.when(gg == gi)
            def _(gi=gi):
                acc = refs[gi][0].astype(F32)
                for k in range(1, n):
                    acc = acc + refs[gi][k].astype(F32)
                o_ref[...] = acc

    def in_spec(gi):
        return pl.BlockSpec((n, tr, cdim), lambda gg, i: (0, jnp.where(gg == gi, i, 0), 0))

    return pl.pallas_call(
        body, name=name, out_shape=jax.ShapeDtypeStruct((len(groups) * r, cdim), F32), grid=(len(groups), steps),
        in_specs=[in_spec(gi) for gi in range(len(groups))],
        out_specs=pl.BlockSpec((tr, cdim), lambda gg, i: (gg * steps + i, 0)),
        compiler_params=_params(("parallel", "parallel")),
    )(*groups)


def _mm_tn(a, b, *, name, after=None, tm=512, tn=1024):
    k, m = a.shape
    n = b.shape[1]
    tm, tn = _tile(m, tm), _tile(n, tn)

    def body(a_ref, b_ref, *rest):
        o_ref, at_ref = rest[-2], rest[-1]

        @pl.when(pl.program_id(1) == 0)
        def _():
            at_ref[...] = a_ref[...].astype(BF16).T

        o_ref[...] = _dot(at_ref[...], b_ref[...].astype(BF16), 1, 0).astype(BF16)

    ins = [a, b] + ([] if after is None else [after])
    return pl.pallas_call(
        body, name=name, out_shape=jax.ShapeDtypeStruct((m, n), BF16), grid=(m // tm, n // tn),
        in_specs=[pl.BlockSpec((k, tm), lambda i, j: (0, i)), pl.BlockSpec((k, tn), lambda i, j: (0, j))] + [ANY] * (len(ins) - 2),
        out_specs=pl.BlockSpec((tm, tn), lambda i, j: (i, j)),
        scratch_shapes=[pltpu.VMEM((tm, k), BF16)], compiler_params=_params(("parallel", "arbitrary")),
    )(*ins)


def _mm2(a1, b1, a2, b2, *, name, after=None, tm=512, tn=512):
    m = a1.shape[0]
    n = b1.shape[1]
    tm, tn = _tile(m, tm), _tile(n, tn)

    def body(a1_ref, b1_ref, a2_ref, b2_ref, *rest):
        rest[-1][...] = (_dot(a1_ref[...].astype(BF16), b1_ref[...], 1, 0)
                         + _dot(a2_ref[...].astype(BF16), b2_ref[...], 1, 0))

    ins = [a1, b1, a2, b2] + ([] if after is None else [after])

    def a_spec(t):
        return pl.BlockSpec((tm, t.shape[1]), lambda i, j: (i, 0))

    def b_spec(t):
        return pl.BlockSpec((t.shape[0], tn), lambda i, j: (0, j))

    return pl.pallas_call(
        body, name=name, out_shape=jax.ShapeDtypeStruct((m, n), F32), grid=(m // tm, n // tn),
        in_specs=[a_spec(a1), b_spec(b1), a_spec(a2), b_spec(b2)] + [ANY] * (len(ins) - 4),
        out_specs=pl.BlockSpec((tm, tn), lambda i, j: (i, j)), compiler_params=_params(("parallel", "parallel")),
    )(*ins)


def _mm(a, b, *, name, ta=False, tb=False, res=None, colscale=None, emit_acc=False,
        out_dtype=F32, tm=512, tn=512):
    m, k = (a.shape[1], a.shape[0]) if ta else a.shape
    n = b.shape[0] if tb else b.shape[1]
    tm, tn = _tile(m, tm), _tile(n, tn)
    ca, cb = (0 if ta else 1), (1 if tb else 0)
    a_spec = pl.BlockSpec((k, tm), lambda i, j: (0, i)) if ta else pl.BlockSpec((tm, k), lambda i, j: (i, 0))
    b_spec = pl.BlockSpec((tn, k), lambda i, j: (j, 0)) if tb else pl.BlockSpec((k, tn), lambda i, j: (0, j))
    tile = pl.BlockSpec((tm, tn), lambda i, j: (i, j))
    ins, in_specs = [a, b], [a_spec, b_spec]
    if res is not None:
        ins.append(res)
        in_specs.append(tile)
    if colscale is not None:
        ins.append(colscale)
        in_specs.append(pl.BlockSpec((1, tn), lambda i, j: (0, j)))
    n_in = len(ins)

    def body(*refs):
        outs = refs[n_in:]
        acc = _dot(refs[0][...].astype(BF16), refs[1][...].astype(BF16), ca, cb)
        val, p = acc, 2
        if res is not None:
            r_val, p = refs[p][...], p + 1
        if colscale is not None:
            val = val * refs[p][...]
        if res is not None:
            val = r_val + val
        if emit_acc:
            outs[0][...] = acc
        outs[-1][...] = val.astype(out_dtype)

    out_shape = [jax.ShapeDtypeStruct((m, n), out_dtype)]
    out_specs = [tile]
    if emit_acc:
        out_shape.insert(0, jax.ShapeDtypeStruct((m, n), F32))
        out_specs.insert(0, tile)
    out = pl.pallas_call(
        body, name=name, out_shape=out_shape, grid=(m // tm, n // tn), in_specs=in_specs, out_specs=out_specs,
        compiler_params=_params(("parallel", "parallel")),
    )(*ins)
    return out if emit_acc else out[0]


def _norm_fwd(x, g, scale, shift, name):
    s, d = x.shape
    tr = 256

    def body(x_ref, g_ref, sc_ref, sh_ref, h_ref):
        xv = x_ref[...]
        rstd = lax.rsqrt(jnp.mean(xv * xv, axis=-1, keepdims=True) + RMS_EPS)
        h_ref[...] = (xv * rstd * g_ref[...] * (1.0 + sc_ref[...]) + sh_ref[...]).astype(BF16)

    rowspec = pl.BlockSpec((1, d), lambda i: (0, 0))
    return pl.pallas_call(
        body, name=name, out_shape=jax.ShapeDtypeStruct((s, d), BF16), grid=(s // tr,),
        in_specs=[pl.BlockSpec((tr, d), lambda i: (i, 0)), rowspec, rowspec, rowspec],
        out_specs=pl.BlockSpec((tr, d), lambda i: (i, 0)),
        compiler_params=_params(("parallel",)),
    )(x, g, scale, shift)


def _norm_bwd(x, dh, dres, g, scale, name):
    s, d = x.shape
    tr = 256

    def body(x_ref, dh_ref, dr_ref, g_ref, sc_ref, dx_ref, a_ref, b_ref):
        @pl.when(pl.program_id(0) == 0)
        def _():
            a_ref[...] = jnp.zeros_like(a_ref)
            b_ref[...] = jnp.zeros_like(b_ref)

        xv = x_ref[...]
        rstd = lax.rsqrt(jnp.mean(xv * xv, axis=-1, keepdims=True) + RMS_EPS)
        xhat = xv * rstd
        dhv = dh_ref[...]
        dxhat = dhv * (g_ref[...] * (1.0 + sc_ref[...]))
        mean_term = jnp.mean(dxhat * xhat, axis=-1, keepdims=True)
        dx_ref[...] = dr_ref[...] + rstd * (dxhat - xhat * mean_term)
        a_ref[...] += jnp.sum(dhv, axis=0, keepdims=True)
        b_ref[...] += jnp.sum(dhv * xhat, axis=0, keepdims=True)

    rowspec = pl.BlockSpec((1, d), lambda i: (0, 0))
    tile = pl.BlockSpec((tr, d), lambda i: (i, 0))
    return pl.pallas_call(
        body, name=name,
        out_shape=[jax.ShapeDtypeStruct((s, d), F32), jax.ShapeDtypeStruct((1, d), F32), jax.ShapeDtypeStruct((1, d), F32)],
        grid=(s // tr,), in_specs=[tile, tile, tile, rowspec, rowspec], out_specs=[tile, rowspec, rowspec],
        compiler_params=_params(("arbitrary",)),
    )(x, dh, dres, g, scale)


def _gate_bwd(dxn, f, colscale, coef, name):
    s, d = dxn.shape
    tr = 256

    def body(dx_ref, f_ref, cs_ref, df_ref, dg_ref):
        @pl.when(pl.program_id(0) == 0)
        def _():
            dg_ref[...] = jnp.zeros_like(dg_ref)

        dxv = dx_ref[...]
        df_ref[...] = (dxv * cs_ref[...]).astype(BF16)
        dg_ref[...] += coef * jnp.sum(dxv * f_ref[...], axis=0, keepdims=True)

    rowspec = pl.BlockSpec((1, d), lambda i: (0, 0))
    tile = pl.BlockSpec((tr, d), lambda i: (i, 0))
    return pl.pallas_call(
        body, name=name, out_shape=[jax.ShapeDtypeStruct((s, d), BF16), jax.ShapeDtypeStruct((1, d), F32)],
        grid=(s // tr,), in_specs=[tile, tile, rowspec], out_specs=[tile, rowspec],
        compiler_params=_params(("arbitrary",)),
    )(dxn, f, colscale)


def _ffn_up(h, wg, wu, name):
    s, d = h.shape
    f = wg.shape[0]
    tm, tn = s, _tile(f, 256)

    def body(h_ref, wg_ref, wu_ref, a_ref, u_ref, s_ref):
        hv = h_ref[...]
        a = _dot(hv, wg_ref[...], 1, 1)
        u = _dot(hv, wu_ref[...], 1, 1)
        a_ref[...] = a.astype(BF16)
        u_ref[...] = u.astype(BF16)
        s_ref[...] = (a * _sigmoid(a) * u).astype(BF16)

    tile = pl.BlockSpec((tm, tn), lambda i, j: (i, j))
    wspec = pl.BlockSpec((tn, d), lambda i, j: (j, 0))
    return pl.pallas_call(
        body, name=name,
        out_shape=[jax.ShapeDtypeStruct((s, f), BF16), jax.ShapeDtypeStruct((s, f), BF16), jax.ShapeDtypeStruct((s, f), BF16)],
        grid=(s // tm, f // tn), in_specs=[pl.BlockSpec((tm, d), lambda i, j: (i, 0)), wspec, wspec],
        out_specs=[tile, tile, tile], compiler_params=_params(("parallel", "parallel")),
    )(h, wg, wu)


def _ffn_bwd_ds(df, wd, a, u, name):
    s, d = df.shape
    f = wd.shape[0]
    tm, tn = 1024, _tile(f, 256)

    def body(df_ref, wd_ref, a_ref, u_ref, da_ref, du_ref):
        ds = _dot(df_ref[...], wd_ref[...], 1, 1)
        av = a_ref[...].astype(F32)
        sg = _sigmoid(av)
        da_ref[...] = (ds * u_ref[...].astype(F32) * (sg * (1.0 + av * (1.0 - sg)))).astype(BF16)
        du_ref[...] = (ds * (av * sg)).astype(BF16)

    tile = pl.BlockSpec((tm, tn), lambda i, j: (i, j))
    return pl.pallas_call(
        body, name=name, out_shape=[jax.ShapeDtypeStruct((s, f), BF16), jax.ShapeDtypeStruct((s, f), BF16)],
        grid=(s // tm, f // tn),
        in_specs=[pl.BlockSpec((tm, d), lambda i, j: (i, 0)), pl.BlockSpec((tn, d), lambda i, j: (j, 0)), tile, tile],
        out_specs=[tile, tile], compiler_params=_params(("parallel", "parallel")),
    )(df, wd, a, u)


def _merge_fwd(o_sb, o_dil, o_swa, gates, wb_sb, wb_dil, wb_swa, name):
    s = o_sb.shape[0]
    d = D_MODEL
    tm = 256

    def body(osb_ref, odl_ref, osw_ref, g_ref, wsb_ref, wdl_ref, wsw_ref, m_ref):
        acc = _sigmoid(g_ref[:, 0:d]) * _dot(osb_ref[...].astype(BF16), wsb_ref[...], 1, 0)
        acc += _sigmoid(g_ref[:, d:2 * d]) * _dot(odl_ref[...].astype(BF16), wdl_ref[...], 1, 0)
        acc += _sigmoid(g_ref[:, 2 * d:3 * d]) * _dot(osw_ref[...].astype(BF16), wsw_ref[...], 1, 0)
        m_ref[...] = acc.astype(BF16)

    def rows(w):
        return pl.BlockSpec((tm, w), lambda i: (i, 0))

    def whole(w):
        return pl.BlockSpec((w, d), lambda i: (0, 0))

    return pl.pallas_call(
        body, name=name, out_shape=jax.ShapeDtypeStruct((s, d), BF16), grid=(s // tm,),
        in_specs=[rows(256), rows(128), rows(384), rows(3 * d), whole(256), whole(128), whole(384)],
        out_specs=rows(d), compiler_params=_params(("parallel",)),
    )(o_sb, o_dil, o_swa, gates, wb_sb, wb_dil, wb_swa)


def _merge_bwd(dmerged, o_sb, o_dil, o_swa, gates, wb_sb, wb_dil, wb_swa, name):
    s = o_sb.shape[0]
    d = D_MODEL
    tm = 256

    def body(dm_ref, osb_ref, odl_ref, osw_ref, g_ref, wsb_ref, wdl_ref, wsw_ref,
             dg_ref, dosb_ref, dodl_ref, dosw_ref, dbsb_ref, dbdl_ref, dbsw_ref):
        dm = dm_ref[...]
        for idx, (o_ref, w_ref, do_ref, db_ref) in enumerate((
                (osb_ref, wsb_ref, dosb_ref, dbsb_ref), (odl_ref, wdl_ref, dodl_ref, dbdl_ref),
                (osw_ref, wsw_ref, dosw_ref, dbsw_ref))):
            w = w_ref[...]
            br = _dot(o_ref[...].astype(BF16), w, 1, 0)
            sg = _sigmoid(g_ref[:, idx * d:(idx + 1) * d])
            dbr = (dm * sg).astype(BF16)
            dg_ref[:, idx * d:(idx + 1) * d] = dm * br * (sg * (1.0 - sg))
            db_ref[...] = dbr
            do_ref[...] = _dot(dbr, w, 1, 1)

    def rows(w):
        return pl.BlockSpec((tm, w), lambda i: (i, 0))

    def whole(w):
        return pl.BlockSpec((w, d), lambda i: (0, 0))

    def shp(w, dt):
        return jax.ShapeDtypeStruct((s, w), dt)

    return pl.pallas_call(
        body, name=name,
        out_shape=[shp(3 * d, F32), shp(256, F32), shp(128, F32), shp(384, F32), shp(d, BF16), shp(d, BF16), shp(d, BF16)],
        grid=(s // tm,),
        in_specs=[rows(d), rows(256), rows(128), rows(384), rows(3 * d), whole(256), whole(128), whole(384)],
        out_specs=[rows(3 * d), rows(256), rows(128), rows(384), rows(d), rows(d), rows(d)],
        compiler_params=_params(("parallel",)),
    )(dmerged, o_sb, o_dil, o_swa, gates, wb_sb, wb_dil, wb_swa)


def _final_loss(x, target, g, name):
    s, d = x.shape
    tr = 256

    def body(x_ref, t_ref, g_ref, loss_ref, dx_ref, dg_ref):
        @pl.when(pl.program_id(0) == 0)
        def _():
            loss_ref[...] = jnp.zeros_like(loss_ref)
            dg_ref[...] = jnp.zeros_like(dg_ref)

        xv = x_ref[...]
        gv = g_ref[...]
        rstd = lax.rsqrt(jnp.mean(xv * xv, axis=-1, keepdims=True) + RMS_EPS)
        xhat = xv * rstd
        err = xhat * gv - t_ref[...]
        loss_ref[...] += 0.5 * jnp.sum(jnp.mean(err * err, axis=-1, keepdims=True))
        dy = err * (1.0 / d)
        dxhat = dy * gv
        mean_term = jnp.mean(dxhat * xhat, axis=-1, keepdims=True)
        dx_ref[...] = rstd * (dxhat - xhat * mean_term)
        dg_ref[...] += jnp.sum(dy * xhat, axis=0, keepdims=True)

    rowspec = pl.BlockSpec((1, d), lambda i: (0, 0))
    tile = pl.BlockSpec((tr, d), lambda i: (i, 0))
    return pl.pallas_call(
        body, name=name,
        out_shape=[jax.ShapeDtypeStruct((1, LANES), F32), jax.ShapeDtypeStruct((s, d), F32), jax.ShapeDtypeStruct((1, d), F32)],
        grid=(s // tr,), in_specs=[tile, tile, rowspec],
        out_specs=[pl.BlockSpec((1, LANES), lambda i: (0, 0)), tile, rowspec],
        compiler_params=_params(("arbitrary",)),
    )(x, target, g)


def _adamw(w, g, m, v, name):
    shape = w.shape
    cols = shape[-1]
    rows = int(np.prod(shape[:-1])) if len(shape) > 1 else 1
    tr = rows
    for cand in (1024, 512, 256, 128, 64, 32, 16, 8):
        if rows % cand == 0 and rows > cand and cand * cols * 4 <= (1 << 21):
            tr = cand
            break

    def body(w_ref, g_ref, m_ref, v_ref, d_ref, nm_ref, nv_ref):
        gv = g_ref[...]
        nm = ADAM_B1 * m_ref[...] + (1.0 - ADAM_B1) * gv
        nv = ADAM_B2 * v_ref[...] + (1.0 - ADAM_B2) * (gv * gv)
        m_hat = nm / (1.0 - ADAM_B1 ** ADAM_STEP)
        v_hat = nv / (1.0 - ADAM_B2 ** ADAM_STEP)
        d_ref[...] = -ADAM_LR * (m_hat / (jnp.sqrt(v_hat) + ADAM_EPS) + ADAM_WD * w_ref[...])
        nm_ref[...] = nm
        nv_ref[...] = nv

    tile = pl.BlockSpec((tr, cols), lambda i: (i, 0))
    flat = [t.reshape(rows, cols) for t in (w, g, m, v)]
    out = pl.pallas_call(
        body, name=name, out_shape=[jax.ShapeDtypeStruct((rows, cols), F32)] * 3, grid=(rows // tr,),
        in_specs=[tile] * 4, out_specs=[tile] * 3, compiler_params=_params(("parallel",)),
    )(*flat)
    return tuple(t.reshape(shape) for t in out)


def _ada_fwd(c_all, w, name):
    n = w.shape[1]

    def body(c_ref, w_ref, o_ref):
        cv = c_ref[...]
        o_ref[...] = jnp.dot(cv * _sigmoid(cv), w_ref[...], preferred_element_type=F32, precision=lax.Precision.HIGHEST)

    return pl.pallas_call(body, name=name, out_shape=jax.ShapeDtypeStruct((N_DEV, n), F32), compiler_params=_params())(c_all, w)


def _ada_bwd(c_all_t, dmod, name):
    n = dmod.shape[1]

    def body(c_ref, d_ref, o_ref):
        cv = c_ref[...]
        o_ref[...] = jnp.dot(cv * _sigmoid(cv), d_ref[...], preferred_element_type=F32, precision=lax.Precision.HIGHEST)

    return pl.pallas_call(body, name=name, out_shape=jax.ShapeDtypeStruct((D_MODEL, n), F32), compiler_params=_params())(c_all_t, dmod)


def _bucket_tables():
    rel = np.arange(BLK)[:, None] + BLK - np.arange(2 * BLK)[None, :]
    max_exact = N_BUCKETS // 2

    def bucket(n):
        nf = np.maximum(n, 1).astype(np.float32)
        large = max_exact + (np.log(nf / np.float32(max_exact)) / np.float32(math.log(MAX_REL_DIST / max_exact))
                             * np.float32(N_BUCKETS - max_exact)).astype(np.int32)
        return np.where(n < max_exact, n, np.minimum(large, N_BUCKETS - 1))

    tabs = []
    for dil, max_dist in ((1, 128), (4, 128), (16, 128), (1, SWA_WINDOW - 1)):
        in_band = (rel >= 0) & (rel <= max_dist)
        tabs.append(np.where(in_band, bucket(np.maximum(rel, 0) * dil), -1))
    return np.stack(tabs).astype(np.int32)


N_SOFT = H_DIL + H_SWA_Q


def _table_of_head(h):
    return jnp.minimum(h // 2, 3)


def _bias_build(rel_bias, tables, name):
    def body(rel_ref, t_ref, o_ref):
        h = pl.program_id(0)
        tb = t_ref[0]
        out = jnp.full((BLK, 2 * BLK), NEG, F32)
        for b in range(N_BUCKETS):
            out = jnp.where(tb == b, rel_ref[b, h], out)
        o_ref[0] = out

    return pl.pallas_call(
        body, name=name, out_shape=jax.ShapeDtypeStruct((N_SOFT, BLK, 2 * BLK), F32), grid=(N_SOFT,),
        in_specs=[pl.BlockSpec(memory_space=pltpu.SMEM),
                  pl.BlockSpec((1, BLK, 2 * BLK), lambda h: (_table_of_head(h), 0, 0))],
        out_specs=pl.BlockSpec((1, BLK, 2 * BLK), lambda h: (h, 0, 0)),
        compiler_params=_params(("parallel",)),
    )(rel_bias, tables)


def _bias_grad(dbias, tables, name):
    def body(d_ref, t_ref, o_ref):
        tb = t_ref[0]
        dv = d_ref[0]
        lane = lax.broadcasted_iota(jnp.int32, (1, LANES), 1)
        out = jnp.zeros((1, LANES), F32)
        for b in range(N_BUCKETS):
            out = jnp.where(lane == b, jnp.sum(jnp.where(tb == b, dv, 0.0)), out)
        o_ref[0] = out

    return pl.pallas_call(
        body, name=name, out_shape=jax.ShapeDtypeStruct((N_SOFT, 1, LANES), F32), grid=(N_SOFT,),
        in_specs=[pl.BlockSpec((1, BLK, 2 * BLK), lambda h: (h, 0, 0)),
                  pl.BlockSpec((1, BLK, 2 * BLK), lambda h: (_table_of_head(h), 0, 0))],
        out_specs=pl.BlockSpec((1, 1, LANES), lambda h: (h, 0, 0)),
        compiler_params=_params(("parallel",)),
    )(dbias, tables)


def _band_specs(g, bias_div):
    qspec = pl.BlockSpec((1, BLK, HEAD_DIM), lambda n, i: (n, i, 0))
    prev = pl.BlockSpec((1, BLK, HEAD_DIM), lambda n, i: (n // g, jnp.maximum(i - 1, 0), 0))
    cur = pl.BlockSpec((1, BLK, HEAD_DIM), lambda n, i: (n // g, i, 0))
    bspec = pl.BlockSpec((1, BLK, 2 * BLK), lambda n, i: (n // bias_div, 0, 0))
    sspec = pl.BlockSpec((1, 1, LANES), lambda n, i: (n, 0, 0))
    colspec = pl.BlockSpec((1, BLK, 1), lambda n, i: (n, i, 0))
    return qspec, prev, cur, bspec, sspec, colspec


def _band_scores(q_ref, kp_ref, kc_ref, b_ref, first):
    qv = q_ref[0]
    bv = b_ref[0]
    sp = _dot(qv, kp_ref[0], 1, 1) + bv[:, :BLK]
    sp = jnp.where(first, NEG, sp)
    sc = _dot(qv, kc_ref[0], 1, 1) + bv[:, BLK:]
    return sp, sc


def _band_fwd(q, k, v, bias, sink, *, g, bias_div, has_sink, name):
    nq, length, _ = q.shape

    def body(q_ref, kp_ref, kc_ref, vp_ref, vc_ref, b_ref, s_ref, o_ref, lse_ref):
        sp, sc = _band_scores(q_ref, kp_ref, kc_ref, b_ref, pl.program_id(1) == 0)
        m = jnp.maximum(jnp.max(sp, axis=1, keepdims=True), jnp.max(sc, axis=1, keepdims=True))
        if has_sink:
            sk = s_ref[0][:, :1]
            m = jnp.maximum(m, sk)
        pp, pc = jnp.exp(sp - m), jnp.exp(sc - m)
        den = jnp.sum(pp, axis=1, keepdims=True) + jnp.sum(pc, axis=1, keepdims=True)
        if has_sink:
            den = den + jnp.exp(sk - m)
        acc = _dot(pp.astype(BF16), vp_ref[0], 1, 0) + _dot(pc.astype(BF16), vc_ref[0], 1, 0)
        o_ref[0] = acc / den
        lse_ref[0] = m + jnp.log(den)

    qspec, prev, cur, bspec, sspec, colspec = _band_specs(g, bias_div)
    return pl.pallas_call(
        body, name=name,
        out_shape=[jax.ShapeDtypeStruct((nq, length, HEAD_DIM), F32), jax.ShapeDtypeStruct((nq, length, 1), F32)],
        grid=(nq, length // BLK), in_specs=[qspec, prev, cur, prev, cur, bspec, sspec],
        out_specs=[qspec, colspec], compiler_params=_params(("parallel", "parallel")),
    )(q, k, k, v, v, bias, sink)


def _band_bwd(q, k, v, bias, sink, o, lse, do, dlse, *, g, bias_div, has_sink, name):
    nq, length, _ = q.shape
    nk, nbias = nq // g, nq // bias_div

    def body(q_ref, kp_ref, kc_ref, vp_ref, vc_ref, b_ref, s_ref, o_ref, lse_ref, do_ref, dlse_ref,
             dq_ref, dk_ref, dv_ref, db_ref, dsk_ref):
        n, i = pl.program_id(0), pl.program_id(1)

        @pl.when((n % g == 0) & (i == 0))
        def _():
            dk_ref[...] = jnp.zeros_like(dk_ref)
            dv_ref[...] = jnp.zeros_like(dv_ref)

        @pl.when((n % bias_div == 0) & (i == 0))
        def _():
            db_ref[...] = jnp.zeros_like(db_ref)

        @pl.when(i == 0)
        def _():
            dsk_ref[...] = jnp.zeros_like(dsk_ref)

        sp, sc = _band_scores(q_ref, kp_ref, kc_ref, b_ref, i == 0)
        lse_v = lse_ref[0]
        pp, pc = jnp.exp(sp - lse_v), jnp.exp(sc - lse_v)
        dov = do_ref[0]
        dob = dov.astype(BF16)
        coef = dlse_ref[0] - jnp.sum(dov * o_ref[0], axis=1, keepdims=True)
        dsp = pp * (_dot(dob, vp_ref[0], 1, 1) + coef)
        dsc = pc * (_dot(dob, vc_ref[0], 1, 1) + coef)
        dspb, dscb = dsp.astype(BF16), dsc.astype(BF16)
        dq_ref[0] = (_dot(dspb, kp_ref[0], 1, 0) + _dot(dscb, kc_ref[0], 1, 0)) * (HEAD_DIM ** -0.5)
        qv = q_ref[0]
        cur = pl.ds(pl.multiple_of(i * BLK, BLK), BLK)
        prv = pl.ds(pl.multiple_of(jnp.maximum(i - 1, 0) * BLK, BLK), BLK)
        dk_ref[0, cur, :] += _dot(dscb, qv, 0, 0)
        dk_ref[0, prv, :] += _dot(dspb, qv, 0, 0)
        dv_ref[0, cur, :] += _dot(pc.astype(BF16), dob, 0, 0)
        dv_ref[0, prv, :] += _dot(pp.astype(BF16), dob, 0, 0)
        db_ref[0, :, :BLK] += dsp
        db_ref[0, :, BLK:] += dsc
        if has_sink:
            dsk_ref[0] += jnp.sum(jnp.exp(s_ref[0][:, :1] - lse_v) * coef)

    qspec, prev, cur, bspec, sspec, colspec = _band_specs(g, bias_div)
    kvfull = pl.BlockSpec((1, length, HEAD_DIM), lambda n, i: (n // g, 0, 0))
    return pl.pallas_call(
        body, name=name,
        out_shape=[jax.ShapeDtypeStruct((nq, length, HEAD_DIM), F32), jax.ShapeDtypeStruct((nk, length, HEAD_DIM), F32),
                   jax.ShapeDtypeStruct((nk, length, HEAD_DIM), F32), jax.ShapeDtypeStruct((nbias, BLK, 2 * BLK), F32),
                   jax.ShapeDtypeStruct((nq, 1, LANES), F32)],
        grid=(nq, length // BLK),
        in_specs=[qspec, prev, cur, prev, cur, bspec, sspec, qspec, colspec, qspec, colspec],
        out_specs=[qspec, kvfull, kvfull, bspec, sspec], compiler_params=_params(("arbitrary", "arbitrary")),
    )(q, k, k, v, v, bias, sink, o, lse, do, dlse)


def _dil_merge(os_, lses, dout, name):
    tr = 512
    n = len(os_)
    tile = pl.BlockSpec((1, tr, HEAD_DIM), lambda h, i: (h, i, 0))
    col = pl.BlockSpec((1, tr, 1), lambda h, i: (h, i, 0))

    def weights(l_refs):
        ls = [r[0] for r in l_refs]
        m = ls[0]
        for lv in ls[1:]:
            m = jnp.maximum(m, lv)
        es = [jnp.exp(lv - m) for lv in ls]
        den = es[0]
        for e in es[1:]:
            den = den + e
        return [e / den for e in es]

    if dout is None:
        def body(*refs):
            alphas = weights(refs[n:2 * n])
            acc = alphas[0] * refs[0][0]
            for gi in range(1, n):
                acc = acc + alphas[gi] * refs[gi][0]
            refs[2 * n][0] = acc

        return pl.pallas_call(
            body, name=name, out_shape=jax.ShapeDtypeStruct(os_[0].shape, F32), grid=(2, SEQ // tr),
            in_specs=[tile] * n + [col] * n, out_specs=tile, compiler_params=_params(("parallel", "parallel")),
        )(*os_, *lses)

    def body(*refs):
        alphas = weights(refs[n:2 * n])
        dov = refs[2 * n][0]
        outs = refs[2 * n + 1:]
        das = [jnp.sum(dov * refs[gi][0], axis=1, keepdims=True) for gi in range(n)]
        dbar = alphas[0] * das[0]
        for gi in range(1, n):
            dbar = dbar + alphas[gi] * das[gi]
        for gi in range(n):
            outs[gi][0] = alphas[gi] * dov
            outs[n + gi][0] = alphas[gi] * (das[gi] - dbar)

    return pl.pallas_call(
        body, name=name,
        out_shape=[jax.ShapeDtypeStruct(os_[0].shape, F32)] * n + [jax.ShapeDtypeStruct(lses[0].shape, F32)] * n,
        grid=(2, SEQ // tr), in_specs=[tile] * n + [col] * n + [tile], out_specs=[tile] * n + [col] * n,
        compiler_params=_params(("parallel", "parallel")),
    )(*os_, *lses, dout)


def _tri(cmp):
    r = lax.broadcasted_iota(jnp.int32, (SB_TILE, SB_TILE), 0)
    c = lax.broadcasted_iota(jnp.int32, (SB_TILE, SB_TILE), 1)
    return cmp(r, c).astype(BF16)


def _cum(x, tri, terms):
    acc, rest = None, x
    for _ in range(terms):
        part = rest.astype(BF16)
        rest = rest - part.astype(F32)
        d = _dot(part, tri, 1, 0)
        acc = d if acc is None else acc + d
    return acc


def _sb_logits(q, k_ref, j, i):
    t = SB_TILE
    ks = k_ref[0, pl.ds(pl.multiple_of(j * t, t), t), :]
    z = _dot(q, ks, 1, 1)
    rows = i * t + lax.broadcasted_iota(jnp.int32, (t, t), 0)
    cols = j * t + lax.broadcasted_iota(jnp.int32, (t, t), 1)
    mask = cols < rows
    e = jnp.exp(-jnp.abs(z))
    lf = jnp.where(mask, -(jnp.maximum(z, 0.0) + jnp.log(1.0 + e)), 0.0)
    return ks, z, e, lf, mask


def _sb_fwd(q, k, v, name):
    h, s, _ = q.shape
    t = SB_TILE

    def body(q_ref, k_ref, v_ref, o_ref, tot_ref):
        i = pl.program_id(1)
        qv = q_ref[0]
        after = _tri(lambda r, c: r > c)

        def step(jj, carry):
            right, acc = carry
            j = i - jj
            _, z, _, lf, mask = _sb_logits(qv, k_ref, j, i)
            between = right + _cum(lf, after, 3)
            w = jnp.where(mask, jnp.exp(z + lf + between), 0.0)
            vs = v_ref[0, pl.ds(pl.multiple_of(j * t, t), t), :]
            return right + jnp.sum(lf, axis=1, keepdims=True), acc + _dot(w.astype(BF16), vs, 1, 0)

        right, acc = lax.fori_loop(0, i + 1, step, (jnp.zeros((t, 1), F32), jnp.zeros((t, HEAD_DIM), F32)))
        o_ref[0] = acc
        tot_ref[0] = right

    tile = pl.BlockSpec((1, t, HEAD_DIM), lambda hh, i: (hh, i, 0))
    full = pl.BlockSpec((1, s, HEAD_DIM), lambda hh, i: (hh, 0, 0))
    return pl.pallas_call(
        body, name=name, out_shape=[jax.ShapeDtypeStruct((h, s, HEAD_DIM), F32), jax.ShapeDtypeStruct((h, s, 1), F32)],
        grid=(h, s // t), in_specs=[tile, full, full],
        out_specs=[tile, pl.BlockSpec((1, t, 1), lambda hh, i: (hh, i, 0))],
        compiler_params=_params(("parallel", "parallel")),
    )(q, k, v)


def _sb_bwd(q, k, v, tot, do, name):
    h, s, _ = q.shape
    t = SB_TILE

    def body(q_ref, k_ref, v_ref, tot_ref, do_ref, dq_ref, dk_ref, dv_ref):
        i = pl.program_id(1)

        @pl.when(i == 0)
        def _():
            dk_ref[...] = jnp.zeros_like(dk_ref)
            dv_ref[...] = jnp.zeros_like(dv_ref)

        qv = q_ref[0]
        dob = do_ref[0].astype(BF16)
        total = tot_ref[0]
        upto = _tri(lambda r, c: r <= c)
        before = _tri(lambda r, c: r < c)

        def step(j, carry):
            left, cleft, dq = carry
            ks, z, e, lf, mask = _sb_logits(qv, k_ref, j, i)
            rows = pl.ds(pl.multiple_of(j * t, t), t)
            vs = v_ref[0, rows, :]
            between = total - (left + _cum(lf, upto, 3))
            w = jnp.where(mask, jnp.exp(z + lf + between), 0.0)
            dlog = w * _dot(dob, vs, 1, 1)
            cfail = cleft + _cum(dlog, before, 2)
            sig = jnp.where(z >= 0.0, 1.0, e) / (1.0 + e)
            dz = jnp.where(mask, dlog * (1.0 - sig) - sig * cfail, 0.0).astype(BF16)
            dk_ref[0, rows, :] += _dot(dz, qv, 0, 0)
            dv_ref[0, rows, :] += _dot(w.astype(BF16), dob, 0, 0)
            return (left + jnp.sum(lf, axis=1, keepdims=True), cleft + jnp.sum(dlog, axis=1, keepdims=True),
                    dq + _dot(dz, ks, 1, 0))

        zero = jnp.zeros((t, 1), F32)
        _, _, dq = lax.fori_loop(0, i + 1, step, (zero, zero, jnp.zeros((t, HEAD_DIM), F32)))
        dq_ref[0] = dq * (HEAD_DIM ** -0.5)

    tile = pl.BlockSpec((1, t, HEAD_DIM), lambda hh, i: (hh, i, 0))
    full = pl.BlockSpec((1, s, HEAD_DIM), lambda hh, i: (hh, 0, 0))
    shp = jax.ShapeDtypeStruct((h, s, HEAD_DIM), F32)
    return pl.pallas_call(
        body, name=name, out_shape=[shp, shp, shp], grid=(h, s // t),
        in_specs=[tile, full, full, pl.BlockSpec((1, t, 1), lambda hh, i: (hh, i, 0)), tile],
        out_specs=[tile, full, full], compiler_params=_params(("arbitrary", "arbitrary")),
    )(q, k, v, tot, do)


def _heads(t):
    return t.reshape(SEQ, -1, HEAD_DIM).transpose(1, 0, 2)


def _unheads(t):
    return t.transpose(1, 0, 2).reshape(SEQ, -1)


def _to_dil(t, d):
    xdim = t.shape[-1]
    return t.reshape(2, SEQ // d, d, xdim).transpose(0, 2, 1, 3).reshape(2 * d, SEQ // d, xdim)


def _from_dil(t, d):
    xdim = t.shape[-1]
    return t.reshape(2, d, SEQ // d, xdim).transpose(0, 2, 1, 3).reshape(2, SEQ, xdim)


def _split_qkv(qkv):
    parts, off = [], 0
    for w in QKV_SPLITS:
        parts.append(qkv[:, off:off + w])
        off += w
    return parts


def _mixer_fwd(qkv, bias, sinks_l, tag):
    scale = HEAD_DIM ** -0.5
    q_sb, k_sb, v_sb, q_dl, k_dl, v_dl, q_sw, k_sw, v_sw = _split_qkv(qkv)
    hq = lambda t: _heads((t * scale).astype(BF16))
    hk = lambda t: _heads(t.astype(BF16))
    st = {}
    st["sb"] = (hq(q_sb), hk(k_sb), hk(v_sb))
    o_sb, st["sb_tot"] = _sb_fwd(*st["sb"], name=f"sb_fwd_{tag}")

    qd, kd, vd = hq(q_dl), hk(k_dl), hk(v_dl)
    no_sink = jnp.zeros((1, 1, LANES), F32)
    st["dil"], outs, lses = [], [], []
    for gi, (_, d) in enumerate(DIL_PATTERNS):
        hs = slice(2 * gi, 2 * gi + 2)
        qg, kg, vg = _to_dil(qd[hs], d), _to_dil(kd[hs], d), _to_dil(vd[hs], d)
        sink = jnp.broadcast_to(no_sink, (2 * d, 1, LANES))
        og, lg = _band_fwd(qg, kg, vg, bias[hs], sink, g=1, bias_div=d, has_sink=False, name=f"dil{gi}_fwd_{tag}")
        st["dil"].append((qg, kg, vg, sink, og, lg))
        outs.append(_from_dil(og, d))
        lses.append(_from_dil(lg, d))
    st["dil_outs"], st["dil_lses"] = outs, lses
    o_dil = _dil_merge(outs, lses, None, name=f"dil_merge_fwd_{tag}")

    sink = jnp.broadcast_to(sinks_l.reshape(H_SWA_Q, 1, 1), (H_SWA_Q, 1, LANES))
    st["swa"] = (hq(q_sw), hk(k_sw), hk(v_sw), sink)
    o_sw, l_sw = _band_fwd(*st["swa"][:3], bias[H_DIL:], sink, g=H_SWA_Q // H_SWA_KV, bias_div=1, has_sink=True,
                           name=f"swa_fwd_{tag}")
    st["swa_out"] = (o_sw, l_sw)
    return (_unheads(o_sb), _unheads(o_dil), _unheads(o_sw)), st


def _mixer_bwd(st, bias, do_sb, do_dil, do_swa, tag):
    dq_sb, dk_sb, dv_sb = _sb_bwd(*st["sb"], st["sb_tot"], _heads(do_sb), name=f"sb_bwd_{tag}")

    dmerge = _dil_merge(st["dil_outs"], st["dil_lses"], _heads(do_dil), name=f"dil_merge_bwd_{tag}")
    dqs, dks, dvs, dbs = [], [], [], []
    for gi, (_, d) in enumerate(DIL_PATTERNS):
        qg, kg, vg, sink, og, lg = st["dil"][gi]
        hs = slice(2 * gi, 2 * gi + 2)
        dq, dk, dv, db, _ = _band_bwd(qg, kg, vg, bias[hs], sink, og, lg, _to_dil(dmerge[gi], d), _to_dil(dmerge[3 + gi], d),
                                      g=1, bias_div=d, has_sink=False, name=f"dil{gi}_bwd_{tag}")
        dqs.append(_from_dil(dq, d))
        dks.append(_from_dil(dk, d))
        dvs.append(_from_dil(dv, d))
        dbs.append(db)

    q_sw, k_sw, v_sw, sink = st["swa"]
    o_sw, l_sw = st["swa_out"]
    dq_sw, dk_sw, dv_sw, db_sw, dsink = _band_bwd(q_sw, k_sw, v_sw, bias[H_DIL:], sink, o_sw, l_sw, _heads(do_swa),
                                                  jnp.zeros_like(l_sw), g=H_SWA_Q // H_SWA_KV, bias_div=1, has_sink=True,
                                                  name=f"swa_bwd_{tag}")
    dqkv = jnp.concatenate(
        [_unheads(dq_sb), _unheads(dk_sb), _unheads(dv_sb),
         _unheads(jnp.concatenate(dqs, 0)), _unheads(jnp.concatenate(dks, 0)), _unheads(jnp.concatenate(dvs, 0)),
         _unheads(dq_sw), _unheads(dk_sw), _unheads(dv_sw)], axis=1)
    return dqkv, jnp.concatenate(dbs + [db_sw], 0), dsink[:, 0, 0]


PIECES = ("ffn0", "mix", "ffn1")


def _ffn_fwd(x_in, w, gain, mod_j, tag):
    st = {"x": x_in, "w": w}
    st["h"] = _norm_fwd(x_in, _row(gain), _row(mod_j[1]), _row(mod_j[0]), name=f"norm_fwd_{tag}")
    st["a"], st["u"], st["s"] = _ffn_up(st["h"], w["gate"], w["up"], name=f"up_{tag}")
    st["f"], x_out = _mm(st["s"], w["down"], res=x_in, colscale=_row(0.5 * mod_j[2]), emit_acc=True, name=f"down_{tag}")
    return x_out, st


def _ffn_bwd(dx_out, st, gain, mod_j, tag, done):
    w = st["w"]

    def latest(new, old):
        return old if new is None else new

    df, dgate = _gate_bwd(dx_out, st["f"], _row(0.5 * mod_j[2]), 0.5, name=f"gate_bwd_{tag}")
    token = done({"down": _mm_tn(st["s"], df, name=f"dwd_{tag}")})
    da, du = _ffn_bwd_ds(df, w["down"], st["a"], st["u"], name=f"ds_{tag}")
    token = latest(done({"gate": _mm_tn(da, st["h"], after=token, name=f"dwg_{tag}")}), token)
    token = latest(done({"up": _mm_tn(du, st["h"], after=token, name=f"dwu_{tag}")}), token)
    dh = _mm2(da, w["gate"], du, w["up"], after=token, name=f"dh_{tag}")
    dx_in, sum_dh, sum_dhx = _norm_bwd(st["x"], dh, dx_out, _row(gain), _row(mod_j[1]), name=f"norm_bwd_{tag}")
    dmod = jnp.concatenate([sum_dh, gain * sum_dhx, dgate], 0)
    return dx_in, dmod, (1.0 + mod_j[1]) * sum_dhx[0]


def _mix_fwd(x_in, w, gain, mod_j, bias, sinks_l, tag):
    st = {"x": x_in, "w": w}
    st["h"] = _norm_fwd(x_in, _row(gain), _row(mod_j[1]), _row(mod_j[0]), name=f"norm_fwd_mix_{tag}")
    qkv = _mm(st["h"], w["qkv"], tb=True, name=f"qkv_{tag}")
    st["gates"] = _mm(st["h"], w["gates"], tb=True, name=f"gates_{tag}")
    st["o"], st["mix"] = _mixer_fwd(qkv, bias, sinks_l, tag)
    st["merged"] = _merge_fwd(*st["o"], st["gates"], w["br_sb"], w["br_dil"], w["br_swa"], name=f"merge_fwd_{tag}")
    st["f"], x_out = _mm(st["merged"], w["out"], res=x_in, colscale=_row(mod_j[2]), emit_acc=True, name=f"out_{tag}")
    return x_out, st


def _mix_bwd(dx_out, st, gain, mod_j, bias, tag, done):
    w = st["w"]
    df, dgate = _gate_bwd(dx_out, st["f"], _row(mod_j[2]), 1.0, name=f"gate_bwd_mix_{tag}")
    g = {"out": _mm_tn(st["merged"], df, name=f"dw_out_{tag}")}
    dmerged = _mm(df, w["out"], tb=True, name=f"dmerged_{tag}")
    dgates, do_sb, do_dil, do_swa, dbr_sb, dbr_dil, dbr_swa = _merge_bwd(
        dmerged, *st["o"], st["gates"], w["br_sb"], w["br_dil"], w["br_swa"], name=f"merge_bwd_{tag}")
    g["br_sb"] = _mm_tn(st["o"][0], dbr_sb, name=f"dw_br_sb_{tag}")
    g["br_dil"] = _mm_tn(st["o"][1], dbr_dil, name=f"dw_br_dil_{tag}")
    g["br_swa"] = _mm_tn(st["o"][2], dbr_swa, name=f"dw_br_swa_{tag}")
    dqkv, dbias, dsinks = _mixer_bwd(st["mix"], bias, do_sb, do_dil, do_swa, tag)
    g["qkv"] = _mm_tn(dqkv, st["h"], name=f"dw_qkv_{tag}")
    g["gates"] = _mm_tn(dgates, st["h"], name=f"dw_gates_{tag}")
    dh = _mm2(dqkv, w["qkv"], dgates, w["gates"], after=done(g), tm=256, name=f"dh_mix_{tag}")
    dx_in, sum_dh, sum_dhx = _norm_bwd(st["x"], dh, dx_out, _row(gain), _row(mod_j[1]), name=f"norm_bwd_mix_{tag}")
    dmod = jnp.concatenate([sum_dh, gain * sum_dhx, dgate], 0)
    return dx_in, dmod, (1.0 + mod_j[1]) * sum_dhx[0], dbias, dsinks


def _local_step(x, target, mod, gains, weights_of, rel_bias, sinks, final_gain, grads_done):
    tables = jnp.asarray(_bucket_tables())
    bias = _bias_build(rel_bias, tables, name="bias_build")
    states, h = [], x
    for l in range(DEPTH):
        st = {}
        for j, piece in enumerate(PIECES):
            w = weights_of(l, piece, h)
            if piece == "mix":
                h, st[piece] = _mix_fwd(h, w, gains[l, j], mod[l, j], bias, sinks[l], f"l{l}")
            else:
                h, st[piece] = _ffn_fwd(h, w, gains[l, j], mod[l, j], f"{piece}_l{l}")
        states.append(st)
    loss, dx, dfinal = _final_loss(h, target, _row(final_gain), name="final_loss")
    dmods = [[None] * 3 for _ in range(DEPTH)]
    dgains = [[None] * 3 for _ in range(DEPTH)]
    dsinks = [None] * DEPTH
    dbias = None
    for l in reversed(range(DEPTH)):
        for j in reversed(range(3)):
            piece = PIECES[j]
            done = lambda grads, l=l, piece=piece: grads_done(l, piece, grads)
            if piece == "mix":
                dx, dmods[l][j], dgains[l][j], db, dsinks[l] = _mix_bwd(dx, states[l][piece], gains[l, j], mod[l, j], bias, f"l{l}", done)
                dbias = db if dbias is None else dbias + db
            else:
                dx, dmods[l][j], dgains[l][j] = _ffn_bwd(dx, states[l][piece], gains[l, j], mod[l, j], f"{piece}_l{l}", done)
    drel = _bias_grad(dbias, tables, name="bias_grad")[:, 0, :N_BUCKETS].T
    dmod = jnp.stack([jnp.stack(m) for m in dmods])
    dgain = jnp.stack([jnp.stack(g) for g in dgains])
    return loss, dx, dmod, dgain, dfinal[0], drel, jnp.stack(dsinks)


BR_ROWS = (H_SB * HEAD_DIM, 2 * HEAD_DIM, H_SWA_Q * HEAD_DIM)


def _lanes_unshard(g, lead):
    _, rows, _ = g.shape
    r = rows // lead
    return g.reshape(N_DEV, lead, r, LANES).transpose(1, 2, 0, 3).reshape(lead, r, N_DEV * LANES)


def _lanes_shard(full):
    lead, r, _ = full.shape
    return full.reshape(lead, r, N_DEV, LANES).transpose(2, 0, 1, 3).reshape(N_DEV, lead * r, LANES)


def _pack_rows(parts, dtype):
    flat = jnp.concatenate([p.astype(dtype).reshape(-1) for p in parts])
    pad = (-flat.shape[0]) % (16 * LANES)
    if pad:
        flat = jnp.concatenate([flat, jnp.zeros((pad,), dtype)])
    return flat.reshape(-1, LANES)


def _unshard(gathered, axis):
    moved = jnp.moveaxis(gathered, 0, axis)
    shape = list(moved.shape)
    shape[axis:axis + 2] = [shape[axis] * shape[axis + 1]]
    return moved.reshape(shape)


def kernel(x, c, w_ada, b_ada, norm_gain, w_ffn_gate, w_ffn_up, w_ffn_down, w_in, w_br_sb, w_br_dil, w_br_swa, w_out, sinks, rel_bias, final_gain, loss_target, m_w_ada, m_b_ada, m_norm_gain, m_w_ffn_gate, m_w_ffn_up, m_w_ffn_down, m_w_in, m_w_br_sb, m_w_br_dil, m_w_br_swa, m_w_out, m_sinks, m_rel_bias, m_final_gain, v_w_ada, v_b_ada, v_norm_gain, v_w_ffn_gate, v_w_ffn_up, v_w_ffn_down, v_w_in, v_w_br_sb, v_w_br_dil, v_w_br_swa, v_w_out, v_sinks, v_rel_bias, v_final_gain):
    me = 4 * lax.axis_index("x") + 2 * lax.axis_index("y") + lax.axis_index("c")
    d = D_MODEL
    gate_t, up_t, in_t = jnp.swapaxes(w_ffn_gate, 2, 3), jnp.swapaxes(w_ffn_up, 2, 3), jnp.swapaxes(w_in, 1, 2)

    def piece_shards(l, piece):
        bf = lambda t: t.astype(BF16)
        if piece == "mix":
            return [bf(in_t[l]), jnp.concatenate([bf(w_br_sb[l]), bf(w_br_dil[l]), bf(w_br_swa[l])], 0), bf(w_out[l])]
        i = PIECES.index(piece) // 2
        return [bf(gate_t[l, i]), bf(up_t[l, i]), bf(w_ffn_down[l, i])]

    br_off = np.concatenate([[0], np.cumsum(BR_ROWS)])

    def piece_weights(gathered, piece):
        if piece == "mix":
            g_in, g_br, g_out = gathered
            f_in = g_in.reshape(D_QKV + D_GATES, d)
            f_br = [_lanes_unshard(g_br[:, br_off[k]:br_off[k + 1]], 1)[0] for k in range(3)]
            return {"qkv": f_in[:D_QKV], "gates": f_in[D_QKV:], "br_sb": f_br[0], "br_dil": f_br[1], "br_swa": f_br[2],
                    "out": g_out.reshape(d, d)}
        return {n: g.reshape(D_FF, d) for n, g in zip(("gate", "up", "down"), gathered)}

    order = [(l, piece) for l in range(DEPTH) for piece in PIECES]
    ahead = 3
    in_flight = {}

    def start_gather(k, after):
        l, piece = order[k]
        in_flight[k], token = _exchange_start(piece_shards(l, piece), after, gather=True, name=f"gather_{piece}_l{l}_start")
        return token

    token = c
    for k in range(ahead):
        token = start_gather(k, token)

    def weights_of(l, piece, h):
        k = order.index((l, piece))
        after = start_gather(k + ahead, h) if k + ahead < len(order) else h
        return piece_weights(_exchange_wait(in_flight[k], after, gather=True, name=f"gather_{piece}_l{l}_wait"), piece)

    small, = _all_gather([_pack_rows([c, norm_gain], F32) + token[0, 0]], name="gather_cond")
    c_all = small[:, :d // LANES].reshape(N_DEV, d)
    gains = _unshard(small[:, d // LANES:d // LANES + 6].reshape(N_DEV, DEPTH, 3, LANES), 2)

    cols = w_ada.shape[2]
    mod_cols = jnp.stack([_ada_fwd(c_all, w_ada[l], name=f"ada_fwd_l{l}") for l in range(DEPTH)])
    mod_all, = _all_gather([_pack_rows([mod_cols], F32)], name="gather_mod")
    mod_all = mod_all.reshape(N_DEV, -1)[:, :DEPTH * N_DEV * cols].reshape(N_DEV, DEPTH, N_DEV, cols)
    mod_mine = lax.dynamic_index_in_dim(mod_all, me, axis=2, keepdims=False)
    mod = (mod_mine.transpose(1, 0, 2).reshape(DEPTH, N_DEV * cols) + b_ada).reshape(DEPTH, 3, 3, d)

    exchanges, have = {}, {}

    def grads_done(l, piece, g):
        key = (l, piece)
        have.setdefault(key, {}).update(g)
        if piece == "mix":
            if len(have[key]) < 6:
                return None
            g = have[key]
            s_br = jnp.concatenate([_lanes_shard(g[n][None]) for n in ("br_sb", "br_dil", "br_swa")], 1)
            groups = [(("in", "br", "out"), [jnp.concatenate([g["qkv"], g["gates"]], 0).reshape(N_DEV, -1, d), s_br,
                                             g["out"].reshape(N_DEV, -1, d)])]
        elif key == order[0]:
            groups = [((n,), [t.reshape(N_DEV, -1, d)]) for n, t in g.items()]
        elif len(have[key]) < 3:
            return None
        else:
            groups = [(("gate", "up", "down"), [have[key][n].reshape(N_DEV, -1, d) for n in ("gate", "up", "down")])]
        token = None
        for names, sg in groups:
            state, token = _exchange_start(sg, sg[0], gather=False, name=f"exchange_{piece}_l{l}_{names[0]}_start")
            exchanges.setdefault(key, []).append((names, state))
        return token

    loss, dx, dmod, dgains, dfinal, drel, dsinks = _local_step(
        x[0], loss_target[0], mod, gains, weights_of, rel_bias, sinks, final_gain, grads_done)

    small_parts = [dmod, dgains, dfinal, drel.T, dsinks, loss[0, :1]]
    small_sizes = [int(np.prod(p.shape)) for p in small_parts]
    small_all, = _all_gather([_pack_rows(small_parts, F32)], name="gather_small")
    small_sum = _sum_parts([small_all], name="sum_small").reshape(-1)
    offs = np.concatenate([[0], np.cumsum(small_sizes)])
    g_b_ada = small_sum[offs[0]:offs[1]].reshape(DEPTH, 9 * d)
    g_gain_full = small_sum[offs[1]:offs[2]].reshape(DEPTH, 3, d)
    g_norm_gain = lax.dynamic_slice_in_dim(g_gain_full, me * LANES, LANES, axis=2)
    g_final = small_sum[offs[2]:offs[3]]
    g_rel = small_sum[offs[3]:offs[4]].reshape(N_SOFT, N_BUCKETS).T
    g_sinks = small_sum[offs[4]:offs[5]].reshape(DEPTH, H_SWA_Q)
    loss_total = small_sum[offs[5]]

    dmod_all = small_all.reshape(N_DEV, -1)[:, :DEPTH * 9 * d].reshape(N_DEV, DEPTH, 9 * d)
    dmod_cols = lax.dynamic_slice_in_dim(dmod_all, me * cols, cols, axis=2)
    g_w_ada = jnp.stack([_ada_bwd(c_all.T, dmod_cols[:, l], name=f"ada_bwd_l{l}") for l in range(DEPTH)])

    state = {"w_ada": (w_ada, m_w_ada, v_w_ada), "b_ada": (b_ada, m_b_ada, v_b_ada),
             "norm_gain": (norm_gain, m_norm_gain, v_norm_gain), "w_ffn_gate": (w_ffn_gate, m_w_ffn_gate, v_w_ffn_gate),
             "w_ffn_up": (w_ffn_up, m_w_ffn_up, v_w_ffn_up), "w_ffn_down": (w_ffn_down, m_w_ffn_down, v_w_ffn_down),
             "w_in": (w_in, m_w_in, v_w_in), "w_br_sb": (w_br_sb, m_w_br_sb, v_w_br_sb),
             "w_br_dil": (w_br_dil, m_w_br_dil, v_w_br_dil), "w_br_swa": (w_br_swa, m_w_br_swa, v_w_br_swa),
             "w_out": (w_out, m_w_out, v_w_out), "sinks": (sinks, m_sinks, v_sinks),
             "rel_bias": (rel_bias, m_rel_bias, v_rel_bias), "final_gain": (final_gain, m_final_gain, v_final_gain)}
    grad, update = {}, {}

    def adamw(n, g, transposed=False):
        w, m, v = (jnp.swapaxes(t, -1, -2) for t in state[n]) if transposed else state[n]
        if w.ndim == 1:
            out = tuple(t.reshape(w.shape) for t in _adamw(_row(w), _row(g), _row(m), _row(v), name=f"adamw_{n}"))
        else:
            out = _adamw(w, g, m, v, name=f"adamw_{n}")
        if transposed:
            grad[n], update[n] = jnp.swapaxes(g, -1, -2), tuple(jnp.swapaxes(t, -1, -2) for t in out)
        else:
            grad[n], update[n] = g, out

    for n, g in (("w_ada", g_w_ada), ("b_ada", g_b_ada), ("norm_gain", g_norm_gain), ("sinks", g_sinks),
                 ("rel_bias", g_rel), ("final_gain", g_final)):
        adamw(n, g)

    after = update["w_ada"][0]
    parts = {}
    for key in reversed(order):
        for names, ex_state in exchanges[key]:
            landed = _exchange_wait(ex_state, after, gather=False, name=f"exchange_{key[1]}_l{key[0]}_{names[0]}_wait")
            parts.setdefault(key, {}).update(zip(names, landed))
            after = landed[0]
    ffn_keys = [key for key in order if key[1] != "mix"]
    mix_keys = [key for key in order if key[1] == "mix"]
    sums = {n: _sum_parts([parts[key][n] for key in ffn_keys], name=f"sum_grads_{n}") for n in ("gate", "up", "down")}
    sums.update({n: _sum_parts([parts[key][n] for key in mix_keys], name=f"sum_grads_{n}") for n in ("in", "br", "out")})
    br_sums = sums["br"].reshape(DEPTH, -1, LANES)
    adamw("w_ffn_gate", sums["gate"].reshape(gate_t.shape), transposed=True)
    adamw("w_ffn_up", sums["up"].reshape(up_t.shape), transposed=True)
    adamw("w_ffn_down", sums["down"].reshape(w_ffn_down.shape))
    adamw("w_in", sums["in"].reshape(in_t.shape), transposed=True)
    adamw("w_br_sb", br_sums[:, br_off[0]:br_off[1]])
    adamw("w_br_dil", br_sums[:, br_off[1]:br_off[2]])
    adamw("w_br_swa", br_sums[:, br_off[2]:br_off[3]])
    adamw("w_out", sums["out"].reshape(w_out.shape))

    names = ["w_ada", "b_ada", "norm_gain", "w_ffn_gate", "w_ffn_up", "w_ffn_down", "w_in", "w_br_sb", "w_br_dil",
             "w_br_swa", "w_out", "sinks", "rel_bias", "final_gain"]
    return (loss_total, dx[None], *[grad[n] for n in names], *[update[n][0] for n in names],
            *[update[n][1] for n in names], *[update[n][2] for n in names])
```

```python
import math

import numpy as np
import jax
import jax.numpy as jnp
from jax import lax
from jax.experimental import pallas as pl
from jax.experimental.pallas import tpu as pltpu

F32, BF16 = jnp.float32, jnp.bfloat16

SEQ, D_MODEL, D_FF, HEAD_DIM = 2048, 1024, 2816, 64
DEPTH = 2
BLK = 128
H_SB, H_DIL, H_SWA_Q, H_SWA_KV = 4, 6, 6, 2
DIL_PATTERNS = ((128, 1), (512, 4), (2048, 16))
SWA_WINDOW = 128
N_BUCKETS, MAX_REL_DIST = 32, 2048
RMS_EPS = 1e-6
D_QKV = 2560
D_GATES = 3 * D_MODEL
QKV_SPLITS = (256, 256, 256, 384, 384, 384, 384, 128, 128)
ADAM_LR, ADAM_B1, ADAM_B2, ADAM_EPS, ADAM_WD, ADAM_STEP = 0.001, 0.9, 0.999, 1e-08, 0.01, 10

N_DEV = 8
LANES = 128
NEG = -1e30
SB_TILE = 256
VMEM_LIMIT_BYTES = 48 * 1024 * 1024
HBM = pl.BlockSpec(memory_space=pltpu.HBM)
MESH = pl.DeviceIdType.MESH


def _tile(n, target):
    t = (min(n, target) // LANES) * LANES
    while t >= LANES:
        if n % t == 0:
            return t
        t -= LANES
    return n


def _row_tile(r, cap):
    t = (min(r, cap) // 16) * 16
    while t > 16 and r % t:
        t -= 16
    return t


def _params(semantics=None):
    return pltpu.CompilerParams(dimension_semantics=semantics, vmem_limit_bytes=VMEM_LIMIT_BYTES)


def _dot(a, b, ca, cb):
    return lax.dot_general(a, b, (((ca,), (cb,)), ((), ())), preferred_element_type=F32)


def _sigmoid(a):
    return 1.0 / (1.0 + jnp.exp(-a))


def _row(v):
    return v.reshape(1, -1)


def _all_gather(arrs, name, after=None):
    n = len(arrs)
    ins = list(arrs) + ([] if after is None else [after])

    def body(*refs):
        x_refs, out_refs = refs[:n], refs[len(ins):len(ins) + n]
        send_sems, recv_sems, local_sems = refs[len(ins) + n:]
        x, y, c = lax.axis_index("x"), lax.axis_index("y"), lax.axis_index("c")
        me, sibling = (x, y, c), (x, y, 1 - c)
        chips = [(1 - x, y), (x, 1 - y), (1 - x, 1 - y)]

        def slot(t, px, py, pc):
            return out_refs[t].at[4 * px + 2 * py + pc]

        def copy(t, k, block, to, src=None):
            return pltpu.make_async_remote_copy(
                src_ref=slot(t, *block) if src is None else src, dst_ref=slot(t, *block),
                send_sem=send_sems.at[7 * t + k], recv_sem=recv_sems.at[7 * t + k], device_id=to, device_id_type=MESH)

        mine = [pltpu.make_async_copy(x_refs[t], slot(t, *me), local_sems.at[t]) for t in range(n)]
        for cp in mine:
            cp.start()
        first = []
        for t in range(n):
            first.append(copy(t, 0, me, sibling, src=x_refs[t]))
            first += [copy(t, 1 + j, me, (*chip, c), src=x_refs[t]) for j, chip in enumerate(chips)]
        for cp in first:
            cp.start()
        passed = []
        for j, chip in enumerate(chips):
            for t in range(n):
                copy(t, 1 + j, (*chip, c), me).wait_recv()
                passed.append(copy(t, 4 + j, (*chip, c), sibling))
                passed[-1].start()
        for t in range(n):
            copy(t, 0, sibling, me).wait_recv()
        for j, chip in enumerate(chips):
            for t in range(n):
                copy(t, 4 + j, (*chip, 1 - c), me).wait_recv()
        for cp in first + passed:
            cp.wait_send()
        for cp in mine:
            cp.wait()

    return pl.pallas_call(
        body, name=name, out_shape=[jax.ShapeDtypeStruct((N_DEV,) + a.shape, a.dtype) for a in arrs],
        in_specs=[HBM] * n + [pl.BlockSpec(memory_space=pl.ANY)] * (len(ins) - n), out_specs=[HBM] * n,
        scratch_shapes=[pltpu.SemaphoreType.DMA((7 * n,)), pltpu.SemaphoreType.DMA((7 * n,)), pltpu.SemaphoreType.DMA((n,))],
    )(*ins)


def _direct_copies(x_refs, land_refs, send_sems, recv_sems, local_sems, gather):
    x, y, c = lax.axis_index("x"), lax.axis_index("y"), lax.axis_index("c")
    me = 4 * x + 2 * y + c
    sends, recvs = [], []
    for k in range(1, N_DEV):
        px = 1 - x if (k >> 2) & 1 else x
        py = 1 - y if (k >> 1) & 1 else y
        pc = 1 - c if k & 1 else c
        peer = 4 * px + 2 * py + pc
        for t, (x_ref, land_ref) in enumerate(zip(x_refs, land_refs)):
            sem = 7 * t + k - 1
            for out, src, slot in ((sends, x_ref if gather else x_ref.at[peer], me),
                                   (recvs, x_ref if gather else x_ref.at[me], peer)):
                out.append(pltpu.make_async_remote_copy(
                    src_ref=src, dst_ref=land_ref.at[slot], send_sem=send_sems.at[sem], recv_sem=recv_sems.at[sem],
                    device_id=(px, py, pc), device_id_type=MESH))
    own = [pltpu.make_async_copy(x_ref if gather else x_ref.at[me], land_ref.at[me], local_sems.at[t])
           for t, (x_ref, land_ref) in enumerate(zip(x_refs, land_refs))]
    return sends, recvs, own


SEM =pl.BlockSpec(memory_space=pltpu.SEMAPHORE)
ANY = pl.BlockSpec(memory_space=pl.ANY)
SIDE_EFFECT = pltpu.SideEffectType.DATAFLOW_SIDE_EFFECTING


def _exchange_start(arrs, after, *, gather, name):
    n = len(arrs)
    lands = [lax.empty(((N_DEV,) + a.shape) if gather else a.shape, a.dtype) for a in arrs]

    def body(*refs):
        sends, _, own = _direct_copies(refs[:n], refs[n:2 * n], *refs[2 * n + 1:2 * n + 4], gather)
        for cp in own + sends:
            cp.start()
        refs[-1][...] = jnp.zeros_like(refs[-1])

    ops = [pltpu.with_memory_space_constraint(a, pltpu.HBM) for a in list(arrs) + lands]
    out = pl.pallas_call(
        body, name=name,
        out_shape=(pltpu.SemaphoreType.DMA((7 * n,)), pltpu.SemaphoreType.DMA((7 * n,)), pltpu.SemaphoreType.DMA((n,)),
                   *[pltpu.HBM(a.shape, a.dtype) for a in ops], jax.ShapeDtypeStruct((8, LANES), F32)),
        in_specs=[HBM] * (2 * n) + [ANY],
        out_specs=(SEM, SEM, SEM, *[HBM] * (2 * n), pl.BlockSpec(memory_space=pltpu.VMEM)),
        input_output_aliases={t: 3 + t for t in range(2 * n)},
        compiler_params=pltpu.CompilerParams(has_side_effects=SIDE_EFFECT),
    )(*ops, after)
    return (out[:3], out[3:3 + n], out[3 + n:3 + 2 * n]), out[-1]


def _exchange_wait(state, after, *, gather, name):
    sems, arrs, lands = state
    n = len(arrs)

    def body(*refs):
        sends, recvs, own = _direct_copies(refs[:n], refs[n:2 * n], *refs[2 * n:2 * n + 3], gather)
        for cp in own:
            cp.wait()
        for cp in sends:
            cp.wait_send()
        for cp in recvs:
            cp.wait_recv()

    out = pl.pallas_call(
        body, name=name, out_shape=tuple(pltpu.HBM(a.shape, a.dtype) for a in list(arrs) + list(lands)),
        in_specs=[HBM] * (2 * n) + [SEM, SEM, SEM, ANY], out_specs=tuple([HBM] * (2 * n)),
        input_output_aliases={t: t for t in range(2 * n)},
        compiler_params=pltpu.CompilerParams(has_side_effects=SIDE_EFFECT),
    )(*arrs, *lands, *sems, after)
    return out[n:]


def _sum_parts(groups, name):
    n, r, cdim = groups[0].shape
    tr = _row_tile(r, max(16, (1 << 21) // (n * cdim * groups[0].dtype.itemsize)))
    steps = r // tr

    def body(*refs):
        o_ref = refs[-1]
        gg = pl.program_id(0)
        for gi in range(len(groups)):
            @pl.when(gg == gi)
            def _(gi=gi):
                acc = refs[gi][0].astype(F32)
                for k in range(1, n):
                    acc = acc + refs[gi][k].astype(F32)
                o_ref[...] = acc

    def in_spec(gi):
        return pl.BlockSpec((n, tr, cdim), lambda gg, i: (0, jnp.where(gg == gi, i, 0), 0))

    return pl.pallas_call(
        body, name=name, out_shape=jax.ShapeDtypeStruct((len(groups) * r, cdim), F32), grid=(len(groups), steps),
        in_specs=[in_spec(gi) for gi in range(len(groups))],
        out_specs=pl.BlockSpec((tr, cdim), lambda gg, i: (gg * steps + i, 0)),
        compiler_params=_params(("parallel", "parallel")),
    )(*groups)


def _mm_tn(a, b, *, name, after=None, tm=512, tn=1024):
    k, m = a.shape
    n = b.shape[1]
    tm, tn = _tile(m, tm), _tile(n, tn)

    def body(a_ref, b_ref, *rest):
        o_ref, at_ref = rest[-2], rest[-1]

        @pl.when(pl.program_id(1) == 0)
        def _():
            at_ref[...] = a_ref[...].astype(BF16).T

        o_ref[...] = _dot(at_ref[...], b_ref[...].astype(BF16), 1, 0).astype(BF16)

    ins = [a, b] + ([] if after is None else [after])
    return pl.pallas_call(
        body, name=name, out_shape=jax.ShapeDtypeStruct((m, n), BF16), grid=(m // tm, n // tn),
        in_specs=[pl.BlockSpec((k, tm), lambda i, j: (0, i)), pl.BlockSpec((k, tn), lambda i, j: (0, j))] + [ANY] * (len(ins) - 2),
        out_specs=pl.BlockSpec((tm, tn), lambda i, j: (i, j)),
        scratch_shapes=[pltpu.VMEM((tm, k), BF16)], compiler_params=_params(("parallel", "arbitrary")),
    )(*ins)


def _mm2(a1, b1, a2, b2, *, name, after=None, tm=256, tn=1024):
    m = a1.shape[0]
    n = b1.shape[1]
    tm, tn = _tile(m, tm), _tile(n, tn)

    def body(a1_ref, b1_ref, a2_ref, b2_ref, *rest):
        rest[-1][...] = (_dot(a1_ref[...].astype(BF16), b1_ref[...], 1, 0)
                         + _dot(a2_ref[...].astype(BF16), b2_ref[...], 1, 0))

    ins = [a1, b1, a2, b2] + ([] if after is None else [after])

    def a_spec(t):
        return pl.BlockSpec((tm, t.shape[1]), lambda i, j: (i, 0))

    def b_spec(t):
        return pl.BlockSpec((t.shape[0], tn), lambda i, j: (0, j))

    return pl.pallas_call(
        body, name=name, out_shape=jax.ShapeDtypeStruct((m, n), F32), grid=(m // tm, n // tn),
        in_specs=[a_spec(a1), b_spec(b1), a_spec(a2), b_spec(b2)] + [ANY] * (len(ins) - 4),
        out_specs=pl.BlockSpec((tm, tn), lambda i, j: (i, j)), compiler_params=_params(("parallel", "parallel")),
    )(*ins)


def _mm(a, b, *, name, ta=False, tb=False, res=None, colscale=None, emit_acc=False,
        out_dtype=F32, tm=512, tn=512):
    m, k = (a.shape[1], a.shape[0]) if ta else a.shape
    n = b.shape[0] if tb else b.shape[1]
    tm, tn = _tile(m, tm), _tile(n, tn)
    ca, cb = (0 if ta else 1), (1 if tb else 0)
    a_spec = pl.BlockSpec((k, tm), lambda i, j: (0, i)) if ta else pl.BlockSpec((tm, k), lambda i, j: (i, 0))
    b_spec = pl.BlockSpec((tn, k), lambda i, j: (j, 0)) if tb else pl.BlockSpec((k, tn), lambda i, j: (0, j))
    tile = pl.BlockSpec((tm, tn), lambda i, j: (i, j))
    ins, in_specs = [a, b], [a_spec, b_spec]
    if res is not None:
        ins.append(res)
        in_specs.append(tile)
    if colscale is not None:
        ins.append(colscale)
        in_specs.append(pl.BlockSpec((1, tn), lambda i, j: (0, j)))
    n_in = len(ins)

    def body(*refs):
        outs = refs[n_in:]
        acc = _dot(refs[0][...].astype(BF16), refs[1][...].astype(BF16), ca, cb)
        val, p = acc, 2
        if res is not None:
            r_val, p = refs[p][...], p + 1
        if colscale is not None:
            val = val * refs[p][...]
        if res is not None:
            val = r_val + val
        if emit_acc:
            outs[0][...] = acc
        outs[-1][...] = val.astype(out_dtype)

    out_shape = [jax.ShapeDtypeStruct((m, n), out_dtype)]
    out_specs = [tile]
    if emit_acc:
        out_shape.insert(0, jax.ShapeDtypeStruct((m, n), F32))
        out_specs.insert(0, tile)
    out = pl.pallas_call(
        body, name=name, out_shape=out_shape, grid=(m // tm, n // tn), in_specs=in_specs, out_specs=out_specs,
        compiler_params=_params(("parallel", "parallel")),
    )(*ins)
    return out if emit_acc else out[0]


def _norm_fwd(x, g, scale, shift, name):
    s, d = x.shape
    tr = 256

    def body(x_ref, g_ref, sc_ref, sh_ref, h_ref):
        xv = x_ref[...]
        rstd = lax.rsqrt(jnp.mean(xv * xv, axis=-1, keepdims=True) + RMS_EPS)
        h_ref[...] = (xv * rstd * g_ref[...] * (1.0 + sc_ref[...]) + sh_ref[...]).astype(BF16)

    rowspec = pl.BlockSpec((1, d), lambda i: (0, 0))
    return pl.pallas_call(
        body, name=name, out_shape=jax.ShapeDtypeStruct((s, d), BF16), grid=(s // tr,),
        in_specs=[pl.BlockSpec((tr, d), lambda i: (i, 0)), rowspec, rowspec, rowspec],
        out_specs=pl.BlockSpec((tr, d), lambda i: (i, 0)),
        compiler_params=_params(("parallel",)),
    )(x, g, scale, shift)


def _norm_bwd(x, dh, dres, g, scale, name):
    s, d = x.shape
    tr = 256

    def body(x_ref, dh_ref, dr_ref, g_ref, sc_ref, dx_ref, a_ref, b_ref):
        @pl.when(pl.program_id(0) == 0)
        def _():
            a_ref[...] = jnp.zeros_like(a_ref)
            b_ref[...] = jnp.zeros_like(b_ref)

        xv = x_ref[...]
        rstd = lax.rsqrt(jnp.mean(xv * xv, axis=-1, keepdims=True) + RMS_EPS)
        xhat = xv * rstd
        dhv = dh_ref[...]
        dxhat = dhv * (g_ref[...] * (1.0 + sc_ref[...]))
        mean_term = jnp.mean(dxhat * xhat, axis=-1, keepdims=True)
        dx_ref[...] = dr_ref[...] + rstd * (dxhat - xhat * mean_term)
        a_ref[...] += jnp.sum(dhv, axis=0, keepdims=True)
        b_ref[...] += jnp.sum(dhv * xhat, axis=0, keepdims=True)

    rowspec = pl.BlockSpec((1, d), lambda i: (0, 0))
    tile = pl.BlockSpec((tr, d), lambda i: (i, 0))
    return pl.pallas_call(
        body, name=name,
        out_shape=[jax.ShapeDtypeStruct((s, d), F32), jax.ShapeDtypeStruct((1, d), F32), jax.ShapeDtypeStruct((1, d), F32)],
        grid=(s // tr,), in_specs=[tile, tile, tile, rowspec, rowspec], out_specs=[tile, rowspec, rowspec],
        compiler_params=_params(("arbitrary",)),
    )(x, dh, dres, g, scale)


def _gate_bwd(dxn, f, colscale, coef, name):
    s, d = dxn.shape
    tr = 256

    def body(dx_ref, f_ref, cs_ref, df_ref, dg_ref):
        @pl.when(pl.program_id(0) == 0)
        def _():
            dg_ref[...] = jnp.zeros_like(dg_ref)

        dxv = dx_ref[...]
        df_ref[...] = (dxv * cs_ref[...]).astype(BF16)
        dg_ref[...] += coef * jnp.sum(dxv * f_ref[...], axis=0, keepdims=True)

    rowspec = pl.BlockSpec((1, d), lambda i: (0, 0))
    tile = pl.BlockSpec((tr, d), lambda i: (i, 0))
    return pl.pallas_call(
        body, name=name, out_shape=[jax.ShapeDtypeStruct((s, d), BF16), jax.ShapeDtypeStruct((1, d), F32)],
        grid=(s // tr,), in_specs=[tile, tile, rowspec], out_specs=[tile, rowspec],
        compiler_params=_params(("arbitrary",)),
    )(dxn, f, colscale)


def _ffn_up(h, wg, wu, name):
    s, d = h.shape
    f = wg.shape[0]
    tm, tn = s, _tile(f, 256)

    def body(h_ref, wg_ref, wu_ref, a_ref, u_ref, s_ref):
        hv = h_ref[...]
        a = _dot(hv, wg_ref[...], 1, 1)
        u = _dot(hv, wu_ref[...], 1, 1)
        a_ref[...] = a.astype(BF16)
        u_ref[...] = u.astype(BF16)
        s_ref[...] = (a * _sigmoid(a) * u).astype(BF16)

    tile = pl.BlockSpec((tm, tn), lambda i, j: (i, j))
    wspec = pl.BlockSpec((tn, d), lambda i, j: (j, 0))
    return pl.pallas_call(
        body, name=name,
        out_shape=[jax.ShapeDtypeStruct((s, f), BF16), jax.ShapeDtypeStruct((s, f), BF16), jax.ShapeDtypeStruct((s, f), BF16)],
        grid=(s // tm, f // tn), in_specs=[pl.BlockSpec((tm, d), lambda i, j: (i, 0)), wspec, wspec],
        out_specs=[tile, tile, tile], compiler_params=_params(("parallel", "parallel")),
    )(h, wg, wu)


def _ffn_bwd_ds(df, wd, a, u, name):
    s, d = df.shape
    f = wd.shape[0]
    tm = 128

    def body(df_ref, wd_ref, a_ref, u_ref, da_ref, du_ref):
        ds = _dot(df_ref[...], wd_ref[...], 1, 1)
        av = a_ref[...].astype(F32)
        sg = _sigmoid(av)
        da_ref[...] = (ds * u_ref[...].astype(F32) * (sg * (1.0 + av * (1.0 - sg)))).astype(BF16)
        du_ref[...] = (ds * (av * sg)).astype(BF16)

    rows = pl.BlockSpec((tm, f), lambda i: (i, 0))
    return pl.pallas_call(
        body, name=name, out_shape=[jax.ShapeDtypeStruct((s, f), BF16), jax.ShapeDtypeStruct((s, f), BF16)],
        grid=(s // tm,),
        in_specs=[pl.BlockSpec((tm, d), lambda i: (i, 0)), pl.BlockSpec((f, d), lambda i: (0, 0)), rows, rows],
        out_specs=[rows, rows], compiler_params=_params(("parallel",)),
    )(df, wd, a, u)


def _merge_fwd(o_sb, o_dil, o_swa, gates, wb_sb, wb_dil, wb_swa, name):
    s = o_sb.shape[0]
    d = D_MODEL
    tm = 256

    def body(osb_ref, odl_ref, osw_ref, g_ref, wsb_ref, wdl_ref, wsw_ref, m_ref):
        acc = _sigmoid(g_ref[:, 0:d]) * _dot(osb_ref[...].astype(BF16), wsb_ref[...], 1, 0)
        acc += _sigmoid(g_ref[:, d:2 * d]) * _dot(odl_ref[...].astype(BF16), wdl_ref[...], 1, 0)
        acc += _sigmoid(g_ref[:, 2 * d:3 * d]) * _dot(osw_ref[...].astype(BF16), wsw_ref[...], 1, 0)
        m_ref[...] = acc.astype(BF16)

    def rows(w):
        return pl.BlockSpec((tm, w), lambda i: (i, 0))

    def whole(w):
        return pl.BlockSpec((w, d), lambda i: (0, 0))

    return pl.pallas_call(
        body, name=name, out_shape=jax.ShapeDtypeStruct((s, d), BF16), grid=(s // tm,),
        in_specs=[rows(256), rows(128), rows(384), rows(3 * d), whole(256), whole(128), whole(384)],
        out_specs=rows(d), compiler_params=_params(("parallel",)),
    )(o_sb, o_dil, o_swa, gates, wb_sb, wb_dil, wb_swa)


def _merge_bwd(dmerged, o_sb, o_dil, o_swa, gates, wb_sb, wb_dil, wb_swa, name):
    s = o_sb.shape[0]
    d = D_MODEL
    tm = 256

    def body(dm_ref, osb_ref, odl_ref, osw_ref, g_ref, wsb_ref, wdl_ref, wsw_ref,
             dg_ref, dosb_ref, dodl_ref, dosw_ref, dbsb_ref, dbdl_ref, dbsw_ref):
        dm = dm_ref[...]
        for idx, (o_ref, w_ref, do_ref, db_ref) in enumerate((
                (osb_ref, wsb_ref, dosb_ref, dbsb_ref), (odl_ref, wdl_ref, dodl_ref, dbdl_ref),
                (osw_ref, wsw_ref, dosw_ref, dbsw_ref))):
            w = w_ref[...]
            br = _dot(o_ref[...].astype(BF16), w, 1, 0)
            sg = _sigmoid(g_ref[:, idx * d:(idx + 1) * d])
            dbr = (dm * sg).astype(BF16)
            dg_ref[:, idx * d:(idx + 1) * d] = dm * br * (sg * (1.0 - sg))
            db_ref[...] = dbr
            do_ref[...] = _dot(dbr, w, 1, 1)

    def rows(w):
        return pl.BlockSpec((tm, w), lambda i: (i, 0))

    def whole(w):
        return pl.BlockSpec((w, d), lambda i: (0, 0))

    def shp(w, dt):
        return jax.ShapeDtypeStruct((s, w), dt)

    return pl.pallas_call(
        body, name=name,
        out_shape=[shp(3 * d, F32), shp(256, F32), shp(128, F32), shp(384, F32), shp(d, BF16), shp(d, BF16), shp(d, BF16)],
        grid=(s // tm,),
        in_specs=[rows(d), rows(256), rows(128), rows(384), rows(3 * d), whole(256), whole(128), whole(384)],
        out_specs=[rows(3 * d), rows(256), rows(128), rows(384), rows(d), rows(d), rows(d)],
        compiler_params=_params(("parallel",)),
    )(dmerged, o_sb, o_dil, o_swa, gates, wb_sb, wb_dil, wb_swa)


def _final_loss(x, target, g, name):
    s, d = x.shape
    tr = 256

    def body(x_ref, t_ref, g_ref, loss_ref, dx_ref, dg_ref):
        @pl.when(pl.program_id(0) == 0)
        def _():
            loss_ref[...] = jnp.zeros_like(loss_ref)
            dg_ref[...] = jnp.zeros_like(dg_ref)

        xv = x_ref[...]
        gv = g_ref[...]
        rstd = lax.rsqrt(jnp.mean(xv * xv, axis=-1, keepdims=True) + RMS_EPS)
        xhat = xv * rstd
        err = xhat * gv - t_ref[...]
        loss_ref[...] += 0.5 * jnp.sum(jnp.mean(err * err, axis=-1, keepdims=True))
        dy = err * (1.0 / d)
        dxhat = dy * gv
        mean_term = jnp.mean(dxhat * xhat, axis=-1, keepdims=True)
        dx_ref[...] = rstd * (dxhat - xhat * mean_term)
        dg_ref[...] += jnp.sum(dy * xhat, axis=0, keepdims=True)

    rowspec = pl.BlockSpec((1, d), lambda i: (0, 0))
    tile = pl.BlockSpec((tr, d), lambda i: (i, 0))
    return pl.pallas_call(
        body, name=name,
        out_shape=[jax.ShapeDtypeStruct((1, LANES), F32), jax.ShapeDtypeStruct((s, d), F32), jax.ShapeDtypeStruct((1, d), F32)],
        grid=(s // tr,), in_specs=[tile, tile, rowspec],
        out_specs=[pl.BlockSpec((1, LANES), lambda i: (0, 0)), tile, rowspec],
        compiler_params=_params(("arbitrary",)),
    )(x, target, g)


def _adamw(w, g, m, v, name):
    shape = w.shape
    cols = shape[-1]
    rows = int(np.prod(shape[:-1])) if len(shape) > 1 else 1
    tr = rows
    for cand in (1024, 512, 256, 128, 64, 32, 16, 8):
        if rows % cand == 0 and rows > cand and cand * cols * 4 <= (1 << 21):
            tr = cand
            break

    def body(w_ref, g_ref, m_ref, v_ref, d_ref, nm_ref, nv_ref):
        gv = g_ref[...]
        nm = ADAM_B1 * m_ref[...] + (1.0 - ADAM_B1) * gv
        nv = ADAM_B2 * v_ref[...] + (1.0 - ADAM_B2) * (gv * gv)
        m_hat = nm / (1.0 - ADAM_B1 ** ADAM_STEP)
        v_hat = nv / (1.0 - ADAM_B2 ** ADAM_STEP)
        d_ref[...] = -ADAM_LR * (m_hat / (jnp.sqrt(v_hat) + ADAM_EPS) + ADAM_WD * w_ref[...])
        nm_ref[...] = nm
        nv_ref[...] = nv

    tile = pl.BlockSpec((tr, cols), lambda i: (i, 0))
    flat = [t.reshape(rows, cols) for t in (w, g, m, v)]
    out = pl.pallas_call(
        body, name=name, out_shape=[jax.ShapeDtypeStruct((rows, cols), F32)] * 3, grid=(rows // tr,),
        in_specs=[tile] * 4, out_specs=[tile] * 3, compiler_params=_params(("parallel",)),
    )(*flat)
    return tuple(t.reshape(shape) for t in out)


def _ada_fwd(c_all, w, name):
    n = w.shape[1]

    def body(c_ref, w_ref, o_ref):
        cv = c_ref[...]
        o_ref[...] = jnp.dot(cv * _sigmoid(cv), w_ref[...], preferred_element_type=F32, precision=lax.Precision.HIGHEST)

    return pl.pallas_call(body, name=name, out_shape=jax.ShapeDtypeStruct((N_DEV, n), F32), compiler_params=_params())(c_all, w)


def _ada_bwd(c_all_t, dmod, name):
    n = dmod.shape[1]

    def body(c_ref, d_ref, o_ref):
        cv = c_ref[...]
        o_ref[...] = jnp.dot(cv * _sigmoid(cv), d_ref[...], preferred_element_type=F32, precision=lax.Precision.HIGHEST)

    return pl.pallas_call(body, name=name, out_shape=jax.ShapeDtypeStruct((D_MODEL, n), F32), compiler_params=_params())(c_all_t, dmod)


def _bucket_tables():
    rel = np.arange(BLK)[:, None] + BLK - np.arange(2 * BLK)[None, :]
    max_exact = N_BUCKETS // 2

    def bucket(n):
        nf = np.maximum(n, 1).astype(np.float32)
        large = max_exact + (np.log(nf / np.float32(max_exact)) / np.float32(math.log(MAX_REL_DIST / max_exact))
                             * np.float32(N_BUCKETS - max_exact)).astype(np.int32)
        return np.where(n < max_exact, n, np.minimum(large, N_BUCKETS - 1))

    tabs = []
    for dil, max_dist in ((1, 128), (4, 128), (16, 128), (1, SWA_WINDOW - 1)):
        in_band = (rel >= 0) & (rel <= max_dist)
        tabs.append(np.where(in_band, bucket(np.maximum(rel, 0) * dil), -1))
    return np.stack(tabs).astype(np.int32)


N_SOFT = H_DIL + H_SWA_Q


def _table_of_head(h):
    return jnp.minimum(h // 2, 3)


def _bias_build(rel_bias, tables, name):
    def body(rel_ref, t_ref, o_ref):
        h = pl.program_id(0)
        tb = t_ref[0]
        out = jnp.full((BLK, 2 * BLK), NEG, F32)
        for b in range(N_BUCKETS):
            out = jnp.where(tb == b, rel_ref[b, h], out)
        o_ref[0] = out

    return pl.pallas_call(
        body, name=name, out_shape=jax.ShapeDtypeStruct((N_SOFT, BLK, 2 * BLK), F32), grid=(N_SOFT,),
        in_specs=[pl.BlockSpec(memory_space=pltpu.SMEM),
                  pl.BlockSpec((1, BLK, 2 * BLK), lambda h: (_table_of_head(h), 0, 0))],
        out_specs=pl.BlockSpec((1, BLK, 2 * BLK), lambda h: (h, 0, 0)),
        compiler_params=_params(("parallel",)),
    )(rel_bias, tables)


def _bias_grad(dbias, tables, name):
    def body(d_ref, t_ref, o_ref):
        tb = t_ref[0]
        dv = d_ref[0]
        lane = lax.broadcasted_iota(jnp.int32, (1, LANES), 1)
        out = jnp.zeros((1, LANES), F32)
        for b in range(N_BUCKETS):
            out = jnp.where(lane == b, jnp.sum(jnp.where(tb == b, dv, 0.0)), out)
        o_ref[0] = out

    return pl.pallas_call(
        body, name=name, out_shape=jax.ShapeDtypeStruct((N_SOFT, 1, LANES), F32), grid=(N_SOFT,),
        in_specs=[pl.BlockSpec((1, BLK, 2 * BLK), lambda h: (h, 0, 0)),
                  pl.BlockSpec((1, BLK, 2 * BLK), lambda h: (_table_of_head(h), 0, 0))],
        out_specs=pl.BlockSpec((1, 1, LANES), lambda h: (h, 0, 0)),
        compiler_params=_params(("parallel",)),
    )(dbias, tables)


def _band_specs(g, bias_div):
    qspec = pl.BlockSpec((1, BLK, HEAD_DIM), lambda n, i: (n, i, 0))
    prev = pl.BlockSpec((1, BLK, HEAD_DIM), lambda n, i: (n // g, jnp.maximum(i - 1, 0), 0))
    cur = pl.BlockSpec((1, BLK, HEAD_DIM), lambda n, i: (n // g, i, 0))
    bspec = pl.BlockSpec((1, BLK, 2 * BLK), lambda n, i: (n // bias_div, 0, 0))
    sspec = pl.BlockSpec((1, 1, LANES), lambda n, i: (n, 0, 0))
    colspec = pl.BlockSpec((1, BLK, 1), lambda n, i: (n, i, 0))
    return qspec, prev, cur, bspec, sspec, colspec


def _band_scores(q_ref, kp_ref, kc_ref, b_ref, first):
    qv = q_ref[0]
    bv = b_ref[0]
    sp = _dot(qv, kp_ref[0], 1, 1) + bv[:, :BLK]
    sp = jnp.where(first, NEG, sp)
    sc = _dot(qv, kc_ref[0], 1, 1) + bv[:, BLK:]
    return sp, sc


def _band_fwd(q, k, v, bias, sink, *, g, bias_div, has_sink, name):
    nq, length, _ = q.shape

    def body(q_ref, kp_ref, kc_ref, vp_ref, vc_ref, b_ref, s_ref, o_ref, lse_ref):
        sp, sc = _band_scores(q_ref, kp_ref, kc_ref, b_ref, pl.program_id(1) == 0)
        m = jnp.maximum(jnp.max(sp, axis=1, keepdims=True), jnp.max(sc, axis=1, keepdims=True))
        if has_sink:
            sk = s_ref[0][:, :1]
            m = jnp.maximum(m, sk)
        pp, pc = jnp.exp(sp - m), jnp.exp(sc - m)
        den = jnp.sum(pp, axis=1, keepdims=True) + jnp.sum(pc, axis=1, keepdims=True)
        if has_sink:
            den = den + jnp.exp(sk - m)
        acc = _dot(pp.astype(BF16), vp_ref[0], 1, 0) + _dot(pc.astype(BF16), vc_ref[0], 1, 0)
        o_ref[0] = acc / den
        lse_ref[0] = m + jnp.log(den)

    qspec, prev, cur, bspec, sspec, colspec = _band_specs(g, bias_div)
    return pl.pallas_call(
        body, name=name,
        out_shape=[jax.ShapeDtypeStruct((nq, length, HEAD_DIM), F32), jax.ShapeDtypeStruct((nq, length, 1), F32)],
        grid=(nq, length // BLK), in_specs=[qspec, prev, cur, prev, cur, bspec, sspec],
        out_specs=[qspec, colspec], compiler_params=_params(("parallel", "parallel")),
    )(q, k, k, v, v, bias, sink)


def _band_bwd(q, k, v, bias, sink, o, lse, do, dlse, *, g, bias_div, has_sink, name):
    nq, length, _ = q.shape
    nk, nbias = nq // g, nq // bias_div

    def body(q_ref, kp_ref, kc_ref, vp_ref, vc_ref, b_ref, s_ref, o_ref, lse_ref, do_ref, dlse_ref,
             dq_ref, dk_ref, dv_ref, db_ref, dsk_ref):
        n, i = pl.program_id(0), pl.program_id(1)

        @pl.when((n % g == 0) & (i == 0))
        def _():
            dk_ref[...] = jnp.zeros_like(dk_ref)
            dv_ref[...] = jnp.zeros_like(dv_ref)

        @pl.when((n % bias_div == 0) & (i == 0))
        def _():
            db_ref[...] = jnp.zeros_like(db_ref)

        @pl.when(i == 0)
        def _():
            dsk_ref[...] = jnp.zeros_like(dsk_ref)

        sp, sc = _band_scores(q_ref, kp_ref, kc_ref, b_ref, i == 0)
        lse_v = lse_ref[0]
        pp, pc = jnp.exp(sp - lse_v), jnp.exp(sc - lse_v)
        dov = do_ref[0]
        dob = dov.astype(BF16)
        coef = dlse_ref[0] - jnp.sum(dov * o_ref[0], axis=1, keepdims=True)
        dsp = pp * (_dot(dob, vp_ref[0], 1, 1) + coef)
        dsc = pc * (_dot(dob, vc_ref[0], 1, 1) + coef)
        dspb, dscb = dsp.astype(BF16), dsc.astype(BF16)
        dq_ref[0] = (_dot(dspb, kp_ref[0], 1, 0) + _dot(dscb, kc_ref[0], 1, 0)) * (HEAD_DIM ** -0.5)
        qv = q_ref[0]
        cur = pl.ds(pl.multiple_of(i * BLK, BLK), BLK)
        prv = pl.ds(pl.multiple_of(jnp.maximum(i - 1, 0) * BLK, BLK), BLK)
        dk_ref[0, cur, :] += _dot(dscb, qv, 0, 0)
        dk_ref[0, prv, :] += _dot(dspb, qv, 0, 0)
        dv_ref[0, cur, :] += _dot(pc.astype(BF16), dob, 0, 0)
        dv_ref[0, prv, :] += _dot(pp.astype(BF16), dob, 0, 0)
        db_ref[0, :, :BLK] += dsp
        db_ref[0, :, BLK:] += dsc
        if has_sink:
            dsk_ref[0] += jnp.sum(jnp.exp(s_ref[0][:, :1] - lse_v) * coef)

    qspec, prev, cur, bspec, sspec, colspec = _band_specs(g, bias_div)
    kvfull = pl.BlockSpec((1, length, HEAD_DIM), lambda n, i: (n // g, 0, 0))
    return pl.pallas_call(
        body, name=name,
        out_shape=[jax.ShapeDtypeStruct((nq, length, HEAD_DIM), F32), jax.ShapeDtypeStruct((nk, length, HEAD_DIM), F32),
                   jax.ShapeDtypeStruct((nk, length, HEAD_DIM), F32), jax.ShapeDtypeStruct((nbias, BLK, 2 * BLK), F32),
                   jax.ShapeDtypeStruct((nq, 1, LANES), F32)],
        grid=(nq, length // BLK),
        in_specs=[qspec, prev, cur, prev, cur, bspec, sspec, qspec, colspec, qspec, colspec],
        out_specs=[qspec, kvfull, kvfull, bspec, sspec], compiler_params=_params(("arbitrary", "arbitrary")),
    )(q, k, k, v, v, bias, sink, o, lse, do, dlse)


def _dil_merge(os_, lses, dout, name):
    tr = 512
    n = len(os_)
    tile = pl.BlockSpec((1, tr, HEAD_DIM), lambda h, i: (h, i, 0))
    col = pl.BlockSpec((1, tr, 1), lambda h, i: (h, i, 0))

    def weights(l_refs):
        ls = [r[0] for r in l_refs]
        m = ls[0]
        for lv in ls[1:]:
            m = jnp.maximum(m, lv)
        es = [jnp.exp(lv - m) for lv in ls]
        den = es[0]
        for e in es[1:]:
            den = den + e
        return [e / den for e in es]

    if dout is None:
        def body(*refs):
            alphas = weights(refs[n:2 * n])
            acc = alphas[0] * refs[0][0]
            for gi in range(1, n):
                acc = acc + alphas[gi] * refs[gi][0]
            refs[2 * n][0] = acc

        return pl.pallas_call(
            body, name=name, out_shape=jax.ShapeDtypeStruct(os_[0].shape, F32), grid=(2, SEQ // tr),
            in_specs=[tile] * n + [col] * n, out_specs=tile, compiler_params=_params(("parallel", "parallel")),
        )(*os_, *lses)

    def body(*refs):
        alphas = weights(refs[n:2 * n])
        dov = refs[2 * n][0]
        outs = refs[2 * n + 1:]
        das = [jnp.sum(dov * refs[gi][0], axis=1, keepdims=True) for gi in range(n)]
        dbar = alphas[0] * das[0]
        for gi in range(1, n):
            dbar = dbar + alphas[gi] * das[gi]
        for gi in range(n):
            outs[gi][0] = alphas[gi] * dov
            outs[n + gi][0] = alphas[gi] * (das[gi] - dbar)

    return pl.pallas_call(
        body, name=name,
        out_shape=[jax.ShapeDtypeStruct(os_[0].shape, F32)] * n + [jax.ShapeDtypeStruct(lses[0].shape, F32)] * n,
        grid=(2, SEQ // tr), in_specs=[tile] * n + [col] * n + [tile], out_specs=[tile] * n + [col] * n,
        compiler_params=_params(("parallel", "parallel")),
    )(*os_, *lses, dout)


def _tri(cmp):
    r = lax.broadcasted_iota(jnp.int32, (SB_TILE, SB_TILE), 0)
    c = lax.broadcasted_iota(jnp.int32, (SB_TILE, SB_TILE), 1)
    return cmp(r, c).astype(BF16)


def _cum(x, tri, terms):
    acc, rest = None, x
    for _ in range(terms):
        part = rest.astype(BF16)
        rest = rest - part.astype(F32)
        d = _dot(part, tri, 1, 0)
        acc = d if acc is None else acc + d
    return acc


def _sb_logits(q, k_ref, j, i):
    t = SB_TILE
    ks = k_ref[0, pl.ds(pl.multiple_of(j * t, t), t), :]
    z = _dot(q, ks, 1, 1)
    rows = i * t + lax.broadcasted_iota(jnp.int32, (t, t), 0)
    cols = j * t + lax.broadcasted_iota(jnp.int32, (t, t), 1)
    mask = cols < rows
    e = jnp.exp(-jnp.abs(z))
    lf = jnp.where(mask, -(jnp.maximum(z, 0.0) + jnp.log(1.0 + e)), 0.0)
    return ks, z, e, lf, mask


def _sb_fwd(q, k, v, name):
    h, s, _ = q.shape
    t = SB_TILE

    def body(q_ref, k_ref, v_ref, o_ref, tot_ref):
        i = pl.program_id(1)
        qv = q_ref[0]
        after = _tri(lambda r, c: r > c)

        def step(jj, carry):
            right, acc = carry
            j = i - jj
            _, z, _, lf, mask = _sb_logits(qv, k_ref, j, i)
            between = right + _cum(lf, after, 3)
            w = jnp.where(mask, jnp.exp(z + lf + between), 0.0)
            vs = v_ref[0, pl.ds(pl.multiple_of(j * t, t), t), :]
            return right + jnp.sum(lf, axis=1, keepdims=True), acc + _dot(w.astype(BF16), vs, 1, 0)

        right, acc = lax.fori_loop(0, i + 1, step, (jnp.zeros((t, 1), F32), jnp.zeros((t, HEAD_DIM), F32)))
        o_ref[0] = acc
        tot_ref[0] = right

    tile = pl.BlockSpec((1, t, HEAD_DIM), lambda hh, i: (hh, i, 0))
    full = pl.BlockSpec((1, s, HEAD_DIM), lambda hh, i: (hh, 0, 0))
    return pl.pallas_call(
        body, name=name, out_shape=[jax.ShapeDtypeStruct((h, s, HEAD_DIM), F32), jax.ShapeDtypeStruct((h, s, 1), F32)],
        grid=(h, s // t), in_specs=[tile, full, full],
        out_specs=[tile, pl.BlockSpec((1, t, 1), lambda hh, i: (hh, i, 0))],
        compiler_params=_params(("parallel", "parallel")),
    )(q, k, v)


def _sb_bwd(q, k, v, tot, do, name):
    h, s, _ = q.shape
    t = SB_TILE

    def body(q_ref, k_ref, v_ref, tot_ref, do_ref, dq_ref, dk_ref, dv_ref):
        i = pl.program_id(1)

        @pl.when(i == 0)
        def _():
            dk_ref[...] = jnp.zeros_like(dk_ref)
            dv_ref[...] = jnp.zeros_like(dv_ref)

        qv = q_ref[0]
        dob = do_ref[0].astype(BF16)
        total = tot_ref[0]
        upto = _tri(lambda r, c: r <= c)
        before = _tri(lambda r, c: r < c)

        def step(j, carry):
            left, cleft, dq = carry
            ks, z, e, lf, mask = _sb_logits(qv, k_ref, j, i)
            rows = pl.ds(pl.multiple_of(j * t, t), t)
            vs = v_ref[0, rows, :]
            between = total - (left + _cum(lf, upto, 3))
            w = jnp.where(mask, jnp.exp(z + lf + between), 0.0)
            dlog = w * _dot(dob, vs, 1, 1)
            cfail = cleft + _cum(dlog, before, 2)
            sig = jnp.where(z >= 0.0, 1.0, e) / (1.0 + e)
            dz = jnp.where(mask, dlog * (1.0 - sig) - sig * cfail, 0.0).astype(BF16)
            dk_ref[0, rows, :] += _dot(dz, qv, 0, 0)
            dv_ref[0, rows, :] += _dot(w.astype(BF16), dob, 0, 0)
            return (left + jnp.sum(lf, axis=1, keepdims=True), cleft + jnp.sum(dlog, axis=1, keepdims=True),
                    dq + _dot(dz, ks, 1, 0))

        zero = jnp.zeros((t, 1), F32)
        _, _, dq = lax.fori_loop(0, i + 1, step, (zero, zero, jnp.zeros((t, HEAD_DIM), F32)))
        dq_ref[0] = dq * (HEAD_DIM ** -0.5)

    tile = pl.BlockSpec((1, t, HEAD_DIM), lambda hh, i: (hh, i, 0))
    full = pl.BlockSpec((1, s, HEAD_DIM), lambda hh, i: (hh, 0, 0))
    shp = jax.ShapeDtypeStruct((h, s, HEAD_DIM), F32)
    return pl.pallas_call(
        body, name=name, out_shape=[shp, shp, shp], grid=(h, s // t),
        in_specs=[tile, full, full, pl.BlockSpec((1, t, 1), lambda hh, i: (hh, i, 0)), tile],
        out_specs=[tile, full, full], compiler_params=_params(("arbitrary", "arbitrary")),
    )(q, k, v, tot, do)


def _heads(t):
    return t.reshape(SEQ, -1, HEAD_DIM).transpose(1, 0, 2)


def _unheads(t):
    return t.transpose(1, 0, 2).reshape(SEQ, -1)


def _to_dil(t, d):
    xdim = t.shape[-1]
    return t.reshape(2, SEQ // d, d, xdim).transpose(0, 2, 1, 3).reshape(2 * d, SEQ // d, xdim)


def _from_dil(t, d):
    xdim = t.shape[-1]
    return t.reshape(2, d, SEQ // d, xdim).transpose(0, 2, 1, 3).reshape(2, SEQ, xdim)


def _split_qkv(qkv):
    parts, off = [], 0
    for w in QKV_SPLITS:
        parts.append(qkv[:, off:off + w])
        off += w
    return parts


def _mixer_fwd(qkv, bias, sinks_l, tag):
    scale = HEAD_DIM ** -0.5
    q_sb, k_sb, v_sb, q_dl, k_dl, v_dl, q_sw, k_sw, v_sw = _split_qkv(qkv)
    hq = lambda t: _heads((t * scale).astype(BF16))
    hk = lambda t: _heads(t.astype(BF16))
    st = {}
    st["sb"] = (hq(q_sb), hk(k_sb), hk(v_sb))
    o_sb, st["sb_tot"] = _sb_fwd(*st["sb"], name=f"sb_fwd_{tag}")

    qd, kd, vd = hq(q_dl), hk(k_dl), hk(v_dl)
    no_sink = jnp.zeros((1, 1, LANES), F32)
    st["dil"], outs, lses = [], [], []
    for gi, (_, d) in enumerate(DIL_PATTERNS):
        hs = slice(2 * gi, 2 * gi + 2)
        qg, kg, vg = _to_dil(qd[hs], d), _to_dil(kd[hs], d), _to_dil(vd[hs], d)
        sink = jnp.broadcast_to(no_sink, (2 * d, 1, LANES))
        og, lg = _band_fwd(qg, kg, vg, bias[hs], sink, g=1, bias_div=d, has_sink=False, name=f"dil{gi}_fwd_{tag}")
        st["dil"].append((qg, kg, vg, sink, og, lg))
        outs.append(_from_dil(og, d))
        lses.append(_from_dil(lg, d))
    st["dil_outs"], st["dil_lses"] = outs, lses
    o_dil = _dil_merge(outs, lses, None, name=f"dil_merge_fwd_{tag}")

    sink = jnp.broadcast_to(sinks_l.reshape(H_SWA_Q, 1, 1), (H_SWA_Q, 1, LANES))
    st["swa"] = (hq(q_sw), hk(k_sw), hk(v_sw), sink)
    o_sw, l_sw = _band_fwd(*st["swa"][:3], bias[H_DIL:], sink, g=H_SWA_Q // H_SWA_KV, bias_div=1, has_sink=True,
                           name=f"swa_fwd_{tag}")
    st["swa_out"] = (o_sw, l_sw)
    return (_unheads(o_sb), _unheads(o_dil), _unheads(o_sw)), st


def _mixer_bwd(st, bias, do_sb, do_dil, do_swa, tag):
    dq_sb, dk_sb, dv_sb = _sb_bwd(*st["sb"], st["sb_tot"], _heads(do_sb), name=f"sb_bwd_{tag}")

    dmerge = _dil_merge(st["dil_outs"], st["dil_lses"], _heads(do_dil), name=f"dil_merge_bwd_{tag}")
    dqs, dks, dvs, dbs = [], [], [], []
    for gi, (_, d) in enumerate(DIL_PATTERNS):
        qg, kg, vg, sink, og, lg = st["dil"][gi]
        hs = slice(2 * gi, 2 * gi + 2)
        dq, dk, dv, db, _ = _band_bwd(qg, kg, vg, bias[hs], sink, og, lg, _to_dil(dmerge[gi], d), _to_dil(dmerge[3 + gi], d),
                                      g=1, bias_div=d, has_sink=False, name=f"dil{gi}_bwd_{tag}")
        dqs.append(_from_dil(dq, d))
        dks.append(_from_dil(dk, d))
        dvs.append(_from_dil(dv, d))
        dbs.append(db)

    q_sw, k_sw, v_sw, sink = st["swa"]
    o_sw, l_sw = st["swa_out"]
    dq_sw, dk_sw, dv_sw, db_sw, dsink = _band_bwd(q_sw, k_sw, v_sw, bias[H_DIL:], sink, o_sw, l_sw, _heads(do_swa),
                                                  jnp.zeros_like(l_sw), g=H_SWA_Q // H_SWA_KV, bias_div=1, has_sink=True,
                                                  name=f"swa_bwd_{tag}")
    dqkv = jnp.concatenate(
        [_unheads(dq_sb), _unheads(dk_sb), _unheads(dv_sb),
         _unheads(jnp.concatenate(dqs, 0)), _unheads(jnp.concatenate(dks, 0)), _unheads(jnp.concatenate(dvs, 0)),
         _unheads(dq_sw), _unheads(dk_sw), _unheads(dv_sw)], axis=1)
    return dqkv, jnp.concatenate(dbs + [db_sw], 0), dsink[:, 0, 0]


PIECES = ("ffn0", "mix", "ffn1")


def _ffn_fwd(x_in, w, gain, mod_j, tag):
    st = {"x": x_in, "w": w}
    st["h"] = _norm_fwd(x_in, _row(gain), _row(mod_j[1]), _row(mod_j[0]), name=f"norm_fwd_{tag}")
    st["a"], st["u"], st["s"] = _ffn_up(st["h"], w["gate"], w["up"], name=f"up_{tag}")
    st["f"], x_out = _mm(st["s"], w["down"], res=x_in, colscale=_row(0.5 * mod_j[2]), emit_acc=True, tm=256, tn=1024,
                         name=f"down_{tag}")
    return x_out, st


def _ffn_bwd(dx_out, st, gain, mod_j, tag, done):
    w = st["w"]

    def latest(new, old):
        return old if new is None else new

    df, dgate = _gate_bwd(dx_out, st["f"], _row(0.5 * mod_j[2]), 0.5, name=f"gate_bwd_{tag}")
    token = done({"down": _mm_tn(st["s"], df, name=f"dwd_{tag}")})
    da, du = _ffn_bwd_ds(df, w["down"], st["a"], st["u"], name=f"ds_{tag}")
    token = latest(done({"gate": _mm_tn(da, st["h"], after=token, name=f"dwg_{tag}")}), token)
    token = latest(done({"up": _mm_tn(du, st["h"], after=token, name=f"dwu_{tag}")}), token)
    dh = _mm2(da, w["gate"], du, w["up"], after=token, name=f"dh_{tag}")
    dx_in, sum_dh, sum_dhx = _norm_bwd(st["x"], dh, dx_out, _row(gain), _row(mod_j[1]), name=f"norm_bwd_{tag}")
    dmod = jnp.concatenate([sum_dh, gain * sum_dhx, dgate], 0)
    return dx_in, dmod, (1.0 + mod_j[1]) * sum_dhx[0]


def _mix_fwd(x_in, w, gain, mod_j, bias, sinks_l, tag):
    st = {"x": x_in, "w": w}
    st["h"] = _norm_fwd(x_in, _row(gain), _row(mod_j[1]), _row(mod_j[0]), name=f"norm_fwd_mix_{tag}")
    qkv = _mm(st["h"], w["qkv"], tb=True, name=f"qkv_{tag}")
    st["gates"] = _mm(st["h"], w["gates"], tb=True, name=f"gates_{tag}")
    st["o"], st["mix"] = _mixer_fwd(qkv, bias, sinks_l, tag)
    st["merged"] = _merge_fwd(*st["o"], st["gates"], w["br_sb"], w["br_dil"], w["br_swa"], name=f"merge_fwd_{tag}")
    st["f"], x_out = _mm(st["merged"], w["out"], res=x_in, colscale=_row(mod_j[2]), emit_acc=True, name=f"out_{tag}")
    return x_out, st


def _mix_bwd(dx_out, st, gain, mod_j, bias, tag, done):
    w = st["w"]
    df, dgate = _gate_bwd(dx_out, st["f"], _row(mod_j[2]), 1.0, name=f"gate_bwd_mix_{tag}")
    g = {"out": _mm_tn(st["merged"], df, name=f"dw_out_{tag}")}
    dmerged = _mm(df, w["out"], tb=True, name=f"dmerged_{tag}")
    dgates, do_sb, do_dil, do_swa, dbr_sb, dbr_dil, dbr_swa = _merge_bwd(
        dmerged, *st["o"], st["gates"], w["br_sb"], w["br_dil"], w["br_swa"], name=f"merge_bwd_{tag}")
    g["br_sb"] = _mm_tn(st["o"][0], dbr_sb, name=f"dw_br_sb_{tag}")
    g["br_dil"] = _mm_tn(st["o"][1], dbr_dil, name=f"dw_br_dil_{tag}")
    g["br_swa"] = _mm_tn(st["o"][2], dbr_swa, name=f"dw_br_swa_{tag}")
    dqkv, dbias, dsinks = _mixer_bwd(st["mix"], bias, do_sb, do_dil, do_swa, tag)
    g["qkv"] = _mm_tn(dqkv, st["h"], name=f"dw_qkv_{tag}")
    g["gates"] = _mm_tn(dgates, st["h"], name=f"dw_gates_{tag}")
    dh = _mm2(dqkv, w["qkv"], dgates, w["gates"], after=done(g), tm=128, name=f"dh_mix_{tag}")
    dx_in, sum_dh, sum_dhx = _norm_bwd(st["x"], dh, dx_out, _row(gain), _row(mod_j[1]), name=f"norm_bwd_mix_{tag}")
    dmod = jnp.concatenate([sum_dh, gain * sum_dhx, dgate], 0)
    return dx_in, dmod, (1.0 + mod_j[1]) * sum_dhx[0], dbias, dsinks


def _local_step(x, target, mod, gains, weights_of, rel_bias, sinks, final_gain, grads_done):
    tables = jnp.asarray(_bucket_tables())
    bias = _bias_build(rel_bias, tables, name="bias_build")
    states, h = [], x
    for l in range(DEPTH):
        st = {}
        for j, piece in enumerate(PIECES):
            w = weights_of(l, piece, h)
            if piece == "mix":
                h, st[piece] = _mix_fwd(h, w, gains[l, j], mod[l, j], bias, sinks[l], f"l{l}")
            else:
                h, st[piece] = _ffn_fwd(h, w, gains[l, j], mod[l, j], f"{piece}_l{l}")
        states.append(st)
    loss, dx, dfinal = _final_loss(h, target, _row(final_gain), name="final_loss")
    dmods = [[None] * 3 for _ in range(DEPTH)]
    dgains = [[None] * 3 for _ in range(DEPTH)]
    dsinks = [None] * DEPTH
    dbias = None
    for l in reversed(range(DEPTH)):
        for j in reversed(range(3)):
            piece = PIECES[j]
            done = lambda grads, l=l, piece=piece: grads_done(l, piece, grads)
            if piece == "mix":
                dx, dmods[l][j], dgains[l][j], db, dsinks[l] = _mix_bwd(dx, states[l][piece], gains[l, j], mod[l, j], bias, f"l{l}", done)
                dbias = db if dbias is None else dbias + db
            else:
                dx, dmods[l][j], dgains[l][j] = _ffn_bwd(dx, states[l][piece], gains[l, j], mod[l, j], f"{piece}_l{l}", done)
    drel = _bias_grad(dbias, tables, name="bias_grad")[:, 0, :N_BUCKETS].T
    dmod = jnp.stack([jnp.stack(m) for m in dmods])
    dgain = jnp.stack([jnp.stack(g) for g in dgains])
    return loss, dx, dmod, dgain, dfinal[0], drel, jnp.stack(dsinks)


BR_ROWS = (H_SB * HEAD_DIM, 2 * HEAD_DIM, H_SWA_Q * HEAD_DIM)


def _lanes_unshard(g, lead):
    _, rows, _ = g.shape
    r = rows // lead
    return g.reshape(N_DEV, lead, r, LANES).transpose(1, 2, 0, 3).reshape(lead, r, N_DEV * LANES)


def _lanes_shard(full):
    lead, r, _ = full.shape
    return full.reshape(lead, r, N_DEV, LANES).transpose(2, 0, 1, 3).reshape(N_DEV, lead * r, LANES)


def _pack_rows(parts, dtype):
    flat = jnp.concatenate([p.astype(dtype).reshape(-1) for p in parts])
    pad = (-flat.shape[0]) % (16 * LANES)
    if pad:
        flat = jnp.concatenate([flat, jnp.zeros((pad,), dtype)])
    return flat.reshape(-1, LANES)


def _unshard(gathered, axis):
    moved = jnp.moveaxis(gathered, 0, axis)
    shape = list(moved.shape)
    shape[axis:axis + 2] = [shape[axis] * shape[axis + 1]]
    return moved.reshape(shape)


def kernel(x, c, w_ada, b_ada, norm_gain, w_ffn_gate, w_ffn_up, w_ffn_down, w_in, w_br_sb, w_br_dil, w_br_swa, w_out, sinks, rel_bias, final_gain, loss_target, m_w_ada, m_b_ada, m_norm_gain, m_w_ffn_gate, m_w_ffn_up, m_w_ffn_down, m_w_in, m_w_br_sb, m_w_br_dil, m_w_br_swa, m_w_out, m_sinks, m_rel_bias, m_final_gain, v_w_ada, v_b_ada, v_norm_gain, v_w_ffn_gate, v_w_ffn_up, v_w_ffn_down, v_w_in, v_w_br_sb, v_w_br_dil, v_w_br_swa, v_w_out, v_sinks, v_rel_bias, v_final_gain):
    me = 4 * lax.axis_index("x") + 2 * lax.axis_index("y") + lax.axis_index("c")
    d = D_MODEL
    gate_t, up_t, in_t = jnp.swapaxes(w_ffn_gate, 2, 3), jnp.swapaxes(w_ffn_up, 2, 3), jnp.swapaxes(w_in, 1, 2)

    def piece_shards(l, piece):
        bf = lambda t: t.astype(BF16)
        if piece == "mix":
            return [bf(in_t[l]), jnp.concatenate([bf(w_br_sb[l]), bf(w_br_dil[l]), bf(w_br_swa[l])], 0), bf(w_out[l])]
        i = PIECES.index(piece) // 2
        return [bf(gate_t[l, i]), bf(up_t[l, i]), bf(w_ffn_down[l, i])]

    br_off = np.concatenate([[0], np.cumsum(BR_ROWS)])

    def piece_weights(gathered, piece):
        if piece == "mix":
            g_in, g_br, g_out = gathered
            f_in = g_in.reshape(D_QKV + D_GATES, d)
            f_br = [_lanes_unshard(g_br[:, br_off[k]:br_off[k + 1]], 1)[0] for k in range(3)]
            return {"qkv": f_in[:D_QKV], "gates": f_in[D_QKV:], "br_sb": f_br[0], "br_dil": f_br[1], "br_swa": f_br[2],
                    "out": g_out.reshape(d, d)}
        return {n: g.reshape(D_FF, d) for n, g in zip(("gate", "up", "down"), gathered)}

    small, = _all_gather([_pack_rows([c, norm_gain], F32)], name="gather_cond")
    c_all = small[:, :d // LANES].reshape(N_DEV, d)
    gains = _unshard(small[:, d // LANES:d // LANES + 6].reshape(N_DEV, DEPTH, 3, LANES), 2)

    cols = w_ada.shape[2]
    mod_cols = jnp.stack([_ada_fwd(c_all, w_ada[l], name=f"ada_fwd_l{l}") for l in range(DEPTH)])
    mod_all, = _all_gather([_pack_rows([mod_cols], F32)], name="gather_mod")
    mod_all = mod_all.reshape(N_DEV, -1)[:, :DEPTH * N_DEV * cols].reshape(N_DEV, DEPTH, N_DEV, cols)
    mod_mine = lax.dynamic_index_in_dim(mod_all, me, axis=2, keepdims=False)
    mod = (mod_mine.transpose(1, 0, 2).reshape(DEPTH, N_DEV * cols) + b_ada).reshape(DEPTH, 3, 3, d)

    order = [(l, piece) for l in range(DEPTH) for piece in PIECES]
    eager, ahead = 2, 3
    in_flight = {}
    n_tensors = 3
    first = _all_gather([s for k in range(eager) for s in piece_shards(*order[k])], after=mod_all, name="gather_first")

    def start_gather(k, after):
        l, piece = order[k]
        in_flight[k], token = _exchange_start(piece_shards(l, piece), after, gather=True, name=f"gather_{piece}_l{l}_start")
        return token

    token = first[0]
    for k in range(eager, eager + ahead - 1):
        token = start_gather(k, token)
    mod = mod + token[0, 0]

    def weights_of(l, piece, h):
        k = order.index((l, piece))
        after = start_gather(k + ahead, h) if eager <= k + ahead < len(order) and k + ahead not in in_flight else h
        if k < eager:
            return piece_weights(first[n_tensors * k:n_tensors * (k + 1)], piece)
        return piece_weights(_exchange_wait(in_flight[k], after, gather=True, name=f"gather_{piece}_l{l}_wait"), piece)

    exchanges, have = {}, {}

    def grads_done(l, piece, g):
        key = (l, piece)
        have.setdefault(key, {}).update(g)
        if piece == "mix":
            if len(have[key]) < 6:
                return None
            g = have[key]
            s_br = jnp.concatenate([_lanes_shard(g[n][None]) for n in ("br_sb", "br_dil", "br_swa")], 1)
            groups = [(("in", "br", "out"), [jnp.concatenate([g["qkv"], g["gates"]], 0).reshape(N_DEV, -1, d), s_br,
                                             g["out"].reshape(N_DEV, -1, d)])]
        elif key == order[0]:
            groups = [((n,), [t.reshape(N_DEV, -1, d)]) for n, t in g.items()]
        elif len(have[key]) < 3:
            return None
        else:
            groups = [(("gate", "up", "down"), [have[key][n].reshape(N_DEV, -1, d) for n in ("gate", "up", "down")])]
        token = None
        for names, sg in groups:
            state, token = _exchange_start(sg, sg[0], gather=False, name=f"exchange_{piece}_l{l}_{names[0]}_start")
            exchanges.setdefault(key, []).append((names, state))
        return token

    loss, dx, dmod, dgains, dfinal, drel, dsinks = _local_step(
        x[0], loss_target[0], mod, gains, weights_of, rel_bias, sinks, final_gain, grads_done)

    small_parts = [dmod, dgains, dfinal, drel.T, dsinks, loss[0, :1]]
    small_sizes = [int(np.prod(p.shape)) for p in small_parts]
    small_all, = _all_gather([_pack_rows(small_parts, F32)], name="gather_small")
    small_sum = _sum_parts([small_all], name="sum_small").reshape(-1)
    offs = np.concatenate([[0], np.cumsum(small_sizes)])
    g_b_ada = small_sum[offs[0]:offs[1]].reshape(DEPTH, 9 * d)
    g_gain_full = small_sum[offs[1]:offs[2]].reshape(DEPTH, 3, d)
    g_norm_gain = lax.dynamic_slice_in_dim(g_gain_full, me * LANES, LANES, axis=2)
    g_final = small_sum[offs[2]:offs[3]]
    g_rel = small_sum[offs[3]:offs[4]].reshape(N_SOFT, N_BUCKETS).T
    g_sinks = small_sum[offs[4]:offs[5]].reshape(DEPTH, H_SWA_Q)
    loss_total = small_sum[offs[5]]

    dmod_all = small_all.reshape(N_DEV, -1)[:, :DEPTH * 9 * d].reshape(N_DEV, DEPTH, 9 * d)
    dmod_cols = lax.dynamic_slice_in_dim(dmod_all, me * cols, cols, axis=2)
    g_w_ada = jnp.stack([_ada_bwd(c_all.T, dmod_cols[:, l], name=f"ada_bwd_l{l}") for l in range(DEPTH)])

    state = {"w_ada": (w_ada, m_w_ada, v_w_ada), "b_ada": (b_ada, m_b_ada, v_b_ada),
             "norm_gain": (norm_gain, m_norm_gain, v_norm_gain), "w_ffn_gate": (w_ffn_gate, m_w_ffn_gate, v_w_ffn_gate),
             "w_ffn_up": (w_ffn_up, m_w_ffn_up, v_w_ffn_up), "w_ffn_down": (w_ffn_down, m_w_ffn_down, v_w_ffn_down),
             "w_in": (w_in, m_w_in, v_w_in), "w_br_sb": (w_br_sb, m_w_br_sb, v_w_br_sb),
             "w_br_dil": (w_br_dil, m_w_br_dil, v_w_br_dil), "w_br_swa": (w_br_swa, m_w_br_swa, v_w_br_swa),
             "w_out": (w_out, m_w_out, v_w_out), "sinks": (sinks, m_sinks, v_sinks),
             "rel_bias": (rel_bias, m_rel_bias, v_rel_bias), "final_gain": (final_gain, m_final_gain, v_final_gain)}
    grad, update = {}, {}

    def adamw(n, g, transposed=False):
        w, m, v = (jnp.swapaxes(t, -1, -2) for t in state[n]) if transposed else state[n]
        if w.ndim == 1:
            out = tuple(t.reshape(w.shape) for t in _adamw(_row(w), _row(g), _row(m), _row(v), name=f"adamw_{n}"))
        else:
            out = _adamw(w, g, m, v, name=f"adamw_{n}")
        if transposed:
            grad[n], update[n] = jnp.swapaxes(g, -1, -2), tuple(jnp.swapaxes(t, -1, -2) for t in out)
        else:
            grad[n], update[n] = g, out

    for n, g in (("w_ada", g_w_ada), ("b_ada", g_b_ada), ("norm_gain", g_norm_gain), ("sinks", g_sinks),
                 ("rel_bias", g_rel), ("final_gain", g_final)):
        adamw(n, g)

    after = update["w_ada"][0]
    parts = {}
    for key in reversed(order):
        for names, ex_state in exchanges[key]:
            landed = _exchange_wait(ex_state, after, gather=False, name=f"exchange_{key[1]}_l{key[0]}_{names[0]}_wait")
            parts.setdefault(key, {}).update(zip(names, landed))
            after = landed[0]
    ffn_keys = [key for key in order if key[1] != "mix"]
    mix_keys = [key for key in order if key[1] == "mix"]
    sums = {n: _sum_parts([parts[key][n] for key in ffn_keys], name=f"sum_grads_{n}") for n in ("gate", "up", "down")}
    sums.update({n: _sum_parts([parts[key][n] for key in mix_keys], name=f"sum_grads_{n}") for n in ("in", "br", "out")})
    br_sums = sums["br"].reshape(DEPTH, -1, LANES)
    adamw("w_ffn_gate", sums["gate"].reshape(gate_t.shape), transposed=True)
    adamw("w_ffn_up", sums["up"].reshape(up_t.shape), transposed=True)
    adamw("w_ffn_down", sums["down"].reshape(w_ffn_down.shape))
    adamw("w_in", sums["in"].reshape(in_t.shape), transposed=True)
    adamw("w_br_sb", br_sums[:, br_off[0]:br_off[1]])
    adamw("w_br_dil", br_sums[:, br_off[1]:br_off[2]])
    adamw("w_br_swa", br_sums[:, br_off[2]:br_off[3]])
    adamw("w_out", sums["out"].reshape(w_out.shape))

    names = ["w_ada", "b_ada", "norm_gain", "w_ffn_gate", "w_ffn_up", "w_ffn_down", "w_in", "w_br_sb", "w_br_dil",
             "w_br_swa", "w_out", "sinks", "rel_bias", "final_gain"]
    return (loss_total, dx[None], *[grad[n] for n in names], *[update[n][0] for n in names],
            *[update[n][1] for n in names], *[update[n][2] for n in names])
```

```python
import math

import numpy as np
import jax
import jax.numpy as jnp
from jax import lax
from jax.experimental import pallas as pl
from jax.experimental.pallas import tpu as pltpu

F32, BF16 = jnp.float32, jnp.bfloat16

SEQ, D_MODEL, D_FF, HEAD_DIM = 2048, 1024, 2816, 64
DEPTH = 2
BLK = 128
H_SB, H_DIL, H_SWA_Q, H_SWA_KV = 4, 6, 6, 2
DIL_PATTERNS = ((128, 1), (512, 4), (2048, 16))
SWA_WINDOW = 128
N_BUCKETS, MAX_REL_DIST = 32, 2048
RMS_EPS = 1e-6
D_QKV = 2560
D_GATES = 3 * D_MODEL
QKV_SPLITS = (256, 256, 256, 384, 384, 384, 384, 128, 128)
ADAM_LR, ADAM_B1, ADAM_B2, ADAM_EPS, ADAM_WD, ADAM_STEP = 0.001, 0.9, 0.999, 1e-08, 0.01, 10

N_DEV = 8
LANES = 128
NEG = -1e30
SB_TILE = 256
VMEM_LIMIT_BYTES = 48 * 1024 * 1024
HBM = pl.BlockSpec(memory_space=pltpu.HBM)
MESH = pl.DeviceIdType.MESH


def _tile(n, target):
    t = (min(n, target) // LANES) * LANES
    while t >= LANES:
        if n % t == 0:
            return t
        t -= LANES
    return n


def _row_tile(r, cap):
    t = (min(r, cap) // 16) * 16
    while t > 16 and r % t:
        t -= 16
    return t


def _params(semantics=None):
    return pltpu.CompilerParams(dimension_semantics=semantics, vmem_limit_bytes=VMEM_LIMIT_BYTES)


def _dot(a, b, ca, cb):
    return lax.dot_general(a, b, (((ca,), (cb,)), ((), ())), preferred_element_type=F32)


def _sigmoid(a):
    return 1.0 / (1.0 + jnp.exp(-a))


def _row(v):
    return v.reshape(1, -1)


def _all_gather(arrs, name, after=None):
    n = len(arrs)
    ins = list(arrs) + ([] if after is None else [after])

    def body(*refs):
        x_refs, out_refs = refs[:n], refs[len(ins):len(ins) + n]
        send_sems, recv_sems, local_sems = refs[len(ins) + n:]
        x, y, c = lax.axis_index("x"), lax.axis_index("y"), lax.axis_index("c")
        me, sibling = (x, y, c), (x, y, 1 - c)
        chips = [(1 - x, y), (x, 1 - y), (1 - x, 1 - y)]

        def slot(t, px, py, pc):
            return out_refs[t].at[4 * px + 2 * py + pc]

        def copy(t, k, block, to, src=None):
            return pltpu.make_async_remote_copy(
                src_ref=slot(t, *block) if src is None else src, dst_ref=slot(t, *block),
                send_sem=send_sems.at[7 * t + k], recv_sem=recv_sems.at[7 * t + k], device_id=to, device_id_type=MESH)

        mine = [pltpu.make_async_copy(x_refs[t], slot(t, *me), local_sems.at[t]) for t in range(n)]
        for cp in mine:
            cp.start()
        first = []
        for t in range(n):
            first.append(copy(t, 0, me, sibling, src=x_refs[t]))
            first += [copy(t, 1 + j, me, (*chip, c), src=x_refs[t]) for j, chip in enumerate(chips)]
        for cp in first:
            cp.start()
        passed = []
        for j, chip in enumerate(chips):
            for t in range(n):
                copy(t, 1 + j, (*chip, c), me).wait_recv()
                passed.append(copy(t, 4 + j, (*chip, c), sibling))
                passed[-1].start()
        for t in range(n):
            copy(t, 0, sibling, me).wait_recv()
        for j, chip in enumerate(chips):
            for t in range(n):
                copy(t, 4 + j, (*chip, 1 - c), me).wait_recv()
        for cp in first + passed:
            cp.wait_send()
        for cp in mine:
            cp.wait()

    return pl.pallas_call(
        body, name=name, out_shape=[jax.ShapeDtypeStruct((N_DEV,) + a.shape, a.dtype) for a in arrs],
        in_specs=[HBM] * n + [pl.BlockSpec(memory_space=pl.ANY)] * (len(ins) - n), out_specs=[HBM] * n,
        scratch_shapes=[pltpu.SemaphoreType.DMA((7 * n,)), pltpu.SemaphoreType.DMA((7 * n,)), pltpu.SemaphoreType.DMA((n,))],
    )(*ins)


def _direct_copies(x_refs, land_refs, send_sems, recv_sems, local_sems, gather):
    x, y, c = lax.axis_index("x"), lax.axis_index("y"), lax.axis_index("c")
    me = 4 * x + 2 * y + c
    sends, recvs = [], []
    for k in range(1, N_DEV):
        px = 1 - x if (k >> 2) & 1 else x
        py = 1 - y if (k >> 1) & 1 else y
        pc = 1 - c if k & 1 else c
        peer = 4 * px + 2 * py + pc
        for t, (x_ref, land_ref) in enumerate(zip(x_refs, land_refs)):
            sem = 7 * t + k - 1
            for out, src, slot in ((sends, x_ref if gather else x_ref.at[peer], me),
                                   (recvs, x_ref if gather else x_ref.at[me], peer)):
                out.append(pltpu.make_async_remote_copy(
                    src_ref=src, dst_ref=land_ref.at[slot], send_sem=send_sems.at[sem], recv_sem=recv_sems.at[sem],
                    device_id=(px, py, pc), device_id_type=MESH))
    own = [pltpu.make_async_copy(x_ref if gather else x_ref.at[me], land_ref.at[me], local_sems.at[t])
           for t, (x_ref, land_ref) in enumerate(zip(x_refs, land_refs))]
    return sends, recvs, own


SEM =pl.BlockSpec(memory_space=pltpu.SEMAPHORE)
ANY = pl.BlockSpec(memory_space=pl.ANY)
SIDE_EFFECT = pltpu.SideEffectType.DATAFLOW_SIDE_EFFECTING


def _exchange_start(arrs, after, *, gather, name):
    n = len(arrs)
    lands = [lax.empty(((N_DEV,) + a.shape) if gather else a.shape, a.dtype) for a in arrs]

    def body(*refs):
        sends, _, own = _direct_copies(refs[:n], refs[n:2 * n], *refs[2 * n + 1:2 * n + 4], gather)
        for cp in own + sends:
            cp.start()
        refs[-1][...] = jnp.zeros_like(refs[-1])

    ops = [pltpu.with_memory_space_constraint(a, pltpu.HBM) for a in list(arrs) + lands]
    out = pl.pallas_call(
        body, name=name,
        out_shape=(pltpu.SemaphoreType.DMA((7 * n,)), pltpu.SemaphoreType.DMA((7 * n,)), pltpu.SemaphoreType.DMA((n,)),
                   *[pltpu.HBM(a.shape, a.dtype) for a in ops], jax.ShapeDtypeStruct((8, LANES), F32)),
        in_specs=[HBM] * (2 * n) + [ANY],
        out_specs=(SEM, SEM, SEM, *[HBM] * (2 * n), pl.BlockSpec(memory_space=pltpu.VMEM)),
        input_output_aliases={t: 3 + t for t in range(2 * n)},
        compiler_params=pltpu.CompilerParams(has_side_effects=SIDE_EFFECT),
    )(*ops, after)
    return (out[:3], out[3:3 + n], out[3 + n:3 + 2 * n]), out[-1]


def _exchange_wait(state, after, *, gather, name):
    sems, arrs, lands = state
    n = len(arrs)

    def body(*refs):
        sends, recvs, own = _direct_copies(refs[:n], refs[n:2 * n], *refs[2 * n:2 * n + 3], gather)
        for cp in own:
            cp.wait()
        for cp in sends:
            cp.wait_send()
        for cp in recvs:
            cp.wait_recv()

    out = pl.pallas_call(
        body, name=name, out_shape=tuple(pltpu.HBM(a.shape, a.dtype) for a in list(arrs) + list(lands)),
        in_specs=[HBM] * (2 * n) + [SEM, SEM, SEM, ANY], out_specs=tuple([HBM] * (2 * n)),
        input_output_aliases={t: t for t in range(2 * n)},
        compiler_params=pltpu.CompilerParams(has_side_effects=SIDE_EFFECT),
    )(*arrs, *lands, *sems, after)
    return out[n:]


def _sum_parts(groups, name):
    n, r, cdim = groups[0].shape
    tr = _row_tile(r, max(16, (1 << 21) // (n * cdim * groups[0].dtype.itemsize)))
    steps = r // tr

    def body(*refs):
        o_ref = refs[-1]
        gg = pl.program_id(0)
        for gi in range(len(groups)):
            @pl.when(gg == gi)
            def _(gi=gi):
                acc = refs[gi][0].astype(F32)
                for k in range(1, n):
                    acc = acc + refs[gi][k].astype(F32)
                o_ref[...] = acc

    def in_spec(gi):
        return pl.BlockSpec((n, tr, cdim), lambda gg, i: (0, jnp.where(gg == gi, i, 0), 0))

    return pl.pallas_call(
        body, name=name, out_shape=jax.ShapeDtypeStruct((len(groups) * r, cdim), F32), grid=(len(groups), steps),
        in_specs=[in_spec(gi) for gi in range(len(groups))],
        out_specs=pl.BlockSpec((tr, cdim), lambda gg, i: (gg * steps + i, 0)),
        compiler_params=_params(("parallel", "parallel")),
    )(*groups)


def _mm_tn(a, b, *, name, after=None, tm=512, tn=1024):
    k, m = a.shape
    n = b.shape[1]
    tm, tn = _tile(m, tm), _tile(n, tn)

    def body(a_ref, b_ref, *rest):
        o_ref, at_ref = rest[-2], rest[-1]

        @pl.when(pl.program_id(1) == 0)
        def _():
            at_ref[...] = a_ref[...].astype(BF16).T

        o_ref[...] = _dot(at_ref[...], b_ref[...].astype(BF16), 1, 0).astype(BF16)

    ins = [a, b] + ([] if after is None else [after])
    return pl.pallas_call(
        body, name=name, out_shape=jax.ShapeDtypeStruct((m, n), BF16), grid=(m // tm, n // tn),
        in_specs=[pl.BlockSpec((k, tm), lambda i, j: (0, i)), pl.BlockSpec((k, tn), lambda i, j: (0, j))] + [ANY] * (len(ins) - 2),
        out_specs=pl.BlockSpec((tm, tn), lambda i, j: (i, j)),
        scratch_shapes=[pltpu.VMEM((tm, k), BF16)], compiler_params=_params(("parallel", "arbitrary")),
    )(*ins)


def _mm2(a1, b1, a2, b2, *, name, after=None, tm=256, tn=1024):
    m = a1.shape[0]
    n = b1.shape[1]
    tm, tn = _tile(m, tm), _tile(n, tn)

    def body(a1_ref, b1_ref, a2_ref, b2_ref, *rest):
        rest[-1][...] = (_dot(a1_ref[...].astype(BF16), b1_ref[...], 1, 0)
                         + _dot(a2_ref[...].astype(BF16), b2_ref[...], 1, 0))

    ins = [a1, b1, a2, b2] + ([] if after is None else [after])

    def a_spec(t):
        return pl.BlockSpec((tm, t.shape[1]), lambda i, j: (i, 0))

    def b_spec(t):
        return pl.BlockSpec((t.shape[0], tn), lambda i, j: (0, j))

    return pl.pallas_call(
        body, name=name, out_shape=jax.ShapeDtypeStruct((m, n), F32), grid=(m // tm, n // tn),
        in_specs=[a_spec(a1), b_spec(b1), a_spec(a2), b_spec(b2)] + [ANY] * (len(ins) - 4),
        out_specs=pl.BlockSpec((tm, tn), lambda i, j: (i, j)), compiler_params=_params(("parallel", "parallel")),
    )(*ins)


def _mm(a, b, *, name, ta=False, tb=False, res=None, colscale=None, emit_acc=False,
        out_dtype=F32, tm=512, tn=512):
    m, k = (a.shape[1], a.shape[0]) if ta else a.shape
    n = b.shape[0] if tb else b.shape[1]
    tm, tn = _tile(m, tm), _tile(n, tn)
    ca, cb = (0 if ta else 1), (1 if tb else 0)
    a_spec = pl.BlockSpec((k, tm), lambda i, j: (0, i)) if ta else pl.BlockSpec((tm, k), lambda i, j: (i, 0))
    b_spec = pl.BlockSpec((tn, k), lambda i, j: (j, 0)) if tb else pl.BlockSpec((k, tn), lambda i, j: (0, j))
    tile = pl.BlockSpec((tm, tn), lambda i, j: (i, j))
    ins, in_specs = [a, b], [a_spec, b_spec]
    if res is not None:
        ins.append(res)
        in_specs.append(tile)
    if colscale is not None:
        ins.append(colscale)
        in_specs.append(pl.BlockSpec((1, tn), lambda i, j: (0, j)))
    n_in = len(ins)

    def body(*refs):
        outs = refs[n_in:]
        acc = _dot(refs[0][...].astype(BF16), refs[1][...].astype(BF16), ca, cb)
        val, p = acc, 2
        if res is not None:
            r_val, p = refs[p][...], p + 1
        if colscale is not None:
            val = val * refs[p][...]
        if res is not None:
            val = r_val + val
        if emit_acc:
            outs[0][...] = acc
        outs[-1][...] = val.astype(out_dtype)

    out_shape = [jax.ShapeDtypeStruct((m, n), out_dtype)]
    out_specs = [tile]
    if emit_acc:
        out_shape.insert(0, jax.ShapeDtypeStruct((m, n), F32))
        out_specs.insert(0, tile)
    out = pl.pallas_call(
        body, name=name, out_shape=out_shape, grid=(m // tm, n // tn), in_specs=in_specs, out_specs=out_specs,
        compiler_params=_params(("parallel", "parallel")),
    )(*ins)
    return out if emit_acc else out[0]


def _norm_fwd(x, g, scale, shift, name, after=None):
    s, d = x.shape
    tr = 256

    def body(x_ref, g_ref, sc_ref, sh_ref, *rest):
        xv = x_ref[...]
        rstd = lax.rsqrt(jnp.mean(xv * xv, axis=-1, keepdims=True) + RMS_EPS)
        rest[-1][...] = (xv * rstd * g_ref[...] * (1.0 + sc_ref[...]) + sh_ref[...]).astype(BF16)

    rowspec = pl.BlockSpec((1, d), lambda i: (0, 0))
    ins = [x, g, scale, shift] + ([] if after is None else [after])
    return pl.pallas_call(
        body, name=name, out_shape=jax.ShapeDtypeStruct((s, d), BF16), grid=(s // tr,),
        in_specs=[pl.BlockSpec((tr, d), lambda i: (i, 0)), rowspec, rowspec, rowspec] + [ANY] * (len(ins) - 4),
        out_specs=pl.BlockSpec((tr, d), lambda i: (i, 0)),
        compiler_params=_params(("parallel",)),
    )(*ins)


def _norm_bwd(x, dh, dres, g, scale, name):
    s, d = x.shape
    tr = 256

    def body(x_ref, dh_ref, dr_ref, g_ref, sc_ref, dx_ref, a_ref, b_ref):
        @pl.when(pl.program_id(0) == 0)
        def _():
            a_ref[...] = jnp.zeros_like(a_ref)
            b_ref[...] = jnp.zeros_like(b_ref)

        xv = x_ref[...]
        rstd = lax.rsqrt(jnp.mean(xv * xv, axis=-1, keepdims=True) + RMS_EPS)
        xhat = xv * rstd
        dhv = dh_ref[...]
        dxhat = dhv * (g_ref[...] * (1.0 + sc_ref[...]))
        mean_term = jnp.mean(dxhat * xhat, axis=-1, keepdims=True)
        dx_ref[...] = dr_ref[...] + rstd * (dxhat - xhat * mean_term)
        a_ref[...] += jnp.sum(dhv, axis=0, keepdims=True)
        b_ref[...] += jnp.sum(dhv * xhat, axis=0, keepdims=True)

    rowspec = pl.BlockSpec((1, d), lambda i: (0, 0))
    tile = pl.BlockSpec((tr, d), lambda i: (i, 0))
    return pl.pallas_call(
        body, name=name,
        out_shape=[jax.ShapeDtypeStruct((s, d), F32), jax.ShapeDtypeStruct((1, d), F32), jax.ShapeDtypeStruct((1, d), F32)],
        grid=(s // tr,), in_specs=[tile, tile, tile, rowspec, rowspec], out_specs=[tile, rowspec, rowspec],
        compiler_params=_params(("arbitrary",)),
    )(x, dh, dres, g, scale)


def _gate_bwd(dxn, f, colscale, coef, name):
    s, d = dxn.shape
    tr = 256

    def body(dx_ref, f_ref, cs_ref, df_ref, dg_ref):
        @pl.when(pl.program_id(0) == 0)
        def _():
            dg_ref[...] = jnp.zeros_like(dg_ref)

        dxv = dx_ref[...]
        df_ref[...] = (dxv * cs_ref[...]).astype(BF16)
        dg_ref[...] += coef * jnp.sum(dxv * f_ref[...], axis=0, keepdims=True)

    rowspec = pl.BlockSpec((1, d), lambda i: (0, 0))
    tile = pl.BlockSpec((tr, d), lambda i: (i, 0))
    return pl.pallas_call(
        body, name=name, out_shape=[jax.ShapeDtypeStruct((s, d), BF16), jax.ShapeDtypeStruct((1, d), F32)],
        grid=(s // tr,), in_specs=[tile, tile, rowspec], out_specs=[tile, rowspec],
        compiler_params=_params(("arbitrary",)),
    )(dxn, f, colscale)


def _ffn_up(h, wg, wu, name):
    s, d = h.shape
    f = wg.shape[0]
    tm, tn = s, _tile(f, 256)

    def body(h_ref, wg_ref, wu_ref, a_ref, u_ref, s_ref):
        hv = h_ref[...]
        a = _dot(hv, wg_ref[...], 1, 1)
        u = _dot(hv, wu_ref[...], 1, 1)
        a_ref[...] = a.astype(BF16)
        u_ref[...] = u.astype(BF16)
        s_ref[...] = (a * _sigmoid(a) * u).astype(BF16)

    tile = pl.BlockSpec((tm, tn), lambda i, j: (i, j))
    wspec = pl.BlockSpec((tn, d), lambda i, j: (j, 0))
    return pl.pallas_call(
        body, name=name,
        out_shape=[jax.ShapeDtypeStruct((s, f), BF16), jax.ShapeDtypeStruct((s, f), BF16), jax.ShapeDtypeStruct((s, f), BF16)],
        grid=(s // tm, f // tn), in_specs=[pl.BlockSpec((tm, d), lambda i, j: (i, 0)), wspec, wspec],
        out_specs=[tile, tile, tile], compiler_params=_params(("parallel", "parallel")),
    )(h, wg, wu)


def _ffn_bwd_ds(df, wd, a, u, name):
    s, d = df.shape
    f = wd.shape[0]
    tm, tn = 1024, _tile(f, 256)

    def body(df_ref, wd_ref, a_ref, u_ref, da_ref, du_ref):
        ds = _dot(df_ref[...], wd_ref[...], 1, 1)
        av = a_ref[...].astype(F32)
        sg = _sigmoid(av)
        da_ref[...] = (ds * u_ref[...].astype(F32) * (sg * (1.0 + av * (1.0 - sg)))).astype(BF16)
        du_ref[...] = (ds * (av * sg)).astype(BF16)

    tile = pl.BlockSpec((tm, tn), lambda i, j: (i, j))
    return pl.pallas_call(
        body, name=name, out_shape=[jax.ShapeDtypeStruct((s, f), BF16), jax.ShapeDtypeStruct((s, f), BF16)],
        grid=(s // tm, f // tn),
        in_specs=[pl.BlockSpec((tm, d), lambda i, j: (i, 0)), pl.BlockSpec((tn, d), lambda i, j: (j, 0)), tile, tile],
        out_specs=[tile, tile], compiler_params=_params(("parallel", "parallel")),
    )(df, wd, a, u)


def _merge_fwd(o_sb, o_dil, o_swa, gates, wb_sb, wb_dil, wb_swa, name):
    s = o_sb.shape[0]
    d = D_MODEL
    tm = 256

    def body(osb_ref, odl_ref, osw_ref, g_ref, wsb_ref, wdl_ref, wsw_ref, m_ref):
        acc = _sigmoid(g_ref[:, 0:d]) * _dot(osb_ref[...].astype(BF16), wsb_ref[...], 1, 0)
        acc += _sigmoid(g_ref[:, d:2 * d]) * _dot(odl_ref[...].astype(BF16), wdl_ref[...], 1, 0)
        acc += _sigmoid(g_ref[:, 2 * d:3 * d]) * _dot(osw_ref[...].astype(BF16), wsw_ref[...], 1, 0)
        m_ref[...] = acc.astype(BF16)

    def rows(w):
        return pl.BlockSpec((tm, w), lambda i: (i, 0))

    def whole(w):
        return pl.BlockSpec((w, d), lambda i: (0, 0))

    return pl.pallas_call(
        body, name=name, out_shape=jax.ShapeDtypeStruct((s, d), BF16), grid=(s // tm,),
        in_specs=[rows(256), rows(128), rows(384), rows(3 * d), whole(256), whole(128), whole(384)],
        out_specs=rows(d), compiler_params=_params(("parallel",)),
    )(o_sb, o_dil, o_swa, gates, wb_sb, wb_dil, wb_swa)


def _merge_bwd(dmerged, o_sb, o_dil, o_swa, gates, wb_sb, wb_dil, wb_swa, name):
    s = o_sb.shape[0]
    d = D_MODEL
    tm = 256

    def body(dm_ref, osb_ref, odl_ref, osw_ref, g_ref, wsb_ref, wdl_ref, wsw_ref,
             dg_ref, dosb_ref, dodl_ref, dosw_ref, dbsb_ref, dbdl_ref, dbsw_ref):
        dm = dm_ref[...]
        for idx, (o_ref, w_ref, do_ref, db_ref) in enumerate((
                (osb_ref, wsb_ref, dosb_ref, dbsb_ref), (odl_ref, wdl_ref, dodl_ref, dbdl_ref),
                (osw_ref, wsw_ref, dosw_ref, dbsw_ref))):
            w = w_ref[...]
            br = _dot(o_ref[...].astype(BF16), w, 1, 0)
            sg = _sigmoid(g_ref[:, idx * d:(idx + 1) * d])
            dbr = (dm * sg).astype(BF16)
            dg_ref[:, idx * d:(idx + 1) * d] = dm * br * (sg * (1.0 - sg))
            db_ref[...] = dbr
            do_ref[...] = _dot(dbr, w, 1, 1)

    def rows(w):
        return pl.BlockSpec((tm, w), lambda i: (i, 0))

    def whole(w):
        return pl.BlockSpec((w, d), lambda i: (0, 0))

    def shp(w, dt):
        return jax.ShapeDtypeStruct((s, w), dt)

    return pl.pallas_call(
        body, name=name,
        out_shape=[shp(3 * d, F32), shp(256, F32), shp(128, F32), shp(384, F32), shp(d, BF16), shp(d, BF16), shp(d, BF16)],
        grid=(s // tm,),
        in_specs=[rows(d), rows(256), rows(128), rows(384), rows(3 * d), whole(256), whole(128), whole(384)],
        out_specs=[rows(3 * d), rows(256), rows(128), rows(384), rows(d), rows(d), rows(d)],
        compiler_params=_params(("parallel",)),
    )(dmerged, o_sb, o_dil, o_swa, gates, wb_sb, wb_dil, wb_swa)


def _final_loss(x, target, g, name):
    s, d = x.shape
    tr = 256

    def body(x_ref, t_ref, g_ref, loss_ref, dx_ref, dg_ref):
        @pl.when(pl.program_id(0) == 0)
        def _():
            loss_ref[...] = jnp.zeros_like(loss_ref)
            dg_ref[...] = jnp.zeros_like(dg_ref)

        xv = x_ref[...]
        gv = g_ref[...]
        rstd = lax.rsqrt(jnp.mean(xv * xv, axis=-1, keepdims=True) + RMS_EPS)
        xhat = xv * rstd
        err = xhat * gv - t_ref[...]
        loss_ref[...] += 0.5 * jnp.sum(jnp.mean(err * err, axis=-1, keepdims=True))
        dy = err * (1.0 / d)
        dxhat = dy * gv
        mean_term = jnp.mean(dxhat * xhat, axis=-1, keepdims=True)
        dx_ref[...] = rstd * (dxhat - xhat * mean_term)
        dg_ref[...] += jnp.sum(dy * xhat, axis=0, keepdims=True)

    rowspec = pl.BlockSpec((1, d), lambda i: (0, 0))
    tile = pl.BlockSpec((tr, d), lambda i: (i, 0))
    return pl.pallas_call(
        body, name=name,
        out_shape=[jax.ShapeDtypeStruct((1, LANES), F32), jax.ShapeDtypeStruct((s, d), F32), jax.ShapeDtypeStruct((1, d), F32)],
        grid=(s // tr,), in_specs=[tile, tile, rowspec],
        out_specs=[pl.BlockSpec((1, LANES), lambda i: (0, 0)), tile, rowspec],
        compiler_params=_params(("arbitrary",)),
    )(x, target, g)


def _adamw(w, g, m, v, name):
    shape = w.shape
    cols = shape[-1]
    rows = int(np.prod(shape[:-1])) if len(shape) > 1 else 1
    tr = rows
    for cand in (1024, 512, 256, 128, 64, 32, 16, 8):
        if rows % cand == 0 and rows > cand and cand * cols * 4 <= (1 << 21):
            tr = cand
            break

    def body(w_ref, g_ref, m_ref, v_ref, d_ref, nm_ref, nv_ref):
        gv = g_ref[...]
        nm = ADAM_B1 * m_ref[...] + (1.0 - ADAM_B1) * gv
        nv = ADAM_B2 * v_ref[...] + (1.0 - ADAM_B2) * (gv * gv)
        m_hat = nm / (1.0 - ADAM_B1 ** ADAM_STEP)
        v_hat = nv / (1.0 - ADAM_B2 ** ADAM_STEP)
        d_ref[...] = -ADAM_LR * (m_hat / (jnp.sqrt(v_hat) + ADAM_EPS) + ADAM_WD * w_ref[...])
        nm_ref[...] = nm
        nv_ref[...] = nv

    tile = pl.BlockSpec((tr, cols), lambda i: (i, 0))
    flat = [t.reshape(rows, cols) for t in (w, g, m, v)]
    out = pl.pallas_call(
        body, name=name, out_shape=[jax.ShapeDtypeStruct((rows, cols), F32)] * 3, grid=(rows // tr,),
        in_specs=[tile] * 4, out_specs=[tile] * 3, compiler_params=_params(("parallel",)),
    )(*flat)
    return tuple(t.reshape(shape) for t in out)


def _ada_fwd(c_all, w, name):
    n = w.shape[1]

    def body(c_ref, w_ref, o_ref):
        cv = c_ref[...]
        o_ref[...] = jnp.dot(cv * _sigmoid(cv), w_ref[...], preferred_element_type=F32, precision=lax.Precision.HIGHEST)

    return pl.pallas_call(body, name=name, out_shape=jax.ShapeDtypeStruct((N_DEV, n), F32), compiler_params=_params())(c_all, w)


def _ada_bwd(c_all_t, dmod, name):
    n = dmod.shape[1]

    def body(c_ref, d_ref, o_ref):
        cv = c_ref[...]
        o_ref[...] = jnp.dot(cv * _sigmoid(cv), d_ref[...], preferred_element_type=F32, precision=lax.Precision.HIGHEST)

    return pl.pallas_call(body, name=name, out_shape=jax.ShapeDtypeStruct((D_MODEL, n), F32), compiler_params=_params())(c_all_t, dmod)


def _bucket_tables():
    rel = np.arange(BLK)[:, None] + BLK - np.arange(2 * BLK)[None, :]
    max_exact = N_BUCKETS // 2

    def bucket(n):
        nf = np.maximum(n, 1).astype(np.float32)
        large = max_exact + (np.log(nf / np.float32(max_exact)) / np.float32(math.log(MAX_REL_DIST / max_exact))
                             * np.float32(N_BUCKETS - max_exact)).astype(np.int32)
        return np.where(n < max_exact, n, np.minimum(large, N_BUCKETS - 1))

    tabs = []
    for dil, max_dist in ((1, 128), (4, 128), (16, 128), (1, SWA_WINDOW - 1)):
        in_band = (rel >= 0) & (rel <= max_dist)
        tabs.append(np.where(in_band, bucket(np.maximum(rel, 0) * dil), -1))
    return np.stack(tabs).astype(np.int32)


N_SOFT = H_DIL + H_SWA_Q


def _table_of_head(h):
    return jnp.minimum(h // 2, 3)


def _bias_build(rel_bias, tables, name):
    def body(rel_ref, t_ref, o_ref):
        h = pl.program_id(0)
        tb = t_ref[0]
        out = jnp.full((BLK, 2 * BLK), NEG, F32)
        for b in range(N_BUCKETS):
            out = jnp.where(tb == b, rel_ref[b, h], out)
        o_ref[0] = out

    return pl.pallas_call(
        body, name=name, out_shape=jax.ShapeDtypeStruct((N_SOFT, BLK, 2 * BLK), F32), grid=(N_SOFT,),
        in_specs=[pl.BlockSpec(memory_space=pltpu.SMEM),
                  pl.BlockSpec((1, BLK, 2 * BLK), lambda h: (_table_of_head(h), 0, 0))],
        out_specs=pl.BlockSpec((1, BLK, 2 * BLK), lambda h: (h, 0, 0)),
        compiler_params=_params(("parallel",)),
    )(rel_bias, tables)


def _bias_grad(dbias, tables, name):
    def body(d_ref, t_ref, o_ref):
        tb = t_ref[0]
        dv = d_ref[0]
        lane = lax.broadcasted_iota(jnp.int32, (1, LANES), 1)
        out = jnp.zeros((1, LANES), F32)
        for b in range(N_BUCKETS):
            out = jnp.where(lane == b, jnp.sum(jnp.where(tb == b, dv, 0.0)), out)
        o_ref[0] = out

    return pl.pallas_call(
        body, name=name, out_shape=jax.ShapeDtypeStruct((N_SOFT, 1, LANES), F32), grid=(N_SOFT,),
        in_specs=[pl.BlockSpec((1, BLK, 2 * BLK), lambda h: (h, 0, 0)),
                  pl.BlockSpec((1, BLK, 2 * BLK), lambda h: (_table_of_head(h), 0, 0))],
        out_specs=pl.BlockSpec((1, 1, LANES), lambda h: (h, 0, 0)),
        compiler_params=_params(("parallel",)),
    )(dbias, tables)


def _band_specs(g, bias_div):
    qspec = pl.BlockSpec((1, BLK, HEAD_DIM), lambda n, i: (n, i, 0))
    prev = pl.BlockSpec((1, BLK, HEAD_DIM), lambda n, i: (n // g, jnp.maximum(i - 1, 0), 0))
    cur = pl.BlockSpec((1, BLK, HEAD_DIM), lambda n, i: (n // g, i, 0))
    bspec = pl.BlockSpec((1, BLK, 2 * BLK), lambda n, i: (n // bias_div, 0, 0))
    sspec = pl.BlockSpec((1, 1, LANES), lambda n, i: (n, 0, 0))
    colspec = pl.BlockSpec((1, BLK, 1), lambda n, i: (n, i, 0))
    return qspec, prev, cur, bspec, sspec, colspec


def _band_scores(q_ref, kp_ref, kc_ref, b_ref, first):
    qv = q_ref[0]
    bv = b_ref[0]
    sp = _dot(qv, kp_ref[0], 1, 1) + bv[:, :BLK]
    sp = jnp.where(first, NEG, sp)
    sc = _dot(qv, kc_ref[0], 1, 1) + bv[:, BLK:]
    return sp, sc


def _band_fwd(q, k, v, bias, sink, *, g, bias_div, has_sink, name):
    nq, length, _ = q.shape

    def body(q_ref, kp_ref, kc_ref, vp_ref, vc_ref, b_ref, s_ref, o_ref, lse_ref):
        sp, sc = _band_scores(q_ref, kp_ref, kc_ref, b_ref, pl.program_id(1) == 0)
        m = jnp.maximum(jnp.max(sp, axis=1, keepdims=True), jnp.max(sc, axis=1, keepdims=True))
        if has_sink:
            sk = s_ref[0][:, :1]
            m = jnp.maximum(m, sk)
        pp, pc = jnp.exp(sp - m), jnp.exp(sc - m)
        den = jnp.sum(pp, axis=1, keepdims=True) + jnp.sum(pc, axis=1, keepdims=True)
        if has_sink:
            den = den + jnp.exp(sk - m)
        acc = _dot(pp.astype(BF16), vp_ref[0], 1, 0) + _dot(pc.astype(BF16), vc_ref[0], 1, 0)
        o_ref[0] = acc / den
        lse_ref[0] = m + jnp.log(den)

    qspec, prev, cur, bspec, sspec, colspec = _band_specs(g, bias_div)
    return pl.pallas_call(
        body, name=name,
        out_shape=[jax.ShapeDtypeStruct((nq, length, HEAD_DIM), F32), jax.ShapeDtypeStruct((nq, length, 1), F32)],
        grid=(nq, length // BLK), in_specs=[qspec, prev, cur, prev, cur, bspec, sspec],
        out_specs=[qspec, colspec], compiler_params=_params(("parallel", "parallel")),
    )(q, k, k, v, v, bias, sink)


def _band_bwd(q, k, v, bias, sink, o, lse, do, dlse, *, g, bias_div, has_sink, name):
    nq, length, _ = q.shape
    nk, nbias = nq // g, nq // bias_div

    def body(q_ref, kp_ref, kc_ref, vp_ref, vc_ref, b_ref, s_ref, o_ref, lse_ref, do_ref, dlse_ref,
             dq_ref, dk_ref, dv_ref, db_ref, dsk_ref):
        n, i = pl.program_id(0), pl.program_id(1)

        @pl.when((n % g == 0) & (i == 0))
        def _():
            dk_ref[...] = jnp.zeros_like(dk_ref)
            dv_ref[...] = jnp.zeros_like(dv_ref)

        @pl.when((n % bias_div == 0) & (i == 0))
        def _():
            db_ref[...] = jnp.zeros_like(db_ref)

        @pl.when(i == 0)
        def _():
            dsk_ref[...] = jnp.zeros_like(dsk_ref)

        sp, sc = _band_scores(q_ref, kp_ref, kc_ref, b_ref, i == 0)
        lse_v = lse_ref[0]
        pp, pc = jnp.exp(sp - lse_v), jnp.exp(sc - lse_v)
        dov = do_ref[0]
        dob = dov.astype(BF16)
        coef = dlse_ref[0] - jnp.sum(dov * o_ref[0], axis=1, keepdims=True)
        dsp = pp * (_dot(dob, vp_ref[0], 1, 1) + coef)
        dsc = pc * (_dot(dob, vc_ref[0], 1, 1) + coef)
        dspb, dscb = dsp.astype(BF16), dsc.astype(BF16)
        dq_ref[0] = (_dot(dspb, kp_ref[0], 1, 0) + _dot(dscb, kc_ref[0], 1, 0)) * (HEAD_DIM ** -0.5)
        qv = q_ref[0]
        cur = pl.ds(pl.multiple_of(i * BLK, BLK), BLK)
        prv = pl.ds(pl.multiple_of(jnp.maximum(i - 1, 0) * BLK, BLK), BLK)
        dk_ref[0, cur, :] += _dot(dscb, qv, 0, 0)
        dk_ref[0, prv, :] += _dot(dspb, qv, 0, 0)
        dv_ref[0, cur, :] += _dot(pc.astype(BF16), dob, 0, 0)
        dv_ref[0, prv, :] += _dot(pp.astype(BF16), dob, 0, 0)
        db_ref[0, :, :BLK] += dsp
        db_ref[0, :, BLK:] += dsc
        if has_sink:
            dsk_ref[0] += jnp.sum(jnp.exp(s_ref[0][:, :1] - lse_v) * coef)

    qspec, prev, cur, bspec, sspec, colspec = _band_specs(g, bias_div)
    kvfull = pl.BlockSpec((1, length, HEAD_DIM), lambda n, i: (n // g, 0, 0))
    return pl.pallas_call(
        body, name=name,
        out_shape=[jax.ShapeDtypeStruct((nq, length, HEAD_DIM), F32), jax.ShapeDtypeStruct((nk, length, HEAD_DIM), F32),
                   jax.ShapeDtypeStruct((nk, length, HEAD_DIM), F32), jax.ShapeDtypeStruct((nbias, BLK, 2 * BLK), F32),
                   jax.ShapeDtypeStruct((nq, 1, LANES), F32)],
        grid=(nq, length // BLK),
        in_specs=[qspec, prev, cur, prev, cur, bspec, sspec, qspec, colspec, qspec, colspec],
        out_specs=[qspec, kvfull, kvfull, bspec, sspec], compiler_params=_params(("arbitrary", "arbitrary")),
    )(q, k, k, v, v, bias, sink, o, lse, do, dlse)


def _dil_merge(os_, lses, dout, name):
    tr = 512
    n = len(os_)
    tile = pl.BlockSpec((1, tr, HEAD_DIM), lambda h, i: (h, i, 0))
    col = pl.BlockSpec((1, tr, 1), lambda h, i: (h, i, 0))

    def weights(l_refs):
        ls = [r[0] for r in l_refs]
        m = ls[0]
        for lv in ls[1:]:
            m = jnp.maximum(m, lv)
        es = [jnp.exp(lv - m) for lv in ls]
        den = es[0]
        for e in es[1:]:
            den = den + e
        return [e / den for e in es]

    if dout is None:
        def body(*refs):
            alphas = weights(refs[n:2 * n])
            acc = alphas[0] * refs[0][0]
            for gi in range(1, n):
                acc = acc + alphas[gi] * refs[gi][0]
            refs[2 * n][0] = acc

        return pl.pallas_call(
            body, name=name, out_shape=jax.ShapeDtypeStruct(os_[0].shape, F32), grid=(2, SEQ // tr),
            in_specs=[tile] * n + [col] * n, out_specs=tile, compiler_params=_params(("parallel", "parallel")),
        )(*os_, *lses)

    def body(*refs):
        alphas = weights(refs[n:2 * n])
        dov = refs[2 * n][0]
        outs = refs[2 * n + 1:]
        das = [jnp.sum(dov * refs[gi][0], axis=1, keepdims=True) for gi in range(n)]
        dbar = alphas[0] * das[0]
        for gi in range(1, n):
            dbar = dbar + alphas[gi] * das[gi]
        for gi in range(n):
            outs[gi][0] = alphas[gi] * dov
            outs[n + gi][0] = alphas[gi] * (das[gi] - dbar)

    return pl.pallas_call(
        body, name=name,
        out_shape=[jax.ShapeDtypeStruct(os_[0].shape, F32)] * n + [jax.ShapeDtypeStruct(lses[0].shape, F32)] * n,
        grid=(2, SEQ // tr), in_specs=[tile] * n + [col] * n + [tile], out_specs=[tile] * n + [col] * n,
        compiler_params=_params(("parallel", "parallel")),
    )(*os_, *lses, dout)


def _tri(cmp):
    r = lax.broadcasted_iota(jnp.int32, (SB_TILE, SB_TILE), 0)
    c = lax.broadcasted_iota(jnp.int32, (SB_TILE, SB_TILE), 1)
    return cmp(r, c).astype(BF16)


def _cum(x, tri, terms):
    acc, rest = None, x
    for _ in range(terms):
        part = rest.astype(BF16)
        rest = rest - part.astype(F32)
        d = _dot(part, tri, 1, 0)
        acc = d if acc is None else acc + d
    return acc


def _sb_logits(q, k_ref, j, i):
    t = SB_TILE
    ks = k_ref[0, pl.ds(pl.multiple_of(j * t, t), t), :]
    z = _dot(q, ks, 1, 1)
    rows = i * t + lax.broadcasted_iota(jnp.int32, (t, t), 0)
    cols = j * t + lax.broadcasted_iota(jnp.int32, (t, t), 1)
    mask = cols < rows
    e = jnp.exp(-jnp.abs(z))
    lf = jnp.where(mask, -(jnp.maximum(z, 0.0) + jnp.log(1.0 + e)), 0.0)
    return ks, z, e, lf, mask


def _sb_fwd(q, k, v, name):
    h, s, _ = q.shape
    t = SB_TILE

    def body(q_ref, k_ref, v_ref, o_ref, tot_ref):
        i = pl.program_id(1)
        qv = q_ref[0]
        after = _tri(lambda r, c: r > c)

        def step(jj, carry):
            right, acc = carry
            j = i - jj
            _, z, _, lf, mask = _sb_logits(qv, k_ref, j, i)
            between = right + _cum(lf, after, 3)
            w = jnp.where(mask, jnp.exp(z + lf + between), 0.0)
            vs = v_ref[0, pl.ds(pl.multiple_of(j * t, t), t), :]
            return right + jnp.sum(lf, axis=1, keepdims=True), acc + _dot(w.astype(BF16), vs, 1, 0)

        right, acc = lax.fori_loop(0, i + 1, step, (jnp.zeros((t, 1), F32), jnp.zeros((t, HEAD_DIM), F32)))
        o_ref[0] = acc
        tot_ref[0] = right

    tile = pl.BlockSpec((1, t, HEAD_DIM), lambda hh, i: (hh, i, 0))
    full = pl.BlockSpec((1, s, HEAD_DIM), lambda hh, i: (hh, 0, 0))
    return pl.pallas_call(
        body, name=name, out_shape=[jax.ShapeDtypeStruct((h, s, HEAD_DIM), F32), jax.ShapeDtypeStruct((h, s, 1), F32)],
        grid=(h, s // t), in_specs=[tile, full, full],
        out_specs=[tile, pl.BlockSpec((1, t, 1), lambda hh, i: (hh, i, 0))],
        compiler_params=_params(("parallel", "parallel")),
    )(q, k, v)


def _sb_bwd(q, k, v, tot, do, name):
    h, s, _ = q.shape
    t = SB_TILE

    def body(q_ref, k_ref, v_ref, tot_ref, do_ref, dq_ref, dk_ref, dv_ref):
        i = pl.program_id(1)

        @pl.when(i == 0)
        def _():
            dk_ref[...] = jnp.zeros_like(dk_ref)
            dv_ref[...] = jnp.zeros_like(dv_ref)

        qv = q_ref[0]
        dob = do_ref[0].astype(BF16)
        total = tot_ref[0]
        upto = _tri(lambda r, c: r <= c)
        before = _tri(lambda r, c: r < c)

        def step(j, carry):
            left, cleft, dq = carry
            ks, z, e, lf, mask = _sb_logits(qv, k_ref, j, i)
            rows = pl.ds(pl.multiple_of(j * t, t), t)
            vs = v_ref[0, rows, :]
            between = total - (left + _cum(lf, upto, 3))
            w = jnp.where(mask, jnp.exp(z + lf + between), 0.0)
            dlog = w * _dot(dob, vs, 1, 1)
            cfail = cleft + _cum(dlog, before, 2)
            sig = jnp.where(z >= 0.0, 1.0, e) / (1.0 + e)
            dz = jnp.where(mask, dlog * (1.0 - sig) - sig * cfail, 0.0).astype(BF16)
            dk_ref[0, rows, :] += _dot(dz, qv, 0, 0)
            dv_ref[0, rows, :] += _dot(w.astype(BF16), dob, 0, 0)
            return (left + jnp.sum(lf, axis=1, keepdims=True), cleft + jnp.sum(dlog, axis=1, keepdims=True),
                    dq + _dot(dz, ks, 1, 0))

        zero = jnp.zeros((t, 1), F32)
        _, _, dq = lax.fori_loop(0, i + 1, step, (zero, zero, jnp.zeros((t, HEAD_DIM), F32)))
        dq_ref[0] = dq * (HEAD_DIM ** -0.5)

    tile = pl.BlockSpec((1, t, HEAD_DIM), lambda hh, i: (hh, i, 0))
    full = pl.BlockSpec((1, s, HEAD_DIM), lambda hh, i: (hh, 0, 0))
    shp = jax.ShapeDtypeStruct((h, s, HEAD_DIM), F32)
    return pl.pallas_call(
        body, name=name, out_shape=[shp, shp, shp], grid=(h, s // t),
        in_specs=[tile, full, full, pl.BlockSpec((1, t, 1), lambda hh, i: (hh, i, 0)), tile],
        out_specs=[tile, full, full], compiler_params=_params(("arbitrary", "arbitrary")),
    )(q, k, v, tot, do)


def _heads(t):
    return t.reshape(SEQ, -1, HEAD_DIM).transpose(1, 0, 2)


def _unheads(t):
    return t.transpose(1, 0, 2).reshape(SEQ, -1)


def _to_dil(t, d):
    xdim = t.shape[-1]
    return t.reshape(2, SEQ // d, d, xdim).transpose(0, 2, 1, 3).reshape(2 * d, SEQ // d, xdim)


def _from_dil(t, d):
    xdim = t.shape[-1]
    return t.reshape(2, d, SEQ // d, xdim).transpose(0, 2, 1, 3).reshape(2, SEQ, xdim)


def _split_qkv(qkv):
    parts, off = [], 0
    for w in QKV_SPLITS:
        parts.append(qkv[:, off:off + w])
        off += w
    return parts


def _mixer_fwd(qkv, bias, sinks_l, tag):
    scale = HEAD_DIM ** -0.5
    q_sb, k_sb, v_sb, q_dl, k_dl, v_dl, q_sw, k_sw, v_sw = _split_qkv(qkv)
    hq = lambda t: _heads((t * scale).astype(BF16))
    hk = lambda t: _heads(t.astype(BF16))
    st = {}
    st["sb"] = (hq(q_sb), hk(k_sb), hk(v_sb))
    o_sb, st["sb_tot"] = _sb_fwd(*st["sb"], name=f"sb_fwd_{tag}")

    qd, kd, vd = hq(q_dl), hk(k_dl), hk(v_dl)
    no_sink = jnp.zeros((1, 1, LANES), F32)
    st["dil"], outs, lses = [], [], []
    for gi, (_, d) in enumerate(DIL_PATTERNS):
        hs = slice(2 * gi, 2 * gi + 2)
        qg, kg, vg = _to_dil(qd[hs], d), _to_dil(kd[hs], d), _to_dil(vd[hs], d)
        sink = jnp.broadcast_to(no_sink, (2 * d, 1, LANES))
        og, lg = _band_fwd(qg, kg, vg, bias[hs], sink, g=1, bias_div=d, has_sink=False, name=f"dil{gi}_fwd_{tag}")
        st["dil"].append((qg, kg, vg, sink, og, lg))
        outs.append(_from_dil(og, d))
        lses.append(_from_dil(lg, d))
    st["dil_outs"], st["dil_lses"] = outs, lses
    o_dil = _dil_merge(outs, lses, None, name=f"dil_merge_fwd_{tag}")

    sink = jnp.broadcast_to(sinks_l.reshape(H_SWA_Q, 1, 1), (H_SWA_Q, 1, LANES))
    st["swa"] = (hq(q_sw), hk(k_sw), hk(v_sw), sink)
    o_sw, l_sw = _band_fwd(*st["swa"][:3], bias[H_DIL:], sink, g=H_SWA_Q // H_SWA_KV, bias_div=1, has_sink=True,
                           name=f"swa_fwd_{tag}")
    st["swa_out"] = (o_sw, l_sw)
    return (_unheads(o_sb), _unheads(o_dil), _unheads(o_sw)), st


def _mixer_bwd(st, bias, do_sb, do_dil, do_swa, tag):
    dq_sb, dk_sb, dv_sb = _sb_bwd(*st["sb"], st["sb_tot"], _heads(do_sb), name=f"sb_bwd_{tag}")

    dmerge = _dil_merge(st["dil_outs"], st["dil_lses"], _heads(do_dil), name=f"dil_merge_bwd_{tag}")
    dqs, dks, dvs, dbs = [], [], [], []
    for gi, (_, d) in enumerate(DIL_PATTERNS):
        qg, kg, vg, sink, og, lg = st["dil"][gi]
        hs = slice(2 * gi, 2 * gi + 2)
        dq, dk, dv, db, _ = _band_bwd(qg, kg, vg, bias[hs], sink, og, lg, _to_dil(dmerge[gi], d), _to_dil(dmerge[3 + gi], d),
                                      g=1, bias_div=d, has_sink=False, name=f"dil{gi}_bwd_{tag}")
        dqs.append(_from_dil(dq, d))
        dks.append(_from_dil(dk, d))
        dvs.append(_from_dil(dv, d))
        dbs.append(db)

    q_sw, k_sw, v_sw, sink = st["swa"]
    o_sw, l_sw = st["swa_out"]
    dq_sw, dk_sw, dv_sw, db_sw, dsink = _band_bwd(q_sw, k_sw, v_sw, bias[H_DIL:], sink, o_sw, l_sw, _heads(do_swa),
                                                  jnp.zeros_like(l_sw), g=H_SWA_Q // H_SWA_KV, bias_div=1, has_sink=True,
                                                  name=f"swa_bwd_{tag}")
    dqkv = jnp.concatenate(
        [_unheads(dq_sb), _unheads(dk_sb), _unheads(dv_sb),
         _unheads(jnp.concatenate(dqs, 0)), _unheads(jnp.concatenate(dks, 0)), _unheads(jnp.concatenate(dvs, 0)),
         _unheads(dq_sw), _unheads(dk_sw), _unheads(dv_sw)], axis=1)
    return dqkv, jnp.concatenate(dbs + [db_sw], 0), dsink[:, 0, 0]


PIECES = ("ffn0", "mix", "ffn1")


def _ffn_fwd(x_in, w, gain, mod_j, tag, after=None):
    st = {"x": x_in, "w": w}
    st["h"] = _norm_fwd(x_in, _row(gain), _row(mod_j[1]), _row(mod_j[0]), name=f"norm_fwd_{tag}", after=after)
    st["a"], st["u"], st["s"] = _ffn_up(st["h"], w["gate"], w["up"], name=f"up_{tag}")
    st["f"], x_out = _mm(st["s"], w["down"], res=x_in, colscale=_row(0.5 * mod_j[2]), emit_acc=True, tm=256, tn=1024,
                         name=f"down_{tag}")
    return x_out, st


def _ffn_bwd(dx_out, st, gain, mod_j, tag, done):
    w = st["w"]

    def latest(new, old):
        return old if new is None else new

    df, dgate = _gate_bwd(dx_out, st["f"], _row(0.5 * mod_j[2]), 0.5, name=f"gate_bwd_{tag}")
    token = done({"down": _mm_tn(st["s"], df, name=f"dwd_{tag}")})
    da, du = _ffn_bwd_ds(df, w["down"], st["a"], st["u"], name=f"ds_{tag}")
    token = latest(done({"gate": _mm_tn(da, st["h"], after=token, name=f"dwg_{tag}")}), token)
    token = latest(done({"up": _mm_tn(du, st["h"], after=token, name=f"dwu_{tag}")}), token)
    dh = _mm2(da, w["gate"], du, w["up"], after=token, name=f"dh_{tag}")
    dx_in, sum_dh, sum_dhx = _norm_bwd(st["x"], dh, dx_out, _row(gain), _row(mod_j[1]), name=f"norm_bwd_{tag}")
    dmod = jnp.concatenate([sum_dh, gain * sum_dhx, dgate], 0)
    return dx_in, dmod, (1.0 + mod_j[1]) * sum_dhx[0]


def _mix_fwd(x_in, w, gain, mod_j, bias, sinks_l, tag, after=None):
    st = {"x": x_in, "w": w}
    st["h"] = _norm_fwd(x_in, _row(gain), _row(mod_j[1]), _row(mod_j[0]), name=f"norm_fwd_mix_{tag}", after=after)
    qkv = _mm(st["h"], w["qkv"], tb=True, name=f"qkv_{tag}")
    st["gates"] = _mm(st["h"], w["gates"], tb=True, name=f"gates_{tag}")
    st["o"], st["mix"] = _mixer_fwd(qkv, bias, sinks_l, tag)
    st["merged"] = _merge_fwd(*st["o"], st["gates"], w["br_sb"], w["br_dil"], w["br_swa"], name=f"merge_fwd_{tag}")
    st["f"], x_out = _mm(st["merged"], w["out"], res=x_in, colscale=_row(mod_j[2]), emit_acc=True, name=f"out_{tag}")
    return x_out, st


def _mix_bwd(dx_out, st, gain, mod_j, bias, tag, done):
    w = st["w"]
    df, dgate = _gate_bwd(dx_out, st["f"], _row(mod_j[2]), 1.0, name=f"gate_bwd_mix_{tag}")
    g = {"out": _mm_tn(st["merged"], df, name=f"dw_out_{tag}")}
    dmerged = _mm(df, w["out"], tb=True, name=f"dmerged_{tag}")
    dgates, do_sb, do_dil, do_swa, dbr_sb, dbr_dil, dbr_swa = _merge_bwd(
        dmerged, *st["o"], st["gates"], w["br_sb"], w["br_dil"], w["br_swa"], name=f"merge_bwd_{tag}")
    g["br_sb"] = _mm_tn(st["o"][0], dbr_sb, name=f"dw_br_sb_{tag}")
    g["br_dil"] = _mm_tn(st["o"][1], dbr_dil, name=f"dw_br_dil_{tag}")
    g["br_swa"] = _mm_tn(st["o"][2], dbr_swa, name=f"dw_br_swa_{tag}")
    dqkv, dbias, dsinks = _mixer_bwd(st["mix"], bias, do_sb, do_dil, do_swa, tag)
    g["qkv"] = _mm_tn(dqkv, st["h"], name=f"dw_qkv_{tag}")
    g["gates"] = _mm_tn(dgates, st["h"], name=f"dw_gates_{tag}")
    dh = _mm2(dqkv, w["qkv"], dgates, w["gates"], after=done(g), tm=128, name=f"dh_mix_{tag}")
    dx_in, sum_dh, sum_dhx = _norm_bwd(st["x"], dh, dx_out, _row(gain), _row(mod_j[1]), name=f"norm_bwd_mix_{tag}")
    dmod = jnp.concatenate([sum_dh, gain * sum_dhx, dgate], 0)
    return dx_in, dmod, (1.0 + mod_j[1]) * sum_dhx[0], dbias, dsinks


def _local_step(x, target, mod, gains, weights_of, rel_bias, sinks, final_gain, grads_done):
    tables = jnp.asarray(_bucket_tables())
    bias = _bias_build(rel_bias, tables, name="bias_build")
    states, h = [], x
    for l in range(DEPTH):
        st = {}
        for j, piece in enumerate(PIECES):
            w, after = weights_of(l, piece, h)
            if piece == "mix":
                h, st[piece] = _mix_fwd(h, w, gains[l, j], mod[l, j], bias, sinks[l], f"l{l}", after)
            else:
                h, st[piece] = _ffn_fwd(h, w, gains[l, j], mod[l, j], f"{piece}_l{l}", after)
        states.append(st)
    loss, dx, dfinal = _final_loss(h, target, _row(final_gain), name="final_loss")
    dmods = [[None] * 3 for _ in range(DEPTH)]
    dgains = [[None] * 3 for _ in range(DEPTH)]
    dsinks = [None] * DEPTH
    dbias = None
    for l in reversed(range(DEPTH)):
        for j in reversed(range(3)):
            piece = PIECES[j]
            done = lambda grads, l=l, piece=piece: grads_done(l, piece, grads)
            if piece == "mix":
                dx, dmods[l][j], dgains[l][j], db, dsinks[l] = _mix_bwd(dx, states[l][piece], gains[l, j], mod[l, j], bias, f"l{l}", done)
                dbias = db if dbias is None else dbias + db
            else:
                dx, dmods[l][j], dgains[l][j] = _ffn_bwd(dx, states[l][piece], gains[l, j], mod[l, j], f"{piece}_l{l}", done)
    drel = _bias_grad(dbias, tables, name="bias_grad")[:, 0, :N_BUCKETS].T
    dmod = jnp.stack([jnp.stack(m) for m in dmods])
    dgain = jnp.stack([jnp.stack(g) for g in dgains])
    return loss, dx, dmod, dgain, dfinal[0], drel, jnp.stack(dsinks)


BR_ROWS = (H_SB * HEAD_DIM, 2 * HEAD_DIM, H_SWA_Q * HEAD_DIM)


def _lanes_unshard(g, lead):
    _, rows, _ = g.shape
    r = rows // lead
    return g.reshape(N_DEV, lead, r, LANES).transpose(1, 2, 0, 3).reshape(lead, r, N_DEV * LANES)


def _lanes_shard(full):
    lead, r, _ = full.shape
    return full.reshape(lead, r, N_DEV, LANES).transpose(2, 0, 1, 3).reshape(N_DEV, lead * r, LANES)


def _pack_rows(parts, dtype):
    flat = jnp.concatenate([p.astype(dtype).reshape(-1) for p in parts])
    pad = (-flat.shape[0]) % (16 * LANES)
    if pad:
        flat = jnp.concatenate([flat, jnp.zeros((pad,), dtype)])
    return flat.reshape(-1, LANES)


def _unshard(gathered, axis):
    moved = jnp.moveaxis(gathered, 0, axis)
    shape = list(moved.shape)
    shape[axis:axis + 2] = [shape[axis] * shape[axis + 1]]
    return moved.reshape(shape)


def kernel(x, c, w_ada, b_ada, norm_gain, w_ffn_gate, w_ffn_up, w_ffn_down, w_in, w_br_sb, w_br_dil, w_br_swa, w_out, sinks, rel_bias, final_gain, loss_target, m_w_ada, m_b_ada, m_norm_gain, m_w_ffn_gate, m_w_ffn_up, m_w_ffn_down, m_w_in, m_w_br_sb, m_w_br_dil, m_w_br_swa, m_w_out, m_sinks, m_rel_bias, m_final_gain, v_w_ada, v_b_ada, v_norm_gain, v_w_ffn_gate, v_w_ffn_up, v_w_ffn_down, v_w_in, v_w_br_sb, v_w_br_dil, v_w_br_swa, v_w_out, v_sinks, v_rel_bias, v_final_gain):
    me = 4 * lax.axis_index("x") + 2 * lax.axis_index("y") + lax.axis_index("c")
    d = D_MODEL
    gate_t, up_t, in_t = jnp.swapaxes(w_ffn_gate, 2, 3), jnp.swapaxes(w_ffn_up, 2, 3), jnp.swapaxes(w_in, 1, 2)

    def piece_shards(l, piece):
        bf = lambda t: t.astype(BF16)
        if piece == "mix":
            return [bf(in_t[l]), jnp.concatenate([bf(w_br_sb[l]), bf(w_br_dil[l]), bf(w_br_swa[l])], 0), bf(w_out[l])]
        i = PIECES.index(piece) // 2
        return [bf(gate_t[l, i]), bf(up_t[l, i]), bf(w_ffn_down[l, i])]

    br_off = np.concatenate([[0], np.cumsum(BR_ROWS)])

    def piece_weights(gathered, piece):
        if piece == "mix":
            g_in, g_br, g_out = gathered
            f_in = g_in.reshape(D_QKV + D_GATES, d)
            f_br = [_lanes_unshard(g_br[:, br_off[k]:br_off[k + 1]], 1)[0] for k in range(3)]
            return {"qkv": f_in[:D_QKV], "gates": f_in[D_QKV:], "br_sb": f_br[0], "br_dil": f_br[1], "br_swa": f_br[2],
                    "out": g_out.reshape(d, d)}
        return {n: g.reshape(D_FF, d) for n, g in zip(("gate", "up", "down"), gathered)}

    small, = _all_gather([_pack_rows([c, norm_gain], F32)], name="gather_cond")
    c_all = small[:, :d // LANES].reshape(N_DEV, d)
    gains = _unshard(small[:, d // LANES:d // LANES + 6].reshape(N_DEV, DEPTH, 3, LANES), 2)

    cols = w_ada.shape[2]
    mod_cols = jnp.stack([_ada_fwd(c_all, w_ada[l], name=f"ada_fwd_l{l}") for l in range(DEPTH)])
    mod_all, = _all_gather([_pack_rows([mod_cols], F32)], name="gather_mod")
    mod_all = mod_all.reshape(N_DEV, -1)[:, :DEPTH * N_DEV * cols].reshape(N_DEV, DEPTH, N_DEV, cols)
    mod_mine = lax.dynamic_index_in_dim(mod_all, me, axis=2, keepdims=False)
    mod = (mod_mine.transpose(1, 0, 2).reshape(DEPTH, N_DEV * cols) + b_ada).reshape(DEPTH, 3, 3, d)

    order = [(l, piece) for l in range(DEPTH) for piece in PIECES]
    eager, ahead = 2, 3
    in_flight = {}
    n_tensors = 3
    first = _all_gather([s for k in range(eager) for s in piece_shards(*order[k])], after=mod_all, name="gather_first")

    def start_gather(k, after):
        l, piece = order[k]
        in_flight[k], token = _exchange_start(piece_shards(l, piece), after, gather=True, name=f"gather_{piece}_l{l}_start")
        return token

    token = first[0]
    for k in range(eager, eager + ahead - 1):
        token = start_gather(k, token)
    mod = mod + token[0, 0]

    def weights_of(l, piece, h):
        k = order.index((l, piece))
        started = eager <= k + ahead < len(order) and k + ahead not in in_flight
        token = start_gather(k + ahead, h) if started else None
        if k < eager:
            return piece_weights(first[n_tensors * k:n_tensors * (k + 1)], piece), token
        landed = _exchange_wait(in_flight[k], h if token is None else token, gather=True, name=f"gather_{piece}_l{l}_wait")
        return piece_weights(landed, piece), token

    exchanges, have = {}, {}

    def grads_done(l, piece, g):
        key = (l, piece)
        have.setdefault(key, {}).update(g)
        if piece == "mix":
            if len(have[key]) < 6:
                return None
            g = have[key]
            s_br = jnp.concatenate([_lanes_shard(g[n][None]) for n in ("br_sb", "br_dil", "br_swa")], 1)
            groups = [(("in", "br", "out"), [jnp.concatenate([g["qkv"], g["gates"]], 0).reshape(N_DEV, -1, d), s_br,
                                             g["out"].reshape(N_DEV, -1, d)])]
        elif key == order[0]:
            groups = [((n,), [t.reshape(N_DEV, -1, d)]) for n, t in g.items()]
        elif len(have[key]) < 3:
            return None
        else:
            groups = [(("gate", "up", "down"), [have[key][n].reshape(N_DEV, -1, d) for n in ("gate", "up", "down")])]
        token = None
        for names, sg in groups:
            state, token = _exchange_start(sg, sg[0], gather=False, name=f"exchange_{piece}_l{l}_{names[0]}_start")
            exchanges.setdefault(key, []).append((names, state))
        return token

    loss, dx, dmod, dgains, dfinal, drel, dsinks = _local_step(
        x[0], loss_target[0], mod, gains, weights_of, rel_bias, sinks, final_gain, grads_done)

    small_parts = [dmod, dgains, dfinal, drel.T, dsinks, loss[0, :1]]
    small_sizes = [int(np.prod(p.shape)) for p in small_parts]
    small_all, = _all_gather([_pack_rows(small_parts, F32)], name="gather_small")
    small_sum = _sum_parts([small_all], name="sum_small").reshape(-1)
    offs = np.concatenate([[0], np.cumsum(small_sizes)])
    g_b_ada = small_sum[offs[0]:offs[1]].reshape(DEPTH, 9 * d)
    g_gain_full = small_sum[offs[1]:offs[2]].reshape(DEPTH, 3, d)
    g_norm_gain = lax.dynamic_slice_in_dim(g_gain_full, me * LANES, LANES, axis=2)
    g_final = small_sum[offs[2]:offs[3]]
    g_rel = small_sum[offs[3]:offs[4]].reshape(N_SOFT, N_BUCKETS).T
    g_sinks = small_sum[offs[4]:offs[5]].reshape(DEPTH, H_SWA_Q)
    loss_total = small_sum[offs[5]]

    dmod_all = small_all.reshape(N_DEV, -1)[:, :DEPTH * 9 * d].reshape(N_DEV, DEPTH, 9 * d)
    dmod_cols = lax.dynamic_slice_in_dim(dmod_all, me * cols, cols, axis=2)
    g_w_ada = jnp.stack([_ada_bwd(c_all.T, dmod_cols[:, l], name=f"ada_bwd_l{l}") for l in range(DEPTH)])

    state = {"w_ada": (w_ada, m_w_ada, v_w_ada), "b_ada": (b_ada, m_b_ada, v_b_ada),
             "norm_gain": (norm_gain, m_norm_gain, v_norm_gain), "w_ffn_gate": (w_ffn_gate, m_w_ffn_gate, v_w_ffn_gate),
             "w_ffn_up": (w_ffn_up, m_w_ffn_up, v_w_ffn_up), "w_ffn_down": (w_ffn_down, m_w_ffn_down, v_w_ffn_down),
             "w_in": (w_in, m_w_in, v_w_in), "w_br_sb": (w_br_sb, m_w_br_sb, v_w_br_sb),
             "w_br_dil": (w_br_dil, m_w_br_dil, v_w_br_dil), "w_br_swa": (w_br_swa, m_w_br_swa, v_w_br_swa),
             "w_out": (w_out, m_w_out, v_w_out), "sinks": (sinks, m_sinks, v_sinks),
             "rel_bias": (rel_bias, m_rel_bias, v_rel_bias), "final_gain": (final_gain, m_final_gain, v_final_gain)}
    grad, update = {}, {}

    def adamw(n, g, transposed=False):
        w, m, v = (jnp.swapaxes(t, -1, -2) for t in state[n]) if transposed else state[n]
        if w.ndim == 1:
            out = tuple(t.reshape(w.shape) for t in _adamw(_row(w), _row(g), _row(m), _row(v), name=f"adamw_{n}"))
        else:
            out = _adamw(w, g, m, v, name=f"adamw_{n}")
        if transposed:
            grad[n], update[n] = jnp.swapaxes(g, -1, -2), tuple(jnp.swapaxes(t, -1, -2) for t in out)
        else:
            grad[n], update[n] = g, out

    for n, g in (("w_ada", g_w_ada), ("b_ada", g_b_ada), ("norm_gain", g_norm_gain), ("sinks", g_sinks),
                 ("rel_bias", g_rel), ("final_gain", g_final)):
        adamw(n, g)

    after = update["w_ada"][0]
    parts = {}
    for key in reversed(order):
        for names, ex_state in exchanges[key]:
            landed = _exchange_wait(ex_state, after, gather=False, name=f"exchange_{key[1]}_l{key[0]}_{names[0]}_wait")
            parts.setdefault(key, {}).update(zip(names, landed))
            after = landed[0]
    ffn_keys = [key for key in order if key[1] != "mix"]
    mix_keys = [key for key in order if key[1] == "mix"]
    sums = {n: _sum_parts([parts[key][n] for key in ffn_keys], name=f"sum_grads_{n}") for n in ("gate", "up", "down")}
    sums.update({n: _sum_parts([parts[key][n] for key in mix_keys], name=f"sum_grads_{n}") for n in ("in", "br", "out")})
    br_sums = sums["br"].reshape(DEPTH, -1, LANES)
    adamw("w_ffn_gate", sums["gate"].reshape(gate_t.shape), transposed=True)
    adamw("w_ffn_up", sums["up"].reshape(up_t.shape), transposed=True)
    adamw("w_ffn_down", sums["down"].reshape(w_ffn_down.shape))
    adamw("w_in", sums["in"].reshape(in_t.shape), transposed=True)
    adamw("w_br_sb", br_sums[:, br_off[0]:br_off[1]])
    adamw("w_br_dil", br_sums[:, br_off[1]:br_off[2]])
    adamw("w_br_swa", br_sums[:, br_off[2]:br_off[3]])
    adamw("w_out", sums["out"].reshape(w_out.shape))

    names = ["w_ada", "b_ada", "norm_gain", "w_ffn_gate", "w_ffn_up", "w_ffn_down", "w_in", "w_br_sb", "w_br_dil",
             "w_br_swa", "w_out", "sinks", "rel_bias", "final_gain"]
    return (loss_total, dx[None], *[grad[n] for n in names], *[update[n][0] for n in names],
            *[update[n][1] for n in names], *[update[n][2] for n in names])
```

```python
import math

import numpy as np
import jax
import jax.numpy as jnp
from jax import lax
from jax.experimental import pallas as pl
from jax.experimental.pallas import tpu as pltpu

F32, BF16 = jnp.float32, jnp.bfloat16

SEQ, D_MODEL, D_FF, HEAD_DIM = 2048, 1024, 2816, 64
DEPTH = 2
BLK = 128
H_SB, H_DIL, H_SWA_Q, H_SWA_KV = 4, 6, 6, 2
DIL_PATTERNS = ((128, 1), (512, 4), (2048, 16))
SWA_WINDOW = 128
N_BUCKETS, MAX_REL_DIST = 32, 2048
RMS_EPS = 1e-6
D_QKV = 2560
D_GATES = 3 * D_MODEL
ADAM_LR, ADAM_B1, ADAM_B2, ADAM_EPS, ADAM_WD, ADAM_STEP = 0.001, 0.9, 0.999, 1e-08, 0.01, 10

N_DEV = 8
LANES = 128
NEG = -1e30
SB_TILE = 256
VMEM_LIMIT_BYTES = 48 * 1024 * 1024
HBM = pl.BlockSpec(memory_space=pltpu.HBM)
MESH = pl.DeviceIdType.MESH


def _tile(n, target):
    t = (min(n, target) // LANES) * LANES
    while t >= LANES:
        if n % t == 0:
            return t
        t -= LANES
    return n


def _row_tile(r, cap):
    t = (min(r, cap) // 16) * 16
    while t > 16 and r % t:
        t -= 16
    return t


def _params(semantics=None):
    return pltpu.CompilerParams(dimension_semantics=semantics, vmem_limit_bytes=VMEM_LIMIT_BYTES)


def _dot(a, b, ca, cb):
    return lax.dot_general(a, b, (((ca,), (cb,)), ((), ())), preferred_element_type=F32)


def _sigmoid(a):
    return 1.0 / (1.0 + jnp.exp(-a))


def _row(v):
    return v.reshape(1, -1)


def _all_gather(arrs, name, after=None):
    n = len(arrs)
    ins = list(arrs) + ([] if after is None else [after])

    def body(*refs):
        x_refs, out_refs = refs[:n], refs[len(ins):len(ins) + n]
        send_sems, recv_sems, local_sems = refs[len(ins) + n:]
        x, y, c = lax.axis_index("x"), lax.axis_index("y"), lax.axis_index("c")
        me, sibling = (x, y, c), (x, y, 1 - c)
        chips = [(1 - x, y), (x, 1 - y), (1 - x, 1 - y)]

        def slot(t, px, py, pc):
            return out_refs[t].at[4 * px + 2 * py + pc]

        def copy(t, k, block, to, src=None):
            return pltpu.make_async_remote_copy(
                src_ref=slot(t, *block) if src is None else src, dst_ref=slot(t, *block),
                send_sem=send_sems.at[7 * t + k], recv_sem=recv_sems.at[7 * t + k], device_id=to, device_id_type=MESH)

        mine = [pltpu.make_async_copy(x_refs[t], slot(t, *me), local_sems.at[t]) for t in range(n)]
        for cp in mine:
            cp.start()
        first = []
        for t in range(n):
            first.append(copy(t, 0, me, sibling, src=x_refs[t]))
            first += [copy(t, 1 + j, me, (*chip, c), src=x_refs[t]) for j, chip in enumerate(chips)]
        for cp in first:
            cp.start()
        passed = []
        for j, chip in enumerate(chips):
            for t in range(n):
                copy(t, 1 + j, (*chip, c), me).wait_recv()
                passed.append(copy(t, 4 + j, (*chip, c), sibling))
                passed[-1].start()
        for t in range(n):
            copy(t, 0, sibling, me).wait_recv()
        for j, chip in enumerate(chips):
            for t in range(n):
                copy(t, 4 + j, (*chip, 1 - c), me).wait_recv()
        for cp in first + passed:
            cp.wait_send()
        for cp in mine:
            cp.wait()

    return pl.pallas_call(
        body, name=name, out_shape=[jax.ShapeDtypeStruct((N_DEV,) + a.shape, a.dtype) for a in arrs],
        in_specs=[HBM] * n + [pl.BlockSpec(memory_space=pl.ANY)] * (len(ins) - n), out_specs=[HBM] * n,
        scratch_shapes=[pltpu.SemaphoreType.DMA((7 * n,)), pltpu.SemaphoreType.DMA((7 * n,)), pltpu.SemaphoreType.DMA((n,))],
    )(*ins)


def _direct_copies(x_refs, land_refs, send_sems, recv_sems, local_sems, gather):
    x, y, c = lax.axis_index("x"), lax.axis_index("y"), lax.axis_index("c")
    me = 4 * x + 2 * y + c
    sends, recvs = [], []
    for k in range(1, N_DEV):
        px = 1 - x if (k >> 2) & 1 else x
        py = 1 - y if (k >> 1) & 1 else y
        pc = 1 - c if k & 1 else c
        peer = 4 * px + 2 * py + pc
        for t, (x_ref, land_ref) in enumerate(zip(x_refs, land_refs)):
            sem = 7 * t + k - 1
            for out, src, slot in ((sends, x_ref if gather else x_ref.at[peer], me),
                                   (recvs, x_ref if gather else x_ref.at[me], peer)):
                out.append(pltpu.make_async_remote_copy(
                    src_ref=src, dst_ref=land_ref.at[slot], send_sem=send_sems.at[sem], recv_sem=recv_sems.at[sem],
                    device_id=(px, py, pc), device_id_type=MESH))
    own = [pltpu.make_async_copy(x_ref if gather else x_ref.at[me], land_ref.at[me], local_sems.at[t])
           for t, (x_ref, land_ref) in enumerate(zip(x_refs, land_refs))]
    return sends, recvs, own


SEM =pl.BlockSpec(memory_space=pltpu.SEMAPHORE)
ANY = pl.BlockSpec(memory_space=pl.ANY)
SIDE_EFFECT = pltpu.SideEffectType.DATAFLOW_SIDE_EFFECTING


def _exchange_start(arrs, after, *, gather, name):
    n = len(arrs)
    lands = [lax.empty(((N_DEV,) + a.shape) if gather else a.shape, a.dtype) for a in arrs]

    def body(*refs):
        sends, _, own = _direct_copies(refs[:n], refs[n:2 * n], *refs[2 * n + 1:2 * n + 4], gather)
        for cp in own + sends:
            cp.start()
        refs[-1][...] = jnp.zeros_like(refs[-1])

    ops = [pltpu.with_memory_space_constraint(a, pltpu.HBM) for a in list(arrs) + lands]
    out = pl.pallas_call(
        body, name=name,
        out_shape=(pltpu.SemaphoreType.DMA((7 * n,)), pltpu.SemaphoreType.DMA((7 * n,)), pltpu.SemaphoreType.DMA((n,)),
                   *[pltpu.HBM(a.shape, a.dtype) for a in ops], jax.ShapeDtypeStruct((8, LANES), F32)),
        in_specs=[HBM] * (2 * n) + [ANY],
        out_specs=(SEM, SEM, SEM, *[HBM] * (2 * n), pl.BlockSpec(memory_space=pltpu.VMEM)),
        input_output_aliases={t: 3 + t for t in range(2 * n)},
        compiler_params=pltpu.CompilerParams(has_side_effects=SIDE_EFFECT),
    )(*ops, after)
    return (out[:3], out[3:3 + n], out[3 + n:3 + 2 * n]), out[-1]


def _exchange_wait(state, after, *, gather, name):
    sems, arrs, lands = state
    n = len(arrs)

    def body(*refs):
        sends, recvs, own = _direct_copies(refs[:n], refs[n:2 * n], *refs[2 * n:2 * n + 3], gather)
        for cp in own:
            cp.wait()
        for cp in sends:
            cp.wait_send()
        for cp in recvs:
            cp.wait_recv()

    out = pl.pallas_call(
        body, name=name, out_shape=tuple(pltpu.HBM(a.shape, a.dtype) for a in list(arrs) + list(lands)),
        in_specs=[HBM] * (2 * n) + [SEM, SEM, SEM, ANY], out_specs=tuple([HBM] * (2 * n)),
        input_output_aliases={t: t for t in range(2 * n)},
        compiler_params=pltpu.CompilerParams(has_side_effects=SIDE_EFFECT),
    )(*arrs, *lands, *sems, after)
    return out[n:]


def _sum_parts(groups, name):
    n, r, cdim = groups[0].shape
    tr = _row_tile(r, max(16, (1 << 21) // (n * cdim * groups[0].dtype.itemsize)))
    steps = r // tr

    def body(*refs):
        o_ref = refs[-1]
        gg = pl.program_id(0)
        for gi in range(len(groups)):
            @pl.when(gg == gi)
            def _(gi=gi):
                acc = refs[gi][0].astype(F32)
                for k in range(1, n):
                    acc = acc + refs[gi][k].astype(F32)
                o_ref[...] = acc

    def in_spec(gi):
        return pl.BlockSpec((n, tr, cdim), lambda gg, i: (0, jnp.where(gg == gi, i, 0), 0))

    return pl.pallas_call(
        body, name=name, out_shape=jax.ShapeDtypeStruct((len(groups) * r, cdim), F32), grid=(len(groups), steps),
        in_specs=[in_spec(gi) for gi in range(len(groups))],
        out_specs=pl.BlockSpec((tr, cdim), lambda gg, i: (gg * steps + i, 0)),
        compiler_params=_params(("parallel", "parallel")),
    )(*groups)


def _mm_tn(a, b, *, name, after=None, tm=512, tn=1024):
    k, m = a.shape
    n = b.shape[1]
    tm, tn = _tile(m, tm), _tile(n, tn)

    def body(a_ref, b_ref, *rest):
        o_ref, at_ref = rest[-2], rest[-1]

        @pl.when(pl.program_id(1) == 0)
        def _():
            at_ref[...] = a_ref[...].astype(BF16).T

        o_ref[...] = _dot(at_ref[...], b_ref[...].astype(BF16), 1, 0).astype(BF16)

    ins = [a, b] + ([] if after is None else [after])
    return pl.pallas_call(
        body, name=name, out_shape=jax.ShapeDtypeStruct((m, n), BF16), grid=(m // tm, n // tn),
        in_specs=[pl.BlockSpec((k, tm), lambda i, j: (0, i)), pl.BlockSpec((k, tn), lambda i, j: (0, j))] + [ANY] * (len(ins) - 2),
        out_specs=pl.BlockSpec((tm, tn), lambda i, j: (i, j)),
        scratch_shapes=[pltpu.VMEM((tm, k), BF16)], compiler_params=_params(("parallel", "arbitrary")),
    )(*ins)


def _mm2(a1, b1, a2, b2, *, name, after=None, tm=256, tn=1024):
    m = a1.shape[0]
    n = b1.shape[1]
    tm, tn = _tile(m, tm), _tile(n, tn)

    def body(a1_ref, b1_ref, a2_ref, b2_ref, *rest):
        rest[-1][...] = (_dot(a1_ref[...].astype(BF16), b1_ref[...], 1, 0)
                         + _dot(a2_ref[...].astype(BF16), b2_ref[...], 1, 0))

    ins = [a1, b1, a2, b2] + ([] if after is None else [after])

    def a_spec(t):
        return pl.BlockSpec((tm, t.shape[1]), lambda i, j: (i, 0))

    def b_spec(t):
        return pl.BlockSpec((t.shape[0], tn), lambda i, j: (0, j))

    return pl.pallas_call(
        body, name=name, out_shape=jax.ShapeDtypeStruct((m, n), F32), grid=(m // tm, n // tn),
        in_specs=[a_spec(a1), b_spec(b1), a_spec(a2), b_spec(b2)] + [ANY] * (len(ins) - 4),
        out_specs=pl.BlockSpec((tm, tn), lambda i, j: (i, j)), compiler_params=_params(("parallel", "parallel")),
    )(*ins)


def _mm(a, b, *, name, ta=False, tb=False, res=None, colscale=None, emit_acc=False,
        out_dtype=F32, tm=512, tn=512):
    m, k = (a.shape[1], a.shape[0]) if ta else a.shape
    n = b.shape[0] if tb else b.shape[1]
    tm, tn = _tile(m, tm), _tile(n, tn)
    ca, cb = (0 if ta else 1), (1 if tb else 0)
    a_spec = pl.BlockSpec((k, tm), lambda i, j: (0, i)) if ta else pl.BlockSpec((tm, k), lambda i, j: (i, 0))
    b_spec = pl.BlockSpec((tn, k), lambda i, j: (j, 0)) if tb else pl.BlockSpec((k, tn), lambda i, j: (0, j))
    tile = pl.BlockSpec((tm, tn), lambda i, j: (i, j))
    ins, in_specs = [a, b], [a_spec, b_spec]
    if res is not None:
        ins.append(res)
        in_specs.append(tile)
    if colscale is not None:
        ins.append(colscale)
        in_specs.append(pl.BlockSpec((1, tn), lambda i, j: (0, j)))
    n_in = len(ins)

    def body(*refs):
        outs = refs[n_in:]
        acc = _dot(refs[0][...].astype(BF16), refs[1][...].astype(BF16), ca, cb)
        val, p = acc, 2
        if res is not None:
            r_val, p = refs[p][...], p + 1
        if colscale is not None:
            val = val * refs[p][...]
        if res is not None:
            val = r_val + val
        if emit_acc:
            outs[0][...] = acc
        outs[-1][...] = val.astype(out_dtype)

    out_shape = [jax.ShapeDtypeStruct((m, n), out_dtype)]
    out_specs = [tile]
    if emit_acc:
        out_shape.insert(0, jax.ShapeDtypeStruct((m, n), F32))
        out_specs.insert(0, tile)
    out = pl.pallas_call(
        body, name=name, out_shape=out_shape, grid=(m // tm, n // tn), in_specs=in_specs, out_specs=out_specs,
        compiler_params=_params(("parallel", "parallel")),
    )(*ins)
    return out if emit_acc else out[0]


def _norm_fwd(x, g, scale, shift, name, after=None):
    s, d = x.shape
    tr = 256

    def body(x_ref, g_ref, sc_ref, sh_ref, *rest):
        xv = x_ref[...]
        rstd = lax.rsqrt(jnp.mean(xv * xv, axis=-1, keepdims=True) + RMS_EPS)
        rest[-1][...] = (xv * rstd * g_ref[...] * (1.0 + sc_ref[...]) + sh_ref[...]).astype(BF16)

    rowspec = pl.BlockSpec((1, d), lambda i: (0, 0))
    ins = [x, g, scale, shift] + ([] if after is None else [after])
    return pl.pallas_call(
        body, name=name, out_shape=jax.ShapeDtypeStruct((s, d), BF16), grid=(s // tr,),
        in_specs=[pl.BlockSpec((tr, d), lambda i: (i, 0)), rowspec, rowspec, rowspec] + [ANY] * (len(ins) - 4),
        out_specs=pl.BlockSpec((tr, d), lambda i: (i, 0)),
        compiler_params=_params(("parallel",)),
    )(*ins)


def _norm_bwd(x, dh, dres, g, scale, name):
    s, d = x.shape
    tr = 256

    def body(x_ref, dh_ref, dr_ref, g_ref, sc_ref, dx_ref, a_ref, b_ref):
        @pl.when(pl.program_id(0) == 0)
        def _():
            a_ref[...] = jnp.zeros_like(a_ref)
            b_ref[...] = jnp.zeros_like(b_ref)

        xv = x_ref[...]
        rstd = lax.rsqrt(jnp.mean(xv * xv, axis=-1, keepdims=True) + RMS_EPS)
        xhat = xv * rstd
        dhv = dh_ref[...]
        dxhat = dhv * (g_ref[...] * (1.0 + sc_ref[...]))
        mean_term = jnp.mean(dxhat * xhat, axis=-1, keepdims=True)
        dx_ref[...] = dr_ref[...] + rstd * (dxhat - xhat * mean_term)
        a_ref[...] += jnp.sum(dhv, axis=0, keepdims=True)
        b_ref[...] += jnp.sum(dhv * xhat, axis=0, keepdims=True)

    rowspec = pl.BlockSpec((1, d), lambda i: (0, 0))
    tile = pl.BlockSpec((tr, d), lambda i: (i, 0))
    return pl.pallas_call(
        body, name=name,
        out_shape=[jax.ShapeDtypeStruct((s, d), F32), jax.ShapeDtypeStruct((1, d), F32), jax.ShapeDtypeStruct((1, d), F32)],
        grid=(s // tr,), in_specs=[tile, tile, tile, rowspec, rowspec], out_specs=[tile, rowspec, rowspec],
        compiler_params=_params(("arbitrary",)),
    )(x, dh, dres, g, scale)


def _gate_bwd(dxn, f, colscale, coef, name):
    s, d = dxn.shape
    tr = 256

    def body(dx_ref, f_ref, cs_ref, df_ref, dg_ref):
        @pl.when(pl.program_id(0) == 0)
        def _():
            dg_ref[...] = jnp.zeros_like(dg_ref)

        dxv = dx_ref[...]
        df_ref[...] = (dxv * cs_ref[...]).astype(BF16)
        dg_ref[...] += coef * jnp.sum(dxv * f_ref[...], axis=0, keepdims=True)

    rowspec = pl.BlockSpec((1, d), lambda i: (0, 0))
    tile = pl.BlockSpec((tr, d), lambda i: (i, 0))
    return pl.pallas_call(
        body, name=name, out_shape=[jax.ShapeDtypeStruct((s, d), BF16), jax.ShapeDtypeStruct((1, d), F32)],
        grid=(s // tr,), in_specs=[tile, tile, rowspec], out_specs=[tile, rowspec],
        compiler_params=_params(("arbitrary",)),
    )(dxn, f, colscale)


def _ffn_up(h, wg, wu, name):
    s, d = h.shape
    f = wg.shape[0]
    tm, tn = s, _tile(f, 256)

    def body(h_ref, wg_ref, wu_ref, a_ref, u_ref, s_ref):
        hv = h_ref[...]
        a = _dot(hv, wg_ref[...], 1, 1)
        u = _dot(hv, wu_ref[...], 1, 1)
        a_ref[...] = a.astype(BF16)
        u_ref[...] = u.astype(BF16)
        s_ref[...] = (a * _sigmoid(a) * u).astype(BF16)

    tile = pl.BlockSpec((tm, tn), lambda i, j: (i, j))
    wspec = pl.BlockSpec((tn, d), lambda i, j: (j, 0))
    return pl.pallas_call(
        body, name=name,
        out_shape=[jax.ShapeDtypeStruct((s, f), BF16), jax.ShapeDtypeStruct((s, f), BF16), jax.ShapeDtypeStruct((s, f), BF16)],
        grid=(s // tm, f // tn), in_specs=[pl.BlockSpec((tm, d), lambda i, j: (i, 0)), wspec, wspec],
        out_specs=[tile, tile, tile], compiler_params=_params(("parallel", "parallel")),
    )(h, wg, wu)


def _ffn_bwd_ds(df, wd, a, u, name):
    s, d = df.shape
    f = wd.shape[0]
    tm, tn = 1024, _tile(f, 256)

    def body(df_ref, wd_ref, a_ref, u_ref, da_ref, du_ref):
        ds = _dot(df_ref[...], wd_ref[...], 1, 1)
        av = a_ref[...].astype(F32)
        sg = _sigmoid(av)
        da_ref[...] = (ds * u_ref[...].astype(F32) * (sg * (1.0 + av * (1.0 - sg)))).astype(BF16)
        du_ref[...] = (ds * (av * sg)).astype(BF16)

    tile = pl.BlockSpec((tm, tn), lambda i, j: (i, j))
    return pl.pallas_call(
        body, name=name, out_shape=[jax.ShapeDtypeStruct((s, f), BF16), jax.ShapeDtypeStruct((s, f), BF16)],
        grid=(s // tm, f // tn),
        in_specs=[pl.BlockSpec((tm, d), lambda i, j: (i, 0)), pl.BlockSpec((tn, d), lambda i, j: (j, 0)), tile, tile],
        out_specs=[tile, tile], compiler_params=_params(("parallel", "parallel")),
    )(df, wd, a, u)


def _merge_fwd(o_sb, o_dil, o_swa, gates, wb_sb, wb_dil, wb_swa, name):
    s, d = SEQ, D_MODEL
    tm = 256

    def body(osb_ref, odl_ref, osw_ref, g_ref, wsb_ref, wdl_ref, wsw_ref, m_ref, tsb_ref, tdl_ref, tsw_ref):
        for h in range(osb_ref.shape[0]):
            tsb_ref[:, h * HEAD_DIM:(h + 1) * HEAD_DIM] = osb_ref[h].astype(BF16)
        for h in range(osw_ref.shape[0]):
            tsw_ref[:, h * HEAD_DIM:(h + 1) * HEAD_DIM] = osw_ref[h].astype(BF16)
        tdl_ref[...] = odl_ref[...].astype(BF16)
        acc = _sigmoid(g_ref[:, 0:d]) * _dot(tsb_ref[...], wsb_ref[...], 1, 0)
        acc += _sigmoid(g_ref[:, d:2 * d]) * _dot(tdl_ref[...], wdl_ref[...], 1, 0)
        acc += _sigmoid(g_ref[:, 2 * d:3 * d]) * _dot(tsw_ref[...], wsw_ref[...], 1, 0)
        m_ref[...] = acc.astype(BF16)

    def rows(w):
        return pl.BlockSpec((tm, w), lambda i: (i, 0))

    def heads(n):
        return pl.BlockSpec((n, tm, HEAD_DIM), lambda i: (0, i, 0))

    def whole(w):
        return pl.BlockSpec((w, d), lambda i: (0, 0))

    return pl.pallas_call(
        body, name=name, out_shape=[jax.ShapeDtypeStruct((s, w), BF16) for w in (d, 256, 128, 384)], grid=(s // tm,),
        in_specs=[heads(H_SB), rows(128), heads(H_SWA_Q), rows(3 * d), whole(256), whole(128), whole(384)],
        out_specs=[rows(d), rows(256), rows(128), rows(384)], compiler_params=_params(("parallel",)),
    )(o_sb, o_dil, o_swa, gates, wb_sb, wb_dil, wb_swa)


def _merge_bwd(dmerged, t_sb, t_dil, t_swa, gates, wb_sb, wb_dil, wb_swa, name):
    s, d = SEQ, D_MODEL
    tm = 256

    def body(dm_ref, tsb_ref, tdl_ref, tsw_ref, g_ref, wsb_ref, wdl_ref, wsw_ref,
             dg_ref, dosb_ref, dodl_ref, dosw_ref, dbsb_ref, dbdl_ref, dbsw_ref):
        dm = dm_ref[...]
        for idx, (t_ref, w_ref, do_ref, db_ref) in enumerate((
                (tsb_ref, wsb_ref, dosb_ref, dbsb_ref), (tdl_ref, wdl_ref, dodl_ref, dbdl_ref),
                (tsw_ref, wsw_ref, dosw_ref, dbsw_ref))):
            w = w_ref[...]
            br = _dot(t_ref[...], w, 1, 0)
            sg = _sigmoid(g_ref[:, idx * d:(idx + 1) * d])
            dbr = (dm * sg).astype(BF16)
            dg_ref[:, idx * d:(idx + 1) * d] = dm * br * (sg * (1.0 - sg))
            db_ref[...] = dbr
            do = _dot(dbr, w, 1, 1)
            if len(do_ref.shape) == 2:
                do_ref[...] = do
            else:
                for h in range(do_ref.shape[0]):
                    do_ref[h] = do[:, h * HEAD_DIM:(h + 1) * HEAD_DIM]

    def rows(w):
        return pl.BlockSpec((tm, w), lambda i: (i, 0))

    def heads(n):
        return pl.BlockSpec((n, tm, HEAD_DIM), lambda i: (0, i, 0))

    def whole(w):
        return pl.BlockSpec((w, d), lambda i: (0, 0))

    def shp(w, dt):
        return jax.ShapeDtypeStruct((s, w), dt)

    def hshp(n):
        return jax.ShapeDtypeStruct((n, s, HEAD_DIM), F32)

    return pl.pallas_call(
        body, name=name,
        out_shape=[shp(3 * d, F32), hshp(H_SB), shp(128, F32), hshp(H_SWA_Q), shp(d, BF16), shp(d, BF16), shp(d, BF16)],
        grid=(s // tm,),
        in_specs=[rows(d), rows(256), rows(128), rows(384), rows(3 * d), whole(256), whole(128), whole(384)],
        out_specs=[rows(3 * d), heads(H_SB), rows(128), heads(H_SWA_Q), rows(d), rows(d), rows(d)],
        compiler_params=_params(("parallel",)),
    )(dmerged, t_sb, t_dil, t_swa, gates, wb_sb, wb_dil, wb_swa)


def _final_loss(x, target, g, name):
    s, d = x.shape
    tr = 256

    def body(x_ref, t_ref, g_ref, loss_ref, dx_ref, dg_ref):
        @pl.when(pl.program_id(0) == 0)
        def _():
            loss_ref[...] = jnp.zeros_like(loss_ref)
            dg_ref[...] = jnp.zeros_like(dg_ref)

        xv = x_ref[...]
        gv = g_ref[...]
        rstd = lax.rsqrt(jnp.mean(xv * xv, axis=-1, keepdims=True) + RMS_EPS)
        xhat = xv * rstd
        err = xhat * gv - t_ref[...]
        loss_ref[...] += 0.5 * jnp.sum(jnp.mean(err * err, axis=-1, keepdims=True))
        dy = err * (1.0 / d)
        dxhat = dy * gv
        mean_term = jnp.mean(dxhat * xhat, axis=-1, keepdims=True)
        dx_ref[...] = rstd * (dxhat - xhat * mean_term)
        dg_ref[...] += jnp.sum(dy * xhat, axis=0, keepdims=True)

    rowspec = pl.BlockSpec((1, d), lambda i: (0, 0))
    tile = pl.BlockSpec((tr, d), lambda i: (i, 0))
    return pl.pallas_call(
        body, name=name,
        out_shape=[jax.ShapeDtypeStruct((1, LANES), F32), jax.ShapeDtypeStruct((s, d), F32), jax.ShapeDtypeStruct((1, d), F32)],
        grid=(s // tr,), in_specs=[tile, tile, rowspec],
        out_specs=[pl.BlockSpec((1, LANES), lambda i: (0, 0)), tile, rowspec],
        compiler_params=_params(("arbitrary",)),
    )(x, target, g)


def _adamw(w, g, m, v, name):
    shape = w.shape
    cols = shape[-1]
    rows = int(np.prod(shape[:-1])) if len(shape) > 1 else 1
    tr = rows
    for cand in (1024, 512, 256, 128, 64, 32, 16, 8):
        if rows % cand == 0 and rows > cand and cand * cols * 4 <= (1 << 21):
            tr = cand
            break

    def body(w_ref, g_ref, m_ref, v_ref, d_ref, nm_ref, nv_ref):
        gv = g_ref[...]
        nm = ADAM_B1 * m_ref[...] + (1.0 - ADAM_B1) * gv
        nv = ADAM_B2 * v_ref[...] + (1.0 - ADAM_B2) * (gv * gv)
        m_hat = nm / (1.0 - ADAM_B1 ** ADAM_STEP)
        v_hat = nv / (1.0 - ADAM_B2 ** ADAM_STEP)
        d_ref[...] = -ADAM_LR * (m_hat / (jnp.sqrt(v_hat) + ADAM_EPS) + ADAM_WD * w_ref[...])
        nm_ref[...] = nm
        nv_ref[...] = nv

    tile = pl.BlockSpec((tr, cols), lambda i: (i, 0))
    flat = [t.reshape(rows, cols) for t in (w, g, m, v)]
    out = pl.pallas_call(
        body, name=name, out_shape=[jax.ShapeDtypeStruct((rows, cols), F32)] * 3, grid=(rows // tr,),
        in_specs=[tile] * 4, out_specs=[tile] * 3, compiler_params=_params(("parallel",)),
    )(*flat)
    return tuple(t.reshape(shape) for t in out)


def _ada_fwd(c_all, w, name):
    n = w.shape[1]

    def body(c_ref, w_ref, o_ref):
        cv = c_ref[...]
        o_ref[...] = jnp.dot(cv * _sigmoid(cv), w_ref[...], preferred_element_type=F32, precision=lax.Precision.HIGHEST)

    return pl.pallas_call(body, name=name, out_shape=jax.ShapeDtypeStruct((N_DEV, n), F32), compiler_params=_params())(c_all, w)


def _ada_bwd(c_all_t, dmod, name):
    n = dmod.shape[1]

    def body(c_ref, d_ref, o_ref):
        cv = c_ref[...]
        o_ref[...] = jnp.dot(cv * _sigmoid(cv), d_ref[...], preferred_element_type=F32, precision=lax.Precision.HIGHEST)

    return pl.pallas_call(body, name=name, out_shape=jax.ShapeDtypeStruct((D_MODEL, n), F32), compiler_params=_params())(c_all_t, dmod)


def _bucket_tables():
    rel = np.arange(BLK)[:, None] + BLK - np.arange(2 * BLK)[None, :]
    max_exact = N_BUCKETS // 2

    def bucket(n):
        nf = np.maximum(n, 1).astype(np.float32)
        large = max_exact + (np.log(nf / np.float32(max_exact)) / np.float32(math.log(MAX_REL_DIST / max_exact))
                             * np.float32(N_BUCKETS - max_exact)).astype(np.int32)
        return np.where(n < max_exact, n, np.minimum(large, N_BUCKETS - 1))

    tabs = []
    for dil, max_dist in ((1, 128), (4, 128), (16, 128), (1, SWA_WINDOW - 1)):
        in_band = (rel >= 0) & (rel <= max_dist)
        tabs.append(np.where(in_band, bucket(np.maximum(rel, 0) * dil), -1))
    return np.stack(tabs).astype(np.int32)


N_SOFT = H_DIL + H_SWA_Q


def _table_of_head(h):
    return jnp.minimum(h // 2, 3)


def _bias_build(rel_bias, tables, name):
    def body(rel_ref, t_ref, o_ref):
        h = pl.program_id(0)
        tb = t_ref[0]
        out = jnp.full((BLK, 2 * BLK), NEG, F32)
        for b in range(N_BUCKETS):
            out = jnp.where(tb == b, rel_ref[b, h], out)
        o_ref[0] = out

    return pl.pallas_call(
        body, name=name, out_shape=jax.ShapeDtypeStruct((N_SOFT, BLK, 2 * BLK), F32), grid=(N_SOFT,),
        in_specs=[pl.BlockSpec(memory_space=pltpu.SMEM),
                  pl.BlockSpec((1, BLK, 2 * BLK), lambda h: (_table_of_head(h), 0, 0))],
        out_specs=pl.BlockSpec((1, BLK, 2 * BLK), lambda h: (h, 0, 0)),
        compiler_params=_params(("parallel",)),
    )(rel_bias, tables)


def _bias_grad(dbias, tables, name):
    def body(d_ref, t_ref, o_ref):
        tb = t_ref[0]
        dv = d_ref[0]
        lane = lax.broadcasted_iota(jnp.int32, (1, LANES), 1)
        out = jnp.zeros((1, LANES), F32)
        for b in range(N_BUCKETS):
            out = jnp.where(lane == b, jnp.sum(jnp.where(tb == b, dv, 0.0)), out)
        o_ref[0] = out

    return pl.pallas_call(
        body, name=name, out_shape=jax.ShapeDtypeStruct((N_SOFT, 1, LANES), F32), grid=(N_SOFT,),
        in_specs=[pl.BlockSpec((1, BLK, 2 * BLK), lambda h: (h, 0, 0)),
                  pl.BlockSpec((1, BLK, 2 * BLK), lambda h: (_table_of_head(h), 0, 0))],
        out_specs=pl.BlockSpec((1, 1, LANES), lambda h: (h, 0, 0)),
        compiler_params=_params(("parallel",)),
    )(dbias, tables)


def _band_specs(g, bias_div, offs):
    def seq(off, div, prev):
        return pl.BlockSpec((1, BLK, HEAD_DIM),
                            lambda n, i: (off + n // div, jnp.maximum(i - 1, 0) if prev else i, 0))

    xspecs = [seq(offs[0], 1, False), seq(offs[1], g, True), seq(offs[1], g, False),
              seq(offs[2], g, True), seq(offs[2], g, False)]
    qspec = pl.BlockSpec((1, BLK, HEAD_DIM), lambda n, i: (n, i, 0))
    bspec = pl.BlockSpec((1, BLK, 2 * BLK), lambda n, i: (n // bias_div, 0, 0))
    sspec = pl.BlockSpec((1, 1, LANES), lambda n, i: (n, 0, 0))
    colspec = pl.BlockSpec((1, BLK, 1), lambda n, i: (n, i, 0))
    return xspecs, qspec, bspec, sspec, colspec


def _band_scores(q_ref, kp_ref, kc_ref, b_ref, first):
    qv = q_ref[0]
    bv = b_ref[0]
    sp = _dot(qv, kp_ref[0], 1, 1) + bv[:, :BLK]
    sp = jnp.where(first, NEG, sp)
    sc = _dot(qv, kc_ref[0], 1, 1) + bv[:, BLK:]
    return sp, sc


def _band_fwd(x, bias, sink, *, nq, offs, g, bias_div, has_sink, name):
    length = x.shape[1]

    def body(q_ref, kp_ref, kc_ref, vp_ref, vc_ref, b_ref, s_ref, o_ref, lse_ref):
        sp, sc = _band_scores(q_ref, kp_ref, kc_ref, b_ref, pl.program_id(1) == 0)
        m = jnp.maximum(jnp.max(sp, axis=1, keepdims=True), jnp.max(sc, axis=1, keepdims=True))
        if has_sink:
            sk = s_ref[0][:, :1]
            m = jnp.maximum(m, sk)
        pp, pc = jnp.exp(sp - m), jnp.exp(sc - m)
        den = jnp.sum(pp, axis=1, keepdims=True) + jnp.sum(pc, axis=1, keepdims=True)
        if has_sink:
            den = den + jnp.exp(sk - m)
        acc = _dot(pp.astype(BF16), vp_ref[0], 1, 0) + _dot(pc.astype(BF16), vc_ref[0], 1, 0)
        o_ref[0] = acc / den
        lse_ref[0] = m + jnp.log(den)

    xspecs, qspec, bspec, sspec, colspec = _band_specs(g, bias_div, offs)
    return pl.pallas_call(
        body, name=name,
        out_shape=[jax.ShapeDtypeStruct((nq, length, HEAD_DIM), F32), jax.ShapeDtypeStruct((nq, length, 1), F32)],
        grid=(nq, length // BLK), in_specs=xspecs + [bspec, sspec],
        out_specs=[qspec, colspec], compiler_params=_params(("parallel", "parallel")),
    )(x, x, x, x, x, bias, sink)


def _band_bwd(x, bias, sink, o, lse, do, dlse, *, nq, offs, g, bias_div, has_sink, name):
    length = x.shape[1]
    nk, nbias = nq // g, nq // bias_div

    def body(q_ref, kp_ref, kc_ref, vp_ref, vc_ref, b_ref, s_ref, o_ref, lse_ref, do_ref, dlse_ref,
             dq_ref, dk_ref, dv_ref, db_ref, dsk_ref):
        n, i = pl.program_id(0), pl.program_id(1)

        @pl.when((n % g == 0) & (i == 0))
        def _():
            dk_ref[...] = jnp.zeros_like(dk_ref)
            dv_ref[...] = jnp.zeros_like(dv_ref)

        @pl.when((n % bias_div == 0) & (i == 0))
        def _():
            db_ref[...] = jnp.zeros_like(db_ref)

        @pl.when(i == 0)
        def _():
            dsk_ref[...] = jnp.zeros_like(dsk_ref)

        sp, sc = _band_scores(q_ref, kp_ref, kc_ref, b_ref, i == 0)
        lse_v = lse_ref[0]
        pp, pc = jnp.exp(sp - lse_v), jnp.exp(sc - lse_v)
        dov = do_ref[0]
        dob = dov.astype(BF16)
        coef = dlse_ref[0] - jnp.sum(dov * o_ref[0], axis=1, keepdims=True)
        dsp = pp * (_dot(dob, vp_ref[0], 1, 1) + coef)
        dsc = pc * (_dot(dob, vc_ref[0], 1, 1) + coef)
        dspb, dscb = dsp.astype(BF16), dsc.astype(BF16)
        dq_ref[0] = (_dot(dspb, kp_ref[0], 1, 0) + _dot(dscb, kc_ref[0], 1, 0)) * (HEAD_DIM ** -0.5)
        qv = q_ref[0]
        cur = pl.ds(pl.multiple_of(i * BLK, BLK), BLK)
        prv = pl.ds(pl.multiple_of(jnp.maximum(i - 1, 0) * BLK, BLK), BLK)
        dk_ref[0, cur, :] += _dot(dscb, qv, 0, 0)
        dk_ref[0, prv, :] += _dot(dspb, qv, 0, 0)
        dv_ref[0, cur, :] += _dot(pc.astype(BF16), dob, 0, 0)
        dv_ref[0, prv, :] += _dot(pp.astype(BF16), dob, 0, 0)
        db_ref[0, :, :BLK] += dsp
        db_ref[0, :, BLK:] += dsc
        if has_sink:
            dsk_ref[0] += jnp.sum(jnp.exp(s_ref[0][:, :1] - lse_v) * coef)

    xspecs, qspec, bspec, sspec, colspec = _band_specs(g, bias_div, offs)
    kvfull = pl.BlockSpec((1, length, HEAD_DIM), lambda n, i: (n // g, 0, 0))
    return pl.pallas_call(
        body, name=name,
        out_shape=[jax.ShapeDtypeStruct((nq, length, HEAD_DIM), F32), jax.ShapeDtypeStruct((nk, length, HEAD_DIM), F32),
                   jax.ShapeDtypeStruct((nk, length, HEAD_DIM), F32), jax.ShapeDtypeStruct((nbias, BLK, 2 * BLK), F32),
                   jax.ShapeDtypeStruct((nq, 1, LANES), F32)],
        grid=(nq, length // BLK),
        in_specs=xspecs + [bspec, sspec, qspec, colspec, qspec, colspec],
        out_specs=[qspec, kvfull, kvfull, bspec, sspec], compiler_params=_params(("arbitrary", "arbitrary")),
    )(x, x, x, x, x, bias, sink, o, lse, do, dlse)


TOK_TILE = 512


def _dil_merge(outs, lses, dout, name):
    tr = TOK_TILE
    dils = [d for _, d in DIL_PATTERNS]
    n = len(dils)
    o4 = [o.reshape(2, d, SEQ // d, HEAD_DIM) for o, d in zip(outs, dils)]
    l4 = [l.reshape(2, d, SEQ // d, 1) for l, d in zip(lses, dils)]
    o_specs = [pl.BlockSpec((2, d, tr // d, HEAD_DIM), lambda i: (0, 0, i, 0)) for d in dils]
    l_specs = [pl.BlockSpec((2, d, tr // d, 1), lambda i: (0, 0, i, 0)) for d in dils]
    tok = pl.BlockSpec((tr, 2 * HEAD_DIM), lambda i: (i, 0))
    scratch = ([pltpu.VMEM((tr, 2 * HEAD_DIM), F32) for _ in dils] + [pltpu.VMEM((tr, 1), F32) for _ in range(2 * n)]
               + [pltpu.VMEM((tr // d, 2 * HEAD_DIM), F32) for d in dils])

    def to_tokens(o_ref, l_ref, d, pair, cols, stage):
        for r in range(d):
            rows = pl.ds(r, tr // d, stride=d) if d > 1 else slice(None)
            stage[:, :HEAD_DIM] = o_ref[0, r]
            stage[:, HEAD_DIM:] = o_ref[1, r]
            pair[rows, :] = stage[...]
            for h in range(2):
                cols[h][rows, :] = l_ref[h, r]
        return pair[...], [cols[0][...], cols[1][...]]

    def weights(ls):
        left = lax.broadcasted_iota(jnp.int32, (tr, 2 * HEAD_DIM), 1) < HEAD_DIM
        per_head = []
        for h in range(2):
            m = ls[0][h]
            for g in range(1, n):
                m = jnp.maximum(m, ls[g][h])
            es = [jnp.exp(ls[g][h] - m) for g in range(n)]
            den = es[0]
            for e in es[1:]:
                den = den + e
            per_head.append([e / den for e in es])
        return per_head, [jnp.where(left, per_head[0][g], per_head[1][g]) for g in range(n)], left

    def load(refs):
        pairs, cols, stages = refs[:n], refs[n:3 * n], refs[3 * n:]
        return pairs, [cols[2 * g:2 * g + 2] for g in range(n)], stages

    if dout is None:
        def body(*refs):
            pairs, cols, stages = load(refs[2 * n + 1:])
            toks = [to_tokens(refs[g], refs[n + g], dils[g], pairs[g], cols[g], stages[g]) for g in range(n)]
            _, alphas, _ = weights([t[1] for t in toks])
            acc = alphas[0] * toks[0][0]
            for g in range(1, n):
                acc = acc + alphas[g] * toks[g][0]
            refs[2 * n][...] = acc

        return pl.pallas_call(
            body, name=name, out_shape=jax.ShapeDtypeStruct((SEQ, 2 * HEAD_DIM), F32), grid=(SEQ // tr,),
            in_specs=o_specs + l_specs, out_specs=tok, scratch_shapes=scratch, compiler_params=_params(("parallel",)),
        )(*o4, *l4)

    def body(*refs):
        do_refs, dl_refs = refs[2 * n + 1:3 * n + 1], refs[3 * n + 1:4 * n + 1]
        pairs, cols, stages = load(refs[4 * n + 1:])
        toks = [to_tokens(refs[g], refs[n + g], dils[g], pairs[g], cols[g], stages[g]) for g in range(n)]
        per_head, alphas, left = weights([t[1] for t in toks])
        dov = refs[2 * n][...]
        das = []
        for g in range(n):
            prod = dov * toks[g][0]
            das.append([jnp.sum(jnp.where(left, prod, 0.0), axis=1, keepdims=True),
                        jnp.sum(jnp.where(left, 0.0, prod), axis=1, keepdims=True)])
        dbar = [sum(per_head[h][g] * das[g][h] for g in range(n)) for h in range(2)]
        for g, d in enumerate(dils):
            pairs[g][...] = alphas[g] * dov
            for h in range(2):
                cols[g][h][...] = per_head[h][g] * (das[g][h] - dbar[h])
            for r in range(d):
                rows = pl.ds(r, tr // d, stride=d) if d > 1 else slice(None)
                v = pairs[g][rows, :]
                for h in range(2):
                    do_refs[g][h, r] = v[:, h * HEAD_DIM:(h + 1) * HEAD_DIM]
                    dl_refs[g][h, r] = cols[g][h][rows, :]

    out = pl.pallas_call(
        body, name=name,
        out_shape=[jax.ShapeDtypeStruct(o.shape, F32) for o in o4] + [jax.ShapeDtypeStruct(l.shape, F32) for l in l4],
        grid=(SEQ // tr,), in_specs=o_specs + l_specs + [tok], out_specs=o_specs + l_specs, scratch_shapes=scratch,
        compiler_params=_params(("parallel",)),
    )(*o4, *l4, dout)
    return [t.reshape(s.shape) for t, s in zip(out, list(outs) + list(lses))]


def _tri(cmp):
    r = lax.broadcasted_iota(jnp.int32, (SB_TILE, SB_TILE), 0)
    c = lax.broadcasted_iota(jnp.int32, (SB_TILE, SB_TILE), 1)
    return cmp(r, c).astype(BF16)


def _cum(x, tri, terms):
    acc, rest = None, x
    for _ in range(terms):
        part = rest.astype(BF16)
        rest = rest - part.astype(F32)
        d = _dot(part, tri, 1, 0)
        acc = d if acc is None else acc + d
    return acc


def _sb_logits(q, k_ref, j, i):
    t = SB_TILE
    ks = k_ref[0, pl.ds(pl.multiple_of(j * t, t), t), :]
    z = _dot(q, ks, 1, 1)
    rows = i * t + lax.broadcasted_iota(jnp.int32, (t, t), 0)
    cols = j * t + lax.broadcasted_iota(jnp.int32, (t, t), 1)
    mask = cols < rows
    e = jnp.exp(-jnp.abs(z))
    lf = jnp.where(mask, -(jnp.maximum(z, 0.0) + jnp.log(1.0 + e)), 0.0)
    return ks, z, e, lf, mask


def _sb_specs(h, s):
    t = SB_TILE
    tile = pl.BlockSpec((1, t, HEAD_DIM), lambda hh, i: (hh, i, 0))
    keys = pl.BlockSpec((1, s, HEAD_DIM), lambda hh, i: (h + hh, 0, 0))
    values = pl.BlockSpec((1, s, HEAD_DIM), lambda hh, i: (2 * h + hh, 0, 0))
    return tile, keys, values


def _sb_fwd(x, name):
    h, s = x.shape[0] // 3, x.shape[1]
    t = SB_TILE

    def body(q_ref, k_ref, v_ref, o_ref, tot_ref):
        i = pl.program_id(1)
        qv = q_ref[0]
        after = _tri(lambda r, c: r > c)

        def step(jj, carry):
            right, acc = carry
            j = i - jj
            _, z, _, lf, mask = _sb_logits(qv, k_ref, j, i)
            between = right + _cum(lf, after, 3)
            w = jnp.where(mask, jnp.exp(z + lf + between), 0.0)
            vs = v_ref[0, pl.ds(pl.multiple_of(j * t, t), t), :]
            return right + jnp.sum(lf, axis=1, keepdims=True), acc + _dot(w.astype(BF16), vs, 1, 0)

        right, acc = lax.fori_loop(0, i + 1, step, (jnp.zeros((t, 1), F32), jnp.zeros((t, HEAD_DIM), F32)))
        o_ref[0] = acc
        tot_ref[0] = right

    tile, keys, values = _sb_specs(h, s)
    return pl.pallas_call(
        body, name=name, out_shape=[jax.ShapeDtypeStruct((h, s, HEAD_DIM), F32), jax.ShapeDtypeStruct((h, s, 1), F32)],
        grid=(h, s // t), in_specs=[tile, keys, values],
        out_specs=[tile, pl.BlockSpec((1, t, 1), lambda hh, i: (hh, i, 0))],
        compiler_params=_params(("parallel", "parallel")),
    )(x, x, x)


def _sb_bwd(x, tot, do, name):
    h, s = x.shape[0] // 3, x.shape[1]
    t = SB_TILE

    def body(q_ref, k_ref, v_ref, tot_ref, do_ref, dq_ref, dk_ref, dv_ref):
        i = pl.program_id(1)

        @pl.when(i == 0)
        def _():
            dk_ref[...] = jnp.zeros_like(dk_ref)
            dv_ref[...] = jnp.zeros_like(dv_ref)

        qv = q_ref[0]
        dob = do_ref[0].astype(BF16)
        total = tot_ref[0]
        upto = _tri(lambda r, c: r <= c)
        before = _tri(lambda r, c: r < c)

        def step(j, carry):
            left, cleft, dq = carry
            ks, z, e, lf, mask = _sb_logits(qv, k_ref, j, i)
            rows = pl.ds(pl.multiple_of(j * t, t), t)
            vs = v_ref[0, rows, :]
            between = total - (left + _cum(lf, upto, 3))
            w = jnp.where(mask, jnp.exp(z + lf + between), 0.0)
            dlog = w * _dot(dob, vs, 1, 1)
            cfail = cleft + _cum(dlog, before, 2)
            sig = jnp.where(z >= 0.0, 1.0, e) / (1.0 + e)
            dz = jnp.where(mask, dlog * (1.0 - sig) - sig * cfail, 0.0).astype(BF16)
            dk_ref[0, rows, :] += _dot(dz, qv, 0, 0)
            dv_ref[0, rows, :] += _dot(w.astype(BF16), dob, 0, 0)
            return (left + jnp.sum(lf, axis=1, keepdims=True), cleft + jnp.sum(dlog, axis=1, keepdims=True),
                    dq + _dot(dz, ks, 1, 0))

        zero = jnp.zeros((t, 1), F32)
        _, _, dq = lax.fori_loop(0, i + 1, step, (zero, zero, jnp.zeros((t, HEAD_DIM), F32)))
        dq_ref[0] = dq * (HEAD_DIM ** -0.5)

    tile, keys, values = _sb_specs(h, s)
    full = pl.BlockSpec((1, s, HEAD_DIM), lambda hh, i: (hh, 0, 0))
    shp = jax.ShapeDtypeStruct((h, s, HEAD_DIM), F32)
    return pl.pallas_call(
        body, name=name, out_shape=[shp, shp, shp], grid=(h, s // t),
        in_specs=[tile, keys, values, pl.BlockSpec((1, t, 1), lambda hh, i: (hh, i, 0)), tile],
        out_specs=[tile, full, full], compiler_params=_params(("arbitrary", "arbitrary")),
    )(x, x, x, tot, do)


COL_SB, COL_DIL, COL_SWA = 0, 3 * H_SB * HEAD_DIM, 3 * H_SB * HEAD_DIM + 3 * H_DIL * HEAD_DIM
N_SWA = H_SWA_Q + 2 * H_SWA_KV


def _dil_col(t, g):
    return COL_DIL + t * H_DIL * HEAD_DIM + g * 2 * HEAD_DIM


def _split_heads(qkv, name):
    tr = TOK_TILE
    scale = HEAD_DIM ** -0.5
    dils = [d for _, d in DIL_PATTERNS]

    def body(x_ref, sb_ref, d0_ref, d1_ref, d2_ref, swa_ref, pair):
        def head(col, scaled):
            v = x_ref[:, col:col + HEAD_DIM]
            return (v * scale if scaled else v).astype(BF16)

        for hh in range(3 * H_SB):
            sb_ref[hh] = head(COL_SB + hh * HEAD_DIM, hh < H_SB)
        for hh in range(N_SWA):
            swa_ref[hh] = head(COL_SWA + hh * HEAD_DIM, hh < H_SWA_Q)
        for t in range(3):
            for g, (d, out_ref) in enumerate(zip(dils, (d0_ref, d1_ref, d2_ref))):
                col = _dil_col(t, g)
                if d == 1:
                    for h in range(2):
                        out_ref[t * 2 + h] = head(col + h * HEAD_DIM, t == 0)
                    continue
                pair[...] = x_ref[:, col:col + 2 * HEAD_DIM]
                for r in range(d):
                    v = pair[pl.ds(r, tr // d, stride=d), :]
                    v = v * scale if t == 0 else v
                    for h in range(2):
                        out_ref[t * 2 * d + h * d + r] = v[:, h * HEAD_DIM:(h + 1) * HEAD_DIM].astype(BF16)

    def heads(n, length):
        return jax.ShapeDtypeStruct((n, length, HEAD_DIM), BF16)

    def spec(n, rows):
        return pl.BlockSpec((n, rows, HEAD_DIM), lambda i: (0, i, 0))

    return pl.pallas_call(
        body, name=name,
        out_shape=[heads(3 * H_SB, SEQ)] + [heads(6 * d, SEQ // d) for d in dils] + [heads(N_SWA, SEQ)],
        grid=(SEQ // tr,), in_specs=[pl.BlockSpec((tr, D_QKV), lambda i: (i, 0))],
        out_specs=[spec(3 * H_SB, tr)] + [spec(6 * d, tr // d) for d in dils] + [spec(N_SWA, tr)],
        scratch_shapes=[pltpu.VMEM((tr, 2 * HEAD_DIM), F32)], compiler_params=_params(("parallel",)),
    )(qkv)


def _join_heads(sb, dil, swa, name):
    tr = TOK_TILE
    dils = [d for _, d in DIL_PATTERNS]

    def body(*refs):
        sb_refs, dil_refs, swa_refs = refs[:3], [refs[3 + 3 * g:6 + 3 * g] for g in range(3)], refs[12:15]
        o_ref, pair, stages = refs[15], refs[16], refs[17:]

        def put(col, v):
            o_ref[:, col:col + v.shape[1]] = v.astype(BF16)

        for t in range(3):
            for h in range(H_SB):
                put(COL_SB + (t * H_SB + h) * HEAD_DIM, sb_refs[t][h])
        col = COL_SWA
        for ref in swa_refs:
            for h in range(ref.shape[0]):
                put(col, ref[h])
                col += HEAD_DIM
        for t in range(3):
            for g, d in enumerate(dils):
                ref, col = dil_refs[g][t], _dil_col(t, g)
                if d == 1:
                    for h in range(2):
                        put(col + h * HEAD_DIM, ref[h])
                    continue
                stage = stages[g - 1]
                for r in range(d):
                    stage[:, :HEAD_DIM] = ref[r]
                    stage[:, HEAD_DIM:] = ref[d + r]
                    pair[pl.ds(r, tr // d, stride=d), :] = stage[...]
                put(col, pair[...])

    def spec(n, rows):
        return pl.BlockSpec((n, rows, HEAD_DIM), lambda i: (0, i, 0))

    ins = list(sb) + [t for g in range(3) for t in dil[g]] + list(swa)
    in_specs = ([spec(H_SB, tr)] * 3 + [spec(2 * d, tr // d) for d in dils for _ in range(3)]
                + [spec(H_SWA_Q, tr), spec(H_SWA_KV, tr), spec(H_SWA_KV, tr)])
    return pl.pallas_call(
        body, name=name, out_shape=jax.ShapeDtypeStruct((SEQ, D_QKV), BF16), grid=(SEQ // tr,), in_specs=in_specs,
        out_specs=pl.BlockSpec((tr, D_QKV), lambda i: (i, 0)),
        scratch_shapes=[pltpu.VMEM((tr, 2 * HEAD_DIM), F32)] + [pltpu.VMEM((tr // d, 2 * HEAD_DIM), F32) for d in dils[1:]],
        compiler_params=_params(("parallel",)),
    )(*ins)


def _mixer_fwd(qkv, bias, sinks_l, tag):
    sb, d0, d1, d2, swa = _split_heads(qkv, name=f"split_heads_{tag}")
    st = {"sb": sb, "dil": (d0, d1, d2), "swa": swa}
    o_sb, st["sb_tot"] = _sb_fwd(sb, name=f"sb_fwd_{tag}")
    st["dil_out"], st["dil_lse"], st["dil_sink"] = [], [], []
    for gi, (_, d) in enumerate(DIL_PATTERNS):
        sink = jnp.zeros((2 * d, 1, LANES), F32)
        og, lg = _band_fwd(st["dil"][gi], bias[2 * gi:2 * gi + 2], sink, nq=2 * d, offs=(0, 2 * d, 4 * d), g=1, bias_div=d,
                           has_sink=False, name=f"dil{gi}_fwd_{tag}")
        st["dil_out"].append(og)
        st["dil_lse"].append(lg)
        st["dil_sink"].append(sink)
    o_dil = _dil_merge(st["dil_out"], st["dil_lse"], None, name=f"dil_merge_fwd_{tag}")
    st["swa_sink"] = jnp.broadcast_to(sinks_l.reshape(H_SWA_Q, 1, 1), (H_SWA_Q, 1, LANES))
    st["swa_out"] = _band_fwd(swa, bias[H_DIL:], st["swa_sink"], nq=H_SWA_Q, offs=(0, H_SWA_Q, H_SWA_Q + H_SWA_KV),
                              g=H_SWA_Q // H_SWA_KV, bias_div=1, has_sink=True, name=f"swa_fwd_{tag}")
    return (o_sb, o_dil, st["swa_out"][0]), st


def _mixer_bwd(st, bias, do_sb, do_dil, do_swa, tag):
    d_sb = _sb_bwd(st["sb"], st["sb_tot"], do_sb, name=f"sb_bwd_{tag}")
    dmerge = _dil_merge(st["dil_out"], st["dil_lse"], do_dil, name=f"dil_merge_bwd_{tag}")
    d_dil, dbs = [], []
    for gi, (_, d) in enumerate(DIL_PATTERNS):
        dq, dk, dv, db, _ = _band_bwd(st["dil"][gi], bias[2 * gi:2 * gi + 2], st["dil_sink"][gi], st["dil_out"][gi],
                                      st["dil_lse"][gi], dmerge[gi], dmerge[3 + gi], nq=2 * d, offs=(0, 2 * d, 4 * d),
                                      g=1, bias_div=d, has_sink=False, name=f"dil{gi}_bwd_{tag}")
        d_dil.append((dq, dk, dv))
        dbs.append(db)
    o_sw, l_sw = st["swa_out"]
    dq_sw, dk_sw, dv_sw, db_sw, dsink = _band_bwd(st["swa"], bias[H_DIL:], st["swa_sink"], o_sw, l_sw, do_swa,
                                                  jnp.zeros_like(l_sw), nq=H_SWA_Q, offs=(0, H_SWA_Q, H_SWA_Q + H_SWA_KV),
                                                  g=H_SWA_Q // H_SWA_KV, bias_div=1, has_sink=True, name=f"swa_bwd_{tag}")
    dqkv = _join_heads(d_sb, d_dil, (dq_sw, dk_sw, dv_sw), name=f"join_heads_{tag}")
    return dqkv, jnp.concatenate(dbs + [db_sw], 0), dsink[:, 0, 0]


PIECES = ("ffn0", "mix", "ffn1")


def _ffn_fwd(x_in, w, gain, mod_j, tag, after=None):
    st = {"x": x_in, "w": w}
    st["h"] = _norm_fwd(x_in, _row(gain), _row(mod_j[1]), _row(mod_j[0]), name=f"norm_fwd_{tag}", after=after)
    st["a"], st["u"], st["s"] = _ffn_up(st["h"], w["gate"], w["up"], name=f"up_{tag}")
    st["f"], x_out = _mm(st["s"], w["down"], res=x_in, colscale=_row(0.5 * mod_j[2]), emit_acc=True, tm=256, tn=1024,
                         name=f"down_{tag}")
    return x_out, st


def _ffn_bwd(dx_out, st, gain, mod_j, tag, done):
    w = st["w"]

    def latest(new, old):
        return old if new is None else new

    df, dgate = _gate_bwd(dx_out, st["f"], _row(0.5 * mod_j[2]), 0.5, name=f"gate_bwd_{tag}")
    token = done({"down": _mm_tn(st["s"], df, name=f"dwd_{tag}")})
    da, du = _ffn_bwd_ds(df, w["down"], st["a"], st["u"], name=f"ds_{tag}")
    token = latest(done({"gate": _mm_tn(da, st["h"], after=token, name=f"dwg_{tag}")}), token)
    token = latest(done({"up": _mm_tn(du, st["h"], after=token, name=f"dwu_{tag}")}), token)
    dh = _mm2(da, w["gate"], du, w["up"], after=token, name=f"dh_{tag}")
    dx_in, sum_dh, sum_dhx = _norm_bwd(st["x"], dh, dx_out, _row(gain), _row(mod_j[1]), name=f"norm_bwd_{tag}")
    dmod = jnp.concatenate([sum_dh, gain * sum_dhx, dgate], 0)
    return dx_in, dmod, (1.0 + mod_j[1]) * sum_dhx[0]


def _mix_fwd(x_in, w, gain, mod_j, bias, sinks_l, tag, after=None):
    st = {"x": x_in, "w": w}
    st["h"] = _norm_fwd(x_in, _row(gain), _row(mod_j[1]), _row(mod_j[0]), name=f"norm_fwd_mix_{tag}", after=after)
    qkv = _mm(st["h"], w["qkv"], tb=True, name=f"qkv_{tag}")
    st["gates"] = _mm(st["h"], w["gates"], tb=True, name=f"gates_{tag}")
    outs, st["mix"] = _mixer_fwd(qkv, bias, sinks_l, tag)
    st["merged"], *st["t"] = _merge_fwd(*outs, st["gates"], w["br_sb"], w["br_dil"], w["br_swa"], name=f"merge_fwd_{tag}")
    st["f"], x_out = _mm(st["merged"], w["out"], res=x_in, colscale=_row(mod_j[2]), emit_acc=True, name=f"out_{tag}")
    return x_out, st


def _mix_bwd(dx_out, st, gain, mod_j, bias, tag, done):
    w = st["w"]
    df, dgate = _gate_bwd(dx_out, st["f"], _row(mod_j[2]), 1.0, name=f"gate_bwd_mix_{tag}")
    g = {"out": _mm_tn(st["merged"], df, name=f"dw_out_{tag}")}
    dmerged = _mm(df, w["out"], tb=True, name=f"dmerged_{tag}")
    dgates, do_sb, do_dil, do_swa, dbr_sb, dbr_dil, dbr_swa = _merge_bwd(
        dmerged, *st["t"], st["gates"], w["br_sb"], w["br_dil"], w["br_swa"], name=f"merge_bwd_{tag}")
    g["br_sb"] = _mm_tn(st["t"][0], dbr_sb, name=f"dw_br_sb_{tag}")
    g["br_dil"] = _mm_tn(st["t"][1], dbr_dil, name=f"dw_br_dil_{tag}")
    g["br_swa"] = _mm_tn(st["t"][2], dbr_swa, name=f"dw_br_swa_{tag}")
    dqkv, dbias, dsinks = _mixer_bwd(st["mix"], bias, do_sb, do_dil, do_swa, tag)
    g["qkv"] = _mm_tn(dqkv, st["h"], name=f"dw_qkv_{tag}")
    g["gates"] = _mm_tn(dgates, st["h"], name=f"dw_gates_{tag}")
    dh = _mm2(dqkv, w["qkv"], dgates, w["gates"], after=done(g), tm=128, name=f"dh_mix_{tag}")
    dx_in, sum_dh, sum_dhx = _norm_bwd(st["x"], dh, dx_out, _row(gain), _row(mod_j[1]), name=f"norm_bwd_mix_{tag}")
    dmod = jnp.concatenate([sum_dh, gain * sum_dhx, dgate], 0)
    return dx_in, dmod, (1.0 + mod_j[1]) * sum_dhx[0], dbias, dsinks


def _local_step(x, target, mod, gains, weights_of, rel_bias, sinks, final_gain, grads_done):
    tables = jnp.asarray(_bucket_tables())
    bias = _bias_build(rel_bias, tables, name="bias_build")
    states, h = [], x
    for l in range(DEPTH):
        st = {}
        for j, piece in enumerate(PIECES):
            w, after = weights_of(l, piece, h)
            if piece == "mix":
                h, st[piece] = _mix_fwd(h, w, gains[l, j], mod[l, j], bias, sinks[l], f"l{l}", after)
            else:
                h, st[piece] = _ffn_fwd(h, w, gains[l, j], mod[l, j], f"{piece}_l{l}", after)
        states.append(st)
    loss, dx, dfinal = _final_loss(h, target, _row(final_gain), name="final_loss")
    dmods = [[None] * 3 for _ in range(DEPTH)]
    dgains = [[None] * 3 for _ in range(DEPTH)]
    dsinks = [None] * DEPTH
    dbias = None
    for l in reversed(range(DEPTH)):
        for j in reversed(range(3)):
            piece = PIECES[j]
            done = lambda grads, l=l, piece=piece: grads_done(l, piece, grads)
            if piece == "mix":
                dx, dmods[l][j], dgains[l][j], db, dsinks[l] = _mix_bwd(dx, states[l][piece], gains[l, j], mod[l, j], bias, f"l{l}", done)
                dbias = db if dbias is None else dbias + db
            else:
                dx, dmods[l][j], dgains[l][j] = _ffn_bwd(dx, states[l][piece], gains[l, j], mod[l, j], f"{piece}_l{l}", done)
    drel = _bias_grad(dbias, tables, name="bias_grad")[:, 0, :N_BUCKETS].T
    dmod = jnp.stack([jnp.stack(m) for m in dmods])
    dgain = jnp.stack([jnp.stack(g) for g in dgains])
    return loss, dx, dmod, dgain, dfinal[0], drel, jnp.stack(dsinks)


BR_ROWS = (H_SB * HEAD_DIM, 2 * HEAD_DIM, H_SWA_Q * HEAD_DIM)


def _lanes_unshard(g, lead):
    _, rows, _ = g.shape
    r = rows // lead
    return g.reshape(N_DEV, lead, r, LANES).transpose(1, 2, 0, 3).reshape(lead, r, N_DEV * LANES)


def _lanes_shard(full):
    lead, r, _ = full.shape
    return full.reshape(lead, r, N_DEV, LANES).transpose(2, 0, 1, 3).reshape(N_DEV, lead * r, LANES)


def _pack_rows(parts, dtype):
    flat = jnp.concatenate([p.astype(dtype).reshape(-1) for p in parts])
    pad = (-flat.shape[0]) % (16 * LANES)
    if pad:
        flat = jnp.concatenate([flat, jnp.zeros((pad,), dtype)])
    return flat.reshape(-1, LANES)


def _unshard(gathered, axis):
    moved = jnp.moveaxis(gathered, 0, axis)
    shape = list(moved.shape)
    shape[axis:axis + 2] = [shape[axis] * shape[axis + 1]]
    return moved.reshape(shape)


def kernel(x, c, w_ada, b_ada, norm_gain, w_ffn_gate, w_ffn_up, w_ffn_down, w_in, w_br_sb, w_br_dil, w_br_swa, w_out, sinks, rel_bias, final_gain, loss_target, m_w_ada, m_b_ada, m_norm_gain, m_w_ffn_gate, m_w_ffn_up, m_w_ffn_down, m_w_in, m_w_br_sb, m_w_br_dil, m_w_br_swa, m_w_out, m_sinks, m_rel_bias, m_final_gain, v_w_ada, v_b_ada, v_norm_gain, v_w_ffn_gate, v_w_ffn_up, v_w_ffn_down, v_w_in, v_w_br_sb, v_w_br_dil, v_w_br_swa, v_w_out, v_sinks, v_rel_bias, v_final_gain):
    me = 4 * lax.axis_index("x") + 2 * lax.axis_index("y") + lax.axis_index("c")
    d = D_MODEL
    gate_t, up_t, in_t = jnp.swapaxes(w_ffn_gate, 2, 3), jnp.swapaxes(w_ffn_up, 2, 3), jnp.swapaxes(w_in, 1, 2)

    def piece_shards(l, piece):
        bf = lambda t: t.astype(BF16)
        if piece == "mix":
            return [bf(in_t[l]), jnp.concatenate([bf(w_br_sb[l]), bf(w_br_dil[l]), bf(w_br_swa[l])], 0), bf(w_out[l])]
        i = PIECES.index(piece) // 2
        return [bf(gate_t[l, i]), bf(up_t[l, i]), bf(w_ffn_down[l, i])]

    br_off = np.concatenate([[0], np.cumsum(BR_ROWS)])

    def piece_weights(gathered, piece):
        if piece == "mix":
            g_in, g_br, g_out = gathered
            f_in = g_in.reshape(D_QKV + D_GATES, d)
            f_br = [_lanes_unshard(g_br[:, br_off[k]:br_off[k + 1]], 1)[0] for k in range(3)]
            return {"qkv": f_in[:D_QKV], "gates": f_in[D_QKV:], "br_sb": f_br[0], "br_dil": f_br[1], "br_swa": f_br[2],
                    "out": g_out.reshape(d, d)}
        return {n: g.reshape(D_FF, d) for n, g in zip(("gate", "up", "down"), gathered)}

    small, = _all_gather([_pack_rows([c, norm_gain], F32)], name="gather_cond")
    c_all = small[:, :d // LANES].reshape(N_DEV, d)
    gains = _unshard(small[:, d // LANES:d // LANES + 6].reshape(N_DEV, DEPTH, 3, LANES), 2)

    cols = w_ada.shape[2]
    mod_cols = jnp.stack([_ada_fwd(c_all, w_ada[l], name=f"ada_fwd_l{l}") for l in range(DEPTH)])
    mod_all, = _all_gather([_pack_rows([mod_cols], F32)], name="gather_mod")
    mod_all = mod_all.reshape(N_DEV, -1)[:, :DEPTH * N_DEV * cols].reshape(N_DEV, DEPTH, N_DEV, cols)
    mod_mine = lax.dynamic_index_in_dim(mod_all, me, axis=2, keepdims=False)
    mod = (mod_mine.transpose(1, 0, 2).reshape(DEPTH, N_DEV * cols) + b_ada).reshape(DEPTH, 3, 3, d)

    order = [(l, piece) for l in range(DEPTH) for piece in PIECES]
    eager, ahead = 2, 3
    in_flight = {}
    n_tensors = 3
    first = _all_gather([s for k in range(eager) for s in piece_shards(*order[k])], after=mod_all, name="gather_first")

    def start_gather(k, after):
        l, piece = order[k]
        in_flight[k], token = _exchange_start(piece_shards(l, piece), after, gather=True, name=f"gather_{piece}_l{l}_start")
        return token

    token = first[0]
    for k in range(eager, eager + ahead - 1):
        token = start_gather(k, token)
    mod = mod + token[0, 0]

    def weights_of(l, piece, h):
        k = order.index((l, piece))
        started = eager <= k + ahead < len(order) and k + ahead not in in_flight
        token = start_gather(k + ahead, h) if started else None
        if k < eager:
            return piece_weights(first[n_tensors * k:n_tensors * (k + 1)], piece), token
        landed = _exchange_wait(in_flight[k], h if token is None else token, gather=True, name=f"gather_{piece}_l{l}_wait")
        return piece_weights(landed, piece), token

    exchanges, have = {}, {}

    def grads_done(l, piece, g):
        key = (l, piece)
        have.setdefault(key, {}).update(g)
        if piece == "mix":
            if len(have[key]) < 6:
                return None
            g = have[key]
            s_br = jnp.concatenate([_lanes_shard(g[n][None]) for n in ("br_sb", "br_dil", "br_swa")], 1)
            groups = [(("in", "br", "out"), [jnp.concatenate([g["qkv"], g["gates"]], 0).reshape(N_DEV, -1, d), s_br,
                                             g["out"].reshape(N_DEV, -1, d)])]
        elif key == order[0]:
            groups = [((n,), [t.reshape(N_DEV, -1, d)]) for n, t in g.items()]
        elif len(have[key]) < 3:
            return None
        else:
            groups = [(("gate", "up", "down"), [have[key][n].reshape(N_DEV, -1, d) for n in ("gate", "up", "down")])]
        token = None
        for names, sg in groups:
            state, token = _exchange_start(sg, sg[0], gather=False, name=f"exchange_{piece}_l{l}_{names[0]}_start")
            exchanges.setdefault(key, []).append((names, state))
        return token

    loss, dx, dmod, dgains, dfinal, drel, dsinks = _local_step(
        x[0], loss_target[0], mod, gains, weights_of, rel_bias, sinks, final_gain, grads_done)

    small_parts = [dmod, dgains, dfinal, drel.T, dsinks, loss[0, :1]]
    small_sizes = [int(np.prod(p.shape)) for p in small_parts]
    small_all, = _all_gather([_pack_rows(small_parts, F32)], name="gather_small")
    small_sum = _sum_parts([small_all], name="sum_small").reshape(-1)
    offs = np.concatenate([[0], np.cumsum(small_sizes)])
    g_b_ada = small_sum[offs[0]:offs[1]].reshape(DEPTH, 9 * d)
    g_gain_full = small_sum[offs[1]:offs[2]].reshape(DEPTH, 3, d)
    g_norm_gain = lax.dynamic_slice_in_dim(g_gain_full, me * LANES, LANES, axis=2)
    g_final = small_sum[offs[2]:offs[3]]
    g_rel = small_sum[offs[3]:offs[4]].reshape(N_SOFT, N_BUCKETS).T
    g_sinks = small_sum[offs[4]:offs[5]].reshape(DEPTH, H_SWA_Q)
    loss_total = small_sum[offs[5]]

    dmod_all = small_all.reshape(N_DEV, -1)[:, :DEPTH * 9 * d].reshape(N_DEV, DEPTH, 9 * d)
    dmod_cols = lax.dynamic_slice_in_dim(dmod_all, me * cols, cols, axis=2)
    g_w_ada = jnp.stack([_ada_bwd(c_all.T, dmod_cols[:, l], name=f"ada_bwd_l{l}") for l in range(DEPTH)])

    state = {"w_ada": (w_ada, m_w_ada, v_w_ada), "b_ada": (b_ada, m_b_ada, v_b_ada),
             "norm_gain": (norm_gain, m_norm_gain, v_norm_gain), "w_ffn_gate": (w_ffn_gate, m_w_ffn_gate, v_w_ffn_gate),
             "w_ffn_up": (w_ffn_up, m_w_ffn_up, v_w_ffn_up), "w_ffn_down": (w_ffn_down, m_w_ffn_down, v_w_ffn_down),
             "w_in": (w_in, m_w_in, v_w_in), "w_br_sb": (w_br_sb, m_w_br_sb, v_w_br_sb),
             "w_br_dil": (w_br_dil, m_w_br_dil, v_w_br_dil), "w_br_swa": (w_br_swa, m_w_br_swa, v_w_br_swa),
             "w_out": (w_out, m_w_out, v_w_out), "sinks": (sinks, m_sinks, v_sinks),
             "rel_bias": (rel_bias, m_rel_bias, v_rel_bias), "final_gain": (final_gain, m_final_gain, v_final_gain)}
    grad, update = {}, {}

    def adamw(n, g, transposed=False):
        w, m, v = (jnp.swapaxes(t, -1, -2) for t in state[n]) if transposed else state[n]
        if w.ndim == 1:
            out = tuple(t.reshape(w.shape) for t in _adamw(_row(w), _row(g), _row(m), _row(v), name=f"adamw_{n}"))
        else:
            out = _adamw(w, g, m, v, name=f"adamw_{n}")
        if transposed:
            grad[n], update[n] = jnp.swapaxes(g, -1, -2), tuple(jnp.swapaxes(t, -1, -2) for t in out)
        else:
            grad[n], update[n] = g, out

    for n, g in (("w_ada", g_w_ada), ("b_ada", g_b_ada), ("norm_gain", g_norm_gain), ("sinks", g_sinks),
                 ("rel_bias", g_rel), ("final_gain", g_final)):
        adamw(n, g)

    after = update["w_ada"][0]
    parts = {}
    for key in reversed(order):
        for names, ex_state in exchanges[key]:
            landed = _exchange_wait(ex_state, after, gather=False, name=f"exchange_{key[1]}_l{key[0]}_{names[0]}_wait")
            parts.setdefault(key, {}).update(zip(names, landed))
            after = landed[0]
    ffn_keys = [key for key in order if key[1] != "mix"]
    mix_keys = [key for key in order if key[1] == "mix"]
    sums = {n: _sum_parts([parts[key][n] for key in ffn_keys], name=f"sum_grads_{n}") for n in ("gate", "up", "down")}
    sums.update({n: _sum_parts([parts[key][n] for key in mix_keys], name=f"sum_grads_{n}") for n in ("in", "br", "out")})
    br_sums = sums["br"].reshape(DEPTH, -1, LANES)
    adamw("w_ffn_gate", sums["gate"].reshape(gate_t.shape), transposed=True)
    adamw("w_ffn_up", sums["up"].reshape(up_t.shape), transposed=True)
    adamw("w_ffn_down", sums["down"].reshape(w_ffn_down.shape))
    adamw("w_in", sums["in"].reshape(in_t.shape), transposed=True)
    adamw("w_br_sb", br_sums[:, br_off[0]:br_off[1]])
    adamw("w_br_dil", br_sums[:, br_off[1]:br_off[2]])
    adamw("w_br_swa", br_sums[:, br_off[2]:br_off[3]])
    adamw("w_out", sums["out"].reshape(w_out.shape))

    names = ["w_ada", "b_ada", "norm_gain", "w_ffn_gate", "w_ffn_up", "w_ffn_down", "w_in", "w_br_sb", "w_br_dil",
             "w_br_swa", "w_out", "sinks", "rel_bias", "final_gain"]
    return (loss_total, dx[None], *[grad[n] for n in names], *[update[n][0] for n in names],
            *[update[n][1] for n in names], *[update[n][2] for n in names])
```

```python
import math

import numpy as np
import jax
import jax.numpy as jnp
from jax import lax
from jax.experimental import pallas as pl
from jax.experimental.pallas import tpu as pltpu

F32, BF16 = jnp.float32, jnp.bfloat16

SEQ, D_MODEL, D_FF, HEAD_DIM = 2048, 1024, 2816, 64
DEPTH = 2
BLK = 128
H_SB, H_DIL, H_SWA_Q, H_SWA_KV = 4, 6, 6, 2
DIL_PATTERNS = ((128, 1), (512, 4), (2048, 16))
SWA_WINDOW = 128
N_BUCKETS, MAX_REL_DIST = 32, 2048
RMS_EPS = 1e-6
D_QKV = 2560
D_GATES = 3 * D_MODEL
ADAM_LR, ADAM_B1, ADAM_B2, ADAM_EPS, ADAM_WD, ADAM_STEP = 0.001, 0.9, 0.999, 1e-08, 0.01, 10

N_DEV = 8
LANES = 128
NEG = -1e30
SB_TILE = 256
VMEM_LIMIT_BYTES = 48 * 1024 * 1024
HBM = pl.BlockSpec(memory_space=pltpu.HBM)
MESH = pl.DeviceIdType.MESH


def _tile(n, target):
    t = (min(n, target) // LANES) * LANES
    while t >= LANES:
        if n % t == 0:
            return t
        t -= LANES
    return n


def _row_tile(r, cap):
    t = (min(r, cap) // 16) * 16
    while t > 16 and r % t:
        t -= 16
    return t


def _params(semantics=None):
    return pltpu.CompilerParams(dimension_semantics=semantics, vmem_limit_bytes=VMEM_LIMIT_BYTES)


def _dot(a, b, ca, cb):
    return lax.dot_general(a, b, (((ca,), (cb,)), ((), ())), preferred_element_type=F32)


def _sigmoid(a):
    return 1.0 / (1.0 + jnp.exp(-a))


def _row(v):
    return v.reshape(1, -1)


def _all_gather(arrs, name, after=None):
    n = len(arrs)
    ins = list(arrs) + ([] if after is None else [after])

    def body(*refs):
        x_refs, out_refs = refs[:n], refs[len(ins):len(ins) + n]
        send_sems, recv_sems, local_sems = refs[len(ins) + n:]
        x, y, c = lax.axis_index("x"), lax.axis_index("y"), lax.axis_index("c")
        me, sibling = (x, y, c), (x, y, 1 - c)
        chips = [(1 - x, y), (x, 1 - y), (1 - x, 1 - y)]

        def slot(t, px, py, pc):
            return out_refs[t].at[4 * px + 2 * py + pc]

        def copy(t, k, block, to, src=None):
            return pltpu.make_async_remote_copy(
                src_ref=slot(t, *block) if src is None else src, dst_ref=slot(t, *block),
                send_sem=send_sems.at[7 * t + k], recv_sem=recv_sems.at[7 * t + k], device_id=to, device_id_type=MESH)

        mine = [pltpu.make_async_copy(x_refs[t], slot(t, *me), local_sems.at[t]) for t in range(n)]
        for cp in mine:
            cp.start()
        first = []
        for t in range(n):
            first.append(copy(t, 0, me, sibling, src=x_refs[t]))
            first += [copy(t, 1 + j, me, (*chip, c), src=x_refs[t]) for j, chip in enumerate(chips)]
        for cp in first:
            cp.start()
        passed = []
        for j, chip in enumerate(chips):
            for t in range(n):
                copy(t, 1 + j, (*chip, c), me).wait_recv()
                passed.append(copy(t, 4 + j, (*chip, c), sibling))
                passed[-1].start()
        for t in range(n):
            copy(t, 0, sibling, me).wait_recv()
        for j, chip in enumerate(chips):
            for t in range(n):
                copy(t, 4 + j, (*chip, 1 - c), me).wait_recv()
        for cp in first + passed:
            cp.wait_send()
        for cp in mine:
            cp.wait()

    return pl.pallas_call(
        body, name=name, out_shape=[jax.ShapeDtypeStruct((N_DEV,) + a.shape, a.dtype) for a in arrs],
        in_specs=[HBM] * n + [pl.BlockSpec(memory_space=pl.ANY)] * (len(ins) - n), out_specs=[HBM] * n,
        scratch_shapes=[pltpu.SemaphoreType.DMA((7 * n,)), pltpu.SemaphoreType.DMA((7 * n,)), pltpu.SemaphoreType.DMA((n,))],
    )(*ins)


def _direct_copies(x_refs, land_refs, send_sems, recv_sems, local_sems, gather):
    x, y, c = lax.axis_index("x"), lax.axis_index("y"), lax.axis_index("c")
    me = 4 * x + 2 * y + c
    sends, recvs = [], []
    for k in range(1, N_DEV):
        px = 1 - x if (k >> 2) & 1 else x
        py = 1 - y if (k >> 1) & 1 else y
        pc = 1 - c if k & 1 else c
        peer = 4 * px + 2 * py + pc
        for t, (x_ref, land_ref) in enumerate(zip(x_refs, land_refs)):
            sem = 7 * t + k - 1
            for out, src, slot in ((sends, x_ref if gather else x_ref.at[peer], me),
                                   (recvs, x_ref if gather else x_ref.at[me], peer)):
                out.append(pltpu.make_async_remote_copy(
                    src_ref=src, dst_ref=land_ref.at[slot], send_sem=send_sems.at[sem], recv_sem=recv_sems.at[sem],
                    device_id=(px, py, pc), device_id_type=MESH))
    own = [pltpu.make_async_copy(x_ref if gather else x_ref.at[me], land_ref.at[me], local_sems.at[t])
           for t, (x_ref, land_ref) in enumerate(zip(x_refs, land_refs))]
    return sends, recvs, own


SEM =pl.BlockSpec(memory_space=pltpu.SEMAPHORE)
ANY = pl.BlockSpec(memory_space=pl.ANY)
SIDE_EFFECT = pltpu.SideEffectType.DATAFLOW_SIDE_EFFECTING


def _exchange_start(arrs, after, *, gather, name):
    n = len(arrs)
    lands = [lax.empty(((N_DEV,) + a.shape) if gather else a.shape, a.dtype) for a in arrs]

    def body(*refs):
        sends, _, own = _direct_copies(refs[:n], refs[n:2 * n], *refs[2 * n + 1:2 * n + 4], gather)
        for cp in own + sends:
            cp.start()
        refs[-1][...] = jnp.zeros_like(refs[-1])

    ops = [pltpu.with_memory_space_constraint(a, pltpu.HBM) for a in list(arrs) + lands]
    out = pl.pallas_call(
        body, name=name,
        out_shape=(pltpu.SemaphoreType.DMA((7 * n,)), pltpu.SemaphoreType.DMA((7 * n,)), pltpu.SemaphoreType.DMA((n,)),
                   *[pltpu.HBM(a.shape, a.dtype) for a in ops], jax.ShapeDtypeStruct((8, LANES), F32)),
        in_specs=[HBM] * (2 * n) + [ANY],
        out_specs=(SEM, SEM, SEM, *[HBM] * (2 * n), pl.BlockSpec(memory_space=pltpu.VMEM)),
        input_output_aliases={t: 3 + t for t in range(2 * n)},
        compiler_params=pltpu.CompilerParams(has_side_effects=SIDE_EFFECT),
    )(*ops, after)
    return (out[:3], out[3:3 + n], out[3 + n:3 + 2 * n]), out[-1]


def _exchange_wait(state, after, *, gather, name):
    sems, arrs, lands = state
    n = len(arrs)

    def body(*refs):
        sends, recvs, own = _direct_copies(refs[:n], refs[n:2 * n], *refs[2 * n:2 * n + 3], gather)
        for cp in own:
            cp.wait()
        for cp in sends:
            cp.wait_send()
        for cp in recvs:
            cp.wait_recv()

    out = pl.pallas_call(
        body, name=name, out_shape=tuple(pltpu.HBM(a.shape, a.dtype) for a in list(arrs) + list(lands)),
        in_specs=[HBM] * (2 * n) + [SEM, SEM, SEM, ANY], out_specs=tuple([HBM] * (2 * n)),
        input_output_aliases={t: t for t in range(2 * n)},
        compiler_params=pltpu.CompilerParams(has_side_effects=SIDE_EFFECT),
    )(*arrs, *lands, *sems, after)
    return out[n:]


def _sum_parts(groups, name):
    n, r, cdim = groups[0].shape
    tr = _row_tile(r, max(16, (1 << 21) // (n * cdim * groups[0].dtype.itemsize)))
    steps = r // tr

    def body(*refs):
        o_ref = refs[-1]
        gg = pl.program_id(0)
        for gi in range(len(groups)):
            @pl.when(gg == gi)
            def _(gi=gi):
                acc = refs[gi][0].astype(F32)
                for k in range(1, n):
                    acc = acc + refs[gi][k].astype(F32)
                o_ref[...] = acc

    def in_spec(gi):
        return pl.BlockSpec((n, tr, cdim), lambda gg, i: (0, jnp.where(gg == gi, i, 0), 0))

    return pl.pallas_call(
        body, name=name, out_shape=jax.ShapeDtypeStruct((len(groups) * r, cdim), F32), grid=(len(groups), steps),
        in_specs=[in_spec(gi) for gi in range(len(groups))],
        out_specs=pl.BlockSpec((tr, cdim), lambda gg, i: (gg * steps + i, 0)),
        compiler_params=_params(("parallel", "parallel")),
    )(*groups)


def _mm_tn(a, b, *, name, after=None, tm=512, tn=1024):
    k, m = a.shape
    n = b.shape[1]
    tm, tn = _tile(m, tm), _tile(n, tn)

    def body(a_ref, b_ref, *rest):
        o_ref, at_ref = rest[-2], rest[-1]

        @pl.when(pl.program_id(1) == 0)
        def _():
            at_ref[...] = a_ref[...].astype(BF16).T

        o_ref[...] = _dot(at_ref[...], b_ref[...].astype(BF16), 1, 0).astype(BF16)

    ins = [a, b] + ([] if after is None else [after])
    return pl.pallas_call(
        body, name=name, out_shape=jax.ShapeDtypeStruct((m, n), BF16), grid=(m // tm, n // tn),
        in_specs=[pl.BlockSpec((k, tm), lambda i, j: (0, i)), pl.BlockSpec((k, tn), lambda i, j: (0, j))] + [ANY] * (len(ins) - 2),
        out_specs=pl.BlockSpec((tm, tn), lambda i, j: (i, j)),
        scratch_shapes=[pltpu.VMEM((tm, k), BF16)], compiler_params=_params(("parallel", "arbitrary")),
    )(*ins)


def _mm2(a1, b1, a2, b2, *, name, after=None, tm=256, tn=1024):
    m = a1.shape[0]
    n = b1.shape[1]
    tm, tn = _tile(m, tm), _tile(n, tn)

    def body(a1_ref, b1_ref, a2_ref, b2_ref, *rest):
        rest[-1][...] = (_dot(a1_ref[...].astype(BF16), b1_ref[...], 1, 0)
                         + _dot(a2_ref[...].astype(BF16), b2_ref[...], 1, 0))

    ins = [a1, b1, a2, b2] + ([] if after is None else [after])

    def a_spec(t):
        return pl.BlockSpec((tm, t.shape[1]), lambda i, j: (i, 0))

    def b_spec(t):
        return pl.BlockSpec((t.shape[0], tn), lambda i, j: (0, j))

    return pl.pallas_call(
        body, name=name, out_shape=jax.ShapeDtypeStruct((m, n), F32), grid=(m // tm, n // tn),
        in_specs=[a_spec(a1), b_spec(b1), a_spec(a2), b_spec(b2)] + [ANY] * (len(ins) - 4),
        out_specs=pl.BlockSpec((tm, tn), lambda i, j: (i, j)), compiler_params=_params(("parallel", "parallel")),
    )(*ins)


def _mm(a, b, *, name, ta=False, tb=False, res=None, colscale=None, emit_acc=False,
        out_dtype=F32, tm=512, tn=512):
    m, k = (a.shape[1], a.shape[0]) if ta else a.shape
    n = b.shape[0] if tb else b.shape[1]
    tm, tn = _tile(m, tm), _tile(n, tn)
    ca, cb = (0 if ta else 1), (1 if tb else 0)
    a_spec = pl.BlockSpec((k, tm), lambda i, j: (0, i)) if ta else pl.BlockSpec((tm, k), lambda i, j: (i, 0))
    b_spec = pl.BlockSpec((tn, k), lambda i, j: (j, 0)) if tb else pl.BlockSpec((k, tn), lambda i, j: (0, j))
    tile = pl.BlockSpec((tm, tn), lambda i, j: (i, j))
    ins, in_specs = [a, b], [a_spec, b_spec]
    if res is not None:
        ins.append(res)
        in_specs.append(tile)
    if colscale is not None:
        ins.append(colscale)
        in_specs.append(pl.BlockSpec((1, tn), lambda i, j: (0, j)))
    n_in = len(ins)

    def body(*refs):
        outs = refs[n_in:]
        acc = _dot(refs[0][...].astype(BF16), refs[1][...].astype(BF16), ca, cb)
        val, p = acc, 2
        if res is not None:
            r_val, p = refs[p][...], p + 1
        if colscale is not None:
            val = val * refs[p][...]
        if res is not None:
            val = r_val + val
        if emit_acc:
            outs[0][...] = acc
        outs[-1][...] = val.astype(out_dtype)

    out_shape = [jax.ShapeDtypeStruct((m, n), out_dtype)]
    out_specs = [tile]
    if emit_acc:
        out_shape.insert(0, jax.ShapeDtypeStruct((m, n), F32))
        out_specs.insert(0, tile)
    out = pl.pallas_call(
        body, name=name, out_shape=out_shape, grid=(m // tm, n // tn), in_specs=in_specs, out_specs=out_specs,
        compiler_params=_params(("parallel", "parallel")),
    )(*ins)
    return out if emit_acc else out[0]


def _norm_fwd(x, g, scale, shift, name, after=None):
    s, d = x.shape
    tr = 256

    def body(x_ref, g_ref, sc_ref, sh_ref, *rest):
        xv = x_ref[...]
        rstd = lax.rsqrt(jnp.mean(xv * xv, axis=-1, keepdims=True) + RMS_EPS)
        rest[-1][...] = (xv * rstd * g_ref[...] * (1.0 + sc_ref[...]) + sh_ref[...]).astype(BF16)

    rowspec = pl.BlockSpec((1, d), lambda i: (0, 0))
    ins = [x, g, scale, shift] + ([] if after is None else [after])
    return pl.pallas_call(
        body, name=name, out_shape=jax.ShapeDtypeStruct((s, d), BF16), grid=(s // tr,),
        in_specs=[pl.BlockSpec((tr, d), lambda i: (i, 0)), rowspec, rowspec, rowspec] + [ANY] * (len(ins) - 4),
        out_specs=pl.BlockSpec((tr, d), lambda i: (i, 0)),
        compiler_params=_params(("parallel",)),
    )(*ins)


def _norm_bwd(x, dh, dres, g, scale, name):
    s, d = x.shape
    tr = 256

    def body(x_ref, dh_ref, dr_ref, g_ref, sc_ref, dx_ref, a_ref, b_ref):
        @pl.when(pl.program_id(0) == 0)
        def _():
            a_ref[...] = jnp.zeros_like(a_ref)
            b_ref[...] = jnp.zeros_like(b_ref)

        xv = x_ref[...]
        rstd = lax.rsqrt(jnp.mean(xv * xv, axis=-1, keepdims=True) + RMS_EPS)
        xhat = xv * rstd
        dhv = dh_ref[...]
        dxhat = dhv * (g_ref[...] * (1.0 + sc_ref[...]))
        mean_term = jnp.mean(dxhat * xhat, axis=-1, keepdims=True)
        dx_ref[...] = dr_ref[...] + rstd * (dxhat - xhat * mean_term)
        a_ref[...] += jnp.sum(dhv, axis=0, keepdims=True)
        b_ref[...] += jnp.sum(dhv * xhat, axis=0, keepdims=True)

    rowspec = pl.BlockSpec((1, d), lambda i: (0, 0))
    tile = pl.BlockSpec((tr, d), lambda i: (i, 0))
    return pl.pallas_call(
        body, name=name,
        out_shape=[jax.ShapeDtypeStruct((s, d), F32), jax.ShapeDtypeStruct((1, d), F32), jax.ShapeDtypeStruct((1, d), F32)],
        grid=(s // tr,), in_specs=[tile, tile, tile, rowspec, rowspec], out_specs=[tile, rowspec, rowspec],
        compiler_params=_params(("arbitrary",)),
    )(x, dh, dres, g, scale)


def _gate_bwd(dxn, f, colscale, coef, name):
    s, d = dxn.shape
    tr = 256

    def body(dx_ref, f_ref, cs_ref, df_ref, dg_ref):
        @pl.when(pl.program_id(0) == 0)
        def _():
            dg_ref[...] = jnp.zeros_like(dg_ref)

        dxv = dx_ref[...]
        df_ref[...] = (dxv * cs_ref[...]).astype(BF16)
        dg_ref[...] += coef * jnp.sum(dxv * f_ref[...], axis=0, keepdims=True)

    rowspec = pl.BlockSpec((1, d), lambda i: (0, 0))
    tile = pl.BlockSpec((tr, d), lambda i: (i, 0))
    return pl.pallas_call(
        body, name=name, out_shape=[jax.ShapeDtypeStruct((s, d), BF16), jax.ShapeDtypeStruct((1, d), F32)],
        grid=(s // tr,), in_specs=[tile, tile, rowspec], out_specs=[tile, rowspec],
        compiler_params=_params(("arbitrary",)),
    )(dxn, f, colscale)


def _ffn_up(h, wg, wu, name):
    s, d = h.shape
    f = wg.shape[0]
    tm, tn = s, _tile(f, 256)

    def body(h_ref, wg_ref, wu_ref, a_ref, u_ref, s_ref):
        hv = h_ref[...]
        a = _dot(hv, wg_ref[...], 1, 1)
        u = _dot(hv, wu_ref[...], 1, 1)
        a_ref[...] = a.astype(BF16)
        u_ref[...] = u.astype(BF16)
        s_ref[...] = (a * _sigmoid(a) * u).astype(BF16)

    tile = pl.BlockSpec((tm, tn), lambda i, j: (i, j))
    wspec = pl.BlockSpec((tn, d), lambda i, j: (j, 0))
    return pl.pallas_call(
        body, name=name,
        out_shape=[jax.ShapeDtypeStruct((s, f), BF16), jax.ShapeDtypeStruct((s, f), BF16), jax.ShapeDtypeStruct((s, f), BF16)],
        grid=(s // tm, f // tn), in_specs=[pl.BlockSpec((tm, d), lambda i, j: (i, 0)), wspec, wspec],
        out_specs=[tile, tile, tile], compiler_params=_params(("parallel", "parallel")),
    )(h, wg, wu)


def _ffn_bwd_ds(df, wd, a, u, name):
    s, d = df.shape
    f = wd.shape[0]
    tm, tn = 1024, _tile(f, 256)

    def body(df_ref, wd_ref, a_ref, u_ref, da_ref, du_ref):
        ds = _dot(df_ref[...], wd_ref[...], 1, 1)
        av = a_ref[...].astype(F32)
        sg = _sigmoid(av)
        da_ref[...] = (ds * u_ref[...].astype(F32) * (sg * (1.0 + av * (1.0 - sg)))).astype(BF16)
        du_ref[...] = (ds * (av * sg)).astype(BF16)

    tile = pl.BlockSpec((tm, tn), lambda i, j: (i, j))
    return pl.pallas_call(
        body, name=name, out_shape=[jax.ShapeDtypeStruct((s, f), BF16), jax.ShapeDtypeStruct((s, f), BF16)],
        grid=(s // tm, f // tn),
        in_specs=[pl.BlockSpec((tm, d), lambda i, j: (i, 0)), pl.BlockSpec((tn, d), lambda i, j: (j, 0)), tile, tile],
        out_specs=[tile, tile], compiler_params=_params(("parallel", "parallel")),
    )(df, wd, a, u)


def _merge_fwd(o_sb, o_dil, o_swa, gates, wb_sb, wb_dil, wb_swa, name):
    s, d = SEQ, D_MODEL
    tm = 256

    def body(osb_ref, odl_ref, osw_ref, g_ref, wsb_ref, wdl_ref, wsw_ref, m_ref, tsb_ref, tdl_ref, tsw_ref):
        for h in range(osb_ref.shape[0]):
            tsb_ref[:, h * HEAD_DIM:(h + 1) * HEAD_DIM] = osb_ref[h].astype(BF16)
        for h in range(osw_ref.shape[0]):
            tsw_ref[:, h * HEAD_DIM:(h + 1) * HEAD_DIM] = osw_ref[h].astype(BF16)
        tdl_ref[...] = odl_ref[...].astype(BF16)
        acc = _sigmoid(g_ref[:, 0:d]) * _dot(tsb_ref[...], wsb_ref[...], 1, 0)
        acc += _sigmoid(g_ref[:, d:2 * d]) * _dot(tdl_ref[...], wdl_ref[...], 1, 0)
        acc += _sigmoid(g_ref[:, 2 * d:3 * d]) * _dot(tsw_ref[...], wsw_ref[...], 1, 0)
        m_ref[...] = acc.astype(BF16)

    def rows(w):
        return pl.BlockSpec((tm, w), lambda i: (i, 0))

    def heads(n):
        return pl.BlockSpec((n, tm, HEAD_DIM), lambda i: (0, i, 0))

    def whole(w):
        return pl.BlockSpec((w, d), lambda i: (0, 0))

    return pl.pallas_call(
        body, name=name, out_shape=[jax.ShapeDtypeStruct((s, w), BF16) for w in (d, 256, 128, 384)], grid=(s // tm,),
        in_specs=[heads(H_SB), rows(128), heads(H_SWA_Q), rows(3 * d), whole(256), whole(128), whole(384)],
        out_specs=[rows(d), rows(256), rows(128), rows(384)], compiler_params=_params(("parallel",)),
    )(o_sb, o_dil, o_swa, gates, wb_sb, wb_dil, wb_swa)


def _merge_bwd(dmerged, t_sb, t_dil, t_swa, gates, wb_sb, wb_dil, wb_swa, name):
    s, d = SEQ, D_MODEL
    tm = 256

    def body(dm_ref, tsb_ref, tdl_ref, tsw_ref, g_ref, wsb_ref, wdl_ref, wsw_ref,
             dg_ref, dosb_ref, dodl_ref, dosw_ref, dbsb_ref, dbdl_ref, dbsw_ref):
        dm = dm_ref[...]
        for idx, (t_ref, w_ref, do_ref, db_ref) in enumerate((
                (tsb_ref, wsb_ref, dosb_ref, dbsb_ref), (tdl_ref, wdl_ref, dodl_ref, dbdl_ref),
                (tsw_ref, wsw_ref, dosw_ref, dbsw_ref))):
            w = w_ref[...]
            br = _dot(t_ref[...], w, 1, 0)
            sg = _sigmoid(g_ref[:, idx * d:(idx + 1) * d])
            dbr = (dm * sg).astype(BF16)
            dg_ref[:, idx * d:(idx + 1) * d] = dm * br * (sg * (1.0 - sg))
            db_ref[...] = dbr
            do = _dot(dbr, w, 1, 1)
            if len(do_ref.shape) == 2:
                do_ref[...] = do
            else:
                for h in range(do_ref.shape[0]):
                    do_ref[h] = do[:, h * HEAD_DIM:(h + 1) * HEAD_DIM]

    def rows(w):
        return pl.BlockSpec((tm, w), lambda i: (i, 0))

    def heads(n):
        return pl.BlockSpec((n, tm, HEAD_DIM), lambda i: (0, i, 0))

    def whole(w):
        return pl.BlockSpec((w, d), lambda i: (0, 0))

    def shp(w, dt):
        return jax.ShapeDtypeStruct((s, w), dt)

    def hshp(n):
        return jax.ShapeDtypeStruct((n, s, HEAD_DIM), F32)

    return pl.pallas_call(
        body, name=name,
        out_shape=[shp(3 * d, F32), hshp(H_SB), shp(128, F32), hshp(H_SWA_Q), shp(d, BF16), shp(d, BF16), shp(d, BF16)],
        grid=(s // tm,),
        in_specs=[rows(d), rows(256), rows(128), rows(384), rows(3 * d), whole(256), whole(128), whole(384)],
        out_specs=[rows(3 * d), heads(H_SB), rows(128), heads(H_SWA_Q), rows(d), rows(d), rows(d)],
        compiler_params=_params(("parallel",)),
    )(dmerged, t_sb, t_dil, t_swa, gates, wb_sb, wb_dil, wb_swa)


def _final_loss(x, target, g, name):
    s, d = x.shape
    tr = 256

    def body(x_ref, t_ref, g_ref, loss_ref, dx_ref, dg_ref):
        @pl.when(pl.program_id(0) == 0)
        def _():
            loss_ref[...] = jnp.zeros_like(loss_ref)
            dg_ref[...] = jnp.zeros_like(dg_ref)

        xv = x_ref[...]
        gv = g_ref[...]
        rstd = lax.rsqrt(jnp.mean(xv * xv, axis=-1, keepdims=True) + RMS_EPS)
        xhat = xv * rstd
        err = xhat * gv - t_ref[...]
        loss_ref[...] += 0.5 * jnp.sum(jnp.mean(err * err, axis=-1, keepdims=True))
        dy = err * (1.0 / d)
        dxhat = dy * gv
        mean_term = jnp.mean(dxhat * xhat, axis=-1, keepdims=True)
        dx_ref[...] = rstd * (dxhat - xhat * mean_term)
        dg_ref[...] += jnp.sum(dy * xhat, axis=0, keepdims=True)

    rowspec = pl.BlockSpec((1, d), lambda i: (0, 0))
    tile = pl.BlockSpec((tr, d), lambda i: (i, 0))
    return pl.pallas_call(
        body, name=name,
        out_shape=[jax.ShapeDtypeStruct((1, LANES), F32), jax.ShapeDtypeStruct((s, d), F32), jax.ShapeDtypeStruct((1, d), F32)],
        grid=(s // tr,), in_specs=[tile, tile, rowspec],
        out_specs=[pl.BlockSpec((1, LANES), lambda i: (0, 0)), tile, rowspec],
        compiler_params=_params(("arbitrary",)),
    )(x, target, g)


def _adamw(w, g, m, v, name):
    shape = w.shape
    cols = shape[-1]
    rows = int(np.prod(shape[:-1])) if len(shape) > 1 else 1
    tr = rows
    for cand in (1024, 512, 256, 128, 64, 32, 16, 8):
        if rows % cand == 0 and rows > cand and cand * cols * 4 <= (1 << 21):
            tr = cand
            break

    def body(w_ref, g_ref, m_ref, v_ref, d_ref, nm_ref, nv_ref):
        gv = g_ref[...]
        nm = ADAM_B1 * m_ref[...] + (1.0 - ADAM_B1) * gv
        nv = ADAM_B2 * v_ref[...] + (1.0 - ADAM_B2) * (gv * gv)
        m_hat = nm / (1.0 - ADAM_B1 ** ADAM_STEP)
        v_hat = nv / (1.0 - ADAM_B2 ** ADAM_STEP)
        d_ref[...] = -ADAM_LR * (m_hat / (jnp.sqrt(v_hat) + ADAM_EPS) + ADAM_WD * w_ref[...])
        nm_ref[...] = nm
        nv_ref[...] = nv

    tile = pl.BlockSpec((tr, cols), lambda i: (i, 0))
    flat = [t.reshape(rows, cols) for t in (w, g, m, v)]
    out = pl.pallas_call(
        body, name=name, out_shape=[jax.ShapeDtypeStruct((rows, cols), F32)] * 3, grid=(rows // tr,),
        in_specs=[tile] * 4, out_specs=[tile] * 3, compiler_params=_params(("parallel",)),
    )(*flat)
    return tuple(t.reshape(shape) for t in out)


def _ada_fwd(c_all, w, name):
    n = w.shape[1]

    def body(c_ref, w_ref, o_ref):
        cv = c_ref[...]
        o_ref[...] = jnp.dot(cv * _sigmoid(cv), w_ref[...], preferred_element_type=F32, precision=lax.Precision.HIGHEST)

    return pl.pallas_call(body, name=name, out_shape=jax.ShapeDtypeStruct((N_DEV, n), F32), compiler_params=_params())(c_all, w)


def _ada_bwd(c_all_t, dmod, name):
    n = dmod.shape[1]

    def body(c_ref, d_ref, o_ref):
        cv = c_ref[...]
        o_ref[...] = jnp.dot(cv * _sigmoid(cv), d_ref[...], preferred_element_type=F32, precision=lax.Precision.HIGHEST)

    return pl.pallas_call(body, name=name, out_shape=jax.ShapeDtypeStruct((D_MODEL, n), F32), compiler_params=_params())(c_all_t, dmod)


def _bucket_tables():
    rel = np.arange(BLK)[:, None] + BLK - np.arange(2 * BLK)[None, :]
    max_exact = N_BUCKETS // 2

    def bucket(n):
        nf = np.maximum(n, 1).astype(np.float32)
        large = max_exact + (np.log(nf / np.float32(max_exact)) / np.float32(math.log(MAX_REL_DIST / max_exact))
                             * np.float32(N_BUCKETS - max_exact)).astype(np.int32)
        return np.where(n < max_exact, n, np.minimum(large, N_BUCKETS - 1))

    tabs = []
    for dil, max_dist in ((1, 128), (4, 128), (16, 128), (1, SWA_WINDOW - 1)):
        in_band = (rel >= 0) & (rel <= max_dist)
        tabs.append(np.where(in_band, bucket(np.maximum(rel, 0) * dil), -1))
    return np.stack(tabs).astype(np.int32)


N_SOFT = H_DIL + H_SWA_Q


def _table_of_head(h):
    return jnp.minimum(h // 2, 3)


def _bias_build(rel_bias, tables, name):
    def body(rel_ref, t_ref, o_ref):
        h = pl.program_id(0)
        tb = t_ref[0]
        out = jnp.full((BLK, 2 * BLK), NEG, F32)
        for b in range(N_BUCKETS):
            out = jnp.where(tb == b, rel_ref[b, h], out)
        o_ref[0] = out

    return pl.pallas_call(
        body, name=name, out_shape=jax.ShapeDtypeStruct((N_SOFT, BLK, 2 * BLK), F32), grid=(N_SOFT,),
        in_specs=[pl.BlockSpec(memory_space=pltpu.SMEM),
                  pl.BlockSpec((1, BLK, 2 * BLK), lambda h: (_table_of_head(h), 0, 0))],
        out_specs=pl.BlockSpec((1, BLK, 2 * BLK), lambda h: (h, 0, 0)),
        compiler_params=_params(("parallel",)),
    )(rel_bias, tables)


def _bias_grad(dbias, tables, name):
    def body(d_ref, t_ref, o_ref):
        tb = t_ref[0]
        dv = d_ref[0]
        lane = lax.broadcasted_iota(jnp.int32, (1, LANES), 1)
        out = jnp.zeros((1, LANES), F32)
        for b in range(N_BUCKETS):
            out = jnp.where(lane == b, jnp.sum(jnp.where(tb == b, dv, 0.0)), out)
        o_ref[0] = out

    return pl.pallas_call(
        body, name=name, out_shape=jax.ShapeDtypeStruct((N_SOFT, 1, LANES), F32), grid=(N_SOFT,),
        in_specs=[pl.BlockSpec((1, BLK, 2 * BLK), lambda h: (h, 0, 0)),
                  pl.BlockSpec((1, BLK, 2 * BLK), lambda h: (_table_of_head(h), 0, 0))],
        out_specs=pl.BlockSpec((1, 1, LANES), lambda h: (h, 0, 0)),
        compiler_params=_params(("parallel",)),
    )(dbias, tables)


def _band_layout(g, bias_div):
    assert g == 1 or bias_div == 1
    return bias_div if g == 1 else 1


def _band_specs(length, g, bias_div, offs):
    ns = _band_layout(g, bias_div)

    def seqs(off, div=1):
        return pl.BlockSpec((ns, length, HEAD_DIM), lambda s: (off // ns + s // div, 0, 0))

    xspecs = [seqs(offs[0]), seqs(offs[1], g), seqs(offs[2], g)]
    bspec = pl.BlockSpec((1, BLK, 2 * BLK), lambda s: (s, 0, 0))
    sspec = pl.BlockSpec((ns, 1, LANES), lambda s: (s, 0, 0))
    colspec = pl.BlockSpec((ns, length, 1), lambda s: (s, 0, 0))
    return xspecs, seqs(0), seqs(0, g), bspec, sspec, colspec


def _band_sweep(length, ns, one):
    nblk = length // BLK
    for qq in range(ns):
        if ns * nblk <= 16:
            for i in range(nblk):
                one(qq, i * BLK, max(i - 1, 0) * BLK, i == 0)
        else:
            def step(i, carry, qq=qq):
                one(qq, pl.multiple_of(i * BLK, BLK), pl.multiple_of(jnp.maximum(i - 1, 0) * BLK, BLK), i == 0)
                return carry

            lax.fori_loop(0, nblk, step, 0, unroll=2)


def _band_scores(q_ref, k_ref, b_ref, qq, kq, bq, cur, prv, first):
    qv = q_ref[qq, pl.ds(cur, BLK), :]
    bv = b_ref[bq]
    if first is True:
        sp = jnp.full((BLK, BLK), NEG, F32)
    else:
        sp = _dot(qv, k_ref[kq, pl.ds(prv, BLK), :], 1, 1) + bv[:, :BLK]
        sp = sp if first is False else jnp.where(first, NEG, sp)
    sc = _dot(qv, k_ref[kq, pl.ds(cur, BLK), :], 1, 1) + bv[:, BLK:]
    return qv, sp, sc


def _band_fwd(x, bias, sink, *, nq, offs, g, bias_div, has_sink, name):
    length = x.shape[1]
    ns = _band_layout(g, bias_div)

    def body(q_ref, k_ref, v_ref, b_ref, s_ref, o_ref, lse_ref):
        def one(qq, cur, prv, first):
            kq, bq = qq, 0
            _, sp, sc = _band_scores(q_ref, k_ref, b_ref, qq, kq, bq, cur, prv, first)
            m = jnp.maximum(jnp.max(sp, axis=1, keepdims=True), jnp.max(sc, axis=1, keepdims=True))
            if has_sink:
                sk = s_ref[qq][:, :1]
                m = jnp.maximum(m, sk)
            pp, pc = jnp.exp(sp - m), jnp.exp(sc - m)
            den = jnp.sum(pp, axis=1, keepdims=True) + jnp.sum(pc, axis=1, keepdims=True)
            if has_sink:
                den = den + jnp.exp(sk - m)
            acc = (_dot(pp.astype(BF16), v_ref[kq, pl.ds(prv, BLK), :], 1, 0)
                   + _dot(pc.astype(BF16), v_ref[kq, pl.ds(cur, BLK), :], 1, 0))
            o_ref[qq, pl.ds(cur, BLK), :] = acc / den
            lse_ref[qq, pl.ds(cur, BLK), :] = m + jnp.log(den)

        _band_sweep(length, ns, one)

    xspecs, qspec, _, bspec, sspec, colspec = _band_specs(length, g, bias_div, offs)
    return pl.pallas_call(
        body, name=name,
        out_shape=[jax.ShapeDtypeStruct((nq, length, HEAD_DIM), F32), jax.ShapeDtypeStruct((nq, length, 1), F32)],
        grid=(nq // ns,), in_specs=xspecs + [bspec, sspec],
        out_specs=[qspec, colspec], compiler_params=_params(("parallel",)),
    )(x, x, x, bias, sink)


def _band_bwd(x, bias, sink, o, lse, do, dlse, *, nq, offs, g, bias_div, has_sink, name):
    length = x.shape[1]
    ns = _band_layout(g, bias_div)
    nk, nbias = nq // g, nq // bias_div

    def body(q_ref, k_ref, v_ref, b_ref, s_ref, o_ref, lse_ref, do_ref, dlse_ref,
             dq_ref, dk_ref, dv_ref, db_ref, dsk_ref, dkp_ref, dvp_ref):
        for ref in (db_ref, dsk_ref, dkp_ref, dvp_ref):
            ref[...] = jnp.zeros_like(ref)

        @pl.when(pl.program_id(0) % g == 0)
        def _():
            dk_ref[...] = jnp.zeros_like(dk_ref)
            dv_ref[...] = jnp.zeros_like(dv_ref)

        def one(qq, cur, prv, first):
            kq, bq = qq, 0
            qv, sp, sc = _band_scores(q_ref, k_ref, b_ref, qq, kq, bq, cur, prv, first)
            rows, prow = pl.ds(cur, BLK), pl.ds(prv, BLK)
            lse_v = lse_ref[qq, rows, :]
            pp, pc = jnp.exp(sp - lse_v), jnp.exp(sc - lse_v)
            dov = do_ref[qq, rows, :]
            dob = dov.astype(BF16)
            coef = dlse_ref[qq, rows, :] - jnp.sum(dov * o_ref[qq, rows, :], axis=1, keepdims=True)
            dsp = pp * (_dot(dob, v_ref[kq, prow, :], 1, 1) + coef)
            dsc = pc * (_dot(dob, v_ref[kq, rows, :], 1, 1) + coef)
            dspb, dscb = dsp.astype(BF16), dsc.astype(BF16)
            dq_ref[qq, rows, :] = ((_dot(dspb, k_ref[kq, prow, :], 1, 0) + _dot(dscb, k_ref[kq, rows, :], 1, 0))
                                   * (HEAD_DIM ** -0.5))
            dk_ref[kq, rows, :] += _dot(dscb, qv, 0, 0)
            dkp_ref[kq, prow, :] += _dot(dspb, qv, 0, 0)
            dv_ref[kq, rows, :] += _dot(pc.astype(BF16), dob, 0, 0)
            dvp_ref[kq, prow, :] += _dot(pp.astype(BF16), dob, 0, 0)
            db_ref[bq, :, :BLK] += dsp
            db_ref[bq, :, BLK:] += dsc
            if has_sink:
                dsk_ref[qq] += jnp.sum(jnp.exp(s_ref[qq][:, :1] - lse_v) * coef)

        _band_sweep(length, ns, one)
        dk_ref[...] += dkp_ref[...]
        dv_ref[...] += dvp_ref[...]

    xspecs, qspec, kvspec, bspec, sspec, colspec = _band_specs(length, g, bias_div, offs)
    return pl.pallas_call(
        body, name=name,
        out_shape=[jax.ShapeDtypeStruct((nq, length, HEAD_DIM), F32), jax.ShapeDtypeStruct((nk, length, HEAD_DIM), F32),
                   jax.ShapeDtypeStruct((nk, length, HEAD_DIM), F32), jax.ShapeDtypeStruct((nbias, BLK, 2 * BLK), F32),
                   jax.ShapeDtypeStruct((nq, 1, LANES), F32)],
        grid=(nq // ns,),
        in_specs=xspecs + [bspec, sspec, qspec, colspec, qspec, colspec],
        out_specs=[qspec, kvspec, kvspec, bspec, sspec],
        scratch_shapes=[pltpu.VMEM((ns, length, HEAD_DIM), F32), pltpu.VMEM((ns, length, HEAD_DIM), F32)],
        compiler_params=_params(("arbitrary",)),
    )(x, x, x, bias, sink, o, lse, do, dlse)


TOK_TILE = 512


def _dil_merge(outs, lses, dout, name):
    tr = TOK_TILE
    dils = [d for _, d in DIL_PATTERNS]
    n = len(dils)
    o4 = [o.reshape(2, d, SEQ // d, HEAD_DIM) for o, d in zip(outs, dils)]
    l4 = [l.reshape(2, d, SEQ // d, 1) for l, d in zip(lses, dils)]
    o_specs = [pl.BlockSpec((2, d, tr // d, HEAD_DIM), lambda i: (0, 0, i, 0)) for d in dils]
    l_specs = [pl.BlockSpec((2, d, tr // d, 1), lambda i: (0, 0, i, 0)) for d in dils]
    tok = pl.BlockSpec((tr, 2 * HEAD_DIM), lambda i: (i, 0))
    scratch = ([pltpu.VMEM((tr, 2 * HEAD_DIM), F32) for _ in dils] + [pltpu.VMEM((tr, 1), F32) for _ in range(2 * n)]
               + [pltpu.VMEM((tr // d, 2 * HEAD_DIM), F32) for d in dils])

    def to_tokens(o_ref, l_ref, d, pair, cols, stage):
        for r in range(d):
            rows = pl.ds(r, tr // d, stride=d) if d > 1 else slice(None)
            stage[:, :HEAD_DIM] = o_ref[0, r]
            stage[:, HEAD_DIM:] = o_ref[1, r]
            pair[rows, :] = stage[...]
            for h in range(2):
                cols[h][rows, :] = l_ref[h, r]
        return pair[...], [cols[0][...], cols[1][...]]

    def weights(ls):
        left = lax.broadcasted_iota(jnp.int32, (tr, 2 * HEAD_DIM), 1) < HEAD_DIM
        per_head = []
        for h in range(2):
            m = ls[0][h]
            for g in range(1, n):
                m = jnp.maximum(m, ls[g][h])
            es = [jnp.exp(ls[g][h] - m) for g in range(n)]
            den = es[0]
            for e in es[1:]:
                den = den + e
            per_head.append([e / den for e in es])
        return per_head, [jnp.where(left, per_head[0][g], per_head[1][g]) for g in range(n)], left

    def load(refs):
        pairs, cols, stages = refs[:n], refs[n:3 * n], refs[3 * n:]
        return pairs, [cols[2 * g:2 * g + 2] for g in range(n)], stages

    if dout is None:
        def body(*refs):
            pairs, cols, stages = load(refs[2 * n + 1:])
            toks = [to_tokens(refs[g], refs[n + g], dils[g], pairs[g], cols[g], stages[g]) for g in range(n)]
            _, alphas, _ = weights([t[1] for t in toks])
            acc = alphas[0] * toks[0][0]
            for g in range(1, n):
                acc = acc + alphas[g] * toks[g][0]
            refs[2 * n][...] = acc

        return pl.pallas_call(
            body, name=name, out_shape=jax.ShapeDtypeStruct((SEQ, 2 * HEAD_DIM), F32), grid=(SEQ // tr,),
            in_specs=o_specs + l_specs, out_specs=tok, scratch_shapes=scratch, compiler_params=_params(("parallel",)),
        )(*o4, *l4)

    def body(*refs):
        do_refs, dl_refs = refs[2 * n + 1:3 * n + 1], refs[3 * n + 1:4 * n + 1]
        pairs, cols, stages = load(refs[4 * n + 1:])
        toks = [to_tokens(refs[g], refs[n + g], dils[g], pairs[g], cols[g], stages[g]) for g in range(n)]
        per_head, alphas, left = weights([t[1] for t in toks])
        dov = refs[2 * n][...]
        das = []
        for g in range(n):
            prod = dov * toks[g][0]
            das.append([jnp.sum(jnp.where(left, prod, 0.0), axis=1, keepdims=True),
                        jnp.sum(jnp.where(left, 0.0, prod), axis=1, keepdims=True)])
        dbar = [sum(per_head[h][g] * das[g][h] for g in range(n)) for h in range(2)]
        for g, d in enumerate(dils):
            pairs[g][...] = alphas[g] * dov
            for h in range(2):
                cols[g][h][...] = per_head[h][g] * (das[g][h] - dbar[h])
            for r in range(d):
                rows = pl.ds(r, tr // d, stride=d) if d > 1 else slice(None)
                v = pairs[g][rows, :]
                for h in range(2):
                    do_refs[g][h, r] = v[:, h * HEAD_DIM:(h + 1) * HEAD_DIM]
                    dl_refs[g][h, r] = cols[g][h][rows, :]

    out = pl.pallas_call(
        body, name=name,
        out_shape=[jax.ShapeDtypeStruct(o.shape, F32) for o in o4] + [jax.ShapeDtypeStruct(l.shape, F32) for l in l4],
        grid=(SEQ // tr,), in_specs=o_specs + l_specs + [tok], out_specs=o_specs + l_specs, scratch_shapes=scratch,
        compiler_params=_params(("parallel",)),
    )(*o4, *l4, dout)
    return [t.reshape(s.shape) for t, s in zip(out, list(outs) + list(lses))]


def _tri(cmp):
    r = lax.broadcasted_iota(jnp.int32, (SB_TILE, SB_TILE), 0)
    c = lax.broadcasted_iota(jnp.int32, (SB_TILE, SB_TILE), 1)
    return cmp(r, c).astype(BF16)


def _cum(x, tri, terms):
    acc, rest = None, x
    for _ in range(terms):
        part = rest.astype(BF16)
        rest = rest - part.astype(F32)
        d = _dot(part, tri, 1, 0)
        acc = d if acc is None else acc + d
    return acc


def _sb_logits(q, k_ref, j, i):
    t = SB_TILE
    ks = k_ref[0, pl.ds(pl.multiple_of(j * t, t), t), :]
    z = _dot(q, ks, 1, 1)
    rows = i * t + lax.broadcasted_iota(jnp.int32, (t, t), 0)
    cols = j * t + lax.broadcasted_iota(jnp.int32, (t, t), 1)
    mask = cols < rows
    e = jnp.exp(-jnp.abs(z))
    lf = jnp.where(mask, -(jnp.maximum(z, 0.0) + jnp.log(1.0 + e)), 0.0)
    return ks, z, e, lf, mask


def _sb_specs(h, s):
    t = SB_TILE
    tile = pl.BlockSpec((1, t, HEAD_DIM), lambda hh, i: (hh, i, 0))
    keys = pl.BlockSpec((1, s, HEAD_DIM), lambda hh, i: (h + hh, 0, 0))
    values = pl.BlockSpec((1, s, HEAD_DIM), lambda hh, i: (2 * h + hh, 0, 0))
    return tile, keys, values


def _sb_fwd(x, name):
    h, s = x.shape[0] // 3, x.shape[1]
    t = SB_TILE

    def body(q_ref, k_ref, v_ref, o_ref, tot_ref):
        i = pl.program_id(1)
        qv = q_ref[0]
        after = _tri(lambda r, c: r > c)

        def step(jj, carry):
            right, acc = carry
            j = i - jj
            _, z, _, lf, mask = _sb_logits(qv, k_ref, j, i)
            between = right + _cum(lf, after, 3)
            w = jnp.where(mask, jnp.exp(z + lf + between), 0.0)
            vs = v_ref[0, pl.ds(pl.multiple_of(j * t, t), t), :]
            return right + jnp.sum(lf, axis=1, keepdims=True), acc + _dot(w.astype(BF16), vs, 1, 0)

        right, acc = lax.fori_loop(0, i + 1, step, (jnp.zeros((t, 1), F32), jnp.zeros((t, HEAD_DIM), F32)))
        o_ref[0] = acc
        tot_ref[0] = right

    tile, keys, values = _sb_specs(h, s)
    return pl.pallas_call(
        body, name=name, out_shape=[jax.ShapeDtypeStruct((h, s, HEAD_DIM), F32), jax.ShapeDtypeStruct((h, s, 1), F32)],
        grid=(h, s // t), in_specs=[tile, keys, values],
        out_specs=[tile, pl.BlockSpec((1, t, 1), lambda hh, i: (hh, i, 0))],
        compiler_params=_params(("parallel", "parallel")),
    )(x, x, x)


def _sb_bwd(x, tot, do, name):
    h, s = x.shape[0] // 3, x.shape[1]
    t = SB_TILE

    def body(q_ref, k_ref, v_ref, tot_ref, do_ref, dq_ref, dk_ref, dv_ref):
        i = pl.program_id(1)

        @pl.when(i == 0)
        def _():
            dk_ref[...] = jnp.zeros_like(dk_ref)
            dv_ref[...] = jnp.zeros_like(dv_ref)

        qv = q_ref[0]
        dob = do_ref[0].astype(BF16)
        total = tot_ref[0]
        upto = _tri(lambda r, c: r <= c)
        before = _tri(lambda r, c: r < c)

        def step(j, carry):
            left, cleft, dq = carry
            ks, z, e, lf, mask = _sb_logits(qv, k_ref, j, i)
            rows = pl.ds(pl.multiple_of(j * t, t), t)
            vs = v_ref[0, rows, :]
            between = total - (left + _cum(lf, upto, 3))
            w = jnp.where(mask, jnp.exp(z + lf + between), 0.0)
            dlog = w * _dot(dob, vs, 1, 1)
            cfail = cleft + _cum(dlog, before, 2)
            sig = jnp.where(z >= 0.0, 1.0, e) / (1.0 + e)
            dz = jnp.where(mask, dlog * (1.0 - sig) - sig * cfail, 0.0).astype(BF16)
            dk_ref[0, rows, :] += _dot(dz, qv, 0, 0)
            dv_ref[0, rows, :] += _dot(w.astype(BF16), dob, 0, 0)
            return (left + jnp.sum(lf, axis=1, keepdims=True), cleft + jnp.sum(dlog, axis=1, keepdims=True),
                    dq + _dot(dz, ks, 1, 0))

        zero = jnp.zeros((t, 1), F32)
        _, _, dq = lax.fori_loop(0, i + 1, step, (zero, zero, jnp.zeros((t, HEAD_DIM), F32)))
        dq_ref[0] = dq * (HEAD_DIM ** -0.5)

    tile, keys, values = _sb_specs(h, s)
    full = pl.BlockSpec((1, s, HEAD_DIM), lambda hh, i: (hh, 0, 0))
    shp = jax.ShapeDtypeStruct((h, s, HEAD_DIM), F32)
    return pl.pallas_call(
        body, name=name, out_shape=[shp, shp, shp], grid=(h, s // t),
        in_specs=[tile, keys, values, pl.BlockSpec((1, t, 1), lambda hh, i: (hh, i, 0)), tile],
        out_specs=[tile, full, full], compiler_params=_params(("arbitrary", "arbitrary")),
    )(x, x, x, tot, do)


COL_SB, COL_DIL, COL_SWA = 0, 3 * H_SB * HEAD_DIM, 3 * H_SB * HEAD_DIM + 3 * H_DIL * HEAD_DIM
N_SWA = H_SWA_Q + 2 * H_SWA_KV


def _dil_col(t, g):
    return COL_DIL + t * H_DIL * HEAD_DIM + g * 2 * HEAD_DIM


def _split_heads(qkv, name):
    tr = TOK_TILE
    scale = HEAD_DIM ** -0.5
    dils = [d for _, d in DIL_PATTERNS]

    def body(x_ref, sb_ref, d0_ref, d1_ref, d2_ref, swa_ref, pair):
        def head(col, scaled):
            v = x_ref[:, col:col + HEAD_DIM]
            return (v * scale if scaled else v).astype(BF16)

        for hh in range(3 * H_SB):
            sb_ref[hh] = head(COL_SB + hh * HEAD_DIM, hh < H_SB)
        for hh in range(N_SWA):
            swa_ref[hh] = head(COL_SWA + hh * HEAD_DIM, hh < H_SWA_Q)
        for t in range(3):
            for g, (d, out_ref) in enumerate(zip(dils, (d0_ref, d1_ref, d2_ref))):
                col = _dil_col(t, g)
                if d == 1:
                    for h in range(2):
                        out_ref[t * 2 + h] = head(col + h * HEAD_DIM, t == 0)
                    continue
                pair[...] = x_ref[:, col:col + 2 * HEAD_DIM]
                for r in range(d):
                    v = pair[pl.ds(r, tr // d, stride=d), :]
                    v = v * scale if t == 0 else v
                    for h in range(2):
                        out_ref[t * 2 * d + h * d + r] = v[:, h * HEAD_DIM:(h + 1) * HEAD_DIM].astype(BF16)

    def heads(n, length):
        return jax.ShapeDtypeStruct((n, length, HEAD_DIM), BF16)

    def spec(n, rows):
        return pl.BlockSpec((n, rows, HEAD_DIM), lambda i: (0, i, 0))

    return pl.pallas_call(
        body, name=name,
        out_shape=[heads(3 * H_SB, SEQ)] + [heads(6 * d, SEQ // d) for d in dils] + [heads(N_SWA, SEQ)],
        grid=(SEQ // tr,), in_specs=[pl.BlockSpec((tr, D_QKV), lambda i: (i, 0))],
        out_specs=[spec(3 * H_SB, tr)] + [spec(6 * d, tr // d) for d in dils] + [spec(N_SWA, tr)],
        scratch_shapes=[pltpu.VMEM((tr, 2 * HEAD_DIM), F32)], compiler_params=_params(("parallel",)),
    )(qkv)


def _join_heads(sb, dil, swa, name):
    tr = TOK_TILE
    dils = [d for _, d in DIL_PATTERNS]

    def body(*refs):
        sb_refs, dil_refs, swa_refs = refs[:3], [refs[3 + 3 * g:6 + 3 * g] for g in range(3)], refs[12:15]
        o_ref, pair, stages = refs[15], refs[16], refs[17:]

        def put(col, v):
            o_ref[:, col:col + v.shape[1]] = v.astype(BF16)

        for t in range(3):
            for h in range(H_SB):
                put(COL_SB + (t * H_SB + h) * HEAD_DIM, sb_refs[t][h])
        col = COL_SWA
        for ref in swa_refs:
            for h in range(ref.shape[0]):
                put(col, ref[h])
                col += HEAD_DIM
        for t in range(3):
            for g, d in enumerate(dils):
                ref, col = dil_refs[g][t], _dil_col(t, g)
                if d == 1:
                    for h in range(2):
                        put(col + h * HEAD_DIM, ref[h])
                    continue
                stage = stages[g - 1]
                for r in range(d):
                    stage[:, :HEAD_DIM] = ref[r]
                    stage[:, HEAD_DIM:] = ref[d + r]
                    pair[pl.ds(r, tr // d, stride=d), :] = stage[...]
                put(col, pair[...])

    def spec(n, rows):
        return pl.BlockSpec((n, rows, HEAD_DIM), lambda i: (0, i, 0))

    ins = list(sb) + [t for g in range(3) for t in dil[g]] + list(swa)
    in_specs = ([spec(H_SB, tr)] * 3 + [spec(2 * d, tr // d) for d in dils for _ in range(3)]
                + [spec(H_SWA_Q, tr), spec(H_SWA_KV, tr), spec(H_SWA_KV, tr)])
    return pl.pallas_call(
        body, name=name, out_shape=jax.ShapeDtypeStruct((SEQ, D_QKV), BF16), grid=(SEQ // tr,), in_specs=in_specs,
        out_specs=pl.BlockSpec((tr, D_QKV), lambda i: (i, 0)),
        scratch_shapes=[pltpu.VMEM((tr, 2 * HEAD_DIM), F32)] + [pltpu.VMEM((tr // d, 2 * HEAD_DIM), F32) for d in dils[1:]],
        compiler_params=_params(("parallel",)),
    )(*ins)


def _mixer_fwd(qkv, bias, sinks_l, tag):
    sb, d0, d1, d2, swa = _split_heads(qkv, name=f"split_heads_{tag}")
    st = {"sb": sb, "dil": (d0, d1, d2), "swa": swa}
    o_sb, st["sb_tot"] = _sb_fwd(sb, name=f"sb_fwd_{tag}")
    st["dil_out"], st["dil_lse"], st["dil_sink"] = [], [], []
    for gi, (_, d) in enumerate(DIL_PATTERNS):
        sink = jnp.zeros((2 * d, 1, LANES), F32)
        og, lg = _band_fwd(st["dil"][gi], bias[2 * gi:2 * gi + 2], sink, nq=2 * d, offs=(0, 2 * d, 4 * d), g=1, bias_div=d,
                           has_sink=False, name=f"dil{gi}_fwd_{tag}")
        st["dil_out"].append(og)
        st["dil_lse"].append(lg)
        st["dil_sink"].append(sink)
    o_dil = _dil_merge(st["dil_out"], st["dil_lse"], None, name=f"dil_merge_fwd_{tag}")
    st["swa_sink"] = jnp.broadcast_to(sinks_l.reshape(H_SWA_Q, 1, 1), (H_SWA_Q, 1, LANES))
    st["swa_out"] = _band_fwd(swa, bias[H_DIL:], st["swa_sink"], nq=H_SWA_Q, offs=(0, H_SWA_Q, H_SWA_Q + H_SWA_KV),
                              g=H_SWA_Q // H_SWA_KV, bias_div=1, has_sink=True, name=f"swa_fwd_{tag}")
    return (o_sb, o_dil, st["swa_out"][0]), st


def _mixer_bwd(st, bias, do_sb, do_dil, do_swa, tag):
    d_sb = _sb_bwd(st["sb"], st["sb_tot"], do_sb, name=f"sb_bwd_{tag}")
    dmerge = _dil_merge(st["dil_out"], st["dil_lse"], do_dil, name=f"dil_merge_bwd_{tag}")
    d_dil, dbs = [], []
    for gi, (_, d) in enumerate(DIL_PATTERNS):
        dq, dk, dv, db, _ = _band_bwd(st["dil"][gi], bias[2 * gi:2 * gi + 2], st["dil_sink"][gi], st["dil_out"][gi],
                                      st["dil_lse"][gi], dmerge[gi], dmerge[3 + gi], nq=2 * d, offs=(0, 2 * d, 4 * d),
                                      g=1, bias_div=d, has_sink=False, name=f"dil{gi}_bwd_{tag}")
        d_dil.append((dq, dk, dv))
        dbs.append(db)
    o_sw, l_sw = st["swa_out"]
    dq_sw, dk_sw, dv_sw, db_sw, dsink = _band_bwd(st["swa"], bias[H_DIL:], st["swa_sink"], o_sw, l_sw, do_swa,
                                                  jnp.zeros_like(l_sw), nq=H_SWA_Q, offs=(0, H_SWA_Q, H_SWA_Q + H_SWA_KV),
                                                  g=H_SWA_Q // H_SWA_KV, bias_div=1, has_sink=True, name=f"swa_bwd_{tag}")
    dqkv = _join_heads(d_sb, d_dil, (dq_sw, dk_sw, dv_sw), name=f"join_heads_{tag}")
    return dqkv, jnp.concatenate(dbs + [db_sw], 0), dsink[:, 0, 0]


PIECES = ("ffn0", "mix", "ffn1")


def _ffn_fwd(x_in, w, gain, mod_j, tag, after=None):
    st = {"x": x_in, "w": w}
    st["h"] = _norm_fwd(x_in, _row(gain), _row(mod_j[1]), _row(mod_j[0]), name=f"norm_fwd_{tag}", after=after)
    st["a"], st["u"], st["s"] = _ffn_up(st["h"], w["gate"], w["up"], name=f"up_{tag}")
    st["f"], x_out = _mm(st["s"], w["down"], res=x_in, colscale=_row(0.5 * mod_j[2]), emit_acc=True, tm=256, tn=1024,
                         name=f"down_{tag}")
    return x_out, st


def _ffn_bwd(dx_out, st, gain, mod_j, tag, done):
    w = st["w"]

    def latest(new, old):
        return old if new is None else new

    df, dgate = _gate_bwd(dx_out, st["f"], _row(0.5 * mod_j[2]), 0.5, name=f"gate_bwd_{tag}")
    token = done({"down": _mm_tn(st["s"], df, name=f"dwd_{tag}")})
    da, du = _ffn_bwd_ds(df, w["down"], st["a"], st["u"], name=f"ds_{tag}")
    token = latest(done({"gate": _mm_tn(da, st["h"], after=token, name=f"dwg_{tag}")}), token)
    token = latest(done({"up": _mm_tn(du, st["h"], after=token, name=f"dwu_{tag}")}), token)
    dh = _mm2(da, w["gate"], du, w["up"], after=token, name=f"dh_{tag}")
    dx_in, sum_dh, sum_dhx = _norm_bwd(st["x"], dh, dx_out, _row(gain), _row(mod_j[1]), name=f"norm_bwd_{tag}")
    dmod = jnp.concatenate([sum_dh, gain * sum_dhx, dgate], 0)
    return dx_in, dmod, (1.0 + mod_j[1]) * sum_dhx[0]


def _mix_fwd(x_in, w, gain, mod_j, bias, sinks_l, tag, after=None):
    st = {"x": x_in, "w": w}
    st["h"] = _norm_fwd(x_in, _row(gain), _row(mod_j[1]), _row(mod_j[0]), name=f"norm_fwd_mix_{tag}", after=after)
    qkv = _mm(st["h"], w["qkv"], tb=True, name=f"qkv_{tag}")
    st["gates"] = _mm(st["h"], w["gates"], tb=True, name=f"gates_{tag}")
    outs, st["mix"] = _mixer_fwd(qkv, bias, sinks_l, tag)
    st["merged"], *st["t"] = _merge_fwd(*outs, st["gates"], w["br_sb"], w["br_dil"], w["br_swa"], name=f"merge_fwd_{tag}")
    st["f"], x_out = _mm(st["merged"], w["out"], res=x_in, colscale=_row(mod_j[2]), emit_acc=True, name=f"out_{tag}")
    return x_out, st


def _mix_bwd(dx_out, st, gain, mod_j, bias, tag, done):
    w = st["w"]
    df, dgate = _gate_bwd(dx_out, st["f"], _row(mod_j[2]), 1.0, name=f"gate_bwd_mix_{tag}")
    g = {"out": _mm_tn(st["merged"], df, name=f"dw_out_{tag}")}
    dmerged = _mm(df, w["out"], tb=True, name=f"dmerged_{tag}")
    dgates, do_sb, do_dil, do_swa, dbr_sb, dbr_dil, dbr_swa = _merge_bwd(
        dmerged, *st["t"], st["gates"], w["br_sb"], w["br_dil"], w["br_swa"], name=f"merge_bwd_{tag}")
    g["br_sb"] = _mm_tn(st["t"][0], dbr_sb, name=f"dw_br_sb_{tag}")
    g["br_dil"] = _mm_tn(st["t"][1], dbr_dil, name=f"dw_br_dil_{tag}")
    g["br_swa"] = _mm_tn(st["t"][2], dbr_swa, name=f"dw_br_swa_{tag}")
    dqkv, dbias, dsinks = _mixer_bwd(st["mix"], bias, do_sb, do_dil, do_swa, tag)
    g["qkv"] = _mm_tn(dqkv, st["h"], name=f"dw_qkv_{tag}")
    g["gates"] = _mm_tn(dgates, st["h"], name=f"dw_gates_{tag}")
    dh = _mm2(dqkv, w["qkv"], dgates, w["gates"], after=done(g), tm=128, name=f"dh_mix_{tag}")
    dx_in, sum_dh, sum_dhx = _norm_bwd(st["x"], dh, dx_out, _row(gain), _row(mod_j[1]), name=f"norm_bwd_mix_{tag}")
    dmod = jnp.concatenate([sum_dh, gain * sum_dhx, dgate], 0)
    return dx_in, dmod, (1.0 + mod_j[1]) * sum_dhx[0], dbias, dsinks


def _local_step(x, target, mod, gains, weights_of, rel_bias, sinks, final_gain, grads_done):
    tables = jnp.asarray(_bucket_tables())
    bias = _bias_build(rel_bias, tables, name="bias_build")
    states, h = [], x
    for l in range(DEPTH):
        st = {}
        for j, piece in enumerate(PIECES):
            w, after = weights_of(l, piece, h)
            if piece == "mix":
                h, st[piece] = _mix_fwd(h, w, gains[l, j], mod[l, j], bias, sinks[l], f"l{l}", after)
            else:
                h, st[piece] = _ffn_fwd(h, w, gains[l, j], mod[l, j], f"{piece}_l{l}", after)
        states.append(st)
    loss, dx, dfinal = _final_loss(h, target, _row(final_gain), name="final_loss")
    dmods = [[None] * 3 for _ in range(DEPTH)]
    dgains = [[None] * 3 for _ in range(DEPTH)]
    dsinks = [None] * DEPTH
    dbias = None
    for l in reversed(range(DEPTH)):
        for j in reversed(range(3)):
            piece = PIECES[j]
            done = lambda grads, l=l, piece=piece: grads_done(l, piece, grads)
            if piece == "mix":
                dx, dmods[l][j], dgains[l][j], db, dsinks[l] = _mix_bwd(dx, states[l][piece], gains[l, j], mod[l, j], bias, f"l{l}", done)
                dbias = db if dbias is None else dbias + db
            else:
                dx, dmods[l][j], dgains[l][j] = _ffn_bwd(dx, states[l][piece], gains[l, j], mod[l, j], f"{piece}_l{l}", done)
    drel = _bias_grad(dbias, tables, name="bias_grad")[:, 0, :N_BUCKETS].T
    dmod = jnp.stack([jnp.stack(m) for m in dmods])
    dgain = jnp.stack([jnp.stack(g) for g in dgains])
    return loss, dx, dmod, dgain, dfinal[0], drel, jnp.stack(dsinks)


BR_ROWS = (H_SB * HEAD_DIM, 2 * HEAD_DIM, H_SWA_Q * HEAD_DIM)


def _lanes_unshard(g, lead):
    _, rows, _ = g.shape
    r = rows // lead
    return g.reshape(N_DEV, lead, r, LANES).transpose(1, 2, 0, 3).reshape(lead, r, N_DEV * LANES)


def _lanes_shard(full):
    lead, r, _ = full.shape
    return full.reshape(lead, r, N_DEV, LANES).transpose(2, 0, 1, 3).reshape(N_DEV, lead * r, LANES)


def _pack_rows(parts, dtype):
    flat = jnp.concatenate([p.astype(dtype).reshape(-1) for p in parts])
    pad = (-flat.shape[0]) % (16 * LANES)
    if pad:
        flat = jnp.concatenate([flat, jnp.zeros((pad,), dtype)])
    return flat.reshape(-1, LANES)


def _unshard(gathered, axis):
    moved = jnp.moveaxis(gathered, 0, axis)
    shape = list(moved.shape)
    shape[axis:axis + 2] = [shape[axis] * shape[axis + 1]]
    return moved.reshape(shape)


def kernel(x, c, w_ada, b_ada, norm_gain, w_ffn_gate, w_ffn_up, w_ffn_down, w_in, w_br_sb, w_br_dil, w_br_swa, w_out, sinks, rel_bias, final_gain, loss_target, m_w_ada, m_b_ada, m_norm_gain, m_w_ffn_gate, m_w_ffn_up, m_w_ffn_down, m_w_in, m_w_br_sb, m_w_br_dil, m_w_br_swa, m_w_out, m_sinks, m_rel_bias, m_final_gain, v_w_ada, v_b_ada, v_norm_gain, v_w_ffn_gate, v_w_ffn_up, v_w_ffn_down, v_w_in, v_w_br_sb, v_w_br_dil, v_w_br_swa, v_w_out, v_sinks, v_rel_bias, v_final_gain):
    me = 4 * lax.axis_index("x") + 2 * lax.axis_index("y") + lax.axis_index("c")
    d = D_MODEL
    gate_t, up_t, in_t = jnp.swapaxes(w_ffn_gate, 2, 3), jnp.swapaxes(w_ffn_up, 2, 3), jnp.swapaxes(w_in, 1, 2)

    def piece_shards(l, piece):
        bf = lambda t: t.astype(BF16)
        if piece == "mix":
            return [bf(in_t[l]), jnp.concatenate([bf(w_br_sb[l]), bf(w_br_dil[l]), bf(w_br_swa[l])], 0), bf(w_out[l])]
        i = PIECES.index(piece) // 2
        return [bf(gate_t[l, i]), bf(up_t[l, i]), bf(w_ffn_down[l, i])]

    br_off = np.concatenate([[0], np.cumsum(BR_ROWS)])

    def piece_weights(gathered, piece):
        if piece == "mix":
            g_in, g_br, g_out = gathered
            f_in = g_in.reshape(D_QKV + D_GATES, d)
            f_br = [_lanes_unshard(g_br[:, br_off[k]:br_off[k + 1]], 1)[0] for k in range(3)]
            return {"qkv": f_in[:D_QKV], "gates": f_in[D_QKV:], "br_sb": f_br[0], "br_dil": f_br[1], "br_swa": f_br[2],
                    "out": g_out.reshape(d, d)}
        return {n: g.reshape(D_FF, d) for n, g in zip(("gate", "up", "down"), gathered)}

    small, = _all_gather([_pack_rows([c, norm_gain], F32)], name="gather_cond")
    c_all = small[:, :d // LANES].reshape(N_DEV, d)
    gains = _unshard(small[:, d // LANES:d // LANES + 6].reshape(N_DEV, DEPTH, 3, LANES), 2)

    cols = w_ada.shape[2]
    mod_cols = jnp.stack([_ada_fwd(c_all, w_ada[l], name=f"ada_fwd_l{l}") for l in range(DEPTH)])
    mod_all, = _all_gather([_pack_rows([mod_cols], F32)], name="gather_mod")
    mod_all = mod_all.reshape(N_DEV, -1)[:, :DEPTH * N_DEV * cols].reshape(N_DEV, DEPTH, N_DEV, cols)
    mod_mine = lax.dynamic_index_in_dim(mod_all, me, axis=2, keepdims=False)
    mod = (mod_mine.transpose(1, 0, 2).reshape(DEPTH, N_DEV * cols) + b_ada).reshape(DEPTH, 3, 3, d)

    order = [(l, piece) for l in range(DEPTH) for piece in PIECES]
    eager, ahead = 2, 3
    in_flight = {}
    n_tensors = 3
    first = _all_gather([s for k in range(eager) for s in piece_shards(*order[k])], after=mod_all, name="gather_first")

    def start_gather(k, after):
        l, piece = order[k]
        in_flight[k], token = _exchange_start(piece_shards(l, piece), after, gather=True, name=f"gather_{piece}_l{l}_start")
        return token

    token = first[0]
    for k in range(eager, eager + ahead - 1):
        token = start_gather(k, token)
    mod = mod + token[0, 0]

    def weights_of(l, piece, h):
        k = order.index((l, piece))
        started = eager <= k + ahead < len(order) and k + ahead not in in_flight
        token = start_gather(k + ahead, h) if started else None
        if k < eager:
            return piece_weights(first[n_tensors * k:n_tensors * (k + 1)], piece), token
        landed = _exchange_wait(in_flight[k], h if token is None else token, gather=True, name=f"gather_{piece}_l{l}_wait")
        return piece_weights(landed, piece), token

    exchanges, have = {}, {}

    def grads_done(l, piece, g):
        key = (l, piece)
        have.setdefault(key, {}).update(g)
        if piece == "mix":
            if len(have[key]) < 6:
                return None
            g = have[key]
            s_br = jnp.concatenate([_lanes_shard(g[n][None]) for n in ("br_sb", "br_dil", "br_swa")], 1)
            groups = [(("in", "br", "out"), [jnp.concatenate([g["qkv"], g["gates"]], 0).reshape(N_DEV, -1, d), s_br,
                                             g["out"].reshape(N_DEV, -1, d)])]
        elif key == order[0]:
            groups = [((n,), [t.reshape(N_DEV, -1, d)]) for n, t in g.items()]
        elif len(have[key]) < 3:
            return None
        else:
            groups = [(("gate", "up", "down"), [have[key][n].reshape(N_DEV, -1, d) for n in ("gate", "up", "down")])]
        token = None
        for names, sg in groups:
            state, token = _exchange_start(sg, sg[0], gather=False, name=f"exchange_{piece}_l{l}_{names[0]}_start")
            exchanges.setdefault(key, []).append((names, state))
        return token

    loss, dx, dmod, dgains, dfinal, drel, dsinks = _local_step(
        x[0], loss_target[0], mod, gains, weights_of, rel_bias, sinks, final_gain, grads_done)

    small_parts = [dmod, dgains, dfinal, drel.T, dsinks, loss[0, :1]]
    small_sizes = [int(np.prod(p.shape)) for p in small_parts]
    small_all, = _all_gather([_pack_rows(small_parts, F32)], name="gather_small")
    small_sum = _sum_parts([small_all], name="sum_small").reshape(-1)
    offs = np.concatenate([[0], np.cumsum(small_sizes)])
    g_b_ada = small_sum[offs[0]:offs[1]].reshape(DEPTH, 9 * d)
    g_gain_full = small_sum[offs[1]:offs[2]].reshape(DEPTH, 3, d)
    g_norm_gain = lax.dynamic_slice_in_dim(g_gain_full, me * LANES, LANES, axis=2)
    g_final = small_sum[offs[2]:offs[3]]
    g_rel = small_sum[offs[3]:offs[4]].reshape(N_SOFT, N_BUCKETS).T
    g_sinks = small_sum[offs[4]:offs[5]].reshape(DEPTH, H_SWA_Q)
    loss_total = small_sum[offs[5]]

    dmod_all = small_all.reshape(N_DEV, -1)[:, :DEPTH * 9 * d].reshape(N_DEV, DEPTH, 9 * d)
    dmod_cols = lax.dynamic_slice_in_dim(dmod_all, me * cols, cols, axis=2)
    g_w_ada = jnp.stack([_ada_bwd(c_all.T, dmod_cols[:, l], name=f"ada_bwd_l{l}") for l in range(DEPTH)])

    state = {"w_ada": (w_ada, m_w_ada, v_w_ada), "b_ada": (b_ada, m_b_ada, v_b_ada),
             "norm_gain": (norm_gain, m_norm_gain, v_norm_gain), "w_ffn_gate": (w_ffn_gate, m_w_ffn_gate, v_w_ffn_gate),
             "w_ffn_up": (w_ffn_up, m_w_ffn_up, v_w_ffn_up), "w_ffn_down": (w_ffn_down, m_w_ffn_down, v_w_ffn_down),
             "w_in": (w_in, m_w_in, v_w_in), "w_br_sb": (w_br_sb, m_w_br_sb, v_w_br_sb),
             "w_br_dil": (w_br_dil, m_w_br_dil, v_w_br_dil), "w_br_swa": (w_br_swa, m_w_br_swa, v_w_br_swa),
             "w_out": (w_out, m_w_out, v_w_out), "sinks": (sinks, m_sinks, v_sinks),
             "rel_bias": (rel_bias, m_rel_bias, v_rel_bias), "final_gain": (final_gain, m_final_gain, v_final_gain)}
    grad, update = {}, {}

    def adamw(n, g, transposed=False):
        w, m, v = (jnp.swapaxes(t, -1, -2) for t in state[n]) if transposed else state[n]
        if w.ndim == 1:
            out = tuple(t.reshape(w.shape) for t in _adamw(_row(w), _row(g), _row(m), _row(v), name=f"adamw_{n}"))
        else:
            out = _adamw(w, g, m, v, name=f"adamw_{n}")
        if transposed:
            grad[n], update[n] = jnp.swapaxes(g, -1, -2), tuple(jnp.swapaxes(t, -1, -2) for t in out)
        else:
            grad[n], update[n] = g, out

    for n, g in (("w_ada", g_w_ada), ("b_ada", g_b_ada), ("norm_gain", g_norm_gain), ("sinks", g_sinks),
                 ("rel_bias", g_rel), ("final_gain", g_final)):
        adamw(n, g)

    after = update["w_ada"][0]
    parts = {}
    for key in reversed(order):
        for names, ex_state in exchanges[key]:
            landed = _exchange_wait(ex_state, after, gather=False, name=f"exchange_{key[1]}_l{key[0]}_{names[0]}_wait")
            parts.setdefault(key, {}).update(zip(names, landed))
            after = landed[0]
    ffn_keys = [key for key in order if key[1] != "mix"]
    mix_keys = [key for key in order if key[1] == "mix"]
    sums = {n: _sum_parts([parts[key][n] for key in ffn_keys], name=f"sum_grads_{n}") for n in ("gate", "up", "down")}
    sums.update({n: _sum_parts([parts[key][n] for key in mix_keys], name=f"sum_grads_{n}") for n in ("in", "br", "out")})
    br_sums = sums["br"].reshape(DEPTH, -1, LANES)
    adamw("w_ffn_gate", sums["gate"].reshape(gate_t.shape), transposed=True)
    adamw("w_ffn_up", sums["up"].reshape(up_t.shape), transposed=True)
    adamw("w_ffn_down", sums["down"].reshape(w_ffn_down.shape))
    adamw("w_in", sums["in"].reshape(in_t.shape), transposed=True)
    adamw("w_br_sb", br_sums[:, br_off[0]:br_off[1]])
    adamw("w_br_dil", br_sums[:, br_off[1]:br_off[2]])
    adamw("w_br_swa", br_sums[:, br_off[2]:br_off[3]])
    adamw("w_out", sums["out"].reshape(w_out.shape))

    names = ["w_ada", "b_ada", "norm_gain", "w_ffn_gate", "w_ffn_up", "w_ffn_down", "w_in", "w_br_sb", "w_br_dil",
             "w_br_swa", "w_out", "sinks", "rel_bias", "final_gain"]
    return (loss_total, dx[None], *[grad[n] for n in names], *[update[n][0] for n in names],
            *[update[n][1] for n in names], *[update[n][2] for n in names])
```

```python
import math

import numpy as np
import jax
import jax.numpy as jnp
from jax import lax
from jax.experimental import pallas as pl
from jax.experimental.pallas import tpu as pltpu

F32, BF16 = jnp.float32, jnp.bfloat16

SEQ, D_MODEL, D_FF, HEAD_DIM = 2048, 1024, 2816, 64
DEPTH = 2
BLK = 128
H_SB, H_DIL, H_SWA_Q, H_SWA_KV = 4, 6, 6, 2
DIL_PATTERNS = ((128, 1), (512, 4), (2048, 16))
SWA_WINDOW = 128
N_BUCKETS, MAX_REL_DIST = 32, 2048
RMS_EPS = 1e-6
D_QKV = 2560
D_GATES = 3 * D_MODEL
ADAM_LR, ADAM_B1, ADAM_B2, ADAM_EPS, ADAM_WD, ADAM_STEP = 0.001, 0.9, 0.999, 1e-08, 0.01, 10

N_DEV = 8
LANES = 128
NEG = -1e30
SB_TILE = 256
VMEM_LIMIT_BYTES = 48 * 1024 * 1024
HBM = pl.BlockSpec(memory_space=pltpu.HBM)
MESH = pl.DeviceIdType.MESH


def _tile(n, target):
    t = (min(n, target) // LANES) * LANES
    while t >= LANES:
        if n % t == 0:
            return t
        t -= LANES
    return n


def _row_tile(r, cap):
    t = (min(r, cap) // 16) * 16
    while t > 16 and r % t:
        t -= 16
    return t


def _params(semantics=None):
    return pltpu.CompilerParams(dimension_semantics=semantics, vmem_limit_bytes=VMEM_LIMIT_BYTES)


def _dot(a, b, ca, cb):
    return lax.dot_general(a, b, (((ca,), (cb,)), ((), ())), preferred_element_type=F32)


def _sigmoid(a):
    return 1.0 / (1.0 + jnp.exp(-a))


def _row(v):
    return v.reshape(1, -1)


def _all_gather(arrs, name, after=None):
    n = len(arrs)
    ins = list(arrs) + ([] if after is None else [after])

    def body(*refs):
        x_refs, out_refs = refs[:n], refs[len(ins):len(ins) + n]
        send_sems, recv_sems, local_sems = refs[len(ins) + n:]
        x, y, c = lax.axis_index("x"), lax.axis_index("y"), lax.axis_index("c")
        me, sibling = (x, y, c), (x, y, 1 - c)
        chips = [(1 - x, y), (x, 1 - y), (1 - x, 1 - y)]

        def slot(t, px, py, pc):
            return out_refs[t].at[4 * px + 2 * py + pc]

        def copy(t, k, block, to, src=None):
            return pltpu.make_async_remote_copy(
                src_ref=slot(t, *block) if src is None else src, dst_ref=slot(t, *block),
                send_sem=send_sems.at[7 * t + k], recv_sem=recv_sems.at[7 * t + k], device_id=to, device_id_type=MESH)

        mine = [pltpu.make_async_copy(x_refs[t], slot(t, *me), local_sems.at[t]) for t in range(n)]
        for cp in mine:
            cp.start()
        first = []
        for t in range(n):
            first.append(copy(t, 0, me, sibling, src=x_refs[t]))
            first += [copy(t, 1 + j, me, (*chip, c), src=x_refs[t]) for j, chip in enumerate(chips)]
        for cp in first:
            cp.start()
        passed = []
        for j, chip in enumerate(chips):
            for t in range(n):
                copy(t, 1 + j, (*chip, c), me).wait_recv()
                passed.append(copy(t, 4 + j, (*chip, c), sibling))
                passed[-1].start()
        for t in range(n):
            copy(t, 0, sibling, me).wait_recv()
        for j, chip in enumerate(chips):
            for t in range(n):
                copy(t, 4 + j, (*chip, 1 - c), me).wait_recv()
        for cp in first + passed:
            cp.wait_send()
        for cp in mine:
            cp.wait()

    return pl.pallas_call(
        body, name=name, out_shape=[jax.ShapeDtypeStruct((N_DEV,) + a.shape, a.dtype) for a in arrs],
        in_specs=[HBM] * n + [pl.BlockSpec(memory_space=pl.ANY)] * (len(ins) - n), out_specs=[HBM] * n,
        scratch_shapes=[pltpu.SemaphoreType.DMA((7 * n,)), pltpu.SemaphoreType.DMA((7 * n,)), pltpu.SemaphoreType.DMA((n,))],
    )(*ins)


def _direct_copies(x_refs, land_refs, send_sems, recv_sems, local_sems, gather):
    x, y, c = lax.axis_index("x"), lax.axis_index("y"), lax.axis_index("c")
    me = 4 * x + 2 * y + c
    sends, recvs = [], []
    for k in range(1, N_DEV):
        px = 1 - x if (k >> 2) & 1 else x
        py = 1 - y if (k >> 1) & 1 else y
        pc = 1 - c if k & 1 else c
        peer = 4 * px + 2 * py + pc
        for t, (x_ref, land_ref) in enumerate(zip(x_refs, land_refs)):
            sem = 7 * t + k - 1
            for out, src, slot in ((sends, x_ref if gather else x_ref.at[peer], me),
                                   (recvs, x_ref if gather else x_ref.at[me], peer)):
                out.append(pltpu.make_async_remote_copy(
                    src_ref=src, dst_ref=land_ref.at[slot], send_sem=send_sems.at[sem], recv_sem=recv_sems.at[sem],
                    device_id=(px, py, pc), device_id_type=MESH))
    own = [pltpu.make_async_copy(x_ref if gather else x_ref.at[me], land_ref.at[me], local_sems.at[t])
           for t, (x_ref, land_ref) in enumerate(zip(x_refs, land_refs))]
    return sends, recvs, own


SEM =pl.BlockSpec(memory_space=pltpu.SEMAPHORE)
ANY = pl.BlockSpec(memory_space=pl.ANY)
SIDE_EFFECT = pltpu.SideEffectType.DATAFLOW_SIDE_EFFECTING


def _exchange_start(arrs, after, *, gather, name):
    n = len(arrs)
    lands = [lax.empty(((N_DEV,) + a.shape) if gather else a.shape, a.dtype) for a in arrs]

    def body(*refs):
        sends, _, own = _direct_copies(refs[:n], refs[n:2 * n], *refs[2 * n + 1:2 * n + 4], gather)
        for cp in own + sends:
            cp.start()
        refs[-1][...] = jnp.zeros_like(refs[-1])

    ops = [pltpu.with_memory_space_constraint(a, pltpu.HBM) for a in list(arrs) + lands]
    out = pl.pallas_call(
        body, name=name,
        out_shape=(pltpu.SemaphoreType.DMA((7 * n,)), pltpu.SemaphoreType.DMA((7 * n,)), pltpu.SemaphoreType.DMA((n,)),
                   *[pltpu.HBM(a.shape, a.dtype) for a in ops], jax.ShapeDtypeStruct((8, LANES), F32)),
        in_specs=[HBM] * (2 * n) + [ANY],
        out_specs=(SEM, SEM, SEM, *[HBM] * (2 * n), pl.BlockSpec(memory_space=pltpu.VMEM)),
        input_output_aliases={t: 3 + t for t in range(2 * n)},
        compiler_params=pltpu.CompilerParams(has_side_effects=SIDE_EFFECT),
    )(*ops, after)
    return (out[:3], out[3:3 + n], out[3 + n:3 + 2 * n]), out[-1]


def _exchange_wait(state, after, *, gather, name):
    sems, arrs, lands = state
    n = len(arrs)

    def body(*refs):
        sends, recvs, own = _direct_copies(refs[:n], refs[n:2 * n], *refs[2 * n:2 * n + 3], gather)
        for cp in own:
            cp.wait()
        for cp in sends:
            cp.wait_send()
        for cp in recvs:
            cp.wait_recv()

    out = pl.pallas_call(
        body, name=name, out_shape=tuple(pltpu.HBM(a.shape, a.dtype) for a in list(arrs) + list(lands)),
        in_specs=[HBM] * (2 * n) + [SEM, SEM, SEM, ANY], out_specs=tuple([HBM] * (2 * n)),
        input_output_aliases={t: t for t in range(2 * n)},
        compiler_params=pltpu.CompilerParams(has_side_effects=SIDE_EFFECT),
    )(*arrs, *lands, *sems, after)
    return out[n:]


def _sum_parts(groups, name):
    n, r, cdim = groups[0].shape
    tr = _row_tile(r, max(16, (1 << 21) // (n * cdim * groups[0].dtype.itemsize)))
    steps = r // tr

    def body(*refs):
        o_ref = refs[-1]
        gg = pl.program_id(0)
        for gi in range(len(groups)):
            @pl.when(gg == gi)
            def _(gi=gi):
                acc = refs[gi][0].astype(F32)
                for k in range(1, n):
                    acc = acc + refs[gi][k].astype(F32)
                o_ref[...] = acc

    def in_spec(gi):
        return pl.BlockSpec((n, tr, cdim), lambda gg, i: (0, jnp.where(gg == gi, i, 0), 0))

    return pl.pallas_call(
        body, name=name, out_shape=jax.ShapeDtypeStruct((len(groups) * r, cdim), F32), grid=(len(groups), steps),
        in_specs=[in_spec(gi) for gi in range(len(groups))],
        out_specs=pl.BlockSpec((tr, cdim), lambda gg, i: (gg * steps + i, 0)),
        compiler_params=_params(("parallel", "parallel")),
    )(*groups)


def _mm_tn(a, b, *, name, after=None, tm=512, tn=1024):
    k, m = a.shape
    n = b.shape[1]
    tm, tn = _tile(m, tm), _tile(n, tn)

    def body(a_ref, b_ref, *rest):
        o_ref, at_ref = rest[-2], rest[-1]

        @pl.when(pl.program_id(1) == 0)
        def _():
            at_ref[...] = a_ref[...].astype(BF16).T

        o_ref[...] = _dot(at_ref[...], b_ref[...].astype(BF16), 1, 0).astype(BF16)

    ins = [a, b] + ([] if after is None else [after])
    return pl.pallas_call(
        body, name=name, out_shape=jax.ShapeDtypeStruct((m, n), BF16), grid=(m // tm, n // tn),
        in_specs=[pl.BlockSpec((k, tm), lambda i, j: (0, i)), pl.BlockSpec((k, tn), lambda i, j: (0, j))] + [ANY] * (len(ins) - 2),
        out_specs=pl.BlockSpec((tm, tn), lambda i, j: (i, j)),
        scratch_shapes=[pltpu.VMEM((tm, k), BF16)], compiler_params=_params(("parallel", "arbitrary")),
    )(*ins)


def _mm2(a1, b1, a2, b2, *, name, after=None, tm=256, tn=1024):
    m = a1.shape[0]
    n = b1.shape[1]
    tm, tn = _tile(m, tm), _tile(n, tn)

    def body(a1_ref, b1_ref, a2_ref, b2_ref, *rest):
        rest[-1][...] = (_dot(a1_ref[...].astype(BF16), b1_ref[...], 1, 0)
                         + _dot(a2_ref[...].astype(BF16), b2_ref[...], 1, 0))

    ins = [a1, b1, a2, b2] + ([] if after is None else [after])

    def a_spec(t):
        return pl.BlockSpec((tm, t.shape[1]), lambda i, j: (i, 0))

    def b_spec(t):
        return pl.BlockSpec((t.shape[0], tn), lambda i, j: (0, j))

    return pl.pallas_call(
        body, name=name, out_shape=jax.ShapeDtypeStruct((m, n), F32), grid=(m // tm, n // tn),
        in_specs=[a_spec(a1), b_spec(b1), a_spec(a2), b_spec(b2)] + [ANY] * (len(ins) - 4),
        out_specs=pl.BlockSpec((tm, tn), lambda i, j: (i, j)), compiler_params=_params(("parallel", "parallel")),
    )(*ins)


def _mm(a, b, *, name, ta=False, tb=False, res=None, colscale=None, emit_acc=False,
        out_dtype=F32, tm=512, tn=512):
    m, k = (a.shape[1], a.shape[0]) if ta else a.shape
    n = b.shape[0] if tb else b.shape[1]
    tm, tn = _tile(m, tm), _tile(n, tn)
    ca, cb = (0 if ta else 1), (1 if tb else 0)
    a_spec = pl.BlockSpec((k, tm), lambda i, j: (0, i)) if ta else pl.BlockSpec((tm, k), lambda i, j: (i, 0))
    b_spec = pl.BlockSpec((tn, k), lambda i, j: (j, 0)) if tb else pl.BlockSpec((k, tn), lambda i, j: (0, j))
    tile = pl.BlockSpec((tm, tn), lambda i, j: (i, j))
    ins, in_specs = [a, b], [a_spec, b_spec]
    if res is not None:
        ins.append(res)
        in_specs.append(tile)
    if colscale is not None:
        ins.append(colscale)
        in_specs.append(pl.BlockSpec((1, tn), lambda i, j: (0, j)))
    n_in = len(ins)

    def body(*refs):
        outs = refs[n_in:]
        acc = _dot(refs[0][...].astype(BF16), refs[1][...].astype(BF16), ca, cb)
        val, p = acc, 2
        if res is not None:
            r_val, p = refs[p][...], p + 1
        if colscale is not None:
            val = val * refs[p][...]
        if res is not None:
            val = r_val + val
        if emit_acc:
            outs[0][...] = acc
        outs[-1][...] = val.astype(out_dtype)

    out_shape = [jax.ShapeDtypeStruct((m, n), out_dtype)]
    out_specs = [tile]
    if emit_acc:
        out_shape.insert(0, jax.ShapeDtypeStruct((m, n), F32))
        out_specs.insert(0, tile)
    out = pl.pallas_call(
        body, name=name, out_shape=out_shape, grid=(m // tm, n // tn), in_specs=in_specs, out_specs=out_specs,
        compiler_params=_params(("parallel", "parallel")),
    )(*ins)
    return out if emit_acc else out[0]


def _norm_fwd(x, g, scale, shift, name, after=None):
    s, d = x.shape
    tr = 256

    def body(x_ref, g_ref, sc_ref, sh_ref, *rest):
        xv = x_ref[...]
        rstd = lax.rsqrt(jnp.mean(xv * xv, axis=-1, keepdims=True) + RMS_EPS)
        rest[-1][...] = (xv * rstd * g_ref[...] * (1.0 + sc_ref[...]) + sh_ref[...]).astype(BF16)

    rowspec = pl.BlockSpec((1, d), lambda i: (0, 0))
    ins = [x, g, scale, shift] + ([] if after is None else [after])
    return pl.pallas_call(
        body, name=name, out_shape=jax.ShapeDtypeStruct((s, d), BF16), grid=(s // tr,),
        in_specs=[pl.BlockSpec((tr, d), lambda i: (i, 0)), rowspec, rowspec, rowspec] + [ANY] * (len(ins) - 4),
        out_specs=pl.BlockSpec((tr, d), lambda i: (i, 0)),
        compiler_params=_params(("parallel",)),
    )(*ins)


def _norm_bwd(x, dh, dres, g, scale, name):
    s, d = x.shape
    tr = 256

    def body(x_ref, dh_ref, dr_ref, g_ref, sc_ref, dx_ref, a_ref, b_ref):
        @pl.when(pl.program_id(0) == 0)
        def _():
            a_ref[...] = jnp.zeros_like(a_ref)
            b_ref[...] = jnp.zeros_like(b_ref)

        xv = x_ref[...]
        rstd = lax.rsqrt(jnp.mean(xv * xv, axis=-1, keepdims=True) + RMS_EPS)
        xhat = xv * rstd
        dhv = dh_ref[...]
        dxhat = dhv * (g_ref[...] * (1.0 + sc_ref[...]))
        mean_term = jnp.mean(dxhat * xhat, axis=-1, keepdims=True)
        dx_ref[...] = dr_ref[...] + rstd * (dxhat - xhat * mean_term)
        a_ref[...] += jnp.sum(dhv, axis=0, keepdims=True)
        b_ref[...] += jnp.sum(dhv * xhat, axis=0, keepdims=True)

    rowspec = pl.BlockSpec((1, d), lambda i: (0, 0))
    tile = pl.BlockSpec((tr, d), lambda i: (i, 0))
    return pl.pallas_call(
        body, name=name,
        out_shape=[jax.ShapeDtypeStruct((s, d), F32), jax.ShapeDtypeStruct((1, d), F32), jax.ShapeDtypeStruct((1, d), F32)],
        grid=(s // tr,), in_specs=[tile, tile, tile, rowspec, rowspec], out_specs=[tile, rowspec, rowspec],
        compiler_params=_params(("arbitrary",)),
    )(x, dh, dres, g, scale)


def _gate_bwd(dxn, f, colscale, coef, name):
    s, d = dxn.shape
    tr = 256

    def body(dx_ref, f_ref, cs_ref, df_ref, dg_ref):
        @pl.when(pl.program_id(0) == 0)
        def _():
            dg_ref[...] = jnp.zeros_like(dg_ref)

        dxv = dx_ref[...]
        df_ref[...] = (dxv * cs_ref[...]).astype(BF16)
        dg_ref[...] += coef * jnp.sum(dxv * f_ref[...], axis=0, keepdims=True)

    rowspec = pl.BlockSpec((1, d), lambda i: (0, 0))
    tile = pl.BlockSpec((tr, d), lambda i: (i, 0))
    return pl.pallas_call(
        body, name=name, out_shape=[jax.ShapeDtypeStruct((s, d), BF16), jax.ShapeDtypeStruct((1, d), F32)],
        grid=(s // tr,), in_specs=[tile, tile, rowspec], out_specs=[tile, rowspec],
        compiler_params=_params(("arbitrary",)),
    )(dxn, f, colscale)


def _ffn_up(h, wg, wu, name):
    s, d = h.shape
    f = wg.shape[0]
    tm, tn = s, _tile(f, 256)

    def body(h_ref, wg_ref, wu_ref, a_ref, u_ref, s_ref):
        hv = h_ref[...]
        a = _dot(hv, wg_ref[...], 1, 1)
        u = _dot(hv, wu_ref[...], 1, 1)
        a_ref[...] = a.astype(BF16)
        u_ref[...] = u.astype(BF16)
        s_ref[...] = (a * _sigmoid(a) * u).astype(BF16)

    tile = pl.BlockSpec((tm, tn), lambda i, j: (i, j))
    wspec = pl.BlockSpec((tn, d), lambda i, j: (j, 0))
    return pl.pallas_call(
        body, name=name,
        out_shape=[jax.ShapeDtypeStruct((s, f), BF16), jax.ShapeDtypeStruct((s, f), BF16), jax.ShapeDtypeStruct((s, f), BF16)],
        grid=(s // tm, f // tn), in_specs=[pl.BlockSpec((tm, d), lambda i, j: (i, 0)), wspec, wspec],
        out_specs=[tile, tile, tile], compiler_params=_params(("parallel", "parallel")),
    )(h, wg, wu)


def _ffn_bwd_ds(df, wd, a, u, name):
    s, d = df.shape
    f = wd.shape[0]
    tm, tn = 1024, _tile(f, 256)

    def body(df_ref, wd_ref, a_ref, u_ref, da_ref, du_ref):
        ds = _dot(df_ref[...], wd_ref[...], 1, 1)
        av = a_ref[...].astype(F32)
        sg = _sigmoid(av)
        da_ref[...] = (ds * u_ref[...].astype(F32) * (sg * (1.0 + av * (1.0 - sg)))).astype(BF16)
        du_ref[...] = (ds * (av * sg)).astype(BF16)

    tile = pl.BlockSpec((tm, tn), lambda i, j: (i, j))
    return pl.pallas_call(
        body, name=name, out_shape=[jax.ShapeDtypeStruct((s, f), BF16), jax.ShapeDtypeStruct((s, f), BF16)],
        grid=(s // tm, f // tn),
        in_specs=[pl.BlockSpec((tm, d), lambda i, j: (i, 0)), pl.BlockSpec((tn, d), lambda i, j: (j, 0)), tile, tile],
        out_specs=[tile, tile], compiler_params=_params(("parallel", "parallel")),
    )(df, wd, a, u)


def _merge_fwd(o_sb, o_dil, o_swa, gates, wb_sb, wb_dil, wb_swa, name):
    s, d = SEQ, D_MODEL
    tm = 256

    def body(osb_ref, odl_ref, osw_ref, g_ref, wsb_ref, wdl_ref, wsw_ref, m_ref, tsb_ref, tdl_ref, tsw_ref):
        for h in range(osb_ref.shape[0]):
            tsb_ref[:, h * HEAD_DIM:(h + 1) * HEAD_DIM] = osb_ref[h].astype(BF16)
        for h in range(osw_ref.shape[0]):
            tsw_ref[:, h * HEAD_DIM:(h + 1) * HEAD_DIM] = osw_ref[h].astype(BF16)
        tdl_ref[...] = odl_ref[...].astype(BF16)
        acc = _sigmoid(g_ref[:, 0:d]) * _dot(tsb_ref[...], wsb_ref[...], 1, 0)
        acc += _sigmoid(g_ref[:, d:2 * d]) * _dot(tdl_ref[...], wdl_ref[...], 1, 0)
        acc += _sigmoid(g_ref[:, 2 * d:3 * d]) * _dot(tsw_ref[...], wsw_ref[...], 1, 0)
        m_ref[...] = acc.astype(BF16)

    def rows(w):
        return pl.BlockSpec((tm, w), lambda i: (i, 0))

    def heads(n):
        return pl.BlockSpec((n, tm, HEAD_DIM), lambda i: (0, i, 0))

    def whole(w):
        return pl.BlockSpec((w, d), lambda i: (0, 0))

    return pl.pallas_call(
        body, name=name, out_shape=[jax.ShapeDtypeStruct((s, w), BF16) for w in (d, 256, 128, 384)], grid=(s // tm,),
        in_specs=[heads(H_SB), rows(128), heads(H_SWA_Q), rows(3 * d), whole(256), whole(128), whole(384)],
        out_specs=[rows(d), rows(256), rows(128), rows(384)], compiler_params=_params(("parallel",)),
    )(o_sb, o_dil, o_swa, gates, wb_sb, wb_dil, wb_swa)


def _merge_bwd(dmerged, t_sb, t_dil, t_swa, gates, wb_sb, wb_dil, wb_swa, name):
    s, d = SEQ, D_MODEL
    tm = 256

    def body(dm_ref, tsb_ref, tdl_ref, tsw_ref, g_ref, wsb_ref, wdl_ref, wsw_ref,
             dg_ref, dosb_ref, dodl_ref, dosw_ref, dbsb_ref, dbdl_ref, dbsw_ref):
        dm = dm_ref[...]
        for idx, (t_ref, w_ref, do_ref, db_ref) in enumerate((
                (tsb_ref, wsb_ref, dosb_ref, dbsb_ref), (tdl_ref, wdl_ref, dodl_ref, dbdl_ref),
                (tsw_ref, wsw_ref, dosw_ref, dbsw_ref))):
            w = w_ref[...]
            br = _dot(t_ref[...], w, 1, 0)
            sg = _sigmoid(g_ref[:, idx * d:(idx + 1) * d])
            dbr = (dm * sg).astype(BF16)
            dg_ref[:, idx * d:(idx + 1) * d] = dm * br * (sg * (1.0 - sg))
            db_ref[...] = dbr
            do = _dot(dbr, w, 1, 1)
            if len(do_ref.shape) == 2:
                do_ref[...] = do
            else:
                for h in range(do_ref.shape[0]):
                    do_ref[h] = do[:, h * HEAD_DIM:(h + 1) * HEAD_DIM]

    def rows(w):
        return pl.BlockSpec((tm, w), lambda i: (i, 0))

    def heads(n):
        return pl.BlockSpec((n, tm, HEAD_DIM), lambda i: (0, i, 0))

    def whole(w):
        return pl.BlockSpec((w, d), lambda i: (0, 0))

    def shp(w, dt):
        return jax.ShapeDtypeStruct((s, w), dt)

    def hshp(n):
        return jax.ShapeDtypeStruct((n, s, HEAD_DIM), F32)

    return pl.pallas_call(
        body, name=name,
        out_shape=[shp(3 * d, F32), hshp(H_SB), shp(128, F32), hshp(H_SWA_Q), shp(d, BF16), shp(d, BF16), shp(d, BF16)],
        grid=(s // tm,),
        in_specs=[rows(d), rows(256), rows(128), rows(384), rows(3 * d), whole(256), whole(128), whole(384)],
        out_specs=[rows(3 * d), heads(H_SB), rows(128), heads(H_SWA_Q), rows(d), rows(d), rows(d)],
        compiler_params=_params(("parallel",)),
    )(dmerged, t_sb, t_dil, t_swa, gates, wb_sb, wb_dil, wb_swa)


def _final_loss(x, target, g, name):
    s, d = x.shape
    tr = 256

    def body(x_ref, t_ref, g_ref, loss_ref, dx_ref, dg_ref):
        @pl.when(pl.program_id(0) == 0)
        def _():
            loss_ref[...] = jnp.zeros_like(loss_ref)
            dg_ref[...] = jnp.zeros_like(dg_ref)

        xv = x_ref[...]
        gv = g_ref[...]
        rstd = lax.rsqrt(jnp.mean(xv * xv, axis=-1, keepdims=True) + RMS_EPS)
        xhat = xv * rstd
        err = xhat * gv - t_ref[...]
        loss_ref[...] += 0.5 * jnp.sum(jnp.mean(err * err, axis=-1, keepdims=True))
        dy = err * (1.0 / d)
        dxhat = dy * gv
        mean_term = jnp.mean(dxhat * xhat, axis=-1, keepdims=True)
        dx_ref[...] = rstd * (dxhat - xhat * mean_term)
        dg_ref[...] += jnp.sum(dy * xhat, axis=0, keepdims=True)

    rowspec = pl.BlockSpec((1, d), lambda i: (0, 0))
    tile = pl.BlockSpec((tr, d), lambda i: (i, 0))
    return pl.pallas_call(
        body, name=name,
        out_shape=[jax.ShapeDtypeStruct((1, LANES), F32), jax.ShapeDtypeStruct((s, d), F32), jax.ShapeDtypeStruct((1, d), F32)],
        grid=(s // tr,), in_specs=[tile, tile, rowspec],
        out_specs=[pl.BlockSpec((1, LANES), lambda i: (0, 0)), tile, rowspec],
        compiler_params=_params(("arbitrary",)),
    )(x, target, g)


def _adamw(w, g, m, v, name):
    shape = w.shape
    cols = shape[-1]
    rows = int(np.prod(shape[:-1])) if len(shape) > 1 else 1
    tr = rows
    for cand in (1024, 512, 256, 128, 64, 32, 16, 8):
        if rows % cand == 0 and rows > cand and cand * cols * 4 <= (1 << 21):
            tr = cand
            break

    def body(w_ref, g_ref, m_ref, v_ref, d_ref, nm_ref, nv_ref):
        gv = g_ref[...]
        nm = ADAM_B1 * m_ref[...] + (1.0 - ADAM_B1) * gv
        nv = ADAM_B2 * v_ref[...] + (1.0 - ADAM_B2) * (gv * gv)
        m_hat = nm / (1.0 - ADAM_B1 ** ADAM_STEP)
        v_hat = nv / (1.0 - ADAM_B2 ** ADAM_STEP)
        d_ref[...] = -ADAM_LR * (m_hat / (jnp.sqrt(v_hat) + ADAM_EPS) + ADAM_WD * w_ref[...])
        nm_ref[...] = nm
        nv_ref[...] = nv

    tile = pl.BlockSpec((tr, cols), lambda i: (i, 0))
    flat = [t.reshape(rows, cols) for t in (w, g, m, v)]
    out = pl.pallas_call(
        body, name=name, out_shape=[jax.ShapeDtypeStruct((rows, cols), F32)] * 3, grid=(rows // tr,),
        in_specs=[tile] * 4, out_specs=[tile] * 3, compiler_params=_params(("parallel",)),
    )(*flat)
    return tuple(t.reshape(shape) for t in out)


def _ada_fwd(c_all, w, name):
    n = w.shape[1]

    def body(c_ref, w_ref, o_ref):
        cv = c_ref[...]
        o_ref[...] = jnp.dot(cv * _sigmoid(cv), w_ref[...], preferred_element_type=F32, precision=lax.Precision.HIGHEST)

    return pl.pallas_call(body, name=name, out_shape=jax.ShapeDtypeStruct((N_DEV, n), F32), compiler_params=_params())(c_all, w)


def _ada_bwd(c_all_t, dmod, name):
    n = dmod.shape[1]

    def body(c_ref, d_ref, o_ref):
        cv = c_ref[...]
        o_ref[...] = jnp.dot(cv * _sigmoid(cv), d_ref[...], preferred_element_type=F32, precision=lax.Precision.HIGHEST)

    return pl.pallas_call(body, name=name, out_shape=jax.ShapeDtypeStruct((D_MODEL, n), F32), compiler_params=_params())(c_all_t, dmod)


def _bucket_tables():
    rel = np.arange(BLK)[:, None] + BLK - np.arange(2 * BLK)[None, :]
    max_exact = N_BUCKETS // 2

    def bucket(n):
        nf = np.maximum(n, 1).astype(np.float32)
        large = max_exact + (np.log(nf / np.float32(max_exact)) / np.float32(math.log(MAX_REL_DIST / max_exact))
                             * np.float32(N_BUCKETS - max_exact)).astype(np.int32)
        return np.where(n < max_exact, n, np.minimum(large, N_BUCKETS - 1))

    tabs = []
    for dil, max_dist in ((1, 128), (4, 128), (16, 128), (1, SWA_WINDOW - 1)):
        in_band = (rel >= 0) & (rel <= max_dist)
        tabs.append(np.where(in_band, bucket(np.maximum(rel, 0) * dil), -1))
    return np.stack(tabs).astype(np.int32)


N_SOFT = H_DIL + H_SWA_Q


def _table_of_head(h):
    return jnp.minimum(h // 2, 3)


def _bias_build(rel_bias, tables, name):
    def body(rel_ref, t_ref, o_ref):
        h = pl.program_id(0)
        tb = t_ref[0]
        out = jnp.full((BLK, 2 * BLK), NEG, F32)
        for b in range(N_BUCKETS):
            out = jnp.where(tb == b, rel_ref[b, h], out)
        o_ref[0] = out

    return pl.pallas_call(
        body, name=name, out_shape=jax.ShapeDtypeStruct((N_SOFT, BLK, 2 * BLK), F32), grid=(N_SOFT,),
        in_specs=[pl.BlockSpec(memory_space=pltpu.SMEM),
                  pl.BlockSpec((1, BLK, 2 * BLK), lambda h: (_table_of_head(h), 0, 0))],
        out_specs=pl.BlockSpec((1, BLK, 2 * BLK), lambda h: (h, 0, 0)),
        compiler_params=_params(("parallel",)),
    )(rel_bias, tables)


def _bias_grad(dbias, tables, name):
    def body(d_ref, t_ref, o_ref):
        tb = t_ref[0]
        dv = d_ref[0]
        lane = lax.broadcasted_iota(jnp.int32, (1, LANES), 1)
        out = jnp.zeros((1, LANES), F32)
        for b in range(N_BUCKETS):
            out = jnp.where(lane == b, jnp.sum(jnp.where(tb == b, dv, 0.0)), out)
        o_ref[0] = out

    return pl.pallas_call(
        body, name=name, out_shape=jax.ShapeDtypeStruct((N_SOFT, 1, LANES), F32), grid=(N_SOFT,),
        in_specs=[pl.BlockSpec((1, BLK, 2 * BLK), lambda h: (h, 0, 0)),
                  pl.BlockSpec((1, BLK, 2 * BLK), lambda h: (_table_of_head(h), 0, 0))],
        out_specs=pl.BlockSpec((1, 1, LANES), lambda h: (h, 0, 0)),
        compiler_params=_params(("parallel",)),
    )(dbias, tables)


def _band_layout(g, bias_div):
    assert g == 1 or bias_div == 1
    return bias_div if g == 1 else 1


def _band_specs(length, g, bias_div, offs):
    ns = _band_layout(g, bias_div)

    def seqs(off, div=1):
        return pl.BlockSpec((ns, length, HEAD_DIM), lambda s: (off // ns + s // div, 0, 0))

    xspecs = [seqs(offs[0]), seqs(offs[1], g), seqs(offs[2], g)]
    bspec = pl.BlockSpec((1, BLK, 2 * BLK), lambda s: (s, 0, 0))
    sspec = pl.BlockSpec((ns, 1, LANES), lambda s: (s, 0, 0))
    colspec = pl.BlockSpec((ns, length, 1), lambda s: (s, 0, 0))
    return xspecs, seqs(0), seqs(0, g), bspec, sspec, colspec


def _band_sweep(length, ns, one):
    nblk = length // BLK
    for qq in range(ns):
        if ns * nblk <= 16:
            for i in range(nblk):
                one(qq, i * BLK, max(i - 1, 0) * BLK, i == 0)
        else:
            def step(i, carry, qq=qq):
                one(qq, pl.multiple_of(i * BLK, BLK), pl.multiple_of(jnp.maximum(i - 1, 0) * BLK, BLK), i == 0)
                return carry

            lax.fori_loop(0, nblk, step, 0, unroll=2)


def _band_scores(q_ref, k_ref, b_ref, qq, kq, bq, cur, prv, first):
    qv = q_ref[qq, pl.ds(cur, BLK), :]
    bv = b_ref[bq]
    if first is True:
        sp = jnp.full((BLK, BLK), NEG, F32)
    else:
        sp = _dot(qv, k_ref[kq, pl.ds(prv, BLK), :], 1, 1) + bv[:, :BLK]
        sp = sp if first is False else jnp.where(first, NEG, sp)
    sc = _dot(qv, k_ref[kq, pl.ds(cur, BLK), :], 1, 1) + bv[:, BLK:]
    return qv, sp, sc


def _band_fwd(x, bias, sink, *, nq, offs, g, bias_div, has_sink, name):
    length = x.shape[1]
    ns = _band_layout(g, bias_div)

    def body(q_ref, k_ref, v_ref, b_ref, s_ref, o_ref, lse_ref):
        def one(qq, cur, prv, first):
            kq, bq = qq, 0
            _, sp, sc = _band_scores(q_ref, k_ref, b_ref, qq, kq, bq, cur, prv, first)
            m = jnp.maximum(jnp.max(sp, axis=1, keepdims=True), jnp.max(sc, axis=1, keepdims=True))
            if has_sink:
                sk = s_ref[qq][:, :1]
                m = jnp.maximum(m, sk)
            pp, pc = jnp.exp(sp - m), jnp.exp(sc - m)
            den = jnp.sum(pp, axis=1, keepdims=True) + jnp.sum(pc, axis=1, keepdims=True)
            if has_sink:
                den = den + jnp.exp(sk - m)
            acc = (_dot(pp.astype(BF16), v_ref[kq, pl.ds(prv, BLK), :], 1, 0)
                   + _dot(pc.astype(BF16), v_ref[kq, pl.ds(cur, BLK), :], 1, 0))
            o_ref[qq, pl.ds(cur, BLK), :] = acc / den
            lse_ref[qq, pl.ds(cur, BLK), :] = m + jnp.log(den)

        _band_sweep(length, ns, one)

    xspecs, qspec, _, bspec, sspec, colspec = _band_specs(length, g, bias_div, offs)
    return pl.pallas_call(
        body, name=name,
        out_shape=[jax.ShapeDtypeStruct((nq, length, HEAD_DIM), F32), jax.ShapeDtypeStruct((nq, length, 1), F32)],
        grid=(nq // ns,), in_specs=xspecs + [bspec, sspec],
        out_specs=[qspec, colspec], compiler_params=_params(("parallel",)),
    )(x, x, x, bias, sink)


def _band_bwd(x, bias, sink, o, lse, do, dlse, *, nq, offs, g, bias_div, has_sink, name):
    length = x.shape[1]
    ns = _band_layout(g, bias_div)
    nk, nbias = nq // g, nq // bias_div

    def body(q_ref, k_ref, v_ref, b_ref, s_ref, o_ref, lse_ref, do_ref, dlse_ref,
             dq_ref, dk_ref, dv_ref, db_ref, dsk_ref, dkp_ref, dvp_ref):
        for ref in (db_ref, dsk_ref, dkp_ref, dvp_ref):
            ref[...] = jnp.zeros_like(ref)

        @pl.when(pl.program_id(0) % g == 0)
        def _():
            dk_ref[...] = jnp.zeros_like(dk_ref)
            dv_ref[...] = jnp.zeros_like(dv_ref)

        def one(qq, cur, prv, first):
            kq, bq = qq, 0
            qv, sp, sc = _band_scores(q_ref, k_ref, b_ref, qq, kq, bq, cur, prv, first)
            rows, prow = pl.ds(cur, BLK), pl.ds(prv, BLK)
            lse_v = lse_ref[qq, rows, :]
            pp, pc = jnp.exp(sp - lse_v), jnp.exp(sc - lse_v)
            dov = do_ref[qq, rows, :]
            dob = dov.astype(BF16)
            coef = dlse_ref[qq, rows, :] - jnp.sum(dov * o_ref[qq, rows, :], axis=1, keepdims=True)
            dsp = pp * (_dot(dob, v_ref[kq, prow, :], 1, 1) + coef)
            dsc = pc * (_dot(dob, v_ref[kq, rows, :], 1, 1) + coef)
            dspb, dscb = dsp.astype(BF16), dsc.astype(BF16)
            dq_ref[qq, rows, :] = ((_dot(dspb, k_ref[kq, prow, :], 1, 0) + _dot(dscb, k_ref[kq, rows, :], 1, 0))
                                   * (HEAD_DIM ** -0.5))
            dk_ref[kq, rows, :] += _dot(dscb, qv, 0, 0)
            dkp_ref[kq, prow, :] += _dot(dspb, qv, 0, 0)
            dv_ref[kq, rows, :] += _dot(pc.astype(BF16), dob, 0, 0)
            dvp_ref[kq, prow, :] += _dot(pp.astype(BF16), dob, 0, 0)
            db_ref[bq, :, :BLK] += dsp
            db_ref[bq, :, BLK:] += dsc
            if has_sink:
                dsk_ref[qq] += jnp.sum(jnp.exp(s_ref[qq][:, :1] - lse_v) * coef)

        _band_sweep(length, ns, one)
        dk_ref[...] += dkp_ref[...]
        dv_ref[...] += dvp_ref[...]

    xspecs, qspec, kvspec, bspec, sspec, colspec = _band_specs(length, g, bias_div, offs)
    return pl.pallas_call(
        body, name=name,
        out_shape=[jax.ShapeDtypeStruct((nq, length, HEAD_DIM), F32), jax.ShapeDtypeStruct((nk, length, HEAD_DIM), F32),
                   jax.ShapeDtypeStruct((nk, length, HEAD_DIM), F32), jax.ShapeDtypeStruct((nbias, BLK, 2 * BLK), F32),
                   jax.ShapeDtypeStruct((nq, 1, LANES), F32)],
        grid=(nq // ns,),
        in_specs=xspecs + [bspec, sspec, qspec, colspec, qspec, colspec],
        out_specs=[qspec, kvspec, kvspec, bspec, sspec],
        scratch_shapes=[pltpu.VMEM((ns, length, HEAD_DIM), F32), pltpu.VMEM((ns, length, HEAD_DIM), F32)],
        compiler_params=_params(("arbitrary",)),
    )(x, x, x, bias, sink, o, lse, do, dlse)


TOK_TILE = 512


def _dil_merge(outs, lses, dout, name):
    tr = TOK_TILE
    dils = [d for _, d in DIL_PATTERNS]
    n = len(dils)
    o4 = [o.reshape(2, d, SEQ // d, HEAD_DIM) for o, d in zip(outs, dils)]
    l4 = [l.reshape(2, d, SEQ // d, 1) for l, d in zip(lses, dils)]
    o_specs = [pl.BlockSpec((2, d, tr // d, HEAD_DIM), lambda i: (0, 0, i, 0)) for d in dils]
    l_specs = [pl.BlockSpec((2, d, tr // d, 1), lambda i: (0, 0, i, 0)) for d in dils]
    tok = pl.BlockSpec((tr, 2 * HEAD_DIM), lambda i: (i, 0))
    scratch = ([pltpu.VMEM((tr, 2 * HEAD_DIM), F32) for _ in dils] + [pltpu.VMEM((tr, 1), F32) for _ in range(2 * n)]
               + [pltpu.VMEM((tr // d, 2 * HEAD_DIM), F32) for d in dils])

    def to_tokens(o_ref, l_ref, d, pair, cols, stage):
        for r in range(d):
            rows = pl.ds(r, tr // d, stride=d) if d > 1 else slice(None)
            stage[:, :HEAD_DIM] = o_ref[0, r]
            stage[:, HEAD_DIM:] = o_ref[1, r]
            pair[rows, :] = stage[...]
            for h in range(2):
                cols[h][rows, :] = l_ref[h, r]
        return pair[...], [cols[0][...], cols[1][...]]

    def weights(ls):
        left = lax.broadcasted_iota(jnp.int32, (tr, 2 * HEAD_DIM), 1) < HEAD_DIM
        per_head = []
        for h in range(2):
            m = ls[0][h]
            for g in range(1, n):
                m = jnp.maximum(m, ls[g][h])
            es = [jnp.exp(ls[g][h] - m) for g in range(n)]
            den = es[0]
            for e in es[1:]:
                den = den + e
            per_head.append([e / den for e in es])
        return per_head, [jnp.where(left, per_head[0][g], per_head[1][g]) for g in range(n)], left

    def load(refs):
        pairs, cols, stages = refs[:n], refs[n:3 * n], refs[3 * n:]
        return pairs, [cols[2 * g:2 * g + 2] for g in range(n)], stages

    if dout is None:
        def body(*refs):
            pairs, cols, stages = load(refs[2 * n + 1:])
            toks = [to_tokens(refs[g], refs[n + g], dils[g], pairs[g], cols[g], stages[g]) for g in range(n)]
            _, alphas, _ = weights([t[1] for t in toks])
            acc = alphas[0] * toks[0][0]
            for g in range(1, n):
                acc = acc + alphas[g] * toks[g][0]
            refs[2 * n][...] = acc

        return pl.pallas_call(
            body, name=name, out_shape=jax.ShapeDtypeStruct((SEQ, 2 * HEAD_DIM), F32), grid=(SEQ // tr,),
            in_specs=o_specs + l_specs, out_specs=tok, scratch_shapes=scratch, compiler_params=_params(("parallel",)),
        )(*o4, *l4)

    def body(*refs):
        do_refs, dl_refs = refs[2 * n + 1:3 * n + 1], refs[3 * n + 1:4 * n + 1]
        pairs, cols, stages = load(refs[4 * n + 1:])
        toks = [to_tokens(refs[g], refs[n + g], dils[g], pairs[g], cols[g], stages[g]) for g in range(n)]
        per_head, alphas, left = weights([t[1] for t in toks])
        dov = refs[2 * n][...]
        das = []
        for g in range(n):
            prod = dov * toks[g][0]
            das.append([jnp.sum(jnp.where(left, prod, 0.0), axis=1, keepdims=True),
                        jnp.sum(jnp.where(left, 0.0, prod), axis=1, keepdims=True)])
        dbar = [sum(per_head[h][g] * das[g][h] for g in range(n)) for h in range(2)]
        for g, d in enumerate(dils):
            pairs[g][...] = alphas[g] * dov
            for h in range(2):
                cols[g][h][...] = per_head[h][g] * (das[g][h] - dbar[h])
            for r in range(d):
                rows = pl.ds(r, tr // d, stride=d) if d > 1 else slice(None)
                v = pairs[g][rows, :]
                for h in range(2):
                    do_refs[g][h, r] = v[:, h * HEAD_DIM:(h + 1) * HEAD_DIM]
                    dl_refs[g][h, r] = cols[g][h][rows, :]

    out = pl.pallas_call(
        body, name=name,
        out_shape=[jax.ShapeDtypeStruct(o.shape, F32) for o in o4] + [jax.ShapeDtypeStruct(l.shape, F32) for l in l4],
        grid=(SEQ // tr,), in_specs=o_specs + l_specs + [tok], out_specs=o_specs + l_specs, scratch_shapes=scratch,
        compiler_params=_params(("parallel",)),
    )(*o4, *l4, dout)
    return [t.reshape(s.shape) for t, s in zip(out, list(outs) + list(lses))]


def _tri(cmp):
    r = lax.broadcasted_iota(jnp.int32, (SB_TILE, SB_TILE), 0)
    c = lax.broadcasted_iota(jnp.int32, (SB_TILE, SB_TILE), 1)
    return cmp(r, c).astype(BF16)


def _cum(x, tri, terms):
    acc, rest = None, x
    for _ in range(terms):
        part = rest.astype(BF16)
        rest = rest - part.astype(F32)
        d = _dot(part, tri, 1, 0)
        acc = d if acc is None else acc + d
    return acc


def _sb_logits(q, ks, diagonal):
    t = SB_TILE
    z = _dot(q, ks, 1, 1)
    e = jnp.exp(-jnp.abs(z))
    lf = -(jnp.maximum(z, 0.0) + jnp.log(1.0 + e))
    if not diagonal:
        return z, e, lf, None
    mask = lax.broadcasted_iota(jnp.int32, (t, t), 1) < lax.broadcasted_iota(jnp.int32, (t, t), 0)
    return z, e, jnp.where(mask, lf, 0.0), mask


def _sb_specs(h, s):
    t = SB_TILE
    tile = pl.BlockSpec((h, t, HEAD_DIM), lambda i: (0, i, 0))
    keys = pl.BlockSpec((h, s, HEAD_DIM), lambda i: (1, 0, 0))
    values = pl.BlockSpec((h, s, HEAD_DIM), lambda i: (2, 0, 0))
    return tile, keys, values, pl.BlockSpec((h, t, 1), lambda i: (0, i, 0))


def _sb_fwd(x, name):
    h, s = x.shape[0] // 3, x.shape[1]
    t = SB_TILE

    def body(q_ref, k_ref, v_ref, o_ref, tot_ref):
        i = pl.program_id(0)
        after = _tri(lambda r, c: r > c)

        def tile(j, carry, diagonal):
            rows = pl.ds(pl.multiple_of(j * t, t), t)
            out = []
            for hh, (right, acc) in enumerate(carry):
                z, _, lf, mask = _sb_logits(q_ref[hh], k_ref[hh, rows, :], diagonal)
                w = jnp.exp(z + lf + (right + _cum(lf, after, 2)))
                w = w if mask is None else jnp.where(mask, w, 0.0)
                out.append((right + jnp.sum(lf, axis=1, keepdims=True), acc + _dot(w.astype(BF16), v_ref[hh, rows, :], 1, 0)))
            return tuple(out)

        carry = tile(i, tuple((jnp.zeros((t, 1), F32), jnp.zeros((t, HEAD_DIM), F32)) for _ in range(h)), True)
        carry = lax.fori_loop(0, i, lambda jj, c: tile(i - 1 - jj, c, False), carry)
        for hh, (right, acc) in enumerate(carry):
            o_ref[hh] = acc
            tot_ref[hh] = right

    tile_spec, keys, values, col = _sb_specs(h, s)
    return pl.pallas_call(
        body, name=name, out_shape=[jax.ShapeDtypeStruct((h, s, HEAD_DIM), F32), jax.ShapeDtypeStruct((h, s, 1), F32)],
        grid=(s // t,), in_specs=[tile_spec, keys, values], out_specs=[tile_spec, col],
        compiler_params=_params(("parallel",)),
    )(x, x, x)


def _sb_bwd(x, tot, do, name):
    h, s = x.shape[0] // 3, x.shape[1]
    t = SB_TILE

    def body(q_ref, k_ref, v_ref, tot_ref, do_ref, dq_ref, dk_ref, dv_ref):
        i = pl.program_id(0)

        @pl.when(i == 0)
        def _():
            dk_ref[...] = jnp.zeros_like(dk_ref)
            dv_ref[...] = jnp.zeros_like(dv_ref)

        upto = _tri(lambda r, c: r <= c)
        before = _tri(lambda r, c: r < c)

        def tile(j, carry, diagonal):
            rows = pl.ds(pl.multiple_of(j * t, t), t)
            out = []
            for hh, (left, cleft, dq) in enumerate(carry):
                qv, ks, dob = q_ref[hh], k_ref[hh, rows, :], do_ref[hh].astype(BF16)
                z, e, lf, mask = _sb_logits(qv, ks, diagonal)
                between = tot_ref[hh] - (left + _cum(lf, upto, 2))
                w = jnp.exp(z + lf + between)
                w = w if mask is None else jnp.where(mask, w, 0.0)
                dlog = w * _dot(dob, v_ref[hh, rows, :], 1, 1)
                cfail = cleft + _cum(dlog, before, 2)
                sig = jnp.where(z >= 0.0, 1.0, e) / (1.0 + e)
                dz = dlog * (1.0 - sig) - sig * cfail
                dz = (dz if mask is None else jnp.where(mask, dz, 0.0)).astype(BF16)
                dk_ref[hh, rows, :] += _dot(dz, qv, 0, 0)
                dv_ref[hh, rows, :] += _dot(w.astype(BF16), dob, 0, 0)
                out.append((left + jnp.sum(lf, axis=1, keepdims=True), cleft + jnp.sum(dlog, axis=1, keepdims=True),
                            dq + _dot(dz, ks, 1, 0)))
            return tuple(out)

        zero = jnp.zeros((t, 1), F32)
        carry = lax.fori_loop(0, i, lambda j, c: tile(j, c, False),
                              tuple((zero, zero, jnp.zeros((t, HEAD_DIM), F32)) for _ in range(h)))
        for hh, (_, _, dq) in enumerate(tile(i, carry, True)):
            dq_ref[hh] = dq * (HEAD_DIM ** -0.5)

    tile_spec, keys, values, col = _sb_specs(h, s)
    full = pl.BlockSpec((h, s, HEAD_DIM), lambda i: (0, 0, 0))
    shp = jax.ShapeDtypeStruct((h, s, HEAD_DIM), F32)
    return pl.pallas_call(
        body, name=name, out_shape=[shp, shp, shp], grid=(s // t,),
        in_specs=[tile_spec, keys, values, col, tile_spec],
        out_specs=[tile_spec, full, full], compiler_params=_params(("arbitrary",)),
    )(x, x, x, tot, do)


COL_SB, COL_DIL, COL_SWA = 0, 3 * H_SB * HEAD_DIM, 3 * H_SB * HEAD_DIM + 3 * H_DIL * HEAD_DIM
N_SWA = H_SWA_Q + 2 * H_SWA_KV


def _dil_col(t, g):
    return COL_DIL + t * H_DIL * HEAD_DIM + g * 2 * HEAD_DIM


def _split_heads(qkv, name):
    tr = TOK_TILE
    scale = HEAD_DIM ** -0.5
    dils = [d for _, d in DIL_PATTERNS]

    def body(x_ref, sb_ref, d0_ref, d1_ref, d2_ref, swa_ref, pair):
        def head(col, scaled):
            v = x_ref[:, col:col + HEAD_DIM]
            return (v * scale if scaled else v).astype(BF16)

        for hh in range(3 * H_SB):
            sb_ref[hh] = head(COL_SB + hh * HEAD_DIM, hh < H_SB)
        for hh in range(N_SWA):
            swa_ref[hh] = head(COL_SWA + hh * HEAD_DIM, hh < H_SWA_Q)
        for t in range(3):
            for g, (d, out_ref) in enumerate(zip(dils, (d0_ref, d1_ref, d2_ref))):
                col = _dil_col(t, g)
                if d == 1:
                    for h in range(2):
                        out_ref[t * 2 + h] = head(col + h * HEAD_DIM, t == 0)
                    continue
                pair[...] = x_ref[:, col:col + 2 * HEAD_DIM]
                for r in range(d):
                    v = pair[pl.ds(r, tr // d, stride=d), :]
                    v = v * scale if t == 0 else v
                    for h in range(2):
                        out_ref[t * 2 * d + h * d + r] = v[:, h * HEAD_DIM:(h + 1) * HEAD_DIM].astype(BF16)

    def heads(n, length):
        return jax.ShapeDtypeStruct((n, length, HEAD_DIM), BF16)

    def spec(n, rows):
        return pl.BlockSpec((n, rows, HEAD_DIM), lambda i: (0, i, 0))

    return pl.pallas_call(
        body, name=name,
        out_shape=[heads(3 * H_SB, SEQ)] + [heads(6 * d, SEQ // d) for d in dils] + [heads(N_SWA, SEQ)],
        grid=(SEQ // tr,), in_specs=[pl.BlockSpec((tr, D_QKV), lambda i: (i, 0))],
        out_specs=[spec(3 * H_SB, tr)] + [spec(6 * d, tr // d) for d in dils] + [spec(N_SWA, tr)],
        scratch_shapes=[pltpu.VMEM((tr, 2 * HEAD_DIM), F32)], compiler_params=_params(("parallel",)),
    )(qkv)


def _join_heads(sb, dil, swa, name):
    tr = TOK_TILE
    dils = [d for _, d in DIL_PATTERNS]

    def body(*refs):
        sb_refs, dil_refs, swa_refs = refs[:3], [refs[3 + 3 * g:6 + 3 * g] for g in range(3)], refs[12:15]
        o_ref, pair, stages = refs[15], refs[16], refs[17:]

        def put(col, v):
            o_ref[:, col:col + v.shape[1]] = v.astype(BF16)

        for t in range(3):
            for h in range(H_SB):
                put(COL_SB + (t * H_SB + h) * HEAD_DIM, sb_refs[t][h])
        col = COL_SWA
        for ref in swa_refs:
            for h in range(ref.shape[0]):
                put(col, ref[h])
                col += HEAD_DIM
        for t in range(3):
            for g, d in enumerate(dils):
                ref, col = dil_refs[g][t], _dil_col(t, g)
                if d == 1:
                    for h in range(2):
                        put(col + h * HEAD_DIM, ref[h])
                    continue
                stage = stages[g - 1]
                for r in range(d):
                    stage[:, :HEAD_DIM] = ref[r]
                    stage[:, HEAD_DIM:] = ref[d + r]
                    pair[pl.ds(r, tr // d, stride=d), :] = stage[...]
                put(col, pair[...])

    def spec(n, rows):
        return pl.BlockSpec((n, rows, HEAD_DIM), lambda i: (0, i, 0))

    ins = list(sb) + [t for g in range(3) for t in dil[g]] + list(swa)
    in_specs = ([spec(H_SB, tr)] * 3 + [spec(2 * d, tr // d) for d in dils for _ in range(3)]
                + [spec(H_SWA_Q, tr), spec(H_SWA_KV, tr), spec(H_SWA_KV, tr)])
    return pl.pallas_call(
        body, name=name, out_shape=jax.ShapeDtypeStruct((SEQ, D_QKV), BF16), grid=(SEQ // tr,), in_specs=in_specs,
        out_specs=pl.BlockSpec((tr, D_QKV), lambda i: (i, 0)),
        scratch_shapes=[pltpu.VMEM((tr, 2 * HEAD_DIM), F32)] + [pltpu.VMEM((tr // d, 2 * HEAD_DIM), F32) for d in dils[1:]],
        compiler_params=_params(("parallel",)),
    )(*ins)


def _mixer_fwd(qkv, bias, sinks_l, tag):
    sb, d0, d1, d2, swa = _split_heads(qkv, name=f"split_heads_{tag}")
    st = {"sb": sb, "dil": (d0, d1, d2), "swa": swa}
    o_sb, st["sb_tot"] = _sb_fwd(sb, name=f"sb_fwd_{tag}")
    st["dil_out"], st["dil_lse"], st["dil_sink"] = [], [], []
    for gi, (_, d) in enumerate(DIL_PATTERNS):
        sink = jnp.zeros((2 * d, 1, LANES), F32)
        og, lg = _band_fwd(st["dil"][gi], bias[2 * gi:2 * gi + 2], sink, nq=2 * d, offs=(0, 2 * d, 4 * d), g=1, bias_div=d,
                           has_sink=False, name=f"dil{gi}_fwd_{tag}")
        st["dil_out"].append(og)
        st["dil_lse"].append(lg)
        st["dil_sink"].append(sink)
    o_dil = _dil_merge(st["dil_out"], st["dil_lse"], None, name=f"dil_merge_fwd_{tag}")
    st["swa_sink"] = jnp.broadcast_to(sinks_l.reshape(H_SWA_Q, 1, 1), (H_SWA_Q, 1, LANES))
    st["swa_out"] = _band_fwd(swa, bias[H_DIL:], st["swa_sink"], nq=H_SWA_Q, offs=(0, H_SWA_Q, H_SWA_Q + H_SWA_KV),
                              g=H_SWA_Q // H_SWA_KV, bias_div=1, has_sink=True, name=f"swa_fwd_{tag}")
    return (o_sb, o_dil, st["swa_out"][0]), st


def _mixer_bwd(st, bias, do_sb, do_dil, do_swa, tag):
    d_sb = _sb_bwd(st["sb"], st["sb_tot"], do_sb, name=f"sb_bwd_{tag}")
    dmerge = _dil_merge(st["dil_out"], st["dil_lse"], do_dil, name=f"dil_merge_bwd_{tag}")
    d_dil, dbs = [], []
    for gi, (_, d) in enumerate(DIL_PATTERNS):
        dq, dk, dv, db, _ = _band_bwd(st["dil"][gi], bias[2 * gi:2 * gi + 2], st["dil_sink"][gi], st["dil_out"][gi],
                                      st["dil_lse"][gi], dmerge[gi], dmerge[3 + gi], nq=2 * d, offs=(0, 2 * d, 4 * d),
                                      g=1, bias_div=d, has_sink=False, name=f"dil{gi}_bwd_{tag}")
        d_dil.append((dq, dk, dv))
        dbs.append(db)
    o_sw, l_sw = st["swa_out"]
    dq_sw, dk_sw, dv_sw, db_sw, dsink = _band_bwd(st["swa"], bias[H_DIL:], st["swa_sink"], o_sw, l_sw, do_swa,
                                                  jnp.zeros_like(l_sw), nq=H_SWA_Q, offs=(0, H_SWA_Q, H_SWA_Q + H_SWA_KV),
                                                  g=H_SWA_Q // H_SWA_KV, bias_div=1, has_sink=True, name=f"swa_bwd_{tag}")
    dqkv = _join_heads(d_sb, d_dil, (dq_sw, dk_sw, dv_sw), name=f"join_heads_{tag}")
    return dqkv, jnp.concatenate(dbs + [db_sw], 0), dsink[:, 0, 0]


PIECES = ("ffn0", "mix", "ffn1")


def _ffn_fwd(x_in, w, gain, mod_j, tag, after=None):
    st = {"x": x_in, "w": w}
    st["h"] = _norm_fwd(x_in, _row(gain), _row(mod_j[1]), _row(mod_j[0]), name=f"norm_fwd_{tag}", after=after)
    st["a"], st["u"], st["s"] = _ffn_up(st["h"], w["gate"], w["up"], name=f"up_{tag}")
    st["f"], x_out = _mm(st["s"], w["down"], res=x_in, colscale=_row(0.5 * mod_j[2]), emit_acc=True, tm=256, tn=1024,
                         name=f"down_{tag}")
    return x_out, st


def _ffn_bwd(dx_out, st, gain, mod_j, tag, done):
    w = st["w"]

    def latest(new, old):
        return old if new is None else new

    df, dgate = _gate_bwd(dx_out, st["f"], _row(0.5 * mod_j[2]), 0.5, name=f"gate_bwd_{tag}")
    token = done({"down": _mm_tn(st["s"], df, name=f"dwd_{tag}")})
    da, du = _ffn_bwd_ds(df, w["down"], st["a"], st["u"], name=f"ds_{tag}")
    token = latest(done({"gate": _mm_tn(da, st["h"], after=token, name=f"dwg_{tag}")}), token)
    token = latest(done({"up": _mm_tn(du, st["h"], after=token, name=f"dwu_{tag}")}), token)
    dh = _mm2(da, w["gate"], du, w["up"], after=token, name=f"dh_{tag}")
    dx_in, sum_dh, sum_dhx = _norm_bwd(st["x"], dh, dx_out, _row(gain), _row(mod_j[1]), name=f"norm_bwd_{tag}")
    dmod = jnp.concatenate([sum_dh, gain * sum_dhx, dgate], 0)
    return dx_in, dmod, (1.0 + mod_j[1]) * sum_dhx[0]


def _mix_fwd(x_in, w, gain, mod_j, bias, sinks_l, tag, after=None):
    st = {"x": x_in, "w": w}
    st["h"] = _norm_fwd(x_in, _row(gain), _row(mod_j[1]), _row(mod_j[0]), name=f"norm_fwd_mix_{tag}", after=after)
    qkv = _mm(st["h"], w["qkv"], tb=True, name=f"qkv_{tag}")
    st["gates"] = _mm(st["h"], w["gates"], tb=True, name=f"gates_{tag}")
    outs, st["mix"] = _mixer_fwd(qkv, bias, sinks_l, tag)
    st["merged"], *st["t"] = _merge_fwd(*outs, st["gates"], w["br_sb"], w["br_dil"], w["br_swa"], name=f"merge_fwd_{tag}")
    st["f"], x_out = _mm(st["merged"], w["out"], res=x_in, colscale=_row(mod_j[2]), emit_acc=True, name=f"out_{tag}")
    return x_out, st


def _mix_bwd(dx_out, st, gain, mod_j, bias, tag, done):
    w = st["w"]
    df, dgate = _gate_bwd(dx_out, st["f"], _row(mod_j[2]), 1.0, name=f"gate_bwd_mix_{tag}")
    g = {"out": _mm_tn(st["merged"], df, name=f"dw_out_{tag}")}
    dmerged = _mm(df, w["out"], tb=True, name=f"dmerged_{tag}")
    dgates, do_sb, do_dil, do_swa, dbr_sb, dbr_dil, dbr_swa = _merge_bwd(
        dmerged, *st["t"], st["gates"], w["br_sb"], w["br_dil"], w["br_swa"], name=f"merge_bwd_{tag}")
    g["br_sb"] = _mm_tn(st["t"][0], dbr_sb, name=f"dw_br_sb_{tag}")
    g["br_dil"] = _mm_tn(st["t"][1], dbr_dil, name=f"dw_br_dil_{tag}")
    g["br_swa"] = _mm_tn(st["t"][2], dbr_swa, name=f"dw_br_swa_{tag}")
    dqkv, dbias, dsinks = _mixer_bwd(st["mix"], bias, do_sb, do_dil, do_swa, tag)
    g["qkv"] = _mm_tn(dqkv, st["h"], name=f"dw_qkv_{tag}")
    g["gates"] = _mm_tn(dgates, st["h"], name=f"dw_gates_{tag}")
    dh = _mm2(dqkv, w["qkv"], dgates, w["gates"], after=done(g), tm=128, name=f"dh_mix_{tag}")
    dx_in, sum_dh, sum_dhx = _norm_bwd(st["x"], dh, dx_out, _row(gain), _row(mod_j[1]), name=f"norm_bwd_mix_{tag}")
    dmod = jnp.concatenate([sum_dh, gain * sum_dhx, dgate], 0)
    return dx_in, dmod, (1.0 + mod_j[1]) * sum_dhx[0], dbias, dsinks


def _local_step(x, target, mod, gains, weights_of, rel_bias, sinks, final_gain, grads_done):
    tables = jnp.asarray(_bucket_tables())
    bias = _bias_build(rel_bias, tables, name="bias_build")
    states, h = [], x
    for l in range(DEPTH):
        st = {}
        for j, piece in enumerate(PIECES):
            w, after = weights_of(l, piece, h)
            if piece == "mix":
                h, st[piece] = _mix_fwd(h, w, gains[l, j], mod[l, j], bias, sinks[l], f"l{l}", after)
            else:
                h, st[piece] = _ffn_fwd(h, w, gains[l, j], mod[l, j], f"{piece}_l{l}", after)
        states.append(st)
    loss, dx, dfinal = _final_loss(h, target, _row(final_gain), name="final_loss")
    dmods = [[None] * 3 for _ in range(DEPTH)]
    dgains = [[None] * 3 for _ in range(DEPTH)]
    dsinks = [None] * DEPTH
    dbias = None
    for l in reversed(range(DEPTH)):
        for j in reversed(range(3)):
            piece = PIECES[j]
            done = lambda grads, l=l, piece=piece: grads_done(l, piece, grads)
            if piece == "mix":
                dx, dmods[l][j], dgains[l][j], db, dsinks[l] = _mix_bwd(dx, states[l][piece], gains[l, j], mod[l, j], bias, f"l{l}", done)
                dbias = db if dbias is None else dbias + db
            else:
                dx, dmods[l][j], dgains[l][j] = _ffn_bwd(dx, states[l][piece], gains[l, j], mod[l, j], f"{piece}_l{l}", done)
    drel = _bias_grad(dbias, tables, name="bias_grad")[:, 0, :N_BUCKETS].T
    dmod = jnp.stack([jnp.stack(m) for m in dmods])
    dgain = jnp.stack([jnp.stack(g) for g in dgains])
    return loss, dx, dmod, dgain, dfinal[0], drel, jnp.stack(dsinks)


BR_ROWS = (H_SB * HEAD_DIM, 2 * HEAD_DIM, H_SWA_Q * HEAD_DIM)


def _lanes_unshard(g, lead):
    _, rows, _ = g.shape
    r = rows // lead
    return g.reshape(N_DEV, lead, r, LANES).transpose(1, 2, 0, 3).reshape(lead, r, N_DEV * LANES)


def _lanes_shard(full):
    lead, r, _ = full.shape
    return full.reshape(lead, r, N_DEV, LANES).transpose(2, 0, 1, 3).reshape(N_DEV, lead * r, LANES)


def _pack_rows(parts, dtype):
    flat = jnp.concatenate([p.astype(dtype).reshape(-1) for p in parts])
    pad = (-flat.shape[0]) % (16 * LANES)
    if pad:
        flat = jnp.concatenate([flat, jnp.zeros((pad,), dtype)])
    return flat.reshape(-1, LANES)


def _unshard(gathered, axis):
    moved = jnp.moveaxis(gathered, 0, axis)
    shape = list(moved.shape)
    shape[axis:axis + 2] = [shape[axis] * shape[axis + 1]]
    return moved.reshape(shape)


def kernel(x, c, w_ada, b_ada, norm_gain, w_ffn_gate, w_ffn_up, w_ffn_down, w_in, w_br_sb, w_br_dil, w_br_swa, w_out, sinks, rel_bias, final_gain, loss_target, m_w_ada, m_b_ada, m_norm_gain, m_w_ffn_gate, m_w_ffn_up, m_w_ffn_down, m_w_in, m_w_br_sb, m_w_br_dil, m_w_br_swa, m_w_out, m_sinks, m_rel_bias, m_final_gain, v_w_ada, v_b_ada, v_norm_gain, v_w_ffn_gate, v_w_ffn_up, v_w_ffn_down, v_w_in, v_w_br_sb, v_w_br_dil, v_w_br_swa, v_w_out, v_sinks, v_rel_bias, v_final_gain):
    me = 4 * lax.axis_index("x") + 2 * lax.axis_index("y") + lax.axis_index("c")
    d = D_MODEL
    gate_t, up_t, in_t = jnp.swapaxes(w_ffn_gate, 2, 3), jnp.swapaxes(w_ffn_up, 2, 3), jnp.swapaxes(w_in, 1, 2)

    def piece_shards(l, piece):
        bf = lambda t: t.astype(BF16)
        if piece == "mix":
            return [bf(in_t[l]), jnp.concatenate([bf(w_br_sb[l]), bf(w_br_dil[l]), bf(w_br_swa[l])], 0), bf(w_out[l])]
        i = PIECES.index(piece) // 2
        return [bf(gate_t[l, i]), bf(up_t[l, i]), bf(w_ffn_down[l, i])]

    br_off = np.concatenate([[0], np.cumsum(BR_ROWS)])

    def piece_weights(gathered, piece):
        if piece == "mix":
            g_in, g_br, g_out = gathered
            f_in = g_in.reshape(D_QKV + D_GATES, d)
            f_br = [_lanes_unshard(g_br[:, br_off[k]:br_off[k + 1]], 1)[0] for k in range(3)]
            return {"qkv": f_in[:D_QKV], "gates": f_in[D_QKV:], "br_sb": f_br[0], "br_dil": f_br[1], "br_swa": f_br[2],
                    "out": g_out.reshape(d, d)}
        return {n: g.reshape(D_FF, d) for n, g in zip(("gate", "up", "down"), gathered)}

    small, = _all_gather([_pack_rows([c, norm_gain], F32)], name="gather_cond")
    c_all = small[:, :d // LANES].reshape(N_DEV, d)
    gains = _unshard(small[:, d // LANES:d // LANES + 6].reshape(N_DEV, DEPTH, 3, LANES), 2)

    cols = w_ada.shape[2]
    mod_cols = jnp.stack([_ada_fwd(c_all, w_ada[l], name=f"ada_fwd_l{l}") for l in range(DEPTH)])
    mod_all, = _all_gather([_pack_rows([mod_cols], F32)], name="gather_mod")
    mod_all = mod_all.reshape(N_DEV, -1)[:, :DEPTH * N_DEV * cols].reshape(N_DEV, DEPTH, N_DEV, cols)
    mod_mine = lax.dynamic_index_in_dim(mod_all, me, axis=2, keepdims=False)
    mod = (mod_mine.transpose(1, 0, 2).reshape(DEPTH, N_DEV * cols) + b_ada).reshape(DEPTH, 3, 3, d)

    order = [(l, piece) for l in range(DEPTH) for piece in PIECES]
    eager, ahead = 2, 3
    in_flight = {}
    n_tensors = 3
    first = _all_gather([s for k in range(eager) for s in piece_shards(*order[k])], after=mod_all, name="gather_first")

    def start_gather(k, after):
        l, piece = order[k]
        in_flight[k], token = _exchange_start(piece_shards(l, piece), after, gather=True, name=f"gather_{piece}_l{l}_start")
        return token

    token = first[0]
    for k in range(eager, eager + ahead - 1):
        token = start_gather(k, token)
    mod = mod + token[0, 0]

    def weights_of(l, piece, h):
        k = order.index((l, piece))
        started = eager <= k + ahead < len(order) and k + ahead not in in_flight
        token = start_gather(k + ahead, h) if started else None
        if k < eager:
            return piece_weights(first[n_tensors * k:n_tensors * (k + 1)], piece), token
        landed = _exchange_wait(in_flight[k], h if token is None else token, gather=True, name=f"gather_{piece}_l{l}_wait")
        return piece_weights(landed, piece), token

    exchanges, have = {}, {}

    def grads_done(l, piece, g):
        key = (l, piece)
        have.setdefault(key, {}).update(g)
        if piece == "mix":
            if len(have[key]) < 6:
                return None
            g = have[key]
            s_br = jnp.concatenate([_lanes_shard(g[n][None]) for n in ("br_sb", "br_dil", "br_swa")], 1)
            groups = [(("in", "br", "out"), [jnp.concatenate([g["qkv"], g["gates"]], 0).reshape(N_DEV, -1, d), s_br,
                                             g["out"].reshape(N_DEV, -1, d)])]
        elif key == order[0]:
            groups = [((n,), [t.reshape(N_DEV, -1, d)]) for n, t in g.items()]
        elif len(have[key]) < 3:
            return None
        else:
            groups = [(("gate", "up", "down"), [have[key][n].reshape(N_DEV, -1, d) for n in ("gate", "up", "down")])]
        token = None
        for names, sg in groups:
            state, token = _exchange_start(sg, sg[0], gather=False, name=f"exchange_{piece}_l{l}_{names[0]}_start")
            exchanges.setdefault(key, []).append((names, state))
        return token

    loss, dx, dmod, dgains, dfinal, drel, dsinks = _local_step(
        x[0], loss_target[0], mod, gains, weights_of, rel_bias, sinks, final_gain, grads_done)

    small_parts = [dmod, dgains, dfinal, drel.T, dsinks, loss[0, :1]]
    small_sizes = [int(np.prod(p.shape)) for p in small_parts]
    small_all, = _all_gather([_pack_rows(small_parts, F32)], name="gather_small")
    small_sum = _sum_parts([small_all], name="sum_small").reshape(-1)
    offs = np.concatenate([[0], np.cumsum(small_sizes)])
    g_b_ada = small_sum[offs[0]:offs[1]].reshape(DEPTH, 9 * d)
    g_gain_full = small_sum[offs[1]:offs[2]].reshape(DEPTH, 3, d)
    g_norm_gain = lax.dynamic_slice_in_dim(g_gain_full, me * LANES, LANES, axis=2)
    g_final = small_sum[offs[2]:offs[3]]
    g_rel = small_sum[offs[3]:offs[4]].reshape(N_SOFT, N_BUCKETS).T
    g_sinks = small_sum[offs[4]:offs[5]].reshape(DEPTH, H_SWA_Q)
    loss_total = small_sum[offs[5]]

    dmod_all = small_all.reshape(N_DEV, -1)[:, :DEPTH * 9 * d].reshape(N_DEV, DEPTH, 9 * d)
    dmod_cols = lax.dynamic_slice_in_dim(dmod_all, me * cols, cols, axis=2)
    g_w_ada = jnp.stack([_ada_bwd(c_all.T, dmod_cols[:, l], name=f"ada_bwd_l{l}") for l in range(DEPTH)])

    state = {"w_ada": (w_ada, m_w_ada, v_w_ada), "b_ada": (b_ada, m_b_ada, v_b_ada),
             "norm_gain": (norm_gain, m_norm_gain, v_norm_gain), "w_ffn_gate": (w_ffn_gate, m_w_ffn_gate, v_w_ffn_gate),
             "w_ffn_up": (w_ffn_up, m_w_ffn_up, v_w_ffn_up), "w_ffn_down": (w_ffn_down, m_w_ffn_down, v_w_ffn_down),
             "w_in": (w_in, m_w_in, v_w_in), "w_br_sb": (w_br_sb, m_w_br_sb, v_w_br_sb),
             "w_br_dil": (w_br_dil, m_w_br_dil, v_w_br_dil), "w_br_swa": (w_br_swa, m_w_br_swa, v_w_br_swa),
             "w_out": (w_out, m_w_out, v_w_out), "sinks": (sinks, m_sinks, v_sinks),
             "rel_bias": (rel_bias, m_rel_bias, v_rel_bias), "final_gain": (final_gain, m_final_gain, v_final_gain)}
    grad, update = {}, {}

    def adamw(n, g, transposed=False):
        w, m, v = (jnp.swapaxes(t, -1, -2) for t in state[n]) if transposed else state[n]
        if w.ndim == 1:
            out = tuple(t.reshape(w.shape) for t in _adamw(_row(w), _row(g), _row(m), _row(v), name=f"adamw_{n}"))
        else:
            out = _adamw(w, g, m, v, name=f"adamw_{n}")
        if transposed:
            grad[n], update[n] = jnp.swapaxes(g, -1, -2), tuple(jnp.swapaxes(t, -1, -2) for t in out)
        else:
            grad[n], update[n] = g, out

    for n, g in (("w_ada", g_w_ada), ("b_ada", g_b_ada), ("norm_gain", g_norm_gain), ("sinks", g_sinks),
                 ("rel_bias", g_rel), ("final_gain", g_final)):
        adamw(n, g)

    after = update["w_ada"][0]
    parts = {}
    for key in reversed(order):
        for names, ex_state in exchanges[key]:
            landed = _exchange_wait(ex_state, after, gather=False, name=f"exchange_{key[1]}_l{key[0]}_{names[0]}_wait")
            parts.setdefault(key, {}).update(zip(names, landed))
            after = landed[0]
    ffn_keys = [key for key in order if key[1] != "mix"]
    mix_keys = [key for key in order if key[1] == "mix"]
    sums = {n: _sum_parts([parts[key][n] for key in ffn_keys], name=f"sum_grads_{n}") for n in ("gate", "up", "down")}
    sums.update({n: _sum_parts([parts[key][n] for key in mix_keys], name=f"sum_grads_{n}") for n in ("in", "br", "out")})
    br_sums = sums["br"].reshape(DEPTH, -1, LANES)
    adamw("w_ffn_gate", sums["gate"].reshape(gate_t.shape), transposed=True)
    adamw("w_ffn_up", sums["up"].reshape(up_t.shape), transposed=True)
    adamw("w_ffn_down", sums["down"].reshape(w_ffn_down.shape))
    adamw("w_in", sums["in"].reshape(in_t.shape), transposed=True)
    adamw("w_br_sb", br_sums[:, br_off[0]:br_off[1]])
    adamw("w_br_dil", br_sums[:, br_off[1]:br_off[2]])
    adamw("w_br_swa", br_sums[:, br_off[2]:br_off[3]])
    adamw("w_out", sums["out"].reshape(w_out.shape))

    names = ["w_ada", "b_ada", "norm_gain", "w_ffn_gate", "w_ffn_up", "w_ffn_down", "w_in", "w_br_sb", "w_br_dil",
             "w_br_swa", "w_out", "sinks", "rel_bias", "final_gain"]
    return (loss_total, dx[None], *[grad[n] for n in names], *[update[n][0] for n in names],
            *[update[n][1] for n in names], *[update[n][2] for n in names])
```

```python
import math

import numpy as np
import jax
import jax.numpy as jnp
from jax import lax
from jax.experimental import pallas as pl
from jax.experimental.pallas import tpu as pltpu

F32, BF16 = jnp.float32, jnp.bfloat16

SEQ, D_MODEL, D_FF, HEAD_DIM = 2048, 1024, 2816, 64
DEPTH = 2
BLK = 128
H_SB, H_DIL, H_SWA_Q, H_SWA_KV = 4, 6, 6, 2
DIL_PATTERNS = ((128, 1), (512, 4), (2048, 16))
SWA_WINDOW = 128
N_BUCKETS, MAX_REL_DIST = 32, 2048
RMS_EPS = 1e-6
D_QKV = 2560
D_GATES = 3 * D_MODEL
ADAM_LR, ADAM_B1, ADAM_B2, ADAM_EPS, ADAM_WD, ADAM_STEP = 0.001, 0.9, 0.999, 1e-08, 0.01, 10

N_DEV = 8
LANES = 128
NEG = -1e30
SB_TILE = 256
VMEM_LIMIT_BYTES = 48 * 1024 * 1024
HBM = pl.BlockSpec(memory_space=pltpu.HBM)
MESH = pl.DeviceIdType.MESH


def _tile(n, target):
    t = (min(n, target) // LANES) * LANES
    while t >= LANES:
        if n % t == 0:
            return t
        t -= LANES
    return n


def _row_tile(r, cap):
    t = (min(r, cap) // 16) * 16
    while t > 16 and r % t:
        t -= 16
    return t


def _params(semantics=None):
    return pltpu.CompilerParams(dimension_semantics=semantics, vmem_limit_bytes=VMEM_LIMIT_BYTES)


def _dot(a, b, ca, cb):
    return lax.dot_general(a, b, (((ca,), (cb,)), ((), ())), preferred_element_type=F32)


def _sigmoid(a):
    return 1.0 / (1.0 + jnp.exp(-a))


def _row(v):
    return v.reshape(1, -1)


def _all_gather(arrs, name, after=None):
    n = len(arrs)
    ins = list(arrs) + ([] if after is None else [after])

    def body(*refs):
        x_refs, out_refs = refs[:n], refs[len(ins):len(ins) + n]
        send_sems, recv_sems, local_sems = refs[len(ins) + n:]
        x, y, c = lax.axis_index("x"), lax.axis_index("y"), lax.axis_index("c")
        me, sibling = (x, y, c), (x, y, 1 - c)
        chips = [(1 - x, y), (x, 1 - y), (1 - x, 1 - y)]

        def slot(t, px, py, pc):
            return out_refs[t].at[4 * px + 2 * py + pc]

        def copy(t, k, block, to, src=None):
            return pltpu.make_async_remote_copy(
                src_ref=slot(t, *block) if src is None else src, dst_ref=slot(t, *block),
                send_sem=send_sems.at[7 * t + k], recv_sem=recv_sems.at[7 * t + k], device_id=to, device_id_type=MESH)

        mine = [pltpu.make_async_copy(x_refs[t], slot(t, *me), local_sems.at[t]) for t in range(n)]
        for cp in mine:
            cp.start()
        first = []
        for t in range(n):
            first.append(copy(t, 0, me, sibling, src=x_refs[t]))
            first += [copy(t, 1 + j, me, (*chip, c), src=x_refs[t]) for j, chip in enumerate(chips)]
        for cp in first:
            cp.start()
        passed = []
        for j, chip in enumerate(chips):
            for t in range(n):
                copy(t, 1 + j, (*chip, c), me).wait_recv()
                passed.append(copy(t, 4 + j, (*chip, c), sibling))
                passed[-1].start()
        for t in range(n):
            copy(t, 0, sibling, me).wait_recv()
        for j, chip in enumerate(chips):
            for t in range(n):
                copy(t, 4 + j, (*chip, 1 - c), me).wait_recv()
        for cp in first + passed:
            cp.wait_send()
        for cp in mine:
            cp.wait()

    return pl.pallas_call(
        body, name=name, out_shape=[jax.ShapeDtypeStruct((N_DEV,) + a.shape, a.dtype) for a in arrs],
        in_specs=[HBM] * n + [pl.BlockSpec(memory_space=pl.ANY)] * (len(ins) - n), out_specs=[HBM] * n,
        scratch_shapes=[pltpu.SemaphoreType.DMA((7 * n,)), pltpu.SemaphoreType.DMA((7 * n,)), pltpu.SemaphoreType.DMA((n,))],
    )(*ins)


def _direct_copies(x_refs, land_refs, send_sems, recv_sems, local_sems, gather):
    x, y, c = lax.axis_index("x"), lax.axis_index("y"), lax.axis_index("c")
    me = 4 * x + 2 * y + c
    sends, recvs = [], []
    for k in range(1, N_DEV):
        px = 1 - x if (k >> 2) & 1 else x
        py = 1 - y if (k >> 1) & 1 else y
        pc = 1 - c if k & 1 else c
        peer = 4 * px + 2 * py + pc
        for t, (x_ref, land_ref) in enumerate(zip(x_refs, land_refs)):
            sem = 7 * t + k - 1
            for out, src, slot in ((sends, x_ref if gather else x_ref.at[peer], me),
                                   (recvs, x_ref if gather else x_ref.at[me], peer)):
                out.append(pltpu.make_async_remote_copy(
                    src_ref=src, dst_ref=land_ref.at[slot], send_sem=send_sems.at[sem], recv_sem=recv_sems.at[sem],
                    device_id=(px, py, pc), device_id_type=MESH))
    own = [pltpu.make_async_copy(x_ref if gather else x_ref.at[me], land_ref.at[me], local_sems.at[t])
           for t, (x_ref, land_ref) in enumerate(zip(x_refs, land_refs))]
    return sends, recvs, own


SEM =pl.BlockSpec(memory_space=pltpu.SEMAPHORE)
ANY = pl.BlockSpec(memory_space=pl.ANY)
SIDE_EFFECT = pltpu.SideEffectType.DATAFLOW_SIDE_EFFECTING


def _exchange_start(arrs, after, *, gather, name):
    n = len(arrs)
    lands = [lax.empty(((N_DEV,) + a.shape) if gather else a.shape, a.dtype) for a in arrs]

    def body(*refs):
        sends, _, own = _direct_copies(refs[:n], refs[n:2 * n], *refs[2 * n + 1:2 * n + 4], gather)
        for cp in own + sends:
            cp.start()
        refs[-1][...] = jnp.zeros_like(refs[-1])

    ops = [pltpu.with_memory_space_constraint(a, pltpu.HBM) for a in list(arrs) + lands]
    out = pl.pallas_call(
        body, name=name,
        out_shape=(pltpu.SemaphoreType.DMA((7 * n,)), pltpu.SemaphoreType.DMA((7 * n,)), pltpu.SemaphoreType.DMA((n,)),
                   *[pltpu.HBM(a.shape, a.dtype) for a in ops], jax.ShapeDtypeStruct((8, LANES), F32)),
        in_specs=[HBM] * (2 * n) + [ANY],
        out_specs=(SEM, SEM, SEM, *[HBM] * (2 * n), pl.BlockSpec(memory_space=pltpu.VMEM)),
        input_output_aliases={t: 3 + t for t in range(2 * n)},
        compiler_params=pltpu.CompilerParams(has_side_effects=SIDE_EFFECT),
    )(*ops, after)
    return (out[:3], out[3:3 + n], out[3 + n:3 + 2 * n]), out[-1]


def _exchange_wait(state, after, *, gather, name):
    sems, arrs, lands = state
    n = len(arrs)

    def body(*refs):
        sends, recvs, own = _direct_copies(refs[:n], refs[n:2 * n], *refs[2 * n:2 * n + 3], gather)
        for cp in own:
            cp.wait()
        for cp in sends:
            cp.wait_send()
        for cp in recvs:
            cp.wait_recv()

    out = pl.pallas_call(
        body, name=name, out_shape=tuple(pltpu.HBM(a.shape, a.dtype) for a in list(arrs) + list(lands)),
        in_specs=[HBM] * (2 * n) + [SEM, SEM, SEM, ANY], out_specs=tuple([HBM] * (2 * n)),
        input_output_aliases={t: t for t in range(2 * n)},
        compiler_params=pltpu.CompilerParams(has_side_effects=SIDE_EFFECT),
    )(*arrs, *lands, *sems, after)
    return out[n:]


def _sum_parts(groups, name):
    n, r, cdim = groups[0].shape
    tr = _row_tile(r, max(16, (1 << 21) // (n * cdim * groups[0].dtype.itemsize)))
    steps = r // tr

    def body(*refs):
        o_ref = refs[-1]
        gg = pl.program_id(0)
        for gi in range(len(groups)):
            @pl.when(gg == gi)
            def _(gi=gi):
                acc = refs[gi][0].astype(F32)
                for k in range(1, n):
                    acc = acc + refs[gi][k].astype(F32)
                o_ref[...] = acc

    def in_spec(gi):
        return pl.BlockSpec((n, tr, cdim), lambda gg, i: (0, jnp.where(gg == gi, i, 0), 0))

    return pl.pallas_call(
        body, name=name, out_shape=jax.ShapeDtypeStruct((len(groups) * r, cdim), F32), grid=(len(groups), steps),
        in_specs=[in_spec(gi) for gi in range(len(groups))],
        out_specs=pl.BlockSpec((tr, cdim), lambda gg, i: (gg * steps + i, 0)),
        compiler_params=_params(("parallel", "parallel")),
    )(*groups)


def _mm_tn(a, b, *, name, after=None, tm=512, tn=1024):
    k, m = a.shape
    n = b.shape[1]
    tm, tn = _tile(m, tm), _tile(n, tn)

    def body(a_ref, b_ref, *rest):
        o_ref, at_ref = rest[-2], rest[-1]

        @pl.when(pl.program_id(1) == 0)
        def _():
            at_ref[...] = a_ref[...].astype(BF16).T

        o_ref[...] = _dot(at_ref[...], b_ref[...].astype(BF16), 1, 0).astype(BF16)

    ins = [a, b] + ([] if after is None else [after])
    return pl.pallas_call(
        body, name=name, out_shape=jax.ShapeDtypeStruct((m, n), BF16), grid=(m // tm, n // tn),
        in_specs=[pl.BlockSpec((k, tm), lambda i, j: (0, i)), pl.BlockSpec((k, tn), lambda i, j: (0, j))] + [ANY] * (len(ins) - 2),
        out_specs=pl.BlockSpec((tm, tn), lambda i, j: (i, j)),
        scratch_shapes=[pltpu.VMEM((tm, k), BF16)], compiler_params=_params(("parallel", "arbitrary")),
    )(*ins)


def _mm2(a1, b1, a2, b2, *, name, after=None, tm=256, tn=1024):
    m = a1.shape[0]
    n = b1.shape[1]
    tm, tn = _tile(m, tm), _tile(n, tn)

    def body(a1_ref, b1_ref, a2_ref, b2_ref, *rest):
        rest[-1][...] = (_dot(a1_ref[...].astype(BF16), b1_ref[...], 1, 0)
                         + _dot(a2_ref[...].astype(BF16), b2_ref[...], 1, 0))

    ins = [a1, b1, a2, b2] + ([] if after is None else [after])

    def a_spec(t):
        return pl.BlockSpec((tm, t.shape[1]), lambda i, j: (i, 0))

    def b_spec(t):
        return pl.BlockSpec((t.shape[0], tn), lambda i, j: (0, j))

    return pl.pallas_call(
        body, name=name, out_shape=jax.ShapeDtypeStruct((m, n), F32), grid=(m // tm, n // tn),
        in_specs=[a_spec(a1), b_spec(b1), a_spec(a2), b_spec(b2)] + [ANY] * (len(ins) - 4),
        out_specs=pl.BlockSpec((tm, tn), lambda i, j: (i, j)), compiler_params=_params(("parallel", "parallel")),
    )(*ins)


def _mm(a, b, *, name, ta=False, tb=False, res=None, colscale=None, emit_acc=False,
        out_dtype=F32, tm=512, tn=512):
    m, k = (a.shape[1], a.shape[0]) if ta else a.shape
    n = b.shape[0] if tb else b.shape[1]
    tm, tn = _tile(m, tm), _tile(n, tn)
    ca, cb = (0 if ta else 1), (1 if tb else 0)
    a_spec = pl.BlockSpec((k, tm), lambda i, j: (0, i)) if ta else pl.BlockSpec((tm, k), lambda i, j: (i, 0))
    b_spec = pl.BlockSpec((tn, k), lambda i, j: (j, 0)) if tb else pl.BlockSpec((k, tn), lambda i, j: (0, j))
    tile = pl.BlockSpec((tm, tn), lambda i, j: (i, j))
    ins, in_specs = [a, b], [a_spec, b_spec]
    if res is not None:
        ins.append(res)
        in_specs.append(tile)
    if colscale is not None:
        ins.append(colscale)
        in_specs.append(pl.BlockSpec((1, tn), lambda i, j: (0, j)))
    n_in = len(ins)

    def body(*refs):
        outs = refs[n_in:]
        acc = _dot(refs[0][...].astype(BF16), refs[1][...].astype(BF16), ca, cb)
        val, p = acc, 2
        if res is not None:
            r_val, p = refs[p][...], p + 1
        if colscale is not None:
            val = val * refs[p][...]
        if res is not None:
            val = r_val + val
        if emit_acc:
            outs[0][...] = acc
        outs[-1][...] = val.astype(out_dtype)

    out_shape = [jax.ShapeDtypeStruct((m, n), out_dtype)]
    out_specs = [tile]
    if emit_acc:
        out_shape.insert(0, jax.ShapeDtypeStruct((m, n), F32))
        out_specs.insert(0, tile)
    out = pl.pallas_call(
        body, name=name, out_shape=out_shape, grid=(m // tm, n // tn), in_specs=in_specs, out_specs=out_specs,
        compiler_params=_params(("parallel", "parallel")),
    )(*ins)
    return out if emit_acc else out[0]


def _norm_fwd(x, g, scale, shift, name, after=None):
    s, d = x.shape
    tr = 256

    def body(x_ref, g_ref, sc_ref, sh_ref, *rest):
        xv = x_ref[...]
        rstd = lax.rsqrt(jnp.mean(xv * xv, axis=-1, keepdims=True) + RMS_EPS)
        rest[-1][...] = (xv * rstd * g_ref[...] * (1.0 + sc_ref[...]) + sh_ref[...]).astype(BF16)

    rowspec = pl.BlockSpec((1, d), lambda i: (0, 0))
    ins = [x, g, scale, shift] + ([] if after is None else [after])
    return pl.pallas_call(
        body, name=name, out_shape=jax.ShapeDtypeStruct((s, d), BF16), grid=(s // tr,),
        in_specs=[pl.BlockSpec((tr, d), lambda i: (i, 0)), rowspec, rowspec, rowspec] + [ANY] * (len(ins) - 4),
        out_specs=pl.BlockSpec((tr, d), lambda i: (i, 0)),
        compiler_params=_params(("parallel",)),
    )(*ins)


def _norm_bwd(x, dh, dres, g, scale, name):
    s, d = x.shape
    tr = 256

    def body(x_ref, dh_ref, dr_ref, g_ref, sc_ref, dx_ref, a_ref, b_ref):
        @pl.when(pl.program_id(0) == 0)
        def _():
            a_ref[...] = jnp.zeros_like(a_ref)
            b_ref[...] = jnp.zeros_like(b_ref)

        xv = x_ref[...]
        rstd = lax.rsqrt(jnp.mean(xv * xv, axis=-1, keepdims=True) + RMS_EPS)
        xhat = xv * rstd
        dhv = dh_ref[...]
        dxhat = dhv * (g_ref[...] * (1.0 + sc_ref[...]))
        mean_term = jnp.mean(dxhat * xhat, axis=-1, keepdims=True)
        dx_ref[...] = dr_ref[...] + rstd * (dxhat - xhat * mean_term)
        a_ref[...] += jnp.sum(dhv, axis=0, keepdims=True)
        b_ref[...] += jnp.sum(dhv * xhat, axis=0, keepdims=True)

    rowspec = pl.BlockSpec((1, d), lambda i: (0, 0))
    tile = pl.BlockSpec((tr, d), lambda i: (i, 0))
    return pl.pallas_call(
        body, name=name,
        out_shape=[jax.ShapeDtypeStruct((s, d), F32), jax.ShapeDtypeStruct((1, d), F32), jax.ShapeDtypeStruct((1, d), F32)],
        grid=(s // tr,), in_specs=[tile, tile, tile, rowspec, rowspec], out_specs=[tile, rowspec, rowspec],
        compiler_params=_params(("arbitrary",)),
    )(x, dh, dres, g, scale)


def _gate_bwd(dxn, f, colscale, coef, name):
    s, d = dxn.shape
    tr = 256

    def body(dx_ref, f_ref, cs_ref, df_ref, dg_ref):
        @pl.when(pl.program_id(0) == 0)
        def _():
            dg_ref[...] = jnp.zeros_like(dg_ref)

        dxv = dx_ref[...]
        df_ref[...] = (dxv * cs_ref[...]).astype(BF16)
        dg_ref[...] += coef * jnp.sum(dxv * f_ref[...], axis=0, keepdims=True)

    rowspec = pl.BlockSpec((1, d), lambda i: (0, 0))
    tile = pl.BlockSpec((tr, d), lambda i: (i, 0))
    return pl.pallas_call(
        body, name=name, out_shape=[jax.ShapeDtypeStruct((s, d), BF16), jax.ShapeDtypeStruct((1, d), F32)],
        grid=(s // tr,), in_specs=[tile, tile, rowspec], out_specs=[tile, rowspec],
        compiler_params=_params(("arbitrary",)),
    )(dxn, f, colscale)


def _ffn_up(h, wg, wu, name, tm=SEQ, tn=256):
    s, d = h.shape
    f = wg.shape[0]

    def body(h_ref, wg_ref, wu_ref, a_ref, u_ref, s_ref):
        hv = h_ref[...]
        a = _dot(hv, wg_ref[...], 1, 1)
        u = _dot(hv, wu_ref[...], 1, 1)
        a_ref[...] = a.astype(BF16)
        u_ref[...] = u.astype(BF16)
        s_ref[...] = (a * _sigmoid(a) * u).astype(BF16)

    tile = pl.BlockSpec((tm, tn), lambda i, j: (i, j))
    wspec = pl.BlockSpec((tn, d), lambda i, j: (j, 0))
    return pl.pallas_call(
        body, name=name,
        out_shape=[jax.ShapeDtypeStruct((s, f), BF16), jax.ShapeDtypeStruct((s, f), BF16), jax.ShapeDtypeStruct((s, f), BF16)],
        grid=(s // tm, f // tn), in_specs=[pl.BlockSpec((tm, d), lambda i, j: (i, 0)), wspec, wspec],
        out_specs=[tile, tile, tile], compiler_params=_params(("parallel", "parallel")),
    )(h, wg, wu)


def _ffn_bwd_ds(df, wd, a, u, name, tm=SEQ, tn=256):
    s, d = df.shape
    f = wd.shape[0]

    def body(df_ref, wd_ref, a_ref, u_ref, da_ref, du_ref):
        ds = _dot(df_ref[...], wd_ref[...], 1, 1)
        av = a_ref[...].astype(F32)
        sg = _sigmoid(av)
        da_ref[...] = (ds * u_ref[...].astype(F32) * (sg * (1.0 + av * (1.0 - sg)))).astype(BF16)
        du_ref[...] = (ds * (av * sg)).astype(BF16)

    tile = pl.BlockSpec((tm, tn), lambda i, j: (i, j))
    return pl.pallas_call(
        body, name=name, out_shape=[jax.ShapeDtypeStruct((s, f), BF16), jax.ShapeDtypeStruct((s, f), BF16)],
        grid=(s // tm, f // tn),
        in_specs=[pl.BlockSpec((tm, d), lambda i, j: (i, 0)), pl.BlockSpec((tn, d), lambda i, j: (j, 0)), tile, tile],
        out_specs=[tile, tile], compiler_params=_params(("parallel", "parallel")),
    )(df, wd, a, u)


def _merge_fwd(o_sb, o_dil, o_swa, gates, wb_sb, wb_dil, wb_swa, name):
    s, d = SEQ, D_MODEL
    tm = 256

    def body(osb_ref, odl_ref, osw_ref, g_ref, wsb_ref, wdl_ref, wsw_ref, m_ref, tsb_ref, tdl_ref, tsw_ref):
        for h in range(osb_ref.shape[0]):
            tsb_ref[:, h * HEAD_DIM:(h + 1) * HEAD_DIM] = osb_ref[h].astype(BF16)
        for h in range(osw_ref.shape[0]):
            tsw_ref[:, h * HEAD_DIM:(h + 1) * HEAD_DIM] = osw_ref[h].astype(BF16)
        tdl_ref[...] = odl_ref[...].astype(BF16)
        acc = _sigmoid(g_ref[:, 0:d]) * _dot(tsb_ref[...], wsb_ref[...], 1, 0)
        acc += _sigmoid(g_ref[:, d:2 * d]) * _dot(tdl_ref[...], wdl_ref[...], 1, 0)
        acc += _sigmoid(g_ref[:, 2 * d:3 * d]) * _dot(tsw_ref[...], wsw_ref[...], 1, 0)
        m_ref[...] = acc.astype(BF16)

    def rows(w):
        return pl.BlockSpec((tm, w), lambda i: (i, 0))

    def heads(n):
        return pl.BlockSpec((n, tm, HEAD_DIM), lambda i: (0, i, 0))

    def whole(w):
        return pl.BlockSpec((w, d), lambda i: (0, 0))

    return pl.pallas_call(
        body, name=name, out_shape=[jax.ShapeDtypeStruct((s, w), BF16) for w in (d, 256, 128, 384)], grid=(s // tm,),
        in_specs=[heads(H_SB), rows(128), heads(H_SWA_Q), rows(3 * d), whole(256), whole(128), whole(384)],
        out_specs=[rows(d), rows(256), rows(128), rows(384)], compiler_params=_params(("parallel",)),
    )(o_sb, o_dil, o_swa, gates, wb_sb, wb_dil, wb_swa)


def _merge_bwd(dmerged, t_sb, t_dil, t_swa, gates, wb_sb, wb_dil, wb_swa, name):
    s, d = SEQ, D_MODEL
    tm = 256

    def body(dm_ref, tsb_ref, tdl_ref, tsw_ref, g_ref, wsb_ref, wdl_ref, wsw_ref,
             dg_ref, dosb_ref, dodl_ref, dosw_ref, dbsb_ref, dbdl_ref, dbsw_ref):
        dm = dm_ref[...]
        for idx, (t_ref, w_ref, do_ref, db_ref) in enumerate((
                (tsb_ref, wsb_ref, dosb_ref, dbsb_ref), (tdl_ref, wdl_ref, dodl_ref, dbdl_ref),
                (tsw_ref, wsw_ref, dosw_ref, dbsw_ref))):
            w = w_ref[...]
            br = _dot(t_ref[...], w, 1, 0)
            sg = _sigmoid(g_ref[:, idx * d:(idx + 1) * d])
            dbr = (dm * sg).astype(BF16)
            dg_ref[:, idx * d:(idx + 1) * d] = (dm * br * (sg * (1.0 - sg))).astype(BF16)
            db_ref[...] = dbr
            do = _dot(dbr, w, 1, 1)
            if len(do_ref.shape) == 2:
                do_ref[...] = do
            else:
                for h in range(do_ref.shape[0]):
                    do_ref[h] = do[:, h * HEAD_DIM:(h + 1) * HEAD_DIM]

    def rows(w):
        return pl.BlockSpec((tm, w), lambda i: (i, 0))

    def heads(n):
        return pl.BlockSpec((n, tm, HEAD_DIM), lambda i: (0, i, 0))

    def whole(w):
        return pl.BlockSpec((w, d), lambda i: (0, 0))

    def shp(w, dt):
        return jax.ShapeDtypeStruct((s, w), dt)

    def hshp(n):
        return jax.ShapeDtypeStruct((n, s, HEAD_DIM), F32)

    return pl.pallas_call(
        body, name=name,
        out_shape=[shp(3 * d, BF16), hshp(H_SB), shp(128, F32), hshp(H_SWA_Q), shp(d, BF16), shp(d, BF16), shp(d, BF16)],
        grid=(s // tm,),
        in_specs=[rows(d), rows(256), rows(128), rows(384), rows(3 * d), whole(256), whole(128), whole(384)],
        out_specs=[rows(3 * d), heads(H_SB), rows(128), heads(H_SWA_Q), rows(d), rows(d), rows(d)],
        compiler_params=_params(("parallel",)),
    )(dmerged, t_sb, t_dil, t_swa, gates, wb_sb, wb_dil, wb_swa)


def _final_loss(x, target, g, name):
    s, d = x.shape
    tr = 256

    def body(x_ref, t_ref, g_ref, loss_ref, dx_ref, dg_ref):
        @pl.when(pl.program_id(0) == 0)
        def _():
            loss_ref[...] = jnp.zeros_like(loss_ref)
            dg_ref[...] = jnp.zeros_like(dg_ref)

        xv = x_ref[...]
        gv = g_ref[...]
        rstd = lax.rsqrt(jnp.mean(xv * xv, axis=-1, keepdims=True) + RMS_EPS)
        xhat = xv * rstd
        err = xhat * gv - t_ref[...]
        loss_ref[...] += 0.5 * jnp.sum(jnp.mean(err * err, axis=-1, keepdims=True))
        dy = err * (1.0 / d)
        dxhat = dy * gv
        mean_term = jnp.mean(dxhat * xhat, axis=-1, keepdims=True)
        dx_ref[...] = rstd * (dxhat - xhat * mean_term)
        dg_ref[...] += jnp.sum(dy * xhat, axis=0, keepdims=True)

    rowspec = pl.BlockSpec((1, d), lambda i: (0, 0))
    tile = pl.BlockSpec((tr, d), lambda i: (i, 0))
    return pl.pallas_call(
        body, name=name,
        out_shape=[jax.ShapeDtypeStruct((1, LANES), F32), jax.ShapeDtypeStruct((s, d), F32), jax.ShapeDtypeStruct((1, d), F32)],
        grid=(s // tr,), in_specs=[tile, tile, rowspec],
        out_specs=[pl.BlockSpec((1, LANES), lambda i: (0, 0)), tile, rowspec],
        compiler_params=_params(("arbitrary",)),
    )(x, target, g)


def _adamw(w, g, m, v, name):
    shape = w.shape
    cols = shape[-1]
    rows = int(np.prod(shape[:-1])) if len(shape) > 1 else 1
    tr = rows
    for cand in (1024, 512, 256, 128, 64, 32, 16, 8):
        if rows % cand == 0 and rows > cand and cand * cols * 4 <= (1 << 21):
            tr = cand
            break

    def body(w_ref, g_ref, m_ref, v_ref, d_ref, nm_ref, nv_ref):
        gv = g_ref[...]
        nm = ADAM_B1 * m_ref[...] + (1.0 - ADAM_B1) * gv
        nv = ADAM_B2 * v_ref[...] + (1.0 - ADAM_B2) * (gv * gv)
        m_hat = nm / (1.0 - ADAM_B1 ** ADAM_STEP)
        v_hat = nv / (1.0 - ADAM_B2 ** ADAM_STEP)
        d_ref[...] = -ADAM_LR * (m_hat / (jnp.sqrt(v_hat) + ADAM_EPS) + ADAM_WD * w_ref[...])
        nm_ref[...] = nm
        nv_ref[...] = nv

    tile = pl.BlockSpec((tr, cols), lambda i: (i, 0))
    flat = [t.reshape(rows, cols) for t in (w, g, m, v)]
    out = pl.pallas_call(
        body, name=name, out_shape=[jax.ShapeDtypeStruct((rows, cols), F32)] * 3, grid=(rows // tr,),
        in_specs=[tile] * 4, out_specs=[tile] * 3, compiler_params=_params(("parallel",)),
    )(*flat)
    return tuple(t.reshape(shape) for t in out)


def _ada_fwd(c_all, w, name):
    n = w.shape[1]

    def body(c_ref, w_ref, o_ref):
        cv = c_ref[...]
        o_ref[...] = jnp.dot(cv * _sigmoid(cv), w_ref[...], preferred_element_type=F32, precision=lax.Precision.HIGHEST)

    return pl.pallas_call(body, name=name, out_shape=jax.ShapeDtypeStruct((N_DEV, n), F32), compiler_params=_params())(c_all, w)


def _ada_bwd(c_all_t, dmod, name):
    n = dmod.shape[1]

    def body(c_ref, d_ref, o_ref):
        cv = c_ref[...]
        o_ref[...] = jnp.dot(cv * _sigmoid(cv), d_ref[...], preferred_element_type=F32, precision=lax.Precision.HIGHEST)

    return pl.pallas_call(body, name=name, out_shape=jax.ShapeDtypeStruct((D_MODEL, n), F32), compiler_params=_params())(c_all_t, dmod)


def _bucket_tables():
    rel = np.arange(BLK)[:, None] + BLK - np.arange(2 * BLK)[None, :]
    max_exact = N_BUCKETS // 2

    def bucket(n):
        nf = np.maximum(n, 1).astype(np.float32)
        large = max_exact + (np.log(nf / np.float32(max_exact)) / np.float32(math.log(MAX_REL_DIST / max_exact))
                             * np.float32(N_BUCKETS - max_exact)).astype(np.int32)
        return np.where(n < max_exact, n, np.minimum(large, N_BUCKETS - 1))

    tabs = []
    for dil, max_dist in ((1, 128), (4, 128), (16, 128), (1, SWA_WINDOW - 1)):
        in_band = (rel >= 0) & (rel <= max_dist)
        tabs.append(np.where(in_band, bucket(np.maximum(rel, 0) * dil), -1))
    return np.stack(tabs).astype(np.int32)


N_SOFT = H_DIL + H_SWA_Q


def _table_of_head(h):
    return jnp.minimum(h // 2, 3)


def _bias_build(rel_bias, tables, name):
    def body(rel_ref, t_ref, o_ref):
        h = pl.program_id(0)
        tb = t_ref[0]
        out = jnp.full((BLK, 2 * BLK), NEG, F32)
        for b in range(N_BUCKETS):
            out = jnp.where(tb == b, rel_ref[b, h], out)
        o_ref[0] = out

    return pl.pallas_call(
        body, name=name, out_shape=jax.ShapeDtypeStruct((N_SOFT, BLK, 2 * BLK), F32), grid=(N_SOFT,),
        in_specs=[pl.BlockSpec(memory_space=pltpu.SMEM),
                  pl.BlockSpec((1, BLK, 2 * BLK), lambda h: (_table_of_head(h), 0, 0))],
        out_specs=pl.BlockSpec((1, BLK, 2 * BLK), lambda h: (h, 0, 0)),
        compiler_params=_params(("parallel",)),
    )(rel_bias, tables)


def _bias_grad(dbias, tables, name):
    def body(d_ref, t_ref, o_ref):
        tb = t_ref[0]
        dv = d_ref[0]
        lane = lax.broadcasted_iota(jnp.int32, (1, LANES), 1)
        out = jnp.zeros((1, LANES), F32)
        for b in range(N_BUCKETS):
            out = jnp.where(lane == b, jnp.sum(jnp.where(tb == b, dv, 0.0)), out)
        o_ref[0] = out

    return pl.pallas_call(
        body, name=name, out_shape=jax.ShapeDtypeStruct((N_SOFT, 1, LANES), F32), grid=(N_SOFT,),
        in_specs=[pl.BlockSpec((1, BLK, 2 * BLK), lambda h: (h, 0, 0)),
                  pl.BlockSpec((1, BLK, 2 * BLK), lambda h: (_table_of_head(h), 0, 0))],
        out_specs=pl.BlockSpec((1, 1, LANES), lambda h: (h, 0, 0)),
        compiler_params=_params(("parallel",)),
    )(dbias, tables)


def _band_layout(g, bias_div):
    assert g == 1 or bias_div == 1
    return bias_div if g == 1 else 1


def _band_specs(length, g, bias_div, offs):
    ns = _band_layout(g, bias_div)

    def seqs(off, div=1):
        return pl.BlockSpec((ns, length, HEAD_DIM), lambda s: (off // ns + s // div, 0, 0))

    xspecs = [seqs(offs[0]), seqs(offs[1], g), seqs(offs[2], g)]
    bspec = pl.BlockSpec((1, BLK, 2 * BLK), lambda s: (s, 0, 0))
    sspec = pl.BlockSpec((ns, 1, LANES), lambda s: (s, 0, 0))
    colspec = pl.BlockSpec((ns, length, 1), lambda s: (s, 0, 0))
    return xspecs, seqs(0), seqs(0, g), bspec, sspec, colspec


def _band_sweep(length, ns, one):
    nblk = length // BLK
    for qq in range(ns):
        if ns * nblk <= 16:
            for i in range(nblk):
                one(qq, i * BLK, max(i - 1, 0) * BLK, i == 0)
        else:
            def step(i, carry, qq=qq):
                one(qq, pl.multiple_of(i * BLK, BLK), pl.multiple_of(jnp.maximum(i - 1, 0) * BLK, BLK), i == 0)
                return carry

            lax.fori_loop(0, nblk, step, 0, unroll=2)


def _band_scores(q_ref, k_ref, b_ref, qq, kq, bq, cur, prv, first):
    qv = q_ref[qq, pl.ds(cur, BLK), :]
    bv = b_ref[bq]
    if first is True:
        sp = jnp.full((BLK, BLK), NEG, F32)
    else:
        sp = _dot(qv, k_ref[kq, pl.ds(prv, BLK), :], 1, 1) + bv[:, :BLK]
        sp = sp if first is False else jnp.where(first, NEG, sp)
    sc = _dot(qv, k_ref[kq, pl.ds(cur, BLK), :], 1, 1) + bv[:, BLK:]
    return qv, sp, sc


def _band_fwd(x, bias, sink, *, nq, offs, g, bias_div, has_sink, name):
    length = x.shape[1]
    ns = _band_layout(g, bias_div)

    def body(q_ref, k_ref, v_ref, b_ref, s_ref, o_ref, lse_ref):
        def one(qq, cur, prv, first):
            kq, bq = qq, 0
            _, sp, sc = _band_scores(q_ref, k_ref, b_ref, qq, kq, bq, cur, prv, first)
            m = jnp.maximum(jnp.max(sp, axis=1, keepdims=True), jnp.max(sc, axis=1, keepdims=True))
            if has_sink:
                sk = s_ref[qq][:, :1]
                m = jnp.maximum(m, sk)
            pp, pc = jnp.exp(sp - m), jnp.exp(sc - m)
            den = jnp.sum(pp, axis=1, keepdims=True) + jnp.sum(pc, axis=1, keepdims=True)
            if has_sink:
                den = den + jnp.exp(sk - m)
            acc = (_dot(pp.astype(BF16), v_ref[kq, pl.ds(prv, BLK), :], 1, 0)
                   + _dot(pc.astype(BF16), v_ref[kq, pl.ds(cur, BLK), :], 1, 0))
            o_ref[qq, pl.ds(cur, BLK), :] = acc / den
            lse_ref[qq, pl.ds(cur, BLK), :] = m + jnp.log(den)

        _band_sweep(length, ns, one)

    xspecs, qspec, _, bspec, sspec, colspec = _band_specs(length, g, bias_div, offs)
    return pl.pallas_call(
        body, name=name,
        out_shape=[jax.ShapeDtypeStruct((nq, length, HEAD_DIM), F32), jax.ShapeDtypeStruct((nq, length, 1), F32)],
        grid=(nq // ns,), in_specs=xspecs + [bspec, sspec],
        out_specs=[qspec, colspec], compiler_params=_params(("parallel",)),
    )(x, x, x, bias, sink)


def _band_bwd(x, bias, sink, o, lse, do, dlse, *, nq, offs, g, bias_div, has_sink, name):
    length = x.shape[1]
    ns = _band_layout(g, bias_div)
    nk, nbias = nq // g, nq // bias_div

    def body(q_ref, k_ref, v_ref, b_ref, s_ref, o_ref, lse_ref, do_ref, dlse_ref,
             dq_ref, dk_ref, dv_ref, db_ref, dsk_ref, dkp_ref, dvp_ref):
        for ref in (db_ref, dsk_ref, dkp_ref, dvp_ref):
            ref[...] = jnp.zeros_like(ref)

        @pl.when(pl.program_id(0) % g == 0)
        def _():
            dk_ref[...] = jnp.zeros_like(dk_ref)
            dv_ref[...] = jnp.zeros_like(dv_ref)

        def one(qq, cur, prv, first):
            kq, bq = qq, 0
            qv, sp, sc = _band_scores(q_ref, k_ref, b_ref, qq, kq, bq, cur, prv, first)
            rows, prow = pl.ds(cur, BLK), pl.ds(prv, BLK)
            lse_v = lse_ref[qq, rows, :]
            pp, pc = jnp.exp(sp - lse_v), jnp.exp(sc - lse_v)
            dov = do_ref[qq, rows, :]
            dob = dov.astype(BF16)
            coef = dlse_ref[qq, rows, :] - jnp.sum(dov * o_ref[qq, rows, :], axis=1, keepdims=True)
            dsp = pp * (_dot(dob, v_ref[kq, prow, :], 1, 1) + coef)
            dsc = pc * (_dot(dob, v_ref[kq, rows, :], 1, 1) + coef)
            dspb, dscb = dsp.astype(BF16), dsc.astype(BF16)
            dq_ref[qq, rows, :] = ((_dot(dspb, k_ref[kq, prow, :], 1, 0) + _dot(dscb, k_ref[kq, rows, :], 1, 0))
                                   * (HEAD_DIM ** -0.5))
            dk_ref[kq, rows, :] += _dot(dscb, qv, 0, 0)
            dkp_ref[kq, prow, :] += _dot(dspb, qv, 0, 0)
            dv_ref[kq, rows, :] += _dot(pc.astype(BF16), dob, 0, 0)
            dvp_ref[kq, prow, :] += _dot(pp.astype(BF16), dob, 0, 0)
            db_ref[bq, :, :BLK] += dsp
            db_ref[bq, :, BLK:] += dsc
            if has_sink:
                dsk_ref[qq] += jnp.sum(jnp.exp(s_ref[qq][:, :1] - lse_v) * coef)

        _band_sweep(length, ns, one)
        dk_ref[...] += dkp_ref[...]
        dv_ref[...] += dvp_ref[...]

    xspecs, qspec, kvspec, bspec, sspec, colspec = _band_specs(length, g, bias_div, offs)
    return pl.pallas_call(
        body, name=name,
        out_shape=[jax.ShapeDtypeStruct((nq, length, HEAD_DIM), F32), jax.ShapeDtypeStruct((nk, length, HEAD_DIM), F32),
                   jax.ShapeDtypeStruct((nk, length, HEAD_DIM), F32), jax.ShapeDtypeStruct((nbias, BLK, 2 * BLK), F32),
                   jax.ShapeDtypeStruct((nq, 1, LANES), F32)],
        grid=(nq // ns,),
        in_specs=xspecs + [bspec, sspec, qspec, colspec, qspec, colspec],
        out_specs=[qspec, kvspec, kvspec, bspec, sspec],
        scratch_shapes=[pltpu.VMEM((ns, length, HEAD_DIM), F32), pltpu.VMEM((ns, length, HEAD_DIM), F32)],
        compiler_params=_params(("arbitrary",)),
    )(x, x, x, bias, sink, o, lse, do, dlse)


TOK_TILE = 512


def _dil_merge(outs, lses, dout, name):
    tr = TOK_TILE
    dils = [d for _, d in DIL_PATTERNS]
    n = len(dils)
    o4 = [o.reshape(2, d, SEQ // d, HEAD_DIM) for o, d in zip(outs, dils)]
    l4 = [l.reshape(2, d, SEQ // d, 1) for l, d in zip(lses, dils)]
    o_specs = [pl.BlockSpec((2, d, tr // d, HEAD_DIM), lambda i: (0, 0, i, 0)) for d in dils]
    l_specs = [pl.BlockSpec((2, d, tr // d, 1), lambda i: (0, 0, i, 0)) for d in dils]
    tok = pl.BlockSpec((tr, 2 * HEAD_DIM), lambda i: (i, 0))
    scratch = ([pltpu.VMEM((tr, 2 * HEAD_DIM), F32) for _ in dils] + [pltpu.VMEM((tr, 1), F32) for _ in range(2 * n)]
               + [pltpu.VMEM((tr // d, 2 * HEAD_DIM), F32) for d in dils])

    def to_tokens(o_ref, l_ref, d, pair, cols, stage):
        for r in range(d):
            rows = pl.ds(r, tr // d, stride=d) if d > 1 else slice(None)
            stage[:, :HEAD_DIM] = o_ref[0, r]
            stage[:, HEAD_DIM:] = o_ref[1, r]
            pair[rows, :] = stage[...]
            for h in range(2):
                cols[h][rows, :] = l_ref[h, r]
        return pair[...], [cols[0][...], cols[1][...]]

    def weights(ls):
        left = lax.broadcasted_iota(jnp.int32, (tr, 2 * HEAD_DIM), 1) < HEAD_DIM
        per_head = []
        for h in range(2):
            m = ls[0][h]
            for g in range(1, n):
                m = jnp.maximum(m, ls[g][h])
            es = [jnp.exp(ls[g][h] - m) for g in range(n)]
            den = es[0]
            for e in es[1:]:
                den = den + e
            per_head.append([e / den for e in es])
        return per_head, [jnp.where(left, per_head[0][g], per_head[1][g]) for g in range(n)], left

    def load(refs):
        pairs, cols, stages = refs[:n], refs[n:3 * n], refs[3 * n:]
        return pairs, [cols[2 * g:2 * g + 2] for g in range(n)], stages

    if dout is None:
        def body(*refs):
            pairs, cols, stages = load(refs[2 * n + 1:])
            toks = [to_tokens(refs[g], refs[n + g], dils[g], pairs[g], cols[g], stages[g]) for g in range(n)]
            _, alphas, _ = weights([t[1] for t in toks])
            acc = alphas[0] * toks[0][0]
            for g in range(1, n):
                acc = acc + alphas[g] * toks[g][0]
            refs[2 * n][...] = acc

        return pl.pallas_call(
            body, name=name, out_shape=jax.ShapeDtypeStruct((SEQ, 2 * HEAD_DIM), F32), grid=(SEQ // tr,),
            in_specs=o_specs + l_specs, out_specs=tok, scratch_shapes=scratch, compiler_params=_params(("parallel",)),
        )(*o4, *l4)

    def body(*refs):
        do_refs, dl_refs = refs[2 * n + 1:3 * n + 1], refs[3 * n + 1:4 * n + 1]
        pairs, cols, stages = load(refs[4 * n + 1:])
        toks = [to_tokens(refs[g], refs[n + g], dils[g], pairs[g], cols[g], stages[g]) for g in range(n)]
        per_head, alphas, left = weights([t[1] for t in toks])
        dov = refs[2 * n][...]
        das = []
        for g in range(n):
            prod = dov * toks[g][0]
            das.append([jnp.sum(jnp.where(left, prod, 0.0), axis=1, keepdims=True),
                        jnp.sum(jnp.where(left, 0.0, prod), axis=1, keepdims=True)])
        dbar = [sum(per_head[h][g] * das[g][h] for g in range(n)) for h in range(2)]
        for g, d in enumerate(dils):
            pairs[g][...] = alphas[g] * dov
            for h in range(2):
                cols[g][h][...] = per_head[h][g] * (das[g][h] - dbar[h])
            for r in range(d):
                rows = pl.ds(r, tr // d, stride=d) if d > 1 else slice(None)
                v = pairs[g][rows, :]
                for h in range(2):
                    do_refs[g][h, r] = v[:, h * HEAD_DIM:(h + 1) * HEAD_DIM]
                    dl_refs[g][h, r] = cols[g][h][rows, :]

    out = pl.pallas_call(
        body, name=name,
        out_shape=[jax.ShapeDtypeStruct(o.shape, F32) for o in o4] + [jax.ShapeDtypeStruct(l.shape, F32) for l in l4],
        grid=(SEQ // tr,), in_specs=o_specs + l_specs + [tok], out_specs=o_specs + l_specs, scratch_shapes=scratch,
        compiler_params=_params(("parallel",)),
    )(*o4, *l4, dout)
    return [t.reshape(s.shape) for t, s in zip(out, list(outs) + list(lses))]


def _tri(cmp):
    r = lax.broadcasted_iota(jnp.int32, (SB_TILE, SB_TILE), 0)
    c = lax.broadcasted_iota(jnp.int32, (SB_TILE, SB_TILE), 1)
    return cmp(r, c).astype(BF16)


def _cum(x, tri, terms):
    acc, rest = None, x
    for _ in range(terms):
        part = rest.astype(BF16)
        rest = rest - part.astype(F32)
        d = _dot(part, tri, 1, 0)
        acc = d if acc is None else acc + d
    return acc


def _sb_logits(q, ks, diagonal):
    t = SB_TILE
    z = _dot(q, ks, 1, 1)
    e = jnp.exp(-jnp.abs(z))
    lf = -(jnp.maximum(z, 0.0) + jnp.log(1.0 + e))
    if not diagonal:
        return z, e, lf, None
    mask = lax.broadcasted_iota(jnp.int32, (t, t), 1) < lax.broadcasted_iota(jnp.int32, (t, t), 0)
    return z, e, jnp.where(mask, lf, 0.0), mask


def _sb_specs(h, s):
    t = SB_TILE
    tile = pl.BlockSpec((h, t, HEAD_DIM), lambda i: (0, i, 0))
    keys = pl.BlockSpec((h, s, HEAD_DIM), lambda i: (1, 0, 0))
    values = pl.BlockSpec((h, s, HEAD_DIM), lambda i: (2, 0, 0))
    return tile, keys, values, pl.BlockSpec((h, t, 1), lambda i: (0, i, 0))


def _sb_fwd(x, name):
    h, s = x.shape[0] // 3, x.shape[1]
    t = SB_TILE

    def body(q_ref, k_ref, v_ref, o_ref, tot_ref):
        i = pl.program_id(0)
        after = _tri(lambda r, c: r > c)

        def tile(j, carry, diagonal):
            rows = pl.ds(pl.multiple_of(j * t, t), t)
            out = []
            for hh, (right, acc) in enumerate(carry):
                z, _, lf, mask = _sb_logits(q_ref[hh], k_ref[hh, rows, :], diagonal)
                w = jnp.exp(z + lf + (right + _cum(lf, after, 2)))
                w = w if mask is None else jnp.where(mask, w, 0.0)
                out.append((right + jnp.sum(lf, axis=1, keepdims=True), acc + _dot(w.astype(BF16), v_ref[hh, rows, :], 1, 0)))
            return tuple(out)

        carry = tile(i, tuple((jnp.zeros((t, 1), F32), jnp.zeros((t, HEAD_DIM), F32)) for _ in range(h)), True)
        carry = lax.fori_loop(0, i, lambda jj, c: tile(i - 1 - jj, c, False), carry)
        for hh, (right, acc) in enumerate(carry):
            o_ref[hh] = acc
            tot_ref[hh] = right

    tile_spec, keys, values, col = _sb_specs(h, s)
    return pl.pallas_call(
        body, name=name, out_shape=[jax.ShapeDtypeStruct((h, s, HEAD_DIM), F32), jax.ShapeDtypeStruct((h, s, 1), F32)],
        grid=(s // t,), in_specs=[tile_spec, keys, values], out_specs=[tile_spec, col],
        compiler_params=_params(("parallel",)),
    )(x, x, x)


def _sb_bwd(x, tot, do, name):
    h, s = x.shape[0] // 3, x.shape[1]
    t = SB_TILE

    def body(q_ref, k_ref, v_ref, tot_ref, do_ref, dq_ref, dk_ref, dv_ref):
        i = pl.program_id(0)

        @pl.when(i == 0)
        def _():
            dk_ref[...] = jnp.zeros_like(dk_ref)
            dv_ref[...] = jnp.zeros_like(dv_ref)

        upto = _tri(lambda r, c: r <= c)
        before = _tri(lambda r, c: r < c)

        def tile(j, carry, diagonal):
            rows = pl.ds(pl.multiple_of(j * t, t), t)
            out = []
            for hh, (left, cleft, dq) in enumerate(carry):
                qv, ks, dob = q_ref[hh], k_ref[hh, rows, :], do_ref[hh].astype(BF16)
                z, e, lf, mask = _sb_logits(qv, ks, diagonal)
                between = tot_ref[hh] - (left + _cum(lf, upto, 2))
                w = jnp.exp(z + lf + between)
                w = w if mask is None else jnp.where(mask, w, 0.0)
                dlog = w * _dot(dob, v_ref[hh, rows, :], 1, 1)
                cfail = cleft + _cum(dlog, before, 2)
                sig = jnp.where(z >= 0.0, 1.0, e) / (1.0 + e)
                dz = dlog * (1.0 - sig) - sig * cfail
                dz = (dz if mask is None else jnp.where(mask, dz, 0.0)).astype(BF16)
                dk_ref[hh, rows, :] += _dot(dz, qv, 0, 0)
                dv_ref[hh, rows, :] += _dot(w.astype(BF16), dob, 0, 0)
                out.append((left + jnp.sum(lf, axis=1, keepdims=True), cleft + jnp.sum(dlog, axis=1, keepdims=True),
                            dq + _dot(dz, ks, 1, 0)))
            return tuple(out)

        zero = jnp.zeros((t, 1), F32)
        carry = lax.fori_loop(0, i, lambda j, c: tile(j, c, False),
                              tuple((zero, zero, jnp.zeros((t, HEAD_DIM), F32)) for _ in range(h)))
        for hh, (_, _, dq) in enumerate(tile(i, carry, True)):
            dq_ref[hh] = dq * (HEAD_DIM ** -0.5)

    tile_spec, keys, values, col = _sb_specs(h, s)
    full = pl.BlockSpec((h, s, HEAD_DIM), lambda i: (0, 0, 0))
    shp = jax.ShapeDtypeStruct((h, s, HEAD_DIM), F32)
    return pl.pallas_call(
        body, name=name, out_shape=[shp, shp, shp], grid=(s // t,),
        in_specs=[tile_spec, keys, values, col, tile_spec],
        out_specs=[tile_spec, full, full], compiler_params=_params(("arbitrary",)),
    )(x, x, x, tot, do)


COL_SB, COL_DIL, COL_SWA = 0, 3 * H_SB * HEAD_DIM, 3 * H_SB * HEAD_DIM + 3 * H_DIL * HEAD_DIM
N_SWA = H_SWA_Q + 2 * H_SWA_KV


def _dil_col(t, g):
    return COL_DIL + t * H_DIL * HEAD_DIM + g * 2 * HEAD_DIM


def _split_heads(qkv, name):
    tr = TOK_TILE
    scale = HEAD_DIM ** -0.5
    dils = [d for _, d in DIL_PATTERNS]

    def body(x_ref, sb_ref, d0_ref, d1_ref, d2_ref, swa_ref, pair):
        def head(col, scaled):
            v = x_ref[:, col:col + HEAD_DIM]
            return (v * scale if scaled else v).astype(BF16)

        for hh in range(3 * H_SB):
            sb_ref[hh] = head(COL_SB + hh * HEAD_DIM, hh < H_SB)
        for hh in range(N_SWA):
            swa_ref[hh] = head(COL_SWA + hh * HEAD_DIM, hh < H_SWA_Q)
        for t in range(3):
            for g, (d, out_ref) in enumerate(zip(dils, (d0_ref, d1_ref, d2_ref))):
                col = _dil_col(t, g)
                if d == 1:
                    for h in range(2):
                        out_ref[t * 2 + h] = head(col + h * HEAD_DIM, t == 0)
                    continue
                pair[...] = x_ref[:, col:col + 2 * HEAD_DIM]
                for r in range(d):
                    v = pair[pl.ds(r, tr // d, stride=d), :]
                    v = v * scale if t == 0 else v
                    for h in range(2):
                        out_ref[t * 2 * d + h * d + r] = v[:, h * HEAD_DIM:(h + 1) * HEAD_DIM].astype(BF16)

    def heads(n, length):
        return jax.ShapeDtypeStruct((n, length, HEAD_DIM), BF16)

    def spec(n, rows):
        return pl.BlockSpec((n, rows, HEAD_DIM), lambda i: (0, i, 0))

    return pl.pallas_call(
        body, name=name,
        out_shape=[heads(3 * H_SB, SEQ)] + [heads(6 * d, SEQ // d) for d in dils] + [heads(N_SWA, SEQ)],
        grid=(SEQ // tr,), in_specs=[pl.BlockSpec((tr, D_QKV), lambda i: (i, 0))],
        out_specs=[spec(3 * H_SB, tr)] + [spec(6 * d, tr // d) for d in dils] + [spec(N_SWA, tr)],
        scratch_shapes=[pltpu.VMEM((tr, 2 * HEAD_DIM), F32)], compiler_params=_params(("parallel",)),
    )(qkv)


def _join_heads(sb, dil, swa, name):
    tr = TOK_TILE
    dils = [d for _, d in DIL_PATTERNS]

    def body(*refs):
        sb_refs, dil_refs, swa_refs = refs[:3], [refs[3 + 3 * g:6 + 3 * g] for g in range(3)], refs[12:15]
        o_ref, pair, stages = refs[15], refs[16], refs[17:]

        def put(col, v):
            o_ref[:, col:col + v.shape[1]] = v.astype(BF16)

        for t in range(3):
            for h in range(H_SB):
                put(COL_SB + (t * H_SB + h) * HEAD_DIM, sb_refs[t][h])
        col = COL_SWA
        for ref in swa_refs:
            for h in range(ref.shape[0]):
                put(col, ref[h])
                col += HEAD_DIM
        for t in range(3):
            for g, d in enumerate(dils):
                ref, col = dil_refs[g][t], _dil_col(t, g)
                if d == 1:
                    for h in range(2):
                        put(col + h * HEAD_DIM, ref[h])
                    continue
                stage = stages[g - 1]
                for r in range(d):
                    stage[:, :HEAD_DIM] = ref[r]
                    stage[:, HEAD_DIM:] = ref[d + r]
                    pair[pl.ds(r, tr // d, stride=d), :] = stage[...]
                put(col, pair[...])

    def spec(n, rows):
        return pl.BlockSpec((n, rows, HEAD_DIM), lambda i: (0, i, 0))

    ins = list(sb) + [t for g in range(3) for t in dil[g]] + list(swa)
    in_specs = ([spec(H_SB, tr)] * 3 + [spec(2 * d, tr // d) for d in dils for _ in range(3)]
                + [spec(H_SWA_Q, tr), spec(H_SWA_KV, tr), spec(H_SWA_KV, tr)])
    return pl.pallas_call(
        body, name=name, out_shape=jax.ShapeDtypeStruct((SEQ, D_QKV), BF16), grid=(SEQ // tr,), in_specs=in_specs,
        out_specs=pl.BlockSpec((tr, D_QKV), lambda i: (i, 0)),
        scratch_shapes=[pltpu.VMEM((tr, 2 * HEAD_DIM), F32)] + [pltpu.VMEM((tr // d, 2 * HEAD_DIM), F32) for d in dils[1:]],
        compiler_params=_params(("parallel",)),
    )(*ins)


def _mixer_fwd(qkv, bias, sinks_l, tag):
    sb, d0, d1, d2, swa = _split_heads(qkv, name=f"split_heads_{tag}")
    st = {"sb": sb, "dil": (d0, d1, d2), "swa": swa}
    o_sb, st["sb_tot"] = _sb_fwd(sb, name=f"sb_fwd_{tag}")
    st["dil_out"], st["dil_lse"], st["dil_sink"] = [], [], []
    for gi, (_, d) in enumerate(DIL_PATTERNS):
        sink = jnp.zeros((2 * d, 1, LANES), F32)
        og, lg = _band_fwd(st["dil"][gi], bias[2 * gi:2 * gi + 2], sink, nq=2 * d, offs=(0, 2 * d, 4 * d), g=1, bias_div=d,
                           has_sink=False, name=f"dil{gi}_fwd_{tag}")
        st["dil_out"].append(og)
        st["dil_lse"].append(lg)
        st["dil_sink"].append(sink)
    o_dil = _dil_merge(st["dil_out"], st["dil_lse"], None, name=f"dil_merge_fwd_{tag}")
    st["swa_sink"] = jnp.broadcast_to(sinks_l.reshape(H_SWA_Q, 1, 1), (H_SWA_Q, 1, LANES))
    st["swa_out"] = _band_fwd(swa, bias[H_DIL:], st["swa_sink"], nq=H_SWA_Q, offs=(0, H_SWA_Q, H_SWA_Q + H_SWA_KV),
                              g=H_SWA_Q // H_SWA_KV, bias_div=1, has_sink=True, name=f"swa_fwd_{tag}")
    return (o_sb, o_dil, st["swa_out"][0]), st


def _mixer_bwd(st, bias, do_sb, do_dil, do_swa, tag):
    d_sb = _sb_bwd(st["sb"], st["sb_tot"], do_sb, name=f"sb_bwd_{tag}")
    dmerge = _dil_merge(st["dil_out"], st["dil_lse"], do_dil, name=f"dil_merge_bwd_{tag}")
    d_dil, dbs = [], []
    for gi, (_, d) in enumerate(DIL_PATTERNS):
        dq, dk, dv, db, _ = _band_bwd(st["dil"][gi], bias[2 * gi:2 * gi + 2], st["dil_sink"][gi], st["dil_out"][gi],
                                      st["dil_lse"][gi], dmerge[gi], dmerge[3 + gi], nq=2 * d, offs=(0, 2 * d, 4 * d),
                                      g=1, bias_div=d, has_sink=False, name=f"dil{gi}_bwd_{tag}")
        d_dil.append((dq, dk, dv))
        dbs.append(db)
    o_sw, l_sw = st["swa_out"]
    dq_sw, dk_sw, dv_sw, db_sw, dsink = _band_bwd(st["swa"], bias[H_DIL:], st["swa_sink"], o_sw, l_sw, do_swa,
                                                  jnp.zeros_like(l_sw), nq=H_SWA_Q, offs=(0, H_SWA_Q, H_SWA_Q + H_SWA_KV),
                                                  g=H_SWA_Q // H_SWA_KV, bias_div=1, has_sink=True, name=f"swa_bwd_{tag}")
    dqkv = _join_heads(d_sb, d_dil, (dq_sw, dk_sw, dv_sw), name=f"join_heads_{tag}")
    return dqkv, jnp.concatenate(dbs + [db_sw], 0), dsink[:, 0, 0]


PIECES = ("ffn0", "mix", "ffn1")


def _ffn_fwd(x_in, w, gain, mod_j, tag, after=None):
    st = {"x": x_in, "w": w}
    st["h"] = _norm_fwd(x_in, _row(gain), _row(mod_j[1]), _row(mod_j[0]), name=f"norm_fwd_{tag}", after=after)
    st["a"], st["u"], st["s"] = _ffn_up(st["h"], w["gate"], w["up"], name=f"up_{tag}")
    st["f"], x_out = _mm(st["s"], w["down"], res=x_in, colscale=_row(0.5 * mod_j[2]), emit_acc=True, tm=512, tn=1024,
                         name=f"down_{tag}")
    return x_out, st


def _ffn_bwd(dx_out, st, gain, mod_j, tag, done):
    w = st["w"]

    def latest(new, old):
        return old if new is None else new

    df, dgate = _gate_bwd(dx_out, st["f"], _row(0.5 * mod_j[2]), 0.5, name=f"gate_bwd_{tag}")
    token = done({"down": _mm_tn(st["s"], df, tm=D_FF // 2, name=f"dwd_{tag}")})
    da, du = _ffn_bwd_ds(df, w["down"], st["a"], st["u"], name=f"ds_{tag}")
    token = latest(done({"gate": _mm_tn(da, st["h"], after=token, tm=D_FF // 2, name=f"dwg_{tag}")}), token)
    token = latest(done({"up": _mm_tn(du, st["h"], after=token, tm=D_FF // 2, name=f"dwu_{tag}")}), token)
    dh = _mm2(da, w["gate"], du, w["up"], after=token, name=f"dh_{tag}")
    dx_in, sum_dh, sum_dhx = _norm_bwd(st["x"], dh, dx_out, _row(gain), _row(mod_j[1]), name=f"norm_bwd_{tag}")
    dmod = jnp.concatenate([sum_dh, gain * sum_dhx, dgate], 0)
    return dx_in, dmod, (1.0 + mod_j[1]) * sum_dhx[0]


def _mix_fwd(x_in, w, gain, mod_j, bias, sinks_l, tag, after=None):
    st = {"x": x_in, "w": w}
    st["h"] = _norm_fwd(x_in, _row(gain), _row(mod_j[1]), _row(mod_j[0]), name=f"norm_fwd_mix_{tag}", after=after)
    qkv = _mm(st["h"], w["qkv"], tb=True, tm=SEQ, name=f"qkv_{tag}")
    st["gates"] = _mm(st["h"], w["gates"], tb=True, tm=SEQ, name=f"gates_{tag}")
    outs, st["mix"] = _mixer_fwd(qkv, bias, sinks_l, tag)
    st["merged"], *st["t"] = _merge_fwd(*outs, st["gates"], w["br_sb"], w["br_dil"], w["br_swa"], name=f"merge_fwd_{tag}")
    st["f"], x_out = _mm(st["merged"], w["out"], res=x_in, colscale=_row(mod_j[2]), emit_acc=True, name=f"out_{tag}")
    return x_out, st


def _mix_bwd(dx_out, st, gain, mod_j, bias, tag, done):
    w = st["w"]
    df, dgate = _gate_bwd(dx_out, st["f"], _row(mod_j[2]), 1.0, name=f"gate_bwd_mix_{tag}")
    g = {"out": _mm_tn(st["merged"], df, name=f"dw_out_{tag}")}
    dmerged = _mm(df, w["out"], tb=True, name=f"dmerged_{tag}")
    dgates, do_sb, do_dil, do_swa, dbr_sb, dbr_dil, dbr_swa = _merge_bwd(
        dmerged, *st["t"], st["gates"], w["br_sb"], w["br_dil"], w["br_swa"], name=f"merge_bwd_{tag}")
    g["br_sb"] = _mm_tn(st["t"][0], dbr_sb, name=f"dw_br_sb_{tag}")
    g["br_dil"] = _mm_tn(st["t"][1], dbr_dil, name=f"dw_br_dil_{tag}")
    g["br_swa"] = _mm_tn(st["t"][2], dbr_swa, name=f"dw_br_swa_{tag}")
    dqkv, dbias, dsinks = _mixer_bwd(st["mix"], bias, do_sb, do_dil, do_swa, tag)
    g["qkv"] = _mm_tn(dqkv, st["h"], name=f"dw_qkv_{tag}")
    g["gates"] = _mm_tn(dgates, st["h"], name=f"dw_gates_{tag}")
    dh = _mm2(dqkv, w["qkv"], dgates, w["gates"], after=done(g), tm=512, name=f"dh_mix_{tag}")
    dx_in, sum_dh, sum_dhx = _norm_bwd(st["x"], dh, dx_out, _row(gain), _row(mod_j[1]), name=f"norm_bwd_mix_{tag}")
    dmod = jnp.concatenate([sum_dh, gain * sum_dhx, dgate], 0)
    return dx_in, dmod, (1.0 + mod_j[1]) * sum_dhx[0], dbias, dsinks


def _local_step(x, target, mod, gains, weights_of, rel_bias, sinks, final_gain, grads_done):
    tables = jnp.asarray(_bucket_tables())
    bias = _bias_build(rel_bias, tables, name="bias_build")
    states, h = [], x
    for l in range(DEPTH):
        st = {}
        for j, piece in enumerate(PIECES):
            w, after = weights_of(l, piece, h)
            if piece == "mix":
                h, st[piece] = _mix_fwd(h, w, gains[l, j], mod[l, j], bias, sinks[l], f"l{l}", after)
            else:
                h, st[piece] = _ffn_fwd(h, w, gains[l, j], mod[l, j], f"{piece}_l{l}", after)
        states.append(st)
    loss, dx, dfinal = _final_loss(h, target, _row(final_gain), name="final_loss")
    dmods = [[None] * 3 for _ in range(DEPTH)]
    dgains = [[None] * 3 for _ in range(DEPTH)]
    dsinks = [None] * DEPTH
    dbias = None
    for l in reversed(range(DEPTH)):
        for j in reversed(range(3)):
            piece = PIECES[j]
            done = lambda grads, l=l, piece=piece: grads_done(l, piece, grads)
            if piece == "mix":
                dx, dmods[l][j], dgains[l][j], db, dsinks[l] = _mix_bwd(dx, states[l][piece], gains[l, j], mod[l, j], bias, f"l{l}", done)
                dbias = db if dbias is None else dbias + db
            else:
                dx, dmods[l][j], dgains[l][j] = _ffn_bwd(dx, states[l][piece], gains[l, j], mod[l, j], f"{piece}_l{l}", done)
    drel = _bias_grad(dbias, tables, name="bias_grad")[:, 0, :N_BUCKETS].T
    dmod = jnp.stack([jnp.stack(m) for m in dmods])
    dgain = jnp.stack([jnp.stack(g) for g in dgains])
    return loss, dx, dmod, dgain, dfinal[0], drel, jnp.stack(dsinks)


BR_ROWS = (H_SB * HEAD_DIM, 2 * HEAD_DIM, H_SWA_Q * HEAD_DIM)


def _lanes_unshard(g, lead):
    _, rows, _ = g.shape
    r = rows // lead
    return g.reshape(N_DEV, lead, r, LANES).transpose(1, 2, 0, 3).reshape(lead, r, N_DEV * LANES)


def _lanes_shard(full):
    lead, r, _ = full.shape
    return full.reshape(lead, r, N_DEV, LANES).transpose(2, 0, 1, 3).reshape(N_DEV, lead * r, LANES)


def _pack_rows(parts, dtype):
    flat = jnp.concatenate([p.astype(dtype).reshape(-1) for p in parts])
    pad = (-flat.shape[0]) % (16 * LANES)
    if pad:
        flat = jnp.concatenate([flat, jnp.zeros((pad,), dtype)])
    return flat.reshape(-1, LANES)


def _unshard(gathered, axis):
    moved = jnp.moveaxis(gathered, 0, axis)
    shape = list(moved.shape)
    shape[axis:axis + 2] = [shape[axis] * shape[axis + 1]]
    return moved.reshape(shape)


def kernel(x, c, w_ada, b_ada, norm_gain, w_ffn_gate, w_ffn_up, w_ffn_down, w_in, w_br_sb, w_br_dil, w_br_swa, w_out, sinks, rel_bias, final_gain, loss_target, m_w_ada, m_b_ada, m_norm_gain, m_w_ffn_gate, m_w_ffn_up, m_w_ffn_down, m_w_in, m_w_br_sb, m_w_br_dil, m_w_br_swa, m_w_out, m_sinks, m_rel_bias, m_final_gain, v_w_ada, v_b_ada, v_norm_gain, v_w_ffn_gate, v_w_ffn_up, v_w_ffn_down, v_w_in, v_w_br_sb, v_w_br_dil, v_w_br_swa, v_w_out, v_sinks, v_rel_bias, v_final_gain):
    me = 4 * lax.axis_index("x") + 2 * lax.axis_index("y") + lax.axis_index("c")
    d = D_MODEL
    gate_t, up_t, in_t = jnp.swapaxes(w_ffn_gate, 2, 3), jnp.swapaxes(w_ffn_up, 2, 3), jnp.swapaxes(w_in, 1, 2)

    def piece_shards(l, piece):
        bf = lambda t: t.astype(BF16)
        if piece == "mix":
            return [bf(in_t[l]), jnp.concatenate([bf(w_br_sb[l]), bf(w_br_dil[l]), bf(w_br_swa[l])], 0), bf(w_out[l])]
        i = PIECES.index(piece) // 2
        return [bf(gate_t[l, i]), bf(up_t[l, i]), bf(w_ffn_down[l, i])]

    br_off = np.concatenate([[0], np.cumsum(BR_ROWS)])

    def piece_weights(gathered, piece):
        if piece == "mix":
            g_in, g_br, g_out = gathered
            f_in = g_in.reshape(D_QKV + D_GATES, d)
            f_br = [_lanes_unshard(g_br[:, br_off[k]:br_off[k + 1]], 1)[0] for k in range(3)]
            return {"qkv": f_in[:D_QKV], "gates": f_in[D_QKV:], "br_sb": f_br[0], "br_dil": f_br[1], "br_swa": f_br[2],
                    "out": g_out.reshape(d, d)}
        return {n: g.reshape(D_FF, d) for n, g in zip(("gate", "up", "down"), gathered)}

    small, = _all_gather([_pack_rows([c, norm_gain], F32)], name="gather_cond")
    c_all = small[:, :d // LANES].reshape(N_DEV, d)
    gains = _unshard(small[:, d // LANES:d // LANES + 6].reshape(N_DEV, DEPTH, 3, LANES), 2)

    cols = w_ada.shape[2]
    mod_cols = jnp.stack([_ada_fwd(c_all, w_ada[l], name=f"ada_fwd_l{l}") for l in range(DEPTH)])
    mod_all, = _all_gather([_pack_rows([mod_cols], F32)], name="gather_mod")
    mod_all = mod_all.reshape(N_DEV, -1)[:, :DEPTH * N_DEV * cols].reshape(N_DEV, DEPTH, N_DEV, cols)
    mod_mine = lax.dynamic_index_in_dim(mod_all, me, axis=2, keepdims=False)
    mod = (mod_mine.transpose(1, 0, 2).reshape(DEPTH, N_DEV * cols) + b_ada).reshape(DEPTH, 3, 3, d)

    order = [(l, piece) for l in range(DEPTH) for piece in PIECES]
    eager, ahead = 2, 3
    in_flight = {}
    n_tensors = 3
    first = _all_gather([s for k in range(eager) for s in piece_shards(*order[k])], after=mod_all, name="gather_first")

    def start_gather(k, after):
        l, piece = order[k]
        in_flight[k], token = _exchange_start(piece_shards(l, piece), after, gather=True, name=f"gather_{piece}_l{l}_start")
        return token

    token = first[0]
    for k in range(eager, eager + ahead - 1):
        token = start_gather(k, token)
    mod = mod + token[0, 0]

    def weights_of(l, piece, h):
        k = order.index((l, piece))
        started = eager <= k + ahead < len(order) and k + ahead not in in_flight
        token = start_gather(k + ahead, h) if started else None
        if k < eager:
            return piece_weights(first[n_tensors * k:n_tensors * (k + 1)], piece), token
        landed = _exchange_wait(in_flight[k], h if token is None else token, gather=True, name=f"gather_{piece}_l{l}_wait")
        return piece_weights(landed, piece), token

    exchanges, have = {}, {}

    def grads_done(l, piece, g):
        key = (l, piece)
        have.setdefault(key, {}).update(g)
        if piece == "mix":
            if len(have[key]) < 6:
                return None
            g = have[key]
            s_br = jnp.concatenate([_lanes_shard(g[n][None]) for n in ("br_sb", "br_dil", "br_swa")], 1)
            groups = [(("in", "br", "out"), [jnp.concatenate([g["qkv"], g["gates"]], 0).reshape(N_DEV, -1, d), s_br,
                                             g["out"].reshape(N_DEV, -1, d)])]
        elif key == order[0]:
            groups = [((n,), [t.reshape(N_DEV, -1, d)]) for n, t in g.items()]
        elif len(have[key]) < 3:
            return None
        else:
            groups = [(("gate", "up", "down"), [have[key][n].reshape(N_DEV, -1, d) for n in ("gate", "up", "down")])]
        token = None
        for names, sg in groups:
            state, token = _exchange_start(sg, sg[0], gather=False, name=f"exchange_{piece}_l{l}_{names[0]}_start")
            exchanges.setdefault(key, []).append((names, state))
        return token

    loss, dx, dmod, dgains, dfinal, drel, dsinks = _local_step(
        x[0], loss_target[0], mod, gains, weights_of, rel_bias, sinks, final_gain, grads_done)

    small_parts = [dmod, dgains, dfinal, drel.T, dsinks, loss[0, :1]]
    small_sizes = [int(np.prod(p.shape)) for p in small_parts]
    small_all, = _all_gather([_pack_rows(small_parts, F32)], name="gather_small")
    small_sum = _sum_parts([small_all], name="sum_small").reshape(-1)
    offs = np.concatenate([[0], np.cumsum(small_sizes)])
    g_b_ada = small_sum[offs[0]:offs[1]].reshape(DEPTH, 9 * d)
    g_gain_full = small_sum[offs[1]:offs[2]].reshape(DEPTH, 3, d)
    g_norm_gain = lax.dynamic_slice_in_dim(g_gain_full, me * LANES, LANES, axis=2)
    g_final = small_sum[offs[2]:offs[3]]
    g_rel = small_sum[offs[3]:offs[4]].reshape(N_SOFT, N_BUCKETS).T
    g_sinks = small_sum[offs[4]:offs[5]].reshape(DEPTH, H_SWA_Q)
    loss_total = small_sum[offs[5]]

    dmod_all = small_all.reshape(N_DEV, -1)[:, :DEPTH * 9 * d].reshape(N_DEV, DEPTH, 9 * d)
    dmod_cols = lax.dynamic_slice_in_dim(dmod_all, me * cols, cols, axis=2)
    g_w_ada = jnp.stack([_ada_bwd(c_all.T, dmod_cols[:, l], name=f"ada_bwd_l{l}") for l in range(DEPTH)])

    state = {"w_ada": (w_ada, m_w_ada, v_w_ada), "b_ada": (b_ada, m_b_ada, v_b_ada),
             "norm_gain": (norm_gain, m_norm_gain, v_norm_gain), "w_ffn_gate": (w_ffn_gate, m_w_ffn_gate, v_w_ffn_gate),
             "w_ffn_up": (w_ffn_up, m_w_ffn_up, v_w_ffn_up), "w_ffn_down": (w_ffn_down, m_w_ffn_down, v_w_ffn_down),
             "w_in": (w_in, m_w_in, v_w_in), "w_br_sb": (w_br_sb, m_w_br_sb, v_w_br_sb),
             "w_br_dil": (w_br_dil, m_w_br_dil, v_w_br_dil), "w_br_swa": (w_br_swa, m_w_br_swa, v_w_br_swa),
             "w_out": (w_out, m_w_out, v_w_out), "sinks": (sinks, m_sinks, v_sinks),
             "rel_bias": (rel_bias, m_rel_bias, v_rel_bias), "final_gain": (final_gain, m_final_gain, v_final_gain)}
    grad, update = {}, {}

    def adamw(n, g, transposed=False):
        w, m, v = (jnp.swapaxes(t, -1, -2) for t in state[n]) if transposed else state[n]
        if w.ndim == 1:
            out = tuple(t.reshape(w.shape) for t in _adamw(_row(w), _row(g), _row(m), _row(v), name=f"adamw_{n}"))
        else:
            out = _adamw(w, g, m, v, name=f"adamw_{n}")
        if transposed:
            grad[n], update[n] = jnp.swapaxes(g, -1, -2), tuple(jnp.swapaxes(t, -1, -2) for t in out)
        else:
            grad[n], update[n] = g, out

    for n, g in (("w_ada", g_w_ada), ("b_ada", g_b_ada), ("norm_gain", g_norm_gain), ("sinks", g_sinks),
                 ("rel_bias", g_rel), ("final_gain", g_final)):
        adamw(n, g)

    after = update["w_ada"][0]
    parts = {}
    for key in reversed(order):
        for names, ex_state in exchanges[key]:
            landed = _exchange_wait(ex_state, after, gather=False, name=f"exchange_{key[1]}_l{key[0]}_{names[0]}_wait")
            parts.setdefault(key, {}).update(zip(names, landed))
            after = landed[0]
    ffn_keys = [key for key in order if key[1] != "mix"]
    mix_keys = [key for key in order if key[1] == "mix"]
    sums = {n: _sum_parts([parts[key][n] for key in ffn_keys], name=f"sum_grads_{n}") for n in ("gate", "up", "down")}
    sums.update({n: _sum_parts([parts[key][n] for key in mix_keys], name=f"sum_grads_{n}") for n in ("in", "br", "out")})
    br_sums = sums["br"].reshape(DEPTH, -1, LANES)
    adamw("w_ffn_gate", sums["gate"].reshape(gate_t.shape), transposed=True)
    adamw("w_ffn_up", sums["up"].reshape(up_t.shape), transposed=True)
    adamw("w_ffn_down", sums["down"].reshape(w_ffn_down.shape))
    adamw("w_in", sums["in"].reshape(in_t.shape), transposed=True)
    adamw("w_br_sb", br_sums[:, br_off[0]:br_off[1]])
    adamw("w_br_dil", br_sums[:, br_off[1]:br_off[2]])
    adamw("w_br_swa", br_sums[:, br_off[2]:br_off[3]])
    adamw("w_out", sums["out"].reshape(w_out.shape))

    names = ["w_ada", "b_ada", "norm_gain", "w_ffn_gate", "w_ffn_up", "w_ffn_down", "w_in", "w_br_sb", "w_br_dil",
             "w_br_swa", "w_out", "sinks", "rel_bias", "final_gain"]
    return (loss_total, dx[None], *[grad[n] for n in names], *[update[n][0] for n in names],
            *[update[n][1] for n in names], *[update[n][2] for n in names])
```

```python
import math

import numpy as np
import jax
import jax.numpy as jnp
from jax import lax
from jax.experimental import pallas as pl
from jax.experimental.pallas import tpu as pltpu

F32, BF16 = jnp.float32, jnp.bfloat16

SEQ, D_MODEL, D_FF, HEAD_DIM = 2048, 1024, 2816, 64
DEPTH = 2
BLK = 128
H_SB, H_DIL, H_SWA_Q, H_SWA_KV = 4, 6, 6, 2
DIL_PATTERNS = ((128, 1), (512, 4), (2048, 16))
SWA_WINDOW = 128
N_BUCKETS, MAX_REL_DIST = 32, 2048
RMS_EPS = 1e-6
D_QKV = 2560
D_GATES = 3 * D_MODEL
ADAM_LR, ADAM_B1, ADAM_B2, ADAM_EPS, ADAM_WD, ADAM_STEP = 0.001, 0.9, 0.999, 1e-08, 0.01, 10

N_DEV = 8
LANES = 128
NEG = -1e30
SB_TILE = 256
VMEM_LIMIT_BYTES = 48 * 1024 * 1024
HBM = pl.BlockSpec(memory_space=pltpu.HBM)
MESH = pl.DeviceIdType.MESH


def _tile(n, target):
    t = (min(n, target) // LANES) * LANES
    while t >= LANES:
        if n % t == 0:
            return t
        t -= LANES
    return n


def _row_tile(r, cap):
    t = (min(r, cap) // 16) * 16
    while t > 16 and r % t:
        t -= 16
    return t


def _params(semantics=None):
    return pltpu.CompilerParams(dimension_semantics=semantics, vmem_limit_bytes=VMEM_LIMIT_BYTES)


def _dot(a, b, ca, cb):
    return lax.dot_general(a, b, (((ca,), (cb,)), ((), ())), preferred_element_type=F32)


def _sigmoid(a):
    return 1.0 / (1.0 + jnp.exp(-a))


def _row(v):
    return v.reshape(1, -1)


def _all_gather(arrs, name, after=None):
    n = len(arrs)
    ins = list(arrs) + ([] if after is None else [after])

    def body(*refs):
        x_refs, out_refs = refs[:n], refs[len(ins):len(ins) + n]
        send_sems, recv_sems, local_sems = refs[len(ins) + n:]
        x, y, c = lax.axis_index("x"), lax.axis_index("y"), lax.axis_index("c")
        me, sibling = (x, y, c), (x, y, 1 - c)
        chips = [(1 - x, y), (x, 1 - y), (1 - x, 1 - y)]

        def slot(t, px, py, pc):
            return out_refs[t].at[4 * px + 2 * py + pc]

        def copy(t, k, block, to, src=None):
            return pltpu.make_async_remote_copy(
                src_ref=slot(t, *block) if src is None else src, dst_ref=slot(t, *block),
                send_sem=send_sems.at[7 * t + k], recv_sem=recv_sems.at[7 * t + k], device_id=to, device_id_type=MESH)

        mine = [pltpu.make_async_copy(x_refs[t], slot(t, *me), local_sems.at[t]) for t in range(n)]
        for cp in mine:
            cp.start()
        first = []
        for t in range(n):
            first.append(copy(t, 0, me, sibling, src=x_refs[t]))
            first += [copy(t, 1 + j, me, (*chip, c), src=x_refs[t]) for j, chip in enumerate(chips)]
        for cp in first:
            cp.start()
        passed = []
        for j, chip in enumerate(chips):
            for t in range(n):
                copy(t, 1 + j, (*chip, c), me).wait_recv()
                passed.append(copy(t, 4 + j, (*chip, c), sibling))
                passed[-1].start()
        for t in range(n):
            copy(t, 0, sibling, me).wait_recv()
        for j, chip in enumerate(chips):
            for t in range(n):
                copy(t, 4 + j, (*chip, 1 - c), me).wait_recv()
        for cp in first + passed:
            cp.wait_send()
        for cp in mine:
            cp.wait()

    return pl.pallas_call(
        body, name=name, out_shape=[jax.ShapeDtypeStruct((N_DEV,) + a.shape, a.dtype) for a in arrs],
        in_specs=[HBM] * n + [pl.BlockSpec(memory_space=pl.ANY)] * (len(ins) - n), out_specs=[HBM] * n,
        scratch_shapes=[pltpu.SemaphoreType.DMA((7 * n,)), pltpu.SemaphoreType.DMA((7 * n,)), pltpu.SemaphoreType.DMA((n,))],
    )(*ins)


def _direct_copies(x_refs, land_refs, send_sems, recv_sems, local_sems, gather):
    x, y, c = lax.axis_index("x"), lax.axis_index("y"), lax.axis_index("c")
    me = 4 * x + 2 * y + c
    sends, recvs = [], []
    for k in range(1, N_DEV):
        px = 1 - x if (k >> 2) & 1 else x
        py = 1 - y if (k >> 1) & 1 else y
        pc = 1 - c if k & 1 else c
        peer = 4 * px + 2 * py + pc
        for t, (x_ref, land_ref) in enumerate(zip(x_refs, land_refs)):
            sem = 7 * t + k - 1
            for out, src, slot in ((sends, x_ref if gather else x_ref.at[peer], me),
                                   (recvs, x_ref if gather else x_ref.at[me], peer)):
                out.append(pltpu.make_async_remote_copy(
                    src_ref=src, dst_ref=land_ref.at[slot], send_sem=send_sems.at[sem], recv_sem=recv_sems.at[sem],
                    device_id=(px, py, pc), device_id_type=MESH))
    own = [pltpu.make_async_copy(x_ref if gather else x_ref.at[me], land_ref.at[me], local_sems.at[t])
           for t, (x_ref, land_ref) in enumerate(zip(x_refs, land_refs))]
    return sends, recvs, own


SEM =pl.BlockSpec(memory_space=pltpu.SEMAPHORE)
ANY = pl.BlockSpec(memory_space=pl.ANY)
SIDE_EFFECT = pltpu.SideEffectType.DATAFLOW_SIDE_EFFECTING


def _exchange_start(arrs, after, *, gather, name):
    n = len(arrs)
    lands = [lax.empty(((N_DEV,) + a.shape) if gather else a.shape, a.dtype) for a in arrs]

    def body(*refs):
        sends, _, own = _direct_copies(refs[:n], refs[n:2 * n], *refs[2 * n + 1:2 * n + 4], gather)
        for cp in own + sends:
            cp.start()
        refs[-1][...] = jnp.zeros_like(refs[-1])

    ops = [pltpu.with_memory_space_constraint(a, pltpu.HBM) for a in list(arrs) + lands]
    out = pl.pallas_call(
        body, name=name,
        out_shape=(pltpu.SemaphoreType.DMA((7 * n,)), pltpu.SemaphoreType.DMA((7 * n,)), pltpu.SemaphoreType.DMA((n,)),
                   *[pltpu.HBM(a.shape, a.dtype) for a in ops], jax.ShapeDtypeStruct((8, LANES), F32)),
        in_specs=[HBM] * (2 * n) + [ANY],
        out_specs=(SEM, SEM, SEM, *[HBM] * (2 * n), pl.BlockSpec(memory_space=pltpu.VMEM)),
        input_output_aliases={t: 3 + t for t in range(2 * n)},
        compiler_params=pltpu.CompilerParams(has_side_effects=SIDE_EFFECT),
    )(*ops, after)
    return (out[:3], out[3:3 + n], out[3 + n:3 + 2 * n]), out[-1]


def _exchange_wait(state, after, *, gather, name):
    sems, arrs, lands = state
    n = len(arrs)

    def body(*refs):
        sends, recvs, own = _direct_copies(refs[:n], refs[n:2 * n], *refs[2 * n:2 * n + 3], gather)
        for cp in own:
            cp.wait()
        for cp in sends:
            cp.wait_send()
        for cp in recvs:
            cp.wait_recv()

    out = pl.pallas_call(
        body, name=name, out_shape=tuple(pltpu.HBM(a.shape, a.dtype) for a in list(arrs) + list(lands)),
        in_specs=[HBM] * (2 * n) + [SEM, SEM, SEM, ANY], out_specs=tuple([HBM] * (2 * n)),
        input_output_aliases={t: t for t in range(2 * n)},
        compiler_params=pltpu.CompilerParams(has_side_effects=SIDE_EFFECT),
    )(*arrs, *lands, *sems, after)
    return out[n:]


def _sum_parts(groups, name):
    n, r, cdim = groups[0].shape
    tr = _row_tile(r, max(16, (1 << 21) // (n * cdim * groups[0].dtype.itemsize)))
    steps = r // tr

    def body(*refs):
        o_ref = refs[-1]
        gg = pl.program_id(0)
        for gi in range(len(groups)):
            @pl.when(gg == gi)
            def _(gi=gi):
                acc = refs[gi][0].astype(F32)
                for k in range(1, n):
                    acc = acc + refs[gi][k].astype(F32)
                o_ref[...] = acc

    def in_spec(gi):
        return pl.BlockSpec((n, tr, cdim), lambda gg, i: (0, jnp.where(gg == gi, i, 0), 0))

    return pl.pallas_call(
        body, name=name, out_shape=jax.ShapeDtypeStruct((len(groups) * r, cdim), F32), grid=(len(groups), steps),
        in_specs=[in_spec(gi) for gi in range(len(groups))],
        out_specs=pl.BlockSpec((tr, cdim), lambda gg, i: (gg * steps + i, 0)),
        compiler_params=_params(("parallel", "parallel")),
    )(*groups)


def _mm_tn(a, b, *, name, after=None, tm=512, tn=1024):
    k, m = a.shape
    n = b.shape[1]
    tm, tn = _tile(m, tm), _tile(n, tn)

    def body(a_ref, b_ref, *rest):
        o_ref, at_ref = rest[-2], rest[-1]

        @pl.when(pl.program_id(1) == 0)
        def _():
            at_ref[...] = a_ref[...].astype(BF16).T

        o_ref[...] = _dot(at_ref[...], b_ref[...].astype(BF16), 1, 0).astype(BF16)

    ins = [a, b] + ([] if after is None else [after])
    return pl.pallas_call(
        body, name=name, out_shape=jax.ShapeDtypeStruct((m, n), BF16), grid=(m // tm, n // tn),
        in_specs=[pl.BlockSpec((k, tm), lambda i, j: (0, i)), pl.BlockSpec((k, tn), lambda i, j: (0, j))] + [ANY] * (len(ins) - 2),
        out_specs=pl.BlockSpec((tm, tn), lambda i, j: (i, j)),
        scratch_shapes=[pltpu.VMEM((tm, k), BF16)], compiler_params=_params(("parallel", "arbitrary")),
    )(*ins)


def _mm2(a1, b1, a2, b2, *, name, after=None, tm=256, tn=1024):
    m = a1.shape[0]
    n = b1.shape[1]
    tm, tn = _tile(m, tm), _tile(n, tn)

    def body(a1_ref, b1_ref, a2_ref, b2_ref, *rest):
        rest[-1][...] = (_dot(a1_ref[...].astype(BF16), b1_ref[...], 1, 0)
                         + _dot(a2_ref[...].astype(BF16), b2_ref[...], 1, 0))

    ins = [a1, b1, a2, b2] + ([] if after is None else [after])

    def a_spec(t):
        return pl.BlockSpec((tm, t.shape[1]), lambda i, j: (i, 0))

    def b_spec(t):
        return pl.BlockSpec((t.shape[0], tn), lambda i, j: (0, j))

    return pl.pallas_call(
        body, name=name, out_shape=jax.ShapeDtypeStruct((m, n), F32), grid=(m // tm, n // tn),
        in_specs=[a_spec(a1), b_spec(b1), a_spec(a2), b_spec(b2)] + [ANY] * (len(ins) - 4),
        out_specs=pl.BlockSpec((tm, tn), lambda i, j: (i, j)), compiler_params=_params(("parallel", "parallel")),
    )(*ins)


def _mm(a, b, *, name, ta=False, tb=False, res=None, colscale=None, emit_acc=False,
        out_dtype=F32, tm=512, tn=512):
    m, k = (a.shape[1], a.shape[0]) if ta else a.shape
    n = b.shape[0] if tb else b.shape[1]
    tm, tn = _tile(m, tm), _tile(n, tn)
    ca, cb = (0 if ta else 1), (1 if tb else 0)
    a_spec = pl.BlockSpec((k, tm), lambda i, j: (0, i)) if ta else pl.BlockSpec((tm, k), lambda i, j: (i, 0))
    b_spec = pl.BlockSpec((tn, k), lambda i, j: (j, 0)) if tb else pl.BlockSpec((k, tn), lambda i, j: (0, j))
    tile = pl.BlockSpec((tm, tn), lambda i, j: (i, j))
    ins, in_specs = [a, b], [a_spec, b_spec]
    if res is not None:
        ins.append(res)
        in_specs.append(tile)
    if colscale is not None:
        ins.append(colscale)
        in_specs.append(pl.BlockSpec((1, tn), lambda i, j: (0, j)))
    n_in = len(ins)

    def body(*refs):
        outs = refs[n_in:]
        acc = _dot(refs[0][...].astype(BF16), refs[1][...].astype(BF16), ca, cb)
        val, p = acc, 2
        if res is not None:
            r_val, p = refs[p][...], p + 1
        if colscale is not None:
            val = val * refs[p][...]
        if res is not None:
            val = r_val + val
        if emit_acc:
            outs[0][...] = acc
        outs[-1][...] = val.astype(out_dtype)

    out_shape = [jax.ShapeDtypeStruct((m, n), out_dtype)]
    out_specs = [tile]
    if emit_acc:
        out_shape.insert(0, jax.ShapeDtypeStruct((m, n), F32))
        out_specs.insert(0, tile)
    out = pl.pallas_call(
        body, name=name, out_shape=out_shape, grid=(m // tm, n // tn), in_specs=in_specs, out_specs=out_specs,
        compiler_params=_params(("parallel", "parallel")),
    )(*ins)
    return out if emit_acc else out[0]


def _norm_fwd(x, g, scale, shift, name, after=None):
    s, d = x.shape
    tr = 256

    def body(x_ref, g_ref, sc_ref, sh_ref, *rest):
        xv = x_ref[...]
        rstd = lax.rsqrt(jnp.mean(xv * xv, axis=-1, keepdims=True) + RMS_EPS)
        rest[-1][...] = (xv * rstd * g_ref[...] * (1.0 + sc_ref[...]) + sh_ref[...]).astype(BF16)

    rowspec = pl.BlockSpec((1, d), lambda i: (0, 0))
    ins = [x, g, scale, shift] + ([] if after is None else [after])
    return pl.pallas_call(
        body, name=name, out_shape=jax.ShapeDtypeStruct((s, d), BF16), grid=(s // tr,),
        in_specs=[pl.BlockSpec((tr, d), lambda i: (i, 0)), rowspec, rowspec, rowspec] + [ANY] * (len(ins) - 4),
        out_specs=pl.BlockSpec((tr, d), lambda i: (i, 0)),
        compiler_params=_params(("parallel",)),
    )(*ins)


def _norm_bwd(x, dh, dres, g, scale, name):
    s, d = x.shape
    tr = 256

    def body(x_ref, dh_ref, dr_ref, g_ref, sc_ref, dx_ref, a_ref, b_ref):
        @pl.when(pl.program_id(0) == 0)
        def _():
            a_ref[...] = jnp.zeros_like(a_ref)
            b_ref[...] = jnp.zeros_like(b_ref)

        xv = x_ref[...]
        rstd = lax.rsqrt(jnp.mean(xv * xv, axis=-1, keepdims=True) + RMS_EPS)
        xhat = xv * rstd
        dhv = dh_ref[...]
        dxhat = dhv * (g_ref[...] * (1.0 + sc_ref[...]))
        mean_term = jnp.mean(dxhat * xhat, axis=-1, keepdims=True)
        dx_ref[...] = dr_ref[...] + rstd * (dxhat - xhat * mean_term)
        a_ref[...] += jnp.sum(dhv, axis=0, keepdims=True)
        b_ref[...] += jnp.sum(dhv * xhat, axis=0, keepdims=True)

    rowspec = pl.BlockSpec((1, d), lambda i: (0, 0))
    tile = pl.BlockSpec((tr, d), lambda i: (i, 0))
    return pl.pallas_call(
        body, name=name,
        out_shape=[jax.ShapeDtypeStruct((s, d), F32), jax.ShapeDtypeStruct((1, d), F32), jax.ShapeDtypeStruct((1, d), F32)],
        grid=(s // tr,), in_specs=[tile, tile, tile, rowspec, rowspec], out_specs=[tile, rowspec, rowspec],
        compiler_params=_params(("arbitrary",)),
    )(x, dh, dres, g, scale)


def _gate_bwd(dxn, f, colscale, coef, name):
    s, d = dxn.shape
    tr = 256

    def body(dx_ref, f_ref, cs_ref, df_ref, dg_ref):
        @pl.when(pl.program_id(0) == 0)
        def _():
            dg_ref[...] = jnp.zeros_like(dg_ref)

        dxv = dx_ref[...]
        df_ref[...] = (dxv * cs_ref[...]).astype(BF16)
        dg_ref[...] += coef * jnp.sum(dxv * f_ref[...], axis=0, keepdims=True)

    rowspec = pl.BlockSpec((1, d), lambda i: (0, 0))
    tile = pl.BlockSpec((tr, d), lambda i: (i, 0))
    return pl.pallas_call(
        body, name=name, out_shape=[jax.ShapeDtypeStruct((s, d), BF16), jax.ShapeDtypeStruct((1, d), F32)],
        grid=(s // tr,), in_specs=[tile, tile, rowspec], out_specs=[tile, rowspec],
        compiler_params=_params(("arbitrary",)),
    )(dxn, f, colscale)


def _ffn_up(h, wg, wu, name, tm=SEQ, tn=256):
    s, d = h.shape
    f = wg.shape[0]

    def body(h_ref, wg_ref, wu_ref, a_ref, u_ref, s_ref):
        hv = h_ref[...]
        a = _dot(hv, wg_ref[...], 1, 1)
        u = _dot(hv, wu_ref[...], 1, 1)
        a_ref[...] = a.astype(BF16)
        u_ref[...] = u.astype(BF16)
        s_ref[...] = (a * _sigmoid(a) * u).astype(BF16)

    tile = pl.BlockSpec((tm, tn), lambda i, j: (i, j))
    wspec = pl.BlockSpec((tn, d), lambda i, j: (j, 0))
    return pl.pallas_call(
        body, name=name,
        out_shape=[jax.ShapeDtypeStruct((s, f), BF16), jax.ShapeDtypeStruct((s, f), BF16), jax.ShapeDtypeStruct((s, f), BF16)],
        grid=(s // tm, f // tn), in_specs=[pl.BlockSpec((tm, d), lambda i, j: (i, 0)), wspec, wspec],
        out_specs=[tile, tile, tile], compiler_params=_params(("parallel", "parallel")),
    )(h, wg, wu)


def _ffn_bwd_ds(df, wd, a, u, name, tm=SEQ, tn=256):
    s, d = df.shape
    f = wd.shape[0]

    def body(df_ref, wd_ref, a_ref, u_ref, da_ref, du_ref):
        ds = _dot(df_ref[...], wd_ref[...], 1, 1)
        av = a_ref[...].astype(F32)
        sg = _sigmoid(av)
        da_ref[...] = (ds * u_ref[...].astype(F32) * (sg * (1.0 + av * (1.0 - sg)))).astype(BF16)
        du_ref[...] = (ds * (av * sg)).astype(BF16)

    tile = pl.BlockSpec((tm, tn), lambda i, j: (i, j))
    return pl.pallas_call(
        body, name=name, out_shape=[jax.ShapeDtypeStruct((s, f), BF16), jax.ShapeDtypeStruct((s, f), BF16)],
        grid=(s // tm, f // tn),
        in_specs=[pl.BlockSpec((tm, d), lambda i, j: (i, 0)), pl.BlockSpec((tn, d), lambda i, j: (j, 0)), tile, tile],
        out_specs=[tile, tile], compiler_params=_params(("parallel", "parallel")),
    )(df, wd, a, u)


def _merge_fwd(o_sb, o_dil, o_swa, gates, wb_sb, wb_dil, wb_swa, name):
    s, d = SEQ, D_MODEL
    tm = 256

    def body(osb_ref, odl_ref, osw_ref, g_ref, wsb_ref, wdl_ref, wsw_ref, m_ref, tsb_ref, tdl_ref, tsw_ref):
        for h in range(osb_ref.shape[0]):
            tsb_ref[:, h * HEAD_DIM:(h + 1) * HEAD_DIM] = osb_ref[h].astype(BF16)
        for h in range(osw_ref.shape[0]):
            tsw_ref[:, h * HEAD_DIM:(h + 1) * HEAD_DIM] = osw_ref[h].astype(BF16)
        tdl_ref[...] = odl_ref[...].astype(BF16)
        acc = _sigmoid(g_ref[:, 0:d]) * _dot(tsb_ref[...], wsb_ref[...], 1, 0)
        acc += _sigmoid(g_ref[:, d:2 * d]) * _dot(tdl_ref[...], wdl_ref[...], 1, 0)
        acc += _sigmoid(g_ref[:, 2 * d:3 * d]) * _dot(tsw_ref[...], wsw_ref[...], 1, 0)
        m_ref[...] = acc.astype(BF16)

    def rows(w):
        return pl.BlockSpec((tm, w), lambda i: (i, 0))

    def heads(n):
        return pl.BlockSpec((n, tm, HEAD_DIM), lambda i: (0, i, 0))

    def whole(w):
        return pl.BlockSpec((w, d), lambda i: (0, 0))

    return pl.pallas_call(
        body, name=name, out_shape=[jax.ShapeDtypeStruct((s, w), BF16) for w in (d, 256, 128, 384)], grid=(s // tm,),
        in_specs=[heads(H_SB), rows(128), heads(H_SWA_Q), rows(3 * d), whole(256), whole(128), whole(384)],
        out_specs=[rows(d), rows(256), rows(128), rows(384)], compiler_params=_params(("parallel",)),
    )(o_sb, o_dil, o_swa, gates, wb_sb, wb_dil, wb_swa)


def _merge_bwd(dmerged, t_sb, t_dil, t_swa, gates, wb_sb, wb_dil, wb_swa, name):
    s, d = SEQ, D_MODEL
    tm = 256

    def body(dm_ref, tsb_ref, tdl_ref, tsw_ref, g_ref, wsb_ref, wdl_ref, wsw_ref,
             dg_ref, dosb_ref, dodl_ref, dosw_ref, dbsb_ref, dbdl_ref, dbsw_ref):
        dm = dm_ref[...]
        for idx, (t_ref, w_ref, do_ref, db_ref) in enumerate((
                (tsb_ref, wsb_ref, dosb_ref, dbsb_ref), (tdl_ref, wdl_ref, dodl_ref, dbdl_ref),
                (tsw_ref, wsw_ref, dosw_ref, dbsw_ref))):
            w = w_ref[...]
            br = _dot(t_ref[...], w, 1, 0)
            sg = _sigmoid(g_ref[:, idx * d:(idx + 1) * d])
            dbr = (dm * sg).astype(BF16)
            dg_ref[:, idx * d:(idx + 1) * d] = (dm * br * (sg * (1.0 - sg))).astype(BF16)
            db_ref[...] = dbr
            do = _dot(dbr, w, 1, 1)
            if len(do_ref.shape) == 2:
                do_ref[...] = do
            else:
                for h in range(do_ref.shape[0]):
                    do_ref[h] = do[:, h * HEAD_DIM:(h + 1) * HEAD_DIM]

    def rows(w):
        return pl.BlockSpec((tm, w), lambda i: (i, 0))

    def heads(n):
        return pl.BlockSpec((n, tm, HEAD_DIM), lambda i: (0, i, 0))

    def whole(w):
        return pl.BlockSpec((w, d), lambda i: (0, 0))

    def shp(w, dt):
        return jax.ShapeDtypeStruct((s, w), dt)

    def hshp(n):
        return jax.ShapeDtypeStruct((n, s, HEAD_DIM), F32)

    return pl.pallas_call(
        body, name=name,
        out_shape=[shp(3 * d, BF16), hshp(H_SB), shp(128, F32), hshp(H_SWA_Q), shp(d, BF16), shp(d, BF16), shp(d, BF16)],
        grid=(s // tm,),
        in_specs=[rows(d), rows(256), rows(128), rows(384), rows(3 * d), whole(256), whole(128), whole(384)],
        out_specs=[rows(3 * d), heads(H_SB), rows(128), heads(H_SWA_Q), rows(d), rows(d), rows(d)],
        compiler_params=_params(("parallel",)),
    )(dmerged, t_sb, t_dil, t_swa, gates, wb_sb, wb_dil, wb_swa)


def _final_loss(x, target, g, name):
    s, d = x.shape
    tr = 256

    def body(x_ref, t_ref, g_ref, loss_ref, dx_ref, dg_ref):
        @pl.when(pl.program_id(0) == 0)
        def _():
            loss_ref[...] = jnp.zeros_like(loss_ref)
            dg_ref[...] = jnp.zeros_like(dg_ref)

        xv = x_ref[...]
        gv = g_ref[...]
        rstd = lax.rsqrt(jnp.mean(xv * xv, axis=-1, keepdims=True) + RMS_EPS)
        xhat = xv * rstd
        err = xhat * gv - t_ref[...]
        loss_ref[...] += 0.5 * jnp.sum(jnp.mean(err * err, axis=-1, keepdims=True))
        dy = err * (1.0 / d)
        dxhat = dy * gv
        mean_term = jnp.mean(dxhat * xhat, axis=-1, keepdims=True)
        dx_ref[...] = rstd * (dxhat - xhat * mean_term)
        dg_ref[...] += jnp.sum(dy * xhat, axis=0, keepdims=True)

    rowspec = pl.BlockSpec((1, d), lambda i: (0, 0))
    tile = pl.BlockSpec((tr, d), lambda i: (i, 0))
    return pl.pallas_call(
        body, name=name,
        out_shape=[jax.ShapeDtypeStruct((1, LANES), F32), jax.ShapeDtypeStruct((s, d), F32), jax.ShapeDtypeStruct((1, d), F32)],
        grid=(s // tr,), in_specs=[tile, tile, rowspec],
        out_specs=[pl.BlockSpec((1, LANES), lambda i: (0, 0)), tile, rowspec],
        compiler_params=_params(("arbitrary",)),
    )(x, target, g)


def _adamw(w, g, m, v, name):
    shape = w.shape
    cols = shape[-1]
    rows = int(np.prod(shape[:-1])) if len(shape) > 1 else 1
    tr = rows
    for cand in (1024, 512, 256, 128, 64, 32, 16, 8):
        if rows % cand == 0 and rows > cand and cand * cols * 4 <= (1 << 21):
            tr = cand
            break

    def body(w_ref, g_ref, m_ref, v_ref, d_ref, nm_ref, nv_ref):
        d_ref[...], nm_ref[...], nv_ref[...] = _adam_update(w_ref[...], g_ref[...], m_ref[...], v_ref[...])

    tile = pl.BlockSpec((tr, cols), lambda i: (i, 0))
    flat = [t.reshape(rows, cols) for t in (w, g, m, v)]
    out = pl.pallas_call(
        body, name=name, out_shape=[jax.ShapeDtypeStruct((rows, cols), F32)] * 3, grid=(rows // tr,),
        in_specs=[tile] * 4, out_specs=[tile] * 3, compiler_params=_params(("parallel",)),
    )(*flat)
    return tuple(t.reshape(shape) for t in out)


def _adam_update(w, gv, m, v):
    nm = ADAM_B1 * m + (1.0 - ADAM_B1) * gv
    nv = ADAM_B2 * v + (1.0 - ADAM_B2) * (gv * gv)
    m_hat = nm / (1.0 - ADAM_B1 ** ADAM_STEP)
    v_hat = nv / (1.0 - ADAM_B2 ** ADAM_STEP)
    return -ADAM_LR * (m_hat / (jnp.sqrt(v_hat) + ADAM_EPS) + ADAM_WD * w), nm, nv


def _reduce_adamw(groups, w, m, v, row0, prev, name):
    n, r, cdim = groups[0].shape
    rows = w.shape[0]
    tr = _row_tile(r, max(16, (1 << 22) // (n * cdim * groups[0].dtype.itemsize)))
    steps = r // tr
    ng = len(groups)

    def body(*refs):
        w_ref, m_ref, v_ref = refs[ng:ng + 3]
        g_out, d_out, m_out, v_out = refs[-4:]
        gg = pl.program_id(0)
        for gi in range(ng):
            @pl.when(gg == gi)
            def _(gi=gi):
                acc = refs[gi][0].astype(F32)
                for k in range(1, n):
                    acc = acc + refs[gi][k].astype(F32)
                g_out[...] = acc
                d_out[...], m_out[...], v_out[...] = _adam_update(w_ref[...], acc, m_ref[...], v_ref[...])

    def part_spec(gi):
        return pl.BlockSpec((n, tr, cdim), lambda gg, i: (0, jnp.where(gg == gi, i, 0), 0))

    tile = pl.BlockSpec((tr, cdim), lambda gg, i: (row0 // tr + gg * steps + i, 0))
    extra = [] if prev is None else list(prev)
    return pl.pallas_call(
        body, name=name, out_shape=[jax.ShapeDtypeStruct((rows, cdim), F32)] * 4, grid=(ng, steps),
        in_specs=[part_spec(gi) for gi in range(ng)] + [tile] * 3 + [ANY] * len(extra), out_specs=[tile] * 4,
        input_output_aliases={ng + 3 + k: k for k in range(len(extra))},
        compiler_params=_params(("parallel", "parallel")),
    )(*groups, w, m, v, *extra)


def _ada_fwd(c_all, w, name):
    n = w.shape[1]

    def body(c_ref, w_ref, o_ref):
        cv = c_ref[...]
        o_ref[...] = jnp.dot(cv * _sigmoid(cv), w_ref[...], preferred_element_type=F32, precision=lax.Precision.HIGHEST)

    return pl.pallas_call(body, name=name, out_shape=jax.ShapeDtypeStruct((N_DEV, n), F32), compiler_params=_params())(c_all, w)


def _ada_bwd(c_all_t, dmod, name):
    n = dmod.shape[1]

    def body(c_ref, d_ref, o_ref):
        cv = c_ref[...]
        o_ref[...] = jnp.dot(cv * _sigmoid(cv), d_ref[...], preferred_element_type=F32, precision=lax.Precision.HIGHEST)

    return pl.pallas_call(body, name=name, out_shape=jax.ShapeDtypeStruct((D_MODEL, n), F32), compiler_params=_params())(c_all_t, dmod)


def _bucket_tables():
    rel = np.arange(BLK)[:, None] + BLK - np.arange(2 * BLK)[None, :]
    max_exact = N_BUCKETS // 2

    def bucket(n):
        nf = np.maximum(n, 1).astype(np.float32)
        large = max_exact + (np.log(nf / np.float32(max_exact)) / np.float32(math.log(MAX_REL_DIST / max_exact))
                             * np.float32(N_BUCKETS - max_exact)).astype(np.int32)
        return np.where(n < max_exact, n, np.minimum(large, N_BUCKETS - 1))

    tabs = []
    for dil, max_dist in ((1, 128), (4, 128), (16, 128), (1, SWA_WINDOW - 1)):
        in_band = (rel >= 0) & (rel <= max_dist)
        tabs.append(np.where(in_band, bucket(np.maximum(rel, 0) * dil), -1))
    return np.stack(tabs).astype(np.int32)


N_SOFT = H_DIL + H_SWA_Q


def _table_of_head(h):
    return jnp.minimum(h // 2, 3)


def _bias_build(rel_bias, tables, name):
    def body(rel_ref, t_ref, o_ref):
        h = pl.program_id(0)
        tb = t_ref[0]
        out = jnp.full((BLK, 2 * BLK), NEG, F32)
        for b in range(N_BUCKETS):
            out = jnp.where(tb == b, rel_ref[b, h], out)
        o_ref[0] = out

    return pl.pallas_call(
        body, name=name, out_shape=jax.ShapeDtypeStruct((N_SOFT, BLK, 2 * BLK), F32), grid=(N_SOFT,),
        in_specs=[pl.BlockSpec(memory_space=pltpu.SMEM),
                  pl.BlockSpec((1, BLK, 2 * BLK), lambda h: (_table_of_head(h), 0, 0))],
        out_specs=pl.BlockSpec((1, BLK, 2 * BLK), lambda h: (h, 0, 0)),
        compiler_params=_params(("parallel",)),
    )(rel_bias, tables)


def _bias_grad(dbias, tables, name):
    def body(d_ref, t_ref, o_ref):
        tb = t_ref[0]
        dv = d_ref[0]
        lane = lax.broadcasted_iota(jnp.int32, (1, LANES), 1)
        out = jnp.zeros((1, LANES), F32)
        for b in range(N_BUCKETS):
            out = jnp.where(lane == b, jnp.sum(jnp.where(tb == b, dv, 0.0)), out)
        o_ref[0] = out

    return pl.pallas_call(
        body, name=name, out_shape=jax.ShapeDtypeStruct((N_SOFT, 1, LANES), F32), grid=(N_SOFT,),
        in_specs=[pl.BlockSpec((1, BLK, 2 * BLK), lambda h: (h, 0, 0)),
                  pl.BlockSpec((1, BLK, 2 * BLK), lambda h: (_table_of_head(h), 0, 0))],
        out_specs=pl.BlockSpec((1, 1, LANES), lambda h: (h, 0, 0)),
        compiler_params=_params(("parallel",)),
    )(dbias, tables)


def _band_layout(g, bias_div):
    assert g == 1 or bias_div == 1
    return bias_div if g == 1 else 1


def _band_specs(length, g, bias_div, offs):
    ns = _band_layout(g, bias_div)

    def seqs(off, div=1):
        return pl.BlockSpec((ns, length, HEAD_DIM), lambda s: (off // ns + s // div, 0, 0))

    xspecs = [seqs(offs[0]), seqs(offs[1], g), seqs(offs[2], g)]
    bspec = pl.BlockSpec((1, BLK, 2 * BLK), lambda s: (s, 0, 0))
    sspec = pl.BlockSpec((ns, 1, LANES), lambda s: (s, 0, 0))
    colspec = pl.BlockSpec((ns, length, 1), lambda s: (s, 0, 0))
    return xspecs, seqs(0), seqs(0, g), bspec, sspec, colspec


def _band_sweep(length, ns, one):
    nblk = length // BLK
    for qq in range(ns):
        if ns * nblk <= 16:
            for i in range(nblk):
                one(qq, i * BLK, max(i - 1, 0) * BLK, i == 0)
        else:
            def step(i, carry, qq=qq):
                one(qq, pl.multiple_of(i * BLK, BLK), pl.multiple_of(jnp.maximum(i - 1, 0) * BLK, BLK), i == 0)
                return carry

            lax.fori_loop(0, nblk, step, 0, unroll=2)


def _band_scores(q_ref, k_ref, b_ref, qq, kq, bq, cur, prv, first):
    qv = q_ref[qq, pl.ds(cur, BLK), :]
    bv = b_ref[bq]
    if first is True:
        sp = jnp.full((BLK, BLK), NEG, F32)
    else:
        sp = _dot(qv, k_ref[kq, pl.ds(prv, BLK), :], 1, 1) + bv[:, :BLK]
        sp = sp if first is False else jnp.where(first, NEG, sp)
    sc = _dot(qv, k_ref[kq, pl.ds(cur, BLK), :], 1, 1) + bv[:, BLK:]
    return qv, sp, sc


def _band_fwd(x, bias, sink, *, nq, offs, g, bias_div, has_sink, name):
    length = x.shape[1]
    ns = _band_layout(g, bias_div)

    def body(q_ref, k_ref, v_ref, b_ref, s_ref, o_ref, lse_ref):
        def one(qq, cur, prv, first):
            kq, bq = qq, 0
            _, sp, sc = _band_scores(q_ref, k_ref, b_ref, qq, kq, bq, cur, prv, first)
            m = jnp.maximum(jnp.max(sp, axis=1, keepdims=True), jnp.max(sc, axis=1, keepdims=True))
            if has_sink:
                sk = s_ref[qq][:, :1]
                m = jnp.maximum(m, sk)
            pp, pc = jnp.exp(sp - m), jnp.exp(sc - m)
            den = jnp.sum(pp, axis=1, keepdims=True) + jnp.sum(pc, axis=1, keepdims=True)
            if has_sink:
                den = den + jnp.exp(sk - m)
            acc = (_dot(pp.astype(BF16), v_ref[kq, pl.ds(prv, BLK), :], 1, 0)
                   + _dot(pc.astype(BF16), v_ref[kq, pl.ds(cur, BLK), :], 1, 0))
            o_ref[qq, pl.ds(cur, BLK), :] = acc / den
            lse_ref[qq, pl.ds(cur, BLK), :] = m + jnp.log(den)

        _band_sweep(length, ns, one)

    xspecs, qspec, _, bspec, sspec, colspec = _band_specs(length, g, bias_div, offs)
    return pl.pallas_call(
        body, name=name,
        out_shape=[jax.ShapeDtypeStruct((nq, length, HEAD_DIM), F32), jax.ShapeDtypeStruct((nq, length, 1), F32)],
        grid=(nq // ns,), in_specs=xspecs + [bspec, sspec],
        out_specs=[qspec, colspec], compiler_params=_params(("parallel",)),
    )(x, x, x, bias, sink)


def _band_bwd(x, bias, sink, o, lse, do, dlse, *, nq, offs, g, bias_div, has_sink, name):
    length = x.shape[1]
    ns = _band_layout(g, bias_div)
    nk, nbias = nq // g, nq // bias_div

    def body(q_ref, k_ref, v_ref, b_ref, s_ref, o_ref, lse_ref, do_ref, dlse_ref,
             dq_ref, dk_ref, dv_ref, db_ref, dsk_ref, dkp_ref, dvp_ref):
        for ref in (db_ref, dsk_ref, dkp_ref, dvp_ref):
            ref[...] = jnp.zeros_like(ref)

        @pl.when(pl.program_id(0) % g == 0)
        def _():
            dk_ref[...] = jnp.zeros_like(dk_ref)
            dv_ref[...] = jnp.zeros_like(dv_ref)

        def one(qq, cur, prv, first):
            kq, bq = qq, 0
            qv, sp, sc = _band_scores(q_ref, k_ref, b_ref, qq, kq, bq, cur, prv, first)
            rows, prow = pl.ds(cur, BLK), pl.ds(prv, BLK)
            lse_v = lse_ref[qq, rows, :]
            pp, pc = jnp.exp(sp - lse_v), jnp.exp(sc - lse_v)
            dov = do_ref[qq, rows, :]
            dob = dov.astype(BF16)
            coef = dlse_ref[qq, rows, :] - jnp.sum(dov * o_ref[qq, rows, :], axis=1, keepdims=True)
            dsp = pp * (_dot(dob, v_ref[kq, prow, :], 1, 1) + coef)
            dsc = pc * (_dot(dob, v_ref[kq, rows, :], 1, 1) + coef)
            dspb, dscb = dsp.astype(BF16), dsc.astype(BF16)
            dq_ref[qq, rows, :] = ((_dot(dspb, k_ref[kq, prow, :], 1, 0) + _dot(dscb, k_ref[kq, rows, :], 1, 0))
                                   * (HEAD_DIM ** -0.5))
            dk_ref[kq, rows, :] += _dot(dscb, qv, 0, 0)
            dkp_ref[kq, prow, :] += _dot(dspb, qv, 0, 0)
            dv_ref[kq, rows, :] += _dot(pc.astype(BF16), dob, 0, 0)
            dvp_ref[kq, prow, :] += _dot(pp.astype(BF16), dob, 0, 0)
            db_ref[bq, :, :BLK] += dsp
            db_ref[bq, :, BLK:] += dsc
            if has_sink:
                dsk_ref[qq] += jnp.sum(jnp.exp(s_ref[qq][:, :1] - lse_v) * coef)

        _band_sweep(length, ns, one)
        dk_ref[...] += dkp_ref[...]
        dv_ref[...] += dvp_ref[...]

    xspecs, qspec, kvspec, bspec, sspec, colspec = _band_specs(length, g, bias_div, offs)
    return pl.pallas_call(
        body, name=name,
        out_shape=[jax.ShapeDtypeStruct((nq, length, HEAD_DIM), F32), jax.ShapeDtypeStruct((nk, length, HEAD_DIM), F32),
                   jax.ShapeDtypeStruct((nk, length, HEAD_DIM), F32), jax.ShapeDtypeStruct((nbias, BLK, 2 * BLK), F32),
                   jax.ShapeDtypeStruct((nq, 1, LANES), F32)],
        grid=(nq // ns,),
        in_specs=xspecs + [bspec, sspec, qspec, colspec, qspec, colspec],
        out_specs=[qspec, kvspec, kvspec, bspec, sspec],
        scratch_shapes=[pltpu.VMEM((ns, length, HEAD_DIM), F32), pltpu.VMEM((ns, length, HEAD_DIM), F32)],
        compiler_params=_params(("arbitrary",)),
    )(x, x, x, bias, sink, o, lse, do, dlse)


TOK_TILE = 512


def _dil_merge(outs, lses, dout, name):
    tr = TOK_TILE
    dils = [d for _, d in DIL_PATTERNS]
    n = len(dils)
    o4 = [o.reshape(2, d, SEQ // d, HEAD_DIM) for o, d in zip(outs, dils)]
    l4 = [l.reshape(2, d, SEQ // d, 1) for l, d in zip(lses, dils)]
    o_specs = [pl.BlockSpec((2, d, tr // d, HEAD_DIM), lambda i: (0, 0, i, 0)) for d in dils]
    l_specs = [pl.BlockSpec((2, d, tr // d, 1), lambda i: (0, 0, i, 0)) for d in dils]
    tok = pl.BlockSpec((tr, 2 * HEAD_DIM), lambda i: (i, 0))
    scratch = ([pltpu.VMEM((tr, 2 * HEAD_DIM), F32) for _ in dils] + [pltpu.VMEM((tr, 1), F32) for _ in range(2 * n)]
               + [pltpu.VMEM((tr // d, 2 * HEAD_DIM), F32) for d in dils])

    def to_tokens(o_ref, l_ref, d, pair, cols, stage):
        for r in range(d):
            rows = pl.ds(r, tr // d, stride=d) if d > 1 else slice(None)
            stage[:, :HEAD_DIM] = o_ref[0, r]
            stage[:, HEAD_DIM:] = o_ref[1, r]
            pair[rows, :] = stage[...]
            for h in range(2):
                cols[h][rows, :] = l_ref[h, r]
        return pair[...], [cols[0][...], cols[1][...]]

    def weights(ls):
        left = lax.broadcasted_iota(jnp.int32, (tr, 2 * HEAD_DIM), 1) < HEAD_DIM
        per_head = []
        for h in range(2):
            m = ls[0][h]
            for g in range(1, n):
                m = jnp.maximum(m, ls[g][h])
            es = [jnp.exp(ls[g][h] - m) for g in range(n)]
            den = es[0]
            for e in es[1:]:
                den = den + e
            per_head.append([e / den for e in es])
        return per_head, [jnp.where(left, per_head[0][g], per_head[1][g]) for g in range(n)], left

    def load(refs):
        pairs, cols, stages = refs[:n], refs[n:3 * n], refs[3 * n:]
        return pairs, [cols[2 * g:2 * g + 2] for g in range(n)], stages

    if dout is None:
        def body(*refs):
            pairs, cols, stages = load(refs[2 * n + 1:])
            toks = [to_tokens(refs[g], refs[n + g], dils[g], pairs[g], cols[g], stages[g]) for g in range(n)]
            _, alphas, _ = weights([t[1] for t in toks])
            acc = alphas[0] * toks[0][0]
            for g in range(1, n):
                acc = acc + alphas[g] * toks[g][0]
            refs[2 * n][...] = acc

        return pl.pallas_call(
            body, name=name, out_shape=jax.ShapeDtypeStruct((SEQ, 2 * HEAD_DIM), F32), grid=(SEQ // tr,),
            in_specs=o_specs + l_specs, out_specs=tok, scratch_shapes=scratch, compiler_params=_params(("parallel",)),
        )(*o4, *l4)

    def body(*refs):
        do_refs, dl_refs = refs[2 * n + 1:3 * n + 1], refs[3 * n + 1:4 * n + 1]
        pairs, cols, stages = load(refs[4 * n + 1:])
        toks = [to_tokens(refs[g], refs[n + g], dils[g], pairs[g], cols[g], stages[g]) for g in range(n)]
        per_head, alphas, left = weights([t[1] for t in toks])
        dov = refs[2 * n][...]
        das = []
        for g in range(n):
            prod = dov * toks[g][0]
            das.append([jnp.sum(jnp.where(left, prod, 0.0), axis=1, keepdims=True),
                        jnp.sum(jnp.where(left, 0.0, prod), axis=1, keepdims=True)])
        dbar = [sum(per_head[h][g] * das[g][h] for g in range(n)) for h in range(2)]
        for g, d in enumerate(dils):
            pairs[g][...] = alphas[g] * dov
            for h in range(2):
                cols[g][h][...] = per_head[h][g] * (das[g][h] - dbar[h])
            for r in range(d):
                rows = pl.ds(r, tr // d, stride=d) if d > 1 else slice(None)
                v = pairs[g][rows, :]
                for h in range(2):
                    do_refs[g][h, r] = v[:, h * HEAD_DIM:(h + 1) * HEAD_DIM]
                    dl_refs[g][h, r] = cols[g][h][rows, :]

    out = pl.pallas_call(
        body, name=name,
        out_shape=[jax.ShapeDtypeStruct(o.shape, F32) for o in o4] + [jax.ShapeDtypeStruct(l.shape, F32) for l in l4],
        grid=(SEQ // tr,), in_specs=o_specs + l_specs + [tok], out_specs=o_specs + l_specs, scratch_shapes=scratch,
        compiler_params=_params(("parallel",)),
    )(*o4, *l4, dout)
    return [t.reshape(s.shape) for t, s in zip(out, list(outs) + list(lses))]


def _tri(cmp):
    r = lax.broadcasted_iota(jnp.int32, (SB_TILE, SB_TILE), 0)
    c = lax.broadcasted_iota(jnp.int32, (SB_TILE, SB_TILE), 1)
    return cmp(r, c).astype(BF16)


def _cum(x, tri, terms):
    acc, rest = None, x
    for _ in range(terms):
        part = rest.astype(BF16)
        rest = rest - part.astype(F32)
        d = _dot(part, tri, 1, 0)
        acc = d if acc is None else acc + d
    return acc


def _sb_logits(q, ks, diagonal):
    t = SB_TILE
    z = _dot(q, ks, 1, 1)
    e = jnp.exp(-jnp.abs(z))
    lf = -(jnp.maximum(z, 0.0) + jnp.log(1.0 + e))
    if not diagonal:
        return z, e, lf, None
    mask = lax.broadcasted_iota(jnp.int32, (t, t), 1) < lax.broadcasted_iota(jnp.int32, (t, t), 0)
    return z, e, jnp.where(mask, lf, 0.0), mask


def _sb_specs(h, s):
    t = SB_TILE
    tile = pl.BlockSpec((h, t, HEAD_DIM), lambda i: (0, i, 0))
    keys = pl.BlockSpec((h, s, HEAD_DIM), lambda i: (1, 0, 0))
    values = pl.BlockSpec((h, s, HEAD_DIM), lambda i: (2, 0, 0))
    return tile, keys, values, pl.BlockSpec((h, t, 1), lambda i: (0, i, 0))


def _sb_fwd(x, name):
    h, s = x.shape[0] // 3, x.shape[1]
    t = SB_TILE

    def body(q_ref, k_ref, v_ref, o_ref, tot_ref):
        i = pl.program_id(0)
        after = _tri(lambda r, c: r > c)

        def tile(j, carry, diagonal):
            rows = pl.ds(pl.multiple_of(j * t, t), t)
            out = []
            for hh, (right, acc) in enumerate(carry):
                z, _, lf, mask = _sb_logits(q_ref[hh], k_ref[hh, rows, :], diagonal)
                w = jnp.exp(z + lf + (right + _cum(lf, after, 2)))
                w = w if mask is None else jnp.where(mask, w, 0.0)
                out.append((right + jnp.sum(lf, axis=1, keepdims=True), acc + _dot(w.astype(BF16), v_ref[hh, rows, :], 1, 0)))
            return tuple(out)

        carry = tile(i, tuple((jnp.zeros((t, 1), F32), jnp.zeros((t, HEAD_DIM), F32)) for _ in range(h)), True)
        carry = lax.fori_loop(0, i, lambda jj, c: tile(i - 1 - jj, c, False), carry)
        for hh, (right, acc) in enumerate(carry):
            o_ref[hh] = acc
            tot_ref[hh] = right

    tile_spec, keys, values, col = _sb_specs(h, s)
    return pl.pallas_call(
        body, name=name, out_shape=[jax.ShapeDtypeStruct((h, s, HEAD_DIM), F32), jax.ShapeDtypeStruct((h, s, 1), F32)],
        grid=(s // t,), in_specs=[tile_spec, keys, values], out_specs=[tile_spec, col],
        compiler_params=_params(("parallel",)),
    )(x, x, x)


def _sb_bwd(x, tot, do, name):
    h, s = x.shape[0] // 3, x.shape[1]
    t = SB_TILE

    def body(q_ref, k_ref, v_ref, tot_ref, do_ref, dq_ref, dk_ref, dv_ref):
        i = pl.program_id(0)

        @pl.when(i == 0)
        def _():
            dk_ref[...] = jnp.zeros_like(dk_ref)
            dv_ref[...] = jnp.zeros_like(dv_ref)

        upto = _tri(lambda r, c: r <= c)
        before = _tri(lambda r, c: r < c)

        def tile(j, carry, diagonal):
            rows = pl.ds(pl.multiple_of(j * t, t), t)
            out = []
            for hh, (left, cleft, dq) in enumerate(carry):
                qv, ks, dob = q_ref[hh], k_ref[hh, rows, :], do_ref[hh].astype(BF16)
                z, e, lf, mask = _sb_logits(qv, ks, diagonal)
                between = tot_ref[hh] - (left + _cum(lf, upto, 2))
                w = jnp.exp(z + lf + between)
                w = w if mask is None else jnp.where(mask, w, 0.0)
                dlog = w * _dot(dob, v_ref[hh, rows, :], 1, 1)
                cfail = cleft + _cum(dlog, before, 2)
                sig = jnp.where(z >= 0.0, 1.0, e) / (1.0 + e)
                dz = dlog * (1.0 - sig) - sig * cfail
                dz = (dz if mask is None else jnp.where(mask, dz, 0.0)).astype(BF16)
                dk_ref[hh, rows, :] += _dot(dz, qv, 0, 0)
                dv_ref[hh, rows, :] += _dot(w.astype(BF16), dob, 0, 0)
                out.append((left + jnp.sum(lf, axis=1, keepdims=True), cleft + jnp.sum(dlog, axis=1, keepdims=True),
                            dq + _dot(dz, ks, 1, 0)))
            return tuple(out)

        zero = jnp.zeros((t, 1), F32)
        carry = lax.fori_loop(0, i, lambda j, c: tile(j, c, False),
                              tuple((zero, zero, jnp.zeros((t, HEAD_DIM), F32)) for _ in range(h)))
        for hh, (_, _, dq) in enumerate(tile(i, carry, True)):
            dq_ref[hh] = dq * (HEAD_DIM ** -0.5)

    tile_spec, keys, values, col = _sb_specs(h, s)
    full = pl.BlockSpec((h, s, HEAD_DIM), lambda i: (0, 0, 0))
    shp = jax.ShapeDtypeStruct((h, s, HEAD_DIM), F32)
    return pl.pallas_call(
        body, name=name, out_shape=[shp, shp, shp], grid=(s // t,),
        in_specs=[tile_spec, keys, values, col, tile_spec],
        out_specs=[tile_spec, full, full], compiler_params=_params(("arbitrary",)),
    )(x, x, x, tot, do)


COL_SB, COL_DIL, COL_SWA = 0, 3 * H_SB * HEAD_DIM, 3 * H_SB * HEAD_DIM + 3 * H_DIL * HEAD_DIM
N_SWA = H_SWA_Q + 2 * H_SWA_KV


def _dil_col(t, g):
    return COL_DIL + t * H_DIL * HEAD_DIM + g * 2 * HEAD_DIM


def _split_heads(qkv, name):
    tr = TOK_TILE
    scale = HEAD_DIM ** -0.5
    dils = [d for _, d in DIL_PATTERNS]

    def body(x_ref, sb_ref, d0_ref, d1_ref, d2_ref, swa_ref, pair):
        def head(col, scaled):
            v = x_ref[:, col:col + HEAD_DIM]
            return (v * scale if scaled else v).astype(BF16)

        for hh in range(3 * H_SB):
            sb_ref[hh] = head(COL_SB + hh * HEAD_DIM, hh < H_SB)
        for hh in range(N_SWA):
            swa_ref[hh] = head(COL_SWA + hh * HEAD_DIM, hh < H_SWA_Q)
        for t in range(3):
            for g, (d, out_ref) in enumerate(zip(dils, (d0_ref, d1_ref, d2_ref))):
                col = _dil_col(t, g)
                if d == 1:
                    for h in range(2):
                        out_ref[t * 2 + h] = head(col + h * HEAD_DIM, t == 0)
                    continue
                pair[...] = x_ref[:, col:col + 2 * HEAD_DIM]
                for r in range(d):
                    v = pair[pl.ds(r, tr // d, stride=d), :]
                    v = v * scale if t == 0 else v
                    for h in range(2):
                        out_ref[t * 2 * d + h * d + r] = v[:, h * HEAD_DIM:(h + 1) * HEAD_DIM].astype(BF16)

    def heads(n, length):
        return jax.ShapeDtypeStruct((n, length, HEAD_DIM), BF16)

    def spec(n, rows):
        return pl.BlockSpec((n, rows, HEAD_DIM), lambda i: (0, i, 0))

    return pl.pallas_call(
        body, name=name,
        out_shape=[heads(3 * H_SB, SEQ)] + [heads(6 * d, SEQ // d) for d in dils] + [heads(N_SWA, SEQ)],
        grid=(SEQ // tr,), in_specs=[pl.BlockSpec((tr, D_QKV), lambda i: (i, 0))],
        out_specs=[spec(3 * H_SB, tr)] + [spec(6 * d, tr // d) for d in dils] + [spec(N_SWA, tr)],
        scratch_shapes=[pltpu.VMEM((tr, 2 * HEAD_DIM), F32)], compiler_params=_params(("parallel",)),
    )(qkv)


def _join_heads(sb, dil, swa, name):
    tr = TOK_TILE
    dils = [d for _, d in DIL_PATTERNS]

    def body(*refs):
        sb_refs, dil_refs, swa_refs = refs[:3], [refs[3 + 3 * g:6 + 3 * g] for g in range(3)], refs[12:15]
        o_ref, pair, stages = refs[15], refs[16], refs[17:]

        def put(col, v):
            o_ref[:, col:col + v.shape[1]] = v.astype(BF16)

        for t in range(3):
            for h in range(H_SB):
                put(COL_SB + (t * H_SB + h) * HEAD_DIM, sb_refs[t][h])
        col = COL_SWA
        for ref in swa_refs:
            for h in range(ref.shape[0]):
                put(col, ref[h])
                col += HEAD_DIM
        for t in range(3):
            for g, d in enumerate(dils):
                ref, col = dil_refs[g][t], _dil_col(t, g)
                if d == 1:
                    for h in range(2):
                        put(col + h * HEAD_DIM, ref[h])
                    continue
                stage = stages[g - 1]
                for r in range(d):
                    stage[:, :HEAD_DIM] = ref[r]
                    stage[:, HEAD_DIM:] = ref[d + r]
                    pair[pl.ds(r, tr // d, stride=d), :] = stage[...]
                put(col, pair[...])

    def spec(n, rows):
        return pl.BlockSpec((n, rows, HEAD_DIM), lambda i: (0, i, 0))

    ins = list(sb) + [t for g in range(3) for t in dil[g]] + list(swa)
    in_specs = ([spec(H_SB, tr)] * 3 + [spec(2 * d, tr // d) for d in dils for _ in range(3)]
                + [spec(H_SWA_Q, tr), spec(H_SWA_KV, tr), spec(H_SWA_KV, tr)])
    return pl.pallas_call(
        body, name=name, out_shape=jax.ShapeDtypeStruct((SEQ, D_QKV), BF16), grid=(SEQ // tr,), in_specs=in_specs,
        out_specs=pl.BlockSpec((tr, D_QKV), lambda i: (i, 0)),
        scratch_shapes=[pltpu.VMEM((tr, 2 * HEAD_DIM), F32)] + [pltpu.VMEM((tr // d, 2 * HEAD_DIM), F32) for d in dils[1:]],
        compiler_params=_params(("parallel",)),
    )(*ins)


def _mixer_fwd(qkv, bias, sinks_l, tag):
    sb, d0, d1, d2, swa = _split_heads(qkv, name=f"split_heads_{tag}")
    st = {"sb": sb, "dil": (d0, d1, d2), "swa": swa}
    o_sb, st["sb_tot"] = _sb_fwd(sb, name=f"sb_fwd_{tag}")
    st["dil_out"], st["dil_lse"], st["dil_sink"] = [], [], []
    for gi, (_, d) in enumerate(DIL_PATTERNS):
        sink = jnp.zeros((2 * d, 1, LANES), F32)
        og, lg = _band_fwd(st["dil"][gi], bias[2 * gi:2 * gi + 2], sink, nq=2 * d, offs=(0, 2 * d, 4 * d), g=1, bias_div=d,
                           has_sink=False, name=f"dil{gi}_fwd_{tag}")
        st["dil_out"].append(og)
        st["dil_lse"].append(lg)
        st["dil_sink"].append(sink)
    o_dil = _dil_merge(st["dil_out"], st["dil_lse"], None, name=f"dil_merge_fwd_{tag}")
    st["swa_sink"] = jnp.broadcast_to(sinks_l.reshape(H_SWA_Q, 1, 1), (H_SWA_Q, 1, LANES))
    st["swa_out"] = _band_fwd(swa, bias[H_DIL:], st["swa_sink"], nq=H_SWA_Q, offs=(0, H_SWA_Q, H_SWA_Q + H_SWA_KV),
                              g=H_SWA_Q // H_SWA_KV, bias_div=1, has_sink=True, name=f"swa_fwd_{tag}")
    return (o_sb, o_dil, st["swa_out"][0]), st


def _mixer_bwd(st, bias, do_sb, do_dil, do_swa, tag):
    d_sb = _sb_bwd(st["sb"], st["sb_tot"], do_sb, name=f"sb_bwd_{tag}")
    dmerge = _dil_merge(st["dil_out"], st["dil_lse"], do_dil, name=f"dil_merge_bwd_{tag}")
    d_dil, dbs = [], []
    for gi, (_, d) in enumerate(DIL_PATTERNS):
        dq, dk, dv, db, _ = _band_bwd(st["dil"][gi], bias[2 * gi:2 * gi + 2], st["dil_sink"][gi], st["dil_out"][gi],
                                      st["dil_lse"][gi], dmerge[gi], dmerge[3 + gi], nq=2 * d, offs=(0, 2 * d, 4 * d),
                                      g=1, bias_div=d, has_sink=False, name=f"dil{gi}_bwd_{tag}")
        d_dil.append((dq, dk, dv))
        dbs.append(db)
    o_sw, l_sw = st["swa_out"]
    dq_sw, dk_sw, dv_sw, db_sw, dsink = _band_bwd(st["swa"], bias[H_DIL:], st["swa_sink"], o_sw, l_sw, do_swa,
                                                  jnp.zeros_like(l_sw), nq=H_SWA_Q, offs=(0, H_SWA_Q, H_SWA_Q + H_SWA_KV),
                                                  g=H_SWA_Q // H_SWA_KV, bias_div=1, has_sink=True, name=f"swa_bwd_{tag}")
    dqkv = _join_heads(d_sb, d_dil, (dq_sw, dk_sw, dv_sw), name=f"join_heads_{tag}")
    return dqkv, jnp.concatenate(dbs + [db_sw], 0), dsink[:, 0, 0]


PIECES = ("ffn0", "mix", "ffn1")


def _ffn_fwd(x_in, w, gain, mod_j, tag, after=None):
    st = {"x": x_in, "w": w}
    st["h"] = _norm_fwd(x_in, _row(gain), _row(mod_j[1]), _row(mod_j[0]), name=f"norm_fwd_{tag}", after=after)
    st["a"], st["u"], st["s"] = _ffn_up(st["h"], w["gate"], w["up"], name=f"up_{tag}")
    st["f"], x_out = _mm(st["s"], w["down"], res=x_in, colscale=_row(0.5 * mod_j[2]), emit_acc=True, tm=512, tn=1024,
                         name=f"down_{tag}")
    return x_out, st


def _ffn_bwd(dx_out, st, gain, mod_j, tag, done):
    w = st["w"]

    def latest(new, old):
        return old if new is None else new

    df, dgate = _gate_bwd(dx_out, st["f"], _row(0.5 * mod_j[2]), 0.5, name=f"gate_bwd_{tag}")
    token = done({"down": _mm_tn(st["s"], df, tm=D_FF // 2, name=f"dwd_{tag}")})
    da, du = _ffn_bwd_ds(df, w["down"], st["a"], st["u"], name=f"ds_{tag}")
    token = latest(done({"gate": _mm_tn(da, st["h"], after=token, tm=D_FF // 2, name=f"dwg_{tag}")}), token)
    token = latest(done({"up": _mm_tn(du, st["h"], after=token, tm=D_FF // 2, name=f"dwu_{tag}")}), token)
    dh = _mm2(da, w["gate"], du, w["up"], after=token, name=f"dh_{tag}")
    dx_in, sum_dh, sum_dhx = _norm_bwd(st["x"], dh, dx_out, _row(gain), _row(mod_j[1]), name=f"norm_bwd_{tag}")
    dmod = jnp.concatenate([sum_dh, gain * sum_dhx, dgate], 0)
    return dx_in, dmod, (1.0 + mod_j[1]) * sum_dhx[0]


def _mix_fwd(x_in, w, gain, mod_j, bias, sinks_l, tag, after=None):
    st = {"x": x_in, "w": w}
    st["h"] = _norm_fwd(x_in, _row(gain), _row(mod_j[1]), _row(mod_j[0]), name=f"norm_fwd_mix_{tag}", after=after)
    qkv = _mm(st["h"], w["qkv"], tb=True, tm=SEQ, name=f"qkv_{tag}")
    st["gates"] = _mm(st["h"], w["gates"], tb=True, tm=SEQ, name=f"gates_{tag}")
    outs, st["mix"] = _mixer_fwd(qkv, bias, sinks_l, tag)
    st["merged"], *st["t"] = _merge_fwd(*outs, st["gates"], w["br_sb"], w["br_dil"], w["br_swa"], name=f"merge_fwd_{tag}")
    st["f"], x_out = _mm(st["merged"], w["out"], res=x_in, colscale=_row(mod_j[2]), emit_acc=True, name=f"out_{tag}")
    return x_out, st


def _mix_bwd(dx_out, st, gain, mod_j, bias, tag, done):
    w = st["w"]
    df, dgate = _gate_bwd(dx_out, st["f"], _row(mod_j[2]), 1.0, name=f"gate_bwd_mix_{tag}")
    g = {"out": _mm_tn(st["merged"], df, name=f"dw_out_{tag}")}
    dmerged = _mm(df, w["out"], tb=True, name=f"dmerged_{tag}")
    dgates, do_sb, do_dil, do_swa, dbr_sb, dbr_dil, dbr_swa = _merge_bwd(
        dmerged, *st["t"], st["gates"], w["br_sb"], w["br_dil"], w["br_swa"], name=f"merge_bwd_{tag}")
    g["br_sb"] = _mm_tn(st["t"][0], dbr_sb, name=f"dw_br_sb_{tag}")
    g["br_dil"] = _mm_tn(st["t"][1], dbr_dil, name=f"dw_br_dil_{tag}")
    g["br_swa"] = _mm_tn(st["t"][2], dbr_swa, name=f"dw_br_swa_{tag}")
    dqkv, dbias, dsinks = _mixer_bwd(st["mix"], bias, do_sb, do_dil, do_swa, tag)
    g["qkv"] = _mm_tn(dqkv, st["h"], name=f"dw_qkv_{tag}")
    g["gates"] = _mm_tn(dgates, st["h"], name=f"dw_gates_{tag}")
    dh = _mm2(dqkv, w["qkv"], dgates, w["gates"], after=done(g), tm=512, name=f"dh_mix_{tag}")
    dx_in, sum_dh, sum_dhx = _norm_bwd(st["x"], dh, dx_out, _row(gain), _row(mod_j[1]), name=f"norm_bwd_mix_{tag}")
    dmod = jnp.concatenate([sum_dh, gain * sum_dhx, dgate], 0)
    return dx_in, dmod, (1.0 + mod_j[1]) * sum_dhx[0], dbias, dsinks


def _local_step(x, target, mod, gains, weights_of, rel_bias, sinks, final_gain, grads_done):
    tables = jnp.asarray(_bucket_tables())
    bias = _bias_build(rel_bias, tables, name="bias_build")
    states, h = [], x
    for l in range(DEPTH):
        st = {}
        for j, piece in enumerate(PIECES):
            w, after = weights_of(l, piece, h)
            if piece == "mix":
                h, st[piece] = _mix_fwd(h, w, gains[l, j], mod[l, j], bias, sinks[l], f"l{l}", after)
            else:
                h, st[piece] = _ffn_fwd(h, w, gains[l, j], mod[l, j], f"{piece}_l{l}", after)
        states.append(st)
    loss, dx, dfinal = _final_loss(h, target, _row(final_gain), name="final_loss")
    dmods = [[None] * 3 for _ in range(DEPTH)]
    dgains = [[None] * 3 for _ in range(DEPTH)]
    dsinks = [None] * DEPTH
    dbias = None
    for l in reversed(range(DEPTH)):
        for j in reversed(range(3)):
            piece = PIECES[j]
            done = lambda grads, l=l, piece=piece: grads_done(l, piece, grads)
            if piece == "mix":
                dx, dmods[l][j], dgains[l][j], db, dsinks[l] = _mix_bwd(dx, states[l][piece], gains[l, j], mod[l, j], bias, f"l{l}", done)
                dbias = db if dbias is None else dbias + db
            else:
                dx, dmods[l][j], dgains[l][j] = _ffn_bwd(dx, states[l][piece], gains[l, j], mod[l, j], f"{piece}_l{l}", done)
    drel = _bias_grad(dbias, tables, name="bias_grad")[:, 0, :N_BUCKETS].T
    dmod = jnp.stack([jnp.stack(m) for m in dmods])
    dgain = jnp.stack([jnp.stack(g) for g in dgains])
    return loss, dx, dmod, dgain, dfinal[0], drel, jnp.stack(dsinks)


BR_ROWS = (H_SB * HEAD_DIM, 2 * HEAD_DIM, H_SWA_Q * HEAD_DIM)


def _lanes_unshard(g, lead):
    _, rows, _ = g.shape
    r = rows // lead
    return g.reshape(N_DEV, lead, r, LANES).transpose(1, 2, 0, 3).reshape(lead, r, N_DEV * LANES)


def _lanes_shard(full):
    lead, r, _ = full.shape
    return full.reshape(lead, r, N_DEV, LANES).transpose(2, 0, 1, 3).reshape(N_DEV, lead * r, LANES)


def _pack_rows(parts, dtype):
    flat = jnp.concatenate([p.astype(dtype).reshape(-1) for p in parts])
    pad = (-flat.shape[0]) % (16 * LANES)
    if pad:
        flat = jnp.concatenate([flat, jnp.zeros((pad,), dtype)])
    return flat.reshape(-1, LANES)


def _unshard(gathered, axis):
    moved = jnp.moveaxis(gathered, 0, axis)
    shape = list(moved.shape)
    shape[axis:axis + 2] = [shape[axis] * shape[axis + 1]]
    return moved.reshape(shape)


def kernel(x, c, w_ada, b_ada, norm_gain, w_ffn_gate, w_ffn_up, w_ffn_down, w_in, w_br_sb, w_br_dil, w_br_swa, w_out, sinks, rel_bias, final_gain, loss_target, m_w_ada, m_b_ada, m_norm_gain, m_w_ffn_gate, m_w_ffn_up, m_w_ffn_down, m_w_in, m_w_br_sb, m_w_br_dil, m_w_br_swa, m_w_out, m_sinks, m_rel_bias, m_final_gain, v_w_ada, v_b_ada, v_norm_gain, v_w_ffn_gate, v_w_ffn_up, v_w_ffn_down, v_w_in, v_w_br_sb, v_w_br_dil, v_w_br_swa, v_w_out, v_sinks, v_rel_bias, v_final_gain):
    me = 4 * lax.axis_index("x") + 2 * lax.axis_index("y") + lax.axis_index("c")
    d = D_MODEL
    gate_t, up_t, in_t = jnp.swapaxes(w_ffn_gate, 2, 3), jnp.swapaxes(w_ffn_up, 2, 3), jnp.swapaxes(w_in, 1, 2)

    def piece_shards(l, piece):
        bf = lambda t: t.astype(BF16)
        if piece == "mix":
            return [bf(in_t[l]), jnp.concatenate([bf(w_br_sb[l]), bf(w_br_dil[l]), bf(w_br_swa[l])], 0), bf(w_out[l])]
        i = PIECES.index(piece) // 2
        return [bf(gate_t[l, i]), bf(up_t[l, i]), bf(w_ffn_down[l, i])]

    br_off = np.concatenate([[0], np.cumsum(BR_ROWS)])

    def piece_weights(gathered, piece):
        if piece == "mix":
            g_in, g_br, g_out = gathered
            f_in = g_in.reshape(D_QKV + D_GATES, d)
            f_br = [_lanes_unshard(g_br[:, br_off[k]:br_off[k + 1]], 1)[0] for k in range(3)]
            return {"qkv": f_in[:D_QKV], "gates": f_in[D_QKV:], "br_sb": f_br[0], "br_dil": f_br[1], "br_swa": f_br[2],
                    "out": g_out.reshape(d, d)}
        return {n: g.reshape(D_FF, d) for n, g in zip(("gate", "up", "down"), gathered)}

    small, = _all_gather([_pack_rows([c, norm_gain], F32)], name="gather_cond")
    c_all = small[:, :d // LANES].reshape(N_DEV, d)
    gains = _unshard(small[:, d // LANES:d // LANES + 6].reshape(N_DEV, DEPTH, 3, LANES), 2)

    cols = w_ada.shape[2]
    mod_cols = jnp.stack([_ada_fwd(c_all, w_ada[l], name=f"ada_fwd_l{l}") for l in range(DEPTH)])
    mod_all, = _all_gather([_pack_rows([mod_cols], F32)], name="gather_mod")
    mod_all = mod_all.reshape(N_DEV, -1)[:, :DEPTH * N_DEV * cols].reshape(N_DEV, DEPTH, N_DEV, cols)
    mod_mine = lax.dynamic_index_in_dim(mod_all, me, axis=2, keepdims=False)
    mod = (mod_mine.transpose(1, 0, 2).reshape(DEPTH, N_DEV * cols) + b_ada).reshape(DEPTH, 3, 3, d)

    order = [(l, piece) for l in range(DEPTH) for piece in PIECES]
    eager, ahead = 2, 3
    in_flight = {}
    n_tensors = 3
    first = _all_gather([s for k in range(eager) for s in piece_shards(*order[k])], after=mod_all, name="gather_first")

    def start_gather(k, after):
        l, piece = order[k]
        in_flight[k], token = _exchange_start(piece_shards(l, piece), after, gather=True, name=f"gather_{piece}_l{l}_start")
        return token

    token = first[0]
    for k in range(eager, eager + ahead - 1):
        token = start_gather(k, token)
    mod = mod + token[0, 0]

    def weights_of(l, piece, h):
        k = order.index((l, piece))
        started = eager <= k + ahead < len(order) and k + ahead not in in_flight
        token = start_gather(k + ahead, h) if started else None
        if k < eager:
            return piece_weights(first[n_tensors * k:n_tensors * (k + 1)], piece), token
        landed = _exchange_wait(in_flight[k], h if token is None else token, gather=True, name=f"gather_{piece}_l{l}_wait")
        return piece_weights(landed, piece), token

    exchanges, have = {}, {}

    def grads_done(l, piece, g):
        key = (l, piece)
        have.setdefault(key, {}).update(g)
        if piece == "mix":
            if len(have[key]) < 6:
                return None
            g = have[key]
            s_br = jnp.concatenate([_lanes_shard(g[n][None]) for n in ("br_sb", "br_dil", "br_swa")], 1)
            groups = [(("in", "br", "out"), [jnp.concatenate([g["qkv"], g["gates"]], 0).reshape(N_DEV, -1, d), s_br,
                                             g["out"].reshape(N_DEV, -1, d)])]
        elif key == order[0]:
            groups = [((n,), [t.reshape(N_DEV, -1, d)]) for n, t in g.items()]
        elif len(have[key]) < 3:
            return None
        else:
            groups = [(("gate", "up", "down"), [have[key][n].reshape(N_DEV, -1, d) for n in ("gate", "up", "down")])]
        token = None
        for names, sg in groups:
            state, token = _exchange_start(sg, sg[0], gather=False, name=f"exchange_{piece}_l{l}_{names[0]}_start")
            exchanges.setdefault(key, []).append((names, state))
        return token

    loss, dx, dmod, dgains, dfinal, drel, dsinks = _local_step(
        x[0], loss_target[0], mod, gains, weights_of, rel_bias, sinks, final_gain, grads_done)

    flat = lambda t: t.reshape(-1, t.shape[-1])
    transposed = lambda ts: tuple(flat(jnp.swapaxes(t, -1, -2)) for t in ts)
    families = {
        "gate": transposed((w_ffn_gate, m_w_ffn_gate, v_w_ffn_gate)), "up": transposed((w_ffn_up, m_w_ffn_up, v_w_ffn_up)),
        "down": tuple(flat(t) for t in (w_ffn_down, m_w_ffn_down, v_w_ffn_down)),
        "in": transposed((w_in, m_w_in, v_w_in)),
        "br": tuple(flat(jnp.concatenate(ts, 1)) for ts in ((w_br_sb, w_br_dil, w_br_swa), (m_w_br_sb, m_w_br_dil, m_w_br_swa),
                                                            (v_w_br_sb, v_w_br_dil, v_w_br_swa))),
        "out": tuple(flat(t) for t in (w_out, m_w_out, v_w_out))}
    parts, stepped = {}, {}

    def land(l, after):
        for key in reversed([k for k in order if k[0] == l]):
            for names, ex_state in exchanges[key]:
                landed = _exchange_wait(ex_state, after, gather=False, name=f"exchange_{key[1]}_l{key[0]}_{names[0]}_wait")
                parts.setdefault(key, {}).update(zip(names, landed))
                after = landed[0]

    def step_layer(l):
        last = None
        for n, (w2, m2, v2) in families.items():
            groups = [parts[key][n] for key in order if key[0] == l and n in parts[key]]
            rows_per_layer = w2.shape[0] // DEPTH
            stepped[n] = _reduce_adamw(groups, w2, m2, v2, l * rows_per_layer, stepped.get(n), name=f"reduce_adamw_{n}_l{l}")
            last = stepped[n][1]
        return last

    land(1, dx)
    after_l1 = step_layer(1)

    small_parts = [dmod, dgains, dfinal, drel.T, dsinks, loss[0, :1]]
    small_sizes = [int(np.prod(p.shape)) for p in small_parts]
    small_all, = _all_gather([_pack_rows(small_parts, F32)], after=after_l1, name="gather_small")
    small_sum = _sum_parts([small_all], name="sum_small").reshape(-1)
    offs = np.concatenate([[0], np.cumsum(small_sizes)])
    g_b_ada = small_sum[offs[0]:offs[1]].reshape(DEPTH, 9 * d)
    g_gain_full = small_sum[offs[1]:offs[2]].reshape(DEPTH, 3, d)
    g_norm_gain = lax.dynamic_slice_in_dim(g_gain_full, me * LANES, LANES, axis=2)
    g_final = small_sum[offs[2]:offs[3]]
    g_rel = small_sum[offs[3]:offs[4]].reshape(N_SOFT, N_BUCKETS).T
    g_sinks = small_sum[offs[4]:offs[5]].reshape(DEPTH, H_SWA_Q)
    loss_total = small_sum[offs[5]]

    dmod_all = small_all.reshape(N_DEV, -1)[:, :DEPTH * 9 * d].reshape(N_DEV, DEPTH, 9 * d)
    dmod_cols = lax.dynamic_slice_in_dim(dmod_all, me * cols, cols, axis=2)
    g_w_ada = jnp.stack([_ada_bwd(c_all.T, dmod_cols[:, l], name=f"ada_bwd_l{l}") for l in range(DEPTH)])

    small_state = {"w_ada": (w_ada, m_w_ada, v_w_ada), "b_ada": (b_ada, m_b_ada, v_b_ada),
                   "norm_gain": (norm_gain, m_norm_gain, v_norm_gain), "sinks": (sinks, m_sinks, v_sinks),
                   "rel_bias": (rel_bias, m_rel_bias, v_rel_bias), "final_gain": (final_gain, m_final_gain, v_final_gain)}
    grad, update = {}, {}
    for n, g in (("w_ada", g_w_ada), ("b_ada", g_b_ada), ("norm_gain", g_norm_gain), ("sinks", g_sinks),
                 ("rel_bias", g_rel), ("final_gain", g_final)):
        w, m, v = small_state[n]
        grad[n] = g
        if w.ndim == 1:
            update[n] = tuple(t.reshape(w.shape) for t in _adamw(_row(w), _row(g), _row(m), _row(v), name=f"adamw_{n}"))
        else:
            update[n] = _adamw(w, g, m, v, name=f"adamw_{n}")

    land(0, update["w_ada"][0])
    step_layer(0)

    def unflat(n, like, swapped):
        shape = jnp.swapaxes(like, -1, -2).shape if swapped else like.shape
        out = [t.reshape(shape) for t in stepped[n]]
        return [jnp.swapaxes(t, -1, -2) for t in out] if swapped else out

    results = {"w_ffn_gate": unflat("gate", w_ffn_gate, True), "w_ffn_up": unflat("up", w_ffn_up, True),
               "w_ffn_down": unflat("down", w_ffn_down, False), "w_in": unflat("in", w_in, True),
               "w_out": unflat("out", w_out, False)}
    br = [t.reshape(DEPTH, -1, LANES) for t in stepped["br"]]
    for k, n in enumerate(("w_br_sb", "w_br_dil", "w_br_swa")):
        results[n] = [t[:, br_off[k]:br_off[k + 1]] for t in br]
    for n, (g, dl, nm, nv) in results.items():
        grad[n], update[n] = g, (dl, nm, nv)

    names = ["w_ada", "b_ada", "norm_gain", "w_ffn_gate", "w_ffn_up", "w_ffn_down", "w_in", "w_br_sb", "w_br_dil",
             "w_br_swa", "w_out", "sinks", "rel_bias", "final_gain"]
    return (loss_total, dx[None], *[grad[n] for n in names], *[update[n][0] for n in names],
            *[update[n][1] for n in names], *[update[n][2] for n in names])
```

```python
import math

import numpy as np
import jax
import jax.numpy as jnp
from jax import lax
from jax.experimental import pallas as pl
from jax.experimental.pallas import tpu as pltpu

F32, BF16 = jnp.float32, jnp.bfloat16

SEQ, D_MODEL, D_FF, HEAD_DIM = 2048, 1024, 2816, 64
DEPTH = 2
BLK = 128
H_SB, H_DIL, H_SWA_Q, H_SWA_KV = 4, 6, 6, 2
DIL_PATTERNS = ((128, 1), (512, 4), (2048, 16))
SWA_WINDOW = 128
N_BUCKETS, MAX_REL_DIST = 32, 2048
RMS_EPS = 1e-6
D_QKV = 2560
D_GATES = 3 * D_MODEL
ADAM_LR, ADAM_B1, ADAM_B2, ADAM_EPS, ADAM_WD, ADAM_STEP = 0.001, 0.9, 0.999, 1e-08, 0.01, 10

N_DEV = 8
LANES = 128
NEG = -1e30
SB_TILE = 256
VMEM_LIMIT_BYTES = 48 * 1024 * 1024
HBM = pl.BlockSpec(memory_space=pltpu.HBM)
MESH = pl.DeviceIdType.MESH


def _tile(n, target):
    t = (min(n, target) // LANES) * LANES
    while t >= LANES:
        if n % t == 0:
            return t
        t -= LANES
    return n


def _row_tile(r, cap):
    t = (min(r, cap) // 16) * 16
    while t > 16 and r % t:
        t -= 16
    return t


def _params(semantics=None):
    return pltpu.CompilerParams(dimension_semantics=semantics, vmem_limit_bytes=VMEM_LIMIT_BYTES)


def _dot(a, b, ca, cb):
    return lax.dot_general(a, b, (((ca,), (cb,)), ((), ())), preferred_element_type=F32)


def _sigmoid(a):
    return 1.0 / (1.0 + jnp.exp(-a))


def _row(v):
    return v.reshape(1, -1)


def _all_gather(arrs, name, after=None):
    n = len(arrs)
    ins = list(arrs) + ([] if after is None else [after])

    def body(*refs):
        x_refs, out_refs = refs[:n], refs[len(ins):len(ins) + n]
        send_sems, recv_sems, local_sems = refs[len(ins) + n:]
        x, y, c = lax.axis_index("x"), lax.axis_index("y"), lax.axis_index("c")
        me, sibling = (x, y, c), (x, y, 1 - c)
        chips = [(1 - x, y), (x, 1 - y), (1 - x, 1 - y)]

        def slot(t, px, py, pc):
            return out_refs[t].at[4 * px + 2 * py + pc]

        def copy(t, k, block, to, src=None):
            return pltpu.make_async_remote_copy(
                src_ref=slot(t, *block) if src is None else src, dst_ref=slot(t, *block),
                send_sem=send_sems.at[7 * t + k], recv_sem=recv_sems.at[7 * t + k], device_id=to, device_id_type=MESH)

        mine = [pltpu.make_async_copy(x_refs[t], slot(t, *me), local_sems.at[t]) for t in range(n)]
        for cp in mine:
            cp.start()
        first = []
        for t in range(n):
            first.append(copy(t, 0, me, sibling, src=x_refs[t]))
            first += [copy(t, 1 + j, me, (*chip, c), src=x_refs[t]) for j, chip in enumerate(chips)]
        for cp in first:
            cp.start()
        passed = []
        for j, chip in enumerate(chips):
            for t in range(n):
                copy(t, 1 + j, (*chip, c), me).wait_recv()
                passed.append(copy(t, 4 + j, (*chip, c), sibling))
                passed[-1].start()
        for t in range(n):
            copy(t, 0, sibling, me).wait_recv()
        for j, chip in enumerate(chips):
            for t in range(n):
                copy(t, 4 + j, (*chip, 1 - c), me).wait_recv()
        for cp in first + passed:
            cp.wait_send()
        for cp in mine:
            cp.wait()

    return pl.pallas_call(
        body, name=name, out_shape=[jax.ShapeDtypeStruct((N_DEV,) + a.shape, a.dtype) for a in arrs],
        in_specs=[HBM] * n + [pl.BlockSpec(memory_space=pl.ANY)] * (len(ins) - n), out_specs=[HBM] * n,
        scratch_shapes=[pltpu.SemaphoreType.DMA((7 * n,)), pltpu.SemaphoreType.DMA((7 * n,)), pltpu.SemaphoreType.DMA((n,))],
    )(*ins)


def _direct_copies(x_refs, land_refs, send_sems, recv_sems, local_sems, gather):
    x, y, c = lax.axis_index("x"), lax.axis_index("y"), lax.axis_index("c")
    me = 4 * x + 2 * y + c
    sends, recvs = [], []
    for k in range(1, N_DEV):
        px = 1 - x if (k >> 2) & 1 else x
        py = 1 - y if (k >> 1) & 1 else y
        pc = 1 - c if k & 1 else c
        peer = 4 * px + 2 * py + pc
        for t, (x_ref, land_ref) in enumerate(zip(x_refs, land_refs)):
            sem = 7 * t + k - 1
            for out, src, slot in ((sends, x_ref if gather else x_ref.at[peer], me),
                                   (recvs, x_ref if gather else x_ref.at[me], peer)):
                out.append(pltpu.make_async_remote_copy(
                    src_ref=src, dst_ref=land_ref.at[slot], send_sem=send_sems.at[sem], recv_sem=recv_sems.at[sem],
                    device_id=(px, py, pc), device_id_type=MESH))
    own = [pltpu.make_async_copy(x_ref if gather else x_ref.at[me], land_ref.at[me], local_sems.at[t])
           for t, (x_ref, land_ref) in enumerate(zip(x_refs, land_refs))]
    return sends, recvs, own


SEM =pl.BlockSpec(memory_space=pltpu.SEMAPHORE)
ANY = pl.BlockSpec(memory_space=pl.ANY)
SIDE_EFFECT = pltpu.SideEffectType.DATAFLOW_SIDE_EFFECTING


def _exchange_start(arrs, after, *, gather, name):
    n = len(arrs)
    lands = [lax.empty(((N_DEV,) + a.shape) if gather else a.shape, a.dtype) for a in arrs]

    def body(*refs):
        sends, _, own = _direct_copies(refs[:n], refs[n:2 * n], *refs[2 * n + 1:2 * n + 4], gather)
        for cp in own + sends:
            cp.start()
        refs[-1][...] = jnp.zeros_like(refs[-1])

    ops = [pltpu.with_memory_space_constraint(a, pltpu.HBM) for a in list(arrs) + lands]
    out = pl.pallas_call(
        body, name=name,
        out_shape=(pltpu.SemaphoreType.DMA((7 * n,)), pltpu.SemaphoreType.DMA((7 * n,)), pltpu.SemaphoreType.DMA((n,)),
                   *[pltpu.HBM(a.shape, a.dtype) for a in ops], jax.ShapeDtypeStruct((8, LANES), F32)),
        in_specs=[HBM] * (2 * n) + [ANY],
        out_specs=(SEM, SEM, SEM, *[HBM] * (2 * n), pl.BlockSpec(memory_space=pltpu.VMEM)),
        input_output_aliases={t: 3 + t for t in range(2 * n)},
        compiler_params=pltpu.CompilerParams(has_side_effects=SIDE_EFFECT),
    )(*ops, after)
    return (out[:3], out[3:3 + n], out[3 + n:3 + 2 * n]), out[-1]


def _exchange_wait(state, after, *, gather, name):
    sems, arrs, lands = state
    n = len(arrs)

    def body(*refs):
        sends, recvs, own = _direct_copies(refs[:n], refs[n:2 * n], *refs[2 * n:2 * n + 3], gather)
        for cp in own:
            cp.wait()
        for cp in sends:
            cp.wait_send()
        for cp in recvs:
            cp.wait_recv()

    out = pl.pallas_call(
        body, name=name, out_shape=tuple(pltpu.HBM(a.shape, a.dtype) for a in list(arrs) + list(lands)),
        in_specs=[HBM] * (2 * n) + [SEM, SEM, SEM, ANY], out_specs=tuple([HBM] * (2 * n)),
        input_output_aliases={t: t for t in range(2 * n)},
        compiler_params=pltpu.CompilerParams(has_side_effects=SIDE_EFFECT),
    )(*arrs, *lands, *sems, after)
    return out[n:]


def _sum_parts(groups, name):
    n, r, cdim = groups[0].shape
    tr = _row_tile(r, max(16, (1 << 21) // (n * cdim * groups[0].dtype.itemsize)))
    steps = r // tr

    def body(*refs):
        o_ref = refs[-1]
        gg = pl.program_id(0)
        for gi in range(len(groups)):
            @pl.when(gg == gi)
            def _(gi=gi):
                acc = refs[gi][0].astype(F32)
                for k in range(1, n):
                    acc = acc + refs[gi][k].astype(F32)
                o_ref[...] = acc

    def in_spec(gi):
        return pl.BlockSpec((n, tr, cdim), lambda gg, i: (0, jnp.where(gg == gi, i, 0), 0))

    return pl.pallas_call(
        body, name=name, out_shape=jax.ShapeDtypeStruct((len(groups) * r, cdim), F32), grid=(len(groups), steps),
        in_specs=[in_spec(gi) for gi in range(len(groups))],
        out_specs=pl.BlockSpec((tr, cdim), lambda gg, i: (gg * steps + i, 0)),
        compiler_params=_params(("parallel", "parallel")),
    )(*groups)


def _mm_tn(a, b, *, name, after=None, tm=512, tn=1024):
    k, m = a.shape
    n = b.shape[1]
    tm, tn = _tile(m, tm), _tile(n, tn)

    def body(a_ref, b_ref, *rest):
        o_ref, at_ref = rest[-2], rest[-1]

        @pl.when(pl.program_id(1) == 0)
        def _():
            at_ref[...] = a_ref[...].astype(BF16).T

        o_ref[...] = _dot(at_ref[...], b_ref[...].astype(BF16), 1, 0).astype(BF16)

    ins = [a, b] + ([] if after is None else [after])
    return pl.pallas_call(
        body, name=name, out_shape=jax.ShapeDtypeStruct((m, n), BF16), grid=(m // tm, n // tn),
        in_specs=[pl.BlockSpec((k, tm), lambda i, j: (0, i)), pl.BlockSpec((k, tn), lambda i, j: (0, j))] + [ANY] * (len(ins) - 2),
        out_specs=pl.BlockSpec((tm, tn), lambda i, j: (i, j)),
        scratch_shapes=[pltpu.VMEM((tm, k), BF16)], compiler_params=_params(("parallel", "arbitrary")),
    )(*ins)


def _mm2(a1, b1, a2, b2, *, name, after=None, tm=256, tn=1024):
    m = a1.shape[0]
    n = b1.shape[1]
    tm, tn = _tile(m, tm), _tile(n, tn)

    def body(a1_ref, b1_ref, a2_ref, b2_ref, *rest):
        rest[-1][...] = (_dot(a1_ref[...].astype(BF16), b1_ref[...], 1, 0)
                         + _dot(a2_ref[...].astype(BF16), b2_ref[...], 1, 0))

    ins = [a1, b1, a2, b2] + ([] if after is None else [after])

    def a_spec(t):
        return pl.BlockSpec((tm, t.shape[1]), lambda i, j: (i, 0))

    def b_spec(t):
        return pl.BlockSpec((t.shape[0], tn), lambda i, j: (0, j))

    return pl.pallas_call(
        body, name=name, out_shape=jax.ShapeDtypeStruct((m, n), F32), grid=(m // tm, n // tn),
        in_specs=[a_spec(a1), b_spec(b1), a_spec(a2), b_spec(b2)] + [ANY] * (len(ins) - 4),
        out_specs=pl.BlockSpec((tm, tn), lambda i, j: (i, j)), compiler_params=_params(("parallel", "parallel")),
    )(*ins)


def _mm(a, b, *, name, ta=False, tb=False, res=None, colscale=None, emit_acc=False,
        out_dtype=F32, tm=512, tn=512):
    m, k = (a.shape[1], a.shape[0]) if ta else a.shape
    n = b.shape[0] if tb else b.shape[1]
    tm, tn = _tile(m, tm), _tile(n, tn)
    ca, cb = (0 if ta else 1), (1 if tb else 0)
    a_spec = pl.BlockSpec((k, tm), lambda i, j: (0, i)) if ta else pl.BlockSpec((tm, k), lambda i, j: (i, 0))
    b_spec = pl.BlockSpec((tn, k), lambda i, j: (j, 0)) if tb else pl.BlockSpec((k, tn), lambda i, j: (0, j))
    tile = pl.BlockSpec((tm, tn), lambda i, j: (i, j))
    ins, in_specs = [a, b], [a_spec, b_spec]
    if res is not None:
        ins.append(res)
        in_specs.append(tile)
    if colscale is not None:
        ins.append(colscale)
        in_specs.append(pl.BlockSpec((1, tn), lambda i, j: (0, j)))
    n_in = len(ins)

    def body(*refs):
        outs = refs[n_in:]
        acc = _dot(refs[0][...].astype(BF16), refs[1][...].astype(BF16), ca, cb)
        val, p = acc, 2
        if res is not None:
            r_val, p = refs[p][...], p + 1
        if colscale is not None:
            val = val * refs[p][...]
        if res is not None:
            val = r_val + val
        if emit_acc:
            outs[0][...] = acc
        outs[-1][...] = val.astype(out_dtype)

    out_shape = [jax.ShapeDtypeStruct((m, n), out_dtype)]
    out_specs = [tile]
    if emit_acc:
        out_shape.insert(0, jax.ShapeDtypeStruct((m, n), F32))
        out_specs.insert(0, tile)
    out = pl.pallas_call(
        body, name=name, out_shape=out_shape, grid=(m // tm, n // tn), in_specs=in_specs, out_specs=out_specs,
        compiler_params=_params(("parallel", "parallel")),
    )(*ins)
    return out if emit_acc else out[0]


def _norm_fwd(x, g, scale, shift, name, after=None):
    s, d = x.shape
    tr = 256

    def body(x_ref, g_ref, sc_ref, sh_ref, *rest):
        xv = x_ref[...]
        rstd = lax.rsqrt(jnp.mean(xv * xv, axis=-1, keepdims=True) + RMS_EPS)
        rest[-1][...] = (xv * rstd * g_ref[...] * (1.0 + sc_ref[...]) + sh_ref[...]).astype(BF16)

    rowspec = pl.BlockSpec((1, d), lambda i: (0, 0))
    ins = [x, g, scale, shift] + ([] if after is None else [after])
    return pl.pallas_call(
        body, name=name, out_shape=jax.ShapeDtypeStruct((s, d), BF16), grid=(s // tr,),
        in_specs=[pl.BlockSpec((tr, d), lambda i: (i, 0)), rowspec, rowspec, rowspec] + [ANY] * (len(ins) - 4),
        out_specs=pl.BlockSpec((tr, d), lambda i: (i, 0)),
        compiler_params=_params(("parallel",)),
    )(*ins)


def _norm_bwd(x, dh, dres, g, scale, name):
    s, d = x.shape
    tr = 256

    def body(x_ref, dh_ref, dr_ref, g_ref, sc_ref, dx_ref, a_ref, b_ref):
        @pl.when(pl.program_id(0) == 0)
        def _():
            a_ref[...] = jnp.zeros_like(a_ref)
            b_ref[...] = jnp.zeros_like(b_ref)

        xv = x_ref[...]
        rstd = lax.rsqrt(jnp.mean(xv * xv, axis=-1, keepdims=True) + RMS_EPS)
        xhat = xv * rstd
        dhv = dh_ref[...]
        dxhat = dhv * (g_ref[...] * (1.0 + sc_ref[...]))
        mean_term = jnp.mean(dxhat * xhat, axis=-1, keepdims=True)
        dx_ref[...] = dr_ref[...] + rstd * (dxhat - xhat * mean_term)
        a_ref[...] += jnp.sum(dhv, axis=0, keepdims=True)
        b_ref[...] += jnp.sum(dhv * xhat, axis=0, keepdims=True)

    rowspec = pl.BlockSpec((1, d), lambda i: (0, 0))
    tile = pl.BlockSpec((tr, d), lambda i: (i, 0))
    return pl.pallas_call(
        body, name=name,
        out_shape=[jax.ShapeDtypeStruct((s, d), F32), jax.ShapeDtypeStruct((1, d), F32), jax.ShapeDtypeStruct((1, d), F32)],
        grid=(s // tr,), in_specs=[tile, tile, tile, rowspec, rowspec], out_specs=[tile, rowspec, rowspec],
        compiler_params=_params(("arbitrary",)),
    )(x, dh, dres, g, scale)


def _gate_bwd(dxn, f, colscale, coef, name):
    s, d = dxn.shape
    tr = 256

    def body(dx_ref, f_ref, cs_ref, df_ref, dg_ref):
        @pl.when(pl.program_id(0) == 0)
        def _():
            dg_ref[...] = jnp.zeros_like(dg_ref)

        dxv = dx_ref[...]
        df_ref[...] = (dxv * cs_ref[...]).astype(BF16)
        dg_ref[...] += coef * jnp.sum(dxv * f_ref[...], axis=0, keepdims=True)

    rowspec = pl.BlockSpec((1, d), lambda i: (0, 0))
    tile = pl.BlockSpec((tr, d), lambda i: (i, 0))
    return pl.pallas_call(
        body, name=name, out_shape=[jax.ShapeDtypeStruct((s, d), BF16), jax.ShapeDtypeStruct((1, d), F32)],
        grid=(s // tr,), in_specs=[tile, tile, rowspec], out_specs=[tile, rowspec],
        compiler_params=_params(("arbitrary",)),
    )(dxn, f, colscale)


def _ffn_up(h, wg, wu, name, tm=SEQ, tn=256):
    s, d = h.shape
    f = wg.shape[0]

    def body(h_ref, wg_ref, wu_ref, a_ref, u_ref, s_ref):
        hv = h_ref[...]
        a = _dot(hv, wg_ref[...], 1, 1)
        u = _dot(hv, wu_ref[...], 1, 1)
        a_ref[...] = a.astype(BF16)
        u_ref[...] = u.astype(BF16)
        s_ref[...] = (a * _sigmoid(a) * u).astype(BF16)

    tile = pl.BlockSpec((tm, tn), lambda i, j: (i, j))
    wspec = pl.BlockSpec((tn, d), lambda i, j: (j, 0))
    return pl.pallas_call(
        body, name=name,
        out_shape=[jax.ShapeDtypeStruct((s, f), BF16), jax.ShapeDtypeStruct((s, f), BF16), jax.ShapeDtypeStruct((s, f), BF16)],
        grid=(s // tm, f // tn), in_specs=[pl.BlockSpec((tm, d), lambda i, j: (i, 0)), wspec, wspec],
        out_specs=[tile, tile, tile], compiler_params=_params(("parallel", "parallel")),
    )(h, wg, wu)


def _ffn_bwd_ds(df, wd, a, u, name, tm=SEQ, tn=256):
    s, d = df.shape
    f = wd.shape[0]

    def body(df_ref, wd_ref, a_ref, u_ref, da_ref, du_ref):
        ds = _dot(df_ref[...], wd_ref[...], 1, 1)
        av = a_ref[...].astype(F32)
        sg = _sigmoid(av)
        da_ref[...] = (ds * u_ref[...].astype(F32) * (sg * (1.0 + av * (1.0 - sg)))).astype(BF16)
        du_ref[...] = (ds * (av * sg)).astype(BF16)

    tile = pl.BlockSpec((tm, tn), lambda i, j: (i, j))
    return pl.pallas_call(
        body, name=name, out_shape=[jax.ShapeDtypeStruct((s, f), BF16), jax.ShapeDtypeStruct((s, f), BF16)],
        grid=(s // tm, f // tn),
        in_specs=[pl.BlockSpec((tm, d), lambda i, j: (i, 0)), pl.BlockSpec((tn, d), lambda i, j: (j, 0)), tile, tile],
        out_specs=[tile, tile], compiler_params=_params(("parallel", "parallel")),
    )(df, wd, a, u)


def _merge_fwd(o_sb, o_dil, o_swa, gates, wb_sb, wb_dil, wb_swa, name):
    s, d = SEQ, D_MODEL
    tm = 256

    def body(osb_ref, odl_ref, osw_ref, g_ref, wsb_ref, wdl_ref, wsw_ref, m_ref, tsb_ref, tdl_ref, tsw_ref):
        for h in range(osb_ref.shape[0]):
            tsb_ref[:, h * HEAD_DIM:(h + 1) * HEAD_DIM] = osb_ref[h].astype(BF16)
        for h in range(osw_ref.shape[0]):
            tsw_ref[:, h * HEAD_DIM:(h + 1) * HEAD_DIM] = osw_ref[h].astype(BF16)
        tdl_ref[...] = odl_ref[...].astype(BF16)
        acc = _sigmoid(g_ref[:, 0:d]) * _dot(tsb_ref[...], wsb_ref[...], 1, 0)
        acc += _sigmoid(g_ref[:, d:2 * d]) * _dot(tdl_ref[...], wdl_ref[...], 1, 0)
        acc += _sigmoid(g_ref[:, 2 * d:3 * d]) * _dot(tsw_ref[...], wsw_ref[...], 1, 0)
        m_ref[...] = acc.astype(BF16)

    def rows(w):
        return pl.BlockSpec((tm, w), lambda i: (i, 0))

    def heads(n):
        return pl.BlockSpec((n, tm, HEAD_DIM), lambda i: (0, i, 0))

    def whole(w):
        return pl.BlockSpec((w, d), lambda i: (0, 0))

    return pl.pallas_call(
        body, name=name, out_shape=[jax.ShapeDtypeStruct((s, w), BF16) for w in (d, 256, 128, 384)], grid=(s // tm,),
        in_specs=[heads(H_SB), rows(128), heads(H_SWA_Q), rows(3 * d), whole(256), whole(128), whole(384)],
        out_specs=[rows(d), rows(256), rows(128), rows(384)], compiler_params=_params(("parallel",)),
    )(o_sb, o_dil, o_swa, gates, wb_sb, wb_dil, wb_swa)


def _merge_bwd(dmerged, t_sb, t_dil, t_swa, gates, wb_sb, wb_dil, wb_swa, name):
    s, d = SEQ, D_MODEL
    tm = 256

    def body(dm_ref, tsb_ref, tdl_ref, tsw_ref, g_ref, wsb_ref, wdl_ref, wsw_ref,
             dg_ref, dosb_ref, dodl_ref, dosw_ref, dbsb_ref, dbdl_ref, dbsw_ref):
        dm = dm_ref[...]
        for idx, (t_ref, w_ref, do_ref, db_ref) in enumerate((
                (tsb_ref, wsb_ref, dosb_ref, dbsb_ref), (tdl_ref, wdl_ref, dodl_ref, dbdl_ref),
                (tsw_ref, wsw_ref, dosw_ref, dbsw_ref))):
            w = w_ref[...]
            br = _dot(t_ref[...], w, 1, 0)
            sg = _sigmoid(g_ref[:, idx * d:(idx + 1) * d])
            dbr = (dm * sg).astype(BF16)
            dg_ref[:, idx * d:(idx + 1) * d] = (dm * br * (sg * (1.0 - sg))).astype(BF16)
            db_ref[...] = dbr
            do = _dot(dbr, w, 1, 1)
            if len(do_ref.shape) == 2:
                do_ref[...] = do
            else:
                for h in range(do_ref.shape[0]):
                    do_ref[h] = do[:, h * HEAD_DIM:(h + 1) * HEAD_DIM]

    def rows(w):
        return pl.BlockSpec((tm, w), lambda i: (i, 0))

    def heads(n):
        return pl.BlockSpec((n, tm, HEAD_DIM), lambda i: (0, i, 0))

    def whole(w):
        return pl.BlockSpec((w, d), lambda i: (0, 0))

    def shp(w, dt):
        return jax.ShapeDtypeStruct((s, w), dt)

    def hshp(n):
        return jax.ShapeDtypeStruct((n, s, HEAD_DIM), F32)

    return pl.pallas_call(
        body, name=name,
        out_shape=[shp(3 * d, BF16), hshp(H_SB), shp(128, F32), hshp(H_SWA_Q), shp(d, BF16), shp(d, BF16), shp(d, BF16)],
        grid=(s // tm,),
        in_specs=[rows(d), rows(256), rows(128), rows(384), rows(3 * d), whole(256), whole(128), whole(384)],
        out_specs=[rows(3 * d), heads(H_SB), rows(128), heads(H_SWA_Q), rows(d), rows(d), rows(d)],
        compiler_params=_params(("parallel",)),
    )(dmerged, t_sb, t_dil, t_swa, gates, wb_sb, wb_dil, wb_swa)


def _final_loss(x, target, g, name):
    s, d = x.shape
    tr = 256

    def body(x_ref, t_ref, g_ref, loss_ref, dx_ref, dg_ref):
        @pl.when(pl.program_id(0) == 0)
        def _():
            loss_ref[...] = jnp.zeros_like(loss_ref)
            dg_ref[...] = jnp.zeros_like(dg_ref)

        xv = x_ref[...]
        gv = g_ref[...]
        rstd = lax.rsqrt(jnp.mean(xv * xv, axis=-1, keepdims=True) + RMS_EPS)
        xhat = xv * rstd
        err = xhat * gv - t_ref[...]
        loss_ref[...] += 0.5 * jnp.sum(jnp.mean(err * err, axis=-1, keepdims=True))
        dy = err * (1.0 / d)
        dxhat = dy * gv
        mean_term = jnp.mean(dxhat * xhat, axis=-1, keepdims=True)
        dx_ref[...] = rstd * (dxhat - xhat * mean_term)
        dg_ref[...] += jnp.sum(dy * xhat, axis=0, keepdims=True)

    rowspec = pl.BlockSpec((1, d), lambda i: (0, 0))
    tile = pl.BlockSpec((tr, d), lambda i: (i, 0))
    return pl.pallas_call(
        body, name=name,
        out_shape=[jax.ShapeDtypeStruct((1, LANES), F32), jax.ShapeDtypeStruct((s, d), F32), jax.ShapeDtypeStruct((1, d), F32)],
        grid=(s // tr,), in_specs=[tile, tile, rowspec],
        out_specs=[pl.BlockSpec((1, LANES), lambda i: (0, 0)), tile, rowspec],
        compiler_params=_params(("arbitrary",)),
    )(x, target, g)


def _adamw(w, g, m, v, name):
    shape = w.shape
    cols = shape[-1]
    rows = int(np.prod(shape[:-1])) if len(shape) > 1 else 1
    tr = rows
    for cand in (1024, 512, 256, 128, 64, 32, 16, 8):
        if rows % cand == 0 and rows > cand and cand * cols * 4 <= (1 << 21):
            tr = cand
            break

    def body(w_ref, g_ref, m_ref, v_ref, d_ref, nm_ref, nv_ref):
        d_ref[...], nm_ref[...], nv_ref[...] = _adam_update(w_ref[...], g_ref[...], m_ref[...], v_ref[...])

    tile = pl.BlockSpec((tr, cols), lambda i: (i, 0))
    flat = [t.reshape(rows, cols) for t in (w, g, m, v)]
    out = pl.pallas_call(
        body, name=name, out_shape=[jax.ShapeDtypeStruct((rows, cols), F32)] * 3, grid=(rows // tr,),
        in_specs=[tile] * 4, out_specs=[tile] * 3, compiler_params=_params(("parallel",)),
    )(*flat)
    return tuple(t.reshape(shape) for t in out)


def _adam_update(w, gv, m, v):
    nm = ADAM_B1 * m + (1.0 - ADAM_B1) * gv
    nv = ADAM_B2 * v + (1.0 - ADAM_B2) * (gv * gv)
    m_hat = nm / (1.0 - ADAM_B1 ** ADAM_STEP)
    v_hat = nv / (1.0 - ADAM_B2 ** ADAM_STEP)
    return -ADAM_LR * (m_hat / (jnp.sqrt(v_hat) + ADAM_EPS) + ADAM_WD * w), nm, nv


def _reduce_adamw(groups, w, m, v, row0, prev, name, after=None):
    n, r, cdim = groups[0].shape
    rows = w.shape[0]
    tr = _row_tile(r, max(16, (1 << 22) // (n * cdim * groups[0].dtype.itemsize)))
    steps = r // tr
    ng = len(groups)

    def body(*refs):
        w_ref, m_ref, v_ref = refs[ng:ng + 3]
        g_out, d_out, m_out, v_out = refs[-4:]
        gg = pl.program_id(0)
        for gi in range(ng):
            @pl.when(gg == gi)
            def _(gi=gi):
                acc = refs[gi][0].astype(F32)
                for k in range(1, n):
                    acc = acc + refs[gi][k].astype(F32)
                g_out[...] = acc
                d_out[...], m_out[...], v_out[...] = _adam_update(w_ref[...], acc, m_ref[...], v_ref[...])

    def part_spec(gi):
        return pl.BlockSpec((n, tr, cdim), lambda gg, i: (0, jnp.where(gg == gi, i, 0), 0))

    tile = pl.BlockSpec((tr, cdim), lambda gg, i: (row0 // tr + gg * steps + i, 0))
    extra = ([] if prev is None else list(prev)) + ([] if after is None else [after])
    return pl.pallas_call(
        body, name=name, out_shape=[jax.ShapeDtypeStruct((rows, cdim), F32)] * 4, grid=(ng, steps),
        in_specs=[part_spec(gi) for gi in range(ng)] + [tile] * 3 + [ANY] * len(extra), out_specs=[tile] * 4,
        input_output_aliases={} if prev is None else {ng + 3 + k: k for k in range(4)},
        compiler_params=_params(("parallel", "parallel")),
    )(*groups, w, m, v, *extra)


def _ada_fwd(c_all, w, name):
    n = w.shape[1]

    def body(c_ref, w_ref, o_ref):
        cv = c_ref[...]
        o_ref[...] = jnp.dot(cv * _sigmoid(cv), w_ref[...], preferred_element_type=F32, precision=lax.Precision.HIGHEST)

    return pl.pallas_call(body, name=name, out_shape=jax.ShapeDtypeStruct((N_DEV, n), F32), compiler_params=_params())(c_all, w)


def _ada_bwd(c_all_t, dmod, name):
    n = dmod.shape[1]

    def body(c_ref, d_ref, o_ref):
        cv = c_ref[...]
        o_ref[...] = jnp.dot(cv * _sigmoid(cv), d_ref[...], preferred_element_type=F32, precision=lax.Precision.HIGHEST)

    return pl.pallas_call(body, name=name, out_shape=jax.ShapeDtypeStruct((D_MODEL, n), F32), compiler_params=_params())(c_all_t, dmod)


def _bucket_tables():
    rel = np.arange(BLK)[:, None] + BLK - np.arange(2 * BLK)[None, :]
    max_exact = N_BUCKETS // 2

    def bucket(n):
        nf = np.maximum(n, 1).astype(np.float32)
        large = max_exact + (np.log(nf / np.float32(max_exact)) / np.float32(math.log(MAX_REL_DIST / max_exact))
                             * np.float32(N_BUCKETS - max_exact)).astype(np.int32)
        return np.where(n < max_exact, n, np.minimum(large, N_BUCKETS - 1))

    tabs = []
    for dil, max_dist in ((1, 128), (4, 128), (16, 128), (1, SWA_WINDOW - 1)):
        in_band = (rel >= 0) & (rel <= max_dist)
        tabs.append(np.where(in_band, bucket(np.maximum(rel, 0) * dil), -1))
    return np.stack(tabs).astype(np.int32)


N_SOFT = H_DIL + H_SWA_Q


def _table_of_head(h):
    return jnp.minimum(h // 2, 3)


def _bias_build(rel_bias, tables, name):
    def body(rel_ref, t_ref, o_ref):
        h = pl.program_id(0)
        tb = t_ref[0]
        out = jnp.full((BLK, 2 * BLK), NEG, F32)
        for b in range(N_BUCKETS):
            out = jnp.where(tb == b, rel_ref[b, h], out)
        o_ref[0] = out

    return pl.pallas_call(
        body, name=name, out_shape=jax.ShapeDtypeStruct((N_SOFT, BLK, 2 * BLK), F32), grid=(N_SOFT,),
        in_specs=[pl.BlockSpec(memory_space=pltpu.SMEM),
                  pl.BlockSpec((1, BLK, 2 * BLK), lambda h: (_table_of_head(h), 0, 0))],
        out_specs=pl.BlockSpec((1, BLK, 2 * BLK), lambda h: (h, 0, 0)),
        compiler_params=_params(("parallel",)),
    )(rel_bias, tables)


def _bias_grad(dbias, tables, name):
    def body(d_ref, t_ref, o_ref):
        tb = t_ref[0]
        dv = d_ref[0]
        lane = lax.broadcasted_iota(jnp.int32, (1, LANES), 1)
        out = jnp.zeros((1, LANES), F32)
        for b in range(N_BUCKETS):
            out = jnp.where(lane == b, jnp.sum(jnp.where(tb == b, dv, 0.0)), out)
        o_ref[0] = out

    return pl.pallas_call(
        body, name=name, out_shape=jax.ShapeDtypeStruct((N_SOFT, 1, LANES), F32), grid=(N_SOFT,),
        in_specs=[pl.BlockSpec((1, BLK, 2 * BLK), lambda h: (h, 0, 0)),
                  pl.BlockSpec((1, BLK, 2 * BLK), lambda h: (_table_of_head(h), 0, 0))],
        out_specs=pl.BlockSpec((1, 1, LANES), lambda h: (h, 0, 0)),
        compiler_params=_params(("parallel",)),
    )(dbias, tables)


def _band_layout(g, bias_div):
    assert g == 1 or bias_div == 1
    return bias_div if g == 1 else 1


def _band_specs(length, g, bias_div, offs):
    ns = _band_layout(g, bias_div)

    def seqs(off, div=1):
        return pl.BlockSpec((ns, length, HEAD_DIM), lambda s: (off // ns + s // div, 0, 0))

    xspecs = [seqs(offs[0]), seqs(offs[1], g), seqs(offs[2], g)]
    bspec = pl.BlockSpec((1, BLK, 2 * BLK), lambda s: (s, 0, 0))
    sspec = pl.BlockSpec((ns, 1, LANES), lambda s: (s, 0, 0))
    colspec = pl.BlockSpec((ns, length, 1), lambda s: (s, 0, 0))
    return xspecs, seqs(0), seqs(0, g), bspec, sspec, colspec


def _band_sweep(length, ns, one):
    nblk = length // BLK
    for qq in range(ns):
        if ns * nblk <= 16:
            for i in range(nblk):
                one(qq, i * BLK, max(i - 1, 0) * BLK, i == 0)
        else:
            def step(i, carry, qq=qq):
                one(qq, pl.multiple_of(i * BLK, BLK), pl.multiple_of(jnp.maximum(i - 1, 0) * BLK, BLK), i == 0)
                return carry

            lax.fori_loop(0, nblk, step, 0, unroll=2)


def _band_scores(q_ref, k_ref, b_ref, qq, kq, bq, cur, prv, first):
    qv = q_ref[qq, pl.ds(cur, BLK), :]
    bv = b_ref[bq]
    if first is True:
        sp = jnp.full((BLK, BLK), NEG, F32)
    else:
        sp = _dot(qv, k_ref[kq, pl.ds(prv, BLK), :], 1, 1) + bv[:, :BLK]
        sp = sp if first is False else jnp.where(first, NEG, sp)
    sc = _dot(qv, k_ref[kq, pl.ds(cur, BLK), :], 1, 1) + bv[:, BLK:]
    return qv, sp, sc


def _band_fwd(x, bias, sink, *, nq, offs, g, bias_div, has_sink, name):
    length = x.shape[1]
    ns = _band_layout(g, bias_div)

    def body(q_ref, k_ref, v_ref, b_ref, s_ref, o_ref, lse_ref):
        def one(qq, cur, prv, first):
            kq, bq = qq, 0
            _, sp, sc = _band_scores(q_ref, k_ref, b_ref, qq, kq, bq, cur, prv, first)
            m = jnp.maximum(jnp.max(sp, axis=1, keepdims=True), jnp.max(sc, axis=1, keepdims=True))
            if has_sink:
                sk = s_ref[qq][:, :1]
                m = jnp.maximum(m, sk)
            pp, pc = jnp.exp(sp - m), jnp.exp(sc - m)
            den = jnp.sum(pp, axis=1, keepdims=True) + jnp.sum(pc, axis=1, keepdims=True)
            if has_sink:
                den = den + jnp.exp(sk - m)
            acc = (_dot(pp.astype(BF16), v_ref[kq, pl.ds(prv, BLK), :], 1, 0)
                   + _dot(pc.astype(BF16), v_ref[kq, pl.ds(cur, BLK), :], 1, 0))
            o_ref[qq, pl.ds(cur, BLK), :] = acc / den
            lse_ref[qq, pl.ds(cur, BLK), :] = m + jnp.log(den)

        _band_sweep(length, ns, one)

    xspecs, qspec, _, bspec, sspec, colspec = _band_specs(length, g, bias_div, offs)
    return pl.pallas_call(
        body, name=name,
        out_shape=[jax.ShapeDtypeStruct((nq, length, HEAD_DIM), F32), jax.ShapeDtypeStruct((nq, length, 1), F32)],
        grid=(nq // ns,), in_specs=xspecs + [bspec, sspec],
        out_specs=[qspec, colspec], compiler_params=_params(("parallel",)),
    )(x, x, x, bias, sink)


def _band_bwd(x, bias, sink, o, lse, do, dlse, *, nq, offs, g, bias_div, has_sink, name):
    length = x.shape[1]
    ns = _band_layout(g, bias_div)
    nk, nbias = nq // g, nq // bias_div

    def body(q_ref, k_ref, v_ref, b_ref, s_ref, o_ref, lse_ref, do_ref, dlse_ref,
             dq_ref, dk_ref, dv_ref, db_ref, dsk_ref, dkp_ref, dvp_ref):
        for ref in (db_ref, dsk_ref, dkp_ref, dvp_ref):
            ref[...] = jnp.zeros_like(ref)

        @pl.when(pl.program_id(0) % g == 0)
        def _():
            dk_ref[...] = jnp.zeros_like(dk_ref)
            dv_ref[...] = jnp.zeros_like(dv_ref)

        def one(qq, cur, prv, first):
            kq, bq = qq, 0
            qv, sp, sc = _band_scores(q_ref, k_ref, b_ref, qq, kq, bq, cur, prv, first)
            rows, prow = pl.ds(cur, BLK), pl.ds(prv, BLK)
            lse_v = lse_ref[qq, rows, :]
            pp, pc = jnp.exp(sp - lse_v), jnp.exp(sc - lse_v)
            dov = do_ref[qq, rows, :]
            dob = dov.astype(BF16)
            coef = dlse_ref[qq, rows, :] - jnp.sum(dov * o_ref[qq, rows, :], axis=1, keepdims=True)
            dsp = pp * (_dot(dob, v_ref[kq, prow, :], 1, 1) + coef)
            dsc = pc * (_dot(dob, v_ref[kq, rows, :], 1, 1) + coef)
            dspb, dscb = dsp.astype(BF16), dsc.astype(BF16)
            dq_ref[qq, rows, :] = ((_dot(dspb, k_ref[kq, prow, :], 1, 0) + _dot(dscb, k_ref[kq, rows, :], 1, 0))
                                   * (HEAD_DIM ** -0.5))
            dk_ref[kq, rows, :] += _dot(dscb, qv, 0, 0)
            dkp_ref[kq, prow, :] += _dot(dspb, qv, 0, 0)
            dv_ref[kq, rows, :] += _dot(pc.astype(BF16), dob, 0, 0)
            dvp_ref[kq, prow, :] += _dot(pp.astype(BF16), dob, 0, 0)
            db_ref[bq, :, :BLK] += dsp
            db_ref[bq, :, BLK:] += dsc
            if has_sink:
                dsk_ref[qq] += jnp.sum(jnp.exp(s_ref[qq][:, :1] - lse_v) * coef)

        _band_sweep(length, ns, one)
        dk_ref[...] += dkp_ref[...]
        dv_ref[...] += dvp_ref[...]

    xspecs, qspec, kvspec, bspec, sspec, colspec = _band_specs(length, g, bias_div, offs)
    return pl.pallas_call(
        body, name=name,
        out_shape=[jax.ShapeDtypeStruct((nq, length, HEAD_DIM), F32), jax.ShapeDtypeStruct((nk, length, HEAD_DIM), F32),
                   jax.ShapeDtypeStruct((nk, length, HEAD_DIM), F32), jax.ShapeDtypeStruct((nbias, BLK, 2 * BLK), F32),
                   jax.ShapeDtypeStruct((nq, 1, LANES), F32)],
        grid=(nq // ns,),
        in_specs=xspecs + [bspec, sspec, qspec, colspec, qspec, colspec],
        out_specs=[qspec, kvspec, kvspec, bspec, sspec],
        scratch_shapes=[pltpu.VMEM((ns, length, HEAD_DIM), F32), pltpu.VMEM((ns, length, HEAD_DIM), F32)],
        compiler_params=_params(("arbitrary",)),
    )(x, x, x, bias, sink, o, lse, do, dlse)


TOK_TILE = 512


def _dil_merge(outs, lses, dout, name):
    tr = TOK_TILE
    dils = [d for _, d in DIL_PATTERNS]
    n = len(dils)
    o4 = [o.reshape(2, d, SEQ // d, HEAD_DIM) for o, d in zip(outs, dils)]
    l4 = [l.reshape(2, d, SEQ // d, 1) for l, d in zip(lses, dils)]
    o_specs = [pl.BlockSpec((2, d, tr // d, HEAD_DIM), lambda i: (0, 0, i, 0)) for d in dils]
    l_specs = [pl.BlockSpec((2, d, tr // d, 1), lambda i: (0, 0, i, 0)) for d in dils]
    tok = pl.BlockSpec((tr, 2 * HEAD_DIM), lambda i: (i, 0))
    scratch = ([pltpu.VMEM((tr, 2 * HEAD_DIM), F32) for _ in dils] + [pltpu.VMEM((tr, 1), F32) for _ in range(2 * n)]
               + [pltpu.VMEM((tr // d, 2 * HEAD_DIM), F32) for d in dils])

    def to_tokens(o_ref, l_ref, d, pair, cols, stage):
        for r in range(d):
            rows = pl.ds(r, tr // d, stride=d) if d > 1 else slice(None)
            stage[:, :HEAD_DIM] = o_ref[0, r]
            stage[:, HEAD_DIM:] = o_ref[1, r]
            pair[rows, :] = stage[...]
            for h in range(2):
                cols[h][rows, :] = l_ref[h, r]
        return pair[...], [cols[0][...], cols[1][...]]

    def weights(ls):
        left = lax.broadcasted_iota(jnp.int32, (tr, 2 * HEAD_DIM), 1) < HEAD_DIM
        per_head = []
        for h in range(2):
            m = ls[0][h]
            for g in range(1, n):
                m = jnp.maximum(m, ls[g][h])
            es = [jnp.exp(ls[g][h] - m) for g in range(n)]
            den = es[0]
            for e in es[1:]:
                den = den + e
            per_head.append([e / den for e in es])
        return per_head, [jnp.where(left, per_head[0][g], per_head[1][g]) for g in range(n)], left

    def load(refs):
        pairs, cols, stages = refs[:n], refs[n:3 * n], refs[3 * n:]
        return pairs, [cols[2 * g:2 * g + 2] for g in range(n)], stages

    if dout is None:
        def body(*refs):
            pairs, cols, stages = load(refs[2 * n + 1:])
            toks = [to_tokens(refs[g], refs[n + g], dils[g], pairs[g], cols[g], stages[g]) for g in range(n)]
            _, alphas, _ = weights([t[1] for t in toks])
            acc = alphas[0] * toks[0][0]
            for g in range(1, n):
                acc = acc + alphas[g] * toks[g][0]
            refs[2 * n][...] = acc

        return pl.pallas_call(
            body, name=name, out_shape=jax.ShapeDtypeStruct((SEQ, 2 * HEAD_DIM), F32), grid=(SEQ // tr,),
            in_specs=o_specs + l_specs, out_specs=tok, scratch_shapes=scratch, compiler_params=_params(("parallel",)),
        )(*o4, *l4)

    def body(*refs):
        do_refs, dl_refs = refs[2 * n + 1:3 * n + 1], refs[3 * n + 1:4 * n + 1]
        pairs, cols, stages = load(refs[4 * n + 1:])
        toks = [to_tokens(refs[g], refs[n + g], dils[g], pairs[g], cols[g], stages[g]) for g in range(n)]
        per_head, alphas, left = weights([t[1] for t in toks])
        dov = refs[2 * n][...]
        das = []
        for g in range(n):
            prod = dov * toks[g][0]
            das.append([jnp.sum(jnp.where(left, prod, 0.0), axis=1, keepdims=True),
                        jnp.sum(jnp.where(left, 0.0, prod), axis=1, keepdims=True)])
        dbar = [sum(per_head[h][g] * das[g][h] for g in range(n)) for h in range(2)]
        for g, d in enumerate(dils):
            pairs[g][...] = alphas[g] * dov
            for h in range(2):
                cols[g][h][...] = per_head[h][g] * (das[g][h] - dbar[h])
            for r in range(d):
                rows = pl.ds(r, tr // d, stride=d) if d > 1 else slice(None)
                v = pairs[g][rows, :]
                for h in range(2):
                    do_refs[g][h, r] = v[:, h * HEAD_DIM:(h + 1) * HEAD_DIM]
                    dl_refs[g][h, r] = cols[g][h][rows, :]

    out = pl.pallas_call(
        body, name=name,
        out_shape=[jax.ShapeDtypeStruct(o.shape, F32) for o in o4] + [jax.ShapeDtypeStruct(l.shape, F32) for l in l4],
        grid=(SEQ // tr,), in_specs=o_specs + l_specs + [tok], out_specs=o_specs + l_specs, scratch_shapes=scratch,
        compiler_params=_params(("parallel",)),
    )(*o4, *l4, dout)
    return [t.reshape(s.shape) for t, s in zip(out, list(outs) + list(lses))]


def _tri(cmp):
    r = lax.broadcasted_iota(jnp.int32, (SB_TILE, SB_TILE), 0)
    c = lax.broadcasted_iota(jnp.int32, (SB_TILE, SB_TILE), 1)
    return cmp(r, c).astype(BF16)


def _cum(x, tri, terms):
    acc, rest = None, x
    for _ in range(terms):
        part = rest.astype(BF16)
        rest = rest - part.astype(F32)
        d = _dot(part, tri, 1, 0)
        acc = d if acc is None else acc + d
    return acc


def _sb_logits(q, ks, diagonal):
    t = SB_TILE
    z = _dot(q, ks, 1, 1)
    e = jnp.exp(-jnp.abs(z))
    lf = -(jnp.maximum(z, 0.0) + jnp.log(1.0 + e))
    if not diagonal:
        return z, e, lf, None
    mask = lax.broadcasted_iota(jnp.int32, (t, t), 1) < lax.broadcasted_iota(jnp.int32, (t, t), 0)
    return z, e, jnp.where(mask, lf, 0.0), mask


def _sb_specs(h, s):
    t = SB_TILE
    tile = pl.BlockSpec((h, t, HEAD_DIM), lambda i: (0, i, 0))
    keys = pl.BlockSpec((h, s, HEAD_DIM), lambda i: (1, 0, 0))
    values = pl.BlockSpec((h, s, HEAD_DIM), lambda i: (2, 0, 0))
    return tile, keys, values, pl.BlockSpec((h, t, 1), lambda i: (0, i, 0))


def _sb_fwd(x, name):
    h, s = x.shape[0] // 3, x.shape[1]
    t = SB_TILE

    def body(q_ref, k_ref, v_ref, o_ref, tot_ref):
        i = pl.program_id(0)
        after = _tri(lambda r, c: r > c)

        def tile(j, carry, diagonal):
            rows = pl.ds(pl.multiple_of(j * t, t), t)
            out = []
            for hh, (right, acc) in enumerate(carry):
                z, _, lf, mask = _sb_logits(q_ref[hh], k_ref[hh, rows, :], diagonal)
                w = jnp.exp(z + lf + (right + _cum(lf, after, 2)))
                w = w if mask is None else jnp.where(mask, w, 0.0)
                out.append((right + jnp.sum(lf, axis=1, keepdims=True), acc + _dot(w.astype(BF16), v_ref[hh, rows, :], 1, 0)))
            return tuple(out)

        carry = tile(i, tuple((jnp.zeros((t, 1), F32), jnp.zeros((t, HEAD_DIM), F32)) for _ in range(h)), True)
        carry = lax.fori_loop(0, i, lambda jj, c: tile(i - 1 - jj, c, False), carry)
        for hh, (right, acc) in enumerate(carry):
            o_ref[hh] = acc
            tot_ref[hh] = right

    tile_spec, keys, values, col = _sb_specs(h, s)
    return pl.pallas_call(
        body, name=name, out_shape=[jax.ShapeDtypeStruct((h, s, HEAD_DIM), F32), jax.ShapeDtypeStruct((h, s, 1), F32)],
        grid=(s // t,), in_specs=[tile_spec, keys, values], out_specs=[tile_spec, col],
        compiler_params=_params(("parallel",)),
    )(x, x, x)


def _sb_bwd(x, tot, do, name):
    h, s = x.shape[0] // 3, x.shape[1]
    t = SB_TILE

    def body(q_ref, k_ref, v_ref, tot_ref, do_ref, dq_ref, dk_ref, dv_ref):
        i = pl.program_id(0)

        @pl.when(i == 0)
        def _():
            dk_ref[...] = jnp.zeros_like(dk_ref)
            dv_ref[...] = jnp.zeros_like(dv_ref)

        upto = _tri(lambda r, c: r <= c)
        before = _tri(lambda r, c: r < c)

        def tile(j, carry, diagonal):
            rows = pl.ds(pl.multiple_of(j * t, t), t)
            out = []
            for hh, (left, cleft, dq) in enumerate(carry):
                qv, ks, dob = q_ref[hh], k_ref[hh, rows, :], do_ref[hh].astype(BF16)
                z, e, lf, mask = _sb_logits(qv, ks, diagonal)
                between = tot_ref[hh] - (left + _cum(lf, upto, 2))
                w = jnp.exp(z + lf + between)
                w = w if mask is None else jnp.where(mask, w, 0.0)
                dlog = w * _dot(dob, v_ref[hh, rows, :], 1, 1)
                cfail = cleft + _cum(dlog, before, 2)
                sig = jnp.where(z >= 0.0, 1.0, e) / (1.0 + e)
                dz = dlog * (1.0 - sig) - sig * cfail
                dz = (dz if mask is None else jnp.where(mask, dz, 0.0)).astype(BF16)
                dk_ref[hh, rows, :] += _dot(dz, qv, 0, 0)
                dv_ref[hh, rows, :] += _dot(w.astype(BF16), dob, 0, 0)
                out.append((left + jnp.sum(lf, axis=1, keepdims=True), cleft + jnp.sum(dlog, axis=1, keepdims=True),
                            dq + _dot(dz, ks, 1, 0)))
            return tuple(out)

        zero = jnp.zeros((t, 1), F32)
        carry = lax.fori_loop(0, i, lambda j, c: tile(j, c, False),
                              tuple((zero, zero, jnp.zeros((t, HEAD_DIM), F32)) for _ in range(h)))
        for hh, (_, _, dq) in enumerate(tile(i, carry, True)):
            dq_ref[hh] = dq * (HEAD_DIM ** -0.5)

    tile_spec, keys, values, col = _sb_specs(h, s)
    full = pl.BlockSpec((h, s, HEAD_DIM), lambda i: (0, 0, 0))
    shp = jax.ShapeDtypeStruct((h, s, HEAD_DIM), F32)
    return pl.pallas_call(
        body, name=name, out_shape=[shp, shp, shp], grid=(s // t,),
        in_specs=[tile_spec, keys, values, col, tile_spec],
        out_specs=[tile_spec, full, full], compiler_params=_params(("arbitrary",)),
    )(x, x, x, tot, do)


COL_SB, COL_DIL, COL_SWA = 0, 3 * H_SB * HEAD_DIM, 3 * H_SB * HEAD_DIM + 3 * H_DIL * HEAD_DIM
N_SWA = H_SWA_Q + 2 * H_SWA_KV


def _dil_col(t, g):
    return COL_DIL + t * H_DIL * HEAD_DIM + g * 2 * HEAD_DIM


def _split_heads(qkv, name):
    tr = TOK_TILE
    scale = HEAD_DIM ** -0.5
    dils = [d for _, d in DIL_PATTERNS]

    def body(x_ref, sb_ref, d0_ref, d1_ref, d2_ref, swa_ref, pair):
        def head(col, scaled):
            v = x_ref[:, col:col + HEAD_DIM]
            return (v * scale if scaled else v).astype(BF16)

        for hh in range(3 * H_SB):
            sb_ref[hh] = head(COL_SB + hh * HEAD_DIM, hh < H_SB)
        for hh in range(N_SWA):
            swa_ref[hh] = head(COL_SWA + hh * HEAD_DIM, hh < H_SWA_Q)
        for t in range(3):
            for g, (d, out_ref) in enumerate(zip(dils, (d0_ref, d1_ref, d2_ref))):
                col = _dil_col(t, g)
                if d == 1:
                    for h in range(2):
                        out_ref[t * 2 + h] = head(col + h * HEAD_DIM, t == 0)
                    continue
                pair[...] = x_ref[:, col:col + 2 * HEAD_DIM]
                for r in range(d):
                    v = pair[pl.ds(r, tr // d, stride=d), :]
                    v = v * scale if t == 0 else v
                    for h in range(2):
                        out_ref[t * 2 * d + h * d + r] = v[:, h * HEAD_DIM:(h + 1) * HEAD_DIM].astype(BF16)

    def heads(n, length):
        return jax.ShapeDtypeStruct((n, length, HEAD_DIM), BF16)

    def spec(n, rows):
        return pl.BlockSpec((n, rows, HEAD_DIM), lambda i: (0, i, 0))

    return pl.pallas_call(
        body, name=name,
        out_shape=[heads(3 * H_SB, SEQ)] + [heads(6 * d, SEQ // d) for d in dils] + [heads(N_SWA, SEQ)],
        grid=(SEQ // tr,), in_specs=[pl.BlockSpec((tr, D_QKV), lambda i: (i, 0))],
        out_specs=[spec(3 * H_SB, tr)] + [spec(6 * d, tr // d) for d in dils] + [spec(N_SWA, tr)],
        scratch_shapes=[pltpu.VMEM((tr, 2 * HEAD_DIM), F32)], compiler_params=_params(("parallel",)),
    )(qkv)


def _join_heads(sb, dil, swa, name):
    tr = TOK_TILE
    dils = [d for _, d in DIL_PATTERNS]

    def body(*refs):
        sb_refs, dil_refs, swa_refs = refs[:3], [refs[3 + 3 * g:6 + 3 * g] for g in range(3)], refs[12:15]
        o_ref, pair, stages = refs[15], refs[16], refs[17:]

        def put(col, v):
            o_ref[:, col:col + v.shape[1]] = v.astype(BF16)

        for t in range(3):
            for h in range(H_SB):
                put(COL_SB + (t * H_SB + h) * HEAD_DIM, sb_refs[t][h])
        col = COL_SWA
        for ref in swa_refs:
            for h in range(ref.shape[0]):
                put(col, ref[h])
                col += HEAD_DIM
        for t in range(3):
            for g, d in enumerate(dils):
                ref, col = dil_refs[g][t], _dil_col(t, g)
                if d == 1:
                    for h in range(2):
                        put(col + h * HEAD_DIM, ref[h])
                    continue
                stage = stages[g - 1]
                for r in range(d):
                    stage[:, :HEAD_DIM] = ref[r]
                    stage[:, HEAD_DIM:] = ref[d + r]
                    pair[pl.ds(r, tr // d, stride=d), :] = stage[...]
                put(col, pair[...])

    def spec(n, rows):
        return pl.BlockSpec((n, rows, HEAD_DIM), lambda i: (0, i, 0))

    ins = list(sb) + [t for g in range(3) for t in dil[g]] + list(swa)
    in_specs = ([spec(H_SB, tr)] * 3 + [spec(2 * d, tr // d) for d in dils for _ in range(3)]
                + [spec(H_SWA_Q, tr), spec(H_SWA_KV, tr), spec(H_SWA_KV, tr)])
    return pl.pallas_call(
        body, name=name, out_shape=jax.ShapeDtypeStruct((SEQ, D_QKV), BF16), grid=(SEQ // tr,), in_specs=in_specs,
        out_specs=pl.BlockSpec((tr, D_QKV), lambda i: (i, 0)),
        scratch_shapes=[pltpu.VMEM((tr, 2 * HEAD_DIM), F32)] + [pltpu.VMEM((tr // d, 2 * HEAD_DIM), F32) for d in dils[1:]],
        compiler_params=_params(("parallel",)),
    )(*ins)


def _mixer_fwd(qkv, bias, sinks_l, tag):
    sb, d0, d1, d2, swa = _split_heads(qkv, name=f"split_heads_{tag}")
    st = {"sb": sb, "dil": (d0, d1, d2), "swa": swa}
    o_sb, st["sb_tot"] = _sb_fwd(sb, name=f"sb_fwd_{tag}")
    st["dil_out"], st["dil_lse"], st["dil_sink"] = [], [], []
    for gi, (_, d) in enumerate(DIL_PATTERNS):
        sink = jnp.zeros((2 * d, 1, LANES), F32)
        og, lg = _band_fwd(st["dil"][gi], bias[2 * gi:2 * gi + 2], sink, nq=2 * d, offs=(0, 2 * d, 4 * d), g=1, bias_div=d,
                           has_sink=False, name=f"dil{gi}_fwd_{tag}")
        st["dil_out"].append(og)
        st["dil_lse"].append(lg)
        st["dil_sink"].append(sink)
    o_dil = _dil_merge(st["dil_out"], st["dil_lse"], None, name=f"dil_merge_fwd_{tag}")
    st["swa_sink"] = jnp.broadcast_to(sinks_l.reshape(H_SWA_Q, 1, 1), (H_SWA_Q, 1, LANES))
    st["swa_out"] = _band_fwd(swa, bias[H_DIL:], st["swa_sink"], nq=H_SWA_Q, offs=(0, H_SWA_Q, H_SWA_Q + H_SWA_KV),
                              g=H_SWA_Q // H_SWA_KV, bias_div=1, has_sink=True, name=f"swa_fwd_{tag}")
    return (o_sb, o_dil, st["swa_out"][0]), st


def _mixer_bwd(st, bias, do_sb, do_dil, do_swa, tag):
    d_sb = _sb_bwd(st["sb"], st["sb_tot"], do_sb, name=f"sb_bwd_{tag}")
    dmerge = _dil_merge(st["dil_out"], st["dil_lse"], do_dil, name=f"dil_merge_bwd_{tag}")
    d_dil, dbs = [], []
    for gi, (_, d) in enumerate(DIL_PATTERNS):
        dq, dk, dv, db, _ = _band_bwd(st["dil"][gi], bias[2 * gi:2 * gi + 2], st["dil_sink"][gi], st["dil_out"][gi],
                                      st["dil_lse"][gi], dmerge[gi], dmerge[3 + gi], nq=2 * d, offs=(0, 2 * d, 4 * d),
                                      g=1, bias_div=d, has_sink=False, name=f"dil{gi}_bwd_{tag}")
        d_dil.append((dq, dk, dv))
        dbs.append(db)
    o_sw, l_sw = st["swa_out"]
    dq_sw, dk_sw, dv_sw, db_sw, dsink = _band_bwd(st["swa"], bias[H_DIL:], st["swa_sink"], o_sw, l_sw, do_swa,
                                                  jnp.zeros_like(l_sw), nq=H_SWA_Q, offs=(0, H_SWA_Q, H_SWA_Q + H_SWA_KV),
                                                  g=H_SWA_Q // H_SWA_KV, bias_div=1, has_sink=True, name=f"swa_bwd_{tag}")
    dqkv = _join_heads(d_sb, d_dil, (dq_sw, dk_sw, dv_sw), name=f"join_heads_{tag}")
    return dqkv, jnp.concatenate(dbs + [db_sw], 0), dsink[:, 0, 0]


PIECES = ("ffn0", "mix", "ffn1")


def _ffn_fwd(x_in, w, gain, mod_j, tag, after=None):
    st = {"x": x_in, "w": w}
    st["h"] = _norm_fwd(x_in, _row(gain), _row(mod_j[1]), _row(mod_j[0]), name=f"norm_fwd_{tag}", after=after)
    st["a"], st["u"], st["s"] = _ffn_up(st["h"], w["gate"], w["up"], name=f"up_{tag}")
    st["f"], x_out = _mm(st["s"], w["down"], res=x_in, colscale=_row(0.5 * mod_j[2]), emit_acc=True, tm=512, tn=1024,
                         name=f"down_{tag}")
    return x_out, st


def _ffn_bwd(dx_out, st, gain, mod_j, tag, done):
    w = st["w"]

    def latest(new, old):
        return old if new is None else new

    df, dgate = _gate_bwd(dx_out, st["f"], _row(0.5 * mod_j[2]), 0.5, name=f"gate_bwd_{tag}")
    token = done({"down": _mm_tn(st["s"], df, tm=D_FF // 2, name=f"dwd_{tag}")})
    da, du = _ffn_bwd_ds(df, w["down"], st["a"], st["u"], name=f"ds_{tag}")
    token = latest(done({"gate": _mm_tn(da, st["h"], after=token, tm=D_FF // 2, name=f"dwg_{tag}")}), token)
    token = latest(done({"up": _mm_tn(du, st["h"], after=token, tm=D_FF // 2, name=f"dwu_{tag}")}), token)
    dh = _mm2(da, w["gate"], du, w["up"], after=token, name=f"dh_{tag}")
    dx_in, sum_dh, sum_dhx = _norm_bwd(st["x"], dh, dx_out, _row(gain), _row(mod_j[1]), name=f"norm_bwd_{tag}")
    dmod = jnp.concatenate([sum_dh, gain * sum_dhx, dgate], 0)
    return dx_in, dmod, (1.0 + mod_j[1]) * sum_dhx[0]


def _mix_fwd(x_in, w, gain, mod_j, bias, sinks_l, tag, after=None):
    st = {"x": x_in, "w": w}
    st["h"] = _norm_fwd(x_in, _row(gain), _row(mod_j[1]), _row(mod_j[0]), name=f"norm_fwd_mix_{tag}", after=after)
    qkv = _mm(st["h"], w["qkv"], tb=True, tm=SEQ, name=f"qkv_{tag}")
    st["gates"] = _mm(st["h"], w["gates"], tb=True, tm=SEQ, name=f"gates_{tag}")
    outs, st["mix"] = _mixer_fwd(qkv, bias, sinks_l, tag)
    st["merged"], *st["t"] = _merge_fwd(*outs, st["gates"], w["br_sb"], w["br_dil"], w["br_swa"], name=f"merge_fwd_{tag}")
    st["f"], x_out = _mm(st["merged"], w["out"], res=x_in, colscale=_row(mod_j[2]), emit_acc=True, name=f"out_{tag}")
    return x_out, st


def _mix_bwd(dx_out, st, gain, mod_j, bias, tag, done):
    w = st["w"]
    df, dgate = _gate_bwd(dx_out, st["f"], _row(mod_j[2]), 1.0, name=f"gate_bwd_mix_{tag}")
    g = {"out": _mm_tn(st["merged"], df, name=f"dw_out_{tag}")}
    dmerged = _mm(df, w["out"], tb=True, name=f"dmerged_{tag}")
    dgates, do_sb, do_dil, do_swa, dbr_sb, dbr_dil, dbr_swa = _merge_bwd(
        dmerged, *st["t"], st["gates"], w["br_sb"], w["br_dil"], w["br_swa"], name=f"merge_bwd_{tag}")
    g["br_sb"] = _mm_tn(st["t"][0], dbr_sb, name=f"dw_br_sb_{tag}")
    g["br_dil"] = _mm_tn(st["t"][1], dbr_dil, name=f"dw_br_dil_{tag}")
    g["br_swa"] = _mm_tn(st["t"][2], dbr_swa, name=f"dw_br_swa_{tag}")
    dqkv, dbias, dsinks = _mixer_bwd(st["mix"], bias, do_sb, do_dil, do_swa, tag)
    g["qkv"] = _mm_tn(dqkv, st["h"], name=f"dw_qkv_{tag}")
    g["gates"] = _mm_tn(dgates, st["h"], name=f"dw_gates_{tag}")
    dh = _mm2(dqkv, w["qkv"], dgates, w["gates"], after=done(g), tm=512, name=f"dh_mix_{tag}")
    dx_in, sum_dh, sum_dhx = _norm_bwd(st["x"], dh, dx_out, _row(gain), _row(mod_j[1]), name=f"norm_bwd_mix_{tag}")
    dmod = jnp.concatenate([sum_dh, gain * sum_dhx, dgate], 0)
    return dx_in, dmod, (1.0 + mod_j[1]) * sum_dhx[0], dbias, dsinks


def _local_step(x, target, mod, gains, weights_of, rel_bias, sinks, final_gain, grads_done):
    tables = jnp.asarray(_bucket_tables())
    bias = _bias_build(rel_bias, tables, name="bias_build")
    states, h = [], x
    for l in range(DEPTH):
        st = {}
        for j, piece in enumerate(PIECES):
            w, after = weights_of(l, piece, h)
            if piece == "mix":
                h, st[piece] = _mix_fwd(h, w, gains[l, j], mod[l, j], bias, sinks[l], f"l{l}", after)
            else:
                h, st[piece] = _ffn_fwd(h, w, gains[l, j], mod[l, j], f"{piece}_l{l}", after)
        states.append(st)
    loss, dx, dfinal = _final_loss(h, target, _row(final_gain), name="final_loss")
    dmods = [[None] * 3 for _ in range(DEPTH)]
    dgains = [[None] * 3 for _ in range(DEPTH)]
    dsinks = [None] * DEPTH
    dbias = None
    for l in reversed(range(DEPTH)):
        for j in reversed(range(3)):
            piece = PIECES[j]
            done = lambda grads, l=l, piece=piece: grads_done(l, piece, grads)
            if piece == "mix":
                dx, dmods[l][j], dgains[l][j], db, dsinks[l] = _mix_bwd(dx, states[l][piece], gains[l, j], mod[l, j], bias, f"l{l}", done)
                dbias = db if dbias is None else dbias + db
            else:
                dx, dmods[l][j], dgains[l][j] = _ffn_bwd(dx, states[l][piece], gains[l, j], mod[l, j], f"{piece}_l{l}", done)
    drel = _bias_grad(dbias, tables, name="bias_grad")[:, 0, :N_BUCKETS].T
    dmod = jnp.stack([jnp.stack(m) for m in dmods])
    dgain = jnp.stack([jnp.stack(g) for g in dgains])
    return loss, dx, dmod, dgain, dfinal[0], drel, jnp.stack(dsinks)


BR_ROWS = (H_SB * HEAD_DIM, 2 * HEAD_DIM, H_SWA_Q * HEAD_DIM)


def _lanes_unshard(g, lead):
    _, rows, _ = g.shape
    r = rows // lead
    return g.reshape(N_DEV, lead, r, LANES).transpose(1, 2, 0, 3).reshape(lead, r, N_DEV * LANES)


def _lanes_shard(full):
    lead, r, _ = full.shape
    return full.reshape(lead, r, N_DEV, LANES).transpose(2, 0, 1, 3).reshape(N_DEV, lead * r, LANES)


def _pack_rows(parts, dtype):
    flat = jnp.concatenate([p.astype(dtype).reshape(-1) for p in parts])
    pad = (-flat.shape[0]) % (16 * LANES)
    if pad:
        flat = jnp.concatenate([flat, jnp.zeros((pad,), dtype)])
    return flat.reshape(-1, LANES)


def _unshard(gathered, axis):
    moved = jnp.moveaxis(gathered, 0, axis)
    shape = list(moved.shape)
    shape[axis:axis + 2] = [shape[axis] * shape[axis + 1]]
    return moved.reshape(shape)


def kernel(x, c, w_ada, b_ada, norm_gain, w_ffn_gate, w_ffn_up, w_ffn_down, w_in, w_br_sb, w_br_dil, w_br_swa, w_out, sinks, rel_bias, final_gain, loss_target, m_w_ada, m_b_ada, m_norm_gain, m_w_ffn_gate, m_w_ffn_up, m_w_ffn_down, m_w_in, m_w_br_sb, m_w_br_dil, m_w_br_swa, m_w_out, m_sinks, m_rel_bias, m_final_gain, v_w_ada, v_b_ada, v_norm_gain, v_w_ffn_gate, v_w_ffn_up, v_w_ffn_down, v_w_in, v_w_br_sb, v_w_br_dil, v_w_br_swa, v_w_out, v_sinks, v_rel_bias, v_final_gain):
    me = 4 * lax.axis_index("x") + 2 * lax.axis_index("y") + lax.axis_index("c")
    d = D_MODEL
    gate_t, up_t, in_t = jnp.swapaxes(w_ffn_gate, 2, 3), jnp.swapaxes(w_ffn_up, 2, 3), jnp.swapaxes(w_in, 1, 2)

    def piece_shards(l, piece):
        bf = lambda t: t.astype(BF16)
        if piece == "mix":
            return [bf(in_t[l]), jnp.concatenate([bf(w_br_sb[l]), bf(w_br_dil[l]), bf(w_br_swa[l])], 0), bf(w_out[l])]
        i = PIECES.index(piece) // 2
        return [bf(gate_t[l, i]), bf(up_t[l, i]), bf(w_ffn_down[l, i])]

    br_off = np.concatenate([[0], np.cumsum(BR_ROWS)])

    def piece_weights(gathered, piece):
        if piece == "mix":
            g_in, g_br, g_out = gathered
            f_in = g_in.reshape(D_QKV + D_GATES, d)
            f_br = [_lanes_unshard(g_br[:, br_off[k]:br_off[k + 1]], 1)[0] for k in range(3)]
            return {"qkv": f_in[:D_QKV], "gates": f_in[D_QKV:], "br_sb": f_br[0], "br_dil": f_br[1], "br_swa": f_br[2],
                    "out": g_out.reshape(d, d)}
        return {n: g.reshape(D_FF, d) for n, g in zip(("gate", "up", "down"), gathered)}

    small, = _all_gather([_pack_rows([c, norm_gain], F32)], name="gather_cond")
    c_all = small[:, :d // LANES].reshape(N_DEV, d)
    gains = _unshard(small[:, d // LANES:d // LANES + 6].reshape(N_DEV, DEPTH, 3, LANES), 2)

    cols = w_ada.shape[2]
    mod_cols = jnp.stack([_ada_fwd(c_all, w_ada[l], name=f"ada_fwd_l{l}") for l in range(DEPTH)])
    mod_all, = _all_gather([_pack_rows([mod_cols], F32)], name="gather_mod")
    mod_all = mod_all.reshape(N_DEV, -1)[:, :DEPTH * N_DEV * cols].reshape(N_DEV, DEPTH, N_DEV, cols)
    mod_mine = lax.dynamic_index_in_dim(mod_all, me, axis=2, keepdims=False)
    mod = (mod_mine.transpose(1, 0, 2).reshape(DEPTH, N_DEV * cols) + b_ada).reshape(DEPTH, 3, 3, d)

    order = [(l, piece) for l in range(DEPTH) for piece in PIECES]
    eager, ahead = 2, 3
    in_flight = {}
    n_tensors = 3
    first = _all_gather([s for k in range(eager) for s in piece_shards(*order[k])], after=mod_all, name="gather_first")

    def start_gather(k, after):
        l, piece = order[k]
        in_flight[k], token = _exchange_start(piece_shards(l, piece), after, gather=True, name=f"gather_{piece}_l{l}_start")
        return token

    token = first[0]
    for k in range(eager, eager + ahead - 1):
        token = start_gather(k, token)
    mod = mod + token[0, 0]

    def weights_of(l, piece, h):
        k = order.index((l, piece))
        started = eager <= k + ahead < len(order) and k + ahead not in in_flight
        token = start_gather(k + ahead, h) if started else None
        if k < eager:
            return piece_weights(first[n_tensors * k:n_tensors * (k + 1)], piece), token
        landed = _exchange_wait(in_flight[k], h if token is None else token, gather=True, name=f"gather_{piece}_l{l}_wait")
        return piece_weights(landed, piece), token

    exchanges, have = {}, {}

    def grads_done(l, piece, g):
        key = (l, piece)
        have.setdefault(key, {}).update(g)
        if piece == "mix":
            if len(have[key]) < 6:
                return None
            g = have[key]
            s_br = jnp.concatenate([_lanes_shard(g[n][None]) for n in ("br_sb", "br_dil", "br_swa")], 1)
            groups = [(("in", "br", "out"), [jnp.concatenate([g["qkv"], g["gates"]], 0).reshape(N_DEV, -1, d), s_br,
                                             g["out"].reshape(N_DEV, -1, d)])]
        elif key == order[0]:
            groups = [((n,), [t.reshape(N_DEV, -1, d)]) for n, t in g.items()]
        elif len(have[key]) < 3:
            return None
        else:
            groups = [(("gate", "up", "down"), [have[key][n].reshape(N_DEV, -1, d) for n in ("gate", "up", "down")])]
        token = None
        for names, sg in groups:
            state, token = _exchange_start(sg, sg[0], gather=False, name=f"exchange_{piece}_l{l}_{names[0]}_start")
            exchanges.setdefault(key, []).append((names, state))
        return token

    loss, dx, dmod, dgains, dfinal, drel, dsinks = _local_step(
        x[0], loss_target[0], mod, gains, weights_of, rel_bias, sinks, final_gain, grads_done)

    flat = lambda t: t.reshape(-1, t.shape[-1])
    transposed = lambda ts: tuple(flat(jnp.swapaxes(t, -1, -2)) for t in ts)
    families = {
        "gate": transposed((w_ffn_gate, m_w_ffn_gate, v_w_ffn_gate)), "up": transposed((w_ffn_up, m_w_ffn_up, v_w_ffn_up)),
        "down": tuple(flat(t) for t in (w_ffn_down, m_w_ffn_down, v_w_ffn_down)),
        "in": transposed((w_in, m_w_in, v_w_in)),
        "br": tuple(flat(jnp.concatenate(ts, 1)) for ts in ((w_br_sb, w_br_dil, w_br_swa), (m_w_br_sb, m_w_br_dil, m_w_br_swa),
                                                            (v_w_br_sb, v_w_br_dil, v_w_br_swa))),
        "out": tuple(flat(t) for t in (w_out, m_w_out, v_w_out))}
    parts, stepped = {}, {}

    def land(l, after):
        for key in reversed([k for k in order if k[0] == l]):
            for names, ex_state in exchanges[key]:
                landed = _exchange_wait(ex_state, after, gather=False, name=f"exchange_{key[1]}_l{key[0]}_{names[0]}_wait")
                parts.setdefault(key, {}).update(zip(names, landed))
                after = landed[0]

    def step_layer(l):
        last = None
        for n, (w2, m2, v2) in families.items():
            groups = [parts[key][n] for key in order if key[0] == l and n in parts[key]]
            rows_per_layer = w2.shape[0] // DEPTH
            stepped[n] = _reduce_adamw(groups, w2, m2, v2, l * rows_per_layer, stepped.get(n), after=last,
                                       name=f"reduce_adamw_{n}_l{l}")
            last = stepped[n][1]
        return last

    land(1, dx)
    after_l1 = step_layer(1)

    small_parts = [dmod, dgains, dfinal, drel.T, dsinks, loss[0, :1]]
    small_sizes = [int(np.prod(p.shape)) for p in small_parts]
    small_all, = _all_gather([_pack_rows(small_parts, F32)], after=after_l1, name="gather_small")
    small_sum = _sum_parts([small_all], name="sum_small").reshape(-1)
    offs = np.concatenate([[0], np.cumsum(small_sizes)])
    g_b_ada = small_sum[offs[0]:offs[1]].reshape(DEPTH, 9 * d)
    g_gain_full = small_sum[offs[1]:offs[2]].reshape(DEPTH, 3, d)
    g_norm_gain = lax.dynamic_slice_in_dim(g_gain_full, me * LANES, LANES, axis=2)
    g_final = small_sum[offs[2]:offs[3]]
    g_rel = small_sum[offs[3]:offs[4]].reshape(N_SOFT, N_BUCKETS).T
    g_sinks = small_sum[offs[4]:offs[5]].reshape(DEPTH, H_SWA_Q)
    loss_total = small_sum[offs[5]]

    dmod_all = small_all.reshape(N_DEV, -1)[:, :DEPTH * 9 * d].reshape(N_DEV, DEPTH, 9 * d)
    dmod_cols = lax.dynamic_slice_in_dim(dmod_all, me * cols, cols, axis=2)
    g_w_ada = jnp.stack([_ada_bwd(c_all.T, dmod_cols[:, l], name=f"ada_bwd_l{l}") for l in range(DEPTH)])

    small_state = {"w_ada": (w_ada, m_w_ada, v_w_ada), "b_ada": (b_ada, m_b_ada, v_b_ada),
                   "norm_gain": (norm_gain, m_norm_gain, v_norm_gain), "sinks": (sinks, m_sinks, v_sinks),
                   "rel_bias": (rel_bias, m_rel_bias, v_rel_bias), "final_gain": (final_gain, m_final_gain, v_final_gain)}
    grad, update = {}, {}
    for n, g in (("w_ada", g_w_ada), ("b_ada", g_b_ada), ("norm_gain", g_norm_gain), ("sinks", g_sinks),
                 ("rel_bias", g_rel), ("final_gain", g_final)):
        w, m, v = small_state[n]
        grad[n] = g
        if w.ndim == 1:
            update[n] = tuple(t.reshape(w.shape) for t in _adamw(_row(w), _row(g), _row(m), _row(v), name=f"adamw_{n}"))
        else:
            update[n] = _adamw(w, g, m, v, name=f"adamw_{n}")

    land(0, update["w_ada"][0])
    step_layer(0)

    def unflat(n, like, swapped):
        shape = jnp.swapaxes(like, -1, -2).shape if swapped else like.shape
        out = [t.reshape(shape) for t in stepped[n]]
        return [jnp.swapaxes(t, -1, -2) for t in out] if swapped else out

    results = {"w_ffn_gate": unflat("gate", w_ffn_gate, True), "w_ffn_up": unflat("up", w_ffn_up, True),
               "w_ffn_down": unflat("down", w_ffn_down, False), "w_in": unflat("in", w_in, True),
               "w_out": unflat("out", w_out, False)}
    br = [t.reshape(DEPTH, -1, LANES) for t in stepped["br"]]
    for k, n in enumerate(("w_br_sb", "w_br_dil", "w_br_swa")):
        results[n] = [t[:, br_off[k]:br_off[k + 1]] for t in br]
    for n, (g, dl, nm, nv) in results.items():
        grad[n], update[n] = g, (dl, nm, nv)

    names = ["w_ada", "b_ada", "norm_gain", "w_ffn_gate", "w_ffn_up", "w_ffn_down", "w_in", "w_br_sb", "w_br_dil",
             "w_br_swa", "w_out", "sinks", "rel_bias", "final_gain"]
    return (loss_total, dx[None], *[grad[n] for n in names], *[update[n][0] for n in names],
            *[update[n][1] for n in names], *[update[n][2] for n in names])
```

```python
import math

import numpy as np
import jax
import jax.numpy as jnp
from jax import lax
from jax.experimental import pallas as pl
from jax.experimental.pallas import tpu as pltpu

F32, BF16 = jnp.float32, jnp.bfloat16

SEQ, D_MODEL, D_FF, HEAD_DIM = 2048, 1024, 2816, 64
DEPTH = 2
BLK = 128
H_SB, H_DIL, H_SWA_Q, H_SWA_KV = 4, 6, 6, 2
DIL_PATTERNS = ((128, 1), (512, 4), (2048, 16))
SWA_WINDOW = 128
N_BUCKETS, MAX_REL_DIST = 32, 2048
RMS_EPS = 1e-6
D_QKV = 2560
D_GATES = 3 * D_MODEL
ADAM_LR, ADAM_B1, ADAM_B2, ADAM_EPS, ADAM_WD, ADAM_STEP = 0.001, 0.9, 0.999, 1e-08, 0.01, 10

N_DEV = 8
LANES = 128
NEG = -1e30
SB_TILE = 512
VMEM_LIMIT_BYTES = 48 * 1024 * 1024
HBM = pl.BlockSpec(memory_space=pltpu.HBM)
MESH = pl.DeviceIdType.MESH


def _tile(n, target):
    t = (min(n, target) // LANES) * LANES
    while t >= LANES:
        if n % t == 0:
            return t
        t -= LANES
    return n


def _row_tile(r, cap):
    t = (min(r, cap) // 16) * 16
    while t > 16 and r % t:
        t -= 16
    return t


def _params(semantics=None):
    return pltpu.CompilerParams(dimension_semantics=semantics, vmem_limit_bytes=VMEM_LIMIT_BYTES)


def _dot(a, b, ca, cb):
    return lax.dot_general(a, b, (((ca,), (cb,)), ((), ())), preferred_element_type=F32)


def _sigmoid(a):
    return 1.0 / (1.0 + jnp.exp(-a))


def _row(v):
    return v.reshape(1, -1)


def _all_gather(arrs, name, after=None):
    n = len(arrs)
    ins = list(arrs) + ([] if after is None else [after])

    def body(*refs):
        x_refs, out_refs = refs[:n], refs[len(ins):len(ins) + n]
        send_sems, recv_sems, local_sems = refs[len(ins) + n:]
        x, y, c = lax.axis_index("x"), lax.axis_index("y"), lax.axis_index("c")
        me, sibling = (x, y, c), (x, y, 1 - c)
        chips = [(1 - x, y), (x, 1 - y), (1 - x, 1 - y)]

        def slot(t, px, py, pc):
            return out_refs[t].at[4 * px + 2 * py + pc]

        def copy(t, k, block, to, src=None):
            return pltpu.make_async_remote_copy(
                src_ref=slot(t, *block) if src is None else src, dst_ref=slot(t, *block),
                send_sem=send_sems.at[7 * t + k], recv_sem=recv_sems.at[7 * t + k], device_id=to, device_id_type=MESH)

        mine = [pltpu.make_async_copy(x_refs[t], slot(t, *me), local_sems.at[t]) for t in range(n)]
        for cp in mine:
            cp.start()
        first = []
        for t in range(n):
            first.append(copy(t, 0, me, sibling, src=x_refs[t]))
            first += [copy(t, 1 + j, me, (*chip, c), src=x_refs[t]) for j, chip in enumerate(chips)]
        for cp in first:
            cp.start()
        passed = []
        for j, chip in enumerate(chips):
            for t in range(n):
                copy(t, 1 + j, (*chip, c), me).wait_recv()
                passed.append(copy(t, 4 + j, (*chip, c), sibling))
                passed[-1].start()
        for t in range(n):
            copy(t, 0, sibling, me).wait_recv()
        for j, chip in enumerate(chips):
            for t in range(n):
                copy(t, 4 + j, (*chip, 1 - c), me).wait_recv()
        for cp in first + passed:
            cp.wait_send()
        for cp in mine:
            cp.wait()

    return pl.pallas_call(
        body, name=name, out_shape=[jax.ShapeDtypeStruct((N_DEV,) + a.shape, a.dtype) for a in arrs],
        in_specs=[HBM] * n + [pl.BlockSpec(memory_space=pl.ANY)] * (len(ins) - n), out_specs=[HBM] * n,
        scratch_shapes=[pltpu.SemaphoreType.DMA((7 * n,)), pltpu.SemaphoreType.DMA((7 * n,)), pltpu.SemaphoreType.DMA((n,))],
    )(*ins)


def _direct_copies(x_refs, land_refs, send_sems, recv_sems, local_sems, gather):
    x, y, c = lax.axis_index("x"), lax.axis_index("y"), lax.axis_index("c")
    me = 4 * x + 2 * y + c
    sends, recvs = [], []
    for k in range(1, N_DEV):
        px = 1 - x if (k >> 2) & 1 else x
        py = 1 - y if (k >> 1) & 1 else y
        pc = 1 - c if k & 1 else c
        peer = 4 * px + 2 * py + pc
        for t, (x_ref, land_ref) in enumerate(zip(x_refs, land_refs)):
            sem = 7 * t + k - 1
            for out, src, slot in ((sends, x_ref if gather else x_ref.at[peer], me),
                                   (recvs, x_ref if gather else x_ref.at[me], peer)):
                out.append(pltpu.make_async_remote_copy(
                    src_ref=src, dst_ref=land_ref.at[slot], send_sem=send_sems.at[sem], recv_sem=recv_sems.at[sem],
                    device_id=(px, py, pc), device_id_type=MESH))
    own = [pltpu.make_async_copy(x_ref if gather else x_ref.at[me], land_ref.at[me], local_sems.at[t])
           for t, (x_ref, land_ref) in enumerate(zip(x_refs, land_refs))]
    return sends, recvs, own


SEM =pl.BlockSpec(memory_space=pltpu.SEMAPHORE)
ANY = pl.BlockSpec(memory_space=pl.ANY)
SIDE_EFFECT = pltpu.SideEffectType.DATAFLOW_SIDE_EFFECTING


def _exchange_start(arrs, after, *, gather, name):
    n = len(arrs)
    lands = [lax.empty(((N_DEV,) + a.shape) if gather else a.shape, a.dtype) for a in arrs]

    def body(*refs):
        sends, _, own = _direct_copies(refs[:n], refs[n:2 * n], *refs[2 * n + 1:2 * n + 4], gather)
        for cp in own + sends:
            cp.start()
        refs[-1][...] = jnp.zeros_like(refs[-1])

    ops = [pltpu.with_memory_space_constraint(a, pltpu.HBM) for a in list(arrs) + lands]
    out = pl.pallas_call(
        body, name=name,
        out_shape=(pltpu.SemaphoreType.DMA((7 * n,)), pltpu.SemaphoreType.DMA((7 * n,)), pltpu.SemaphoreType.DMA((n,)),
                   *[pltpu.HBM(a.shape, a.dtype) for a in ops], jax.ShapeDtypeStruct((8, LANES), F32)),
        in_specs=[HBM] * (2 * n) + [ANY],
        out_specs=(SEM, SEM, SEM, *[HBM] * (2 * n), pl.BlockSpec(memory_space=pltpu.VMEM)),
        input_output_aliases={t: 3 + t for t in range(2 * n)},
        compiler_params=pltpu.CompilerParams(has_side_effects=SIDE_EFFECT),
    )(*ops, after)
    return (out[:3], out[3:3 + n], out[3 + n:3 + 2 * n]), out[-1]


def _exchange_wait(state, after, *, gather, name):
    sems, arrs, lands = state
    n = len(arrs)

    def body(*refs):
        sends, recvs, own = _direct_copies(refs[:n], refs[n:2 * n], *refs[2 * n:2 * n + 3], gather)
        for cp in own:
            cp.wait()
        for cp in sends:
            cp.wait_send()
        for cp in recvs:
            cp.wait_recv()

    out = pl.pallas_call(
        body, name=name, out_shape=tuple(pltpu.HBM(a.shape, a.dtype) for a in list(arrs) + list(lands)),
        in_specs=[HBM] * (2 * n) + [SEM, SEM, SEM, ANY], out_specs=tuple([HBM] * (2 * n)),
        input_output_aliases={t: t for t in range(2 * n)},
        compiler_params=pltpu.CompilerParams(has_side_effects=SIDE_EFFECT),
    )(*arrs, *lands, *sems, after)
    return out[n:]


def _sum_parts(groups, name):
    n, r, cdim = groups[0].shape
    tr = _row_tile(r, max(16, (1 << 21) // (n * cdim * groups[0].dtype.itemsize)))
    steps = r // tr

    def body(*refs):
        o_ref = refs[-1]
        gg = pl.program_id(0)
        for gi in range(len(groups)):
            @pl.when(gg == gi)
            def _(gi=gi):
                acc = refs[gi][0].astype(F32)
                for k in range(1, n):
                    acc = acc + refs[gi][k].astype(F32)
                o_ref[...] = acc

    def in_spec(gi):
        return pl.BlockSpec((n, tr, cdim), lambda gg, i: (0, jnp.where(gg == gi, i, 0), 0))

    return pl.pallas_call(
        body, name=name, out_shape=jax.ShapeDtypeStruct((len(groups) * r, cdim), F32), grid=(len(groups), steps),
        in_specs=[in_spec(gi) for gi in range(len(groups))],
        out_specs=pl.BlockSpec((tr, cdim), lambda gg, i: (gg * steps + i, 0)),
        compiler_params=_params(("parallel", "parallel")),
    )(*groups)


def _mm_tn(a, b, *, name, after=None, tm=512, tn=1024):
    k, m = a.shape
    n = b.shape[1]
    tm, tn = _tile(m, tm), _tile(n, tn)

    def body(a_ref, b_ref, *rest):
        o_ref, at_ref = rest[-2], rest[-1]

        @pl.when(pl.program_id(1) == 0)
        def _():
            at_ref[...] = a_ref[...].astype(BF16).T

        o_ref[...] = _dot(at_ref[...], b_ref[...].astype(BF16), 1, 0).astype(BF16)

    ins = [a, b] + ([] if after is None else [after])
    return pl.pallas_call(
        body, name=name, out_shape=jax.ShapeDtypeStruct((m, n), BF16), grid=(m // tm, n // tn),
        in_specs=[pl.BlockSpec((k, tm), lambda i, j: (0, i)), pl.BlockSpec((k, tn), lambda i, j: (0, j))] + [ANY] * (len(ins) - 2),
        out_specs=pl.BlockSpec((tm, tn), lambda i, j: (i, j)),
        scratch_shapes=[pltpu.VMEM((tm, k), BF16)], compiler_params=_params(("parallel", "arbitrary")),
    )(*ins)


def _mm2(a1, b1, a2, b2, *, name, after=None, tm=256, tn=1024):
    m = a1.shape[0]
    n = b1.shape[1]
    tm, tn = _tile(m, tm), _tile(n, tn)

    def body(a1_ref, b1_ref, a2_ref, b2_ref, *rest):
        rest[-1][...] = (_dot(a1_ref[...].astype(BF16), b1_ref[...], 1, 0)
                         + _dot(a2_ref[...].astype(BF16), b2_ref[...], 1, 0))

    ins = [a1, b1, a2, b2] + ([] if after is None else [after])

    def a_spec(t):
        return pl.BlockSpec((tm, t.shape[1]), lambda i, j: (i, 0))

    def b_spec(t):
        return pl.BlockSpec((t.shape[0], tn), lambda i, j: (0, j))

    return pl.pallas_call(
        body, name=name, out_shape=jax.ShapeDtypeStruct((m, n), F32), grid=(m // tm, n // tn),
        in_specs=[a_spec(a1), b_spec(b1), a_spec(a2), b_spec(b2)] + [ANY] * (len(ins) - 4),
        out_specs=pl.BlockSpec((tm, tn), lambda i, j: (i, j)), compiler_params=_params(("parallel", "parallel")),
    )(*ins)


def _mm(a, b, *, name, ta=False, tb=False, res=None, colscale=None, emit_acc=False,
        out_dtype=F32, tm=512, tn=512):
    m, k = (a.shape[1], a.shape[0]) if ta else a.shape
    n = b.shape[0] if tb else b.shape[1]
    tm, tn = _tile(m, tm), _tile(n, tn)
    ca, cb = (0 if ta else 1), (1 if tb else 0)
    a_spec = pl.BlockSpec((k, tm), lambda i, j: (0, i)) if ta else pl.BlockSpec((tm, k), lambda i, j: (i, 0))
    b_spec = pl.BlockSpec((tn, k), lambda i, j: (j, 0)) if tb else pl.BlockSpec((k, tn), lambda i, j: (0, j))
    tile = pl.BlockSpec((tm, tn), lambda i, j: (i, j))
    ins, in_specs = [a, b], [a_spec, b_spec]
    if res is not None:
        ins.append(res)
        in_specs.append(tile)
    if colscale is not None:
        ins.append(colscale)
        in_specs.append(pl.BlockSpec((1, tn), lambda i, j: (0, j)))
    n_in = len(ins)

    def body(*refs):
        outs = refs[n_in:]
        acc = _dot(refs[0][...].astype(BF16), refs[1][...].astype(BF16), ca, cb)
        val, p = acc, 2
        if res is not None:
            r_val, p = refs[p][...], p + 1
        if colscale is not None:
            val = val * refs[p][...]
        if res is not None:
            val = r_val + val
        if emit_acc:
            outs[0][...] = acc
        outs[-1][...] = val.astype(out_dtype)

    out_shape = [jax.ShapeDtypeStruct((m, n), out_dtype)]
    out_specs = [tile]
    if emit_acc:
        out_shape.insert(0, jax.ShapeDtypeStruct((m, n), F32))
        out_specs.insert(0, tile)
    out = pl.pallas_call(
        body, name=name, out_shape=out_shape, grid=(m // tm, n // tn), in_specs=in_specs, out_specs=out_specs,
        compiler_params=_params(("parallel", "parallel")),
    )(*ins)
    return out if emit_acc else out[0]


def _norm_fwd(x, g, scale, shift, name, after=None):
    s, d = x.shape
    tr = 256

    def body(x_ref, g_ref, sc_ref, sh_ref, *rest):
        xv = x_ref[...]
        rstd = lax.rsqrt(jnp.mean(xv * xv, axis=-1, keepdims=True) + RMS_EPS)
        rest[-1][...] = (xv * rstd * g_ref[...] * (1.0 + sc_ref[...]) + sh_ref[...]).astype(BF16)

    rowspec = pl.BlockSpec((1, d), lambda i: (0, 0))
    ins = [x, g, scale, shift] + ([] if after is None else [after])
    return pl.pallas_call(
        body, name=name, out_shape=jax.ShapeDtypeStruct((s, d), BF16), grid=(s // tr,),
        in_specs=[pl.BlockSpec((tr, d), lambda i: (i, 0)), rowspec, rowspec, rowspec] + [ANY] * (len(ins) - 4),
        out_specs=pl.BlockSpec((tr, d), lambda i: (i, 0)),
        compiler_params=_params(("parallel",)),
    )(*ins)


def _norm_bwd(x, dh, dres, g, scale, name):
    s, d = x.shape
    tr = 256

    def body(x_ref, dh_ref, dr_ref, g_ref, sc_ref, dx_ref, a_ref, b_ref):
        @pl.when(pl.program_id(0) == 0)
        def _():
            a_ref[...] = jnp.zeros_like(a_ref)
            b_ref[...] = jnp.zeros_like(b_ref)

        xv = x_ref[...]
        rstd = lax.rsqrt(jnp.mean(xv * xv, axis=-1, keepdims=True) + RMS_EPS)
        xhat = xv * rstd
        dhv = dh_ref[...]
        dxhat = dhv * (g_ref[...] * (1.0 + sc_ref[...]))
        mean_term = jnp.mean(dxhat * xhat, axis=-1, keepdims=True)
        dx_ref[...] = dr_ref[...] + rstd * (dxhat - xhat * mean_term)
        a_ref[...] += jnp.sum(dhv, axis=0, keepdims=True)
        b_ref[...] += jnp.sum(dhv * xhat, axis=0, keepdims=True)

    rowspec = pl.BlockSpec((1, d), lambda i: (0, 0))
    tile = pl.BlockSpec((tr, d), lambda i: (i, 0))
    return pl.pallas_call(
        body, name=name,
        out_shape=[jax.ShapeDtypeStruct((s, d), F32), jax.ShapeDtypeStruct((1, d), F32), jax.ShapeDtypeStruct((1, d), F32)],
        grid=(s // tr,), in_specs=[tile, tile, tile, rowspec, rowspec], out_specs=[tile, rowspec, rowspec],
        compiler_params=_params(("arbitrary",)),
    )(x, dh, dres, g, scale)


def _gate_bwd(dxn, f, colscale, coef, name):
    s, d = dxn.shape
    tr = 256

    def body(dx_ref, f_ref, cs_ref, df_ref, dg_ref):
        @pl.when(pl.program_id(0) == 0)
        def _():
            dg_ref[...] = jnp.zeros_like(dg_ref)

        dxv = dx_ref[...]
        df_ref[...] = (dxv * cs_ref[...]).astype(BF16)
        dg_ref[...] += coef * jnp.sum(dxv * f_ref[...], axis=0, keepdims=True)

    rowspec = pl.BlockSpec((1, d), lambda i: (0, 0))
    tile = pl.BlockSpec((tr, d), lambda i: (i, 0))
    return pl.pallas_call(
        body, name=name, out_shape=[jax.ShapeDtypeStruct((s, d), BF16), jax.ShapeDtypeStruct((1, d), F32)],
        grid=(s // tr,), in_specs=[tile, tile, rowspec], out_specs=[tile, rowspec],
        compiler_params=_params(("arbitrary",)),
    )(dxn, f, colscale)


def _ffn_up(h, wg, wu, name, tm=SEQ, tn=256):
    s, d = h.shape
    f = wg.shape[0]

    def body(h_ref, wg_ref, wu_ref, a_ref, u_ref, s_ref):
        hv = h_ref[...]
        a = _dot(hv, wg_ref[...], 1, 1)
        u = _dot(hv, wu_ref[...], 1, 1)
        a_ref[...] = a.astype(BF16)
        u_ref[...] = u.astype(BF16)
        s_ref[...] = (a * _sigmoid(a) * u).astype(BF16)

    tile = pl.BlockSpec((tm, tn), lambda i, j: (i, j))
    wspec = pl.BlockSpec((tn, d), lambda i, j: (j, 0))
    return pl.pallas_call(
        body, name=name,
        out_shape=[jax.ShapeDtypeStruct((s, f), BF16), jax.ShapeDtypeStruct((s, f), BF16), jax.ShapeDtypeStruct((s, f), BF16)],
        grid=(s // tm, f // tn), in_specs=[pl.BlockSpec((tm, d), lambda i, j: (i, 0)), wspec, wspec],
        out_specs=[tile, tile, tile], compiler_params=_params(("parallel", "parallel")),
    )(h, wg, wu)


def _ffn_bwd_ds(df, wd, a, u, name, tm=SEQ, tn=256):
    s, d = df.shape
    f = wd.shape[0]

    def body(df_ref, wd_ref, a_ref, u_ref, da_ref, du_ref):
        ds = _dot(df_ref[...], wd_ref[...], 1, 1)
        av = a_ref[...].astype(F32)
        sg = _sigmoid(av)
        da_ref[...] = (ds * u_ref[...].astype(F32) * (sg * (1.0 + av * (1.0 - sg)))).astype(BF16)
        du_ref[...] = (ds * (av * sg)).astype(BF16)

    tile = pl.BlockSpec((tm, tn), lambda i, j: (i, j))
    return pl.pallas_call(
        body, name=name, out_shape=[jax.ShapeDtypeStruct((s, f), BF16), jax.ShapeDtypeStruct((s, f), BF16)],
        grid=(s // tm, f // tn),
        in_specs=[pl.BlockSpec((tm, d), lambda i, j: (i, 0)), pl.BlockSpec((tn, d), lambda i, j: (j, 0)), tile, tile],
        out_specs=[tile, tile], compiler_params=_params(("parallel", "parallel")),
    )(df, wd, a, u)


def _merge_fwd(o_sb, o_dil, o_swa, gates, wb_sb, wb_dil, wb_swa, name):
    s, d = SEQ, D_MODEL
    tm = 256

    def body(osb_ref, odl_ref, osw_ref, g_ref, wsb_ref, wdl_ref, wsw_ref, m_ref, tsb_ref, tdl_ref, tsw_ref):
        for h in range(osb_ref.shape[0]):
            tsb_ref[:, h * HEAD_DIM:(h + 1) * HEAD_DIM] = osb_ref[h].astype(BF16)
        for h in range(osw_ref.shape[0]):
            tsw_ref[:, h * HEAD_DIM:(h + 1) * HEAD_DIM] = osw_ref[h].astype(BF16)
        tdl_ref[...] = odl_ref[...].astype(BF16)
        acc = _sigmoid(g_ref[:, 0:d]) * _dot(tsb_ref[...], wsb_ref[...], 1, 0)
        acc += _sigmoid(g_ref[:, d:2 * d]) * _dot(tdl_ref[...], wdl_ref[...], 1, 0)
        acc += _sigmoid(g_ref[:, 2 * d:3 * d]) * _dot(tsw_ref[...], wsw_ref[...], 1, 0)
        m_ref[...] = acc.astype(BF16)

    def rows(w):
        return pl.BlockSpec((tm, w), lambda i: (i, 0))

    def heads(n):
        return pl.BlockSpec((n, tm, HEAD_DIM), lambda i: (0, i, 0))

    def whole(w):
        return pl.BlockSpec((w, d), lambda i: (0, 0))

    return pl.pallas_call(
        body, name=name, out_shape=[jax.ShapeDtypeStruct((s, w), BF16) for w in (d, 256, 128, 384)], grid=(s // tm,),
        in_specs=[heads(H_SB), rows(128), heads(H_SWA_Q), rows(3 * d), whole(256), whole(128), whole(384)],
        out_specs=[rows(d), rows(256), rows(128), rows(384)], compiler_params=_params(("parallel",)),
    )(o_sb, o_dil, o_swa, gates, wb_sb, wb_dil, wb_swa)


def _merge_bwd(dmerged, t_sb, t_dil, t_swa, gates, wb_sb, wb_dil, wb_swa, name):
    s, d = SEQ, D_MODEL
    tm = 256

    def body(dm_ref, tsb_ref, tdl_ref, tsw_ref, g_ref, wsb_ref, wdl_ref, wsw_ref,
             dg_ref, dosb_ref, dodl_ref, dosw_ref, dbsb_ref, dbdl_ref, dbsw_ref):
        dm = dm_ref[...]
        for idx, (t_ref, w_ref, do_ref, db_ref) in enumerate((
                (tsb_ref, wsb_ref, dosb_ref, dbsb_ref), (tdl_ref, wdl_ref, dodl_ref, dbdl_ref),
                (tsw_ref, wsw_ref, dosw_ref, dbsw_ref))):
            w = w_ref[...]
            br = _dot(t_ref[...], w, 1, 0)
            sg = _sigmoid(g_ref[:, idx * d:(idx + 1) * d])
            dbr = (dm * sg).astype(BF16)
            dg_ref[:, idx * d:(idx + 1) * d] = (dm * br * (sg * (1.0 - sg))).astype(BF16)
            db_ref[...] = dbr
            do = _dot(dbr, w, 1, 1)
            if len(do_ref.shape) == 2:
                do_ref[...] = do
            else:
                for h in range(do_ref.shape[0]):
                    do_ref[h] = do[:, h * HEAD_DIM:(h + 1) * HEAD_DIM]

    def rows(w):
        return pl.BlockSpec((tm, w), lambda i: (i, 0))

    def heads(n):
        return pl.BlockSpec((n, tm, HEAD_DIM), lambda i: (0, i, 0))

    def whole(w):
        return pl.BlockSpec((w, d), lambda i: (0, 0))

    def shp(w, dt):
        return jax.ShapeDtypeStruct((s, w), dt)

    def hshp(n):
        return jax.ShapeDtypeStruct((n, s, HEAD_DIM), F32)

    return pl.pallas_call(
        body, name=name,
        out_shape=[shp(3 * d, BF16), hshp(H_SB), shp(128, F32), hshp(H_SWA_Q), shp(d, BF16), shp(d, BF16), shp(d, BF16)],
        grid=(s // tm,),
        in_specs=[rows(d), rows(256), rows(128), rows(384), rows(3 * d), whole(256), whole(128), whole(384)],
        out_specs=[rows(3 * d), heads(H_SB), rows(128), heads(H_SWA_Q), rows(d), rows(d), rows(d)],
        compiler_params=_params(("parallel",)),
    )(dmerged, t_sb, t_dil, t_swa, gates, wb_sb, wb_dil, wb_swa)


def _final_loss(x, target, g, name):
    s, d = x.shape
    tr = 256

    def body(x_ref, t_ref, g_ref, loss_ref, dx_ref, dg_ref):
        @pl.when(pl.program_id(0) == 0)
        def _():
            loss_ref[...] = jnp.zeros_like(loss_ref)
            dg_ref[...] = jnp.zeros_like(dg_ref)

        xv = x_ref[...]
        gv = g_ref[...]
        rstd = lax.rsqrt(jnp.mean(xv * xv, axis=-1, keepdims=True) + RMS_EPS)
        xhat = xv * rstd
        err = xhat * gv - t_ref[...]
        loss_ref[...] += 0.5 * jnp.sum(jnp.mean(err * err, axis=-1, keepdims=True))
        dy = err * (1.0 / d)
        dxhat = dy * gv
        mean_term = jnp.mean(dxhat * xhat, axis=-1, keepdims=True)
        dx_ref[...] = rstd * (dxhat - xhat * mean_term)
        dg_ref[...] += jnp.sum(dy * xhat, axis=0, keepdims=True)

    rowspec = pl.BlockSpec((1, d), lambda i: (0, 0))
    tile = pl.BlockSpec((tr, d), lambda i: (i, 0))
    return pl.pallas_call(
        body, name=name,
        out_shape=[jax.ShapeDtypeStruct((1, LANES), F32), jax.ShapeDtypeStruct((s, d), F32), jax.ShapeDtypeStruct((1, d), F32)],
        grid=(s // tr,), in_specs=[tile, tile, rowspec],
        out_specs=[pl.BlockSpec((1, LANES), lambda i: (0, 0)), tile, rowspec],
        compiler_params=_params(("arbitrary",)),
    )(x, target, g)


def _adamw(w, g, m, v, name):
    shape = w.shape
    cols = shape[-1]
    rows = int(np.prod(shape[:-1])) if len(shape) > 1 else 1
    tr = rows
    for cand in (1024, 512, 256, 128, 64, 32, 16, 8):
        if rows % cand == 0 and rows > cand and cand * cols * 4 <= (1 << 21):
            tr = cand
            break

    def body(w_ref, g_ref, m_ref, v_ref, d_ref, nm_ref, nv_ref):
        d_ref[...], nm_ref[...], nv_ref[...] = _adam_update(w_ref[...], g_ref[...], m_ref[...], v_ref[...])

    tile = pl.BlockSpec((tr, cols), lambda i: (i, 0))
    flat = [t.reshape(rows, cols) for t in (w, g, m, v)]
    out = pl.pallas_call(
        body, name=name, out_shape=[jax.ShapeDtypeStruct((rows, cols), F32)] * 3, grid=(rows // tr,),
        in_specs=[tile] * 4, out_specs=[tile] * 3, compiler_params=_params(("parallel",)),
    )(*flat)
    return tuple(t.reshape(shape) for t in out)


def _adam_update(w, gv, m, v):
    nm = ADAM_B1 * m + (1.0 - ADAM_B1) * gv
    nv = ADAM_B2 * v + (1.0 - ADAM_B2) * (gv * gv)
    m_hat = nm / (1.0 - ADAM_B1 ** ADAM_STEP)
    v_hat = nv / (1.0 - ADAM_B2 ** ADAM_STEP)
    return -ADAM_LR * (m_hat / (jnp.sqrt(v_hat) + ADAM_EPS) + ADAM_WD * w), nm, nv


def _reduce_adamw(groups, w, m, v, row0, prev, name, after=None):
    n, r, cdim = groups[0].shape
    rows = w.shape[0]
    tr = _row_tile(r, max(16, (1 << 22) // (n * cdim * groups[0].dtype.itemsize)))
    steps = r // tr
    ng = len(groups)

    def body(*refs):
        w_ref, m_ref, v_ref = refs[ng:ng + 3]
        g_out, d_out, m_out, v_out = refs[-4:]
        gg = pl.program_id(0)
        for gi in range(ng):
            @pl.when(gg == gi)
            def _(gi=gi):
                acc = refs[gi][0].astype(F32)
                for k in range(1, n):
                    acc = acc + refs[gi][k].astype(F32)
                g_out[...] = acc
                d_out[...], m_out[...], v_out[...] = _adam_update(w_ref[...], acc, m_ref[...], v_ref[...])

    def part_spec(gi):
        return pl.BlockSpec((n, tr, cdim), lambda gg, i: (0, jnp.where(gg == gi, i, 0), 0))

    tile = pl.BlockSpec((tr, cdim), lambda gg, i: (row0 // tr + gg * steps + i, 0))
    extra = ([] if prev is None else list(prev)) + ([] if after is None else [after])
    return pl.pallas_call(
        body, name=name, out_shape=[jax.ShapeDtypeStruct((rows, cdim), F32)] * 4, grid=(ng, steps),
        in_specs=[part_spec(gi) for gi in range(ng)] + [tile] * 3 + [ANY] * len(extra), out_specs=[tile] * 4,
        input_output_aliases={} if prev is None else {ng + 3 + k: k for k in range(4)},
        compiler_params=_params(("parallel", "parallel")),
    )(*groups, w, m, v, *extra)


def _ada_fwd(c_all, w, name):
    n = w.shape[1]

    def body(c_ref, w_ref, o_ref):
        cv = c_ref[...]
        o_ref[...] = jnp.dot(cv * _sigmoid(cv), w_ref[...], preferred_element_type=F32, precision=lax.Precision.HIGHEST)

    return pl.pallas_call(body, name=name, out_shape=jax.ShapeDtypeStruct((N_DEV, n), F32), compiler_params=_params())(c_all, w)


def _ada_bwd(c_all_t, dmod, name):
    n = dmod.shape[1]

    def body(c_ref, d_ref, o_ref):
        cv = c_ref[...]
        o_ref[...] = jnp.dot(cv * _sigmoid(cv), d_ref[...], preferred_element_type=F32, precision=lax.Precision.HIGHEST)

    return pl.pallas_call(body, name=name, out_shape=jax.ShapeDtypeStruct((D_MODEL, n), F32), compiler_params=_params())(c_all_t, dmod)


def _bucket_tables():
    rel = np.arange(BLK)[:, None] + BLK - np.arange(2 * BLK)[None, :]
    max_exact = N_BUCKETS // 2

    def bucket(n):
        nf = np.maximum(n, 1).astype(np.float32)
        large = max_exact + (np.log(nf / np.float32(max_exact)) / np.float32(math.log(MAX_REL_DIST / max_exact))
                             * np.float32(N_BUCKETS - max_exact)).astype(np.int32)
        return np.where(n < max_exact, n, np.minimum(large, N_BUCKETS - 1))

    tabs = []
    for dil, max_dist in ((1, 128), (4, 128), (16, 128), (1, SWA_WINDOW - 1)):
        in_band = (rel >= 0) & (rel <= max_dist)
        tabs.append(np.where(in_band, bucket(np.maximum(rel, 0) * dil), -1))
    return np.stack(tabs).astype(np.int32)


N_SOFT = H_DIL + H_SWA_Q


def _table_of_head(h):
    return jnp.minimum(h // 2, 3)


def _bias_build(rel_bias, tables, name):
    def body(rel_ref, t_ref, o_ref):
        h = pl.program_id(0)
        tb = t_ref[0]
        out = jnp.full((BLK, 2 * BLK), NEG, F32)
        for b in range(N_BUCKETS):
            out = jnp.where(tb == b, rel_ref[b, h], out)
        o_ref[0] = out

    return pl.pallas_call(
        body, name=name, out_shape=jax.ShapeDtypeStruct((N_SOFT, BLK, 2 * BLK), F32), grid=(N_SOFT,),
        in_specs=[pl.BlockSpec(memory_space=pltpu.SMEM),
                  pl.BlockSpec((1, BLK, 2 * BLK), lambda h: (_table_of_head(h), 0, 0))],
        out_specs=pl.BlockSpec((1, BLK, 2 * BLK), lambda h: (h, 0, 0)),
        compiler_params=_params(("parallel",)),
    )(rel_bias, tables)


def _bias_grad(dbias, tables, name):
    def body(d_ref, t_ref, o_ref):
        tb = t_ref[0]
        dv = d_ref[0]
        lane = lax.broadcasted_iota(jnp.int32, (1, LANES), 1)
        out = jnp.zeros((1, LANES), F32)
        for b in range(N_BUCKETS):
            out = jnp.where(lane == b, jnp.sum(jnp.where(tb == b, dv, 0.0)), out)
        o_ref[0] = out

    return pl.pallas_call(
        body, name=name, out_shape=jax.ShapeDtypeStruct((N_SOFT, 1, LANES), F32), grid=(N_SOFT,),
        in_specs=[pl.BlockSpec((1, BLK, 2 * BLK), lambda h: (h, 0, 0)),
                  pl.BlockSpec((1, BLK, 2 * BLK), lambda h: (_table_of_head(h), 0, 0))],
        out_specs=pl.BlockSpec((1, 1, LANES), lambda h: (h, 0, 0)),
        compiler_params=_params(("parallel",)),
    )(dbias, tables)


def _band_layout(g, bias_div):
    assert g == 1 or bias_div == 1
    return bias_div if g == 1 else 1


def _band_specs(length, g, bias_div, offs):
    ns = _band_layout(g, bias_div)

    def seqs(off, div=1):
        return pl.BlockSpec((ns, length, HEAD_DIM), lambda s: (off // ns + s // div, 0, 0))

    xspecs = [seqs(offs[0]), seqs(offs[1], g), seqs(offs[2], g)]
    bspec = pl.BlockSpec((1, BLK, 2 * BLK), lambda s: (s, 0, 0))
    sspec = pl.BlockSpec((ns, 1, LANES), lambda s: (s, 0, 0))
    colspec = pl.BlockSpec((ns, length, 1), lambda s: (s, 0, 0))
    return xspecs, seqs(0), seqs(0, g), bspec, sspec, colspec


def _band_sweep(length, ns, one):
    nblk = length // BLK
    for qq in range(ns):
        if ns * nblk <= 16:
            for i in range(nblk):
                one(qq, i * BLK, max(i - 1, 0) * BLK, i == 0)
        else:
            def step(i, carry, qq=qq):
                one(qq, pl.multiple_of(i * BLK, BLK), pl.multiple_of(jnp.maximum(i - 1, 0) * BLK, BLK), i == 0)
                return carry

            lax.fori_loop(0, nblk, step, 0, unroll=2)


def _band_scores(q_ref, k_ref, b_ref, qq, kq, bq, cur, prv, first):
    qv = q_ref[qq, pl.ds(cur, BLK), :]
    bv = b_ref[bq]
    if first is True:
        sp = jnp.full((BLK, BLK), NEG, F32)
    else:
        sp = _dot(qv, k_ref[kq, pl.ds(prv, BLK), :], 1, 1) + bv[:, :BLK]
        sp = sp if first is False else jnp.where(first, NEG, sp)
    sc = _dot(qv, k_ref[kq, pl.ds(cur, BLK), :], 1, 1) + bv[:, BLK:]
    return qv, sp, sc


def _band_fwd(x, bias, sink, *, nq, offs, g, bias_div, has_sink, name):
    length = x.shape[1]
    ns = _band_layout(g, bias_div)

    def body(q_ref, k_ref, v_ref, b_ref, s_ref, o_ref, lse_ref):
        def one(qq, cur, prv, first):
            kq, bq = qq, 0
            _, sp, sc = _band_scores(q_ref, k_ref, b_ref, qq, kq, bq, cur, prv, first)
            m = jnp.maximum(jnp.max(sp, axis=1, keepdims=True), jnp.max(sc, axis=1, keepdims=True))
            if has_sink:
                sk = s_ref[qq][:, :1]
                m = jnp.maximum(m, sk)
            pp, pc = jnp.exp(sp - m), jnp.exp(sc - m)
            den = jnp.sum(pp, axis=1, keepdims=True) + jnp.sum(pc, axis=1, keepdims=True)
            if has_sink:
                den = den + jnp.exp(sk - m)
            acc = (_dot(pp.astype(BF16), v_ref[kq, pl.ds(prv, BLK), :], 1, 0)
                   + _dot(pc.astype(BF16), v_ref[kq, pl.ds(cur, BLK), :], 1, 0))
            o_ref[qq, pl.ds(cur, BLK), :] = acc / den
            lse_ref[qq, pl.ds(cur, BLK), :] = m + jnp.log(den)

        _band_sweep(length, ns, one)

    xspecs, qspec, _, bspec, sspec, colspec = _band_specs(length, g, bias_div, offs)
    return pl.pallas_call(
        body, name=name,
        out_shape=[jax.ShapeDtypeStruct((nq, length, HEAD_DIM), F32), jax.ShapeDtypeStruct((nq, length, 1), F32)],
        grid=(nq // ns,), in_specs=xspecs + [bspec, sspec],
        out_specs=[qspec, colspec], compiler_params=_params(("parallel",)),
    )(x, x, x, bias, sink)


def _band_bwd(x, bias, sink, o, lse, do, dlse, *, nq, offs, g, bias_div, has_sink, name):
    length = x.shape[1]
    ns = _band_layout(g, bias_div)
    nk, nbias = nq // g, nq // bias_div

    def body(q_ref, k_ref, v_ref, b_ref, s_ref, o_ref, lse_ref, do_ref, dlse_ref,
             dq_ref, dk_ref, dv_ref, db_ref, dsk_ref, dkp_ref, dvp_ref):
        for ref in (db_ref, dsk_ref, dkp_ref, dvp_ref):
            ref[...] = jnp.zeros_like(ref)

        @pl.when(pl.program_id(0) % g == 0)
        def _():
            dk_ref[...] = jnp.zeros_like(dk_ref)
            dv_ref[...] = jnp.zeros_like(dv_ref)

        def one(qq, cur, prv, first):
            kq, bq = qq, 0
            qv, sp, sc = _band_scores(q_ref, k_ref, b_ref, qq, kq, bq, cur, prv, first)
            rows, prow = pl.ds(cur, BLK), pl.ds(prv, BLK)
            lse_v = lse_ref[qq, rows, :]
            pp, pc = jnp.exp(sp - lse_v), jnp.exp(sc - lse_v)
            dov = do_ref[qq, rows, :]
            dob = dov.astype(BF16)
            coef = dlse_ref[qq, rows, :] - jnp.sum(dov * o_ref[qq, rows, :], axis=1, keepdims=True)
            dsp = pp * (_dot(dob, v_ref[kq, prow, :], 1, 1) + coef)
            dsc = pc * (_dot(dob, v_ref[kq, rows, :], 1, 1) + coef)
            dspb, dscb = dsp.astype(BF16), dsc.astype(BF16)
            dq_ref[qq, rows, :] = ((_dot(dspb, k_ref[kq, prow, :], 1, 0) + _dot(dscb, k_ref[kq, rows, :], 1, 0))
                                   * (HEAD_DIM ** -0.5))
            dk_ref[kq, rows, :] += _dot(dscb, qv, 0, 0)
            dkp_ref[kq, prow, :] += _dot(dspb, qv, 0, 0)
            dv_ref[kq, rows, :] += _dot(pc.astype(BF16), dob, 0, 0)
            dvp_ref[kq, prow, :] += _dot(pp.astype(BF16), dob, 0, 0)
            db_ref[bq, :, :BLK] += dsp
            db_ref[bq, :, BLK:] += dsc
            if has_sink:
                dsk_ref[qq] += jnp.sum(jnp.exp(s_ref[qq][:, :1] - lse_v) * coef)

        _band_sweep(length, ns, one)
        dk_ref[...] += dkp_ref[...]
        dv_ref[...] += dvp_ref[...]

    xspecs, qspec, kvspec, bspec, sspec, colspec = _band_specs(length, g, bias_div, offs)
    return pl.pallas_call(
        body, name=name,
        out_shape=[jax.ShapeDtypeStruct((nq, length, HEAD_DIM), F32), jax.ShapeDtypeStruct((nk, length, HEAD_DIM), F32),
                   jax.ShapeDtypeStruct((nk, length, HEAD_DIM), F32), jax.ShapeDtypeStruct((nbias, BLK, 2 * BLK), F32),
                   jax.ShapeDtypeStruct((nq, 1, LANES), F32)],
        grid=(nq // ns,),
        in_specs=xspecs + [bspec, sspec, qspec, colspec, qspec, colspec],
        out_specs=[qspec, kvspec, kvspec, bspec, sspec],
        scratch_shapes=[pltpu.VMEM((ns, length, HEAD_DIM), F32), pltpu.VMEM((ns, length, HEAD_DIM), F32)],
        compiler_params=_params(("arbitrary",)),
    )(x, x, x, bias, sink, o, lse, do, dlse)


TOK_TILE = 512


def _dil_merge(outs, lses, dout, name):
    tr = TOK_TILE
    dils = [d for _, d in DIL_PATTERNS]
    n = len(dils)
    o4 = [o.reshape(2, d, SEQ // d, HEAD_DIM) for o, d in zip(outs, dils)]
    l4 = [l.reshape(2, d, SEQ // d, 1) for l, d in zip(lses, dils)]
    o_specs = [pl.BlockSpec((2, d, tr // d, HEAD_DIM), lambda i: (0, 0, i, 0)) for d in dils]
    l_specs = [pl.BlockSpec((2, d, tr // d, 1), lambda i: (0, 0, i, 0)) for d in dils]
    tok = pl.BlockSpec((tr, 2 * HEAD_DIM), lambda i: (i, 0))
    scratch = ([pltpu.VMEM((tr, 2 * HEAD_DIM), F32) for _ in dils] + [pltpu.VMEM((tr, 1), F32) for _ in range(2 * n)]
               + [pltpu.VMEM((tr // d, 2 * HEAD_DIM), F32) for d in dils])

    def to_tokens(o_ref, l_ref, d, pair, cols, stage):
        for r in range(d):
            rows = pl.ds(r, tr // d, stride=d) if d > 1 else slice(None)
            stage[:, :HEAD_DIM] = o_ref[0, r]
            stage[:, HEAD_DIM:] = o_ref[1, r]
            pair[rows, :] = stage[...]
            for h in range(2):
                cols[h][rows, :] = l_ref[h, r]
        return pair[...], [cols[0][...], cols[1][...]]

    def weights(ls):
        left = lax.broadcasted_iota(jnp.int32, (tr, 2 * HEAD_DIM), 1) < HEAD_DIM
        per_head = []
        for h in range(2):
            m = ls[0][h]
            for g in range(1, n):
                m = jnp.maximum(m, ls[g][h])
            es = [jnp.exp(ls[g][h] - m) for g in range(n)]
            den = es[0]
            for e in es[1:]:
                den = den + e
            per_head.append([e / den for e in es])
        return per_head, [jnp.where(left, per_head[0][g], per_head[1][g]) for g in range(n)], left

    def load(refs):
        pairs, cols, stages = refs[:n], refs[n:3 * n], refs[3 * n:]
        return pairs, [cols[2 * g:2 * g + 2] for g in range(n)], stages

    if dout is None:
        def body(*refs):
            pairs, cols, stages = load(refs[2 * n + 1:])
            toks = [to_tokens(refs[g], refs[n + g], dils[g], pairs[g], cols[g], stages[g]) for g in range(n)]
            _, alphas, _ = weights([t[1] for t in toks])
            acc = alphas[0] * toks[0][0]
            for g in range(1, n):
                acc = acc + alphas[g] * toks[g][0]
            refs[2 * n][...] = acc

        return pl.pallas_call(
            body, name=name, out_shape=jax.ShapeDtypeStruct((SEQ, 2 * HEAD_DIM), F32), grid=(SEQ // tr,),
            in_specs=o_specs + l_specs, out_specs=tok, scratch_shapes=scratch, compiler_params=_params(("parallel",)),
        )(*o4, *l4)

    def body(*refs):
        do_refs, dl_refs = refs[2 * n + 1:3 * n + 1], refs[3 * n + 1:4 * n + 1]
        pairs, cols, stages = load(refs[4 * n + 1:])
        toks = [to_tokens(refs[g], refs[n + g], dils[g], pairs[g], cols[g], stages[g]) for g in range(n)]
        per_head, alphas, left = weights([t[1] for t in toks])
        dov = refs[2 * n][...]
        das = []
        for g in range(n):
            prod = dov * toks[g][0]
            das.append([jnp.sum(jnp.where(left, prod, 0.0), axis=1, keepdims=True),
                        jnp.sum(jnp.where(left, 0.0, prod), axis=1, keepdims=True)])
        dbar = [sum(per_head[h][g] * das[g][h] for g in range(n)) for h in range(2)]
        for g, d in enumerate(dils):
            pairs[g][...] = alphas[g] * dov
            for h in range(2):
                cols[g][h][...] = per_head[h][g] * (das[g][h] - dbar[h])
            for r in range(d):
                rows = pl.ds(r, tr // d, stride=d) if d > 1 else slice(None)
                v = pairs[g][rows, :]
                for h in range(2):
                    do_refs[g][h, r] = v[:, h * HEAD_DIM:(h + 1) * HEAD_DIM]
                    dl_refs[g][h, r] = cols[g][h][rows, :]

    out = pl.pallas_call(
        body, name=name,
        out_shape=[jax.ShapeDtypeStruct(o.shape, F32) for o in o4] + [jax.ShapeDtypeStruct(l.shape, F32) for l in l4],
        grid=(SEQ // tr,), in_specs=o_specs + l_specs + [tok], out_specs=o_specs + l_specs, scratch_shapes=scratch,
        compiler_params=_params(("parallel",)),
    )(*o4, *l4, dout)
    return [t.reshape(s.shape) for t, s in zip(out, list(outs) + list(lses))]


def _tri(cmp):
    r = lax.broadcasted_iota(jnp.int32, (SB_TILE, SB_TILE), 0)
    c = lax.broadcasted_iota(jnp.int32, (SB_TILE, SB_TILE), 1)
    return cmp(r, c).astype(BF16)


def _cum(x, tri, terms):
    acc, rest = None, x
    for _ in range(terms):
        part = rest.astype(BF16)
        rest = rest - part.astype(F32)
        d = _dot(part, tri, 1, 0)
        acc = d if acc is None else acc + d
    return acc


def _sb_logits(q, ks, diagonal):
    t = SB_TILE
    z = _dot(q, ks, 1, 1)
    e = jnp.exp(-jnp.abs(z))
    lf = -(jnp.maximum(z, 0.0) + jnp.log(1.0 + e))
    if not diagonal:
        return z, e, lf, None
    mask = lax.broadcasted_iota(jnp.int32, (t, t), 1) < lax.broadcasted_iota(jnp.int32, (t, t), 0)
    return z, e, jnp.where(mask, lf, 0.0), mask


def _sb_specs(h, s):
    t = SB_TILE
    tile = pl.BlockSpec((h, t, HEAD_DIM), lambda i: (0, i, 0))
    keys = pl.BlockSpec((h, s, HEAD_DIM), lambda i: (1, 0, 0))
    values = pl.BlockSpec((h, s, HEAD_DIM), lambda i: (2, 0, 0))
    return tile, keys, values, pl.BlockSpec((h, t, 1), lambda i: (0, i, 0))


def _sb_fwd(x, name):
    h, s = x.shape[0] // 3, x.shape[1]
    t = SB_TILE

    def body(q_ref, k_ref, v_ref, o_ref, tot_ref):
        i = pl.program_id(0)
        after = _tri(lambda r, c: r > c)

        def tile(j, carry, diagonal):
            rows = pl.ds(pl.multiple_of(j * t, t), t)
            out = []
            for hh, (right, acc) in enumerate(carry):
                z, _, lf, mask = _sb_logits(q_ref[hh], k_ref[hh, rows, :], diagonal)
                w = jnp.exp(z + lf + (right + _cum(lf, after, 2)))
                w = w if mask is None else jnp.where(mask, w, 0.0)
                out.append((right + jnp.sum(lf, axis=1, keepdims=True), acc + _dot(w.astype(BF16), v_ref[hh, rows, :], 1, 0)))
            return tuple(out)

        carry = tile(i, tuple((jnp.zeros((t, 1), F32), jnp.zeros((t, HEAD_DIM), F32)) for _ in range(h)), True)
        carry = lax.fori_loop(0, i, lambda jj, c: tile(i - 1 - jj, c, False), carry)
        for hh, (right, acc) in enumerate(carry):
            o_ref[hh] = acc
            tot_ref[hh] = right

    tile_spec, keys, values, col = _sb_specs(h, s)
    return pl.pallas_call(
        body, name=name, out_shape=[jax.ShapeDtypeStruct((h, s, HEAD_DIM), F32), jax.ShapeDtypeStruct((h, s, 1), F32)],
        grid=(s // t,), in_specs=[tile_spec, keys, values], out_specs=[tile_spec, col],
        compiler_params=_params(("parallel",)),
    )(x, x, x)


def _sb_bwd(x, tot, do, name):
    h, s = x.shape[0] // 3, x.shape[1]
    t = SB_TILE

    def body(q_ref, k_ref, v_ref, tot_ref, do_ref, dq_ref, dk_ref, dv_ref):
        i = pl.program_id(0)

        @pl.when(i == 0)
        def _():
            dk_ref[...] = jnp.zeros_like(dk_ref)
            dv_ref[...] = jnp.zeros_like(dv_ref)

        upto = _tri(lambda r, c: r <= c)
        before = _tri(lambda r, c: r < c)

        def tile(j, carry, diagonal):
            rows = pl.ds(pl.multiple_of(j * t, t), t)
            out = []
            for hh, (left, cleft, dq) in enumerate(carry):
                qv, ks, dob = q_ref[hh], k_ref[hh, rows, :], do_ref[hh].astype(BF16)
                z, e, lf, mask = _sb_logits(qv, ks, diagonal)
                between = tot_ref[hh] - (left + _cum(lf, upto, 2))
                w = jnp.exp(z + lf + between)
                w = w if mask is None else jnp.where(mask, w, 0.0)
                dlog = w * _dot(dob, v_ref[hh, rows, :], 1, 1)
                cfail = cleft + _cum(dlog, before, 2)
                sig = jnp.where(z >= 0.0, 1.0, e) / (1.0 + e)
                dz = dlog * (1.0 - sig) - sig * cfail
                dz = (dz if mask is None else jnp.where(mask, dz, 0.0)).astype(BF16)
                dk_ref[hh, rows, :] += _dot(dz, qv, 0, 0)
                dv_ref[hh, rows, :] += _dot(w.astype(BF16), dob, 0, 0)
                out.append((left + jnp.sum(lf, axis=1, keepdims=True), cleft + jnp.sum(dlog, axis=1, keepdims=True),
                            dq + _dot(dz, ks, 1, 0)))
            return tuple(out)

        zero = jnp.zeros((t, 1), F32)
        carry = lax.fori_loop(0, i, lambda j, c: tile(j, c, False),
                              tuple((zero, zero, jnp.zeros((t, HEAD_DIM), F32)) for _ in range(h)))
        for hh, (_, _, dq) in enumerate(tile(i, carry, True)):
            dq_ref[hh] = dq * (HEAD_DIM ** -0.5)

    tile_spec, keys, values, col = _sb_specs(h, s)
    full = pl.BlockSpec((h, s, HEAD_DIM), lambda i: (0, 0, 0))
    shp = jax.ShapeDtypeStruct((h, s, HEAD_DIM), F32)
    return pl.pallas_call(
        body, name=name, out_shape=[shp, shp, shp], grid=(s // t,),
        in_specs=[tile_spec, keys, values, col, tile_spec],
        out_specs=[tile_spec, full, full], compiler_params=_params(("arbitrary",)),
    )(x, x, x, tot, do)


COL_SB, COL_DIL, COL_SWA = 0, 3 * H_SB * HEAD_DIM, 3 * H_SB * HEAD_DIM + 3 * H_DIL * HEAD_DIM
N_SWA = H_SWA_Q + 2 * H_SWA_KV


def _dil_col(t, g):
    return COL_DIL + t * H_DIL * HEAD_DIM + g * 2 * HEAD_DIM


def _split_heads(qkv, name):
    tr = TOK_TILE
    scale = HEAD_DIM ** -0.5
    dils = [d for _, d in DIL_PATTERNS]

    def body(x_ref, sb_ref, d0_ref, d1_ref, d2_ref, swa_ref, pair):
        def head(col, scaled):
            v = x_ref[:, col:col + HEAD_DIM]
            return (v * scale if scaled else v).astype(BF16)

        for hh in range(3 * H_SB):
            sb_ref[hh] = head(COL_SB + hh * HEAD_DIM, hh < H_SB)
        for hh in range(N_SWA):
            swa_ref[hh] = head(COL_SWA + hh * HEAD_DIM, hh < H_SWA_Q)
        for t in range(3):
            for g, (d, out_ref) in enumerate(zip(dils, (d0_ref, d1_ref, d2_ref))):
                col = _dil_col(t, g)
                if d == 1:
                    for h in range(2):
                        out_ref[t * 2 + h] = head(col + h * HEAD_DIM, t == 0)
                    continue
                pair[...] = x_ref[:, col:col + 2 * HEAD_DIM]
                for r in range(d):
                    v = pair[pl.ds(r, tr // d, stride=d), :]
                    v = v * scale if t == 0 else v
                    for h in range(2):
                        out_ref[t * 2 * d + h * d + r] = v[:, h * HEAD_DIM:(h + 1) * HEAD_DIM].astype(BF16)

    def heads(n, length):
        return jax.ShapeDtypeStruct((n, length, HEAD_DIM), BF16)

    def spec(n, rows):
        return pl.BlockSpec((n, rows, HEAD_DIM), lambda i: (0, i, 0))

    return pl.pallas_call(
        body, name=name,
        out_shape=[heads(3 * H_SB, SEQ)] + [heads(6 * d, SEQ // d) for d in dils] + [heads(N_SWA, SEQ)],
        grid=(SEQ // tr,), in_specs=[pl.BlockSpec((tr, D_QKV), lambda i: (i, 0))],
        out_specs=[spec(3 * H_SB, tr)] + [spec(6 * d, tr // d) for d in dils] + [spec(N_SWA, tr)],
        scratch_shapes=[pltpu.VMEM((tr, 2 * HEAD_DIM), F32)], compiler_params=_params(("parallel",)),
    )(qkv)


def _join_heads(sb, dil, swa, name):
    tr = TOK_TILE
    dils = [d for _, d in DIL_PATTERNS]

    def body(*refs):
        sb_refs, dil_refs, swa_refs = refs[:3], [refs[3 + 3 * g:6 + 3 * g] for g in range(3)], refs[12:15]
        o_ref, pair, stages = refs[15], refs[16], refs[17:]

        def put(col, v):
            o_ref[:, col:col + v.shape[1]] = v.astype(BF16)

        for t in range(3):
            for h in range(H_SB):
                put(COL_SB + (t * H_SB + h) * HEAD_DIM, sb_refs[t][h])
        col = COL_SWA
        for ref in swa_refs:
            for h in range(ref.shape[0]):
                put(col, ref[h])
                col += HEAD_DIM
        for t in range(3):
            for g, d in enumerate(dils):
                ref, col = dil_refs[g][t], _dil_col(t, g)
                if d == 1:
                    for h in range(2):
                        put(col + h * HEAD_DIM, ref[h])
                    continue
                stage = stages[g - 1]
                for r in range(d):
                    stage[:, :HEAD_DIM] = ref[r]
                    stage[:, HEAD_DIM:] = ref[d + r]
                    pair[pl.ds(r, tr // d, stride=d), :] = stage[...]
                put(col, pair[...])

    def spec(n, rows):
        return pl.BlockSpec((n, rows, HEAD_DIM), lambda i: (0, i, 0))

    ins = list(sb) + [t for g in range(3) for t in dil[g]] + list(swa)
    in_specs = ([spec(H_SB, tr)] * 3 + [spec(2 * d, tr // d) for d in dils for _ in range(3)]
                + [spec(H_SWA_Q, tr), spec(H_SWA_KV, tr), spec(H_SWA_KV, tr)])
    return pl.pallas_call(
        body, name=name, out_shape=jax.ShapeDtypeStruct((SEQ, D_QKV), BF16), grid=(SEQ // tr,), in_specs=in_specs,
        out_specs=pl.BlockSpec((tr, D_QKV), lambda i: (i, 0)),
        scratch_shapes=[pltpu.VMEM((tr, 2 * HEAD_DIM), F32)] + [pltpu.VMEM((tr // d, 2 * HEAD_DIM), F32) for d in dils[1:]],
        compiler_params=_params(("parallel",)),
    )(*ins)


def _mixer_fwd(qkv, bias, sinks_l, tag):
    sb, d0, d1, d2, swa = _split_heads(qkv, name=f"split_heads_{tag}")
    st = {"sb": sb, "dil": (d0, d1, d2), "swa": swa}
    o_sb, st["sb_tot"] = _sb_fwd(sb, name=f"sb_fwd_{tag}")
    st["dil_out"], st["dil_lse"], st["dil_sink"] = [], [], []
    for gi, (_, d) in enumerate(DIL_PATTERNS):
        sink = jnp.zeros((2 * d, 1, LANES), F32)
        og, lg = _band_fwd(st["dil"][gi], bias[2 * gi:2 * gi + 2], sink, nq=2 * d, offs=(0, 2 * d, 4 * d), g=1, bias_div=d,
                           has_sink=False, name=f"dil{gi}_fwd_{tag}")
        st["dil_out"].append(og)
        st["dil_lse"].append(lg)
        st["dil_sink"].append(sink)
    o_dil = _dil_merge(st["dil_out"], st["dil_lse"], None, name=f"dil_merge_fwd_{tag}")
    st["swa_sink"] = jnp.broadcast_to(sinks_l.reshape(H_SWA_Q, 1, 1), (H_SWA_Q, 1, LANES))
    st["swa_out"] = _band_fwd(swa, bias[H_DIL:], st["swa_sink"], nq=H_SWA_Q, offs=(0, H_SWA_Q, H_SWA_Q + H_SWA_KV),
                              g=H_SWA_Q // H_SWA_KV, bias_div=1, has_sink=True, name=f"swa_fwd_{tag}")
    return (o_sb, o_dil, st["swa_out"][0]), st


def _mixer_bwd(st, bias, do_sb, do_dil, do_swa, tag):
    d_sb = _sb_bwd(st["sb"], st["sb_tot"], do_sb, name=f"sb_bwd_{tag}")
    dmerge = _dil_merge(st["dil_out"], st["dil_lse"], do_dil, name=f"dil_merge_bwd_{tag}")
    d_dil, dbs = [], []
    for gi, (_, d) in enumerate(DIL_PATTERNS):
        dq, dk, dv, db, _ = _band_bwd(st["dil"][gi], bias[2 * gi:2 * gi + 2], st["dil_sink"][gi], st["dil_out"][gi],
                                      st["dil_lse"][gi], dmerge[gi], dmerge[3 + gi], nq=2 * d, offs=(0, 2 * d, 4 * d),
                                      g=1, bias_div=d, has_sink=False, name=f"dil{gi}_bwd_{tag}")
        d_dil.append((dq, dk, dv))
        dbs.append(db)
    o_sw, l_sw = st["swa_out"]
    dq_sw, dk_sw, dv_sw, db_sw, dsink = _band_bwd(st["swa"], bias[H_DIL:], st["swa_sink"], o_sw, l_sw, do_swa,
                                                  jnp.zeros_like(l_sw), nq=H_SWA_Q, offs=(0, H_SWA_Q, H_SWA_Q + H_SWA_KV),
                                                  g=H_SWA_Q // H_SWA_KV, bias_div=1, has_sink=True, name=f"swa_bwd_{tag}")
    dqkv = _join_heads(d_sb, d_dil, (dq_sw, dk_sw, dv_sw), name=f"join_heads_{tag}")
    return dqkv, jnp.concatenate(dbs + [db_sw], 0), dsink[:, 0, 0]


PIECES = ("ffn0", "mix", "ffn1")


def _ffn_fwd(x_in, w, gain, mod_j, tag, after=None):
    st = {"x": x_in, "w": w}
    st["h"] = _norm_fwd(x_in, _row(gain), _row(mod_j[1]), _row(mod_j[0]), name=f"norm_fwd_{tag}", after=after)
    st["a"], st["u"], st["s"] = _ffn_up(st["h"], w["gate"], w["up"], name=f"up_{tag}")
    st["f"], x_out = _mm(st["s"], w["down"], res=x_in, colscale=_row(0.5 * mod_j[2]), emit_acc=True, tm=512, tn=1024,
                         name=f"down_{tag}")
    return x_out, st


def _ffn_bwd(dx_out, st, gain, mod_j, tag, done):
    w = st["w"]

    def latest(new, old):
        return old if new is None else new

    df, dgate = _gate_bwd(dx_out, st["f"], _row(0.5 * mod_j[2]), 0.5, name=f"gate_bwd_{tag}")
    token = done({"down": _mm_tn(st["s"], df, tm=D_FF // 2, name=f"dwd_{tag}")})
    da, du = _ffn_bwd_ds(df, w["down"], st["a"], st["u"], name=f"ds_{tag}")
    token = latest(done({"gate": _mm_tn(da, st["h"], after=token, tm=D_FF // 2, name=f"dwg_{tag}")}), token)
    token = latest(done({"up": _mm_tn(du, st["h"], after=token, tm=D_FF // 2, name=f"dwu_{tag}")}), token)
    dh = _mm2(da, w["gate"], du, w["up"], after=token, name=f"dh_{tag}")
    dx_in, sum_dh, sum_dhx = _norm_bwd(st["x"], dh, dx_out, _row(gain), _row(mod_j[1]), name=f"norm_bwd_{tag}")
    dmod = jnp.concatenate([sum_dh, gain * sum_dhx, dgate], 0)
    return dx_in, dmod, (1.0 + mod_j[1]) * sum_dhx[0]


def _mix_fwd(x_in, w, gain, mod_j, bias, sinks_l, tag, after=None):
    st = {"x": x_in, "w": w}
    st["h"] = _norm_fwd(x_in, _row(gain), _row(mod_j[1]), _row(mod_j[0]), name=f"norm_fwd_mix_{tag}", after=after)
    qkv = _mm(st["h"], w["qkv"], tb=True, tm=SEQ, name=f"qkv_{tag}")
    st["gates"] = _mm(st["h"], w["gates"], tb=True, tm=SEQ, name=f"gates_{tag}")
    outs, st["mix"] = _mixer_fwd(qkv, bias, sinks_l, tag)
    st["merged"], *st["t"] = _merge_fwd(*outs, st["gates"], w["br_sb"], w["br_dil"], w["br_swa"], name=f"merge_fwd_{tag}")
    st["f"], x_out = _mm(st["merged"], w["out"], res=x_in, colscale=_row(mod_j[2]), emit_acc=True, name=f"out_{tag}")
    return x_out, st


def _mix_bwd(dx_out, st, gain, mod_j, bias, tag, done):
    w = st["w"]
    df, dgate = _gate_bwd(dx_out, st["f"], _row(mod_j[2]), 1.0, name=f"gate_bwd_mix_{tag}")
    g = {"out": _mm_tn(st["merged"], df, name=f"dw_out_{tag}")}
    dmerged = _mm(df, w["out"], tb=True, name=f"dmerged_{tag}")
    dgates, do_sb, do_dil, do_swa, dbr_sb, dbr_dil, dbr_swa = _merge_bwd(
        dmerged, *st["t"], st["gates"], w["br_sb"], w["br_dil"], w["br_swa"], name=f"merge_bwd_{tag}")
    g["br_sb"] = _mm_tn(st["t"][0], dbr_sb, name=f"dw_br_sb_{tag}")
    g["br_dil"] = _mm_tn(st["t"][1], dbr_dil, name=f"dw_br_dil_{tag}")
    g["br_swa"] = _mm_tn(st["t"][2], dbr_swa, name=f"dw_br_swa_{tag}")
    dqkv, dbias, dsinks = _mixer_bwd(st["mix"], bias, do_sb, do_dil, do_swa, tag)
    g["qkv"] = _mm_tn(dqkv, st["h"], name=f"dw_qkv_{tag}")
    g["gates"] = _mm_tn(dgates, st["h"], name=f"dw_gates_{tag}")
    dh = _mm2(dqkv, w["qkv"], dgates, w["gates"], after=done(g), tm=512, name=f"dh_mix_{tag}")
    dx_in, sum_dh, sum_dhx = _norm_bwd(st["x"], dh, dx_out, _row(gain), _row(mod_j[1]), name=f"norm_bwd_mix_{tag}")
    dmod = jnp.concatenate([sum_dh, gain * sum_dhx, dgate], 0)
    return dx_in, dmod, (1.0 + mod_j[1]) * sum_dhx[0], dbias, dsinks


def _local_step(x, target, mod, gains, weights_of, rel_bias, sinks, final_gain, grads_done):
    tables = jnp.asarray(_bucket_tables())
    bias = _bias_build(rel_bias, tables, name="bias_build")
    states, h = [], x
    for l in range(DEPTH):
        st = {}
        for j, piece in enumerate(PIECES):
            w, after = weights_of(l, piece, h)
            if piece == "mix":
                h, st[piece] = _mix_fwd(h, w, gains[l, j], mod[l, j], bias, sinks[l], f"l{l}", after)
            else:
                h, st[piece] = _ffn_fwd(h, w, gains[l, j], mod[l, j], f"{piece}_l{l}", after)
        states.append(st)
    loss, dx, dfinal = _final_loss(h, target, _row(final_gain), name="final_loss")
    dmods = [[None] * 3 for _ in range(DEPTH)]
    dgains = [[None] * 3 for _ in range(DEPTH)]
    dsinks = [None] * DEPTH
    dbias = None
    for l in reversed(range(DEPTH)):
        for j in reversed(range(3)):
            piece = PIECES[j]
            done = lambda grads, l=l, piece=piece: grads_done(l, piece, grads)
            if piece == "mix":
                dx, dmods[l][j], dgains[l][j], db, dsinks[l] = _mix_bwd(dx, states[l][piece], gains[l, j], mod[l, j], bias, f"l{l}", done)
                dbias = db if dbias is None else dbias + db
            else:
                dx, dmods[l][j], dgains[l][j] = _ffn_bwd(dx, states[l][piece], gains[l, j], mod[l, j], f"{piece}_l{l}", done)
    drel = _bias_grad(dbias, tables, name="bias_grad")[:, 0, :N_BUCKETS].T
    dmod = jnp.stack([jnp.stack(m) for m in dmods])
    dgain = jnp.stack([jnp.stack(g) for g in dgains])
    return loss, dx, dmod, dgain, dfinal[0], drel, jnp.stack(dsinks)


BR_ROWS = (H_SB * HEAD_DIM, 2 * HEAD_DIM, H_SWA_Q * HEAD_DIM)


def _lanes_unshard(g, lead):
    _, rows, _ = g.shape
    r = rows // lead
    return g.reshape(N_DEV, lead, r, LANES).transpose(1, 2, 0, 3).reshape(lead, r, N_DEV * LANES)


def _lanes_shard(full):
    lead, r, _ = full.shape
    return full.reshape(lead, r, N_DEV, LANES).transpose(2, 0, 1, 3).reshape(N_DEV, lead * r, LANES)


def _pack_rows(parts, dtype):
    flat = jnp.concatenate([p.astype(dtype).reshape(-1) for p in parts])
    pad = (-flat.shape[0]) % (16 * LANES)
    if pad:
        flat = jnp.concatenate([flat, jnp.zeros((pad,), dtype)])
    return flat.reshape(-1, LANES)


def _unshard(gathered, axis):
    moved = jnp.moveaxis(gathered, 0, axis)
    shape = list(moved.shape)
    shape[axis:axis + 2] = [shape[axis] * shape[axis + 1]]
    return moved.reshape(shape)


def kernel(x, c, w_ada, b_ada, norm_gain, w_ffn_gate, w_ffn_up, w_ffn_down, w_in, w_br_sb, w_br_dil, w_br_swa, w_out, sinks, rel_bias, final_gain, loss_target, m_w_ada, m_b_ada, m_norm_gain, m_w_ffn_gate, m_w_ffn_up, m_w_ffn_down, m_w_in, m_w_br_sb, m_w_br_dil, m_w_br_swa, m_w_out, m_sinks, m_rel_bias, m_final_gain, v_w_ada, v_b_ada, v_norm_gain, v_w_ffn_gate, v_w_ffn_up, v_w_ffn_down, v_w_in, v_w_br_sb, v_w_br_dil, v_w_br_swa, v_w_out, v_sinks, v_rel_bias, v_final_gain):
    me = 4 * lax.axis_index("x") + 2 * lax.axis_index("y") + lax.axis_index("c")
    d = D_MODEL
    gate_t, up_t, in_t = jnp.swapaxes(w_ffn_gate, 2, 3), jnp.swapaxes(w_ffn_up, 2, 3), jnp.swapaxes(w_in, 1, 2)

    def piece_shards(l, piece):
        bf = lambda t: t.astype(BF16)
        if piece == "mix":
            return [bf(in_t[l]), jnp.concatenate([bf(w_br_sb[l]), bf(w_br_dil[l]), bf(w_br_swa[l])], 0), bf(w_out[l])]
        i = PIECES.index(piece) // 2
        return [bf(gate_t[l, i]), bf(up_t[l, i]), bf(w_ffn_down[l, i])]

    br_off = np.concatenate([[0], np.cumsum(BR_ROWS)])

    def piece_weights(gathered, piece):
        if piece == "mix":
            g_in, g_br, g_out = gathered
            f_in = g_in.reshape(D_QKV + D_GATES, d)
            f_br = [_lanes_unshard(g_br[:, br_off[k]:br_off[k + 1]], 1)[0] for k in range(3)]
            return {"qkv": f_in[:D_QKV], "gates": f_in[D_QKV:], "br_sb": f_br[0], "br_dil": f_br[1], "br_swa": f_br[2],
                    "out": g_out.reshape(d, d)}
        return {n: g.reshape(D_FF, d) for n, g in zip(("gate", "up", "down"), gathered)}

    small, = _all_gather([_pack_rows([c, norm_gain], F32)], name="gather_cond")
    c_all = small[:, :d // LANES].reshape(N_DEV, d)
    gains = _unshard(small[:, d // LANES:d // LANES + 6].reshape(N_DEV, DEPTH, 3, LANES), 2)

    cols = w_ada.shape[2]
    mod_cols = jnp.stack([_ada_fwd(c_all, w_ada[l], name=f"ada_fwd_l{l}") for l in range(DEPTH)])
    mod_all, = _all_gather([_pack_rows([mod_cols], F32)], name="gather_mod")
    mod_all = mod_all.reshape(N_DEV, -1)[:, :DEPTH * N_DEV * cols].reshape(N_DEV, DEPTH, N_DEV, cols)
    mod_mine = lax.dynamic_index_in_dim(mod_all, me, axis=2, keepdims=False)
    mod = (mod_mine.transpose(1, 0, 2).reshape(DEPTH, N_DEV * cols) + b_ada).reshape(DEPTH, 3, 3, d)

    order = [(l, piece) for l in range(DEPTH) for piece in PIECES]
    eager, ahead = 2, 3
    in_flight = {}
    n_tensors = 3
    first = _all_gather([s for k in range(eager) for s in piece_shards(*order[k])], after=mod_all, name="gather_first")

    def start_gather(k, after):
        l, piece = order[k]
        in_flight[k], token = _exchange_start(piece_shards(l, piece), after, gather=True, name=f"gather_{piece}_l{l}_start")
        return token

    token = first[0]
    for k in range(eager, eager + ahead - 1):
        token = start_gather(k, token)
    mod = mod + token[0, 0]

    def weights_of(l, piece, h):
        k = order.index((l, piece))
        started = eager <= k + ahead < len(order) and k + ahead not in in_flight
        token = start_gather(k + ahead, h) if started else None
        if k < eager:
            return piece_weights(first[n_tensors * k:n_tensors * (k + 1)], piece), token
        landed = _exchange_wait(in_flight[k], h if token is None else token, gather=True, name=f"gather_{piece}_l{l}_wait")
        return piece_weights(landed, piece), token

    exchanges, have = {}, {}

    def grads_done(l, piece, g):
        key = (l, piece)
        have.setdefault(key, {}).update(g)
        if piece == "mix":
            if len(have[key]) < 6:
                return None
            g = have[key]
            s_br = jnp.concatenate([_lanes_shard(g[n][None]) for n in ("br_sb", "br_dil", "br_swa")], 1)
            groups = [(("in", "br", "out"), [jnp.concatenate([g["qkv"], g["gates"]], 0).reshape(N_DEV, -1, d), s_br,
                                             g["out"].reshape(N_DEV, -1, d)])]
        elif key == order[0]:
            groups = [((n,), [t.reshape(N_DEV, -1, d)]) for n, t in g.items()]
        elif len(have[key]) < 3:
            return None
        else:
            groups = [(("gate", "up", "down"), [have[key][n].reshape(N_DEV, -1, d) for n in ("gate", "up", "down")])]
        token = None
        for names, sg in groups:
            state, token = _exchange_start(sg, sg[0], gather=False, name=f"exchange_{piece}_l{l}_{names[0]}_start")
            exchanges.setdefault(key, []).append((names, state))
        return token

    loss, dx, dmod, dgains, dfinal, drel, dsinks = _local_step(
        x[0], loss_target[0], mod, gains, weights_of, rel_bias, sinks, final_gain, grads_done)

    flat = lambda t: t.reshape(-1, t.shape[-1])
    transposed = lambda ts: tuple(flat(jnp.swapaxes(t, -1, -2)) for t in ts)
    families = {
        "gate": transposed((w_ffn_gate, m_w_ffn_gate, v_w_ffn_gate)), "up": transposed((w_ffn_up, m_w_ffn_up, v_w_ffn_up)),
        "down": tuple(flat(t) for t in (w_ffn_down, m_w_ffn_down, v_w_ffn_down)),
        "in": transposed((w_in, m_w_in, v_w_in)),
        "br": tuple(flat(jnp.concatenate(ts, 1)) for ts in ((w_br_sb, w_br_dil, w_br_swa), (m_w_br_sb, m_w_br_dil, m_w_br_swa),
                                                            (v_w_br_sb, v_w_br_dil, v_w_br_swa))),
        "out": tuple(flat(t) for t in (w_out, m_w_out, v_w_out))}
    parts, stepped = {}, {}

    def land(l, after):
        for key in reversed([k for k in order if k[0] == l]):
            for names, ex_state in exchanges[key]:
                landed = _exchange_wait(ex_state, after, gather=False, name=f"exchange_{key[1]}_l{key[0]}_{names[0]}_wait")
                parts.setdefault(key, {}).update(zip(names, landed))
                after = landed[0]

    def step_layer(l):
        last = None
        for n, (w2, m2, v2) in families.items():
            groups = [parts[key][n] for key in order if key[0] == l and n in parts[key]]
            rows_per_layer = w2.shape[0] // DEPTH
            stepped[n] = _reduce_adamw(groups, w2, m2, v2, l * rows_per_layer, stepped.get(n), after=last,
                                       name=f"reduce_adamw_{n}_l{l}")
            last = stepped[n][1]
        return last

    land(1, dx)
    after_l1 = step_layer(1)

    small_parts = [dmod, dgains, dfinal, drel.T, dsinks, loss[0, :1]]
    small_sizes = [int(np.prod(p.shape)) for p in small_parts]
    small_all, = _all_gather([_pack_rows(small_parts, F32)], after=after_l1, name="gather_small")
    small_sum = _sum_parts([small_all], name="sum_small").reshape(-1)
    offs = np.concatenate([[0], np.cumsum(small_sizes)])
    g_b_ada = small_sum[offs[0]:offs[1]].reshape(DEPTH, 9 * d)
    g_gain_full = small_sum[offs[1]:offs[2]].reshape(DEPTH, 3, d)
    g_norm_gain = lax.dynamic_slice_in_dim(g_gain_full, me * LANES, LANES, axis=2)
    g_final = small_sum[offs[2]:offs[3]]
    g_rel = small_sum[offs[3]:offs[4]].reshape(N_SOFT, N_BUCKETS).T
    g_sinks = small_sum[offs[4]:offs[5]].reshape(DEPTH, H_SWA_Q)
    loss_total = small_sum[offs[5]]

    dmod_all = small_all.reshape(N_DEV, -1)[:, :DEPTH * 9 * d].reshape(N_DEV, DEPTH, 9 * d)
    dmod_cols = lax.dynamic_slice_in_dim(dmod_all, me * cols, cols, axis=2)
    g_w_ada = jnp.stack([_ada_bwd(c_all.T, dmod_cols[:, l], name=f"ada_bwd_l{l}") for l in range(DEPTH)])

    small_state = {"w_ada": (w_ada, m_w_ada, v_w_ada), "b_ada": (b_ada, m_b_ada, v_b_ada),
                   "norm_gain": (norm_gain, m_norm_gain, v_norm_gain), "sinks": (sinks, m_sinks, v_sinks),
                   "rel_bias": (rel_bias, m_rel_bias, v_rel_bias), "final_gain": (final_gain, m_final_gain, v_final_gain)}
    grad, update = {}, {}
    for n, g in (("w_ada", g_w_ada), ("b_ada", g_b_ada), ("norm_gain", g_norm_gain), ("sinks", g_sinks),
                 ("rel_bias", g_rel), ("final_gain", g_final)):
        w, m, v = small_state[n]
        grad[n] = g
        if w.ndim == 1:
            update[n] = tuple(t.reshape(w.shape) for t in _adamw(_row(w), _row(g), _row(m), _row(v), name=f"adamw_{n}"))
        else:
            update[n] = _adamw(w, g, m, v, name=f"adamw_{n}")

    land(0, update["w_ada"][0])
    step_layer(0)

    def unflat(n, like, swapped):
        shape = jnp.swapaxes(like, -1, -2).shape if swapped else like.shape
        out = [t.reshape(shape) for t in stepped[n]]
        return [jnp.swapaxes(t, -1, -2) for t in out] if swapped else out

    results = {"w_ffn_gate": unflat("gate", w_ffn_gate, True), "w_ffn_up": unflat("up", w_ffn_up, True),
               "w_ffn_down": unflat("down", w_ffn_down, False), "w_in": unflat("in", w_in, True),
               "w_out": unflat("out", w_out, False)}
    br = [t.reshape(DEPTH, -1, LANES) for t in stepped["br"]]
    for k, n in enumerate(("w_br_sb", "w_br_dil", "w_br_swa")):
        results[n] = [t[:, br_off[k]:br_off[k + 1]] for t in br]
    for n, (g, dl, nm, nv) in results.items():
        grad[n], update[n] = g, (dl, nm, nv)

    names = ["w_ada", "b_ada", "norm_gain", "w_ffn_gate", "w_ffn_up", "w_ffn_down", "w_in", "w_br_sb", "w_br_dil",
             "w_br_swa", "w_out", "sinks", "rel_bias", "final_gain"]
    return (loss_total, dx[None], *[grad[n] for n in names], *[update[n][0] for n in names],
            *[update[n][1] for n in names], *[update[n][2] for n in names])
```

```python
import math

import numpy as np
import jax
import jax.numpy as jnp
from jax import lax
from jax.experimental import pallas as pl
from jax.experimental.pallas import tpu as pltpu

F32, BF16 = jnp.float32, jnp.bfloat16

SEQ, D_MODEL, D_FF, HEAD_DIM = 2048, 1024, 2816, 64
DEPTH = 2
BLK = 128
H_SB, H_DIL, H_SWA_Q, H_SWA_KV = 4, 6, 6, 2
DIL_PATTERNS = ((128, 1), (512, 4), (2048, 16))
SWA_WINDOW = 128
N_BUCKETS, MAX_REL_DIST = 32, 2048
RMS_EPS = 1e-6
D_QKV = 2560
D_GATES = 3 * D_MODEL
ADAM_LR, ADAM_B1, ADAM_B2, ADAM_EPS, ADAM_WD, ADAM_STEP = 0.001, 0.9, 0.999, 1e-08, 0.01, 10

N_DEV = 8
LANES = 128
NEG = -1e30
SB_TILE = 512
VMEM_LIMIT_BYTES = 48 * 1024 * 1024
HBM = pl.BlockSpec(memory_space=pltpu.HBM)
MESH = pl.DeviceIdType.MESH


def _tile(n, target):
    t = (min(n, target) // LANES) * LANES
    while t >= LANES:
        if n % t == 0:
            return t
        t -= LANES
    return n


def _row_tile(r, cap):
    t = (min(r, cap) // 16) * 16
    while t > 16 and r % t:
        t -= 16
    return t


def _params(semantics=None):
    return pltpu.CompilerParams(dimension_semantics=semantics, vmem_limit_bytes=VMEM_LIMIT_BYTES)


def _dot(a, b, ca, cb):
    return lax.dot_general(a, b, (((ca,), (cb,)), ((), ())), preferred_element_type=F32)


def _sigmoid(a):
    return 1.0 / (1.0 + jnp.exp(-a))


def _row(v):
    return v.reshape(1, -1)


def _all_gather(arrs, name, after=None):
    n = len(arrs)
    ins = list(arrs) + ([] if after is None else [after])

    def body(*refs):
        x_refs, out_refs = refs[:n], refs[len(ins):len(ins) + n]
        send_sems, recv_sems, local_sems = refs[len(ins) + n:]
        x, y, c = lax.axis_index("x"), lax.axis_index("y"), lax.axis_index("c")
        me, sibling = (x, y, c), (x, y, 1 - c)
        chips = [(1 - x, y), (x, 1 - y), (1 - x, 1 - y)]

        def slot(t, px, py, pc):
            return out_refs[t].at[4 * px + 2 * py + pc]

        def copy(t, k, block, to, src=None):
            return pltpu.make_async_remote_copy(
                src_ref=slot(t, *block) if src is None else src, dst_ref=slot(t, *block),
                send_sem=send_sems.at[7 * t + k], recv_sem=recv_sems.at[7 * t + k], device_id=to, device_id_type=MESH)

        mine = [pltpu.make_async_copy(x_refs[t], slot(t, *me), local_sems.at[t]) for t in range(n)]
        for cp in mine:
            cp.start()
        first = []
        for t in range(n):
            first.append(copy(t, 0, me, sibling, src=x_refs[t]))
            first += [copy(t, 1 + j, me, (*chip, c), src=x_refs[t]) for j, chip in enumerate(chips)]
        for cp in first:
            cp.start()
        passed = []
        for j, chip in enumerate(chips):
            for t in range(n):
                copy(t, 1 + j, (*chip, c), me).wait_recv()
                passed.append(copy(t, 4 + j, (*chip, c), sibling))
                passed[-1].start()
        for t in range(n):
            copy(t, 0, sibling, me).wait_recv()
        for j, chip in enumerate(chips):
            for t in range(n):
                copy(t, 4 + j, (*chip, 1 - c), me).wait_recv()
        for cp in first + passed:
            cp.wait_send()
        for cp in mine:
            cp.wait()

    return pl.pallas_call(
        body, name=name, out_shape=[jax.ShapeDtypeStruct((N_DEV,) + a.shape, a.dtype) for a in arrs],
        in_specs=[HBM] * n + [pl.BlockSpec(memory_space=pl.ANY)] * (len(ins) - n), out_specs=[HBM] * n,
        scratch_shapes=[pltpu.SemaphoreType.DMA((7 * n,)), pltpu.SemaphoreType.DMA((7 * n,)), pltpu.SemaphoreType.DMA((n,))],
    )(*ins)


def _direct_copies(x_refs, land_refs, send_sems, recv_sems, local_sems, gather):
    x, y, c = lax.axis_index("x"), lax.axis_index("y"), lax.axis_index("c")
    me = 4 * x + 2 * y + c
    sends, recvs = [], []
    for k in range(1, N_DEV):
        px = 1 - x if (k >> 2) & 1 else x
        py = 1 - y if (k >> 1) & 1 else y
        pc = 1 - c if k & 1 else c
        peer = 4 * px + 2 * py + pc
        for t, (x_ref, land_ref) in enumerate(zip(x_refs, land_refs)):
            sem = 7 * t + k - 1
            for out, src, slot in ((sends, x_ref if gather else x_ref.at[peer], me),
                                   (recvs, x_ref if gather else x_ref.at[me], peer)):
                out.append(pltpu.make_async_remote_copy(
                    src_ref=src, dst_ref=land_ref.at[slot], send_sem=send_sems.at[sem], recv_sem=recv_sems.at[sem],
                    device_id=(px, py, pc), device_id_type=MESH))
    own = [pltpu.make_async_copy(x_ref if gather else x_ref.at[me], land_ref.at[me], local_sems.at[t])
           for t, (x_ref, land_ref) in enumerate(zip(x_refs, land_refs))]
    return sends, recvs, own


SEM =pl.BlockSpec(memory_space=pltpu.SEMAPHORE)
ANY = pl.BlockSpec(memory_space=pl.ANY)
SIDE_EFFECT = pltpu.SideEffectType.DATAFLOW_SIDE_EFFECTING


def _exchange_start(arrs, after, *, gather, name):
    n = len(arrs)
    lands = [lax.empty(((N_DEV,) + a.shape) if gather else a.shape, a.dtype) for a in arrs]
    extra = [] if after is None else [after]

    def body(*refs):
        sems = refs[2 * n + len(extra):2 * n + len(extra) + 3]
        sends, _, own = _direct_copies(refs[:n], refs[n:2 * n], *sems, gather)
        for cp in own + sends:
            cp.start()
        refs[-1][...] = jnp.zeros_like(refs[-1])

    ops = [pltpu.with_memory_space_constraint(a, pltpu.HBM) for a in list(arrs) + lands]
    out = pl.pallas_call(
        body, name=name,
        out_shape=(pltpu.SemaphoreType.DMA((7 * n,)), pltpu.SemaphoreType.DMA((7 * n,)), pltpu.SemaphoreType.DMA((n,)),
                   *[pltpu.HBM(a.shape, a.dtype) for a in ops], jax.ShapeDtypeStruct((8, LANES), F32)),
        in_specs=[HBM] * (2 * n) + [ANY] * len(extra),
        out_specs=(SEM, SEM, SEM, *[HBM] * (2 * n), pl.BlockSpec(memory_space=pltpu.VMEM)),
        input_output_aliases={t: 3 + t for t in range(2 * n)},
        compiler_params=pltpu.CompilerParams(has_side_effects=SIDE_EFFECT),
    )(*ops, *extra)
    return (out[:3], out[3:3 + n], out[3 + n:3 + 2 * n]), out[-1]


def _exchange_wait(state, after, *, gather, name):
    sems, arrs, lands = state
    n = len(arrs)

    def body(*refs):
        sends, recvs, own = _direct_copies(refs[:n], refs[n:2 * n], *refs[2 * n:2 * n + 3], gather)
        for cp in own:
            cp.wait()
        for cp in sends:
            cp.wait_send()
        for cp in recvs:
            cp.wait_recv()

    out = pl.pallas_call(
        body, name=name, out_shape=tuple(pltpu.HBM(a.shape, a.dtype) for a in list(arrs) + list(lands)),
        in_specs=[HBM] * (2 * n) + [SEM, SEM, SEM, ANY], out_specs=tuple([HBM] * (2 * n)),
        input_output_aliases={t: t for t in range(2 * n)},
        compiler_params=pltpu.CompilerParams(has_side_effects=SIDE_EFFECT),
    )(*arrs, *lands, *sems, after)
    return out[n:]


def _sum_parts(groups, name):
    n, r, cdim = groups[0].shape
    tr = _row_tile(r, max(16, (1 << 21) // (n * cdim * groups[0].dtype.itemsize)))
    steps = r // tr

    def body(*refs):
        o_ref = refs[-1]
        gg = pl.program_id(0)
        for gi in range(len(groups)):
            @pl.when(gg == gi)
            def _(gi=gi):
                acc = refs[gi][0].astype(F32)
                for k in range(1, n):
                    acc = acc + refs[gi][k].astype(F32)
                o_ref[...] = acc

    def in_spec(gi):
        return pl.BlockSpec((n, tr, cdim), lambda gg, i: (0, jnp.where(gg == gi, i, 0), 0))

    return pl.pallas_call(
        body, name=name, out_shape=jax.ShapeDtypeStruct((len(groups) * r, cdim), F32), grid=(len(groups), steps),
        in_specs=[in_spec(gi) for gi in range(len(groups))],
        out_specs=pl.BlockSpec((tr, cdim), lambda gg, i: (gg * steps + i, 0)),
        compiler_params=_params(("parallel", "parallel")),
    )(*groups)


def _mm_tn(a, b, *, name, after=None, tm=512, tn=1024, out_rows=None, row0=0, prev=None):
    k, m = a.shape
    n = b.shape[1]
    tm, tn = _tile(m, tm), _tile(n, tn)
    out_rows = m if out_rows is None else out_rows

    def body(a_ref, b_ref, *rest):
        o_ref, at_ref = rest[-2], rest[-1]

        @pl.when(pl.program_id(1) == 0)
        def _():
            at_ref[...] = a_ref[...].astype(BF16).T

        o_ref[...] = _dot(at_ref[...], b_ref[...].astype(BF16), 1, 0).astype(BF16)

    ins = [a, b] + [t for t in (after, prev) if t is not None]
    return pl.pallas_call(
        body, name=name, out_shape=jax.ShapeDtypeStruct((out_rows, n), BF16), grid=(m // tm, n // tn),
        in_specs=[pl.BlockSpec((k, tm), lambda i, j: (0, i)), pl.BlockSpec((k, tn), lambda i, j: (0, j))] + [ANY] * (len(ins) - 2),
        out_specs=pl.BlockSpec((tm, tn), lambda i, j: (row0 // tm + i, j)),
        input_output_aliases={} if prev is None else {len(ins) - 1: 0},
        scratch_shapes=[pltpu.VMEM((tm, k), BF16)], compiler_params=_params(("parallel", "arbitrary")),
    )(*ins)


def _mm2(a1, b1, a2, b2, *, name, after=None, tm=256, tn=1024, b_rows=None):
    m = a1.shape[0]
    n = b1.shape[1]
    tm, tn = _tile(m, tm), _tile(n, tn)

    def body(a1_ref, b1_ref, a2_ref, b2_ref, *rest):
        rest[-1][...] = (_dot(a1_ref[...].astype(BF16), b1_ref[...], 1, 0)
                         + _dot(a2_ref[...].astype(BF16), b2_ref[...], 1, 0))

    ins = [a1, b1, a2, b2] + ([] if after is None else [after])

    def a_spec(t):
        return pl.BlockSpec((tm, t.shape[1]), lambda i, j: (i, 0))

    def b_spec(t, a, which):
        if b_rows is None:
            return pl.BlockSpec((t.shape[0], tn), lambda i, j: (0, j))
        start = b_rows[which]
        return pl.BlockSpec((pl.Element(a.shape[1]), pl.Element(tn)), lambda i, j: (start, j * tn))

    return pl.pallas_call(
        body, name=name, out_shape=jax.ShapeDtypeStruct((m, n), F32), grid=(m // tm, n // tn),
        in_specs=[a_spec(a1), b_spec(b1, a1, 0), a_spec(a2), b_spec(b2, a2, 1)] + [ANY] * (len(ins) - 4),
        out_specs=pl.BlockSpec((tm, tn), lambda i, j: (i, j)), compiler_params=_params(("parallel", "parallel")),
    )(*ins)


def _mm(a, b, *, name, ta=False, tb=False, res=None, colscale=None, emit_acc=False,
        out_dtype=F32, tm=512, tn=512, b_rows=None):
    m, k = (a.shape[1], a.shape[0]) if ta else a.shape
    n = b.shape[0] if tb else b.shape[1]
    b_start = 0
    if b_rows is not None:
        b_start, n = b_rows
    tm, tn = _tile(m, tm), _tile(n, tn)
    ca, cb = (0 if ta else 1), (1 if tb else 0)
    a_spec = pl.BlockSpec((k, tm), lambda i, j: (0, i)) if ta else pl.BlockSpec((tm, k), lambda i, j: (i, 0))
    b_spec = (pl.BlockSpec((tn, k), lambda i, j: (b_start // tn + j, 0)) if tb
              else pl.BlockSpec((k, tn), lambda i, j: (0, j)))
    tile = pl.BlockSpec((tm, tn), lambda i, j: (i, j))
    ins, in_specs = [a, b], [a_spec, b_spec]
    if res is not None:
        ins.append(res)
        in_specs.append(tile)
    if colscale is not None:
        ins.append(colscale)
        in_specs.append(pl.BlockSpec((1, tn), lambda i, j: (0, j)))
    n_in = len(ins)

    def body(*refs):
        outs = refs[n_in:]
        acc = _dot(refs[0][...].astype(BF16), refs[1][...].astype(BF16), ca, cb)
        val, p = acc, 2
        if res is not None:
            r_val, p = refs[p][...], p + 1
        if colscale is not None:
            val = val * refs[p][...]
        if res is not None:
            val = r_val + val
        if emit_acc:
            outs[0][...] = acc
        outs[-1][...] = val.astype(out_dtype)

    out_shape = [jax.ShapeDtypeStruct((m, n), out_dtype)]
    out_specs = [tile]
    if emit_acc:
        out_shape.insert(0, jax.ShapeDtypeStruct((m, n), F32))
        out_specs.insert(0, tile)
    out = pl.pallas_call(
        body, name=name, out_shape=out_shape, grid=(m // tm, n // tn), in_specs=in_specs, out_specs=out_specs,
        compiler_params=_params(("parallel", "parallel")),
    )(*ins)
    return out if emit_acc else out[0]


def _norm_fwd(x, g, scale, shift, name, after=None):
    s, d = x.shape
    tr = 256

    def body(x_ref, g_ref, sc_ref, sh_ref, *rest):
        xv = x_ref[...]
        rstd = lax.rsqrt(jnp.mean(xv * xv, axis=-1, keepdims=True) + RMS_EPS)
        rest[-1][...] = (xv * rstd * g_ref[...] * (1.0 + sc_ref[...]) + sh_ref[...]).astype(BF16)

    rowspec = pl.BlockSpec((1, d), lambda i: (0, 0))
    ins = [x, g, scale, shift] + ([] if after is None else [after])
    return pl.pallas_call(
        body, name=name, out_shape=jax.ShapeDtypeStruct((s, d), BF16), grid=(s // tr,),
        in_specs=[pl.BlockSpec((tr, d), lambda i: (i, 0)), rowspec, rowspec, rowspec] + [ANY] * (len(ins) - 4),
        out_specs=pl.BlockSpec((tr, d), lambda i: (i, 0)),
        compiler_params=_params(("parallel",)),
    )(*ins)


def _norm_bwd(x, dh, dres, g, scale, name):
    s, d = x.shape
    tr = 256

    def body(x_ref, dh_ref, dr_ref, g_ref, sc_ref, dx_ref, a_ref, b_ref):
        @pl.when(pl.program_id(0) == 0)
        def _():
            a_ref[...] = jnp.zeros_like(a_ref)
            b_ref[...] = jnp.zeros_like(b_ref)

        xv = x_ref[...]
        rstd = lax.rsqrt(jnp.mean(xv * xv, axis=-1, keepdims=True) + RMS_EPS)
        xhat = xv * rstd
        dhv = dh_ref[...]
        dxhat = dhv * (g_ref[...] * (1.0 + sc_ref[...]))
        mean_term = jnp.mean(dxhat * xhat, axis=-1, keepdims=True)
        dx_ref[...] = dr_ref[...] + rstd * (dxhat - xhat * mean_term)
        a_ref[...] += jnp.sum(dhv, axis=0, keepdims=True)
        b_ref[...] += jnp.sum(dhv * xhat, axis=0, keepdims=True)

    rowspec = pl.BlockSpec((1, d), lambda i: (0, 0))
    tile = pl.BlockSpec((tr, d), lambda i: (i, 0))
    return pl.pallas_call(
        body, name=name,
        out_shape=[jax.ShapeDtypeStruct((s, d), F32), jax.ShapeDtypeStruct((1, d), F32), jax.ShapeDtypeStruct((1, d), F32)],
        grid=(s // tr,), in_specs=[tile, tile, tile, rowspec, rowspec], out_specs=[tile, rowspec, rowspec],
        compiler_params=_params(("arbitrary",)),
    )(x, dh, dres, g, scale)


def _gate_bwd(dxn, f, colscale, coef, name):
    s, d = dxn.shape
    tr = 256

    def body(dx_ref, f_ref, cs_ref, df_ref, dg_ref):
        @pl.when(pl.program_id(0) == 0)
        def _():
            dg_ref[...] = jnp.zeros_like(dg_ref)

        dxv = dx_ref[...]
        df_ref[...] = (dxv * cs_ref[...]).astype(BF16)
        dg_ref[...] += coef * jnp.sum(dxv * f_ref[...], axis=0, keepdims=True)

    rowspec = pl.BlockSpec((1, d), lambda i: (0, 0))
    tile = pl.BlockSpec((tr, d), lambda i: (i, 0))
    return pl.pallas_call(
        body, name=name, out_shape=[jax.ShapeDtypeStruct((s, d), BF16), jax.ShapeDtypeStruct((1, d), F32)],
        grid=(s // tr,), in_specs=[tile, tile, rowspec], out_specs=[tile, rowspec],
        compiler_params=_params(("arbitrary",)),
    )(dxn, f, colscale)


def _ffn_up(h, wg, wu, name, tm=SEQ, tn=256):
    s, d = h.shape
    f = wg.shape[0]

    def body(h_ref, wg_ref, wu_ref, a_ref, u_ref, s_ref):
        hv = h_ref[...]
        a = _dot(hv, wg_ref[...], 1, 1)
        u = _dot(hv, wu_ref[...], 1, 1)
        a_ref[...] = a.astype(BF16)
        u_ref[...] = u.astype(BF16)
        s_ref[...] = (a * _sigmoid(a) * u).astype(BF16)

    tile = pl.BlockSpec((tm, tn), lambda i, j: (i, j))
    wspec = pl.BlockSpec((tn, d), lambda i, j: (j, 0))
    return pl.pallas_call(
        body, name=name,
        out_shape=[jax.ShapeDtypeStruct((s, f), BF16), jax.ShapeDtypeStruct((s, f), BF16), jax.ShapeDtypeStruct((s, f), BF16)],
        grid=(s // tm, f // tn), in_specs=[pl.BlockSpec((tm, d), lambda i, j: (i, 0)), wspec, wspec],
        out_specs=[tile, tile, tile], compiler_params=_params(("parallel", "parallel")),
    )(h, wg, wu)


def _ffn_bwd_ds(df, wd, a, u, name, tm=SEQ, tn=256):
    s, d = df.shape
    f = wd.shape[0]

    def body(df_ref, wd_ref, a_ref, u_ref, da_ref, du_ref):
        ds = _dot(df_ref[...], wd_ref[...], 1, 1)
        av = a_ref[...].astype(F32)
        sg = _sigmoid(av)
        da_ref[...] = (ds * u_ref[...].astype(F32) * (sg * (1.0 + av * (1.0 - sg)))).astype(BF16)
        du_ref[...] = (ds * (av * sg)).astype(BF16)

    tile = pl.BlockSpec((tm, tn), lambda i, j: (i, j))
    return pl.pallas_call(
        body, name=name, out_shape=[jax.ShapeDtypeStruct((s, f), BF16), jax.ShapeDtypeStruct((s, f), BF16)],
        grid=(s // tm, f // tn),
        in_specs=[pl.BlockSpec((tm, d), lambda i, j: (i, 0)), pl.BlockSpec((tn, d), lambda i, j: (j, 0)), tile, tile],
        out_specs=[tile, tile], compiler_params=_params(("parallel", "parallel")),
    )(df, wd, a, u)


def _merge_fwd(o_sb, o_dil, o_swa, gates, wb_sb, wb_dil, wb_swa, name):
    s, d = SEQ, D_MODEL
    tm = 256

    def body(osb_ref, odl_ref, osw_ref, g_ref, wsb_ref, wdl_ref, wsw_ref, m_ref, tsb_ref, tdl_ref, tsw_ref):
        for h in range(osb_ref.shape[0]):
            tsb_ref[:, h * HEAD_DIM:(h + 1) * HEAD_DIM] = osb_ref[h].astype(BF16)
        for h in range(osw_ref.shape[0]):
            tsw_ref[:, h * HEAD_DIM:(h + 1) * HEAD_DIM] = osw_ref[h].astype(BF16)
        tdl_ref[...] = odl_ref[...].astype(BF16)
        acc = _sigmoid(g_ref[:, 0:d]) * _dot(tsb_ref[...], wsb_ref[...], 1, 0)
        acc += _sigmoid(g_ref[:, d:2 * d]) * _dot(tdl_ref[...], wdl_ref[...], 1, 0)
        acc += _sigmoid(g_ref[:, 2 * d:3 * d]) * _dot(tsw_ref[...], wsw_ref[...], 1, 0)
        m_ref[...] = acc.astype(BF16)

    def rows(w):
        return pl.BlockSpec((tm, w), lambda i: (i, 0))

    def heads(n):
        return pl.BlockSpec((n, tm, HEAD_DIM), lambda i: (0, i, 0))

    def whole(w):
        return pl.BlockSpec((w, d), lambda i: (0, 0))

    return pl.pallas_call(
        body, name=name, out_shape=[jax.ShapeDtypeStruct((s, w), BF16) for w in (d, 256, 128, 384)], grid=(s // tm,),
        in_specs=[heads(H_SB), rows(128), heads(H_SWA_Q), rows(3 * d), whole(256), whole(128), whole(384)],
        out_specs=[rows(d), rows(256), rows(128), rows(384)], compiler_params=_params(("parallel",)),
    )(o_sb, o_dil, o_swa, gates, wb_sb, wb_dil, wb_swa)


def _merge_bwd(dmerged, t_sb, t_dil, t_swa, gates, wb_sb, wb_dil, wb_swa, name):
    s, d = SEQ, D_MODEL
    tm = 256

    def body(dm_ref, tsb_ref, tdl_ref, tsw_ref, g_ref, wsb_ref, wdl_ref, wsw_ref,
             dg_ref, dosb_ref, dodl_ref, dosw_ref, dbsb_ref, dbdl_ref, dbsw_ref):
        dm = dm_ref[...]
        for idx, (t_ref, w_ref, do_ref, db_ref) in enumerate((
                (tsb_ref, wsb_ref, dosb_ref, dbsb_ref), (tdl_ref, wdl_ref, dodl_ref, dbdl_ref),
                (tsw_ref, wsw_ref, dosw_ref, dbsw_ref))):
            w = w_ref[...]
            br = _dot(t_ref[...], w, 1, 0)
            sg = _sigmoid(g_ref[:, idx * d:(idx + 1) * d])
            dbr = (dm * sg).astype(BF16)
            dg_ref[:, idx * d:(idx + 1) * d] = (dm * br * (sg * (1.0 - sg))).astype(BF16)
            db_ref[...] = dbr
            do = _dot(dbr, w, 1, 1)
            if len(do_ref.shape) == 2:
                do_ref[...] = do
            else:
                for h in range(do_ref.shape[0]):
                    do_ref[h] = do[:, h * HEAD_DIM:(h + 1) * HEAD_DIM]

    def rows(w):
        return pl.BlockSpec((tm, w), lambda i: (i, 0))

    def heads(n):
        return pl.BlockSpec((n, tm, HEAD_DIM), lambda i: (0, i, 0))

    def whole(w):
        return pl.BlockSpec((w, d), lambda i: (0, 0))

    def shp(w, dt):
        return jax.ShapeDtypeStruct((s, w), dt)

    def hshp(n):
        return jax.ShapeDtypeStruct((n, s, HEAD_DIM), F32)

    return pl.pallas_call(
        body, name=name,
        out_shape=[shp(3 * d, BF16), hshp(H_SB), shp(128, F32), hshp(H_SWA_Q), shp(d, BF16), shp(d, BF16), shp(d, BF16)],
        grid=(s // tm,),
        in_specs=[rows(d), rows(256), rows(128), rows(384), rows(3 * d), whole(256), whole(128), whole(384)],
        out_specs=[rows(3 * d), heads(H_SB), rows(128), heads(H_SWA_Q), rows(d), rows(d), rows(d)],
        compiler_params=_params(("parallel",)),
    )(dmerged, t_sb, t_dil, t_swa, gates, wb_sb, wb_dil, wb_swa)


def _final_loss(x, target, g, name):
    s, d = x.shape
    tr = 256

    def body(x_ref, t_ref, g_ref, loss_ref, dx_ref, dg_ref):
        @pl.when(pl.program_id(0) == 0)
        def _():
            loss_ref[...] = jnp.zeros_like(loss_ref)
            dg_ref[...] = jnp.zeros_like(dg_ref)

        xv = x_ref[...]
        gv = g_ref[...]
        rstd = lax.rsqrt(jnp.mean(xv * xv, axis=-1, keepdims=True) + RMS_EPS)
        xhat = xv * rstd
        err = xhat * gv - t_ref[...]
        loss_ref[...] += 0.5 * jnp.sum(jnp.mean(err * err, axis=-1, keepdims=True))
        dy = err * (1.0 / d)
        dxhat = dy * gv
        mean_term = jnp.mean(dxhat * xhat, axis=-1, keepdims=True)
        dx_ref[...] = rstd * (dxhat - xhat * mean_term)
        dg_ref[...] += jnp.sum(dy * xhat, axis=0, keepdims=True)

    rowspec = pl.BlockSpec((1, d), lambda i: (0, 0))
    tile = pl.BlockSpec((tr, d), lambda i: (i, 0))
    return pl.pallas_call(
        body, name=name,
        out_shape=[jax.ShapeDtypeStruct((1, LANES), F32), jax.ShapeDtypeStruct((s, d), F32), jax.ShapeDtypeStruct((1, d), F32)],
        grid=(s // tr,), in_specs=[tile, tile, rowspec],
        out_specs=[pl.BlockSpec((1, LANES), lambda i: (0, 0)), tile, rowspec],
        compiler_params=_params(("arbitrary",)),
    )(x, target, g)


def _adamw(w, g, m, v, name):
    shape = w.shape
    cols = shape[-1]
    rows = int(np.prod(shape[:-1])) if len(shape) > 1 else 1
    tr = rows
    for cand in (1024, 512, 256, 128, 64, 32, 16, 8):
        if rows % cand == 0 and rows > cand and cand * cols * 4 <= (1 << 21):
            tr = cand
            break

    def body(w_ref, g_ref, m_ref, v_ref, d_ref, nm_ref, nv_ref):
        d_ref[...], nm_ref[...], nv_ref[...] = _adam_update(w_ref[...], g_ref[...], m_ref[...], v_ref[...])

    tile = pl.BlockSpec((tr, cols), lambda i: (i, 0))
    flat = [t.reshape(rows, cols) for t in (w, g, m, v)]
    out = pl.pallas_call(
        body, name=name, out_shape=[jax.ShapeDtypeStruct((rows, cols), F32)] * 3, grid=(rows // tr,),
        in_specs=[tile] * 4, out_specs=[tile] * 3, compiler_params=_params(("parallel",)),
    )(*flat)
    return tuple(t.reshape(shape) for t in out)


def _adam_update(w, gv, m, v):
    nm = ADAM_B1 * m + (1.0 - ADAM_B1) * gv
    nv = ADAM_B2 * v + (1.0 - ADAM_B2) * (gv * gv)
    m_hat = nm / (1.0 - ADAM_B1 ** ADAM_STEP)
    v_hat = nv / (1.0 - ADAM_B2 ** ADAM_STEP)
    return -ADAM_LR * (m_hat / (jnp.sqrt(v_hat) + ADAM_EPS) + ADAM_WD * w), nm, nv


def _reduce_adamw(groups, w, m, v, row0, prev, name, after=None):
    n, r, cdim = groups[0].shape
    rows = w.shape[0]
    tr = _row_tile(r, max(16, (1 << 22) // (n * cdim * groups[0].dtype.itemsize)))
    steps = r // tr
    ng = len(groups)

    def body(*refs):
        w_ref, m_ref, v_ref = refs[ng:ng + 3]
        g_out, d_out, m_out, v_out = refs[-4:]
        gg = pl.program_id(0)
        for gi in range(ng):
            @pl.when(gg == gi)
            def _(gi=gi):
                acc = refs[gi][0].astype(F32)
                for k in range(1, n):
                    acc = acc + refs[gi][k].astype(F32)
                g_out[...] = acc
                d_out[...], m_out[...], v_out[...] = _adam_update(w_ref[...], acc, m_ref[...], v_ref[...])

    def part_spec(gi):
        return pl.BlockSpec((n, tr, cdim), lambda gg, i: (0, jnp.where(gg == gi, i, 0), 0))

    tile = pl.BlockSpec((tr, cdim), lambda gg, i: (row0 // tr + gg * steps + i, 0))
    extra = ([] if prev is None else list(prev)) + ([] if after is None else [after])
    return pl.pallas_call(
        body, name=name, out_shape=[jax.ShapeDtypeStruct((rows, cdim), F32)] * 4, grid=(ng, steps),
        in_specs=[part_spec(gi) for gi in range(ng)] + [tile] * 3 + [ANY] * len(extra), out_specs=[tile] * 4,
        input_output_aliases={} if prev is None else {ng + 3 + k: k for k in range(4)},
        compiler_params=_params(("parallel", "parallel")),
    )(*groups, w, m, v, *extra)


def _ada_fwd(c_all, w, name):
    n = w.shape[1]

    def body(c_ref, w_ref, o_ref):
        cv = c_ref[...]
        o_ref[...] = jnp.dot(cv * _sigmoid(cv), w_ref[...], preferred_element_type=F32, precision=lax.Precision.HIGHEST)

    return pl.pallas_call(body, name=name, out_shape=jax.ShapeDtypeStruct((N_DEV, n), F32), compiler_params=_params())(c_all, w)


def _ada_bwd(c_all_t, dmod, name):
    n = dmod.shape[1]

    def body(c_ref, d_ref, o_ref):
        cv = c_ref[...]
        o_ref[...] = jnp.dot(cv * _sigmoid(cv), d_ref[...], preferred_element_type=F32, precision=lax.Precision.HIGHEST)

    return pl.pallas_call(body, name=name, out_shape=jax.ShapeDtypeStruct((D_MODEL, n), F32), compiler_params=_params())(c_all_t, dmod)


def _bucket_tables():
    rel = np.arange(BLK)[:, None] + BLK - np.arange(2 * BLK)[None, :]
    max_exact = N_BUCKETS // 2

    def bucket(n):
        nf = np.maximum(n, 1).astype(np.float32)
        large = max_exact + (np.log(nf / np.float32(max_exact)) / np.float32(math.log(MAX_REL_DIST / max_exact))
                             * np.float32(N_BUCKETS - max_exact)).astype(np.int32)
        return np.where(n < max_exact, n, np.minimum(large, N_BUCKETS - 1))

    tabs = []
    for dil, max_dist in ((1, 128), (4, 128), (16, 128), (1, SWA_WINDOW - 1)):
        in_band = (rel >= 0) & (rel <= max_dist)
        tabs.append(np.where(in_band, bucket(np.maximum(rel, 0) * dil), -1))
    return np.stack(tabs).astype(np.int32)


N_SOFT = H_DIL + H_SWA_Q


def _table_of_head(h):
    return jnp.minimum(h // 2, 3)


def _bias_build(rel_bias, tables, name):
    def body(rel_ref, t_ref, o_ref):
        h = pl.program_id(0)
        tb = t_ref[0]
        out = jnp.full((BLK, 2 * BLK), NEG, F32)
        for b in range(N_BUCKETS):
            out = jnp.where(tb == b, rel_ref[b, h], out)
        o_ref[0] = out

    return pl.pallas_call(
        body, name=name, out_shape=jax.ShapeDtypeStruct((N_SOFT, BLK, 2 * BLK), F32), grid=(N_SOFT,),
        in_specs=[pl.BlockSpec(memory_space=pltpu.SMEM),
                  pl.BlockSpec((1, BLK, 2 * BLK), lambda h: (_table_of_head(h), 0, 0))],
        out_specs=pl.BlockSpec((1, BLK, 2 * BLK), lambda h: (h, 0, 0)),
        compiler_params=_params(("parallel",)),
    )(rel_bias, tables)


def _bias_grad(dbias, tables, name):
    def body(d_ref, t_ref, o_ref):
        tb = t_ref[0]
        dv = d_ref[0]
        lane = lax.broadcasted_iota(jnp.int32, (1, LANES), 1)
        out = jnp.zeros((1, LANES), F32)
        for b in range(N_BUCKETS):
            out = jnp.where(lane == b, jnp.sum(jnp.where(tb == b, dv, 0.0)), out)
        o_ref[0] = out

    return pl.pallas_call(
        body, name=name, out_shape=jax.ShapeDtypeStruct((N_SOFT, 1, LANES), F32), grid=(N_SOFT,),
        in_specs=[pl.BlockSpec((1, BLK, 2 * BLK), lambda h: (h, 0, 0)),
                  pl.BlockSpec((1, BLK, 2 * BLK), lambda h: (_table_of_head(h), 0, 0))],
        out_specs=pl.BlockSpec((1, 1, LANES), lambda h: (h, 0, 0)),
        compiler_params=_params(("parallel",)),
    )(dbias, tables)


def _band_layout(g, bias_div):
    assert g == 1 or bias_div == 1
    return bias_div if g == 1 else 1


def _band_specs(length, g, bias_div, offs):
    ns = _band_layout(g, bias_div)

    def seqs(off, div=1):
        return pl.BlockSpec((ns, length, HEAD_DIM), lambda s: (off // ns + s // div, 0, 0))

    xspecs = [seqs(offs[0]), seqs(offs[1], g), seqs(offs[2], g)]
    bspec = pl.BlockSpec((1, BLK, 2 * BLK), lambda s: (s, 0, 0))
    sspec = pl.BlockSpec((ns, 1, LANES), lambda s: (s, 0, 0))
    colspec = pl.BlockSpec((ns, length, 1), lambda s: (s, 0, 0))
    return xspecs, seqs(0), seqs(0, g), bspec, sspec, colspec


def _band_sweep(length, ns, one):
    nblk = length // BLK
    for qq in range(ns):
        if ns * nblk <= 16:
            for i in range(nblk):
                one(qq, i * BLK, max(i - 1, 0) * BLK, i == 0)
        else:
            def step(i, carry, qq=qq):
                one(qq, pl.multiple_of(i * BLK, BLK), pl.multiple_of(jnp.maximum(i - 1, 0) * BLK, BLK), i == 0)
                return carry

            lax.fori_loop(0, nblk, step, 0, unroll=2)


def _band_scores(q_ref, k_ref, b_ref, qq, kq, bq, cur, prv, first):
    qv = q_ref[qq, pl.ds(cur, BLK), :]
    bv = b_ref[bq]
    if first is True:
        sp = jnp.full((BLK, BLK), NEG, F32)
    else:
        sp = _dot(qv, k_ref[kq, pl.ds(prv, BLK), :], 1, 1) + bv[:, :BLK]
        sp = sp if first is False else jnp.where(first, NEG, sp)
    sc = _dot(qv, k_ref[kq, pl.ds(cur, BLK), :], 1, 1) + bv[:, BLK:]
    return qv, sp, sc


def _band_fwd(x, bias, sink, *, nq, offs, g, bias_div, has_sink, name):
    length = x.shape[1]
    ns = _band_layout(g, bias_div)

    def body(q_ref, k_ref, v_ref, b_ref, s_ref, o_ref, lse_ref):
        def one(qq, cur, prv, first):
            kq, bq = qq, 0
            _, sp, sc = _band_scores(q_ref, k_ref, b_ref, qq, kq, bq, cur, prv, first)
            m = jnp.maximum(jnp.max(sp, axis=1, keepdims=True), jnp.max(sc, axis=1, keepdims=True))
            if has_sink:
                sk = s_ref[qq][:, :1]
                m = jnp.maximum(m, sk)
            pp, pc = jnp.exp(sp - m), jnp.exp(sc - m)
            den = jnp.sum(pp, axis=1, keepdims=True) + jnp.sum(pc, axis=1, keepdims=True)
            if has_sink:
                den = den + jnp.exp(sk - m)
            acc = (_dot(pp.astype(BF16), v_ref[kq, pl.ds(prv, BLK), :], 1, 0)
                   + _dot(pc.astype(BF16), v_ref[kq, pl.ds(cur, BLK), :], 1, 0))
            o_ref[qq, pl.ds(cur, BLK), :] = acc / den
            lse_ref[qq, pl.ds(cur, BLK), :] = m + jnp.log(den)

        _band_sweep(length, ns, one)

    xspecs, qspec, _, bspec, sspec, colspec = _band_specs(length, g, bias_div, offs)
    return pl.pallas_call(
        body, name=name,
        out_shape=[jax.ShapeDtypeStruct((nq, length, HEAD_DIM), F32), jax.ShapeDtypeStruct((nq, length, 1), F32)],
        grid=(nq // ns,), in_specs=xspecs + [bspec, sspec],
        out_specs=[qspec, colspec], compiler_params=_params(("parallel",)),
    )(x, x, x, bias, sink)


def _band_bwd(x, bias, sink, o, lse, do, dlse, *, nq, offs, g, bias_div, has_sink, name):
    length = x.shape[1]
    ns = _band_layout(g, bias_div)
    nk, nbias = nq // g, nq // bias_div

    def body(q_ref, k_ref, v_ref, b_ref, s_ref, o_ref, lse_ref, do_ref, dlse_ref,
             dq_ref, dk_ref, dv_ref, db_ref, dsk_ref, dkp_ref, dvp_ref):
        for ref in (db_ref, dsk_ref, dkp_ref, dvp_ref):
            ref[...] = jnp.zeros_like(ref)

        @pl.when(pl.program_id(0) % g == 0)
        def _():
            dk_ref[...] = jnp.zeros_like(dk_ref)
            dv_ref[...] = jnp.zeros_like(dv_ref)

        def one(qq, cur, prv, first):
            kq, bq = qq, 0
            qv, sp, sc = _band_scores(q_ref, k_ref, b_ref, qq, kq, bq, cur, prv, first)
            rows, prow = pl.ds(cur, BLK), pl.ds(prv, BLK)
            lse_v = lse_ref[qq, rows, :]
            pp, pc = jnp.exp(sp - lse_v), jnp.exp(sc - lse_v)
            dov = do_ref[qq, rows, :]
            dob = dov.astype(BF16)
            coef = dlse_ref[qq, rows, :] - jnp.sum(dov * o_ref[qq, rows, :], axis=1, keepdims=True)
            dsp = pp * (_dot(dob, v_ref[kq, prow, :], 1, 1) + coef)
            dsc = pc * (_dot(dob, v_ref[kq, rows, :], 1, 1) + coef)
            dspb, dscb = dsp.astype(BF16), dsc.astype(BF16)
            dq_ref[qq, rows, :] = ((_dot(dspb, k_ref[kq, prow, :], 1, 0) + _dot(dscb, k_ref[kq, rows, :], 1, 0))
                                   * (HEAD_DIM ** -0.5))
            dk_ref[kq, rows, :] += _dot(dscb, qv, 0, 0)
            dkp_ref[kq, prow, :] += _dot(dspb, qv, 0, 0)
            dv_ref[kq, rows, :] += _dot(pc.astype(BF16), dob, 0, 0)
            dvp_ref[kq, prow, :] += _dot(pp.astype(BF16), dob, 0, 0)
            db_ref[bq, :, :BLK] += dsp
            db_ref[bq, :, BLK:] += dsc
            if has_sink:
                dsk_ref[qq] += jnp.sum(jnp.exp(s_ref[qq][:, :1] - lse_v) * coef)

        _band_sweep(length, ns, one)
        dk_ref[...] += dkp_ref[...]
        dv_ref[...] += dvp_ref[...]

    xspecs, qspec, kvspec, bspec, sspec, colspec = _band_specs(length, g, bias_div, offs)
    return pl.pallas_call(
        body, name=name,
        out_shape=[jax.ShapeDtypeStruct((nq, length, HEAD_DIM), F32), jax.ShapeDtypeStruct((nk, length, HEAD_DIM), F32),
                   jax.ShapeDtypeStruct((nk, length, HEAD_DIM), F32), jax.ShapeDtypeStruct((nbias, BLK, 2 * BLK), F32),
                   jax.ShapeDtypeStruct((nq, 1, LANES), F32)],
        grid=(nq // ns,),
        in_specs=xspecs + [bspec, sspec, qspec, colspec, qspec, colspec],
        out_specs=[qspec, kvspec, kvspec, bspec, sspec],
        scratch_shapes=[pltpu.VMEM((ns, length, HEAD_DIM), F32), pltpu.VMEM((ns, length, HEAD_DIM), F32)],
        compiler_params=_params(("arbitrary",)),
    )(x, x, x, bias, sink, o, lse, do, dlse)


TOK_TILE = 512


def _dil_merge(outs, lses, dout, name):
    tr = TOK_TILE
    dils = [d for _, d in DIL_PATTERNS]
    n = len(dils)
    o4 = [o.reshape(2, d, SEQ // d, HEAD_DIM) for o, d in zip(outs, dils)]
    l4 = [l.reshape(2, d, SEQ // d, 1) for l, d in zip(lses, dils)]
    o_specs = [pl.BlockSpec((2, d, tr // d, HEAD_DIM), lambda i: (0, 0, i, 0)) for d in dils]
    l_specs = [pl.BlockSpec((2, d, tr // d, 1), lambda i: (0, 0, i, 0)) for d in dils]
    tok = pl.BlockSpec((tr, 2 * HEAD_DIM), lambda i: (i, 0))
    scratch = ([pltpu.VMEM((tr, 2 * HEAD_DIM), F32) for _ in dils] + [pltpu.VMEM((tr, 1), F32) for _ in range(2 * n)]
               + [pltpu.VMEM((tr // d, 2 * HEAD_DIM), F32) for d in dils])

    def to_tokens(o_ref, l_ref, d, pair, cols, stage):
        for r in range(d):
            rows = pl.ds(r, tr // d, stride=d) if d > 1 else slice(None)
            stage[:, :HEAD_DIM] = o_ref[0, r]
            stage[:, HEAD_DIM:] = o_ref[1, r]
            pair[rows, :] = stage[...]
            for h in range(2):
                cols[h][rows, :] = l_ref[h, r]
        return pair[...], [cols[0][...], cols[1][...]]

    def weights(ls):
        left = lax.broadcasted_iota(jnp.int32, (tr, 2 * HEAD_DIM), 1) < HEAD_DIM
        per_head = []
        for h in range(2):
            m = ls[0][h]
            for g in range(1, n):
                m = jnp.maximum(m, ls[g][h])
            es = [jnp.exp(ls[g][h] - m) for g in range(n)]
            den = es[0]
            for e in es[1:]:
                den = den + e
            per_head.append([e / den for e in es])
        return per_head, [jnp.where(left, per_head[0][g], per_head[1][g]) for g in range(n)], left

    def load(refs):
        pairs, cols, stages = refs[:n], refs[n:3 * n], refs[3 * n:]
        return pairs, [cols[2 * g:2 * g + 2] for g in range(n)], stages

    if dout is None:
        def body(*refs):
            pairs, cols, stages = load(refs[2 * n + 1:])
            toks = [to_tokens(refs[g], refs[n + g], dils[g], pairs[g], cols[g], stages[g]) for g in range(n)]
            _, alphas, _ = weights([t[1] for t in toks])
            acc = alphas[0] * toks[0][0]
            for g in range(1, n):
                acc = acc + alphas[g] * toks[g][0]
            refs[2 * n][...] = acc

        return pl.pallas_call(
            body, name=name, out_shape=jax.ShapeDtypeStruct((SEQ, 2 * HEAD_DIM), F32), grid=(SEQ // tr,),
            in_specs=o_specs + l_specs, out_specs=tok, scratch_shapes=scratch, compiler_params=_params(("parallel",)),
        )(*o4, *l4)

    def body(*refs):
        do_refs, dl_refs = refs[2 * n + 1:3 * n + 1], refs[3 * n + 1:4 * n + 1]
        pairs, cols, stages = load(refs[4 * n + 1:])
        toks = [to_tokens(refs[g], refs[n + g], dils[g], pairs[g], cols[g], stages[g]) for g in range(n)]
        per_head, alphas, left = weights([t[1] for t in toks])
        dov = refs[2 * n][...]
        das = []
        for g in range(n):
            prod = dov * toks[g][0]
            das.append([jnp.sum(jnp.where(left, prod, 0.0), axis=1, keepdims=True),
                        jnp.sum(jnp.where(left, 0.0, prod), axis=1, keepdims=True)])
        dbar = [sum(per_head[h][g] * das[g][h] for g in range(n)) for h in range(2)]
        for g, d in enumerate(dils):
            pairs[g][...] = alphas[g] * dov
            for h in range(2):
                cols[g][h][...] = per_head[h][g] * (das[g][h] - dbar[h])
            for r in range(d):
                rows = pl.ds(r, tr // d, stride=d) if d > 1 else slice(None)
                v = pairs[g][rows, :]
                for h in range(2):
                    do_refs[g][h, r] = v[:, h * HEAD_DIM:(h + 1) * HEAD_DIM]
                    dl_refs[g][h, r] = cols[g][h][rows, :]

    out = pl.pallas_call(
        body, name=name,
        out_shape=[jax.ShapeDtypeStruct(o.shape, F32) for o in o4] + [jax.ShapeDtypeStruct(l.shape, F32) for l in l4],
        grid=(SEQ // tr,), in_specs=o_specs + l_specs + [tok], out_specs=o_specs + l_specs, scratch_shapes=scratch,
        compiler_params=_params(("parallel",)),
    )(*o4, *l4, dout)
    return [t.reshape(s.shape) for t, s in zip(out, list(outs) + list(lses))]


def _tri(cmp):
    r = lax.broadcasted_iota(jnp.int32, (SB_TILE, SB_TILE), 0)
    c = lax.broadcasted_iota(jnp.int32, (SB_TILE, SB_TILE), 1)
    return cmp(r, c).astype(BF16)


def _cum(x, tri, terms):
    acc, rest = None, x
    for _ in range(terms):
        part = rest.astype(BF16)
        rest = rest - part.astype(F32)
        d = _dot(part, tri, 1, 0)
        acc = d if acc is None else acc + d
    return acc


def _sb_logits(q, ks, diagonal):
    t = SB_TILE
    z = _dot(q, ks, 1, 1)
    e = jnp.exp(-jnp.abs(z))
    lf = -(jnp.maximum(z, 0.0) + jnp.log(1.0 + e))
    if not diagonal:
        return z, e, lf, None
    mask = lax.broadcasted_iota(jnp.int32, (t, t), 1) < lax.broadcasted_iota(jnp.int32, (t, t), 0)
    return z, e, jnp.where(mask, lf, 0.0), mask


def _sb_specs(h, s):
    t = SB_TILE
    tile = pl.BlockSpec((h, t, HEAD_DIM), lambda i: (0, i, 0))
    keys = pl.BlockSpec((h, s, HEAD_DIM), lambda i: (1, 0, 0))
    values = pl.BlockSpec((h, s, HEAD_DIM), lambda i: (2, 0, 0))
    return tile, keys, values, pl.BlockSpec((h, t, 1), lambda i: (0, i, 0))


def _sb_fwd(x, name):
    h, s = x.shape[0] // 3, x.shape[1]
    t = SB_TILE

    def body(q_ref, k_ref, v_ref, o_ref, tot_ref):
        i = pl.program_id(0)
        after = _tri(lambda r, c: r > c)

        def tile(j, carry, diagonal):
            rows = pl.ds(pl.multiple_of(j * t, t), t)
            out = []
            for hh, (right, acc) in enumerate(carry):
                z, _, lf, mask = _sb_logits(q_ref[hh], k_ref[hh, rows, :], diagonal)
                w = jnp.exp(z + lf + (right + _cum(lf, after, 2)))
                w = w if mask is None else jnp.where(mask, w, 0.0)
                out.append((right + jnp.sum(lf, axis=1, keepdims=True), acc + _dot(w.astype(BF16), v_ref[hh, rows, :], 1, 0)))
            return tuple(out)

        carry = tile(i, tuple((jnp.zeros((t, 1), F32), jnp.zeros((t, HEAD_DIM), F32)) for _ in range(h)), True)
        carry = lax.fori_loop(0, i, lambda jj, c: tile(i - 1 - jj, c, False), carry)
        for hh, (right, acc) in enumerate(carry):
            o_ref[hh] = acc
            tot_ref[hh] = right

    tile_spec, keys, values, col = _sb_specs(h, s)
    return pl.pallas_call(
        body, name=name, out_shape=[jax.ShapeDtypeStruct((h, s, HEAD_DIM), F32), jax.ShapeDtypeStruct((h, s, 1), F32)],
        grid=(s // t,), in_specs=[tile_spec, keys, values], out_specs=[tile_spec, col],
        compiler_params=_params(("parallel",)),
    )(x, x, x)


def _sb_bwd(x, tot, do, name):
    h, s = x.shape[0] // 3, x.shape[1]
    t = SB_TILE

    def body(q_ref, k_ref, v_ref, tot_ref, do_ref, dq_ref, dk_ref, dv_ref):
        i = pl.program_id(0)

        @pl.when(i == 0)
        def _():
            dk_ref[...] = jnp.zeros_like(dk_ref)
            dv_ref[...] = jnp.zeros_like(dv_ref)

        upto = _tri(lambda r, c: r <= c)
        before = _tri(lambda r, c: r < c)

        def tile(j, carry, diagonal):
            rows = pl.ds(pl.multiple_of(j * t, t), t)
            out = []
            for hh, (left, cleft, dq) in enumerate(carry):
                qv, ks, dob = q_ref[hh], k_ref[hh, rows, :], do_ref[hh].astype(BF16)
                z, e, lf, mask = _sb_logits(qv, ks, diagonal)
                between = tot_ref[hh] - (left + _cum(lf, upto, 2))
                w = jnp.exp(z + lf + between)
                w = w if mask is None else jnp.where(mask, w, 0.0)
                dlog = w * _dot(dob, v_ref[hh, rows, :], 1, 1)
                cfail = cleft + _cum(dlog, before, 2)
                sig = jnp.where(z >= 0.0, 1.0, e) / (1.0 + e)
                dz = dlog * (1.0 - sig) - sig * cfail
                dz = (dz if mask is None else jnp.where(mask, dz, 0.0)).astype(BF16)
                dk_ref[hh, rows, :] += _dot(dz, qv, 0, 0)
                dv_ref[hh, rows, :] += _dot(w.astype(BF16), dob, 0, 0)
                out.append((left + jnp.sum(lf, axis=1, keepdims=True), cleft + jnp.sum(dlog, axis=1, keepdims=True),
                            dq + _dot(dz, ks, 1, 0)))
            return tuple(out)

        zero = jnp.zeros((t, 1), F32)
        carry = lax.fori_loop(0, i, lambda j, c: tile(j, c, False),
                              tuple((zero, zero, jnp.zeros((t, HEAD_DIM), F32)) for _ in range(h)))
        for hh, (_, _, dq) in enumerate(tile(i, carry, True)):
            dq_ref[hh] = dq * (HEAD_DIM ** -0.5)

    tile_spec, keys, values, col = _sb_specs(h, s)
    full = pl.BlockSpec((h, s, HEAD_DIM), lambda i: (0, 0, 0))
    shp = jax.ShapeDtypeStruct((h, s, HEAD_DIM), F32)
    return pl.pallas_call(
        body, name=name, out_shape=[shp, shp, shp], grid=(s // t,),
        in_specs=[tile_spec, keys, values, col, tile_spec],
        out_specs=[tile_spec, full, full], compiler_params=_params(("arbitrary",)),
    )(x, x, x, tot, do)


COL_SB, COL_DIL, COL_SWA = 0, 3 * H_SB * HEAD_DIM, 3 * H_SB * HEAD_DIM + 3 * H_DIL * HEAD_DIM
N_SWA = H_SWA_Q + 2 * H_SWA_KV


def _dil_col(t, g):
    return COL_DIL + t * H_DIL * HEAD_DIM + g * 2 * HEAD_DIM


def _split_heads(qkv, name):
    tr = TOK_TILE
    scale = HEAD_DIM ** -0.5
    dils = [d for _, d in DIL_PATTERNS]

    def body(x_ref, sb_ref, d0_ref, d1_ref, d2_ref, swa_ref, pair):
        def head(col, scaled):
            v = x_ref[:, col:col + HEAD_DIM]
            return (v * scale if scaled else v).astype(BF16)

        for hh in range(3 * H_SB):
            sb_ref[hh] = head(COL_SB + hh * HEAD_DIM, hh < H_SB)
        for hh in range(N_SWA):
            swa_ref[hh] = head(COL_SWA + hh * HEAD_DIM, hh < H_SWA_Q)
        for t in range(3):
            for g, (d, out_ref) in enumerate(zip(dils, (d0_ref, d1_ref, d2_ref))):
                col = _dil_col(t, g)
                if d == 1:
                    for h in range(2):
                        out_ref[t * 2 + h] = head(col + h * HEAD_DIM, t == 0)
                    continue
                pair[...] = x_ref[:, col:col + 2 * HEAD_DIM]
                for r in range(d):
                    v = pair[pl.ds(r, tr // d, stride=d), :]
                    v = v * scale if t == 0 else v
                    for h in range(2):
                        out_ref[t * 2 * d + h * d + r] = v[:, h * HEAD_DIM:(h + 1) * HEAD_DIM].astype(BF16)

    def heads(n, length):
        return jax.ShapeDtypeStruct((n, length, HEAD_DIM), BF16)

    def spec(n, rows):
        return pl.BlockSpec((n, rows, HEAD_DIM), lambda i: (0, i, 0))

    return pl.pallas_call(
        body, name=name,
        out_shape=[heads(3 * H_SB, SEQ)] + [heads(6 * d, SEQ // d) for d in dils] + [heads(N_SWA, SEQ)],
        grid=(SEQ // tr,), in_specs=[pl.BlockSpec((tr, D_QKV), lambda i: (i, 0))],
        out_specs=[spec(3 * H_SB, tr)] + [spec(6 * d, tr // d) for d in dils] + [spec(N_SWA, tr)],
        scratch_shapes=[pltpu.VMEM((tr, 2 * HEAD_DIM), F32)], compiler_params=_params(("parallel",)),
    )(qkv)


def _join_heads(sb, dil, swa, name):
    tr = TOK_TILE
    dils = [d for _, d in DIL_PATTERNS]

    def body(*refs):
        sb_refs, dil_refs, swa_refs = refs[:3], [refs[3 + 3 * g:6 + 3 * g] for g in range(3)], refs[12:15]
        o_ref, pair, stages = refs[15], refs[16], refs[17:]

        def put(col, v):
            o_ref[:, col:col + v.shape[1]] = v.astype(BF16)

        for t in range(3):
            for h in range(H_SB):
                put(COL_SB + (t * H_SB + h) * HEAD_DIM, sb_refs[t][h])
        col = COL_SWA
        for ref in swa_refs:
            for h in range(ref.shape[0]):
                put(col, ref[h])
                col += HEAD_DIM
        for t in range(3):
            for g, d in enumerate(dils):
                ref, col = dil_refs[g][t], _dil_col(t, g)
                if d == 1:
                    for h in range(2):
                        put(col + h * HEAD_DIM, ref[h])
                    continue
                stage = stages[g - 1]
                for r in range(d):
                    stage[:, :HEAD_DIM] = ref[r]
                    stage[:, HEAD_DIM:] = ref[d + r]
                    pair[pl.ds(r, tr // d, stride=d), :] = stage[...]
                put(col, pair[...])

    def spec(n, rows):
        return pl.BlockSpec((n, rows, HEAD_DIM), lambda i: (0, i, 0))

    ins = list(sb) + [t for g in range(3) for t in dil[g]] + list(swa)
    in_specs = ([spec(H_SB, tr)] * 3 + [spec(2 * d, tr // d) for d in dils for _ in range(3)]
                + [spec(H_SWA_Q, tr), spec(H_SWA_KV, tr), spec(H_SWA_KV, tr)])
    return pl.pallas_call(
        body, name=name, out_shape=jax.ShapeDtypeStruct((SEQ, D_QKV), BF16), grid=(SEQ // tr,), in_specs=in_specs,
        out_specs=pl.BlockSpec((tr, D_QKV), lambda i: (i, 0)),
        scratch_shapes=[pltpu.VMEM((tr, 2 * HEAD_DIM), F32)] + [pltpu.VMEM((tr // d, 2 * HEAD_DIM), F32) for d in dils[1:]],
        compiler_params=_params(("parallel",)),
    )(*ins)


def _mixer_fwd(qkv, bias, sinks_l, tag):
    sb, d0, d1, d2, swa = _split_heads(qkv, name=f"split_heads_{tag}")
    st = {"sb": sb, "dil": (d0, d1, d2), "swa": swa}
    o_sb, st["sb_tot"] = _sb_fwd(sb, name=f"sb_fwd_{tag}")
    st["dil_out"], st["dil_lse"], st["dil_sink"] = [], [], []
    for gi, (_, d) in enumerate(DIL_PATTERNS):
        sink = jnp.zeros((2 * d, 1, LANES), F32)
        og, lg = _band_fwd(st["dil"][gi], bias[2 * gi:2 * gi + 2], sink, nq=2 * d, offs=(0, 2 * d, 4 * d), g=1, bias_div=d,
                           has_sink=False, name=f"dil{gi}_fwd_{tag}")
        st["dil_out"].append(og)
        st["dil_lse"].append(lg)
        st["dil_sink"].append(sink)
    o_dil = _dil_merge(st["dil_out"], st["dil_lse"], None, name=f"dil_merge_fwd_{tag}")
    st["swa_sink"] = jnp.broadcast_to(sinks_l.reshape(H_SWA_Q, 1, 1), (H_SWA_Q, 1, LANES))
    st["swa_out"] = _band_fwd(swa, bias[H_DIL:], st["swa_sink"], nq=H_SWA_Q, offs=(0, H_SWA_Q, H_SWA_Q + H_SWA_KV),
                              g=H_SWA_Q // H_SWA_KV, bias_div=1, has_sink=True, name=f"swa_fwd_{tag}")
    return (o_sb, o_dil, st["swa_out"][0]), st


def _mixer_bwd(st, bias, do_sb, do_dil, do_swa, tag):
    d_sb = _sb_bwd(st["sb"], st["sb_tot"], do_sb, name=f"sb_bwd_{tag}")
    dmerge = _dil_merge(st["dil_out"], st["dil_lse"], do_dil, name=f"dil_merge_bwd_{tag}")
    d_dil, dbs = [], []
    for gi, (_, d) in enumerate(DIL_PATTERNS):
        dq, dk, dv, db, _ = _band_bwd(st["dil"][gi], bias[2 * gi:2 * gi + 2], st["dil_sink"][gi], st["dil_out"][gi],
                                      st["dil_lse"][gi], dmerge[gi], dmerge[3 + gi], nq=2 * d, offs=(0, 2 * d, 4 * d),
                                      g=1, bias_div=d, has_sink=False, name=f"dil{gi}_bwd_{tag}")
        d_dil.append((dq, dk, dv))
        dbs.append(db)
    o_sw, l_sw = st["swa_out"]
    dq_sw, dk_sw, dv_sw, db_sw, dsink = _band_bwd(st["swa"], bias[H_DIL:], st["swa_sink"], o_sw, l_sw, do_swa,
                                                  jnp.zeros_like(l_sw), nq=H_SWA_Q, offs=(0, H_SWA_Q, H_SWA_Q + H_SWA_KV),
                                                  g=H_SWA_Q // H_SWA_KV, bias_div=1, has_sink=True, name=f"swa_bwd_{tag}")
    dqkv = _join_heads(d_sb, d_dil, (dq_sw, dk_sw, dv_sw), name=f"join_heads_{tag}")
    return dqkv, jnp.concatenate(dbs + [db_sw], 0), dsink[:, 0, 0]


PIECES = ("ffn0", "mix", "ffn1")


def _ffn_fwd(x_in, w, gain, mod_j, tag, after=None):
    st = {"x": x_in, "w": w}
    st["h"] = _norm_fwd(x_in, _row(gain), _row(mod_j[1]), _row(mod_j[0]), name=f"norm_fwd_{tag}", after=after)
    st["a"], st["u"], st["s"] = _ffn_up(st["h"], w["gate"], w["up"], name=f"up_{tag}")
    st["f"], x_out = _mm(st["s"], w["down"], res=x_in, colscale=_row(0.5 * mod_j[2]), emit_acc=True, tm=512, tn=1024,
                         name=f"down_{tag}")
    return x_out, st


def _ffn_bwd(dx_out, st, gain, mod_j, tag, done):
    w = st["w"]

    def latest(new, old):
        return old if new is None else new

    df, dgate = _gate_bwd(dx_out, st["f"], _row(0.5 * mod_j[2]), 0.5, name=f"gate_bwd_{tag}")
    token = done({"down": _mm_tn(st["s"], df, tm=D_FF // 2, name=f"dwd_{tag}")})
    da, du = _ffn_bwd_ds(df, w["down"], st["a"], st["u"], name=f"ds_{tag}")
    token = latest(done({"gate": _mm_tn(da, st["h"], after=token, tm=D_FF // 2, name=f"dwg_{tag}")}), token)
    token = latest(done({"up": _mm_tn(du, st["h"], after=token, tm=D_FF // 2, name=f"dwu_{tag}")}), token)
    dh = _mm2(da, w["gate"], du, w["up"], after=token, name=f"dh_{tag}")
    dx_in, sum_dh, sum_dhx = _norm_bwd(st["x"], dh, dx_out, _row(gain), _row(mod_j[1]), name=f"norm_bwd_{tag}")
    dmod = jnp.concatenate([sum_dh, gain * sum_dhx, dgate], 0)
    return dx_in, dmod, (1.0 + mod_j[1]) * sum_dhx[0]


def _mix_fwd(x_in, w, gain, mod_j, bias, sinks_l, tag, after=None):
    st = {"x": x_in, "w": w}
    st["h"] = _norm_fwd(x_in, _row(gain), _row(mod_j[1]), _row(mod_j[0]), name=f"norm_fwd_mix_{tag}", after=after)
    qkv = _mm(st["h"], w["in"], tb=True, tm=SEQ, b_rows=(0, D_QKV), name=f"qkv_{tag}")
    st["gates"] = _mm(st["h"], w["in"], tb=True, tm=SEQ, b_rows=(D_QKV, D_GATES), name=f"gates_{tag}")
    outs, st["mix"] = _mixer_fwd(qkv, bias, sinks_l, tag)
    st["merged"], *st["t"] = _merge_fwd(*outs, st["gates"], w["br_sb"], w["br_dil"], w["br_swa"], name=f"merge_fwd_{tag}")
    st["f"], x_out = _mm(st["merged"], w["out"], res=x_in, colscale=_row(mod_j[2]), emit_acc=True, name=f"out_{tag}")
    return x_out, st


def _mix_bwd(dx_out, st, gain, mod_j, bias, tag, done):
    w = st["w"]
    df, dgate = _gate_bwd(dx_out, st["f"], _row(mod_j[2]), 1.0, name=f"gate_bwd_mix_{tag}")
    g = {"out": _mm_tn(st["merged"], df, name=f"dw_out_{tag}")}
    dmerged = _mm(df, w["out"], tb=True, name=f"dmerged_{tag}")
    dgates, do_sb, do_dil, do_swa, dbr_sb, dbr_dil, dbr_swa = _merge_bwd(
        dmerged, *st["t"], st["gates"], w["br_sb"], w["br_dil"], w["br_swa"], name=f"merge_bwd_{tag}")
    g["br_sb"] = _mm_tn(st["t"][0], dbr_sb, name=f"dw_br_sb_{tag}")
    g["br_dil"] = _mm_tn(st["t"][1], dbr_dil, name=f"dw_br_dil_{tag}")
    g["br_swa"] = _mm_tn(st["t"][2], dbr_swa, name=f"dw_br_swa_{tag}")
    dqkv, dbias, dsinks = _mixer_bwd(st["mix"], bias, do_sb, do_dil, do_swa, tag)
    dw_qkv = _mm_tn(dqkv, st["h"], out_rows=D_QKV + D_GATES, name=f"dw_qkv_{tag}")
    g["in"] = _mm_tn(dgates, st["h"], out_rows=D_QKV + D_GATES, row0=D_QKV, prev=dw_qkv, name=f"dw_gates_{tag}")
    dh = _mm2(dqkv, w["in"], dgates, w["in"], after=done(g), tm=512, b_rows=(0, D_QKV), name=f"dh_mix_{tag}")
    dx_in, sum_dh, sum_dhx = _norm_bwd(st["x"], dh, dx_out, _row(gain), _row(mod_j[1]), name=f"norm_bwd_mix_{tag}")
    dmod = jnp.concatenate([sum_dh, gain * sum_dhx, dgate], 0)
    return dx_in, dmod, (1.0 + mod_j[1]) * sum_dhx[0], dbias, dsinks


def _local_step(x, target, mod, gains, weights_of, rel_bias, sinks, final_gain, grads_done):
    tables = jnp.asarray(_bucket_tables())
    bias = _bias_build(rel_bias, tables, name="bias_build")
    states, h = [], x
    for l in range(DEPTH):
        st = {}
        for j, piece in enumerate(PIECES):
            w, after = weights_of(l, piece, h)
            if piece == "mix":
                h, st[piece] = _mix_fwd(h, w, gains[l, j], mod[l, j], bias, sinks[l], f"l{l}", after)
            else:
                h, st[piece] = _ffn_fwd(h, w, gains[l, j], mod[l, j], f"{piece}_l{l}", after)
        states.append(st)
    loss, dx, dfinal = _final_loss(h, target, _row(final_gain), name="final_loss")
    dmods = [[None] * 3 for _ in range(DEPTH)]
    dgains = [[None] * 3 for _ in range(DEPTH)]
    dsinks = [None] * DEPTH
    dbias = None
    for l in reversed(range(DEPTH)):
        for j in reversed(range(3)):
            piece = PIECES[j]
            done = lambda grads, l=l, piece=piece: grads_done(l, piece, grads)
            if piece == "mix":
                dx, dmods[l][j], dgains[l][j], db, dsinks[l] = _mix_bwd(dx, states[l][piece], gains[l, j], mod[l, j], bias, f"l{l}", done)
                dbias = db if dbias is None else dbias + db
            else:
                dx, dmods[l][j], dgains[l][j] = _ffn_bwd(dx, states[l][piece], gains[l, j], mod[l, j], f"{piece}_l{l}", done)
    drel = _bias_grad(dbias, tables, name="bias_grad")[:, 0, :N_BUCKETS].T
    dmod = jnp.stack([jnp.stack(m) for m in dmods])
    dgain = jnp.stack([jnp.stack(g) for g in dgains])
    return loss, dx, dmod, dgain, dfinal[0], drel, jnp.stack(dsinks)


BR_ROWS = (H_SB * HEAD_DIM, 2 * HEAD_DIM, H_SWA_Q * HEAD_DIM)


def _lanes_unshard(g, lead):
    _, rows, _ = g.shape
    r = rows // lead
    return g.reshape(N_DEV, lead, r, LANES).transpose(1, 2, 0, 3).reshape(lead, r, N_DEV * LANES)


def _lanes_shard(full):
    lead, r, _ = full.shape
    return full.reshape(lead, r, N_DEV, LANES).transpose(2, 0, 1, 3).reshape(N_DEV, lead * r, LANES)


def _pack_rows(parts, dtype):
    flat = jnp.concatenate([p.astype(dtype).reshape(-1) for p in parts])
    pad = (-flat.shape[0]) % (16 * LANES)
    if pad:
        flat = jnp.concatenate([flat, jnp.zeros((pad,), dtype)])
    return flat.reshape(-1, LANES)


def _unshard(gathered, axis):
    moved = jnp.moveaxis(gathered, 0, axis)
    shape = list(moved.shape)
    shape[axis:axis + 2] = [shape[axis] * shape[axis + 1]]
    return moved.reshape(shape)


def kernel(x, c, w_ada, b_ada, norm_gain, w_ffn_gate, w_ffn_up, w_ffn_down, w_in, w_br_sb, w_br_dil, w_br_swa, w_out, sinks, rel_bias, final_gain, loss_target, m_w_ada, m_b_ada, m_norm_gain, m_w_ffn_gate, m_w_ffn_up, m_w_ffn_down, m_w_in, m_w_br_sb, m_w_br_dil, m_w_br_swa, m_w_out, m_sinks, m_rel_bias, m_final_gain, v_w_ada, v_b_ada, v_norm_gain, v_w_ffn_gate, v_w_ffn_up, v_w_ffn_down, v_w_in, v_w_br_sb, v_w_br_dil, v_w_br_swa, v_w_out, v_sinks, v_rel_bias, v_final_gain):
    me = 4 * lax.axis_index("x") + 2 * lax.axis_index("y") + lax.axis_index("c")
    d = D_MODEL
    gate_t, up_t, in_t = jnp.swapaxes(w_ffn_gate, 2, 3), jnp.swapaxes(w_ffn_up, 2, 3), jnp.swapaxes(w_in, 1, 2)

    def piece_shards(l, piece):
        bf = lambda t: t.astype(BF16)
        if piece == "mix":
            return [bf(in_t[l]), jnp.concatenate([bf(w_br_sb[l]), bf(w_br_dil[l]), bf(w_br_swa[l])], 0), bf(w_out[l])]
        i = PIECES.index(piece) // 2
        return [bf(gate_t[l, i]), bf(up_t[l, i]), bf(w_ffn_down[l, i])]

    br_off = np.concatenate([[0], np.cumsum(BR_ROWS)])

    def piece_weights(gathered, piece):
        if piece == "mix":
            g_in, g_br, g_out = gathered
            f_br = [_lanes_unshard(g_br[:, br_off[k]:br_off[k + 1]], 1)[0] for k in range(3)]
            return {"in": g_in.reshape(D_QKV + D_GATES, d), "br_sb": f_br[0], "br_dil": f_br[1], "br_swa": f_br[2],
                    "out": g_out.reshape(d, d)}
        return {n: g.reshape(D_FF, d) for n, g in zip(("gate", "up", "down"), gathered)}

    small, = _all_gather([_pack_rows([c, norm_gain], F32)], name="gather_cond")
    c_all = small[:, :d // LANES].reshape(N_DEV, d)
    gains = _unshard(small[:, d // LANES:d // LANES + 6].reshape(N_DEV, DEPTH, 3, LANES), 2)

    cols = w_ada.shape[2]
    mod_cols = jnp.stack([_ada_fwd(c_all, w_ada[l], name=f"ada_fwd_l{l}") for l in range(DEPTH)])
    mod_all, = _all_gather([_pack_rows([mod_cols], F32)], name="gather_mod")
    mod_all = mod_all.reshape(N_DEV, -1)[:, :DEPTH * N_DEV * cols].reshape(N_DEV, DEPTH, N_DEV, cols)
    mod_mine = lax.dynamic_index_in_dim(mod_all, me, axis=2, keepdims=False)
    mod = (mod_mine.transpose(1, 0, 2).reshape(DEPTH, N_DEV * cols) + b_ada).reshape(DEPTH, 3, 3, d)

    order = [(l, piece) for l in range(DEPTH) for piece in PIECES]
    eager, ahead = 2, 3
    in_flight = {}
    n_tensors = 3
    first = _all_gather([s for k in range(eager) for s in piece_shards(*order[k])], after=mod_all, name="gather_first")

    def start_gather(k, after):
        l, piece = order[k]
        in_flight[k], token = _exchange_start(piece_shards(l, piece), after, gather=True, name=f"gather_{piece}_l{l}_start")
        return token

    token = first[0]
    for k in range(eager, eager + ahead - 1):
        token = start_gather(k, token)
    mod = mod + token[0, 0]

    def weights_of(l, piece, h):
        k = order.index((l, piece))
        started = eager <= k + ahead < len(order) and k + ahead not in in_flight
        token = start_gather(k + ahead, h) if started else None
        if k < eager:
            return piece_weights(first[n_tensors * k:n_tensors * (k + 1)], piece), token
        landed = _exchange_wait(in_flight[k], h if token is None else token, gather=True, name=f"gather_{piece}_l{l}_wait")
        return piece_weights(landed, piece), token

    exchanges, have = {}, {}

    def grads_done(l, piece, g):
        key = (l, piece)
        have.setdefault(key, {}).update(g)
        if piece == "mix":
            if len(have[key]) < 5:
                return None
            g = have[key]
            s_br = jnp.concatenate([_lanes_shard(g[n][None]) for n in ("br_sb", "br_dil", "br_swa")], 1)
            groups = [(("in", "br", "out"), [g["in"].reshape(N_DEV, -1, d), s_br, g["out"].reshape(N_DEV, -1, d)])]
        elif key == order[0]:
            groups = [((n,), [t.reshape(N_DEV, -1, d)]) for n, t in g.items()]
        elif len(have[key]) < 3:
            return None
        else:
            groups = [(("gate", "up", "down"), [have[key][n].reshape(N_DEV, -1, d) for n in ("gate", "up", "down")])]
        token = None
        for names, sg in groups:
            state, token = _exchange_start(sg, None, gather=False, name=f"exchange_{piece}_l{l}_{names[0]}_start")
            exchanges.setdefault(key, []).append((names, state))
        return token

    loss, dx, dmod, dgains, dfinal, drel, dsinks = _local_step(
        x[0], loss_target[0], mod, gains, weights_of, rel_bias, sinks, final_gain, grads_done)

    flat = lambda t: t.reshape(-1, t.shape[-1])
    transposed = lambda ts: tuple(flat(jnp.swapaxes(t, -1, -2)) for t in ts)
    families = {
        "gate": transposed((w_ffn_gate, m_w_ffn_gate, v_w_ffn_gate)), "up": transposed((w_ffn_up, m_w_ffn_up, v_w_ffn_up)),
        "down": tuple(flat(t) for t in (w_ffn_down, m_w_ffn_down, v_w_ffn_down)),
        "in": transposed((w_in, m_w_in, v_w_in)),
        "br": tuple(flat(jnp.concatenate(ts, 1)) for ts in ((w_br_sb, w_br_dil, w_br_swa), (m_w_br_sb, m_w_br_dil, m_w_br_swa),
                                                            (v_w_br_sb, v_w_br_dil, v_w_br_swa))),
        "out": tuple(flat(t) for t in (w_out, m_w_out, v_w_out))}
    parts, stepped = {}, {}

    def land(l, after):
        for key in reversed([k for k in order if k[0] == l]):
            for names, ex_state in exchanges[key]:
                landed = _exchange_wait(ex_state, after, gather=False, name=f"exchange_{key[1]}_l{key[0]}_{names[0]}_wait")
                parts.setdefault(key, {}).update(zip(names, landed))
                after = landed[0]

    def step_layer(l):
        last = None
        for n, (w2, m2, v2) in families.items():
            groups = [parts[key][n] for key in order if key[0] == l and n in parts[key]]
            rows_per_layer = w2.shape[0] // DEPTH
            stepped[n] = _reduce_adamw(groups, w2, m2, v2, l * rows_per_layer, stepped.get(n), after=last,
                                       name=f"reduce_adamw_{n}_l{l}")
            last = stepped[n][1]
        return last

    land(1, dx)
    after_l1 = step_layer(1)

    small_parts = [dmod, dgains, dfinal, drel.T, dsinks, loss[0, :1]]
    small_sizes = [int(np.prod(p.shape)) for p in small_parts]
    small_all, = _all_gather([_pack_rows(small_parts, F32)], after=after_l1, name="gather_small")
    small_sum = _sum_parts([small_all], name="sum_small").reshape(-1)
    offs = np.concatenate([[0], np.cumsum(small_sizes)])
    g_b_ada = small_sum[offs[0]:offs[1]].reshape(DEPTH, 9 * d)
    g_gain_full = small_sum[offs[1]:offs[2]].reshape(DEPTH, 3, d)
    g_norm_gain = lax.dynamic_slice_in_dim(g_gain_full, me * LANES, LANES, axis=2)
    g_final = small_sum[offs[2]:offs[3]]
    g_rel = small_sum[offs[3]:offs[4]].reshape(N_SOFT, N_BUCKETS).T
    g_sinks = small_sum[offs[4]:offs[5]].reshape(DEPTH, H_SWA_Q)
    loss_total = small_sum[offs[5]]

    dmod_all = small_all.reshape(N_DEV, -1)[:, :DEPTH * 9 * d].reshape(N_DEV, DEPTH, 9 * d)
    dmod_cols = lax.dynamic_slice_in_dim(dmod_all, me * cols, cols, axis=2)
    g_w_ada = jnp.stack([_ada_bwd(c_all.T, dmod_cols[:, l], name=f"ada_bwd_l{l}") for l in range(DEPTH)])

    small_state = {"w_ada": (w_ada, m_w_ada, v_w_ada), "b_ada": (b_ada, m_b_ada, v_b_ada),
                   "norm_gain": (norm_gain, m_norm_gain, v_norm_gain), "sinks": (sinks, m_sinks, v_sinks),
                   "rel_bias": (rel_bias, m_rel_bias, v_rel_bias), "final_gain": (final_gain, m_final_gain, v_final_gain)}
    grad, update = {}, {}
    for n, g in (("w_ada", g_w_ada), ("b_ada", g_b_ada), ("norm_gain", g_norm_gain), ("sinks", g_sinks),
                 ("rel_bias", g_rel), ("final_gain", g_final)):
        w, m, v = small_state[n]
        grad[n] = g
        if w.ndim == 1:
            update[n] = tuple(t.reshape(w.shape) for t in _adamw(_row(w), _row(g), _row(m), _row(v), name=f"adamw_{n}"))
        else:
            update[n] = _adamw(w, g, m, v, name=f"adamw_{n}")

    land(0, update["w_ada"][0])
    step_layer(0)

    def unflat(n, like, swapped):
        shape = jnp.swapaxes(like, -1, -2).shape if swapped else like.shape
        out = [t.reshape(shape) for t in stepped[n]]
        return [jnp.swapaxes(t, -1, -2) for t in out] if swapped else out

    results = {"w_ffn_gate": unflat("gate", w_ffn_gate, True), "w_ffn_up": unflat("up", w_ffn_up, True),
               "w_ffn_down": unflat("down", w_ffn_down, False), "w_in": unflat("in", w_in, True),
               "w_out": unflat("out", w_out, False)}
    br = [t.reshape(DEPTH, -1, LANES) for t in stepped["br"]]
    for k, n in enumerate(("w_br_sb", "w_br_dil", "w_br_swa")):
        results[n] = [t[:, br_off[k]:br_off[k + 1]] for t in br]
    for n, (g, dl, nm, nv) in results.items():
        grad[n], update[n] = g, (dl, nm, nv)

    names = ["w_ada", "b_ada", "norm_gain", "w_ffn_gate", "w_ffn_up", "w_ffn_down", "w_in", "w_br_sb", "w_br_dil",
             "w_br_swa", "w_out", "sinks", "rel_bias", "final_gain"]
    return (loss_total, dx[None], *[grad[n] for n in names], *[update[n][0] for n in names],
            *[update[n][1] for n in names], *[update[n][2] for n in names])
```

```python
import math

import numpy as np
import jax
import jax.numpy as jnp
from jax import lax
from jax.experimental import pallas as pl
from jax.experimental.pallas import tpu as pltpu

F32, BF16 = jnp.float32, jnp.bfloat16

SEQ, D_MODEL, D_FF, HEAD_DIM = 2048, 1024, 2816, 64
DEPTH = 2
BLK = 128
H_SB, H_DIL, H_SWA_Q, H_SWA_KV = 4, 6, 6, 2
DIL_PATTERNS = ((128, 1), (512, 4), (2048, 16))
SWA_WINDOW = 128
N_BUCKETS, MAX_REL_DIST = 32, 2048
RMS_EPS = 1e-6
D_QKV = 2560
D_GATES = 3 * D_MODEL
ADAM_LR, ADAM_B1, ADAM_B2, ADAM_EPS, ADAM_WD, ADAM_STEP = 0.001, 0.9, 0.999, 1e-08, 0.01, 10

N_DEV = 8
LANES = 128
NEG = -1e30
SB_TILE = 512
VMEM_LIMIT_BYTES = 48 * 1024 * 1024
HBM = pl.BlockSpec(memory_space=pltpu.HBM)
MESH = pl.DeviceIdType.MESH


def _tile(n, target):
    t = (min(n, target) // LANES) * LANES
    while t >= LANES:
        if n % t == 0:
            return t
        t -= LANES
    return n


def _row_tile(r, cap):
    t = (min(r, cap) // 16) * 16
    while t > 16 and r % t:
        t -= 16
    return t


def _params(semantics=None):
    return pltpu.CompilerParams(dimension_semantics=semantics, vmem_limit_bytes=VMEM_LIMIT_BYTES)


def _dot(a, b, ca, cb):
    return lax.dot_general(a, b, (((ca,), (cb,)), ((), ())), preferred_element_type=F32)


def _sigmoid(a):
    return 1.0 / (1.0 + jnp.exp(-a))


def _row(v):
    return v.reshape(1, -1)


def _all_gather(arrs, name, after=None):
    n = len(arrs)
    ins = list(arrs) + ([] if after is None else [after])

    def body(*refs):
        x_refs, out_refs = refs[:n], refs[len(ins):len(ins) + n]
        send_sems, recv_sems, local_sems = refs[len(ins) + n:]
        x, y, c = lax.axis_index("x"), lax.axis_index("y"), lax.axis_index("c")
        me, sibling = (x, y, c), (x, y, 1 - c)
        chips = [(1 - x, y), (x, 1 - y), (1 - x, 1 - y)]

        def slot(t, px, py, pc):
            return out_refs[t].at[4 * px + 2 * py + pc]

        def copy(t, k, block, to, src=None):
            return pltpu.make_async_remote_copy(
                src_ref=slot(t, *block) if src is None else src, dst_ref=slot(t, *block),
                send_sem=send_sems.at[7 * t + k], recv_sem=recv_sems.at[7 * t + k], device_id=to, device_id_type=MESH)

        mine = [pltpu.make_async_copy(x_refs[t], slot(t, *me), local_sems.at[t]) for t in range(n)]
        for cp in mine:
            cp.start()
        first = []
        for t in range(n):
            first.append(copy(t, 0, me, sibling, src=x_refs[t]))
            first += [copy(t, 1 + j, me, (*chip, c), src=x_refs[t]) for j, chip in enumerate(chips)]
        for cp in first:
            cp.start()
        passed = []
        for j, chip in enumerate(chips):
            for t in range(n):
                copy(t, 1 + j, (*chip, c), me).wait_recv()
                passed.append(copy(t, 4 + j, (*chip, c), sibling))
                passed[-1].start()
        for t in range(n):
            copy(t, 0, sibling, me).wait_recv()
        for j, chip in enumerate(chips):
            for t in range(n):
                copy(t, 4 + j, (*chip, 1 - c), me).wait_recv()
        for cp in first + passed:
            cp.wait_send()
        for cp in mine:
            cp.wait()

    return pl.pallas_call(
        body, name=name, out_shape=[jax.ShapeDtypeStruct((N_DEV,) + a.shape, a.dtype) for a in arrs],
        in_specs=[HBM] * n + [pl.BlockSpec(memory_space=pl.ANY)] * (len(ins) - n), out_specs=[HBM] * n,
        scratch_shapes=[pltpu.SemaphoreType.DMA((7 * n,)), pltpu.SemaphoreType.DMA((7 * n,)), pltpu.SemaphoreType.DMA((n,))],
    )(*ins)


def _direct_copies(x_refs, land_refs, send_sems, recv_sems, local_sems, gather):
    x, y, c = lax.axis_index("x"), lax.axis_index("y"), lax.axis_index("c")
    me = 4 * x + 2 * y + c
    sends, recvs = [], []
    for k in range(1, N_DEV):
        px = 1 - x if (k >> 2) & 1 else x
        py = 1 - y if (k >> 1) & 1 else y
        pc = 1 - c if k & 1 else c
        peer = 4 * px + 2 * py + pc
        for t, (x_ref, land_ref) in enumerate(zip(x_refs, land_refs)):
            sem = 7 * t + k - 1
            for out, src, slot in ((sends, x_ref if gather else x_ref.at[peer], me),
                                   (recvs, x_ref if gather else x_ref.at[me], peer)):
                out.append(pltpu.make_async_remote_copy(
                    src_ref=src, dst_ref=land_ref.at[slot], send_sem=send_sems.at[sem], recv_sem=recv_sems.at[sem],
                    device_id=(px, py, pc), device_id_type=MESH))
    own = [pltpu.make_async_copy(x_ref if gather else x_ref.at[me], land_ref.at[me], local_sems.at[t])
           for t, (x_ref, land_ref) in enumerate(zip(x_refs, land_refs))]
    return sends, recvs, own


SEM =pl.BlockSpec(memory_space=pltpu.SEMAPHORE)
ANY = pl.BlockSpec(memory_space=pl.ANY)
SIDE_EFFECT = pltpu.SideEffectType.DATAFLOW_SIDE_EFFECTING


def _exchange_start(arrs, after, *, gather, name):
    n = len(arrs)
    lands = [lax.empty(((N_DEV,) + a.shape) if gather else a.shape, a.dtype) for a in arrs]
    extra = [] if after is None else [after]

    def body(*refs):
        sems = refs[2 * n + len(extra):2 * n + len(extra) + 3]
        sends, _, own = _direct_copies(refs[:n], refs[n:2 * n], *sems, gather)
        for cp in own + sends:
            cp.start()
        refs[-1][...] = jnp.zeros_like(refs[-1])

    ops = [pltpu.with_memory_space_constraint(a, pltpu.HBM) for a in list(arrs) + lands]
    out = pl.pallas_call(
        body, name=name,
        out_shape=(pltpu.SemaphoreType.DMA((7 * n,)), pltpu.SemaphoreType.DMA((7 * n,)), pltpu.SemaphoreType.DMA((n,)),
                   *[pltpu.HBM(a.shape, a.dtype) for a in ops], jax.ShapeDtypeStruct((8, LANES), F32)),
        in_specs=[HBM] * (2 * n) + [ANY] * len(extra),
        out_specs=(SEM, SEM, SEM, *[HBM] * (2 * n), pl.BlockSpec(memory_space=pltpu.VMEM)),
        input_output_aliases={t: 3 + t for t in range(2 * n)},
        compiler_params=pltpu.CompilerParams(has_side_effects=SIDE_EFFECT),
    )(*ops, *extra)
    return (out[:3], out[3:3 + n], out[3 + n:3 + 2 * n]), out[-1]


def _exchange_wait(state, after, *, gather, name):
    sems, arrs, lands = state
    n = len(arrs)

    def body(*refs):
        sends, recvs, own = _direct_copies(refs[:n], refs[n:2 * n], *refs[2 * n:2 * n + 3], gather)
        for cp in own:
            cp.wait()
        for cp in sends:
            cp.wait_send()
        for cp in recvs:
            cp.wait_recv()

    out = pl.pallas_call(
        body, name=name, out_shape=tuple(pltpu.HBM(a.shape, a.dtype) for a in list(arrs) + list(lands)),
        in_specs=[HBM] * (2 * n) + [SEM, SEM, SEM, ANY], out_specs=tuple([HBM] * (2 * n)),
        input_output_aliases={t: t for t in range(2 * n)},
        compiler_params=pltpu.CompilerParams(has_side_effects=SIDE_EFFECT),
    )(*arrs, *lands, *sems, after)
    return out[n:]


def _relay_copies(x_refs, land_refs, sems_a, sems_b):
    x, y, c = lax.axis_index("x"), lax.axis_index("y"), lax.axis_index("c")
    me = 4 * x + 2 * y + c
    sibling = (x, y, 1 - c)
    chips = [(1 - x, y), (x, 1 - y), (1 - x, 1 - y)]

    def slot(px, py, pc):
        return 4 * px + 2 * py + pc

    def copy(src, land_ref, dst_slot, send_sems, recv_sems, k, to):
        return pltpu.make_async_remote_copy(src_ref=src, dst_ref=land_ref.at[dst_slot], send_sem=send_sems.at[k],
                                            recv_sem=recv_sems.at[k], device_id=to, device_id_type=MESH)

    a_send, a_recv, a_own, b_send, b_recv = [], [], [], [], []
    for t, (x_ref, land_ref) in enumerate(zip(x_refs, land_refs)):
        peers = [sibling] + [(*chip, c) for chip in chips]
        if sems_a is not None:
            for k, peer in enumerate(peers):
                a_send.append(copy(x_ref, land_ref, me, sems_a[0], sems_a[1], 4 * t + k, peer))
                a_recv.append(copy(x_ref, land_ref, slot(*peer), sems_a[0], sems_a[1], 4 * t + k, peer))
            a_own.append(pltpu.make_async_copy(x_ref, land_ref.at[me], sems_a[2].at[t]))
        if sems_b is not None:
            for j, chip in enumerate(chips):
                b_send.append(copy(land_ref.at[slot(*chip, c)], land_ref, slot(*chip, c), sems_b[0], sems_b[1], 3 * t + j, sibling))
                b_recv.append(copy(land_ref.at[slot(*chip, c)], land_ref, slot(*chip, 1 - c), sems_b[0], sems_b[1], 3 * t + j,
                                   sibling))
    return (a_send, a_recv, a_own), (b_send, b_recv)


def _relay_start(arrs, after, name):
    n = len(arrs)
    lands = [lax.empty((N_DEV,) + a.shape, a.dtype) for a in arrs]

    def body(*refs):
        (sends, _, own), _ = _relay_copies(refs[:n], refs[n:2 * n], refs[2 * n + 1:2 * n + 4], None)
        for cp in own + sends:
            cp.start()
        refs[-1][...] = jnp.zeros_like(refs[-1])

    ops = [pltpu.with_memory_space_constraint(a, pltpu.HBM) for a in list(arrs) + lands]
    out = pl.pallas_call(
        body, name=name,
        out_shape=(pltpu.SemaphoreType.DMA((4 * n,)), pltpu.SemaphoreType.DMA((4 * n,)), pltpu.SemaphoreType.DMA((n,)),
                   *[pltpu.HBM(a.shape, a.dtype) for a in ops], jax.ShapeDtypeStruct((8, LANES), F32)),
        in_specs=[HBM] * (2 * n) + [ANY],
        out_specs=(SEM, SEM, SEM, *[HBM] * (2 * n), pl.BlockSpec(memory_space=pltpu.VMEM)),
        input_output_aliases={t: 3 + t for t in range(2 * n)},
        compiler_params=pltpu.CompilerParams(has_side_effects=SIDE_EFFECT),
    )(*ops, after)
    return (out[:3], out[3:3 + n], out[3 + n:3 + 2 * n]), out[-1]


def _relay_pass(state, after, name):
    sems_a, arrs, lands = state
    n = len(arrs)

    def body(*refs):
        sems_b = refs[2 * n + 4:2 * n + 6]
        (a_send, a_recv, a_own), (b_send, _) = _relay_copies(refs[:n], refs[n:2 * n], refs[2 * n:2 * n + 3], sems_b)
        for cp in a_own:
            cp.wait()
        for cp in a_send:
            cp.wait_send()
        for cp in a_recv:
            cp.wait_recv()
        for cp in b_send:
            cp.start()
        refs[-1][...] = jnp.zeros_like(refs[-1])

    out = pl.pallas_call(
        body, name=name,
        out_shape=(pltpu.SemaphoreType.DMA((3 * n,)), pltpu.SemaphoreType.DMA((3 * n,)),
                   *[pltpu.HBM(a.shape, a.dtype) for a in list(arrs) + list(lands)], jax.ShapeDtypeStruct((8, LANES), F32)),
        in_specs=[HBM] * (2 * n) + [SEM, SEM, SEM, ANY],
        out_specs=(SEM, SEM, *[HBM] * (2 * n), pl.BlockSpec(memory_space=pltpu.VMEM)),
        input_output_aliases={t: 2 + t for t in range(2 * n)},
        compiler_params=pltpu.CompilerParams(has_side_effects=SIDE_EFFECT),
    )(*arrs, *lands, *sems_a, after)
    return (out[:2], out[2:2 + n], out[2 + n:2 + 2 * n]), out[-1]


def _relay_wait(state, after, name):
    sems_b, arrs, lands = state
    n = len(arrs)

    def body(*refs):
        _, (b_send, b_recv) = _relay_copies(refs[:n], refs[n:2 * n], None, refs[2 * n:2 * n + 2])
        for cp in b_send:
            cp.wait_send()
        for cp in b_recv:
            cp.wait_recv()

    out = pl.pallas_call(
        body, name=name, out_shape=tuple(pltpu.HBM(a.shape, a.dtype) for a in list(arrs) + list(lands)),
        in_specs=[HBM] * (2 * n) + [SEM, SEM, ANY], out_specs=tuple([HBM] * (2 * n)),
        input_output_aliases={t: t for t in range(2 * n)},
        compiler_params=pltpu.CompilerParams(has_side_effects=SIDE_EFFECT),
    )(*arrs, *lands, *sems_b, after)
    return out[n:]


def _sum_parts(groups, name):
    n, r, cdim = groups[0].shape
    tr = _row_tile(r, max(16, (1 << 21) // (n * cdim * groups[0].dtype.itemsize)))
    steps = r // tr

    def body(*refs):
        o_ref = refs[-1]
        gg = pl.program_id(0)
        for gi in range(len(groups)):
            @pl.when(gg == gi)
            def _(gi=gi):
                acc = refs[gi][0].astype(F32)
                for k in range(1, n):
                    acc = acc + refs[gi][k].astype(F32)
                o_ref[...] = acc

    def in_spec(gi):
        return pl.BlockSpec((n, tr, cdim), lambda gg, i: (0, jnp.where(gg == gi, i, 0), 0))

    return pl.pallas_call(
        body, name=name, out_shape=jax.ShapeDtypeStruct((len(groups) * r, cdim), F32), grid=(len(groups), steps),
        in_specs=[in_spec(gi) for gi in range(len(groups))],
        out_specs=pl.BlockSpec((tr, cdim), lambda gg, i: (gg * steps + i, 0)),
        compiler_params=_params(("parallel", "parallel")),
    )(*groups)


def _mm_tn(a, b, *, name, after=None, tm=512, tn=1024, out_rows=None, row0=0, prev=None):
    k, m = a.shape
    n = b.shape[1]
    tm, tn = _tile(m, tm), _tile(n, tn)
    out_rows = m if out_rows is None else out_rows

    def body(a_ref, b_ref, *rest):
        o_ref, at_ref = rest[-2], rest[-1]

        @pl.when(pl.program_id(1) == 0)
        def _():
            at_ref[...] = a_ref[...].astype(BF16).T

        o_ref[...] = _dot(at_ref[...], b_ref[...].astype(BF16), 1, 0).astype(BF16)

    ins = [a, b] + [t for t in (after, prev) if t is not None]
    return pl.pallas_call(
        body, name=name, out_shape=jax.ShapeDtypeStruct((out_rows, n), BF16), grid=(m // tm, n // tn),
        in_specs=[pl.BlockSpec((k, tm), lambda i, j: (0, i)), pl.BlockSpec((k, tn), lambda i, j: (0, j))] + [ANY] * (len(ins) - 2),
        out_specs=pl.BlockSpec((tm, tn), lambda i, j: (row0 // tm + i, j)),
        input_output_aliases={} if prev is None else {len(ins) - 1: 0},
        scratch_shapes=[pltpu.VMEM((tm, k), BF16)], compiler_params=_params(("parallel", "arbitrary")),
    )(*ins)


def _mm2(a1, b1, a2, b2, *, name, after=None, tm=256, tn=1024, b_rows=None):
    m = a1.shape[0]
    n = b1.shape[1]
    tm, tn = _tile(m, tm), _tile(n, tn)

    def body(a1_ref, b1_ref, a2_ref, b2_ref, *rest):
        rest[-1][...] = (_dot(a1_ref[...].astype(BF16), b1_ref[...], 1, 0)
                         + _dot(a2_ref[...].astype(BF16), b2_ref[...], 1, 0))

    ins = [a1, b1, a2, b2] + ([] if after is None else [after])

    def a_spec(t):
        return pl.BlockSpec((tm, t.shape[1]), lambda i, j: (i, 0))

    def b_spec(t, a, which):
        if b_rows is None:
            return pl.BlockSpec((t.shape[0], tn), lambda i, j: (0, j))
        start = b_rows[which]
        return pl.BlockSpec((pl.Element(a.shape[1]), pl.Element(tn)), lambda i, j: (start, j * tn))

    return pl.pallas_call(
        body, name=name, out_shape=jax.ShapeDtypeStruct((m, n), F32), grid=(m // tm, n // tn),
        in_specs=[a_spec(a1), b_spec(b1, a1, 0), a_spec(a2), b_spec(b2, a2, 1)] + [ANY] * (len(ins) - 4),
        out_specs=pl.BlockSpec((tm, tn), lambda i, j: (i, j)), compiler_params=_params(("parallel", "parallel")),
    )(*ins)


def _mm(a, b, *, name, ta=False, tb=False, res=None, colscale=None, emit_acc=False,
        out_dtype=F32, tm=512, tn=512, b_rows=None):
    m, k = (a.shape[1], a.shape[0]) if ta else a.shape
    n = b.shape[0] if tb else b.shape[1]
    b_start = 0
    if b_rows is not None:
        b_start, n = b_rows
    tm, tn = _tile(m, tm), _tile(n, tn)
    ca, cb = (0 if ta else 1), (1 if tb else 0)
    a_spec = pl.BlockSpec((k, tm), lambda i, j: (0, i)) if ta else pl.BlockSpec((tm, k), lambda i, j: (i, 0))
    b_spec = (pl.BlockSpec((tn, k), lambda i, j: (b_start // tn + j, 0)) if tb
              else pl.BlockSpec((k, tn), lambda i, j: (0, j)))
    tile = pl.BlockSpec((tm, tn), lambda i, j: (i, j))
    ins, in_specs = [a, b], [a_spec, b_spec]
    if res is not None:
        ins.append(res)
        in_specs.append(tile)
    if colscale is not None:
        ins.append(colscale)
        in_specs.append(pl.BlockSpec((1, tn), lambda i, j: (0, j)))
    n_in = len(ins)

    def body(*refs):
        outs = refs[n_in:]
        acc = _dot(refs[0][...].astype(BF16), refs[1][...].astype(BF16), ca, cb)
        val, p = acc, 2
        if res is not None:
            r_val, p = refs[p][...], p + 1
        if colscale is not None:
            val = val * refs[p][...]
        if res is not None:
            val = r_val + val
        if emit_acc:
            outs[0][...] = acc
        outs[-1][...] = val.astype(out_dtype)

    out_shape = [jax.ShapeDtypeStruct((m, n), out_dtype)]
    out_specs = [tile]
    if emit_acc:
        out_shape.insert(0, jax.ShapeDtypeStruct((m, n), F32))
        out_specs.insert(0, tile)
    out = pl.pallas_call(
        body, name=name, out_shape=out_shape, grid=(m // tm, n // tn), in_specs=in_specs, out_specs=out_specs,
        compiler_params=_params(("parallel", "parallel")),
    )(*ins)
    return out if emit_acc else out[0]


def _norm_fwd(x, g, scale, shift, name, after=None):
    s, d = x.shape
    tr = 256

    def body(x_ref, g_ref, sc_ref, sh_ref, *rest):
        xv = x_ref[...]
        rstd = lax.rsqrt(jnp.mean(xv * xv, axis=-1, keepdims=True) + RMS_EPS)
        rest[-1][...] = (xv * rstd * g_ref[...] * (1.0 + sc_ref[...]) + sh_ref[...]).astype(BF16)

    rowspec = pl.BlockSpec((1, d), lambda i: (0, 0))
    ins = [x, g, scale, shift] + ([] if after is None else [after])
    return pl.pallas_call(
        body, name=name, out_shape=jax.ShapeDtypeStruct((s, d), BF16), grid=(s // tr,),
        in_specs=[pl.BlockSpec((tr, d), lambda i: (i, 0)), rowspec, rowspec, rowspec] + [ANY] * (len(ins) - 4),
        out_specs=pl.BlockSpec((tr, d), lambda i: (i, 0)),
        compiler_params=_params(("parallel",)),
    )(*ins)


def _norm_bwd(x, dh, dres, g, scale, name):
    s, d = x.shape
    tr = 256

    def body(x_ref, dh_ref, dr_ref, g_ref, sc_ref, dx_ref, a_ref, b_ref):
        @pl.when(pl.program_id(0) == 0)
        def _():
            a_ref[...] = jnp.zeros_like(a_ref)
            b_ref[...] = jnp.zeros_like(b_ref)

        xv = x_ref[...]
        rstd = lax.rsqrt(jnp.mean(xv * xv, axis=-1, keepdims=True) + RMS_EPS)
        xhat = xv * rstd
        dhv = dh_ref[...]
        dxhat = dhv * (g_ref[...] * (1.0 + sc_ref[...]))
        mean_term = jnp.mean(dxhat * xhat, axis=-1, keepdims=True)
        dx_ref[...] = dr_ref[...] + rstd * (dxhat - xhat * mean_term)
        a_ref[...] += jnp.sum(dhv, axis=0, keepdims=True)
        b_ref[...] += jnp.sum(dhv * xhat, axis=0, keepdims=True)

    rowspec = pl.BlockSpec((1, d), lambda i: (0, 0))
    tile = pl.BlockSpec((tr, d), lambda i: (i, 0))
    return pl.pallas_call(
        body, name=name,
        out_shape=[jax.ShapeDtypeStruct((s, d), F32), jax.ShapeDtypeStruct((1, d), F32), jax.ShapeDtypeStruct((1, d), F32)],
        grid=(s // tr,), in_specs=[tile, tile, tile, rowspec, rowspec], out_specs=[tile, rowspec, rowspec],
        compiler_params=_params(("arbitrary",)),
    )(x, dh, dres, g, scale)


def _gate_bwd(dxn, f, colscale, coef, name):
    s, d = dxn.shape
    tr = 256

    def body(dx_ref, f_ref, cs_ref, df_ref, dg_ref):
        @pl.when(pl.program_id(0) == 0)
        def _():
            dg_ref[...] = jnp.zeros_like(dg_ref)

        dxv = dx_ref[...]
        df_ref[...] = (dxv * cs_ref[...]).astype(BF16)
        dg_ref[...] += coef * jnp.sum(dxv * f_ref[...], axis=0, keepdims=True)

    rowspec = pl.BlockSpec((1, d), lambda i: (0, 0))
    tile = pl.BlockSpec((tr, d), lambda i: (i, 0))
    return pl.pallas_call(
        body, name=name, out_shape=[jax.ShapeDtypeStruct((s, d), BF16), jax.ShapeDtypeStruct((1, d), F32)],
        grid=(s // tr,), in_specs=[tile, tile, rowspec], out_specs=[tile, rowspec],
        compiler_params=_params(("arbitrary",)),
    )(dxn, f, colscale)


def _ffn_up(h, wg, wu, name, tm=SEQ, tn=256):
    s, d = h.shape
    f = wg.shape[0]

    def body(h_ref, wg_ref, wu_ref, a_ref, u_ref, s_ref):
        hv = h_ref[...]
        a = _dot(hv, wg_ref[...], 1, 1)
        u = _dot(hv, wu_ref[...], 1, 1)
        a_ref[...] = a.astype(BF16)
        u_ref[...] = u.astype(BF16)
        s_ref[...] = (a * _sigmoid(a) * u).astype(BF16)

    tile = pl.BlockSpec((tm, tn), lambda i, j: (i, j))
    wspec = pl.BlockSpec((tn, d), lambda i, j: (j, 0))
    return pl.pallas_call(
        body, name=name,
        out_shape=[jax.ShapeDtypeStruct((s, f), BF16), jax.ShapeDtypeStruct((s, f), BF16), jax.ShapeDtypeStruct((s, f), BF16)],
        grid=(s // tm, f // tn), in_specs=[pl.BlockSpec((tm, d), lambda i, j: (i, 0)), wspec, wspec],
        out_specs=[tile, tile, tile], compiler_params=_params(("parallel", "parallel")),
    )(h, wg, wu)


def _ffn_bwd_ds(df, wd, a, u, name, tm=SEQ, tn=256):
    s, d = df.shape
    f = wd.shape[0]

    def body(df_ref, wd_ref, a_ref, u_ref, da_ref, du_ref):
        ds = _dot(df_ref[...], wd_ref[...], 1, 1)
        av = a_ref[...].astype(F32)
        sg = _sigmoid(av)
        da_ref[...] = (ds * u_ref[...].astype(F32) * (sg * (1.0 + av * (1.0 - sg)))).astype(BF16)
        du_ref[...] = (ds * (av * sg)).astype(BF16)

    tile = pl.BlockSpec((tm, tn), lambda i, j: (i, j))
    return pl.pallas_call(
        body, name=name, out_shape=[jax.ShapeDtypeStruct((s, f), BF16), jax.ShapeDtypeStruct((s, f), BF16)],
        grid=(s // tm, f // tn),
        in_specs=[pl.BlockSpec((tm, d), lambda i, j: (i, 0)), pl.BlockSpec((tn, d), lambda i, j: (j, 0)), tile, tile],
        out_specs=[tile, tile], compiler_params=_params(("parallel", "parallel")),
    )(df, wd, a, u)


def _merge_fwd(o_sb, o_dil, o_swa, gates, wb_sb, wb_dil, wb_swa, name):
    s, d = SEQ, D_MODEL
    tm = 256

    def body(osb_ref, odl_ref, osw_ref, g_ref, wsb_ref, wdl_ref, wsw_ref, m_ref, tsb_ref, tdl_ref, tsw_ref):
        for h in range(osb_ref.shape[0]):
            tsb_ref[:, h * HEAD_DIM:(h + 1) * HEAD_DIM] = osb_ref[h].astype(BF16)
        for h in range(osw_ref.shape[0]):
            tsw_ref[:, h * HEAD_DIM:(h + 1) * HEAD_DIM] = osw_ref[h].astype(BF16)
        tdl_ref[...] = odl_ref[...].astype(BF16)
        acc = _sigmoid(g_ref[:, 0:d]) * _dot(tsb_ref[...], wsb_ref[...], 1, 0)
        acc += _sigmoid(g_ref[:, d:2 * d]) * _dot(tdl_ref[...], wdl_ref[...], 1, 0)
        acc += _sigmoid(g_ref[:, 2 * d:3 * d]) * _dot(tsw_ref[...], wsw_ref[...], 1, 0)
        m_ref[...] = acc.astype(BF16)

    def rows(w):
        return pl.BlockSpec((tm, w), lambda i: (i, 0))

    def heads(n):
        return pl.BlockSpec((n, tm, HEAD_DIM), lambda i: (0, i, 0))

    def whole(w):
        return pl.BlockSpec((w, d), lambda i: (0, 0))

    return pl.pallas_call(
        body, name=name, out_shape=[jax.ShapeDtypeStruct((s, w), BF16) for w in (d, 256, 128, 384)], grid=(s // tm,),
        in_specs=[heads(H_SB), rows(128), heads(H_SWA_Q), rows(3 * d), whole(256), whole(128), whole(384)],
        out_specs=[rows(d), rows(256), rows(128), rows(384)], compiler_params=_params(("parallel",)),
    )(o_sb, o_dil, o_swa, gates, wb_sb, wb_dil, wb_swa)


def _merge_bwd(dmerged, t_sb, t_dil, t_swa, gates, wb_sb, wb_dil, wb_swa, name):
    s, d = SEQ, D_MODEL
    tm = 256

    def body(dm_ref, tsb_ref, tdl_ref, tsw_ref, g_ref, wsb_ref, wdl_ref, wsw_ref,
             dg_ref, dosb_ref, dodl_ref, dosw_ref, dbsb_ref, dbdl_ref, dbsw_ref):
        dm = dm_ref[...]
        for idx, (t_ref, w_ref, do_ref, db_ref) in enumerate((
                (tsb_ref, wsb_ref, dosb_ref, dbsb_ref), (tdl_ref, wdl_ref, dodl_ref, dbdl_ref),
                (tsw_ref, wsw_ref, dosw_ref, dbsw_ref))):
            w = w_ref[...]
            br = _dot(t_ref[...], w, 1, 0)
            sg = _sigmoid(g_ref[:, idx * d:(idx + 1) * d])
            dbr = (dm * sg).astype(BF16)
            dg_ref[:, idx * d:(idx + 1) * d] = (dm * br * (sg * (1.0 - sg))).astype(BF16)
            db_ref[...] = dbr
            do = _dot(dbr, w, 1, 1)
            if len(do_ref.shape) == 2:
                do_ref[...] = do
            else:
                for h in range(do_ref.shape[0]):
                    do_ref[h] = do[:, h * HEAD_DIM:(h + 1) * HEAD_DIM]

    def rows(w):
        return pl.BlockSpec((tm, w), lambda i: (i, 0))

    def heads(n):
        return pl.BlockSpec((n, tm, HEAD_DIM), lambda i: (0, i, 0))

    def whole(w):
        return pl.BlockSpec((w, d), lambda i: (0, 0))

    def shp(w, dt):
        return jax.ShapeDtypeStruct((s, w), dt)

    def hshp(n):
        return jax.ShapeDtypeStruct((n, s, HEAD_DIM), F32)

    return pl.pallas_call(
        body, name=name,
        out_shape=[shp(3 * d, BF16), hshp(H_SB), shp(128, F32), hshp(H_SWA_Q), shp(d, BF16), shp(d, BF16), shp(d, BF16)],
        grid=(s // tm,),
        in_specs=[rows(d), rows(256), rows(128), rows(384), rows(3 * d), whole(256), whole(128), whole(384)],
        out_specs=[rows(3 * d), heads(H_SB), rows(128), heads(H_SWA_Q), rows(d), rows(d), rows(d)],
        compiler_params=_params(("parallel",)),
    )(dmerged, t_sb, t_dil, t_swa, gates, wb_sb, wb_dil, wb_swa)


def _final_loss(x, target, g, name):
    s, d = x.shape
    tr = 256

    def body(x_ref, t_ref, g_ref, loss_ref, dx_ref, dg_ref):
        @pl.when(pl.program_id(0) == 0)
        def _():
            loss_ref[...] = jnp.zeros_like(loss_ref)
            dg_ref[...] = jnp.zeros_like(dg_ref)

        xv = x_ref[...]
        gv = g_ref[...]
        rstd = lax.rsqrt(jnp.mean(xv * xv, axis=-1, keepdims=True) + RMS_EPS)
        xhat = xv * rstd
        err = xhat * gv - t_ref[...]
        loss_ref[...] += 0.5 * jnp.sum(jnp.mean(err * err, axis=-1, keepdims=True))
        dy = err * (1.0 / d)
        dxhat = dy * gv
        mean_term = jnp.mean(dxhat * xhat, axis=-1, keepdims=True)
        dx_ref[...] = rstd * (dxhat - xhat * mean_term)
        dg_ref[...] += jnp.sum(dy * xhat, axis=0, keepdims=True)

    rowspec = pl.BlockSpec((1, d), lambda i: (0, 0))
    tile = pl.BlockSpec((tr, d), lambda i: (i, 0))
    return pl.pallas_call(
        body, name=name,
        out_shape=[jax.ShapeDtypeStruct((1, LANES), F32), jax.ShapeDtypeStruct((s, d), F32), jax.ShapeDtypeStruct((1, d), F32)],
        grid=(s // tr,), in_specs=[tile, tile, rowspec],
        out_specs=[pl.BlockSpec((1, LANES), lambda i: (0, 0)), tile, rowspec],
        compiler_params=_params(("arbitrary",)),
    )(x, target, g)


def _adamw(w, g, m, v, name):
    shape = w.shape
    cols = shape[-1]
    rows = int(np.prod(shape[:-1])) if len(shape) > 1 else 1
    tr = rows
    for cand in (1024, 512, 256, 128, 64, 32, 16, 8):
        if rows % cand == 0 and rows > cand and cand * cols * 4 <= (1 << 21):
            tr = cand
            break

    def body(w_ref, g_ref, m_ref, v_ref, d_ref, nm_ref, nv_ref):
        d_ref[...], nm_ref[...], nv_ref[...] = _adam_update(w_ref[...], g_ref[...], m_ref[...], v_ref[...])

    tile = pl.BlockSpec((tr, cols), lambda i: (i, 0))
    flat = [t.reshape(rows, cols) for t in (w, g, m, v)]
    out = pl.pallas_call(
        body, name=name, out_shape=[jax.ShapeDtypeStruct((rows, cols), F32)] * 3, grid=(rows // tr,),
        in_specs=[tile] * 4, out_specs=[tile] * 3, compiler_params=_params(("parallel",)),
    )(*flat)
    return tuple(t.reshape(shape) for t in out)


def _adam_update(w, gv, m, v):
    nm = ADAM_B1 * m + (1.0 - ADAM_B1) * gv
    nv = ADAM_B2 * v + (1.0 - ADAM_B2) * (gv * gv)
    m_hat = nm / (1.0 - ADAM_B1 ** ADAM_STEP)
    v_hat = nv / (1.0 - ADAM_B2 ** ADAM_STEP)
    return -ADAM_LR * (m_hat / (jnp.sqrt(v_hat) + ADAM_EPS) + ADAM_WD * w), nm, nv


def _reduce_adamw(groups, w, m, v, row0, prev, name, after=None):
    n, r, cdim = groups[0].shape
    rows = w.shape[0]
    tr = _row_tile(r, max(16, (1 << 22) // (n * cdim * groups[0].dtype.itemsize)))
    steps = r // tr
    ng = len(groups)

    def body(*refs):
        w_ref, m_ref, v_ref = refs[ng:ng + 3]
        g_out, d_out, m_out, v_out = refs[-4:]
        gg = pl.program_id(0)
        for gi in range(ng):
            @pl.when(gg == gi)
            def _(gi=gi):
                acc = refs[gi][0].astype(F32)
                for k in range(1, n):
                    acc = acc + refs[gi][k].astype(F32)
                g_out[...] = acc
                d_out[...], m_out[...], v_out[...] = _adam_update(w_ref[...], acc, m_ref[...], v_ref[...])

    def part_spec(gi):
        return pl.BlockSpec((n, tr, cdim), lambda gg, i: (0, jnp.where(gg == gi, i, 0), 0))

    tile = pl.BlockSpec((tr, cdim), lambda gg, i: (row0 // tr + gg * steps + i, 0))
    extra = ([] if prev is None else list(prev)) + ([] if after is None else [after])
    return pl.pallas_call(
        body, name=name, out_shape=[jax.ShapeDtypeStruct((rows, cdim), F32)] * 4, grid=(ng, steps),
        in_specs=[part_spec(gi) for gi in range(ng)] + [tile] * 3 + [ANY] * len(extra), out_specs=[tile] * 4,
        input_output_aliases={} if prev is None else {ng + 3 + k: k for k in range(4)},
        compiler_params=_params(("parallel", "parallel")),
    )(*groups, w, m, v, *extra)


def _ada_fwd(c_all, w, name):
    n = w.shape[1]

    def body(c_ref, w_ref, o_ref):
        cv = c_ref[...]
        o_ref[...] = jnp.dot(cv * _sigmoid(cv), w_ref[...], preferred_element_type=F32, precision=lax.Precision.HIGHEST)

    return pl.pallas_call(body, name=name, out_shape=jax.ShapeDtypeStruct((N_DEV, n), F32), compiler_params=_params())(c_all, w)


def _ada_bwd(c_all_t, dmod, name):
    n = dmod.shape[1]

    def body(c_ref, d_ref, o_ref):
        cv = c_ref[...]
        o_ref[...] = jnp.dot(cv * _sigmoid(cv), d_ref[...], preferred_element_type=F32, precision=lax.Precision.HIGHEST)

    return pl.pallas_call(body, name=name, out_shape=jax.ShapeDtypeStruct((D_MODEL, n), F32), compiler_params=_params())(c_all_t, dmod)


def _bucket_tables():
    rel = np.arange(BLK)[:, None] + BLK - np.arange(2 * BLK)[None, :]
    max_exact = N_BUCKETS // 2

    def bucket(n):
        nf = np.maximum(n, 1).astype(np.float32)
        large = max_exact + (np.log(nf / np.float32(max_exact)) / np.float32(math.log(MAX_REL_DIST / max_exact))
                             * np.float32(N_BUCKETS - max_exact)).astype(np.int32)
        return np.where(n < max_exact, n, np.minimum(large, N_BUCKETS - 1))

    tabs = []
    for dil, max_dist in ((1, 128), (4, 128), (16, 128), (1, SWA_WINDOW - 1)):
        in_band = (rel >= 0) & (rel <= max_dist)
        tabs.append(np.where(in_band, bucket(np.maximum(rel, 0) * dil), -1))
    return np.stack(tabs).astype(np.int32)


N_SOFT = H_DIL + H_SWA_Q


def _table_of_head(h):
    return jnp.minimum(h // 2, 3)


def _bias_build(rel_bias, tables, name):
    def body(rel_ref, t_ref, o_ref):
        h = pl.program_id(0)
        tb = t_ref[0]
        out = jnp.full((BLK, 2 * BLK), NEG, F32)
        for b in range(N_BUCKETS):
            out = jnp.where(tb == b, rel_ref[b, h], out)
        o_ref[0] = out

    return pl.pallas_call(
        body, name=name, out_shape=jax.ShapeDtypeStruct((N_SOFT, BLK, 2 * BLK), F32), grid=(N_SOFT,),
        in_specs=[pl.BlockSpec(memory_space=pltpu.SMEM),
                  pl.BlockSpec((1, BLK, 2 * BLK), lambda h: (_table_of_head(h), 0, 0))],
        out_specs=pl.BlockSpec((1, BLK, 2 * BLK), lambda h: (h, 0, 0)),
        compiler_params=_params(("parallel",)),
    )(rel_bias, tables)


def _bias_grad(dbias, tables, name):
    def body(d_ref, t_ref, o_ref):
        tb = t_ref[0]
        dv = d_ref[0]
        lane = lax.broadcasted_iota(jnp.int32, (1, LANES), 1)
        out = jnp.zeros((1, LANES), F32)
        for b in range(N_BUCKETS):
            out = jnp.where(lane == b, jnp.sum(jnp.where(tb == b, dv, 0.0)), out)
        o_ref[0] = out

    return pl.pallas_call(
        body, name=name, out_shape=jax.ShapeDtypeStruct((N_SOFT, 1, LANES), F32), grid=(N_SOFT,),
        in_specs=[pl.BlockSpec((1, BLK, 2 * BLK), lambda h: (h, 0, 0)),
                  pl.BlockSpec((1, BLK, 2 * BLK), lambda h: (_table_of_head(h), 0, 0))],
        out_specs=pl.BlockSpec((1, 1, LANES), lambda h: (h, 0, 0)),
        compiler_params=_params(("parallel",)),
    )(dbias, tables)


def _band_layout(g, bias_div):
    assert g == 1 or bias_div == 1
    return bias_div if g == 1 else 1


def _band_specs(length, g, bias_div, offs):
    ns = _band_layout(g, bias_div)

    def seqs(off, div=1):
        return pl.BlockSpec((ns, length, HEAD_DIM), lambda s: (off // ns + s // div, 0, 0))

    xspecs = [seqs(offs[0]), seqs(offs[1], g), seqs(offs[2], g)]
    bspec = pl.BlockSpec((1, BLK, 2 * BLK), lambda s: (s, 0, 0))
    sspec = pl.BlockSpec((ns, 1, LANES), lambda s: (s, 0, 0))
    colspec = pl.BlockSpec((ns, length, 1), lambda s: (s, 0, 0))
    return xspecs, seqs(0), seqs(0, g), bspec, sspec, colspec


def _band_sweep(length, ns, one):
    nblk = length // BLK
    for qq in range(ns):
        if ns * nblk <= 16:
            for i in range(nblk):
                one(qq, i * BLK, max(i - 1, 0) * BLK, i == 0)
        else:
            def step(i, carry, qq=qq):
                one(qq, pl.multiple_of(i * BLK, BLK), pl.multiple_of(jnp.maximum(i - 1, 0) * BLK, BLK), i == 0)
                return carry

            lax.fori_loop(0, nblk, step, 0, unroll=2)


def _band_scores(q_ref, k_ref, b_ref, qq, kq, bq, cur, prv, first):
    qv = q_ref[qq, pl.ds(cur, BLK), :]
    bv = b_ref[bq]
    if first is True:
        sp = jnp.full((BLK, BLK), NEG, F32)
    else:
        sp = _dot(qv, k_ref[kq, pl.ds(prv, BLK), :], 1, 1) + bv[:, :BLK]
        sp = sp if first is False else jnp.where(first, NEG, sp)
    sc = _dot(qv, k_ref[kq, pl.ds(cur, BLK), :], 1, 1) + bv[:, BLK:]
    return qv, sp, sc


def _band_fwd(x, bias, sink, *, nq, offs, g, bias_div, has_sink, name):
    length = x.shape[1]
    ns = _band_layout(g, bias_div)

    def body(q_ref, k_ref, v_ref, b_ref, s_ref, o_ref, lse_ref):
        def one(qq, cur, prv, first):
            kq, bq = qq, 0
            _, sp, sc = _band_scores(q_ref, k_ref, b_ref, qq, kq, bq, cur, prv, first)
            m = jnp.maximum(jnp.max(sp, axis=1, keepdims=True), jnp.max(sc, axis=1, keepdims=True))
            if has_sink:
                sk = s_ref[qq][:, :1]
                m = jnp.maximum(m, sk)
            pp, pc = jnp.exp(sp - m), jnp.exp(sc - m)
            den = jnp.sum(pp, axis=1, keepdims=True) + jnp.sum(pc, axis=1, keepdims=True)
            if has_sink:
                den = den + jnp.exp(sk - m)
            acc = (_dot(pp.astype(BF16), v_ref[kq, pl.ds(prv, BLK), :], 1, 0)
                   + _dot(pc.astype(BF16), v_ref[kq, pl.ds(cur, BLK), :], 1, 0))
            o_ref[qq, pl.ds(cur, BLK), :] = acc / den
            lse_ref[qq, pl.ds(cur, BLK), :] = m + jnp.log(den)

        _band_sweep(length, ns, one)

    xspecs, qspec, _, bspec, sspec, colspec = _band_specs(length, g, bias_div, offs)
    return pl.pallas_call(
        body, name=name,
        out_shape=[jax.ShapeDtypeStruct((nq, length, HEAD_DIM), F32), jax.ShapeDtypeStruct((nq, length, 1), F32)],
        grid=(nq // ns,), in_specs=xspecs + [bspec, sspec],
        out_specs=[qspec, colspec], compiler_params=_params(("parallel",)),
    )(x, x, x, bias, sink)


def _band_bwd(x, bias, sink, o, lse, do, dlse, *, nq, offs, g, bias_div, has_sink, name):
    length = x.shape[1]
    ns = _band_layout(g, bias_div)
    nk, nbias = nq // g, nq // bias_div

    def body(q_ref, k_ref, v_ref, b_ref, s_ref, o_ref, lse_ref, do_ref, dlse_ref,
             dq_ref, dk_ref, dv_ref, db_ref, dsk_ref, dkp_ref, dvp_ref):
        for ref in (db_ref, dsk_ref, dkp_ref, dvp_ref):
            ref[...] = jnp.zeros_like(ref)

        @pl.when(pl.program_id(0) % g == 0)
        def _():
            dk_ref[...] = jnp.zeros_like(dk_ref)
            dv_ref[...] = jnp.zeros_like(dv_ref)

        def one(qq, cur, prv, first):
            kq, bq = qq, 0
            qv, sp, sc = _band_scores(q_ref, k_ref, b_ref, qq, kq, bq, cur, prv, first)
            rows, prow = pl.ds(cur, BLK), pl.ds(prv, BLK)
            lse_v = lse_ref[qq, rows, :]
            pp, pc = jnp.exp(sp - lse_v), jnp.exp(sc - lse_v)
            dov = do_ref[qq, rows, :]
            dob = dov.astype(BF16)
            coef = dlse_ref[qq, rows, :] - jnp.sum(dov * o_ref[qq, rows, :], axis=1, keepdims=True)
            dsp = pp * (_dot(dob, v_ref[kq, prow, :], 1, 1) + coef)
            dsc = pc * (_dot(dob, v_ref[kq, rows, :], 1, 1) + coef)
            dspb, dscb = dsp.astype(BF16), dsc.astype(BF16)
            dq_ref[qq, rows, :] = ((_dot(dspb, k_ref[kq, prow, :], 1, 0) + _dot(dscb, k_ref[kq, rows, :], 1, 0))
                                   * (HEAD_DIM ** -0.5))
            dk_ref[kq, rows, :] += _dot(dscb, qv, 0, 0)
            dkp_ref[kq, prow, :] += _dot(dspb, qv, 0, 0)
            dv_ref[kq, rows, :] += _dot(pc.astype(BF16), dob, 0, 0)
            dvp_ref[kq, prow, :] += _dot(pp.astype(BF16), dob, 0, 0)
            db_ref[bq, :, :BLK] += dsp
            db_ref[bq, :, BLK:] += dsc
            if has_sink:
                dsk_ref[qq] += jnp.sum(jnp.exp(s_ref[qq][:, :1] - lse_v) * coef)

        _band_sweep(length, ns, one)
        dk_ref[...] += dkp_ref[...]
        dv_ref[...] += dvp_ref[...]

    xspecs, qspec, kvspec, bspec, sspec, colspec = _band_specs(length, g, bias_div, offs)
    return pl.pallas_call(
        body, name=name,
        out_shape=[jax.ShapeDtypeStruct((nq, length, HEAD_DIM), F32), jax.ShapeDtypeStruct((nk, length, HEAD_DIM), F32),
                   jax.ShapeDtypeStruct((nk, length, HEAD_DIM), F32), jax.ShapeDtypeStruct((nbias, BLK, 2 * BLK), F32),
                   jax.ShapeDtypeStruct((nq, 1, LANES), F32)],
        grid=(nq // ns,),
        in_specs=xspecs + [bspec, sspec, qspec, colspec, qspec, colspec],
        out_specs=[qspec, kvspec, kvspec, bspec, sspec],
        scratch_shapes=[pltpu.VMEM((ns, length, HEAD_DIM), F32), pltpu.VMEM((ns, length, HEAD_DIM), F32)],
        compiler_params=_params(("arbitrary",)),
    )(x, x, x, bias, sink, o, lse, do, dlse)


TOK_TILE = 512


def _dil_merge(outs, lses, dout, name):
    tr = TOK_TILE
    dils = [d for _, d in DIL_PATTERNS]
    n = len(dils)
    o4 = [o.reshape(2, d, SEQ // d, HEAD_DIM) for o, d in zip(outs, dils)]
    l4 = [l.reshape(2, d, SEQ // d, 1) for l, d in zip(lses, dils)]
    o_specs = [pl.BlockSpec((2, d, tr // d, HEAD_DIM), lambda i: (0, 0, i, 0)) for d in dils]
    l_specs = [pl.BlockSpec((2, d, tr // d, 1), lambda i: (0, 0, i, 0)) for d in dils]
    tok = pl.BlockSpec((tr, 2 * HEAD_DIM), lambda i: (i, 0))
    scratch = ([pltpu.VMEM((tr, 2 * HEAD_DIM), F32) for _ in dils] + [pltpu.VMEM((tr, 1), F32) for _ in range(2 * n)]
               + [pltpu.VMEM((tr // d, 2 * HEAD_DIM), F32) for d in dils])

    def to_tokens(o_ref, l_ref, d, pair, cols, stage):
        for r in range(d):
            rows = pl.ds(r, tr // d, stride=d) if d > 1 else slice(None)
            stage[:, :HEAD_DIM] = o_ref[0, r]
            stage[:, HEAD_DIM:] = o_ref[1, r]
            pair[rows, :] = stage[...]
            for h in range(2):
                cols[h][rows, :] = l_ref[h, r]
        return pair[...], [cols[0][...], cols[1][...]]

    def weights(ls):
        left = lax.broadcasted_iota(jnp.int32, (tr, 2 * HEAD_DIM), 1) < HEAD_DIM
        per_head = []
        for h in range(2):
            m = ls[0][h]
            for g in range(1, n):
                m = jnp.maximum(m, ls[g][h])
            es = [jnp.exp(ls[g][h] - m) for g in range(n)]
            den = es[0]
            for e in es[1:]:
                den = den + e
            per_head.append([e / den for e in es])
        return per_head, [jnp.where(left, per_head[0][g], per_head[1][g]) for g in range(n)], left

    def load(refs):
        pairs, cols, stages = refs[:n], refs[n:3 * n], refs[3 * n:]
        return pairs, [cols[2 * g:2 * g + 2] for g in range(n)], stages

    if dout is None:
        def body(*refs):
            pairs, cols, stages = load(refs[2 * n + 1:])
            toks = [to_tokens(refs[g], refs[n + g], dils[g], pairs[g], cols[g], stages[g]) for g in range(n)]
            _, alphas, _ = weights([t[1] for t in toks])
            acc = alphas[0] * toks[0][0]
            for g in range(1, n):
                acc = acc + alphas[g] * toks[g][0]
            refs[2 * n][...] = acc

        return pl.pallas_call(
            body, name=name, out_shape=jax.ShapeDtypeStruct((SEQ, 2 * HEAD_DIM), F32), grid=(SEQ // tr,),
            in_specs=o_specs + l_specs, out_specs=tok, scratch_shapes=scratch, compiler_params=_params(("parallel",)),
        )(*o4, *l4)

    def body(*refs):
        do_refs, dl_refs = refs[2 * n + 1:3 * n + 1], refs[3 * n + 1:4 * n + 1]
        pairs, cols, stages = load(refs[4 * n + 1:])
        toks = [to_tokens(refs[g], refs[n + g], dils[g], pairs[g], cols[g], stages[g]) for g in range(n)]
        per_head, alphas, left = weights([t[1] for t in toks])
        dov = refs[2 * n][...]
        das = []
        for g in range(n):
            prod = dov * toks[g][0]
            das.append([jnp.sum(jnp.where(left, prod, 0.0), axis=1, keepdims=True),
                        jnp.sum(jnp.where(left, 0.0, prod), axis=1, keepdims=True)])
        dbar = [sum(per_head[h][g] * das[g][h] for g in range(n)) for h in range(2)]
        for g, d in enumerate(dils):
            pairs[g][...] = alphas[g] * dov
            for h in range(2):
                cols[g][h][...] = per_head[h][g] * (das[g][h] - dbar[h])
            for r in range(d):
                rows = pl.ds(r, tr // d, stride=d) if d > 1 else slice(None)
                v = pairs[g][rows, :]
                for h in range(2):
                    do_refs[g][h, r] = v[:, h * HEAD_DIM:(h + 1) * HEAD_DIM]
                    dl_refs[g][h, r] = cols[g][h][rows, :]

    out = pl.pallas_call(
        body, name=name,
        out_shape=[jax.ShapeDtypeStruct(o.shape, F32) for o in o4] + [jax.ShapeDtypeStruct(l.shape, F32) for l in l4],
        grid=(SEQ // tr,), in_specs=o_specs + l_specs + [tok], out_specs=o_specs + l_specs, scratch_shapes=scratch,
        compiler_params=_params(("parallel",)),
    )(*o4, *l4, dout)
    return [t.reshape(s.shape) for t, s in zip(out, list(outs) + list(lses))]


def _tri(cmp):
    r = lax.broadcasted_iota(jnp.int32, (SB_TILE, SB_TILE), 0)
    c = lax.broadcasted_iota(jnp.int32, (SB_TILE, SB_TILE), 1)
    return cmp(r, c).astype(BF16)


def _cum(x, tri, terms):
    acc, rest = None, x
    for _ in range(terms):
        part = rest.astype(BF16)
        rest = rest - part.astype(F32)
        d = _dot(part, tri, 1, 0)
        acc = d if acc is None else acc + d
    return acc


def _sb_logits(q, ks, diagonal):
    t = SB_TILE
    z = _dot(q, ks, 1, 1)
    e = jnp.exp(-jnp.abs(z))
    lf = -(jnp.maximum(z, 0.0) + jnp.log(1.0 + e))
    if not diagonal:
        return z, e, lf, None
    mask = lax.broadcasted_iota(jnp.int32, (t, t), 1) < lax.broadcasted_iota(jnp.int32, (t, t), 0)
    return z, e, jnp.where(mask, lf, 0.0), mask


def _sb_specs(h, s):
    t = SB_TILE
    tile = pl.BlockSpec((h, t, HEAD_DIM), lambda i: (0, i, 0))
    keys = pl.BlockSpec((h, s, HEAD_DIM), lambda i: (1, 0, 0))
    values = pl.BlockSpec((h, s, HEAD_DIM), lambda i: (2, 0, 0))
    return tile, keys, values, pl.BlockSpec((h, t, 1), lambda i: (0, i, 0))


def _sb_fwd(x, name):
    h, s = x.shape[0] // 3, x.shape[1]
    t = SB_TILE

    def body(q_ref, k_ref, v_ref, o_ref, tot_ref):
        i = pl.program_id(0)
        after = _tri(lambda r, c: r > c)

        def tile(j, carry, diagonal):
            rows = pl.ds(pl.multiple_of(j * t, t), t)
            out = []
            for hh, (right, acc) in enumerate(carry):
                z, _, lf, mask = _sb_logits(q_ref[hh], k_ref[hh, rows, :], diagonal)
                w = jnp.exp(z + lf + (right + _cum(lf, after, 2)))
                w = w if mask is None else jnp.where(mask, w, 0.0)
                out.append((right + jnp.sum(lf, axis=1, keepdims=True), acc + _dot(w.astype(BF16), v_ref[hh, rows, :], 1, 0)))
            return tuple(out)

        carry = tile(i, tuple((jnp.zeros((t, 1), F32), jnp.zeros((t, HEAD_DIM), F32)) for _ in range(h)), True)
        carry = lax.fori_loop(0, i, lambda jj, c: tile(i - 1 - jj, c, False), carry)
        for hh, (right, acc) in enumerate(carry):
            o_ref[hh] = acc
            tot_ref[hh] = right

    tile_spec, keys, values, col = _sb_specs(h, s)
    return pl.pallas_call(
        body, name=name, out_shape=[jax.ShapeDtypeStruct((h, s, HEAD_DIM), F32), jax.ShapeDtypeStruct((h, s, 1), F32)],
        grid=(s // t,), in_specs=[tile_spec, keys, values], out_specs=[tile_spec, col],
        compiler_params=_params(("parallel",)),
    )(x, x, x)


def _sb_bwd(x, tot, do, name):
    h, s = x.shape[0] // 3, x.shape[1]
    t = SB_TILE

    def body(q_ref, k_ref, v_ref, tot_ref, do_ref, dq_ref, dk_ref, dv_ref):
        i = pl.program_id(0)

        @pl.when(i == 0)
        def _():
            dk_ref[...] = jnp.zeros_like(dk_ref)
            dv_ref[...] = jnp.zeros_like(dv_ref)

        upto = _tri(lambda r, c: r <= c)
        before = _tri(lambda r, c: r < c)

        def tile(j, carry, diagonal):
            rows = pl.ds(pl.multiple_of(j * t, t), t)
            out = []
            for hh, (left, cleft, dq) in enumerate(carry):
                qv, ks, dob = q_ref[hh], k_ref[hh, rows, :], do_ref[hh].astype(BF16)
                z, e, lf, mask = _sb_logits(qv, ks, diagonal)
                between = tot_ref[hh] - (left + _cum(lf, upto, 2))
                w = jnp.exp(z + lf + between)
                w = w if mask is None else jnp.where(mask, w, 0.0)
                dlog = w * _dot(dob, v_ref[hh, rows, :], 1, 1)
                cfail = cleft + _cum(dlog, before, 2)
                sig = jnp.where(z >= 0.0, 1.0, e) / (1.0 + e)
                dz = dlog * (1.0 - sig) - sig * cfail
                dz = (dz if mask is None else jnp.where(mask, dz, 0.0)).astype(BF16)
                dk_ref[hh, rows, :] += _dot(dz, qv, 0, 0)
                dv_ref[hh, rows, :] += _dot(w.astype(BF16), dob, 0, 0)
                out.append((left + jnp.sum(lf, axis=1, keepdims=True), cleft + jnp.sum(dlog, axis=1, keepdims=True),
                            dq + _dot(dz, ks, 1, 0)))
            return tuple(out)

        zero = jnp.zeros((t, 1), F32)
        carry = lax.fori_loop(0, i, lambda j, c: tile(j, c, False),
                              tuple((zero, zero, jnp.zeros((t, HEAD_DIM), F32)) for _ in range(h)))
        for hh, (_, _, dq) in enumerate(tile(i, carry, True)):
            dq_ref[hh] = dq * (HEAD_DIM ** -0.5)

    tile_spec, keys, values, col = _sb_specs(h, s)
    full = pl.BlockSpec((h, s, HEAD_DIM), lambda i: (0, 0, 0))
    shp = jax.ShapeDtypeStruct((h, s, HEAD_DIM), F32)
    return pl.pallas_call(
        body, name=name, out_shape=[shp, shp, shp], grid=(s // t,),
        in_specs=[tile_spec, keys, values, col, tile_spec],
        out_specs=[tile_spec, full, full], compiler_params=_params(("arbitrary",)),
    )(x, x, x, tot, do)


COL_SB, COL_DIL, COL_SWA = 0, 3 * H_SB * HEAD_DIM, 3 * H_SB * HEAD_DIM + 3 * H_DIL * HEAD_DIM
N_SWA = H_SWA_Q + 2 * H_SWA_KV


def _dil_col(t, g):
    return COL_DIL + t * H_DIL * HEAD_DIM + g * 2 * HEAD_DIM


def _split_heads(qkv, name):
    tr = TOK_TILE
    scale = HEAD_DIM ** -0.5
    dils = [d for _, d in DIL_PATTERNS]

    def body(x_ref, sb_ref, d0_ref, d1_ref, d2_ref, swa_ref, pair):
        def head(col, scaled):
            v = x_ref[:, col:col + HEAD_DIM]
            return (v * scale if scaled else v).astype(BF16)

        for hh in range(3 * H_SB):
            sb_ref[hh] = head(COL_SB + hh * HEAD_DIM, hh < H_SB)
        for hh in range(N_SWA):
            swa_ref[hh] = head(COL_SWA + hh * HEAD_DIM, hh < H_SWA_Q)
        for t in range(3):
            for g, (d, out_ref) in enumerate(zip(dils, (d0_ref, d1_ref, d2_ref))):
                col = _dil_col(t, g)
                if d == 1:
                    for h in range(2):
                        out_ref[t * 2 + h] = head(col + h * HEAD_DIM, t == 0)
                    continue
                pair[...] = x_ref[:, col:col + 2 * HEAD_DIM]
                for r in range(d):
                    v = pair[pl.ds(r, tr // d, stride=d), :]
                    v = v * scale if t == 0 else v
                    for h in range(2):
                        out_ref[t * 2 * d + h * d + r] = v[:, h * HEAD_DIM:(h + 1) * HEAD_DIM].astype(BF16)

    def heads(n, length):
        return jax.ShapeDtypeStruct((n, length, HEAD_DIM), BF16)

    def spec(n, rows):
        return pl.BlockSpec((n, rows, HEAD_DIM), lambda i: (0, i, 0))

    return pl.pallas_call(
        body, name=name,
        out_shape=[heads(3 * H_SB, SEQ)] + [heads(6 * d, SEQ // d) for d in dils] + [heads(N_SWA, SEQ)],
        grid=(SEQ // tr,), in_specs=[pl.BlockSpec((tr, D_QKV), lambda i: (i, 0))],
        out_specs=[spec(3 * H_SB, tr)] + [spec(6 * d, tr // d) for d in dils] + [spec(N_SWA, tr)],
        scratch_shapes=[pltpu.VMEM((tr, 2 * HEAD_DIM), F32)], compiler_params=_params(("parallel",)),
    )(qkv)


def _join_heads(sb, dil, swa, name):
    tr = TOK_TILE
    dils = [d for _, d in DIL_PATTERNS]

    def body(*refs):
        sb_refs, dil_refs, swa_refs = refs[:3], [refs[3 + 3 * g:6 + 3 * g] for g in range(3)], refs[12:15]
        o_ref, pair, stages = refs[15], refs[16], refs[17:]

        def put(col, v):
            o_ref[:, col:col + v.shape[1]] = v.astype(BF16)

        for t in range(3):
            for h in range(H_SB):
                put(COL_SB + (t * H_SB + h) * HEAD_DIM, sb_refs[t][h])
        col = COL_SWA
        for ref in swa_refs:
            for h in range(ref.shape[0]):
                put(col, ref[h])
                col += HEAD_DIM
        for t in range(3):
            for g, d in enumerate(dils):
                ref, col = dil_refs[g][t], _dil_col(t, g)
                if d == 1:
                    for h in range(2):
                        put(col + h * HEAD_DIM, ref[h])
                    continue
                stage = stages[g - 1]
                for r in range(d):
                    stage[:, :HEAD_DIM] = ref[r]
                    stage[:, HEAD_DIM:] = ref[d + r]
                    pair[pl.ds(r, tr // d, stride=d), :] = stage[...]
                put(col, pair[...])

    def spec(n, rows):
        return pl.BlockSpec((n, rows, HEAD_DIM), lambda i: (0, i, 0))

    ins = list(sb) + [t for g in range(3) for t in dil[g]] + list(swa)
    in_specs = ([spec(H_SB, tr)] * 3 + [spec(2 * d, tr // d) for d in dils for _ in range(3)]
                + [spec(H_SWA_Q, tr), spec(H_SWA_KV, tr), spec(H_SWA_KV, tr)])
    return pl.pallas_call(
        body, name=name, out_shape=jax.ShapeDtypeStruct((SEQ, D_QKV), BF16), grid=(SEQ // tr,), in_specs=in_specs,
        out_specs=pl.BlockSpec((tr, D_QKV), lambda i: (i, 0)),
        scratch_shapes=[pltpu.VMEM((tr, 2 * HEAD_DIM), F32)] + [pltpu.VMEM((tr // d, 2 * HEAD_DIM), F32) for d in dils[1:]],
        compiler_params=_params(("parallel",)),
    )(*ins)


def _mixer_fwd(qkv, bias, sinks_l, tag):
    sb, d0, d1, d2, swa = _split_heads(qkv, name=f"split_heads_{tag}")
    st = {"sb": sb, "dil": (d0, d1, d2), "swa": swa}
    o_sb, st["sb_tot"] = _sb_fwd(sb, name=f"sb_fwd_{tag}")
    st["dil_out"], st["dil_lse"], st["dil_sink"] = [], [], []
    for gi, (_, d) in enumerate(DIL_PATTERNS):
        sink = jnp.zeros((2 * d, 1, LANES), F32)
        og, lg = _band_fwd(st["dil"][gi], bias[2 * gi:2 * gi + 2], sink, nq=2 * d, offs=(0, 2 * d, 4 * d), g=1, bias_div=d,
                           has_sink=False, name=f"dil{gi}_fwd_{tag}")
        st["dil_out"].append(og)
        st["dil_lse"].append(lg)
        st["dil_sink"].append(sink)
    o_dil = _dil_merge(st["dil_out"], st["dil_lse"], None, name=f"dil_merge_fwd_{tag}")
    st["swa_sink"] = jnp.broadcast_to(sinks_l.reshape(H_SWA_Q, 1, 1), (H_SWA_Q, 1, LANES))
    st["swa_out"] = _band_fwd(swa, bias[H_DIL:], st["swa_sink"], nq=H_SWA_Q, offs=(0, H_SWA_Q, H_SWA_Q + H_SWA_KV),
                              g=H_SWA_Q // H_SWA_KV, bias_div=1, has_sink=True, name=f"swa_fwd_{tag}")
    return (o_sb, o_dil, st["swa_out"][0]), st


def _mixer_bwd(st, bias, do_sb, do_dil, do_swa, tag):
    d_sb = _sb_bwd(st["sb"], st["sb_tot"], do_sb, name=f"sb_bwd_{tag}")
    dmerge = _dil_merge(st["dil_out"], st["dil_lse"], do_dil, name=f"dil_merge_bwd_{tag}")
    d_dil, dbs = [], []
    for gi, (_, d) in enumerate(DIL_PATTERNS):
        dq, dk, dv, db, _ = _band_bwd(st["dil"][gi], bias[2 * gi:2 * gi + 2], st["dil_sink"][gi], st["dil_out"][gi],
                                      st["dil_lse"][gi], dmerge[gi], dmerge[3 + gi], nq=2 * d, offs=(0, 2 * d, 4 * d),
                                      g=1, bias_div=d, has_sink=False, name=f"dil{gi}_bwd_{tag}")
        d_dil.append((dq, dk, dv))
        dbs.append(db)
    o_sw, l_sw = st["swa_out"]
    dq_sw, dk_sw, dv_sw, db_sw, dsink = _band_bwd(st["swa"], bias[H_DIL:], st["swa_sink"], o_sw, l_sw, do_swa,
                                                  jnp.zeros_like(l_sw), nq=H_SWA_Q, offs=(0, H_SWA_Q, H_SWA_Q + H_SWA_KV),
                                                  g=H_SWA_Q // H_SWA_KV, bias_div=1, has_sink=True, name=f"swa_bwd_{tag}")
    dqkv = _join_heads(d_sb, d_dil, (dq_sw, dk_sw, dv_sw), name=f"join_heads_{tag}")
    return dqkv, jnp.concatenate(dbs + [db_sw], 0), dsink[:, 0, 0]


PIECES = ("ffn0", "mix", "ffn1")


def _ffn_fwd(x_in, w, gain, mod_j, tag, after=None):
    st = {"x": x_in, "w": w}
    st["h"] = _norm_fwd(x_in, _row(gain), _row(mod_j[1]), _row(mod_j[0]), name=f"norm_fwd_{tag}", after=after)
    st["a"], st["u"], st["s"] = _ffn_up(st["h"], w["gate"], w["up"], name=f"up_{tag}")
    st["f"], x_out = _mm(st["s"], w["down"], res=x_in, colscale=_row(0.5 * mod_j[2]), emit_acc=True, tm=512, tn=1024,
                         name=f"down_{tag}")
    return x_out, st


def _ffn_bwd(dx_out, st, gain, mod_j, tag, done):
    w = st["w"]

    def latest(new, old):
        return old if new is None else new

    df, dgate = _gate_bwd(dx_out, st["f"], _row(0.5 * mod_j[2]), 0.5, name=f"gate_bwd_{tag}")
    token = done({"down": _mm_tn(st["s"], df, tm=D_FF // 2, name=f"dwd_{tag}")})
    da, du = _ffn_bwd_ds(df, w["down"], st["a"], st["u"], name=f"ds_{tag}")
    token = latest(done({"gate": _mm_tn(da, st["h"], after=token, tm=D_FF // 2, name=f"dwg_{tag}")}), token)
    token = latest(done({"up": _mm_tn(du, st["h"], after=token, tm=D_FF // 2, name=f"dwu_{tag}")}), token)
    dh = _mm2(da, w["gate"], du, w["up"], after=token, name=f"dh_{tag}")
    dx_in, sum_dh, sum_dhx = _norm_bwd(st["x"], dh, dx_out, _row(gain), _row(mod_j[1]), name=f"norm_bwd_{tag}")
    dmod = jnp.concatenate([sum_dh, gain * sum_dhx, dgate], 0)
    return dx_in, dmod, (1.0 + mod_j[1]) * sum_dhx[0]


def _mix_fwd(x_in, w, gain, mod_j, bias, sinks_l, tag, after=None):
    st = {"x": x_in, "w": w}
    st["h"] = _norm_fwd(x_in, _row(gain), _row(mod_j[1]), _row(mod_j[0]), name=f"norm_fwd_mix_{tag}", after=after)
    qkv = _mm(st["h"], w["in"], tb=True, tm=SEQ, b_rows=(0, D_QKV), name=f"qkv_{tag}")
    st["gates"] = _mm(st["h"], w["in"], tb=True, tm=SEQ, b_rows=(D_QKV, D_GATES), name=f"gates_{tag}")
    outs, st["mix"] = _mixer_fwd(qkv, bias, sinks_l, tag)
    st["merged"], *st["t"] = _merge_fwd(*outs, st["gates"], w["br_sb"], w["br_dil"], w["br_swa"], name=f"merge_fwd_{tag}")
    st["f"], x_out = _mm(st["merged"], w["out"], res=x_in, colscale=_row(mod_j[2]), emit_acc=True, name=f"out_{tag}")
    return x_out, st


def _mix_bwd(dx_out, st, gain, mod_j, bias, tag, done):
    w = st["w"]
    df, dgate = _gate_bwd(dx_out, st["f"], _row(mod_j[2]), 1.0, name=f"gate_bwd_mix_{tag}")
    g = {"out": _mm_tn(st["merged"], df, name=f"dw_out_{tag}")}
    dmerged = _mm(df, w["out"], tb=True, name=f"dmerged_{tag}")
    dgates, do_sb, do_dil, do_swa, dbr_sb, dbr_dil, dbr_swa = _merge_bwd(
        dmerged, *st["t"], st["gates"], w["br_sb"], w["br_dil"], w["br_swa"], name=f"merge_bwd_{tag}")
    g["br_sb"] = _mm_tn(st["t"][0], dbr_sb, name=f"dw_br_sb_{tag}")
    g["br_dil"] = _mm_tn(st["t"][1], dbr_dil, name=f"dw_br_dil_{tag}")
    g["br_swa"] = _mm_tn(st["t"][2], dbr_swa, name=f"dw_br_swa_{tag}")
    dqkv, dbias, dsinks = _mixer_bwd(st["mix"], bias, do_sb, do_dil, do_swa, tag)
    dw_qkv = _mm_tn(dqkv, st["h"], out_rows=D_QKV + D_GATES, name=f"dw_qkv_{tag}")
    g["in"] = _mm_tn(dgates, st["h"], out_rows=D_QKV + D_GATES, row0=D_QKV, prev=dw_qkv, name=f"dw_gates_{tag}")
    dh = _mm2(dqkv, w["in"], dgates, w["in"], after=done(g), tm=512, b_rows=(0, D_QKV), name=f"dh_mix_{tag}")
    dx_in, sum_dh, sum_dhx = _norm_bwd(st["x"], dh, dx_out, _row(gain), _row(mod_j[1]), name=f"norm_bwd_mix_{tag}")
    dmod = jnp.concatenate([sum_dh, gain * sum_dhx, dgate], 0)
    return dx_in, dmod, (1.0 + mod_j[1]) * sum_dhx[0], dbias, dsinks


def _local_step(x, target, mod, gains, weights_of, rel_bias, sinks, final_gain, grads_done):
    tables = jnp.asarray(_bucket_tables())
    bias = _bias_build(rel_bias, tables, name="bias_build")
    states, h = [], x
    for l in range(DEPTH):
        st = {}
        for j, piece in enumerate(PIECES):
            w, after = weights_of(l, piece, h)
            if piece == "mix":
                h, st[piece] = _mix_fwd(h, w, gains[l, j], mod[l, j], bias, sinks[l], f"l{l}", after)
            else:
                h, st[piece] = _ffn_fwd(h, w, gains[l, j], mod[l, j], f"{piece}_l{l}", after)
        states.append(st)
    loss, dx, dfinal = _final_loss(h, target, _row(final_gain), name="final_loss")
    dmods = [[None] * 3 for _ in range(DEPTH)]
    dgains = [[None] * 3 for _ in range(DEPTH)]
    dsinks = [None] * DEPTH
    dbias = None
    for l in reversed(range(DEPTH)):
        for j in reversed(range(3)):
            piece = PIECES[j]
            done = lambda grads, l=l, piece=piece: grads_done(l, piece, grads)
            if piece == "mix":
                dx, dmods[l][j], dgains[l][j], db, dsinks[l] = _mix_bwd(dx, states[l][piece], gains[l, j], mod[l, j], bias, f"l{l}", done)
                dbias = db if dbias is None else dbias + db
            else:
                dx, dmods[l][j], dgains[l][j] = _ffn_bwd(dx, states[l][piece], gains[l, j], mod[l, j], f"{piece}_l{l}", done)
    drel = _bias_grad(dbias, tables, name="bias_grad")[:, 0, :N_BUCKETS].T
    dmod = jnp.stack([jnp.stack(m) for m in dmods])
    dgain = jnp.stack([jnp.stack(g) for g in dgains])
    return loss, dx, dmod, dgain, dfinal[0], drel, jnp.stack(dsinks)


BR_ROWS = (H_SB * HEAD_DIM, 2 * HEAD_DIM, H_SWA_Q * HEAD_DIM)


def _lanes_unshard(g, lead):
    _, rows, _ = g.shape
    r = rows // lead
    return g.reshape(N_DEV, lead, r, LANES).transpose(1, 2, 0, 3).reshape(lead, r, N_DEV * LANES)


def _lanes_shard(full):
    lead, r, _ = full.shape
    return full.reshape(lead, r, N_DEV, LANES).transpose(2, 0, 1, 3).reshape(N_DEV, lead * r, LANES)


def _pack_rows(parts, dtype):
    flat = jnp.concatenate([p.astype(dtype).reshape(-1) for p in parts])
    pad = (-flat.shape[0]) % (16 * LANES)
    if pad:
        flat = jnp.concatenate([flat, jnp.zeros((pad,), dtype)])
    return flat.reshape(-1, LANES)


def _unshard(gathered, axis):
    moved = jnp.moveaxis(gathered, 0, axis)
    shape = list(moved.shape)
    shape[axis:axis + 2] = [shape[axis] * shape[axis + 1]]
    return moved.reshape(shape)


def kernel(x, c, w_ada, b_ada, norm_gain, w_ffn_gate, w_ffn_up, w_ffn_down, w_in, w_br_sb, w_br_dil, w_br_swa, w_out, sinks, rel_bias, final_gain, loss_target, m_w_ada, m_b_ada, m_norm_gain, m_w_ffn_gate, m_w_ffn_up, m_w_ffn_down, m_w_in, m_w_br_sb, m_w_br_dil, m_w_br_swa, m_w_out, m_sinks, m_rel_bias, m_final_gain, v_w_ada, v_b_ada, v_norm_gain, v_w_ffn_gate, v_w_ffn_up, v_w_ffn_down, v_w_in, v_w_br_sb, v_w_br_dil, v_w_br_swa, v_w_out, v_sinks, v_rel_bias, v_final_gain):
    me = 4 * lax.axis_index("x") + 2 * lax.axis_index("y") + lax.axis_index("c")
    d = D_MODEL
    gate_t, up_t, in_t = jnp.swapaxes(w_ffn_gate, 2, 3), jnp.swapaxes(w_ffn_up, 2, 3), jnp.swapaxes(w_in, 1, 2)

    def piece_shards(l, piece):
        bf = lambda t: t.astype(BF16)
        if piece == "mix":
            return [bf(in_t[l]), jnp.concatenate([bf(w_br_sb[l]), bf(w_br_dil[l]), bf(w_br_swa[l])], 0), bf(w_out[l])]
        i = PIECES.index(piece) // 2
        return [bf(gate_t[l, i]), bf(up_t[l, i]), bf(w_ffn_down[l, i])]

    br_off = np.concatenate([[0], np.cumsum(BR_ROWS)])

    def piece_weights(gathered, piece):
        if piece == "mix":
            g_in, g_br, g_out = gathered
            f_br = [_lanes_unshard(g_br[:, br_off[k]:br_off[k + 1]], 1)[0] for k in range(3)]
            return {"in": g_in.reshape(D_QKV + D_GATES, d), "br_sb": f_br[0], "br_dil": f_br[1], "br_swa": f_br[2],
                    "out": g_out.reshape(d, d)}
        return {n: g.reshape(D_FF, d) for n, g in zip(("gate", "up", "down"), gathered)}

    small, = _all_gather([_pack_rows([c, norm_gain], F32)], name="gather_cond")
    c_all = small[:, :d // LANES].reshape(N_DEV, d)
    gains = _unshard(small[:, d // LANES:d // LANES + 6].reshape(N_DEV, DEPTH, 3, LANES), 2)

    cols = w_ada.shape[2]
    mod_cols = jnp.stack([_ada_fwd(c_all, w_ada[l], name=f"ada_fwd_l{l}") for l in range(DEPTH)])
    mod_all, = _all_gather([_pack_rows([mod_cols], F32)], name="gather_mod")
    mod_all = mod_all.reshape(N_DEV, -1)[:, :DEPTH * N_DEV * cols].reshape(N_DEV, DEPTH, N_DEV, cols)
    mod_mine = lax.dynamic_index_in_dim(mod_all, me, axis=2, keepdims=False)
    mod = (mod_mine.transpose(1, 0, 2).reshape(DEPTH, N_DEV * cols) + b_ada).reshape(DEPTH, 3, 3, d)

    order = [(l, piece) for l in range(DEPTH) for piece in PIECES]
    ahead = 3
    in_flight = {}
    first = _all_gather(piece_shards(*order[0]), after=mod_all, name="gather_first")

    def start_gather(k, after):
        l, piece = order[k]
        in_flight[k], token = _relay_start(piece_shards(l, piece), after, name=f"gather_{piece}_l{l}_start")
        return token

    token = first[0]
    for k in range(1, 1 + ahead):
        token = start_gather(k, token)
    mod = mod + token[0, 0]

    def weights_of(l, piece, h):
        k = order.index((l, piece))
        token = start_gather(k + ahead, h) if k + ahead < len(order) and k + ahead not in in_flight else None
        if k == 0:
            return piece_weights(first, piece), token
        passed, token_b = _relay_pass(in_flight[k], h if token is None else token, name=f"gather_{piece}_l{l}_pass")
        return piece_weights(_relay_wait(passed, token_b, name=f"gather_{piece}_l{l}_wait"), piece), token

    exchanges, have = {}, {}

    def grads_done(l, piece, g):
        key = (l, piece)
        have.setdefault(key, {}).update(g)
        if piece == "mix":
            if len(have[key]) < 5:
                return None
            g = have[key]
            s_br = jnp.concatenate([_lanes_shard(g[n][None]) for n in ("br_sb", "br_dil", "br_swa")], 1)
            groups = [(("in", "br", "out"), [g["in"].reshape(N_DEV, -1, d), s_br, g["out"].reshape(N_DEV, -1, d)])]
        elif key == order[0]:
            groups = [((n,), [t.reshape(N_DEV, -1, d)]) for n, t in g.items()]
        elif len(have[key]) < 3:
            return None
        else:
            groups = [(("gate", "up", "down"), [have[key][n].reshape(N_DEV, -1, d) for n in ("gate", "up", "down")])]
        token = None
        for names, sg in groups:
            state, token = _exchange_start(sg, None, gather=False, name=f"exchange_{piece}_l{l}_{names[0]}_start")
            exchanges.setdefault(key, []).append((names, state))
        return token

    loss, dx, dmod, dgains, dfinal, drel, dsinks = _local_step(
        x[0], loss_target[0], mod, gains, weights_of, rel_bias, sinks, final_gain, grads_done)

    flat = lambda t: t.reshape(-1, t.shape[-1])
    transposed = lambda ts: tuple(flat(jnp.swapaxes(t, -1, -2)) for t in ts)
    families = {
        "gate": transposed((w_ffn_gate, m_w_ffn_gate, v_w_ffn_gate)), "up": transposed((w_ffn_up, m_w_ffn_up, v_w_ffn_up)),
        "down": tuple(flat(t) for t in (w_ffn_down, m_w_ffn_down, v_w_ffn_down)),
        "in": transposed((w_in, m_w_in, v_w_in)),
        "br": tuple(flat(jnp.concatenate(ts, 1)) for ts in ((w_br_sb, w_br_dil, w_br_swa), (m_w_br_sb, m_w_br_dil, m_w_br_swa),
                                                            (v_w_br_sb, v_w_br_dil, v_w_br_swa))),
        "out": tuple(flat(t) for t in (w_out, m_w_out, v_w_out))}
    parts, stepped = {}, {}

    def land(l, after):
        for key in reversed([k for k in order if k[0] == l]):
            for names, ex_state in exchanges[key]:
                landed = _exchange_wait(ex_state, after, gather=False, name=f"exchange_{key[1]}_l{key[0]}_{names[0]}_wait")
                parts.setdefault(key, {}).update(zip(names, landed))
                after = landed[0]

    def step_layer(l):
        last = None
        for n, (w2, m2, v2) in families.items():
            groups = [parts[key][n] for key in order if key[0] == l and n in parts[key]]
            rows_per_layer = w2.shape[0] // DEPTH
            stepped[n] = _reduce_adamw(groups, w2, m2, v2, l * rows_per_layer, stepped.get(n), after=last,
                                       name=f"reduce_adamw_{n}_l{l}")
            last = stepped[n][1]
        return last

    land(1, dx)
    after_l1 = step_layer(1)

    small_parts = [dmod, dgains, dfinal, drel.T, dsinks, loss[0, :1]]
    small_sizes = [int(np.prod(p.shape)) for p in small_parts]
    small_all, = _all_gather([_pack_rows(small_parts, F32)], after=after_l1, name="gather_small")
    small_sum = _sum_parts([small_all], name="sum_small").reshape(-1)
    offs = np.concatenate([[0], np.cumsum(small_sizes)])
    g_b_ada = small_sum[offs[0]:offs[1]].reshape(DEPTH, 9 * d)
    g_gain_full = small_sum[offs[1]:offs[2]].reshape(DEPTH, 3, d)
    g_norm_gain = lax.dynamic_slice_in_dim(g_gain_full, me * LANES, LANES, axis=2)
    g_final = small_sum[offs[2]:offs[3]]
    g_rel = small_sum[offs[3]:offs[4]].reshape(N_SOFT, N_BUCKETS).T
    g_sinks = small_sum[offs[4]:offs[5]].reshape(DEPTH, H_SWA_Q)
    loss_total = small_sum[offs[5]]

    dmod_all = small_all.reshape(N_DEV, -1)[:, :DEPTH * 9 * d].reshape(N_DEV, DEPTH, 9 * d)
    dmod_cols = lax.dynamic_slice_in_dim(dmod_all, me * cols, cols, axis=2)
    g_w_ada = jnp.stack([_ada_bwd(c_all.T, dmod_cols[:, l], name=f"ada_bwd_l{l}") for l in range(DEPTH)])

    small_state = {"w_ada": (w_ada, m_w_ada, v_w_ada), "b_ada": (b_ada, m_b_ada, v_b_ada),
                   "norm_gain": (norm_gain, m_norm_gain, v_norm_gain), "sinks": (sinks, m_sinks, v_sinks),
                   "rel_bias": (rel_bias, m_rel_bias, v_rel_bias), "final_gain": (final_gain, m_final_gain, v_final_gain)}
    grad, update = {}, {}
    for n, g in (("w_ada", g_w_ada), ("b_ada", g_b_ada), ("norm_gain", g_norm_gain), ("sinks", g_sinks),
                 ("rel_bias", g_rel), ("final_gain", g_final)):
        w, m, v = small_state[n]
        grad[n] = g
        if w.ndim == 1:
            update[n] = tuple(t.reshape(w.shape) for t in _adamw(_row(w), _row(g), _row(m), _row(v), name=f"adamw_{n}"))
        else:
            update[n] = _adamw(w, g, m, v, name=f"adamw_{n}")

    land(0, update["w_ada"][0])
    step_layer(0)

    def unflat(n, like, swapped):
        shape = jnp.swapaxes(like, -1, -2).shape if swapped else like.shape
        out = [t.reshape(shape) for t in stepped[n]]
        return [jnp.swapaxes(t, -1, -2) for t in out] if swapped else out

    results = {"w_ffn_gate": unflat("gate", w_ffn_gate, True), "w_ffn_up": unflat("up", w_ffn_up, True),
               "w_ffn_down": unflat("down", w_ffn_down, False), "w_in": unflat("in", w_in, True),
               "w_out": unflat("out", w_out, False)}
    br = [t.reshape(DEPTH, -1, LANES) for t in stepped["br"]]
    for k, n in enumerate(("w_br_sb", "w_br_dil", "w_br_swa")):
        results[n] = [t[:, br_off[k]:br_off[k + 1]] for t in br]
    for n, (g, dl, nm, nv) in results.items():
        grad[n], update[n] = g, (dl, nm, nv)

    names = ["w_ada", "b_ada", "norm_gain", "w_ffn_gate", "w_ffn_up", "w_ffn_down", "w_in", "w_br_sb", "w_br_dil",
             "w_br_swa", "w_out", "sinks", "rel_bias", "final_gain"]
    return (loss_total, dx[None], *[grad[n] for n in names], *[update[n][0] for n in names],
            *[update[n][1] for n in names], *[update[n][2] for n in names])
```

```python
import math

import numpy as np
import jax
import jax.numpy as jnp
from jax import lax
from jax.experimental import pallas as pl
from jax.experimental.pallas import tpu as pltpu

F32, BF16 = jnp.float32, jnp.bfloat16

SEQ, D_MODEL, D_FF, HEAD_DIM = 2048, 1024, 2816, 64
DEPTH = 2
BLK = 128
H_SB, H_DIL, H_SWA_Q, H_SWA_KV = 4, 6, 6, 2
DIL_PATTERNS = ((128, 1), (512, 4), (2048, 16))
SWA_WINDOW = 128
N_BUCKETS, MAX_REL_DIST = 32, 2048
RMS_EPS = 1e-6
D_QKV = 2560
D_GATES = 3 * D_MODEL
ADAM_LR, ADAM_B1, ADAM_B2, ADAM_EPS, ADAM_WD, ADAM_STEP = 0.001, 0.9, 0.999, 1e-08, 0.01, 10

N_DEV = 8
LANES = 128
NEG = -1e30
SB_TILE = 512
VMEM_LIMIT_BYTES = 48 * 1024 * 1024
HBM = pl.BlockSpec(memory_space=pltpu.HBM)
MESH = pl.DeviceIdType.MESH


def _tile(n, target):
    t = (min(n, target) // LANES) * LANES
    while t >= LANES:
        if n % t == 0:
            return t
        t -= LANES
    return n


def _row_tile(r, cap):
    t = (min(r, cap) // 16) * 16
    while t > 16 and r % t:
        t -= 16
    return t


def _params(semantics=None):
    return pltpu.CompilerParams(dimension_semantics=semantics, vmem_limit_bytes=VMEM_LIMIT_BYTES)


def _dot(a, b, ca, cb):
    return lax.dot_general(a, b, (((ca,), (cb,)), ((), ())), preferred_element_type=F32)


def _sigmoid(a):
    return 1.0 / (1.0 + jnp.exp(-a))


def _row(v):
    return v.reshape(1, -1)


def _all_gather(arrs, name, after=None):
    n = len(arrs)
    ins = list(arrs) + ([] if after is None else [after])

    def body(*refs):
        x_refs, out_refs = refs[:n], refs[len(ins):len(ins) + n]
        send_sems, recv_sems, local_sems = refs[len(ins) + n:]
        x, y, c = lax.axis_index("x"), lax.axis_index("y"), lax.axis_index("c")
        me, sibling = (x, y, c), (x, y, 1 - c)
        chips = [(1 - x, y), (x, 1 - y), (1 - x, 1 - y)]

        def slot(t, px, py, pc):
            return out_refs[t].at[4 * px + 2 * py + pc]

        def copy(t, k, block, to, src=None):
            return pltpu.make_async_remote_copy(
                src_ref=slot(t, *block) if src is None else src, dst_ref=slot(t, *block),
                send_sem=send_sems.at[7 * t + k], recv_sem=recv_sems.at[7 * t + k], device_id=to, device_id_type=MESH)

        mine = [pltpu.make_async_copy(x_refs[t], slot(t, *me), local_sems.at[t]) for t in range(n)]
        for cp in mine:
            cp.start()
        first = []
        for t in range(n):
            first.append(copy(t, 0, me, sibling, src=x_refs[t]))
            first += [copy(t, 1 + j, me, (*chip, c), src=x_refs[t]) for j, chip in enumerate(chips)]
        for cp in first:
            cp.start()
        passed = []
        for j, chip in enumerate(chips):
            for t in range(n):
                copy(t, 1 + j, (*chip, c), me).wait_recv()
                passed.append(copy(t, 4 + j, (*chip, c), sibling))
                passed[-1].start()
        for t in range(n):
            copy(t, 0, sibling, me).wait_recv()
        for j, chip in enumerate(chips):
            for t in range(n):
                copy(t, 4 + j, (*chip, 1 - c), me).wait_recv()
        for cp in first + passed:
            cp.wait_send()
        for cp in mine:
            cp.wait()

    return pl.pallas_call(
        body, name=name, out_shape=[jax.ShapeDtypeStruct((N_DEV,) + a.shape, a.dtype) for a in arrs],
        in_specs=[HBM] * n + [pl.BlockSpec(memory_space=pl.ANY)] * (len(ins) - n), out_specs=[HBM] * n,
        scratch_shapes=[pltpu.SemaphoreType.DMA((7 * n,)), pltpu.SemaphoreType.DMA((7 * n,)), pltpu.SemaphoreType.DMA((n,))],
    )(*ins)


def _direct_copies(x_refs, land_refs, send_sems, recv_sems, local_sems, gather):
    x, y, c = lax.axis_index("x"), lax.axis_index("y"), lax.axis_index("c")
    me = 4 * x + 2 * y + c
    sends, recvs = [], []
    for k in range(1, N_DEV):
        px = 1 - x if (k >> 2) & 1 else x
        py = 1 - y if (k >> 1) & 1 else y
        pc = 1 - c if k & 1 else c
        peer = 4 * px + 2 * py + pc
        for t, (x_ref, land_ref) in enumerate(zip(x_refs, land_refs)):
            sem = 7 * t + k - 1
            for out, src, slot in ((sends, x_ref if gather else x_ref.at[peer], me),
                                   (recvs, x_ref if gather else x_ref.at[me], peer)):
                out.append(pltpu.make_async_remote_copy(
                    src_ref=src, dst_ref=land_ref.at[slot], send_sem=send_sems.at[sem], recv_sem=recv_sems.at[sem],
                    device_id=(px, py, pc), device_id_type=MESH))
    own = [pltpu.make_async_copy(x_ref if gather else x_ref.at[me], land_ref.at[me], local_sems.at[t])
           for t, (x_ref, land_ref) in enumerate(zip(x_refs, land_refs))]
    return sends, recvs, own


SEM =pl.BlockSpec(memory_space=pltpu.SEMAPHORE)
ANY = pl.BlockSpec(memory_space=pl.ANY)
SIDE_EFFECT = pltpu.SideEffectType.DATAFLOW_SIDE_EFFECTING


def _exchange_start(arrs, after, *, gather, name):
    n = len(arrs)
    lands = [lax.empty(((N_DEV,) + a.shape) if gather else a.shape, a.dtype) for a in arrs]
    extra = [] if after is None else [after]

    def body(*refs):
        sems = refs[2 * n + len(extra):2 * n + len(extra) + 3]
        sends, _, own = _direct_copies(refs[:n], refs[n:2 * n], *sems, gather)
        for cp in own + sends:
            cp.start()
        refs[-1][...] = jnp.zeros_like(refs[-1])

    ops = [pltpu.with_memory_space_constraint(a, pltpu.HBM) for a in list(arrs) + lands]
    out = pl.pallas_call(
        body, name=name,
        out_shape=(pltpu.SemaphoreType.DMA((7 * n,)), pltpu.SemaphoreType.DMA((7 * n,)), pltpu.SemaphoreType.DMA((n,)),
                   *[pltpu.HBM(a.shape, a.dtype) for a in ops], jax.ShapeDtypeStruct((8, LANES), F32)),
        in_specs=[HBM] * (2 * n) + [ANY] * len(extra),
        out_specs=(SEM, SEM, SEM, *[HBM] * (2 * n), pl.BlockSpec(memory_space=pltpu.VMEM)),
        input_output_aliases={t: 3 + t for t in range(2 * n)},
        compiler_params=pltpu.CompilerParams(has_side_effects=SIDE_EFFECT),
    )(*ops, *extra)
    return (out[:3], out[3:3 + n], out[3 + n:3 + 2 * n]), out[-1]


def _exchange_wait(state, after, *, gather, name):
    sems, arrs, lands = state
    n = len(arrs)

    def body(*refs):
        sends, recvs, own = _direct_copies(refs[:n], refs[n:2 * n], *refs[2 * n:2 * n + 3], gather)
        for cp in own:
            cp.wait()
        for cp in sends:
            cp.wait_send()
        for cp in recvs:
            cp.wait_recv()

    out = pl.pallas_call(
        body, name=name, out_shape=tuple(pltpu.HBM(a.shape, a.dtype) for a in list(arrs) + list(lands)),
        in_specs=[HBM] * (2 * n) + [SEM, SEM, SEM, ANY], out_specs=tuple([HBM] * (2 * n)),
        input_output_aliases={t: t for t in range(2 * n)},
        compiler_params=pltpu.CompilerParams(has_side_effects=SIDE_EFFECT),
    )(*arrs, *lands, *sems, after)
    return out[n:]


def _relay_copies(x_refs, land_refs, sems_a, sems_b):
    x, y, c = lax.axis_index("x"), lax.axis_index("y"), lax.axis_index("c")
    me = 4 * x + 2 * y + c
    sibling = (x, y, 1 - c)
    chips = [(1 - x, y), (x, 1 - y), (1 - x, 1 - y)]

    def slot(px, py, pc):
        return 4 * px + 2 * py + pc

    def copy(src, land_ref, dst_slot, send_sems, recv_sems, k, to):
        return pltpu.make_async_remote_copy(src_ref=src, dst_ref=land_ref.at[dst_slot], send_sem=send_sems.at[k],
                                            recv_sem=recv_sems.at[k], device_id=to, device_id_type=MESH)

    a_send, a_recv, a_own, b_send, b_recv = [], [], [], [], []
    for t, (x_ref, land_ref) in enumerate(zip(x_refs, land_refs)):
        peers = [sibling] + [(*chip, c) for chip in chips]
        if sems_a is not None:
            for k, peer in enumerate(peers):
                a_send.append(copy(x_ref, land_ref, me, sems_a[0], sems_a[1], 4 * t + k, peer))
                a_recv.append(copy(x_ref, land_ref, slot(*peer), sems_a[0], sems_a[1], 4 * t + k, peer))
            a_own.append(pltpu.make_async_copy(x_ref, land_ref.at[me], sems_a[2].at[t]))
        if sems_b is not None:
            for j, chip in enumerate(chips):
                b_send.append(copy(land_ref.at[slot(*chip, c)], land_ref, slot(*chip, c), sems_b[0], sems_b[1], 3 * t + j, sibling))
                b_recv.append(copy(land_ref.at[slot(*chip, c)], land_ref, slot(*chip, 1 - c), sems_b[0], sems_b[1], 3 * t + j,
                                   sibling))
    return (a_send, a_recv, a_own), (b_send, b_recv)


def _relay_start(arrs, after, name):
    n = len(arrs)
    lands = [lax.empty((N_DEV,) + a.shape, a.dtype) for a in arrs]

    def body(*refs):
        (sends, _, own), _ = _relay_copies(refs[:n], refs[n:2 * n], refs[2 * n + 1:2 * n + 4], None)
        for cp in own + sends:
            cp.start()
        refs[-1][...] = jnp.zeros_like(refs[-1])

    ops = [pltpu.with_memory_space_constraint(a, pltpu.HBM) for a in list(arrs) + lands]
    out = pl.pallas_call(
        body, name=name,
        out_shape=(pltpu.SemaphoreType.DMA((4 * n,)), pltpu.SemaphoreType.DMA((4 * n,)), pltpu.SemaphoreType.DMA((n,)),
                   *[pltpu.HBM(a.shape, a.dtype) for a in ops], jax.ShapeDtypeStruct((8, LANES), F32)),
        in_specs=[HBM] * (2 * n) + [ANY],
        out_specs=(SEM, SEM, SEM, *[HBM] * (2 * n), pl.BlockSpec(memory_space=pltpu.VMEM)),
        input_output_aliases={t: 3 + t for t in range(2 * n)},
        compiler_params=pltpu.CompilerParams(has_side_effects=SIDE_EFFECT),
    )(*ops, after)
    return (out[:3], out[3:3 + n], out[3 + n:3 + 2 * n]), out[-1]


def _relay_pass(state, after, name):
    sems_a, arrs, lands = state
    n = len(arrs)

    def body(*refs):
        sems_b = refs[2 * n + 4:2 * n + 6]
        (a_send, a_recv, a_own), (b_send, _) = _relay_copies(refs[:n], refs[n:2 * n], refs[2 * n:2 * n + 3], sems_b)
        for cp in a_own:
            cp.wait()
        for cp in a_send:
            cp.wait_send()
        for cp in a_recv:
            cp.wait_recv()
        for cp in b_send:
            cp.start()
        refs[-1][...] = jnp.zeros_like(refs[-1])

    out = pl.pallas_call(
        body, name=name,
        out_shape=(pltpu.SemaphoreType.DMA((3 * n,)), pltpu.SemaphoreType.DMA((3 * n,)),
                   *[pltpu.HBM(a.shape, a.dtype) for a in list(arrs) + list(lands)], jax.ShapeDtypeStruct((8, LANES), F32)),
        in_specs=[HBM] * (2 * n) + [SEM, SEM, SEM, ANY],
        out_specs=(SEM, SEM, *[HBM] * (2 * n), pl.BlockSpec(memory_space=pltpu.VMEM)),
        input_output_aliases={t: 2 + t for t in range(2 * n)},
        compiler_params=pltpu.CompilerParams(has_side_effects=SIDE_EFFECT),
    )(*arrs, *lands, *sems_a, after)
    return (out[:2], out[2:2 + n], out[2 + n:2 + 2 * n]), out[-1]


def _relay_wait(state, after, name):
    sems_b, arrs, lands = state
    n = len(arrs)

    def body(*refs):
        _, (b_send, b_recv) = _relay_copies(refs[:n], refs[n:2 * n], None, refs[2 * n:2 * n + 2])
        for cp in b_send:
            cp.wait_send()
        for cp in b_recv:
            cp.wait_recv()

    out = pl.pallas_call(
        body, name=name, out_shape=tuple(pltpu.HBM(a.shape, a.dtype) for a in list(arrs) + list(lands)),
        in_specs=[HBM] * (2 * n) + [SEM, SEM, ANY], out_specs=tuple([HBM] * (2 * n)),
        input_output_aliases={t: t for t in range(2 * n)},
        compiler_params=pltpu.CompilerParams(has_side_effects=SIDE_EFFECT),
    )(*arrs, *lands, *sems_b, after)
    return out[n:]


def _sum_parts(groups, name):
    n, r, cdim = groups[0].shape
    tr = _row_tile(r, max(16, (1 << 21) // (n * cdim * groups[0].dtype.itemsize)))
    steps = r // tr

    def body(*refs):
        o_ref = refs[-1]
        gg = pl.program_id(0)
        for gi in range(len(groups)):
            @pl.when(gg == gi)
            def _(gi=gi):
                acc = refs[gi][0].astype(F32)
                for k in range(1, n):
                    acc = acc + refs[gi][k].astype(F32)
                o_ref[...] = acc

    def in_spec(gi):
        return pl.BlockSpec((n, tr, cdim), lambda gg, i: (0, jnp.where(gg == gi, i, 0), 0))

    return pl.pallas_call(
        body, name=name, out_shape=jax.ShapeDtypeStruct((len(groups) * r, cdim), F32), grid=(len(groups), steps),
        in_specs=[in_spec(gi) for gi in range(len(groups))],
        out_specs=pl.BlockSpec((tr, cdim), lambda gg, i: (gg * steps + i, 0)),
        compiler_params=_params(("parallel", "parallel")),
    )(*groups)


def _mm_tn(a, b, *, name, after=None, tm=512, tn=1024, out_rows=None, row0=0, prev=None):
    k, m = a.shape
    n = b.shape[1]
    tm, tn = _tile(m, tm), _tile(n, tn)
    out_rows = m if out_rows is None else out_rows

    def body(a_ref, b_ref, *rest):
        o_ref, at_ref = rest[-2], rest[-1]

        @pl.when(pl.program_id(1) == 0)
        def _():
            at_ref[...] = a_ref[...].astype(BF16).T

        o_ref[...] = _dot(at_ref[...], b_ref[...].astype(BF16), 1, 0).astype(BF16)

    ins = [a, b] + [t for t in (after, prev) if t is not None]
    return pl.pallas_call(
        body, name=name, out_shape=jax.ShapeDtypeStruct((out_rows, n), BF16), grid=(m // tm, n // tn),
        in_specs=[pl.BlockSpec((k, tm), lambda i, j: (0, i)), pl.BlockSpec((k, tn), lambda i, j: (0, j))] + [ANY] * (len(ins) - 2),
        out_specs=pl.BlockSpec((tm, tn), lambda i, j: (row0 // tm + i, j)),
        input_output_aliases={} if prev is None else {len(ins) - 1: 0},
        scratch_shapes=[pltpu.VMEM((tm, k), BF16)], compiler_params=_params(("parallel", "arbitrary")),
    )(*ins)


def _mm2(a1, b1, a2, b2, *, name, after=None, tm=256, tn=1024, b_rows=None):
    m = a1.shape[0]
    n = b1.shape[1]
    tm, tn = _tile(m, tm), _tile(n, tn)

    def body(a1_ref, b1_ref, a2_ref, b2_ref, *rest):
        rest[-1][...] = (_dot(a1_ref[...].astype(BF16), b1_ref[...], 1, 0)
                         + _dot(a2_ref[...].astype(BF16), b2_ref[...], 1, 0))

    ins = [a1, b1, a2, b2] + ([] if after is None else [after])

    def a_spec(t):
        return pl.BlockSpec((tm, t.shape[1]), lambda i, j: (i, 0))

    def b_spec(t, a, which):
        if b_rows is None:
            return pl.BlockSpec((t.shape[0], tn), lambda i, j: (0, j))
        start = b_rows[which]
        return pl.BlockSpec((pl.Element(a.shape[1]), pl.Element(tn)), lambda i, j: (start, j * tn))

    return pl.pallas_call(
        body, name=name, out_shape=jax.ShapeDtypeStruct((m, n), F32), grid=(m // tm, n // tn),
        in_specs=[a_spec(a1), b_spec(b1, a1, 0), a_spec(a2), b_spec(b2, a2, 1)] + [ANY] * (len(ins) - 4),
        out_specs=pl.BlockSpec((tm, tn), lambda i, j: (i, j)), compiler_params=_params(("parallel", "parallel")),
    )(*ins)


def _mm(a, b, *, name, ta=False, tb=False, res=None, colscale=None, emit_acc=False,
        out_dtype=F32, tm=512, tn=512, b_rows=None):
    m, k = (a.shape[1], a.shape[0]) if ta else a.shape
    n = b.shape[0] if tb else b.shape[1]
    b_start = 0
    if b_rows is not None:
        b_start, n = b_rows
    tm, tn = _tile(m, tm), _tile(n, tn)
    ca, cb = (0 if ta else 1), (1 if tb else 0)
    a_spec = pl.BlockSpec((k, tm), lambda i, j: (0, i)) if ta else pl.BlockSpec((tm, k), lambda i, j: (i, 0))
    b_spec = (pl.BlockSpec((tn, k), lambda i, j: (b_start // tn + j, 0)) if tb
              else pl.BlockSpec((k, tn), lambda i, j: (0, j)))
    tile = pl.BlockSpec((tm, tn), lambda i, j: (i, j))
    ins, in_specs = [a, b], [a_spec, b_spec]
    if res is not None:
        ins.append(res)
        in_specs.append(tile)
    if colscale is not None:
        ins.append(colscale)
        in_specs.append(pl.BlockSpec((1, tn), lambda i, j: (0, j)))
    n_in = len(ins)

    def body(*refs):
        outs = refs[n_in:]
        acc = _dot(refs[0][...].astype(BF16), refs[1][...].astype(BF16), ca, cb)
        val, p = acc, 2
        if res is not None:
            r_val, p = refs[p][...], p + 1
        if colscale is not None:
            val = val * refs[p][...]
        if res is not None:
            val = r_val + val
        if emit_acc:
            outs[0][...] = acc
        outs[-1][...] = val.astype(out_dtype)

    out_shape = [jax.ShapeDtypeStruct((m, n), out_dtype)]
    out_specs = [tile]
    if emit_acc:
        out_shape.insert(0, jax.ShapeDtypeStruct((m, n), F32))
        out_specs.insert(0, tile)
    out = pl.pallas_call(
        body, name=name, out_shape=out_shape, grid=(m // tm, n // tn), in_specs=in_specs, out_specs=out_specs,
        compiler_params=_params(("parallel", "parallel")),
    )(*ins)
    return out if emit_acc else out[0]


def _norm_fwd(x, g, scale, shift, name, after=None):
    s, d = x.shape
    tr = 256

    def body(x_ref, g_ref, sc_ref, sh_ref, *rest):
        xv = x_ref[...]
        rstd = lax.rsqrt(jnp.mean(xv * xv, axis=-1, keepdims=True) + RMS_EPS)
        rest[-1][...] = (xv * rstd * g_ref[...] * (1.0 + sc_ref[...]) + sh_ref[...]).astype(BF16)

    rowspec = pl.BlockSpec((1, d), lambda i: (0, 0))
    ins = [x, g, scale, shift] + ([] if after is None else [after])
    return pl.pallas_call(
        body, name=name, out_shape=jax.ShapeDtypeStruct((s, d), BF16), grid=(s // tr,),
        in_specs=[pl.BlockSpec((tr, d), lambda i: (i, 0)), rowspec, rowspec, rowspec] + [ANY] * (len(ins) - 4),
        out_specs=pl.BlockSpec((tr, d), lambda i: (i, 0)),
        compiler_params=_params(("parallel",)),
    )(*ins)


def _norm_bwd(x, dh, dres, g, scale, name):
    s, d = x.shape
    tr = 256

    def body(x_ref, dh_ref, dr_ref, g_ref, sc_ref, dx_ref, a_ref, b_ref):
        @pl.when(pl.program_id(0) == 0)
        def _():
            a_ref[...] = jnp.zeros_like(a_ref)
            b_ref[...] = jnp.zeros_like(b_ref)

        xv = x_ref[...]
        rstd = lax.rsqrt(jnp.mean(xv * xv, axis=-1, keepdims=True) + RMS_EPS)
        xhat = xv * rstd
        dhv = dh_ref[...]
        dxhat = dhv * (g_ref[...] * (1.0 + sc_ref[...]))
        mean_term = jnp.mean(dxhat * xhat, axis=-1, keepdims=True)
        dx_ref[...] = dr_ref[...] + rstd * (dxhat - xhat * mean_term)
        a_ref[...] += jnp.sum(dhv, axis=0, keepdims=True)
        b_ref[...] += jnp.sum(dhv * xhat, axis=0, keepdims=True)

    rowspec = pl.BlockSpec((1, d), lambda i: (0, 0))
    tile = pl.BlockSpec((tr, d), lambda i: (i, 0))
    return pl.pallas_call(
        body, name=name,
        out_shape=[jax.ShapeDtypeStruct((s, d), F32), jax.ShapeDtypeStruct((1, d), F32), jax.ShapeDtypeStruct((1, d), F32)],
        grid=(s // tr,), in_specs=[tile, tile, tile, rowspec, rowspec], out_specs=[tile, rowspec, rowspec],
        compiler_params=_params(("arbitrary",)),
    )(x, dh, dres, g, scale)


def _gate_bwd(dxn, f, colscale, coef, name):
    s, d = dxn.shape
    tr = 256

    def body(dx_ref, f_ref, cs_ref, df_ref, dg_ref):
        @pl.when(pl.program_id(0) == 0)
        def _():
            dg_ref[...] = jnp.zeros_like(dg_ref)

        dxv = dx_ref[...]
        df_ref[...] = (dxv * cs_ref[...]).astype(BF16)
        dg_ref[...] += coef * jnp.sum(dxv * f_ref[...], axis=0, keepdims=True)

    rowspec = pl.BlockSpec((1, d), lambda i: (0, 0))
    tile = pl.BlockSpec((tr, d), lambda i: (i, 0))
    return pl.pallas_call(
        body, name=name, out_shape=[jax.ShapeDtypeStruct((s, d), BF16), jax.ShapeDtypeStruct((1, d), F32)],
        grid=(s // tr,), in_specs=[tile, tile, rowspec], out_specs=[tile, rowspec],
        compiler_params=_params(("arbitrary",)),
    )(dxn, f, colscale)


def _ffn_up(h, wg, wu, name, tm=SEQ, tn=256):
    s, d = h.shape
    f = wg.shape[0]

    def body(h_ref, wg_ref, wu_ref, a_ref, u_ref, s_ref):
        hv = h_ref[...]
        a = _dot(hv, wg_ref[...], 1, 1)
        u = _dot(hv, wu_ref[...], 1, 1)
        a_ref[...] = a.astype(BF16)
        u_ref[...] = u.astype(BF16)
        s_ref[...] = (a * _sigmoid(a) * u).astype(BF16)

    tile = pl.BlockSpec((tm, tn), lambda i, j: (i, j))
    wspec = pl.BlockSpec((tn, d), lambda i, j: (j, 0))
    return pl.pallas_call(
        body, name=name,
        out_shape=[jax.ShapeDtypeStruct((s, f), BF16), jax.ShapeDtypeStruct((s, f), BF16), jax.ShapeDtypeStruct((s, f), BF16)],
        grid=(s // tm, f // tn), in_specs=[pl.BlockSpec((tm, d), lambda i, j: (i, 0)), wspec, wspec],
        out_specs=[tile, tile, tile], compiler_params=_params(("parallel", "parallel")),
    )(h, wg, wu)


def _ffn_bwd_ds(df, wd, a, u, name, tm=SEQ, tn=256):
    s, d = df.shape
    f = wd.shape[0]

    def body(df_ref, wd_ref, a_ref, u_ref, da_ref, du_ref):
        ds = _dot(df_ref[...], wd_ref[...], 1, 1)
        av = a_ref[...].astype(F32)
        sg = _sigmoid(av)
        da_ref[...] = (ds * u_ref[...].astype(F32) * (sg * (1.0 + av * (1.0 - sg)))).astype(BF16)
        du_ref[...] = (ds * (av * sg)).astype(BF16)

    tile = pl.BlockSpec((tm, tn), lambda i, j: (i, j))
    return pl.pallas_call(
        body, name=name, out_shape=[jax.ShapeDtypeStruct((s, f), BF16), jax.ShapeDtypeStruct((s, f), BF16)],
        grid=(s // tm, f // tn),
        in_specs=[pl.BlockSpec((tm, d), lambda i, j: (i, 0)), pl.BlockSpec((tn, d), lambda i, j: (j, 0)), tile, tile],
        out_specs=[tile, tile], compiler_params=_params(("parallel", "parallel")),
    )(df, wd, a, u)


def _merge_fwd(o_sb, o_dil, o_swa, gates, wb_sb, wb_dil, wb_swa, name):
    s, d = SEQ, D_MODEL
    tm = 256

    def body(osb_ref, odl_ref, osw_ref, g_ref, wsb_ref, wdl_ref, wsw_ref, m_ref, tsb_ref, tdl_ref, tsw_ref):
        for h in range(osb_ref.shape[0]):
            tsb_ref[:, h * HEAD_DIM:(h + 1) * HEAD_DIM] = osb_ref[h].astype(BF16)
        for h in range(osw_ref.shape[0]):
            tsw_ref[:, h * HEAD_DIM:(h + 1) * HEAD_DIM] = osw_ref[h].astype(BF16)
        tdl_ref[...] = odl_ref[...].astype(BF16)
        acc = _sigmoid(g_ref[:, 0:d]) * _dot(tsb_ref[...], wsb_ref[...], 1, 0)
        acc += _sigmoid(g_ref[:, d:2 * d]) * _dot(tdl_ref[...], wdl_ref[...], 1, 0)
        acc += _sigmoid(g_ref[:, 2 * d:3 * d]) * _dot(tsw_ref[...], wsw_ref[...], 1, 0)
        m_ref[...] = acc.astype(BF16)

    def rows(w):
        return pl.BlockSpec((tm, w), lambda i: (i, 0))

    def heads(n):
        return pl.BlockSpec((n, tm, HEAD_DIM), lambda i: (0, i, 0))

    def whole(w):
        return pl.BlockSpec((w, d), lambda i: (0, 0))

    return pl.pallas_call(
        body, name=name, out_shape=[jax.ShapeDtypeStruct((s, w), BF16) for w in (d, 256, 128, 384)], grid=(s // tm,),
        in_specs=[heads(H_SB), rows(128), heads(H_SWA_Q), rows(3 * d), whole(256), whole(128), whole(384)],
        out_specs=[rows(d), rows(256), rows(128), rows(384)], compiler_params=_params(("parallel",)),
    )(o_sb, o_dil, o_swa, gates, wb_sb, wb_dil, wb_swa)


def _merge_bwd(dmerged, t_sb, t_dil, t_swa, gates, wb_sb, wb_dil, wb_swa, name):
    s, d = SEQ, D_MODEL
    tm = 256

    def body(dm_ref, tsb_ref, tdl_ref, tsw_ref, g_ref, wsb_ref, wdl_ref, wsw_ref,
             dg_ref, dosb_ref, dodl_ref, dosw_ref, dbsb_ref, dbdl_ref, dbsw_ref):
        dm = dm_ref[...]
        for idx, (t_ref, w_ref, do_ref, db_ref) in enumerate((
                (tsb_ref, wsb_ref, dosb_ref, dbsb_ref), (tdl_ref, wdl_ref, dodl_ref, dbdl_ref),
                (tsw_ref, wsw_ref, dosw_ref, dbsw_ref))):
            w = w_ref[...]
            br = _dot(t_ref[...], w, 1, 0)
            sg = _sigmoid(g_ref[:, idx * d:(idx + 1) * d])
            dbr = (dm * sg).astype(BF16)
            dg_ref[:, idx * d:(idx + 1) * d] = (dm * br * (sg * (1.0 - sg))).astype(BF16)
            db_ref[...] = dbr
            do = _dot(dbr, w, 1, 1)
            if len(do_ref.shape) == 2:
                do_ref[...] = do
            else:
                for h in range(do_ref.shape[0]):
                    do_ref[h] = do[:, h * HEAD_DIM:(h + 1) * HEAD_DIM]

    def rows(w):
        return pl.BlockSpec((tm, w), lambda i: (i, 0))

    def heads(n):
        return pl.BlockSpec((n, tm, HEAD_DIM), lambda i: (0, i, 0))

    def whole(w):
        return pl.BlockSpec((w, d), lambda i: (0, 0))

    def shp(w, dt):
        return jax.ShapeDtypeStruct((s, w), dt)

    def hshp(n):
        return jax.ShapeDtypeStruct((n, s, HEAD_DIM), F32)

    return pl.pallas_call(
        body, name=name,
        out_shape=[shp(3 * d, BF16), hshp(H_SB), shp(128, F32), hshp(H_SWA_Q), shp(d, BF16), shp(d, BF16), shp(d, BF16)],
        grid=(s // tm,),
        in_specs=[rows(d), rows(256), rows(128), rows(384), rows(3 * d), whole(256), whole(128), whole(384)],
        out_specs=[rows(3 * d), heads(H_SB), rows(128), heads(H_SWA_Q), rows(d), rows(d), rows(d)],
        compiler_params=_params(("parallel",)),
    )(dmerged, t_sb, t_dil, t_swa, gates, wb_sb, wb_dil, wb_swa)


def _final_loss(x, target, g, name):
    s, d = x.shape
    tr = 256

    def body(x_ref, t_ref, g_ref, loss_ref, dx_ref, dg_ref):
        @pl.when(pl.program_id(0) == 0)
        def _():
            loss_ref[...] = jnp.zeros_like(loss_ref)
            dg_ref[...] = jnp.zeros_like(dg_ref)

        xv = x_ref[...]
        gv = g_ref[...]
        rstd = lax.rsqrt(jnp.mean(xv * xv, axis=-1, keepdims=True) + RMS_EPS)
        xhat = xv * rstd
        err = xhat * gv - t_ref[...]
        loss_ref[...] += 0.5 * jnp.sum(jnp.mean(err * err, axis=-1, keepdims=True))
        dy = err * (1.0 / d)
        dxhat = dy * gv
        mean_term = jnp.mean(dxhat * xhat, axis=-1, keepdims=True)
        dx_ref[...] = rstd * (dxhat - xhat * mean_term)
        dg_ref[...] += jnp.sum(dy * xhat, axis=0, keepdims=True)

    rowspec = pl.BlockSpec((1, d), lambda i: (0, 0))
    tile = pl.BlockSpec((tr, d), lambda i: (i, 0))
    return pl.pallas_call(
        body, name=name,
        out_shape=[jax.ShapeDtypeStruct((1, LANES), F32), jax.ShapeDtypeStruct((s, d), F32), jax.ShapeDtypeStruct((1, d), F32)],
        grid=(s // tr,), in_specs=[tile, tile, rowspec],
        out_specs=[pl.BlockSpec((1, LANES), lambda i: (0, 0)), tile, rowspec],
        compiler_params=_params(("arbitrary",)),
    )(x, target, g)


def _adamw(w, g, m, v, name):
    shape = w.shape
    cols = shape[-1]
    rows = int(np.prod(shape[:-1])) if len(shape) > 1 else 1
    tr = rows
    for cand in (1024, 512, 256, 128, 64, 32, 16, 8):
        if rows % cand == 0 and rows > cand and cand * cols * 4 <= (1 << 21):
            tr = cand
            break

    def body(w_ref, g_ref, m_ref, v_ref, d_ref, nm_ref, nv_ref):
        d_ref[...], nm_ref[...], nv_ref[...] = _adam_update(w_ref[...], g_ref[...], m_ref[...], v_ref[...])

    tile = pl.BlockSpec((tr, cols), lambda i: (i, 0))
    flat = [t.reshape(rows, cols) for t in (w, g, m, v)]
    out = pl.pallas_call(
        body, name=name, out_shape=[jax.ShapeDtypeStruct((rows, cols), F32)] * 3, grid=(rows // tr,),
        in_specs=[tile] * 4, out_specs=[tile] * 3, compiler_params=_params(("parallel",)),
    )(*flat)
    return tuple(t.reshape(shape) for t in out)


def _adam_update(w, gv, m, v):
    nm = ADAM_B1 * m + (1.0 - ADAM_B1) * gv
    nv = ADAM_B2 * v + (1.0 - ADAM_B2) * (gv * gv)
    m_hat = nm / (1.0 - ADAM_B1 ** ADAM_STEP)
    v_hat = nv / (1.0 - ADAM_B2 ** ADAM_STEP)
    return -ADAM_LR * (m_hat / (jnp.sqrt(v_hat) + ADAM_EPS) + ADAM_WD * w), nm, nv


def _reduce_adamw(groups, w, m, v, row0, prev, name, after=None):
    n, r, cdim = groups[0].shape
    rows = w.shape[0]
    tr = _row_tile(r, max(16, (1 << 22) // (n * cdim * groups[0].dtype.itemsize)))
    steps = r // tr
    ng = len(groups)

    def body(*refs):
        w_ref, m_ref, v_ref = refs[ng:ng + 3]
        g_out, d_out, m_out, v_out = refs[-4:]
        gg = pl.program_id(0)
        for gi in range(ng):
            @pl.when(gg == gi)
            def _(gi=gi):
                acc = refs[gi][0].astype(F32)
                for k in range(1, n):
                    acc = acc + refs[gi][k].astype(F32)
                g_out[...] = acc
                d_out[...], m_out[...], v_out[...] = _adam_update(w_ref[...], acc, m_ref[...], v_ref[...])

    def part_spec(gi):
        return pl.BlockSpec((n, tr, cdim), lambda gg, i: (0, jnp.where(gg == gi, i, 0), 0))

    tile = pl.BlockSpec((tr, cdim), lambda gg, i: (row0 // tr + gg * steps + i, 0))
    extra = ([] if prev is None else list(prev)) + ([] if after is None else [after])
    return pl.pallas_call(
        body, name=name, out_shape=[jax.ShapeDtypeStruct((rows, cdim), F32)] * 4, grid=(ng, steps),
        in_specs=[part_spec(gi) for gi in range(ng)] + [tile] * 3 + [ANY] * len(extra), out_specs=[tile] * 4,
        input_output_aliases={} if prev is None else {ng + 3 + k: k for k in range(4)},
        compiler_params=_params(("parallel", "parallel")),
    )(*groups, w, m, v, *extra)


def _ada_fwd(c_all, w, name):
    n = w.shape[1]

    def body(c_ref, w_ref, o_ref):
        cv = c_ref[...]
        o_ref[...] = jnp.dot(cv * _sigmoid(cv), w_ref[...], preferred_element_type=F32, precision=lax.Precision.HIGHEST)

    return pl.pallas_call(body, name=name, out_shape=jax.ShapeDtypeStruct((N_DEV, n), F32), compiler_params=_params())(c_all, w)


def _ada_bwd(c_all_t, dmod, name):
    n = dmod.shape[1]

    def body(c_ref, d_ref, o_ref):
        cv = c_ref[...]
        o_ref[...] = jnp.dot(cv * _sigmoid(cv), d_ref[...], preferred_element_type=F32, precision=lax.Precision.HIGHEST)

    return pl.pallas_call(body, name=name, out_shape=jax.ShapeDtypeStruct((D_MODEL, n), F32), compiler_params=_params())(c_all_t, dmod)


def _bucket_tables():
    rel = np.arange(BLK)[:, None] + BLK - np.arange(2 * BLK)[None, :]
    max_exact = N_BUCKETS // 2

    def bucket(n):
        nf = np.maximum(n, 1).astype(np.float32)
        large = max_exact + (np.log(nf / np.float32(max_exact)) / np.float32(math.log(MAX_REL_DIST / max_exact))
                             * np.float32(N_BUCKETS - max_exact)).astype(np.int32)
        return np.where(n < max_exact, n, np.minimum(large, N_BUCKETS - 1))

    tabs = []
    for dil, max_dist in ((1, 128), (4, 128), (16, 128), (1, SWA_WINDOW - 1)):
        in_band = (rel >= 0) & (rel <= max_dist)
        tabs.append(np.where(in_band, bucket(np.maximum(rel, 0) * dil), -1))
    return np.stack(tabs).astype(np.int32)


N_SOFT = H_DIL + H_SWA_Q


def _table_of_head(h):
    return jnp.minimum(h // 2, 3)


def _bias_build(rel_bias, tables, name):
    def body(rel_ref, t_ref, o_ref):
        h = pl.program_id(0)
        tb = t_ref[0]
        out = jnp.full((BLK, 2 * BLK), NEG, F32)
        for b in range(N_BUCKETS):
            out = jnp.where(tb == b, rel_ref[b, h], out)
        o_ref[0] = out

    return pl.pallas_call(
        body, name=name, out_shape=jax.ShapeDtypeStruct((N_SOFT, BLK, 2 * BLK), F32), grid=(N_SOFT,),
        in_specs=[pl.BlockSpec(memory_space=pltpu.SMEM),
                  pl.BlockSpec((1, BLK, 2 * BLK), lambda h: (_table_of_head(h), 0, 0))],
        out_specs=pl.BlockSpec((1, BLK, 2 * BLK), lambda h: (h, 0, 0)),
        compiler_params=_params(("parallel",)),
    )(rel_bias, tables)


def _bias_grad(dbias, tables, name):
    def body(d_ref, t_ref, o_ref):
        tb = t_ref[0]
        dv = d_ref[0]
        lane = lax.broadcasted_iota(jnp.int32, (1, LANES), 1)
        out = jnp.zeros((1, LANES), F32)
        for b in range(N_BUCKETS):
            out = jnp.where(lane == b, jnp.sum(jnp.where(tb == b, dv, 0.0)), out)
        o_ref[0] = out

    return pl.pallas_call(
        body, name=name, out_shape=jax.ShapeDtypeStruct((N_SOFT, 1, LANES), F32), grid=(N_SOFT,),
        in_specs=[pl.BlockSpec((1, BLK, 2 * BLK), lambda h: (h, 0, 0)),
                  pl.BlockSpec((1, BLK, 2 * BLK), lambda h: (_table_of_head(h), 0, 0))],
        out_specs=pl.BlockSpec((1, 1, LANES), lambda h: (h, 0, 0)),
        compiler_params=_params(("parallel",)),
    )(dbias, tables)


def _band_layout(g, bias_div):
    assert g == 1 or bias_div == 1
    return bias_div if g == 1 else 1


def _band_specs(length, g, bias_div, offs):
    ns = _band_layout(g, bias_div)

    def seqs(off, div=1):
        return pl.BlockSpec((ns, length, HEAD_DIM), lambda s: (off // ns + s // div, 0, 0))

    xspecs = [seqs(offs[0]), seqs(offs[1], g), seqs(offs[2], g)]
    bspec = pl.BlockSpec((1, BLK, 2 * BLK), lambda s: (s, 0, 0))
    sspec = pl.BlockSpec((ns, 1, LANES), lambda s: (s, 0, 0))
    colspec = pl.BlockSpec((ns, length, 1), lambda s: (s, 0, 0))
    return xspecs, seqs(0), seqs(0, g), bspec, sspec, colspec


def _band_sweep(length, ns, one):
    nblk = length // BLK
    for qq in range(ns):
        if ns * nblk <= 16:
            for i in range(nblk):
                one(qq, i * BLK, max(i - 1, 0) * BLK, i == 0)
        else:
            def step(i, carry, qq=qq):
                one(qq, pl.multiple_of(i * BLK, BLK), pl.multiple_of(jnp.maximum(i - 1, 0) * BLK, BLK), i == 0)
                return carry

            lax.fori_loop(0, nblk, step, 0, unroll=2)


def _band_scores(q_ref, k_ref, b_ref, qq, kq, bq, cur, prv, first):
    qv = q_ref[qq, pl.ds(cur, BLK), :]
    bv = b_ref[bq]
    if first is True:
        sp = jnp.full((BLK, BLK), NEG, F32)
    else:
        sp = _dot(qv, k_ref[kq, pl.ds(prv, BLK), :], 1, 1) + bv[:, :BLK]
        sp = sp if first is False else jnp.where(first, NEG, sp)
    sc = _dot(qv, k_ref[kq, pl.ds(cur, BLK), :], 1, 1) + bv[:, BLK:]
    return qv, sp, sc


def _band_fwd(x, bias, sink, *, nq, offs, g, bias_div, has_sink, name):
    length = x.shape[1]
    ns = _band_layout(g, bias_div)

    def body(q_ref, k_ref, v_ref, b_ref, s_ref, o_ref, lse_ref):
        def one(qq, cur, prv, first):
            kq, bq = qq, 0
            _, sp, sc = _band_scores(q_ref, k_ref, b_ref, qq, kq, bq, cur, prv, first)
            m = jnp.maximum(jnp.max(sp, axis=1, keepdims=True), jnp.max(sc, axis=1, keepdims=True))
            if has_sink:
                sk = s_ref[qq][:, :1]
                m = jnp.maximum(m, sk)
            pp, pc = jnp.exp(sp - m), jnp.exp(sc - m)
            den = jnp.sum(pp, axis=1, keepdims=True) + jnp.sum(pc, axis=1, keepdims=True)
            if has_sink:
                den = den + jnp.exp(sk - m)
            acc = (_dot(pp.astype(BF16), v_ref[kq, pl.ds(prv, BLK), :], 1, 0)
                   + _dot(pc.astype(BF16), v_ref[kq, pl.ds(cur, BLK), :], 1, 0))
            o_ref[qq, pl.ds(cur, BLK), :] = acc / den
            lse_ref[qq, pl.ds(cur, BLK), :] = m + jnp.log(den)

        _band_sweep(length, ns, one)

    xspecs, qspec, _, bspec, sspec, colspec = _band_specs(length, g, bias_div, offs)
    return pl.pallas_call(
        body, name=name,
        out_shape=[jax.ShapeDtypeStruct((nq, length, HEAD_DIM), F32), jax.ShapeDtypeStruct((nq, length, 1), F32)],
        grid=(nq // ns,), in_specs=xspecs + [bspec, sspec],
        out_specs=[qspec, colspec], compiler_params=_params(("parallel",)),
    )(x, x, x, bias, sink)


def _band_bwd(x, bias, sink, o, lse, do, dlse, *, nq, offs, g, bias_div, has_sink, name):
    length = x.shape[1]
    ns = _band_layout(g, bias_div)
    nk, nbias = nq // g, nq // bias_div

    def body(q_ref, k_ref, v_ref, b_ref, s_ref, o_ref, lse_ref, do_ref, dlse_ref,
             dq_ref, dk_ref, dv_ref, db_ref, dsk_ref, dkp_ref, dvp_ref):
        for ref in (db_ref, dsk_ref, dkp_ref, dvp_ref):
            ref[...] = jnp.zeros_like(ref)

        @pl.when(pl.program_id(0) % g == 0)
        def _():
            dk_ref[...] = jnp.zeros_like(dk_ref)
            dv_ref[...] = jnp.zeros_like(dv_ref)

        def one(qq, cur, prv, first):
            kq, bq = qq, 0
            qv, sp, sc = _band_scores(q_ref, k_ref, b_ref, qq, kq, bq, cur, prv, first)
            rows, prow = pl.ds(cur, BLK), pl.ds(prv, BLK)
            lse_v = lse_ref[qq, rows, :]
            pp, pc = jnp.exp(sp - lse_v), jnp.exp(sc - lse_v)
            dov = do_ref[qq, rows, :]
            dob = dov.astype(BF16)
            coef = dlse_ref[qq, rows, :] - jnp.sum(dov * o_ref[qq, rows, :], axis=1, keepdims=True)
            dsp = pp * (_dot(dob, v_ref[kq, prow, :], 1, 1) + coef)
            dsc = pc * (_dot(dob, v_ref[kq, rows, :], 1, 1) + coef)
            dspb, dscb = dsp.astype(BF16), dsc.astype(BF16)
            dq_ref[qq, rows, :] = ((_dot(dspb, k_ref[kq, prow, :], 1, 0) + _dot(dscb, k_ref[kq, rows, :], 1, 0))
                                   * (HEAD_DIM ** -0.5))
            dk_ref[kq, rows, :] += _dot(dscb, qv, 0, 0)
            dkp_ref[kq, prow, :] += _dot(dspb, qv, 0, 0)
            dv_ref[kq, rows, :] += _dot(pc.astype(BF16), dob, 0, 0)
            dvp_ref[kq, prow, :] += _dot(pp.astype(BF16), dob, 0, 0)
            db_ref[bq, :, :BLK] += dsp
            db_ref[bq, :, BLK:] += dsc
            if has_sink:
                dsk_ref[qq] += jnp.sum(jnp.exp(s_ref[qq][:, :1] - lse_v) * coef)

        _band_sweep(length, ns, one)
        dk_ref[...] += dkp_ref[...]
        dv_ref[...] += dvp_ref[...]

    xspecs, qspec, kvspec, bspec, sspec, colspec = _band_specs(length, g, bias_div, offs)
    return pl.pallas_call(
        body, name=name,
        out_shape=[jax.ShapeDtypeStruct((nq, length, HEAD_DIM), F32), jax.ShapeDtypeStruct((nk, length, HEAD_DIM), F32),
                   jax.ShapeDtypeStruct((nk, length, HEAD_DIM), F32), jax.ShapeDtypeStruct((nbias, BLK, 2 * BLK), F32),
                   jax.ShapeDtypeStruct((nq, 1, LANES), F32)],
        grid=(nq // ns,),
        in_specs=xspecs + [bspec, sspec, qspec, colspec, qspec, colspec],
        out_specs=[qspec, kvspec, kvspec, bspec, sspec],
        scratch_shapes=[pltpu.VMEM((ns, length, HEAD_DIM), F32), pltpu.VMEM((ns, length, HEAD_DIM), F32)],
        compiler_params=_params(("arbitrary",)),
    )(x, x, x, bias, sink, o, lse, do, dlse)


TOK_TILE = 512


def _dil_merge(outs, lses, dout, name):
    tr = TOK_TILE
    dils = [d for _, d in DIL_PATTERNS]
    n = len(dils)
    o4 = [o.reshape(2, d, SEQ // d, HEAD_DIM) for o, d in zip(outs, dils)]
    l4 = [l.reshape(2, d, SEQ // d, 1) for l, d in zip(lses, dils)]
    o_specs = [pl.BlockSpec((2, d, tr // d, HEAD_DIM), lambda i: (0, 0, i, 0)) for d in dils]
    l_specs = [pl.BlockSpec((2, d, tr // d, 1), lambda i: (0, 0, i, 0)) for d in dils]
    tok = pl.BlockSpec((tr, 2 * HEAD_DIM), lambda i: (i, 0))
    scratch = ([pltpu.VMEM((tr, 2 * HEAD_DIM), F32) for _ in dils] + [pltpu.VMEM((tr, 1), F32) for _ in range(2 * n)]
               + [pltpu.VMEM((tr // d, 2 * HEAD_DIM), F32) for d in dils])

    def to_tokens(o_ref, l_ref, d, pair, cols, stage):
        for r in range(d):
            rows = pl.ds(r, tr // d, stride=d) if d > 1 else slice(None)
            stage[:, :HEAD_DIM] = o_ref[0, r]
            stage[:, HEAD_DIM:] = o_ref[1, r]
            pair[rows, :] = stage[...]
            for h in range(2):
                cols[h][rows, :] = l_ref[h, r]
        return pair[...], [cols[0][...], cols[1][...]]

    def weights(ls):
        left = lax.broadcasted_iota(jnp.int32, (tr, 2 * HEAD_DIM), 1) < HEAD_DIM
        per_head = []
        for h in range(2):
            m = ls[0][h]
            for g in range(1, n):
                m = jnp.maximum(m, ls[g][h])
            es = [jnp.exp(ls[g][h] - m) for g in range(n)]
            den = es[0]
            for e in es[1:]:
                den = den + e
            per_head.append([e / den for e in es])
        return per_head, [jnp.where(left, per_head[0][g], per_head[1][g]) for g in range(n)], left

    def load(refs):
        pairs, cols, stages = refs[:n], refs[n:3 * n], refs[3 * n:]
        return pairs, [cols[2 * g:2 * g + 2] for g in range(n)], stages

    if dout is None:
        def body(*refs):
            pairs, cols, stages = load(refs[2 * n + 1:])
            toks = [to_tokens(refs[g], refs[n + g], dils[g], pairs[g], cols[g], stages[g]) for g in range(n)]
            _, alphas, _ = weights([t[1] for t in toks])
            acc = alphas[0] * toks[0][0]
            for g in range(1, n):
                acc = acc + alphas[g] * toks[g][0]
            refs[2 * n][...] = acc

        return pl.pallas_call(
            body, name=name, out_shape=jax.ShapeDtypeStruct((SEQ, 2 * HEAD_DIM), F32), grid=(SEQ // tr,),
            in_specs=o_specs + l_specs, out_specs=tok, scratch_shapes=scratch, compiler_params=_params(("parallel",)),
        )(*o4, *l4)

    def body(*refs):
        do_refs, dl_refs = refs[2 * n + 1:3 * n + 1], refs[3 * n + 1:4 * n + 1]
        pairs, cols, stages = load(refs[4 * n + 1:])
        toks = [to_tokens(refs[g], refs[n + g], dils[g], pairs[g], cols[g], stages[g]) for g in range(n)]
        per_head, alphas, left = weights([t[1] for t in toks])
        dov = refs[2 * n][...]
        das = []
        for g in range(n):
            prod = dov * toks[g][0]
            das.append([jnp.sum(jnp.where(left, prod, 0.0), axis=1, keepdims=True),
                        jnp.sum(jnp.where(left, 0.0, prod), axis=1, keepdims=True)])
        dbar = [sum(per_head[h][g] * das[g][h] for g in range(n)) for h in range(2)]
        for g, d in enumerate(dils):
            pairs[g][...] = alphas[g] * dov
            for h in range(2):
                cols[g][h][...] = per_head[h][g] * (das[g][h] - dbar[h])
            for r in range(d):
                rows = pl.ds(r, tr // d, stride=d) if d > 1 else slice(None)
                v = pairs[g][rows, :]
                for h in range(2):
                    do_refs[g][h, r] = v[:, h * HEAD_DIM:(h + 1) * HEAD_DIM]
                    dl_refs[g][h, r] = cols[g][h][rows, :]

    out = pl.pallas_call(
        body, name=name,
        out_shape=[jax.ShapeDtypeStruct(o.shape, F32) for o in o4] + [jax.ShapeDtypeStruct(l.shape, F32) for l in l4],
        grid=(SEQ // tr,), in_specs=o_specs + l_specs + [tok], out_specs=o_specs + l_specs, scratch_shapes=scratch,
        compiler_params=_params(("parallel",)),
    )(*o4, *l4, dout)
    return [t.reshape(s.shape) for t, s in zip(out, list(outs) + list(lses))]


def _tri(cmp):
    r = lax.broadcasted_iota(jnp.int32, (SB_TILE, SB_TILE), 0)
    c = lax.broadcasted_iota(jnp.int32, (SB_TILE, SB_TILE), 1)
    return cmp(r, c).astype(BF16)


def _cum(x, tri, terms):
    acc, rest = None, x
    for _ in range(terms):
        part = rest.astype(BF16)
        rest = rest - part.astype(F32)
        d = _dot(part, tri, 1, 0)
        acc = d if acc is None else acc + d
    return acc


def _sb_logits(q, ks, diagonal):
    t = SB_TILE
    z = _dot(q, ks, 1, 1)
    e = jnp.exp(-jnp.abs(z))
    lf = -(jnp.maximum(z, 0.0) + jnp.log(1.0 + e))
    if not diagonal:
        return z, e, lf, None
    mask = lax.broadcasted_iota(jnp.int32, (t, t), 1) < lax.broadcasted_iota(jnp.int32, (t, t), 0)
    return z, e, jnp.where(mask, lf, 0.0), mask


def _sb_specs(h, s):
    t = SB_TILE
    tile = pl.BlockSpec((h, t, HEAD_DIM), lambda i: (0, i, 0))
    keys = pl.BlockSpec((h, s, HEAD_DIM), lambda i: (1, 0, 0))
    values = pl.BlockSpec((h, s, HEAD_DIM), lambda i: (2, 0, 0))
    return tile, keys, values, pl.BlockSpec((h, t, 1), lambda i: (0, i, 0))


def _sb_fwd(x, name):
    h, s = x.shape[0] // 3, x.shape[1]
    t = SB_TILE

    def body(q_ref, k_ref, v_ref, o_ref, tot_ref):
        i = pl.program_id(0)
        after = _tri(lambda r, c: r > c)

        def tile(j, carry, diagonal):
            rows = pl.ds(pl.multiple_of(j * t, t), t)
            out = []
            for hh, (right, acc) in enumerate(carry):
                z, _, lf, mask = _sb_logits(q_ref[hh], k_ref[hh, rows, :], diagonal)
                w = jnp.exp(z + lf + (right + _cum(lf, after, 2)))
                w = w if mask is None else jnp.where(mask, w, 0.0)
                out.append((right + jnp.sum(lf, axis=1, keepdims=True), acc + _dot(w.astype(BF16), v_ref[hh, rows, :], 1, 0)))
            return tuple(out)

        carry = tile(i, tuple((jnp.zeros((t, 1), F32), jnp.zeros((t, HEAD_DIM), F32)) for _ in range(h)), True)
        carry = lax.fori_loop(0, i, lambda jj, c: tile(i - 1 - jj, c, False), carry)
        for hh, (right, acc) in enumerate(carry):
            o_ref[hh] = acc
            tot_ref[hh] = right

    tile_spec, keys, values, col = _sb_specs(h, s)
    return pl.pallas_call(
        body, name=name, out_shape=[jax.ShapeDtypeStruct((h, s, HEAD_DIM), F32), jax.ShapeDtypeStruct((h, s, 1), F32)],
        grid=(s // t,), in_specs=[tile_spec, keys, values], out_specs=[tile_spec, col],
        compiler_params=_params(("parallel",)),
    )(x, x, x)


def _sb_bwd(x, tot, do, name):
    h, s = x.shape[0] // 3, x.shape[1]
    t = SB_TILE

    def body(q_ref, k_ref, v_ref, tot_ref, do_ref, dq_ref, dk_ref, dv_ref):
        i = pl.program_id(0)

        @pl.when(i == 0)
        def _():
            dk_ref[...] = jnp.zeros_like(dk_ref)
            dv_ref[...] = jnp.zeros_like(dv_ref)

        upto = _tri(lambda r, c: r <= c)
        before = _tri(lambda r, c: r < c)

        def tile(j, carry, diagonal):
            rows = pl.ds(pl.multiple_of(j * t, t), t)
            out = []
            for hh, (left, cleft, dq) in enumerate(carry):
                qv, ks, dob = q_ref[hh], k_ref[hh, rows, :], do_ref[hh].astype(BF16)
                z, e, lf, mask = _sb_logits(qv, ks, diagonal)
                between = tot_ref[hh] - (left + _cum(lf, upto, 2))
                w = jnp.exp(z + lf + between)
                w = w if mask is None else jnp.where(mask, w, 0.0)
                dlog = w * _dot(dob, v_ref[hh, rows, :], 1, 1)
                cfail = cleft + _cum(dlog, before, 2)
                sig = jnp.where(z >= 0.0, 1.0, e) / (1.0 + e)
                dz = dlog * (1.0 - sig) - sig * cfail
                dz = (dz if mask is None else jnp.where(mask, dz, 0.0)).astype(BF16)
                dk_ref[hh, rows, :] += _dot(dz, qv, 0, 0)
                dv_ref[hh, rows, :] += _dot(w.astype(BF16), dob, 0, 0)
                out.append((left + jnp.sum(lf, axis=1, keepdims=True), cleft + jnp.sum(dlog, axis=1, keepdims=True),
                            dq + _dot(dz, ks, 1, 0)))
            return tuple(out)

        zero = jnp.zeros((t, 1), F32)
        carry = lax.fori_loop(0, i, lambda j, c: tile(j, c, False),
                              tuple((zero, zero, jnp.zeros((t, HEAD_DIM), F32)) for _ in range(h)))
        for hh, (_, _, dq) in enumerate(tile(i, carry, True)):
            dq_ref[hh] = dq * (HEAD_DIM ** -0.5)

    tile_spec, keys, values, col = _sb_specs(h, s)
    full = pl.BlockSpec((h, s, HEAD_DIM), lambda i: (0, 0, 0))
    shp = jax.ShapeDtypeStruct((h, s, HEAD_DIM), F32)
    return pl.pallas_call(
        body, name=name, out_shape=[shp, shp, shp], grid=(s // t,),
        in_specs=[tile_spec, keys, values, col, tile_spec],
        out_specs=[tile_spec, full, full], compiler_params=_params(("arbitrary",)),
    )(x, x, x, tot, do)


COL_SB, COL_DIL, COL_SWA = 0, 3 * H_SB * HEAD_DIM, 3 * H_SB * HEAD_DIM + 3 * H_DIL * HEAD_DIM
N_SWA = H_SWA_Q + 2 * H_SWA_KV


def _dil_col(t, g):
    return COL_DIL + t * H_DIL * HEAD_DIM + g * 2 * HEAD_DIM


def _split_heads(qkv, name):
    tr = TOK_TILE
    scale = HEAD_DIM ** -0.5
    dils = [d for _, d in DIL_PATTERNS]

    def body(x_ref, sb_ref, d0_ref, d1_ref, d2_ref, swa_ref, pair):
        def head(col, scaled):
            v = x_ref[:, col:col + HEAD_DIM]
            return (v * scale if scaled else v).astype(BF16)

        for hh in range(3 * H_SB):
            sb_ref[hh] = head(COL_SB + hh * HEAD_DIM, hh < H_SB)
        for hh in range(N_SWA):
            swa_ref[hh] = head(COL_SWA + hh * HEAD_DIM, hh < H_SWA_Q)
        for t in range(3):
            for g, (d, out_ref) in enumerate(zip(dils, (d0_ref, d1_ref, d2_ref))):
                col = _dil_col(t, g)
                if d == 1:
                    for h in range(2):
                        out_ref[t * 2 + h] = head(col + h * HEAD_DIM, t == 0)
                    continue
                pair[...] = x_ref[:, col:col + 2 * HEAD_DIM]
                for r in range(d):
                    v = pair[pl.ds(r, tr // d, stride=d), :]
                    v = v * scale if t == 0 else v
                    for h in range(2):
                        out_ref[t * 2 * d + h * d + r] = v[:, h * HEAD_DIM:(h + 1) * HEAD_DIM].astype(BF16)

    def heads(n, length):
        return jax.ShapeDtypeStruct((n, length, HEAD_DIM), BF16)

    def spec(n, rows):
        return pl.BlockSpec((n, rows, HEAD_DIM), lambda i: (0, i, 0))

    return pl.pallas_call(
        body, name=name,
        out_shape=[heads(3 * H_SB, SEQ)] + [heads(6 * d, SEQ // d) for d in dils] + [heads(N_SWA, SEQ)],
        grid=(SEQ // tr,), in_specs=[pl.BlockSpec((tr, D_QKV), lambda i: (i, 0))],
        out_specs=[spec(3 * H_SB, tr)] + [spec(6 * d, tr // d) for d in dils] + [spec(N_SWA, tr)],
        scratch_shapes=[pltpu.VMEM((tr, 2 * HEAD_DIM), F32)], compiler_params=_params(("parallel",)),
    )(qkv)


def _join_heads(sb, dil, swa, name):
    tr = TOK_TILE
    dils = [d for _, d in DIL_PATTERNS]

    def body(*refs):
        sb_refs, dil_refs, swa_refs = refs[:3], [refs[3 + 3 * g:6 + 3 * g] for g in range(3)], refs[12:15]
        o_ref, pair, stages = refs[15], refs[16], refs[17:]

        def put(col, v):
            o_ref[:, col:col + v.shape[1]] = v.astype(BF16)

        for t in range(3):
            for h in range(H_SB):
                put(COL_SB + (t * H_SB + h) * HEAD_DIM, sb_refs[t][h])
        col = COL_SWA
        for ref in swa_refs:
            for h in range(ref.shape[0]):
                put(col, ref[h])
                col += HEAD_DIM
        for t in range(3):
            for g, d in enumerate(dils):
                ref, col = dil_refs[g][t], _dil_col(t, g)
                if d == 1:
                    for h in range(2):
                        put(col + h * HEAD_DIM, ref[h])
                    continue
                stage = stages[g - 1]
                for r in range(d):
                    stage[:, :HEAD_DIM] = ref[r]
                    stage[:, HEAD_DIM:] = ref[d + r]
                    pair[pl.ds(r, tr // d, stride=d), :] = stage[...]
                put(col, pair[...])

    def spec(n, rows):
        return pl.BlockSpec((n, rows, HEAD_DIM), lambda i: (0, i, 0))

    ins = list(sb) + [t for g in range(3) for t in dil[g]] + list(swa)
    in_specs = ([spec(H_SB, tr)] * 3 + [spec(2 * d, tr // d) for d in dils for _ in range(3)]
                + [spec(H_SWA_Q, tr), spec(H_SWA_KV, tr), spec(H_SWA_KV, tr)])
    return pl.pallas_call(
        body, name=name, out_shape=jax.ShapeDtypeStruct((SEQ, D_QKV), BF16), grid=(SEQ // tr,), in_specs=in_specs,
        out_specs=pl.BlockSpec((tr, D_QKV), lambda i: (i, 0)),
        scratch_shapes=[pltpu.VMEM((tr, 2 * HEAD_DIM), F32)] + [pltpu.VMEM((tr // d, 2 * HEAD_DIM), F32) for d in dils[1:]],
        compiler_params=_params(("parallel",)),
    )(*ins)


def _mixer_fwd(qkv, bias, sinks_l, tag):
    sb, d0, d1, d2, swa = _split_heads(qkv, name=f"split_heads_{tag}")
    st = {"sb": sb, "dil": (d0, d1, d2), "swa": swa}
    o_sb, st["sb_tot"] = _sb_fwd(sb, name=f"sb_fwd_{tag}")
    st["dil_out"], st["dil_lse"], st["dil_sink"] = [], [], []
    for gi, (_, d) in enumerate(DIL_PATTERNS):
        sink = jnp.zeros((2 * d, 1, LANES), F32)
        og, lg = _band_fwd(st["dil"][gi], bias[2 * gi:2 * gi + 2], sink, nq=2 * d, offs=(0, 2 * d, 4 * d), g=1, bias_div=d,
                           has_sink=False, name=f"dil{gi}_fwd_{tag}")
        st["dil_out"].append(og)
        st["dil_lse"].append(lg)
        st["dil_sink"].append(sink)
    o_dil = _dil_merge(st["dil_out"], st["dil_lse"], None, name=f"dil_merge_fwd_{tag}")
    st["swa_sink"] = jnp.broadcast_to(sinks_l.reshape(H_SWA_Q, 1, 1), (H_SWA_Q, 1, LANES))
    st["swa_out"] = _band_fwd(swa, bias[H_DIL:], st["swa_sink"], nq=H_SWA_Q, offs=(0, H_SWA_Q, H_SWA_Q + H_SWA_KV),
                              g=H_SWA_Q // H_SWA_KV, bias_div=1, has_sink=True, name=f"swa_fwd_{tag}")
    return (o_sb, o_dil, st["swa_out"][0]), st


def _mixer_bwd(st, bias, do_sb, do_dil, do_swa, tag):
    d_sb = _sb_bwd(st["sb"], st["sb_tot"], do_sb, name=f"sb_bwd_{tag}")
    dmerge = _dil_merge(st["dil_out"], st["dil_lse"], do_dil, name=f"dil_merge_bwd_{tag}")
    d_dil, dbs = [], []
    for gi, (_, d) in enumerate(DIL_PATTERNS):
        dq, dk, dv, db, _ = _band_bwd(st["dil"][gi], bias[2 * gi:2 * gi + 2], st["dil_sink"][gi], st["dil_out"][gi],
                                      st["dil_lse"][gi], dmerge[gi], dmerge[3 + gi], nq=2 * d, offs=(0, 2 * d, 4 * d),
                                      g=1, bias_div=d, has_sink=False, name=f"dil{gi}_bwd_{tag}")
        d_dil.append((dq, dk, dv))
        dbs.append(db)
    o_sw, l_sw = st["swa_out"]
    dq_sw, dk_sw, dv_sw, db_sw, dsink = _band_bwd(st["swa"], bias[H_DIL:], st["swa_sink"], o_sw, l_sw, do_swa,
                                                  jnp.zeros_like(l_sw), nq=H_SWA_Q, offs=(0, H_SWA_Q, H_SWA_Q + H_SWA_KV),
                                                  g=H_SWA_Q // H_SWA_KV, bias_div=1, has_sink=True, name=f"swa_bwd_{tag}")
    dqkv = _join_heads(d_sb, d_dil, (dq_sw, dk_sw, dv_sw), name=f"join_heads_{tag}")
    return dqkv, jnp.concatenate(dbs + [db_sw], 0), dsink[:, 0, 0]


PIECES = ("ffn0", "mix", "ffn1")


def _ffn_fwd(x_in, w, gain, mod_j, tag, after=None):
    st = {"x": x_in, "w": w}
    st["h"] = _norm_fwd(x_in, _row(gain), _row(mod_j[1]), _row(mod_j[0]), name=f"norm_fwd_{tag}", after=after)
    st["a"], st["u"], st["s"] = _ffn_up(st["h"], w["gate"], w["up"], name=f"up_{tag}")
    st["f"], x_out = _mm(st["s"], w["down"], res=x_in, colscale=_row(0.5 * mod_j[2]), emit_acc=True, tm=512, tn=1024,
                         name=f"down_{tag}")
    return x_out, st


def _ffn_bwd(dx_out, st, gain, mod_j, tag, done):
    w = st["w"]

    def latest(new, old):
        return old if new is None else new

    df, dgate = _gate_bwd(dx_out, st["f"], _row(0.5 * mod_j[2]), 0.5, name=f"gate_bwd_{tag}")
    token = done({"down": _mm_tn(st["s"], df, tm=D_FF // 2, name=f"dwd_{tag}")})
    da, du = _ffn_bwd_ds(df, w["down"], st["a"], st["u"], name=f"ds_{tag}")
    token = latest(done({"gate": _mm_tn(da, st["h"], after=token, tm=D_FF // 2, name=f"dwg_{tag}")}), token)
    token = latest(done({"up": _mm_tn(du, st["h"], after=token, tm=D_FF // 2, name=f"dwu_{tag}")}), token)
    dh = _mm2(da, w["gate"], du, w["up"], after=token, name=f"dh_{tag}")
    dx_in, sum_dh, sum_dhx = _norm_bwd(st["x"], dh, dx_out, _row(gain), _row(mod_j[1]), name=f"norm_bwd_{tag}")
    dmod = jnp.concatenate([sum_dh, gain * sum_dhx, dgate], 0)
    return dx_in, dmod, (1.0 + mod_j[1]) * sum_dhx[0]


def _mix_fwd(x_in, w, gain, mod_j, bias, sinks_l, tag, after=None):
    st = {"x": x_in, "w": w}
    st["h"] = _norm_fwd(x_in, _row(gain), _row(mod_j[1]), _row(mod_j[0]), name=f"norm_fwd_mix_{tag}", after=after)
    qkv = _mm(st["h"], w["in"], tb=True, tm=SEQ, b_rows=(0, D_QKV), name=f"qkv_{tag}")
    st["gates"] = _mm(st["h"], w["in"], tb=True, tm=SEQ, b_rows=(D_QKV, D_GATES), name=f"gates_{tag}")
    outs, st["mix"] = _mixer_fwd(qkv, bias, sinks_l, tag)
    st["merged"], *st["t"] = _merge_fwd(*outs, st["gates"], w["br_sb"], w["br_dil"], w["br_swa"], name=f"merge_fwd_{tag}")
    st["f"], x_out = _mm(st["merged"], w["out"], res=x_in, colscale=_row(mod_j[2]), emit_acc=True, name=f"out_{tag}")
    return x_out, st


def _mix_bwd(dx_out, st, gain, mod_j, bias, tag, done):
    w = st["w"]
    df, dgate = _gate_bwd(dx_out, st["f"], _row(mod_j[2]), 1.0, name=f"gate_bwd_mix_{tag}")
    g = {"out": _mm_tn(st["merged"], df, name=f"dw_out_{tag}")}
    dmerged = _mm(df, w["out"], tb=True, name=f"dmerged_{tag}")
    dgates, do_sb, do_dil, do_swa, dbr_sb, dbr_dil, dbr_swa = _merge_bwd(
        dmerged, *st["t"], st["gates"], w["br_sb"], w["br_dil"], w["br_swa"], name=f"merge_bwd_{tag}")
    g["br_sb"] = _mm_tn(st["t"][0], dbr_sb, name=f"dw_br_sb_{tag}")
    g["br_dil"] = _mm_tn(st["t"][1], dbr_dil, name=f"dw_br_dil_{tag}")
    g["br_swa"] = _mm_tn(st["t"][2], dbr_swa, name=f"dw_br_swa_{tag}")
    dqkv, dbias, dsinks = _mixer_bwd(st["mix"], bias, do_sb, do_dil, do_swa, tag)
    dw_qkv = _mm_tn(dqkv, st["h"], out_rows=D_QKV + D_GATES, name=f"dw_qkv_{tag}")
    g["in"] = _mm_tn(dgates, st["h"], out_rows=D_QKV + D_GATES, row0=D_QKV, prev=dw_qkv, name=f"dw_gates_{tag}")
    dh = _mm2(dqkv, w["in"], dgates, w["in"], after=done(g), tm=512, b_rows=(0, D_QKV), name=f"dh_mix_{tag}")
    dx_in, sum_dh, sum_dhx = _norm_bwd(st["x"], dh, dx_out, _row(gain), _row(mod_j[1]), name=f"norm_bwd_mix_{tag}")
    dmod = jnp.concatenate([sum_dh, gain * sum_dhx, dgate], 0)
    return dx_in, dmod, (1.0 + mod_j[1]) * sum_dhx[0], dbias, dsinks


def _local_step(x, target, mod, gains, weights_of, rel_bias, sinks, final_gain, grads_done):
    tables = jnp.asarray(_bucket_tables())
    bias = _bias_build(rel_bias, tables, name="bias_build")
    states, h = [], x
    for l in range(DEPTH):
        st = {}
        for j, piece in enumerate(PIECES):
            w, after = weights_of(l, piece, h)
            if piece == "mix":
                h, st[piece] = _mix_fwd(h, w, gains[l, j], mod[l, j], bias, sinks[l], f"l{l}", after)
            else:
                h, st[piece] = _ffn_fwd(h, w, gains[l, j], mod[l, j], f"{piece}_l{l}", after)
        states.append(st)
    loss, dx, dfinal = _final_loss(h, target, _row(final_gain), name="final_loss")
    dmods = [[None] * 3 for _ in range(DEPTH)]
    dgains = [[None] * 3 for _ in range(DEPTH)]
    dsinks = [None] * DEPTH
    dbias = None
    for l in reversed(range(DEPTH)):
        for j in reversed(range(3)):
            piece = PIECES[j]
            done = lambda grads, l=l, piece=piece: grads_done(l, piece, grads)
            if piece == "mix":
                dx, dmods[l][j], dgains[l][j], db, dsinks[l] = _mix_bwd(dx, states[l][piece], gains[l, j], mod[l, j], bias, f"l{l}", done)
                dbias = db if dbias is None else dbias + db
            else:
                dx, dmods[l][j], dgains[l][j] = _ffn_bwd(dx, states[l][piece], gains[l, j], mod[l, j], f"{piece}_l{l}", done)
    drel = _bias_grad(dbias, tables, name="bias_grad")[:, 0, :N_BUCKETS].T
    dmod = jnp.stack([jnp.stack(m) for m in dmods])
    dgain = jnp.stack([jnp.stack(g) for g in dgains])
    return loss, dx, dmod, dgain, dfinal[0], drel, jnp.stack(dsinks)


BR_ROWS = (H_SB * HEAD_DIM, 2 * HEAD_DIM, H_SWA_Q * HEAD_DIM)


def _lanes_unshard(g, lead):
    _, rows, _ = g.shape
    r = rows // lead
    return g.reshape(N_DEV, lead, r, LANES).transpose(1, 2, 0, 3).reshape(lead, r, N_DEV * LANES)


def _lanes_shard(full):
    lead, r, _ = full.shape
    return full.reshape(lead, r, N_DEV, LANES).transpose(2, 0, 1, 3).reshape(N_DEV, lead * r, LANES)


def _pack_rows(parts, dtype):
    flat = jnp.concatenate([p.astype(dtype).reshape(-1) for p in parts])
    pad = (-flat.shape[0]) % (16 * LANES)
    if pad:
        flat = jnp.concatenate([flat, jnp.zeros((pad,), dtype)])
    return flat.reshape(-1, LANES)


def _unshard(gathered, axis):
    moved = jnp.moveaxis(gathered, 0, axis)
    shape = list(moved.shape)
    shape[axis:axis + 2] = [shape[axis] * shape[axis + 1]]
    return moved.reshape(shape)


def kernel(x, c, w_ada, b_ada, norm_gain, w_ffn_gate, w_ffn_up, w_ffn_down, w_in, w_br_sb, w_br_dil, w_br_swa, w_out, sinks, rel_bias, final_gain, loss_target, m_w_ada, m_b_ada, m_norm_gain, m_w_ffn_gate, m_w_ffn_up, m_w_ffn_down, m_w_in, m_w_br_sb, m_w_br_dil, m_w_br_swa, m_w_out, m_sinks, m_rel_bias, m_final_gain, v_w_ada, v_b_ada, v_norm_gain, v_w_ffn_gate, v_w_ffn_up, v_w_ffn_down, v_w_in, v_w_br_sb, v_w_br_dil, v_w_br_swa, v_w_out, v_sinks, v_rel_bias, v_final_gain):
    me = 4 * lax.axis_index("x") + 2 * lax.axis_index("y") + lax.axis_index("c")
    d = D_MODEL
    gate_t, up_t, in_t = jnp.swapaxes(w_ffn_gate, 2, 3), jnp.swapaxes(w_ffn_up, 2, 3), jnp.swapaxes(w_in, 1, 2)

    def piece_shards(l, piece):
        bf = lambda t: t.astype(BF16)
        if piece == "mix":
            return [bf(in_t[l]), jnp.concatenate([bf(w_br_sb[l]), bf(w_br_dil[l]), bf(w_br_swa[l])], 0), bf(w_out[l])]
        i = PIECES.index(piece) // 2
        return [bf(gate_t[l, i]), bf(up_t[l, i]), bf(w_ffn_down[l, i])]

    br_off = np.concatenate([[0], np.cumsum(BR_ROWS)])

    def piece_weights(gathered, piece):
        if piece == "mix":
            g_in, g_br, g_out = gathered
            f_br = [_lanes_unshard(g_br[:, br_off[k]:br_off[k + 1]], 1)[0] for k in range(3)]
            return {"in": g_in.reshape(D_QKV + D_GATES, d), "br_sb": f_br[0], "br_dil": f_br[1], "br_swa": f_br[2],
                    "out": g_out.reshape(d, d)}
        return {n: g.reshape(D_FF, d) for n, g in zip(("gate", "up", "down"), gathered)}

    small, = _all_gather([_pack_rows([c, norm_gain], F32)], name="gather_cond")
    c_all = small[:, :d // LANES].reshape(N_DEV, d)
    gains = _unshard(small[:, d // LANES:d // LANES + 6].reshape(N_DEV, DEPTH, 3, LANES), 2)

    cols = w_ada.shape[2]
    mod_cols = jnp.stack([_ada_fwd(c_all, w_ada[l], name=f"ada_fwd_l{l}") for l in range(DEPTH)])
    mod_all, = _all_gather([_pack_rows([mod_cols], F32)], name="gather_mod")
    mod_all = mod_all.reshape(N_DEV, -1)[:, :DEPTH * N_DEV * cols].reshape(N_DEV, DEPTH, N_DEV, cols)
    mod_mine = lax.dynamic_index_in_dim(mod_all, me, axis=2, keepdims=False)
    mod = (mod_mine.transpose(1, 0, 2).reshape(DEPTH, N_DEV * cols) + b_ada).reshape(DEPTH, 3, 3, d)

    order = [(l, piece) for l in range(DEPTH) for piece in PIECES]
    ahead = 3
    in_flight, passed = {}, {}
    first = _all_gather(piece_shards(*order[0]), after=mod_all, name="gather_first")

    def start_gather(k, after):
        l, piece = order[k]
        in_flight[k], token = _relay_start(piece_shards(l, piece), after, name=f"gather_{piece}_l{l}_start")
        return token

    token = first[0]
    for k in range(1, 1 + ahead):
        token = start_gather(k, token)
    mod = mod + token[0, 0]

    def weights_of(l, piece, h):
        k = order.index((l, piece))
        token = start_gather(k + ahead, h) if k + ahead < len(order) and k + ahead not in in_flight else None
        if k + 1 < len(order):
            nl, npiece = order[k + 1]
            passed[k + 1], token = _relay_pass(in_flight[k + 1], h if token is None else token,
                                               name=f"gather_{npiece}_l{nl}_pass")
        if k == 0:
            return piece_weights(first, piece), token
        landed = _relay_wait(passed[k], h if token is None else token, name=f"gather_{piece}_l{l}_wait")
        return piece_weights(landed, piece), token

    exchanges, have, deferred = {}, {}, []

    def grads_done(l, piece, g):
        key = (l, piece)
        have.setdefault(key, {}).update(g)
        if piece == "mix":
            if len(have[key]) < 5:
                return None
            g = have[key]
            s_br = jnp.concatenate([_lanes_shard(g[n][None]) for n in ("br_sb", "br_dil", "br_swa")], 1)
            groups = [(("in", "br", "out"), [g["in"].reshape(N_DEV, -1, d), s_br, g["out"].reshape(N_DEV, -1, d)])]
        elif key == order[0]:
            groups = [((n,), [t.reshape(N_DEV, -1, d)]) for n, t in g.items()]
            if "down" not in g:
                deferred.extend(groups)
                return None
        elif len(have[key]) < 3:
            return None
        else:
            groups = [(("gate", "up", "down"), [have[key][n].reshape(N_DEV, -1, d) for n in ("gate", "up", "down")])]
        token = None
        for names, sg in groups:
            state, token = _exchange_start(sg, None, gather=False, name=f"exchange_{piece}_l{l}_{names[0]}_start")
            exchanges.setdefault(key, []).append((names, state))
        return token

    loss, dx, dmod, dgains, dfinal, drel, dsinks = _local_step(
        x[0], loss_target[0], mod, gains, weights_of, rel_bias, sinks, final_gain, grads_done)

    flat = lambda t: t.reshape(-1, t.shape[-1])
    transposed = lambda ts: tuple(flat(jnp.swapaxes(t, -1, -2)) for t in ts)
    families = {
        "gate": transposed((w_ffn_gate, m_w_ffn_gate, v_w_ffn_gate)), "up": transposed((w_ffn_up, m_w_ffn_up, v_w_ffn_up)),
        "down": tuple(flat(t) for t in (w_ffn_down, m_w_ffn_down, v_w_ffn_down)),
        "in": transposed((w_in, m_w_in, v_w_in)),
        "br": tuple(flat(jnp.concatenate(ts, 1)) for ts in ((w_br_sb, w_br_dil, w_br_swa), (m_w_br_sb, m_w_br_dil, m_w_br_swa),
                                                            (v_w_br_sb, v_w_br_dil, v_w_br_swa))),
        "out": tuple(flat(t) for t in (w_out, m_w_out, v_w_out))}
    parts, stepped = {}, {}

    def land(l, after):
        for key in reversed([k for k in order if k[0] == l]):
            for names, ex_state in exchanges[key]:
                landed = _exchange_wait(ex_state, after, gather=False, name=f"exchange_{key[1]}_l{key[0]}_{names[0]}_wait")
                parts.setdefault(key, {}).update(zip(names, landed))
                after = landed[0]

    def step_layer(l):
        last = None
        for n, (w2, m2, v2) in families.items():
            groups = [parts[key][n] for key in order if key[0] == l and n in parts[key]]
            rows_per_layer = w2.shape[0] // DEPTH
            stepped[n] = _reduce_adamw(groups, w2, m2, v2, l * rows_per_layer, stepped.get(n), after=last,
                                       name=f"reduce_adamw_{n}_l{l}")
            last = stepped[n][1]
        return last

    small_parts = [dmod, dgains, dfinal, drel.T, dsinks, loss[0, :1]]
    small_sizes = [int(np.prod(p.shape)) for p in small_parts]
    small_all, = _all_gather([_pack_rows(small_parts, F32)], name="gather_small")
    token = small_all
    for names, sg in deferred:
        state, token = _exchange_start(sg, token, gather=False, name=f"exchange_ffn0_l0_{names[0]}_start")
        exchanges.setdefault(order[0], []).append((names, state))

    land(1, token)
    after_l1 = step_layer(1)
    small_sum = _sum_parts([small_all], name="sum_small").reshape(-1)
    offs = np.concatenate([[0], np.cumsum(small_sizes)])
    g_b_ada = small_sum[offs[0]:offs[1]].reshape(DEPTH, 9 * d)
    g_gain_full = small_sum[offs[1]:offs[2]].reshape(DEPTH, 3, d)
    g_norm_gain = lax.dynamic_slice_in_dim(g_gain_full, me * LANES, LANES, axis=2)
    g_final = small_sum[offs[2]:offs[3]]
    g_rel = small_sum[offs[3]:offs[4]].reshape(N_SOFT, N_BUCKETS).T
    g_sinks = small_sum[offs[4]:offs[5]].reshape(DEPTH, H_SWA_Q)
    loss_total = small_sum[offs[5]]

    dmod_all = small_all.reshape(N_DEV, -1)[:, :DEPTH * 9 * d].reshape(N_DEV, DEPTH, 9 * d)
    dmod_cols = lax.dynamic_slice_in_dim(dmod_all, me * cols, cols, axis=2)
    g_w_ada = jnp.stack([_ada_bwd(c_all.T, dmod_cols[:, l], name=f"ada_bwd_l{l}") for l in range(DEPTH)])

    small_state = {"w_ada": (w_ada, m_w_ada, v_w_ada), "b_ada": (b_ada, m_b_ada, v_b_ada),
                   "norm_gain": (norm_gain, m_norm_gain, v_norm_gain), "sinks": (sinks, m_sinks, v_sinks),
                   "rel_bias": (rel_bias, m_rel_bias, v_rel_bias), "final_gain": (final_gain, m_final_gain, v_final_gain)}
    grad, update = {}, {}
    for n, g in (("w_ada", g_w_ada), ("b_ada", g_b_ada), ("norm_gain", g_norm_gain), ("sinks", g_sinks),
                 ("rel_bias", g_rel), ("final_gain", g_final)):
        w, m, v = small_state[n]
        grad[n] = g
        if w.ndim == 1:
            update[n] = tuple(t.reshape(w.shape) for t in _adamw(_row(w), _row(g), _row(m), _row(v), name=f"adamw_{n}"))
        else:
            update[n] = _adamw(w, g, m, v, name=f"adamw_{n}")

    land(0, after_l1)
    step_layer(0)

    def unflat(n, like, swapped):
        shape = jnp.swapaxes(like, -1, -2).shape if swapped else like.shape
        out = [t.reshape(shape) for t in stepped[n]]
        return [jnp.swapaxes(t, -1, -2) for t in out] if swapped else out

    results = {"w_ffn_gate": unflat("gate", w_ffn_gate, True), "w_ffn_up": unflat("up", w_ffn_up, True),
               "w_ffn_down": unflat("down", w_ffn_down, False), "w_in": unflat("in", w_in, True),
               "w_out": unflat("out", w_out, False)}
    br = [t.reshape(DEPTH, -1, LANES) for t in stepped["br"]]
    for k, n in enumerate(("w_br_sb", "w_br_dil", "w_br_swa")):
        results[n] = [t[:, br_off[k]:br_off[k + 1]] for t in br]
    for n, (g, dl, nm, nv) in results.items():
        grad[n], update[n] = g, (dl, nm, nv)

    names = ["w_ada", "b_ada", "norm_gain", "w_ffn_gate", "w_ffn_up", "w_ffn_down", "w_in", "w_br_sb", "w_br_dil",
             "w_br_swa", "w_out", "sinks", "rel_bias", "final_gain"]
    return (loss_total, dx[None], *[grad[n] for n in names], *[update[n][0] for n in names],
            *[update[n][1] for n in names], *[update[n][2] for n in names])
```

```python
import math

import numpy as np
import jax
import jax.numpy as jnp
from jax import lax
from jax.experimental import pallas as pl
from jax.experimental.pallas import tpu as pltpu

F32, BF16 = jnp.float32, jnp.bfloat16

SEQ, D_MODEL, D_FF, HEAD_DIM = 2048, 1024, 2816, 64
DEPTH = 2
BLK = 128
H_SB, H_DIL, H_SWA_Q, H_SWA_KV = 4, 6, 6, 2
DIL_PATTERNS = ((128, 1), (512, 4), (2048, 16))
SWA_WINDOW = 128
N_BUCKETS, MAX_REL_DIST = 32, 2048
RMS_EPS = 1e-6
D_QKV = 2560
D_GATES = 3 * D_MODEL
ADAM_LR, ADAM_B1, ADAM_B2, ADAM_EPS, ADAM_WD, ADAM_STEP = 0.001, 0.9, 0.999, 1e-08, 0.01, 10

N_DEV = 8
LANES = 128
NEG = -1e30
SB_TILE = 512
VMEM_LIMIT_BYTES = 48 * 1024 * 1024
HBM = pl.BlockSpec(memory_space=pltpu.HBM)
MESH = pl.DeviceIdType.MESH


def _tile(n, target):
    t = (min(n, target) // LANES) * LANES
    while t >= LANES:
        if n % t == 0:
            return t
        t -= LANES
    return n


def _row_tile(r, cap):
    t = (min(r, cap) // 16) * 16
    while t > 16 and r % t:
        t -= 16
    return t


def _params(semantics=None):
    return pltpu.CompilerParams(dimension_semantics=semantics, vmem_limit_bytes=VMEM_LIMIT_BYTES)


def _dot(a, b, ca, cb):
    return lax.dot_general(a, b, (((ca,), (cb,)), ((), ())), preferred_element_type=F32)


def _sigmoid(a):
    return 1.0 / (1.0 + jnp.exp(-a))


def _row(v):
    return v.reshape(1, -1)


def _all_gather(arrs, name, after=None):
    n = len(arrs)
    ins = list(arrs) + ([] if after is None else [after])

    def body(*refs):
        x_refs, out_refs = refs[:n], refs[len(ins):len(ins) + n]
        send_sems, recv_sems, local_sems = refs[len(ins) + n:]
        x, y, c = lax.axis_index("x"), lax.axis_index("y"), lax.axis_index("c")
        me, sibling = (x, y, c), (x, y, 1 - c)
        chips = [(1 - x, y), (x, 1 - y), (1 - x, 1 - y)]

        def slot(t, px, py, pc):
            return out_refs[t].at[4 * px + 2 * py + pc]

        def copy(t, k, block, to, src=None):
            return pltpu.make_async_remote_copy(
                src_ref=slot(t, *block) if src is None else src, dst_ref=slot(t, *block),
                send_sem=send_sems.at[7 * t + k], recv_sem=recv_sems.at[7 * t + k], device_id=to, device_id_type=MESH)

        mine = [pltpu.make_async_copy(x_refs[t], slot(t, *me), local_sems.at[t]) for t in range(n)]
        for cp in mine:
            cp.start()
        first = []
        for t in range(n):
            first.append(copy(t, 0, me, sibling, src=x_refs[t]))
            first += [copy(t, 1 + j, me, (*chip, c), src=x_refs[t]) for j, chip in enumerate(chips)]
        for cp in first:
            cp.start()
        passed = []
        for j, chip in enumerate(chips):
            for t in range(n):
                copy(t, 1 + j, (*chip, c), me).wait_recv()
                passed.append(copy(t, 4 + j, (*chip, c), sibling))
                passed[-1].start()
        for t in range(n):
            copy(t, 0, sibling, me).wait_recv()
        for j, chip in enumerate(chips):
            for t in range(n):
                copy(t, 4 + j, (*chip, 1 - c), me).wait_recv()
        for cp in first + passed:
            cp.wait_send()
        for cp in mine:
            cp.wait()

    return pl.pallas_call(
        body, name=name, out_shape=[jax.ShapeDtypeStruct((N_DEV,) + a.shape, a.dtype) for a in arrs],
        in_specs=[HBM] * n + [pl.BlockSpec(memory_space=pl.ANY)] * (len(ins) - n), out_specs=[HBM] * n,
        scratch_shapes=[pltpu.SemaphoreType.DMA((7 * n,)), pltpu.SemaphoreType.DMA((7 * n,)), pltpu.SemaphoreType.DMA((n,))],
    )(*ins)


def _direct_copies(x_refs, land_refs, send_sems, recv_sems, local_sems, gather):
    x, y, c = lax.axis_index("x"), lax.axis_index("y"), lax.axis_index("c")
    me = 4 * x + 2 * y + c
    sends, recvs = [], []
    for k in range(1, N_DEV):
        px = 1 - x if (k >> 2) & 1 else x
        py = 1 - y if (k >> 1) & 1 else y
        pc = 1 - c if k & 1 else c
        peer = 4 * px + 2 * py + pc
        for t, (x_ref, land_ref) in enumerate(zip(x_refs, land_refs)):
            sem = 7 * t + k - 1
            for out, src, slot in ((sends, x_ref if gather else x_ref.at[peer], me),
                                   (recvs, x_ref if gather else x_ref.at[me], peer)):
                out.append(pltpu.make_async_remote_copy(
                    src_ref=src, dst_ref=land_ref.at[slot], send_sem=send_sems.at[sem], recv_sem=recv_sems.at[sem],
                    device_id=(px, py, pc), device_id_type=MESH))
    own = [pltpu.make_async_copy(x_ref if gather else x_ref.at[me], land_ref.at[me], local_sems.at[t])
           for t, (x_ref, land_ref) in enumerate(zip(x_refs, land_refs))]
    return sends, recvs, own


SEM =pl.BlockSpec(memory_space=pltpu.SEMAPHORE)
ANY = pl.BlockSpec(memory_space=pl.ANY)
SIDE_EFFECT = pltpu.SideEffectType.DATAFLOW_SIDE_EFFECTING


def _exchange_start(arrs, after, *, gather, name):
    n = len(arrs)
    lands = [lax.empty(((N_DEV,) + a.shape) if gather else a.shape, a.dtype) for a in arrs]
    extra = [] if after is None else [after]

    def body(*refs):
        sems = refs[2 * n + len(extra):2 * n + len(extra) + 3]
        sends, _, own = _direct_copies(refs[:n], refs[n:2 * n], *sems, gather)
        for cp in own + sends:
            cp.start()
        refs[-1][...] = jnp.zeros_like(refs[-1])

    ops = [pltpu.with_memory_space_constraint(a, pltpu.HBM) for a in list(arrs) + lands]
    out = pl.pallas_call(
        body, name=name,
        out_shape=(pltpu.SemaphoreType.DMA((7 * n,)), pltpu.SemaphoreType.DMA((7 * n,)), pltpu.SemaphoreType.DMA((n,)),
                   *[pltpu.HBM(a.shape, a.dtype) for a in ops], jax.ShapeDtypeStruct((8, LANES), F32)),
        in_specs=[HBM] * (2 * n) + [ANY] * len(extra),
        out_specs=(SEM, SEM, SEM, *[HBM] * (2 * n), pl.BlockSpec(memory_space=pltpu.VMEM)),
        input_output_aliases={t: 3 + t for t in range(2 * n)},
        compiler_params=pltpu.CompilerParams(has_side_effects=SIDE_EFFECT),
    )(*ops, *extra)
    return (out[:3], out[3:3 + n], out[3 + n:3 + 2 * n]), out[-1]


def _exchange_wait(state, after, *, gather, name):
    sems, arrs, lands = state
    n = len(arrs)

    def body(*refs):
        sends, recvs, own = _direct_copies(refs[:n], refs[n:2 * n], *refs[2 * n:2 * n + 3], gather)
        for cp in own:
            cp.wait()
        for cp in sends:
            cp.wait_send()
        for cp in recvs:
            cp.wait_recv()

    out = pl.pallas_call(
        body, name=name, out_shape=tuple(pltpu.HBM(a.shape, a.dtype) for a in list(arrs) + list(lands)),
        in_specs=[HBM] * (2 * n) + [SEM, SEM, SEM, ANY], out_specs=tuple([HBM] * (2 * n)),
        input_output_aliases={t: t for t in range(2 * n)},
        compiler_params=pltpu.CompilerParams(has_side_effects=SIDE_EFFECT),
    )(*arrs, *lands, *sems, after)
    return out[n:]


def _relay_copies(x_refs, land_refs, sems_a, sems_b):
    x, y, c = lax.axis_index("x"), lax.axis_index("y"), lax.axis_index("c")
    me = 4 * x + 2 * y + c
    sibling = (x, y, 1 - c)
    chips = [(1 - x, y), (x, 1 - y), (1 - x, 1 - y)]

    def slot(px, py, pc):
        return 4 * px + 2 * py + pc

    def copy(src, land_ref, dst_slot, send_sems, recv_sems, k, to):
        return pltpu.make_async_remote_copy(src_ref=src, dst_ref=land_ref.at[dst_slot], send_sem=send_sems.at[k],
                                            recv_sem=recv_sems.at[k], device_id=to, device_id_type=MESH)

    a_send, a_recv, a_own, b_send, b_recv = [], [], [], [], []
    for t, (x_ref, land_ref) in enumerate(zip(x_refs, land_refs)):
        peers = [sibling] + [(*chip, c) for chip in chips]
        if sems_a is not None:
            for k, peer in enumerate(peers):
                a_send.append(copy(x_ref, land_ref, me, sems_a[0], sems_a[1], 4 * t + k, peer))
                a_recv.append(copy(x_ref, land_ref, slot(*peer), sems_a[0], sems_a[1], 4 * t + k, peer))
            a_own.append(pltpu.make_async_copy(x_ref, land_ref.at[me], sems_a[2].at[t]))
        if sems_b is not None:
            for j, chip in enumerate(chips):
                b_send.append(copy(land_ref.at[slot(*chip, c)], land_ref, slot(*chip, c), sems_b[0], sems_b[1], 3 * t + j, sibling))
                b_recv.append(copy(land_ref.at[slot(*chip, c)], land_ref, slot(*chip, 1 - c), sems_b[0], sems_b[1], 3 * t + j,
                                   sibling))
    return (a_send, a_recv, a_own), (b_send, b_recv)


def _relay_start(arrs, after, name):
    n = len(arrs)
    lands = [lax.empty((N_DEV,) + a.shape, a.dtype) for a in arrs]

    def body(*refs):
        (sends, _, own), _ = _relay_copies(refs[:n], refs[n:2 * n], refs[2 * n + 1:2 * n + 4], None)
        for cp in own + sends:
            cp.start()
        refs[-1][...] = jnp.zeros_like(refs[-1])

    ops = [pltpu.with_memory_space_constraint(a, pltpu.HBM) for a in list(arrs) + lands]
    out = pl.pallas_call(
        body, name=name,
        out_shape=(pltpu.SemaphoreType.DMA((4 * n,)), pltpu.SemaphoreType.DMA((4 * n,)), pltpu.SemaphoreType.DMA((n,)),
                   *[pltpu.HBM(a.shape, a.dtype) for a in ops], jax.ShapeDtypeStruct((8, LANES), F32)),
        in_specs=[HBM] * (2 * n) + [ANY],
        out_specs=(SEM, SEM, SEM, *[HBM] * (2 * n), pl.BlockSpec(memory_space=pltpu.VMEM)),
        input_output_aliases={t: 3 + t for t in range(2 * n)},
        compiler_params=pltpu.CompilerParams(has_side_effects=SIDE_EFFECT),
    )(*ops, after)
    return (out[:3], out[3:3 + n], out[3 + n:3 + 2 * n]), out[-1]


def _relay_pass(state, after, name):
    sems_a, arrs, lands = state
    n = len(arrs)

    def body(*refs):
        sems_b = refs[2 * n + 4:2 * n + 6]
        (a_send, a_recv, a_own), (b_send, _) = _relay_copies(refs[:n], refs[n:2 * n], refs[2 * n:2 * n + 3], sems_b)
        for cp in a_own:
            cp.wait()
        for cp in a_send:
            cp.wait_send()
        for cp in a_recv:
            cp.wait_recv()
        for cp in b_send:
            cp.start()
        refs[-1][...] = jnp.zeros_like(refs[-1])

    out = pl.pallas_call(
        body, name=name,
        out_shape=(pltpu.SemaphoreType.DMA((3 * n,)), pltpu.SemaphoreType.DMA((3 * n,)),
                   *[pltpu.HBM(a.shape, a.dtype) for a in list(arrs) + list(lands)], jax.ShapeDtypeStruct((8, LANES), F32)),
        in_specs=[HBM] * (2 * n) + [SEM, SEM, SEM, ANY],
        out_specs=(SEM, SEM, *[HBM] * (2 * n), pl.BlockSpec(memory_space=pltpu.VMEM)),
        input_output_aliases={t: 2 + t for t in range(2 * n)},
        compiler_params=pltpu.CompilerParams(has_side_effects=SIDE_EFFECT),
    )(*arrs, *lands, *sems_a, after)
    return (out[:2], out[2:2 + n], out[2 + n:2 + 2 * n]), out[-1]


def _relay_wait(state, after, name):
    sems_b, arrs, lands = state
    n = len(arrs)

    def body(*refs):
        _, (b_send, b_recv) = _relay_copies(refs[:n], refs[n:2 * n], None, refs[2 * n:2 * n + 2])
        for cp in b_send:
            cp.wait_send()
        for cp in b_recv:
            cp.wait_recv()

    out = pl.pallas_call(
        body, name=name, out_shape=tuple(pltpu.HBM(a.shape, a.dtype) for a in list(arrs) + list(lands)),
        in_specs=[HBM] * (2 * n) + [SEM, SEM, ANY], out_specs=tuple([HBM] * (2 * n)),
        input_output_aliases={t: t for t in range(2 * n)},
        compiler_params=pltpu.CompilerParams(has_side_effects=SIDE_EFFECT),
    )(*arrs, *lands, *sems_b, after)
    return out[n:]


def _sum_parts(groups, name):
    n, r, cdim = groups[0].shape
    tr = _row_tile(r, max(16, (1 << 21) // (n * cdim * groups[0].dtype.itemsize)))
    steps = r // tr

    def body(*refs):
        o_ref = refs[-1]
        gg = pl.program_id(0)
        for gi in range(len(groups)):
            @pl.when(gg == gi)
            def _(gi=gi):
                acc = refs[gi][0].astype(F32)
                for k in range(1, n):
                    acc = acc + refs[gi][k].astype(F32)
                o_ref[...] = acc

    def in_spec(gi):
        return pl.BlockSpec((n, tr, cdim), lambda gg, i: (0, jnp.where(gg == gi, i, 0), 0))

    return pl.pallas_call(
        body, name=name, out_shape=jax.ShapeDtypeStruct((len(groups) * r, cdim), F32), grid=(len(groups), steps),
        in_specs=[in_spec(gi) for gi in range(len(groups))],
        out_specs=pl.BlockSpec((tr, cdim), lambda gg, i: (gg * steps + i, 0)),
        compiler_params=_params(("parallel", "parallel")),
    )(*groups)


def _mm_tn(a, b, *, name, after=None, tm=512, tn=1024, out_rows=None, row0=0, prev=None):
    k, m = a.shape
    n = b.shape[1]
    tm, tn = _tile(m, tm), _tile(n, tn)
    out_rows = m if out_rows is None else out_rows

    def body(a_ref, b_ref, *rest):
        o_ref, at_ref = rest[-2], rest[-1]

        @pl.when(pl.program_id(1) == 0)
        def _():
            at_ref[...] = a_ref[...].astype(BF16).T

        o_ref[...] = _dot(at_ref[...], b_ref[...].astype(BF16), 1, 0).astype(BF16)

    ins = [a, b] + [t for t in (after, prev) if t is not None]
    return pl.pallas_call(
        body, name=name, out_shape=jax.ShapeDtypeStruct((out_rows, n), BF16), grid=(m // tm, n // tn),
        in_specs=[pl.BlockSpec((k, tm), lambda i, j: (0, i)), pl.BlockSpec((k, tn), lambda i, j: (0, j))] + [ANY] * (len(ins) - 2),
        out_specs=pl.BlockSpec((tm, tn), lambda i, j: (row0 // tm + i, j)),
        input_output_aliases={} if prev is None else {len(ins) - 1: 0},
        scratch_shapes=[pltpu.VMEM((tm, k), BF16)], compiler_params=_params(("parallel", "arbitrary")),
    )(*ins)


def _mm2(a1, b1, a2, b2, *, name, after=None, tm=256, tn=1024, b_rows=None):
    m = a1.shape[0]
    n = b1.shape[1]
    tm, tn = _tile(m, tm), _tile(n, tn)

    def body(a1_ref, b1_ref, a2_ref, b2_ref, *rest):
        rest[-1][...] = (_dot(a1_ref[...].astype(BF16), b1_ref[...], 1, 0)
                         + _dot(a2_ref[...].astype(BF16), b2_ref[...], 1, 0))

    ins = [a1, b1, a2, b2] + ([] if after is None else [after])

    def a_spec(t):
        return pl.BlockSpec((tm, t.shape[1]), lambda i, j: (i, 0))

    def b_spec(t, a, which):
        if b_rows is None:
            return pl.BlockSpec((t.shape[0], tn), lambda i, j: (0, j))
        start = b_rows[which]
        return pl.BlockSpec((pl.Element(a.shape[1]), pl.Element(tn)), lambda i, j: (start, j * tn))

    return pl.pallas_call(
        body, name=name, out_shape=jax.ShapeDtypeStruct((m, n), F32), grid=(m // tm, n // tn),
        in_specs=[a_spec(a1), b_spec(b1, a1, 0), a_spec(a2), b_spec(b2, a2, 1)] + [ANY] * (len(ins) - 4),
        out_specs=pl.BlockSpec((tm, tn), lambda i, j: (i, j)), compiler_params=_params(("parallel", "parallel")),
    )(*ins)


def _mm(a, b, *, name, ta=False, tb=False, res=None, colscale=None, emit_acc=False,
        out_dtype=F32, tm=512, tn=512, b_rows=None):
    m, k = (a.shape[1], a.shape[0]) if ta else a.shape
    n = b.shape[0] if tb else b.shape[1]
    b_start = 0
    if b_rows is not None:
        b_start, n = b_rows
    tm, tn = _tile(m, tm), _tile(n, tn)
    ca, cb = (0 if ta else 1), (1 if tb else 0)
    a_spec = pl.BlockSpec((k, tm), lambda i, j: (0, i)) if ta else pl.BlockSpec((tm, k), lambda i, j: (i, 0))
    b_spec = (pl.BlockSpec((tn, k), lambda i, j: (b_start // tn + j, 0)) if tb
              else pl.BlockSpec((k, tn), lambda i, j: (0, j)))
    tile = pl.BlockSpec((tm, tn), lambda i, j: (i, j))
    ins, in_specs = [a, b], [a_spec, b_spec]
    if res is not None:
        ins.append(res)
        in_specs.append(tile)
    if colscale is not None:
        ins.append(colscale)
        in_specs.append(pl.BlockSpec((1, tn), lambda i, j: (0, j)))
    n_in = len(ins)

    def body(*refs):
        outs = refs[n_in:]
        acc = _dot(refs[0][...].astype(BF16), refs[1][...].astype(BF16), ca, cb)
        val, p = acc, 2
        if res is not None:
            r_val, p = refs[p][...], p + 1
        if colscale is not None:
            val = val * refs[p][...]
        if res is not None:
            val = r_val + val
        if emit_acc:
            outs[0][...] = acc
        outs[-1][...] = val.astype(out_dtype)

    out_shape = [jax.ShapeDtypeStruct((m, n), out_dtype)]
    out_specs = [tile]
    if emit_acc:
        out_shape.insert(0, jax.ShapeDtypeStruct((m, n), F32))
        out_specs.insert(0, tile)
    out = pl.pallas_call(
        body, name=name, out_shape=out_shape, grid=(m // tm, n // tn), in_specs=in_specs, out_specs=out_specs,
        compiler_params=_params(("parallel", "parallel")),
    )(*ins)
    return out if emit_acc else out[0]


def _norm_fwd(x, g, scale, shift, name, after=None):
    s, d = x.shape
    tr = 256

    def body(x_ref, g_ref, sc_ref, sh_ref, *rest):
        xv = x_ref[...]
        rstd = lax.rsqrt(jnp.mean(xv * xv, axis=-1, keepdims=True) + RMS_EPS)
        rest[-1][...] = (xv * rstd * g_ref[...] * (1.0 + sc_ref[...]) + sh_ref[...]).astype(BF16)

    rowspec = pl.BlockSpec((1, d), lambda i: (0, 0))
    ins = [x, g, scale, shift] + ([] if after is None else [after])
    return pl.pallas_call(
        body, name=name, out_shape=jax.ShapeDtypeStruct((s, d), BF16), grid=(s // tr,),
        in_specs=[pl.BlockSpec((tr, d), lambda i: (i, 0)), rowspec, rowspec, rowspec] + [ANY] * (len(ins) - 4),
        out_specs=pl.BlockSpec((tr, d), lambda i: (i, 0)),
        compiler_params=_params(("parallel",)),
    )(*ins)


def _norm_bwd(x, dh, dres, g, scale, name):
    s, d = x.shape
    tr = 256

    def body(x_ref, dh_ref, dr_ref, g_ref, sc_ref, dx_ref, a_ref, b_ref):
        @pl.when(pl.program_id(0) == 0)
        def _():
            a_ref[...] = jnp.zeros_like(a_ref)
            b_ref[...] = jnp.zeros_like(b_ref)

        xv = x_ref[...]
        rstd = lax.rsqrt(jnp.mean(xv * xv, axis=-1, keepdims=True) + RMS_EPS)
        xhat = xv * rstd
        dhv = dh_ref[...]
        dxhat = dhv * (g_ref[...] * (1.0 + sc_ref[...]))
        mean_term = jnp.mean(dxhat * xhat, axis=-1, keepdims=True)
        dx_ref[...] = dr_ref[...] + rstd * (dxhat - xhat * mean_term)
        a_ref[...] += jnp.sum(dhv, axis=0, keepdims=True)
        b_ref[...] += jnp.sum(dhv * xhat, axis=0, keepdims=True)

    rowspec = pl.BlockSpec((1, d), lambda i: (0, 0))
    tile = pl.BlockSpec((tr, d), lambda i: (i, 0))
    return pl.pallas_call(
        body, name=name,
        out_shape=[jax.ShapeDtypeStruct((s, d), F32), jax.ShapeDtypeStruct((1, d), F32), jax.ShapeDtypeStruct((1, d), F32)],
        grid=(s // tr,), in_specs=[tile, tile, tile, rowspec, rowspec], out_specs=[tile, rowspec, rowspec],
        compiler_params=_params(("arbitrary",)),
    )(x, dh, dres, g, scale)


def _gate_bwd(dxn, f, colscale, coef, name):
    s, d = dxn.shape
    tr = 256

    def body(dx_ref, f_ref, cs_ref, df_ref, dg_ref):
        @pl.when(pl.program_id(0) == 0)
        def _():
            dg_ref[...] = jnp.zeros_like(dg_ref)

        dxv = dx_ref[...]
        df_ref[...] = (dxv * cs_ref[...]).astype(BF16)
        dg_ref[...] += coef * jnp.sum(dxv * f_ref[...], axis=0, keepdims=True)

    rowspec = pl.BlockSpec((1, d), lambda i: (0, 0))
    tile = pl.BlockSpec((tr, d), lambda i: (i, 0))
    return pl.pallas_call(
        body, name=name, out_shape=[jax.ShapeDtypeStruct((s, d), BF16), jax.ShapeDtypeStruct((1, d), F32)],
        grid=(s // tr,), in_specs=[tile, tile, rowspec], out_specs=[tile, rowspec],
        compiler_params=_params(("arbitrary",)),
    )(dxn, f, colscale)


def _ffn_up(h, wg, wu, name, tm=SEQ, tn=256):
    s, d = h.shape
    f = wg.shape[0]

    def body(h_ref, wg_ref, wu_ref, a_ref, u_ref, s_ref):
        hv = h_ref[...]
        a = _dot(hv, wg_ref[...], 1, 1)
        u = _dot(hv, wu_ref[...], 1, 1)
        a_ref[...] = a.astype(BF16)
        u_ref[...] = u.astype(BF16)
        s_ref[...] = (a * _sigmoid(a) * u).astype(BF16)

    tile = pl.BlockSpec((tm, tn), lambda i, j: (i, j))
    wspec = pl.BlockSpec((tn, d), lambda i, j: (j, 0))
    return pl.pallas_call(
        body, name=name,
        out_shape=[jax.ShapeDtypeStruct((s, f), BF16), jax.ShapeDtypeStruct((s, f), BF16), jax.ShapeDtypeStruct((s, f), BF16)],
        grid=(s // tm, f // tn), in_specs=[pl.BlockSpec((tm, d), lambda i, j: (i, 0)), wspec, wspec],
        out_specs=[tile, tile, tile], compiler_params=_params(("parallel", "parallel")),
    )(h, wg, wu)


def _ffn_bwd_ds(df, wd, a, u, name, tm=SEQ, tn=256):
    s, d = df.shape
    f = wd.shape[0]

    def body(df_ref, wd_ref, a_ref, u_ref, da_ref, du_ref):
        ds = _dot(df_ref[...], wd_ref[...], 1, 1)
        av = a_ref[...].astype(F32)
        sg = _sigmoid(av)
        da_ref[...] = (ds * u_ref[...].astype(F32) * (sg * (1.0 + av * (1.0 - sg)))).astype(BF16)
        du_ref[...] = (ds * (av * sg)).astype(BF16)

    tile = pl.BlockSpec((tm, tn), lambda i, j: (i, j))
    return pl.pallas_call(
        body, name=name, out_shape=[jax.ShapeDtypeStruct((s, f), BF16), jax.ShapeDtypeStruct((s, f), BF16)],
        grid=(s // tm, f // tn),
        in_specs=[pl.BlockSpec((tm, d), lambda i, j: (i, 0)), pl.BlockSpec((tn, d), lambda i, j: (j, 0)), tile, tile],
        out_specs=[tile, tile], compiler_params=_params(("parallel", "parallel")),
    )(df, wd, a, u)


def _merge_fwd(o_sb, o_dil, o_swa, gates, wb_sb, wb_dil, wb_swa, name):
    s, d = SEQ, D_MODEL
    tm = 256

    def body(osb_ref, odl_ref, osw_ref, g_ref, wsb_ref, wdl_ref, wsw_ref, m_ref, tsb_ref, tdl_ref, tsw_ref):
        for h in range(osb_ref.shape[0]):
            tsb_ref[:, h * HEAD_DIM:(h + 1) * HEAD_DIM] = osb_ref[h].astype(BF16)
        for h in range(osw_ref.shape[0]):
            tsw_ref[:, h * HEAD_DIM:(h + 1) * HEAD_DIM] = osw_ref[h].astype(BF16)
        tdl_ref[...] = odl_ref[...].astype(BF16)
        acc = _sigmoid(g_ref[:, 0:d]) * _dot(tsb_ref[...], wsb_ref[...], 1, 0)
        acc += _sigmoid(g_ref[:, d:2 * d]) * _dot(tdl_ref[...], wdl_ref[...], 1, 0)
        acc += _sigmoid(g_ref[:, 2 * d:3 * d]) * _dot(tsw_ref[...], wsw_ref[...], 1, 0)
        m_ref[...] = acc.astype(BF16)

    def rows(w):
        return pl.BlockSpec((tm, w), lambda i: (i, 0))

    def heads(n):
        return pl.BlockSpec((n, tm, HEAD_DIM), lambda i: (0, i, 0))

    def whole(w):
        return pl.BlockSpec((w, d), lambda i: (0, 0))

    return pl.pallas_call(
        body, name=name, out_shape=[jax.ShapeDtypeStruct((s, w), BF16) for w in (d, 256, 128, 384)], grid=(s // tm,),
        in_specs=[heads(H_SB), rows(128), heads(H_SWA_Q), rows(3 * d), whole(256), whole(128), whole(384)],
        out_specs=[rows(d), rows(256), rows(128), rows(384)], compiler_params=_params(("parallel",)),
    )(o_sb, o_dil, o_swa, gates, wb_sb, wb_dil, wb_swa)


def _merge_bwd(dmerged, t_sb, t_dil, t_swa, gates, wb_sb, wb_dil, wb_swa, name):
    s, d = SEQ, D_MODEL
    tm = 256

    def body(dm_ref, tsb_ref, tdl_ref, tsw_ref, g_ref, wsb_ref, wdl_ref, wsw_ref,
             dg_ref, dosb_ref, dodl_ref, dosw_ref, dbsb_ref, dbdl_ref, dbsw_ref):
        dm = dm_ref[...]
        for idx, (t_ref, w_ref, do_ref, db_ref) in enumerate((
                (tsb_ref, wsb_ref, dosb_ref, dbsb_ref), (tdl_ref, wdl_ref, dodl_ref, dbdl_ref),
                (tsw_ref, wsw_ref, dosw_ref, dbsw_ref))):
            w = w_ref[...]
            br = _dot(t_ref[...], w, 1, 0)
            sg = _sigmoid(g_ref[:, idx * d:(idx + 1) * d])
            dbr = (dm * sg).astype(BF16)
            dg_ref[:, idx * d:(idx + 1) * d] = (dm * br * (sg * (1.0 - sg))).astype(BF16)
            db_ref[...] = dbr
            do = _dot(dbr, w, 1, 1)
            if len(do_ref.shape) == 2:
                do_ref[...] = do
            else:
                for h in range(do_ref.shape[0]):
                    do_ref[h] = do[:, h * HEAD_DIM:(h + 1) * HEAD_DIM]

    def rows(w):
        return pl.BlockSpec((tm, w), lambda i: (i, 0))

    def heads(n):
        return pl.BlockSpec((n, tm, HEAD_DIM), lambda i: (0, i, 0))

    def whole(w):
        return pl.BlockSpec((w, d), lambda i: (0, 0))

    def shp(w, dt):
        return jax.ShapeDtypeStruct((s, w), dt)

    def hshp(n):
        return jax.ShapeDtypeStruct((n, s, HEAD_DIM), F32)

    return pl.pallas_call(
        body, name=name,
        out_shape=[shp(3 * d, BF16), hshp(H_SB), shp(128, F32), hshp(H_SWA_Q), shp(d, BF16), shp(d, BF16), shp(d, BF16)],
        grid=(s // tm,),
        in_specs=[rows(d), rows(256), rows(128), rows(384), rows(3 * d), whole(256), whole(128), whole(384)],
        out_specs=[rows(3 * d), heads(H_SB), rows(128), heads(H_SWA_Q), rows(d), rows(d), rows(d)],
        compiler_params=_params(("parallel",)),
    )(dmerged, t_sb, t_dil, t_swa, gates, wb_sb, wb_dil, wb_swa)


def _final_loss(x, target, g, name):
    s, d = x.shape
    tr = 256

    def body(x_ref, t_ref, g_ref, loss_ref, dx_ref, dg_ref):
        @pl.when(pl.program_id(0) == 0)
        def _():
            loss_ref[...] = jnp.zeros_like(loss_ref)
            dg_ref[...] = jnp.zeros_like(dg_ref)

        xv = x_ref[...]
        gv = g_ref[...]
        rstd = lax.rsqrt(jnp.mean(xv * xv, axis=-1, keepdims=True) + RMS_EPS)
        xhat = xv * rstd
        err = xhat * gv - t_ref[...]
        loss_ref[...] += 0.5 * jnp.sum(jnp.mean(err * err, axis=-1, keepdims=True))
        dy = err * (1.0 / d)
        dxhat = dy * gv
        mean_term = jnp.mean(dxhat * xhat, axis=-1, keepdims=True)
        dx_ref[...] = rstd * (dxhat - xhat * mean_term)
        dg_ref[...] += jnp.sum(dy * xhat, axis=0, keepdims=True)

    rowspec = pl.BlockSpec((1, d), lambda i: (0, 0))
    tile = pl.BlockSpec((tr, d), lambda i: (i, 0))
    return pl.pallas_call(
        body, name=name,
        out_shape=[jax.ShapeDtypeStruct((1, LANES), F32), jax.ShapeDtypeStruct((s, d), F32), jax.ShapeDtypeStruct((1, d), F32)],
        grid=(s // tr,), in_specs=[tile, tile, rowspec],
        out_specs=[pl.BlockSpec((1, LANES), lambda i: (0, 0)), tile, rowspec],
        compiler_params=_params(("arbitrary",)),
    )(x, target, g)


def _adamw(w, g, m, v, name):
    shape = w.shape
    cols = shape[-1]
    rows = int(np.prod(shape[:-1])) if len(shape) > 1 else 1
    tr = rows
    for cand in (1024, 512, 256, 128, 64, 32, 16, 8):
        if rows % cand == 0 and rows > cand and cand * cols * 4 <= (1 << 21):
            tr = cand
            break

    def body(w_ref, g_ref, m_ref, v_ref, d_ref, nm_ref, nv_ref):
        d_ref[...], nm_ref[...], nv_ref[...] = _adam_update(w_ref[...], g_ref[...], m_ref[...], v_ref[...])

    tile = pl.BlockSpec((tr, cols), lambda i: (i, 0))
    flat = [t.reshape(rows, cols) for t in (w, g, m, v)]
    out = pl.pallas_call(
        body, name=name, out_shape=[jax.ShapeDtypeStruct((rows, cols), F32)] * 3, grid=(rows // tr,),
        in_specs=[tile] * 4, out_specs=[tile] * 3, compiler_params=_params(("parallel",)),
    )(*flat)
    return tuple(t.reshape(shape) for t in out)


def _adam_update(w, gv, m, v):
    nm = ADAM_B1 * m + (1.0 - ADAM_B1) * gv
    nv = ADAM_B2 * v + (1.0 - ADAM_B2) * (gv * gv)
    m_hat = nm / (1.0 - ADAM_B1 ** ADAM_STEP)
    v_hat = nv / (1.0 - ADAM_B2 ** ADAM_STEP)
    return -ADAM_LR * (m_hat / (jnp.sqrt(v_hat) + ADAM_EPS) + ADAM_WD * w), nm, nv


def _reduce_adamw(groups, w, m, v, row0, prev, name, after=None):
    n, r, cdim = groups[0].shape
    rows = w.shape[0]
    tr = _row_tile(r, max(16, (1 << 22) // (n * cdim * groups[0].dtype.itemsize)))
    steps = r // tr
    ng = len(groups)

    def body(*refs):
        w_ref, m_ref, v_ref = refs[ng:ng + 3]
        g_out, d_out, m_out, v_out = refs[-4:]
        gg = pl.program_id(0)
        for gi in range(ng):
            @pl.when(gg == gi)
            def _(gi=gi):
                acc = refs[gi][0].astype(F32)
                for k in range(1, n):
                    acc = acc + refs[gi][k].astype(F32)
                g_out[...] = acc
                d_out[...], m_out[...], v_out[...] = _adam_update(w_ref[...], acc, m_ref[...], v_ref[...])

    def part_spec(gi):
        return pl.BlockSpec((n, tr, cdim), lambda gg, i: (0, jnp.where(gg == gi, i, 0), 0))

    tile = pl.BlockSpec((tr, cdim), lambda gg, i: (row0 // tr + gg * steps + i, 0))
    extra = ([] if prev is None else list(prev)) + ([] if after is None else [after])
    return pl.pallas_call(
        body, name=name, out_shape=[jax.ShapeDtypeStruct((rows, cdim), F32)] * 4, grid=(ng, steps),
        in_specs=[part_spec(gi) for gi in range(ng)] + [tile] * 3 + [ANY] * len(extra), out_specs=[tile] * 4,
        input_output_aliases={} if prev is None else {ng + 3 + k: k for k in range(4)},
        compiler_params=_params(("parallel", "parallel")),
    )(*groups, w, m, v, *extra)


def _ada_fwd(c_all, w, name):
    n = w.shape[1]

    def body(c_ref, w_ref, o_ref):
        cv = c_ref[...]
        o_ref[...] = jnp.dot(cv * _sigmoid(cv), w_ref[...], preferred_element_type=F32, precision=lax.Precision.HIGHEST)

    return pl.pallas_call(body, name=name, out_shape=jax.ShapeDtypeStruct((N_DEV, n), F32), compiler_params=_params())(c_all, w)


def _ada_bwd(c_all_t, dmod, name):
    n = dmod.shape[1]

    def body(c_ref, d_ref, o_ref):
        cv = c_ref[...]
        o_ref[...] = jnp.dot(cv * _sigmoid(cv), d_ref[...], preferred_element_type=F32, precision=lax.Precision.HIGHEST)

    return pl.pallas_call(body, name=name, out_shape=jax.ShapeDtypeStruct((D_MODEL, n), F32), compiler_params=_params())(c_all_t, dmod)


def _bucket_tables():
    rel = np.arange(BLK)[:, None] + BLK - np.arange(2 * BLK)[None, :]
    max_exact = N_BUCKETS // 2

    def bucket(n):
        nf = np.maximum(n, 1).astype(np.float32)
        large = max_exact + (np.log(nf / np.float32(max_exact)) / np.float32(math.log(MAX_REL_DIST / max_exact))
                             * np.float32(N_BUCKETS - max_exact)).astype(np.int32)
        return np.where(n < max_exact, n, np.minimum(large, N_BUCKETS - 1))

    tabs = []
    for dil, max_dist in ((1, 128), (4, 128), (16, 128), (1, SWA_WINDOW - 1)):
        in_band = (rel >= 0) & (rel <= max_dist)
        tabs.append(np.where(in_band, bucket(np.maximum(rel, 0) * dil), -1))
    return np.stack(tabs).astype(np.int32)


N_SOFT = H_DIL + H_SWA_Q


def _table_of_head(h):
    return jnp.minimum(h // 2, 3)


def _bias_build(rel_bias, tables, name):
    def body(rel_ref, t_ref, o_ref):
        h = pl.program_id(0)
        tb = t_ref[0]
        out = jnp.full((BLK, 2 * BLK), NEG, F32)
        for b in range(N_BUCKETS):
            out = jnp.where(tb == b, rel_ref[b, h], out)
        o_ref[0] = out

    return pl.pallas_call(
        body, name=name, out_shape=jax.ShapeDtypeStruct((N_SOFT, BLK, 2 * BLK), F32), grid=(N_SOFT,),
        in_specs=[pl.BlockSpec(memory_space=pltpu.SMEM),
                  pl.BlockSpec((1, BLK, 2 * BLK), lambda h: (_table_of_head(h), 0, 0))],
        out_specs=pl.BlockSpec((1, BLK, 2 * BLK), lambda h: (h, 0, 0)),
        compiler_params=_params(("parallel",)),
    )(rel_bias, tables)


def _bias_grad(dbias, tables, name):
    def body(d_ref, t_ref, o_ref):
        tb = t_ref[0]
        dv = d_ref[0]
        lane = lax.broadcasted_iota(jnp.int32, (1, LANES), 1)
        out = jnp.zeros((1, LANES), F32)
        for b in range(N_BUCKETS):
            out = jnp.where(lane == b, jnp.sum(jnp.where(tb == b, dv, 0.0)), out)
        o_ref[0] = out

    return pl.pallas_call(
        body, name=name, out_shape=jax.ShapeDtypeStruct((N_SOFT, 1, LANES), F32), grid=(N_SOFT,),
        in_specs=[pl.BlockSpec((1, BLK, 2 * BLK), lambda h: (h, 0, 0)),
                  pl.BlockSpec((1, BLK, 2 * BLK), lambda h: (_table_of_head(h), 0, 0))],
        out_specs=pl.BlockSpec((1, 1, LANES), lambda h: (h, 0, 0)),
        compiler_params=_params(("parallel",)),
    )(dbias, tables)


def _band_layout(g, bias_div):
    assert g == 1 or bias_div == 1
    return bias_div if g == 1 else 1


def _band_specs(length, g, bias_div, offs):
    ns = _band_layout(g, bias_div)

    def seqs(off, div=1):
        return pl.BlockSpec((ns, length, HEAD_DIM), lambda s: (off // ns + s // div, 0, 0))

    xspecs = [seqs(offs[0]), seqs(offs[1], g), seqs(offs[2], g)]
    bspec = pl.BlockSpec((1, BLK, 2 * BLK), lambda s: (s, 0, 0))
    sspec = pl.BlockSpec((ns, 1, LANES), lambda s: (s, 0, 0))
    colspec = pl.BlockSpec((ns, length, 1), lambda s: (s, 0, 0))
    return xspecs, seqs(0), seqs(0, g), bspec, sspec, colspec


def _band_sweep(length, ns, one):
    nblk = length // BLK
    for qq in range(ns):
        if ns * nblk <= 16:
            for i in range(nblk):
                one(qq, i * BLK, max(i - 1, 0) * BLK, i == 0)
        else:
            def step(i, carry, qq=qq):
                one(qq, pl.multiple_of(i * BLK, BLK), pl.multiple_of(jnp.maximum(i - 1, 0) * BLK, BLK), i == 0)
                return carry

            lax.fori_loop(0, nblk, step, 0, unroll=2)


def _band_scores(q_ref, k_ref, b_ref, qq, kq, bq, cur, prv, first):
    qv = q_ref[qq, pl.ds(cur, BLK), :]
    bv = b_ref[bq]
    if first is True:
        sp = jnp.full((BLK, BLK), NEG, F32)
    else:
        sp = _dot(qv, k_ref[kq, pl.ds(prv, BLK), :], 1, 1) + bv[:, :BLK]
        sp = sp if first is False else jnp.where(first, NEG, sp)
    sc = _dot(qv, k_ref[kq, pl.ds(cur, BLK), :], 1, 1) + bv[:, BLK:]
    return qv, sp, sc


def _band_fwd(x, bias, sink, *, nq, offs, g, bias_div, has_sink, name):
    length = x.shape[1]
    ns = _band_layout(g, bias_div)

    def body(q_ref, k_ref, v_ref, b_ref, s_ref, o_ref, lse_ref):
        def one(qq, cur, prv, first):
            kq, bq = qq, 0
            _, sp, sc = _band_scores(q_ref, k_ref, b_ref, qq, kq, bq, cur, prv, first)
            m = jnp.maximum(jnp.max(sp, axis=1, keepdims=True), jnp.max(sc, axis=1, keepdims=True))
            if has_sink:
                sk = s_ref[qq][:, :1]
                m = jnp.maximum(m, sk)
            pp, pc = jnp.exp(sp - m), jnp.exp(sc - m)
            den = jnp.sum(pp, axis=1, keepdims=True) + jnp.sum(pc, axis=1, keepdims=True)
            if has_sink:
                den = den + jnp.exp(sk - m)
            acc = (_dot(pp.astype(BF16), v_ref[kq, pl.ds(prv, BLK), :], 1, 0)
                   + _dot(pc.astype(BF16), v_ref[kq, pl.ds(cur, BLK), :], 1, 0))
            o_ref[qq, pl.ds(cur, BLK), :] = acc / den
            lse_ref[qq, pl.ds(cur, BLK), :] = m + jnp.log(den)

        _band_sweep(length, ns, one)

    xspecs, qspec, _, bspec, sspec, colspec = _band_specs(length, g, bias_div, offs)
    return pl.pallas_call(
        body, name=name,
        out_shape=[jax.ShapeDtypeStruct((nq, length, HEAD_DIM), F32), jax.ShapeDtypeStruct((nq, length, 1), F32)],
        grid=(nq // ns,), in_specs=xspecs + [bspec, sspec],
        out_specs=[qspec, colspec], compiler_params=_params(("parallel",)),
    )(x, x, x, bias, sink)


def _band_bwd(x, bias, sink, o, lse, do, dlse, *, nq, offs, g, bias_div, has_sink, name):
    length = x.shape[1]
    ns = _band_layout(g, bias_div)
    nk, nbias = nq // g, nq // bias_div

    def body(q_ref, k_ref, v_ref, b_ref, s_ref, o_ref, lse_ref, do_ref, dlse_ref,
             dq_ref, dk_ref, dv_ref, db_ref, dsk_ref, dkp_ref, dvp_ref):
        for ref in (db_ref, dsk_ref, dkp_ref, dvp_ref):
            ref[...] = jnp.zeros_like(ref)

        @pl.when(pl.program_id(0) % g == 0)
        def _():
            dk_ref[...] = jnp.zeros_like(dk_ref)
            dv_ref[...] = jnp.zeros_like(dv_ref)

        def one(qq, cur, prv, first):
            kq, bq = qq, 0
            qv, sp, sc = _band_scores(q_ref, k_ref, b_ref, qq, kq, bq, cur, prv, first)
            rows, prow = pl.ds(cur, BLK), pl.ds(prv, BLK)
            lse_v = lse_ref[qq, rows, :]
            pp, pc = jnp.exp(sp - lse_v), jnp.exp(sc - lse_v)
            dov = do_ref[qq, rows, :]
            dob = dov.astype(BF16)
            coef = dlse_ref[qq, rows, :] - jnp.sum(dov * o_ref[qq, rows, :], axis=1, keepdims=True)
            dsp = pp * (_dot(dob, v_ref[kq, prow, :], 1, 1) + coef)
            dsc = pc * (_dot(dob, v_ref[kq, rows, :], 1, 1) + coef)
            dspb, dscb = dsp.astype(BF16), dsc.astype(BF16)
            dq_ref[qq, rows, :] = ((_dot(dspb, k_ref[kq, prow, :], 1, 0) + _dot(dscb, k_ref[kq, rows, :], 1, 0))
                                   * (HEAD_DIM ** -0.5))
            dk_ref[kq, rows, :] += _dot(dscb, qv, 0, 0)
            dkp_ref[kq, prow, :] += _dot(dspb, qv, 0, 0)
            dv_ref[kq, rows, :] += _dot(pc.astype(BF16), dob, 0, 0)
            dvp_ref[kq, prow, :] += _dot(pp.astype(BF16), dob, 0, 0)
            db_ref[bq, :, :BLK] += dsp
            db_ref[bq, :, BLK:] += dsc
            if has_sink:
                dsk_ref[qq] += jnp.sum(jnp.exp(s_ref[qq][:, :1] - lse_v) * coef)

        _band_sweep(length, ns, one)
        dk_ref[...] += dkp_ref[...]
        dv_ref[...] += dvp_ref[...]

    xspecs, qspec, kvspec, bspec, sspec, colspec = _band_specs(length, g, bias_div, offs)
    return pl.pallas_call(
        body, name=name,
        out_shape=[jax.ShapeDtypeStruct((nq, length, HEAD_DIM), F32), jax.ShapeDtypeStruct((nk, length, HEAD_DIM), F32),
                   jax.ShapeDtypeStruct((nk, length, HEAD_DIM), F32), jax.ShapeDtypeStruct((nbias, BLK, 2 * BLK), F32),
                   jax.ShapeDtypeStruct((nq, 1, LANES), F32)],
        grid=(nq // ns,),
        in_specs=xspecs + [bspec, sspec, qspec, colspec, qspec, colspec],
        out_specs=[qspec, kvspec, kvspec, bspec, sspec],
        scratch_shapes=[pltpu.VMEM((ns, length, HEAD_DIM), F32), pltpu.VMEM((ns, length, HEAD_DIM), F32)],
        compiler_params=_params(("arbitrary",)),
    )(x, x, x, bias, sink, o, lse, do, dlse)


TOK_TILE = 512


def _dil_merge(outs, lses, dout, name):
    tr = TOK_TILE
    dils = [d for _, d in DIL_PATTERNS]
    n = len(dils)
    o4 = [o.reshape(2, d, SEQ // d, HEAD_DIM) for o, d in zip(outs, dils)]
    l4 = [l.reshape(2, d, SEQ // d, 1) for l, d in zip(lses, dils)]
    o_specs = [pl.BlockSpec((2, d, tr // d, HEAD_DIM), lambda i: (0, 0, i, 0)) for d in dils]
    l_specs = [pl.BlockSpec((2, d, tr // d, 1), lambda i: (0, 0, i, 0)) for d in dils]
    tok = pl.BlockSpec((tr, 2 * HEAD_DIM), lambda i: (i, 0))
    scratch = ([pltpu.VMEM((tr, 2 * HEAD_DIM), F32) for _ in dils] + [pltpu.VMEM((tr, 1), F32) for _ in range(2 * n)]
               + [pltpu.VMEM((tr // d, 2 * HEAD_DIM), F32) for d in dils])

    def to_tokens(o_ref, l_ref, d, pair, cols, stage):
        for r in range(d):
            rows = pl.ds(r, tr // d, stride=d) if d > 1 else slice(None)
            stage[:, :HEAD_DIM] = o_ref[0, r]
            stage[:, HEAD_DIM:] = o_ref[1, r]
            pair[rows, :] = stage[...]
            for h in range(2):
                cols[h][rows, :] = l_ref[h, r]
        return pair[...], [cols[0][...], cols[1][...]]

    def weights(ls):
        left = lax.broadcasted_iota(jnp.int32, (tr, 2 * HEAD_DIM), 1) < HEAD_DIM
        per_head = []
        for h in range(2):
            m = ls[0][h]
            for g in range(1, n):
                m = jnp.maximum(m, ls[g][h])
            es = [jnp.exp(ls[g][h] - m) for g in range(n)]
            den = es[0]
            for e in es[1:]:
                den = den + e
            per_head.append([e / den for e in es])
        return per_head, [jnp.where(left, per_head[0][g], per_head[1][g]) for g in range(n)], left

    def load(refs):
        pairs, cols, stages = refs[:n], refs[n:3 * n], refs[3 * n:]
        return pairs, [cols[2 * g:2 * g + 2] for g in range(n)], stages

    if dout is None:
        def body(*refs):
            pairs, cols, stages = load(refs[2 * n + 1:])
            toks = [to_tokens(refs[g], refs[n + g], dils[g], pairs[g], cols[g], stages[g]) for g in range(n)]
            _, alphas, _ = weights([t[1] for t in toks])
            acc = alphas[0] * toks[0][0]
            for g in range(1, n):
                acc = acc + alphas[g] * toks[g][0]
            refs[2 * n][...] = acc

        return pl.pallas_call(
            body, name=name, out_shape=jax.ShapeDtypeStruct((SEQ, 2 * HEAD_DIM), F32), grid=(SEQ // tr,),
            in_specs=o_specs + l_specs, out_specs=tok, scratch_shapes=scratch, compiler_params=_params(("parallel",)),
        )(*o4, *l4)

    def body(*refs):
        do_refs, dl_refs = refs[2 * n + 1:3 * n + 1], refs[3 * n + 1:4 * n + 1]
        pairs, cols, stages = load(refs[4 * n + 1:])
        toks = [to_tokens(refs[g], refs[n + g], dils[g], pairs[g], cols[g], stages[g]) for g in range(n)]
        per_head, alphas, left = weights([t[1] for t in toks])
        dov = refs[2 * n][...]
        das = []
        for g in range(n):
            prod = dov * toks[g][0]
            das.append([jnp.sum(jnp.where(left, prod, 0.0), axis=1, keepdims=True),
                        jnp.sum(jnp.where(left, 0.0, prod), axis=1, keepdims=True)])
        dbar = [sum(per_head[h][g] * das[g][h] for g in range(n)) for h in range(2)]
        for g, d in enumerate(dils):
            pairs[g][...] = alphas[g] * dov
            for h in range(2):
                cols[g][h][...] = per_head[h][g] * (das[g][h] - dbar[h])
            for r in range(d):
                rows = pl.ds(r, tr // d, stride=d) if d > 1 else slice(None)
                v = pairs[g][rows, :]
                for h in range(2):
                    do_refs[g][h, r] = v[:, h * HEAD_DIM:(h + 1) * HEAD_DIM]
                    dl_refs[g][h, r] = cols[g][h][rows, :]

    out = pl.pallas_call(
        body, name=name,
        out_shape=[jax.ShapeDtypeStruct(o.shape, F32) for o in o4] + [jax.ShapeDtypeStruct(l.shape, F32) for l in l4],
        grid=(SEQ // tr,), in_specs=o_specs + l_specs + [tok], out_specs=o_specs + l_specs, scratch_shapes=scratch,
        compiler_params=_params(("parallel",)),
    )(*o4, *l4, dout)
    return [t.reshape(s.shape) for t, s in zip(out, list(outs) + list(lses))]


def _tri(cmp):
    r = lax.broadcasted_iota(jnp.int32, (SB_TILE, SB_TILE), 0)
    c = lax.broadcasted_iota(jnp.int32, (SB_TILE, SB_TILE), 1)
    return cmp(r, c).astype(BF16)


def _cum(x, tri, terms):
    acc, rest = None, x
    for _ in range(terms):
        part = rest.astype(BF16)
        rest = rest - part.astype(F32)
        d = _dot(part, tri, 1, 0)
        acc = d if acc is None else acc + d
    return acc


def _sb_logits(q, ks, diagonal):
    t = SB_TILE
    z = _dot(q, ks, 1, 1)
    e = jnp.exp(-jnp.abs(z))
    lf = -(jnp.maximum(z, 0.0) + jnp.log(1.0 + e))
    if not diagonal:
        return z, e, lf, None
    mask = lax.broadcasted_iota(jnp.int32, (t, t), 1) < lax.broadcasted_iota(jnp.int32, (t, t), 0)
    return z, e, jnp.where(mask, lf, 0.0), mask


def _sb_specs(h, s):
    t = SB_TILE
    tile = pl.BlockSpec((h, t, HEAD_DIM), lambda i: (0, i, 0))
    keys = pl.BlockSpec((h, s, HEAD_DIM), lambda i: (1, 0, 0))
    values = pl.BlockSpec((h, s, HEAD_DIM), lambda i: (2, 0, 0))
    return tile, keys, values, pl.BlockSpec((h, t, 1), lambda i: (0, i, 0))


def _sb_fwd(x, name):
    h, s = x.shape[0] // 3, x.shape[1]
    t = SB_TILE

    def body(q_ref, k_ref, v_ref, o_ref, tot_ref):
        i = pl.program_id(0)
        after = _tri(lambda r, c: r > c)

        def tile(j, carry, diagonal):
            rows = pl.ds(pl.multiple_of(j * t, t), t)
            out = []
            for hh, (right, acc) in enumerate(carry):
                z, _, lf, mask = _sb_logits(q_ref[hh], k_ref[hh, rows, :], diagonal)
                w = jnp.exp(z + lf + (right + _cum(lf, after, 2)))
                w = w if mask is None else jnp.where(mask, w, 0.0)
                out.append((right + jnp.sum(lf, axis=1, keepdims=True), acc + _dot(w.astype(BF16), v_ref[hh, rows, :], 1, 0)))
            return tuple(out)

        carry = tile(i, tuple((jnp.zeros((t, 1), F32), jnp.zeros((t, HEAD_DIM), F32)) for _ in range(h)), True)
        carry = lax.fori_loop(0, i, lambda jj, c: tile(i - 1 - jj, c, False), carry)
        for hh, (right, acc) in enumerate(carry):
            o_ref[hh] = acc
            tot_ref[hh] = right

    tile_spec, keys, values, col = _sb_specs(h, s)
    return pl.pallas_call(
        body, name=name, out_shape=[jax.ShapeDtypeStruct((h, s, HEAD_DIM), F32), jax.ShapeDtypeStruct((h, s, 1), F32)],
        grid=(s // t,), in_specs=[tile_spec, keys, values], out_specs=[tile_spec, col],
        compiler_params=_params(("parallel",)),
    )(x, x, x)


def _sb_bwd(x, tot, do, name):
    h, s = x.shape[0] // 3, x.shape[1]
    t = SB_TILE

    def body(q_ref, k_ref, v_ref, tot_ref, do_ref, dq_ref, dk_ref, dv_ref):
        i = pl.program_id(0)

        @pl.when(i == 0)
        def _():
            dk_ref[...] = jnp.zeros_like(dk_ref)
            dv_ref[...] = jnp.zeros_like(dv_ref)

        upto = _tri(lambda r, c: r <= c)
        before = _tri(lambda r, c: r < c)

        def tile(j, carry, diagonal):
            rows = pl.ds(pl.multiple_of(j * t, t), t)
            out = []
            for hh, (left, cleft, dq) in enumerate(carry):
                qv, ks, dob = q_ref[hh], k_ref[hh, rows, :], do_ref[hh].astype(BF16)
                z, e, lf, mask = _sb_logits(qv, ks, diagonal)
                between = tot_ref[hh] - (left + _cum(lf, upto, 2))
                w = jnp.exp(z + lf + between)
                w = w if mask is None else jnp.where(mask, w, 0.0)
                dlog = w * _dot(dob, v_ref[hh, rows, :], 1, 1)
                cfail = cleft + _cum(dlog, before, 2)
                sig = jnp.where(z >= 0.0, 1.0, e) / (1.0 + e)
                dz = dlog * (1.0 - sig) - sig * cfail
                dz = (dz if mask is None else jnp.where(mask, dz, 0.0)).astype(BF16)
                dk_ref[hh, rows, :] += _dot(dz, qv, 0, 0)
                dv_ref[hh, rows, :] += _dot(w.astype(BF16), dob, 0, 0)
                out.append((left + jnp.sum(lf, axis=1, keepdims=True), cleft + jnp.sum(dlog, axis=1, keepdims=True),
                            dq + _dot(dz, ks, 1, 0)))
            return tuple(out)

        zero = jnp.zeros((t, 1), F32)
        carry = lax.fori_loop(0, i, lambda j, c: tile(j, c, False),
                              tuple((zero, zero, jnp.zeros((t, HEAD_DIM), F32)) for _ in range(h)))
        for hh, (_, _, dq) in enumerate(tile(i, carry, True)):
            dq_ref[hh] = dq * (HEAD_DIM ** -0.5)

    tile_spec, keys, values, col = _sb_specs(h, s)
    full = pl.BlockSpec((h, s, HEAD_DIM), lambda i: (0, 0, 0))
    shp = jax.ShapeDtypeStruct((h, s, HEAD_DIM), F32)
    return pl.pallas_call(
        body, name=name, out_shape=[shp, shp, shp], grid=(s // t,),
        in_specs=[tile_spec, keys, values, col, tile_spec],
        out_specs=[tile_spec, full, full], compiler_params=_params(("arbitrary",)),
    )(x, x, x, tot, do)


COL_SB, COL_DIL, COL_SWA = 0, 3 * H_SB * HEAD_DIM, 3 * H_SB * HEAD_DIM + 3 * H_DIL * HEAD_DIM
N_SWA = H_SWA_Q + 2 * H_SWA_KV


def _dil_col(t, g):
    return COL_DIL + t * H_DIL * HEAD_DIM + g * 2 * HEAD_DIM


def _split_heads(qkv, name):
    tr = TOK_TILE
    scale = HEAD_DIM ** -0.5
    dils = [d for _, d in DIL_PATTERNS]

    def body(x_ref, sb_ref, d0_ref, d1_ref, d2_ref, swa_ref, pair):
        def head(col, scaled):
            v = x_ref[:, col:col + HEAD_DIM]
            return (v * scale if scaled else v).astype(BF16)

        for hh in range(3 * H_SB):
            sb_ref[hh] = head(COL_SB + hh * HEAD_DIM, hh < H_SB)
        for hh in range(N_SWA):
            swa_ref[hh] = head(COL_SWA + hh * HEAD_DIM, hh < H_SWA_Q)
        for t in range(3):
            for g, (d, out_ref) in enumerate(zip(dils, (d0_ref, d1_ref, d2_ref))):
                col = _dil_col(t, g)
                if d == 1:
                    for h in range(2):
                        out_ref[t * 2 + h] = head(col + h * HEAD_DIM, t == 0)
                    continue
                pair[...] = x_ref[:, col:col + 2 * HEAD_DIM]
                for r in range(d):
                    v = pair[pl.ds(r, tr // d, stride=d), :]
                    v = v * scale if t == 0 else v
                    for h in range(2):
                        out_ref[t * 2 * d + h * d + r] = v[:, h * HEAD_DIM:(h + 1) * HEAD_DIM].astype(BF16)

    def heads(n, length):
        return jax.ShapeDtypeStruct((n, length, HEAD_DIM), BF16)

    def spec(n, rows):
        return pl.BlockSpec((n, rows, HEAD_DIM), lambda i: (0, i, 0))

    return pl.pallas_call(
        body, name=name,
        out_shape=[heads(3 * H_SB, SEQ)] + [heads(6 * d, SEQ // d) for d in dils] + [heads(N_SWA, SEQ)],
        grid=(SEQ // tr,), in_specs=[pl.BlockSpec((tr, D_QKV), lambda i: (i, 0))],
        out_specs=[spec(3 * H_SB, tr)] + [spec(6 * d, tr // d) for d in dils] + [spec(N_SWA, tr)],
        scratch_shapes=[pltpu.VMEM((tr, 2 * HEAD_DIM), F32)], compiler_params=_params(("parallel",)),
    )(qkv)


def _join_heads(sb, dil, swa, name):
    tr = TOK_TILE
    dils = [d for _, d in DIL_PATTERNS]

    def body(*refs):
        sb_refs, dil_refs, swa_refs = refs[:3], [refs[3 + 3 * g:6 + 3 * g] for g in range(3)], refs[12:15]
        o_ref, pair, stages = refs[15], refs[16], refs[17:]

        def put(col, v):
            o_ref[:, col:col + v.shape[1]] = v.astype(BF16)

        for t in range(3):
            for h in range(H_SB):
                put(COL_SB + (t * H_SB + h) * HEAD_DIM, sb_refs[t][h])
        col = COL_SWA
        for ref in swa_refs:
            for h in range(ref.shape[0]):
                put(col, ref[h])
                col += HEAD_DIM
        for t in range(3):
            for g, d in enumerate(dils):
                ref, col = dil_refs[g][t], _dil_col(t, g)
                if d == 1:
                    for h in range(2):
                        put(col + h * HEAD_DIM, ref[h])
                    continue
                stage = stages[g - 1]
                for r in range(d):
                    stage[:, :HEAD_DIM] = ref[r]
                    stage[:, HEAD_DIM:] = ref[d + r]
                    pair[pl.ds(r, tr // d, stride=d), :] = stage[...]
                put(col, pair[...])

    def spec(n, rows):
        return pl.BlockSpec((n, rows, HEAD_DIM), lambda i: (0, i, 0))

    ins = list(sb) + [t for g in range(3) for t in dil[g]] + list(swa)
    in_specs = ([spec(H_SB, tr)] * 3 + [spec(2 * d, tr // d) for d in dils for _ in range(3)]
                + [spec(H_SWA_Q, tr), spec(H_SWA_KV, tr), spec(H_SWA_KV, tr)])
    return pl.pallas_call(
        body, name=name, out_shape=jax.ShapeDtypeStruct((SEQ, D_QKV), BF16), grid=(SEQ // tr,), in_specs=in_specs,
        out_specs=pl.BlockSpec((tr, D_QKV), lambda i: (i, 0)),
        scratch_shapes=[pltpu.VMEM((tr, 2 * HEAD_DIM), F32)] + [pltpu.VMEM((tr // d, 2 * HEAD_DIM), F32) for d in dils[1:]],
        compiler_params=_params(("parallel",)),
    )(*ins)


def _mixer_fwd(qkv, bias, sinks_l, tag):
    sb, d0, d1, d2, swa = _split_heads(qkv, name=f"split_heads_{tag}")
    st = {"sb": sb, "dil": (d0, d1, d2), "swa": swa}
    o_sb, st["sb_tot"] = _sb_fwd(sb, name=f"sb_fwd_{tag}")
    st["dil_out"], st["dil_lse"], st["dil_sink"] = [], [], []
    for gi, (_, d) in enumerate(DIL_PATTERNS):
        sink = jnp.zeros((2 * d, 1, LANES), F32)
        og, lg = _band_fwd(st["dil"][gi], bias[2 * gi:2 * gi + 2], sink, nq=2 * d, offs=(0, 2 * d, 4 * d), g=1, bias_div=d,
                           has_sink=False, name=f"dil{gi}_fwd_{tag}")
        st["dil_out"].append(og)
        st["dil_lse"].append(lg)
        st["dil_sink"].append(sink)
    o_dil = _dil_merge(st["dil_out"], st["dil_lse"], None, name=f"dil_merge_fwd_{tag}")
    st["swa_sink"] = jnp.broadcast_to(sinks_l.reshape(H_SWA_Q, 1, 1), (H_SWA_Q, 1, LANES))
    st["swa_out"] = _band_fwd(swa, bias[H_DIL:], st["swa_sink"], nq=H_SWA_Q, offs=(0, H_SWA_Q, H_SWA_Q + H_SWA_KV),
                              g=H_SWA_Q // H_SWA_KV, bias_div=1, has_sink=True, name=f"swa_fwd_{tag}")
    return (o_sb, o_dil, st["swa_out"][0]), st


def _mixer_bwd(st, bias, do_sb, do_dil, do_swa, tag):
    d_sb = _sb_bwd(st["sb"], st["sb_tot"], do_sb, name=f"sb_bwd_{tag}")
    dmerge = _dil_merge(st["dil_out"], st["dil_lse"], do_dil, name=f"dil_merge_bwd_{tag}")
    d_dil, dbs = [], []
    for gi, (_, d) in enumerate(DIL_PATTERNS):
        dq, dk, dv, db, _ = _band_bwd(st["dil"][gi], bias[2 * gi:2 * gi + 2], st["dil_sink"][gi], st["dil_out"][gi],
                                      st["dil_lse"][gi], dmerge[gi], dmerge[3 + gi], nq=2 * d, offs=(0, 2 * d, 4 * d),
                                      g=1, bias_div=d, has_sink=False, name=f"dil{gi}_bwd_{tag}")
        d_dil.append((dq, dk, dv))
        dbs.append(db)
    o_sw, l_sw = st["swa_out"]
    dq_sw, dk_sw, dv_sw, db_sw, dsink = _band_bwd(st["swa"], bias[H_DIL:], st["swa_sink"], o_sw, l_sw, do_swa,
                                                  jnp.zeros_like(l_sw), nq=H_SWA_Q, offs=(0, H_SWA_Q, H_SWA_Q + H_SWA_KV),
                                                  g=H_SWA_Q // H_SWA_KV, bias_div=1, has_sink=True, name=f"swa_bwd_{tag}")
    dqkv = _join_heads(d_sb, d_dil, (dq_sw, dk_sw, dv_sw), name=f"join_heads_{tag}")
    return dqkv, jnp.concatenate(dbs + [db_sw], 0), dsink[:, 0, 0]


PIECES = ("ffn0", "mix", "ffn1")


def _ffn_fwd(x_in, w, gain, mod_j, tag, after=None):
    st = {"x": x_in, "w": w}
    st["h"] = _norm_fwd(x_in, _row(gain), _row(mod_j[1]), _row(mod_j[0]), name=f"norm_fwd_{tag}", after=after)
    st["a"], st["u"], st["s"] = _ffn_up(st["h"], w["gate"], w["up"], name=f"up_{tag}")
    st["f"], x_out = _mm(st["s"], w["down"], res=x_in, colscale=_row(0.5 * mod_j[2]), emit_acc=True, tm=512, tn=1024,
                         name=f"down_{tag}")
    return x_out, st


def _ffn_bwd(dx_out, st, gain, mod_j, tag, done):
    w = st["w"]

    def latest(new, old):
        return old if new is None else new

    df, dgate = _gate_bwd(dx_out, st["f"], _row(0.5 * mod_j[2]), 0.5, name=f"gate_bwd_{tag}")
    token = done({"down": _mm_tn(st["s"], df, tm=D_FF // 2, name=f"dwd_{tag}")})
    da, du = _ffn_bwd_ds(df, w["down"], st["a"], st["u"], name=f"ds_{tag}")
    token = latest(done({"gate": _mm_tn(da, st["h"], after=token, tm=D_FF // 2, name=f"dwg_{tag}")}), token)
    token = latest(done({"up": _mm_tn(du, st["h"], after=token, tm=D_FF // 2, name=f"dwu_{tag}")}), token)
    dh = _mm2(da, w["gate"], du, w["up"], after=token, name=f"dh_{tag}")
    dx_in, sum_dh, sum_dhx = _norm_bwd(st["x"], dh, dx_out, _row(gain), _row(mod_j[1]), name=f"norm_bwd_{tag}")
    dmod = jnp.concatenate([sum_dh, gain * sum_dhx, dgate], 0)
    return dx_in, dmod, (1.0 + mod_j[1]) * sum_dhx[0]


def _mix_fwd(x_in, w, gain, mod_j, bias, sinks_l, tag, after=None):
    st = {"x": x_in, "w": w}
    st["h"] = _norm_fwd(x_in, _row(gain), _row(mod_j[1]), _row(mod_j[0]), name=f"norm_fwd_mix_{tag}", after=after)
    qkv = _mm(st["h"], w["in"], tb=True, tm=SEQ, b_rows=(0, D_QKV), name=f"qkv_{tag}")
    st["gates"] = _mm(st["h"], w["in"], tb=True, tm=SEQ, b_rows=(D_QKV, D_GATES), name=f"gates_{tag}")
    outs, st["mix"] = _mixer_fwd(qkv, bias, sinks_l, tag)
    st["merged"], *st["t"] = _merge_fwd(*outs, st["gates"], w["br_sb"], w["br_dil"], w["br_swa"], name=f"merge_fwd_{tag}")
    st["f"], x_out = _mm(st["merged"], w["out"], res=x_in, colscale=_row(mod_j[2]), emit_acc=True, name=f"out_{tag}")
    return x_out, st


def _mix_bwd(dx_out, st, gain, mod_j, bias, tag, done):
    w = st["w"]
    df, dgate = _gate_bwd(dx_out, st["f"], _row(mod_j[2]), 1.0, name=f"gate_bwd_mix_{tag}")
    g = {"out": _mm_tn(st["merged"], df, name=f"dw_out_{tag}")}
    dmerged = _mm(df, w["out"], tb=True, name=f"dmerged_{tag}")
    dgates, do_sb, do_dil, do_swa, dbr_sb, dbr_dil, dbr_swa = _merge_bwd(
        dmerged, *st["t"], st["gates"], w["br_sb"], w["br_dil"], w["br_swa"], name=f"merge_bwd_{tag}")
    g["br_sb"] = _mm_tn(st["t"][0], dbr_sb, name=f"dw_br_sb_{tag}")
    g["br_dil"] = _mm_tn(st["t"][1], dbr_dil, name=f"dw_br_dil_{tag}")
    g["br_swa"] = _mm_tn(st["t"][2], dbr_swa, name=f"dw_br_swa_{tag}")
    dqkv, dbias, dsinks = _mixer_bwd(st["mix"], bias, do_sb, do_dil, do_swa, tag)
    dw_qkv = _mm_tn(dqkv, st["h"], out_rows=D_QKV + D_GATES, name=f"dw_qkv_{tag}")
    g["in"] = _mm_tn(dgates, st["h"], out_rows=D_QKV + D_GATES, row0=D_QKV, prev=dw_qkv, name=f"dw_gates_{tag}")
    dh = _mm2(dqkv, w["in"], dgates, w["in"], after=done(g), tm=512, b_rows=(0, D_QKV), name=f"dh_mix_{tag}")
    dx_in, sum_dh, sum_dhx = _norm_bwd(st["x"], dh, dx_out, _row(gain), _row(mod_j[1]), name=f"norm_bwd_mix_{tag}")
    dmod = jnp.concatenate([sum_dh, gain * sum_dhx, dgate], 0)
    return dx_in, dmod, (1.0 + mod_j[1]) * sum_dhx[0], dbias, dsinks


def _local_step(x, target, mod, gains, weights_of, rel_bias, sinks, final_gain, grads_done):
    tables = jnp.asarray(_bucket_tables())
    bias = _bias_build(rel_bias, tables, name="bias_build")
    states, h = [], x
    for l in range(DEPTH):
        st = {}
        for j, piece in enumerate(PIECES):
            w, after = weights_of(l, piece, h)
            if piece == "mix":
                h, st[piece] = _mix_fwd(h, w, gains[l, j], mod[l, j], bias, sinks[l], f"l{l}", after)
            else:
                h, st[piece] = _ffn_fwd(h, w, gains[l, j], mod[l, j], f"{piece}_l{l}", after)
        states.append(st)
    loss, dx, dfinal = _final_loss(h, target, _row(final_gain), name="final_loss")
    dmods = [[None] * 3 for _ in range(DEPTH)]
    dgains = [[None] * 3 for _ in range(DEPTH)]
    dsinks = [None] * DEPTH
    dbias = None
    for l in reversed(range(DEPTH)):
        for j in reversed(range(3)):
            piece = PIECES[j]
            done = lambda grads, l=l, piece=piece: grads_done(l, piece, grads)
            if piece == "mix":
                dx, dmods[l][j], dgains[l][j], db, dsinks[l] = _mix_bwd(dx, states[l][piece], gains[l, j], mod[l, j], bias, f"l{l}", done)
                dbias = db if dbias is None else dbias + db
            else:
                dx, dmods[l][j], dgains[l][j] = _ffn_bwd(dx, states[l][piece], gains[l, j], mod[l, j], f"{piece}_l{l}", done)
    drel = _bias_grad(dbias, tables, name="bias_grad")[:, 0, :N_BUCKETS].T
    dmod = jnp.stack([jnp.stack(m) for m in dmods])
    dgain = jnp.stack([jnp.stack(g) for g in dgains])
    return loss, dx, dmod, dgain, dfinal[0], drel, jnp.stack(dsinks)


BR_ROWS = (H_SB * HEAD_DIM, 2 * HEAD_DIM, H_SWA_Q * HEAD_DIM)


def _lanes_unshard(g, lead):
    _, rows, _ = g.shape
    r = rows // lead
    return g.reshape(N_DEV, lead, r, LANES).transpose(1, 2, 0, 3).reshape(lead, r, N_DEV * LANES)


def _lanes_shard(full):
    lead, r, _ = full.shape
    return full.reshape(lead, r, N_DEV, LANES).transpose(2, 0, 1, 3).reshape(N_DEV, lead * r, LANES)


def _pack_rows(parts, dtype):
    flat = jnp.concatenate([p.astype(dtype).reshape(-1) for p in parts])
    pad = (-flat.shape[0]) % (16 * LANES)
    if pad:
        flat = jnp.concatenate([flat, jnp.zeros((pad,), dtype)])
    return flat.reshape(-1, LANES)


def _unshard(gathered, axis):
    moved = jnp.moveaxis(gathered, 0, axis)
    shape = list(moved.shape)
    shape[axis:axis + 2] = [shape[axis] * shape[axis + 1]]
    return moved.reshape(shape)


def kernel(x, c, w_ada, b_ada, norm_gain, w_ffn_gate, w_ffn_up, w_ffn_down, w_in, w_br_sb, w_br_dil, w_br_swa, w_out, sinks, rel_bias, final_gain, loss_target, m_w_ada, m_b_ada, m_norm_gain, m_w_ffn_gate, m_w_ffn_up, m_w_ffn_down, m_w_in, m_w_br_sb, m_w_br_dil, m_w_br_swa, m_w_out, m_sinks, m_rel_bias, m_final_gain, v_w_ada, v_b_ada, v_norm_gain, v_w_ffn_gate, v_w_ffn_up, v_w_ffn_down, v_w_in, v_w_br_sb, v_w_br_dil, v_w_br_swa, v_w_out, v_sinks, v_rel_bias, v_final_gain):
    me = 4 * lax.axis_index("x") + 2 * lax.axis_index("y") + lax.axis_index("c")
    d = D_MODEL
    gate_t, up_t, in_t = jnp.swapaxes(w_ffn_gate, 2, 3), jnp.swapaxes(w_ffn_up, 2, 3), jnp.swapaxes(w_in, 1, 2)

    def piece_shards(l, piece):
        bf = lambda t: t.astype(BF16)
        if piece == "mix":
            return [bf(in_t[l]), jnp.concatenate([bf(w_br_sb[l]), bf(w_br_dil[l]), bf(w_br_swa[l])], 0), bf(w_out[l])]
        i = PIECES.index(piece) // 2
        return [bf(gate_t[l, i]), bf(up_t[l, i]), bf(w_ffn_down[l, i])]

    br_off = np.concatenate([[0], np.cumsum(BR_ROWS)])

    def piece_weights(gathered, piece):
        if piece == "mix":
            g_in, g_br, g_out = gathered
            f_br = [_lanes_unshard(g_br[:, br_off[k]:br_off[k + 1]], 1)[0] for k in range(3)]
            return {"in": g_in.reshape(D_QKV + D_GATES, d), "br_sb": f_br[0], "br_dil": f_br[1], "br_swa": f_br[2],
                    "out": g_out.reshape(d, d)}
        return {n: g.reshape(D_FF, d) for n, g in zip(("gate", "up", "down"), gathered)}

    small, = _all_gather([_pack_rows([c, norm_gain], F32)], name="gather_cond")
    c_all = small[:, :d // LANES].reshape(N_DEV, d)
    gains = _unshard(small[:, d // LANES:d // LANES + 6].reshape(N_DEV, DEPTH, 3, LANES), 2)

    cols = w_ada.shape[2]
    mod_cols = jnp.stack([_ada_fwd(c_all, w_ada[l], name=f"ada_fwd_l{l}") for l in range(DEPTH)])
    mod_all, = _all_gather([_pack_rows([mod_cols], F32)], name="gather_mod")
    mod_all = mod_all.reshape(N_DEV, -1)[:, :DEPTH * N_DEV * cols].reshape(N_DEV, DEPTH, N_DEV, cols)
    mod_mine = lax.dynamic_index_in_dim(mod_all, me, axis=2, keepdims=False)
    mod = (mod_mine.transpose(1, 0, 2).reshape(DEPTH, N_DEV * cols) + b_ada).reshape(DEPTH, 3, 3, d)

    order = [(l, piece) for l in range(DEPTH) for piece in PIECES]
    ahead = 3
    in_flight, passed = {}, {}
    first = _all_gather(piece_shards(*order[0]), after=mod_all, name="gather_first")

    def start_gather(k, after):
        l, piece = order[k]
        in_flight[k], token = _relay_start(piece_shards(l, piece), after, name=f"gather_{piece}_l{l}_start")
        return token

    token = first[0]
    for k in range(1, 1 + ahead):
        token = start_gather(k, token)
    mod = mod + token[0, 0]

    def weights_of(l, piece, h):
        k = order.index((l, piece))
        token = start_gather(k + ahead, h) if k + ahead < len(order) and k + ahead not in in_flight else None
        for nxt in ([k] if k in (1, 2) else []) + ([k + 1] if 3 <= k + 1 < len(order) else []):
            nl, npiece = order[nxt]
            passed[nxt], token = _relay_pass(in_flight[nxt], h if token is None else token,
                                             name=f"gather_{npiece}_l{nl}_pass")
        if k == 0:
            return piece_weights(first, piece), token
        landed = _relay_wait(passed[k], h if token is None else token, name=f"gather_{piece}_l{l}_wait")
        return piece_weights(landed, piece), token

    exchanges, have, deferred = {}, {}, []

    def grads_done(l, piece, g):
        key = (l, piece)
        have.setdefault(key, {}).update(g)
        if piece == "mix":
            if len(have[key]) < 5:
                return None
            g = have[key]
            s_br = jnp.concatenate([_lanes_shard(g[n][None]) for n in ("br_sb", "br_dil", "br_swa")], 1)
            groups = [(("in", "br", "out"), [g["in"].reshape(N_DEV, -1, d), s_br, g["out"].reshape(N_DEV, -1, d)])]
        elif key == order[0]:
            deferred.extend(((n,), [t.reshape(N_DEV, -1, d)]) for n, t in g.items())
            return None
        elif len(have[key]) < 3:
            return None
        else:
            groups = [(("gate", "up", "down"), [have[key][n].reshape(N_DEV, -1, d) for n in ("gate", "up", "down")])]
        token = None
        for names, sg in groups:
            state, token = _exchange_start(sg, None, gather=False, name=f"exchange_{piece}_l{l}_{names[0]}_start")
            exchanges.setdefault(key, []).append((names, state))
        return token

    loss, dx, dmod, dgains, dfinal, drel, dsinks = _local_step(
        x[0], loss_target[0], mod, gains, weights_of, rel_bias, sinks, final_gain, grads_done)

    flat = lambda t: t.reshape(-1, t.shape[-1])
    transposed = lambda ts: tuple(flat(jnp.swapaxes(t, -1, -2)) for t in ts)
    families = {
        "gate": transposed((w_ffn_gate, m_w_ffn_gate, v_w_ffn_gate)), "up": transposed((w_ffn_up, m_w_ffn_up, v_w_ffn_up)),
        "down": tuple(flat(t) for t in (w_ffn_down, m_w_ffn_down, v_w_ffn_down)),
        "in": transposed((w_in, m_w_in, v_w_in)),
        "br": tuple(flat(jnp.concatenate(ts, 1)) for ts in ((w_br_sb, w_br_dil, w_br_swa), (m_w_br_sb, m_w_br_dil, m_w_br_swa),
                                                            (v_w_br_sb, v_w_br_dil, v_w_br_swa))),
        "out": tuple(flat(t) for t in (w_out, m_w_out, v_w_out))}
    parts, stepped = {}, {}

    def step(keys, after):
        for key in keys:
            for names, ex_state in exchanges[key]:
                landed = _exchange_wait(ex_state, after, gather=False, name=f"exchange_{key[1]}_l{key[0]}_{names[0]}_wait")
                parts.setdefault(key, {}).update(zip(names, landed))
                after = landed[0]
        for key in keys:
            l, piece = key
            for n, group in parts[key].items():
                w2, m2, v2 = families[n]
                rows = group.shape[1]
                row0 = (2 * l + PIECES.index(piece) // 2) * rows if piece != "mix" else l * rows
                stepped[n] = _reduce_adamw([group], w2, m2, v2, row0, stepped.get(n), after=after,
                                           name=f"reduce_adamw_{n}_{piece}_l{l}")
                after = stepped[n][1]
        return after

    small_parts = [dmod, dgains, dfinal, drel.T, dsinks, loss[0, :1]]
    small_sizes = [int(np.prod(p.shape)) for p in small_parts]
    small_all, = _all_gather([_pack_rows(small_parts, F32)], name="gather_small")
    token = small_all
    for names, sg in deferred:
        state, token = _exchange_start(sg, token, gather=False, name=f"exchange_ffn0_l0_{names[0]}_start")
        exchanges.setdefault(order[0], []).append((names, state))

    after_l1 = step([key for key in reversed(order) if key[0] == 1], token)
    small_sum = _sum_parts([small_all], name="sum_small").reshape(-1)
    offs = np.concatenate([[0], np.cumsum(small_sizes)])
    g_b_ada = small_sum[offs[0]:offs[1]].reshape(DEPTH, 9 * d)
    g_gain_full = small_sum[offs[1]:offs[2]].reshape(DEPTH, 3, d)
    g_norm_gain = lax.dynamic_slice_in_dim(g_gain_full, me * LANES, LANES, axis=2)
    g_final = small_sum[offs[2]:offs[3]]
    g_rel = small_sum[offs[3]:offs[4]].reshape(N_SOFT, N_BUCKETS).T
    g_sinks = small_sum[offs[4]:offs[5]].reshape(DEPTH, H_SWA_Q)
    loss_total = small_sum[offs[5]]

    dmod_all = small_all.reshape(N_DEV, -1)[:, :DEPTH * 9 * d].reshape(N_DEV, DEPTH, 9 * d)
    dmod_cols = lax.dynamic_slice_in_dim(dmod_all, me * cols, cols, axis=2)
    g_w_ada = jnp.stack([_ada_bwd(c_all.T, dmod_cols[:, l], name=f"ada_bwd_l{l}") for l in range(DEPTH)])

    small_state = {"w_ada": (w_ada, m_w_ada, v_w_ada), "b_ada": (b_ada, m_b_ada, v_b_ada),
                   "norm_gain": (norm_gain, m_norm_gain, v_norm_gain), "sinks": (sinks, m_sinks, v_sinks),
                   "rel_bias": (rel_bias, m_rel_bias, v_rel_bias), "final_gain": (final_gain, m_final_gain, v_final_gain)}
    grad, update = {}, {}
    for n, g in (("w_ada", g_w_ada), ("b_ada", g_b_ada), ("norm_gain", g_norm_gain), ("sinks", g_sinks),
                 ("rel_bias", g_rel), ("final_gain", g_final)):
        w, m, v = small_state[n]
        grad[n] = g
        if w.ndim == 1:
            update[n] = tuple(t.reshape(w.shape) for t in _adamw(_row(w), _row(g), _row(m), _row(v), name=f"adamw_{n}"))
        else:
            update[n] = _adamw(w, g, m, v, name=f"adamw_{n}")

    step([order[0]], step([order[2], order[1]], after_l1))

    def unflat(n, like, swapped):
        shape = jnp.swapaxes(like, -1, -2).shape if swapped else like.shape
        out = [t.reshape(shape) for t in stepped[n]]
        return [jnp.swapaxes(t, -1, -2) for t in out] if swapped else out

    results = {"w_ffn_gate": unflat("gate", w_ffn_gate, True), "w_ffn_up": unflat("up", w_ffn_up, True),
               "w_ffn_down": unflat("down", w_ffn_down, False), "w_in": unflat("in", w_in, True),
               "w_out": unflat("out", w_out, False)}
    br = [t.reshape(DEPTH, -1, LANES) for t in stepped["br"]]
    for k, n in enumerate(("w_br_sb", "w_br_dil", "w_br_swa")):
        results[n] = [t[:, br_off[k]:br_off[k + 1]] for t in br]
    for n, (g, dl, nm, nv) in results.items():
        grad[n], update[n] = g, (dl, nm, nv)

    names = ["w_ada", "b_ada", "norm_gain", "w_ffn_gate", "w_ffn_up", "w_ffn_down", "w_in", "w_br_sb", "w_br_dil",
             "w_br_swa", "w_out", "sinks", "rel_bias", "final_gain"]
    return (loss_total, dx[None], *[grad[n] for n in names], *[update[n][0] for n in names],
            *[update[n][1] for n in names], *[update[n][2] for n in names])
```

```python
import math

import numpy as np
import jax
import jax.numpy as jnp
from jax import lax
from jax.experimental import pallas as pl
from jax.experimental.pallas import tpu as pltpu

F32, BF16 = jnp.float32, jnp.bfloat16

SEQ, D_MODEL, D_FF, HEAD_DIM = 2048, 1024, 2816, 64
DEPTH = 2
BLK = 128
H_SB, H_DIL, H_SWA_Q, H_SWA_KV = 4, 6, 6, 2
DIL_PATTERNS = ((128, 1), (512, 4), (2048, 16))
SWA_WINDOW = 128
N_BUCKETS, MAX_REL_DIST = 32, 2048
RMS_EPS = 1e-6
D_QKV = 2560
D_GATES = 3 * D_MODEL
ADAM_LR, ADAM_B1, ADAM_B2, ADAM_EPS, ADAM_WD, ADAM_STEP = 0.001, 0.9, 0.999, 1e-08, 0.01, 10

N_DEV = 8
LANES = 128
NEG = -1e30
SB_TILE = 512
VMEM_LIMIT_BYTES = 48 * 1024 * 1024
HBM = pl.BlockSpec(memory_space=pltpu.HBM)
MESH = pl.DeviceIdType.MESH


def _tile(n, target):
    t = (min(n, target) // LANES) * LANES
    while t >= LANES:
        if n % t == 0:
            return t
        t -= LANES
    return n


def _row_tile(r, cap):
    t = (min(r, cap) // 16) * 16
    while t > 16 and r % t:
        t -= 16
    return t


def _params(semantics=None):
    return pltpu.CompilerParams(dimension_semantics=semantics, vmem_limit_bytes=VMEM_LIMIT_BYTES)


def _dot(a, b, ca, cb):
    return lax.dot_general(a, b, (((ca,), (cb,)), ((), ())), preferred_element_type=F32)


def _sigmoid(a):
    return 1.0 / (1.0 + jnp.exp(-a))


def _row(v):
    return v.reshape(1, -1)


def _all_gather(arrs, name, after=None):
    n = len(arrs)
    ins = list(arrs) + ([] if after is None else [after])

    def body(*refs):
        x_refs, out_refs = refs[:n], refs[len(ins):len(ins) + n]
        send_sems, recv_sems, local_sems = refs[len(ins) + n:]
        x, y, c = lax.axis_index("x"), lax.axis_index("y"), lax.axis_index("c")
        me, sibling = (x, y, c), (x, y, 1 - c)
        chips = [(1 - x, y), (x, 1 - y), (1 - x, 1 - y)]

        def slot(t, px, py, pc):
            return out_refs[t].at[4 * px + 2 * py + pc]

        def copy(t, k, block, to, src=None):
            return pltpu.make_async_remote_copy(
                src_ref=slot(t, *block) if src is None else src, dst_ref=slot(t, *block),
                send_sem=send_sems.at[7 * t + k], recv_sem=recv_sems.at[7 * t + k], device_id=to, device_id_type=MESH)

        mine = [pltpu.make_async_copy(x_refs[t], slot(t, *me), local_sems.at[t]) for t in range(n)]
        for cp in mine:
            cp.start()
        first = []
        for t in range(n):
            first.append(copy(t, 0, me, sibling, src=x_refs[t]))
            first += [copy(t, 1 + j, me, (*chip, c), src=x_refs[t]) for j, chip in enumerate(chips)]
        for cp in first:
            cp.start()
        passed = []
        for j, chip in enumerate(chips):
            for t in range(n):
                copy(t, 1 + j, (*chip, c), me).wait_recv()
                passed.append(copy(t, 4 + j, (*chip, c), sibling))
                passed[-1].start()
        for t in range(n):
            copy(t, 0, sibling, me).wait_recv()
        for j, chip in enumerate(chips):
            for t in range(n):
                copy(t, 4 + j, (*chip, 1 - c), me).wait_recv()
        for cp in first + passed:
            cp.wait_send()
        for cp in mine:
            cp.wait()

    return pl.pallas_call(
        body, name=name, out_shape=[jax.ShapeDtypeStruct((N_DEV,) + a.shape, a.dtype) for a in arrs],
        in_specs=[HBM] * n + [pl.BlockSpec(memory_space=pl.ANY)] * (len(ins) - n), out_specs=[HBM] * n,
        scratch_shapes=[pltpu.SemaphoreType.DMA((7 * n,)), pltpu.SemaphoreType.DMA((7 * n,)), pltpu.SemaphoreType.DMA((n,))],
    )(*ins)


def _direct_copies(x_refs, land_refs, send_sems, recv_sems, local_sems, gather):
    x, y, c = lax.axis_index("x"), lax.axis_index("y"), lax.axis_index("c")
    me = 4 * x + 2 * y + c
    sends, recvs = [], []
    for k in range(1, N_DEV):
        px = 1 - x if (k >> 2) & 1 else x
        py = 1 - y if (k >> 1) & 1 else y
        pc = 1 - c if k & 1 else c
        peer = 4 * px + 2 * py + pc
        for t, (x_ref, land_ref) in enumerate(zip(x_refs, land_refs)):
            sem = 7 * t + k - 1
            for out, src, slot in ((sends, x_ref if gather else x_ref.at[peer], me),
                                   (recvs, x_ref if gather else x_ref.at[me], peer)):
                out.append(pltpu.make_async_remote_copy(
                    src_ref=src, dst_ref=land_ref.at[slot], send_sem=send_sems.at[sem], recv_sem=recv_sems.at[sem],
                    device_id=(px, py, pc), device_id_type=MESH))
    own = [pltpu.make_async_copy(x_ref if gather else x_ref.at[me], land_ref.at[me], local_sems.at[t])
           for t, (x_ref, land_ref) in enumerate(zip(x_refs, land_refs))]
    return sends, recvs, own


SEM =pl.BlockSpec(memory_space=pltpu.SEMAPHORE)
ANY = pl.BlockSpec(memory_space=pl.ANY)
SIDE_EFFECT = pltpu.SideEffectType.DATAFLOW_SIDE_EFFECTING


def _exchange_start(arrs, after, *, gather, name):
    n = len(arrs)
    lands = [lax.empty(((N_DEV,) + a.shape) if gather else a.shape, a.dtype) for a in arrs]
    extra = [] if after is None else [after]

    def body(*refs):
        sems = refs[2 * n + len(extra):2 * n + len(extra) + 3]
        sends, _, own = _direct_copies(refs[:n], refs[n:2 * n], *sems, gather)
        for cp in own + sends:
            cp.start()
        refs[-1][...] = jnp.zeros_like(refs[-1])

    ops = [pltpu.with_memory_space_constraint(a, pltpu.HBM) for a in list(arrs) + lands]
    out = pl.pallas_call(
        body, name=name,
        out_shape=(pltpu.SemaphoreType.DMA((7 * n,)), pltpu.SemaphoreType.DMA((7 * n,)), pltpu.SemaphoreType.DMA((n,)),
                   *[pltpu.HBM(a.shape, a.dtype) for a in ops], jax.ShapeDtypeStruct((8, LANES), F32)),
        in_specs=[HBM] * (2 * n) + [ANY] * len(extra),
        out_specs=(SEM, SEM, SEM, *[HBM] * (2 * n), pl.BlockSpec(memory_space=pltpu.VMEM)),
        input_output_aliases={t: 3 + t for t in range(2 * n)},
        compiler_params=pltpu.CompilerParams(has_side_effects=SIDE_EFFECT),
    )(*ops, *extra)
    return (out[:3], out[3:3 + n], out[3 + n:3 + 2 * n]), out[-1]


def _exchange_wait(state, after, *, gather, name):
    sems, arrs, lands = state
    n = len(arrs)

    def body(*refs):
        sends, recvs, own = _direct_copies(refs[:n], refs[n:2 * n], *refs[2 * n:2 * n + 3], gather)
        for cp in own:
            cp.wait()
        for cp in sends:
            cp.wait_send()
        for cp in recvs:
            cp.wait_recv()

    out = pl.pallas_call(
        body, name=name, out_shape=tuple(pltpu.HBM(a.shape, a.dtype) for a in list(arrs) + list(lands)),
        in_specs=[HBM] * (2 * n) + [SEM, SEM, SEM, ANY], out_specs=tuple([HBM] * (2 * n)),
        input_output_aliases={t: t for t in range(2 * n)},
        compiler_params=pltpu.CompilerParams(has_side_effects=SIDE_EFFECT),
    )(*arrs, *lands, *sems, after)
    return out[n:]


def _relay_copies(x_refs, land_refs, sems_a, sems_b):
    x, y, c = lax.axis_index("x"), lax.axis_index("y"), lax.axis_index("c")
    me = 4 * x + 2 * y + c
    sibling = (x, y, 1 - c)
    chips = [(1 - x, y), (x, 1 - y), (1 - x, 1 - y)]

    def slot(px, py, pc):
        return 4 * px + 2 * py + pc

    def copy(src, land_ref, dst_slot, send_sems, recv_sems, k, to):
        return pltpu.make_async_remote_copy(src_ref=src, dst_ref=land_ref.at[dst_slot], send_sem=send_sems.at[k],
                                            recv_sem=recv_sems.at[k], device_id=to, device_id_type=MESH)

    a_send, a_recv, a_own, b_send, b_recv = [], [], [], [], []
    for t, (x_ref, land_ref) in enumerate(zip(x_refs, land_refs)):
        peers = [sibling] + [(*chip, c) for chip in chips]
        if sems_a is not None:
            for k, peer in enumerate(peers):
                a_send.append(copy(x_ref, land_ref, me, sems_a[0], sems_a[1], 4 * t + k, peer))
                a_recv.append(copy(x_ref, land_ref, slot(*peer), sems_a[0], sems_a[1], 4 * t + k, peer))
            a_own.append(pltpu.make_async_copy(x_ref, land_ref.at[me], sems_a[2].at[t]))
        if sems_b is not None:
            for j, chip in enumerate(chips):
                b_send.append(copy(land_ref.at[slot(*chip, c)], land_ref, slot(*chip, c), sems_b[0], sems_b[1], 3 * t + j, sibling))
                b_recv.append(copy(land_ref.at[slot(*chip, c)], land_ref, slot(*chip, 1 - c), sems_b[0], sems_b[1], 3 * t + j,
                                   sibling))
    return (a_send, a_recv, a_own), (b_send, b_recv)


def _relay_start(arrs, after, name):
    n = len(arrs)
    lands = [lax.empty((N_DEV,) + a.shape, a.dtype) for a in arrs]

    def body(*refs):
        (sends, _, own), _ = _relay_copies(refs[:n], refs[n:2 * n], refs[2 * n + 1:2 * n + 4], None)
        for cp in own + sends:
            cp.start()
        refs[-1][...] = jnp.zeros_like(refs[-1])

    ops = [pltpu.with_memory_space_constraint(a, pltpu.HBM) for a in list(arrs) + lands]
    out = pl.pallas_call(
        body, name=name,
        out_shape=(pltpu.SemaphoreType.DMA((4 * n,)), pltpu.SemaphoreType.DMA((4 * n,)), pltpu.SemaphoreType.DMA((n,)),
                   *[pltpu.HBM(a.shape, a.dtype) for a in ops], jax.ShapeDtypeStruct((8, LANES), F32)),
        in_specs=[HBM] * (2 * n) + [ANY],
        out_specs=(SEM, SEM, SEM, *[HBM] * (2 * n), pl.BlockSpec(memory_space=pltpu.VMEM)),
        input_output_aliases={t: 3 + t for t in range(2 * n)},
        compiler_params=pltpu.CompilerParams(has_side_effects=SIDE_EFFECT),
    )(*ops, after)
    return (out[:3], out[3:3 + n], out[3 + n:3 + 2 * n]), out[-1]


def _relay_pass(state, after, name):
    sems_a, arrs, lands = state
    n = len(arrs)

    def body(*refs):
        sems_b = refs[2 * n + 4:2 * n + 6]
        (a_send, a_recv, a_own), (b_send, _) = _relay_copies(refs[:n], refs[n:2 * n], refs[2 * n:2 * n + 3], sems_b)
        for cp in a_own:
            cp.wait()
        for cp in a_send:
            cp.wait_send()
        for cp in a_recv:
            cp.wait_recv()
        for cp in b_send:
            cp.start()
        refs[-1][...] = jnp.zeros_like(refs[-1])

    out = pl.pallas_call(
        body, name=name,
        out_shape=(pltpu.SemaphoreType.DMA((3 * n,)), pltpu.SemaphoreType.DMA((3 * n,)),
                   *[pltpu.HBM(a.shape, a.dtype) for a in list(arrs) + list(lands)], jax.ShapeDtypeStruct((8, LANES), F32)),
        in_specs=[HBM] * (2 * n) + [SEM, SEM, SEM, ANY],
        out_specs=(SEM, SEM, *[HBM] * (2 * n), pl.BlockSpec(memory_space=pltpu.VMEM)),
        input_output_aliases={t: 2 + t for t in range(2 * n)},
        compiler_params=pltpu.CompilerParams(has_side_effects=SIDE_EFFECT),
    )(*arrs, *lands, *sems_a, after)
    return (out[:2], out[2:2 + n], out[2 + n:2 + 2 * n]), out[-1]


def _relay_wait(state, after, name):
    sems_b, arrs, lands = state
    n = len(arrs)

    def body(*refs):
        _, (b_send, b_recv) = _relay_copies(refs[:n], refs[n:2 * n], None, refs[2 * n:2 * n + 2])
        for cp in b_send:
            cp.wait_send()
        for cp in b_recv:
            cp.wait_recv()

    out = pl.pallas_call(
        body, name=name, out_shape=tuple(pltpu.HBM(a.shape, a.dtype) for a in list(arrs) + list(lands)),
        in_specs=[HBM] * (2 * n) + [SEM, SEM, ANY], out_specs=tuple([HBM] * (2 * n)),
        input_output_aliases={t: t for t in range(2 * n)},
        compiler_params=pltpu.CompilerParams(has_side_effects=SIDE_EFFECT),
    )(*arrs, *lands, *sems_b, after)
    return out[n:]


def _sum_parts(groups, name):
    n, r, cdim = groups[0].shape
    tr = _row_tile(r, max(16, (1 << 21) // (n * cdim * groups[0].dtype.itemsize)))
    steps = r // tr

    def body(*refs):
        o_ref = refs[-1]
        gg = pl.program_id(0)
        for gi in range(len(groups)):
            @pl.when(gg == gi)
            def _(gi=gi):
                acc = refs[gi][0].astype(F32)
                for k in range(1, n):
                    acc = acc + refs[gi][k].astype(F32)
                o_ref[...] = acc

    def in_spec(gi):
        return pl.BlockSpec((n, tr, cdim), lambda gg, i: (0, jnp.where(gg == gi, i, 0), 0))

    return pl.pallas_call(
        body, name=name, out_shape=jax.ShapeDtypeStruct((len(groups) * r, cdim), F32), grid=(len(groups), steps),
        in_specs=[in_spec(gi) for gi in range(len(groups))],
        out_specs=pl.BlockSpec((tr, cdim), lambda gg, i: (gg * steps + i, 0)),
        compiler_params=_params(("parallel", "parallel")),
    )(*groups)


def _mm_tn(a, b, *, name, after=None, tm=512, tn=1024, out_rows=None, row0=0, prev=None):
    k, m = a.shape
    n = b.shape[1]
    tm, tn = _tile(m, tm), _tile(n, tn)
    out_rows = m if out_rows is None else out_rows

    def body(a_ref, b_ref, *rest):
        o_ref, at_ref = rest[-2], rest[-1]

        @pl.when(pl.program_id(1) == 0)
        def _():
            at_ref[...] = a_ref[...].astype(BF16).T

        o_ref[...] = _dot(at_ref[...], b_ref[...].astype(BF16), 1, 0).astype(BF16)

    ins = [a, b] + [t for t in (after, prev) if t is not None]
    return pl.pallas_call(
        body, name=name, out_shape=jax.ShapeDtypeStruct((out_rows, n), BF16), grid=(m // tm, n // tn),
        in_specs=[pl.BlockSpec((k, tm), lambda i, j: (0, i)), pl.BlockSpec((k, tn), lambda i, j: (0, j))] + [ANY] * (len(ins) - 2),
        out_specs=pl.BlockSpec((tm, tn), lambda i, j: (row0 // tm + i, j)),
        input_output_aliases={} if prev is None else {len(ins) - 1: 0},
        scratch_shapes=[pltpu.VMEM((tm, k), BF16)], compiler_params=_params(("parallel", "arbitrary")),
    )(*ins)


def _mm2(a1, b1, a2, b2, *, name, after=None, tm=256, tn=1024, b_rows=None):
    m = a1.shape[0]
    n = b1.shape[1]
    tm, tn = _tile(m, tm), _tile(n, tn)

    def body(a1_ref, b1_ref, a2_ref, b2_ref, *rest):
        rest[-1][...] = (_dot(a1_ref[...].astype(BF16), b1_ref[...], 1, 0)
                         + _dot(a2_ref[...].astype(BF16), b2_ref[...], 1, 0))

    ins = [a1, b1, a2, b2] + ([] if after is None else [after])

    def a_spec(t):
        return pl.BlockSpec((tm, t.shape[1]), lambda i, j: (i, 0))

    def b_spec(t, a, which):
        if b_rows is None:
            return pl.BlockSpec((t.shape[0], tn), lambda i, j: (0, j))
        start = b_rows[which]
        return pl.BlockSpec((pl.Element(a.shape[1]), pl.Element(tn)), lambda i, j: (start, j * tn))

    return pl.pallas_call(
        body, name=name, out_shape=jax.ShapeDtypeStruct((m, n), F32), grid=(m // tm, n // tn),
        in_specs=[a_spec(a1), b_spec(b1, a1, 0), a_spec(a2), b_spec(b2, a2, 1)] + [ANY] * (len(ins) - 4),
        out_specs=pl.BlockSpec((tm, tn), lambda i, j: (i, j)), compiler_params=_params(("parallel", "parallel")),
    )(*ins)


def _mm(a, b, *, name, ta=False, tb=False, res=None, colscale=None, emit_acc=False,
        out_dtype=F32, tm=512, tn=512, b_rows=None):
    m, k = (a.shape[1], a.shape[0]) if ta else a.shape
    n = b.shape[0] if tb else b.shape[1]
    b_start = 0
    if b_rows is not None:
        b_start, n = b_rows
    tm, tn = _tile(m, tm), _tile(n, tn)
    ca, cb = (0 if ta else 1), (1 if tb else 0)
    a_spec = pl.BlockSpec((k, tm), lambda i, j: (0, i)) if ta else pl.BlockSpec((tm, k), lambda i, j: (i, 0))
    b_spec = (pl.BlockSpec((tn, k), lambda i, j: (b_start // tn + j, 0)) if tb
              else pl.BlockSpec((k, tn), lambda i, j: (0, j)))
    tile = pl.BlockSpec((tm, tn), lambda i, j: (i, j))
    ins, in_specs = [a, b], [a_spec, b_spec]
    if res is not None:
        ins.append(res)
        in_specs.append(tile)
    if colscale is not None:
        ins.append(colscale)
        in_specs.append(pl.BlockSpec((1, tn), lambda i, j: (0, j)))
    n_in = len(ins)

    def body(*refs):
        outs = refs[n_in:]
        acc = _dot(refs[0][...].astype(BF16), refs[1][...].astype(BF16), ca, cb)
        val, p = acc, 2
        if res is not None:
            r_val, p = refs[p][...], p + 1
        if colscale is not None:
            val = val * refs[p][...]
        if res is not None:
            val = r_val + val
        if emit_acc:
            outs[0][...] = acc
        outs[-1][...] = val.astype(out_dtype)

    out_shape = [jax.ShapeDtypeStruct((m, n), out_dtype)]
    out_specs = [tile]
    if emit_acc:
        out_shape.insert(0, jax.ShapeDtypeStruct((m, n), F32))
        out_specs.insert(0, tile)
    out = pl.pallas_call(
        body, name=name, out_shape=out_shape, grid=(m // tm, n // tn), in_specs=in_specs, out_specs=out_specs,
        compiler_params=_params(("parallel", "parallel")),
    )(*ins)
    return out if emit_acc else out[0]


def _norm_fwd(x, g, scale, shift, name, after=None):
    s, d = x.shape
    tr = 256

    def body(x_ref, g_ref, sc_ref, sh_ref, *rest):
        xv = x_ref[...]
        rstd = lax.rsqrt(jnp.mean(xv * xv, axis=-1, keepdims=True) + RMS_EPS)
        rest[-1][...] = (xv * rstd * g_ref[...] * (1.0 + sc_ref[...]) + sh_ref[...]).astype(BF16)

    rowspec = pl.BlockSpec((1, d), lambda i: (0, 0))
    ins = [x, g, scale, shift] + ([] if after is None else [after])
    return pl.pallas_call(
        body, name=name, out_shape=jax.ShapeDtypeStruct((s, d), BF16), grid=(s // tr,),
        in_specs=[pl.BlockSpec((tr, d), lambda i: (i, 0)), rowspec, rowspec, rowspec] + [ANY] * (len(ins) - 4),
        out_specs=pl.BlockSpec((tr, d), lambda i: (i, 0)),
        compiler_params=_params(("parallel",)),
    )(*ins)


def _dh_norm_bwd(a1, b1, a2, b2, x, dres, g, scale, nxt, *, name, after=None, b_rows=None):
    s, d = x.shape
    tm = 256
    n_fixed = 8

    def body(a1_ref, b1_ref, a2_ref, b2_ref, x_ref, dr_ref, g_ref, sc_ref, *rest):
        rest = rest[(1 if after is not None else 0):]
        if nxt is not None:
            f_ref, cs_ref, dx_ref, sa_ref, sb_ref, df_ref, dg_ref = rest
        else:
            dx_ref, sa_ref, sb_ref = rest

        @pl.when(pl.program_id(0) == 0)
        def _():
            sa_ref[...] = jnp.zeros_like(sa_ref)
            sb_ref[...] = jnp.zeros_like(sb_ref)
            if nxt is not None:
                dg_ref[...] = jnp.zeros_like(dg_ref)

        dhv = (_dot(a1_ref[...].astype(BF16), b1_ref[...], 1, 0) + _dot(a2_ref[...].astype(BF16), b2_ref[...], 1, 0))
        xv = x_ref[...]
        rstd = lax.rsqrt(jnp.mean(xv * xv, axis=-1, keepdims=True) + RMS_EPS)
        xhat = xv * rstd
        dxhat = dhv * (g_ref[...] * (1.0 + sc_ref[...]))
        mean_term = jnp.mean(dxhat * xhat, axis=-1, keepdims=True)
        dxv = dr_ref[...] + rstd * (dxhat - xhat * mean_term)
        dx_ref[...] = dxv
        sa_ref[...] += jnp.sum(dhv, axis=0, keepdims=True)
        sb_ref[...] += jnp.sum(dhv * xhat, axis=0, keepdims=True)
        if nxt is not None:
            df_ref[...] = (dxv * cs_ref[...]).astype(BF16)
            dg_ref[...] += nxt[2] * jnp.sum(dxv * f_ref[...], axis=0, keepdims=True)

    def a_spec(t):
        return pl.BlockSpec((tm, t.shape[1]), lambda i: (i, 0))

    def b_spec(t, a, which):
        if b_rows is None:
            return pl.BlockSpec((t.shape[0], d), lambda i: (0, 0))
        start = b_rows[which]
        return pl.BlockSpec((pl.Element(a.shape[1]), pl.Element(d)), lambda i: (start, 0))

    rowspec = pl.BlockSpec((1, d), lambda i: (0, 0))
    tile = pl.BlockSpec((tm, d), lambda i: (i, 0))
    ins = [a1, b1, a2, b2, x, dres, g, scale] + ([] if after is None else [after])
    in_specs = [a_spec(a1), b_spec(b1, a1, 0), a_spec(a2), b_spec(b2, a2, 1), tile, tile, rowspec, rowspec]
    in_specs += [ANY] * (len(ins) - n_fixed)
    out_shape = [jax.ShapeDtypeStruct((s, d), F32), jax.ShapeDtypeStruct((1, d), F32), jax.ShapeDtypeStruct((1, d), F32)]
    out_specs = [tile, rowspec, rowspec]
    if nxt is not None:
        ins += [nxt[0], nxt[1]]
        in_specs += [tile, rowspec]
        out_shape += [jax.ShapeDtypeStruct((s, d), BF16), jax.ShapeDtypeStruct((1, d), F32)]
        out_specs += [tile, rowspec]
    out = pl.pallas_call(
        body, name=name, out_shape=out_shape, grid=(s // tm,), in_specs=in_specs, out_specs=out_specs,
        compiler_params=_params(("arbitrary",)),
    )(*ins)
    return out[0], out[1], out[2], (None if nxt is None else (out[3], out[4]))


def _gate_bwd(dxn, f, colscale, coef, name):
    s, d = dxn.shape
    tr = 256

    def body(dx_ref, f_ref, cs_ref, df_ref, dg_ref):
        @pl.when(pl.program_id(0) == 0)
        def _():
            dg_ref[...] = jnp.zeros_like(dg_ref)

        dxv = dx_ref[...]
        df_ref[...] = (dxv * cs_ref[...]).astype(BF16)
        dg_ref[...] += coef * jnp.sum(dxv * f_ref[...], axis=0, keepdims=True)

    rowspec = pl.BlockSpec((1, d), lambda i: (0, 0))
    tile = pl.BlockSpec((tr, d), lambda i: (i, 0))
    return pl.pallas_call(
        body, name=name, out_shape=[jax.ShapeDtypeStruct((s, d), BF16), jax.ShapeDtypeStruct((1, d), F32)],
        grid=(s // tr,), in_specs=[tile, tile, rowspec], out_specs=[tile, rowspec],
        compiler_params=_params(("arbitrary",)),
    )(dxn, f, colscale)


def _ffn_up(h, wg, wu, name, tm=SEQ, tn=256):
    s, d = h.shape
    f = wg.shape[0]

    def body(h_ref, wg_ref, wu_ref, a_ref, u_ref, s_ref):
        hv = h_ref[...]
        a = _dot(hv, wg_ref[...], 1, 1)
        u = _dot(hv, wu_ref[...], 1, 1)
        a_ref[...] = a.astype(BF16)
        u_ref[...] = u.astype(BF16)
        s_ref[...] = (a * _sigmoid(a) * u).astype(BF16)

    tile = pl.BlockSpec((tm, tn), lambda i, j: (i, j))
    wspec = pl.BlockSpec((tn, d), lambda i, j: (j, 0))
    return pl.pallas_call(
        body, name=name,
        out_shape=[jax.ShapeDtypeStruct((s, f), BF16), jax.ShapeDtypeStruct((s, f), BF16), jax.ShapeDtypeStruct((s, f), BF16)],
        grid=(s // tm, f // tn), in_specs=[pl.BlockSpec((tm, d), lambda i, j: (i, 0)), wspec, wspec],
        out_specs=[tile, tile, tile], compiler_params=_params(("parallel", "parallel")),
    )(h, wg, wu)


def _ffn_bwd_ds(df, wd, a, u, name, tm=SEQ, tn=256):
    s, d = df.shape
    f = wd.shape[0]

    def body(df_ref, wd_ref, a_ref, u_ref, da_ref, du_ref):
        ds = _dot(df_ref[...], wd_ref[...], 1, 1)
        av = a_ref[...].astype(F32)
        sg = _sigmoid(av)
        da_ref[...] = (ds * u_ref[...].astype(F32) * (sg * (1.0 + av * (1.0 - sg)))).astype(BF16)
        du_ref[...] = (ds * (av * sg)).astype(BF16)

    tile = pl.BlockSpec((tm, tn), lambda i, j: (i, j))
    return pl.pallas_call(
        body, name=name, out_shape=[jax.ShapeDtypeStruct((s, f), BF16), jax.ShapeDtypeStruct((s, f), BF16)],
        grid=(s // tm, f // tn),
        in_specs=[pl.BlockSpec((tm, d), lambda i, j: (i, 0)), pl.BlockSpec((tn, d), lambda i, j: (j, 0)), tile, tile],
        out_specs=[tile, tile], compiler_params=_params(("parallel", "parallel")),
    )(df, wd, a, u)


def _merge_fwd(o_sb, o_dil, o_swa, gates, wb_sb, wb_dil, wb_swa, name):
    s, d = SEQ, D_MODEL
    tm = 256

    def body(osb_ref, odl_ref, osw_ref, g_ref, wsb_ref, wdl_ref, wsw_ref, m_ref, tsb_ref, tdl_ref, tsw_ref):
        for h in range(osb_ref.shape[0]):
            tsb_ref[:, h * HEAD_DIM:(h + 1) * HEAD_DIM] = osb_ref[h].astype(BF16)
        for h in range(osw_ref.shape[0]):
            tsw_ref[:, h * HEAD_DIM:(h + 1) * HEAD_DIM] = osw_ref[h].astype(BF16)
        tdl_ref[...] = odl_ref[...].astype(BF16)
        acc = _sigmoid(g_ref[:, 0:d]) * _dot(tsb_ref[...], wsb_ref[...], 1, 0)
        acc += _sigmoid(g_ref[:, d:2 * d]) * _dot(tdl_ref[...], wdl_ref[...], 1, 0)
        acc += _sigmoid(g_ref[:, 2 * d:3 * d]) * _dot(tsw_ref[...], wsw_ref[...], 1, 0)
        m_ref[...] = acc.astype(BF16)

    def rows(w):
        return pl.BlockSpec((tm, w), lambda i: (i, 0))

    def heads(n):
        return pl.BlockSpec((n, tm, HEAD_DIM), lambda i: (0, i, 0))

    def whole(w):
        return pl.BlockSpec((w, d), lambda i: (0, 0))

    return pl.pallas_call(
        body, name=name, out_shape=[jax.ShapeDtypeStruct((s, w), BF16) for w in (d, 256, 128, 384)], grid=(s // tm,),
        in_specs=[heads(H_SB), rows(128), heads(H_SWA_Q), rows(3 * d), whole(256), whole(128), whole(384)],
        out_specs=[rows(d), rows(256), rows(128), rows(384)], compiler_params=_params(("parallel",)),
    )(o_sb, o_dil, o_swa, gates, wb_sb, wb_dil, wb_swa)


def _merge_bwd(dmerged, t_sb, t_dil, t_swa, gates, wb_sb, wb_dil, wb_swa, name):
    s, d = SEQ, D_MODEL
    tm = 256

    def body(dm_ref, tsb_ref, tdl_ref, tsw_ref, g_ref, wsb_ref, wdl_ref, wsw_ref,
             dg_ref, dosb_ref, dodl_ref, dosw_ref, dbsb_ref, dbdl_ref, dbsw_ref):
        dm = dm_ref[...]
        for idx, (t_ref, w_ref, do_ref, db_ref) in enumerate((
                (tsb_ref, wsb_ref, dosb_ref, dbsb_ref), (tdl_ref, wdl_ref, dodl_ref, dbdl_ref),
                (tsw_ref, wsw_ref, dosw_ref, dbsw_ref))):
            w = w_ref[...]
            br = _dot(t_ref[...], w, 1, 0)
            sg = _sigmoid(g_ref[:, idx * d:(idx + 1) * d])
            dbr = (dm * sg).astype(BF16)
            dg_ref[:, idx * d:(idx + 1) * d] = (dm * br * (sg * (1.0 - sg))).astype(BF16)
            db_ref[...] = dbr
            do = _dot(dbr, w, 1, 1)
            if len(do_ref.shape) == 2:
                do_ref[...] = do
            else:
                for h in range(do_ref.shape[0]):
                    do_ref[h] = do[:, h * HEAD_DIM:(h + 1) * HEAD_DIM]

    def rows(w):
        return pl.BlockSpec((tm, w), lambda i: (i, 0))

    def heads(n):
        return pl.BlockSpec((n, tm, HEAD_DIM), lambda i: (0, i, 0))

    def whole(w):
        return pl.BlockSpec((w, d), lambda i: (0, 0))

    def shp(w, dt):
        return jax.ShapeDtypeStruct((s, w), dt)

    def hshp(n):
        return jax.ShapeDtypeStruct((n, s, HEAD_DIM), F32)

    return pl.pallas_call(
        body, name=name,
        out_shape=[shp(3 * d, BF16), hshp(H_SB), shp(128, F32), hshp(H_SWA_Q), shp(d, BF16), shp(d, BF16), shp(d, BF16)],
        grid=(s // tm,),
        in_specs=[rows(d), rows(256), rows(128), rows(384), rows(3 * d), whole(256), whole(128), whole(384)],
        out_specs=[rows(3 * d), heads(H_SB), rows(128), heads(H_SWA_Q), rows(d), rows(d), rows(d)],
        compiler_params=_params(("parallel",)),
    )(dmerged, t_sb, t_dil, t_swa, gates, wb_sb, wb_dil, wb_swa)


def _final_loss(x, target, g, name):
    s, d = x.shape
    tr = 256

    def body(x_ref, t_ref, g_ref, loss_ref, dx_ref, dg_ref):
        @pl.when(pl.program_id(0) == 0)
        def _():
            loss_ref[...] = jnp.zeros_like(loss_ref)
            dg_ref[...] = jnp.zeros_like(dg_ref)

        xv = x_ref[...]
        gv = g_ref[...]
        rstd = lax.rsqrt(jnp.mean(xv * xv, axis=-1, keepdims=True) + RMS_EPS)
        xhat = xv * rstd
        err = xhat * gv - t_ref[...]
        loss_ref[...] += 0.5 * jnp.sum(jnp.mean(err * err, axis=-1, keepdims=True))
        dy = err * (1.0 / d)
        dxhat = dy * gv
        mean_term = jnp.mean(dxhat * xhat, axis=-1, keepdims=True)
        dx_ref[...] = rstd * (dxhat - xhat * mean_term)
        dg_ref[...] += jnp.sum(dy * xhat, axis=0, keepdims=True)

    rowspec = pl.BlockSpec((1, d), lambda i: (0, 0))
    tile = pl.BlockSpec((tr, d), lambda i: (i, 0))
    return pl.pallas_call(
        body, name=name,
        out_shape=[jax.ShapeDtypeStruct((1, LANES), F32), jax.ShapeDtypeStruct((s, d), F32), jax.ShapeDtypeStruct((1, d), F32)],
        grid=(s // tr,), in_specs=[tile, tile, rowspec],
        out_specs=[pl.BlockSpec((1, LANES), lambda i: (0, 0)), tile, rowspec],
        compiler_params=_params(("arbitrary",)),
    )(x, target, g)


def _adamw(w, g, m, v, name):
    shape = w.shape
    cols = shape[-1]
    rows = int(np.prod(shape[:-1])) if len(shape) > 1 else 1
    tr = rows
    for cand in (1024, 512, 256, 128, 64, 32, 16, 8):
        if rows % cand == 0 and rows > cand and cand * cols * 4 <= (1 << 21):
            tr = cand
            break

    def body(w_ref, g_ref, m_ref, v_ref, d_ref, nm_ref, nv_ref):
        d_ref[...], nm_ref[...], nv_ref[...] = _adam_update(w_ref[...], g_ref[...], m_ref[...], v_ref[...])

    tile = pl.BlockSpec((tr, cols), lambda i: (i, 0))
    flat = [t.reshape(rows, cols) for t in (w, g, m, v)]
    out = pl.pallas_call(
        body, name=name, out_shape=[jax.ShapeDtypeStruct((rows, cols), F32)] * 3, grid=(rows // tr,),
        in_specs=[tile] * 4, out_specs=[tile] * 3, compiler_params=_params(("parallel",)),
    )(*flat)
    return tuple(t.reshape(shape) for t in out)


def _adam_update(w, gv, m, v):
    nm = ADAM_B1 * m + (1.0 - ADAM_B1) * gv
    nv = ADAM_B2 * v + (1.0 - ADAM_B2) * (gv * gv)
    m_hat = nm / (1.0 - ADAM_B1 ** ADAM_STEP)
    v_hat = nv / (1.0 - ADAM_B2 ** ADAM_STEP)
    return -ADAM_LR * (m_hat / (jnp.sqrt(v_hat) + ADAM_EPS) + ADAM_WD * w), nm, nv


def _reduce_adamw(groups, w, m, v, row0, prev, name, after=None):
    n, r, cdim = groups[0].shape
    rows = w.shape[0]
    tr = _row_tile(r, max(16, (1 << 22) // (n * cdim * groups[0].dtype.itemsize)))
    steps = r // tr
    ng = len(groups)

    def body(*refs):
        w_ref, m_ref, v_ref = refs[ng:ng + 3]
        g_out, d_out, m_out, v_out = refs[-4:]
        gg = pl.program_id(0)
        for gi in range(ng):
            @pl.when(gg == gi)
            def _(gi=gi):
                acc = refs[gi][0].astype(F32)
                for k in range(1, n):
                    acc = acc + refs[gi][k].astype(F32)
                g_out[...] = acc
                d_out[...], m_out[...], v_out[...] = _adam_update(w_ref[...], acc, m_ref[...], v_ref[...])

    def part_spec(gi):
        return pl.BlockSpec((n, tr, cdim), lambda gg, i: (0, jnp.where(gg == gi, i, 0), 0))

    tile = pl.BlockSpec((tr, cdim), lambda gg, i: (row0 // tr + gg * steps + i, 0))
    extra = ([] if prev is None else list(prev)) + ([] if after is None else [after])
    return pl.pallas_call(
        body, name=name, out_shape=[jax.ShapeDtypeStruct((rows, cdim), F32)] * 4, grid=(ng, steps),
        in_specs=[part_spec(gi) for gi in range(ng)] + [tile] * 3 + [ANY] * len(extra), out_specs=[tile] * 4,
        input_output_aliases={} if prev is None else {ng + 3 + k: k for k in range(4)},
        compiler_params=_params(("parallel", "parallel")),
    )(*groups, w, m, v, *extra)


def _ada_fwd(c_all, w, name):
    n = w.shape[1]

    def body(c_ref, w_ref, o_ref):
        cv = c_ref[...]
        o_ref[...] = jnp.dot(cv * _sigmoid(cv), w_ref[...], preferred_element_type=F32, precision=lax.Precision.HIGHEST)

    return pl.pallas_call(body, name=name, out_shape=jax.ShapeDtypeStruct((N_DEV, n), F32), compiler_params=_params())(c_all, w)


def _ada_bwd(c_all_t, dmod, name):
    n = dmod.shape[1]

    def body(c_ref, d_ref, o_ref):
        cv = c_ref[...]
        o_ref[...] = jnp.dot(cv * _sigmoid(cv), d_ref[...], preferred_element_type=F32, precision=lax.Precision.HIGHEST)

    return pl.pallas_call(body, name=name, out_shape=jax.ShapeDtypeStruct((D_MODEL, n), F32), compiler_params=_params())(c_all_t, dmod)


def _bucket_tables():
    rel = np.arange(BLK)[:, None] + BLK - np.arange(2 * BLK)[None, :]
    max_exact = N_BUCKETS // 2

    def bucket(n):
        nf = np.maximum(n, 1).astype(np.float32)
        large = max_exact + (np.log(nf / np.float32(max_exact)) / np.float32(math.log(MAX_REL_DIST / max_exact))
                             * np.float32(N_BUCKETS - max_exact)).astype(np.int32)
        return np.where(n < max_exact, n, np.minimum(large, N_BUCKETS - 1))

    tabs = []
    for dil, max_dist in ((1, 128), (4, 128), (16, 128), (1, SWA_WINDOW - 1)):
        in_band = (rel >= 0) & (rel <= max_dist)
        tabs.append(np.where(in_band, bucket(np.maximum(rel, 0) * dil), -1))
    return np.stack(tabs).astype(np.int32)


N_SOFT = H_DIL + H_SWA_Q


def _table_of_head(h):
    return jnp.minimum(h // 2, 3)


def _bias_build(rel_bias, tables, name):
    def body(rel_ref, t_ref, o_ref):
        h = pl.program_id(0)
        tb = t_ref[0]
        out = jnp.full((BLK, 2 * BLK), NEG, F32)
        for b in range(N_BUCKETS):
            out = jnp.where(tb == b, rel_ref[b, h], out)
        o_ref[0] = out

    return pl.pallas_call(
        body, name=name, out_shape=jax.ShapeDtypeStruct((N_SOFT, BLK, 2 * BLK), F32), grid=(N_SOFT,),
        in_specs=[pl.BlockSpec(memory_space=pltpu.SMEM),
                  pl.BlockSpec((1, BLK, 2 * BLK), lambda h: (_table_of_head(h), 0, 0))],
        out_specs=pl.BlockSpec((1, BLK, 2 * BLK), lambda h: (h, 0, 0)),
        compiler_params=_params(("parallel",)),
    )(rel_bias, tables)


def _bias_grad(dbias, tables, name):
    def body(d_ref, t_ref, o_ref):
        tb = t_ref[0]
        dv = d_ref[0]
        lane = lax.broadcasted_iota(jnp.int32, (1, LANES), 1)
        out = jnp.zeros((1, LANES), F32)
        for b in range(N_BUCKETS):
            out = jnp.where(lane == b, jnp.sum(jnp.where(tb == b, dv, 0.0)), out)
        o_ref[0] = out

    return pl.pallas_call(
        body, name=name, out_shape=jax.ShapeDtypeStruct((N_SOFT, 1, LANES), F32), grid=(N_SOFT,),
        in_specs=[pl.BlockSpec((1, BLK, 2 * BLK), lambda h: (h, 0, 0)),
                  pl.BlockSpec((1, BLK, 2 * BLK), lambda h: (_table_of_head(h), 0, 0))],
        out_specs=pl.BlockSpec((1, 1, LANES), lambda h: (h, 0, 0)),
        compiler_params=_params(("parallel",)),
    )(dbias, tables)


def _band_layout(g, bias_div):
    assert g == 1 or bias_div == 1
    return bias_div if g == 1 else 1


def _band_specs(length, g, bias_div, offs):
    ns = _band_layout(g, bias_div)

    def seqs(off, div=1):
        return pl.BlockSpec((ns, length, HEAD_DIM), lambda s: (off // ns + s // div, 0, 0))

    xspecs = [seqs(offs[0]), seqs(offs[1], g), seqs(offs[2], g)]
    bspec = pl.BlockSpec((1, BLK, 2 * BLK), lambda s: (s, 0, 0))
    sspec = pl.BlockSpec((ns, 1, LANES), lambda s: (s, 0, 0))
    colspec = pl.BlockSpec((ns, length, 1), lambda s: (s, 0, 0))
    return xspecs, seqs(0), seqs(0, g), bspec, sspec, colspec


def _band_sweep(length, ns, one):
    nblk = length // BLK
    for qq in range(ns):
        if ns * nblk <= 16:
            for i in range(nblk):
                one(qq, i * BLK, max(i - 1, 0) * BLK, i == 0)
        else:
            def step(i, carry, qq=qq):
                one(qq, pl.multiple_of(i * BLK, BLK), pl.multiple_of(jnp.maximum(i - 1, 0) * BLK, BLK), i == 0)
                return carry

            lax.fori_loop(0, nblk, step, 0, unroll=2)


def _band_scores(q_ref, k_ref, b_ref, qq, kq, bq, cur, prv, first):
    qv = q_ref[qq, pl.ds(cur, BLK), :]
    bv = b_ref[bq]
    if first is True:
        sp = jnp.full((BLK, BLK), NEG, F32)
    else:
        sp = _dot(qv, k_ref[kq, pl.ds(prv, BLK), :], 1, 1) + bv[:, :BLK]
        sp = sp if first is False else jnp.where(first, NEG, sp)
    sc = _dot(qv, k_ref[kq, pl.ds(cur, BLK), :], 1, 1) + bv[:, BLK:]
    return qv, sp, sc


def _band_fwd(x, bias, sink, *, nq, offs, g, bias_div, has_sink, name):
    length = x.shape[1]
    ns = _band_layout(g, bias_div)

    def body(q_ref, k_ref, v_ref, b_ref, s_ref, o_ref, lse_ref):
        def one(qq, cur, prv, first):
            kq, bq = qq, 0
            _, sp, sc = _band_scores(q_ref, k_ref, b_ref, qq, kq, bq, cur, prv, first)
            m = jnp.maximum(jnp.max(sp, axis=1, keepdims=True), jnp.max(sc, axis=1, keepdims=True))
            if has_sink:
                sk = s_ref[qq][:, :1]
                m = jnp.maximum(m, sk)
            pp, pc = jnp.exp(sp - m), jnp.exp(sc - m)
            den = jnp.sum(pp, axis=1, keepdims=True) + jnp.sum(pc, axis=1, keepdims=True)
            if has_sink:
                den = den + jnp.exp(sk - m)
            acc = (_dot(pp.astype(BF16), v_ref[kq, pl.ds(prv, BLK), :], 1, 0)
                   + _dot(pc.astype(BF16), v_ref[kq, pl.ds(cur, BLK), :], 1, 0))
            o_ref[qq, pl.ds(cur, BLK), :] = acc / den
            lse_ref[qq, pl.ds(cur, BLK), :] = m + jnp.log(den)

        _band_sweep(length, ns, one)

    xspecs, qspec, _, bspec, sspec, colspec = _band_specs(length, g, bias_div, offs)
    return pl.pallas_call(
        body, name=name,
        out_shape=[jax.ShapeDtypeStruct((nq, length, HEAD_DIM), F32), jax.ShapeDtypeStruct((nq, length, 1), F32)],
        grid=(nq // ns,), in_specs=xspecs + [bspec, sspec],
        out_specs=[qspec, colspec], compiler_params=_params(("parallel",)),
    )(x, x, x, bias, sink)


def _band_bwd(x, bias, sink, o, lse, do, dlse, *, nq, offs, g, bias_div, has_sink, name):
    length = x.shape[1]
    ns = _band_layout(g, bias_div)
    nk, nbias = nq // g, nq // bias_div

    def body(q_ref, k_ref, v_ref, b_ref, s_ref, o_ref, lse_ref, do_ref, dlse_ref,
             dq_ref, dk_ref, dv_ref, db_ref, dsk_ref, dkp_ref, dvp_ref):
        for ref in (db_ref, dsk_ref, dkp_ref, dvp_ref):
            ref[...] = jnp.zeros_like(ref)

        @pl.when(pl.program_id(0) % g == 0)
        def _():
            dk_ref[...] = jnp.zeros_like(dk_ref)
            dv_ref[...] = jnp.zeros_like(dv_ref)

        def one(qq, cur, prv, first):
            kq, bq = qq, 0
            qv, sp, sc = _band_scores(q_ref, k_ref, b_ref, qq, kq, bq, cur, prv, first)
            rows, prow = pl.ds(cur, BLK), pl.ds(prv, BLK)
            lse_v = lse_ref[qq, rows, :]
            pp, pc = jnp.exp(sp - lse_v), jnp.exp(sc - lse_v)
            dov = do_ref[qq, rows, :]
            dob = dov.astype(BF16)
            coef = dlse_ref[qq, rows, :] - jnp.sum(dov * o_ref[qq, rows, :], axis=1, keepdims=True)
            dsp = pp * (_dot(dob, v_ref[kq, prow, :], 1, 1) + coef)
            dsc = pc * (_dot(dob, v_ref[kq, rows, :], 1, 1) + coef)
            dspb, dscb = dsp.astype(BF16), dsc.astype(BF16)
            dq_ref[qq, rows, :] = ((_dot(dspb, k_ref[kq, prow, :], 1, 0) + _dot(dscb, k_ref[kq, rows, :], 1, 0))
                                   * (HEAD_DIM ** -0.5))
            dk_ref[kq, rows, :] += _dot(dscb, qv, 0, 0)
            dkp_ref[kq, prow, :] += _dot(dspb, qv, 0, 0)
            dv_ref[kq, rows, :] += _dot(pc.astype(BF16), dob, 0, 0)
            dvp_ref[kq, prow, :] += _dot(pp.astype(BF16), dob, 0, 0)
            db_ref[bq, :, :BLK] += dsp
            db_ref[bq, :, BLK:] += dsc
            if has_sink:
                dsk_ref[qq] += jnp.sum(jnp.exp(s_ref[qq][:, :1] - lse_v) * coef)

        _band_sweep(length, ns, one)
        dk_ref[...] += dkp_ref[...]
        dv_ref[...] += dvp_ref[...]

    xspecs, qspec, kvspec, bspec, sspec, colspec = _band_specs(length, g, bias_div, offs)
    return pl.pallas_call(
        body, name=name,
        out_shape=[jax.ShapeDtypeStruct((nq, length, HEAD_DIM), F32), jax.ShapeDtypeStruct((nk, length, HEAD_DIM), F32),
                   jax.ShapeDtypeStruct((nk, length, HEAD_DIM), F32), jax.ShapeDtypeStruct((nbias, BLK, 2 * BLK), F32),
                   jax.ShapeDtypeStruct((nq, 1, LANES), F32)],
        grid=(nq // ns,),
        in_specs=xspecs + [bspec, sspec, qspec, colspec, qspec, colspec],
        out_specs=[qspec, kvspec, kvspec, bspec, sspec],
        scratch_shapes=[pltpu.VMEM((ns, length, HEAD_DIM), F32), pltpu.VMEM((ns, length, HEAD_DIM), F32)],
        compiler_params=_params(("arbitrary",)),
    )(x, x, x, bias, sink, o, lse, do, dlse)


TOK_TILE = 512


def _dil_merge(outs, lses, dout, name):
    tr = TOK_TILE
    dils = [d for _, d in DIL_PATTERNS]
    n = len(dils)
    o4 = [o.reshape(2, d, SEQ // d, HEAD_DIM) for o, d in zip(outs, dils)]
    l4 = [l.reshape(2, d, SEQ // d, 1) for l, d in zip(lses, dils)]
    o_specs = [pl.BlockSpec((2, d, tr // d, HEAD_DIM), lambda i: (0, 0, i, 0)) for d in dils]
    l_specs = [pl.BlockSpec((2, d, tr // d, 1), lambda i: (0, 0, i, 0)) for d in dils]
    tok = pl.BlockSpec((tr, 2 * HEAD_DIM), lambda i: (i, 0))
    scratch = ([pltpu.VMEM((tr, 2 * HEAD_DIM), F32) for _ in dils] + [pltpu.VMEM((tr, 1), F32) for _ in range(2 * n)]
               + [pltpu.VMEM((tr // d, 2 * HEAD_DIM), F32) for d in dils])

    def to_tokens(o_ref, l_ref, d, pair, cols, stage):
        for r in range(d):
            rows = pl.ds(r, tr // d, stride=d) if d > 1 else slice(None)
            stage[:, :HEAD_DIM] = o_ref[0, r]
            stage[:, HEAD_DIM:] = o_ref[1, r]
            pair[rows, :] = stage[...]
            for h in range(2):
                cols[h][rows, :] = l_ref[h, r]
        return pair[...], [cols[0][...], cols[1][...]]

    def weights(ls):
        left = lax.broadcasted_iota(jnp.int32, (tr, 2 * HEAD_DIM), 1) < HEAD_DIM
        per_head = []
        for h in range(2):
            m = ls[0][h]
            for g in range(1, n):
                m = jnp.maximum(m, ls[g][h])
            es = [jnp.exp(ls[g][h] - m) for g in range(n)]
            den = es[0]
            for e in es[1:]:
                den = den + e
            per_head.append([e / den for e in es])
        return per_head, [jnp.where(left, per_head[0][g], per_head[1][g]) for g in range(n)], left

    def load(refs):
        pairs, cols, stages = refs[:n], refs[n:3 * n], refs[3 * n:]
        return pairs, [cols[2 * g:2 * g + 2] for g in range(n)], stages

    if dout is None:
        def body(*refs):
            pairs, cols, stages = load(refs[2 * n + 1:])
            toks = [to_tokens(refs[g], refs[n + g], dils[g], pairs[g], cols[g], stages[g]) for g in range(n)]
            _, alphas, _ = weights([t[1] for t in toks])
            acc = alphas[0] * toks[0][0]
            for g in range(1, n):
                acc = acc + alphas[g] * toks[g][0]
            refs[2 * n][...] = acc

        return pl.pallas_call(
            body, name=name, out_shape=jax.ShapeDtypeStruct((SEQ, 2 * HEAD_DIM), F32), grid=(SEQ // tr,),
            in_specs=o_specs + l_specs, out_specs=tok, scratch_shapes=scratch, compiler_params=_params(("parallel",)),
        )(*o4, *l4)

    def body(*refs):
        do_refs, dl_refs = refs[2 * n + 1:3 * n + 1], refs[3 * n + 1:4 * n + 1]
        pairs, cols, stages = load(refs[4 * n + 1:])
        toks = [to_tokens(refs[g], refs[n + g], dils[g], pairs[g], cols[g], stages[g]) for g in range(n)]
        per_head, alphas, left = weights([t[1] for t in toks])
        dov = refs[2 * n][...]
        das = []
        for g in range(n):
            prod = dov * toks[g][0]
            das.append([jnp.sum(jnp.where(left, prod, 0.0), axis=1, keepdims=True),
                        jnp.sum(jnp.where(left, 0.0, prod), axis=1, keepdims=True)])
        dbar = [sum(per_head[h][g] * das[g][h] for g in range(n)) for h in range(2)]
        for g, d in enumerate(dils):
            pairs[g][...] = alphas[g] * dov
            for h in range(2):
                cols[g][h][...] = per_head[h][g] * (das[g][h] - dbar[h])
            for r in range(d):
                rows = pl.ds(r, tr // d, stride=d) if d > 1 else slice(None)
                v = pairs[g][rows, :]
                for h in range(2):
                    do_refs[g][h, r] = v[:, h * HEAD_DIM:(h + 1) * HEAD_DIM]
                    dl_refs[g][h, r] = cols[g][h][rows, :]

    out = pl.pallas_call(
        body, name=name,
        out_shape=[jax.ShapeDtypeStruct(o.shape, F32) for o in o4] + [jax.ShapeDtypeStruct(l.shape, F32) for l in l4],
        grid=(SEQ // tr,), in_specs=o_specs + l_specs + [tok], out_specs=o_specs + l_specs, scratch_shapes=scratch,
        compiler_params=_params(("parallel",)),
    )(*o4, *l4, dout)
    return [t.reshape(s.shape) for t, s in zip(out, list(outs) + list(lses))]


def _tri(cmp):
    r = lax.broadcasted_iota(jnp.int32, (SB_TILE, SB_TILE), 0)
    c = lax.broadcasted_iota(jnp.int32, (SB_TILE, SB_TILE), 1)
    return cmp(r, c).astype(BF16)


def _cum(x, tri, terms):
    acc, rest = None, x
    for _ in range(terms):
        part = rest.astype(BF16)
        rest = rest - part.astype(F32)
        d = _dot(part, tri, 1, 0)
        acc = d if acc is None else acc + d
    return acc


def _sb_logits(q, ks, diagonal):
    t = SB_TILE
    z = _dot(q, ks, 1, 1)
    e = jnp.exp(-jnp.abs(z))
    lf = -(jnp.maximum(z, 0.0) + jnp.log(1.0 + e))
    if not diagonal:
        return z, e, lf, None
    mask = lax.broadcasted_iota(jnp.int32, (t, t), 1) < lax.broadcasted_iota(jnp.int32, (t, t), 0)
    return z, e, jnp.where(mask, lf, 0.0), mask


def _sb_specs(h, s):
    t = SB_TILE
    tile = pl.BlockSpec((h, t, HEAD_DIM), lambda i: (0, i, 0))
    keys = pl.BlockSpec((h, s, HEAD_DIM), lambda i: (1, 0, 0))
    values = pl.BlockSpec((h, s, HEAD_DIM), lambda i: (2, 0, 0))
    return tile, keys, values, pl.BlockSpec((h, t, 1), lambda i: (0, i, 0))


def _sb_fwd(x, name):
    h, s = x.shape[0] // 3, x.shape[1]
    t = SB_TILE

    def body(q_ref, k_ref, v_ref, o_ref, tot_ref):
        i = pl.program_id(0)
        after = _tri(lambda r, c: r > c)

        def tile(j, carry, diagonal):
            rows = pl.ds(pl.multiple_of(j * t, t), t)
            out = []
            for hh, (right, acc) in enumerate(carry):
                z, _, lf, mask = _sb_logits(q_ref[hh], k_ref[hh, rows, :], diagonal)
                w = jnp.exp(z + lf + (right + _cum(lf, after, 2)))
                w = w if mask is None else jnp.where(mask, w, 0.0)
                out.append((right + jnp.sum(lf, axis=1, keepdims=True), acc + _dot(w.astype(BF16), v_ref[hh, rows, :], 1, 0)))
            return tuple(out)

        carry = tile(i, tuple((jnp.zeros((t, 1), F32), jnp.zeros((t, HEAD_DIM), F32)) for _ in range(h)), True)
        carry = lax.fori_loop(0, i, lambda jj, c: tile(i - 1 - jj, c, False), carry)
        for hh, (right, acc) in enumerate(carry):
            o_ref[hh] = acc
            tot_ref[hh] = right

    tile_spec, keys, values, col = _sb_specs(h, s)
    return pl.pallas_call(
        body, name=name, out_shape=[jax.ShapeDtypeStruct((h, s, HEAD_DIM), F32), jax.ShapeDtypeStruct((h, s, 1), F32)],
        grid=(s // t,), in_specs=[tile_spec, keys, values], out_specs=[tile_spec, col],
        compiler_params=_params(("parallel",)),
    )(x, x, x)


def _sb_bwd(x, tot, do, name):
    h, s = x.shape[0] // 3, x.shape[1]
    t = SB_TILE

    def body(q_ref, k_ref, v_ref, tot_ref, do_ref, dq_ref, dk_ref, dv_ref):
        i = pl.program_id(0)

        @pl.when(i == 0)
        def _():
            dk_ref[...] = jnp.zeros_like(dk_ref)
            dv_ref[...] = jnp.zeros_like(dv_ref)

        upto = _tri(lambda r, c: r <= c)
        before = _tri(lambda r, c: r < c)

        def tile(j, carry, diagonal):
            rows = pl.ds(pl.multiple_of(j * t, t), t)
            out = []
            for hh, (left, cleft, dq) in enumerate(carry):
                qv, ks, dob = q_ref[hh], k_ref[hh, rows, :], do_ref[hh].astype(BF16)
                z, e, lf, mask = _sb_logits(qv, ks, diagonal)
                between = tot_ref[hh] - (left + _cum(lf, upto, 2))
                w = jnp.exp(z + lf + between)
                w = w if mask is None else jnp.where(mask, w, 0.0)
                dlog = w * _dot(dob, v_ref[hh, rows, :], 1, 1)
                cfail = cleft + _cum(dlog, before, 2)
                sig = jnp.where(z >= 0.0, 1.0, e) / (1.0 + e)
                dz = dlog * (1.0 - sig) - sig * cfail
                dz = (dz if mask is None else jnp.where(mask, dz, 0.0)).astype(BF16)
                dk_ref[hh, rows, :] += _dot(dz, qv, 0, 0)
                dv_ref[hh, rows, :] += _dot(w.astype(BF16), dob, 0, 0)
                out.append((left + jnp.sum(lf, axis=1, keepdims=True), cleft + jnp.sum(dlog, axis=1, keepdims=True),
                            dq + _dot(dz, ks, 1, 0)))
            return tuple(out)

        zero = jnp.zeros((t, 1), F32)
        carry = lax.fori_loop(0, i, lambda j, c: tile(j, c, False),
                              tuple((zero, zero, jnp.zeros((t, HEAD_DIM), F32)) for _ in range(h)))
        for hh, (_, _, dq) in enumerate(tile(i, carry, True)):
            dq_ref[hh] = dq * (HEAD_DIM ** -0.5)

    tile_spec, keys, values, col = _sb_specs(h, s)
    full = pl.BlockSpec((h, s, HEAD_DIM), lambda i: (0, 0, 0))
    shp = jax.ShapeDtypeStruct((h, s, HEAD_DIM), F32)
    return pl.pallas_call(
        body, name=name, out_shape=[shp, shp, shp], grid=(s // t,),
        in_specs=[tile_spec, keys, values, col, tile_spec],
        out_specs=[tile_spec, full, full], compiler_params=_params(("arbitrary",)),
    )(x, x, x, tot, do)


COL_SB, COL_DIL, COL_SWA = 0, 3 * H_SB * HEAD_DIM, 3 * H_SB * HEAD_DIM + 3 * H_DIL * HEAD_DIM
N_SWA = H_SWA_Q + 2 * H_SWA_KV


def _dil_col(t, g):
    return COL_DIL + t * H_DIL * HEAD_DIM + g * 2 * HEAD_DIM


def _split_heads(qkv, name):
    tr = TOK_TILE
    scale = HEAD_DIM ** -0.5
    dils = [d for _, d in DIL_PATTERNS]

    def body(x_ref, sb_ref, d0_ref, d1_ref, d2_ref, swa_ref, pair):
        def head(col, scaled):
            v = x_ref[:, col:col + HEAD_DIM]
            return (v * scale if scaled else v).astype(BF16)

        for hh in range(3 * H_SB):
            sb_ref[hh] = head(COL_SB + hh * HEAD_DIM, hh < H_SB)
        for hh in range(N_SWA):
            swa_ref[hh] = head(COL_SWA + hh * HEAD_DIM, hh < H_SWA_Q)
        for t in range(3):
            for g, (d, out_ref) in enumerate(zip(dils, (d0_ref, d1_ref, d2_ref))):
                col = _dil_col(t, g)
                if d == 1:
                    for h in range(2):
                        out_ref[t * 2 + h] = head(col + h * HEAD_DIM, t == 0)
                    continue
                pair[...] = x_ref[:, col:col + 2 * HEAD_DIM]
                for r in range(d):
                    v = pair[pl.ds(r, tr // d, stride=d), :]
                    v = v * scale if t == 0 else v
                    for h in range(2):
                        out_ref[t * 2 * d + h * d + r] = v[:, h * HEAD_DIM:(h + 1) * HEAD_DIM].astype(BF16)

    def heads(n, length):
        return jax.ShapeDtypeStruct((n, length, HEAD_DIM), BF16)

    def spec(n, rows):
        return pl.BlockSpec((n, rows, HEAD_DIM), lambda i: (0, i, 0))

    return pl.pallas_call(
        body, name=name,
        out_shape=[heads(3 * H_SB, SEQ)] + [heads(6 * d, SEQ // d) for d in dils] + [heads(N_SWA, SEQ)],
        grid=(SEQ // tr,), in_specs=[pl.BlockSpec((tr, D_QKV), lambda i: (i, 0))],
        out_specs=[spec(3 * H_SB, tr)] + [spec(6 * d, tr // d) for d in dils] + [spec(N_SWA, tr)],
        scratch_shapes=[pltpu.VMEM((tr, 2 * HEAD_DIM), F32)], compiler_params=_params(("parallel",)),
    )(qkv)


def _join_heads(sb, dil, swa, name):
    tr = TOK_TILE
    dils = [d for _, d in DIL_PATTERNS]

    def body(*refs):
        sb_refs, dil_refs, swa_refs = refs[:3], [refs[3 + 3 * g:6 + 3 * g] for g in range(3)], refs[12:15]
        o_ref, pair, stages = refs[15], refs[16], refs[17:]

        def put(col, v):
            o_ref[:, col:col + v.shape[1]] = v.astype(BF16)

        for t in range(3):
            for h in range(H_SB):
                put(COL_SB + (t * H_SB + h) * HEAD_DIM, sb_refs[t][h])
        col = COL_SWA
        for ref in swa_refs:
            for h in range(ref.shape[0]):
                put(col, ref[h])
                col += HEAD_DIM
        for t in range(3):
            for g, d in enumerate(dils):
                ref, col = dil_refs[g][t], _dil_col(t, g)
                if d == 1:
                    for h in range(2):
                        put(col + h * HEAD_DIM, ref[h])
                    continue
                stage = stages[g - 1]
                for r in range(d):
                    stage[:, :HEAD_DIM] = ref[r]
                    stage[:, HEAD_DIM:] = ref[d + r]
                    pair[pl.ds(r, tr // d, stride=d), :] = stage[...]
                put(col, pair[...])

    def spec(n, rows):
        return pl.BlockSpec((n, rows, HEAD_DIM), lambda i: (0, i, 0))

    ins = list(sb) + [t for g in range(3) for t in dil[g]] + list(swa)
    in_specs = ([spec(H_SB, tr)] * 3 + [spec(2 * d, tr // d) for d in dils for _ in range(3)]
                + [spec(H_SWA_Q, tr), spec(H_SWA_KV, tr), spec(H_SWA_KV, tr)])
    return pl.pallas_call(
        body, name=name, out_shape=jax.ShapeDtypeStruct((SEQ, D_QKV), BF16), grid=(SEQ // tr,), in_specs=in_specs,
        out_specs=pl.BlockSpec((tr, D_QKV), lambda i: (i, 0)),
        scratch_shapes=[pltpu.VMEM((tr, 2 * HEAD_DIM), F32)] + [pltpu.VMEM((tr // d, 2 * HEAD_DIM), F32) for d in dils[1:]],
        compiler_params=_params(("parallel",)),
    )(*ins)


def _mixer_fwd(qkv, bias, sinks_l, tag):
    sb, d0, d1, d2, swa = _split_heads(qkv, name=f"split_heads_{tag}")
    st = {"sb": sb, "dil": (d0, d1, d2), "swa": swa}
    o_sb, st["sb_tot"] = _sb_fwd(sb, name=f"sb_fwd_{tag}")
    st["dil_out"], st["dil_lse"], st["dil_sink"] = [], [], []
    for gi, (_, d) in enumerate(DIL_PATTERNS):
        sink = jnp.zeros((2 * d, 1, LANES), F32)
        og, lg = _band_fwd(st["dil"][gi], bias[2 * gi:2 * gi + 2], sink, nq=2 * d, offs=(0, 2 * d, 4 * d), g=1, bias_div=d,
                           has_sink=False, name=f"dil{gi}_fwd_{tag}")
        st["dil_out"].append(og)
        st["dil_lse"].append(lg)
        st["dil_sink"].append(sink)
    o_dil = _dil_merge(st["dil_out"], st["dil_lse"], None, name=f"dil_merge_fwd_{tag}")
    st["swa_sink"] = jnp.broadcast_to(sinks_l.reshape(H_SWA_Q, 1, 1), (H_SWA_Q, 1, LANES))
    st["swa_out"] = _band_fwd(swa, bias[H_DIL:], st["swa_sink"], nq=H_SWA_Q, offs=(0, H_SWA_Q, H_SWA_Q + H_SWA_KV),
                              g=H_SWA_Q // H_SWA_KV, bias_div=1, has_sink=True, name=f"swa_fwd_{tag}")
    return (o_sb, o_dil, st["swa_out"][0]), st


def _mixer_bwd(st, bias, do_sb, do_dil, do_swa, tag):
    d_sb = _sb_bwd(st["sb"], st["sb_tot"], do_sb, name=f"sb_bwd_{tag}")
    dmerge = _dil_merge(st["dil_out"], st["dil_lse"], do_dil, name=f"dil_merge_bwd_{tag}")
    d_dil, dbs = [], []
    for gi, (_, d) in enumerate(DIL_PATTERNS):
        dq, dk, dv, db, _ = _band_bwd(st["dil"][gi], bias[2 * gi:2 * gi + 2], st["dil_sink"][gi], st["dil_out"][gi],
                                      st["dil_lse"][gi], dmerge[gi], dmerge[3 + gi], nq=2 * d, offs=(0, 2 * d, 4 * d),
                                      g=1, bias_div=d, has_sink=False, name=f"dil{gi}_bwd_{tag}")
        d_dil.append((dq, dk, dv))
        dbs.append(db)
    o_sw, l_sw = st["swa_out"]
    dq_sw, dk_sw, dv_sw, db_sw, dsink = _band_bwd(st["swa"], bias[H_DIL:], st["swa_sink"], o_sw, l_sw, do_swa,
                                                  jnp.zeros_like(l_sw), nq=H_SWA_Q, offs=(0, H_SWA_Q, H_SWA_Q + H_SWA_KV),
                                                  g=H_SWA_Q // H_SWA_KV, bias_div=1, has_sink=True, name=f"swa_bwd_{tag}")
    dqkv = _join_heads(d_sb, d_dil, (dq_sw, dk_sw, dv_sw), name=f"join_heads_{tag}")
    return dqkv, jnp.concatenate(dbs + [db_sw], 0), dsink[:, 0, 0]


PIECES = ("ffn0", "mix", "ffn1")


def _ffn_fwd(x_in, w, gain, mod_j, tag, after=None):
    st = {"x": x_in, "w": w}
    st["h"] = _norm_fwd(x_in, _row(gain), _row(mod_j[1]), _row(mod_j[0]), name=f"norm_fwd_{tag}", after=after)
    st["a"], st["u"], st["s"] = _ffn_up(st["h"], w["gate"], w["up"], name=f"up_{tag}")
    st["f"], x_out = _mm(st["s"], w["down"], res=x_in, colscale=_row(0.5 * mod_j[2]), emit_acc=True, tm=512, tn=1024,
                         name=f"down_{tag}")
    return x_out, st


def _ffn_bwd(dx_out, st, gain, mod_j, tag, done, pre, nxt):
    w = st["w"]

    def latest(new, old):
        return old if new is None else new

    df, dgate = pre or _gate_bwd(dx_out, st["f"], _row(0.5 * mod_j[2]), 0.5, name=f"gate_bwd_{tag}")
    token = done({"down": _mm_tn(st["s"], df, tm=D_FF // 2, name=f"dwd_{tag}")})
    da, du = _ffn_bwd_ds(df, w["down"], st["a"], st["u"], name=f"ds_{tag}")
    token = latest(done({"gate": _mm_tn(da, st["h"], after=token, tm=D_FF // 2, name=f"dwg_{tag}")}), token)
    token = latest(done({"up": _mm_tn(du, st["h"], after=token, tm=D_FF // 2, name=f"dwu_{tag}")}), token)
    dx_in, sum_dh, sum_dhx, made = _dh_norm_bwd(da, w["gate"], du, w["up"], st["x"], dx_out, _row(gain), _row(mod_j[1]), nxt,
                                                after=token, name=f"dh_{tag}")
    dmod = jnp.concatenate([sum_dh, gain * sum_dhx, dgate], 0)
    return dx_in, dmod, (1.0 + mod_j[1]) * sum_dhx[0], made


def _mix_fwd(x_in, w, gain, mod_j, bias, sinks_l, tag, after=None):
    st = {"x": x_in, "w": w}
    st["h"] = _norm_fwd(x_in, _row(gain), _row(mod_j[1]), _row(mod_j[0]), name=f"norm_fwd_mix_{tag}", after=after)
    qkv = _mm(st["h"], w["in"], tb=True, tm=SEQ, b_rows=(0, D_QKV), name=f"qkv_{tag}")
    st["gates"] = _mm(st["h"], w["in"], tb=True, tm=SEQ, b_rows=(D_QKV, D_GATES), name=f"gates_{tag}")
    outs, st["mix"] = _mixer_fwd(qkv, bias, sinks_l, tag)
    st["merged"], *st["t"] = _merge_fwd(*outs, st["gates"], w["br_sb"], w["br_dil"], w["br_swa"], name=f"merge_fwd_{tag}")
    st["f"], x_out = _mm(st["merged"], w["out"], res=x_in, colscale=_row(mod_j[2]), emit_acc=True, name=f"out_{tag}")
    return x_out, st


def _mix_bwd(dx_out, st, gain, mod_j, bias, tag, done, pre, nxt):
    w = st["w"]
    df, dgate = pre or _gate_bwd(dx_out, st["f"], _row(mod_j[2]), 1.0, name=f"gate_bwd_mix_{tag}")
    g = {"out": _mm_tn(st["merged"], df, name=f"dw_out_{tag}")}
    dmerged = _mm(df, w["out"], tb=True, name=f"dmerged_{tag}")
    dgates, do_sb, do_dil, do_swa, dbr_sb, dbr_dil, dbr_swa = _merge_bwd(
        dmerged, *st["t"], st["gates"], w["br_sb"], w["br_dil"], w["br_swa"], name=f"merge_bwd_{tag}")
    g["br_sb"] = _mm_tn(st["t"][0], dbr_sb, name=f"dw_br_sb_{tag}")
    g["br_dil"] = _mm_tn(st["t"][1], dbr_dil, name=f"dw_br_dil_{tag}")
    g["br_swa"] = _mm_tn(st["t"][2], dbr_swa, name=f"dw_br_swa_{tag}")
    dqkv, dbias, dsinks = _mixer_bwd(st["mix"], bias, do_sb, do_dil, do_swa, tag)
    dw_qkv = _mm_tn(dqkv, st["h"], out_rows=D_QKV + D_GATES, name=f"dw_qkv_{tag}")
    g["in"] = _mm_tn(dgates, st["h"], out_rows=D_QKV + D_GATES, row0=D_QKV, prev=dw_qkv, name=f"dw_gates_{tag}")
    dx_in, sum_dh, sum_dhx, made = _dh_norm_bwd(dqkv, w["in"], dgates, w["in"], st["x"], dx_out, _row(gain), _row(mod_j[1]),
                                                nxt, after=done(g), b_rows=(0, D_QKV), name=f"dh_mix_{tag}")
    dmod = jnp.concatenate([sum_dh, gain * sum_dhx, dgate], 0)
    return dx_in, dmod, (1.0 + mod_j[1]) * sum_dhx[0], dbias, dsinks, made


def _local_step(x, target, mod, gains, weights_of, rel_bias, sinks, final_gain, grads_done):
    tables = jnp.asarray(_bucket_tables())
    bias = _bias_build(rel_bias, tables, name="bias_build")
    states, h = [], x
    for l in range(DEPTH):
        st = {}
        for j, piece in enumerate(PIECES):
            w, after = weights_of(l, piece, h)
            if piece == "mix":
                h, st[piece] = _mix_fwd(h, w, gains[l, j], mod[l, j], bias, sinks[l], f"l{l}", after)
            else:
                h, st[piece] = _ffn_fwd(h, w, gains[l, j], mod[l, j], f"{piece}_l{l}", after)
        states.append(st)
    loss, dx, dfinal = _final_loss(h, target, _row(final_gain), name="final_loss")
    dmods = [[None] * 3 for _ in range(DEPTH)]
    dgains = [[None] * 3 for _ in range(DEPTH)]
    dsinks = [None] * DEPTH
    dbias, made = None, None
    sweep = [(l, j) for l in reversed(range(DEPTH)) for j in reversed(range(3))]
    for k, (l, j) in enumerate(sweep):
        piece = PIECES[j]
        done = lambda grads, l=l, piece=piece: grads_done(l, piece, grads)
        nxt = None
        if k + 1 < len(sweep):
            nl, nj = sweep[k + 1]
            coef = 1.0 if PIECES[nj] == "mix" else 0.5
            nxt = (states[nl][PIECES[nj]]["f"], _row(coef * mod[nl, nj, 2]), coef)
        if piece == "mix":
            dx, dmods[l][j], dgains[l][j], db, dsinks[l], made = _mix_bwd(
                dx, states[l][piece], gains[l, j], mod[l, j], bias, f"l{l}", done, made, nxt)
            dbias = db if dbias is None else dbias + db
        else:
            dx, dmods[l][j], dgains[l][j], made = _ffn_bwd(
                dx, states[l][piece], gains[l, j], mod[l, j], f"{piece}_l{l}", done, made, nxt)
    drel = _bias_grad(dbias, tables, name="bias_grad")[:, 0, :N_BUCKETS].T
    dmod = jnp.stack([jnp.stack(m) for m in dmods])
    dgain = jnp.stack([jnp.stack(g) for g in dgains])
    return loss, dx, dmod, dgain, dfinal[0], drel, jnp.stack(dsinks)


BR_ROWS = (H_SB * HEAD_DIM, 2 * HEAD_DIM, H_SWA_Q * HEAD_DIM)


def _lanes_unshard(g, lead):
    _, rows, _ = g.shape
    r = rows // lead
    return g.reshape(N_DEV, lead, r, LANES).transpose(1, 2, 0, 3).reshape(lead, r, N_DEV * LANES)


def _lanes_shard(full):
    lead, r, _ = full.shape
    return full.reshape(lead, r, N_DEV, LANES).transpose(2, 0, 1, 3).reshape(N_DEV, lead * r, LANES)


def _pack_rows(parts, dtype):
    flat = jnp.concatenate([p.astype(dtype).reshape(-1) for p in parts])
    pad = (-flat.shape[0]) % (16 * LANES)
    if pad:
        flat = jnp.concatenate([flat, jnp.zeros((pad,), dtype)])
    return flat.reshape(-1, LANES)


def _unshard(gathered, axis):
    moved = jnp.moveaxis(gathered, 0, axis)
    shape = list(moved.shape)
    shape[axis:axis + 2] = [shape[axis] * shape[axis + 1]]
    return moved.reshape(shape)


def kernel(x, c, w_ada, b_ada, norm_gain, w_ffn_gate, w_ffn_up, w_ffn_down, w_in, w_br_sb, w_br_dil, w_br_swa, w_out, sinks, rel_bias, final_gain, loss_target, m_w_ada, m_b_ada, m_norm_gain, m_w_ffn_gate, m_w_ffn_up, m_w_ffn_down, m_w_in, m_w_br_sb, m_w_br_dil, m_w_br_swa, m_w_out, m_sinks, m_rel_bias, m_final_gain, v_w_ada, v_b_ada, v_norm_gain, v_w_ffn_gate, v_w_ffn_up, v_w_ffn_down, v_w_in, v_w_br_sb, v_w_br_dil, v_w_br_swa, v_w_out, v_sinks, v_rel_bias, v_final_gain):
    me = 4 * lax.axis_index("x") + 2 * lax.axis_index("y") + lax.axis_index("c")
    d = D_MODEL
    gate_t, up_t, in_t = jnp.swapaxes(w_ffn_gate, 2, 3), jnp.swapaxes(w_ffn_up, 2, 3), jnp.swapaxes(w_in, 1, 2)

    def piece_shards(l, piece):
        bf = lambda t: t.astype(BF16)
        if piece == "mix":
            return [bf(in_t[l]), jnp.concatenate([bf(w_br_sb[l]), bf(w_br_dil[l]), bf(w_br_swa[l])], 0), bf(w_out[l])]
        i = PIECES.index(piece) // 2
        return [bf(gate_t[l, i]), bf(up_t[l, i]), bf(w_ffn_down[l, i])]

    br_off = np.concatenate([[0], np.cumsum(BR_ROWS)])

    def piece_weights(gathered, piece):
        if piece == "mix":
            g_in, g_br, g_out = gathered
            f_br = [_lanes_unshard(g_br[:, br_off[k]:br_off[k + 1]], 1)[0] for k in range(3)]
            return {"in": g_in.reshape(D_QKV + D_GATES, d), "br_sb": f_br[0], "br_dil": f_br[1], "br_swa": f_br[2],
                    "out": g_out.reshape(d, d)}
        return {n: g.reshape(D_FF, d) for n, g in zip(("gate", "up", "down"), gathered)}

    small, = _all_gather([_pack_rows([c, norm_gain], F32)], name="gather_cond")
    c_all = small[:, :d // LANES].reshape(N_DEV, d)
    gains = _unshard(small[:, d // LANES:d // LANES + 6].reshape(N_DEV, DEPTH, 3, LANES), 2)

    cols = w_ada.shape[2]
    mod_cols = jnp.stack([_ada_fwd(c_all, w_ada[l], name=f"ada_fwd_l{l}") for l in range(DEPTH)])
    mod_all, = _all_gather([_pack_rows([mod_cols], F32)], name="gather_mod")
    mod_all = mod_all.reshape(N_DEV, -1)[:, :DEPTH * N_DEV * cols].reshape(N_DEV, DEPTH, N_DEV, cols)
    mod_mine = lax.dynamic_index_in_dim(mod_all, me, axis=2, keepdims=False)
    mod = (mod_mine.transpose(1, 0, 2).reshape(DEPTH, N_DEV * cols) + b_ada).reshape(DEPTH, 3, 3, d)

    order = [(l, piece) for l in range(DEPTH) for piece in PIECES]
    ahead = 3
    in_flight, passed = {}, {}
    first = _all_gather(piece_shards(*order[0]), after=mod_all, name="gather_first")

    def start_gather(k, after):
        l, piece = order[k]
        in_flight[k], token = _relay_start(piece_shards(l, piece), after, name=f"gather_{piece}_l{l}_start")
        return token

    token = first[0]
    for k in range(1, 1 + ahead):
        token = start_gather(k, token)
    mod = mod + token[0, 0]

    def weights_of(l, piece, h):
        k = order.index((l, piece))
        token = start_gather(k + ahead, h) if k + ahead < len(order) and k + ahead not in in_flight else None
        for nxt in ([k] if k in (1, 2) else []) + ([k + 1] if 3 <= k + 1 < len(order) else []):
            nl, npiece = order[nxt]
            passed[nxt], token = _relay_pass(in_flight[nxt], h if token is None else token,
                                             name=f"gather_{npiece}_l{nl}_pass")
        if k == 0:
            return piece_weights(first, piece), token
        landed = _relay_wait(passed[k], h if token is None else token, name=f"gather_{piece}_l{l}_wait")
        return piece_weights(landed, piece), token

    exchanges, have, deferred = {}, {}, []

    def grads_done(l, piece, g):
        key = (l, piece)
        have.setdefault(key, {}).update(g)
        if piece == "mix":
            if len(have[key]) < 5:
                return None
            g = have[key]
            s_br = jnp.concatenate([_lanes_shard(g[n][None]) for n in ("br_sb", "br_dil", "br_swa")], 1)
            groups = [(("in", "br", "out"), [g["in"].reshape(N_DEV, -1, d), s_br, g["out"].reshape(N_DEV, -1, d)])]
        elif key == order[0]:
            deferred.extend(((n,), [t.reshape(N_DEV, -1, d)]) for n, t in g.items())
            return None
        elif len(have[key]) < 3:
            return None
        else:
            groups = [(("gate", "up", "down"), [have[key][n].reshape(N_DEV, -1, d) for n in ("gate", "up", "down")])]
        token = None
        for names, sg in groups:
            state, token = _exchange_start(sg, None, gather=False, name=f"exchange_{piece}_l{l}_{names[0]}_start")
            exchanges.setdefault(key, []).append((names, state))
        return token

    loss, dx, dmod, dgains, dfinal, drel, dsinks = _local_step(
        x[0], loss_target[0], mod, gains, weights_of, rel_bias, sinks, final_gain, grads_done)

    flat = lambda t: t.reshape(-1, t.shape[-1])
    transposed = lambda ts: tuple(flat(jnp.swapaxes(t, -1, -2)) for t in ts)
    families = {
        "gate": transposed((w_ffn_gate, m_w_ffn_gate, v_w_ffn_gate)), "up": transposed((w_ffn_up, m_w_ffn_up, v_w_ffn_up)),
        "down": tuple(flat(t) for t in (w_ffn_down, m_w_ffn_down, v_w_ffn_down)),
        "in": transposed((w_in, m_w_in, v_w_in)),
        "br": tuple(flat(jnp.concatenate(ts, 1)) for ts in ((w_br_sb, w_br_dil, w_br_swa), (m_w_br_sb, m_w_br_dil, m_w_br_swa),
                                                            (v_w_br_sb, v_w_br_dil, v_w_br_swa))),
        "out": tuple(flat(t) for t in (w_out, m_w_out, v_w_out))}
    parts, stepped = {}, {}

    def step(keys, after):
        for key in keys:
            for names, ex_state in exchanges[key]:
                landed = _exchange_wait(ex_state, after, gather=False, name=f"exchange_{key[1]}_l{key[0]}_{names[0]}_wait")
                parts.setdefault(key, {}).update(zip(names, landed))
                after = landed[0]
        for key in keys:
            l, piece = key
            for n, group in parts[key].items():
                w2, m2, v2 = families[n]
                rows = group.shape[1]
                row0 = (2 * l + PIECES.index(piece) // 2) * rows if piece != "mix" else l * rows
                stepped[n] = _reduce_adamw([group], w2, m2, v2, row0, stepped.get(n), after=after,
                                           name=f"reduce_adamw_{n}_{piece}_l{l}")
                after = stepped[n][1]
        return after

    small_parts = [dmod, dgains, dfinal, drel.T, dsinks, loss[0, :1]]
    small_sizes = [int(np.prod(p.shape)) for p in small_parts]
    small_all, = _all_gather([_pack_rows(small_parts, F32)], name="gather_small")
    token = small_all
    for names, sg in deferred:
        state, token = _exchange_start(sg, token, gather=False, name=f"exchange_ffn0_l0_{names[0]}_start")
        exchanges.setdefault(order[0], []).append((names, state))

    after_l1 = step([key for key in reversed(order) if key[0] == 1], token)
    small_sum = _sum_parts([small_all], name="sum_small").reshape(-1)
    offs = np.concatenate([[0], np.cumsum(small_sizes)])
    g_b_ada = small_sum[offs[0]:offs[1]].reshape(DEPTH, 9 * d)
    g_gain_full = small_sum[offs[1]:offs[2]].reshape(DEPTH, 3, d)
    g_norm_gain = lax.dynamic_slice_in_dim(g_gain_full, me * LANES, LANES, axis=2)
    g_final = small_sum[offs[2]:offs[3]]
    g_rel = small_sum[offs[3]:offs[4]].reshape(N_SOFT, N_BUCKETS).T
    g_sinks = small_sum[offs[4]:offs[5]].reshape(DEPTH, H_SWA_Q)
    loss_total = small_sum[offs[5]]

    dmod_all = small_all.reshape(N_DEV, -1)[:, :DEPTH * 9 * d].reshape(N_DEV, DEPTH, 9 * d)
    dmod_cols = lax.dynamic_slice_in_dim(dmod_all, me * cols, cols, axis=2)
    g_w_ada = jnp.stack([_ada_bwd(c_all.T, dmod_cols[:, l], name=f"ada_bwd_l{l}") for l in range(DEPTH)])

    small_state = {"w_ada": (w_ada, m_w_ada, v_w_ada), "b_ada": (b_ada, m_b_ada, v_b_ada),
                   "norm_gain": (norm_gain, m_norm_gain, v_norm_gain), "sinks": (sinks, m_sinks, v_sinks),
                   "rel_bias": (rel_bias, m_rel_bias, v_rel_bias), "final_gain": (final_gain, m_final_gain, v_final_gain)}
    grad, update = {}, {}
    for n, g in (("w_ada", g_w_ada), ("b_ada", g_b_ada), ("norm_gain", g_norm_gain), ("sinks", g_sinks),
                 ("rel_bias", g_rel), ("final_gain", g_final)):
        w, m, v = small_state[n]
        grad[n] = g
        if w.ndim == 1:
            update[n] = tuple(t.reshape(w.shape) for t in _adamw(_row(w), _row(g), _row(m), _row(v), name=f"adamw_{n}"))
        else:
            update[n] = _adamw(w, g, m, v, name=f"adamw_{n}")

    step([order[0]], step([order[2], order[1]], after_l1))

    def unflat(n, like, swapped):
        shape = jnp.swapaxes(like, -1, -2).shape if swapped else like.shape
        out = [t.reshape(shape) for t in stepped[n]]
        return [jnp.swapaxes(t, -1, -2) for t in out] if swapped else out

    results = {"w_ffn_gate": unflat("gate", w_ffn_gate, True), "w_ffn_up": unflat("up", w_ffn_up, True),
               "w_ffn_down": unflat("down", w_ffn_down, False), "w_in": unflat("in", w_in, True),
               "w_out": unflat("out", w_out, False)}
    br = [t.reshape(DEPTH, -1, LANES) for t in stepped["br"]]
    for k, n in enumerate(("w_br_sb", "w_br_dil", "w_br_swa")):
        results[n] = [t[:, br_off[k]:br_off[k + 1]] for t in br]
    for n, (g, dl, nm, nv) in results.items():
        grad[n], update[n] = g, (dl, nm, nv)

    names = ["w_ada", "b_ada", "norm_gain", "w_ffn_gate", "w_ffn_up", "w_ffn_down", "w_in", "w_br_sb", "w_br_dil",
             "w_br_swa", "w_out", "sinks", "rel_bias", "final_gain"]
    return (loss_total, dx[None], *[grad[n] for n in names], *[update[n][0] for n in names],
            *[update[n][1] for n in names], *[update[n][2] for n in names])
```

```python
import math

import numpy as np
import jax
import jax.numpy as jnp
from jax import lax
from jax.experimental import pallas as pl
from jax.experimental.pallas import tpu as pltpu

F32, BF16 = jnp.float32, jnp.bfloat16

SEQ, D_MODEL, D_FF, HEAD_DIM = 2048, 1024, 2816, 64
DEPTH = 2
BLK = 128
H_SB, H_DIL, H_SWA_Q, H_SWA_KV = 4, 6, 6, 2
DIL_PATTERNS = ((128, 1), (512, 4), (2048, 16))
SWA_WINDOW = 128
N_BUCKETS, MAX_REL_DIST = 32, 2048
RMS_EPS = 1e-6
D_QKV = 2560
D_GATES = 3 * D_MODEL
ADAM_LR, ADAM_B1, ADAM_B2, ADAM_EPS, ADAM_WD, ADAM_STEP = 0.001, 0.9, 0.999, 1e-08, 0.01, 10

N_DEV = 8
LANES = 128
NEG = -1e30
SB_TILE = 512
VMEM_LIMIT_BYTES = 48 * 1024 * 1024
HBM = pl.BlockSpec(memory_space=pltpu.HBM)
MESH = pl.DeviceIdType.MESH


def _tile(n, target):
    t = (min(n, target) // LANES) * LANES
    while t >= LANES:
        if n % t == 0:
            return t
        t -= LANES
    return n


def _row_tile(r, cap):
    t = (min(r, cap) // 16) * 16
    while t > 16 and r % t:
        t -= 16
    return t


def _params(semantics=None):
    return pltpu.CompilerParams(dimension_semantics=semantics, vmem_limit_bytes=VMEM_LIMIT_BYTES)


def _dot(a, b, ca, cb):
    return lax.dot_general(a, b, (((ca,), (cb,)), ((), ())), preferred_element_type=F32)


def _sigmoid(a):
    return 1.0 / (1.0 + jnp.exp(-a))


def _row(v):
    return v.reshape(1, -1)


def _all_gather(arrs, name, after=None):
    n = len(arrs)
    ins = list(arrs) + ([] if after is None else [after])

    def body(*refs):
        x_refs, out_refs = refs[:n], refs[len(ins):len(ins) + n]
        send_sems, recv_sems, local_sems = refs[len(ins) + n:]
        x, y, c = lax.axis_index("x"), lax.axis_index("y"), lax.axis_index("c")
        me, sibling = (x, y, c), (x, y, 1 - c)
        chips = [(1 - x, y), (x, 1 - y), (1 - x, 1 - y)]

        def slot(t, px, py, pc):
            return out_refs[t].at[4 * px + 2 * py + pc]

        def copy(t, k, block, to, src=None):
            return pltpu.make_async_remote_copy(
                src_ref=slot(t, *block) if src is None else src, dst_ref=slot(t, *block),
                send_sem=send_sems.at[7 * t + k], recv_sem=recv_sems.at[7 * t + k], device_id=to, device_id_type=MESH)

        mine = [pltpu.make_async_copy(x_refs[t], slot(t, *me), local_sems.at[t]) for t in range(n)]
        for cp in mine:
            cp.start()
        first = []
        for t in range(n):
            first.append(copy(t, 0, me, sibling, src=x_refs[t]))
            first += [copy(t, 1 + j, me, (*chip, c), src=x_refs[t]) for j, chip in enumerate(chips)]
        for cp in first:
            cp.start()
        passed = []
        for j, chip in enumerate(chips):
            for t in range(n):
                copy(t, 1 + j, (*chip, c), me).wait_recv()
                passed.append(copy(t, 4 + j, (*chip, c), sibling))
                passed[-1].start()
        for t in range(n):
            copy(t, 0, sibling, me).wait_recv()
        for j, chip in enumerate(chips):
            for t in range(n):
                copy(t, 4 + j, (*chip, 1 - c), me).wait_recv()
        for cp in first + passed:
            cp.wait_send()
        for cp in mine:
            cp.wait()

    return pl.pallas_call(
        body, name=name, out_shape=[jax.ShapeDtypeStruct((N_DEV,) + a.shape, a.dtype) for a in arrs],
        in_specs=[HBM] * n + [pl.BlockSpec(memory_space=pl.ANY)] * (len(ins) - n), out_specs=[HBM] * n,
        scratch_shapes=[pltpu.SemaphoreType.DMA((7 * n,)), pltpu.SemaphoreType.DMA((7 * n,)), pltpu.SemaphoreType.DMA((n,))],
    )(*ins)


def _direct_copies(x_refs, land_refs, send_sems, recv_sems, local_sems):
    x, y, c = lax.axis_index("x"), lax.axis_index("y"), lax.axis_index("c")
    me = 4 * x + 2 * y + c
    sends, recvs = [], []
    for k in range(1, N_DEV):
        px = 1 - x if (k >> 2) & 1 else x
        py = 1 - y if (k >> 1) & 1 else y
        pc = 1 - c if k & 1 else c
        peer = 4 * px + 2 * py + pc
        for t, (x_ref, land_ref) in enumerate(zip(x_refs, land_refs)):
            sem = 7 * t + k - 1
            for out, src, slot in ((sends, peer, me), (recvs, me, peer)):
                out.append(pltpu.make_async_remote_copy(
                    src_ref=x_ref.at[src], dst_ref=land_ref.at[slot], send_sem=send_sems.at[sem],
                    recv_sem=recv_sems.at[sem], device_id=(px, py, pc), device_id_type=MESH))
    own = [pltpu.make_async_copy(x_ref.at[me], land_ref.at[me], local_sems.at[t])
           for t, (x_ref, land_ref) in enumerate(zip(x_refs, land_refs))]
    return sends, recvs, own


SEM =pl.BlockSpec(memory_space=pltpu.SEMAPHORE)
ANY = pl.BlockSpec(memory_space=pl.ANY)
SIDE_EFFECT = pltpu.SideEffectType.DATAFLOW_SIDE_EFFECTING


def _exchange_start(arrs, after, *, name):
    n = len(arrs)
    lands = [lax.empty(a.shape, a.dtype) for a in arrs]
    extra = [] if after is None else [after]

    def body(*refs):
        sems = refs[2 * n + len(extra):2 * n + len(extra) + 3]
        sends, _, own = _direct_copies(refs[:n], refs[n:2 * n], *sems)
        for cp in own + sends:
            cp.start()
        refs[-1][...] = jnp.zeros_like(refs[-1])

    ops = [pltpu.with_memory_space_constraint(a, pltpu.HBM) for a in list(arrs) + lands]
    out = pl.pallas_call(
        body, name=name,
        out_shape=(pltpu.SemaphoreType.DMA((7 * n,)), pltpu.SemaphoreType.DMA((7 * n,)), pltpu.SemaphoreType.DMA((n,)),
                   *[pltpu.HBM(a.shape, a.dtype) for a in ops], jax.ShapeDtypeStruct((8, LANES), F32)),
        in_specs=[HBM] * (2 * n) + [ANY] * len(extra),
        out_specs=(SEM, SEM, SEM, *[HBM] * (2 * n), pl.BlockSpec(memory_space=pltpu.VMEM)),
        input_output_aliases={t: 3 + t for t in range(2 * n)},
        compiler_params=pltpu.CompilerParams(has_side_effects=SIDE_EFFECT),
    )(*ops, *extra)
    return (out[:3], out[3:3 + n], out[3 + n:3 + 2 * n]), out[-1]


def _exchange_wait(state, after, *, name):
    sems, arrs, lands = state
    n = len(arrs)

    def body(*refs):
        sends, recvs, own = _direct_copies(refs[:n], refs[n:2 * n], *refs[2 * n:2 * n + 3])
        for cp in own:
            cp.wait()
        for cp in sends:
            cp.wait_send()
        for cp in recvs:
            cp.wait_recv()

    out = pl.pallas_call(
        body, name=name, out_shape=tuple(pltpu.HBM(a.shape, a.dtype) for a in list(arrs) + list(lands)),
        in_specs=[HBM] * (2 * n) + [SEM, SEM, SEM, ANY], out_specs=tuple([HBM] * (2 * n)),
        input_output_aliases={t: t for t in range(2 * n)},
        compiler_params=pltpu.CompilerParams(has_side_effects=SIDE_EFFECT),
    )(*arrs, *lands, *sems, after)
    return out[n:]


def _relay_copies(x_refs, land_refs, sems_a, sems_b):
    x, y, c = lax.axis_index("x"), lax.axis_index("y"), lax.axis_index("c")
    me = 4 * x + 2 * y + c
    sibling = (x, y, 1 - c)
    chips = [(1 - x, y), (x, 1 - y), (1 - x, 1 - y)]

    def slot(px, py, pc):
        return 4 * px + 2 * py + pc

    def copy(src, land_ref, dst_slot, send_sems, recv_sems, k, to):
        return pltpu.make_async_remote_copy(src_ref=src, dst_ref=land_ref.at[dst_slot], send_sem=send_sems.at[k],
                                            recv_sem=recv_sems.at[k], device_id=to, device_id_type=MESH)

    a_send, a_recv, a_own, b_send, b_recv = [], [], [], [], []
    for t, (x_ref, land_ref) in enumerate(zip(x_refs, land_refs)):
        peers = [sibling] + [(*chip, c) for chip in chips]
        if sems_a is not None:
            for k, peer in enumerate(peers):
                a_send.append(copy(x_ref, land_ref, me, sems_a[0], sems_a[1], 4 * t + k, peer))
                a_recv.append(copy(x_ref, land_ref, slot(*peer), sems_a[0], sems_a[1], 4 * t + k, peer))
            a_own.append(pltpu.make_async_copy(x_ref, land_ref.at[me], sems_a[2].at[t]))
        if sems_b is not None:
            for j, chip in enumerate(chips):
                b_send.append(copy(land_ref.at[slot(*chip, c)], land_ref, slot(*chip, c), sems_b[0], sems_b[1], 3 * t + j, sibling))
                b_recv.append(copy(land_ref.at[slot(*chip, c)], land_ref, slot(*chip, 1 - c), sems_b[0], sems_b[1], 3 * t + j,
                                   sibling))
    return (a_send, a_recv, a_own), (b_send, b_recv)


def _relay_start(arrs, after, name):
    n = len(arrs)
    lands = [lax.empty((N_DEV,) + a.shape, a.dtype) for a in arrs]

    def body(*refs):
        (sends, _, own), _ = _relay_copies(refs[:n], refs[n:2 * n], refs[2 * n + 1:2 * n + 4], None)
        for cp in own + sends:
            cp.start()
        refs[-1][...] = jnp.zeros_like(refs[-1])

    ops = [pltpu.with_memory_space_constraint(a, pltpu.HBM) for a in list(arrs) + lands]
    out = pl.pallas_call(
        body, name=name,
        out_shape=(pltpu.SemaphoreType.DMA((4 * n,)), pltpu.SemaphoreType.DMA((4 * n,)), pltpu.SemaphoreType.DMA((n,)),
                   *[pltpu.HBM(a.shape, a.dtype) for a in ops], jax.ShapeDtypeStruct((8, LANES), F32)),
        in_specs=[HBM] * (2 * n) + [ANY],
        out_specs=(SEM, SEM, SEM, *[HBM] * (2 * n), pl.BlockSpec(memory_space=pltpu.VMEM)),
        input_output_aliases={t: 3 + t for t in range(2 * n)},
        compiler_params=pltpu.CompilerParams(has_side_effects=SIDE_EFFECT),
    )(*ops, after)
    return (out[:3], out[3:3 + n], out[3 + n:3 + 2 * n]), out[-1]


def _relay_pass(state, after, name):
    sems_a, arrs, lands = state
    n = len(arrs)

    def body(*refs):
        sems_b = refs[2 * n + 4:2 * n + 6]
        (a_send, a_recv, a_own), (b_send, _) = _relay_copies(refs[:n], refs[n:2 * n], refs[2 * n:2 * n + 3], sems_b)
        for cp in a_own:
            cp.wait()
        for cp in a_send:
            cp.wait_send()
        for cp in a_recv:
            cp.wait_recv()
        for cp in b_send:
            cp.start()
        refs[-1][...] = jnp.zeros_like(refs[-1])

    out = pl.pallas_call(
        body, name=name,
        out_shape=(pltpu.SemaphoreType.DMA((3 * n,)), pltpu.SemaphoreType.DMA((3 * n,)),
                   *[pltpu.HBM(a.shape, a.dtype) for a in list(arrs) + list(lands)], jax.ShapeDtypeStruct((8, LANES), F32)),
        in_specs=[HBM] * (2 * n) + [SEM, SEM, SEM, ANY],
        out_specs=(SEM, SEM, *[HBM] * (2 * n), pl.BlockSpec(memory_space=pltpu.VMEM)),
        input_output_aliases={t: 2 + t for t in range(2 * n)},
        compiler_params=pltpu.CompilerParams(has_side_effects=SIDE_EFFECT),
    )(*arrs, *lands, *sems_a, after)
    return (out[:2], out[2:2 + n], out[2 + n:2 + 2 * n]), out[-1]


def _relay_wait(state, after, name):
    sems_b, arrs, lands = state
    n = len(arrs)

    def body(*refs):
        _, (b_send, b_recv) = _relay_copies(refs[:n], refs[n:2 * n], None, refs[2 * n:2 * n + 2])
        for cp in b_send:
            cp.wait_send()
        for cp in b_recv:
            cp.wait_recv()

    out = pl.pallas_call(
        body, name=name, out_shape=tuple(pltpu.HBM(a.shape, a.dtype) for a in list(arrs) + list(lands)),
        in_specs=[HBM] * (2 * n) + [SEM, SEM, ANY], out_specs=tuple([HBM] * (2 * n)),
        input_output_aliases={t: t for t in range(2 * n)},
        compiler_params=pltpu.CompilerParams(has_side_effects=SIDE_EFFECT),
    )(*arrs, *lands, *sems_b, after)
    return out[n:]


def _sum_parts(groups, name):
    n, r, cdim = groups[0].shape
    tr = _row_tile(r, max(16, (1 << 21) // (n * cdim * groups[0].dtype.itemsize)))
    steps = r // tr

    def body(*refs):
        o_ref = refs[-1]
        gg = pl.program_id(0)
        for gi in range(len(groups)):
            @pl.when(gg == gi)
            def _(gi=gi):
                acc = refs[gi][0].astype(F32)
                for k in range(1, n):
                    acc = acc + refs[gi][k].astype(F32)
                o_ref[...] = acc

    def in_spec(gi):
        return pl.BlockSpec((n, tr, cdim), lambda gg, i: (0, jnp.where(gg == gi, i, 0), 0))

    return pl.pallas_call(
        body, name=name, out_shape=jax.ShapeDtypeStruct((len(groups) * r, cdim), F32), grid=(len(groups), steps),
        in_specs=[in_spec(gi) for gi in range(len(groups))],
        out_specs=pl.BlockSpec((tr, cdim), lambda gg, i: (gg * steps + i, 0)),
        compiler_params=_params(("parallel", "parallel")),
    )(*groups)


def _mm_tn(a, b, *, name, after=None, tm=512, tn=1024, out_rows=None, row0=0, prev=None):
    k, m = a.shape
    n = b.shape[1]
    tm, tn = _tile(m, tm), _tile(n, tn)
    out_rows = m if out_rows is None else out_rows

    def body(a_ref, b_ref, *rest):
        o_ref, at_ref = rest[-2], rest[-1]

        @pl.when(pl.program_id(1) == 0)
        def _():
            at_ref[...] = a_ref[...].astype(BF16).T

        o_ref[...] = _dot(at_ref[...], b_ref[...].astype(BF16), 1, 0).astype(BF16)

    ins = [a, b] + [t for t in (after, prev) if t is not None]
    return pl.pallas_call(
        body, name=name, out_shape=jax.ShapeDtypeStruct((out_rows, n), BF16), grid=(m // tm, n // tn),
        in_specs=[pl.BlockSpec((k, tm), lambda i, j: (0, i)), pl.BlockSpec((k, tn), lambda i, j: (0, j))] + [ANY] * (len(ins) - 2),
        out_specs=pl.BlockSpec((tm, tn), lambda i, j: (row0 // tm + i, j)),
        input_output_aliases={} if prev is None else {len(ins) - 1: 0},
        scratch_shapes=[pltpu.VMEM((tm, k), BF16)], compiler_params=_params(("parallel", "arbitrary")),
    )(*ins)


def _mm(a, b, *, name, ta=False, tb=False, res=None, colscale=None, emit_acc=False,
        out_dtype=F32, tm=512, tn=512, b_rows=None):
    m, k = (a.shape[1], a.shape[0]) if ta else a.shape
    n = b.shape[0] if tb else b.shape[1]
    b_start = 0
    if b_rows is not None:
        b_start, n = b_rows
    tm, tn = _tile(m, tm), _tile(n, tn)
    ca, cb = (0 if ta else 1), (1 if tb else 0)
    a_spec = pl.BlockSpec((k, tm), lambda i, j: (0, i)) if ta else pl.BlockSpec((tm, k), lambda i, j: (i, 0))
    b_spec = (pl.BlockSpec((tn, k), lambda i, j: (b_start // tn + j, 0)) if tb
              else pl.BlockSpec((k, tn), lambda i, j: (0, j)))
    tile = pl.BlockSpec((tm, tn), lambda i, j: (i, j))
    ins, in_specs = [a, b], [a_spec, b_spec]
    if res is not None:
        ins.append(res)
        in_specs.append(tile)
    if colscale is not None:
        ins.append(colscale)
        in_specs.append(pl.BlockSpec((1, tn), lambda i, j: (0, j)))
    n_in = len(ins)

    def body(*refs):
        outs = refs[n_in:]
        acc = _dot(refs[0][...].astype(BF16), refs[1][...].astype(BF16), ca, cb)
        val, p = acc, 2
        if res is not None:
            r_val, p = refs[p][...], p + 1
        if colscale is not None:
            val = val * refs[p][...]
        if res is not None:
            val = r_val + val
        if emit_acc:
            outs[0][...] = acc
        outs[-1][...] = val.astype(out_dtype)

    out_shape = [jax.ShapeDtypeStruct((m, n), out_dtype)]
    out_specs = [tile]
    if emit_acc:
        out_shape.insert(0, jax.ShapeDtypeStruct((m, n), F32))
        out_specs.insert(0, tile)
    out = pl.pallas_call(
        body, name=name, out_shape=out_shape, grid=(m // tm, n // tn), in_specs=in_specs, out_specs=out_specs,
        compiler_params=_params(("parallel", "parallel")),
    )(*ins)
    return out if emit_acc else out[0]


def _norm_fwd(x, g, scale, shift, name, after=None):
    s, d = x.shape
    tr = 256

    def body(x_ref, g_ref, sc_ref, sh_ref, *rest):
        xv = x_ref[...]
        rstd = lax.rsqrt(jnp.mean(xv * xv, axis=-1, keepdims=True) + RMS_EPS)
        rest[-1][...] = (xv * rstd * g_ref[...] * (1.0 + sc_ref[...]) + sh_ref[...]).astype(BF16)

    rowspec = pl.BlockSpec((1, d), lambda i: (0, 0))
    ins = [x, g, scale, shift] + ([] if after is None else [after])
    return pl.pallas_call(
        body, name=name, out_shape=jax.ShapeDtypeStruct((s, d), BF16), grid=(s // tr,),
        in_specs=[pl.BlockSpec((tr, d), lambda i: (i, 0)), rowspec, rowspec, rowspec] + [ANY] * (len(ins) - 4),
        out_specs=pl.BlockSpec((tr, d), lambda i: (i, 0)),
        compiler_params=_params(("parallel",)),
    )(*ins)


def _dh_norm_bwd(a1, b1, a2, b2, x, dres, g, scale, nxt, *, name, after=None, b_rows=None):
    s, d = x.shape
    tm = 256
    n_fixed = 8

    def body(a1_ref, b1_ref, a2_ref, b2_ref, x_ref, dr_ref, g_ref, sc_ref, *rest):
        rest = rest[(1 if after is not None else 0):]
        if nxt is not None:
            f_ref, cs_ref, dx_ref, sa_ref, sb_ref, df_ref, dg_ref = rest
        else:
            dx_ref, sa_ref, sb_ref = rest

        @pl.when(pl.program_id(0) == 0)
        def _():
            sa_ref[...] = jnp.zeros_like(sa_ref)
            sb_ref[...] = jnp.zeros_like(sb_ref)
            if nxt is not None:
                dg_ref[...] = jnp.zeros_like(dg_ref)

        dhv = (_dot(a1_ref[...].astype(BF16), b1_ref[...], 1, 0) + _dot(a2_ref[...].astype(BF16), b2_ref[...], 1, 0))
        xv = x_ref[...]
        rstd = lax.rsqrt(jnp.mean(xv * xv, axis=-1, keepdims=True) + RMS_EPS)
        xhat = xv * rstd
        dxhat = dhv * (g_ref[...] * (1.0 + sc_ref[...]))
        mean_term = jnp.mean(dxhat * xhat, axis=-1, keepdims=True)
        dxv = dr_ref[...] + rstd * (dxhat - xhat * mean_term)
        dx_ref[...] = dxv
        sa_ref[...] += jnp.sum(dhv, axis=0, keepdims=True)
        sb_ref[...] += jnp.sum(dhv * xhat, axis=0, keepdims=True)
        if nxt is not None:
            df_ref[...] = (dxv * cs_ref[...]).astype(BF16)
            dg_ref[...] += nxt[2] * jnp.sum(dxv * f_ref[...], axis=0, keepdims=True)

    def a_spec(t):
        return pl.BlockSpec((tm, t.shape[1]), lambda i: (i, 0))

    def b_spec(t, a, which):
        if b_rows is None:
            return pl.BlockSpec((t.shape[0], d), lambda i: (0, 0))
        start = b_rows[which]
        return pl.BlockSpec((pl.Element(a.shape[1]), pl.Element(d)), lambda i: (start, 0))

    rowspec = pl.BlockSpec((1, d), lambda i: (0, 0))
    tile = pl.BlockSpec((tm, d), lambda i: (i, 0))
    ins = [a1, b1, a2, b2, x, dres, g, scale] + ([] if after is None else [after])
    in_specs = [a_spec(a1), b_spec(b1, a1, 0), a_spec(a2), b_spec(b2, a2, 1), tile, tile, rowspec, rowspec]
    in_specs += [ANY] * (len(ins) - n_fixed)
    out_shape = [jax.ShapeDtypeStruct((s, d), F32), jax.ShapeDtypeStruct((1, d), F32), jax.ShapeDtypeStruct((1, d), F32)]
    out_specs = [tile, rowspec, rowspec]
    if nxt is not None:
        ins += [nxt[0], nxt[1]]
        in_specs += [tile, rowspec]
        out_shape += [jax.ShapeDtypeStruct((s, d), BF16), jax.ShapeDtypeStruct((1, d), F32)]
        out_specs += [tile, rowspec]
    out = pl.pallas_call(
        body, name=name, out_shape=out_shape, grid=(s // tm,), in_specs=in_specs, out_specs=out_specs,
        compiler_params=_params(("arbitrary",)),
    )(*ins)
    return out[0], out[1], out[2], (None if nxt is None else (out[3], out[4]))


def _gate_bwd(dxn, f, colscale, coef, name):
    s, d = dxn.shape
    tr = 256

    def body(dx_ref, f_ref, cs_ref, df_ref, dg_ref):
        @pl.when(pl.program_id(0) == 0)
        def _():
            dg_ref[...] = jnp.zeros_like(dg_ref)

        dxv = dx_ref[...]
        df_ref[...] = (dxv * cs_ref[...]).astype(BF16)
        dg_ref[...] += coef * jnp.sum(dxv * f_ref[...], axis=0, keepdims=True)

    rowspec = pl.BlockSpec((1, d), lambda i: (0, 0))
    tile = pl.BlockSpec((tr, d), lambda i: (i, 0))
    return pl.pallas_call(
        body, name=name, out_shape=[jax.ShapeDtypeStruct((s, d), BF16), jax.ShapeDtypeStruct((1, d), F32)],
        grid=(s // tr,), in_specs=[tile, tile, rowspec], out_specs=[tile, rowspec],
        compiler_params=_params(("arbitrary",)),
    )(dxn, f, colscale)


def _ffn_up(h, wg, wu, name, tm=SEQ, tn=256):
    s, d = h.shape
    f = wg.shape[0]

    def body(h_ref, wg_ref, wu_ref, a_ref, u_ref, s_ref):
        hv = h_ref[...]
        a = _dot(hv, wg_ref[...], 1, 1)
        u = _dot(hv, wu_ref[...], 1, 1)
        a_ref[...] = a.astype(BF16)
        u_ref[...] = u.astype(BF16)
        s_ref[...] = (a * _sigmoid(a) * u).astype(BF16)

    tile = pl.BlockSpec((tm, tn), lambda i, j: (i, j))
    wspec = pl.BlockSpec((tn, d), lambda i, j: (j, 0))
    return pl.pallas_call(
        body, name=name,
        out_shape=[jax.ShapeDtypeStruct((s, f), BF16), jax.ShapeDtypeStruct((s, f), BF16), jax.ShapeDtypeStruct((s, f), BF16)],
        grid=(s // tm, f // tn), in_specs=[pl.BlockSpec((tm, d), lambda i, j: (i, 0)), wspec, wspec],
        out_specs=[tile, tile, tile], compiler_params=_params(("parallel", "parallel")),
    )(h, wg, wu)


def _ffn_bwd_ds(df, wd, a, u, name, tm=SEQ, tn=256):
    s, d = df.shape
    f = wd.shape[0]

    def body(df_ref, wd_ref, a_ref, u_ref, da_ref, du_ref):
        ds = _dot(df_ref[...], wd_ref[...], 1, 1)
        av = a_ref[...].astype(F32)
        sg = _sigmoid(av)
        da_ref[...] = (ds * u_ref[...].astype(F32) * (sg * (1.0 + av * (1.0 - sg)))).astype(BF16)
        du_ref[...] = (ds * (av * sg)).astype(BF16)

    tile = pl.BlockSpec((tm, tn), lambda i, j: (i, j))
    return pl.pallas_call(
        body, name=name, out_shape=[jax.ShapeDtypeStruct((s, f), BF16), jax.ShapeDtypeStruct((s, f), BF16)],
        grid=(s // tm, f // tn),
        in_specs=[pl.BlockSpec((tm, d), lambda i, j: (i, 0)), pl.BlockSpec((tn, d), lambda i, j: (j, 0)), tile, tile],
        out_specs=[tile, tile], compiler_params=_params(("parallel", "parallel")),
    )(df, wd, a, u)


def _merge_fwd(o_sb, o_dil, o_swa, gates, wb_sb, wb_dil, wb_swa, name):
    s, d = SEQ, D_MODEL
    tm = 256

    def body(osb_ref, odl_ref, osw_ref, g_ref, wsb_ref, wdl_ref, wsw_ref, m_ref, tsb_ref, tdl_ref, tsw_ref):
        for h in range(osb_ref.shape[0]):
            tsb_ref[:, h * HEAD_DIM:(h + 1) * HEAD_DIM] = osb_ref[h].astype(BF16)
        for h in range(osw_ref.shape[0]):
            tsw_ref[:, h * HEAD_DIM:(h + 1) * HEAD_DIM] = osw_ref[h].astype(BF16)
        tdl_ref[...] = odl_ref[...].astype(BF16)
        acc = _sigmoid(g_ref[:, 0:d]) * _dot(tsb_ref[...], wsb_ref[...], 1, 0)
        acc += _sigmoid(g_ref[:, d:2 * d]) * _dot(tdl_ref[...], wdl_ref[...], 1, 0)
        acc += _sigmoid(g_ref[:, 2 * d:3 * d]) * _dot(tsw_ref[...], wsw_ref[...], 1, 0)
        m_ref[...] = acc.astype(BF16)

    def rows(w):
        return pl.BlockSpec((tm, w), lambda i: (i, 0))

    def heads(n):
        return pl.BlockSpec((n, tm, HEAD_DIM), lambda i: (0, i, 0))

    def whole(w):
        return pl.BlockSpec((w, d), lambda i: (0, 0))

    return pl.pallas_call(
        body, name=name, out_shape=[jax.ShapeDtypeStruct((s, w), BF16) for w in (d, 256, 128, 384)], grid=(s // tm,),
        in_specs=[heads(H_SB), rows(128), heads(H_SWA_Q), rows(3 * d), whole(256), whole(128), whole(384)],
        out_specs=[rows(d), rows(256), rows(128), rows(384)], compiler_params=_params(("parallel",)),
    )(o_sb, o_dil, o_swa, gates, wb_sb, wb_dil, wb_swa)


def _merge_bwd(dmerged, t_sb, t_dil, t_swa, gates, wb_sb, wb_dil, wb_swa, name):
    s, d = SEQ, D_MODEL
    tm = 256

    def body(dm_ref, tsb_ref, tdl_ref, tsw_ref, g_ref, wsb_ref, wdl_ref, wsw_ref,
             dg_ref, dosb_ref, dodl_ref, dosw_ref, dbsb_ref, dbdl_ref, dbsw_ref):
        dm = dm_ref[...]
        for idx, (t_ref, w_ref, do_ref, db_ref) in enumerate((
                (tsb_ref, wsb_ref, dosb_ref, dbsb_ref), (tdl_ref, wdl_ref, dodl_ref, dbdl_ref),
                (tsw_ref, wsw_ref, dosw_ref, dbsw_ref))):
            w = w_ref[...]
            br = _dot(t_ref[...], w, 1, 0)
            sg = _sigmoid(g_ref[:, idx * d:(idx + 1) * d])
            dbr = (dm * sg).astype(BF16)
            dg_ref[:, idx * d:(idx + 1) * d] = (dm * br * (sg * (1.0 - sg))).astype(BF16)
            db_ref[...] = dbr
            do = _dot(dbr, w, 1, 1)
            if len(do_ref.shape) == 2:
                do_ref[...] = do
            else:
                for h in range(do_ref.shape[0]):
                    do_ref[h] = do[:, h * HEAD_DIM:(h + 1) * HEAD_DIM]

    def rows(w):
        return pl.BlockSpec((tm, w), lambda i: (i, 0))

    def heads(n):
        return pl.BlockSpec((n, tm, HEAD_DIM), lambda i: (0, i, 0))

    def whole(w):
        return pl.BlockSpec((w, d), lambda i: (0, 0))

    def shp(w, dt):
        return jax.ShapeDtypeStruct((s, w), dt)

    def hshp(n):
        return jax.ShapeDtypeStruct((n, s, HEAD_DIM), F32)

    return pl.pallas_call(
        body, name=name,
        out_shape=[shp(3 * d, BF16), hshp(H_SB), shp(128, F32), hshp(H_SWA_Q), shp(d, BF16), shp(d, BF16), shp(d, BF16)],
        grid=(s // tm,),
        in_specs=[rows(d), rows(256), rows(128), rows(384), rows(3 * d), whole(256), whole(128), whole(384)],
        out_specs=[rows(3 * d), heads(H_SB), rows(128), heads(H_SWA_Q), rows(d), rows(d), rows(d)],
        compiler_params=_params(("parallel",)),
    )(dmerged, t_sb, t_dil, t_swa, gates, wb_sb, wb_dil, wb_swa)


def _final_loss(x, target, g, name):
    s, d = x.shape
    tr = 256

    def body(x_ref, t_ref, g_ref, loss_ref, dx_ref, dg_ref):
        @pl.when(pl.program_id(0) == 0)
        def _():
            loss_ref[...] = jnp.zeros_like(loss_ref)
            dg_ref[...] = jnp.zeros_like(dg_ref)

        xv = x_ref[...]
        gv = g_ref[...]
        rstd = lax.rsqrt(jnp.mean(xv * xv, axis=-1, keepdims=True) + RMS_EPS)
        xhat = xv * rstd
        err = xhat * gv - t_ref[...]
        loss_ref[...] += 0.5 * jnp.sum(jnp.mean(err * err, axis=-1, keepdims=True))
        dy = err * (1.0 / d)
        dxhat = dy * gv
        mean_term = jnp.mean(dxhat * xhat, axis=-1, keepdims=True)
        dx_ref[...] = rstd * (dxhat - xhat * mean_term)
        dg_ref[...] += jnp.sum(dy * xhat, axis=0, keepdims=True)

    rowspec = pl.BlockSpec((1, d), lambda i: (0, 0))
    tile = pl.BlockSpec((tr, d), lambda i: (i, 0))
    return pl.pallas_call(
        body, name=name,
        out_shape=[jax.ShapeDtypeStruct((1, LANES), F32), jax.ShapeDtypeStruct((s, d), F32), jax.ShapeDtypeStruct((1, d), F32)],
        grid=(s // tr,), in_specs=[tile, tile, rowspec],
        out_specs=[pl.BlockSpec((1, LANES), lambda i: (0, 0)), tile, rowspec],
        compiler_params=_params(("arbitrary",)),
    )(x, target, g)


def _adamw(w, g, m, v, name):
    shape = w.shape
    cols = shape[-1]
    rows = int(np.prod(shape[:-1])) if len(shape) > 1 else 1
    tr = rows
    for cand in (1024, 512, 256, 128, 64, 32, 16, 8):
        if rows % cand == 0 and rows > cand and cand * cols * 4 <= (1 << 21):
            tr = cand
            break

    def body(w_ref, g_ref, m_ref, v_ref, d_ref, nm_ref, nv_ref):
        d_ref[...], nm_ref[...], nv_ref[...] = _adam_update(w_ref[...], g_ref[...], m_ref[...], v_ref[...])

    tile = pl.BlockSpec((tr, cols), lambda i: (i, 0))
    flat = [t.reshape(rows, cols) for t in (w, g, m, v)]
    out = pl.pallas_call(
        body, name=name, out_shape=[jax.ShapeDtypeStruct((rows, cols), F32)] * 3, grid=(rows // tr,),
        in_specs=[tile] * 4, out_specs=[tile] * 3, compiler_params=_params(("parallel",)),
    )(*flat)
    return tuple(t.reshape(shape) for t in out)


def _adam_update(w, gv, m, v):
    nm = ADAM_B1 * m + (1.0 - ADAM_B1) * gv
    nv = ADAM_B2 * v + (1.0 - ADAM_B2) * (gv * gv)
    m_hat = nm / (1.0 - ADAM_B1 ** ADAM_STEP)
    v_hat = nv / (1.0 - ADAM_B2 ** ADAM_STEP)
    return -ADAM_LR * (m_hat / (jnp.sqrt(v_hat) + ADAM_EPS) + ADAM_WD * w), nm, nv


def _reduce_adamw(groups, w, m, v, row0, prev, name, after=None):
    n, r, cdim = groups[0].shape
    rows = w.shape[0]
    tr = _row_tile(r, max(16, (1 << 22) // (n * cdim * groups[0].dtype.itemsize)))
    steps = r // tr
    ng = len(groups)

    def body(*refs):
        w_ref, m_ref, v_ref = refs[ng:ng + 3]
        g_out, d_out, m_out, v_out = refs[-4:]
        gg = pl.program_id(0)
        for gi in range(ng):
            @pl.when(gg == gi)
            def _(gi=gi):
                acc = refs[gi][0].astype(F32)
                for k in range(1, n):
                    acc = acc + refs[gi][k].astype(F32)
                g_out[...] = acc
                d_out[...], m_out[...], v_out[...] = _adam_update(w_ref[...], acc, m_ref[...], v_ref[...])

    def part_spec(gi):
        return pl.BlockSpec((n, tr, cdim), lambda gg, i: (0, jnp.where(gg == gi, i, 0), 0))

    tile = pl.BlockSpec((tr, cdim), lambda gg, i: (row0 // tr + gg * steps + i, 0))
    extra = ([] if prev is None else list(prev)) + ([] if after is None else [after])
    return pl.pallas_call(
        body, name=name, out_shape=[jax.ShapeDtypeStruct((rows, cdim), F32)] * 4, grid=(ng, steps),
        in_specs=[part_spec(gi) for gi in range(ng)] + [tile] * 3 + [ANY] * len(extra), out_specs=[tile] * 4,
        input_output_aliases={} if prev is None else {ng + 3 + k: k for k in range(4)},
        compiler_params=_params(("parallel", "parallel")),
    )(*groups, w, m, v, *extra)


def _ada_fwd(c_all, w, name):
    n = w.shape[1]

    def body(c_ref, w_ref, o_ref):
        cv = c_ref[...]
        o_ref[...] = jnp.dot(cv * _sigmoid(cv), w_ref[...], preferred_element_type=F32, precision=lax.Precision.HIGHEST)

    return pl.pallas_call(body, name=name, out_shape=jax.ShapeDtypeStruct((N_DEV, n), F32), compiler_params=_params())(c_all, w)


def _ada_bwd(c_all_t, dmod, name):
    n = dmod.shape[1]

    def body(c_ref, d_ref, o_ref):
        cv = c_ref[...]
        o_ref[...] = jnp.dot(cv * _sigmoid(cv), d_ref[...], preferred_element_type=F32, precision=lax.Precision.HIGHEST)

    return pl.pallas_call(body, name=name, out_shape=jax.ShapeDtypeStruct((D_MODEL, n), F32), compiler_params=_params())(c_all_t, dmod)


def _bucket_tables():
    rel = np.arange(BLK)[:, None] + BLK - np.arange(2 * BLK)[None, :]
    max_exact = N_BUCKETS // 2

    def bucket(n):
        nf = np.maximum(n, 1).astype(np.float32)
        large = max_exact + (np.log(nf / np.float32(max_exact)) / np.float32(math.log(MAX_REL_DIST / max_exact))
                             * np.float32(N_BUCKETS - max_exact)).astype(np.int32)
        return np.where(n < max_exact, n, np.minimum(large, N_BUCKETS - 1))

    tabs = []
    for dil, max_dist in ((1, 128), (4, 128), (16, 128), (1, SWA_WINDOW - 1)):
        in_band = (rel >= 0) & (rel <= max_dist)
        tabs.append(np.where(in_band, bucket(np.maximum(rel, 0) * dil), -1))
    return np.stack(tabs).astype(np.int32)


N_SOFT = H_DIL + H_SWA_Q


def _table_of_head(h):
    return jnp.minimum(h // 2, 3)


def _bias_build(rel_bias, tables, name):
    def body(rel_ref, t_ref, o_ref):
        h = pl.program_id(0)
        tb = t_ref[0]
        out = jnp.full((BLK, 2 * BLK), NEG, F32)
        for b in range(N_BUCKETS):
            out = jnp.where(tb == b, rel_ref[b, h], out)
        o_ref[0] = out

    return pl.pallas_call(
        body, name=name, out_shape=jax.ShapeDtypeStruct((N_SOFT, BLK, 2 * BLK), F32), grid=(N_SOFT,),
        in_specs=[pl.BlockSpec(memory_space=pltpu.SMEM),
                  pl.BlockSpec((1, BLK, 2 * BLK), lambda h: (_table_of_head(h), 0, 0))],
        out_specs=pl.BlockSpec((1, BLK, 2 * BLK), lambda h: (h, 0, 0)),
        compiler_params=_params(("parallel",)),
    )(rel_bias, tables)


def _bias_grad(dbias, tables, name):
    def body(d_ref, t_ref, o_ref):
        tb = t_ref[0]
        dv = d_ref[0]
        lane = lax.broadcasted_iota(jnp.int32, (1, LANES), 1)
        out = jnp.zeros((1, LANES), F32)
        for b in range(N_BUCKETS):
            out = jnp.where(lane == b, jnp.sum(jnp.where(tb == b, dv, 0.0)), out)
        o_ref[0] = out

    return pl.pallas_call(
        body, name=name, out_shape=jax.ShapeDtypeStruct((N_SOFT, 1, LANES), F32), grid=(N_SOFT,),
        in_specs=[pl.BlockSpec((1, BLK, 2 * BLK), lambda h: (h, 0, 0)),
                  pl.BlockSpec((1, BLK, 2 * BLK), lambda h: (_table_of_head(h), 0, 0))],
        out_specs=pl.BlockSpec((1, 1, LANES), lambda h: (h, 0, 0)),
        compiler_params=_params(("parallel",)),
    )(dbias, tables)


def _band_layout(g, bias_div):
    assert g == 1 or bias_div == 1
    return bias_div if g == 1 else 1


def _band_specs(length, g, bias_div, offs):
    ns = _band_layout(g, bias_div)

    def seqs(off, div=1):
        return pl.BlockSpec((ns, length, HEAD_DIM), lambda s: (off // ns + s // div, 0, 0))

    xspecs = [seqs(offs[0]), seqs(offs[1], g), seqs(offs[2], g)]
    bspec = pl.BlockSpec((1, BLK, 2 * BLK), lambda s: (s, 0, 0))
    sspec = pl.BlockSpec((ns, 1, LANES), lambda s: (s, 0, 0))
    colspec = pl.BlockSpec((ns, length, 1), lambda s: (s, 0, 0))
    return xspecs, seqs(0), seqs(0, g), bspec, sspec, colspec


def _band_sweep(length, ns, one):
    nblk = length // BLK
    for qq in range(ns):
        if ns * nblk <= 16:
            for i in range(nblk):
                one(qq, i * BLK, max(i - 1, 0) * BLK, i == 0)
        else:
            def step(i, carry, qq=qq):
                one(qq, pl.multiple_of(i * BLK, BLK), pl.multiple_of(jnp.maximum(i - 1, 0) * BLK, BLK), i == 0)
                return carry

            lax.fori_loop(0, nblk, step, 0, unroll=2)


def _band_scores(q_ref, k_ref, b_ref, qq, kq, bq, cur, prv, first):
    qv = q_ref[qq, pl.ds(cur, BLK), :]
    bv = b_ref[bq]
    if first is True:
        sp = jnp.full((BLK, BLK), NEG, F32)
    else:
        sp = _dot(qv, k_ref[kq, pl.ds(prv, BLK), :], 1, 1) + bv[:, :BLK]
        sp = sp if first is False else jnp.where(first, NEG, sp)
    sc = _dot(qv, k_ref[kq, pl.ds(cur, BLK), :], 1, 1) + bv[:, BLK:]
    return qv, sp, sc


def _band_fwd(x, bias, sink, *, nq, offs, g, bias_div, has_sink, name):
    length = x.shape[1]
    ns = _band_layout(g, bias_div)

    def body(q_ref, k_ref, v_ref, b_ref, s_ref, o_ref, lse_ref):
        def one(qq, cur, prv, first):
            kq, bq = qq, 0
            _, sp, sc = _band_scores(q_ref, k_ref, b_ref, qq, kq, bq, cur, prv, first)
            m = jnp.maximum(jnp.max(sp, axis=1, keepdims=True), jnp.max(sc, axis=1, keepdims=True))
            if has_sink:
                sk = s_ref[qq][:, :1]
                m = jnp.maximum(m, sk)
            pp, pc = jnp.exp(sp - m), jnp.exp(sc - m)
            den = jnp.sum(pp, axis=1, keepdims=True) + jnp.sum(pc, axis=1, keepdims=True)
            if has_sink:
                den = den + jnp.exp(sk - m)
            acc = (_dot(pp.astype(BF16), v_ref[kq, pl.ds(prv, BLK), :], 1, 0)
                   + _dot(pc.astype(BF16), v_ref[kq, pl.ds(cur, BLK), :], 1, 0))
            o_ref[qq, pl.ds(cur, BLK), :] = acc / den
            lse_ref[qq, pl.ds(cur, BLK), :] = m + jnp.log(den)

        _band_sweep(length, ns, one)

    xspecs, qspec, _, bspec, sspec, colspec = _band_specs(length, g, bias_div, offs)
    return pl.pallas_call(
        body, name=name,
        out_shape=[jax.ShapeDtypeStruct((nq, length, HEAD_DIM), F32), jax.ShapeDtypeStruct((nq, length, 1), F32)],
        grid=(nq // ns,), in_specs=xspecs + [bspec, sspec],
        out_specs=[qspec, colspec], compiler_params=_params(("parallel",)),
    )(x, x, x, bias, sink)


def _band_bwd(x, bias, sink, o, lse, do, dlse, *, nq, offs, g, bias_div, has_sink, name):
    length = x.shape[1]
    ns = _band_layout(g, bias_div)
    nk, nbias = nq // g, nq // bias_div

    def body(q_ref, k_ref, v_ref, b_ref, s_ref, o_ref, lse_ref, do_ref, dlse_ref,
             dq_ref, dk_ref, dv_ref, db_ref, dsk_ref, dkp_ref, dvp_ref):
        for ref in (db_ref, dsk_ref, dkp_ref, dvp_ref):
            ref[...] = jnp.zeros_like(ref)

        @pl.when(pl.program_id(0) % g == 0)
        def _():
            dk_ref[...] = jnp.zeros_like(dk_ref)
            dv_ref[...] = jnp.zeros_like(dv_ref)

        def one(qq, cur, prv, first):
            kq, bq = qq, 0
            qv, sp, sc = _band_scores(q_ref, k_ref, b_ref, qq, kq, bq, cur, prv, first)
            rows, prow = pl.ds(cur, BLK), pl.ds(prv, BLK)
            lse_v = lse_ref[qq, rows, :]
            pp, pc = jnp.exp(sp - lse_v), jnp.exp(sc - lse_v)
            dov = do_ref[qq, rows, :]
            dob = dov.astype(BF16)
            coef = dlse_ref[qq, rows, :] - jnp.sum(dov * o_ref[qq, rows, :], axis=1, keepdims=True)
            dsp = pp * (_dot(dob, v_ref[kq, prow, :], 1, 1) + coef)
            dsc = pc * (_dot(dob, v_ref[kq, rows, :], 1, 1) + coef)
            dspb, dscb = dsp.astype(BF16), dsc.astype(BF16)
            dq_ref[qq, rows, :] = ((_dot(dspb, k_ref[kq, prow, :], 1, 0) + _dot(dscb, k_ref[kq, rows, :], 1, 0))
                                   * (HEAD_DIM ** -0.5))
            dk_ref[kq, rows, :] += _dot(dscb, qv, 0, 0)
            dkp_ref[kq, prow, :] += _dot(dspb, qv, 0, 0)
            dv_ref[kq, rows, :] += _dot(pc.astype(BF16), dob, 0, 0)
            dvp_ref[kq, prow, :] += _dot(pp.astype(BF16), dob, 0, 0)
            db_ref[bq, :, :BLK] += dsp
            db_ref[bq, :, BLK:] += dsc
            if has_sink:
                dsk_ref[qq] += jnp.sum(jnp.exp(s_ref[qq][:, :1] - lse_v) * coef)

        _band_sweep(length, ns, one)
        dk_ref[...] += dkp_ref[...]
        dv_ref[...] += dvp_ref[...]

    xspecs, qspec, kvspec, bspec, sspec, colspec = _band_specs(length, g, bias_div, offs)
    return pl.pallas_call(
        body, name=name,
        out_shape=[jax.ShapeDtypeStruct((nq, length, HEAD_DIM), F32), jax.ShapeDtypeStruct((nk, length, HEAD_DIM), F32),
                   jax.ShapeDtypeStruct((nk, length, HEAD_DIM), F32), jax.ShapeDtypeStruct((nbias, BLK, 2 * BLK), F32),
                   jax.ShapeDtypeStruct((nq, 1, LANES), F32)],
        grid=(nq // ns,),
        in_specs=xspecs + [bspec, sspec, qspec, colspec, qspec, colspec],
        out_specs=[qspec, kvspec, kvspec, bspec, sspec],
        scratch_shapes=[pltpu.VMEM((ns, length, HEAD_DIM), F32), pltpu.VMEM((ns, length, HEAD_DIM), F32)],
        compiler_params=_params(("arbitrary",)),
    )(x, x, x, bias, sink, o, lse, do, dlse)


TOK_TILE = 512


def _dil_merge(outs, lses, dout, name):
    tr = TOK_TILE
    dils = [d for _, d in DIL_PATTERNS]
    n = len(dils)
    o4 = [o.reshape(2, d, SEQ // d, HEAD_DIM) for o, d in zip(outs, dils)]
    l4 = [l.reshape(2, d, SEQ // d, 1) for l, d in zip(lses, dils)]
    o_specs = [pl.BlockSpec((2, d, tr // d, HEAD_DIM), lambda i: (0, 0, i, 0)) for d in dils]
    l_specs = [pl.BlockSpec((2, d, tr // d, 1), lambda i: (0, 0, i, 0)) for d in dils]
    tok = pl.BlockSpec((tr, 2 * HEAD_DIM), lambda i: (i, 0))
    scratch = ([pltpu.VMEM((tr, 2 * HEAD_DIM), F32) for _ in dils] + [pltpu.VMEM((tr, 1), F32) for _ in range(2 * n)]
               + [pltpu.VMEM((tr // d, 2 * HEAD_DIM), F32) for d in dils])

    def to_tokens(o_ref, l_ref, d, pair, cols, stage):
        for r in range(d):
            rows = pl.ds(r, tr // d, stride=d) if d > 1 else slice(None)
            stage[:, :HEAD_DIM] = o_ref[0, r]
            stage[:, HEAD_DIM:] = o_ref[1, r]
            pair[rows, :] = stage[...]
            for h in range(2):
                cols[h][rows, :] = l_ref[h, r]
        return pair[...], [cols[0][...], cols[1][...]]

    def weights(ls):
        left = lax.broadcasted_iota(jnp.int32, (tr, 2 * HEAD_DIM), 1) < HEAD_DIM
        per_head = []
        for h in range(2):
            m = ls[0][h]
            for g in range(1, n):
                m = jnp.maximum(m, ls[g][h])
            es = [jnp.exp(ls[g][h] - m) for g in range(n)]
            den = es[0]
            for e in es[1:]:
                den = den + e
            per_head.append([e / den for e in es])
        return per_head, [jnp.where(left, per_head[0][g], per_head[1][g]) for g in range(n)], left

    def load(refs):
        pairs, cols, stages = refs[:n], refs[n:3 * n], refs[3 * n:]
        return pairs, [cols[2 * g:2 * g + 2] for g in range(n)], stages

    if dout is None:
        def body(*refs):
            pairs, cols, stages = load(refs[2 * n + 1:])
            toks = [to_tokens(refs[g], refs[n + g], dils[g], pairs[g], cols[g], stages[g]) for g in range(n)]
            _, alphas, _ = weights([t[1] for t in toks])
            acc = alphas[0] * toks[0][0]
            for g in range(1, n):
                acc = acc + alphas[g] * toks[g][0]
            refs[2 * n][...] = acc

        return pl.pallas_call(
            body, name=name, out_shape=jax.ShapeDtypeStruct((SEQ, 2 * HEAD_DIM), F32), grid=(SEQ // tr,),
            in_specs=o_specs + l_specs, out_specs=tok, scratch_shapes=scratch, compiler_params=_params(("parallel",)),
        )(*o4, *l4)

    def body(*refs):
        do_refs, dl_refs = refs[2 * n + 1:3 * n + 1], refs[3 * n + 1:4 * n + 1]
        pairs, cols, stages = load(refs[4 * n + 1:])
        toks = [to_tokens(refs[g], refs[n + g], dils[g], pairs[g], cols[g], stages[g]) for g in range(n)]
        per_head, alphas, left = weights([t[1] for t in toks])
        dov = refs[2 * n][...]
        das = []
        for g in range(n):
            prod = dov * toks[g][0]
            das.append([jnp.sum(jnp.where(left, prod, 0.0), axis=1, keepdims=True),
                        jnp.sum(jnp.where(left, 0.0, prod), axis=1, keepdims=True)])
        dbar = [sum(per_head[h][g] * das[g][h] for g in range(n)) for h in range(2)]
        for g, d in enumerate(dils):
            pairs[g][...] = alphas[g] * dov
            for h in range(2):
                cols[g][h][...] = per_head[h][g] * (das[g][h] - dbar[h])
            for r in range(d):
                rows = pl.ds(r, tr // d, stride=d) if d > 1 else slice(None)
                v = pairs[g][rows, :]
                for h in range(2):
                    do_refs[g][h, r] = v[:, h * HEAD_DIM:(h + 1) * HEAD_DIM]
                    dl_refs[g][h, r] = cols[g][h][rows, :]

    out = pl.pallas_call(
        body, name=name,
        out_shape=[jax.ShapeDtypeStruct(o.shape, F32) for o in o4] + [jax.ShapeDtypeStruct(l.shape, F32) for l in l4],
        grid=(SEQ // tr,), in_specs=o_specs + l_specs + [tok], out_specs=o_specs + l_specs, scratch_shapes=scratch,
        compiler_params=_params(("parallel",)),
    )(*o4, *l4, dout)
    return [t.reshape(s.shape) for t, s in zip(out, list(outs) + list(lses))]


def _tri(cmp):
    r = lax.broadcasted_iota(jnp.int32, (SB_TILE, SB_TILE), 0)
    c = lax.broadcasted_iota(jnp.int32, (SB_TILE, SB_TILE), 1)
    return cmp(r, c).astype(BF16)


def _cum(x, tri, terms):
    acc, rest = None, x
    for _ in range(terms):
        part = rest.astype(BF16)
        rest = rest - part.astype(F32)
        d = _dot(part, tri, 1, 0)
        acc = d if acc is None else acc + d
    return acc


def _sb_logits(q, ks, diagonal):
    t = SB_TILE
    z = _dot(q, ks, 1, 1)
    e = jnp.exp(-jnp.abs(z))
    lf = -(jnp.maximum(z, 0.0) + jnp.log(1.0 + e))
    if not diagonal:
        return z, e, lf, None
    mask = lax.broadcasted_iota(jnp.int32, (t, t), 1) < lax.broadcasted_iota(jnp.int32, (t, t), 0)
    return z, e, jnp.where(mask, lf, 0.0), mask


def _sb_specs(h, s):
    t = SB_TILE
    tile = pl.BlockSpec((h, t, HEAD_DIM), lambda i: (0, i, 0))
    keys = pl.BlockSpec((h, s, HEAD_DIM), lambda i: (1, 0, 0))
    values = pl.BlockSpec((h, s, HEAD_DIM), lambda i: (2, 0, 0))
    return tile, keys, values, pl.BlockSpec((h, t, 1), lambda i: (0, i, 0))


def _sb_fwd(x, name):
    h, s = x.shape[0] // 3, x.shape[1]
    t = SB_TILE

    def body(q_ref, k_ref, v_ref, o_ref, tot_ref):
        i = pl.program_id(0)
        after = _tri(lambda r, c: r > c)

        def tile(j, carry, diagonal):
            rows = pl.ds(pl.multiple_of(j * t, t), t)
            out = []
            for hh, (right, acc) in enumerate(carry):
                z, _, lf, mask = _sb_logits(q_ref[hh], k_ref[hh, rows, :], diagonal)
                w = jnp.exp(z + lf + (right + _cum(lf, after, 2)))
                w = w if mask is None else jnp.where(mask, w, 0.0)
                out.append((right + jnp.sum(lf, axis=1, keepdims=True), acc + _dot(w.astype(BF16), v_ref[hh, rows, :], 1, 0)))
            return tuple(out)

        carry = tile(i, tuple((jnp.zeros((t, 1), F32), jnp.zeros((t, HEAD_DIM), F32)) for _ in range(h)), True)
        carry = lax.fori_loop(0, i, lambda jj, c: tile(i - 1 - jj, c, False), carry)
        for hh, (right, acc) in enumerate(carry):
            o_ref[hh] = acc
            tot_ref[hh] = right

    tile_spec, keys, values, col = _sb_specs(h, s)
    return pl.pallas_call(
        body, name=name, out_shape=[jax.ShapeDtypeStruct((h, s, HEAD_DIM), F32), jax.ShapeDtypeStruct((h, s, 1), F32)],
        grid=(s // t,), in_specs=[tile_spec, keys, values], out_specs=[tile_spec, col],
        compiler_params=_params(("parallel",)),
    )(x, x, x)


def _sb_bwd(x, tot, do, name):
    h, s = x.shape[0] // 3, x.shape[1]
    t = SB_TILE

    def body(q_ref, k_ref, v_ref, tot_ref, do_ref, dq_ref, dk_ref, dv_ref):
        i = pl.program_id(0)

        @pl.when(i == 0)
        def _():
            dk_ref[...] = jnp.zeros_like(dk_ref)
            dv_ref[...] = jnp.zeros_like(dv_ref)

        upto = _tri(lambda r, c: r <= c)
        before = _tri(lambda r, c: r < c)

        def tile(j, carry, diagonal):
            rows = pl.ds(pl.multiple_of(j * t, t), t)
            out = []
            for hh, (left, cleft, dq) in enumerate(carry):
                qv, ks, dob = q_ref[hh], k_ref[hh, rows, :], do_ref[hh].astype(BF16)
                z, e, lf, mask = _sb_logits(qv, ks, diagonal)
                between = tot_ref[hh] - (left + _cum(lf, upto, 2))
                w = jnp.exp(z + lf + between)
                w = w if mask is None else jnp.where(mask, w, 0.0)
                dlog = w * _dot(dob, v_ref[hh, rows, :], 1, 1)
                cfail = cleft + _cum(dlog, before, 2)
                sig = jnp.where(z >= 0.0, 1.0, e) / (1.0 + e)
                dz = dlog * (1.0 - sig) - sig * cfail
                dz = (dz if mask is None else jnp.where(mask, dz, 0.0)).astype(BF16)
                dk_ref[hh, rows, :] += _dot(dz, qv, 0, 0)
                dv_ref[hh, rows, :] += _dot(w.astype(BF16), dob, 0, 0)
                out.append((left + jnp.sum(lf, axis=1, keepdims=True), cleft + jnp.sum(dlog, axis=1, keepdims=True),
                            dq + _dot(dz, ks, 1, 0)))
            return tuple(out)

        zero = jnp.zeros((t, 1), F32)
        carry = lax.fori_loop(0, i, lambda j, c: tile(j, c, False),
                              tuple((zero, zero, jnp.zeros((t, HEAD_DIM), F32)) for _ in range(h)))
        for hh, (_, _, dq) in enumerate(tile(i, carry, True)):
            dq_ref[hh] = dq * (HEAD_DIM ** -0.5)

    tile_spec, keys, values, col = _sb_specs(h, s)
    full = pl.BlockSpec((h, s, HEAD_DIM), lambda i: (0, 0, 0))
    shp = jax.ShapeDtypeStruct((h, s, HEAD_DIM), F32)
    return pl.pallas_call(
        body, name=name, out_shape=[shp, shp, shp], grid=(s // t,),
        in_specs=[tile_spec, keys, values, col, tile_spec],
        out_specs=[tile_spec, full, full], compiler_params=_params(("arbitrary",)),
    )(x, x, x, tot, do)


COL_SB, COL_DIL, COL_SWA = 0, 3 * H_SB * HEAD_DIM, 3 * H_SB * HEAD_DIM + 3 * H_DIL * HEAD_DIM
N_SWA = H_SWA_Q + 2 * H_SWA_KV


def _dil_col(t, g):
    return COL_DIL + t * H_DIL * HEAD_DIM + g * 2 * HEAD_DIM


def _split_heads(qkv, name):
    tr = TOK_TILE
    scale = HEAD_DIM ** -0.5
    dils = [d for _, d in DIL_PATTERNS]

    def body(x_ref, sb_ref, d0_ref, d1_ref, d2_ref, swa_ref, pair):
        def head(col, scaled):
            v = x_ref[:, col:col + HEAD_DIM]
            return (v * scale if scaled else v).astype(BF16)

        for hh in range(3 * H_SB):
            sb_ref[hh] = head(COL_SB + hh * HEAD_DIM, hh < H_SB)
        for hh in range(N_SWA):
            swa_ref[hh] = head(COL_SWA + hh * HEAD_DIM, hh < H_SWA_Q)
        for t in range(3):
            for g, (d, out_ref) in enumerate(zip(dils, (d0_ref, d1_ref, d2_ref))):
                col = _dil_col(t, g)
                if d == 1:
                    for h in range(2):
                        out_ref[t * 2 + h] = head(col + h * HEAD_DIM, t == 0)
                    continue
                pair[...] = x_ref[:, col:col + 2 * HEAD_DIM]
                for r in range(d):
                    v = pair[pl.ds(r, tr // d, stride=d), :]
                    v = v * scale if t == 0 else v
                    for h in range(2):
                        out_ref[t * 2 * d + h * d + r] = v[:, h * HEAD_DIM:(h + 1) * HEAD_DIM].astype(BF16)

    def heads(n, length):
        return jax.ShapeDtypeStruct((n, length, HEAD_DIM), BF16)

    def spec(n, rows):
        return pl.BlockSpec((n, rows, HEAD_DIM), lambda i: (0, i, 0))

    return pl.pallas_call(
        body, name=name,
        out_shape=[heads(3 * H_SB, SEQ)] + [heads(6 * d, SEQ // d) for d in dils] + [heads(N_SWA, SEQ)],
        grid=(SEQ // tr,), in_specs=[pl.BlockSpec((tr, D_QKV), lambda i: (i, 0))],
        out_specs=[spec(3 * H_SB, tr)] + [spec(6 * d, tr // d) for d in dils] + [spec(N_SWA, tr)],
        scratch_shapes=[pltpu.VMEM((tr, 2 * HEAD_DIM), F32)], compiler_params=_params(("parallel",)),
    )(qkv)


def _join_heads(sb, dil, swa, name):
    tr = TOK_TILE
    dils = [d for _, d in DIL_PATTERNS]

    def body(*refs):
        sb_refs, dil_refs, swa_refs = refs[:3], [refs[3 + 3 * g:6 + 3 * g] for g in range(3)], refs[12:15]
        o_ref, pair, stages = refs[15], refs[16], refs[17:]

        def put(col, v):
            o_ref[:, col:col + v.shape[1]] = v.astype(BF16)

        for t in range(3):
            for h in range(H_SB):
                put(COL_SB + (t * H_SB + h) * HEAD_DIM, sb_refs[t][h])
        col = COL_SWA
        for ref in swa_refs:
            for h in range(ref.shape[0]):
                put(col, ref[h])
                col += HEAD_DIM
        for t in range(3):
            for g, d in enumerate(dils):
                ref, col = dil_refs[g][t], _dil_col(t, g)
                if d == 1:
                    for h in range(2):
                        put(col + h * HEAD_DIM, ref[h])
                    continue
                stage = stages[g - 1]
                for r in range(d):
                    stage[:, :HEAD_DIM] = ref[r]
                    stage[:, HEAD_DIM:] = ref[d + r]
                    pair[pl.ds(r, tr // d, stride=d), :] = stage[...]
                put(col, pair[...])

    def spec(n, rows):
        return pl.BlockSpec((n, rows, HEAD_DIM), lambda i: (0, i, 0))

    ins = list(sb) + [t for g in range(3) for t in dil[g]] + list(swa)
    in_specs = ([spec(H_SB, tr)] * 3 + [spec(2 * d, tr // d) for d in dils for _ in range(3)]
                + [spec(H_SWA_Q, tr), spec(H_SWA_KV, tr), spec(H_SWA_KV, tr)])
    return pl.pallas_call(
        body, name=name, out_shape=jax.ShapeDtypeStruct((SEQ, D_QKV), BF16), grid=(SEQ // tr,), in_specs=in_specs,
        out_specs=pl.BlockSpec((tr, D_QKV), lambda i: (i, 0)),
        scratch_shapes=[pltpu.VMEM((tr, 2 * HEAD_DIM), F32)] + [pltpu.VMEM((tr // d, 2 * HEAD_DIM), F32) for d in dils[1:]],
        compiler_params=_params(("parallel",)),
    )(*ins)


def _mixer_fwd(qkv, bias, sinks_l, tag):
    sb, d0, d1, d2, swa = _split_heads(qkv, name=f"split_heads_{tag}")
    st = {"sb": sb, "dil": (d0, d1, d2), "swa": swa}
    o_sb, st["sb_tot"] = _sb_fwd(sb, name=f"sb_fwd_{tag}")
    st["dil_out"], st["dil_lse"], st["dil_sink"] = [], [], []
    for gi, (_, d) in enumerate(DIL_PATTERNS):
        sink = jnp.zeros((2 * d, 1, LANES), F32)
        og, lg = _band_fwd(st["dil"][gi], bias[2 * gi:2 * gi + 2], sink, nq=2 * d, offs=(0, 2 * d, 4 * d), g=1, bias_div=d,
                           has_sink=False, name=f"dil{gi}_fwd_{tag}")
        st["dil_out"].append(og)
        st["dil_lse"].append(lg)
        st["dil_sink"].append(sink)
    o_dil = _dil_merge(st["dil_out"], st["dil_lse"], None, name=f"dil_merge_fwd_{tag}")
    st["swa_sink"] = jnp.broadcast_to(sinks_l.reshape(H_SWA_Q, 1, 1), (H_SWA_Q, 1, LANES))
    st["swa_out"] = _band_fwd(swa, bias[H_DIL:], st["swa_sink"], nq=H_SWA_Q, offs=(0, H_SWA_Q, H_SWA_Q + H_SWA_KV),
                              g=H_SWA_Q // H_SWA_KV, bias_div=1, has_sink=True, name=f"swa_fwd_{tag}")
    return (o_sb, o_dil, st["swa_out"][0]), st


def _mixer_bwd(st, bias, do_sb, do_dil, do_swa, tag):
    d_sb = _sb_bwd(st["sb"], st["sb_tot"], do_sb, name=f"sb_bwd_{tag}")
    dmerge = _dil_merge(st["dil_out"], st["dil_lse"], do_dil, name=f"dil_merge_bwd_{tag}")
    d_dil, dbs = [], []
    for gi, (_, d) in enumerate(DIL_PATTERNS):
        dq, dk, dv, db, _ = _band_bwd(st["dil"][gi], bias[2 * gi:2 * gi + 2], st["dil_sink"][gi], st["dil_out"][gi],
                                      st["dil_lse"][gi], dmerge[gi], dmerge[3 + gi], nq=2 * d, offs=(0, 2 * d, 4 * d),
                                      g=1, bias_div=d, has_sink=False, name=f"dil{gi}_bwd_{tag}")
        d_dil.append((dq, dk, dv))
        dbs.append(db)
    o_sw, l_sw = st["swa_out"]
    dq_sw, dk_sw, dv_sw, db_sw, dsink = _band_bwd(st["swa"], bias[H_DIL:], st["swa_sink"], o_sw, l_sw, do_swa,
                                                  jnp.zeros_like(l_sw), nq=H_SWA_Q, offs=(0, H_SWA_Q, H_SWA_Q + H_SWA_KV),
                                                  g=H_SWA_Q // H_SWA_KV, bias_div=1, has_sink=True, name=f"swa_bwd_{tag}")
    dqkv = _join_heads(d_sb, d_dil, (dq_sw, dk_sw, dv_sw), name=f"join_heads_{tag}")
    return dqkv, jnp.concatenate(dbs + [db_sw], 0), dsink[:, 0, 0]


PIECES = ("ffn0", "mix", "ffn1")


def _ffn_fwd(x_in, w, gain, mod_j, tag, after=None):
    st = {"x": x_in, "w": w}
    st["h"] = _norm_fwd(x_in, _row(gain), _row(mod_j[1]), _row(mod_j[0]), name=f"norm_fwd_{tag}", after=after)
    st["a"], st["u"], st["s"] = _ffn_up(st["h"], w["gate"], w["up"], name=f"up_{tag}")
    st["f"], x_out = _mm(st["s"], w["down"], res=x_in, colscale=_row(0.5 * mod_j[2]), emit_acc=True, tm=512, tn=1024,
                         name=f"down_{tag}")
    return x_out, st


def _ffn_bwd(dx_out, st, gain, mod_j, tag, done, pre, nxt):
    w = st["w"]

    def latest(new, old):
        return old if new is None else new

    df, dgate = pre or _gate_bwd(dx_out, st["f"], _row(0.5 * mod_j[2]), 0.5, name=f"gate_bwd_{tag}")
    token = done({"down": _mm_tn(st["s"], df, tm=D_FF // 2, name=f"dwd_{tag}")})
    da, du = _ffn_bwd_ds(df, w["down"], st["a"], st["u"], name=f"ds_{tag}")
    token = latest(done({"gate": _mm_tn(da, st["h"], after=token, tm=D_FF // 2, name=f"dwg_{tag}")}), token)
    token = latest(done({"up": _mm_tn(du, st["h"], after=token, tm=D_FF // 2, name=f"dwu_{tag}")}), token)
    dx_in, sum_dh, sum_dhx, made = _dh_norm_bwd(da, w["gate"], du, w["up"], st["x"], dx_out, _row(gain), _row(mod_j[1]), nxt,
                                                after=token, name=f"dh_{tag}")
    dmod = jnp.concatenate([sum_dh, gain * sum_dhx, dgate], 0)
    return dx_in, dmod, (1.0 + mod_j[1]) * sum_dhx[0], made


def _mix_fwd(x_in, w, gain, mod_j, bias, sinks_l, tag, after=None):
    st = {"x": x_in, "w": w}
    st["h"] = _norm_fwd(x_in, _row(gain), _row(mod_j[1]), _row(mod_j[0]), name=f"norm_fwd_mix_{tag}", after=after)
    qkv = _mm(st["h"], w["in"], tb=True, tm=SEQ, b_rows=(0, D_QKV), name=f"qkv_{tag}")
    st["gates"] = _mm(st["h"], w["in"], tb=True, tm=SEQ, b_rows=(D_QKV, D_GATES), name=f"gates_{tag}")
    outs, st["mix"] = _mixer_fwd(qkv, bias, sinks_l, tag)
    st["merged"], *st["t"] = _merge_fwd(*outs, st["gates"], w["br_sb"], w["br_dil"], w["br_swa"], name=f"merge_fwd_{tag}")
    st["f"], x_out = _mm(st["merged"], w["out"], res=x_in, colscale=_row(mod_j[2]), emit_acc=True, name=f"out_{tag}")
    return x_out, st


def _mix_bwd(dx_out, st, gain, mod_j, bias, tag, done, pre, nxt):
    w = st["w"]
    df, dgate = pre or _gate_bwd(dx_out, st["f"], _row(mod_j[2]), 1.0, name=f"gate_bwd_mix_{tag}")
    g = {"out": _mm_tn(st["merged"], df, name=f"dw_out_{tag}")}
    dmerged = _mm(df, w["out"], tb=True, name=f"dmerged_{tag}")
    dgates, do_sb, do_dil, do_swa, dbr_sb, dbr_dil, dbr_swa = _merge_bwd(
        dmerged, *st["t"], st["gates"], w["br_sb"], w["br_dil"], w["br_swa"], name=f"merge_bwd_{tag}")
    g["br_sb"] = _mm_tn(st["t"][0], dbr_sb, name=f"dw_br_sb_{tag}")
    g["br_dil"] = _mm_tn(st["t"][1], dbr_dil, name=f"dw_br_dil_{tag}")
    g["br_swa"] = _mm_tn(st["t"][2], dbr_swa, name=f"dw_br_swa_{tag}")
    dqkv, dbias, dsinks = _mixer_bwd(st["mix"], bias, do_sb, do_dil, do_swa, tag)
    dw_qkv = _mm_tn(dqkv, st["h"], out_rows=D_QKV + D_GATES, name=f"dw_qkv_{tag}")
    g["in"] = _mm_tn(dgates, st["h"], out_rows=D_QKV + D_GATES, row0=D_QKV, prev=dw_qkv, name=f"dw_gates_{tag}")
    dx_in, sum_dh, sum_dhx, made = _dh_norm_bwd(dqkv, w["in"], dgates, w["in"], st["x"], dx_out, _row(gain), _row(mod_j[1]),
                                                nxt, after=done(g), b_rows=(0, D_QKV), name=f"dh_mix_{tag}")
    dmod = jnp.concatenate([sum_dh, gain * sum_dhx, dgate], 0)
    return dx_in, dmod, (1.0 + mod_j[1]) * sum_dhx[0], dbias, dsinks, made


def _local_step(x, target, mod, gains, weights_of, rel_bias, sinks, final_gain, grads_done):
    tables = jnp.asarray(_bucket_tables())
    bias = _bias_build(rel_bias, tables, name="bias_build")
    states, h = [], x
    for l in range(DEPTH):
        st = {}
        for j, piece in enumerate(PIECES):
            w, after = weights_of(l, piece, h)
            if piece == "mix":
                h, st[piece] = _mix_fwd(h, w, gains[l, j], mod[l, j], bias, sinks[l], f"l{l}", after)
            else:
                h, st[piece] = _ffn_fwd(h, w, gains[l, j], mod[l, j], f"{piece}_l{l}", after)
        states.append(st)
    loss, dx, dfinal = _final_loss(h, target, _row(final_gain), name="final_loss")
    dmods = [[None] * 3 for _ in range(DEPTH)]
    dgains = [[None] * 3 for _ in range(DEPTH)]
    dsinks = [None] * DEPTH
    dbias, made = None, None
    sweep = [(l, j) for l in reversed(range(DEPTH)) for j in reversed(range(3))]
    for k, (l, j) in enumerate(sweep):
        piece = PIECES[j]
        done = lambda grads, l=l, piece=piece: grads_done(l, piece, grads)
        nxt = None
        if k + 1 < len(sweep):
            nl, nj = sweep[k + 1]
            coef = 1.0 if PIECES[nj] == "mix" else 0.5
            nxt = (states[nl][PIECES[nj]]["f"], _row(coef * mod[nl, nj, 2]), coef)
        if piece == "mix":
            dx, dmods[l][j], dgains[l][j], db, dsinks[l], made = _mix_bwd(
                dx, states[l][piece], gains[l, j], mod[l, j], bias, f"l{l}", done, made, nxt)
            dbias = db if dbias is None else dbias + db
        else:
            dx, dmods[l][j], dgains[l][j], made = _ffn_bwd(
                dx, states[l][piece], gains[l, j], mod[l, j], f"{piece}_l{l}", done, made, nxt)
    drel = _bias_grad(dbias, tables, name="bias_grad")[:, 0, :N_BUCKETS].T
    dmod = jnp.stack([jnp.stack(m) for m in dmods])
    dgain = jnp.stack([jnp.stack(g) for g in dgains])
    return loss, dx, dmod, dgain, dfinal[0], drel, jnp.stack(dsinks)


BR_ROWS = (H_SB * HEAD_DIM, 2 * HEAD_DIM, H_SWA_Q * HEAD_DIM)


def _lanes_unshard(g, lead):
    _, rows, _ = g.shape
    r = rows // lead
    return g.reshape(N_DEV, lead, r, LANES).transpose(1, 2, 0, 3).reshape(lead, r, N_DEV * LANES)


def _lanes_shard(full):
    lead, r, _ = full.shape
    return full.reshape(lead, r, N_DEV, LANES).transpose(2, 0, 1, 3).reshape(N_DEV, lead * r, LANES)


def _pack_rows(parts, dtype):
    flat = jnp.concatenate([p.astype(dtype).reshape(-1) for p in parts])
    pad = (-flat.shape[0]) % (16 * LANES)
    if pad:
        flat = jnp.concatenate([flat, jnp.zeros((pad,), dtype)])
    return flat.reshape(-1, LANES)


def _unshard(gathered, axis):
    moved = jnp.moveaxis(gathered, 0, axis)
    shape = list(moved.shape)
    shape[axis:axis + 2] = [shape[axis] * shape[axis + 1]]
    return moved.reshape(shape)


def kernel(x, c, w_ada, b_ada, norm_gain, w_ffn_gate, w_ffn_up, w_ffn_down, w_in, w_br_sb, w_br_dil, w_br_swa, w_out, sinks, rel_bias, final_gain, loss_target, m_w_ada, m_b_ada, m_norm_gain, m_w_ffn_gate, m_w_ffn_up, m_w_ffn_down, m_w_in, m_w_br_sb, m_w_br_dil, m_w_br_swa, m_w_out, m_sinks, m_rel_bias, m_final_gain, v_w_ada, v_b_ada, v_norm_gain, v_w_ffn_gate, v_w_ffn_up, v_w_ffn_down, v_w_in, v_w_br_sb, v_w_br_dil, v_w_br_swa, v_w_out, v_sinks, v_rel_bias, v_final_gain):
    me = 4 * lax.axis_index("x") + 2 * lax.axis_index("y") + lax.axis_index("c")
    d = D_MODEL
    gate_t, up_t, in_t = jnp.swapaxes(w_ffn_gate, 2, 3), jnp.swapaxes(w_ffn_up, 2, 3), jnp.swapaxes(w_in, 1, 2)

    def piece_shards(l, piece):
        bf = lambda t: t.astype(BF16)
        if piece == "mix":
            return [bf(in_t[l]), jnp.concatenate([bf(w_br_sb[l]), bf(w_br_dil[l]), bf(w_br_swa[l])], 0), bf(w_out[l])]
        i = PIECES.index(piece) // 2
        return [bf(gate_t[l, i]), bf(up_t[l, i]), bf(w_ffn_down[l, i])]

    br_off = np.concatenate([[0], np.cumsum(BR_ROWS)])

    def piece_weights(gathered, piece):
        if piece == "mix":
            g_in, g_br, g_out = gathered
            f_br = [_lanes_unshard(g_br[:, br_off[k]:br_off[k + 1]], 1)[0] for k in range(3)]
            return {"in": g_in.reshape(D_QKV + D_GATES, d), "br_sb": f_br[0], "br_dil": f_br[1], "br_swa": f_br[2],
                    "out": g_out.reshape(d, d)}
        return {n: g.reshape(D_FF, d) for n, g in zip(("gate", "up", "down"), gathered)}

    order = [(l, piece) for l in range(DEPTH) for piece in PIECES]
    ahead = 3
    in_flight, passed = {}, {}

    def start_gather(k, after):
        l, piece = order[k]
        in_flight[k], token = _relay_start(piece_shards(l, piece), after, name=f"gather_{piece}_l{l}_start")
        return token

    small, = _all_gather([_pack_rows([c, norm_gain], F32)], after=start_gather(0, c), name="gather_cond")
    c_all = small[:, :d // LANES].reshape(N_DEV, d)
    gains = _unshard(small[:, d // LANES:d // LANES + 6].reshape(N_DEV, DEPTH, 3, LANES), 2)

    cols = w_ada.shape[2]
    mod_cols = jnp.stack([_ada_fwd(c_all, w_ada[l], name=f"ada_fwd_l{l}") for l in range(DEPTH)])
    mod_all, = _all_gather([_pack_rows([mod_cols], F32)], name="gather_mod")
    mod_all = mod_all.reshape(N_DEV, -1)[:, :DEPTH * N_DEV * cols].reshape(N_DEV, DEPTH, N_DEV, cols)
    mod_mine = lax.dynamic_index_in_dim(mod_all, me, axis=2, keepdims=False)
    mod = (mod_mine.transpose(1, 0, 2).reshape(DEPTH, N_DEV * cols) + b_ada).reshape(DEPTH, 3, 3, d)

    token = mod_all
    for k in range(1, 1 + ahead):
        token = start_gather(k, token)
    mod = mod + token[0, 0]

    def weights_of(l, piece, h):
        k = order.index((l, piece))
        token = start_gather(k + ahead, h) if k + ahead < len(order) and k + ahead not in in_flight else None
        for nxt in ([k] if k < 3 else []) + ([k + 1] if 3 <= k + 1 < len(order) else []):
            nl, npiece = order[nxt]
            passed[nxt], token = _relay_pass(in_flight[nxt], h if token is None else token,
                                             name=f"gather_{npiece}_l{nl}_pass")
        landed = _relay_wait(passed[k], h if token is None else token, name=f"gather_{piece}_l{l}_wait")
        return piece_weights(landed, piece), token

    exchanges, have, deferred = {}, {}, []

    def grads_done(l, piece, g):
        key = (l, piece)
        have.setdefault(key, {}).update(g)
        if piece == "mix":
            if len(have[key]) < 5:
                return None
            g = have[key]
            s_br = jnp.concatenate([_lanes_shard(g[n][None]) for n in ("br_sb", "br_dil", "br_swa")], 1)
            groups = [(("in", "br", "out"), [g["in"].reshape(N_DEV, -1, d), s_br, g["out"].reshape(N_DEV, -1, d)])]
        elif key == order[0]:
            deferred.extend(((n,), [t.reshape(N_DEV, -1, d)]) for n, t in g.items())
            return None
        elif len(have[key]) < 3:
            return None
        else:
            groups = [(("gate", "up", "down"), [have[key][n].reshape(N_DEV, -1, d) for n in ("gate", "up", "down")])]
        token = None
        for names, sg in groups:
            state, token = _exchange_start(sg, None, name=f"exchange_{piece}_l{l}_{names[0]}_start")
            exchanges.setdefault(key, []).append((names, state))
        return token

    loss, dx, dmod, dgains, dfinal, drel, dsinks = _local_step(
        x[0], loss_target[0], mod, gains, weights_of, rel_bias, sinks, final_gain, grads_done)

    flat = lambda t: t.reshape(-1, t.shape[-1])
    transposed = lambda ts: tuple(flat(jnp.swapaxes(t, -1, -2)) for t in ts)
    families = {
        "gate": transposed((w_ffn_gate, m_w_ffn_gate, v_w_ffn_gate)), "up": transposed((w_ffn_up, m_w_ffn_up, v_w_ffn_up)),
        "down": tuple(flat(t) for t in (w_ffn_down, m_w_ffn_down, v_w_ffn_down)),
        "in": transposed((w_in, m_w_in, v_w_in)),
        "br": tuple(flat(jnp.concatenate(ts, 1)) for ts in ((w_br_sb, w_br_dil, w_br_swa), (m_w_br_sb, m_w_br_dil, m_w_br_swa),
                                                            (v_w_br_sb, v_w_br_dil, v_w_br_swa))),
        "out": tuple(flat(t) for t in (w_out, m_w_out, v_w_out))}
    parts, stepped = {}, {}

    def step(keys, after):
        for key in keys:
            for names, ex_state in exchanges[key]:
                landed = _exchange_wait(ex_state, after, name=f"exchange_{key[1]}_l{key[0]}_{names[0]}_wait")
                parts.setdefault(key, {}).update(zip(names, landed))
                after = landed[0]
        for key in keys:
            l, piece = key
            for n, group in parts[key].items():
                w2, m2, v2 = families[n]
                rows = group.shape[1]
                row0 = (2 * l + PIECES.index(piece) // 2) * rows if piece != "mix" else l * rows
                stepped[n] = _reduce_adamw([group], w2, m2, v2, row0, stepped.get(n), after=after,
                                           name=f"reduce_adamw_{n}_{piece}_l{l}")
                after = stepped[n][1]
        return after

    small_parts = [dmod, dgains, dfinal, drel.T, dsinks, loss[0, :1]]
    small_sizes = [int(np.prod(p.shape)) for p in small_parts]
    small_all, = _all_gather([_pack_rows(small_parts, F32)], name="gather_small")
    token = small_all
    for names, sg in deferred:
        state, token = _exchange_start(sg, token, name=f"exchange_ffn0_l0_{names[0]}_start")
        exchanges.setdefault(order[0], []).append((names, state))

    after_l1 = step([key for key in reversed(order) if key[0] == 1], token)
    small_sum = _sum_parts([small_all], name="sum_small").reshape(-1)
    offs = np.concatenate([[0], np.cumsum(small_sizes)])
    g_b_ada = small_sum[offs[0]:offs[1]].reshape(DEPTH, 9 * d)
    g_gain_full = small_sum[offs[1]:offs[2]].reshape(DEPTH, 3, d)
    g_norm_gain = lax.dynamic_slice_in_dim(g_gain_full, me * LANES, LANES, axis=2)
    g_final = small_sum[offs[2]:offs[3]]
    g_rel = small_sum[offs[3]:offs[4]].reshape(N_SOFT, N_BUCKETS).T
    g_sinks = small_sum[offs[4]:offs[5]].reshape(DEPTH, H_SWA_Q)
    loss_total = small_sum[offs[5]]

    dmod_all = small_all.reshape(N_DEV, -1)[:, :DEPTH * 9 * d].reshape(N_DEV, DEPTH, 9 * d)
    dmod_cols = lax.dynamic_slice_in_dim(dmod_all, me * cols, cols, axis=2)
    g_w_ada = jnp.stack([_ada_bwd(c_all.T, dmod_cols[:, l], name=f"ada_bwd_l{l}") for l in range(DEPTH)])

    small_state = {"w_ada": (w_ada, m_w_ada, v_w_ada), "b_ada": (b_ada, m_b_ada, v_b_ada),
                   "norm_gain": (norm_gain, m_norm_gain, v_norm_gain), "sinks": (sinks, m_sinks, v_sinks),
                   "rel_bias": (rel_bias, m_rel_bias, v_rel_bias), "final_gain": (final_gain, m_final_gain, v_final_gain)}
    grad, update = {}, {}
    for n, g in (("w_ada", g_w_ada), ("b_ada", g_b_ada), ("norm_gain", g_norm_gain), ("sinks", g_sinks),
                 ("rel_bias", g_rel), ("final_gain", g_final)):
        w, m, v = small_state[n]
        grad[n] = g
        if w.ndim == 1:
            update[n] = tuple(t.reshape(w.shape) for t in _adamw(_row(w), _row(g), _row(m), _row(v), name=f"adamw_{n}"))
        else:
            update[n] = _adamw(w, g, m, v, name=f"adamw_{n}")

    step([order[0]], step([order[2], order[1]], after_l1))

    def unflat(n, like, swapped):
        shape = jnp.swapaxes(like, -1, -2).shape if swapped else like.shape
        out = [t.reshape(shape) for t in stepped[n]]
        return [jnp.swapaxes(t, -1, -2) for t in out] if swapped else out

    results = {"w_ffn_gate": unflat("gate", w_ffn_gate, True), "w_ffn_up": unflat("up", w_ffn_up, True),
               "w_ffn_down": unflat("down", w_ffn_down, False), "w_in": unflat("in", w_in, True),
               "w_out": unflat("out", w_out, False)}
    br = [t.reshape(DEPTH, -1, LANES) for t in stepped["br"]]
    for k, n in enumerate(("w_br_sb", "w_br_dil", "w_br_swa")):
        results[n] = [t[:, br_off[k]:br_off[k + 1]] for t in br]
    for n, (g, dl, nm, nv) in results.items():
        grad[n], update[n] = g, (dl, nm, nv)

    names = ["w_ada", "b_ada", "norm_gain", "w_ffn_gate", "w_ffn_up", "w_ffn_down", "w_in", "w_br_sb", "w_br_dil",
             "w_br_swa", "w_out", "sinks", "rel_bias", "final_gain"]
    return (loss_total, dx[None], *[grad[n] for n in names], *[update[n][0] for n in names],
            *[update[n][1] for n in names], *[update[n][2] for n in names])
```

```python
import math

import numpy as np
import jax
import jax.numpy as jnp
from jax import lax
from jax.experimental import pallas as pl
from jax.experimental.pallas import tpu as pltpu

F32, BF16 = jnp.float32, jnp.bfloat16

SEQ, D_MODEL, D_FF, HEAD_DIM = 2048, 1024, 2816, 64
DEPTH = 2
BLK = 128
H_SB, H_DIL, H_SWA_Q, H_SWA_KV = 4, 6, 6, 2
DIL_PATTERNS = ((128, 1), (512, 4), (2048, 16))
SWA_WINDOW = 128
N_BUCKETS, MAX_REL_DIST = 32, 2048
RMS_EPS = 1e-6
D_QKV = 2560
D_GATES = 3 * D_MODEL
ADAM_LR, ADAM_B1, ADAM_B2, ADAM_EPS, ADAM_WD, ADAM_STEP = 0.001, 0.9, 0.999, 1e-08, 0.01, 10

N_DEV = 8
LANES = 128
NEG = -1e30
SB_TILE = 512
VMEM_LIMIT_BYTES = 48 * 1024 * 1024
HBM = pl.BlockSpec(memory_space=pltpu.HBM)
MESH = pl.DeviceIdType.MESH


def _tile(n, target):
    t = (min(n, target) // LANES) * LANES
    while t >= LANES:
        if n % t == 0:
            return t
        t -= LANES
    return n


def _row_tile(r, cap):
    t = (min(r, cap) // 16) * 16
    while t > 16 and r % t:
        t -= 16
    return t


def _params(semantics=None):
    return pltpu.CompilerParams(dimension_semantics=semantics, vmem_limit_bytes=VMEM_LIMIT_BYTES)


def _dot(a, b, ca, cb):
    return lax.dot_general(a, b, (((ca,), (cb,)), ((), ())), preferred_element_type=F32)


def _sigmoid(a):
    return 1.0 / (1.0 + jnp.exp(-a))


def _row(v):
    return v.reshape(1, -1)


def _all_gather(arrs, name, after=None):
    n = len(arrs)
    ins = list(arrs) + ([] if after is None else [after])

    def body(*refs):
        x_refs, out_refs = refs[:n], refs[len(ins):len(ins) + n]
        send_sems, recv_sems, local_sems = refs[len(ins) + n:]
        x, y, c = lax.axis_index("x"), lax.axis_index("y"), lax.axis_index("c")
        me, sibling = (x, y, c), (x, y, 1 - c)
        chips = [(1 - x, y), (x, 1 - y), (1 - x, 1 - y)]

        def slot(t, px, py, pc):
            return out_refs[t].at[4 * px + 2 * py + pc]

        def copy(t, k, block, to, src=None):
            return pltpu.make_async_remote_copy(
                src_ref=slot(t, *block) if src is None else src, dst_ref=slot(t, *block),
                send_sem=send_sems.at[7 * t + k], recv_sem=recv_sems.at[7 * t + k], device_id=to, device_id_type=MESH)

        mine = [pltpu.make_async_copy(x_refs[t], slot(t, *me), local_sems.at[t]) for t in range(n)]
        for cp in mine:
            cp.start()
        first = []
        for t in range(n):
            first.append(copy(t, 0, me, sibling, src=x_refs[t]))
            first += [copy(t, 1 + j, me, (*chip, c), src=x_refs[t]) for j, chip in enumerate(chips)]
        for cp in first:
            cp.start()
        passed = []
        for j, chip in enumerate(chips):
            for t in range(n):
                copy(t, 1 + j, (*chip, c), me).wait_recv()
                passed.append(copy(t, 4 + j, (*chip, c), sibling))
                passed[-1].start()
        for t in range(n):
            copy(t, 0, sibling, me).wait_recv()
        for j, chip in enumerate(chips):
            for t in range(n):
                copy(t, 4 + j, (*chip, 1 - c), me).wait_recv()
        for cp in first + passed:
            cp.wait_send()
        for cp in mine:
            cp.wait()

    return pl.pallas_call(
        body, name=name, out_shape=[jax.ShapeDtypeStruct((N_DEV,) + a.shape, a.dtype) for a in arrs],
        in_specs=[HBM] * n + [pl.BlockSpec(memory_space=pl.ANY)] * (len(ins) - n), out_specs=[HBM] * n,
        scratch_shapes=[pltpu.SemaphoreType.DMA((7 * n,)), pltpu.SemaphoreType.DMA((7 * n,)), pltpu.SemaphoreType.DMA((n,))],
    )(*ins)


def _direct_copies(x_refs, land_refs, send_sems, recv_sems, local_sems):
    x, y, c = lax.axis_index("x"), lax.axis_index("y"), lax.axis_index("c")
    me = 4 * x + 2 * y + c
    sends, recvs = [], []
    for k in range(1, N_DEV):
        px = 1 - x if (k >> 2) & 1 else x
        py = 1 - y if (k >> 1) & 1 else y
        pc = 1 - c if k & 1 else c
        peer = 4 * px + 2 * py + pc
        for t, (x_ref, land_ref) in enumerate(zip(x_refs, land_refs)):
            sem = 7 * t + k - 1
            for out, src, slot in ((sends, peer, me), (recvs, me, peer)):
                out.append(pltpu.make_async_remote_copy(
                    src_ref=x_ref.at[src], dst_ref=land_ref.at[slot], send_sem=send_sems.at[sem],
                    recv_sem=recv_sems.at[sem], device_id=(px, py, pc), device_id_type=MESH))
    own = [pltpu.make_async_copy(x_ref.at[me], land_ref.at[me], local_sems.at[t])
           for t, (x_ref, land_ref) in enumerate(zip(x_refs, land_refs))]
    return sends, recvs, own


SEM =pl.BlockSpec(memory_space=pltpu.SEMAPHORE)
ANY = pl.BlockSpec(memory_space=pl.ANY)
SIDE_EFFECT = pltpu.SideEffectType.DATAFLOW_SIDE_EFFECTING


def _exchange_start(arrs, after, *, name):
    n = len(arrs)
    lands = [lax.empty(a.shape, a.dtype) for a in arrs]
    extra = [] if after is None else [after]

    def body(*refs):
        sems = refs[2 * n + len(extra):2 * n + len(extra) + 3]
        sends, _, own = _direct_copies(refs[:n], refs[n:2 * n], *sems)
        for cp in own + sends:
            cp.start()
        refs[-1][...] = jnp.zeros_like(refs[-1])

    ops = [pltpu.with_memory_space_constraint(a, pltpu.HBM) for a in list(arrs) + lands]
    out = pl.pallas_call(
        body, name=name,
        out_shape=(pltpu.SemaphoreType.DMA((7 * n,)), pltpu.SemaphoreType.DMA((7 * n,)), pltpu.SemaphoreType.DMA((n,)),
                   *[pltpu.HBM(a.shape, a.dtype) for a in ops], jax.ShapeDtypeStruct((8, LANES), F32)),
        in_specs=[HBM] * (2 * n) + [ANY] * len(extra),
        out_specs=(SEM, SEM, SEM, *[HBM] * (2 * n), pl.BlockSpec(memory_space=pltpu.VMEM)),
        input_output_aliases={t: 3 + t for t in range(2 * n)},
        compiler_params=pltpu.CompilerParams(has_side_effects=SIDE_EFFECT),
    )(*ops, *extra)
    return (out[:3], out[3:3 + n], out[3 + n:3 + 2 * n]), out[-1]


def _exchange_wait(state, after, *, name):
    sems, arrs, lands = state
    n = len(arrs)

    def body(*refs):
        sends, recvs, own = _direct_copies(refs[:n], refs[n:2 * n], *refs[2 * n:2 * n + 3])
        for cp in own:
            cp.wait()
        for cp in sends:
            cp.wait_send()
        for cp in recvs:
            cp.wait_recv()

    out = pl.pallas_call(
        body, name=name, out_shape=tuple(pltpu.HBM(a.shape, a.dtype) for a in list(arrs) + list(lands)),
        in_specs=[HBM] * (2 * n) + [SEM, SEM, SEM, ANY], out_specs=tuple([HBM] * (2 * n)),
        input_output_aliases={t: t for t in range(2 * n)},
        compiler_params=pltpu.CompilerParams(has_side_effects=SIDE_EFFECT),
    )(*arrs, *lands, *sems, after)
    return out[n:]


def _relay_copies(x_refs, land_refs, sems_a, sems_b):
    x, y, c = lax.axis_index("x"), lax.axis_index("y"), lax.axis_index("c")
    me = 4 * x + 2 * y + c
    sibling = (x, y, 1 - c)
    chips = [(1 - x, y), (x, 1 - y), (1 - x, 1 - y)]

    def slot(px, py, pc):
        return 4 * px + 2 * py + pc

    def copy(src, land_ref, dst_slot, send_sems, recv_sems, k, to):
        return pltpu.make_async_remote_copy(src_ref=src, dst_ref=land_ref.at[dst_slot], send_sem=send_sems.at[k],
                                            recv_sem=recv_sems.at[k], device_id=to, device_id_type=MESH)

    a_send, a_recv, a_own, b_send, b_recv = [], [], [], [], []
    for t, (x_ref, land_ref) in enumerate(zip(x_refs, land_refs)):
        peers = [sibling] + [(*chip, c) for chip in chips]
        if sems_a is not None:
            for k, peer in enumerate(peers):
                a_send.append(copy(x_ref, land_ref, me, sems_a[0], sems_a[1], 4 * t + k, peer))
                a_recv.append(copy(x_ref, land_ref, slot(*peer), sems_a[0], sems_a[1], 4 * t + k, peer))
            a_own.append(pltpu.make_async_copy(x_ref, land_ref.at[me], sems_a[2].at[t]))
        if sems_b is not None:
            for j, chip in enumerate(chips):
                b_send.append(copy(land_ref.at[slot(*chip, c)], land_ref, slot(*chip, c), sems_b[0], sems_b[1], 3 * t + j, sibling))
                b_recv.append(copy(land_ref.at[slot(*chip, c)], land_ref, slot(*chip, 1 - c), sems_b[0], sems_b[1], 3 * t + j,
                                   sibling))
    return (a_send, a_recv, a_own), (b_send, b_recv)


def _relay_start(arrs, after, name):
    n = len(arrs)
    lands = [lax.empty((N_DEV,) + a.shape, a.dtype) for a in arrs]

    def body(*refs):
        (sends, _, own), _ = _relay_copies(refs[:n], refs[n:2 * n], refs[2 * n + 1:2 * n + 4], None)
        for cp in own + sends:
            cp.start()
        refs[-1][...] = jnp.zeros_like(refs[-1])

    ops = [pltpu.with_memory_space_constraint(a, pltpu.HBM) for a in list(arrs) + lands]
    out = pl.pallas_call(
        body, name=name,
        out_shape=(pltpu.SemaphoreType.DMA((4 * n,)), pltpu.SemaphoreType.DMA((4 * n,)), pltpu.SemaphoreType.DMA((n,)),
                   *[pltpu.HBM(a.shape, a.dtype) for a in ops], jax.ShapeDtypeStruct((8, LANES), F32)),
        in_specs=[HBM] * (2 * n) + [ANY],
        out_specs=(SEM, SEM, SEM, *[HBM] * (2 * n), pl.BlockSpec(memory_space=pltpu.VMEM)),
        input_output_aliases={t: 3 + t for t in range(2 * n)},
        compiler_params=pltpu.CompilerParams(has_side_effects=SIDE_EFFECT),
    )(*ops, after)
    return (out[:3], out[3:3 + n], out[3 + n:3 + 2 * n]), out[-1]


def _relay_pass(state, after, name):
    sems_a, arrs, lands = state
    n = len(arrs)

    def body(*refs):
        sems_b = refs[2 * n + 4:2 * n + 6]
        (a_send, a_recv, a_own), (b_send, _) = _relay_copies(refs[:n], refs[n:2 * n], refs[2 * n:2 * n + 3], sems_b)
        for cp in a_own:
            cp.wait()
        for cp in a_send:
            cp.wait_send()
        for cp in a_recv:
            cp.wait_recv()
        for cp in b_send:
            cp.start()
        refs[-1][...] = jnp.zeros_like(refs[-1])

    out = pl.pallas_call(
        body, name=name,
        out_shape=(pltpu.SemaphoreType.DMA((3 * n,)), pltpu.SemaphoreType.DMA((3 * n,)),
                   *[pltpu.HBM(a.shape, a.dtype) for a in list(arrs) + list(lands)], jax.ShapeDtypeStruct((8, LANES), F32)),
        in_specs=[HBM] * (2 * n) + [SEM, SEM, SEM, ANY],
        out_specs=(SEM, SEM, *[HBM] * (2 * n), pl.BlockSpec(memory_space=pltpu.VMEM)),
        input_output_aliases={t: 2 + t for t in range(2 * n)},
        compiler_params=pltpu.CompilerParams(has_side_effects=SIDE_EFFECT),
    )(*arrs, *lands, *sems_a, after)
    return (out[:2], out[2:2 + n], out[2 + n:2 + 2 * n]), out[-1]


def _relay_wait(state, after, name):
    sems_b, arrs, lands = state
    n = len(arrs)

    def body(*refs):
        _, (b_send, b_recv) = _relay_copies(refs[:n], refs[n:2 * n], None, refs[2 * n:2 * n + 2])
        for cp in b_send:
            cp.wait_send()
        for cp in b_recv:
            cp.wait_recv()

    out = pl.pallas_call(
        body, name=name, out_shape=tuple(pltpu.HBM(a.shape, a.dtype) for a in list(arrs) + list(lands)),
        in_specs=[HBM] * (2 * n) + [SEM, SEM, ANY], out_specs=tuple([HBM] * (2 * n)),
        input_output_aliases={t: t for t in range(2 * n)},
        compiler_params=pltpu.CompilerParams(has_side_effects=SIDE_EFFECT),
    )(*arrs, *lands, *sems_b, after)
    return out[n:]


def _sum_parts(parts, name, after=None):
    n, r, cdim = parts.shape
    tr = _row_tile(r, max(16, (1 << 21) // (n * cdim * parts.dtype.itemsize)))

    def body(p_ref, *rest):
        acc = p_ref[0].astype(F32)
        for k in range(1, n):
            acc = acc + p_ref[k].astype(F32)
        rest[-1][...] = acc

    ins = [parts] + ([] if after is None else [after])
    return pl.pallas_call(
        body, name=name, out_shape=jax.ShapeDtypeStruct((r, cdim), F32), grid=(r // tr,),
        in_specs=[pl.BlockSpec((n, tr, cdim), lambda i: (0, i, 0))] + [ANY] * (len(ins) - 1),
        out_specs=pl.BlockSpec((tr, cdim), lambda i: (i, 0)), compiler_params=_params(("parallel",)),
    )(*ins)


def _mm_tn(a, b, *, name, after=None, tm=512, tn=1024, out_rows=None, row0=0, prev=None):
    k, m = a.shape
    n = b.shape[1]
    tm, tn = _tile(m, tm), _tile(n, tn)
    out_rows = m if out_rows is None else out_rows

    def body(a_ref, b_ref, *rest):
        o_ref, at_ref = rest[-2], rest[-1]

        @pl.when(pl.program_id(1) == 0)
        def _():
            at_ref[...] = a_ref[...].astype(BF16).T

        o_ref[...] = _dot(at_ref[...], b_ref[...].astype(BF16), 1, 0).astype(BF16)

    ins = [a, b] + [t for t in (after, prev) if t is not None]
    return pl.pallas_call(
        body, name=name, out_shape=jax.ShapeDtypeStruct((out_rows, n), BF16), grid=(m // tm, n // tn),
        in_specs=[pl.BlockSpec((k, tm), lambda i, j: (0, i)), pl.BlockSpec((k, tn), lambda i, j: (0, j))] + [ANY] * (len(ins) - 2),
        out_specs=pl.BlockSpec((tm, tn), lambda i, j: (row0 // tm + i, j)),
        input_output_aliases={} if prev is None else {len(ins) - 1: 0},
        scratch_shapes=[pltpu.VMEM((tm, k), BF16)], compiler_params=_params(("parallel", "arbitrary")),
    )(*ins)


def _mm(a, b, *, name, ta=False, tb=False, res=None, colscale=None, emit_acc=False,
        out_dtype=F32, tm=512, tn=512, b_rows=None):
    m, k = (a.shape[1], a.shape[0]) if ta else a.shape
    n = b.shape[0] if tb else b.shape[1]
    b_start = 0
    if b_rows is not None:
        b_start, n = b_rows
    tm, tn = _tile(m, tm), _tile(n, tn)
    ca, cb = (0 if ta else 1), (1 if tb else 0)
    a_spec = pl.BlockSpec((k, tm), lambda i, j: (0, i)) if ta else pl.BlockSpec((tm, k), lambda i, j: (i, 0))
    b_spec = (pl.BlockSpec((tn, k), lambda i, j: (b_start // tn + j, 0)) if tb
              else pl.BlockSpec((k, tn), lambda i, j: (0, j)))
    tile = pl.BlockSpec((tm, tn), lambda i, j: (i, j))
    ins, in_specs = [a, b], [a_spec, b_spec]
    if res is not None:
        ins.append(res)
        in_specs.append(tile)
    if colscale is not None:
        ins.append(colscale)
        in_specs.append(pl.BlockSpec((1, tn), lambda i, j: (0, j)))
    n_in = len(ins)

    def body(*refs):
        outs = refs[n_in:]
        acc = _dot(refs[0][...].astype(BF16), refs[1][...].astype(BF16), ca, cb)
        val, p = acc, 2
        if res is not None:
            r_val, p = refs[p][...], p + 1
        if colscale is not None:
            val = val * refs[p][...]
        if res is not None:
            val = r_val + val
        if emit_acc:
            outs[0][...] = acc
        outs[-1][...] = val.astype(out_dtype)

    out_shape = [jax.ShapeDtypeStruct((m, n), out_dtype)]
    out_specs = [tile]
    if emit_acc:
        out_shape.insert(0, jax.ShapeDtypeStruct((m, n), F32))
        out_specs.insert(0, tile)
    out = pl.pallas_call(
        body, name=name, out_shape=out_shape, grid=(m // tm, n // tn), in_specs=in_specs, out_specs=out_specs,
        compiler_params=_params(("parallel", "parallel")),
    )(*ins)
    return out if emit_acc else out[0]


def _norm_fwd(x, g, scale, shift, name, after=None):
    s, d = x.shape
    tr = 256

    def body(x_ref, g_ref, sc_ref, sh_ref, *rest):
        xv = x_ref[...]
        rstd = lax.rsqrt(jnp.mean(xv * xv, axis=-1, keepdims=True) + RMS_EPS)
        rest[-1][...] = (xv * rstd * g_ref[...] * (1.0 + sc_ref[...]) + sh_ref[...]).astype(BF16)

    rowspec = pl.BlockSpec((1, d), lambda i: (0, 0))
    ins = [x, g, scale, shift] + ([] if after is None else [after])
    return pl.pallas_call(
        body, name=name, out_shape=jax.ShapeDtypeStruct((s, d), BF16), grid=(s // tr,),
        in_specs=[pl.BlockSpec((tr, d), lambda i: (i, 0)), rowspec, rowspec, rowspec] + [ANY] * (len(ins) - 4),
        out_specs=pl.BlockSpec((tr, d), lambda i: (i, 0)),
        compiler_params=_params(("parallel",)),
    )(*ins)


def _dh_norm_bwd(a1, b1, a2, b2, x, dres, g, scale, nxt, *, name, after=None, b_rows=None):
    s, d = x.shape
    tm = 256
    n_fixed = 8

    def body(a1_ref, b1_ref, a2_ref, b2_ref, x_ref, dr_ref, g_ref, sc_ref, *rest):
        rest = rest[(1 if after is not None else 0):]
        if nxt is not None:
            f_ref, cs_ref, dx_ref, sa_ref, sb_ref, df_ref, dg_ref = rest
        else:
            dx_ref, sa_ref, sb_ref = rest

        @pl.when(pl.program_id(0) == 0)
        def _():
            sa_ref[...] = jnp.zeros_like(sa_ref)
            sb_ref[...] = jnp.zeros_like(sb_ref)
            if nxt is not None:
                dg_ref[...] = jnp.zeros_like(dg_ref)

        dhv = (_dot(a1_ref[...].astype(BF16), b1_ref[...], 1, 0) + _dot(a2_ref[...].astype(BF16), b2_ref[...], 1, 0))
        xv = x_ref[...]
        rstd = lax.rsqrt(jnp.mean(xv * xv, axis=-1, keepdims=True) + RMS_EPS)
        xhat = xv * rstd
        dxhat = dhv * (g_ref[...] * (1.0 + sc_ref[...]))
        mean_term = jnp.mean(dxhat * xhat, axis=-1, keepdims=True)
        dxv = dr_ref[...] + rstd * (dxhat - xhat * mean_term)
        dx_ref[...] = dxv
        sa_ref[...] += jnp.sum(dhv, axis=0, keepdims=True)
        sb_ref[...] += jnp.sum(dhv * xhat, axis=0, keepdims=True)
        if nxt is not None:
            df_ref[...] = (dxv * cs_ref[...]).astype(BF16)
            dg_ref[...] += nxt[2] * jnp.sum(dxv * f_ref[...], axis=0, keepdims=True)

    def a_spec(t):
        return pl.BlockSpec((tm, t.shape[1]), lambda i: (i, 0))

    def b_spec(t, a, which):
        if b_rows is None:
            return pl.BlockSpec((t.shape[0], d), lambda i: (0, 0))
        start = b_rows[which]
        return pl.BlockSpec((pl.Element(a.shape[1]), pl.Element(d)), lambda i: (start, 0))

    rowspec = pl.BlockSpec((1, d), lambda i: (0, 0))
    tile = pl.BlockSpec((tm, d), lambda i: (i, 0))
    ins = [a1, b1, a2, b2, x, dres, g, scale] + ([] if after is None else [after])
    in_specs = [a_spec(a1), b_spec(b1, a1, 0), a_spec(a2), b_spec(b2, a2, 1), tile, tile, rowspec, rowspec]
    in_specs += [ANY] * (len(ins) - n_fixed)
    out_shape = [jax.ShapeDtypeStruct((s, d), F32), jax.ShapeDtypeStruct((1, d), F32), jax.ShapeDtypeStruct((1, d), F32)]
    out_specs = [tile, rowspec, rowspec]
    if nxt is not None:
        ins += [nxt[0], nxt[1]]
        in_specs += [tile, rowspec]
        out_shape += [jax.ShapeDtypeStruct((s, d), BF16), jax.ShapeDtypeStruct((1, d), F32)]
        out_specs += [tile, rowspec]
    out = pl.pallas_call(
        body, name=name, out_shape=out_shape, grid=(s // tm,), in_specs=in_specs, out_specs=out_specs,
        compiler_params=_params(("arbitrary",)),
    )(*ins)
    return out[0], out[1], out[2], (None if nxt is None else (out[3], out[4]))


def _gate_bwd(dxn, f, colscale, coef, name):
    s, d = dxn.shape
    tr = 256

    def body(dx_ref, f_ref, cs_ref, df_ref, dg_ref):
        @pl.when(pl.program_id(0) == 0)
        def _():
            dg_ref[...] = jnp.zeros_like(dg_ref)

        dxv = dx_ref[...]
        df_ref[...] = (dxv * cs_ref[...]).astype(BF16)
        dg_ref[...] += coef * jnp.sum(dxv * f_ref[...], axis=0, keepdims=True)

    rowspec = pl.BlockSpec((1, d), lambda i: (0, 0))
    tile = pl.BlockSpec((tr, d), lambda i: (i, 0))
    return pl.pallas_call(
        body, name=name, out_shape=[jax.ShapeDtypeStruct((s, d), BF16), jax.ShapeDtypeStruct((1, d), F32)],
        grid=(s // tr,), in_specs=[tile, tile, rowspec], out_specs=[tile, rowspec],
        compiler_params=_params(("arbitrary",)),
    )(dxn, f, colscale)


def _ffn_up(h, wg, wu, name, tm=SEQ, tn=256):
    s, d = h.shape
    f = wg.shape[0]

    def body(h_ref, wg_ref, wu_ref, a_ref, u_ref, s_ref):
        hv = h_ref[...]
        a = _dot(hv, wg_ref[...], 1, 1)
        u = _dot(hv, wu_ref[...], 1, 1)
        a_ref[...] = a.astype(BF16)
        u_ref[...] = u.astype(BF16)
        s_ref[...] = (a * _sigmoid(a) * u).astype(BF16)

    tile = pl.BlockSpec((tm, tn), lambda i, j: (i, j))
    wspec = pl.BlockSpec((tn, d), lambda i, j: (j, 0))
    return pl.pallas_call(
        body, name=name,
        out_shape=[jax.ShapeDtypeStruct((s, f), BF16), jax.ShapeDtypeStruct((s, f), BF16), jax.ShapeDtypeStruct((s, f), BF16)],
        grid=(s // tm, f // tn), in_specs=[pl.BlockSpec((tm, d), lambda i, j: (i, 0)), wspec, wspec],
        out_specs=[tile, tile, tile], compiler_params=_params(("parallel", "parallel")),
    )(h, wg, wu)


def _ffn_bwd_ds(df, wd, a, u, name, tm=SEQ, tn=256):
    s, d = df.shape
    f = wd.shape[0]

    def body(df_ref, wd_ref, a_ref, u_ref, da_ref, du_ref):
        ds = _dot(df_ref[...], wd_ref[...], 1, 1)
        av = a_ref[...].astype(F32)
        sg = _sigmoid(av)
        da_ref[...] = (ds * u_ref[...].astype(F32) * (sg * (1.0 + av * (1.0 - sg)))).astype(BF16)
        du_ref[...] = (ds * (av * sg)).astype(BF16)

    tile = pl.BlockSpec((tm, tn), lambda i, j: (i, j))
    return pl.pallas_call(
        body, name=name, out_shape=[jax.ShapeDtypeStruct((s, f), BF16), jax.ShapeDtypeStruct((s, f), BF16)],
        grid=(s // tm, f // tn),
        in_specs=[pl.BlockSpec((tm, d), lambda i, j: (i, 0)), pl.BlockSpec((tn, d), lambda i, j: (j, 0)), tile, tile],
        out_specs=[tile, tile], compiler_params=_params(("parallel", "parallel")),
    )(df, wd, a, u)


def _merge_fwd(o_sb, o_dil, o_swa, gates, wb_sb, wb_dil, wb_swa, name):
    s, d = SEQ, D_MODEL
    tm = 256

    def body(osb_ref, odl_ref, osw_ref, g_ref, wsb_ref, wdl_ref, wsw_ref, m_ref, tsb_ref, tdl_ref, tsw_ref):
        for h in range(osb_ref.shape[0]):
            tsb_ref[:, h * HEAD_DIM:(h + 1) * HEAD_DIM] = osb_ref[h].astype(BF16)
        for h in range(osw_ref.shape[0]):
            tsw_ref[:, h * HEAD_DIM:(h + 1) * HEAD_DIM] = osw_ref[h].astype(BF16)
        tdl_ref[...] = odl_ref[...].astype(BF16)
        acc = _sigmoid(g_ref[:, 0:d]) * _dot(tsb_ref[...], wsb_ref[...], 1, 0)
        acc += _sigmoid(g_ref[:, d:2 * d]) * _dot(tdl_ref[...], wdl_ref[...], 1, 0)
        acc += _sigmoid(g_ref[:, 2 * d:3 * d]) * _dot(tsw_ref[...], wsw_ref[...], 1, 0)
        m_ref[...] = acc.astype(BF16)

    def rows(w):
        return pl.BlockSpec((tm, w), lambda i: (i, 0))

    def heads(n):
        return pl.BlockSpec((n, tm, HEAD_DIM), lambda i: (0, i, 0))

    def whole(w):
        return pl.BlockSpec((w, d), lambda i: (0, 0))

    return pl.pallas_call(
        body, name=name, out_shape=[jax.ShapeDtypeStruct((s, w), BF16) for w in (d, 256, 128, 384)], grid=(s // tm,),
        in_specs=[heads(H_SB), rows(128), heads(H_SWA_Q), rows(3 * d), whole(256), whole(128), whole(384)],
        out_specs=[rows(d), rows(256), rows(128), rows(384)], compiler_params=_params(("parallel",)),
    )(o_sb, o_dil, o_swa, gates, wb_sb, wb_dil, wb_swa)


def _merge_bwd(dmerged, t_sb, t_dil, t_swa, gates, wb_sb, wb_dil, wb_swa, name):
    s, d = SEQ, D_MODEL
    tm = 256

    def body(dm_ref, tsb_ref, tdl_ref, tsw_ref, g_ref, wsb_ref, wdl_ref, wsw_ref,
             dg_ref, dosb_ref, dodl_ref, dosw_ref, dbsb_ref, dbdl_ref, dbsw_ref):
        dm = dm_ref[...]
        for idx, (t_ref, w_ref, do_ref, db_ref) in enumerate((
                (tsb_ref, wsb_ref, dosb_ref, dbsb_ref), (tdl_ref, wdl_ref, dodl_ref, dbdl_ref),
                (tsw_ref, wsw_ref, dosw_ref, dbsw_ref))):
            w = w_ref[...]
            br = _dot(t_ref[...], w, 1, 0)
            sg = _sigmoid(g_ref[:, idx * d:(idx + 1) * d])
            dbr = (dm * sg).astype(BF16)
            dg_ref[:, idx * d:(idx + 1) * d] = (dm * br * (sg * (1.0 - sg))).astype(BF16)
            db_ref[...] = dbr
            do = _dot(dbr, w, 1, 1)
            if len(do_ref.shape) == 2:
                do_ref[...] = do
            else:
                for h in range(do_ref.shape[0]):
                    do_ref[h] = do[:, h * HEAD_DIM:(h + 1) * HEAD_DIM]

    def rows(w):
        return pl.BlockSpec((tm, w), lambda i: (i, 0))

    def heads(n):
        return pl.BlockSpec((n, tm, HEAD_DIM), lambda i: (0, i, 0))

    def whole(w):
        return pl.BlockSpec((w, d), lambda i: (0, 0))

    def shp(w, dt):
        return jax.ShapeDtypeStruct((s, w), dt)

    def hshp(n):
        return jax.ShapeDtypeStruct((n, s, HEAD_DIM), F32)

    return pl.pallas_call(
        body, name=name,
        out_shape=[shp(3 * d, BF16), hshp(H_SB), shp(128, F32), hshp(H_SWA_Q), shp(d, BF16), shp(d, BF16), shp(d, BF16)],
        grid=(s // tm,),
        in_specs=[rows(d), rows(256), rows(128), rows(384), rows(3 * d), whole(256), whole(128), whole(384)],
        out_specs=[rows(3 * d), heads(H_SB), rows(128), heads(H_SWA_Q), rows(d), rows(d), rows(d)],
        compiler_params=_params(("parallel",)),
    )(dmerged, t_sb, t_dil, t_swa, gates, wb_sb, wb_dil, wb_swa)


def _final_loss(x, target, g, name):
    s, d = x.shape
    tr = 256

    def body(x_ref, t_ref, g_ref, loss_ref, dx_ref, dg_ref):
        @pl.when(pl.program_id(0) == 0)
        def _():
            loss_ref[...] = jnp.zeros_like(loss_ref)
            dg_ref[...] = jnp.zeros_like(dg_ref)

        xv = x_ref[...]
        gv = g_ref[...]
        rstd = lax.rsqrt(jnp.mean(xv * xv, axis=-1, keepdims=True) + RMS_EPS)
        xhat = xv * rstd
        err = xhat * gv - t_ref[...]
        loss_ref[...] += 0.5 * jnp.sum(jnp.mean(err * err, axis=-1, keepdims=True))
        dy = err * (1.0 / d)
        dxhat = dy * gv
        mean_term = jnp.mean(dxhat * xhat, axis=-1, keepdims=True)
        dx_ref[...] = rstd * (dxhat - xhat * mean_term)
        dg_ref[...] += jnp.sum(dy * xhat, axis=0, keepdims=True)

    rowspec = pl.BlockSpec((1, d), lambda i: (0, 0))
    tile = pl.BlockSpec((tr, d), lambda i: (i, 0))
    return pl.pallas_call(
        body, name=name,
        out_shape=[jax.ShapeDtypeStruct((1, LANES), F32), jax.ShapeDtypeStruct((s, d), F32), jax.ShapeDtypeStruct((1, d), F32)],
        grid=(s // tr,), in_specs=[tile, tile, rowspec],
        out_specs=[pl.BlockSpec((1, LANES), lambda i: (0, 0)), tile, rowspec],
        compiler_params=_params(("arbitrary",)),
    )(x, target, g)


def _adamw(w, g, m, v, name, after=None):
    shape = w.shape
    cols = shape[-1]
    rows = int(np.prod(shape[:-1])) if len(shape) > 1 else 1
    tr = rows
    for cand in (1024, 512, 256, 128, 64, 32, 16, 8):
        if rows % cand == 0 and rows > cand and cand * cols * 4 <= (1 << 21):
            tr = cand
            break

    def body(w_ref, g_ref, m_ref, v_ref, *rest):
        d_ref, nm_ref, nv_ref = rest[-3:]
        d_ref[...], nm_ref[...], nv_ref[...] = _adam_update(w_ref[...], g_ref[...], m_ref[...], v_ref[...])

    tile = pl.BlockSpec((tr, cols), lambda i: (i, 0))
    flat = [t.reshape(rows, cols) for t in (w, g, m, v)] + ([] if after is None else [after])
    out = pl.pallas_call(
        body, name=name, out_shape=[jax.ShapeDtypeStruct((rows, cols), F32)] * 3, grid=(rows // tr,),
        in_specs=[tile] * 4 + [ANY] * (len(flat) - 4), out_specs=[tile] * 3, compiler_params=_params(("parallel",)),
    )(*flat)
    return tuple(t.reshape(shape) for t in out)


def _adam_update(w, gv, m, v):
    nm = ADAM_B1 * m + (1.0 - ADAM_B1) * gv
    nv = ADAM_B2 * v + (1.0 - ADAM_B2) * (gv * gv)
    m_hat = nm / (1.0 - ADAM_B1 ** ADAM_STEP)
    v_hat = nv / (1.0 - ADAM_B2 ** ADAM_STEP)
    return -ADAM_LR * (m_hat / (jnp.sqrt(v_hat) + ADAM_EPS) + ADAM_WD * w), nm, nv


def _reduce_adamw(groups, w, m, v, row0, prev, name, after=None):
    n, r, cdim = groups[0].shape
    rows = w.shape[0]
    tr = _row_tile(r, max(16, (1 << 22) // (n * cdim * groups[0].dtype.itemsize)))
    steps = r // tr
    ng = len(groups)

    def body(*refs):
        w_ref, m_ref, v_ref = refs[ng:ng + 3]
        g_out, d_out, m_out, v_out = refs[-4:]
        gg = pl.program_id(0)
        for gi in range(ng):
            @pl.when(gg == gi)
            def _(gi=gi):
                acc = refs[gi][0].astype(F32)
                for k in range(1, n):
                    acc = acc + refs[gi][k].astype(F32)
                g_out[...] = acc
                d_out[...], m_out[...], v_out[...] = _adam_update(w_ref[...], acc, m_ref[...], v_ref[...])

    def part_spec(gi):
        return pl.BlockSpec((n, tr, cdim), lambda gg, i: (0, jnp.where(gg == gi, i, 0), 0))

    tile = pl.BlockSpec((tr, cdim), lambda gg, i: (row0 // tr + gg * steps + i, 0))
    extra = ([] if prev is None else list(prev)) + ([] if after is None else [after])
    return pl.pallas_call(
        body, name=name, out_shape=[jax.ShapeDtypeStruct((rows, cdim), F32)] * 4, grid=(ng, steps),
        in_specs=[part_spec(gi) for gi in range(ng)] + [tile] * 3 + [ANY] * len(extra), out_specs=[tile] * 4,
        input_output_aliases={} if prev is None else {ng + 3 + k: k for k in range(4)},
        compiler_params=_params(("parallel", "parallel")),
    )(*groups, w, m, v, *extra)


def _ada_fwd(c_all, w, name):
    n = w.shape[1]

    def body(c_ref, w_ref, o_ref):
        cv = c_ref[...]
        o_ref[...] = jnp.dot(cv * _sigmoid(cv), w_ref[...], preferred_element_type=F32, precision=lax.Precision.HIGHEST)

    return pl.pallas_call(body, name=name, out_shape=jax.ShapeDtypeStruct((N_DEV, n), F32), compiler_params=_params())(c_all, w)


def _ada_bwd(c_all_t, dmod, name):
    n = dmod.shape[1]

    def body(c_ref, d_ref, o_ref):
        cv = c_ref[...]
        o_ref[...] = jnp.dot(cv * _sigmoid(cv), d_ref[...], preferred_element_type=F32, precision=lax.Precision.HIGHEST)

    return pl.pallas_call(body, name=name, out_shape=jax.ShapeDtypeStruct((D_MODEL, n), F32), compiler_params=_params())(c_all_t, dmod)


def _bucket_tables():
    rel = np.arange(BLK)[:, None] + BLK - np.arange(2 * BLK)[None, :]
    max_exact = N_BUCKETS // 2

    def bucket(n):
        nf = np.maximum(n, 1).astype(np.float32)
        large = max_exact + (np.log(nf / np.float32(max_exact)) / np.float32(math.log(MAX_REL_DIST / max_exact))
                             * np.float32(N_BUCKETS - max_exact)).astype(np.int32)
        return np.where(n < max_exact, n, np.minimum(large, N_BUCKETS - 1))

    tabs = []
    for dil, max_dist in ((1, 128), (4, 128), (16, 128), (1, SWA_WINDOW - 1)):
        in_band = (rel >= 0) & (rel <= max_dist)
        tabs.append(np.where(in_band, bucket(np.maximum(rel, 0) * dil), -1))
    return np.stack(tabs).astype(np.int32)


N_SOFT = H_DIL + H_SWA_Q


def _table_of_head(h):
    return jnp.minimum(h // 2, 3)


def _bias_build(rel_bias, tables, name):
    def body(rel_ref, t_ref, o_ref):
        h = pl.program_id(0)
        tb = t_ref[0]
        out = jnp.full((BLK, 2 * BLK), NEG, F32)
        for b in range(N_BUCKETS):
            out = jnp.where(tb == b, rel_ref[b, h], out)
        o_ref[0] = out

    return pl.pallas_call(
        body, name=name, out_shape=jax.ShapeDtypeStruct((N_SOFT, BLK, 2 * BLK), F32), grid=(N_SOFT,),
        in_specs=[pl.BlockSpec(memory_space=pltpu.SMEM),
                  pl.BlockSpec((1, BLK, 2 * BLK), lambda h: (_table_of_head(h), 0, 0))],
        out_specs=pl.BlockSpec((1, BLK, 2 * BLK), lambda h: (h, 0, 0)),
        compiler_params=_params(("parallel",)),
    )(rel_bias, tables)


def _bias_grad(dbias, tables, name):
    def body(d_ref, t_ref, o_ref):
        tb = t_ref[0]
        dv = d_ref[0]
        lane = lax.broadcasted_iota(jnp.int32, (1, LANES), 1)
        out = jnp.zeros((1, LANES), F32)
        for b in range(N_BUCKETS):
            out = jnp.where(lane == b, jnp.sum(jnp.where(tb == b, dv, 0.0)), out)
        o_ref[0] = out

    return pl.pallas_call(
        body, name=name, out_shape=jax.ShapeDtypeStruct((N_SOFT, 1, LANES), F32), grid=(N_SOFT,),
        in_specs=[pl.BlockSpec((1, BLK, 2 * BLK), lambda h: (h, 0, 0)),
                  pl.BlockSpec((1, BLK, 2 * BLK), lambda h: (_table_of_head(h), 0, 0))],
        out_specs=pl.BlockSpec((1, 1, LANES), lambda h: (h, 0, 0)),
        compiler_params=_params(("parallel",)),
    )(dbias, tables)


def _band_layout(g, bias_div):
    assert g == 1 or bias_div == 1
    return bias_div if g == 1 else 1


def _band_specs(length, g, bias_div, offs):
    ns = _band_layout(g, bias_div)

    def seqs(off, div=1):
        return pl.BlockSpec((ns, length, HEAD_DIM), lambda s: (off // ns + s // div, 0, 0))

    xspecs = [seqs(offs[0]), seqs(offs[1], g), seqs(offs[2], g)]
    bspec = pl.BlockSpec((1, BLK, 2 * BLK), lambda s: (s, 0, 0))
    sspec = pl.BlockSpec((ns, 1, LANES), lambda s: (s, 0, 0))
    colspec = pl.BlockSpec((ns, length, 1), lambda s: (s, 0, 0))
    return xspecs, seqs(0), seqs(0, g), bspec, sspec, colspec


def _band_sweep(length, ns, one):
    nblk = length // BLK
    for qq in range(ns):
        if ns * nblk <= 16:
            for i in range(nblk):
                one(qq, i * BLK, max(i - 1, 0) * BLK, i == 0)
        else:
            def step(i, carry, qq=qq):
                one(qq, pl.multiple_of(i * BLK, BLK), pl.multiple_of(jnp.maximum(i - 1, 0) * BLK, BLK), i == 0)
                return carry

            lax.fori_loop(0, nblk, step, 0, unroll=2)


def _band_scores(q_ref, k_ref, b_ref, qq, kq, bq, cur, prv, first):
    qv = q_ref[qq, pl.ds(cur, BLK), :]
    bv = b_ref[bq]
    if first is True:
        sp = jnp.full((BLK, BLK), NEG, F32)
    else:
        sp = _dot(qv, k_ref[kq, pl.ds(prv, BLK), :], 1, 1) + bv[:, :BLK]
        sp = sp if first is False else jnp.where(first, NEG, sp)
    sc = _dot(qv, k_ref[kq, pl.ds(cur, BLK), :], 1, 1) + bv[:, BLK:]
    return qv, sp, sc


def _band_fwd(x, bias, sink, *, nq, offs, g, bias_div, has_sink, name):
    length = x.shape[1]
    ns = _band_layout(g, bias_div)

    def body(q_ref, k_ref, v_ref, b_ref, s_ref, o_ref, lse_ref):
        def one(qq, cur, prv, first):
            kq, bq = qq, 0
            _, sp, sc = _band_scores(q_ref, k_ref, b_ref, qq, kq, bq, cur, prv, first)
            m = jnp.maximum(jnp.max(sp, axis=1, keepdims=True), jnp.max(sc, axis=1, keepdims=True))
            if has_sink:
                sk = s_ref[qq][:, :1]
                m = jnp.maximum(m, sk)
            pp, pc = jnp.exp(sp - m), jnp.exp(sc - m)
            den = jnp.sum(pp, axis=1, keepdims=True) + jnp.sum(pc, axis=1, keepdims=True)
            if has_sink:
                den = den + jnp.exp(sk - m)
            acc = (_dot(pp.astype(BF16), v_ref[kq, pl.ds(prv, BLK), :], 1, 0)
                   + _dot(pc.astype(BF16), v_ref[kq, pl.ds(cur, BLK), :], 1, 0))
            o_ref[qq, pl.ds(cur, BLK), :] = acc / den
            lse_ref[qq, pl.ds(cur, BLK), :] = m + jnp.log(den)

        _band_sweep(length, ns, one)

    xspecs, qspec, _, bspec, sspec, colspec = _band_specs(length, g, bias_div, offs)
    return pl.pallas_call(
        body, name=name,
        out_shape=[jax.ShapeDtypeStruct((nq, length, HEAD_DIM), F32), jax.ShapeDtypeStruct((nq, length, 1), F32)],
        grid=(nq // ns,), in_specs=xspecs + [bspec, sspec],
        out_specs=[qspec, colspec], compiler_params=_params(("parallel",)),
    )(x, x, x, bias, sink)


def _band_bwd(x, bias, sink, o, lse, do, dlse, *, nq, offs, g, bias_div, has_sink, name):
    length = x.shape[1]
    ns = _band_layout(g, bias_div)
    nk, nbias = nq // g, nq // bias_div

    def body(q_ref, k_ref, v_ref, b_ref, s_ref, o_ref, lse_ref, do_ref, dlse_ref,
             dq_ref, dk_ref, dv_ref, db_ref, dsk_ref, dkp_ref, dvp_ref):
        for ref in (db_ref, dsk_ref, dkp_ref, dvp_ref):
            ref[...] = jnp.zeros_like(ref)

        @pl.when(pl.program_id(0) % g == 0)
        def _():
            dk_ref[...] = jnp.zeros_like(dk_ref)
            dv_ref[...] = jnp.zeros_like(dv_ref)

        def one(qq, cur, prv, first):
            kq, bq = qq, 0
            qv, sp, sc = _band_scores(q_ref, k_ref, b_ref, qq, kq, bq, cur, prv, first)
            rows, prow = pl.ds(cur, BLK), pl.ds(prv, BLK)
            lse_v = lse_ref[qq, rows, :]
            pp, pc = jnp.exp(sp - lse_v), jnp.exp(sc - lse_v)
            dov = do_ref[qq, rows, :]
            dob = dov.astype(BF16)
            coef = dlse_ref[qq, rows, :] - jnp.sum(dov * o_ref[qq, rows, :], axis=1, keepdims=True)
            dsp = pp * (_dot(dob, v_ref[kq, prow, :], 1, 1) + coef)
            dsc = pc * (_dot(dob, v_ref[kq, rows, :], 1, 1) + coef)
            dspb, dscb = dsp.astype(BF16), dsc.astype(BF16)
            dq_ref[qq, rows, :] = ((_dot(dspb, k_ref[kq, prow, :], 1, 0) + _dot(dscb, k_ref[kq, rows, :], 1, 0))
                                   * (HEAD_DIM ** -0.5))
            dk_ref[kq, rows, :] += _dot(dscb, qv, 0, 0)
            dkp_ref[kq, prow, :] += _dot(dspb, qv, 0, 0)
            dv_ref[kq, rows, :] += _dot(pc.astype(BF16), dob, 0, 0)
            dvp_ref[kq, prow, :] += _dot(pp.astype(BF16), dob, 0, 0)
            db_ref[bq, :, :BLK] += dsp
            db_ref[bq, :, BLK:] += dsc
            if has_sink:
                dsk_ref[qq] += jnp.sum(jnp.exp(s_ref[qq][:, :1] - lse_v) * coef)

        _band_sweep(length, ns, one)
        dk_ref[...] += dkp_ref[...]
        dv_ref[...] += dvp_ref[...]

    xspecs, qspec, kvspec, bspec, sspec, colspec = _band_specs(length, g, bias_div, offs)
    return pl.pallas_call(
        body, name=name,
        out_shape=[jax.ShapeDtypeStruct((nq, length, HEAD_DIM), F32), jax.ShapeDtypeStruct((nk, length, HEAD_DIM), F32),
                   jax.ShapeDtypeStruct((nk, length, HEAD_DIM), F32), jax.ShapeDtypeStruct((nbias, BLK, 2 * BLK), F32),
                   jax.ShapeDtypeStruct((nq, 1, LANES), F32)],
        grid=(nq // ns,),
        in_specs=xspecs + [bspec, sspec, qspec, colspec, qspec, colspec],
        out_specs=[qspec, kvspec, kvspec, bspec, sspec],
        scratch_shapes=[pltpu.VMEM((ns, length, HEAD_DIM), F32), pltpu.VMEM((ns, length, HEAD_DIM), F32)],
        compiler_params=_params(("arbitrary",)),
    )(x, x, x, bias, sink, o, lse, do, dlse)


TOK_TILE = 512


def _dil_merge(outs, lses, dout, name):
    tr = TOK_TILE
    dils = [d for _, d in DIL_PATTERNS]
    n = len(dils)
    o4 = [o.reshape(2, d, SEQ // d, HEAD_DIM) for o, d in zip(outs, dils)]
    l4 = [l.reshape(2, d, SEQ // d, 1) for l, d in zip(lses, dils)]
    o_specs = [pl.BlockSpec((2, d, tr // d, HEAD_DIM), lambda i: (0, 0, i, 0)) for d in dils]
    l_specs = [pl.BlockSpec((2, d, tr // d, 1), lambda i: (0, 0, i, 0)) for d in dils]
    tok = pl.BlockSpec((tr, 2 * HEAD_DIM), lambda i: (i, 0))
    scratch = ([pltpu.VMEM((tr, 2 * HEAD_DIM), F32) for _ in dils] + [pltpu.VMEM((tr, 1), F32) for _ in range(2 * n)]
               + [pltpu.VMEM((tr // d, 2 * HEAD_DIM), F32) for d in dils])

    def to_tokens(o_ref, l_ref, d, pair, cols, stage):
        for r in range(d):
            rows = pl.ds(r, tr // d, stride=d) if d > 1 else slice(None)
            stage[:, :HEAD_DIM] = o_ref[0, r]
            stage[:, HEAD_DIM:] = o_ref[1, r]
            pair[rows, :] = stage[...]
            for h in range(2):
                cols[h][rows, :] = l_ref[h, r]
        return pair[...], [cols[0][...], cols[1][...]]

    def weights(ls):
        left = lax.broadcasted_iota(jnp.int32, (tr, 2 * HEAD_DIM), 1) < HEAD_DIM
        per_head = []
        for h in range(2):
            m = ls[0][h]
            for g in range(1, n):
                m = jnp.maximum(m, ls[g][h])
            es = [jnp.exp(ls[g][h] - m) for g in range(n)]
            den = es[0]
            for e in es[1:]:
                den = den + e
            per_head.append([e / den for e in es])
        return per_head, [jnp.where(left, per_head[0][g], per_head[1][g]) for g in range(n)], left

    def load(refs):
        pairs, cols, stages = refs[:n], refs[n:3 * n], refs[3 * n:]
        return pairs, [cols[2 * g:2 * g + 2] for g in range(n)], stages

    if dout is None:
        def body(*refs):
            pairs, cols, stages = load(refs[2 * n + 1:])
            toks = [to_tokens(refs[g], refs[n + g], dils[g], pairs[g], cols[g], stages[g]) for g in range(n)]
            _, alphas, _ = weights([t[1] for t in toks])
            acc = alphas[0] * toks[0][0]
            for g in range(1, n):
                acc = acc + alphas[g] * toks[g][0]
            refs[2 * n][...] = acc

        return pl.pallas_call(
            body, name=name, out_shape=jax.ShapeDtypeStruct((SEQ, 2 * HEAD_DIM), F32), grid=(SEQ // tr,),
            in_specs=o_specs + l_specs, out_specs=tok, scratch_shapes=scratch, compiler_params=_params(("parallel",)),
        )(*o4, *l4)

    def body(*refs):
        do_refs, dl_refs = refs[2 * n + 1:3 * n + 1], refs[3 * n + 1:4 * n + 1]
        pairs, cols, stages = load(refs[4 * n + 1:])
        toks = [to_tokens(refs[g], refs[n + g], dils[g], pairs[g], cols[g], stages[g]) for g in range(n)]
        per_head, alphas, left = weights([t[1] for t in toks])
        dov = refs[2 * n][...]
        das = []
        for g in range(n):
            prod = dov * toks[g][0]
            das.append([jnp.sum(jnp.where(left, prod, 0.0), axis=1, keepdims=True),
                        jnp.sum(jnp.where(left, 0.0, prod), axis=1, keepdims=True)])
        dbar = [sum(per_head[h][g] * das[g][h] for g in range(n)) for h in range(2)]
        for g, d in enumerate(dils):
            pairs[g][...] = alphas[g] * dov
            for h in range(2):
                cols[g][h][...] = per_head[h][g] * (das[g][h] - dbar[h])
            for r in range(d):
                rows = pl.ds(r, tr // d, stride=d) if d > 1 else slice(None)
                v = pairs[g][rows, :]
                for h in range(2):
                    do_refs[g][h, r] = v[:, h * HEAD_DIM:(h + 1) * HEAD_DIM]
                    dl_refs[g][h, r] = cols[g][h][rows, :]

    out = pl.pallas_call(
        body, name=name,
        out_shape=[jax.ShapeDtypeStruct(o.shape, F32) for o in o4] + [jax.ShapeDtypeStruct(l.shape, F32) for l in l4],
        grid=(SEQ // tr,), in_specs=o_specs + l_specs + [tok], out_specs=o_specs + l_specs, scratch_shapes=scratch,
        compiler_params=_params(("parallel",)),
    )(*o4, *l4, dout)
    return [t.reshape(s.shape) for t, s in zip(out, list(outs) + list(lses))]


def _tri(cmp):
    r = lax.broadcasted_iota(jnp.int32, (SB_TILE, SB_TILE), 0)
    c = lax.broadcasted_iota(jnp.int32, (SB_TILE, SB_TILE), 1)
    return cmp(r, c).astype(BF16)


def _cum(x, tri, terms):
    acc, rest = None, x
    for _ in range(terms):
        part = rest.astype(BF16)
        rest = rest - part.astype(F32)
        d = _dot(part, tri, 1, 0)
        acc = d if acc is None else acc + d
    return acc


def _sb_logits(q, ks, diagonal):
    t = SB_TILE
    z = _dot(q, ks, 1, 1)
    e = jnp.exp(-jnp.abs(z))
    lf = -(jnp.maximum(z, 0.0) + jnp.log(1.0 + e))
    if not diagonal:
        return z, e, lf, None
    mask = lax.broadcasted_iota(jnp.int32, (t, t), 1) < lax.broadcasted_iota(jnp.int32, (t, t), 0)
    return z, e, jnp.where(mask, lf, 0.0), mask


def _sb_specs(h, s):
    t = SB_TILE
    tile = pl.BlockSpec((h, t, HEAD_DIM), lambda i: (0, i, 0))
    keys = pl.BlockSpec((h, s, HEAD_DIM), lambda i: (1, 0, 0))
    values = pl.BlockSpec((h, s, HEAD_DIM), lambda i: (2, 0, 0))
    return tile, keys, values, pl.BlockSpec((h, t, 1), lambda i: (0, i, 0))


def _sb_fwd(x, name):
    h, s = x.shape[0] // 3, x.shape[1]
    t = SB_TILE

    def body(q_ref, k_ref, v_ref, o_ref, tot_ref):
        i = pl.program_id(0)
        after = _tri(lambda r, c: r > c)

        def tile(j, carry, diagonal):
            rows = pl.ds(pl.multiple_of(j * t, t), t)
            out = []
            for hh, (right, acc) in enumerate(carry):
                z, _, lf, mask = _sb_logits(q_ref[hh], k_ref[hh, rows, :], diagonal)
                w = jnp.exp(z + lf + (right + _cum(lf, after, 2)))
                w = w if mask is None else jnp.where(mask, w, 0.0)
                out.append((right + jnp.sum(lf, axis=1, keepdims=True), acc + _dot(w.astype(BF16), v_ref[hh, rows, :], 1, 0)))
            return tuple(out)

        carry = tile(i, tuple((jnp.zeros((t, 1), F32), jnp.zeros((t, HEAD_DIM), F32)) for _ in range(h)), True)
        carry = lax.fori_loop(0, i, lambda jj, c: tile(i - 1 - jj, c, False), carry)
        for hh, (right, acc) in enumerate(carry):
            o_ref[hh] = acc
            tot_ref[hh] = right

    tile_spec, keys, values, col = _sb_specs(h, s)
    return pl.pallas_call(
        body, name=name, out_shape=[jax.ShapeDtypeStruct((h, s, HEAD_DIM), F32), jax.ShapeDtypeStruct((h, s, 1), F32)],
        grid=(s // t,), in_specs=[tile_spec, keys, values], out_specs=[tile_spec, col],
        compiler_params=_params(("parallel",)),
    )(x, x, x)


def _sb_bwd(x, tot, do, name):
    h, s = x.shape[0] // 3, x.shape[1]
    t = SB_TILE

    def body(q_ref, k_ref, v_ref, tot_ref, do_ref, dq_ref, dk_ref, dv_ref):
        i = pl.program_id(0)

        @pl.when(i == 0)
        def _():
            dk_ref[...] = jnp.zeros_like(dk_ref)
            dv_ref[...] = jnp.zeros_like(dv_ref)

        upto = _tri(lambda r, c: r <= c)
        before = _tri(lambda r, c: r < c)

        def tile(j, carry, diagonal):
            rows = pl.ds(pl.multiple_of(j * t, t), t)
            out = []
            for hh, (left, cleft, dq) in enumerate(carry):
                qv, ks, dob = q_ref[hh], k_ref[hh, rows, :], do_ref[hh].astype(BF16)
                z, e, lf, mask = _sb_logits(qv, ks, diagonal)
                between = tot_ref[hh] - (left + _cum(lf, upto, 2))
                w = jnp.exp(z + lf + between)
                w = w if mask is None else jnp.where(mask, w, 0.0)
                dlog = w * _dot(dob, v_ref[hh, rows, :], 1, 1)
                cfail = cleft + _cum(dlog, before, 2)
                sig = jnp.where(z >= 0.0, 1.0, e) / (1.0 + e)
                dz = dlog * (1.0 - sig) - sig * cfail
                dz = (dz if mask is None else jnp.where(mask, dz, 0.0)).astype(BF16)
                dk_ref[hh, rows, :] += _dot(dz, qv, 0, 0)
                dv_ref[hh, rows, :] += _dot(w.astype(BF16), dob, 0, 0)
                out.append((left + jnp.sum(lf, axis=1, keepdims=True), cleft + jnp.sum(dlog, axis=1, keepdims=True),
                            dq + _dot(dz, ks, 1, 0)))
            return tuple(out)

        zero = jnp.zeros((t, 1), F32)
        carry = lax.fori_loop(0, i, lambda j, c: tile(j, c, False),
                              tuple((zero, zero, jnp.zeros((t, HEAD_DIM), F32)) for _ in range(h)))
        for hh, (_, _, dq) in enumerate(tile(i, carry, True)):
            dq_ref[hh] = dq * (HEAD_DIM ** -0.5)

    tile_spec, keys, values, col = _sb_specs(h, s)
    full = pl.BlockSpec((h, s, HEAD_DIM), lambda i: (0, 0, 0))
    shp = jax.ShapeDtypeStruct((h, s, HEAD_DIM), F32)
    return pl.pallas_call(
        body, name=name, out_shape=[shp, shp, shp], grid=(s // t,),
        in_specs=[tile_spec, keys, values, col, tile_spec],
        out_specs=[tile_spec, full, full], compiler_params=_params(("arbitrary",)),
    )(x, x, x, tot, do)


COL_SB, COL_DIL, COL_SWA = 0, 3 * H_SB * HEAD_DIM, 3 * H_SB * HEAD_DIM + 3 * H_DIL * HEAD_DIM
N_SWA = H_SWA_Q + 2 * H_SWA_KV


def _dil_col(t, g):
    return COL_DIL + t * H_DIL * HEAD_DIM + g * 2 * HEAD_DIM


def _split_heads(qkv, name):
    tr = TOK_TILE
    scale = HEAD_DIM ** -0.5
    dils = [d for _, d in DIL_PATTERNS]

    def body(x_ref, sb_ref, d0_ref, d1_ref, d2_ref, swa_ref, pair):
        def head(col, scaled):
            v = x_ref[:, col:col + HEAD_DIM]
            return (v * scale if scaled else v).astype(BF16)

        for hh in range(3 * H_SB):
            sb_ref[hh] = head(COL_SB + hh * HEAD_DIM, hh < H_SB)
        for hh in range(N_SWA):
            swa_ref[hh] = head(COL_SWA + hh * HEAD_DIM, hh < H_SWA_Q)
        for t in range(3):
            for g, (d, out_ref) in enumerate(zip(dils, (d0_ref, d1_ref, d2_ref))):
                col = _dil_col(t, g)
                if d == 1:
                    for h in range(2):
                        out_ref[t * 2 + h] = head(col + h * HEAD_DIM, t == 0)
                    continue
                pair[...] = x_ref[:, col:col + 2 * HEAD_DIM]
                for r in range(d):
                    v = pair[pl.ds(r, tr // d, stride=d), :]
                    v = v * scale if t == 0 else v
                    for h in range(2):
                        out_ref[t * 2 * d + h * d + r] = v[:, h * HEAD_DIM:(h + 1) * HEAD_DIM].astype(BF16)

    def heads(n, length):
        return jax.ShapeDtypeStruct((n, length, HEAD_DIM), BF16)

    def spec(n, rows):
        return pl.BlockSpec((n, rows, HEAD_DIM), lambda i: (0, i, 0))

    return pl.pallas_call(
        body, name=name,
        out_shape=[heads(3 * H_SB, SEQ)] + [heads(6 * d, SEQ // d) for d in dils] + [heads(N_SWA, SEQ)],
        grid=(SEQ // tr,), in_specs=[pl.BlockSpec((tr, D_QKV), lambda i: (i, 0))],
        out_specs=[spec(3 * H_SB, tr)] + [spec(6 * d, tr // d) for d in dils] + [spec(N_SWA, tr)],
        scratch_shapes=[pltpu.VMEM((tr, 2 * HEAD_DIM), F32)], compiler_params=_params(("parallel",)),
    )(qkv)


def _join_heads(sb, dil, swa, name):
    tr = TOK_TILE
    dils = [d for _, d in DIL_PATTERNS]

    def body(*refs):
        sb_refs, dil_refs, swa_refs = refs[:3], [refs[3 + 3 * g:6 + 3 * g] for g in range(3)], refs[12:15]
        o_ref, pair, stages = refs[15], refs[16], refs[17:]

        def put(col, v):
            o_ref[:, col:col + v.shape[1]] = v.astype(BF16)

        for t in range(3):
            for h in range(H_SB):
                put(COL_SB + (t * H_SB + h) * HEAD_DIM, sb_refs[t][h])
        col = COL_SWA
        for ref in swa_refs:
            for h in range(ref.shape[0]):
                put(col, ref[h])
                col += HEAD_DIM
        for t in range(3):
            for g, d in enumerate(dils):
                ref, col = dil_refs[g][t], _dil_col(t, g)
                if d == 1:
                    for h in range(2):
                        put(col + h * HEAD_DIM, ref[h])
                    continue
                stage = stages[g - 1]
                for r in range(d):
                    stage[:, :HEAD_DIM] = ref[r]
                    stage[:, HEAD_DIM:] = ref[d + r]
                    pair[pl.ds(r, tr // d, stride=d), :] = stage[...]
                put(col, pair[...])

    def spec(n, rows):
        return pl.BlockSpec((n, rows, HEAD_DIM), lambda i: (0, i, 0))

    ins = list(sb) + [t for g in range(3) for t in dil[g]] + list(swa)
    in_specs = ([spec(H_SB, tr)] * 3 + [spec(2 * d, tr // d) for d in dils for _ in range(3)]
                + [spec(H_SWA_Q, tr), spec(H_SWA_KV, tr), spec(H_SWA_KV, tr)])
    return pl.pallas_call(
        body, name=name, out_shape=jax.ShapeDtypeStruct((SEQ, D_QKV), BF16), grid=(SEQ // tr,), in_specs=in_specs,
        out_specs=pl.BlockSpec((tr, D_QKV), lambda i: (i, 0)),
        scratch_shapes=[pltpu.VMEM((tr, 2 * HEAD_DIM), F32)] + [pltpu.VMEM((tr // d, 2 * HEAD_DIM), F32) for d in dils[1:]],
        compiler_params=_params(("parallel",)),
    )(*ins)


def _mixer_fwd(qkv, bias, sinks_l, tag):
    sb, d0, d1, d2, swa = _split_heads(qkv, name=f"split_heads_{tag}")
    st = {"sb": sb, "dil": (d0, d1, d2), "swa": swa}
    o_sb, st["sb_tot"] = _sb_fwd(sb, name=f"sb_fwd_{tag}")
    st["dil_out"], st["dil_lse"], st["dil_sink"] = [], [], []
    for gi, (_, d) in enumerate(DIL_PATTERNS):
        sink = jnp.zeros((2 * d, 1, LANES), F32)
        og, lg = _band_fwd(st["dil"][gi], bias[2 * gi:2 * gi + 2], sink, nq=2 * d, offs=(0, 2 * d, 4 * d), g=1, bias_div=d,
                           has_sink=False, name=f"dil{gi}_fwd_{tag}")
        st["dil_out"].append(og)
        st["dil_lse"].append(lg)
        st["dil_sink"].append(sink)
    o_dil = _dil_merge(st["dil_out"], st["dil_lse"], None, name=f"dil_merge_fwd_{tag}")
    st["swa_sink"] = jnp.broadcast_to(sinks_l.reshape(H_SWA_Q, 1, 1), (H_SWA_Q, 1, LANES))
    st["swa_out"] = _band_fwd(swa, bias[H_DIL:], st["swa_sink"], nq=H_SWA_Q, offs=(0, H_SWA_Q, H_SWA_Q + H_SWA_KV),
                              g=H_SWA_Q // H_SWA_KV, bias_div=1, has_sink=True, name=f"swa_fwd_{tag}")
    return (o_sb, o_dil, st["swa_out"][0]), st


def _mixer_bwd(st, bias, do_sb, do_dil, do_swa, tag):
    d_sb = _sb_bwd(st["sb"], st["sb_tot"], do_sb, name=f"sb_bwd_{tag}")
    dmerge = _dil_merge(st["dil_out"], st["dil_lse"], do_dil, name=f"dil_merge_bwd_{tag}")
    d_dil, dbs = [], []
    for gi, (_, d) in enumerate(DIL_PATTERNS):
        dq, dk, dv, db, _ = _band_bwd(st["dil"][gi], bias[2 * gi:2 * gi + 2], st["dil_sink"][gi], st["dil_out"][gi],
                                      st["dil_lse"][gi], dmerge[gi], dmerge[3 + gi], nq=2 * d, offs=(0, 2 * d, 4 * d),
                                      g=1, bias_div=d, has_sink=False, name=f"dil{gi}_bwd_{tag}")
        d_dil.append((dq, dk, dv))
        dbs.append(db)
    o_sw, l_sw = st["swa_out"]
    dq_sw, dk_sw, dv_sw, db_sw, dsink = _band_bwd(st["swa"], bias[H_DIL:], st["swa_sink"], o_sw, l_sw, do_swa,
                                                  jnp.zeros_like(l_sw), nq=H_SWA_Q, offs=(0, H_SWA_Q, H_SWA_Q + H_SWA_KV),
                                                  g=H_SWA_Q // H_SWA_KV, bias_div=1, has_sink=True, name=f"swa_bwd_{tag}")
    dqkv = _join_heads(d_sb, d_dil, (dq_sw, dk_sw, dv_sw), name=f"join_heads_{tag}")
    return dqkv, jnp.concatenate(dbs + [db_sw], 0), dsink[:, 0, 0]


PIECES = ("ffn0", "mix", "ffn1")


def _ffn_fwd(x_in, w, gain, mod_j, tag, after=None):
    st = {"x": x_in, "w": w}
    st["h"] = _norm_fwd(x_in, _row(gain), _row(mod_j[1]), _row(mod_j[0]), name=f"norm_fwd_{tag}", after=after)
    st["a"], st["u"], st["s"] = _ffn_up(st["h"], w["gate"], w["up"], name=f"up_{tag}")
    st["f"], x_out = _mm(st["s"], w["down"], res=x_in, colscale=_row(0.5 * mod_j[2]), emit_acc=True, tm=512, tn=1024,
                         name=f"down_{tag}")
    return x_out, st


def _ffn_bwd(dx_out, st, gain, mod_j, tag, done, pre, nxt):
    w = st["w"]

    def latest(new, old):
        return old if new is None else new

    df, dgate = pre or _gate_bwd(dx_out, st["f"], _row(0.5 * mod_j[2]), 0.5, name=f"gate_bwd_{tag}")
    token = done({"down": _mm_tn(st["s"], df, tm=D_FF // 2, name=f"dwd_{tag}")})
    da, du = _ffn_bwd_ds(df, w["down"], st["a"], st["u"], name=f"ds_{tag}")
    token = latest(done({"gate": _mm_tn(da, st["h"], after=token, tm=D_FF // 2, name=f"dwg_{tag}")}), token)
    token = latest(done({"up": _mm_tn(du, st["h"], after=token, tm=D_FF // 2, name=f"dwu_{tag}")}), token)
    dx_in, sum_dh, sum_dhx, made = _dh_norm_bwd(da, w["gate"], du, w["up"], st["x"], dx_out, _row(gain), _row(mod_j[1]), nxt,
                                                after=token, name=f"dh_{tag}")
    dmod = jnp.concatenate([sum_dh, gain * sum_dhx, dgate], 0)
    return dx_in, dmod, (1.0 + mod_j[1]) * sum_dhx[0], made


def _mix_fwd(x_in, w, gain, mod_j, bias, sinks_l, tag, after=None):
    st = {"x": x_in, "w": w}
    st["h"] = _norm_fwd(x_in, _row(gain), _row(mod_j[1]), _row(mod_j[0]), name=f"norm_fwd_mix_{tag}", after=after)
    qkv = _mm(st["h"], w["in"], tb=True, tm=SEQ, b_rows=(0, D_QKV), name=f"qkv_{tag}")
    st["gates"] = _mm(st["h"], w["in"], tb=True, tm=SEQ, b_rows=(D_QKV, D_GATES), name=f"gates_{tag}")
    outs, st["mix"] = _mixer_fwd(qkv, bias, sinks_l, tag)
    st["merged"], *st["t"] = _merge_fwd(*outs, st["gates"], w["br_sb"], w["br_dil"], w["br_swa"], name=f"merge_fwd_{tag}")
    st["f"], x_out = _mm(st["merged"], w["out"], res=x_in, colscale=_row(mod_j[2]), emit_acc=True, name=f"out_{tag}")
    return x_out, st


def _mix_bwd(dx_out, st, gain, mod_j, bias, tag, done, pre, nxt):
    w = st["w"]
    df, dgate = pre or _gate_bwd(dx_out, st["f"], _row(mod_j[2]), 1.0, name=f"gate_bwd_mix_{tag}")
    g = {"out": _mm_tn(st["merged"], df, name=f"dw_out_{tag}")}
    dmerged = _mm(df, w["out"], tb=True, name=f"dmerged_{tag}")
    dgates, do_sb, do_dil, do_swa, dbr_sb, dbr_dil, dbr_swa = _merge_bwd(
        dmerged, *st["t"], st["gates"], w["br_sb"], w["br_dil"], w["br_swa"], name=f"merge_bwd_{tag}")
    g["br_sb"] = _mm_tn(st["t"][0], dbr_sb, name=f"dw_br_sb_{tag}")
    g["br_dil"] = _mm_tn(st["t"][1], dbr_dil, name=f"dw_br_dil_{tag}")
    g["br_swa"] = _mm_tn(st["t"][2], dbr_swa, name=f"dw_br_swa_{tag}")
    dqkv, dbias, dsinks = _mixer_bwd(st["mix"], bias, do_sb, do_dil, do_swa, tag)
    dw_qkv = _mm_tn(dqkv, st["h"], out_rows=D_QKV + D_GATES, name=f"dw_qkv_{tag}")
    g["in"] = _mm_tn(dgates, st["h"], out_rows=D_QKV + D_GATES, row0=D_QKV, prev=dw_qkv, name=f"dw_gates_{tag}")
    dx_in, sum_dh, sum_dhx, made = _dh_norm_bwd(dqkv, w["in"], dgates, w["in"], st["x"], dx_out, _row(gain), _row(mod_j[1]),
                                                nxt, after=done(g), b_rows=(0, D_QKV), name=f"dh_mix_{tag}")
    dmod = jnp.concatenate([sum_dh, gain * sum_dhx, dgate], 0)
    return dx_in, dmod, (1.0 + mod_j[1]) * sum_dhx[0], dbias, dsinks, made


def _local_step(x, target, mod, gains, weights_of, rel_bias, sinks, final_gain, grads_done):
    tables = jnp.asarray(_bucket_tables())
    bias = _bias_build(rel_bias, tables, name="bias_build")
    states, h = [], x
    for l in range(DEPTH):
        st = {}
        for j, piece in enumerate(PIECES):
            w, after = weights_of(l, piece, h)
            if piece == "mix":
                h, st[piece] = _mix_fwd(h, w, gains[l, j], mod[l, j], bias, sinks[l], f"l{l}", after)
            else:
                h, st[piece] = _ffn_fwd(h, w, gains[l, j], mod[l, j], f"{piece}_l{l}", after)
        states.append(st)
    loss, dx, dfinal = _final_loss(h, target, _row(final_gain), name="final_loss")
    dmods = [[None] * 3 for _ in range(DEPTH)]
    dgains = [[None] * 3 for _ in range(DEPTH)]
    dsinks = [None] * DEPTH
    dbias, made = None, None
    sweep = [(l, j) for l in reversed(range(DEPTH)) for j in reversed(range(3))]
    for k, (l, j) in enumerate(sweep):
        piece = PIECES[j]
        done = lambda grads, l=l, piece=piece: grads_done(l, piece, grads)
        nxt = None
        if k + 1 < len(sweep):
            nl, nj = sweep[k + 1]
            coef = 1.0 if PIECES[nj] == "mix" else 0.5
            nxt = (states[nl][PIECES[nj]]["f"], _row(coef * mod[nl, nj, 2]), coef)
        if piece == "mix":
            dx, dmods[l][j], dgains[l][j], db, dsinks[l], made = _mix_bwd(
                dx, states[l][piece], gains[l, j], mod[l, j], bias, f"l{l}", done, made, nxt)
            dbias = db if dbias is None else dbias + db
        else:
            dx, dmods[l][j], dgains[l][j], made = _ffn_bwd(
                dx, states[l][piece], gains[l, j], mod[l, j], f"{piece}_l{l}", done, made, nxt)
    drel = _bias_grad(dbias, tables, name="bias_grad")[:, 0, :N_BUCKETS].T
    dmod = jnp.stack([jnp.stack(m) for m in dmods])
    dgain = jnp.stack([jnp.stack(g) for g in dgains])
    return loss, dx, dmod, dgain, dfinal[0], drel, jnp.stack(dsinks)


BR_ROWS = (H_SB * HEAD_DIM, 2 * HEAD_DIM, H_SWA_Q * HEAD_DIM)


def _lanes_unshard(g, lead):
    _, rows, _ = g.shape
    r = rows // lead
    return g.reshape(N_DEV, lead, r, LANES).transpose(1, 2, 0, 3).reshape(lead, r, N_DEV * LANES)


def _lanes_shard(full):
    lead, r, _ = full.shape
    return full.reshape(lead, r, N_DEV, LANES).transpose(2, 0, 1, 3).reshape(N_DEV, lead * r, LANES)


def _pack_rows(parts, dtype):
    flat = jnp.concatenate([p.astype(dtype).reshape(-1) for p in parts])
    pad = (-flat.shape[0]) % (16 * LANES)
    if pad:
        flat = jnp.concatenate([flat, jnp.zeros((pad,), dtype)])
    return flat.reshape(-1, LANES)


def _unshard(gathered, axis):
    moved = jnp.moveaxis(gathered, 0, axis)
    shape = list(moved.shape)
    shape[axis:axis + 2] = [shape[axis] * shape[axis + 1]]
    return moved.reshape(shape)


def kernel(x, c, w_ada, b_ada, norm_gain, w_ffn_gate, w_ffn_up, w_ffn_down, w_in, w_br_sb, w_br_dil, w_br_swa, w_out, sinks, rel_bias, final_gain, loss_target, m_w_ada, m_b_ada, m_norm_gain, m_w_ffn_gate, m_w_ffn_up, m_w_ffn_down, m_w_in, m_w_br_sb, m_w_br_dil, m_w_br_swa, m_w_out, m_sinks, m_rel_bias, m_final_gain, v_w_ada, v_b_ada, v_norm_gain, v_w_ffn_gate, v_w_ffn_up, v_w_ffn_down, v_w_in, v_w_br_sb, v_w_br_dil, v_w_br_swa, v_w_out, v_sinks, v_rel_bias, v_final_gain):
    me = 4 * lax.axis_index("x") + 2 * lax.axis_index("y") + lax.axis_index("c")
    d = D_MODEL
    gate_t, up_t, in_t = jnp.swapaxes(w_ffn_gate, 2, 3), jnp.swapaxes(w_ffn_up, 2, 3), jnp.swapaxes(w_in, 1, 2)

    def piece_shards(l, piece):
        bf = lambda t: t.astype(BF16)
        if piece == "mix":
            return [bf(in_t[l]), jnp.concatenate([bf(w_br_sb[l]), bf(w_br_dil[l]), bf(w_br_swa[l])], 0), bf(w_out[l])]
        i = PIECES.index(piece) // 2
        return [bf(gate_t[l, i]), bf(up_t[l, i]), bf(w_ffn_down[l, i])]

    br_off = np.concatenate([[0], np.cumsum(BR_ROWS)])

    def piece_weights(gathered, piece):
        if piece == "mix":
            g_in, g_br, g_out = gathered
            f_br = [_lanes_unshard(g_br[:, br_off[k]:br_off[k + 1]], 1)[0] for k in range(3)]
            return {"in": g_in.reshape(D_QKV + D_GATES, d), "br_sb": f_br[0], "br_dil": f_br[1], "br_swa": f_br[2],
                    "out": g_out.reshape(d, d)}
        return {n: g.reshape(D_FF, d) for n, g in zip(("gate", "up", "down"), gathered)}

    order = [(l, piece) for l in range(DEPTH) for piece in PIECES]
    ahead = 3
    in_flight, passed = {}, {}

    def start_gather(k, after):
        l, piece = order[k]
        in_flight[k], token = _relay_start(piece_shards(l, piece), after, name=f"gather_{piece}_l{l}_start")
        return token

    small, = _all_gather([_pack_rows([c, norm_gain], F32)], after=start_gather(0, c), name="gather_cond")
    c_all = small[:, :d // LANES].reshape(N_DEV, d)
    gains = _unshard(small[:, d // LANES:d // LANES + 6].reshape(N_DEV, DEPTH, 3, LANES), 2)

    cols = w_ada.shape[2]
    mod_cols = jnp.stack([_ada_fwd(c_all, w_ada[l], name=f"ada_fwd_l{l}") for l in range(DEPTH)])
    mod_all, = _all_gather([_pack_rows([mod_cols], F32)], name="gather_mod")
    mod_all = mod_all.reshape(N_DEV, -1)[:, :DEPTH * N_DEV * cols].reshape(N_DEV, DEPTH, N_DEV, cols)
    mod_mine = lax.dynamic_index_in_dim(mod_all, me, axis=2, keepdims=False)
    mod = (mod_mine.transpose(1, 0, 2).reshape(DEPTH, N_DEV * cols) + b_ada).reshape(DEPTH, 3, 3, d)

    token = mod_all
    for k in range(1, 1 + ahead):
        token = start_gather(k, token)
    mod = mod + token[0, 0]

    def weights_of(l, piece, h):
        k = order.index((l, piece))
        token = start_gather(k + ahead, h) if k + ahead < len(order) and k + ahead not in in_flight else None
        for nxt in ([k] if k < 3 else []) + ([k + 1] if 3 <= k + 1 < len(order) else []):
            nl, npiece = order[nxt]
            passed[nxt], token = _relay_pass(in_flight[nxt], h if token is None else token,
                                             name=f"gather_{npiece}_l{nl}_pass")
        landed = _relay_wait(passed[k], h if token is None else token, name=f"gather_{piece}_l{l}_wait")
        return piece_weights(landed, piece), token

    exchanges, have, deferred = {}, {}, []

    def grads_done(l, piece, g):
        key = (l, piece)
        have.setdefault(key, {}).update(g)
        if piece == "mix":
            if len(have[key]) < 5:
                return None
            g = have[key]
            s_br = jnp.concatenate([_lanes_shard(g[n][None]) for n in ("br_sb", "br_dil", "br_swa")], 1)
            groups = [(("in", "br", "out"), [g["in"].reshape(N_DEV, -1, d), s_br, g["out"].reshape(N_DEV, -1, d)])]
        elif key == order[0]:
            deferred.extend(((n,), [t.reshape(N_DEV, -1, d)]) for n, t in g.items())
            return None
        elif len(have[key]) < 3:
            return None
        else:
            groups = [(("gate", "up", "down"), [have[key][n].reshape(N_DEV, -1, d) for n in ("gate", "up", "down")])]
        token = None
        for names, sg in groups:
            state, token = _exchange_start(sg, None, name=f"exchange_{piece}_l{l}_{names[0]}_start")
            exchanges.setdefault(key, []).append((names, state))
        return token

    loss, dx, dmod, dgains, dfinal, drel, dsinks = _local_step(
        x[0], loss_target[0], mod, gains, weights_of, rel_bias, sinks, final_gain, grads_done)

    flat = lambda t: t.reshape(-1, t.shape[-1])
    transposed = lambda ts: tuple(flat(jnp.swapaxes(t, -1, -2)) for t in ts)
    families = {
        "gate": transposed((w_ffn_gate, m_w_ffn_gate, v_w_ffn_gate)), "up": transposed((w_ffn_up, m_w_ffn_up, v_w_ffn_up)),
        "down": tuple(flat(t) for t in (w_ffn_down, m_w_ffn_down, v_w_ffn_down)),
        "in": transposed((w_in, m_w_in, v_w_in)),
        "br": tuple(flat(jnp.concatenate(ts, 1)) for ts in ((w_br_sb, w_br_dil, w_br_swa), (m_w_br_sb, m_w_br_dil, m_w_br_swa),
                                                            (v_w_br_sb, v_w_br_dil, v_w_br_swa))),
        "out": tuple(flat(t) for t in (w_out, m_w_out, v_w_out))}
    parts, stepped = {}, {}

    def step(keys, after):
        for key in keys:
            for names, ex_state in exchanges[key]:
                landed = _exchange_wait(ex_state, after, name=f"exchange_{key[1]}_l{key[0]}_{names[0]}_wait")
                parts.setdefault(key, {}).update(zip(names, landed))
                after = landed[0]
        for key in keys:
            l, piece = key
            for n, group in parts[key].items():
                w2, m2, v2 = families[n]
                rows = group.shape[1]
                row0 = (2 * l + PIECES.index(piece) // 2) * rows if piece != "mix" else l * rows
                stepped[n] = _reduce_adamw([group], w2, m2, v2, row0, stepped.get(n), after=after,
                                           name=f"reduce_adamw_{n}_{piece}_l{l}")
                after = stepped[n][1]
        return after

    small_parts = [dmod, dgains, dfinal, drel.T, dsinks, loss[0, :1]]
    small_sizes = [int(np.prod(p.shape)) for p in small_parts]
    small_all, = _all_gather([_pack_rows(small_parts, F32)], name="gather_small")
    token = small_all
    for names, sg in deferred:
        state, token = _exchange_start(sg, token, name=f"exchange_ffn0_l0_{names[0]}_start")
        exchanges.setdefault(order[0], []).append((names, state))

    after_l1 = step([key for key in reversed(order) if key[0] == 1], token)
    small_sum = _sum_parts(small_all, name="sum_small", after=token).reshape(-1)
    offs = np.concatenate([[0], np.cumsum(small_sizes)])
    g_b_ada = small_sum[offs[0]:offs[1]].reshape(DEPTH, 9 * d)
    g_gain_full = small_sum[offs[1]:offs[2]].reshape(DEPTH, 3, d)
    g_norm_gain = lax.dynamic_slice_in_dim(g_gain_full, me * LANES, LANES, axis=2)
    g_final = small_sum[offs[2]:offs[3]]
    g_rel = small_sum[offs[3]:offs[4]].reshape(N_SOFT, N_BUCKETS).T
    g_sinks = small_sum[offs[4]:offs[5]].reshape(DEPTH, H_SWA_Q)
    loss_total = small_sum[offs[5]]

    dmod_all = small_all.reshape(N_DEV, -1)[:, :DEPTH * 9 * d].reshape(N_DEV, DEPTH, 9 * d)
    dmod_cols = lax.dynamic_slice_in_dim(dmod_all, me * cols, cols, axis=2)
    g_w_ada = jnp.stack([_ada_bwd(c_all.T, dmod_cols[:, l], name=f"ada_bwd_l{l}") for l in range(DEPTH)])

    small_state = {"w_ada": (w_ada, m_w_ada, v_w_ada), "b_ada": (b_ada, m_b_ada, v_b_ada),
                   "norm_gain": (norm_gain, m_norm_gain, v_norm_gain), "sinks": (sinks, m_sinks, v_sinks),
                   "rel_bias": (rel_bias, m_rel_bias, v_rel_bias), "final_gain": (final_gain, m_final_gain, v_final_gain)}
    after = step([order[2], order[1]], after_l1)
    grad, update = {}, {}
    for n, g in (("w_ada", g_w_ada), ("b_ada", g_b_ada), ("norm_gain", g_norm_gain), ("sinks", g_sinks),
                 ("rel_bias", g_rel), ("final_gain", g_final)):
        w, m, v = small_state[n]
        grad[n] = g
        if w.ndim == 1:
            update[n] = tuple(t.reshape(w.shape)
                              for t in _adamw(_row(w), _row(g), _row(m), _row(v), name=f"adamw_{n}", after=after))
        else:
            update[n] = _adamw(w, g, m, v, name=f"adamw_{n}", after=after)
        after = update[n][0]

    step([order[0]], after)

    def unflat(n, like, swapped):
        shape = jnp.swapaxes(like, -1, -2).shape if swapped else like.shape
        out = [t.reshape(shape) for t in stepped[n]]
        return [jnp.swapaxes(t, -1, -2) for t in out] if swapped else out

    results = {"w_ffn_gate": unflat("gate", w_ffn_gate, True), "w_ffn_up": unflat("up", w_ffn_up, True),
               "w_ffn_down": unflat("down", w_ffn_down, False), "w_in": unflat("in", w_in, True),
               "w_out": unflat("out", w_out, False)}
    br = [t.reshape(DEPTH, -1, LANES) for t in stepped["br"]]
    for k, n in enumerate(("w_br_sb", "w_br_dil", "w_br_swa")):
        results[n] = [t[:, br_off[k]:br_off[k + 1]] for t in br]
    for n, (g, dl, nm, nv) in results.items():
        grad[n], update[n] = g, (dl, nm, nv)

    names = ["w_ada", "b_ada", "norm_gain", "w_ffn_gate", "w_ffn_up", "w_ffn_down", "w_in", "w_br_sb", "w_br_dil",
             "w_br_swa", "w_out", "sinks", "rel_bias", "final_gain"]
    return (loss_total, dx[None], *[grad[n] for n in names], *[update[n][0] for n in names],
            *[update[n][1] for n in names], *[update[n][2] for n in names])
```

```python
import math

import numpy as np
import jax
import jax.numpy as jnp
from jax import lax
from jax.experimental import pallas as pl
from jax.experimental.pallas import tpu as pltpu

F32, BF16 = jnp.float32, jnp.bfloat16

SEQ, D_MODEL, D_FF, HEAD_DIM = 2048, 1024, 2816, 64
DEPTH = 2
BLK = 128
H_SB, H_DIL, H_SWA_Q, H_SWA_KV = 4, 6, 6, 2
DIL_PATTERNS = ((128, 1), (512, 4), (2048, 16))
SWA_WINDOW = 128
N_BUCKETS, MAX_REL_DIST = 32, 2048
RMS_EPS = 1e-6
D_QKV = 2560
D_GATES = 3 * D_MODEL
ADAM_LR, ADAM_B1, ADAM_B2, ADAM_EPS, ADAM_WD, ADAM_STEP = 0.001, 0.9, 0.999, 1e-08, 0.01, 10

N_DEV = 8
LANES = 128
NEG = -1e30
SB_TILE = 512
VMEM_LIMIT_BYTES = 48 * 1024 * 1024
HBM = pl.BlockSpec(memory_space=pltpu.HBM)
MESH = pl.DeviceIdType.MESH


def _tile(n, target):
    t = (min(n, target) // LANES) * LANES
    while t >= LANES:
        if n % t == 0:
            return t
        t -= LANES
    return n


def _row_tile(r, cap):
    t = (min(r, cap) // 16) * 16
    while t > 16 and r % t:
        t -= 16
    return t


def _params(semantics=None):
    return pltpu.CompilerParams(dimension_semantics=semantics, vmem_limit_bytes=VMEM_LIMIT_BYTES)


def _dot(a, b, ca, cb):
    return lax.dot_general(a, b, (((ca,), (cb,)), ((), ())), preferred_element_type=F32)


def _sigmoid(a):
    return 1.0 / (1.0 + jnp.exp(-a))


def _row(v):
    return v.reshape(1, -1)


def _all_gather(arrs, name, after=None):
    n = len(arrs)
    ins = list(arrs) + ([] if after is None else [after])

    def body(*refs):
        x_refs, out_refs = refs[:n], refs[len(ins):len(ins) + n]
        send_sems, recv_sems, local_sems = refs[len(ins) + n:]
        x, y, c = lax.axis_index("x"), lax.axis_index("y"), lax.axis_index("c")
        me, sibling = (x, y, c), (x, y, 1 - c)
        chips = [(1 - x, y), (x, 1 - y), (1 - x, 1 - y)]

        def slot(t, px, py, pc):
            return out_refs[t].at[4 * px + 2 * py + pc]

        def copy(t, k, block, to, src=None):
            return pltpu.make_async_remote_copy(
                src_ref=slot(t, *block) if src is None else src, dst_ref=slot(t, *block),
                send_sem=send_sems.at[7 * t + k], recv_sem=recv_sems.at[7 * t + k], device_id=to, device_id_type=MESH)

        mine = [pltpu.make_async_copy(x_refs[t], slot(t, *me), local_sems.at[t]) for t in range(n)]
        for cp in mine:
            cp.start()
        first = []
        for t in range(n):
            first.append(copy(t, 0, me, sibling, src=x_refs[t]))
            first += [copy(t, 1 + j, me, (*chip, c), src=x_refs[t]) for j, chip in enumerate(chips)]
        for cp in first:
            cp.start()
        passed = []
        for j, chip in enumerate(chips):
            for t in range(n):
                copy(t, 1 + j, (*chip, c), me).wait_recv()
                passed.append(copy(t, 4 + j, (*chip, c), sibling))
                passed[-1].start()
        for t in range(n):
            copy(t, 0, sibling, me).wait_recv()
        for j, chip in enumerate(chips):
            for t in range(n):
                copy(t, 4 + j, (*chip, 1 - c), me).wait_recv()
        for cp in first + passed:
            cp.wait_send()
        for cp in mine:
            cp.wait()

    return pl.pallas_call(
        body, name=name, out_shape=[jax.ShapeDtypeStruct((N_DEV,) + a.shape, a.dtype) for a in arrs],
        in_specs=[HBM] * n + [pl.BlockSpec(memory_space=pl.ANY)] * (len(ins) - n), out_specs=[HBM] * n,
        scratch_shapes=[pltpu.SemaphoreType.DMA((7 * n,)), pltpu.SemaphoreType.DMA((7 * n,)), pltpu.SemaphoreType.DMA((n,))],
    )(*ins)


def _direct_copies(x_refs, land_refs, send_sems, recv_sems, local_sems):
    x, y, c = lax.axis_index("x"), lax.axis_index("y"), lax.axis_index("c")
    me = 4 * x + 2 * y + c
    sends, recvs = [], []
    for k in range(1, N_DEV):
        px = 1 - x if (k >> 2) & 1 else x
        py = 1 - y if (k >> 1) & 1 else y
        pc = 1 - c if k & 1 else c
        peer = 4 * px + 2 * py + pc
        for t, (x_ref, land_ref) in enumerate(zip(x_refs, land_refs)):
            sem = 7 * t + k - 1
            for out, src, slot in ((sends, peer, me), (recvs, me, peer)):
                out.append(pltpu.make_async_remote_copy(
                    src_ref=x_ref.at[src], dst_ref=land_ref.at[slot], send_sem=send_sems.at[sem],
                    recv_sem=recv_sems.at[sem], device_id=(px, py, pc), device_id_type=MESH))
    own = [pltpu.make_async_copy(x_ref.at[me], land_ref.at[me], local_sems.at[t])
           for t, (x_ref, land_ref) in enumerate(zip(x_refs, land_refs))]
    return sends, recvs, own


SEM =pl.BlockSpec(memory_space=pltpu.SEMAPHORE)
ANY = pl.BlockSpec(memory_space=pl.ANY)
SIDE_EFFECT = pltpu.SideEffectType.DATAFLOW_SIDE_EFFECTING


def _exchange_start(arrs, after, *, name):
    n = len(arrs)
    lands = [lax.empty(a.shape, a.dtype) for a in arrs]
    extra = [] if after is None else [after]

    def body(*refs):
        sems = refs[2 * n + len(extra):2 * n + len(extra) + 3]
        sends, _, own = _direct_copies(refs[:n], refs[n:2 * n], *sems)
        for cp in own + sends:
            cp.start()
        refs[-1][...] = jnp.zeros_like(refs[-1])

    ops = [pltpu.with_memory_space_constraint(a, pltpu.HBM) for a in list(arrs) + lands]
    out = pl.pallas_call(
        body, name=name,
        out_shape=(pltpu.SemaphoreType.DMA((7 * n,)), pltpu.SemaphoreType.DMA((7 * n,)), pltpu.SemaphoreType.DMA((n,)),
                   *[pltpu.HBM(a.shape, a.dtype) for a in ops], jax.ShapeDtypeStruct((8, LANES), F32)),
        in_specs=[HBM] * (2 * n) + [ANY] * len(extra),
        out_specs=(SEM, SEM, SEM, *[HBM] * (2 * n), pl.BlockSpec(memory_space=pltpu.VMEM)),
        input_output_aliases={t: 3 + t for t in range(2 * n)},
        compiler_params=pltpu.CompilerParams(has_side_effects=SIDE_EFFECT),
    )(*ops, *extra)
    return (out[:3], out[3:3 + n], out[3 + n:3 + 2 * n]), out[-1]


def _exchange_wait(state, after, *, name):
    sems, arrs, lands = state
    n = len(arrs)

    def body(*refs):
        sends, recvs, own = _direct_copies(refs[:n], refs[n:2 * n], *refs[2 * n:2 * n + 3])
        for cp in own:
            cp.wait()
        for cp in sends:
            cp.wait_send()
        for cp in recvs:
            cp.wait_recv()

    out = pl.pallas_call(
        body, name=name, out_shape=tuple(pltpu.HBM(a.shape, a.dtype) for a in list(arrs) + list(lands)),
        in_specs=[HBM] * (2 * n) + [SEM, SEM, SEM, ANY], out_specs=tuple([HBM] * (2 * n)),
        input_output_aliases={t: t for t in range(2 * n)},
        compiler_params=pltpu.CompilerParams(has_side_effects=SIDE_EFFECT),
    )(*arrs, *lands, *sems, after)
    return out[n:]


def _relay_copies(x_refs, land_refs, sems_a, sems_b):
    x, y, c = lax.axis_index("x"), lax.axis_index("y"), lax.axis_index("c")
    me = 4 * x + 2 * y + c
    sibling = (x, y, 1 - c)
    chips = [(1 - x, y), (x, 1 - y), (1 - x, 1 - y)]

    def slot(px, py, pc):
        return 4 * px + 2 * py + pc

    def copy(src, land_ref, dst_slot, send_sems, recv_sems, k, to):
        return pltpu.make_async_remote_copy(src_ref=src, dst_ref=land_ref.at[dst_slot], send_sem=send_sems.at[k],
                                            recv_sem=recv_sems.at[k], device_id=to, device_id_type=MESH)

    a_send, a_recv, a_own, b_send, b_recv = [], [], [], [], []
    for t, (x_ref, land_ref) in enumerate(zip(x_refs, land_refs)):
        peers = [sibling] + [(*chip, c) for chip in chips]
        if sems_a is not None:
            for k, peer in enumerate(peers):
                a_send.append(copy(x_ref, land_ref, me, sems_a[0], sems_a[1], 4 * t + k, peer))
                a_recv.append(copy(x_ref, land_ref, slot(*peer), sems_a[0], sems_a[1], 4 * t + k, peer))
            a_own.append(pltpu.make_async_copy(x_ref, land_ref.at[me], sems_a[2].at[t]))
        if sems_b is not None:
            for j, chip in enumerate(chips):
                b_send.append(copy(land_ref.at[slot(*chip, c)], land_ref, slot(*chip, c), sems_b[0], sems_b[1], 3 * t + j, sibling))
                b_recv.append(copy(land_ref.at[slot(*chip, c)], land_ref, slot(*chip, 1 - c), sems_b[0], sems_b[1], 3 * t + j,
                                   sibling))
    return (a_send, a_recv, a_own), (b_send, b_recv)


def _relay_start(arrs, after, name):
    n = len(arrs)
    lands = [lax.empty((N_DEV,) + a.shape, a.dtype) for a in arrs]

    def body(*refs):
        (sends, _, own), _ = _relay_copies(refs[:n], refs[n:2 * n], refs[2 * n + 1:2 * n + 4], None)
        for cp in own + sends:
            cp.start()
        refs[-1][...] = jnp.zeros_like(refs[-1])

    ops = [pltpu.with_memory_space_constraint(a, pltpu.HBM) for a in list(arrs) + lands]
    out = pl.pallas_call(
        body, name=name,
        out_shape=(pltpu.SemaphoreType.DMA((4 * n,)), pltpu.SemaphoreType.DMA((4 * n,)), pltpu.SemaphoreType.DMA((n,)),
                   *[pltpu.HBM(a.shape, a.dtype) for a in ops], jax.ShapeDtypeStruct((8, LANES), F32)),
        in_specs=[HBM] * (2 * n) + [ANY],
        out_specs=(SEM, SEM, SEM, *[HBM] * (2 * n), pl.BlockSpec(memory_space=pltpu.VMEM)),
        input_output_aliases={t: 3 + t for t in range(2 * n)},
        compiler_params=pltpu.CompilerParams(has_side_effects=SIDE_EFFECT),
    )(*ops, after)
    return (out[:3], out[3:3 + n], out[3 + n:3 + 2 * n]), out[-1]


def _relay_pass(state, after, name):
    sems_a, arrs, lands = state
    n = len(arrs)

    def body(*refs):
        sems_b = refs[2 * n + 4:2 * n + 6]
        (a_send, a_recv, a_own), (b_send, _) = _relay_copies(refs[:n], refs[n:2 * n], refs[2 * n:2 * n + 3], sems_b)
        for cp in a_own:
            cp.wait()
        for cp in a_send:
            cp.wait_send()
        for cp in a_recv:
            cp.wait_recv()
        for cp in b_send:
            cp.start()
        refs[-1][...] = jnp.zeros_like(refs[-1])

    out = pl.pallas_call(
        body, name=name,
        out_shape=(pltpu.SemaphoreType.DMA((3 * n,)), pltpu.SemaphoreType.DMA((3 * n,)),
                   *[pltpu.HBM(a.shape, a.dtype) for a in list(arrs) + list(lands)], jax.ShapeDtypeStruct((8, LANES), F32)),
        in_specs=[HBM] * (2 * n) + [SEM, SEM, SEM, ANY],
        out_specs=(SEM, SEM, *[HBM] * (2 * n), pl.BlockSpec(memory_space=pltpu.VMEM)),
        input_output_aliases={t: 2 + t for t in range(2 * n)},
        compiler_params=pltpu.CompilerParams(has_side_effects=SIDE_EFFECT),
    )(*arrs, *lands, *sems_a, after)
    return (out[:2], out[2:2 + n], out[2 + n:2 + 2 * n]), out[-1]


def _relay_wait(state, after, name):
    sems_b, arrs, lands = state
    n = len(arrs)

    def body(*refs):
        _, (b_send, b_recv) = _relay_copies(refs[:n], refs[n:2 * n], None, refs[2 * n:2 * n + 2])
        for cp in b_send:
            cp.wait_send()
        for cp in b_recv:
            cp.wait_recv()

    out = pl.pallas_call(
        body, name=name, out_shape=tuple(pltpu.HBM(a.shape, a.dtype) for a in list(arrs) + list(lands)),
        in_specs=[HBM] * (2 * n) + [SEM, SEM, ANY], out_specs=tuple([HBM] * (2 * n)),
        input_output_aliases={t: t for t in range(2 * n)},
        compiler_params=pltpu.CompilerParams(has_side_effects=SIDE_EFFECT),
    )(*arrs, *lands, *sems_b, after)
    return out[n:]


def _sum_parts(parts, name, after=None):
    n, r, cdim = parts.shape
    tr = _row_tile(r, max(16, (1 << 21) // (n * cdim * parts.dtype.itemsize)))

    def body(p_ref, *rest):
        acc = p_ref[0].astype(F32)
        for k in range(1, n):
            acc = acc + p_ref[k].astype(F32)
        rest[-1][...] = acc

    ins = [parts] + ([] if after is None else [after])
    return pl.pallas_call(
        body, name=name, out_shape=jax.ShapeDtypeStruct((r, cdim), F32), grid=(r // tr,),
        in_specs=[pl.BlockSpec((n, tr, cdim), lambda i: (0, i, 0))] + [ANY] * (len(ins) - 1),
        out_specs=pl.BlockSpec((tr, cdim), lambda i: (i, 0)), compiler_params=_params(("parallel",)),
    )(*ins)


def _mm_tn(a, b, *, name, after=None, tm=512, tn=1024, out_rows=None, row0=0, prev=None):
    k, m = a.shape
    n = b.shape[1]
    tm, tn = _tile(m, tm), _tile(n, tn)
    out_rows = m if out_rows is None else out_rows

    def body(a_ref, b_ref, *rest):
        o_ref, at_ref = rest[-2], rest[-1]

        @pl.when(pl.program_id(1) == 0)
        def _():
            at_ref[...] = a_ref[...].astype(BF16).T

        o_ref[...] = _dot(at_ref[...], b_ref[...].astype(BF16), 1, 0).astype(BF16)

    ins = [a, b] + [t for t in (after, prev) if t is not None]
    return pl.pallas_call(
        body, name=name, out_shape=jax.ShapeDtypeStruct((out_rows, n), BF16), grid=(m // tm, n // tn),
        in_specs=[pl.BlockSpec((k, tm), lambda i, j: (0, i)), pl.BlockSpec((k, tn), lambda i, j: (0, j))] + [ANY] * (len(ins) - 2),
        out_specs=pl.BlockSpec((tm, tn), lambda i, j: (row0 // tm + i, j)),
        input_output_aliases={} if prev is None else {len(ins) - 1: 0},
        scratch_shapes=[pltpu.VMEM((tm, k), BF16)], compiler_params=_params(("parallel", "arbitrary")),
    )(*ins)


def _mm(a, b, *, name, ta=False, tb=False, res=None, colscale=None, emit_acc=False,
        out_dtype=F32, tm=512, tn=512, b_rows=None):
    m, k = (a.shape[1], a.shape[0]) if ta else a.shape
    n = b.shape[0] if tb else b.shape[1]
    b_start = 0
    if b_rows is not None:
        b_start, n = b_rows
    tm, tn = _tile(m, tm), _tile(n, tn)
    ca, cb = (0 if ta else 1), (1 if tb else 0)
    a_spec = pl.BlockSpec((k, tm), lambda i, j: (0, i)) if ta else pl.BlockSpec((tm, k), lambda i, j: (i, 0))
    b_spec = (pl.BlockSpec((tn, k), lambda i, j: (b_start // tn + j, 0)) if tb
              else pl.BlockSpec((k, tn), lambda i, j: (0, j)))
    tile = pl.BlockSpec((tm, tn), lambda i, j: (i, j))
    ins, in_specs = [a, b], [a_spec, b_spec]
    if res is not None:
        ins.append(res)
        in_specs.append(tile)
    if colscale is not None:
        ins.append(colscale)
        in_specs.append(pl.BlockSpec((1, tn), lambda i, j: (0, j)))
    n_in = len(ins)

    def body(*refs):
        outs = refs[n_in:]
        acc = _dot(refs[0][...].astype(BF16), refs[1][...].astype(BF16), ca, cb)
        val, p = acc, 2
        if res is not None:
            r_val, p = refs[p][...], p + 1
        if colscale is not None:
            val = val * refs[p][...]
        if res is not None:
            val = r_val + val
        if emit_acc:
            outs[0][...] = acc
        outs[-1][...] = val.astype(out_dtype)

    out_shape = [jax.ShapeDtypeStruct((m, n), out_dtype)]
    out_specs = [tile]
    if emit_acc:
        out_shape.insert(0, jax.ShapeDtypeStruct((m, n), F32))
        out_specs.insert(0, tile)
    out = pl.pallas_call(
        body, name=name, out_shape=out_shape, grid=(m // tm, n // tn), in_specs=in_specs, out_specs=out_specs,
        compiler_params=_params(("parallel", "parallel")),
    )(*ins)
    return out if emit_acc else out[0]


def _norm_fwd(x, g, scale, shift, name, after=None):
    s, d = x.shape
    tr = 256

    def body(x_ref, g_ref, sc_ref, sh_ref, *rest):
        xv = x_ref[...]
        rstd = lax.rsqrt(jnp.mean(xv * xv, axis=-1, keepdims=True) + RMS_EPS)
        rest[-1][...] = (xv * rstd * g_ref[...] * (1.0 + sc_ref[...]) + sh_ref[...]).astype(BF16)

    rowspec = pl.BlockSpec((1, d), lambda i: (0, 0))
    ins = [x, g, scale, shift] + ([] if after is None else [after])
    return pl.pallas_call(
        body, name=name, out_shape=jax.ShapeDtypeStruct((s, d), BF16), grid=(s // tr,),
        in_specs=[pl.BlockSpec((tr, d), lambda i: (i, 0)), rowspec, rowspec, rowspec] + [ANY] * (len(ins) - 4),
        out_specs=pl.BlockSpec((tr, d), lambda i: (i, 0)),
        compiler_params=_params(("parallel",)),
    )(*ins)


def _dh_norm_bwd(a1, b1, a2, b2, x, dres, g, scale, nxt, *, name, after=None, b_rows=None):
    s, d = x.shape
    tm = 256
    n_fixed = 8

    def body(a1_ref, b1_ref, a2_ref, b2_ref, x_ref, dr_ref, g_ref, sc_ref, *rest):
        rest = rest[(1 if after is not None else 0):]
        if nxt is not None:
            f_ref, cs_ref, dx_ref, sa_ref, sb_ref, df_ref, dg_ref = rest
        else:
            dx_ref, sa_ref, sb_ref = rest

        @pl.when(pl.program_id(0) == 0)
        def _():
            sa_ref[...] = jnp.zeros_like(sa_ref)
            sb_ref[...] = jnp.zeros_like(sb_ref)
            if nxt is not None:
                dg_ref[...] = jnp.zeros_like(dg_ref)

        dhv = (_dot(a1_ref[...].astype(BF16), b1_ref[...], 1, 0) + _dot(a2_ref[...].astype(BF16), b2_ref[...], 1, 0))
        xv = x_ref[...]
        rstd = lax.rsqrt(jnp.mean(xv * xv, axis=-1, keepdims=True) + RMS_EPS)
        xhat = xv * rstd
        dxhat = dhv * (g_ref[...] * (1.0 + sc_ref[...]))
        mean_term = jnp.mean(dxhat * xhat, axis=-1, keepdims=True)
        dxv = dr_ref[...] + rstd * (dxhat - xhat * mean_term)
        dx_ref[...] = dxv
        sa_ref[...] += jnp.sum(dhv, axis=0, keepdims=True)
        sb_ref[...] += jnp.sum(dhv * xhat, axis=0, keepdims=True)
        if nxt is not None:
            df_ref[...] = (dxv * cs_ref[...]).astype(BF16)
            dg_ref[...] += nxt[2] * jnp.sum(dxv * f_ref[...], axis=0, keepdims=True)

    def a_spec(t):
        return pl.BlockSpec((tm, t.shape[1]), lambda i: (i, 0))

    def b_spec(t, a, which):
        if b_rows is None:
            return pl.BlockSpec((t.shape[0], d), lambda i: (0, 0))
        start = b_rows[which]
        return pl.BlockSpec((pl.Element(a.shape[1]), pl.Element(d)), lambda i: (start, 0))

    rowspec = pl.BlockSpec((1, d), lambda i: (0, 0))
    tile = pl.BlockSpec((tm, d), lambda i: (i, 0))
    ins = [a1, b1, a2, b2, x, dres, g, scale] + ([] if after is None else [after])
    in_specs = [a_spec(a1), b_spec(b1, a1, 0), a_spec(a2), b_spec(b2, a2, 1), tile, tile, rowspec, rowspec]
    in_specs += [ANY] * (len(ins) - n_fixed)
    out_shape = [jax.ShapeDtypeStruct((s, d), F32), jax.ShapeDtypeStruct((1, d), F32), jax.ShapeDtypeStruct((1, d), F32)]
    out_specs = [tile, rowspec, rowspec]
    if nxt is not None:
        ins += [nxt[0], nxt[1]]
        in_specs += [tile, rowspec]
        out_shape += [jax.ShapeDtypeStruct((s, d), BF16), jax.ShapeDtypeStruct((1, d), F32)]
        out_specs += [tile, rowspec]
    out = pl.pallas_call(
        body, name=name, out_shape=out_shape, grid=(s // tm,), in_specs=in_specs, out_specs=out_specs,
        compiler_params=_params(("arbitrary",)),
    )(*ins)
    return out[0], out[1], out[2], (None if nxt is None else (out[3], out[4]))


def _gate_bwd(dxn, f, colscale, coef, name):
    s, d = dxn.shape
    tr = 256

    def body(dx_ref, f_ref, cs_ref, df_ref, dg_ref):
        @pl.when(pl.program_id(0) == 0)
        def _():
            dg_ref[...] = jnp.zeros_like(dg_ref)

        dxv = dx_ref[...]
        df_ref[...] = (dxv * cs_ref[...]).astype(BF16)
        dg_ref[...] += coef * jnp.sum(dxv * f_ref[...], axis=0, keepdims=True)

    rowspec = pl.BlockSpec((1, d), lambda i: (0, 0))
    tile = pl.BlockSpec((tr, d), lambda i: (i, 0))
    return pl.pallas_call(
        body, name=name, out_shape=[jax.ShapeDtypeStruct((s, d), BF16), jax.ShapeDtypeStruct((1, d), F32)],
        grid=(s // tr,), in_specs=[tile, tile, rowspec], out_specs=[tile, rowspec],
        compiler_params=_params(("arbitrary",)),
    )(dxn, f, colscale)


def _ffn_up(h, wg, wu, name, tm=SEQ, tn=256):
    s, d = h.shape
    f = wg.shape[0]

    def body(h_ref, wg_ref, wu_ref, a_ref, u_ref, s_ref):
        hv = h_ref[...]
        a = _dot(hv, wg_ref[...], 1, 1)
        u = _dot(hv, wu_ref[...], 1, 1)
        a_ref[...] = a.astype(BF16)
        u_ref[...] = u.astype(BF16)
        s_ref[...] = (a * _sigmoid(a) * u).astype(BF16)

    tile = pl.BlockSpec((tm, tn), lambda i, j: (i, j))
    wspec = pl.BlockSpec((tn, d), lambda i, j: (j, 0))
    return pl.pallas_call(
        body, name=name,
        out_shape=[jax.ShapeDtypeStruct((s, f), BF16), jax.ShapeDtypeStruct((s, f), BF16), jax.ShapeDtypeStruct((s, f), BF16)],
        grid=(s // tm, f // tn), in_specs=[pl.BlockSpec((tm, d), lambda i, j: (i, 0)), wspec, wspec],
        out_specs=[tile, tile, tile], compiler_params=_params(("parallel", "parallel")),
    )(h, wg, wu)


def _ffn_bwd_ds(df, wd, a, u, name, tm=SEQ, tn=256):
    s, d = df.shape
    f = wd.shape[0]

    def body(df_ref, wd_ref, a_ref, u_ref, da_ref, du_ref):
        ds = _dot(df_ref[...], wd_ref[...], 1, 1)
        av = a_ref[...].astype(F32)
        sg = _sigmoid(av)
        da_ref[...] = (ds * u_ref[...].astype(F32) * (sg * (1.0 + av * (1.0 - sg)))).astype(BF16)
        du_ref[...] = (ds * (av * sg)).astype(BF16)

    tile = pl.BlockSpec((tm, tn), lambda i, j: (i, j))
    return pl.pallas_call(
        body, name=name, out_shape=[jax.ShapeDtypeStruct((s, f), BF16), jax.ShapeDtypeStruct((s, f), BF16)],
        grid=(s // tm, f // tn),
        in_specs=[pl.BlockSpec((tm, d), lambda i, j: (i, 0)), pl.BlockSpec((tn, d), lambda i, j: (j, 0)), tile, tile],
        out_specs=[tile, tile], compiler_params=_params(("parallel", "parallel")),
    )(df, wd, a, u)


def _merge_fwd(o_sb, o_dil, o_swa, gates, wb_sb, wb_dil, wb_swa, name):
    s, d = SEQ, D_MODEL
    tm = 256

    def body(osb_ref, odl_ref, osw_ref, g_ref, wsb_ref, wdl_ref, wsw_ref, m_ref, tsb_ref, tdl_ref, tsw_ref):
        for h in range(osb_ref.shape[0]):
            tsb_ref[:, h * HEAD_DIM:(h + 1) * HEAD_DIM] = osb_ref[h].astype(BF16)
        for h in range(osw_ref.shape[0]):
            tsw_ref[:, h * HEAD_DIM:(h + 1) * HEAD_DIM] = osw_ref[h].astype(BF16)
        tdl_ref[...] = odl_ref[...].astype(BF16)
        acc = _sigmoid(g_ref[:, 0:d]) * _dot(tsb_ref[...], wsb_ref[...], 1, 0)
        acc += _sigmoid(g_ref[:, d:2 * d]) * _dot(tdl_ref[...], wdl_ref[...], 1, 0)
        acc += _sigmoid(g_ref[:, 2 * d:3 * d]) * _dot(tsw_ref[...], wsw_ref[...], 1, 0)
        m_ref[...] = acc.astype(BF16)

    def rows(w):
        return pl.BlockSpec((tm, w), lambda i: (i, 0))

    def heads(n):
        return pl.BlockSpec((n, tm, HEAD_DIM), lambda i: (0, i, 0))

    def whole(w):
        return pl.BlockSpec((w, d), lambda i: (0, 0))

    return pl.pallas_call(
        body, name=name, out_shape=[jax.ShapeDtypeStruct((s, w), BF16) for w in (d, 256, 128, 384)], grid=(s // tm,),
        in_specs=[heads(H_SB), rows(128), heads(H_SWA_Q), rows(3 * d), whole(256), whole(128), whole(384)],
        out_specs=[rows(d), rows(256), rows(128), rows(384)], compiler_params=_params(("parallel",)),
    )(o_sb, o_dil, o_swa, gates, wb_sb, wb_dil, wb_swa)


def _merge_bwd(dmerged, t_sb, t_dil, t_swa, gates, wb_sb, wb_dil, wb_swa, name):
    s, d = SEQ, D_MODEL
    tm = 256

    def body(dm_ref, tsb_ref, tdl_ref, tsw_ref, g_ref, wsb_ref, wdl_ref, wsw_ref,
             dg_ref, dosb_ref, dodl_ref, dosw_ref, dbsb_ref, dbdl_ref, dbsw_ref):
        dm = dm_ref[...]
        for idx, (t_ref, w_ref, do_ref, db_ref) in enumerate((
                (tsb_ref, wsb_ref, dosb_ref, dbsb_ref), (tdl_ref, wdl_ref, dodl_ref, dbdl_ref),
                (tsw_ref, wsw_ref, dosw_ref, dbsw_ref))):
            w = w_ref[...]
            br = _dot(t_ref[...], w, 1, 0)
            sg = _sigmoid(g_ref[:, idx * d:(idx + 1) * d])
            dbr = (dm * sg).astype(BF16)
            dg_ref[:, idx * d:(idx + 1) * d] = (dm * br * (sg * (1.0 - sg))).astype(BF16)
            db_ref[...] = dbr
            do = _dot(dbr, w, 1, 1)
            if len(do_ref.shape) == 2:
                do_ref[...] = do
            else:
                for h in range(do_ref.shape[0]):
                    do_ref[h] = do[:, h * HEAD_DIM:(h + 1) * HEAD_DIM]

    def rows(w):
        return pl.BlockSpec((tm, w), lambda i: (i, 0))

    def heads(n):
        return pl.BlockSpec((n, tm, HEAD_DIM), lambda i: (0, i, 0))

    def whole(w):
        return pl.BlockSpec((w, d), lambda i: (0, 0))

    def shp(w, dt):
        return jax.ShapeDtypeStruct((s, w), dt)

    def hshp(n):
        return jax.ShapeDtypeStruct((n, s, HEAD_DIM), F32)

    return pl.pallas_call(
        body, name=name,
        out_shape=[shp(3 * d, BF16), hshp(H_SB), shp(128, F32), hshp(H_SWA_Q), shp(d, BF16), shp(d, BF16), shp(d, BF16)],
        grid=(s // tm,),
        in_specs=[rows(d), rows(256), rows(128), rows(384), rows(3 * d), whole(256), whole(128), whole(384)],
        out_specs=[rows(3 * d), heads(H_SB), rows(128), heads(H_SWA_Q), rows(d), rows(d), rows(d)],
        compiler_params=_params(("parallel",)),
    )(dmerged, t_sb, t_dil, t_swa, gates, wb_sb, wb_dil, wb_swa)


def _final_loss(x, target, g, name):
    s, d = x.shape
    tr = 256

    def body(x_ref, t_ref, g_ref, loss_ref, dx_ref, dg_ref):
        @pl.when(pl.program_id(0) == 0)
        def _():
            loss_ref[...] = jnp.zeros_like(loss_ref)
            dg_ref[...] = jnp.zeros_like(dg_ref)

        xv = x_ref[...]
        gv = g_ref[...]
        rstd = lax.rsqrt(jnp.mean(xv * xv, axis=-1, keepdims=True) + RMS_EPS)
        xhat = xv * rstd
        err = xhat * gv - t_ref[...]
        loss_ref[...] += 0.5 * jnp.sum(jnp.mean(err * err, axis=-1, keepdims=True))
        dy = err * (1.0 / d)
        dxhat = dy * gv
        mean_term = jnp.mean(dxhat * xhat, axis=-1, keepdims=True)
        dx_ref[...] = rstd * (dxhat - xhat * mean_term)
        dg_ref[...] += jnp.sum(dy * xhat, axis=0, keepdims=True)

    rowspec = pl.BlockSpec((1, d), lambda i: (0, 0))
    tile = pl.BlockSpec((tr, d), lambda i: (i, 0))
    return pl.pallas_call(
        body, name=name,
        out_shape=[jax.ShapeDtypeStruct((1, LANES), F32), jax.ShapeDtypeStruct((s, d), F32), jax.ShapeDtypeStruct((1, d), F32)],
        grid=(s // tr,), in_specs=[tile, tile, rowspec],
        out_specs=[pl.BlockSpec((1, LANES), lambda i: (0, 0)), tile, rowspec],
        compiler_params=_params(("arbitrary",)),
    )(x, target, g)


def _adamw(w, g, m, v, name, after=None):
    shape = w.shape
    cols = shape[-1]
    rows = int(np.prod(shape[:-1])) if len(shape) > 1 else 1
    tr = rows
    for cand in (1024, 512, 256, 128, 64, 32, 16, 8):
        if rows % cand == 0 and rows > cand and cand * cols * 4 <= (1 << 21):
            tr = cand
            break

    def body(w_ref, g_ref, m_ref, v_ref, *rest):
        d_ref, nm_ref, nv_ref = rest[-3:]
        d_ref[...], nm_ref[...], nv_ref[...] = _adam_update(w_ref[...], g_ref[...], m_ref[...], v_ref[...])

    tile = pl.BlockSpec((tr, cols), lambda i: (i, 0))
    flat = [t.reshape(rows, cols) for t in (w, g, m, v)] + ([] if after is None else [after])
    out = pl.pallas_call(
        body, name=name, out_shape=[jax.ShapeDtypeStruct((rows, cols), F32)] * 3, grid=(rows // tr,),
        in_specs=[tile] * 4 + [ANY] * (len(flat) - 4), out_specs=[tile] * 3, compiler_params=_params(("parallel",)),
    )(*flat)
    return tuple(t.reshape(shape) for t in out)


def _adam_update(w, gv, m, v):
    nm = ADAM_B1 * m + (1.0 - ADAM_B1) * gv
    nv = ADAM_B2 * v + (1.0 - ADAM_B2) * (gv * gv)
    m_hat = nm / (1.0 - ADAM_B1 ** ADAM_STEP)
    v_hat = nv / (1.0 - ADAM_B2 ** ADAM_STEP)
    return -ADAM_LR * (m_hat / (jnp.sqrt(v_hat) + ADAM_EPS) + ADAM_WD * w), nm, nv


def _reduce_adamw(groups, w, m, v, row0, prev, name, after=None):
    n, r, cdim = groups[0].shape
    rows = w.shape[0]
    tr = _row_tile(r, max(16, (1 << 22) // (n * cdim * groups[0].dtype.itemsize)))
    steps = r // tr
    ng = len(groups)

    def body(*refs):
        w_ref, m_ref, v_ref = refs[ng:ng + 3]
        g_out, d_out, m_out, v_out = refs[-4:]
        gg = pl.program_id(0)
        for gi in range(ng):
            @pl.when(gg == gi)
            def _(gi=gi):
                acc = refs[gi][0].astype(F32)
                for k in range(1, n):
                    acc = acc + refs[gi][k].astype(F32)
                g_out[...] = acc
                d_out[...], m_out[...], v_out[...] = _adam_update(w_ref[...], acc, m_ref[...], v_ref[...])

    def part_spec(gi):
        return pl.BlockSpec((n, tr, cdim), lambda gg, i: (0, jnp.where(gg == gi, i, 0), 0))

    tile = pl.BlockSpec((tr, cdim), lambda gg, i: (row0 // tr + gg * steps + i, 0))
    extra = ([] if prev is None else list(prev)) + ([] if after is None else [after])
    return pl.pallas_call(
        body, name=name, out_shape=[jax.ShapeDtypeStruct((rows, cdim), F32)] * 4, grid=(ng, steps),
        in_specs=[part_spec(gi) for gi in range(ng)] + [tile] * 3 + [ANY] * len(extra), out_specs=[tile] * 4,
        input_output_aliases={} if prev is None else {ng + 3 + k: k for k in range(4)},
        compiler_params=_params(("parallel", "parallel")),
    )(*groups, w, m, v, *extra)


def _ada_fwd(c_all, w, name):
    n = w.shape[1]

    def body(c_ref, w_ref, o_ref):
        cv = c_ref[...]
        o_ref[...] = jnp.dot(cv * _sigmoid(cv), w_ref[...], preferred_element_type=F32, precision=lax.Precision.HIGHEST)

    return pl.pallas_call(body, name=name, out_shape=jax.ShapeDtypeStruct((N_DEV, n), F32), compiler_params=_params())(c_all, w)


def _ada_bwd(c_all_t, dmod, name):
    n = dmod.shape[1]

    def body(c_ref, d_ref, o_ref):
        cv = c_ref[...]
        o_ref[...] = jnp.dot(cv * _sigmoid(cv), d_ref[...], preferred_element_type=F32, precision=lax.Precision.HIGHEST)

    return pl.pallas_call(body, name=name, out_shape=jax.ShapeDtypeStruct((D_MODEL, n), F32), compiler_params=_params())(c_all_t, dmod)


def _bucket_tables():
    rel = np.arange(BLK)[:, None] + BLK - np.arange(2 * BLK)[None, :]
    max_exact = N_BUCKETS // 2

    def bucket(n):
        nf = np.maximum(n, 1).astype(np.float32)
        large = max_exact + (np.log(nf / np.float32(max_exact)) / np.float32(math.log(MAX_REL_DIST / max_exact))
                             * np.float32(N_BUCKETS - max_exact)).astype(np.int32)
        return np.where(n < max_exact, n, np.minimum(large, N_BUCKETS - 1))

    tabs = []
    for dil, max_dist in ((1, 128), (4, 128), (16, 128), (1, SWA_WINDOW - 1)):
        in_band = (rel >= 0) & (rel <= max_dist)
        tabs.append(np.where(in_band, bucket(np.maximum(rel, 0) * dil), -1))
    return np.stack(tabs).astype(np.int32)


N_SOFT = H_DIL + H_SWA_Q


def _table_of_head(h):
    return jnp.minimum(h // 2, 3)


def _bias_build(rel_bias, tables, name):
    def body(rel_ref, t_ref, o_ref):
        h = pl.program_id(0)
        tb = t_ref[0]
        out = jnp.full((BLK, 2 * BLK), NEG, F32)
        for b in range(N_BUCKETS):
            out = jnp.where(tb == b, rel_ref[b, h], out)
        o_ref[0] = out

    return pl.pallas_call(
        body, name=name, out_shape=jax.ShapeDtypeStruct((N_SOFT, BLK, 2 * BLK), F32), grid=(N_SOFT,),
        in_specs=[pl.BlockSpec(memory_space=pltpu.SMEM),
                  pl.BlockSpec((1, BLK, 2 * BLK), lambda h: (_table_of_head(h), 0, 0))],
        out_specs=pl.BlockSpec((1, BLK, 2 * BLK), lambda h: (h, 0, 0)),
        compiler_params=_params(("parallel",)),
    )(rel_bias, tables)


def _bias_grad(dbias, tables, name):
    def body(d_ref, t_ref, o_ref):
        tb = t_ref[0]
        dv = d_ref[0]
        lane = lax.broadcasted_iota(jnp.int32, (1, LANES), 1)
        out = jnp.zeros((1, LANES), F32)
        for b in range(N_BUCKETS):
            out = jnp.where(lane == b, jnp.sum(jnp.where(tb == b, dv, 0.0)), out)
        o_ref[0] = out

    return pl.pallas_call(
        body, name=name, out_shape=jax.ShapeDtypeStruct((N_SOFT, 1, LANES), F32), grid=(N_SOFT,),
        in_specs=[pl.BlockSpec((1, BLK, 2 * BLK), lambda h: (h, 0, 0)),
                  pl.BlockSpec((1, BLK, 2 * BLK), lambda h: (_table_of_head(h), 0, 0))],
        out_specs=pl.BlockSpec((1, 1, LANES), lambda h: (h, 0, 0)),
        compiler_params=_params(("parallel",)),
    )(dbias, tables)


def _band_layout(g, bias_div):
    assert g == 1 or bias_div == 1
    return bias_div if g == 1 else 1


def _band_specs(length, g, bias_div, offs):
    ns = _band_layout(g, bias_div)

    def seqs(off, div=1):
        return pl.BlockSpec((ns, length, HEAD_DIM), lambda s: (off // ns + s // div, 0, 0))

    xspecs = [seqs(offs[0]), seqs(offs[1], g), seqs(offs[2], g)]
    bspec = pl.BlockSpec((1, BLK, 2 * BLK), lambda s: (s, 0, 0))
    sspec = pl.BlockSpec((ns, 1, LANES), lambda s: (s, 0, 0))
    colspec = pl.BlockSpec((ns, length, 1), lambda s: (s, 0, 0))
    return xspecs, seqs(0), seqs(0, g), bspec, sspec, colspec


def _band_sweep(length, ns, one):
    nblk = length // BLK
    for qq in range(ns):
        if ns * nblk <= 16:
            for i in range(nblk):
                one(qq, i * BLK, max(i - 1, 0) * BLK, i == 0)
        else:
            def step(i, carry, qq=qq):
                one(qq, pl.multiple_of(i * BLK, BLK), pl.multiple_of(jnp.maximum(i - 1, 0) * BLK, BLK), i == 0)
                return carry

            lax.fori_loop(0, nblk, step, 0, unroll=2)


def _band_scores(q_ref, k_ref, b_ref, qq, kq, bq, cur, prv, first):
    qv = q_ref[qq, pl.ds(cur, BLK), :]
    bv = b_ref[bq]
    if first is True:
        sp = jnp.full((BLK, BLK), NEG, F32)
    else:
        sp = _dot(qv, k_ref[kq, pl.ds(prv, BLK), :], 1, 1) + bv[:, :BLK]
        sp = sp if first is False else jnp.where(first, NEG, sp)
    sc = _dot(qv, k_ref[kq, pl.ds(cur, BLK), :], 1, 1) + bv[:, BLK:]
    return qv, sp, sc


def _band_fwd(x, bias, sink, *, nq, offs, g, bias_div, has_sink, name):
    length = x.shape[1]
    ns = _band_layout(g, bias_div)

    def body(q_ref, k_ref, v_ref, b_ref, s_ref, o_ref, lse_ref):
        def one(qq, cur, prv, first):
            kq, bq = qq, 0
            _, sp, sc = _band_scores(q_ref, k_ref, b_ref, qq, kq, bq, cur, prv, first)
            m = jnp.maximum(jnp.max(sp, axis=1, keepdims=True), jnp.max(sc, axis=1, keepdims=True))
            if has_sink:
                sk = s_ref[qq][:, :1]
                m = jnp.maximum(m, sk)
            pp, pc = jnp.exp(sp - m), jnp.exp(sc - m)
            den = jnp.sum(pp, axis=1, keepdims=True) + jnp.sum(pc, axis=1, keepdims=True)
            if has_sink:
                den = den + jnp.exp(sk - m)
            acc = (_dot(pp.astype(BF16), v_ref[kq, pl.ds(prv, BLK), :], 1, 0)
                   + _dot(pc.astype(BF16), v_ref[kq, pl.ds(cur, BLK), :], 1, 0))
            o_ref[qq, pl.ds(cur, BLK), :] = acc / den
            lse_ref[qq, pl.ds(cur, BLK), :] = m + jnp.log(den)

        _band_sweep(length, ns, one)

    xspecs, qspec, _, bspec, sspec, colspec = _band_specs(length, g, bias_div, offs)
    return pl.pallas_call(
        body, name=name,
        out_shape=[jax.ShapeDtypeStruct((nq, length, HEAD_DIM), F32), jax.ShapeDtypeStruct((nq, length, 1), F32)],
        grid=(nq // ns,), in_specs=xspecs + [bspec, sspec],
        out_specs=[qspec, colspec], compiler_params=_params(("parallel",)),
    )(x, x, x, bias, sink)


def _band_bwd(x, bias, sink, o, lse, do, dlse, *, nq, offs, g, bias_div, has_sink, name):
    length = x.shape[1]
    ns = _band_layout(g, bias_div)
    nk, nbias = nq // g, nq // bias_div

    def body(q_ref, k_ref, v_ref, b_ref, s_ref, o_ref, lse_ref, do_ref, dlse_ref,
             dq_ref, dk_ref, dv_ref, db_ref, dsk_ref, dkp_ref, dvp_ref):
        for ref in (db_ref, dsk_ref, dkp_ref, dvp_ref):
            ref[...] = jnp.zeros_like(ref)

        @pl.when(pl.program_id(0) % g == 0)
        def _():
            dk_ref[...] = jnp.zeros_like(dk_ref)
            dv_ref[...] = jnp.zeros_like(dv_ref)

        def one(qq, cur, prv, first):
            kq, bq = qq, 0
            qv, sp, sc = _band_scores(q_ref, k_ref, b_ref, qq, kq, bq, cur, prv, first)
            rows, prow = pl.ds(cur, BLK), pl.ds(prv, BLK)
            lse_v = lse_ref[qq, rows, :]
            pp, pc = jnp.exp(sp - lse_v), jnp.exp(sc - lse_v)
            dov = do_ref[qq, rows, :]
            dob = dov.astype(BF16)
            coef = dlse_ref[qq, rows, :] - jnp.sum(dov * o_ref[qq, rows, :], axis=1, keepdims=True)
            dsp = pp * (_dot(dob, v_ref[kq, prow, :], 1, 1) + coef)
            dsc = pc * (_dot(dob, v_ref[kq, rows, :], 1, 1) + coef)
            dspb, dscb = dsp.astype(BF16), dsc.astype(BF16)
            dq_ref[qq, rows, :] = ((_dot(dspb, k_ref[kq, prow, :], 1, 0) + _dot(dscb, k_ref[kq, rows, :], 1, 0))
                                   * (HEAD_DIM ** -0.5))
            dk_ref[kq, rows, :] += _dot(dscb, qv, 0, 0)
            dkp_ref[kq, prow, :] += _dot(dspb, qv, 0, 0)
            dv_ref[kq, rows, :] += _dot(pc.astype(BF16), dob, 0, 0)
            dvp_ref[kq, prow, :] += _dot(pp.astype(BF16), dob, 0, 0)
            db_ref[bq, :, :BLK] += dsp
            db_ref[bq, :, BLK:] += dsc
            if has_sink:
                dsk_ref[qq] += jnp.sum(jnp.exp(s_ref[qq][:, :1] - lse_v) * coef)

        _band_sweep(length, ns, one)
        dk_ref[...] += dkp_ref[...]
        dv_ref[...] += dvp_ref[...]

    xspecs, qspec, kvspec, bspec, sspec, colspec = _band_specs(length, g, bias_div, offs)
    return pl.pallas_call(
        body, name=name,
        out_shape=[jax.ShapeDtypeStruct((nq, length, HEAD_DIM), F32), jax.ShapeDtypeStruct((nk, length, HEAD_DIM), F32),
                   jax.ShapeDtypeStruct((nk, length, HEAD_DIM), F32), jax.ShapeDtypeStruct((nbias, BLK, 2 * BLK), F32),
                   jax.ShapeDtypeStruct((nq, 1, LANES), F32)],
        grid=(nq // ns,),
        in_specs=xspecs + [bspec, sspec, qspec, colspec, qspec, colspec],
        out_specs=[qspec, kvspec, kvspec, bspec, sspec],
        scratch_shapes=[pltpu.VMEM((ns, length, HEAD_DIM), F32), pltpu.VMEM((ns, length, HEAD_DIM), F32)],
        compiler_params=_params(("arbitrary",)),
    )(x, x, x, bias, sink, o, lse, do, dlse)


TOK_TILE = 512


def _dil_merge(outs, lses, dout, name):
    tr = TOK_TILE
    dils = [d for _, d in DIL_PATTERNS]
    n = len(dils)
    o4 = [o.reshape(2, d, SEQ // d, HEAD_DIM) for o, d in zip(outs, dils)]
    l4 = [l.reshape(2, d, SEQ // d, 1) for l, d in zip(lses, dils)]
    o_specs = [pl.BlockSpec((2, d, tr // d, HEAD_DIM), lambda i: (0, 0, i, 0)) for d in dils]
    l_specs = [pl.BlockSpec((2, d, tr // d, 1), lambda i: (0, 0, i, 0)) for d in dils]
    tok = pl.BlockSpec((tr, 2 * HEAD_DIM), lambda i: (i, 0))
    scratch = ([pltpu.VMEM((tr, 2 * HEAD_DIM), F32) for _ in dils] + [pltpu.VMEM((tr, 1), F32) for _ in range(2 * n)]
               + [pltpu.VMEM((tr // d, 2 * HEAD_DIM), F32) for d in dils])

    def to_tokens(o_ref, l_ref, d, pair, cols, stage):
        for r in range(d):
            rows = pl.ds(r, tr // d, stride=d) if d > 1 else slice(None)
            stage[:, :HEAD_DIM] = o_ref[0, r]
            stage[:, HEAD_DIM:] = o_ref[1, r]
            pair[rows, :] = stage[...]
            for h in range(2):
                cols[h][rows, :] = l_ref[h, r]
        return pair[...], [cols[0][...], cols[1][...]]

    def weights(ls):
        left = lax.broadcasted_iota(jnp.int32, (tr, 2 * HEAD_DIM), 1) < HEAD_DIM
        per_head = []
        for h in range(2):
            m = ls[0][h]
            for g in range(1, n):
                m = jnp.maximum(m, ls[g][h])
            es = [jnp.exp(ls[g][h] - m) for g in range(n)]
            den = es[0]
            for e in es[1:]:
                den = den + e
            per_head.append([e / den for e in es])
        return per_head, [jnp.where(left, per_head[0][g], per_head[1][g]) for g in range(n)], left

    def load(refs):
        pairs, cols, stages = refs[:n], refs[n:3 * n], refs[3 * n:]
        return pairs, [cols[2 * g:2 * g + 2] for g in range(n)], stages

    if dout is None:
        def body(*refs):
            pairs, cols, stages = load(refs[2 * n + 1:])
            toks = [to_tokens(refs[g], refs[n + g], dils[g], pairs[g], cols[g], stages[g]) for g in range(n)]
            _, alphas, _ = weights([t[1] for t in toks])
            acc = alphas[0] * toks[0][0]
            for g in range(1, n):
                acc = acc + alphas[g] * toks[g][0]
            refs[2 * n][...] = acc

        return pl.pallas_call(
            body, name=name, out_shape=jax.ShapeDtypeStruct((SEQ, 2 * HEAD_DIM), F32), grid=(SEQ // tr,),
            in_specs=o_specs + l_specs, out_specs=tok, scratch_shapes=scratch, compiler_params=_params(("parallel",)),
        )(*o4, *l4)

    def body(*refs):
        do_refs, dl_refs = refs[2 * n + 1:3 * n + 1], refs[3 * n + 1:4 * n + 1]
        pairs, cols, stages = load(refs[4 * n + 1:])
        toks = [to_tokens(refs[g], refs[n + g], dils[g], pairs[g], cols[g], stages[g]) for g in range(n)]
        per_head, alphas, left = weights([t[1] for t in toks])
        dov = refs[2 * n][...]
        das = []
        for g in range(n):
            prod = dov * toks[g][0]
            das.append([jnp.sum(jnp.where(left, prod, 0.0), axis=1, keepdims=True),
                        jnp.sum(jnp.where(left, 0.0, prod), axis=1, keepdims=True)])
        dbar = [sum(per_head[h][g] * das[g][h] for g in range(n)) for h in range(2)]
        for g, d in enumerate(dils):
            pairs[g][...] = alphas[g] * dov
            for h in range(2):
                cols[g][h][...] = per_head[h][g] * (das[g][h] - dbar[h])
            for r in range(d):
                rows = pl.ds(r, tr // d, stride=d) if d > 1 else slice(None)
                v = pairs[g][rows, :]
                for h in range(2):
                    do_refs[g][h, r] = v[:, h * HEAD_DIM:(h + 1) * HEAD_DIM]
                    dl_refs[g][h, r] = cols[g][h][rows, :]

    out = pl.pallas_call(
        body, name=name,
        out_shape=[jax.ShapeDtypeStruct(o.shape, F32) for o in o4] + [jax.ShapeDtypeStruct(l.shape, F32) for l in l4],
        grid=(SEQ // tr,), in_specs=o_specs + l_specs + [tok], out_specs=o_specs + l_specs, scratch_shapes=scratch,
        compiler_params=_params(("parallel",)),
    )(*o4, *l4, dout)
    return [t.reshape(s.shape) for t, s in zip(out, list(outs) + list(lses))]


def _tri(cmp):
    r = lax.broadcasted_iota(jnp.int32, (SB_TILE, SB_TILE), 0)
    c = lax.broadcasted_iota(jnp.int32, (SB_TILE, SB_TILE), 1)
    return cmp(r, c).astype(BF16)


def _cum(x, tri, terms):
    acc, rest = None, x
    for _ in range(terms):
        part = rest.astype(BF16)
        rest = rest - part.astype(F32)
        d = _dot(part, tri, 1, 0)
        acc = d if acc is None else acc + d
    return acc


def _sb_logits(q, ks, diagonal):
    t = SB_TILE
    z = _dot(q, ks, 1, 1)
    e = jnp.exp(-jnp.abs(z))
    lf = -(jnp.maximum(z, 0.0) + jnp.log(1.0 + e))
    if not diagonal:
        return z, e, lf, None
    mask = lax.broadcasted_iota(jnp.int32, (t, t), 1) < lax.broadcasted_iota(jnp.int32, (t, t), 0)
    return z, e, jnp.where(mask, lf, 0.0), mask


def _sb_specs(h, s):
    t = SB_TILE
    tile = pl.BlockSpec((h, t, HEAD_DIM), lambda i: (0, i, 0))
    keys = pl.BlockSpec((h, s, HEAD_DIM), lambda i: (1, 0, 0))
    values = pl.BlockSpec((h, s, HEAD_DIM), lambda i: (2, 0, 0))
    return tile, keys, values, pl.BlockSpec((h, t, 1), lambda i: (0, i, 0))


def _sb_fwd(x, name):
    h, s = x.shape[0] // 3, x.shape[1]
    t = SB_TILE

    def body(q_ref, k_ref, v_ref, o_ref, tot_ref):
        i = pl.program_id(0)
        after = _tri(lambda r, c: r > c)

        def tile(j, carry, diagonal):
            rows = pl.ds(pl.multiple_of(j * t, t), t)
            out = []
            for hh, (right, acc) in enumerate(carry):
                z, _, lf, mask = _sb_logits(q_ref[hh], k_ref[hh, rows, :], diagonal)
                w = jnp.exp(z + lf + (right + _cum(lf, after, 2)))
                w = w if mask is None else jnp.where(mask, w, 0.0)
                out.append((right + jnp.sum(lf, axis=1, keepdims=True), acc + _dot(w.astype(BF16), v_ref[hh, rows, :], 1, 0)))
            return tuple(out)

        carry = tile(i, tuple((jnp.zeros((t, 1), F32), jnp.zeros((t, HEAD_DIM), F32)) for _ in range(h)), True)
        carry = lax.fori_loop(0, i, lambda jj, c: tile(i - 1 - jj, c, False), carry)
        for hh, (right, acc) in enumerate(carry):
            o_ref[hh] = acc
            tot_ref[hh] = right

    tile_spec, keys, values, col = _sb_specs(h, s)
    return pl.pallas_call(
        body, name=name, out_shape=[jax.ShapeDtypeStruct((h, s, HEAD_DIM), F32), jax.ShapeDtypeStruct((h, s, 1), F32)],
        grid=(s // t,), in_specs=[tile_spec, keys, values], out_specs=[tile_spec, col],
        compiler_params=_params(("parallel",)),
    )(x, x, x)


def _sb_bwd(x, tot, do, name):
    h, s = x.shape[0] // 3, x.shape[1]
    t = SB_TILE

    def body(q_ref, k_ref, v_ref, tot_ref, do_ref, dq_ref, dk_ref, dv_ref):
        i = pl.program_id(0)

        @pl.when(i == 0)
        def _():
            dk_ref[...] = jnp.zeros_like(dk_ref)
            dv_ref[...] = jnp.zeros_like(dv_ref)

        upto = _tri(lambda r, c: r <= c)
        before = _tri(lambda r, c: r < c)

        def tile(j, carry, diagonal):
            rows = pl.ds(pl.multiple_of(j * t, t), t)
            out = []
            for hh, (left, cleft, dq) in enumerate(carry):
                qv, ks, dob = q_ref[hh], k_ref[hh, rows, :], do_ref[hh].astype(BF16)
                z, e, lf, mask = _sb_logits(qv, ks, diagonal)
                between = tot_ref[hh] - (left + _cum(lf, upto, 2))
                w = jnp.exp(z + lf + between)
                w = w if mask is None else jnp.where(mask, w, 0.0)
                dlog = w * _dot(dob, v_ref[hh, rows, :], 1, 1)
                cfail = cleft + _cum(dlog, before, 2)
                sig = jnp.where(z >= 0.0, 1.0, e) / (1.0 + e)
                dz = dlog * (1.0 - sig) - sig * cfail
                dz = (dz if mask is None else jnp.where(mask, dz, 0.0)).astype(BF16)
                dk_ref[hh, rows, :] += _dot(dz, qv, 0, 0)
                dv_ref[hh, rows, :] += _dot(w.astype(BF16), dob, 0, 0)
                out.append((left + jnp.sum(lf, axis=1, keepdims=True), cleft + jnp.sum(dlog, axis=1, keepdims=True),
                            dq + _dot(dz, ks, 1, 0)))
            return tuple(out)

        zero = jnp.zeros((t, 1), F32)
        carry = lax.fori_loop(0, i, lambda j, c: tile(j, c, False),
                              tuple((zero, zero, jnp.zeros((t, HEAD_DIM), F32)) for _ in range(h)))
        for hh, (_, _, dq) in enumerate(tile(i, carry, True)):
            dq_ref[hh] = dq * (HEAD_DIM ** -0.5)

    tile_spec, keys, values, col = _sb_specs(h, s)
    full = pl.BlockSpec((h, s, HEAD_DIM), lambda i: (0, 0, 0))
    shp = jax.ShapeDtypeStruct((h, s, HEAD_DIM), F32)
    return pl.pallas_call(
        body, name=name, out_shape=[shp, shp, shp], grid=(s // t,),
        in_specs=[tile_spec, keys, values, col, tile_spec],
        out_specs=[tile_spec, full, full], compiler_params=_params(("arbitrary",)),
    )(x, x, x, tot, do)


COL_SB, COL_DIL, COL_SWA = 0, 3 * H_SB * HEAD_DIM, 3 * H_SB * HEAD_DIM + 3 * H_DIL * HEAD_DIM
N_SWA = H_SWA_Q + 2 * H_SWA_KV


def _dil_col(t, g):
    return COL_DIL + t * H_DIL * HEAD_DIM + g * 2 * HEAD_DIM


def _split_heads(qkv, name):
    tr = TOK_TILE
    scale = HEAD_DIM ** -0.5
    dils = [d for _, d in DIL_PATTERNS]

    def body(x_ref, sb_ref, d0_ref, d1_ref, d2_ref, swa_ref, pair):
        def head(col, scaled):
            v = x_ref[:, col:col + HEAD_DIM]
            return (v * scale if scaled else v).astype(BF16)

        for hh in range(3 * H_SB):
            sb_ref[hh] = head(COL_SB + hh * HEAD_DIM, hh < H_SB)
        for hh in range(N_SWA):
            swa_ref[hh] = head(COL_SWA + hh * HEAD_DIM, hh < H_SWA_Q)
        for t in range(3):
            for g, (d, out_ref) in enumerate(zip(dils, (d0_ref, d1_ref, d2_ref))):
                col = _dil_col(t, g)
                if d == 1:
                    for h in range(2):
                        out_ref[t * 2 + h] = head(col + h * HEAD_DIM, t == 0)
                    continue
                pair[...] = x_ref[:, col:col + 2 * HEAD_DIM]
                for r in range(d):
                    v = pair[pl.ds(r, tr // d, stride=d), :]
                    v = v * scale if t == 0 else v
                    for h in range(2):
                        out_ref[t * 2 * d + h * d + r] = v[:, h * HEAD_DIM:(h + 1) * HEAD_DIM].astype(BF16)

    def heads(n, length):
        return jax.ShapeDtypeStruct((n, length, HEAD_DIM), BF16)

    def spec(n, rows):
        return pl.BlockSpec((n, rows, HEAD_DIM), lambda i: (0, i, 0))

    return pl.pallas_call(
        body, name=name,
        out_shape=[heads(3 * H_SB, SEQ)] + [heads(6 * d, SEQ // d) for d in dils] + [heads(N_SWA, SEQ)],
        grid=(SEQ // tr,), in_specs=[pl.BlockSpec((tr, D_QKV), lambda i: (i, 0))],
        out_specs=[spec(3 * H_SB, tr)] + [spec(6 * d, tr // d) for d in dils] + [spec(N_SWA, tr)],
        scratch_shapes=[pltpu.VMEM((tr, 2 * HEAD_DIM), F32)], compiler_params=_params(("parallel",)),
    )(qkv)


def _join_heads(sb, dil, swa, name):
    tr = TOK_TILE
    dils = [d for _, d in DIL_PATTERNS]

    def body(*refs):
        sb_refs, dil_refs, swa_refs = refs[:3], [refs[3 + 3 * g:6 + 3 * g] for g in range(3)], refs[12:15]
        o_ref, pair, stages = refs[15], refs[16], refs[17:]

        def put(col, v):
            o_ref[:, col:col + v.shape[1]] = v.astype(BF16)

        for t in range(3):
            for h in range(H_SB):
                put(COL_SB + (t * H_SB + h) * HEAD_DIM, sb_refs[t][h])
        col = COL_SWA
        for ref in swa_refs:
            for h in range(ref.shape[0]):
                put(col, ref[h])
                col += HEAD_DIM
        for t in range(3):
            for g, d in enumerate(dils):
                ref, col = dil_refs[g][t], _dil_col(t, g)
                if d == 1:
                    for h in range(2):
                        put(col + h * HEAD_DIM, ref[h])
                    continue
                stage = stages[g - 1]
                for r in range(d):
                    stage[:, :HEAD_DIM] = ref[r]
                    stage[:, HEAD_DIM:] = ref[d + r]
                    pair[pl.ds(r, tr // d, stride=d), :] = stage[...]
                put(col, pair[...])

    def spec(n, rows):
        return pl.BlockSpec((n, rows, HEAD_DIM), lambda i: (0, i, 0))

    ins = list(sb) + [t for g in range(3) for t in dil[g]] + list(swa)
    in_specs = ([spec(H_SB, tr)] * 3 + [spec(2 * d, tr // d) for d in dils for _ in range(3)]
                + [spec(H_SWA_Q, tr), spec(H_SWA_KV, tr), spec(H_SWA_KV, tr)])
    return pl.pallas_call(
        body, name=name, out_shape=jax.ShapeDtypeStruct((SEQ, D_QKV), BF16), grid=(SEQ // tr,), in_specs=in_specs,
        out_specs=pl.BlockSpec((tr, D_QKV), lambda i: (i, 0)),
        scratch_shapes=[pltpu.VMEM((tr, 2 * HEAD_DIM), F32)] + [pltpu.VMEM((tr // d, 2 * HEAD_DIM), F32) for d in dils[1:]],
        compiler_params=_params(("parallel",)),
    )(*ins)


def _mixer_fwd(qkv, bias, sinks_l, tag):
    sb, d0, d1, d2, swa = _split_heads(qkv, name=f"split_heads_{tag}")
    st = {"sb": sb, "dil": (d0, d1, d2), "swa": swa}
    o_sb, st["sb_tot"] = _sb_fwd(sb, name=f"sb_fwd_{tag}")
    st["dil_out"], st["dil_lse"], st["dil_sink"] = [], [], []
    for gi, (_, d) in enumerate(DIL_PATTERNS):
        sink = jnp.zeros((2 * d, 1, LANES), F32)
        og, lg = _band_fwd(st["dil"][gi], bias[2 * gi:2 * gi + 2], sink, nq=2 * d, offs=(0, 2 * d, 4 * d), g=1, bias_div=d,
                           has_sink=False, name=f"dil{gi}_fwd_{tag}")
        st["dil_out"].append(og)
        st["dil_lse"].append(lg)
        st["dil_sink"].append(sink)
    o_dil = _dil_merge(st["dil_out"], st["dil_lse"], None, name=f"dil_merge_fwd_{tag}")
    st["swa_sink"] = jnp.broadcast_to(sinks_l.reshape(H_SWA_Q, 1, 1), (H_SWA_Q, 1, LANES))
    st["swa_out"] = _band_fwd(swa, bias[H_DIL:], st["swa_sink"], nq=H_SWA_Q, offs=(0, H_SWA_Q, H_SWA_Q + H_SWA_KV),
                              g=H_SWA_Q // H_SWA_KV, bias_div=1, has_sink=True, name=f"swa_fwd_{tag}")
    return (o_sb, o_dil, st["swa_out"][0]), st


def _mixer_bwd(st, bias, do_sb, do_dil, do_swa, tag):
    d_sb = _sb_bwd(st["sb"], st["sb_tot"], do_sb, name=f"sb_bwd_{tag}")
    dmerge = _dil_merge(st["dil_out"], st["dil_lse"], do_dil, name=f"dil_merge_bwd_{tag}")
    d_dil, dbs = [], []
    for gi, (_, d) in enumerate(DIL_PATTERNS):
        dq, dk, dv, db, _ = _band_bwd(st["dil"][gi], bias[2 * gi:2 * gi + 2], st["dil_sink"][gi], st["dil_out"][gi],
                                      st["dil_lse"][gi], dmerge[gi], dmerge[3 + gi], nq=2 * d, offs=(0, 2 * d, 4 * d),
                                      g=1, bias_div=d, has_sink=False, name=f"dil{gi}_bwd_{tag}")
        d_dil.append((dq, dk, dv))
        dbs.append(db)
    o_sw, l_sw = st["swa_out"]
    dq_sw, dk_sw, dv_sw, db_sw, dsink = _band_bwd(st["swa"], bias[H_DIL:], st["swa_sink"], o_sw, l_sw, do_swa,
                                                  jnp.zeros_like(l_sw), nq=H_SWA_Q, offs=(0, H_SWA_Q, H_SWA_Q + H_SWA_KV),
                                                  g=H_SWA_Q // H_SWA_KV, bias_div=1, has_sink=True, name=f"swa_bwd_{tag}")
    dqkv = _join_heads(d_sb, d_dil, (dq_sw, dk_sw, dv_sw), name=f"join_heads_{tag}")
    return dqkv, jnp.concatenate(dbs + [db_sw], 0), dsink[:, 0, 0]


PIECES = ("ffn0", "mix", "ffn1")


def _ffn_fwd(x_in, w, gain, mod_j, tag, after=None):
    st = {"x": x_in, "w": w}
    st["h"] = _norm_fwd(x_in, _row(gain), _row(mod_j[1]), _row(mod_j[0]), name=f"norm_fwd_{tag}", after=after)
    st["a"], st["u"], st["s"] = _ffn_up(st["h"], w["gate"], w["up"], name=f"up_{tag}")
    st["f"], x_out = _mm(st["s"], w["down"], res=x_in, colscale=_row(0.5 * mod_j[2]), emit_acc=True, tm=512, tn=1024,
                         name=f"down_{tag}")
    return x_out, st


def _ffn_bwd(dx_out, st, gain, mod_j, tag, done, pre, nxt):
    w = st["w"]

    def latest(new, old):
        return old if new is None else new

    df, dgate = pre or _gate_bwd(dx_out, st["f"], _row(0.5 * mod_j[2]), 0.5, name=f"gate_bwd_{tag}")
    dwd = _mm_tn(st["s"], df, tm=D_FF // 2, name=f"dwd_{tag}")
    token = latest(done({"down": dwd}), dwd)
    da, du = _ffn_bwd_ds(df, w["down"], st["a"], st["u"], name=f"ds_{tag}")
    dwg = _mm_tn(da, st["h"], after=token, tm=D_FF // 2, name=f"dwg_{tag}")
    token = latest(done({"gate": dwg}), dwg)
    dwu = _mm_tn(du, st["h"], after=token, tm=D_FF // 2, name=f"dwu_{tag}")
    token = latest(done({"up": dwu}), dwu)
    dx_in, sum_dh, sum_dhx, made = _dh_norm_bwd(da, w["gate"], du, w["up"], st["x"], dx_out, _row(gain), _row(mod_j[1]), nxt,
                                                after=token, name=f"dh_{tag}")
    dmod = jnp.concatenate([sum_dh, gain * sum_dhx, dgate], 0)
    return dx_in, dmod, (1.0 + mod_j[1]) * sum_dhx[0], made


def _mix_fwd(x_in, w, gain, mod_j, bias, sinks_l, tag, after=None):
    st = {"x": x_in, "w": w}
    st["h"] = _norm_fwd(x_in, _row(gain), _row(mod_j[1]), _row(mod_j[0]), name=f"norm_fwd_mix_{tag}", after=after)
    qkv = _mm(st["h"], w["in"], tb=True, tm=SEQ, b_rows=(0, D_QKV), name=f"qkv_{tag}")
    st["gates"] = _mm(st["h"], w["in"], tb=True, tm=SEQ, b_rows=(D_QKV, D_GATES), name=f"gates_{tag}")
    outs, st["mix"] = _mixer_fwd(qkv, bias, sinks_l, tag)
    st["merged"], *st["t"] = _merge_fwd(*outs, st["gates"], w["br_sb"], w["br_dil"], w["br_swa"], name=f"merge_fwd_{tag}")
    st["f"], x_out = _mm(st["merged"], w["out"], res=x_in, colscale=_row(mod_j[2]), emit_acc=True, name=f"out_{tag}")
    return x_out, st


def _mix_bwd(dx_out, st, gain, mod_j, bias, tag, done, pre, nxt):
    w = st["w"]
    df, dgate = pre or _gate_bwd(dx_out, st["f"], _row(mod_j[2]), 1.0, name=f"gate_bwd_mix_{tag}")
    g = {"out": _mm_tn(st["merged"], df, name=f"dw_out_{tag}")}
    dmerged = _mm(df, w["out"], tb=True, name=f"dmerged_{tag}")
    dgates, do_sb, do_dil, do_swa, dbr_sb, dbr_dil, dbr_swa = _merge_bwd(
        dmerged, *st["t"], st["gates"], w["br_sb"], w["br_dil"], w["br_swa"], name=f"merge_bwd_{tag}")
    g["br_sb"] = _mm_tn(st["t"][0], dbr_sb, name=f"dw_br_sb_{tag}")
    g["br_dil"] = _mm_tn(st["t"][1], dbr_dil, name=f"dw_br_dil_{tag}")
    g["br_swa"] = _mm_tn(st["t"][2], dbr_swa, name=f"dw_br_swa_{tag}")
    dqkv, dbias, dsinks = _mixer_bwd(st["mix"], bias, do_sb, do_dil, do_swa, tag)
    dw_qkv = _mm_tn(dqkv, st["h"], out_rows=D_QKV + D_GATES, name=f"dw_qkv_{tag}")
    g["in"] = _mm_tn(dgates, st["h"], out_rows=D_QKV + D_GATES, row0=D_QKV, prev=dw_qkv, name=f"dw_gates_{tag}")
    dx_in, sum_dh, sum_dhx, made = _dh_norm_bwd(dqkv, w["in"], dgates, w["in"], st["x"], dx_out, _row(gain), _row(mod_j[1]),
                                                nxt, after=done(g), b_rows=(0, D_QKV), name=f"dh_mix_{tag}")
    dmod = jnp.concatenate([sum_dh, gain * sum_dhx, dgate], 0)
    return dx_in, dmod, (1.0 + mod_j[1]) * sum_dhx[0], dbias, dsinks, made


def _local_step(x, target, mod, gains, weights_of, rel_bias, sinks, final_gain, grads_done):
    tables = jnp.asarray(_bucket_tables())
    bias = _bias_build(rel_bias, tables, name="bias_build")
    states, h = [], x
    for l in range(DEPTH):
        st = {}
        for j, piece in enumerate(PIECES):
            w, after = weights_of(l, piece, h)
            if piece == "mix":
                h, st[piece] = _mix_fwd(h, w, gains[l, j], mod[l, j], bias, sinks[l], f"l{l}", after)
            else:
                h, st[piece] = _ffn_fwd(h, w, gains[l, j], mod[l, j], f"{piece}_l{l}", after)
        states.append(st)
    loss, dx, dfinal = _final_loss(h, target, _row(final_gain), name="final_loss")
    dmods = [[None] * 3 for _ in range(DEPTH)]
    dgains = [[None] * 3 for _ in range(DEPTH)]
    dsinks = [None] * DEPTH
    dbias, made = None, None
    sweep = [(l, j) for l in reversed(range(DEPTH)) for j in reversed(range(3))]
    for k, (l, j) in enumerate(sweep):
        piece = PIECES[j]
        done = lambda grads, l=l, piece=piece: grads_done(l, piece, grads)
        nxt = None
        if k + 1 < len(sweep):
            nl, nj = sweep[k + 1]
            coef = 1.0 if PIECES[nj] == "mix" else 0.5
            nxt = (states[nl][PIECES[nj]]["f"], _row(coef * mod[nl, nj, 2]), coef)
        if piece == "mix":
            dx, dmods[l][j], dgains[l][j], db, dsinks[l], made = _mix_bwd(
                dx, states[l][piece], gains[l, j], mod[l, j], bias, f"l{l}", done, made, nxt)
            dbias = db if dbias is None else dbias + db
        else:
            dx, dmods[l][j], dgains[l][j], made = _ffn_bwd(
                dx, states[l][piece], gains[l, j], mod[l, j], f"{piece}_l{l}", done, made, nxt)
    drel = _bias_grad(dbias, tables, name="bias_grad")[:, 0, :N_BUCKETS].T
    dmod = jnp.stack([jnp.stack(m) for m in dmods])
    dgain = jnp.stack([jnp.stack(g) for g in dgains])
    return loss, dx, dmod, dgain, dfinal[0], drel, jnp.stack(dsinks)


BR_ROWS = (H_SB * HEAD_DIM, 2 * HEAD_DIM, H_SWA_Q * HEAD_DIM)


def _lanes_unshard(g, lead):
    _, rows, _ = g.shape
    r = rows // lead
    return g.reshape(N_DEV, lead, r, LANES).transpose(1, 2, 0, 3).reshape(lead, r, N_DEV * LANES)


def _lanes_shard(full):
    lead, r, _ = full.shape
    return full.reshape(lead, r, N_DEV, LANES).transpose(2, 0, 1, 3).reshape(N_DEV, lead * r, LANES)


def _pack_rows(parts, dtype):
    flat = jnp.concatenate([p.astype(dtype).reshape(-1) for p in parts])
    pad = (-flat.shape[0]) % (16 * LANES)
    if pad:
        flat = jnp.concatenate([flat, jnp.zeros((pad,), dtype)])
    return flat.reshape(-1, LANES)


def _unshard(gathered, axis):
    moved = jnp.moveaxis(gathered, 0, axis)
    shape = list(moved.shape)
    shape[axis:axis + 2] = [shape[axis] * shape[axis + 1]]
    return moved.reshape(shape)


def kernel(x, c, w_ada, b_ada, norm_gain, w_ffn_gate, w_ffn_up, w_ffn_down, w_in, w_br_sb, w_br_dil, w_br_swa, w_out, sinks, rel_bias, final_gain, loss_target, m_w_ada, m_b_ada, m_norm_gain, m_w_ffn_gate, m_w_ffn_up, m_w_ffn_down, m_w_in, m_w_br_sb, m_w_br_dil, m_w_br_swa, m_w_out, m_sinks, m_rel_bias, m_final_gain, v_w_ada, v_b_ada, v_norm_gain, v_w_ffn_gate, v_w_ffn_up, v_w_ffn_down, v_w_in, v_w_br_sb, v_w_br_dil, v_w_br_swa, v_w_out, v_sinks, v_rel_bias, v_final_gain):
    me = 4 * lax.axis_index("x") + 2 * lax.axis_index("y") + lax.axis_index("c")
    d = D_MODEL
    gate_t, up_t, in_t = jnp.swapaxes(w_ffn_gate, 2, 3), jnp.swapaxes(w_ffn_up, 2, 3), jnp.swapaxes(w_in, 1, 2)

    def piece_shards(l, piece):
        bf = lambda t: t.astype(BF16)
        if piece == "mix":
            return [bf(in_t[l]), jnp.concatenate([bf(w_br_sb[l]), bf(w_br_dil[l]), bf(w_br_swa[l])], 0), bf(w_out[l])]
        i = PIECES.index(piece) // 2
        return [bf(gate_t[l, i]), bf(up_t[l, i]), bf(w_ffn_down[l, i])]

    br_off = np.concatenate([[0], np.cumsum(BR_ROWS)])

    def piece_weights(gathered, piece):
        if piece == "mix":
            g_in, g_br, g_out = gathered
            f_br = [_lanes_unshard(g_br[:, br_off[k]:br_off[k + 1]], 1)[0] for k in range(3)]
            return {"in": g_in.reshape(D_QKV + D_GATES, d), "br_sb": f_br[0], "br_dil": f_br[1], "br_swa": f_br[2],
                    "out": g_out.reshape(d, d)}
        return {n: g.reshape(D_FF, d) for n, g in zip(("gate", "up", "down"), gathered)}

    order = [(l, piece) for l in range(DEPTH) for piece in PIECES]
    ahead = 3
    in_flight, passed = {}, {}

    def start_gather(k, after):
        l, piece = order[k]
        in_flight[k], token = _relay_start(piece_shards(l, piece), after, name=f"gather_{piece}_l{l}_start")
        return token

    small, = _all_gather([_pack_rows([c, norm_gain], F32)], after=start_gather(0, c), name="gather_cond")
    c_all = small[:, :d // LANES].reshape(N_DEV, d)
    gains = _unshard(small[:, d // LANES:d // LANES + 6].reshape(N_DEV, DEPTH, 3, LANES), 2)

    cols = w_ada.shape[2]
    mod_cols = jnp.stack([_ada_fwd(c_all, w_ada[l], name=f"ada_fwd_l{l}") for l in range(DEPTH)])
    mod_all, = _all_gather([_pack_rows([mod_cols], F32)], name="gather_mod")
    mod_all = mod_all.reshape(N_DEV, -1)[:, :DEPTH * N_DEV * cols].reshape(N_DEV, DEPTH, N_DEV, cols)
    mod_mine = lax.dynamic_index_in_dim(mod_all, me, axis=2, keepdims=False)
    mod = (mod_mine.transpose(1, 0, 2).reshape(DEPTH, N_DEV * cols) + b_ada).reshape(DEPTH, 3, 3, d)

    token = mod_all
    for k in range(1, 1 + ahead):
        token = start_gather(k, token)
    mod = mod + token[0, 0]

    def weights_of(l, piece, h):
        k = order.index((l, piece))
        token = start_gather(k + ahead, h) if k + ahead < len(order) and k + ahead not in in_flight else None
        for nxt in ([k] if k < 3 else []) + ([k + 1] if 3 <= k + 1 < len(order) else []):
            nl, npiece = order[nxt]
            passed[nxt], token = _relay_pass(in_flight[nxt], h if token is None else token,
                                             name=f"gather_{npiece}_l{nl}_pass")
        landed = _relay_wait(passed[k], h if token is None else token, name=f"gather_{piece}_l{l}_wait")
        return piece_weights(landed, piece), token

    exchanges, have, deferred = {}, {}, []

    def grads_done(l, piece, g):
        key = (l, piece)
        have.setdefault(key, {}).update(g)
        if piece == "mix":
            if len(have[key]) < 5:
                return None
            g = have[key]
            s_br = jnp.concatenate([_lanes_shard(g[n][None]) for n in ("br_sb", "br_dil", "br_swa")], 1)
            groups = [(("in", "br", "out"), [g["in"].reshape(N_DEV, -1, d), s_br, g["out"].reshape(N_DEV, -1, d)])]
        elif key == order[0]:
            deferred.extend(((n,), [t.reshape(N_DEV, -1, d)]) for n, t in g.items())
            return None
        elif len(have[key]) < 3:
            return None
        else:
            groups = [(("gate", "up", "down"), [have[key][n].reshape(N_DEV, -1, d) for n in ("gate", "up", "down")])]
        token = None
        for names, sg in groups:
            state, token = _exchange_start(sg, None, name=f"exchange_{piece}_l{l}_{names[0]}_start")
            exchanges.setdefault(key, []).append((names, state))
        return token

    loss, dx, dmod, dgains, dfinal, drel, dsinks = _local_step(
        x[0], loss_target[0], mod, gains, weights_of, rel_bias, sinks, final_gain, grads_done)

    flat = lambda t: t.reshape(-1, t.shape[-1])
    transposed = lambda ts: tuple(flat(jnp.swapaxes(t, -1, -2)) for t in ts)
    families = {
        "gate": transposed((w_ffn_gate, m_w_ffn_gate, v_w_ffn_gate)), "up": transposed((w_ffn_up, m_w_ffn_up, v_w_ffn_up)),
        "down": tuple(flat(t) for t in (w_ffn_down, m_w_ffn_down, v_w_ffn_down)),
        "in": transposed((w_in, m_w_in, v_w_in)),
        "br": tuple(flat(jnp.concatenate(ts, 1)) for ts in ((w_br_sb, w_br_dil, w_br_swa), (m_w_br_sb, m_w_br_dil, m_w_br_swa),
                                                            (v_w_br_sb, v_w_br_dil, v_w_br_swa))),
        "out": tuple(flat(t) for t in (w_out, m_w_out, v_w_out))}
    parts, stepped = {}, {}

    def step(keys, after):
        for key in keys:
            for names, ex_state in exchanges[key]:
                landed = _exchange_wait(ex_state, after, name=f"exchange_{key[1]}_l{key[0]}_{names[0]}_wait")
                parts.setdefault(key, {}).update(zip(names, landed))
                after = landed[0]
        for key in keys:
            l, piece = key
            for n, group in parts[key].items():
                w2, m2, v2 = families[n]
                rows = group.shape[1]
                row0 = (2 * l + PIECES.index(piece) // 2) * rows if piece != "mix" else l * rows
                stepped[n] = _reduce_adamw([group], w2, m2, v2, row0, stepped.get(n), after=after,
                                           name=f"reduce_adamw_{n}_{piece}_l{l}")
                after = stepped[n][1]
        return after

    small_parts = [dmod, dgains, dfinal, drel.T, dsinks, loss[0, :1]]
    small_sizes = [int(np.prod(p.shape)) for p in small_parts]
    small_all, = _all_gather([_pack_rows(small_parts, F32)], name="gather_small")
    token = small_all
    for names, sg in deferred:
        state, token = _exchange_start(sg, token, name=f"exchange_ffn0_l0_{names[0]}_start")
        exchanges.setdefault(order[0], []).append((names, state))

    after_l1 = step([key for key in reversed(order) if key[0] == 1], token)
    small_sum = _sum_parts(small_all, name="sum_small", after=token).reshape(-1)
    offs = np.concatenate([[0], np.cumsum(small_sizes)])
    g_b_ada = small_sum[offs[0]:offs[1]].reshape(DEPTH, 9 * d)
    g_gain_full = small_sum[offs[1]:offs[2]].reshape(DEPTH, 3, d)
    g_norm_gain = lax.dynamic_slice_in_dim(g_gain_full, me * LANES, LANES, axis=2)
    g_final = small_sum[offs[2]:offs[3]]
    g_rel = small_sum[offs[3]:offs[4]].reshape(N_SOFT, N_BUCKETS).T
    g_sinks = small_sum[offs[4]:offs[5]].reshape(DEPTH, H_SWA_Q)
    loss_total = small_sum[offs[5]]

    dmod_all = small_all.reshape(N_DEV, -1)[:, :DEPTH * 9 * d].reshape(N_DEV, DEPTH, 9 * d)
    dmod_cols = lax.dynamic_slice_in_dim(dmod_all, me * cols, cols, axis=2)
    g_w_ada = jnp.stack([_ada_bwd(c_all.T, dmod_cols[:, l], name=f"ada_bwd_l{l}") for l in range(DEPTH)])

    small_state = {"w_ada": (w_ada, m_w_ada, v_w_ada), "b_ada": (b_ada, m_b_ada, v_b_ada),
                   "norm_gain": (norm_gain, m_norm_gain, v_norm_gain), "sinks": (sinks, m_sinks, v_sinks),
                   "rel_bias": (rel_bias, m_rel_bias, v_rel_bias), "final_gain": (final_gain, m_final_gain, v_final_gain)}
    after = step([order[2], order[1]], after_l1)
    grad, update = {}, {}
    for n, g in (("w_ada", g_w_ada), ("b_ada", g_b_ada), ("norm_gain", g_norm_gain), ("sinks", g_sinks),
                 ("rel_bias", g_rel), ("final_gain", g_final)):
        w, m, v = small_state[n]
        grad[n] = g
        if w.ndim == 1:
            update[n] = tuple(t.reshape(w.shape)
                              for t in _adamw(_row(w), _row(g), _row(m), _row(v), name=f"adamw_{n}", after=after))
        else:
            update[n] = _adamw(w, g, m, v, name=f"adamw_{n}", after=after)
        after = update[n][0]

    step([order[0]], after)

    def unflat(n, like, swapped):
        shape = jnp.swapaxes(like, -1, -2).shape if swapped else like.shape
        out = [t.reshape(shape) for t in stepped[n]]
        return [jnp.swapaxes(t, -1, -2) for t in out] if swapped else out

    results = {"w_ffn_gate": unflat("gate", w_ffn_gate, True), "w_ffn_up": unflat("up", w_ffn_up, True),
               "w_ffn_down": unflat("down", w_ffn_down, False), "w_in": unflat("in", w_in, True),
               "w_out": unflat("out", w_out, False)}
    br = [t.reshape(DEPTH, -1, LANES) for t in stepped["br"]]
    for k, n in enumerate(("w_br_sb", "w_br_dil", "w_br_swa")):
        results[n] = [t[:, br_off[k]:br_off[k + 1]] for t in br]
    for n, (g, dl, nm, nv) in results.items():
        grad[n], update[n] = g, (dl, nm, nv)

    names = ["w_ada", "b_ada", "norm_gain", "w_ffn_gate", "w_ffn_up", "w_ffn_down", "w_in", "w_br_sb", "w_br_dil",
             "w_br_swa", "w_out", "sinks", "rel_bias", "final_gain"]
    return (loss_total, dx[None], *[grad[n] for n in names], *[update[n][0] for n in names],
            *[update[n][1] for n in names], *[update[n][2] for n in names])
```

```python
import math

import numpy as np
import jax
import jax.numpy as jnp
from jax import lax
from jax.experimental import pallas as pl
from jax.experimental.pallas import tpu as pltpu

F32, BF16 = jnp.float32, jnp.bfloat16

SEQ, D_MODEL, D_FF, HEAD_DIM = 2048, 1024, 2816, 64
DEPTH = 2
BLK = 128
H_SB, H_DIL, H_SWA_Q, H_SWA_KV = 4, 6, 6, 2
DIL_PATTERNS = ((128, 1), (512, 4), (2048, 16))
SWA_WINDOW = 128
N_BUCKETS, MAX_REL_DIST = 32, 2048
RMS_EPS = 1e-6
D_QKV = 2560
D_GATES = 3 * D_MODEL
ADAM_LR, ADAM_B1, ADAM_B2, ADAM_EPS, ADAM_WD, ADAM_STEP = 0.001, 0.9, 0.999, 1e-08, 0.01, 10

N_DEV = 8
LANES = 128
NEG = -1e30
SB_TILE = 512
VMEM_LIMIT_BYTES = 48 * 1024 * 1024
HBM = pl.BlockSpec(memory_space=pltpu.HBM)
MESH = pl.DeviceIdType.MESH


def _tile(n, target):
    t = (min(n, target) // LANES) * LANES
    while t >= LANES:
        if n % t == 0:
            return t
        t -= LANES
    return n


def _row_tile(r, cap):
    t = (min(r, cap) // 16) * 16
    while t > 16 and r % t:
        t -= 16
    return t


def _params(semantics=None):
    return pltpu.CompilerParams(dimension_semantics=semantics, vmem_limit_bytes=VMEM_LIMIT_BYTES)


def _dot(a, b, ca, cb):
    return lax.dot_general(a, b, (((ca,), (cb,)), ((), ())), preferred_element_type=F32)


def _sigmoid(a):
    return 1.0 / (1.0 + jnp.exp(-a))


def _row(v):
    return v.reshape(1, -1)


def _all_gather(arrs, name, after=None):
    n = len(arrs)
    ins = list(arrs) + ([] if after is None else [after])

    def body(*refs):
        x_refs, out_refs = refs[:n], refs[len(ins):len(ins) + n]
        send_sems, recv_sems, local_sems = refs[len(ins) + n:]
        x, y, c = lax.axis_index("x"), lax.axis_index("y"), lax.axis_index("c")
        me, sibling = (x, y, c), (x, y, 1 - c)
        chips = [(1 - x, y), (x, 1 - y), (1 - x, 1 - y)]

        def slot(t, px, py, pc):
            return out_refs[t].at[4 * px + 2 * py + pc]

        def copy(t, k, block, to, src=None):
            return pltpu.make_async_remote_copy(
                src_ref=slot(t, *block) if src is None else src, dst_ref=slot(t, *block),
                send_sem=send_sems.at[7 * t + k], recv_sem=recv_sems.at[7 * t + k], device_id=to, device_id_type=MESH)

        mine = [pltpu.make_async_copy(x_refs[t], slot(t, *me), local_sems.at[t]) for t in range(n)]
        for cp in mine:
            cp.start()
        first = []
        for t in range(n):
            first.append(copy(t, 0, me, sibling, src=x_refs[t]))
            first += [copy(t, 1 + j, me, (*chip, c), src=x_refs[t]) for j, chip in enumerate(chips)]
        for cp in first:
            cp.start()
        passed = []
        for j, chip in enumerate(chips):
            for t in range(n):
                copy(t, 1 + j, (*chip, c), me).wait_recv()
                passed.append(copy(t, 4 + j, (*chip, c), sibling))
                passed[-1].start()
        for t in range(n):
            copy(t, 0, sibling, me).wait_recv()
        for j, chip in enumerate(chips):
            for t in range(n):
                copy(t, 4 + j, (*chip, 1 - c), me).wait_recv()
        for cp in first + passed:
            cp.wait_send()
        for cp in mine:
            cp.wait()

    return pl.pallas_call(
        body, name=name, out_shape=[jax.ShapeDtypeStruct((N_DEV,) + a.shape, a.dtype) for a in arrs],
        in_specs=[HBM] * n + [pl.BlockSpec(memory_space=pl.ANY)] * (len(ins) - n), out_specs=[HBM] * n,
        scratch_shapes=[pltpu.SemaphoreType.DMA((7 * n,)), pltpu.SemaphoreType.DMA((7 * n,)), pltpu.SemaphoreType.DMA((n,))],
    )(*ins)


def _direct_copies(x_refs, land_refs, send_sems, recv_sems, local_sems):
    x, y, c = lax.axis_index("x"), lax.axis_index("y"), lax.axis_index("c")
    me = 4 * x + 2 * y + c
    sends, recvs = [], []
    for k in range(1, N_DEV):
        px = 1 - x if (k >> 2) & 1 else x
        py = 1 - y if (k >> 1) & 1 else y
        pc = 1 - c if k & 1 else c
        peer = 4 * px + 2 * py + pc
        for t, (x_ref, land_ref) in enumerate(zip(x_refs, land_refs)):
            sem = 7 * t + k - 1
            for out, src, slot in ((sends, peer, me), (recvs, me, peer)):
                out.append(pltpu.make_async_remote_copy(
                    src_ref=x_ref.at[src], dst_ref=land_ref.at[slot], send_sem=send_sems.at[sem],
                    recv_sem=recv_sems.at[sem], device_id=(px, py, pc), device_id_type=MESH))
    own = [pltpu.make_async_copy(x_ref.at[me], land_ref.at[me], local_sems.at[t])
           for t, (x_ref, land_ref) in enumerate(zip(x_refs, land_refs))]
    return sends, recvs, own


SEM =pl.BlockSpec(memory_space=pltpu.SEMAPHORE)
ANY = pl.BlockSpec(memory_space=pl.ANY)
SIDE_EFFECT = pltpu.SideEffectType.DATAFLOW_SIDE_EFFECTING


def _exchange_start(arrs, after, *, name):
    n = len(arrs)
    lands = [lax.empty(a.shape, a.dtype) for a in arrs]
    extra = [] if after is None else [after]

    def body(*refs):
        sems = refs[2 * n + len(extra):2 * n + len(extra) + 3]
        sends, _, own = _direct_copies(refs[:n], refs[n:2 * n], *sems)
        for cp in own + sends:
            cp.start()
        refs[-1][...] = jnp.zeros_like(refs[-1])

    ops = [pltpu.with_memory_space_constraint(a, pltpu.HBM) for a in list(arrs) + lands]
    out = pl.pallas_call(
        body, name=name,
        out_shape=(pltpu.SemaphoreType.DMA((7 * n,)), pltpu.SemaphoreType.DMA((7 * n,)), pltpu.SemaphoreType.DMA((n,)),
                   *[pltpu.HBM(a.shape, a.dtype) for a in ops], jax.ShapeDtypeStruct((8, LANES), F32)),
        in_specs=[HBM] * (2 * n) + [ANY] * len(extra),
        out_specs=(SEM, SEM, SEM, *[HBM] * (2 * n), pl.BlockSpec(memory_space=pltpu.VMEM)),
        input_output_aliases={t: 3 + t for t in range(2 * n)},
        compiler_params=pltpu.CompilerParams(has_side_effects=SIDE_EFFECT),
    )(*ops, *extra)
    return (out[:3], out[3:3 + n], out[3 + n:3 + 2 * n]), out[-1]


def _exchange_wait(state, after, *, name):
    sems, arrs, lands = state
    n = len(arrs)

    def body(*refs):
        sends, recvs, own = _direct_copies(refs[:n], refs[n:2 * n], *refs[2 * n:2 * n + 3])
        for cp in own:
            cp.wait()
        for cp in sends:
            cp.wait_send()
        for cp in recvs:
            cp.wait_recv()

    out = pl.pallas_call(
        body, name=name, out_shape=tuple(pltpu.HBM(a.shape, a.dtype) for a in list(arrs) + list(lands)),
        in_specs=[HBM] * (2 * n) + [SEM, SEM, SEM, ANY], out_specs=tuple([HBM] * (2 * n)),
        input_output_aliases={t: t for t in range(2 * n)},
        compiler_params=pltpu.CompilerParams(has_side_effects=SIDE_EFFECT),
    )(*arrs, *lands, *sems, after)
    return out[n:]


def _relay_copies(x_refs, land_refs, sems_a, sems_b):
    x, y, c = lax.axis_index("x"), lax.axis_index("y"), lax.axis_index("c")
    me = 4 * x + 2 * y + c
    sibling = (x, y, 1 - c)
    chips = [(1 - x, y), (x, 1 - y), (1 - x, 1 - y)]

    def slot(px, py, pc):
        return 4 * px + 2 * py + pc

    def copy(src, land_ref, dst_slot, send_sems, recv_sems, k, to):
        return pltpu.make_async_remote_copy(src_ref=src, dst_ref=land_ref.at[dst_slot], send_sem=send_sems.at[k],
                                            recv_sem=recv_sems.at[k], device_id=to, device_id_type=MESH)

    a_send, a_recv, a_own, b_send, b_recv = [], [], [], [], []
    for t, (x_ref, land_ref) in enumerate(zip(x_refs, land_refs)):
        peers = [sibling] + [(*chip, c) for chip in chips]
        if sems_a is not None:
            for k, peer in enumerate(peers):
                a_send.append(copy(x_ref, land_ref, me, sems_a[0], sems_a[1], 4 * t + k, peer))
                a_recv.append(copy(x_ref, land_ref, slot(*peer), sems_a[0], sems_a[1], 4 * t + k, peer))
            a_own.append(pltpu.make_async_copy(x_ref, land_ref.at[me], sems_a[2].at[t]))
        if sems_b is not None:
            for j, chip in enumerate(chips):
                b_send.append(copy(land_ref.at[slot(*chip, c)], land_ref, slot(*chip, c), sems_b[0], sems_b[1], 3 * t + j, sibling))
                b_recv.append(copy(land_ref.at[slot(*chip, c)], land_ref, slot(*chip, 1 - c), sems_b[0], sems_b[1], 3 * t + j,
                                   sibling))
    return (a_send, a_recv, a_own), (b_send, b_recv)


def _relay_start(arrs, after, name):
    n = len(arrs)
    lands = [lax.empty((N_DEV,) + a.shape, a.dtype) for a in arrs]

    def body(*refs):
        (sends, _, own), _ = _relay_copies(refs[:n], refs[n:2 * n], refs[2 * n + 1:2 * n + 4], None)
        for cp in own + sends:
            cp.start()
        refs[-1][...] = jnp.zeros_like(refs[-1])

    ops = [pltpu.with_memory_space_constraint(a, pltpu.HBM) for a in list(arrs) + lands]
    out = pl.pallas_call(
        body, name=name,
        out_shape=(pltpu.SemaphoreType.DMA((4 * n,)), pltpu.SemaphoreType.DMA((4 * n,)), pltpu.SemaphoreType.DMA((n,)),
                   *[pltpu.HBM(a.shape, a.dtype) for a in ops], jax.ShapeDtypeStruct((8, LANES), F32)),
        in_specs=[HBM] * (2 * n) + [ANY],
        out_specs=(SEM, SEM, SEM, *[HBM] * (2 * n), pl.BlockSpec(memory_space=pltpu.VMEM)),
        input_output_aliases={t: 3 + t for t in range(2 * n)},
        compiler_params=pltpu.CompilerParams(has_side_effects=SIDE_EFFECT),
    )(*ops, after)
    return (out[:3], out[3:3 + n], out[3 + n:3 + 2 * n]), out[-1]


def _relay_pass(state, after, name):
    sems_a, arrs, lands = state
    n = len(arrs)

    def body(*refs):
        sems_b = refs[2 * n + 4:2 * n + 6]
        (a_send, a_recv, a_own), (b_send, _) = _relay_copies(refs[:n], refs[n:2 * n], refs[2 * n:2 * n + 3], sems_b)
        for cp in a_own:
            cp.wait()
        for cp in a_send:
            cp.wait_send()
        for cp in a_recv:
            cp.wait_recv()
        for cp in b_send:
            cp.start()
        refs[-1][...] = jnp.zeros_like(refs[-1])

    out = pl.pallas_call(
        body, name=name,
        out_shape=(pltpu.SemaphoreType.DMA((3 * n,)), pltpu.SemaphoreType.DMA((3 * n,)),
                   *[pltpu.HBM(a.shape, a.dtype) for a in list(arrs) + list(lands)], jax.ShapeDtypeStruct((8, LANES), F32)),
        in_specs=[HBM] * (2 * n) + [SEM, SEM, SEM, ANY],
        out_specs=(SEM, SEM, *[HBM] * (2 * n), pl.BlockSpec(memory_space=pltpu.VMEM)),
        input_output_aliases={t: 2 + t for t in range(2 * n)},
        compiler_params=pltpu.CompilerParams(has_side_effects=SIDE_EFFECT),
    )(*arrs, *lands, *sems_a, after)
    return (out[:2], out[2:2 + n], out[2 + n:2 + 2 * n]), out[-1]


def _relay_wait(state, after, name):
    sems_b, arrs, lands = state
    n = len(arrs)

    def body(*refs):
        _, (b_send, b_recv) = _relay_copies(refs[:n], refs[n:2 * n], None, refs[2 * n:2 * n + 2])
        for cp in b_send:
            cp.wait_send()
        for cp in b_recv:
            cp.wait_recv()

    out = pl.pallas_call(
        body, name=name, out_shape=tuple(pltpu.HBM(a.shape, a.dtype) for a in list(arrs) + list(lands)),
        in_specs=[HBM] * (2 * n) + [SEM, SEM, ANY], out_specs=tuple([HBM] * (2 * n)),
        input_output_aliases={t: t for t in range(2 * n)},
        compiler_params=pltpu.CompilerParams(has_side_effects=SIDE_EFFECT),
    )(*arrs, *lands, *sems_b, after)
    return out[n:]


def _sum_parts(parts, name, after=None):
    n, r, cdim = parts.shape
    tr = _row_tile(r, max(16, (1 << 21) // (n * cdim * parts.dtype.itemsize)))

    def body(p_ref, *rest):
        acc = p_ref[0].astype(F32)
        for k in range(1, n):
            acc = acc + p_ref[k].astype(F32)
        rest[-1][...] = acc

    ins = [parts] + ([] if after is None else [after])
    return pl.pallas_call(
        body, name=name, out_shape=jax.ShapeDtypeStruct((r, cdim), F32), grid=(r // tr,),
        in_specs=[pl.BlockSpec((n, tr, cdim), lambda i: (0, i, 0))] + [ANY] * (len(ins) - 1),
        out_specs=pl.BlockSpec((tr, cdim), lambda i: (i, 0)), compiler_params=_params(("parallel",)),
    )(*ins)


def _mm_tn(a, b, *, name, after=None, tm=512, tn=1024, out_rows=None, row0=0, prev=None):
    k, m = a.shape
    n = b.shape[1]
    tm, tn = _tile(m, tm), _tile(n, tn)
    out_rows = m if out_rows is None else out_rows

    def body(a_ref, b_ref, *rest):
        o_ref, at_ref = rest[-2], rest[-1]

        @pl.when(pl.program_id(1) == 0)
        def _():
            at_ref[...] = a_ref[...].astype(BF16).T

        o_ref[...] = _dot(at_ref[...], b_ref[...].astype(BF16), 1, 0).astype(BF16)

    ins = [a, b] + [t for t in (after, prev) if t is not None]
    return pl.pallas_call(
        body, name=name, out_shape=jax.ShapeDtypeStruct((out_rows, n), BF16), grid=(m // tm, n // tn),
        in_specs=[pl.BlockSpec((k, tm), lambda i, j: (0, i)), pl.BlockSpec((k, tn), lambda i, j: (0, j))] + [ANY] * (len(ins) - 2),
        out_specs=pl.BlockSpec((tm, tn), lambda i, j: (row0 // tm + i, j)),
        input_output_aliases={} if prev is None else {len(ins) - 1: 0},
        scratch_shapes=[pltpu.VMEM((tm, k), BF16)], compiler_params=_params(("parallel", "arbitrary")),
    )(*ins)


def _mm(a, b, *, name, ta=False, tb=False, res=None, colscale=None, emit_acc=False,
        out_dtype=F32, tm=512, tn=512, b_rows=None):
    m, k = (a.shape[1], a.shape[0]) if ta else a.shape
    n = b.shape[0] if tb else b.shape[1]
    b_start = 0
    if b_rows is not None:
        b_start, n = b_rows
    tm, tn = _tile(m, tm), _tile(n, tn)
    ca, cb = (0 if ta else 1), (1 if tb else 0)
    a_spec = pl.BlockSpec((k, tm), lambda i, j: (0, i)) if ta else pl.BlockSpec((tm, k), lambda i, j: (i, 0))
    b_spec = (pl.BlockSpec((tn, k), lambda i, j: (b_start // tn + j, 0)) if tb
              else pl.BlockSpec((k, tn), lambda i, j: (0, j)))
    tile = pl.BlockSpec((tm, tn), lambda i, j: (i, j))
    ins, in_specs = [a, b], [a_spec, b_spec]
    if res is not None:
        ins.append(res)
        in_specs.append(tile)
    if colscale is not None:
        ins.append(colscale)
        in_specs.append(pl.BlockSpec((1, tn), lambda i, j: (0, j)))
    n_in = len(ins)

    def body(*refs):
        outs = refs[n_in:]
        acc = _dot(refs[0][...].astype(BF16), refs[1][...].astype(BF16), ca, cb)
        val, p = acc, 2
        if res is not None:
            r_val, p = refs[p][...], p + 1
        if colscale is not None:
            val = val * refs[p][...]
        if res is not None:
            val = r_val + val
        if emit_acc:
            outs[0][...] = acc
        outs[-1][...] = val.astype(out_dtype)

    out_shape = [jax.ShapeDtypeStruct((m, n), out_dtype)]
    out_specs = [tile]
    if emit_acc:
        out_shape.insert(0, jax.ShapeDtypeStruct((m, n), F32))
        out_specs.insert(0, tile)
    out = pl.pallas_call(
        body, name=name, out_shape=out_shape, grid=(m // tm, n // tn), in_specs=in_specs, out_specs=out_specs,
        compiler_params=_params(("parallel", "parallel")),
    )(*ins)
    return out if emit_acc else out[0]


def _norm_fwd(x, g, scale, shift, name, after=None):
    s, d = x.shape
    tr = 256

    def body(x_ref, g_ref, sc_ref, sh_ref, *rest):
        xv = x_ref[...]
        rstd = lax.rsqrt(jnp.mean(xv * xv, axis=-1, keepdims=True) + RMS_EPS)
        rest[-1][...] = (xv * rstd * g_ref[...] * (1.0 + sc_ref[...]) + sh_ref[...]).astype(BF16)

    rowspec = pl.BlockSpec((1, d), lambda i: (0, 0))
    ins = [x, g, scale, shift] + ([] if after is None else [after])
    return pl.pallas_call(
        body, name=name, out_shape=jax.ShapeDtypeStruct((s, d), BF16), grid=(s // tr,),
        in_specs=[pl.BlockSpec((tr, d), lambda i: (i, 0)), rowspec, rowspec, rowspec] + [ANY] * (len(ins) - 4),
        out_specs=pl.BlockSpec((tr, d), lambda i: (i, 0)),
        compiler_params=_params(("parallel",)),
    )(*ins)


def _dh_norm_bwd(a1, b1, a2, b2, x, dres, g, scale, nxt, *, name, after=None, b_rows=None):
    s, d = x.shape
    tm = 256
    n_fixed = 8

    def body(a1_ref, b1_ref, a2_ref, b2_ref, x_ref, dr_ref, g_ref, sc_ref, *rest):
        rest = rest[(1 if after is not None else 0):]
        if nxt is not None:
            f_ref, cs_ref, dx_ref, sa_ref, sb_ref, df_ref, dg_ref = rest
        else:
            dx_ref, sa_ref, sb_ref = rest

        @pl.when(pl.program_id(0) == 0)
        def _():
            sa_ref[...] = jnp.zeros_like(sa_ref)
            sb_ref[...] = jnp.zeros_like(sb_ref)
            if nxt is not None:
                dg_ref[...] = jnp.zeros_like(dg_ref)

        dhv = (_dot(a1_ref[...].astype(BF16), b1_ref[...], 1, 0) + _dot(a2_ref[...].astype(BF16), b2_ref[...], 1, 0))
        xv = x_ref[...]
        rstd = lax.rsqrt(jnp.mean(xv * xv, axis=-1, keepdims=True) + RMS_EPS)
        xhat = xv * rstd
        dxhat = dhv * (g_ref[...] * (1.0 + sc_ref[...]))
        mean_term = jnp.mean(dxhat * xhat, axis=-1, keepdims=True)
        dxv = dr_ref[...] + rstd * (dxhat - xhat * mean_term)
        dx_ref[...] = dxv
        sa_ref[...] += jnp.sum(dhv, axis=0, keepdims=True)
        sb_ref[...] += jnp.sum(dhv * xhat, axis=0, keepdims=True)
        if nxt is not None:
            df_ref[...] = (dxv * cs_ref[...]).astype(BF16)
            dg_ref[...] += nxt[2] * jnp.sum(dxv * f_ref[...], axis=0, keepdims=True)

    def a_spec(t):
        return pl.BlockSpec((tm, t.shape[1]), lambda i: (i, 0))

    def b_spec(t, a, which):
        if b_rows is None:
            return pl.BlockSpec((t.shape[0], d), lambda i: (0, 0))
        start = b_rows[which]
        return pl.BlockSpec((pl.Element(a.shape[1]), pl.Element(d)), lambda i: (start, 0))

    rowspec = pl.BlockSpec((1, d), lambda i: (0, 0))
    tile = pl.BlockSpec((tm, d), lambda i: (i, 0))
    ins = [a1, b1, a2, b2, x, dres, g, scale] + ([] if after is None else [after])
    in_specs = [a_spec(a1), b_spec(b1, a1, 0), a_spec(a2), b_spec(b2, a2, 1), tile, tile, rowspec, rowspec]
    in_specs += [ANY] * (len(ins) - n_fixed)
    out_shape = [jax.ShapeDtypeStruct((s, d), F32), jax.ShapeDtypeStruct((1, d), F32), jax.ShapeDtypeStruct((1, d), F32)]
    out_specs = [tile, rowspec, rowspec]
    if nxt is not None:
        ins += [nxt[0], nxt[1]]
        in_specs += [tile, rowspec]
        out_shape += [jax.ShapeDtypeStruct((s, d), BF16), jax.ShapeDtypeStruct((1, d), F32)]
        out_specs += [tile, rowspec]
    out = pl.pallas_call(
        body, name=name, out_shape=out_shape, grid=(s // tm,), in_specs=in_specs, out_specs=out_specs,
        compiler_params=_params(("arbitrary",)),
    )(*ins)
    return out[0], out[1], out[2], (None if nxt is None else (out[3], out[4]))


def _gate_bwd(dxn, f, colscale, coef, name):
    s, d = dxn.shape
    tr = 256

    def body(dx_ref, f_ref, cs_ref, df_ref, dg_ref):
        @pl.when(pl.program_id(0) == 0)
        def _():
            dg_ref[...] = jnp.zeros_like(dg_ref)

        dxv = dx_ref[...]
        df_ref[...] = (dxv * cs_ref[...]).astype(BF16)
        dg_ref[...] += coef * jnp.sum(dxv * f_ref[...], axis=0, keepdims=True)

    rowspec = pl.BlockSpec((1, d), lambda i: (0, 0))
    tile = pl.BlockSpec((tr, d), lambda i: (i, 0))
    return pl.pallas_call(
        body, name=name, out_shape=[jax.ShapeDtypeStruct((s, d), BF16), jax.ShapeDtypeStruct((1, d), F32)],
        grid=(s // tr,), in_specs=[tile, tile, rowspec], out_specs=[tile, rowspec],
        compiler_params=_params(("arbitrary",)),
    )(dxn, f, colscale)


def _ffn_up(h, wg, wu, name, tm=SEQ, tn=256):
    s, d = h.shape
    f = wg.shape[0]

    def body(h_ref, wg_ref, wu_ref, a_ref, u_ref, s_ref):
        hv = h_ref[...]
        a = _dot(hv, wg_ref[...], 1, 1)
        u = _dot(hv, wu_ref[...], 1, 1)
        a_ref[...] = a.astype(BF16)
        u_ref[...] = u.astype(BF16)
        s_ref[...] = (a * _sigmoid(a) * u).astype(BF16)

    tile = pl.BlockSpec((tm, tn), lambda i, j: (i, j))
    wspec = pl.BlockSpec((tn, d), lambda i, j: (j, 0))
    return pl.pallas_call(
        body, name=name,
        out_shape=[jax.ShapeDtypeStruct((s, f), BF16), jax.ShapeDtypeStruct((s, f), BF16), jax.ShapeDtypeStruct((s, f), BF16)],
        grid=(s // tm, f // tn), in_specs=[pl.BlockSpec((tm, d), lambda i, j: (i, 0)), wspec, wspec],
        out_specs=[tile, tile, tile], compiler_params=_params(("parallel", "parallel")),
    )(h, wg, wu)


def _ffn_bwd_ds(df, wd, a, u, name, tm=SEQ, tn=256):
    s, d = df.shape
    f = wd.shape[0]

    def body(df_ref, wd_ref, a_ref, u_ref, da_ref, du_ref):
        ds = _dot(df_ref[...], wd_ref[...], 1, 1)
        av = a_ref[...].astype(F32)
        sg = _sigmoid(av)
        da_ref[...] = (ds * u_ref[...].astype(F32) * (sg * (1.0 + av * (1.0 - sg)))).astype(BF16)
        du_ref[...] = (ds * (av * sg)).astype(BF16)

    tile = pl.BlockSpec((tm, tn), lambda i, j: (i, j))
    return pl.pallas_call(
        body, name=name, out_shape=[jax.ShapeDtypeStruct((s, f), BF16), jax.ShapeDtypeStruct((s, f), BF16)],
        grid=(s // tm, f // tn),
        in_specs=[pl.BlockSpec((tm, d), lambda i, j: (i, 0)), pl.BlockSpec((tn, d), lambda i, j: (j, 0)), tile, tile],
        out_specs=[tile, tile], compiler_params=_params(("parallel", "parallel")),
    )(df, wd, a, u)


def _merge_fwd(o_sb, o_dil, o_swa, gates, wb_sb, wb_dil, wb_swa, name):
    s, d = SEQ, D_MODEL
    tm = 256

    def body(osb_ref, odl_ref, osw_ref, g_ref, wsb_ref, wdl_ref, wsw_ref, m_ref, tsb_ref, tdl_ref, tsw_ref):
        for h in range(osb_ref.shape[0]):
            tsb_ref[:, h * HEAD_DIM:(h + 1) * HEAD_DIM] = osb_ref[h].astype(BF16)
        for h in range(osw_ref.shape[0]):
            tsw_ref[:, h * HEAD_DIM:(h + 1) * HEAD_DIM] = osw_ref[h].astype(BF16)
        tdl_ref[...] = odl_ref[...].astype(BF16)
        acc = _sigmoid(g_ref[:, 0:d]) * _dot(tsb_ref[...], wsb_ref[...], 1, 0)
        acc += _sigmoid(g_ref[:, d:2 * d]) * _dot(tdl_ref[...], wdl_ref[...], 1, 0)
        acc += _sigmoid(g_ref[:, 2 * d:3 * d]) * _dot(tsw_ref[...], wsw_ref[...], 1, 0)
        m_ref[...] = acc.astype(BF16)

    def rows(w):
        return pl.BlockSpec((tm, w), lambda i: (i, 0))

    def heads(n):
        return pl.BlockSpec((n, tm, HEAD_DIM), lambda i: (0, i, 0))

    def whole(w):
        return pl.BlockSpec((w, d), lambda i: (0, 0))

    return pl.pallas_call(
        body, name=name, out_shape=[jax.ShapeDtypeStruct((s, w), BF16) for w in (d, 256, 128, 384)], grid=(s // tm,),
        in_specs=[heads(H_SB), rows(128), heads(H_SWA_Q), rows(3 * d), whole(256), whole(128), whole(384)],
        out_specs=[rows(d), rows(256), rows(128), rows(384)], compiler_params=_params(("parallel",)),
    )(o_sb, o_dil, o_swa, gates, wb_sb, wb_dil, wb_swa)


def _merge_bwd(dmerged, t_sb, t_dil, t_swa, gates, wb_sb, wb_dil, wb_swa, name):
    s, d = SEQ, D_MODEL
    tm = 256

    def body(dm_ref, tsb_ref, tdl_ref, tsw_ref, g_ref, wsb_ref, wdl_ref, wsw_ref,
             dg_ref, dosb_ref, dodl_ref, dosw_ref, dbsb_ref, dbdl_ref, dbsw_ref):
        dm = dm_ref[...]
        for idx, (t_ref, w_ref, do_ref, db_ref) in enumerate((
                (tsb_ref, wsb_ref, dosb_ref, dbsb_ref), (tdl_ref, wdl_ref, dodl_ref, dbdl_ref),
                (tsw_ref, wsw_ref, dosw_ref, dbsw_ref))):
            w = w_ref[...]
            br = _dot(t_ref[...], w, 1, 0)
            sg = _sigmoid(g_ref[:, idx * d:(idx + 1) * d])
            dbr = (dm * sg).astype(BF16)
            dg_ref[:, idx * d:(idx + 1) * d] = (dm * br * (sg * (1.0 - sg))).astype(BF16)
            db_ref[...] = dbr
            do = _dot(dbr, w, 1, 1)
            if len(do_ref.shape) == 2:
                do_ref[...] = do
            else:
                for h in range(do_ref.shape[0]):
                    do_ref[h] = do[:, h * HEAD_DIM:(h + 1) * HEAD_DIM]

    def rows(w):
        return pl.BlockSpec((tm, w), lambda i: (i, 0))

    def heads(n):
        return pl.BlockSpec((n, tm, HEAD_DIM), lambda i: (0, i, 0))

    def whole(w):
        return pl.BlockSpec((w, d), lambda i: (0, 0))

    def shp(w, dt):
        return jax.ShapeDtypeStruct((s, w), dt)

    def hshp(n):
        return jax.ShapeDtypeStruct((n, s, HEAD_DIM), F32)

    return pl.pallas_call(
        body, name=name,
        out_shape=[shp(3 * d, BF16), hshp(H_SB), shp(128, F32), hshp(H_SWA_Q), shp(d, BF16), shp(d, BF16), shp(d, BF16)],
        grid=(s // tm,),
        in_specs=[rows(d), rows(256), rows(128), rows(384), rows(3 * d), whole(256), whole(128), whole(384)],
        out_specs=[rows(3 * d), heads(H_SB), rows(128), heads(H_SWA_Q), rows(d), rows(d), rows(d)],
        compiler_params=_params(("parallel",)),
    )(dmerged, t_sb, t_dil, t_swa, gates, wb_sb, wb_dil, wb_swa)


def _final_loss(x, target, g, name):
    s, d = x.shape
    tr = 256

    def body(x_ref, t_ref, g_ref, loss_ref, dx_ref, dg_ref):
        @pl.when(pl.program_id(0) == 0)
        def _():
            loss_ref[...] = jnp.zeros_like(loss_ref)
            dg_ref[...] = jnp.zeros_like(dg_ref)

        xv = x_ref[...]
        gv = g_ref[...]
        rstd = lax.rsqrt(jnp.mean(xv * xv, axis=-1, keepdims=True) + RMS_EPS)
        xhat = xv * rstd
        err = xhat * gv - t_ref[...]
        loss_ref[...] += 0.5 * jnp.sum(jnp.mean(err * err, axis=-1, keepdims=True))
        dy = err * (1.0 / d)
        dxhat = dy * gv
        mean_term = jnp.mean(dxhat * xhat, axis=-1, keepdims=True)
        dx_ref[...] = rstd * (dxhat - xhat * mean_term)
        dg_ref[...] += jnp.sum(dy * xhat, axis=0, keepdims=True)

    rowspec = pl.BlockSpec((1, d), lambda i: (0, 0))
    tile = pl.BlockSpec((tr, d), lambda i: (i, 0))
    return pl.pallas_call(
        body, name=name,
        out_shape=[jax.ShapeDtypeStruct((1, LANES), F32), jax.ShapeDtypeStruct((s, d), F32), jax.ShapeDtypeStruct((1, d), F32)],
        grid=(s // tr,), in_specs=[tile, tile, rowspec],
        out_specs=[pl.BlockSpec((1, LANES), lambda i: (0, 0)), tile, rowspec],
        compiler_params=_params(("arbitrary",)),
    )(x, target, g)


def _adamw(w, g, m, v, name, after=None):
    shape = w.shape
    cols = shape[-1]
    rows = int(np.prod(shape[:-1])) if len(shape) > 1 else 1
    tr = rows
    for cand in (1024, 512, 256, 128, 64, 32, 16, 8):
        if rows % cand == 0 and rows > cand and cand * cols * 4 <= (1 << 21):
            tr = cand
            break

    def body(w_ref, g_ref, m_ref, v_ref, *rest):
        d_ref, nm_ref, nv_ref = rest[-3:]
        d_ref[...], nm_ref[...], nv_ref[...] = _adam_update(w_ref[...], g_ref[...], m_ref[...], v_ref[...])

    tile = pl.BlockSpec((tr, cols), lambda i: (i, 0))
    flat = [t.reshape(rows, cols) for t in (w, g, m, v)] + ([] if after is None else [after])
    out = pl.pallas_call(
        body, name=name, out_shape=[jax.ShapeDtypeStruct((rows, cols), F32)] * 3, grid=(rows // tr,),
        in_specs=[tile] * 4 + [ANY] * (len(flat) - 4), out_specs=[tile] * 3, compiler_params=_params(("parallel",)),
    )(*flat)
    return tuple(t.reshape(shape) for t in out)


def _adam_update(w, gv, m, v):
    nm = ADAM_B1 * m + (1.0 - ADAM_B1) * gv
    nv = ADAM_B2 * v + (1.0 - ADAM_B2) * (gv * gv)
    m_hat = nm / (1.0 - ADAM_B1 ** ADAM_STEP)
    v_hat = nv / (1.0 - ADAM_B2 ** ADAM_STEP)
    return -ADAM_LR * (m_hat / (jnp.sqrt(v_hat) + ADAM_EPS) + ADAM_WD * w), nm, nv


def _reduce_adamw(groups, w, m, v, row0, prev, name, after=None):
    n, r, cdim = groups[0].shape
    rows = w.shape[0]
    tr = _row_tile(r, max(16, (1 << 22) // (n * cdim * groups[0].dtype.itemsize)))
    steps = r // tr
    ng = len(groups)

    def body(*refs):
        w_ref, m_ref, v_ref = refs[ng:ng + 3]
        g_out, d_out, m_out, v_out = refs[-4:]
        gg = pl.program_id(0)
        for gi in range(ng):
            @pl.when(gg == gi)
            def _(gi=gi):
                acc = refs[gi][0].astype(F32)
                for k in range(1, n):
                    acc = acc + refs[gi][k].astype(F32)
                g_out[...] = acc
                d_out[...], m_out[...], v_out[...] = _adam_update(w_ref[...], acc, m_ref[...], v_ref[...])

    def part_spec(gi):
        return pl.BlockSpec((n, tr, cdim), lambda gg, i: (0, jnp.where(gg == gi, i, 0), 0))

    tile = pl.BlockSpec((tr, cdim), lambda gg, i: (row0 // tr + gg * steps + i, 0))
    extra = ([] if prev is None else list(prev)) + ([] if after is None else [after])
    return pl.pallas_call(
        body, name=name, out_shape=[jax.ShapeDtypeStruct((rows, cdim), F32)] * 4, grid=(ng, steps),
        in_specs=[part_spec(gi) for gi in range(ng)] + [tile] * 3 + [ANY] * len(extra), out_specs=[tile] * 4,
        input_output_aliases={} if prev is None else {ng + 3 + k: k for k in range(4)},
        compiler_params=_params(("parallel", "parallel")),
    )(*groups, w, m, v, *extra)


def _ada_fwd(c_all, w, name):
    n = w.shape[1]

    def body(c_ref, w_ref, o_ref):
        cv = c_ref[...]
        o_ref[...] = jnp.dot(cv * _sigmoid(cv), w_ref[...], preferred_element_type=F32, precision=lax.Precision.HIGHEST)

    return pl.pallas_call(body, name=name, out_shape=jax.ShapeDtypeStruct((N_DEV, n), F32), compiler_params=_params())(c_all, w)


def _ada_bwd(c_all_t, dmod, name):
    n = dmod.shape[1]

    def body(c_ref, d_ref, o_ref):
        cv = c_ref[...]
        o_ref[...] = jnp.dot(cv * _sigmoid(cv), d_ref[...], preferred_element_type=F32, precision=lax.Precision.HIGHEST)

    return pl.pallas_call(body, name=name, out_shape=jax.ShapeDtypeStruct((D_MODEL, n), F32), compiler_params=_params())(c_all_t, dmod)


def _bucket_tables():
    rel = np.arange(BLK)[:, None] + BLK - np.arange(2 * BLK)[None, :]
    max_exact = N_BUCKETS // 2

    def bucket(n):
        nf = np.maximum(n, 1).astype(np.float32)
        large = max_exact + (np.log(nf / np.float32(max_exact)) / np.float32(math.log(MAX_REL_DIST / max_exact))
                             * np.float32(N_BUCKETS - max_exact)).astype(np.int32)
        return np.where(n < max_exact, n, np.minimum(large, N_BUCKETS - 1))

    tabs = []
    for dil, max_dist in ((1, 128), (4, 128), (16, 128), (1, SWA_WINDOW - 1)):
        in_band = (rel >= 0) & (rel <= max_dist)
        tabs.append(np.where(in_band, bucket(np.maximum(rel, 0) * dil), -1))
    return np.stack(tabs).astype(np.int32)


N_SOFT = H_DIL + H_SWA_Q


def _table_of_head(h):
    return jnp.minimum(h // 2, 3)


def _bias_build(rel_bias, tables, name):
    def body(rel_ref, t_ref, o_ref):
        h = pl.program_id(0)
        tb = t_ref[0]
        out = jnp.full((BLK, 2 * BLK), NEG, F32)
        for b in range(N_BUCKETS):
            out = jnp.where(tb == b, rel_ref[b, h], out)
        o_ref[0] = out

    return pl.pallas_call(
        body, name=name, out_shape=jax.ShapeDtypeStruct((N_SOFT, BLK, 2 * BLK), F32), grid=(N_SOFT,),
        in_specs=[pl.BlockSpec(memory_space=pltpu.SMEM),
                  pl.BlockSpec((1, BLK, 2 * BLK), lambda h: (_table_of_head(h), 0, 0))],
        out_specs=pl.BlockSpec((1, BLK, 2 * BLK), lambda h: (h, 0, 0)),
        compiler_params=_params(("parallel",)),
    )(rel_bias, tables)


def _bias_grad(dbias, tables, name):
    def body(d_ref, t_ref, o_ref):
        tb = t_ref[0]
        dv = d_ref[0]
        lane = lax.broadcasted_iota(jnp.int32, (1, LANES), 1)
        out = jnp.zeros((1, LANES), F32)
        for b in range(N_BUCKETS):
            out = jnp.where(lane == b, jnp.sum(jnp.where(tb == b, dv, 0.0)), out)
        o_ref[0] = out

    return pl.pallas_call(
        body, name=name, out_shape=jax.ShapeDtypeStruct((N_SOFT, 1, LANES), F32), grid=(N_SOFT,),
        in_specs=[pl.BlockSpec((1, BLK, 2 * BLK), lambda h: (h, 0, 0)),
                  pl.BlockSpec((1, BLK, 2 * BLK), lambda h: (_table_of_head(h), 0, 0))],
        out_specs=pl.BlockSpec((1, 1, LANES), lambda h: (h, 0, 0)),
        compiler_params=_params(("parallel",)),
    )(dbias, tables)


def _band_layout(g, bias_div):
    assert g == 1 or bias_div == 1
    return bias_div if g == 1 else 1


def _band_specs(length, g, bias_div, offs):
    ns = _band_layout(g, bias_div)

    def seqs(off, div=1):
        return pl.BlockSpec((ns, length, HEAD_DIM), lambda s: (off // ns + s // div, 0, 0))

    xspecs = [seqs(offs[0]), seqs(offs[1], g), seqs(offs[2], g)]
    bspec = pl.BlockSpec((1, BLK, 2 * BLK), lambda s: (s, 0, 0))
    sspec = pl.BlockSpec((ns, 1, LANES), lambda s: (s, 0, 0))
    colspec = pl.BlockSpec((ns, length, 1), lambda s: (s, 0, 0))
    return xspecs, seqs(0), seqs(0, g), bspec, sspec, colspec


def _band_sweep(length, ns, one):
    nblk = length // BLK
    for qq in range(ns):
        if ns * nblk <= 16:
            for i in range(nblk):
                one(qq, i * BLK, max(i - 1, 0) * BLK, i == 0)
        else:
            def step(i, carry, qq=qq):
                one(qq, pl.multiple_of(i * BLK, BLK), pl.multiple_of(jnp.maximum(i - 1, 0) * BLK, BLK), i == 0)
                return carry

            lax.fori_loop(0, nblk, step, 0, unroll=2)


def _band_scores(q_ref, k_ref, b_ref, qq, kq, bq, cur, prv, first):
    qv = q_ref[qq, pl.ds(cur, BLK), :]
    bv = b_ref[bq]
    if first is True:
        sp = jnp.full((BLK, BLK), NEG, F32)
    else:
        sp = _dot(qv, k_ref[kq, pl.ds(prv, BLK), :], 1, 1) + bv[:, :BLK]
        sp = sp if first is False else jnp.where(first, NEG, sp)
    sc = _dot(qv, k_ref[kq, pl.ds(cur, BLK), :], 1, 1) + bv[:, BLK:]
    return qv, sp, sc


def _band_fwd(x, bias, sink, *, nq, offs, g, bias_div, has_sink, name):
    length = x.shape[1]
    ns = _band_layout(g, bias_div)

    def body(q_ref, k_ref, v_ref, b_ref, s_ref, o_ref, lse_ref):
        def one(qq, cur, prv, first):
            kq, bq = qq, 0
            _, sp, sc = _band_scores(q_ref, k_ref, b_ref, qq, kq, bq, cur, prv, first)
            m = jnp.maximum(jnp.max(sp, axis=1, keepdims=True), jnp.max(sc, axis=1, keepdims=True))
            if has_sink:
                sk = s_ref[qq][:, :1]
                m = jnp.maximum(m, sk)
            pp, pc = jnp.exp(sp - m), jnp.exp(sc - m)
            den = jnp.sum(pp, axis=1, keepdims=True) + jnp.sum(pc, axis=1, keepdims=True)
            if has_sink:
                den = den + jnp.exp(sk - m)
            acc = (_dot(pp.astype(BF16), v_ref[kq, pl.ds(prv, BLK), :], 1, 0)
                   + _dot(pc.astype(BF16), v_ref[kq, pl.ds(cur, BLK), :], 1, 0))
            o_ref[qq, pl.ds(cur, BLK), :] = acc / den
            lse_ref[qq, pl.ds(cur, BLK), :] = m + jnp.log(den)

        _band_sweep(length, ns, one)

    xspecs, qspec, _, bspec, sspec, colspec = _band_specs(length, g, bias_div, offs)
    return pl.pallas_call(
        body, name=name,
        out_shape=[jax.ShapeDtypeStruct((nq, length, HEAD_DIM), F32), jax.ShapeDtypeStruct((nq, length, 1), F32)],
        grid=(nq // ns,), in_specs=xspecs + [bspec, sspec],
        out_specs=[qspec, colspec], compiler_params=_params(("parallel",)),
    )(x, x, x, bias, sink)


def _band_bwd(x, bias, sink, o, lse, do, dlse, *, nq, offs, g, bias_div, has_sink, name):
    length = x.shape[1]
    ns = _band_layout(g, bias_div)
    nk, nbias = nq // g, nq // bias_div

    def body(q_ref, k_ref, v_ref, b_ref, s_ref, o_ref, lse_ref, do_ref, dlse_ref,
             dq_ref, dk_ref, dv_ref, db_ref, dsk_ref, dkp_ref, dvp_ref):
        for ref in (db_ref, dsk_ref, dkp_ref, dvp_ref):
            ref[...] = jnp.zeros_like(ref)

        @pl.when(pl.program_id(0) % g == 0)
        def _():
            dk_ref[...] = jnp.zeros_like(dk_ref)
            dv_ref[...] = jnp.zeros_like(dv_ref)

        def one(qq, cur, prv, first):
            kq, bq = qq, 0
            qv, sp, sc = _band_scores(q_ref, k_ref, b_ref, qq, kq, bq, cur, prv, first)
            rows, prow = pl.ds(cur, BLK), pl.ds(prv, BLK)
            lse_v = lse_ref[qq, rows, :]
            pp, pc = jnp.exp(sp - lse_v), jnp.exp(sc - lse_v)
            dov = do_ref[qq, rows, :]
            dob = dov.astype(BF16)
            coef = dlse_ref[qq, rows, :] - jnp.sum(dov * o_ref[qq, rows, :], axis=1, keepdims=True)
            dsp = pp * (_dot(dob, v_ref[kq, prow, :], 1, 1) + coef)
            dsc = pc * (_dot(dob, v_ref[kq, rows, :], 1, 1) + coef)
            dspb, dscb = dsp.astype(BF16), dsc.astype(BF16)
            dq_ref[qq, rows, :] = ((_dot(dspb, k_ref[kq, prow, :], 1, 0) + _dot(dscb, k_ref[kq, rows, :], 1, 0))
                                   * (HEAD_DIM ** -0.5))
            dk_ref[kq, rows, :] += _dot(dscb, qv, 0, 0)
            dkp_ref[kq, prow, :] += _dot(dspb, qv, 0, 0)
            dv_ref[kq, rows, :] += _dot(pc.astype(BF16), dob, 0, 0)
            dvp_ref[kq, prow, :] += _dot(pp.astype(BF16), dob, 0, 0)
            db_ref[bq, :, :BLK] += dsp
            db_ref[bq, :, BLK:] += dsc
            if has_sink:
                dsk_ref[qq] += jnp.sum(jnp.exp(s_ref[qq][:, :1] - lse_v) * coef)

        _band_sweep(length, ns, one)
        dk_ref[...] += dkp_ref[...]
        dv_ref[...] += dvp_ref[...]

    xspecs, qspec, kvspec, bspec, sspec, colspec = _band_specs(length, g, bias_div, offs)
    return pl.pallas_call(
        body, name=name,
        out_shape=[jax.ShapeDtypeStruct((nq, length, HEAD_DIM), F32), jax.ShapeDtypeStruct((nk, length, HEAD_DIM), F32),
                   jax.ShapeDtypeStruct((nk, length, HEAD_DIM), F32), jax.ShapeDtypeStruct((nbias, BLK, 2 * BLK), F32),
                   jax.ShapeDtypeStruct((nq, 1, LANES), F32)],
        grid=(nq // ns,),
        in_specs=xspecs + [bspec, sspec, qspec, colspec, qspec, colspec],
        out_specs=[qspec, kvspec, kvspec, bspec, sspec],
        scratch_shapes=[pltpu.VMEM((ns, length, HEAD_DIM), F32), pltpu.VMEM((ns, length, HEAD_DIM), F32)],
        compiler_params=_params(("arbitrary",)),
    )(x, x, x, bias, sink, o, lse, do, dlse)


TOK_TILE = 512


def _dil_merge(outs, lses, dout, name):
    tr = TOK_TILE
    dils = [d for _, d in DIL_PATTERNS]
    n = len(dils)
    o4 = [o.reshape(2, d, SEQ // d, HEAD_DIM) for o, d in zip(outs, dils)]
    l4 = [l.reshape(2, d, SEQ // d, 1) for l, d in zip(lses, dils)]
    o_specs = [pl.BlockSpec((2, d, tr // d, HEAD_DIM), lambda i: (0, 0, i, 0)) for d in dils]
    l_specs = [pl.BlockSpec((2, d, tr // d, 1), lambda i: (0, 0, i, 0)) for d in dils]
    tok = pl.BlockSpec((tr, 2 * HEAD_DIM), lambda i: (i, 0))
    scratch = ([pltpu.VMEM((tr, 2 * HEAD_DIM), F32) for _ in dils] + [pltpu.VMEM((tr, 1), F32) for _ in range(2 * n)]
               + [pltpu.VMEM((tr // d, 2 * HEAD_DIM), F32) for d in dils])

    def to_tokens(o_ref, l_ref, d, pair, cols, stage):
        for r in range(d):
            rows = pl.ds(r, tr // d, stride=d) if d > 1 else slice(None)
            stage[:, :HEAD_DIM] = o_ref[0, r]
            stage[:, HEAD_DIM:] = o_ref[1, r]
            pair[rows, :] = stage[...]
            for h in range(2):
                cols[h][rows, :] = l_ref[h, r]
        return pair[...], [cols[0][...], cols[1][...]]

    def weights(ls):
        left = lax.broadcasted_iota(jnp.int32, (tr, 2 * HEAD_DIM), 1) < HEAD_DIM
        per_head = []
        for h in range(2):
            m = ls[0][h]
            for g in range(1, n):
                m = jnp.maximum(m, ls[g][h])
            es = [jnp.exp(ls[g][h] - m) for g in range(n)]
            den = es[0]
            for e in es[1:]:
                den = den + e
            per_head.append([e / den for e in es])
        return per_head, [jnp.where(left, per_head[0][g], per_head[1][g]) for g in range(n)], left

    def load(refs):
        pairs, cols, stages = refs[:n], refs[n:3 * n], refs[3 * n:]
        return pairs, [cols[2 * g:2 * g + 2] for g in range(n)], stages

    if dout is None:
        def body(*refs):
            pairs, cols, stages = load(refs[2 * n + 1:])
            toks = [to_tokens(refs[g], refs[n + g], dils[g], pairs[g], cols[g], stages[g]) for g in range(n)]
            _, alphas, _ = weights([t[1] for t in toks])
            acc = alphas[0] * toks[0][0]
            for g in range(1, n):
                acc = acc + alphas[g] * toks[g][0]
            refs[2 * n][...] = acc

        return pl.pallas_call(
            body, name=name, out_shape=jax.ShapeDtypeStruct((SEQ, 2 * HEAD_DIM), F32), grid=(SEQ // tr,),
            in_specs=o_specs + l_specs, out_specs=tok, scratch_shapes=scratch, compiler_params=_params(("parallel",)),
        )(*o4, *l4)

    def body(*refs):
        do_refs, dl_refs = refs[2 * n + 1:3 * n + 1], refs[3 * n + 1:4 * n + 1]
        pairs, cols, stages = load(refs[4 * n + 1:])
        toks = [to_tokens(refs[g], refs[n + g], dils[g], pairs[g], cols[g], stages[g]) for g in range(n)]
        per_head, alphas, left = weights([t[1] for t in toks])
        dov = refs[2 * n][...]
        das = []
        for g in range(n):
            prod = dov * toks[g][0]
            das.append([jnp.sum(jnp.where(left, prod, 0.0), axis=1, keepdims=True),
                        jnp.sum(jnp.where(left, 0.0, prod), axis=1, keepdims=True)])
        dbar = [sum(per_head[h][g] * das[g][h] for g in range(n)) for h in range(2)]
        for g, d in enumerate(dils):
            pairs[g][...] = alphas[g] * dov
            for h in range(2):
                cols[g][h][...] = per_head[h][g] * (das[g][h] - dbar[h])
            for r in range(d):
                rows = pl.ds(r, tr // d, stride=d) if d > 1 else slice(None)
                v = pairs[g][rows, :]
                for h in range(2):
                    do_refs[g][h, r] = v[:, h * HEAD_DIM:(h + 1) * HEAD_DIM]
                    dl_refs[g][h, r] = cols[g][h][rows, :]

    out = pl.pallas_call(
        body, name=name,
        out_shape=[jax.ShapeDtypeStruct(o.shape, F32) for o in o4] + [jax.ShapeDtypeStruct(l.shape, F32) for l in l4],
        grid=(SEQ // tr,), in_specs=o_specs + l_specs + [tok], out_specs=o_specs + l_specs, scratch_shapes=scratch,
        compiler_params=_params(("parallel",)),
    )(*o4, *l4, dout)
    return [t.reshape(s.shape) for t, s in zip(out, list(outs) + list(lses))]


def _tri(cmp):
    r = lax.broadcasted_iota(jnp.int32, (SB_TILE, SB_TILE), 0)
    c = lax.broadcasted_iota(jnp.int32, (SB_TILE, SB_TILE), 1)
    return cmp(r, c).astype(BF16)


def _cum(x, tri, terms):
    acc, rest = None, x
    for _ in range(terms):
        part = rest.astype(BF16)
        rest = rest - part.astype(F32)
        d = _dot(part, tri, 1, 0)
        acc = d if acc is None else acc + d
    return acc


def _sb_logits(q, ks, diagonal):
    t = SB_TILE
    z = _dot(q, ks, 1, 1)
    e = jnp.exp(-jnp.abs(z))
    lf = -(jnp.maximum(z, 0.0) + jnp.log(1.0 + e))
    if not diagonal:
        return z, e, lf, None
    mask = lax.broadcasted_iota(jnp.int32, (t, t), 1) < lax.broadcasted_iota(jnp.int32, (t, t), 0)
    return z, e, jnp.where(mask, lf, 0.0), mask


def _sb_specs(h, s):
    t = SB_TILE
    tile = pl.BlockSpec((h, t, HEAD_DIM), lambda i: (0, i, 0))
    keys = pl.BlockSpec((h, s, HEAD_DIM), lambda i: (1, 0, 0))
    values = pl.BlockSpec((h, s, HEAD_DIM), lambda i: (2, 0, 0))
    return tile, keys, values, pl.BlockSpec((h, t, 1), lambda i: (0, i, 0))


def _sb_fwd(x, name):
    h, s = x.shape[0] // 3, x.shape[1]
    t = SB_TILE

    def body(q_ref, k_ref, v_ref, o_ref, tot_ref):
        i = pl.program_id(0)
        after = _tri(lambda r, c: r > c)

        def tile(j, carry, diagonal):
            rows = pl.ds(pl.multiple_of(j * t, t), t)
            out = []
            for hh, (right, acc) in enumerate(carry):
                z, _, lf, mask = _sb_logits(q_ref[hh], k_ref[hh, rows, :], diagonal)
                w = jnp.exp(z + lf + (right + _cum(lf, after, 2)))
                w = w if mask is None else jnp.where(mask, w, 0.0)
                out.append((right + jnp.sum(lf, axis=1, keepdims=True), acc + _dot(w.astype(BF16), v_ref[hh, rows, :], 1, 0)))
            return tuple(out)

        carry = tile(i, tuple((jnp.zeros((t, 1), F32), jnp.zeros((t, HEAD_DIM), F32)) for _ in range(h)), True)
        carry = lax.fori_loop(0, i, lambda jj, c: tile(i - 1 - jj, c, False), carry)
        for hh, (right, acc) in enumerate(carry):
            o_ref[hh] = acc
            tot_ref[hh] = right

    tile_spec, keys, values, col = _sb_specs(h, s)
    return pl.pallas_call(
        body, name=name, out_shape=[jax.ShapeDtypeStruct((h, s, HEAD_DIM), F32), jax.ShapeDtypeStruct((h, s, 1), F32)],
        grid=(s // t,), in_specs=[tile_spec, keys, values], out_specs=[tile_spec, col],
        compiler_params=_params(("parallel",)),
    )(x, x, x)


def _sb_bwd(x, tot, do, name):
    h, s = x.shape[0] // 3, x.shape[1]
    t = SB_TILE

    def body(q_ref, k_ref, v_ref, tot_ref, do_ref, dq_ref, dk_ref, dv_ref):
        i = pl.program_id(0)

        @pl.when(i == 0)
        def _():
            dk_ref[...] = jnp.zeros_like(dk_ref)
            dv_ref[...] = jnp.zeros_like(dv_ref)

        upto = _tri(lambda r, c: r <= c)
        before = _tri(lambda r, c: r < c)

        def tile(j, carry, diagonal):
            rows = pl.ds(pl.multiple_of(j * t, t), t)
            out = []
            for hh, (left, cleft, dq) in enumerate(carry):
                qv, ks, dob = q_ref[hh], k_ref[hh, rows, :], do_ref[hh].astype(BF16)
                z, e, lf, mask = _sb_logits(qv, ks, diagonal)
                between = tot_ref[hh] - (left + _cum(lf, upto, 2))
                w = jnp.exp(z + lf + between)
                w = w if mask is None else jnp.where(mask, w, 0.0)
                dlog = w * _dot(dob, v_ref[hh, rows, :], 1, 1)
                cfail = cleft + _cum(dlog, before, 2)
                sig = jnp.where(z >= 0.0, 1.0, e) / (1.0 + e)
                dz = dlog * (1.0 - sig) - sig * cfail
                dz = (dz if mask is None else jnp.where(mask, dz, 0.0)).astype(BF16)
                dk_ref[hh, rows, :] += _dot(dz, qv, 0, 0)
                dv_ref[hh, rows, :] += _dot(w.astype(BF16), dob, 0, 0)
                out.append((left + jnp.sum(lf, axis=1, keepdims=True), cleft + jnp.sum(dlog, axis=1, keepdims=True),
                            dq + _dot(dz, ks, 1, 0)))
            return tuple(out)

        zero = jnp.zeros((t, 1), F32)
        carry = lax.fori_loop(0, i, lambda j, c: tile(j, c, False),
                              tuple((zero, zero, jnp.zeros((t, HEAD_DIM), F32)) for _ in range(h)))
        for hh, (_, _, dq) in enumerate(tile(i, carry, True)):
            dq_ref[hh] = dq * (HEAD_DIM ** -0.5)

    tile_spec, keys, values, col = _sb_specs(h, s)
    full = pl.BlockSpec((h, s, HEAD_DIM), lambda i: (0, 0, 0))
    shp = jax.ShapeDtypeStruct((h, s, HEAD_DIM), F32)
    return pl.pallas_call(
        body, name=name, out_shape=[shp, shp, shp], grid=(s // t,),
        in_specs=[tile_spec, keys, values, col, tile_spec],
        out_specs=[tile_spec, full, full], compiler_params=_params(("arbitrary",)),
    )(x, x, x, tot, do)


COL_SB, COL_DIL, COL_SWA = 0, 3 * H_SB * HEAD_DIM, 3 * H_SB * HEAD_DIM + 3 * H_DIL * HEAD_DIM
N_SWA = H_SWA_Q + 2 * H_SWA_KV


def _dil_col(t, g):
    return COL_DIL + t * H_DIL * HEAD_DIM + g * 2 * HEAD_DIM


def _split_heads(qkv, name):
    tr = TOK_TILE
    scale = HEAD_DIM ** -0.5
    dils = [d for _, d in DIL_PATTERNS]

    def body(x_ref, sb_ref, d0_ref, d1_ref, d2_ref, swa_ref, pair):
        def head(col, scaled):
            v = x_ref[:, col:col + HEAD_DIM]
            return (v * scale if scaled else v).astype(BF16)

        for hh in range(3 * H_SB):
            sb_ref[hh] = head(COL_SB + hh * HEAD_DIM, hh < H_SB)
        for hh in range(N_SWA):
            swa_ref[hh] = head(COL_SWA + hh * HEAD_DIM, hh < H_SWA_Q)
        for t in range(3):
            for g, (d, out_ref) in enumerate(zip(dils, (d0_ref, d1_ref, d2_ref))):
                col = _dil_col(t, g)
                if d == 1:
                    for h in range(2):
                        out_ref[t * 2 + h] = head(col + h * HEAD_DIM, t == 0)
                    continue
                pair[...] = x_ref[:, col:col + 2 * HEAD_DIM]
                for r in range(d):
                    v = pair[pl.ds(r, tr // d, stride=d), :]
                    v = v * scale if t == 0 else v
                    for h in range(2):
                        out_ref[t * 2 * d + h * d + r] = v[:, h * HEAD_DIM:(h + 1) * HEAD_DIM].astype(BF16)

    def heads(n, length):
        return jax.ShapeDtypeStruct((n, length, HEAD_DIM), BF16)

    def spec(n, rows):
        return pl.BlockSpec((n, rows, HEAD_DIM), lambda i: (0, i, 0))

    return pl.pallas_call(
        body, name=name,
        out_shape=[heads(3 * H_SB, SEQ)] + [heads(6 * d, SEQ // d) for d in dils] + [heads(N_SWA, SEQ)],
        grid=(SEQ // tr,), in_specs=[pl.BlockSpec((tr, D_QKV), lambda i: (i, 0))],
        out_specs=[spec(3 * H_SB, tr)] + [spec(6 * d, tr // d) for d in dils] + [spec(N_SWA, tr)],
        scratch_shapes=[pltpu.VMEM((tr, 2 * HEAD_DIM), F32)], compiler_params=_params(("parallel",)),
    )(qkv)


def _join_heads(sb, dil, swa, name):
    tr = TOK_TILE
    dils = [d for _, d in DIL_PATTERNS]

    def body(*refs):
        sb_refs, dil_refs, swa_refs = refs[:3], [refs[3 + 3 * g:6 + 3 * g] for g in range(3)], refs[12:15]
        o_ref, pair, stages = refs[15], refs[16], refs[17:]

        def put(col, v):
            o_ref[:, col:col + v.shape[1]] = v.astype(BF16)

        for t in range(3):
            for h in range(H_SB):
                put(COL_SB + (t * H_SB + h) * HEAD_DIM, sb_refs[t][h])
        col = COL_SWA
        for ref in swa_refs:
            for h in range(ref.shape[0]):
                put(col, ref[h])
                col += HEAD_DIM
        for t in range(3):
            for g, d in enumerate(dils):
                ref, col = dil_refs[g][t], _dil_col(t, g)
                if d == 1:
                    for h in range(2):
                        put(col + h * HEAD_DIM, ref[h])
                    continue
                stage = stages[g - 1]
                for r in range(d):
                    stage[:, :HEAD_DIM] = ref[r]
                    stage[:, HEAD_DIM:] = ref[d + r]
                    pair[pl.ds(r, tr // d, stride=d), :] = stage[...]
                put(col, pair[...])

    def spec(n, rows):
        return pl.BlockSpec((n, rows, HEAD_DIM), lambda i: (0, i, 0))

    ins = list(sb) + [t for g in range(3) for t in dil[g]] + list(swa)
    in_specs = ([spec(H_SB, tr)] * 3 + [spec(2 * d, tr // d) for d in dils for _ in range(3)]
                + [spec(H_SWA_Q, tr), spec(H_SWA_KV, tr), spec(H_SWA_KV, tr)])
    return pl.pallas_call(
        body, name=name, out_shape=jax.ShapeDtypeStruct((SEQ, D_QKV), BF16), grid=(SEQ // tr,), in_specs=in_specs,
        out_specs=pl.BlockSpec((tr, D_QKV), lambda i: (i, 0)),
        scratch_shapes=[pltpu.VMEM((tr, 2 * HEAD_DIM), F32)] + [pltpu.VMEM((tr // d, 2 * HEAD_DIM), F32) for d in dils[1:]],
        compiler_params=_params(("parallel",)),
    )(*ins)


def _mixer_fwd(qkv, bias, sinks_l, tag):
    sb, d0, d1, d2, swa = _split_heads(qkv, name=f"split_heads_{tag}")
    st = {"sb": sb, "dil": (d0, d1, d2), "swa": swa}
    o_sb, st["sb_tot"] = _sb_fwd(sb, name=f"sb_fwd_{tag}")
    st["dil_out"], st["dil_lse"], st["dil_sink"] = [], [], []
    for gi, (_, d) in enumerate(DIL_PATTERNS):
        sink = jnp.zeros((2 * d, 1, LANES), F32)
        og, lg = _band_fwd(st["dil"][gi], bias[2 * gi:2 * gi + 2], sink, nq=2 * d, offs=(0, 2 * d, 4 * d), g=1, bias_div=d,
                           has_sink=False, name=f"dil{gi}_fwd_{tag}")
        st["dil_out"].append(og)
        st["dil_lse"].append(lg)
        st["dil_sink"].append(sink)
    o_dil = _dil_merge(st["dil_out"], st["dil_lse"], None, name=f"dil_merge_fwd_{tag}")
    st["swa_sink"] = jnp.broadcast_to(sinks_l.reshape(H_SWA_Q, 1, 1), (H_SWA_Q, 1, LANES))
    st["swa_out"] = _band_fwd(swa, bias[H_DIL:], st["swa_sink"], nq=H_SWA_Q, offs=(0, H_SWA_Q, H_SWA_Q + H_SWA_KV),
                              g=H_SWA_Q // H_SWA_KV, bias_div=1, has_sink=True, name=f"swa_fwd_{tag}")
    return (o_sb, o_dil, st["swa_out"][0]), st


def _mixer_bwd(st, bias, do_sb, do_dil, do_swa, tag):
    d_sb = _sb_bwd(st["sb"], st["sb_tot"], do_sb, name=f"sb_bwd_{tag}")
    dmerge = _dil_merge(st["dil_out"], st["dil_lse"], do_dil, name=f"dil_merge_bwd_{tag}")
    d_dil, dbs = [], []
    for gi, (_, d) in enumerate(DIL_PATTERNS):
        dq, dk, dv, db, _ = _band_bwd(st["dil"][gi], bias[2 * gi:2 * gi + 2], st["dil_sink"][gi], st["dil_out"][gi],
                                      st["dil_lse"][gi], dmerge[gi], dmerge[3 + gi], nq=2 * d, offs=(0, 2 * d, 4 * d),
                                      g=1, bias_div=d, has_sink=False, name=f"dil{gi}_bwd_{tag}")
        d_dil.append((dq, dk, dv))
        dbs.append(db)
    o_sw, l_sw = st["swa_out"]
    dq_sw, dk_sw, dv_sw, db_sw, dsink = _band_bwd(st["swa"], bias[H_DIL:], st["swa_sink"], o_sw, l_sw, do_swa,
                                                  jnp.zeros_like(l_sw), nq=H_SWA_Q, offs=(0, H_SWA_Q, H_SWA_Q + H_SWA_KV),
                                                  g=H_SWA_Q // H_SWA_KV, bias_div=1, has_sink=True, name=f"swa_bwd_{tag}")
    dqkv = _join_heads(d_sb, d_dil, (dq_sw, dk_sw, dv_sw), name=f"join_heads_{tag}")
    return dqkv, jnp.concatenate(dbs + [db_sw], 0), dsink[:, 0, 0]


PIECES = ("ffn0", "mix", "ffn1")


def _ffn_fwd(x_in, w, gain, mod_j, tag, after=None):
    st = {"x": x_in}
    st["h"] = _norm_fwd(x_in, _row(gain), _row(mod_j[1]), _row(mod_j[0]), name=f"norm_fwd_{tag}", after=after)
    st["a"], st["u"], st["s"] = _ffn_up(st["h"], w["gate"], w["up"], name=f"up_{tag}")
    w = dict(w, down=w["down"](st["s"])) if callable(w["down"]) else w
    st["w"] = w
    st["f"], x_out = _mm(st["s"], w["down"], res=x_in, colscale=_row(0.5 * mod_j[2]), emit_acc=True, tm=512, tn=1024,
                         name=f"down_{tag}")
    return x_out, st


def _ffn_bwd(dx_out, st, gain, mod_j, tag, done, pre, nxt):
    w = st["w"]

    def latest(new, old):
        return old if new is None else new

    df, dgate = pre or _gate_bwd(dx_out, st["f"], _row(0.5 * mod_j[2]), 0.5, name=f"gate_bwd_{tag}")
    dwd = _mm_tn(st["s"], df, tm=D_FF // 2, name=f"dwd_{tag}")
    token = latest(done({"down": dwd}), dwd)
    da, du = _ffn_bwd_ds(df, w["down"], st["a"], st["u"], name=f"ds_{tag}")
    dwg = _mm_tn(da, st["h"], after=token, tm=D_FF // 2, name=f"dwg_{tag}")
    token = latest(done({"gate": dwg}), dwg)
    dwu = _mm_tn(du, st["h"], after=token, tm=D_FF // 2, name=f"dwu_{tag}")
    token = latest(done({"up": dwu}), dwu)
    dx_in, sum_dh, sum_dhx, made = _dh_norm_bwd(da, w["gate"], du, w["up"], st["x"], dx_out, _row(gain), _row(mod_j[1]), nxt,
                                                after=token, name=f"dh_{tag}")
    dmod = jnp.concatenate([sum_dh, gain * sum_dhx, dgate], 0)
    return dx_in, dmod, (1.0 + mod_j[1]) * sum_dhx[0], made


def _mix_fwd(x_in, w, gain, mod_j, bias, sinks_l, tag, after=None):
    st = {"x": x_in, "w": w}
    st["h"] = _norm_fwd(x_in, _row(gain), _row(mod_j[1]), _row(mod_j[0]), name=f"norm_fwd_mix_{tag}", after=after)
    qkv = _mm(st["h"], w["in"], tb=True, tm=SEQ, b_rows=(0, D_QKV), name=f"qkv_{tag}")
    st["gates"] = _mm(st["h"], w["in"], tb=True, tm=SEQ, b_rows=(D_QKV, D_GATES), name=f"gates_{tag}")
    outs, st["mix"] = _mixer_fwd(qkv, bias, sinks_l, tag)
    st["merged"], *st["t"] = _merge_fwd(*outs, st["gates"], w["br_sb"], w["br_dil"], w["br_swa"], name=f"merge_fwd_{tag}")
    st["f"], x_out = _mm(st["merged"], w["out"], res=x_in, colscale=_row(mod_j[2]), emit_acc=True, name=f"out_{tag}")
    return x_out, st


def _mix_bwd(dx_out, st, gain, mod_j, bias, tag, done, pre, nxt):
    w = st["w"]
    df, dgate = pre or _gate_bwd(dx_out, st["f"], _row(mod_j[2]), 1.0, name=f"gate_bwd_mix_{tag}")
    g = {"out": _mm_tn(st["merged"], df, name=f"dw_out_{tag}")}
    dmerged = _mm(df, w["out"], tb=True, name=f"dmerged_{tag}")
    dgates, do_sb, do_dil, do_swa, dbr_sb, dbr_dil, dbr_swa = _merge_bwd(
        dmerged, *st["t"], st["gates"], w["br_sb"], w["br_dil"], w["br_swa"], name=f"merge_bwd_{tag}")
    g["br_sb"] = _mm_tn(st["t"][0], dbr_sb, name=f"dw_br_sb_{tag}")
    g["br_dil"] = _mm_tn(st["t"][1], dbr_dil, name=f"dw_br_dil_{tag}")
    g["br_swa"] = _mm_tn(st["t"][2], dbr_swa, name=f"dw_br_swa_{tag}")
    dqkv, dbias, dsinks = _mixer_bwd(st["mix"], bias, do_sb, do_dil, do_swa, tag)
    dw_qkv = _mm_tn(dqkv, st["h"], out_rows=D_QKV + D_GATES, name=f"dw_qkv_{tag}")
    g["in"] = _mm_tn(dgates, st["h"], out_rows=D_QKV + D_GATES, row0=D_QKV, prev=dw_qkv, name=f"dw_gates_{tag}")
    dx_in, sum_dh, sum_dhx, made = _dh_norm_bwd(dqkv, w["in"], dgates, w["in"], st["x"], dx_out, _row(gain), _row(mod_j[1]),
                                                nxt, after=done(g), b_rows=(0, D_QKV), name=f"dh_mix_{tag}")
    dmod = jnp.concatenate([sum_dh, gain * sum_dhx, dgate], 0)
    return dx_in, dmod, (1.0 + mod_j[1]) * sum_dhx[0], dbias, dsinks, made


def _local_step(x, target, mod, gains, weights_of, rel_bias, sinks, final_gain, grads_done):
    tables = jnp.asarray(_bucket_tables())
    bias = _bias_build(rel_bias, tables, name="bias_build")
    states, h = [], x
    for l in range(DEPTH):
        st = {}
        for j, piece in enumerate(PIECES):
            w, after = weights_of(l, piece, h)
            if piece == "mix":
                h, st[piece] = _mix_fwd(h, w, gains[l, j], mod[l, j], bias, sinks[l], f"l{l}", after)
            else:
                h, st[piece] = _ffn_fwd(h, w, gains[l, j], mod[l, j], f"{piece}_l{l}", after)
        states.append(st)
    loss, dx, dfinal = _final_loss(h, target, _row(final_gain), name="final_loss")
    dmods = [[None] * 3 for _ in range(DEPTH)]
    dgains = [[None] * 3 for _ in range(DEPTH)]
    dsinks = [None] * DEPTH
    dbias, made = None, None
    sweep = [(l, j) for l in reversed(range(DEPTH)) for j in reversed(range(3))]
    for k, (l, j) in enumerate(sweep):
        piece = PIECES[j]
        done = lambda grads, l=l, piece=piece: grads_done(l, piece, grads)
        nxt = None
        if k + 1 < len(sweep):
            nl, nj = sweep[k + 1]
            coef = 1.0 if PIECES[nj] == "mix" else 0.5
            nxt = (states[nl][PIECES[nj]]["f"], _row(coef * mod[nl, nj, 2]), coef)
        if piece == "mix":
            dx, dmods[l][j], dgains[l][j], db, dsinks[l], made = _mix_bwd(
                dx, states[l][piece], gains[l, j], mod[l, j], bias, f"l{l}", done, made, nxt)
            dbias = db if dbias is None else dbias + db
        else:
            dx, dmods[l][j], dgains[l][j], made = _ffn_bwd(
                dx, states[l][piece], gains[l, j], mod[l, j], f"{piece}_l{l}", done, made, nxt)
    drel = _bias_grad(dbias, tables, name="bias_grad")[:, 0, :N_BUCKETS].T
    dmod = jnp.stack([jnp.stack(m) for m in dmods])
    dgain = jnp.stack([jnp.stack(g) for g in dgains])
    return loss, dx, dmod, dgain, dfinal[0], drel, jnp.stack(dsinks)


BR_ROWS = (H_SB * HEAD_DIM, 2 * HEAD_DIM, H_SWA_Q * HEAD_DIM)


def _lanes_unshard(g, lead):
    _, rows, _ = g.shape
    r = rows // lead
    return g.reshape(N_DEV, lead, r, LANES).transpose(1, 2, 0, 3).reshape(lead, r, N_DEV * LANES)


def _lanes_shard(full):
    lead, r, _ = full.shape
    return full.reshape(lead, r, N_DEV, LANES).transpose(2, 0, 1, 3).reshape(N_DEV, lead * r, LANES)


def _pack_rows(parts, dtype):
    flat = jnp.concatenate([p.astype(dtype).reshape(-1) for p in parts])
    pad = (-flat.shape[0]) % (16 * LANES)
    if pad:
        flat = jnp.concatenate([flat, jnp.zeros((pad,), dtype)])
    return flat.reshape(-1, LANES)


def _unshard(gathered, axis):
    moved = jnp.moveaxis(gathered, 0, axis)
    shape = list(moved.shape)
    shape[axis:axis + 2] = [shape[axis] * shape[axis + 1]]
    return moved.reshape(shape)


def kernel(x, c, w_ada, b_ada, norm_gain, w_ffn_gate, w_ffn_up, w_ffn_down, w_in, w_br_sb, w_br_dil, w_br_swa, w_out, sinks, rel_bias, final_gain, loss_target, m_w_ada, m_b_ada, m_norm_gain, m_w_ffn_gate, m_w_ffn_up, m_w_ffn_down, m_w_in, m_w_br_sb, m_w_br_dil, m_w_br_swa, m_w_out, m_sinks, m_rel_bias, m_final_gain, v_w_ada, v_b_ada, v_norm_gain, v_w_ffn_gate, v_w_ffn_up, v_w_ffn_down, v_w_in, v_w_br_sb, v_w_br_dil, v_w_br_swa, v_w_out, v_sinks, v_rel_bias, v_final_gain):
    me = 4 * lax.axis_index("x") + 2 * lax.axis_index("y") + lax.axis_index("c")
    d = D_MODEL
    gate_t, up_t, in_t = jnp.swapaxes(w_ffn_gate, 2, 3), jnp.swapaxes(w_ffn_up, 2, 3), jnp.swapaxes(w_in, 1, 2)

    def piece_shards(l, piece):
        bf = lambda t: t.astype(BF16)
        if piece == "mix":
            return [bf(in_t[l]), jnp.concatenate([bf(w_br_sb[l]), bf(w_br_dil[l]), bf(w_br_swa[l])], 0), bf(w_out[l])]
        i = PIECES.index(piece) // 2
        return [bf(gate_t[l, i]), bf(up_t[l, i]), bf(w_ffn_down[l, i])]

    br_off = np.concatenate([[0], np.cumsum(BR_ROWS)])

    def piece_weights(gathered, piece):
        if piece == "mix":
            g_in, g_br, g_out = gathered
            f_br = [_lanes_unshard(g_br[:, br_off[k]:br_off[k + 1]], 1)[0] for k in range(3)]
            return {"in": g_in.reshape(D_QKV + D_GATES, d), "br_sb": f_br[0], "br_dil": f_br[1], "br_swa": f_br[2],
                    "out": g_out.reshape(d, d)}
        return {n: g.reshape(D_FF, d) for n, g in zip(("gate", "up", "down"), gathered)}

    order = [(l, piece) for l in range(DEPTH) for piece in PIECES]
    ahead = 3
    in_flight, passed = {}, {}

    def start_gather(k, after, part=slice(None), tag=""):
        l, piece = order[k]
        state, token = _relay_start(piece_shards(l, piece)[part], after, name=f"gather_{piece}{tag}_l{l}_start")
        in_flight.setdefault(k, []).append(state)
        return token

    small, = _all_gather([_pack_rows([c, norm_gain], F32)], after=start_gather(0, c, slice(0, 2)), name="gather_cond")
    c_all = small[:, :d // LANES].reshape(N_DEV, d)
    gains = _unshard(small[:, d // LANES:d // LANES + 6].reshape(N_DEV, DEPTH, 3, LANES), 2)

    cols = w_ada.shape[2]
    mod_cols = jnp.stack([_ada_fwd(c_all, w_ada[l], name=f"ada_fwd_l{l}") for l in range(DEPTH)])
    mod_all, = _all_gather([_pack_rows([mod_cols], F32)], name="gather_mod")
    mod_all = mod_all.reshape(N_DEV, -1)[:, :DEPTH * N_DEV * cols].reshape(N_DEV, DEPTH, N_DEV, cols)
    mod_mine = lax.dynamic_index_in_dim(mod_all, me, axis=2, keepdims=False)
    mod = (mod_mine.transpose(1, 0, 2).reshape(DEPTH, N_DEV * cols) + b_ada).reshape(DEPTH, 3, 3, d)

    token = start_gather(0, mod_all, slice(2, 3), "_down")
    for k in range(1, 1 + ahead):
        token = start_gather(k, token)
    mod = mod + token[0, 0]

    def first_down(s):
        state, token = _relay_pass(in_flight[0][1], s, name="gather_ffn0_down_l0_pass")
        return _relay_wait(state, token, name="gather_ffn0_down_l0_wait")[0].reshape(D_FF, d)

    def weights_of(l, piece, h):
        k = order.index((l, piece))
        token = start_gather(k + ahead, h) if k + ahead < len(order) and k + ahead not in in_flight else None
        for nxt in ([k] if k < 3 else []) + ([k + 1] if 3 <= k + 1 < len(order) else []):
            nl, npiece = order[nxt]
            passed[nxt], token = _relay_pass(in_flight[nxt][0], h if token is None else token,
                                             name=f"gather_{npiece}_l{nl}_pass")
        landed = _relay_wait(passed[k], h if token is None else token, name=f"gather_{piece}_l{l}_wait")
        weights = piece_weights(landed, piece)
        if k == 0:
            weights["down"] = first_down
        return weights, token

    exchanges, have, deferred = {}, {}, []

    def grads_done(l, piece, g):
        key = (l, piece)
        have.setdefault(key, {}).update(g)
        if piece == "mix":
            if len(have[key]) < 5:
                return None
            g = have[key]
            s_br = jnp.concatenate([_lanes_shard(g[n][None]) for n in ("br_sb", "br_dil", "br_swa")], 1)
            groups = [(("in", "br", "out"), [g["in"].reshape(N_DEV, -1, d), s_br, g["out"].reshape(N_DEV, -1, d)])]
        elif key == order[0]:
            deferred.extend(((n,), [t.reshape(N_DEV, -1, d)]) for n, t in g.items())
            return None
        elif len(have[key]) < 3:
            return None
        else:
            groups = [(("gate", "up", "down"), [have[key][n].reshape(N_DEV, -1, d) for n in ("gate", "up", "down")])]
        token = None
        for names, sg in groups:
            state, token = _exchange_start(sg, None, name=f"exchange_{piece}_l{l}_{names[0]}_start")
            exchanges.setdefault(key, []).append((names, state))
        return token

    loss, dx, dmod, dgains, dfinal, drel, dsinks = _local_step(
        x[0], loss_target[0], mod, gains, weights_of, rel_bias, sinks, final_gain, grads_done)

    flat = lambda t: t.reshape(-1, t.shape[-1])
    transposed = lambda ts: tuple(flat(jnp.swapaxes(t, -1, -2)) for t in ts)
    families = {
        "gate": transposed((w_ffn_gate, m_w_ffn_gate, v_w_ffn_gate)), "up": transposed((w_ffn_up, m_w_ffn_up, v_w_ffn_up)),
        "down": tuple(flat(t) for t in (w_ffn_down, m_w_ffn_down, v_w_ffn_down)),
        "in": transposed((w_in, m_w_in, v_w_in)),
        "br": tuple(flat(jnp.concatenate(ts, 1)) for ts in ((w_br_sb, w_br_dil, w_br_swa), (m_w_br_sb, m_w_br_dil, m_w_br_swa),
                                                            (v_w_br_sb, v_w_br_dil, v_w_br_swa))),
        "out": tuple(flat(t) for t in (w_out, m_w_out, v_w_out))}
    parts, stepped = {}, {}

    def step(keys, after):
        for key in keys:
            for names, ex_state in exchanges[key]:
                landed = _exchange_wait(ex_state, after, name=f"exchange_{key[1]}_l{key[0]}_{names[0]}_wait")
                parts.setdefault(key, {}).update(zip(names, landed))
                after = landed[0]
        for key in keys:
            l, piece = key
            for n, group in parts[key].items():
                w2, m2, v2 = families[n]
                rows = group.shape[1]
                row0 = (2 * l + PIECES.index(piece) // 2) * rows if piece != "mix" else l * rows
                stepped[n] = _reduce_adamw([group], w2, m2, v2, row0, stepped.get(n), after=after,
                                           name=f"reduce_adamw_{n}_{piece}_l{l}")
                after = stepped[n][1]
        return after

    small_parts = [dmod, dgains, dfinal, drel.T, dsinks, loss[0, :1]]
    small_sizes = [int(np.prod(p.shape)) for p in small_parts]
    small_all, = _all_gather([_pack_rows(small_parts, F32)], name="gather_small")
    token = small_all
    for names, sg in deferred:
        state, token = _exchange_start(sg, token, name=f"exchange_ffn0_l0_{names[0]}_start")
        exchanges.setdefault(order[0], []).append((names, state))

    after_l1 = step([key for key in reversed(order) if key[0] == 1], token)
    small_sum = _sum_parts(small_all, name="sum_small", after=token).reshape(-1)
    offs = np.concatenate([[0], np.cumsum(small_sizes)])
    g_b_ada = small_sum[offs[0]:offs[1]].reshape(DEPTH, 9 * d)
    g_gain_full = small_sum[offs[1]:offs[2]].reshape(DEPTH, 3, d)
    g_norm_gain = lax.dynamic_slice_in_dim(g_gain_full, me * LANES, LANES, axis=2)
    g_final = small_sum[offs[2]:offs[3]]
    g_rel = small_sum[offs[3]:offs[4]].reshape(N_SOFT, N_BUCKETS).T
    g_sinks = small_sum[offs[4]:offs[5]].reshape(DEPTH, H_SWA_Q)
    loss_total = small_sum[offs[5]]

    dmod_all = small_all.reshape(N_DEV, -1)[:, :DEPTH * 9 * d].reshape(N_DEV, DEPTH, 9 * d)
    dmod_cols = lax.dynamic_slice_in_dim(dmod_all, me * cols, cols, axis=2)
    g_w_ada = jnp.stack([_ada_bwd(c_all.T, dmod_cols[:, l], name=f"ada_bwd_l{l}") for l in range(DEPTH)])

    small_state = {"w_ada": (w_ada, m_w_ada, v_w_ada), "b_ada": (b_ada, m_b_ada, v_b_ada),
                   "norm_gain": (norm_gain, m_norm_gain, v_norm_gain), "sinks": (sinks, m_sinks, v_sinks),
                   "rel_bias": (rel_bias, m_rel_bias, v_rel_bias), "final_gain": (final_gain, m_final_gain, v_final_gain)}
    after = step([order[2], order[1]], after_l1)
    grad, update = {}, {}
    for n, g in (("w_ada", g_w_ada), ("b_ada", g_b_ada), ("norm_gain", g_norm_gain), ("sinks", g_sinks),
                 ("rel_bias", g_rel), ("final_gain", g_final)):
        w, m, v = small_state[n]
        grad[n] = g
        if w.ndim == 1:
            update[n] = tuple(t.reshape(w.shape)
                              for t in _adamw(_row(w), _row(g), _row(m), _row(v), name=f"adamw_{n}", after=after))
        else:
            update[n] = _adamw(w, g, m, v, name=f"adamw_{n}", after=after)
        after = update[n][0]

    step([order[0]], after)

    def unflat(n, like, swapped):
        shape = jnp.swapaxes(like, -1, -2).shape if swapped else like.shape
        out = [t.reshape(shape) for t in stepped[n]]
        return [jnp.swapaxes(t, -1, -2) for t in out] if swapped else out

    results = {"w_ffn_gate": unflat("gate", w_ffn_gate, True), "w_ffn_up": unflat("up", w_ffn_up, True),
               "w_ffn_down": unflat("down", w_ffn_down, False), "w_in": unflat("in", w_in, True),
               "w_out": unflat("out", w_out, False)}
    br = [t.reshape(DEPTH, -1, LANES) for t in stepped["br"]]
    for k, n in enumerate(("w_br_sb", "w_br_dil", "w_br_swa")):
        results[n] = [t[:, br_off[k]:br_off[k + 1]] for t in br]
    for n, (g, dl, nm, nv) in results.items():
        grad[n], update[n] = g, (dl, nm, nv)

    names = ["w_ada", "b_ada", "norm_gain", "w_ffn_gate", "w_ffn_up", "w_ffn_down", "w_in", "w_br_sb", "w_br_dil",
             "w_br_swa", "w_out", "sinks", "rel_bias", "final_gain"]
    return (loss_total, dx[None], *[grad[n] for n in names], *[update[n][0] for n in names],
            *[update[n][1] for n in names], *[update[n][2] for n in names])
```

```python
import math

import numpy as np
import jax
import jax.numpy as jnp
from jax import lax
from jax.experimental import pallas as pl
from jax.experimental.pallas import tpu as pltpu

F32, BF16 = jnp.float32, jnp.bfloat16

SEQ, D_MODEL, D_FF, HEAD_DIM = 2048, 1024, 2816, 64
DEPTH = 2
BLK = 128
H_SB, H_DIL, H_SWA_Q, H_SWA_KV = 4, 6, 6, 2
DIL_PATTERNS = ((128, 1), (512, 4), (2048, 16))
SWA_WINDOW = 128
N_BUCKETS, MAX_REL_DIST = 32, 2048
RMS_EPS = 1e-6
D_QKV = 2560
D_GATES = 3 * D_MODEL
ADAM_LR, ADAM_B1, ADAM_B2, ADAM_EPS, ADAM_WD, ADAM_STEP = 0.001, 0.9, 0.999, 1e-08, 0.01, 10

N_DEV = 8
LANES = 128
NEG = -1e30
SB_TILE = 512
VMEM_LIMIT_BYTES = 48 * 1024 * 1024
HBM = pl.BlockSpec(memory_space=pltpu.HBM)
MESH = pl.DeviceIdType.MESH


def _tile(n, target):
    t = (min(n, target) // LANES) * LANES
    while t >= LANES:
        if n % t == 0:
            return t
        t -= LANES
    return n


def _row_tile(r, cap):
    t = (min(r, cap) // 16) * 16
    while t > 16 and r % t:
        t -= 16
    return t


def _params(semantics=None):
    return pltpu.CompilerParams(dimension_semantics=semantics, vmem_limit_bytes=VMEM_LIMIT_BYTES)


def _dot(a, b, ca, cb):
    return lax.dot_general(a, b, (((ca,), (cb,)), ((), ())), preferred_element_type=F32)


def _sigmoid(a):
    return 1.0 / (1.0 + jnp.exp(-a))


def _row(v):
    return v.reshape(1, -1)


def _all_gather(arrs, name, after=None):
    n = len(arrs)
    ins = list(arrs) + ([] if after is None else [after])

    def body(*refs):
        x_refs, out_refs = refs[:n], refs[len(ins):len(ins) + n]
        send_sems, recv_sems, local_sems = refs[len(ins) + n:]
        x, y, c = lax.axis_index("x"), lax.axis_index("y"), lax.axis_index("c")
        me, sibling = (x, y, c), (x, y, 1 - c)
        chips = [(1 - x, y), (x, 1 - y), (1 - x, 1 - y)]

        def slot(t, px, py, pc):
            return out_refs[t].at[4 * px + 2 * py + pc]

        def copy(t, k, block, to, src=None):
            return pltpu.make_async_remote_copy(
                src_ref=slot(t, *block) if src is None else src, dst_ref=slot(t, *block),
                send_sem=send_sems.at[7 * t + k], recv_sem=recv_sems.at[7 * t + k], device_id=to, device_id_type=MESH)

        mine = [pltpu.make_async_copy(x_refs[t], slot(t, *me), local_sems.at[t]) for t in range(n)]
        for cp in mine:
            cp.start()
        first = []
        for t in range(n):
            first.append(copy(t, 0, me, sibling, src=x_refs[t]))
            first += [copy(t, 1 + j, me, (*chip, c), src=x_refs[t]) for j, chip in enumerate(chips)]
        for cp in first:
            cp.start()
        passed = []
        for j, chip in enumerate(chips):
            for t in range(n):
                copy(t, 1 + j, (*chip, c), me).wait_recv()
                passed.append(copy(t, 4 + j, (*chip, c), sibling))
                passed[-1].start()
        for t in range(n):
            copy(t, 0, sibling, me).wait_recv()
        for j, chip in enumerate(chips):
            for t in range(n):
                copy(t, 4 + j, (*chip, 1 - c), me).wait_recv()
        for cp in first + passed:
            cp.wait_send()
        for cp in mine:
            cp.wait()

    return pl.pallas_call(
        body, name=name, out_shape=[jax.ShapeDtypeStruct((N_DEV,) + a.shape, a.dtype) for a in arrs],
        in_specs=[HBM] * n + [pl.BlockSpec(memory_space=pl.ANY)] * (len(ins) - n), out_specs=[HBM] * n,
        scratch_shapes=[pltpu.SemaphoreType.DMA((7 * n,)), pltpu.SemaphoreType.DMA((7 * n,)), pltpu.SemaphoreType.DMA((n,))],
    )(*ins)


def _direct_copies(x_refs, land_refs, send_sems, recv_sems, local_sems):
    x, y, c = lax.axis_index("x"), lax.axis_index("y"), lax.axis_index("c")
    me = 4 * x + 2 * y + c
    sends, recvs = [], []
    for k in range(1, N_DEV):
        px = 1 - x if (k >> 2) & 1 else x
        py = 1 - y if (k >> 1) & 1 else y
        pc = 1 - c if k & 1 else c
        peer = 4 * px + 2 * py + pc
        for t, (x_ref, land_ref) in enumerate(zip(x_refs, land_refs)):
            sem = 7 * t + k - 1
            for out, src, slot in ((sends, peer, me), (recvs, me, peer)):
                out.append(pltpu.make_async_remote_copy(
                    src_ref=x_ref.at[src], dst_ref=land_ref.at[slot], send_sem=send_sems.at[sem],
                    recv_sem=recv_sems.at[sem], device_id=(px, py, pc), device_id_type=MESH))
    own = [pltpu.make_async_copy(x_ref.at[me], land_ref.at[me], local_sems.at[t])
           for t, (x_ref, land_ref) in enumerate(zip(x_refs, land_refs))]
    return sends, recvs, own


SEM =pl.BlockSpec(memory_space=pltpu.SEMAPHORE)
ANY = pl.BlockSpec(memory_space=pl.ANY)
SIDE_EFFECT = pltpu.SideEffectType.DATAFLOW_SIDE_EFFECTING


def _exchange_start(arrs, after, *, name):
    n = len(arrs)
    lands = [lax.empty(a.shape, a.dtype) for a in arrs]
    extra = [] if after is None else [after]

    def body(*refs):
        sems = refs[2 * n + len(extra):2 * n + len(extra) + 3]
        sends, _, own = _direct_copies(refs[:n], refs[n:2 * n], *sems)
        for cp in own + sends:
            cp.start()
        refs[-1][...] = jnp.zeros_like(refs[-1])

    ops = [pltpu.with_memory_space_constraint(a, pltpu.HBM) for a in list(arrs) + lands]
    out = pl.pallas_call(
        body, name=name,
        out_shape=(pltpu.SemaphoreType.DMA((7 * n,)), pltpu.SemaphoreType.DMA((7 * n,)), pltpu.SemaphoreType.DMA((n,)),
                   *[pltpu.HBM(a.shape, a.dtype) for a in ops], jax.ShapeDtypeStruct((8, LANES), F32)),
        in_specs=[HBM] * (2 * n) + [ANY] * len(extra),
        out_specs=(SEM, SEM, SEM, *[HBM] * (2 * n), pl.BlockSpec(memory_space=pltpu.VMEM)),
        input_output_aliases={t: 3 + t for t in range(2 * n)},
        compiler_params=pltpu.CompilerParams(has_side_effects=SIDE_EFFECT),
    )(*ops, *extra)
    return (out[:3], out[3:3 + n], out[3 + n:3 + 2 * n]), out[-1]


def _exchange_wait(state, after, *, name):
    sems, arrs, lands = state
    n = len(arrs)

    def body(*refs):
        sends, recvs, own = _direct_copies(refs[:n], refs[n:2 * n], *refs[2 * n:2 * n + 3])
        for cp in own:
            cp.wait()
        for cp in sends:
            cp.wait_send()
        for cp in recvs:
            cp.wait_recv()

    out = pl.pallas_call(
        body, name=name, out_shape=tuple(pltpu.HBM(a.shape, a.dtype) for a in list(arrs) + list(lands)),
        in_specs=[HBM] * (2 * n) + [SEM, SEM, SEM, ANY], out_specs=tuple([HBM] * (2 * n)),
        input_output_aliases={t: t for t in range(2 * n)},
        compiler_params=pltpu.CompilerParams(has_side_effects=SIDE_EFFECT),
    )(*arrs, *lands, *sems, after)
    return out[n:]


def _relay_copies(x_refs, land_refs, sems_a, sems_b):
    x, y, c = lax.axis_index("x"), lax.axis_index("y"), lax.axis_index("c")
    me = 4 * x + 2 * y + c
    sibling = (x, y, 1 - c)
    chips = [(1 - x, y), (x, 1 - y), (1 - x, 1 - y)]

    def slot(px, py, pc):
        return 4 * px + 2 * py + pc

    def copy(src, land_ref, dst_slot, send_sems, recv_sems, k, to):
        return pltpu.make_async_remote_copy(src_ref=src, dst_ref=land_ref.at[dst_slot], send_sem=send_sems.at[k],
                                            recv_sem=recv_sems.at[k], device_id=to, device_id_type=MESH)

    a_send, a_recv, a_own, b_send, b_recv = [], [], [], [], []
    for t, (x_ref, land_ref) in enumerate(zip(x_refs, land_refs)):
        peers = [sibling] + [(*chip, c) for chip in chips]
        if sems_a is not None:
            for k, peer in enumerate(peers):
                a_send.append(copy(x_ref, land_ref, me, sems_a[0], sems_a[1], 4 * t + k, peer))
                a_recv.append(copy(x_ref, land_ref, slot(*peer), sems_a[0], sems_a[1], 4 * t + k, peer))
            a_own.append(pltpu.make_async_copy(x_ref, land_ref.at[me], sems_a[2].at[t]))
        if sems_b is not None:
            for j, chip in enumerate(chips):
                b_send.append(copy(land_ref.at[slot(*chip, c)], land_ref, slot(*chip, c), sems_b[0], sems_b[1], 3 * t + j, sibling))
                b_recv.append(copy(land_ref.at[slot(*chip, c)], land_ref, slot(*chip, 1 - c), sems_b[0], sems_b[1], 3 * t + j,
                                   sibling))
    return (a_send, a_recv, a_own), (b_send, b_recv)


def _relay_start(arrs, after, name):
    n = len(arrs)
    lands = [lax.empty((N_DEV,) + a.shape, a.dtype) for a in arrs]

    def body(*refs):
        (sends, _, own), _ = _relay_copies(refs[:n], refs[n:2 * n], refs[2 * n + 1:2 * n + 4], None)
        for cp in own + sends:
            cp.start()
        refs[-1][...] = jnp.zeros_like(refs[-1])

    ops = [pltpu.with_memory_space_constraint(a, pltpu.HBM) for a in list(arrs) + lands]
    out = pl.pallas_call(
        body, name=name,
        out_shape=(pltpu.SemaphoreType.DMA((4 * n,)), pltpu.SemaphoreType.DMA((4 * n,)), pltpu.SemaphoreType.DMA((n,)),
                   *[pltpu.HBM(a.shape, a.dtype) for a in ops], jax.ShapeDtypeStruct((8, LANES), F32)),
        in_specs=[HBM] * (2 * n) + [ANY],
        out_specs=(SEM, SEM, SEM, *[HBM] * (2 * n), pl.BlockSpec(memory_space=pltpu.VMEM)),
        input_output_aliases={t: 3 + t for t in range(2 * n)},
        compiler_params=pltpu.CompilerParams(has_side_effects=SIDE_EFFECT),
    )(*ops, after)
    return (out[:3], out[3:3 + n], out[3 + n:3 + 2 * n]), out[-1]


def _relay_pass(state, after, name):
    sems_a, arrs, lands = state
    n = len(arrs)

    def body(*refs):
        sems_b = refs[2 * n + 4:2 * n + 6]
        (a_send, a_recv, a_own), (b_send, _) = _relay_copies(refs[:n], refs[n:2 * n], refs[2 * n:2 * n + 3], sems_b)
        for cp in a_own:
            cp.wait()
        for cp in a_send:
            cp.wait_send()
        for cp in a_recv:
            cp.wait_recv()
        for cp in b_send:
            cp.start()
        refs[-1][...] = jnp.zeros_like(refs[-1])

    out = pl.pallas_call(
        body, name=name,
        out_shape=(pltpu.SemaphoreType.DMA((3 * n,)), pltpu.SemaphoreType.DMA((3 * n,)),
                   *[pltpu.HBM(a.shape, a.dtype) for a in list(arrs) + list(lands)], jax.ShapeDtypeStruct((8, LANES), F32)),
        in_specs=[HBM] * (2 * n) + [SEM, SEM, SEM, ANY],
        out_specs=(SEM, SEM, *[HBM] * (2 * n), pl.BlockSpec(memory_space=pltpu.VMEM)),
        input_output_aliases={t: 2 + t for t in range(2 * n)},
        compiler_params=pltpu.CompilerParams(has_side_effects=SIDE_EFFECT),
    )(*arrs, *lands, *sems_a, after)
    return (out[:2], out[2:2 + n], out[2 + n:2 + 2 * n]), out[-1]


def _relay_wait(state, after, name):
    sems_b, arrs, lands = state
    n = len(arrs)

    def body(*refs):
        _, (b_send, b_recv) = _relay_copies(refs[:n], refs[n:2 * n], None, refs[2 * n:2 * n + 2])
        for cp in b_send:
            cp.wait_send()
        for cp in b_recv:
            cp.wait_recv()

    out = pl.pallas_call(
        body, name=name, out_shape=tuple(pltpu.HBM(a.shape, a.dtype) for a in list(arrs) + list(lands)),
        in_specs=[HBM] * (2 * n) + [SEM, SEM, ANY], out_specs=tuple([HBM] * (2 * n)),
        input_output_aliases={t: t for t in range(2 * n)},
        compiler_params=pltpu.CompilerParams(has_side_effects=SIDE_EFFECT),
    )(*arrs, *lands, *sems_b, after)
    return out[n:]


def _sum_parts(parts, name, after=None):
    n, r, cdim = parts.shape
    tr = _row_tile(r, max(16, (1 << 21) // (n * cdim * parts.dtype.itemsize)))

    def body(p_ref, *rest):
        acc = p_ref[0].astype(F32)
        for k in range(1, n):
            acc = acc + p_ref[k].astype(F32)
        rest[-1][...] = acc

    ins = [parts] + ([] if after is None else [after])
    return pl.pallas_call(
        body, name=name, out_shape=jax.ShapeDtypeStruct((r, cdim), F32), grid=(r // tr,),
        in_specs=[pl.BlockSpec((n, tr, cdim), lambda i: (0, i, 0))] + [ANY] * (len(ins) - 1),
        out_specs=pl.BlockSpec((tr, cdim), lambda i: (i, 0)), compiler_params=_params(("parallel",)),
    )(*ins)


def _mm_tn(a, b, *, name, after=None, tm=512, tn=1024, out_rows=None, row0=0, prev=None):
    k, m = a.shape
    n = b.shape[1]
    tm, tn = _tile(m, tm), _tile(n, tn)
    out_rows = m if out_rows is None else out_rows

    def body(a_ref, b_ref, *rest):
        o_ref, at_ref = rest[-2], rest[-1]

        @pl.when(pl.program_id(1) == 0)
        def _():
            at_ref[...] = a_ref[...].astype(BF16).T

        o_ref[...] = _dot(at_ref[...], b_ref[...].astype(BF16), 1, 0).astype(BF16)

    ins = [a, b] + [t for t in (after, prev) if t is not None]
    return pl.pallas_call(
        body, name=name, out_shape=jax.ShapeDtypeStruct((out_rows, n), BF16), grid=(m // tm, n // tn),
        in_specs=[pl.BlockSpec((k, tm), lambda i, j: (0, i)), pl.BlockSpec((k, tn), lambda i, j: (0, j))] + [ANY] * (len(ins) - 2),
        out_specs=pl.BlockSpec((tm, tn), lambda i, j: (row0 // tm + i, j)),
        input_output_aliases={} if prev is None else {len(ins) - 1: 0},
        scratch_shapes=[pltpu.VMEM((tm, k), BF16)], compiler_params=_params(("parallel", "arbitrary")),
    )(*ins)


def _mm(a, b, *, name, ta=False, tb=False, res=None, colscale=None, emit_acc=False,
        out_dtype=F32, tm=512, tn=512, b_rows=None):
    m, k = (a.shape[1], a.shape[0]) if ta else a.shape
    n = b.shape[0] if tb else b.shape[1]
    b_start = 0
    if b_rows is not None:
        b_start, n = b_rows
    tm, tn = _tile(m, tm), _tile(n, tn)
    ca, cb = (0 if ta else 1), (1 if tb else 0)
    a_spec = pl.BlockSpec((k, tm), lambda i, j: (0, i)) if ta else pl.BlockSpec((tm, k), lambda i, j: (i, 0))
    b_spec = (pl.BlockSpec((tn, k), lambda i, j: (b_start // tn + j, 0)) if tb
              else pl.BlockSpec((k, tn), lambda i, j: (0, j)))
    tile = pl.BlockSpec((tm, tn), lambda i, j: (i, j))
    ins, in_specs = [a, b], [a_spec, b_spec]
    if res is not None:
        ins.append(res)
        in_specs.append(tile)
    if colscale is not None:
        ins.append(colscale)
        in_specs.append(pl.BlockSpec((1, tn), lambda i, j: (0, j)))
    n_in = len(ins)

    def body(*refs):
        outs = refs[n_in:]
        acc = _dot(refs[0][...].astype(BF16), refs[1][...].astype(BF16), ca, cb)
        val, p = acc, 2
        if res is not None:
            r_val, p = refs[p][...], p + 1
        if colscale is not None:
            val = val * refs[p][...]
        if res is not None:
            val = r_val + val
        if emit_acc:
            outs[0][...] = acc
        outs[-1][...] = val.astype(out_dtype)

    out_shape = [jax.ShapeDtypeStruct((m, n), out_dtype)]
    out_specs = [tile]
    if emit_acc:
        out_shape.insert(0, jax.ShapeDtypeStruct((m, n), F32))
        out_specs.insert(0, tile)
    out = pl.pallas_call(
        body, name=name, out_shape=out_shape, grid=(m // tm, n // tn), in_specs=in_specs, out_specs=out_specs,
        compiler_params=_params(("parallel", "parallel")),
    )(*ins)
    return out if emit_acc else out[0]


def _norm_fwd(x, g, scale, shift, name, after=None):
    s, d = x.shape
    tr = 256

    def body(x_ref, g_ref, sc_ref, sh_ref, *rest):
        xv = x_ref[...]
        rstd = lax.rsqrt(jnp.mean(xv * xv, axis=-1, keepdims=True) + RMS_EPS)
        rest[-1][...] = (xv * rstd * g_ref[...] * (1.0 + sc_ref[...]) + sh_ref[...]).astype(BF16)

    rowspec = pl.BlockSpec((1, d), lambda i: (0, 0))
    ins = [x, g, scale, shift] + ([] if after is None else [after])
    return pl.pallas_call(
        body, name=name, out_shape=jax.ShapeDtypeStruct((s, d), BF16), grid=(s // tr,),
        in_specs=[pl.BlockSpec((tr, d), lambda i: (i, 0)), rowspec, rowspec, rowspec] + [ANY] * (len(ins) - 4),
        out_specs=pl.BlockSpec((tr, d), lambda i: (i, 0)),
        compiler_params=_params(("parallel",)),
    )(*ins)


def _dh_norm_bwd(a1, b1, a2, b2, x, dres, g, scale, nxt, *, name, after=None, b_rows=None):
    s, d = x.shape
    tm = 256
    n_fixed = 8

    def body(a1_ref, b1_ref, a2_ref, b2_ref, x_ref, dr_ref, g_ref, sc_ref, *rest):
        rest = rest[(1 if after is not None else 0):]
        if nxt is not None:
            f_ref, cs_ref, dx_ref, sa_ref, sb_ref, df_ref, dg_ref = rest
        else:
            dx_ref, sa_ref, sb_ref = rest

        @pl.when(pl.program_id(0) == 0)
        def _():
            sa_ref[...] = jnp.zeros_like(sa_ref)
            sb_ref[...] = jnp.zeros_like(sb_ref)
            if nxt is not None:
                dg_ref[...] = jnp.zeros_like(dg_ref)

        dhv = (_dot(a1_ref[...].astype(BF16), b1_ref[...], 1, 0) + _dot(a2_ref[...].astype(BF16), b2_ref[...], 1, 0))
        xv = x_ref[...]
        rstd = lax.rsqrt(jnp.mean(xv * xv, axis=-1, keepdims=True) + RMS_EPS)
        xhat = xv * rstd
        dxhat = dhv * (g_ref[...] * (1.0 + sc_ref[...]))
        mean_term = jnp.mean(dxhat * xhat, axis=-1, keepdims=True)
        dxv = dr_ref[...] + rstd * (dxhat - xhat * mean_term)
        dx_ref[...] = dxv
        sa_ref[...] += jnp.sum(dhv, axis=0, keepdims=True)
        sb_ref[...] += jnp.sum(dhv * xhat, axis=0, keepdims=True)
        if nxt is not None:
            df_ref[...] = (dxv * cs_ref[...]).astype(BF16)
            dg_ref[...] += nxt[2] * jnp.sum(dxv * f_ref[...], axis=0, keepdims=True)

    def a_spec(t):
        return pl.BlockSpec((tm, t.shape[1]), lambda i: (i, 0))

    def b_spec(t, a, which):
        if b_rows is None:
            return pl.BlockSpec((t.shape[0], d), lambda i: (0, 0))
        start = b_rows[which]
        return pl.BlockSpec((pl.Element(a.shape[1]), pl.Element(d)), lambda i: (start, 0))

    rowspec = pl.BlockSpec((1, d), lambda i: (0, 0))
    tile = pl.BlockSpec((tm, d), lambda i: (i, 0))
    ins = [a1, b1, a2, b2, x, dres, g, scale] + ([] if after is None else [after])
    in_specs = [a_spec(a1), b_spec(b1, a1, 0), a_spec(a2), b_spec(b2, a2, 1), tile, tile, rowspec, rowspec]
    in_specs += [ANY] * (len(ins) - n_fixed)
    out_shape = [jax.ShapeDtypeStruct((s, d), F32), jax.ShapeDtypeStruct((1, d), F32), jax.ShapeDtypeStruct((1, d), F32)]
    out_specs = [tile, rowspec, rowspec]
    if nxt is not None:
        ins += [nxt[0], nxt[1]]
        in_specs += [tile, rowspec]
        out_shape += [jax.ShapeDtypeStruct((s, d), BF16), jax.ShapeDtypeStruct((1, d), F32)]
        out_specs += [tile, rowspec]
    out = pl.pallas_call(
        body, name=name, out_shape=out_shape, grid=(s // tm,), in_specs=in_specs, out_specs=out_specs,
        compiler_params=_params(("arbitrary",)),
    )(*ins)
    return out[0], out[1], out[2], (None if nxt is None else (out[3], out[4]))


def _gate_bwd(dxn, f, colscale, coef, name):
    s, d = dxn.shape
    tr = 256

    def body(dx_ref, f_ref, cs_ref, df_ref, dg_ref):
        @pl.when(pl.program_id(0) == 0)
        def _():
            dg_ref[...] = jnp.zeros_like(dg_ref)

        dxv = dx_ref[...]
        df_ref[...] = (dxv * cs_ref[...]).astype(BF16)
        dg_ref[...] += coef * jnp.sum(dxv * f_ref[...], axis=0, keepdims=True)

    rowspec = pl.BlockSpec((1, d), lambda i: (0, 0))
    tile = pl.BlockSpec((tr, d), lambda i: (i, 0))
    return pl.pallas_call(
        body, name=name, out_shape=[jax.ShapeDtypeStruct((s, d), BF16), jax.ShapeDtypeStruct((1, d), F32)],
        grid=(s // tr,), in_specs=[tile, tile, rowspec], out_specs=[tile, rowspec],
        compiler_params=_params(("arbitrary",)),
    )(dxn, f, colscale)


def _ffn_up(h, wg, wu, name, tm=SEQ, tn=256):
    s, d = h.shape
    f = wg.shape[0]

    def body(h_ref, wg_ref, wu_ref, a_ref, u_ref, s_ref):
        hv = h_ref[...]
        a = _dot(hv, wg_ref[...], 1, 1)
        u = _dot(hv, wu_ref[...], 1, 1)
        a_ref[...] = a.astype(BF16)
        u_ref[...] = u.astype(BF16)
        s_ref[...] = (a * _sigmoid(a) * u).astype(BF16)

    tile = pl.BlockSpec((tm, tn), lambda i, j: (i, j))
    wspec = pl.BlockSpec((tn, d), lambda i, j: (j, 0))
    return pl.pallas_call(
        body, name=name,
        out_shape=[jax.ShapeDtypeStruct((s, f), BF16), jax.ShapeDtypeStruct((s, f), BF16), jax.ShapeDtypeStruct((s, f), BF16)],
        grid=(s // tm, f // tn), in_specs=[pl.BlockSpec((tm, d), lambda i, j: (i, 0)), wspec, wspec],
        out_specs=[tile, tile, tile], compiler_params=_params(("parallel", "parallel")),
    )(h, wg, wu)


def _ffn_bwd_ds(df, wd, a, u, name, tm=SEQ, tn=256):
    s, d = df.shape
    f = wd.shape[0]

    def body(df_ref, wd_ref, a_ref, u_ref, da_ref, du_ref):
        ds = _dot(df_ref[...], wd_ref[...], 1, 1)
        av = a_ref[...].astype(F32)
        sg = _sigmoid(av)
        da_ref[...] = (ds * u_ref[...].astype(F32) * (sg * (1.0 + av * (1.0 - sg)))).astype(BF16)
        du_ref[...] = (ds * (av * sg)).astype(BF16)

    tile = pl.BlockSpec((tm, tn), lambda i, j: (i, j))
    return pl.pallas_call(
        body, name=name, out_shape=[jax.ShapeDtypeStruct((s, f), BF16), jax.ShapeDtypeStruct((s, f), BF16)],
        grid=(s // tm, f // tn),
        in_specs=[pl.BlockSpec((tm, d), lambda i, j: (i, 0)), pl.BlockSpec((tn, d), lambda i, j: (j, 0)), tile, tile],
        out_specs=[tile, tile], compiler_params=_params(("parallel", "parallel")),
    )(df, wd, a, u)


def _merge_fwd(o_sb, o_dil, o_swa, gates, wb_sb, wb_dil, wb_swa, name):
    s, d = SEQ, D_MODEL
    tm = 256

    def body(osb_ref, odl_ref, osw_ref, g_ref, wsb_ref, wdl_ref, wsw_ref, m_ref, tsb_ref, tdl_ref, tsw_ref):
        for h in range(osb_ref.shape[0]):
            tsb_ref[:, h * HEAD_DIM:(h + 1) * HEAD_DIM] = osb_ref[h].astype(BF16)
        for h in range(osw_ref.shape[0]):
            tsw_ref[:, h * HEAD_DIM:(h + 1) * HEAD_DIM] = osw_ref[h].astype(BF16)
        tdl_ref[...] = odl_ref[...].astype(BF16)
        acc = _sigmoid(g_ref[:, 0:d]) * _dot(tsb_ref[...], wsb_ref[...], 1, 0)
        acc += _sigmoid(g_ref[:, d:2 * d]) * _dot(tdl_ref[...], wdl_ref[...], 1, 0)
        acc += _sigmoid(g_ref[:, 2 * d:3 * d]) * _dot(tsw_ref[...], wsw_ref[...], 1, 0)
        m_ref[...] = acc.astype(BF16)

    def rows(w):
        return pl.BlockSpec((tm, w), lambda i: (i, 0))

    def heads(n):
        return pl.BlockSpec((n, tm, HEAD_DIM), lambda i: (0, i, 0))

    def whole(w):
        return pl.BlockSpec((w, d), lambda i: (0, 0))

    return pl.pallas_call(
        body, name=name, out_shape=[jax.ShapeDtypeStruct((s, w), BF16) for w in (d, 256, 128, 384)], grid=(s // tm,),
        in_specs=[heads(H_SB), rows(128), heads(H_SWA_Q), rows(3 * d), whole(256), whole(128), whole(384)],
        out_specs=[rows(d), rows(256), rows(128), rows(384)], compiler_params=_params(("parallel",)),
    )(o_sb, o_dil, o_swa, gates, wb_sb, wb_dil, wb_swa)


def _merge_bwd(dmerged, t_sb, t_dil, t_swa, gates, wb_sb, wb_dil, wb_swa, name):
    s, d = SEQ, D_MODEL
    tm = 256

    def body(dm_ref, tsb_ref, tdl_ref, tsw_ref, g_ref, wsb_ref, wdl_ref, wsw_ref,
             dg_ref, dosb_ref, dodl_ref, dosw_ref, dbsb_ref, dbdl_ref, dbsw_ref):
        dm = dm_ref[...]
        for idx, (t_ref, w_ref, do_ref, db_ref) in enumerate((
                (tsb_ref, wsb_ref, dosb_ref, dbsb_ref), (tdl_ref, wdl_ref, dodl_ref, dbdl_ref),
                (tsw_ref, wsw_ref, dosw_ref, dbsw_ref))):
            w = w_ref[...]
            br = _dot(t_ref[...], w, 1, 0)
            sg = _sigmoid(g_ref[:, idx * d:(idx + 1) * d])
            dbr = (dm * sg).astype(BF16)
            dg_ref[:, idx * d:(idx + 1) * d] = (dm * br * (sg * (1.0 - sg))).astype(BF16)
            db_ref[...] = dbr
            do = _dot(dbr, w, 1, 1)
            if len(do_ref.shape) == 2:
                do_ref[...] = do
            else:
                for h in range(do_ref.shape[0]):
                    do_ref[h] = do[:, h * HEAD_DIM:(h + 1) * HEAD_DIM]

    def rows(w):
        return pl.BlockSpec((tm, w), lambda i: (i, 0))

    def heads(n):
        return pl.BlockSpec((n, tm, HEAD_DIM), lambda i: (0, i, 0))

    def whole(w):
        return pl.BlockSpec((w, d), lambda i: (0, 0))

    def shp(w, dt):
        return jax.ShapeDtypeStruct((s, w), dt)

    def hshp(n):
        return jax.ShapeDtypeStruct((n, s, HEAD_DIM), F32)

    return pl.pallas_call(
        body, name=name,
        out_shape=[shp(3 * d, BF16), hshp(H_SB), shp(128, F32), hshp(H_SWA_Q), shp(d, BF16), shp(d, BF16), shp(d, BF16)],
        grid=(s // tm,),
        in_specs=[rows(d), rows(256), rows(128), rows(384), rows(3 * d), whole(256), whole(128), whole(384)],
        out_specs=[rows(3 * d), heads(H_SB), rows(128), heads(H_SWA_Q), rows(d), rows(d), rows(d)],
        compiler_params=_params(("parallel",)),
    )(dmerged, t_sb, t_dil, t_swa, gates, wb_sb, wb_dil, wb_swa)


def _final_loss(x, target, g, name):
    s, d = x.shape
    tr = 256

    def body(x_ref, t_ref, g_ref, loss_ref, dx_ref, dg_ref):
        @pl.when(pl.program_id(0) == 0)
        def _():
            loss_ref[...] = jnp.zeros_like(loss_ref)
            dg_ref[...] = jnp.zeros_like(dg_ref)

        xv = x_ref[...]
        gv = g_ref[...]
        rstd = lax.rsqrt(jnp.mean(xv * xv, axis=-1, keepdims=True) + RMS_EPS)
        xhat = xv * rstd
        err = xhat * gv - t_ref[...]
        loss_ref[...] += 0.5 * jnp.sum(jnp.mean(err * err, axis=-1, keepdims=True))
        dy = err * (1.0 / d)
        dxhat = dy * gv
        mean_term = jnp.mean(dxhat * xhat, axis=-1, keepdims=True)
        dx_ref[...] = rstd * (dxhat - xhat * mean_term)
        dg_ref[...] += jnp.sum(dy * xhat, axis=0, keepdims=True)

    rowspec = pl.BlockSpec((1, d), lambda i: (0, 0))
    tile = pl.BlockSpec((tr, d), lambda i: (i, 0))
    return pl.pallas_call(
        body, name=name,
        out_shape=[jax.ShapeDtypeStruct((1, LANES), F32), jax.ShapeDtypeStruct((s, d), F32), jax.ShapeDtypeStruct((1, d), F32)],
        grid=(s // tr,), in_specs=[tile, tile, rowspec],
        out_specs=[pl.BlockSpec((1, LANES), lambda i: (0, 0)), tile, rowspec],
        compiler_params=_params(("arbitrary",)),
    )(x, target, g)


def _adamw(w, g, m, v, name, after=None):
    shape = w.shape
    cols = shape[-1]
    rows = int(np.prod(shape[:-1])) if len(shape) > 1 else 1
    tr = rows
    for cand in (1024, 512, 256, 128, 64, 32, 16, 8):
        if rows % cand == 0 and rows > cand and cand * cols * 4 <= (1 << 21):
            tr = cand
            break

    def body(w_ref, g_ref, m_ref, v_ref, *rest):
        d_ref, nm_ref, nv_ref = rest[-3:]
        d_ref[...], nm_ref[...], nv_ref[...] = _adam_update(w_ref[...], g_ref[...], m_ref[...], v_ref[...])

    tile = pl.BlockSpec((tr, cols), lambda i: (i, 0))
    flat = [t.reshape(rows, cols) for t in (w, g, m, v)] + ([] if after is None else [after])
    out = pl.pallas_call(
        body, name=name, out_shape=[jax.ShapeDtypeStruct((rows, cols), F32)] * 3, grid=(rows // tr,),
        in_specs=[tile] * 4 + [ANY] * (len(flat) - 4), out_specs=[tile] * 3, compiler_params=_params(("parallel",)),
    )(*flat)
    return tuple(t.reshape(shape) for t in out)


def _adam_update(w, gv, m, v):
    nm = ADAM_B1 * m + (1.0 - ADAM_B1) * gv
    nv = ADAM_B2 * v + (1.0 - ADAM_B2) * (gv * gv)
    m_hat = nm / (1.0 - ADAM_B1 ** ADAM_STEP)
    v_hat = nv / (1.0 - ADAM_B2 ** ADAM_STEP)
    return -ADAM_LR * (m_hat / (jnp.sqrt(v_hat) + ADAM_EPS) + ADAM_WD * w), nm, nv


def _reduce_adamw(groups, w, m, v, row0, prev, name, after=None):
    n, r, cdim = groups[0].shape
    rows = w.shape[0]
    tr = _row_tile(r, max(16, (1 << 22) // (n * cdim * groups[0].dtype.itemsize)))
    steps = r // tr
    ng = len(groups)

    def body(*refs):
        w_ref, m_ref, v_ref = refs[ng:ng + 3]
        g_out, d_out, m_out, v_out = refs[-4:]
        gg = pl.program_id(0)
        for gi in range(ng):
            @pl.when(gg == gi)
            def _(gi=gi):
                acc = refs[gi][0].astype(F32)
                for k in range(1, n):
                    acc = acc + refs[gi][k].astype(F32)
                g_out[...] = acc
                d_out[...], m_out[...], v_out[...] = _adam_update(w_ref[...], acc, m_ref[...], v_ref[...])

    def part_spec(gi):
        return pl.BlockSpec((n, tr, cdim), lambda gg, i: (0, jnp.where(gg == gi, i, 0), 0))

    tile = pl.BlockSpec((tr, cdim), lambda gg, i: (row0 // tr + gg * steps + i, 0))
    extra = ([] if prev is None else list(prev)) + ([] if after is None else [after])
    return pl.pallas_call(
        body, name=name, out_shape=[jax.ShapeDtypeStruct((rows, cdim), F32)] * 4, grid=(ng, steps),
        in_specs=[part_spec(gi) for gi in range(ng)] + [tile] * 3 + [ANY] * len(extra), out_specs=[tile] * 4,
        input_output_aliases={} if prev is None else {ng + 3 + k: k for k in range(4)},
        compiler_params=_params(("parallel", "parallel")),
    )(*groups, w, m, v, *extra)


def _ada_fwd(c_all, w, name):
    n = w.shape[1]

    def body(c_ref, w_ref, o_ref):
        cv = c_ref[...]
        o_ref[...] = jnp.dot(cv * _sigmoid(cv), w_ref[...], preferred_element_type=F32, precision=lax.Precision.HIGHEST)

    return pl.pallas_call(body, name=name, out_shape=jax.ShapeDtypeStruct((N_DEV, n), F32), compiler_params=_params())(c_all, w)


def _ada_bwd(c_all_t, dmod, name):
    n = dmod.shape[1]

    def body(c_ref, d_ref, o_ref):
        cv = c_ref[...]
        o_ref[...] = jnp.dot(cv * _sigmoid(cv), d_ref[...], preferred_element_type=F32, precision=lax.Precision.HIGHEST)

    return pl.pallas_call(body, name=name, out_shape=jax.ShapeDtypeStruct((D_MODEL, n), F32), compiler_params=_params())(c_all_t, dmod)


def _bucket_tables():
    rel = np.arange(BLK)[:, None] + BLK - np.arange(2 * BLK)[None, :]
    max_exact = N_BUCKETS // 2

    def bucket(n):
        nf = np.maximum(n, 1).astype(np.float32)
        large = max_exact + (np.log(nf / np.float32(max_exact)) / np.float32(math.log(MAX_REL_DIST / max_exact))
                             * np.float32(N_BUCKETS - max_exact)).astype(np.int32)
        return np.where(n < max_exact, n, np.minimum(large, N_BUCKETS - 1))

    tabs = []
    for dil, max_dist in ((1, 128), (4, 128), (16, 128), (1, SWA_WINDOW - 1)):
        in_band = (rel >= 0) & (rel <= max_dist)
        tabs.append(np.where(in_band, bucket(np.maximum(rel, 0) * dil), -1))
    return np.stack(tabs).astype(np.int32)


N_SOFT = H_DIL + H_SWA_Q


def _table_of_head(h):
    return jnp.minimum(h // 2, 3)


def _bias_build(rel_bias, tables, name):
    def body(rel_ref, t_ref, o_ref):
        h = pl.program_id(0)
        tb = t_ref[0]
        out = jnp.full((BLK, 2 * BLK), NEG, F32)
        for b in range(N_BUCKETS):
            out = jnp.where(tb == b, rel_ref[b, h], out)
        o_ref[0] = out

    return pl.pallas_call(
        body, name=name, out_shape=jax.ShapeDtypeStruct((N_SOFT, BLK, 2 * BLK), F32), grid=(N_SOFT,),
        in_specs=[pl.BlockSpec(memory_space=pltpu.SMEM),
                  pl.BlockSpec((1, BLK, 2 * BLK), lambda h: (_table_of_head(h), 0, 0))],
        out_specs=pl.BlockSpec((1, BLK, 2 * BLK), lambda h: (h, 0, 0)),
        compiler_params=_params(("parallel",)),
    )(rel_bias, tables)


def _bias_grad(dbias, tables, name):
    def body(d_ref, t_ref, o_ref):
        tb = t_ref[0]
        dv = d_ref[0]
        lane = lax.broadcasted_iota(jnp.int32, (1, LANES), 1)
        out = jnp.zeros((1, LANES), F32)
        for b in range(N_BUCKETS):
            out = jnp.where(lane == b, jnp.sum(jnp.where(tb == b, dv, 0.0)), out)
        o_ref[0] = out

    return pl.pallas_call(
        body, name=name, out_shape=jax.ShapeDtypeStruct((N_SOFT, 1, LANES), F32), grid=(N_SOFT,),
        in_specs=[pl.BlockSpec((1, BLK, 2 * BLK), lambda h: (h, 0, 0)),
                  pl.BlockSpec((1, BLK, 2 * BLK), lambda h: (_table_of_head(h), 0, 0))],
        out_specs=pl.BlockSpec((1, 1, LANES), lambda h: (h, 0, 0)),
        compiler_params=_params(("parallel",)),
    )(dbias, tables)


def _band_layout(g, bias_div):
    assert g == 1 or bias_div == 1
    return bias_div if g == 1 else 1


def _band_specs(length, g, bias_div, offs):
    ns = _band_layout(g, bias_div)

    def seqs(off, div=1):
        return pl.BlockSpec((ns, length, HEAD_DIM), lambda s: (off // ns + s // div, 0, 0))

    xspecs = [seqs(offs[0]), seqs(offs[1], g), seqs(offs[2], g)]
    bspec = pl.BlockSpec((1, BLK, 2 * BLK), lambda s: (s, 0, 0))
    sspec = pl.BlockSpec((ns, 1, LANES), lambda s: (s, 0, 0))
    colspec = pl.BlockSpec((ns, length, 1), lambda s: (s, 0, 0))
    return xspecs, seqs(0), seqs(0, g), bspec, sspec, colspec


def _band_sweep(length, ns, one):
    nblk = length // BLK
    for qq in range(ns):
        if ns * nblk <= 16:
            for i in range(nblk):
                one(qq, i * BLK, max(i - 1, 0) * BLK, i == 0)
        else:
            def step(i, carry, qq=qq):
                one(qq, pl.multiple_of(i * BLK, BLK), pl.multiple_of(jnp.maximum(i - 1, 0) * BLK, BLK), i == 0)
                return carry

            lax.fori_loop(0, nblk, step, 0, unroll=2)


def _band_scores(q_ref, k_ref, b_ref, qq, kq, bq, cur, prv, first):
    qv = q_ref[qq, pl.ds(cur, BLK), :]
    bv = b_ref[bq]
    if first is True:
        sp = jnp.full((BLK, BLK), NEG, F32)
    else:
        sp = _dot(qv, k_ref[kq, pl.ds(prv, BLK), :], 1, 1) + bv[:, :BLK]
        sp = sp if first is False else jnp.where(first, NEG, sp)
    sc = _dot(qv, k_ref[kq, pl.ds(cur, BLK), :], 1, 1) + bv[:, BLK:]
    return qv, sp, sc


def _band_fwd(x, bias, sink, *, nq, offs, g, bias_div, has_sink, name):
    length = x.shape[1]
    ns = _band_layout(g, bias_div)

    def body(q_ref, k_ref, v_ref, b_ref, s_ref, o_ref, lse_ref):
        def one(qq, cur, prv, first):
            kq, bq = qq, 0
            _, sp, sc = _band_scores(q_ref, k_ref, b_ref, qq, kq, bq, cur, prv, first)
            m = jnp.maximum(jnp.max(sp, axis=1, keepdims=True), jnp.max(sc, axis=1, keepdims=True))
            if has_sink:
                sk = s_ref[qq][:, :1]
                m = jnp.maximum(m, sk)
            pp, pc = jnp.exp(sp - m), jnp.exp(sc - m)
            den = jnp.sum(pp, axis=1, keepdims=True) + jnp.sum(pc, axis=1, keepdims=True)
            if has_sink:
                den = den + jnp.exp(sk - m)
            acc = (_dot(pp.astype(BF16), v_ref[kq, pl.ds(prv, BLK), :], 1, 0)
                   + _dot(pc.astype(BF16), v_ref[kq, pl.ds(cur, BLK), :], 1, 0))
            o_ref[qq, pl.ds(cur, BLK), :] = acc / den
            lse_ref[qq, pl.ds(cur, BLK), :] = m + jnp.log(den)

        _band_sweep(length, ns, one)

    xspecs, qspec, _, bspec, sspec, colspec = _band_specs(length, g, bias_div, offs)
    return pl.pallas_call(
        body, name=name,
        out_shape=[jax.ShapeDtypeStruct((nq, length, HEAD_DIM), F32), jax.ShapeDtypeStruct((nq, length, 1), F32)],
        grid=(nq // ns,), in_specs=xspecs + [bspec, sspec],
        out_specs=[qspec, colspec], compiler_params=_params(("parallel",)),
    )(x, x, x, bias, sink)


def _band_bwd(x, bias, sink, o, lse, do, dlse, *, nq, offs, g, bias_div, has_sink, name):
    length = x.shape[1]
    ns = _band_layout(g, bias_div)
    nk, nbias = nq // g, nq // bias_div

    def body(q_ref, k_ref, v_ref, b_ref, s_ref, o_ref, lse_ref, do_ref, dlse_ref,
             dq_ref, dk_ref, dv_ref, db_ref, dsk_ref, dkp_ref, dvp_ref):
        for ref in (db_ref, dsk_ref, dkp_ref, dvp_ref):
            ref[...] = jnp.zeros_like(ref)

        @pl.when(pl.program_id(0) % g == 0)
        def _():
            dk_ref[...] = jnp.zeros_like(dk_ref)
            dv_ref[...] = jnp.zeros_like(dv_ref)

        def one(qq, cur, prv, first):
            kq, bq = qq, 0
            qv, sp, sc = _band_scores(q_ref, k_ref, b_ref, qq, kq, bq, cur, prv, first)
            rows, prow = pl.ds(cur, BLK), pl.ds(prv, BLK)
            lse_v = lse_ref[qq, rows, :]
            pp, pc = jnp.exp(sp - lse_v), jnp.exp(sc - lse_v)
            dov = do_ref[qq, rows, :]
            dob = dov.astype(BF16)
            coef = dlse_ref[qq, rows, :] - jnp.sum(dov * o_ref[qq, rows, :], axis=1, keepdims=True)
            dsp = pp * (_dot(dob, v_ref[kq, prow, :], 1, 1) + coef)
            dsc = pc * (_dot(dob, v_ref[kq, rows, :], 1, 1) + coef)
            dspb, dscb = dsp.astype(BF16), dsc.astype(BF16)
            dq_ref[qq, rows, :] = ((_dot(dspb, k_ref[kq, prow, :], 1, 0) + _dot(dscb, k_ref[kq, rows, :], 1, 0))
                                   * (HEAD_DIM ** -0.5))
            dk_ref[kq, rows, :] += _dot(dscb, qv, 0, 0)
            dkp_ref[kq, prow, :] += _dot(dspb, qv, 0, 0)
            dv_ref[kq, rows, :] += _dot(pc.astype(BF16), dob, 0, 0)
            dvp_ref[kq, prow, :] += _dot(pp.astype(BF16), dob, 0, 0)
            db_ref[bq, :, :BLK] += dsp
            db_ref[bq, :, BLK:] += dsc
            if has_sink:
                dsk_ref[qq] += jnp.sum(jnp.exp(s_ref[qq][:, :1] - lse_v) * coef)

        _band_sweep(length, ns, one)
        dk_ref[...] += dkp_ref[...]
        dv_ref[...] += dvp_ref[...]

    xspecs, qspec, kvspec, bspec, sspec, colspec = _band_specs(length, g, bias_div, offs)
    return pl.pallas_call(
        body, name=name,
        out_shape=[jax.ShapeDtypeStruct((nq, length, HEAD_DIM), F32), jax.ShapeDtypeStruct((nk, length, HEAD_DIM), F32),
                   jax.ShapeDtypeStruct((nk, length, HEAD_DIM), F32), jax.ShapeDtypeStruct((nbias, BLK, 2 * BLK), F32),
                   jax.ShapeDtypeStruct((nq, 1, LANES), F32)],
        grid=(nq // ns,),
        in_specs=xspecs + [bspec, sspec, qspec, colspec, qspec, colspec],
        out_specs=[qspec, kvspec, kvspec, bspec, sspec],
        scratch_shapes=[pltpu.VMEM((ns, length, HEAD_DIM), F32), pltpu.VMEM((ns, length, HEAD_DIM), F32)],
        compiler_params=_params(("arbitrary",)),
    )(x, x, x, bias, sink, o, lse, do, dlse)


TOK_TILE = 512


def _dil_merge(outs, lses, dout, name):
    tr = TOK_TILE
    dils = [d for _, d in DIL_PATTERNS]
    n = len(dils)
    o4 = [o.reshape(2, d, SEQ // d, HEAD_DIM) for o, d in zip(outs, dils)]
    l4 = [l.reshape(2, d, SEQ // d, 1) for l, d in zip(lses, dils)]
    o_specs = [pl.BlockSpec((2, d, tr // d, HEAD_DIM), lambda i: (0, 0, i, 0)) for d in dils]
    l_specs = [pl.BlockSpec((2, d, tr // d, 1), lambda i: (0, 0, i, 0)) for d in dils]
    tok = pl.BlockSpec((tr, 2 * HEAD_DIM), lambda i: (i, 0))
    scratch = ([pltpu.VMEM((tr, 2 * HEAD_DIM), F32) for _ in dils] + [pltpu.VMEM((tr, 1), F32) for _ in range(2 * n)]
               + [pltpu.VMEM((tr // d, 2 * HEAD_DIM), F32) for d in dils])

    def to_tokens(o_ref, l_ref, d, pair, cols, stage):
        for r in range(d):
            rows = pl.ds(r, tr // d, stride=d) if d > 1 else slice(None)
            stage[:, :HEAD_DIM] = o_ref[0, r]
            stage[:, HEAD_DIM:] = o_ref[1, r]
            pair[rows, :] = stage[...]
            for h in range(2):
                cols[h][rows, :] = l_ref[h, r]
        return pair[...], [cols[0][...], cols[1][...]]

    def weights(ls):
        left = lax.broadcasted_iota(jnp.int32, (tr, 2 * HEAD_DIM), 1) < HEAD_DIM
        per_head = []
        for h in range(2):
            m = ls[0][h]
            for g in range(1, n):
                m = jnp.maximum(m, ls[g][h])
            es = [jnp.exp(ls[g][h] - m) for g in range(n)]
            den = es[0]
            for e in es[1:]:
                den = den + e
            per_head.append([e / den for e in es])
        return per_head, [jnp.where(left, per_head[0][g], per_head[1][g]) for g in range(n)], left

    def load(refs):
        pairs, cols, stages = refs[:n], refs[n:3 * n], refs[3 * n:]
        return pairs, [cols[2 * g:2 * g + 2] for g in range(n)], stages

    if dout is None:
        def body(*refs):
            pairs, cols, stages = load(refs[2 * n + 1:])
            toks = [to_tokens(refs[g], refs[n + g], dils[g], pairs[g], cols[g], stages[g]) for g in range(n)]
            _, alphas, _ = weights([t[1] for t in toks])
            acc = alphas[0] * toks[0][0]
            for g in range(1, n):
                acc = acc + alphas[g] * toks[g][0]
            refs[2 * n][...] = acc

        return pl.pallas_call(
            body, name=name, out_shape=jax.ShapeDtypeStruct((SEQ, 2 * HEAD_DIM), F32), grid=(SEQ // tr,),
            in_specs=o_specs + l_specs, out_specs=tok, scratch_shapes=scratch, compiler_params=_params(("parallel",)),
        )(*o4, *l4)

    def body(*refs):
        do_refs, dl_refs = refs[2 * n + 1:3 * n + 1], refs[3 * n + 1:4 * n + 1]
        pairs, cols, stages = load(refs[4 * n + 1:])
        toks = [to_tokens(refs[g], refs[n + g], dils[g], pairs[g], cols[g], stages[g]) for g in range(n)]
        per_head, alphas, left = weights([t[1] for t in toks])
        dov = refs[2 * n][...]
        das = []
        for g in range(n):
            prod = dov * toks[g][0]
            das.append([jnp.sum(jnp.where(left, prod, 0.0), axis=1, keepdims=True),
                        jnp.sum(jnp.where(left, 0.0, prod), axis=1, keepdims=True)])
        dbar = [sum(per_head[h][g] * das[g][h] for g in range(n)) for h in range(2)]
        for g, d in enumerate(dils):
            pairs[g][...] = alphas[g] * dov
            for h in range(2):
                cols[g][h][...] = per_head[h][g] * (das[g][h] - dbar[h])
            for r in range(d):
                rows = pl.ds(r, tr // d, stride=d) if d > 1 else slice(None)
                v = pairs[g][rows, :]
                for h in range(2):
                    do_refs[g][h, r] = v[:, h * HEAD_DIM:(h + 1) * HEAD_DIM]
                    dl_refs[g][h, r] = cols[g][h][rows, :]

    out = pl.pallas_call(
        body, name=name,
        out_shape=[jax.ShapeDtypeStruct(o.shape, F32) for o in o4] + [jax.ShapeDtypeStruct(l.shape, F32) for l in l4],
        grid=(SEQ // tr,), in_specs=o_specs + l_specs + [tok], out_specs=o_specs + l_specs, scratch_shapes=scratch,
        compiler_params=_params(("parallel",)),
    )(*o4, *l4, dout)
    return [t.reshape(s.shape) for t, s in zip(out, list(outs) + list(lses))]


def _tri(cmp):
    r = lax.broadcasted_iota(jnp.int32, (SB_TILE, SB_TILE), 0)
    c = lax.broadcasted_iota(jnp.int32, (SB_TILE, SB_TILE), 1)
    return cmp(r, c).astype(BF16)


def _cum(x, tri, terms):
    acc, rest = None, x
    for _ in range(terms):
        part = rest.astype(BF16)
        rest = rest - part.astype(F32)
        d = _dot(part, tri, 1, 0)
        acc = d if acc is None else acc + d
    return acc


def _sb_logits(q, ks, diagonal):
    t = SB_TILE
    z = _dot(q, ks, 1, 1)
    e = jnp.exp(-jnp.abs(z))
    lf = -(jnp.maximum(z, 0.0) + jnp.log(1.0 + e))
    if not diagonal:
        return z, e, lf, None
    mask = lax.broadcasted_iota(jnp.int32, (t, t), 1) < lax.broadcasted_iota(jnp.int32, (t, t), 0)
    return z, e, jnp.where(mask, lf, 0.0), mask


def _sb_specs(h, s):
    t = SB_TILE
    tile = pl.BlockSpec((h, t, HEAD_DIM), lambda i: (0, i, 0))
    keys = pl.BlockSpec((h, s, HEAD_DIM), lambda i: (1, 0, 0))
    values = pl.BlockSpec((h, s, HEAD_DIM), lambda i: (2, 0, 0))
    return tile, keys, values, pl.BlockSpec((h, t, 1), lambda i: (0, i, 0))


def _sb_fwd(x, name):
    h, s = x.shape[0] // 3, x.shape[1]
    t = SB_TILE

    def body(q_ref, k_ref, v_ref, o_ref, tot_ref):
        i = pl.program_id(0)
        after = _tri(lambda r, c: r > c)

        def tile(j, carry, diagonal):
            rows = pl.ds(pl.multiple_of(j * t, t), t)
            out = []
            for hh, (right, acc) in enumerate(carry):
                z, _, lf, mask = _sb_logits(q_ref[hh], k_ref[hh, rows, :], diagonal)
                w = jnp.exp(z + lf + (right + _cum(lf, after, 2)))
                w = w if mask is None else jnp.where(mask, w, 0.0)
                out.append((right + jnp.sum(lf, axis=1, keepdims=True), acc + _dot(w.astype(BF16), v_ref[hh, rows, :], 1, 0)))
            return tuple(out)

        carry = tile(i, tuple((jnp.zeros((t, 1), F32), jnp.zeros((t, HEAD_DIM), F32)) for _ in range(h)), True)
        carry = lax.fori_loop(0, i, lambda jj, c: tile(i - 1 - jj, c, False), carry)
        for hh, (right, acc) in enumerate(carry):
            o_ref[hh] = acc
            tot_ref[hh] = right

    tile_spec, keys, values, col = _sb_specs(h, s)
    return pl.pallas_call(
        body, name=name, out_shape=[jax.ShapeDtypeStruct((h, s, HEAD_DIM), F32), jax.ShapeDtypeStruct((h, s, 1), F32)],
        grid=(s // t,), in_specs=[tile_spec, keys, values], out_specs=[tile_spec, col],
        compiler_params=_params(("parallel",)),
    )(x, x, x)


def _sb_bwd(x, tot, do, name):
    h, s = x.shape[0] // 3, x.shape[1]
    t = SB_TILE

    def body(q_ref, k_ref, v_ref, tot_ref, do_ref, dq_ref, dk_ref, dv_ref):
        i = pl.program_id(0)

        @pl.when(i == 0)
        def _():
            dk_ref[...] = jnp.zeros_like(dk_ref)
            dv_ref[...] = jnp.zeros_like(dv_ref)

        upto = _tri(lambda r, c: r <= c)
        before = _tri(lambda r, c: r < c)

        def tile(j, carry, diagonal):
            rows = pl.ds(pl.multiple_of(j * t, t), t)
            out = []
            for hh, (left, cleft, dq) in enumerate(carry):
                qv, ks, dob = q_ref[hh], k_ref[hh, rows, :], do_ref[hh].astype(BF16)
                z, e, lf, mask = _sb_logits(qv, ks, diagonal)
                between = tot_ref[hh] - (left + _cum(lf, upto, 2))
                w = jnp.exp(z + lf + between)
                w = w if mask is None else jnp.where(mask, w, 0.0)
                dlog = w * _dot(dob, v_ref[hh, rows, :], 1, 1)
                cfail = cleft + _cum(dlog, before, 2)
                sig = jnp.where(z >= 0.0, 1.0, e) / (1.0 + e)
                dz = dlog * (1.0 - sig) - sig * cfail
                dz = (dz if mask is None else jnp.where(mask, dz, 0.0)).astype(BF16)
                dk_ref[hh, rows, :] += _dot(dz, qv, 0, 0)
                dv_ref[hh, rows, :] += _dot(w.astype(BF16), dob, 0, 0)
                out.append((left + jnp.sum(lf, axis=1, keepdims=True), cleft + jnp.sum(dlog, axis=1, keepdims=True),
                            dq + _dot(dz, ks, 1, 0)))
            return tuple(out)

        zero = jnp.zeros((t, 1), F32)
        carry = lax.fori_loop(0, i, lambda j, c: tile(j, c, False),
                              tuple((zero, zero, jnp.zeros((t, HEAD_DIM), F32)) for _ in range(h)))
        for hh, (_, _, dq) in enumerate(tile(i, carry, True)):
            dq_ref[hh] = dq * (HEAD_DIM ** -0.5)

    tile_spec, keys, values, col = _sb_specs(h, s)
    full = pl.BlockSpec((h, s, HEAD_DIM), lambda i: (0, 0, 0))
    shp = jax.ShapeDtypeStruct((h, s, HEAD_DIM), F32)
    return pl.pallas_call(
        body, name=name, out_shape=[shp, shp, shp], grid=(s // t,),
        in_specs=[tile_spec, keys, values, col, tile_spec],
        out_specs=[tile_spec, full, full], compiler_params=_params(("arbitrary",)),
    )(x, x, x, tot, do)


COL_SB, COL_DIL, COL_SWA = 0, 3 * H_SB * HEAD_DIM, 3 * H_SB * HEAD_DIM + 3 * H_DIL * HEAD_DIM
N_SWA = H_SWA_Q + 2 * H_SWA_KV


def _dil_col(t, g):
    return COL_DIL + t * H_DIL * HEAD_DIM + g * 2 * HEAD_DIM


def _split_heads(qkv, name):
    tr = TOK_TILE
    scale = HEAD_DIM ** -0.5
    dils = [d for _, d in DIL_PATTERNS]

    def body(x_ref, sb_ref, d0_ref, d1_ref, d2_ref, swa_ref, pair):
        def head(col, scaled):
            v = x_ref[:, col:col + HEAD_DIM]
            return (v * scale if scaled else v).astype(BF16)

        for hh in range(3 * H_SB):
            sb_ref[hh] = head(COL_SB + hh * HEAD_DIM, hh < H_SB)
        for hh in range(N_SWA):
            swa_ref[hh] = head(COL_SWA + hh * HEAD_DIM, hh < H_SWA_Q)
        for t in range(3):
            for g, (d, out_ref) in enumerate(zip(dils, (d0_ref, d1_ref, d2_ref))):
                col = _dil_col(t, g)
                if d == 1:
                    for h in range(2):
                        out_ref[t * 2 + h] = head(col + h * HEAD_DIM, t == 0)
                    continue
                pair[...] = x_ref[:, col:col + 2 * HEAD_DIM]
                for r in range(d):
                    v = pair[pl.ds(r, tr // d, stride=d), :]
                    v = v * scale if t == 0 else v
                    for h in range(2):
                        out_ref[t * 2 * d + h * d + r] = v[:, h * HEAD_DIM:(h + 1) * HEAD_DIM].astype(BF16)

    def heads(n, length):
        return jax.ShapeDtypeStruct((n, length, HEAD_DIM), BF16)

    def spec(n, rows):
        return pl.BlockSpec((n, rows, HEAD_DIM), lambda i: (0, i, 0))

    return pl.pallas_call(
        body, name=name,
        out_shape=[heads(3 * H_SB, SEQ)] + [heads(6 * d, SEQ // d) for d in dils] + [heads(N_SWA, SEQ)],
        grid=(SEQ // tr,), in_specs=[pl.BlockSpec((tr, D_QKV), lambda i: (i, 0))],
        out_specs=[spec(3 * H_SB, tr)] + [spec(6 * d, tr // d) for d in dils] + [spec(N_SWA, tr)],
        scratch_shapes=[pltpu.VMEM((tr, 2 * HEAD_DIM), F32)], compiler_params=_params(("parallel",)),
    )(qkv)


def _join_heads(sb, dil, swa, name):
    tr = TOK_TILE
    dils = [d for _, d in DIL_PATTERNS]

    def body(*refs):
        sb_refs, dil_refs, swa_refs = refs[:3], [refs[3 + 3 * g:6 + 3 * g] for g in range(3)], refs[12:15]
        o_ref, pair, stages = refs[15], refs[16], refs[17:]

        def put(col, v):
            o_ref[:, col:col + v.shape[1]] = v.astype(BF16)

        for t in range(3):
            for h in range(H_SB):
                put(COL_SB + (t * H_SB + h) * HEAD_DIM, sb_refs[t][h])
        col = COL_SWA
        for ref in swa_refs:
            for h in range(ref.shape[0]):
                put(col, ref[h])
                col += HEAD_DIM
        for t in range(3):
            for g, d in enumerate(dils):
                ref, col = dil_refs[g][t], _dil_col(t, g)
                if d == 1:
                    for h in range(2):
                        put(col + h * HEAD_DIM, ref[h])
                    continue
                stage = stages[g - 1]
                for r in range(d):
                    stage[:, :HEAD_DIM] = ref[r]
                    stage[:, HEAD_DIM:] = ref[d + r]
                    pair[pl.ds(r, tr // d, stride=d), :] = stage[...]
                put(col, pair[...])

    def spec(n, rows):
        return pl.BlockSpec((n, rows, HEAD_DIM), lambda i: (0, i, 0))

    ins = list(sb) + [t for g in range(3) for t in dil[g]] + list(swa)
    in_specs = ([spec(H_SB, tr)] * 3 + [spec(2 * d, tr // d) for d in dils for _ in range(3)]
                + [spec(H_SWA_Q, tr), spec(H_SWA_KV, tr), spec(H_SWA_KV, tr)])
    return pl.pallas_call(
        body, name=name, out_shape=jax.ShapeDtypeStruct((SEQ, D_QKV), BF16), grid=(SEQ // tr,), in_specs=in_specs,
        out_specs=pl.BlockSpec((tr, D_QKV), lambda i: (i, 0)),
        scratch_shapes=[pltpu.VMEM((tr, 2 * HEAD_DIM), F32)] + [pltpu.VMEM((tr // d, 2 * HEAD_DIM), F32) for d in dils[1:]],
        compiler_params=_params(("parallel",)),
    )(*ins)


def _mixer_fwd(qkv, bias, sinks_l, tag):
    sb, d0, d1, d2, swa = _split_heads(qkv, name=f"split_heads_{tag}")
    st = {"sb": sb, "dil": (d0, d1, d2), "swa": swa}
    o_sb, st["sb_tot"] = _sb_fwd(sb, name=f"sb_fwd_{tag}")
    st["dil_out"], st["dil_lse"], st["dil_sink"] = [], [], []
    for gi, (_, d) in enumerate(DIL_PATTERNS):
        sink = jnp.zeros((2 * d, 1, LANES), F32)
        og, lg = _band_fwd(st["dil"][gi], bias[2 * gi:2 * gi + 2], sink, nq=2 * d, offs=(0, 2 * d, 4 * d), g=1, bias_div=d,
                           has_sink=False, name=f"dil{gi}_fwd_{tag}")
        st["dil_out"].append(og)
        st["dil_lse"].append(lg)
        st["dil_sink"].append(sink)
    o_dil = _dil_merge(st["dil_out"], st["dil_lse"], None, name=f"dil_merge_fwd_{tag}")
    st["swa_sink"] = jnp.broadcast_to(sinks_l.reshape(H_SWA_Q, 1, 1), (H_SWA_Q, 1, LANES))
    st["swa_out"] = _band_fwd(swa, bias[H_DIL:], st["swa_sink"], nq=H_SWA_Q, offs=(0, H_SWA_Q, H_SWA_Q + H_SWA_KV),
                              g=H_SWA_Q // H_SWA_KV, bias_div=1, has_sink=True, name=f"swa_fwd_{tag}")
    return (o_sb, o_dil, st["swa_out"][0]), st


def _mixer_bwd(st, bias, do_sb, do_dil, do_swa, tag):
    d_sb = _sb_bwd(st["sb"], st["sb_tot"], do_sb, name=f"sb_bwd_{tag}")
    dmerge = _dil_merge(st["dil_out"], st["dil_lse"], do_dil, name=f"dil_merge_bwd_{tag}")
    d_dil, dbs = [], []
    for gi, (_, d) in enumerate(DIL_PATTERNS):
        dq, dk, dv, db, _ = _band_bwd(st["dil"][gi], bias[2 * gi:2 * gi + 2], st["dil_sink"][gi], st["dil_out"][gi],
                                      st["dil_lse"][gi], dmerge[gi], dmerge[3 + gi], nq=2 * d, offs=(0, 2 * d, 4 * d),
                                      g=1, bias_div=d, has_sink=False, name=f"dil{gi}_bwd_{tag}")
        d_dil.append((dq, dk, dv))
        dbs.append(db)
    o_sw, l_sw = st["swa_out"]
    dq_sw, dk_sw, dv_sw, db_sw, dsink = _band_bwd(st["swa"], bias[H_DIL:], st["swa_sink"], o_sw, l_sw, do_swa,
                                                  jnp.zeros_like(l_sw), nq=H_SWA_Q, offs=(0, H_SWA_Q, H_SWA_Q + H_SWA_KV),
                                                  g=H_SWA_Q // H_SWA_KV, bias_div=1, has_sink=True, name=f"swa_bwd_{tag}")
    dqkv = _join_heads(d_sb, d_dil, (dq_sw, dk_sw, dv_sw), name=f"join_heads_{tag}")
    return dqkv, jnp.concatenate(dbs + [db_sw], 0), dsink[:, 0, 0]


PIECES = ("ffn0", "mix", "ffn1")


def _ffn_fwd(x_in, w, gain, mod_j, tag, after=None):
    st = {"x": x_in}
    st["h"] = _norm_fwd(x_in, _row(gain), _row(mod_j[1]), _row(mod_j[0]), name=f"norm_fwd_{tag}", after=after)
    st["a"], st["u"], st["s"] = _ffn_up(st["h"], w["gate"], w["up"], name=f"up_{tag}")
    w = dict(w, down=w["down"](st["s"])) if callable(w["down"]) else w
    st["w"] = w
    st["f"], x_out = _mm(st["s"], w["down"], res=x_in, colscale=_row(0.5 * mod_j[2]), emit_acc=True, tm=512, tn=1024,
                         name=f"down_{tag}")
    return x_out, st


def _ffn_bwd(dx_out, st, gain, mod_j, tag, done, pre, nxt):
    w = st["w"]

    def latest(new, old):
        return old if new is None else new

    df, dgate = pre or _gate_bwd(dx_out, st["f"], _row(0.5 * mod_j[2]), 0.5, name=f"gate_bwd_{tag}")
    dwd = _mm_tn(st["s"], df, tm=D_FF // 2, name=f"dwd_{tag}")
    token = latest(done({"down": dwd}), dwd)
    da, du = _ffn_bwd_ds(df, w["down"], st["a"], st["u"], name=f"ds_{tag}")
    dwg = _mm_tn(da, st["h"], after=token, tm=D_FF // 2, name=f"dwg_{tag}")
    token = latest(done({"gate": dwg}), dwg)
    dwu = _mm_tn(du, st["h"], after=token, tm=D_FF // 2, name=f"dwu_{tag}")
    token = latest(done({"up": dwu}), dwu)
    dx_in, sum_dh, sum_dhx, made = _dh_norm_bwd(da, w["gate"], du, w["up"], st["x"], dx_out, _row(gain), _row(mod_j[1]), nxt,
                                                after=token, name=f"dh_{tag}")
    dmod = jnp.concatenate([sum_dh, gain * sum_dhx, dgate], 0)
    return dx_in, dmod, (1.0 + mod_j[1]) * sum_dhx[0], made


def _mix_fwd(x_in, w, gain, mod_j, bias, sinks_l, tag, after=None):
    st = {"x": x_in, "w": w}
    st["h"] = _norm_fwd(x_in, _row(gain), _row(mod_j[1]), _row(mod_j[0]), name=f"norm_fwd_mix_{tag}", after=after)
    qkv = _mm(st["h"], w["in"], tb=True, tm=SEQ, b_rows=(0, D_QKV), name=f"qkv_{tag}")
    st["gates"] = _mm(st["h"], w["in"], tb=True, tm=SEQ, b_rows=(D_QKV, D_GATES), name=f"gates_{tag}")
    outs, st["mix"] = _mixer_fwd(qkv, bias, sinks_l, tag)
    st["merged"], *st["t"] = _merge_fwd(*outs, st["gates"], w["br_sb"], w["br_dil"], w["br_swa"], name=f"merge_fwd_{tag}")
    st["f"], x_out = _mm(st["merged"], w["out"], res=x_in, colscale=_row(mod_j[2]), emit_acc=True, name=f"out_{tag}")
    return x_out, st


def _mix_bwd(dx_out, st, gain, mod_j, bias, tag, done, pre, nxt):
    w = st["w"]
    df, dgate = pre or _gate_bwd(dx_out, st["f"], _row(mod_j[2]), 1.0, name=f"gate_bwd_mix_{tag}")
    g = {"out": _mm_tn(st["merged"], df, name=f"dw_out_{tag}")}
    dmerged = _mm(df, w["out"], tb=True, name=f"dmerged_{tag}")
    dgates, do_sb, do_dil, do_swa, dbr_sb, dbr_dil, dbr_swa = _merge_bwd(
        dmerged, *st["t"], st["gates"], w["br_sb"], w["br_dil"], w["br_swa"], name=f"merge_bwd_{tag}")
    g["br_sb"] = _mm_tn(st["t"][0], dbr_sb, name=f"dw_br_sb_{tag}")
    g["br_dil"] = _mm_tn(st["t"][1], dbr_dil, name=f"dw_br_dil_{tag}")
    g["br_swa"] = _mm_tn(st["t"][2], dbr_swa, name=f"dw_br_swa_{tag}")
    dqkv, dbias, dsinks = _mixer_bwd(st["mix"], bias, do_sb, do_dil, do_swa, tag)
    dw_qkv = _mm_tn(dqkv, st["h"], out_rows=D_QKV + D_GATES, name=f"dw_qkv_{tag}")
    g["in"] = _mm_tn(dgates, st["h"], out_rows=D_QKV + D_GATES, row0=D_QKV, prev=dw_qkv, name=f"dw_gates_{tag}")
    dx_in, sum_dh, sum_dhx, made = _dh_norm_bwd(dqkv, w["in"], dgates, w["in"], st["x"], dx_out, _row(gain), _row(mod_j[1]),
                                                nxt, after=done(g), b_rows=(0, D_QKV), name=f"dh_mix_{tag}")
    dmod = jnp.concatenate([sum_dh, gain * sum_dhx, dgate], 0)
    return dx_in, dmod, (1.0 + mod_j[1]) * sum_dhx[0], dbias, dsinks, made


def _local_step(x, target, mod, gains, weights_of, rel_bias, sinks, final_gain, grads_done):
    tables = jnp.asarray(_bucket_tables())
    bias = _bias_build(rel_bias, tables, name="bias_build")
    states, h = [], x
    for l in range(DEPTH):
        st = {}
        for j, piece in enumerate(PIECES):
            w, after = weights_of(l, piece, h)
            if piece == "mix":
                h, st[piece] = _mix_fwd(h, w, gains[l, j], mod[l, j], bias, sinks[l], f"l{l}", after)
            else:
                h, st[piece] = _ffn_fwd(h, w, gains[l, j], mod[l, j], f"{piece}_l{l}", after)
        states.append(st)
    loss, dx, dfinal = _final_loss(h, target, _row(final_gain), name="final_loss")
    dmods = [[None] * 3 for _ in range(DEPTH)]
    dgains = [[None] * 3 for _ in range(DEPTH)]
    dsinks = [None] * DEPTH
    dbias, made = None, None
    sweep = [(l, j) for l in reversed(range(DEPTH)) for j in reversed(range(3))]
    for k, (l, j) in enumerate(sweep):
        piece = PIECES[j]
        done = lambda grads, l=l, piece=piece: grads_done(l, piece, grads)
        nxt = None
        if k + 1 < len(sweep):
            nl, nj = sweep[k + 1]
            coef = 1.0 if PIECES[nj] == "mix" else 0.5
            nxt = (states[nl][PIECES[nj]]["f"], _row(coef * mod[nl, nj, 2]), coef)
        if piece == "mix":
            dx, dmods[l][j], dgains[l][j], db, dsinks[l], made = _mix_bwd(
                dx, states[l][piece], gains[l, j], mod[l, j], bias, f"l{l}", done, made, nxt)
            dbias = db if dbias is None else dbias + db
        else:
            dx, dmods[l][j], dgains[l][j], made = _ffn_bwd(
                dx, states[l][piece], gains[l, j], mod[l, j], f"{piece}_l{l}", done, made, nxt)
    drel = _bias_grad(dbias, tables, name="bias_grad")[:, 0, :N_BUCKETS].T
    dmod = jnp.stack([jnp.stack(m) for m in dmods])
    dgain = jnp.stack([jnp.stack(g) for g in dgains])
    return loss, dx, dmod, dgain, dfinal[0], drel, jnp.stack(dsinks)


BR_ROWS = (H_SB * HEAD_DIM, 2 * HEAD_DIM, H_SWA_Q * HEAD_DIM)


def _lanes_unshard(g, lead):
    _, rows, _ = g.shape
    r = rows // lead
    return g.reshape(N_DEV, lead, r, LANES).transpose(1, 2, 0, 3).reshape(lead, r, N_DEV * LANES)


def _lanes_shard(full):
    lead, r, _ = full.shape
    return full.reshape(lead, r, N_DEV, LANES).transpose(2, 0, 1, 3).reshape(N_DEV, lead * r, LANES)


def _pack_rows(parts, dtype):
    flat = jnp.concatenate([p.astype(dtype).reshape(-1) for p in parts])
    pad = (-flat.shape[0]) % (16 * LANES)
    if pad:
        flat = jnp.concatenate([flat, jnp.zeros((pad,), dtype)])
    return flat.reshape(-1, LANES)


def _unshard(gathered, axis):
    moved = jnp.moveaxis(gathered, 0, axis)
    shape = list(moved.shape)
    shape[axis:axis + 2] = [shape[axis] * shape[axis + 1]]
    return moved.reshape(shape)


def kernel(x, c, w_ada, b_ada, norm_gain, w_ffn_gate, w_ffn_up, w_ffn_down, w_in, w_br_sb, w_br_dil, w_br_swa, w_out, sinks, rel_bias, final_gain, loss_target, m_w_ada, m_b_ada, m_norm_gain, m_w_ffn_gate, m_w_ffn_up, m_w_ffn_down, m_w_in, m_w_br_sb, m_w_br_dil, m_w_br_swa, m_w_out, m_sinks, m_rel_bias, m_final_gain, v_w_ada, v_b_ada, v_norm_gain, v_w_ffn_gate, v_w_ffn_up, v_w_ffn_down, v_w_in, v_w_br_sb, v_w_br_dil, v_w_br_swa, v_w_out, v_sinks, v_rel_bias, v_final_gain):
    me = 4 * lax.axis_index("x") + 2 * lax.axis_index("y") + lax.axis_index("c")
    d = D_MODEL
    gate_t, up_t, in_t = jnp.swapaxes(w_ffn_gate, 2, 3), jnp.swapaxes(w_ffn_up, 2, 3), jnp.swapaxes(w_in, 1, 2)

    def piece_shards(l, piece):
        bf = lambda t: t.astype(BF16)
        if piece == "mix":
            return [bf(in_t[l]), jnp.concatenate([bf(w_br_sb[l]), bf(w_br_dil[l]), bf(w_br_swa[l])], 0), bf(w_out[l])]
        i = PIECES.index(piece) // 2
        return [bf(gate_t[l, i]), bf(up_t[l, i]), bf(w_ffn_down[l, i])]

    br_off = np.concatenate([[0], np.cumsum(BR_ROWS)])

    def piece_weights(gathered, piece):
        if piece == "mix":
            g_in, g_br, g_out = gathered
            f_br = [_lanes_unshard(g_br[:, br_off[k]:br_off[k + 1]], 1)[0] for k in range(3)]
            return {"in": g_in.reshape(D_QKV + D_GATES, d), "br_sb": f_br[0], "br_dil": f_br[1], "br_swa": f_br[2],
                    "out": g_out.reshape(d, d)}
        return {n: g.reshape(D_FF, d) for n, g in zip(("gate", "up", "down"), gathered)}

    order = [(l, piece) for l in range(DEPTH) for piece in PIECES]
    ahead = 3
    in_flight, passed = {}, {}

    def start_gather(k, after, part=slice(None), tag=""):
        l, piece = order[k]
        state, token = _relay_start(piece_shards(l, piece)[part], after, name=f"gather_{piece}{tag}_l{l}_start")
        in_flight.setdefault(k, []).append(state)
        return token

    small, = _all_gather([_pack_rows([c, norm_gain], F32)], after=start_gather(0, c, slice(0, 1)), name="gather_cond")
    token = start_gather(0, small, slice(1, 2), "_up")
    c_all = small[:, :d // LANES].reshape(N_DEV, d)
    gains = _unshard(small[:, d // LANES:d // LANES + 6].reshape(N_DEV, DEPTH, 3, LANES), 2)

    cols = w_ada.shape[2]
    mod_cols = jnp.stack([_ada_fwd(c_all, w_ada[l], name=f"ada_fwd_l{l}") for l in range(DEPTH)])
    mod_all, = _all_gather([_pack_rows([mod_cols], F32)], after=token, name="gather_mod")
    mod_all = mod_all.reshape(N_DEV, -1)[:, :DEPTH * N_DEV * cols].reshape(N_DEV, DEPTH, N_DEV, cols)
    mod_mine = lax.dynamic_index_in_dim(mod_all, me, axis=2, keepdims=False)
    mod = (mod_mine.transpose(1, 0, 2).reshape(DEPTH, N_DEV * cols) + b_ada).reshape(DEPTH, 3, 3, d)

    token = start_gather(0, mod_all, slice(2, 3), "_down")
    for k in range(1, 1 + ahead):
        token = start_gather(k, token)
    mod = mod + token[0, 0]

    def first_down(s):
        state, token = _relay_pass(in_flight[0][2], s, name="gather_ffn0_down_l0_pass")
        return _relay_wait(state, token, name="gather_ffn0_down_l0_wait")[0].reshape(D_FF, d)

    def weights_of(l, piece, h):
        k = order.index((l, piece))
        token = start_gather(k + ahead, h) if k + ahead < len(order) and k + ahead not in in_flight else None
        for nxt in ([k] if k < 3 else []) + ([k + 1] if 3 <= k + 1 < len(order) else []):
            nl, npiece = order[nxt]
            passed[nxt], token = _relay_pass(in_flight[nxt][0], h if token is None else token,
                                             name=f"gather_{npiece}_l{nl}_pass")
        if k == 0:
            up_state, token = _relay_pass(in_flight[0][1], token, name="gather_ffn0_up_l0_pass")
        landed = _relay_wait(passed[k], h if token is None else token, name=f"gather_{piece}_l{l}_wait")
        weights = piece_weights(landed, piece)
        if k == 0:
            weights["up"] = _relay_wait(up_state, token, name="gather_ffn0_up_l0_wait")[0].reshape(D_FF, d)
            weights["down"] = first_down
        return weights, token

    exchanges, have, deferred = {}, {}, []

    def grads_done(l, piece, g):
        key = (l, piece)
        have.setdefault(key, {}).update(g)
        if piece == "mix":
            if len(have[key]) < 5:
                return None
            g = have[key]
            s_br = jnp.concatenate([_lanes_shard(g[n][None]) for n in ("br_sb", "br_dil", "br_swa")], 1)
            groups = [(("in", "br", "out"), [g["in"].reshape(N_DEV, -1, d), s_br, g["out"].reshape(N_DEV, -1, d)])]
        elif key == order[0]:
            deferred.extend(((n,), [t.reshape(N_DEV, -1, d)]) for n, t in g.items())
            return None
        elif len(have[key]) < 3:
            return None
        else:
            groups = [(("gate", "up", "down"), [have[key][n].reshape(N_DEV, -1, d) for n in ("gate", "up", "down")])]
        token = None
        for names, sg in groups:
            state, token = _exchange_start(sg, None, name=f"exchange_{piece}_l{l}_{names[0]}_start")
            exchanges.setdefault(key, []).append((names, state))
        return token

    loss, dx, dmod, dgains, dfinal, drel, dsinks = _local_step(
        x[0], loss_target[0], mod, gains, weights_of, rel_bias, sinks, final_gain, grads_done)

    flat = lambda t: t.reshape(-1, t.shape[-1])
    transposed = lambda ts: tuple(flat(jnp.swapaxes(t, -1, -2)) for t in ts)
    families = {
        "gate": transposed((w_ffn_gate, m_w_ffn_gate, v_w_ffn_gate)), "up": transposed((w_ffn_up, m_w_ffn_up, v_w_ffn_up)),
        "down": tuple(flat(t) for t in (w_ffn_down, m_w_ffn_down, v_w_ffn_down)),
        "in": transposed((w_in, m_w_in, v_w_in)),
        "br": tuple(flat(jnp.concatenate(ts, 1)) for ts in ((w_br_sb, w_br_dil, w_br_swa), (m_w_br_sb, m_w_br_dil, m_w_br_swa),
                                                            (v_w_br_sb, v_w_br_dil, v_w_br_swa))),
        "out": tuple(flat(t) for t in (w_out, m_w_out, v_w_out))}
    parts, stepped = {}, {}

    def step(keys, after):
        for key in keys:
            for names, ex_state in exchanges[key]:
                landed = _exchange_wait(ex_state, after, name=f"exchange_{key[1]}_l{key[0]}_{names[0]}_wait")
                parts.setdefault(key, {}).update(zip(names, landed))
                after = landed[0]
        for key in keys:
            l, piece = key
            for n, group in parts[key].items():
                w2, m2, v2 = families[n]
                rows = group.shape[1]
                row0 = (2 * l + PIECES.index(piece) // 2) * rows if piece != "mix" else l * rows
                stepped[n] = _reduce_adamw([group], w2, m2, v2, row0, stepped.get(n), after=after,
                                           name=f"reduce_adamw_{n}_{piece}_l{l}")
                after = stepped[n][1]
        return after

    small_parts = [dmod, dgains, dfinal, drel.T, dsinks, loss[0, :1]]
    small_sizes = [int(np.prod(p.shape)) for p in small_parts]
    small_all, = _all_gather([_pack_rows(small_parts, F32)], name="gather_small")
    token = small_all
    for names, sg in deferred:
        state, token = _exchange_start(sg, token, name=f"exchange_ffn0_l0_{names[0]}_start")
        exchanges.setdefault(order[0], []).append((names, state))

    after_l1 = step([key for key in reversed(order) if key[0] == 1], token)
    small_sum = _sum_parts(small_all, name="sum_small", after=token).reshape(-1)
    offs = np.concatenate([[0], np.cumsum(small_sizes)])
    g_b_ada = small_sum[offs[0]:offs[1]].reshape(DEPTH, 9 * d)
    g_gain_full = small_sum[offs[1]:offs[2]].reshape(DEPTH, 3, d)
    g_norm_gain = lax.dynamic_slice_in_dim(g_gain_full, me * LANES, LANES, axis=2)
    g_final = small_sum[offs[2]:offs[3]]
    g_rel = small_sum[offs[3]:offs[4]].reshape(N_SOFT, N_BUCKETS).T
    g_sinks = small_sum[offs[4]:offs[5]].reshape(DEPTH, H_SWA_Q)
    loss_total = small_sum[offs[5]]

    dmod_all = small_all.reshape(N_DEV, -1)[:, :DEPTH * 9 * d].reshape(N_DEV, DEPTH, 9 * d)
    dmod_cols = lax.dynamic_slice_in_dim(dmod_all, me * cols, cols, axis=2)
    g_w_ada = jnp.stack([_ada_bwd(c_all.T, dmod_cols[:, l], name=f"ada_bwd_l{l}") for l in range(DEPTH)])

    small_state = {"w_ada": (w_ada, m_w_ada, v_w_ada), "b_ada": (b_ada, m_b_ada, v_b_ada),
                   "norm_gain": (norm_gain, m_norm_gain, v_norm_gain), "sinks": (sinks, m_sinks, v_sinks),
                   "rel_bias": (rel_bias, m_rel_bias, v_rel_bias), "final_gain": (final_gain, m_final_gain, v_final_gain)}
    after = step([order[2], order[1]], after_l1)
    grad, update = {}, {}
    for n, g in (("w_ada", g_w_ada), ("b_ada", g_b_ada), ("norm_gain", g_norm_gain), ("sinks", g_sinks),
                 ("rel_bias", g_rel), ("final_gain", g_final)):
        w, m, v = small_state[n]
        grad[n] = g
        if w.ndim == 1:
            update[n] = tuple(t.reshape(w.shape)
                              for t in _adamw(_row(w), _row(g), _row(m), _row(v), name=f"adamw_{n}", after=after))
        else:
            update[n] = _adamw(w, g, m, v, name=f"adamw_{n}", after=after)
        after = update[n][0]

    step([order[0]], after)

    def unflat(n, like, swapped):
        shape = jnp.swapaxes(like, -1, -2).shape if swapped else like.shape
        out = [t.reshape(shape) for t in stepped[n]]
        return [jnp.swapaxes(t, -1, -2) for t in out] if swapped else out

    results = {"w_ffn_gate": unflat("gate", w_ffn_gate, True), "w_ffn_up": unflat("up", w_ffn_up, True),
               "w_ffn_down": unflat("down", w_ffn_down, False), "w_in": unflat("in", w_in, True),
               "w_out": unflat("out", w_out, False)}
    br = [t.reshape(DEPTH, -1, LANES) for t in stepped["br"]]
    for k, n in enumerate(("w_br_sb", "w_br_dil", "w_br_swa")):
        results[n] = [t[:, br_off[k]:br_off[k + 1]] for t in br]
    for n, (g, dl, nm, nv) in results.items():
        grad[n], update[n] = g, (dl, nm, nv)

    names = ["w_ada", "b_ada", "norm_gain", "w_ffn_gate", "w_ffn_up", "w_ffn_down", "w_in", "w_br_sb", "w_br_dil",
             "w_br_swa", "w_out", "sinks", "rel_bias", "final_gain"]
    return (loss_total, dx[None], *[grad[n] for n in names], *[update[n][0] for n in names],
            *[update[n][1] for n in names], *[update[n][2] for n in names])
```

```python
import math

import numpy as np
import jax
import jax.numpy as jnp
from jax import lax
from jax.experimental import pallas as pl
from jax.experimental.pallas import tpu as pltpu

F32, BF16 = jnp.float32, jnp.bfloat16

SEQ, D_MODEL, D_FF, HEAD_DIM = 2048, 1024, 2816, 64
DEPTH = 2
BLK = 128
H_SB, H_DIL, H_SWA_Q, H_SWA_KV = 4, 6, 6, 2
DIL_PATTERNS = ((128, 1), (512, 4), (2048, 16))
SWA_WINDOW = 128
N_BUCKETS, MAX_REL_DIST = 32, 2048
RMS_EPS = 1e-6
D_QKV = 2560
D_GATES = 3 * D_MODEL
ADAM_LR, ADAM_B1, ADAM_B2, ADAM_EPS, ADAM_WD, ADAM_STEP = 0.001, 0.9, 0.999, 1e-08, 0.01, 10

N_DEV = 8
LANES = 128
NEG = -1e30
SB_TILE = 512
VMEM_LIMIT_BYTES = 48 * 1024 * 1024
HBM = pl.BlockSpec(memory_space=pltpu.HBM)
MESH = pl.DeviceIdType.MESH


def _tile(n, target):
    t = (min(n, target) // LANES) * LANES
    while t >= LANES:
        if n % t == 0:
            return t
        t -= LANES
    return n


def _row_tile(r, cap):
    t = (min(r, cap) // 16) * 16
    while t > 16 and r % t:
        t -= 16
    return t


def _params(semantics=None):
    return pltpu.CompilerParams(dimension_semantics=semantics, vmem_limit_bytes=VMEM_LIMIT_BYTES)


def _dot(a, b, ca, cb):
    return lax.dot_general(a, b, (((ca,), (cb,)), ((), ())), preferred_element_type=F32)


def _sigmoid(a):
    return 1.0 / (1.0 + jnp.exp(-a))


def _row(v):
    return v.reshape(1, -1)


def _all_gather(arrs, name, after=None):
    n = len(arrs)
    ins = list(arrs) + ([] if after is None else [after])

    def body(*refs):
        x_refs, out_refs = refs[:n], refs[len(ins):len(ins) + n]
        send_sems, recv_sems, local_sems = refs[len(ins) + n:]
        x, y, c = lax.axis_index("x"), lax.axis_index("y"), lax.axis_index("c")
        me, sibling = (x, y, c), (x, y, 1 - c)
        chips = [(1 - x, y), (x, 1 - y), (1 - x, 1 - y)]

        def slot(t, px, py, pc):
            return out_refs[t].at[4 * px + 2 * py + pc]

        def copy(t, k, block, to, src=None):
            return pltpu.make_async_remote_copy(
                src_ref=slot(t, *block) if src is None else src, dst_ref=slot(t, *block),
                send_sem=send_sems.at[7 * t + k], recv_sem=recv_sems.at[7 * t + k], device_id=to, device_id_type=MESH)

        mine = [pltpu.make_async_copy(x_refs[t], slot(t, *me), local_sems.at[t]) for t in range(n)]
        for cp in mine:
            cp.start()
        first = []
        for t in range(n):
            first.append(copy(t, 0, me, sibling, src=x_refs[t]))
            first += [copy(t, 1 + j, me, (*chip, c), src=x_refs[t]) for j, chip in enumerate(chips)]
        for cp in first:
            cp.start()
        passed = []
        for j, chip in enumerate(chips):
            for t in range(n):
                copy(t, 1 + j, (*chip, c), me).wait_recv()
                passed.append(copy(t, 4 + j, (*chip, c), sibling))
                passed[-1].start()
        for t in range(n):
            copy(t, 0, sibling, me).wait_recv()
        for j, chip in enumerate(chips):
            for t in range(n):
                copy(t, 4 + j, (*chip, 1 - c), me).wait_recv()
        for cp in first + passed:
            cp.wait_send()
        for cp in mine:
            cp.wait()

    return pl.pallas_call(
        body, name=name, out_shape=[jax.ShapeDtypeStruct((N_DEV,) + a.shape, a.dtype) for a in arrs],
        in_specs=[HBM] * n + [pl.BlockSpec(memory_space=pl.ANY)] * (len(ins) - n), out_specs=[HBM] * n,
        scratch_shapes=[pltpu.SemaphoreType.DMA((7 * n,)), pltpu.SemaphoreType.DMA((7 * n,)), pltpu.SemaphoreType.DMA((n,))],
    )(*ins)


def _direct_copies(x_refs, land_refs, send_sems, recv_sems, local_sems):
    x, y, c = lax.axis_index("x"), lax.axis_index("y"), lax.axis_index("c")
    me = 4 * x + 2 * y + c
    sends, recvs = [], []
    for k in range(1, N_DEV):
        px = 1 - x if (k >> 2) & 1 else x
        py = 1 - y if (k >> 1) & 1 else y
        pc = 1 - c if k & 1 else c
        peer = 4 * px + 2 * py + pc
        for t, (x_ref, land_ref) in enumerate(zip(x_refs, land_refs)):
            sem = 7 * t + k - 1
            for out, src, slot in ((sends, peer, me), (recvs, me, peer)):
                out.append(pltpu.make_async_remote_copy(
                    src_ref=x_ref.at[src], dst_ref=land_ref.at[slot], send_sem=send_sems.at[sem],
                    recv_sem=recv_sems.at[sem], device_id=(px, py, pc), device_id_type=MESH))
    own = [pltpu.make_async_copy(x_ref.at[me], land_ref.at[me], local_sems.at[t])
           for t, (x_ref, land_ref) in enumerate(zip(x_refs, land_refs))]
    return sends, recvs, own


SEM =pl.BlockSpec(memory_space=pltpu.SEMAPHORE)
ANY = pl.BlockSpec(memory_space=pl.ANY)
SIDE_EFFECT = pltpu.SideEffectType.DATAFLOW_SIDE_EFFECTING


def _exchange_start(arrs, after, *, name):
    n = len(arrs)
    lands = [lax.empty(a.shape, a.dtype) for a in arrs]
    extra = [] if after is None else [after]

    def body(*refs):
        sems = refs[2 * n + len(extra):2 * n + len(extra) + 3]
        sends, _, own = _direct_copies(refs[:n], refs[n:2 * n], *sems)
        for cp in own + sends:
            cp.start()
        refs[-1][...] = jnp.zeros_like(refs[-1])

    ops = [pltpu.with_memory_space_constraint(a, pltpu.HBM) for a in list(arrs) + lands]
    out = pl.pallas_call(
        body, name=name,
        out_shape=(pltpu.SemaphoreType.DMA((7 * n,)), pltpu.SemaphoreType.DMA((7 * n,)), pltpu.SemaphoreType.DMA((n,)),
                   *[pltpu.HBM(a.shape, a.dtype) for a in ops], jax.ShapeDtypeStruct((8, LANES), F32)),
        in_specs=[HBM] * (2 * n) + [ANY] * len(extra),
        out_specs=(SEM, SEM, SEM, *[HBM] * (2 * n), pl.BlockSpec(memory_space=pltpu.VMEM)),
        input_output_aliases={t: 3 + t for t in range(2 * n)},
        compiler_params=pltpu.CompilerParams(has_side_effects=SIDE_EFFECT),
    )(*ops, *extra)
    return (out[:3], out[3:3 + n], out[3 + n:3 + 2 * n]), out[-1]


def _exchange_wait(state, after, *, name):
    sems, arrs, lands = state
    n = len(arrs)

    def body(*refs):
        sends, recvs, own = _direct_copies(refs[:n], refs[n:2 * n], *refs[2 * n:2 * n + 3])
        for cp in own:
            cp.wait()
        for cp in sends:
            cp.wait_send()
        for cp in recvs:
            cp.wait_recv()

    out = pl.pallas_call(
        body, name=name, out_shape=tuple(pltpu.HBM(a.shape, a.dtype) for a in list(arrs) + list(lands)),
        in_specs=[HBM] * (2 * n) + [SEM, SEM, SEM, ANY], out_specs=tuple([HBM] * (2 * n)),
        input_output_aliases={t: t for t in range(2 * n)},
        compiler_params=pltpu.CompilerParams(has_side_effects=SIDE_EFFECT),
    )(*arrs, *lands, *sems, after)
    return out[n:]


def _relay_copies(x_refs, land_refs, sems_a, sems_b):
    x, y, c = lax.axis_index("x"), lax.axis_index("y"), lax.axis_index("c")
    me = 4 * x + 2 * y + c
    sibling = (x, y, 1 - c)
    chips = [(1 - x, y), (x, 1 - y), (1 - x, 1 - y)]

    def slot(px, py, pc):
        return 4 * px + 2 * py + pc

    def copy(src, land_ref, dst_slot, send_sems, recv_sems, k, to):
        return pltpu.make_async_remote_copy(src_ref=src, dst_ref=land_ref.at[dst_slot], send_sem=send_sems.at[k],
                                            recv_sem=recv_sems.at[k], device_id=to, device_id_type=MESH)

    a_send, a_recv, a_own, b_send, b_recv = [], [], [], [], []
    for t, (x_ref, land_ref) in enumerate(zip(x_refs, land_refs)):
        peers = [sibling] + [(*chip, c) for chip in chips]
        if sems_a is not None:
            for k, peer in enumerate(peers):
                a_send.append(copy(x_ref, land_ref, me, sems_a[0], sems_a[1], 4 * t + k, peer))
                a_recv.append(copy(x_ref, land_ref, slot(*peer), sems_a[0], sems_a[1], 4 * t + k, peer))
            a_own.append(pltpu.make_async_copy(x_ref, land_ref.at[me], sems_a[2].at[t]))
        if sems_b is not None:
            for j, chip in enumerate(chips):
                b_send.append(copy(land_ref.at[slot(*chip, c)], land_ref, slot(*chip, c), sems_b[0], sems_b[1], 3 * t + j, sibling))
                b_recv.append(copy(land_ref.at[slot(*chip, c)], land_ref, slot(*chip, 1 - c), sems_b[0], sems_b[1], 3 * t + j,
                                   sibling))
    return (a_send, a_recv, a_own), (b_send, b_recv)


def _relay_start(arrs, after, name):
    n = len(arrs)
    lands = [lax.empty((N_DEV,) + a.shape, a.dtype) for a in arrs]

    def body(*refs):
        (sends, _, own), _ = _relay_copies(refs[:n], refs[n:2 * n], refs[2 * n + 1:2 * n + 4], None)
        for cp in own + sends:
            cp.start()
        refs[-1][...] = jnp.zeros_like(refs[-1])

    ops = [pltpu.with_memory_space_constraint(a, pltpu.HBM) for a in list(arrs) + lands]
    out = pl.pallas_call(
        body, name=name,
        out_shape=(pltpu.SemaphoreType.DMA((4 * n,)), pltpu.SemaphoreType.DMA((4 * n,)), pltpu.SemaphoreType.DMA((n,)),
                   *[pltpu.HBM(a.shape, a.dtype) for a in ops], jax.ShapeDtypeStruct((8, LANES), F32)),
        in_specs=[HBM] * (2 * n) + [ANY],
        out_specs=(SEM, SEM, SEM, *[HBM] * (2 * n), pl.BlockSpec(memory_space=pltpu.VMEM)),
        input_output_aliases={t: 3 + t for t in range(2 * n)},
        compiler_params=pltpu.CompilerParams(has_side_effects=SIDE_EFFECT),
    )(*ops, after)
    return (out[:3], out[3:3 + n], out[3 + n:3 + 2 * n]), out[-1]


def _relay_pass(state, after, name):
    sems_a, arrs, lands = state
    n = len(arrs)

    def body(*refs):
        sems_b = refs[2 * n + 4:2 * n + 6]
        (a_send, a_recv, a_own), (b_send, _) = _relay_copies(refs[:n], refs[n:2 * n], refs[2 * n:2 * n + 3], sems_b)
        for cp in a_own:
            cp.wait()
        for cp in a_send:
            cp.wait_send()
        for cp in a_recv:
            cp.wait_recv()
        for cp in b_send:
            cp.start()
        refs[-1][...] = jnp.zeros_like(refs[-1])

    out = pl.pallas_call(
        body, name=name,
        out_shape=(pltpu.SemaphoreType.DMA((3 * n,)), pltpu.SemaphoreType.DMA((3 * n,)),
                   *[pltpu.HBM(a.shape, a.dtype) for a in list(arrs) + list(lands)], jax.ShapeDtypeStruct((8, LANES), F32)),
        in_specs=[HBM] * (2 * n) + [SEM, SEM, SEM, ANY],
        out_specs=(SEM, SEM, *[HBM] * (2 * n), pl.BlockSpec(memory_space=pltpu.VMEM)),
        input_output_aliases={t: 2 + t for t in range(2 * n)},
        compiler_params=pltpu.CompilerParams(has_side_effects=SIDE_EFFECT),
    )(*arrs, *lands, *sems_a, after)
    return (out[:2], out[2:2 + n], out[2 + n:2 + 2 * n]), out[-1]


def _relay_wait(state, after, name):
    sems_b, arrs, lands = state
    n = len(arrs)

    def body(*refs):
        _, (b_send, b_recv) = _relay_copies(refs[:n], refs[n:2 * n], None, refs[2 * n:2 * n + 2])
        for cp in b_send:
            cp.wait_send()
        for cp in b_recv:
            cp.wait_recv()

    out = pl.pallas_call(
        body, name=name, out_shape=tuple(pltpu.HBM(a.shape, a.dtype) for a in list(arrs) + list(lands)),
        in_specs=[HBM] * (2 * n) + [SEM, SEM, ANY], out_specs=tuple([HBM] * (2 * n)),
        input_output_aliases={t: t for t in range(2 * n)},
        compiler_params=pltpu.CompilerParams(has_side_effects=SIDE_EFFECT),
    )(*arrs, *lands, *sems_b, after)
    return out[n:]


def _sum_parts(parts, name, after=None):
    n, r, cdim = parts.shape
    tr = _row_tile(r, max(16, (1 << 21) // (n * cdim * parts.dtype.itemsize)))

    def body(p_ref, *rest):
        acc = p_ref[0].astype(F32)
        for k in range(1, n):
            acc = acc + p_ref[k].astype(F32)
        rest[-1][...] = acc

    ins = [parts] + ([] if after is None else [after])
    return pl.pallas_call(
        body, name=name, out_shape=jax.ShapeDtypeStruct((r, cdim), F32), grid=(r // tr,),
        in_specs=[pl.BlockSpec((n, tr, cdim), lambda i: (0, i, 0))] + [ANY] * (len(ins) - 1),
        out_specs=pl.BlockSpec((tr, cdim), lambda i: (i, 0)), compiler_params=_params(("parallel",)),
    )(*ins)


def _mm_tn(a, b, *, name, after=None, tm=512, tn=1024, out_rows=None, row0=0, prev=None):
    k, m = a.shape
    n = b.shape[1]
    tm, tn = _tile(m, tm), _tile(n, tn)
    out_rows = m if out_rows is None else out_rows

    def body(a_ref, b_ref, *rest):
        o_ref, at_ref = rest[-2], rest[-1]

        @pl.when(pl.program_id(1) == 0)
        def _():
            at_ref[...] = a_ref[...].astype(BF16).T

        o_ref[...] = _dot(at_ref[...], b_ref[...].astype(BF16), 1, 0).astype(BF16)

    ins = [a, b] + [t for t in (after, prev) if t is not None]
    return pl.pallas_call(
        body, name=name, out_shape=jax.ShapeDtypeStruct((out_rows, n), BF16), grid=(m // tm, n // tn),
        in_specs=[pl.BlockSpec((k, tm), lambda i, j: (0, i)), pl.BlockSpec((k, tn), lambda i, j: (0, j))] + [ANY] * (len(ins) - 2),
        out_specs=pl.BlockSpec((tm, tn), lambda i, j: (row0 // tm + i, j)),
        input_output_aliases={} if prev is None else {len(ins) - 1: 0},
        scratch_shapes=[pltpu.VMEM((tm, k), BF16)], compiler_params=_params(("parallel", "arbitrary")),
    )(*ins)


def _mm(a, b, *, name, ta=False, tb=False, res=None, colscale=None, emit_acc=False,
        out_dtype=F32, tm=512, tn=512, b_rows=None):
    m, k = (a.shape[1], a.shape[0]) if ta else a.shape
    n = b.shape[0] if tb else b.shape[1]
    b_start = 0
    if b_rows is not None:
        b_start, n = b_rows
    tm, tn = _tile(m, tm), _tile(n, tn)
    ca, cb = (0 if ta else 1), (1 if tb else 0)
    a_spec = pl.BlockSpec((k, tm), lambda i, j: (0, i)) if ta else pl.BlockSpec((tm, k), lambda i, j: (i, 0))
    b_spec = (pl.BlockSpec((tn, k), lambda i, j: (b_start // tn + j, 0)) if tb
              else pl.BlockSpec((k, tn), lambda i, j: (0, j)))
    tile = pl.BlockSpec((tm, tn), lambda i, j: (i, j))
    ins, in_specs = [a, b], [a_spec, b_spec]
    if res is not None:
        ins.append(res)
        in_specs.append(tile)
    if colscale is not None:
        ins.append(colscale)
        in_specs.append(pl.BlockSpec((1, tn), lambda i, j: (0, j)))
    n_in = len(ins)

    def body(*refs):
        outs = refs[n_in:]
        acc = _dot(refs[0][...].astype(BF16), refs[1][...].astype(BF16), ca, cb)
        val, p = acc, 2
        if res is not None:
            r_val, p = refs[p][...], p + 1
        if colscale is not None:
            val = val * refs[p][...]
        if res is not None:
            val = r_val + val
        if emit_acc:
            outs[0][...] = acc
        outs[-1][...] = val.astype(out_dtype)

    out_shape = [jax.ShapeDtypeStruct((m, n), out_dtype)]
    out_specs = [tile]
    if emit_acc:
        out_shape.insert(0, jax.ShapeDtypeStruct((m, n), F32))
        out_specs.insert(0, tile)
    out = pl.pallas_call(
        body, name=name, out_shape=out_shape, grid=(m // tm, n // tn), in_specs=in_specs, out_specs=out_specs,
        compiler_params=_params(("parallel", "parallel")),
    )(*ins)
    return out if emit_acc else out[0]


def _norm_fwd(x, g, scale, shift, name, after=None):
    s, d = x.shape
    tr = 256

    def body(x_ref, g_ref, sc_ref, sh_ref, *rest):
        xv = x_ref[...]
        rstd = lax.rsqrt(jnp.mean(xv * xv, axis=-1, keepdims=True) + RMS_EPS)
        rest[-1][...] = (xv * rstd * g_ref[...] * (1.0 + sc_ref[...]) + sh_ref[...]).astype(BF16)

    rowspec = pl.BlockSpec((1, d), lambda i: (0, 0))
    ins = [x, g, scale, shift] + ([] if after is None else [after])
    return pl.pallas_call(
        body, name=name, out_shape=jax.ShapeDtypeStruct((s, d), BF16), grid=(s // tr,),
        in_specs=[pl.BlockSpec((tr, d), lambda i: (i, 0)), rowspec, rowspec, rowspec] + [ANY] * (len(ins) - 4),
        out_specs=pl.BlockSpec((tr, d), lambda i: (i, 0)),
        compiler_params=_params(("parallel",)),
    )(*ins)


def _dh_norm_bwd(a1, b1, a2, b2, x, dres, g, scale, nxt, *, name, after=None, b_rows=None):
    s, d = x.shape
    tm = 256
    n_fixed = 8

    def body(a1_ref, b1_ref, a2_ref, b2_ref, x_ref, dr_ref, g_ref, sc_ref, *rest):
        rest = rest[(1 if after is not None else 0):]
        if nxt is not None:
            f_ref, cs_ref, dx_ref, sa_ref, sb_ref, df_ref, dg_ref = rest
        else:
            dx_ref, sa_ref, sb_ref = rest

        @pl.when(pl.program_id(0) == 0)
        def _():
            sa_ref[...] = jnp.zeros_like(sa_ref)
            sb_ref[...] = jnp.zeros_like(sb_ref)
            if nxt is not None:
                dg_ref[...] = jnp.zeros_like(dg_ref)

        dhv = (_dot(a1_ref[...].astype(BF16), b1_ref[...], 1, 0) + _dot(a2_ref[...].astype(BF16), b2_ref[...], 1, 0))
        xv = x_ref[...]
        rstd = lax.rsqrt(jnp.mean(xv * xv, axis=-1, keepdims=True) + RMS_EPS)
        xhat = xv * rstd
        dxhat = dhv * (g_ref[...] * (1.0 + sc_ref[...]))
        mean_term = jnp.mean(dxhat * xhat, axis=-1, keepdims=True)
        dxv = dr_ref[...] + rstd * (dxhat - xhat * mean_term)
        dx_ref[...] = dxv
        sa_ref[...] += jnp.sum(dhv, axis=0, keepdims=True)
        sb_ref[...] += jnp.sum(dhv * xhat, axis=0, keepdims=True)
        if nxt is not None:
            df_ref[...] = (dxv * cs_ref[...]).astype(BF16)
            dg_ref[...] += nxt[2] * jnp.sum(dxv * f_ref[...], axis=0, keepdims=True)

    def a_spec(t):
        return pl.BlockSpec((tm, t.shape[1]), lambda i: (i, 0))

    def b_spec(t, a, which):
        if b_rows is None:
            return pl.BlockSpec((t.shape[0], d), lambda i: (0, 0))
        start = b_rows[which]
        return pl.BlockSpec((pl.Element(a.shape[1]), pl.Element(d)), lambda i: (start, 0))

    rowspec = pl.BlockSpec((1, d), lambda i: (0, 0))
    tile = pl.BlockSpec((tm, d), lambda i: (i, 0))
    ins = [a1, b1, a2, b2, x, dres, g, scale] + ([] if after is None else [after])
    in_specs = [a_spec(a1), b_spec(b1, a1, 0), a_spec(a2), b_spec(b2, a2, 1), tile, tile, rowspec, rowspec]
    in_specs += [ANY] * (len(ins) - n_fixed)
    out_shape = [jax.ShapeDtypeStruct((s, d), F32), jax.ShapeDtypeStruct((1, d), F32), jax.ShapeDtypeStruct((1, d), F32)]
    out_specs = [tile, rowspec, rowspec]
    if nxt is not None:
        ins += [nxt[0], nxt[1]]
        in_specs += [tile, rowspec]
        out_shape += [jax.ShapeDtypeStruct((s, d), BF16), jax.ShapeDtypeStruct((1, d), F32)]
        out_specs += [tile, rowspec]
    out = pl.pallas_call(
        body, name=name, out_shape=out_shape, grid=(s // tm,), in_specs=in_specs, out_specs=out_specs,
        compiler_params=_params(("arbitrary",)),
    )(*ins)
    return out[0], out[1], out[2], (None if nxt is None else (out[3], out[4]))


def _gate_bwd(dxn, f, colscale, coef, name):
    s, d = dxn.shape
    tr = 256

    def body(dx_ref, f_ref, cs_ref, df_ref, dg_ref):
        @pl.when(pl.program_id(0) == 0)
        def _():
            dg_ref[...] = jnp.zeros_like(dg_ref)

        dxv = dx_ref[...]
        df_ref[...] = (dxv * cs_ref[...]).astype(BF16)
        dg_ref[...] += coef * jnp.sum(dxv * f_ref[...], axis=0, keepdims=True)

    rowspec = pl.BlockSpec((1, d), lambda i: (0, 0))
    tile = pl.BlockSpec((tr, d), lambda i: (i, 0))
    return pl.pallas_call(
        body, name=name, out_shape=[jax.ShapeDtypeStruct((s, d), BF16), jax.ShapeDtypeStruct((1, d), F32)],
        grid=(s // tr,), in_specs=[tile, tile, rowspec], out_specs=[tile, rowspec],
        compiler_params=_params(("arbitrary",)),
    )(dxn, f, colscale)


def _ffn_up(h, wg, wu, name, tm=SEQ, tn=256):
    s, d = h.shape
    f = wg.shape[0]

    def body(h_ref, wg_ref, wu_ref, a_ref, u_ref, s_ref):
        hv = h_ref[...]
        a = _dot(hv, wg_ref[...], 1, 1)
        u = _dot(hv, wu_ref[...], 1, 1)
        a_ref[...] = a.astype(BF16)
        u_ref[...] = u.astype(BF16)
        s_ref[...] = (a * _sigmoid(a) * u).astype(BF16)

    tile = pl.BlockSpec((tm, tn), lambda i, j: (i, j))
    wspec = pl.BlockSpec((tn, d), lambda i, j: (j, 0))
    return pl.pallas_call(
        body, name=name,
        out_shape=[jax.ShapeDtypeStruct((s, f), BF16), jax.ShapeDtypeStruct((s, f), BF16), jax.ShapeDtypeStruct((s, f), BF16)],
        grid=(s // tm, f // tn), in_specs=[pl.BlockSpec((tm, d), lambda i, j: (i, 0)), wspec, wspec],
        out_specs=[tile, tile, tile], compiler_params=_params(("parallel", "parallel")),
    )(h, wg, wu)


def _ffn_bwd_ds(df, wd, a, u, name, tm=SEQ, tn=256):
    s, d = df.shape
    f = wd.shape[0]

    def body(df_ref, wd_ref, a_ref, u_ref, da_ref, du_ref):
        ds = _dot(df_ref[...], wd_ref[...], 1, 1)
        av = a_ref[...].astype(F32)
        sg = _sigmoid(av)
        da_ref[...] = (ds * u_ref[...].astype(F32) * (sg * (1.0 + av * (1.0 - sg)))).astype(BF16)
        du_ref[...] = (ds * (av * sg)).astype(BF16)

    tile = pl.BlockSpec((tm, tn), lambda i, j: (i, j))
    return pl.pallas_call(
        body, name=name, out_shape=[jax.ShapeDtypeStruct((s, f), BF16), jax.ShapeDtypeStruct((s, f), BF16)],
        grid=(s // tm, f // tn),
        in_specs=[pl.BlockSpec((tm, d), lambda i, j: (i, 0)), pl.BlockSpec((tn, d), lambda i, j: (j, 0)), tile, tile],
        out_specs=[tile, tile], compiler_params=_params(("parallel", "parallel")),
    )(df, wd, a, u)


def _merge_fwd(o_sb, o_dil, o_swa, gates, wb_sb, wb_dil, wb_swa, name):
    s, d = SEQ, D_MODEL
    tm = 256

    def body(osb_ref, odl_ref, osw_ref, g_ref, wsb_ref, wdl_ref, wsw_ref, m_ref, tsb_ref, tdl_ref, tsw_ref):
        for h in range(osb_ref.shape[0]):
            tsb_ref[:, h * HEAD_DIM:(h + 1) * HEAD_DIM] = osb_ref[h].astype(BF16)
        for h in range(osw_ref.shape[0]):
            tsw_ref[:, h * HEAD_DIM:(h + 1) * HEAD_DIM] = osw_ref[h].astype(BF16)
        tdl_ref[...] = odl_ref[...].astype(BF16)
        acc = _sigmoid(g_ref[:, 0:d]) * _dot(tsb_ref[...], wsb_ref[...], 1, 0)
        acc += _sigmoid(g_ref[:, d:2 * d]) * _dot(tdl_ref[...], wdl_ref[...], 1, 0)
        acc += _sigmoid(g_ref[:, 2 * d:3 * d]) * _dot(tsw_ref[...], wsw_ref[...], 1, 0)
        m_ref[...] = acc.astype(BF16)

    def rows(w):
        return pl.BlockSpec((tm, w), lambda i: (i, 0))

    def heads(n):
        return pl.BlockSpec((n, tm, HEAD_DIM), lambda i: (0, i, 0))

    def whole(w):
        return pl.BlockSpec((w, d), lambda i: (0, 0))

    return pl.pallas_call(
        body, name=name, out_shape=[jax.ShapeDtypeStruct((s, w), BF16) for w in (d, 256, 128, 384)], grid=(s // tm,),
        in_specs=[heads(H_SB), rows(128), heads(H_SWA_Q), rows(3 * d), whole(256), whole(128), whole(384)],
        out_specs=[rows(d), rows(256), rows(128), rows(384)], compiler_params=_params(("parallel",)),
    )(o_sb, o_dil, o_swa, gates, wb_sb, wb_dil, wb_swa)


def _merge_bwd(dmerged, t_sb, t_dil, t_swa, gates, wb_sb, wb_dil, wb_swa, name):
    s, d = SEQ, D_MODEL
    tm = 256

    def body(dm_ref, tsb_ref, tdl_ref, tsw_ref, g_ref, wsb_ref, wdl_ref, wsw_ref,
             dg_ref, dosb_ref, dodl_ref, dosw_ref, dbsb_ref, dbdl_ref, dbsw_ref):
        dm = dm_ref[...]
        for idx, (t_ref, w_ref, do_ref, db_ref) in enumerate((
                (tsb_ref, wsb_ref, dosb_ref, dbsb_ref), (tdl_ref, wdl_ref, dodl_ref, dbdl_ref),
                (tsw_ref, wsw_ref, dosw_ref, dbsw_ref))):
            w = w_ref[...]
            br = _dot(t_ref[...], w, 1, 0)
            sg = _sigmoid(g_ref[:, idx * d:(idx + 1) * d])
            dbr = (dm * sg).astype(BF16)
            dg_ref[:, idx * d:(idx + 1) * d] = (dm * br * (sg * (1.0 - sg))).astype(BF16)
            db_ref[...] = dbr
            do = _dot(dbr, w, 1, 1)
            if len(do_ref.shape) == 2:
                do_ref[...] = do
            else:
                for h in range(do_ref.shape[0]):
                    do_ref[h] = do[:, h * HEAD_DIM:(h + 1) * HEAD_DIM]

    def rows(w):
        return pl.BlockSpec((tm, w), lambda i: (i, 0))

    def heads(n):
        return pl.BlockSpec((n, tm, HEAD_DIM), lambda i: (0, i, 0))

    def whole(w):
        return pl.BlockSpec((w, d), lambda i: (0, 0))

    def shp(w, dt):
        return jax.ShapeDtypeStruct((s, w), dt)

    def hshp(n):
        return jax.ShapeDtypeStruct((n, s, HEAD_DIM), F32)

    return pl.pallas_call(
        body, name=name,
        out_shape=[shp(3 * d, BF16), hshp(H_SB), shp(128, F32), hshp(H_SWA_Q), shp(d, BF16), shp(d, BF16), shp(d, BF16)],
        grid=(s // tm,),
        in_specs=[rows(d), rows(256), rows(128), rows(384), rows(3 * d), whole(256), whole(128), whole(384)],
        out_specs=[rows(3 * d), heads(H_SB), rows(128), heads(H_SWA_Q), rows(d), rows(d), rows(d)],
        compiler_params=_params(("parallel",)),
    )(dmerged, t_sb, t_dil, t_swa, gates, wb_sb, wb_dil, wb_swa)


def _final_loss(x, target, g, name):
    s, d = x.shape
    tr = 256

    def body(x_ref, t_ref, g_ref, loss_ref, dx_ref, dg_ref):
        @pl.when(pl.program_id(0) == 0)
        def _():
            loss_ref[...] = jnp.zeros_like(loss_ref)
            dg_ref[...] = jnp.zeros_like(dg_ref)

        xv = x_ref[...]
        gv = g_ref[...]
        rstd = lax.rsqrt(jnp.mean(xv * xv, axis=-1, keepdims=True) + RMS_EPS)
        xhat = xv * rstd
        err = xhat * gv - t_ref[...]
        loss_ref[...] += 0.5 * jnp.sum(jnp.mean(err * err, axis=-1, keepdims=True))
        dy = err * (1.0 / d)
        dxhat = dy * gv
        mean_term = jnp.mean(dxhat * xhat, axis=-1, keepdims=True)
        dx_ref[...] = rstd * (dxhat - xhat * mean_term)
        dg_ref[...] += jnp.sum(dy * xhat, axis=0, keepdims=True)

    rowspec = pl.BlockSpec((1, d), lambda i: (0, 0))
    tile = pl.BlockSpec((tr, d), lambda i: (i, 0))
    return pl.pallas_call(
        body, name=name,
        out_shape=[jax.ShapeDtypeStruct((1, LANES), F32), jax.ShapeDtypeStruct((s, d), F32), jax.ShapeDtypeStruct((1, d), F32)],
        grid=(s // tr,), in_specs=[tile, tile, rowspec],
        out_specs=[pl.BlockSpec((1, LANES), lambda i: (0, 0)), tile, rowspec],
        compiler_params=_params(("arbitrary",)),
    )(x, target, g)


def _adamw(w, g, m, v, name, after=None):
    shape = w.shape
    cols = shape[-1]
    rows = int(np.prod(shape[:-1])) if len(shape) > 1 else 1
    tr = rows
    for cand in (1024, 512, 256, 128, 64, 32, 16, 8):
        if rows % cand == 0 and rows > cand and cand * cols * 4 <= (1 << 21):
            tr = cand
            break

    def body(w_ref, g_ref, m_ref, v_ref, *rest):
        d_ref, nm_ref, nv_ref = rest[-3:]
        d_ref[...], nm_ref[...], nv_ref[...] = _adam_update(w_ref[...], g_ref[...], m_ref[...], v_ref[...])

    tile = pl.BlockSpec((tr, cols), lambda i: (i, 0))
    flat = [t.reshape(rows, cols) for t in (w, g, m, v)] + ([] if after is None else [after])
    out = pl.pallas_call(
        body, name=name, out_shape=[jax.ShapeDtypeStruct((rows, cols), F32)] * 3, grid=(rows // tr,),
        in_specs=[tile] * 4 + [ANY] * (len(flat) - 4), out_specs=[tile] * 3, compiler_params=_params(("parallel",)),
    )(*flat)
    return tuple(t.reshape(shape) for t in out)


def _adam_update(w, gv, m, v):
    nm = ADAM_B1 * m + (1.0 - ADAM_B1) * gv
    nv = ADAM_B2 * v + (1.0 - ADAM_B2) * (gv * gv)
    m_hat = nm / (1.0 - ADAM_B1 ** ADAM_STEP)
    v_hat = nv / (1.0 - ADAM_B2 ** ADAM_STEP)
    return -ADAM_LR * (m_hat / (jnp.sqrt(v_hat) + ADAM_EPS) + ADAM_WD * w), nm, nv


def _reduce_adamw(groups, w, m, v, row0, prev, name, after=None):
    n, r, cdim = groups[0].shape
    rows = w.shape[0]
    tr = _row_tile(r, max(16, (1 << 22) // (n * cdim * groups[0].dtype.itemsize)))
    steps = r // tr
    ng = len(groups)

    def body(*refs):
        w_ref, m_ref, v_ref = refs[ng:ng + 3]
        g_out, d_out, m_out, v_out = refs[-4:]
        gg = pl.program_id(0)
        for gi in range(ng):
            @pl.when(gg == gi)
            def _(gi=gi):
                acc = refs[gi][0].astype(F32)
                for k in range(1, n):
                    acc = acc + refs[gi][k].astype(F32)
                g_out[...] = acc
                d_out[...], m_out[...], v_out[...] = _adam_update(w_ref[...], acc, m_ref[...], v_ref[...])

    def part_spec(gi):
        return pl.BlockSpec((n, tr, cdim), lambda gg, i: (0, jnp.where(gg == gi, i, 0), 0))

    tile = pl.BlockSpec((tr, cdim), lambda gg, i: (row0 // tr + gg * steps + i, 0))
    extra = ([] if prev is None else list(prev)) + ([] if after is None else [after])
    return pl.pallas_call(
        body, name=name, out_shape=[jax.ShapeDtypeStruct((rows, cdim), F32)] * 4, grid=(ng, steps),
        in_specs=[part_spec(gi) for gi in range(ng)] + [tile] * 3 + [ANY] * len(extra), out_specs=[tile] * 4,
        input_output_aliases={} if prev is None else {ng + 3 + k: k for k in range(4)},
        compiler_params=_params(("parallel", "parallel")),
    )(*groups, w, m, v, *extra)


def _ada_fwd(c_all, w, name):
    n = w.shape[1]

    def body(c_ref, w_ref, o_ref):
        cv = c_ref[...]
        o_ref[...] = jnp.dot(cv * _sigmoid(cv), w_ref[...], preferred_element_type=F32, precision=lax.Precision.HIGHEST)

    return pl.pallas_call(body, name=name, out_shape=jax.ShapeDtypeStruct((N_DEV, n), F32), compiler_params=_params())(c_all, w)


def _ada_bwd(c_all_t, dmod, name):
    n = dmod.shape[1]

    def body(c_ref, d_ref, o_ref):
        cv = c_ref[...]
        o_ref[...] = jnp.dot(cv * _sigmoid(cv), d_ref[...], preferred_element_type=F32, precision=lax.Precision.HIGHEST)

    return pl.pallas_call(body, name=name, out_shape=jax.ShapeDtypeStruct((D_MODEL, n), F32), compiler_params=_params())(c_all_t, dmod)


def _bucket_tables():
    rel = np.arange(BLK)[:, None] + BLK - np.arange(2 * BLK)[None, :]
    max_exact = N_BUCKETS // 2

    def bucket(n):
        nf = np.maximum(n, 1).astype(np.float32)
        large = max_exact + (np.log(nf / np.float32(max_exact)) / np.float32(math.log(MAX_REL_DIST / max_exact))
                             * np.float32(N_BUCKETS - max_exact)).astype(np.int32)
        return np.where(n < max_exact, n, np.minimum(large, N_BUCKETS - 1))

    tabs = []
    for dil, max_dist in ((1, 128), (4, 128), (16, 128), (1, SWA_WINDOW - 1)):
        in_band = (rel >= 0) & (rel <= max_dist)
        tabs.append(np.where(in_band, bucket(np.maximum(rel, 0) * dil), -1))
    return np.stack(tabs).astype(np.int32)


N_SOFT = H_DIL + H_SWA_Q


def _table_of_head(h):
    return jnp.minimum(h // 2, 3)


def _bias_build(rel_bias, tables, name):
    def body(rel_ref, t_ref, o_ref):
        h = pl.program_id(0)
        tb = t_ref[0]
        out = jnp.full((BLK, 2 * BLK), NEG, F32)
        for b in range(N_BUCKETS):
            out = jnp.where(tb == b, rel_ref[b, h], out)
        o_ref[0] = out

    return pl.pallas_call(
        body, name=name, out_shape=jax.ShapeDtypeStruct((N_SOFT, BLK, 2 * BLK), F32), grid=(N_SOFT,),
        in_specs=[pl.BlockSpec(memory_space=pltpu.SMEM),
                  pl.BlockSpec((1, BLK, 2 * BLK), lambda h: (_table_of_head(h), 0, 0))],
        out_specs=pl.BlockSpec((1, BLK, 2 * BLK), lambda h: (h, 0, 0)),
        compiler_params=_params(("parallel",)),
    )(rel_bias, tables)


def _bias_grad(dbias, tables, name):
    def body(d_ref, t_ref, o_ref):
        tb = t_ref[0]
        dv = d_ref[0]
        lane = lax.broadcasted_iota(jnp.int32, (1, LANES), 1)
        out = jnp.zeros((1, LANES), F32)
        for b in range(N_BUCKETS):
            out = jnp.where(lane == b, jnp.sum(jnp.where(tb == b, dv, 0.0)), out)
        o_ref[0] = out

    return pl.pallas_call(
        body, name=name, out_shape=jax.ShapeDtypeStruct((N_SOFT, 1, LANES), F32), grid=(N_SOFT,),
        in_specs=[pl.BlockSpec((1, BLK, 2 * BLK), lambda h: (h, 0, 0)),
                  pl.BlockSpec((1, BLK, 2 * BLK), lambda h: (_table_of_head(h), 0, 0))],
        out_specs=pl.BlockSpec((1, 1, LANES), lambda h: (h, 0, 0)),
        compiler_params=_params(("parallel",)),
    )(dbias, tables)


def _band_layout(g, bias_div):
    assert g == 1 or bias_div == 1
    return bias_div if g == 1 else 1


def _band_specs(length, g, bias_div, offs):
    ns = _band_layout(g, bias_div)

    def seqs(off, div=1):
        return pl.BlockSpec((ns, length, HEAD_DIM), lambda s: (off // ns + s // div, 0, 0))

    xspecs = [seqs(offs[0]), seqs(offs[1], g), seqs(offs[2], g)]
    bspec = pl.BlockSpec((1, BLK, 2 * BLK), lambda s: (s, 0, 0))
    sspec = pl.BlockSpec((ns, 1, LANES), lambda s: (s, 0, 0))
    colspec = pl.BlockSpec((ns, length, 1), lambda s: (s, 0, 0))
    return xspecs, seqs(0), seqs(0, g), bspec, sspec, colspec


def _band_sweep(length, ns, one):
    nblk = length // BLK
    for qq in range(ns):
        if ns * nblk <= 16:
            for i in range(nblk):
                one(qq, i * BLK, max(i - 1, 0) * BLK, i == 0)
        else:
            def step(i, carry, qq=qq):
                one(qq, pl.multiple_of(i * BLK, BLK), pl.multiple_of(jnp.maximum(i - 1, 0) * BLK, BLK), i == 0)
                return carry

            lax.fori_loop(0, nblk, step, 0, unroll=2)


def _band_scores(q_ref, k_ref, b_ref, qq, kq, bq, cur, prv, first):
    qv = q_ref[qq, pl.ds(cur, BLK), :]
    bv = b_ref[bq]
    if first is True:
        sp = jnp.full((BLK, BLK), NEG, F32)
    else:
        sp = _dot(qv, k_ref[kq, pl.ds(prv, BLK), :], 1, 1) + bv[:, :BLK]
        sp = sp if first is False else jnp.where(first, NEG, sp)
    sc = _dot(qv, k_ref[kq, pl.ds(cur, BLK), :], 1, 1) + bv[:, BLK:]
    return qv, sp, sc


def _band_fwd(x, bias, sink, *, nq, offs, g, bias_div, has_sink, name):
    length = x.shape[1]
    ns = _band_layout(g, bias_div)

    def body(q_ref, k_ref, v_ref, b_ref, s_ref, o_ref, lse_ref):
        def one(qq, cur, prv, first):
            kq, bq = qq, 0
            _, sp, sc = _band_scores(q_ref, k_ref, b_ref, qq, kq, bq, cur, prv, first)
            m = jnp.maximum(jnp.max(sp, axis=1, keepdims=True), jnp.max(sc, axis=1, keepdims=True))
            if has_sink:
                sk = s_ref[qq][:, :1]
                m = jnp.maximum(m, sk)
            pp, pc = jnp.exp(sp - m), jnp.exp(sc - m)
            den = jnp.sum(pp, axis=1, keepdims=True) + jnp.sum(pc, axis=1, keepdims=True)
            if has_sink:
                den = den + jnp.exp(sk - m)
            acc = (_dot(pp.astype(BF16), v_ref[kq, pl.ds(prv, BLK), :], 1, 0)
                   + _dot(pc.astype(BF16), v_ref[kq, pl.ds(cur, BLK), :], 1, 0))
            o_ref[qq, pl.ds(cur, BLK), :] = acc / den
            lse_ref[qq, pl.ds(cur, BLK), :] = m + jnp.log(den)

        _band_sweep(length, ns, one)

    xspecs, qspec, _, bspec, sspec, colspec = _band_specs(length, g, bias_div, offs)
    return pl.pallas_call(
        body, name=name,
        out_shape=[jax.ShapeDtypeStruct((nq, length, HEAD_DIM), F32), jax.ShapeDtypeStruct((nq, length, 1), F32)],
        grid=(nq // ns,), in_specs=xspecs + [bspec, sspec],
        out_specs=[qspec, colspec], compiler_params=_params(("parallel",)),
    )(x, x, x, bias, sink)


def _band_bwd(x, bias, sink, o, lse, do, dlse, *, nq, offs, g, bias_div, has_sink, name):
    length = x.shape[1]
    ns = _band_layout(g, bias_div)
    nk, nbias = nq // g, nq // bias_div

    def body(q_ref, k_ref, v_ref, b_ref, s_ref, o_ref, lse_ref, do_ref, dlse_ref,
             dq_ref, dk_ref, dv_ref, db_ref, dsk_ref, dkp_ref, dvp_ref):
        for ref in (db_ref, dsk_ref, dkp_ref, dvp_ref):
            ref[...] = jnp.zeros_like(ref)

        @pl.when(pl.program_id(0) % g == 0)
        def _():
            dk_ref[...] = jnp.zeros_like(dk_ref)
            dv_ref[...] = jnp.zeros_like(dv_ref)

        def one(qq, cur, prv, first):
            kq, bq = qq, 0
            qv, sp, sc = _band_scores(q_ref, k_ref, b_ref, qq, kq, bq, cur, prv, first)
            rows, prow = pl.ds(cur, BLK), pl.ds(prv, BLK)
            lse_v = lse_ref[qq, rows, :]
            pp, pc = jnp.exp(sp - lse_v), jnp.exp(sc - lse_v)
            dov = do_ref[qq, rows, :]
            dob = dov.astype(BF16)
            coef = dlse_ref[qq, rows, :] - jnp.sum(dov * o_ref[qq, rows, :], axis=1, keepdims=True)
            dsp = pp * (_dot(dob, v_ref[kq, prow, :], 1, 1) + coef)
            dsc = pc * (_dot(dob, v_ref[kq, rows, :], 1, 1) + coef)
            dspb, dscb = dsp.astype(BF16), dsc.astype(BF16)
            dq_ref[qq, rows, :] = ((_dot(dspb, k_ref[kq, prow, :], 1, 0) + _dot(dscb, k_ref[kq, rows, :], 1, 0))
                                   * (HEAD_DIM ** -0.5))
            dk_ref[kq, rows, :] += _dot(dscb, qv, 0, 0)
            dkp_ref[kq, prow, :] += _dot(dspb, qv, 0, 0)
            dv_ref[kq, rows, :] += _dot(pc.astype(BF16), dob, 0, 0)
            dvp_ref[kq, prow, :] += _dot(pp.astype(BF16), dob, 0, 0)
            db_ref[bq, :, :BLK] += dsp
            db_ref[bq, :, BLK:] += dsc
            if has_sink:
                dsk_ref[qq] += jnp.sum(jnp.exp(s_ref[qq][:, :1] - lse_v) * coef)

        _band_sweep(length, ns, one)
        dk_ref[...] += dkp_ref[...]
        dv_ref[...] += dvp_ref[...]

    xspecs, qspec, kvspec, bspec, sspec, colspec = _band_specs(length, g, bias_div, offs)
    return pl.pallas_call(
        body, name=name,
        out_shape=[jax.ShapeDtypeStruct((nq, length, HEAD_DIM), F32), jax.ShapeDtypeStruct((nk, length, HEAD_DIM), F32),
                   jax.ShapeDtypeStruct((nk, length, HEAD_DIM), F32), jax.ShapeDtypeStruct((nbias, BLK, 2 * BLK), F32),
                   jax.ShapeDtypeStruct((nq, 1, LANES), F32)],
        grid=(nq // ns,),
        in_specs=xspecs + [bspec, sspec, qspec, colspec, qspec, colspec],
        out_specs=[qspec, kvspec, kvspec, bspec, sspec],
        scratch_shapes=[pltpu.VMEM((ns, length, HEAD_DIM), F32), pltpu.VMEM((ns, length, HEAD_DIM), F32)],
        compiler_params=_params(("arbitrary",)),
    )(x, x, x, bias, sink, o, lse, do, dlse)


TOK_TILE = 512


def _dil_merge(outs, lses, dout, name):
    tr = TOK_TILE
    dils = [d for _, d in DIL_PATTERNS]
    n = len(dils)
    o4 = [o.reshape(2, d, SEQ // d, HEAD_DIM) for o, d in zip(outs, dils)]
    l4 = [l.reshape(2, d, SEQ // d, 1) for l, d in zip(lses, dils)]
    o_specs = [pl.BlockSpec((2, d, tr // d, HEAD_DIM), lambda i: (0, 0, i, 0)) for d in dils]
    l_specs = [pl.BlockSpec((2, d, tr // d, 1), lambda i: (0, 0, i, 0)) for d in dils]
    tok = pl.BlockSpec((tr, 2 * HEAD_DIM), lambda i: (i, 0))
    scratch = ([pltpu.VMEM((tr, 2 * HEAD_DIM), F32) for _ in dils] + [pltpu.VMEM((tr, 1), F32) for _ in range(2 * n)]
               + [pltpu.VMEM((tr // d, 2 * HEAD_DIM), F32) for d in dils])

    def to_tokens(o_ref, l_ref, d, pair, cols, stage):
        for r in range(d):
            rows = pl.ds(r, tr // d, stride=d) if d > 1 else slice(None)
            stage[:, :HEAD_DIM] = o_ref[0, r]
            stage[:, HEAD_DIM:] = o_ref[1, r]
            pair[rows, :] = stage[...]
            for h in range(2):
                cols[h][rows, :] = l_ref[h, r]
        return pair[...], [cols[0][...], cols[1][...]]

    def weights(ls):
        left = lax.broadcasted_iota(jnp.int32, (tr, 2 * HEAD_DIM), 1) < HEAD_DIM
        per_head = []
        for h in range(2):
            m = ls[0][h]
            for g in range(1, n):
                m = jnp.maximum(m, ls[g][h])
            es = [jnp.exp(ls[g][h] - m) for g in range(n)]
            den = es[0]
            for e in es[1:]:
                den = den + e
            per_head.append([e / den for e in es])
        return per_head, [jnp.where(left, per_head[0][g], per_head[1][g]) for g in range(n)], left

    def load(refs):
        pairs, cols, stages = refs[:n], refs[n:3 * n], refs[3 * n:]
        return pairs, [cols[2 * g:2 * g + 2] for g in range(n)], stages

    if dout is None:
        def body(*refs):
            pairs, cols, stages = load(refs[2 * n + 1:])
            toks = [to_tokens(refs[g], refs[n + g], dils[g], pairs[g], cols[g], stages[g]) for g in range(n)]
            _, alphas, _ = weights([t[1] for t in toks])
            acc = alphas[0] * toks[0][0]
            for g in range(1, n):
                acc = acc + alphas[g] * toks[g][0]
            refs[2 * n][...] = acc

        return pl.pallas_call(
            body, name=name, out_shape=jax.ShapeDtypeStruct((SEQ, 2 * HEAD_DIM), F32), grid=(SEQ // tr,),
            in_specs=o_specs + l_specs, out_specs=tok, scratch_shapes=scratch, compiler_params=_params(("parallel",)),
        )(*o4, *l4)

    def body(*refs):
        do_refs, dl_refs = refs[2 * n + 1:3 * n + 1], refs[3 * n + 1:4 * n + 1]
        pairs, cols, stages = load(refs[4 * n + 1:])
        toks = [to_tokens(refs[g], refs[n + g], dils[g], pairs[g], cols[g], stages[g]) for g in range(n)]
        per_head, alphas, left = weights([t[1] for t in toks])
        dov = refs[2 * n][...]
        das = []
        for g in range(n):
            prod = dov * toks[g][0]
            das.append([jnp.sum(jnp.where(left, prod, 0.0), axis=1, keepdims=True),
                        jnp.sum(jnp.where(left, 0.0, prod), axis=1, keepdims=True)])
        dbar = [sum(per_head[h][g] * das[g][h] for g in range(n)) for h in range(2)]
        for g, d in enumerate(dils):
            pairs[g][...] = alphas[g] * dov
            for h in range(2):
                cols[g][h][...] = per_head[h][g] * (das[g][h] - dbar[h])
            for r in range(d):
                rows = pl.ds(r, tr // d, stride=d) if d > 1 else slice(None)
                v = pairs[g][rows, :]
                for h in range(2):
                    do_refs[g][h, r] = v[:, h * HEAD_DIM:(h + 1) * HEAD_DIM]
                    dl_refs[g][h, r] = cols[g][h][rows, :]

    out = pl.pallas_call(
        body, name=name,
        out_shape=[jax.ShapeDtypeStruct(o.shape, F32) for o in o4] + [jax.ShapeDtypeStruct(l.shape, F32) for l in l4],
        grid=(SEQ // tr,), in_specs=o_specs + l_specs + [tok], out_specs=o_specs + l_specs, scratch_shapes=scratch,
        compiler_params=_params(("parallel",)),
    )(*o4, *l4, dout)
    return [t.reshape(s.shape) for t, s in zip(out, list(outs) + list(lses))]


def _tri(cmp):
    r = lax.broadcasted_iota(jnp.int32, (SB_TILE, SB_TILE), 0)
    c = lax.broadcasted_iota(jnp.int32, (SB_TILE, SB_TILE), 1)
    return cmp(r, c).astype(BF16)


def _cum(x, tri, terms):
    acc, rest = None, x
    for _ in range(terms):
        part = rest.astype(BF16)
        rest = rest - part.astype(F32)
        d = _dot(part, tri, 1, 0)
        acc = d if acc is None else acc + d
    return acc


def _sb_logits(q, ks, diagonal):
    t = SB_TILE
    z = _dot(q, ks, 1, 1)
    e = jnp.exp(-jnp.abs(z))
    lf = -(jnp.maximum(z, 0.0) + jnp.log(1.0 + e))
    if not diagonal:
        return z, e, lf, None
    mask = lax.broadcasted_iota(jnp.int32, (t, t), 1) < lax.broadcasted_iota(jnp.int32, (t, t), 0)
    return z, e, jnp.where(mask, lf, 0.0), mask


def _sb_specs(h, s):
    t = SB_TILE
    tile = pl.BlockSpec((h, t, HEAD_DIM), lambda i: (0, i, 0))
    keys = pl.BlockSpec((h, s, HEAD_DIM), lambda i: (1, 0, 0))
    values = pl.BlockSpec((h, s, HEAD_DIM), lambda i: (2, 0, 0))
    return tile, keys, values, pl.BlockSpec((h, t, 1), lambda i: (0, i, 0))


def _sb_fwd(x, name):
    h, s = x.shape[0] // 3, x.shape[1]
    t = SB_TILE

    def body(q_ref, k_ref, v_ref, o_ref, tot_ref):
        i = pl.program_id(0)
        after = _tri(lambda r, c: r > c)

        def tile(j, carry, diagonal):
            rows = pl.ds(pl.multiple_of(j * t, t), t)
            out = []
            for hh, (right, acc) in enumerate(carry):
                z, _, lf, mask = _sb_logits(q_ref[hh], k_ref[hh, rows, :], diagonal)
                w = jnp.exp(z + lf + (right + _cum(lf, after, 2)))
                w = w if mask is None else jnp.where(mask, w, 0.0)
                out.append((right + jnp.sum(lf, axis=1, keepdims=True), acc + _dot(w.astype(BF16), v_ref[hh, rows, :], 1, 0)))
            return tuple(out)

        carry = tile(i, tuple((jnp.zeros((t, 1), F32), jnp.zeros((t, HEAD_DIM), F32)) for _ in range(h)), True)
        carry = lax.fori_loop(0, i, lambda jj, c: tile(i - 1 - jj, c, False), carry)
        for hh, (right, acc) in enumerate(carry):
            o_ref[hh] = acc
            tot_ref[hh] = right

    tile_spec, keys, values, col = _sb_specs(h, s)
    return pl.pallas_call(
        body, name=name, out_shape=[jax.ShapeDtypeStruct((h, s, HEAD_DIM), F32), jax.ShapeDtypeStruct((h, s, 1), F32)],
        grid=(s // t,), in_specs=[tile_spec, keys, values], out_specs=[tile_spec, col],
        compiler_params=_params(("parallel",)),
    )(x, x, x)


def _sb_bwd(x, tot, do, name):
    h, s = x.shape[0] // 3, x.shape[1]
    t = SB_TILE

    def body(q_ref, k_ref, v_ref, tot_ref, do_ref, dq_ref, dk_ref, dv_ref):
        i = pl.program_id(0)

        @pl.when(i == 0)
        def _():
            dk_ref[...] = jnp.zeros_like(dk_ref)
            dv_ref[...] = jnp.zeros_like(dv_ref)

        upto = _tri(lambda r, c: r <= c)
        before = _tri(lambda r, c: r < c)

        def tile(j, carry, diagonal):
            rows = pl.ds(pl.multiple_of(j * t, t), t)
            out = []
            for hh, (left, cleft, dq) in enumerate(carry):
                qv, ks, dob = q_ref[hh], k_ref[hh, rows, :], do_ref[hh].astype(BF16)
                z, e, lf, mask = _sb_logits(qv, ks, diagonal)
                between = tot_ref[hh] - (left + _cum(lf, upto, 2))
                w = jnp.exp(z + lf + between)
                w = w if mask is None else jnp.where(mask, w, 0.0)
                dlog = w * _dot(dob, v_ref[hh, rows, :], 1, 1)
                cfail = cleft + _cum(dlog, before, 2)
                sig = jnp.where(z >= 0.0, 1.0, e) / (1.0 + e)
                dz = dlog * (1.0 - sig) - sig * cfail
                dz = (dz if mask is None else jnp.where(mask, dz, 0.0)).astype(BF16)
                dk_ref[hh, rows, :] += _dot(dz, qv, 0, 0)
                dv_ref[hh, rows, :] += _dot(w.astype(BF16), dob, 0, 0)
                out.append((left + jnp.sum(lf, axis=1, keepdims=True), cleft + jnp.sum(dlog, axis=1, keepdims=True),
                            dq + _dot(dz, ks, 1, 0)))
            return tuple(out)

        zero = jnp.zeros((t, 1), F32)
        carry = lax.fori_loop(0, i, lambda j, c: tile(j, c, False),
                              tuple((zero, zero, jnp.zeros((t, HEAD_DIM), F32)) for _ in range(h)))
        for hh, (_, _, dq) in enumerate(tile(i, carry, True)):
            dq_ref[hh] = dq * (HEAD_DIM ** -0.5)

    tile_spec, keys, values, col = _sb_specs(h, s)
    full = pl.BlockSpec((h, s, HEAD_DIM), lambda i: (0, 0, 0))
    shp = jax.ShapeDtypeStruct((h, s, HEAD_DIM), F32)
    return pl.pallas_call(
        body, name=name, out_shape=[shp, shp, shp], grid=(s // t,),
        in_specs=[tile_spec, keys, values, col, tile_spec],
        out_specs=[tile_spec, full, full], compiler_params=_params(("arbitrary",)),
    )(x, x, x, tot, do)


COL_SB, COL_DIL, COL_SWA = 0, 3 * H_SB * HEAD_DIM, 3 * H_SB * HEAD_DIM + 3 * H_DIL * HEAD_DIM
N_SWA = H_SWA_Q + 2 * H_SWA_KV


def _dil_col(t, g):
    return COL_DIL + t * H_DIL * HEAD_DIM + g * 2 * HEAD_DIM


def _split_heads(qkv, name):
    tr = TOK_TILE
    scale = HEAD_DIM ** -0.5
    dils = [d for _, d in DIL_PATTERNS]

    def body(x_ref, sb_ref, d0_ref, d1_ref, d2_ref, swa_ref, pair):
        def head(col, scaled):
            v = x_ref[:, col:col + HEAD_DIM]
            return (v * scale if scaled else v).astype(BF16)

        for hh in range(3 * H_SB):
            sb_ref[hh] = head(COL_SB + hh * HEAD_DIM, hh < H_SB)
        for hh in range(N_SWA):
            swa_ref[hh] = head(COL_SWA + hh * HEAD_DIM, hh < H_SWA_Q)
        for t in range(3):
            for g, (d, out_ref) in enumerate(zip(dils, (d0_ref, d1_ref, d2_ref))):
                col = _dil_col(t, g)
                if d == 1:
                    for h in range(2):
                        out_ref[t * 2 + h] = head(col + h * HEAD_DIM, t == 0)
                    continue
                pair[...] = x_ref[:, col:col + 2 * HEAD_DIM]
                for r in range(d):
                    v = pair[pl.ds(r, tr // d, stride=d), :]
                    v = v * scale if t == 0 else v
                    for h in range(2):
                        out_ref[t * 2 * d + h * d + r] = v[:, h * HEAD_DIM:(h + 1) * HEAD_DIM].astype(BF16)

    def heads(n, length):
        return jax.ShapeDtypeStruct((n, length, HEAD_DIM), BF16)

    def spec(n, rows):
        return pl.BlockSpec((n, rows, HEAD_DIM), lambda i: (0, i, 0))

    return pl.pallas_call(
        body, name=name,
        out_shape=[heads(3 * H_SB, SEQ)] + [heads(6 * d, SEQ // d) for d in dils] + [heads(N_SWA, SEQ)],
        grid=(SEQ // tr,), in_specs=[pl.BlockSpec((tr, D_QKV), lambda i: (i, 0))],
        out_specs=[spec(3 * H_SB, tr)] + [spec(6 * d, tr // d) for d in dils] + [spec(N_SWA, tr)],
        scratch_shapes=[pltpu.VMEM((tr, 2 * HEAD_DIM), F32)], compiler_params=_params(("parallel",)),
    )(qkv)


def _join_heads(sb, dil, swa, name):
    tr = TOK_TILE
    dils = [d for _, d in DIL_PATTERNS]

    def body(*refs):
        sb_refs, dil_refs, swa_refs = refs[:3], [refs[3 + 3 * g:6 + 3 * g] for g in range(3)], refs[12:15]
        o_ref, pair, stages = refs[15], refs[16], refs[17:]

        def put(col, v):
            o_ref[:, col:col + v.shape[1]] = v.astype(BF16)

        for t in range(3):
            for h in range(H_SB):
                put(COL_SB + (t * H_SB + h) * HEAD_DIM, sb_refs[t][h])
        col = COL_SWA
        for ref in swa_refs:
            for h in range(ref.shape[0]):
                put(col, ref[h])
                col += HEAD_DIM
        for t in range(3):
            for g, d in enumerate(dils):
                ref, col = dil_refs[g][t], _dil_col(t, g)
                if d == 1:
                    for h in range(2):
                        put(col + h * HEAD_DIM, ref[h])
                    continue
                stage = stages[g - 1]
                for r in range(d):
                    stage[:, :HEAD_DIM] = ref[r]
                    stage[:, HEAD_DIM:] = ref[d + r]
                    pair[pl.ds(r, tr // d, stride=d), :] = stage[...]
                put(col, pair[...])

    def spec(n, rows):
        return pl.BlockSpec((n, rows, HEAD_DIM), lambda i: (0, i, 0))

    ins = list(sb) + [t for g in range(3) for t in dil[g]] + list(swa)
    in_specs = ([spec(H_SB, tr)] * 3 + [spec(2 * d, tr // d) for d in dils for _ in range(3)]
                + [spec(H_SWA_Q, tr), spec(H_SWA_KV, tr), spec(H_SWA_KV, tr)])
    return pl.pallas_call(
        body, name=name, out_shape=jax.ShapeDtypeStruct((SEQ, D_QKV), BF16), grid=(SEQ // tr,), in_specs=in_specs,
        out_specs=pl.BlockSpec((tr, D_QKV), lambda i: (i, 0)),
        scratch_shapes=[pltpu.VMEM((tr, 2 * HEAD_DIM), F32)] + [pltpu.VMEM((tr // d, 2 * HEAD_DIM), F32) for d in dils[1:]],
        compiler_params=_params(("parallel",)),
    )(*ins)


def _mixer_fwd(qkv, bias, sinks_l, tag):
    sb, d0, d1, d2, swa = _split_heads(qkv, name=f"split_heads_{tag}")
    st = {"sb": sb, "dil": (d0, d1, d2), "swa": swa}
    o_sb, st["sb_tot"] = _sb_fwd(sb, name=f"sb_fwd_{tag}")
    st["dil_out"], st["dil_lse"], st["dil_sink"] = [], [], []
    for gi, (_, d) in enumerate(DIL_PATTERNS):
        sink = jnp.zeros((2 * d, 1, LANES), F32)
        og, lg = _band_fwd(st["dil"][gi], bias[2 * gi:2 * gi + 2], sink, nq=2 * d, offs=(0, 2 * d, 4 * d), g=1, bias_div=d,
                           has_sink=False, name=f"dil{gi}_fwd_{tag}")
        st["dil_out"].append(og)
        st["dil_lse"].append(lg)
        st["dil_sink"].append(sink)
    o_dil = _dil_merge(st["dil_out"], st["dil_lse"], None, name=f"dil_merge_fwd_{tag}")
    st["swa_sink"] = jnp.broadcast_to(sinks_l.reshape(H_SWA_Q, 1, 1), (H_SWA_Q, 1, LANES))
    st["swa_out"] = _band_fwd(swa, bias[H_DIL:], st["swa_sink"], nq=H_SWA_Q, offs=(0, H_SWA_Q, H_SWA_Q + H_SWA_KV),
                              g=H_SWA_Q // H_SWA_KV, bias_div=1, has_sink=True, name=f"swa_fwd_{tag}")
    return (o_sb, o_dil, st["swa_out"][0]), st


def _mixer_bwd(st, bias, do_sb, do_dil, do_swa, tag):
    d_sb = _sb_bwd(st["sb"], st["sb_tot"], do_sb, name=f"sb_bwd_{tag}")
    dmerge = _dil_merge(st["dil_out"], st["dil_lse"], do_dil, name=f"dil_merge_bwd_{tag}")
    d_dil, dbs = [], []
    for gi, (_, d) in enumerate(DIL_PATTERNS):
        dq, dk, dv, db, _ = _band_bwd(st["dil"][gi], bias[2 * gi:2 * gi + 2], st["dil_sink"][gi], st["dil_out"][gi],
                                      st["dil_lse"][gi], dmerge[gi], dmerge[3 + gi], nq=2 * d, offs=(0, 2 * d, 4 * d),
                                      g=1, bias_div=d, has_sink=False, name=f"dil{gi}_bwd_{tag}")
        d_dil.append((dq, dk, dv))
        dbs.append(db)
    o_sw, l_sw = st["swa_out"]
    dq_sw, dk_sw, dv_sw, db_sw, dsink = _band_bwd(st["swa"], bias[H_DIL:], st["swa_sink"], o_sw, l_sw, do_swa,
                                                  jnp.zeros_like(l_sw), nq=H_SWA_Q, offs=(0, H_SWA_Q, H_SWA_Q + H_SWA_KV),
                                                  g=H_SWA_Q // H_SWA_KV, bias_div=1, has_sink=True, name=f"swa_bwd_{tag}")
    dqkv = _join_heads(d_sb, d_dil, (dq_sw, dk_sw, dv_sw), name=f"join_heads_{tag}")
    return dqkv, jnp.concatenate(dbs + [db_sw], 0), dsink[:, 0, 0]


PIECES = ("ffn0", "mix", "ffn1")


def _ffn_fwd(x_in, w, gain, mod_j, tag, after=None):
    st = {"x": x_in, "w": w}
    st["h"] = _norm_fwd(x_in, _row(gain), _row(mod_j[1]), _row(mod_j[0]), name=f"norm_fwd_{tag}", after=after)
    st["a"], st["u"], st["s"] = _ffn_up(st["h"], w["gate"], w["up"], name=f"up_{tag}")
    st["f"], x_out = _mm(st["s"], w["down"], res=x_in, colscale=_row(0.5 * mod_j[2]), emit_acc=True, tm=512, tn=1024,
                         name=f"down_{tag}")
    return x_out, st


def _ffn_bwd(dx_out, st, gain, mod_j, tag, done, pre, nxt):
    w = st["w"]

    def latest(new, old):
        return old if new is None else new

    df, dgate = pre or _gate_bwd(dx_out, st["f"], _row(0.5 * mod_j[2]), 0.5, name=f"gate_bwd_{tag}")
    dwd = _mm_tn(st["s"], df, tm=D_FF // 2, name=f"dwd_{tag}")
    token = latest(done({"down": dwd}), dwd)
    da, du = _ffn_bwd_ds(df, w["down"], st["a"], st["u"], name=f"ds_{tag}")
    dwg = _mm_tn(da, st["h"], after=token, tm=D_FF // 2, name=f"dwg_{tag}")
    token = latest(done({"gate": dwg}), dwg)
    dwu = _mm_tn(du, st["h"], after=token, tm=D_FF // 2, name=f"dwu_{tag}")
    token = latest(done({"up": dwu}), dwu)
    dx_in, sum_dh, sum_dhx, made = _dh_norm_bwd(da, w["gate"], du, w["up"], st["x"], dx_out, _row(gain), _row(mod_j[1]), nxt,
                                                after=token, name=f"dh_{tag}")
    dmod = jnp.concatenate([sum_dh, gain * sum_dhx, dgate], 0)
    return dx_in, dmod, (1.0 + mod_j[1]) * sum_dhx[0], made


def _mix_fwd(x_in, w, gain, mod_j, bias, sinks_l, tag, after=None):
    st = {"x": x_in, "w": w}
    st["h"] = _norm_fwd(x_in, _row(gain), _row(mod_j[1]), _row(mod_j[0]), name=f"norm_fwd_mix_{tag}", after=after)
    qkv = _mm(st["h"], w["in"], tb=True, tm=SEQ, b_rows=(0, D_QKV), name=f"qkv_{tag}")
    st["gates"] = _mm(st["h"], w["in"], tb=True, tm=SEQ, b_rows=(D_QKV, D_GATES), name=f"gates_{tag}")
    outs, st["mix"] = _mixer_fwd(qkv, bias, sinks_l, tag)
    st["merged"], *st["t"] = _merge_fwd(*outs, st["gates"], w["br_sb"], w["br_dil"], w["br_swa"], name=f"merge_fwd_{tag}")
    st["f"], x_out = _mm(st["merged"], w["out"], res=x_in, colscale=_row(mod_j[2]), emit_acc=True, name=f"out_{tag}")
    return x_out, st


def _mix_bwd(dx_out, st, gain, mod_j, bias, tag, done, pre, nxt):
    w = st["w"]
    df, dgate = pre or _gate_bwd(dx_out, st["f"], _row(mod_j[2]), 1.0, name=f"gate_bwd_mix_{tag}")
    g = {"out": _mm_tn(st["merged"], df, name=f"dw_out_{tag}")}
    dmerged = _mm(df, w["out"], tb=True, name=f"dmerged_{tag}")
    dgates, do_sb, do_dil, do_swa, dbr_sb, dbr_dil, dbr_swa = _merge_bwd(
        dmerged, *st["t"], st["gates"], w["br_sb"], w["br_dil"], w["br_swa"], name=f"merge_bwd_{tag}")
    g["br_sb"] = _mm_tn(st["t"][0], dbr_sb, name=f"dw_br_sb_{tag}")
    g["br_dil"] = _mm_tn(st["t"][1], dbr_dil, name=f"dw_br_dil_{tag}")
    g["br_swa"] = _mm_tn(st["t"][2], dbr_swa, name=f"dw_br_swa_{tag}")
    dqkv, dbias, dsinks = _mixer_bwd(st["mix"], bias, do_sb, do_dil, do_swa, tag)
    dw_qkv = _mm_tn(dqkv, st["h"], out_rows=D_QKV + D_GATES, name=f"dw_qkv_{tag}")
    g["in"] = _mm_tn(dgates, st["h"], out_rows=D_QKV + D_GATES, row0=D_QKV, prev=dw_qkv, name=f"dw_gates_{tag}")
    dx_in, sum_dh, sum_dhx, made = _dh_norm_bwd(dqkv, w["in"], dgates, w["in"], st["x"], dx_out, _row(gain), _row(mod_j[1]),
                                                nxt, after=done(g), b_rows=(0, D_QKV), name=f"dh_mix_{tag}")
    dmod = jnp.concatenate([sum_dh, gain * sum_dhx, dgate], 0)
    return dx_in, dmod, (1.0 + mod_j[1]) * sum_dhx[0], dbias, dsinks, made


def _local_step(x, target, mod, gains, weights_of, rel_bias, sinks, final_gain, grads_done):
    tables = jnp.asarray(_bucket_tables())
    bias = _bias_build(rel_bias, tables, name="bias_build")
    states, h = [], x
    for l in range(DEPTH):
        st = {}
        for j, piece in enumerate(PIECES):
            w, after = weights_of(l, piece, h)
            if piece == "mix":
                h, st[piece] = _mix_fwd(h, w, gains[l, j], mod[l, j], bias, sinks[l], f"l{l}", after)
            else:
                h, st[piece] = _ffn_fwd(h, w, gains[l, j], mod[l, j], f"{piece}_l{l}", after)
        states.append(st)
    loss, dx, dfinal = _final_loss(h, target, _row(final_gain), name="final_loss")
    dmods = [[None] * 3 for _ in range(DEPTH)]
    dgains = [[None] * 3 for _ in range(DEPTH)]
    dsinks = [None] * DEPTH
    dbias, made = None, None
    sweep = [(l, j) for l in reversed(range(DEPTH)) for j in reversed(range(3))]
    for k, (l, j) in enumerate(sweep):
        piece = PIECES[j]
        done = lambda grads, l=l, piece=piece: grads_done(l, piece, grads)
        nxt = None
        if k + 1 < len(sweep):
            nl, nj = sweep[k + 1]
            coef = 1.0 if PIECES[nj] == "mix" else 0.5
            nxt = (states[nl][PIECES[nj]]["f"], _row(coef * mod[nl, nj, 2]), coef)
        if piece == "mix":
            dx, dmods[l][j], dgains[l][j], db, dsinks[l], made = _mix_bwd(
                dx, states[l][piece], gains[l, j], mod[l, j], bias, f"l{l}", done, made, nxt)
            dbias = db if dbias is None else dbias + db
        else:
            dx, dmods[l][j], dgains[l][j], made = _ffn_bwd(
                dx, states[l][piece], gains[l, j], mod[l, j], f"{piece}_l{l}", done, made, nxt)
    drel = _bias_grad(dbias, tables, name="bias_grad")[:, 0, :N_BUCKETS].T
    dmod = jnp.stack([jnp.stack(m) for m in dmods])
    dgain = jnp.stack([jnp.stack(g) for g in dgains])
    return loss, dx, dmod, dgain, dfinal[0], drel, jnp.stack(dsinks)


BR_ROWS = (H_SB * HEAD_DIM, 2 * HEAD_DIM, H_SWA_Q * HEAD_DIM)


def _lanes_unshard(g, lead):
    _, rows, _ = g.shape
    r = rows // lead
    return g.reshape(N_DEV, lead, r, LANES).transpose(1, 2, 0, 3).reshape(lead, r, N_DEV * LANES)


def _lanes_shard(full):
    lead, r, _ = full.shape
    return full.reshape(lead, r, N_DEV, LANES).transpose(2, 0, 1, 3).reshape(N_DEV, lead * r, LANES)


def _pack_rows(parts, dtype):
    flat = jnp.concatenate([p.astype(dtype).reshape(-1) for p in parts])
    pad = (-flat.shape[0]) % (16 * LANES)
    if pad:
        flat = jnp.concatenate([flat, jnp.zeros((pad,), dtype)])
    return flat.reshape(-1, LANES)


def _unshard(gathered, axis):
    moved = jnp.moveaxis(gathered, 0, axis)
    shape = list(moved.shape)
    shape[axis:axis + 2] = [shape[axis] * shape[axis + 1]]
    return moved.reshape(shape)


def kernel(x, c, w_ada, b_ada, norm_gain, w_ffn_gate, w_ffn_up, w_ffn_down, w_in, w_br_sb, w_br_dil, w_br_swa, w_out, sinks, rel_bias, final_gain, loss_target, m_w_ada, m_b_ada, m_norm_gain, m_w_ffn_gate, m_w_ffn_up, m_w_ffn_down, m_w_in, m_w_br_sb, m_w_br_dil, m_w_br_swa, m_w_out, m_sinks, m_rel_bias, m_final_gain, v_w_ada, v_b_ada, v_norm_gain, v_w_ffn_gate, v_w_ffn_up, v_w_ffn_down, v_w_in, v_w_br_sb, v_w_br_dil, v_w_br_swa, v_w_out, v_sinks, v_rel_bias, v_final_gain):
    me = 4 * lax.axis_index("x") + 2 * lax.axis_index("y") + lax.axis_index("c")
    d = D_MODEL
    gate_t, up_t, in_t = jnp.swapaxes(w_ffn_gate, 2, 3), jnp.swapaxes(w_ffn_up, 2, 3), jnp.swapaxes(w_in, 1, 2)

    def piece_shards(l, piece):
        bf = lambda t: t.astype(BF16)
        if piece == "mix":
            return [bf(in_t[l]), jnp.concatenate([bf(w_br_sb[l]), bf(w_br_dil[l]), bf(w_br_swa[l])], 0), bf(w_out[l])]
        i = PIECES.index(piece) // 2
        return [bf(gate_t[l, i]), bf(up_t[l, i]), bf(w_ffn_down[l, i])]

    br_off = np.concatenate([[0], np.cumsum(BR_ROWS)])

    def piece_weights(gathered, piece):
        if piece == "mix":
            g_in, g_br, g_out = gathered
            f_br = [_lanes_unshard(g_br[:, br_off[k]:br_off[k + 1]], 1)[0] for k in range(3)]
            return {"in": g_in.reshape(D_QKV + D_GATES, d), "br_sb": f_br[0], "br_dil": f_br[1], "br_swa": f_br[2],
                    "out": g_out.reshape(d, d)}
        return {n: g.reshape(D_FF, d) for n, g in zip(("gate", "up", "down"), gathered)}

    order = [(l, piece) for l in range(DEPTH) for piece in PIECES]
    ahead = 3
    in_flight, passed = {}, {}

    def start_gather(k, after):
        l, piece = order[k]
        in_flight[k], token = _relay_start(piece_shards(l, piece), after, name=f"gather_{piece}_l{l}_start")
        return token

    small, = _all_gather([_pack_rows([c, norm_gain], F32)], after=start_gather(0, c), name="gather_cond")
    c_all = small[:, :d // LANES].reshape(N_DEV, d)
    gains = _unshard(small[:, d // LANES:d // LANES + 6].reshape(N_DEV, DEPTH, 3, LANES), 2)

    cols = w_ada.shape[2]
    mod_cols = jnp.stack([_ada_fwd(c_all, w_ada[l], name=f"ada_fwd_l{l}") for l in range(DEPTH)])
    mod_all, = _all_gather([_pack_rows([mod_cols], F32)], name="gather_mod")
    mod_all = mod_all.reshape(N_DEV, -1)[:, :DEPTH * N_DEV * cols].reshape(N_DEV, DEPTH, N_DEV, cols)
    mod_mine = lax.dynamic_index_in_dim(mod_all, me, axis=2, keepdims=False)
    mod = (mod_mine.transpose(1, 0, 2).reshape(DEPTH, N_DEV * cols) + b_ada).reshape(DEPTH, 3, 3, d)

    token = mod_all
    for k in range(1, 1 + ahead):
        token = start_gather(k, token)
    mod = mod + token[0, 0]

    def weights_of(l, piece, h):
        k = order.index((l, piece))
        token = start_gather(k + ahead, h) if k + ahead < len(order) and k + ahead not in in_flight else None
        for nxt in ([k] if k < 3 else []) + ([k + 1] if 3 <= k + 1 < len(order) else []):
            nl, npiece = order[nxt]
            passed[nxt], token = _relay_pass(in_flight[nxt], h if token is None else token,
                                             name=f"gather_{npiece}_l{nl}_pass")
        landed = _relay_wait(passed[k], h if token is None else token, name=f"gather_{piece}_l{l}_wait")
        return piece_weights(landed, piece), token

    exchanges, have, deferred = {}, {}, []

    def grads_done(l, piece, g):
        key = (l, piece)
        have.setdefault(key, {}).update(g)
        if piece == "mix":
            if len(have[key]) < 5:
                return None
            g = have[key]
            s_br = jnp.concatenate([_lanes_shard(g[n][None]) for n in ("br_sb", "br_dil", "br_swa")], 1)
            groups = [(("in", "br", "out"), [g["in"].reshape(N_DEV, -1, d), s_br, g["out"].reshape(N_DEV, -1, d)])]
        elif key == order[0] and "down" in g:
            groups = [(("down",), [g["down"].reshape(N_DEV, -1, d)])]
        elif key == order[0]:
            deferred.extend(((n,), [t.reshape(N_DEV, -1, d)]) for n, t in g.items())
            return None
        elif len(have[key]) < 3:
            return None
        else:
            groups = [(("gate", "up", "down"), [have[key][n].reshape(N_DEV, -1, d) for n in ("gate", "up", "down")])]
        token = None
        for names, sg in groups:
            state, token = _exchange_start(sg, None, name=f"exchange_{piece}_l{l}_{names[0]}_start")
            exchanges.setdefault(key, []).append((names, state))
        return token

    loss, dx, dmod, dgains, dfinal, drel, dsinks = _local_step(
        x[0], loss_target[0], mod, gains, weights_of, rel_bias, sinks, final_gain, grads_done)

    flat = lambda t: t.reshape(-1, t.shape[-1])
    transposed = lambda ts: tuple(flat(jnp.swapaxes(t, -1, -2)) for t in ts)
    families = {
        "gate": transposed((w_ffn_gate, m_w_ffn_gate, v_w_ffn_gate)), "up": transposed((w_ffn_up, m_w_ffn_up, v_w_ffn_up)),
        "down": tuple(flat(t) for t in (w_ffn_down, m_w_ffn_down, v_w_ffn_down)),
        "in": transposed((w_in, m_w_in, v_w_in)),
        "br": tuple(flat(jnp.concatenate(ts, 1)) for ts in ((w_br_sb, w_br_dil, w_br_swa), (m_w_br_sb, m_w_br_dil, m_w_br_swa),
                                                            (v_w_br_sb, v_w_br_dil, v_w_br_swa))),
        "out": tuple(flat(t) for t in (w_out, m_w_out, v_w_out))}
    parts, stepped = {}, {}

    def step(keys, after):
        for key in keys:
            for names, ex_state in exchanges[key]:
                landed = _exchange_wait(ex_state, after, name=f"exchange_{key[1]}_l{key[0]}_{names[0]}_wait")
                parts.setdefault(key, {}).update(zip(names, landed))
                after = landed[0]
        for key in keys:
            l, piece = key
            for n, group in parts[key].items():
                w2, m2, v2 = families[n]
                rows = group.shape[1]
                row0 = (2 * l + PIECES.index(piece) // 2) * rows if piece != "mix" else l * rows
                stepped[n] = _reduce_adamw([group], w2, m2, v2, row0, stepped.get(n), after=after,
                                           name=f"reduce_adamw_{n}_{piece}_l{l}")
                after = stepped[n][1]
        return after

    small_parts = [dmod, dgains, dfinal, drel.T, dsinks, loss[0, :1]]
    small_sizes = [int(np.prod(p.shape)) for p in small_parts]
    small_all, = _all_gather([_pack_rows(small_parts, F32)], name="gather_small")
    token = small_all
    for names, sg in deferred:
        state, token = _exchange_start(sg, token, name=f"exchange_ffn0_l0_{names[0]}_start")
        exchanges.setdefault(order[0], []).append((names, state))

    after_l1 = step([key for key in reversed(order) if key[0] == 1], token)
    small_sum = _sum_parts(small_all, name="sum_small", after=token).reshape(-1)
    offs = np.concatenate([[0], np.cumsum(small_sizes)])
    g_b_ada = small_sum[offs[0]:offs[1]].reshape(DEPTH, 9 * d)
    g_gain_full = small_sum[offs[1]:offs[2]].reshape(DEPTH, 3, d)
    g_norm_gain = lax.dynamic_slice_in_dim(g_gain_full, me * LANES, LANES, axis=2)
    g_final = small_sum[offs[2]:offs[3]]
    g_rel = small_sum[offs[3]:offs[4]].reshape(N_SOFT, N_BUCKETS).T
    g_sinks = small_sum[offs[4]:offs[5]].reshape(DEPTH, H_SWA_Q)
    loss_total = small_sum[offs[5]]

    dmod_all = small_all.reshape(N_DEV, -1)[:, :DEPTH * 9 * d].reshape(N_DEV, DEPTH, 9 * d)
    dmod_cols = lax.dynamic_slice_in_dim(dmod_all, me * cols, cols, axis=2)
    g_w_ada = jnp.stack([_ada_bwd(c_all.T, dmod_cols[:, l], name=f"ada_bwd_l{l}") for l in range(DEPTH)])

    small_state = {"w_ada": (w_ada, m_w_ada, v_w_ada), "b_ada": (b_ada, m_b_ada, v_b_ada),
                   "norm_gain": (norm_gain, m_norm_gain, v_norm_gain), "sinks": (sinks, m_sinks, v_sinks),
                   "rel_bias": (rel_bias, m_rel_bias, v_rel_bias), "final_gain": (final_gain, m_final_gain, v_final_gain)}
    after = step([order[2], order[1]], after_l1)
    grad, update = {}, {}
    for n, g in (("w_ada", g_w_ada), ("b_ada", g_b_ada), ("norm_gain", g_norm_gain), ("sinks", g_sinks),
                 ("rel_bias", g_rel), ("final_gain", g_final)):
        w, m, v = small_state[n]
        grad[n] = g
        if w.ndim == 1:
            update[n] = tuple(t.reshape(w.shape)
                              for t in _adamw(_row(w), _row(g), _row(m), _row(v), name=f"adamw_{n}", after=after))
        else:
            update[n] = _adamw(w, g, m, v, name=f"adamw_{n}", after=after)
        after = update[n][0]

    step([order[0]], after)

    def unflat(n, like, swapped):
        shape = jnp.swapaxes(like, -1, -2).shape if swapped else like.shape
        out = [t.reshape(shape) for t in stepped[n]]
        return [jnp.swapaxes(t, -1, -2) for t in out] if swapped else out

    results = {"w_ffn_gate": unflat("gate", w_ffn_gate, True), "w_ffn_up": unflat("up", w_ffn_up, True),
               "w_ffn_down": unflat("down", w_ffn_down, False), "w_in": unflat("in", w_in, True),
               "w_out": unflat("out", w_out, False)}
    br = [t.reshape(DEPTH, -1, LANES) for t in stepped["br"]]
    for k, n in enumerate(("w_br_sb", "w_br_dil", "w_br_swa")):
        results[n] = [t[:, br_off[k]:br_off[k + 1]] for t in br]
    for n, (g, dl, nm, nv) in results.items():
        grad[n], update[n] = g, (dl, nm, nv)

    names = ["w_ada", "b_ada", "norm_gain", "w_ffn_gate", "w_ffn_up", "w_ffn_down", "w_in", "w_br_sb", "w_br_dil",
             "w_br_swa", "w_out", "sinks", "rel_bias", "final_gain"]
    return (loss_total, dx[None], *[grad[n] for n in names], *[update[n][0] for n in names],
            *[update[n][1] for n in names], *[update[n][2] for n in names])
```

```python
import math

import numpy as np
import jax
import jax.numpy as jnp
from jax import lax
from jax.experimental import pallas as pl
from jax.experimental.pallas import tpu as pltpu

F32, BF16 = jnp.float32, jnp.bfloat16

SEQ, D_MODEL, D_FF, HEAD_DIM = 2048, 1024, 2816, 64
DEPTH = 2
BLK = 128
H_SB, H_DIL, H_SWA_Q, H_SWA_KV = 4, 6, 6, 2
DIL_PATTERNS = ((128, 1), (512, 4), (2048, 16))
SWA_WINDOW = 128
N_BUCKETS, MAX_REL_DIST = 32, 2048
RMS_EPS = 1e-6
D_QKV = 2560
D_GATES = 3 * D_MODEL
ADAM_LR, ADAM_B1, ADAM_B2, ADAM_EPS, ADAM_WD, ADAM_STEP = 0.001, 0.9, 0.999, 1e-08, 0.01, 10

N_DEV = 8
LANES = 128
NEG = -1e30
SB_TILE = 512
VMEM_LIMIT_BYTES = 48 * 1024 * 1024
HBM = pl.BlockSpec(memory_space=pltpu.HBM)
MESH = pl.DeviceIdType.MESH


def _tile(n, target):
    t = (min(n, target) // LANES) * LANES
    while t >= LANES:
        if n % t == 0:
            return t
        t -= LANES
    return n


def _row_tile(r, cap):
    t = (min(r, cap) // 16) * 16
    while t > 16 and r % t:
        t -= 16
    return t


def _params(semantics=None):
    return pltpu.CompilerParams(dimension_semantics=semantics, vmem_limit_bytes=VMEM_LIMIT_BYTES)


def _dot(a, b, ca, cb):
    return lax.dot_general(a, b, (((ca,), (cb,)), ((), ())), preferred_element_type=F32)


def _sigmoid(a):
    return 1.0 / (1.0 + jnp.exp(-a))


def _row(v):
    return v.reshape(1, -1)


def _all_gather(arrs, name, after=None):
    n = len(arrs)
    ins = list(arrs) + ([] if after is None else [after])

    def body(*refs):
        x_refs, out_refs = refs[:n], refs[len(ins):len(ins) + n]
        send_sems, recv_sems, local_sems = refs[len(ins) + n:]
        x, y, c = lax.axis_index("x"), lax.axis_index("y"), lax.axis_index("c")
        me, sibling = (x, y, c), (x, y, 1 - c)
        chips = [(1 - x, y), (x, 1 - y), (1 - x, 1 - y)]

        def slot(t, px, py, pc):
            return out_refs[t].at[4 * px + 2 * py + pc]

        def copy(t, k, block, to, src=None):
            return pltpu.make_async_remote_copy(
                src_ref=slot(t, *block) if src is None else src, dst_ref=slot(t, *block),
                send_sem=send_sems.at[7 * t + k], recv_sem=recv_sems.at[7 * t + k], device_id=to, device_id_type=MESH)

        mine = [pltpu.make_async_copy(x_refs[t], slot(t, *me), local_sems.at[t]) for t in range(n)]
        for cp in mine:
            cp.start()
        first = []
        for t in range(n):
            first.append(copy(t, 0, me, sibling, src=x_refs[t]))
            first += [copy(t, 1 + j, me, (*chip, c), src=x_refs[t]) for j, chip in enumerate(chips)]
        for cp in first:
            cp.start()
        passed = []
        for j, chip in enumerate(chips):
            for t in range(n):
                copy(t, 1 + j, (*chip, c), me).wait_recv()
                passed.append(copy(t, 4 + j, (*chip, c), sibling))
                passed[-1].start()
        for t in range(n):
            copy(t, 0, sibling, me).wait_recv()
        for j, chip in enumerate(chips):
            for t in range(n):
                copy(t, 4 + j, (*chip, 1 - c), me).wait_recv()
        for cp in first + passed:
            cp.wait_send()
        for cp in mine:
            cp.wait()

    return pl.pallas_call(
        body, name=name, out_shape=[jax.ShapeDtypeStruct((N_DEV,) + a.shape, a.dtype) for a in arrs],
        in_specs=[HBM] * n + [pl.BlockSpec(memory_space=pl.ANY)] * (len(ins) - n), out_specs=[HBM] * n,
        scratch_shapes=[pltpu.SemaphoreType.DMA((7 * n,)), pltpu.SemaphoreType.DMA((7 * n,)), pltpu.SemaphoreType.DMA((n,))],
    )(*ins)


def _direct_copies(x_refs, land_refs, send_sems, recv_sems, local_sems):
    x, y, c = lax.axis_index("x"), lax.axis_index("y"), lax.axis_index("c")
    me = 4 * x + 2 * y + c
    sends, recvs = [], []
    for k in range(1, N_DEV):
        px = 1 - x if (k >> 2) & 1 else x
        py = 1 - y if (k >> 1) & 1 else y
        pc = 1 - c if k & 1 else c
        peer = 4 * px + 2 * py + pc
        for t, (x_ref, land_ref) in enumerate(zip(x_refs, land_refs)):
            sem = 7 * t + k - 1
            for out, src, slot in ((sends, peer, me), (recvs, me, peer)):
                out.append(pltpu.make_async_remote_copy(
                    src_ref=x_ref.at[src], dst_ref=land_ref.at[slot], send_sem=send_sems.at[sem],
                    recv_sem=recv_sems.at[sem], device_id=(px, py, pc), device_id_type=MESH))
    own = [pltpu.make_async_copy(x_ref.at[me], land_ref.at[me], local_sems.at[t])
           for t, (x_ref, land_ref) in enumerate(zip(x_refs, land_refs))]
    return sends, recvs, own


SEM =pl.BlockSpec(memory_space=pltpu.SEMAPHORE)
ANY = pl.BlockSpec(memory_space=pl.ANY)
SIDE_EFFECT = pltpu.SideEffectType.DATAFLOW_SIDE_EFFECTING


def _exchange_start(arrs, after, *, name):
    n = len(arrs)
    lands = [lax.empty(a.shape, a.dtype) for a in arrs]
    extra = [] if after is None else [after]

    def body(*refs):
        sems = refs[2 * n + len(extra):2 * n + len(extra) + 3]
        sends, _, own = _direct_copies(refs[:n], refs[n:2 * n], *sems)
        for cp in own + sends:
            cp.start()
        refs[-1][...] = jnp.zeros_like(refs[-1])

    ops = [pltpu.with_memory_space_constraint(a, pltpu.HBM) for a in list(arrs) + lands]
    out = pl.pallas_call(
        body, name=name,
        out_shape=(pltpu.SemaphoreType.DMA((7 * n,)), pltpu.SemaphoreType.DMA((7 * n,)), pltpu.SemaphoreType.DMA((n,)),
                   *[pltpu.HBM(a.shape, a.dtype) for a in ops], jax.ShapeDtypeStruct((8, LANES), F32)),
        in_specs=[HBM] * (2 * n) + [ANY] * len(extra),
        out_specs=(SEM, SEM, SEM, *[HBM] * (2 * n), pl.BlockSpec(memory_space=pltpu.VMEM)),
        input_output_aliases={t: 3 + t for t in range(2 * n)},
        compiler_params=pltpu.CompilerParams(has_side_effects=SIDE_EFFECT),
    )(*ops, *extra)
    return (out[:3], out[3:3 + n], out[3 + n:3 + 2 * n]), out[-1]


def _exchange_wait(state, after, *, name):
    sems, arrs, lands = state
    n = len(arrs)

    def body(*refs):
        sends, recvs, own = _direct_copies(refs[:n], refs[n:2 * n], *refs[2 * n:2 * n + 3])
        for cp in own:
            cp.wait()
        for cp in sends:
            cp.wait_send()
        for cp in recvs:
            cp.wait_recv()

    out = pl.pallas_call(
        body, name=name, out_shape=tuple(pltpu.HBM(a.shape, a.dtype) for a in list(arrs) + list(lands)),
        in_specs=[HBM] * (2 * n) + [SEM, SEM, SEM, ANY], out_specs=tuple([HBM] * (2 * n)),
        input_output_aliases={t: t for t in range(2 * n)},
        compiler_params=pltpu.CompilerParams(has_side_effects=SIDE_EFFECT),
    )(*arrs, *lands, *sems, after)
    return out[n:]


def _relay_copies(x_refs, land_refs, sems_a, sems_b):
    x, y, c = lax.axis_index("x"), lax.axis_index("y"), lax.axis_index("c")
    me = 4 * x + 2 * y + c
    sibling = (x, y, 1 - c)
    chips = [(1 - x, y), (x, 1 - y), (1 - x, 1 - y)]

    def slot(px, py, pc):
        return 4 * px + 2 * py + pc

    def copy(src, land_ref, dst_slot, send_sems, recv_sems, k, to):
        return pltpu.make_async_remote_copy(src_ref=src, dst_ref=land_ref.at[dst_slot], send_sem=send_sems.at[k],
                                            recv_sem=recv_sems.at[k], device_id=to, device_id_type=MESH)

    a_send, a_recv, a_own, b_send, b_recv = [], [], [], [], []
    for t, (x_ref, land_ref) in enumerate(zip(x_refs, land_refs)):
        peers = [sibling] + [(*chip, c) for chip in chips]
        if sems_a is not None:
            for k, peer in enumerate(peers):
                a_send.append(copy(x_ref, land_ref, me, sems_a[0], sems_a[1], 4 * t + k, peer))
                a_recv.append(copy(x_ref, land_ref, slot(*peer), sems_a[0], sems_a[1], 4 * t + k, peer))
            a_own.append(pltpu.make_async_copy(x_ref, land_ref.at[me], sems_a[2].at[t]))
        if sems_b is not None:
            for j, chip in enumerate(chips):
                b_send.append(copy(land_ref.at[slot(*chip, c)], land_ref, slot(*chip, c), sems_b[0], sems_b[1], 3 * t + j, sibling))
                b_recv.append(copy(land_ref.at[slot(*chip, c)], land_ref, slot(*chip, 1 - c), sems_b[0], sems_b[1], 3 * t + j,
                                   sibling))
    return (a_send, a_recv, a_own), (b_send, b_recv)


def _relay_start(arrs, after, name):
    n = len(arrs)
    lands = [lax.empty((N_DEV,) + a.shape, a.dtype) for a in arrs]

    def body(*refs):
        (sends, _, own), _ = _relay_copies(refs[:n], refs[n:2 * n], refs[2 * n + 1:2 * n + 4], None)
        for cp in own + sends:
            cp.start()
        refs[-1][...] = jnp.zeros_like(refs[-1])

    ops = [pltpu.with_memory_space_constraint(a, pltpu.HBM) for a in list(arrs) + lands]
    out = pl.pallas_call(
        body, name=name,
        out_shape=(pltpu.SemaphoreType.DMA((4 * n,)), pltpu.SemaphoreType.DMA((4 * n,)), pltpu.SemaphoreType.DMA((n,)),
                   *[pltpu.HBM(a.shape, a.dtype) for a in ops], jax.ShapeDtypeStruct((8, LANES), F32)),
        in_specs=[HBM] * (2 * n) + [ANY],
        out_specs=(SEM, SEM, SEM, *[HBM] * (2 * n), pl.BlockSpec(memory_space=pltpu.VMEM)),
        input_output_aliases={t: 3 + t for t in range(2 * n)},
        compiler_params=pltpu.CompilerParams(has_side_effects=SIDE_EFFECT),
    )(*ops, after)
    return (out[:3], out[3:3 + n], out[3 + n:3 + 2 * n]), out[-1]


def _relay_pass(state, after, name):
    sems_a, arrs, lands = state
    n = len(arrs)

    def body(*refs):
        sems_b = refs[2 * n + 4:2 * n + 6]
        (a_send, a_recv, a_own), (b_send, _) = _relay_copies(refs[:n], refs[n:2 * n], refs[2 * n:2 * n + 3], sems_b)
        for cp in a_own:
            cp.wait()
        for cp in a_send:
            cp.wait_send()
        for cp in a_recv:
            cp.wait_recv()
        for cp in b_send:
            cp.start()
        refs[-1][...] = jnp.zeros_like(refs[-1])

    out = pl.pallas_call(
        body, name=name,
        out_shape=(pltpu.SemaphoreType.DMA((3 * n,)), pltpu.SemaphoreType.DMA((3 * n,)),
                   *[pltpu.HBM(a.shape, a.dtype) for a in list(arrs) + list(lands)], jax.ShapeDtypeStruct((8, LANES), F32)),
        in_specs=[HBM] * (2 * n) + [SEM, SEM, SEM, ANY],
        out_specs=(SEM, SEM, *[HBM] * (2 * n), pl.BlockSpec(memory_space=pltpu.VMEM)),
        input_output_aliases={t: 2 + t for t in range(2 * n)},
        compiler_params=pltpu.CompilerParams(has_side_effects=SIDE_EFFECT),
    )(*arrs, *lands, *sems_a, after)
    return (out[:2], out[2:2 + n], out[2 + n:2 + 2 * n]), out[-1]


def _relay_wait(state, after, name):
    sems_b, arrs, lands = state
    n = len(arrs)

    def body(*refs):
        _, (b_send, b_recv) = _relay_copies(refs[:n], refs[n:2 * n], None, refs[2 * n:2 * n + 2])
        for cp in b_send:
            cp.wait_send()
        for cp in b_recv:
            cp.wait_recv()

    out = pl.pallas_call(
        body, name=name, out_shape=tuple(pltpu.HBM(a.shape, a.dtype) for a in list(arrs) + list(lands)),
        in_specs=[HBM] * (2 * n) + [SEM, SEM, ANY], out_specs=tuple([HBM] * (2 * n)),
        input_output_aliases={t: t for t in range(2 * n)},
        compiler_params=pltpu.CompilerParams(has_side_effects=SIDE_EFFECT),
    )(*arrs, *lands, *sems_b, after)
    return out[n:]


def _sum_parts(parts, name, after=None):
    n, r, cdim = parts.shape
    tr = _row_tile(r, max(16, (1 << 21) // (n * cdim * parts.dtype.itemsize)))

    def body(p_ref, *rest):
        acc = p_ref[0].astype(F32)
        for k in range(1, n):
            acc = acc + p_ref[k].astype(F32)
        rest[-1][...] = acc

    ins = [parts] + ([] if after is None else [after])
    return pl.pallas_call(
        body, name=name, out_shape=jax.ShapeDtypeStruct((r, cdim), F32), grid=(r // tr,),
        in_specs=[pl.BlockSpec((n, tr, cdim), lambda i: (0, i, 0))] + [ANY] * (len(ins) - 1),
        out_specs=pl.BlockSpec((tr, cdim), lambda i: (i, 0)), compiler_params=_params(("parallel",)),
    )(*ins)


def _mm_tn(a, b, *, name, after=None, tm=512, tn=1024, out_rows=None, row0=0, prev=None):
    k, m = a.shape
    n = b.shape[1]
    tm, tn = _tile(m, tm), _tile(n, tn)
    out_rows = m if out_rows is None else out_rows

    def body(a_ref, b_ref, *rest):
        o_ref, at_ref = rest[-2], rest[-1]

        @pl.when(pl.program_id(1) == 0)
        def _():
            at_ref[...] = a_ref[...].astype(BF16).T

        o_ref[...] = _dot(at_ref[...], b_ref[...].astype(BF16), 1, 0).astype(BF16)

    ins = [a, b] + [t for t in (after, prev) if t is not None]
    return pl.pallas_call(
        body, name=name, out_shape=jax.ShapeDtypeStruct((out_rows, n), BF16), grid=(m // tm, n // tn),
        in_specs=[pl.BlockSpec((k, tm), lambda i, j: (0, i)), pl.BlockSpec((k, tn), lambda i, j: (0, j))] + [ANY] * (len(ins) - 2),
        out_specs=pl.BlockSpec((tm, tn), lambda i, j: (row0 // tm + i, j)),
        input_output_aliases={} if prev is None else {len(ins) - 1: 0},
        scratch_shapes=[pltpu.VMEM((tm, k), BF16)], compiler_params=_params(("parallel", "arbitrary")),
    )(*ins)


def _mm(a, b, *, name, ta=False, tb=False, res=None, colscale=None, emit_acc=False,
        out_dtype=F32, tm=512, tn=512, b_rows=None):
    m, k = (a.shape[1], a.shape[0]) if ta else a.shape
    n = b.shape[0] if tb else b.shape[1]
    b_start = 0
    if b_rows is not None:
        b_start, n = b_rows
    tm, tn = _tile(m, tm), _tile(n, tn)
    ca, cb = (0 if ta else 1), (1 if tb else 0)
    a_spec = pl.BlockSpec((k, tm), lambda i, j: (0, i)) if ta else pl.BlockSpec((tm, k), lambda i, j: (i, 0))
    b_spec = (pl.BlockSpec((tn, k), lambda i, j: (b_start // tn + j, 0)) if tb
              else pl.BlockSpec((k, tn), lambda i, j: (0, j)))
    tile = pl.BlockSpec((tm, tn), lambda i, j: (i, j))
    ins, in_specs = [a, b], [a_spec, b_spec]
    if res is not None:
        ins.append(res)
        in_specs.append(tile)
    if colscale is not None:
        ins.append(colscale)
        in_specs.append(pl.BlockSpec((1, tn), lambda i, j: (0, j)))
    n_in = len(ins)

    def body(*refs):
        outs = refs[n_in:]
        acc = _dot(refs[0][...].astype(BF16), refs[1][...].astype(BF16), ca, cb)
        val, p = acc, 2
        if res is not None:
            r_val, p = refs[p][...], p + 1
        if colscale is not None:
            val = val * refs[p][...]
        if res is not None:
            val = r_val + val
        if emit_acc:
            outs[0][...] = acc
        outs[-1][...] = val.astype(out_dtype)

    out_shape = [jax.ShapeDtypeStruct((m, n), out_dtype)]
    out_specs = [tile]
    if emit_acc:
        out_shape.insert(0, jax.ShapeDtypeStruct((m, n), F32))
        out_specs.insert(0, tile)
    out = pl.pallas_call(
        body, name=name, out_shape=out_shape, grid=(m // tm, n // tn), in_specs=in_specs, out_specs=out_specs,
        compiler_params=_params(("parallel", "parallel")),
    )(*ins)
    return out if emit_acc else out[0]


def _norm_fwd(x, g, scale, shift, name, after=None):
    s, d = x.shape
    tr = 256

    def body(x_ref, g_ref, sc_ref, sh_ref, *rest):
        xv = x_ref[...]
        rstd = lax.rsqrt(jnp.mean(xv * xv, axis=-1, keepdims=True) + RMS_EPS)
        rest[-1][...] = (xv * rstd * g_ref[...] * (1.0 + sc_ref[...]) + sh_ref[...]).astype(BF16)

    rowspec = pl.BlockSpec((1, d), lambda i: (0, 0))
    ins = [x, g, scale, shift] + ([] if after is None else [after])
    return pl.pallas_call(
        body, name=name, out_shape=jax.ShapeDtypeStruct((s, d), BF16), grid=(s // tr,),
        in_specs=[pl.BlockSpec((tr, d), lambda i: (i, 0)), rowspec, rowspec, rowspec] + [ANY] * (len(ins) - 4),
        out_specs=pl.BlockSpec((tr, d), lambda i: (i, 0)),
        compiler_params=_params(("parallel",)),
    )(*ins)


def _dh_norm_bwd(a1, b1, a2, b2, x, dres, g, scale, nxt, *, name, after=None, b_rows=None):
    s, d = x.shape
    tm = 256
    n_fixed = 8

    def body(a1_ref, b1_ref, a2_ref, b2_ref, x_ref, dr_ref, g_ref, sc_ref, *rest):
        rest = rest[(1 if after is not None else 0):]
        if nxt is not None:
            f_ref, cs_ref, dx_ref, sa_ref, sb_ref, df_ref, dg_ref = rest
        else:
            dx_ref, sa_ref, sb_ref = rest

        @pl.when(pl.program_id(0) == 0)
        def _():
            sa_ref[...] = jnp.zeros_like(sa_ref)
            sb_ref[...] = jnp.zeros_like(sb_ref)
            if nxt is not None:
                dg_ref[...] = jnp.zeros_like(dg_ref)

        dhv = (_dot(a1_ref[...].astype(BF16), b1_ref[...], 1, 0) + _dot(a2_ref[...].astype(BF16), b2_ref[...], 1, 0))
        xv = x_ref[...]
        rstd = lax.rsqrt(jnp.mean(xv * xv, axis=-1, keepdims=True) + RMS_EPS)
        xhat = xv * rstd
        dxhat = dhv * (g_ref[...] * (1.0 + sc_ref[...]))
        mean_term = jnp.mean(dxhat * xhat, axis=-1, keepdims=True)
        dxv = dr_ref[...] + rstd * (dxhat - xhat * mean_term)
        dx_ref[...] = dxv
        sa_ref[...] += jnp.sum(dhv, axis=0, keepdims=True)
        sb_ref[...] += jnp.sum(dhv * xhat, axis=0, keepdims=True)
        if nxt is not None:
            df_ref[...] = (dxv * cs_ref[...]).astype(BF16)
            dg_ref[...] += nxt[2] * jnp.sum(dxv * f_ref[...], axis=0, keepdims=True)

    def a_spec(t):
        return pl.BlockSpec((tm, t.shape[1]), lambda i: (i, 0))

    def b_spec(t, a, which):
        if b_rows is None:
            return pl.BlockSpec((t.shape[0], d), lambda i: (0, 0))
        start = b_rows[which]
        return pl.BlockSpec((pl.Element(a.shape[1]), pl.Element(d)), lambda i: (start, 0))

    rowspec = pl.BlockSpec((1, d), lambda i: (0, 0))
    tile = pl.BlockSpec((tm, d), lambda i: (i, 0))
    ins = [a1, b1, a2, b2, x, dres, g, scale] + ([] if after is None else [after])
    in_specs = [a_spec(a1), b_spec(b1, a1, 0), a_spec(a2), b_spec(b2, a2, 1), tile, tile, rowspec, rowspec]
    in_specs += [ANY] * (len(ins) - n_fixed)
    out_shape = [jax.ShapeDtypeStruct((s, d), F32), jax.ShapeDtypeStruct((1, d), F32), jax.ShapeDtypeStruct((1, d), F32)]
    out_specs = [tile, rowspec, rowspec]
    if nxt is not None:
        ins += [nxt[0], nxt[1]]
        in_specs += [tile, rowspec]
        out_shape += [jax.ShapeDtypeStruct((s, d), BF16), jax.ShapeDtypeStruct((1, d), F32)]
        out_specs += [tile, rowspec]
    out = pl.pallas_call(
        body, name=name, out_shape=out_shape, grid=(s // tm,), in_specs=in_specs, out_specs=out_specs,
        compiler_params=_params(("arbitrary",)),
    )(*ins)
    return out[0], out[1], out[2], (None if nxt is None else (out[3], out[4]))


def _gate_bwd(dxn, f, colscale, coef, name):
    s, d = dxn.shape
    tr = 256

    def body(dx_ref, f_ref, cs_ref, df_ref, dg_ref):
        @pl.when(pl.program_id(0) == 0)
        def _():
            dg_ref[...] = jnp.zeros_like(dg_ref)

        dxv = dx_ref[...]
        df_ref[...] = (dxv * cs_ref[...]).astype(BF16)
        dg_ref[...] += coef * jnp.sum(dxv * f_ref[...], axis=0, keepdims=True)

    rowspec = pl.BlockSpec((1, d), lambda i: (0, 0))
    tile = pl.BlockSpec((tr, d), lambda i: (i, 0))
    return pl.pallas_call(
        body, name=name, out_shape=[jax.ShapeDtypeStruct((s, d), BF16), jax.ShapeDtypeStruct((1, d), F32)],
        grid=(s // tr,), in_specs=[tile, tile, rowspec], out_specs=[tile, rowspec],
        compiler_params=_params(("arbitrary",)),
    )(dxn, f, colscale)


def _ffn_up(h, wg, wu, name, tm=SEQ, tn=256):
    s, d = h.shape
    f = wg.shape[0]

    def body(h_ref, wg_ref, wu_ref, a_ref, u_ref, s_ref):
        hv = h_ref[...]
        a = _dot(hv, wg_ref[...], 1, 1)
        u = _dot(hv, wu_ref[...], 1, 1)
        a_ref[...] = a.astype(BF16)
        u_ref[...] = u.astype(BF16)
        s_ref[...] = (a * _sigmoid(a) * u).astype(BF16)

    tile = pl.BlockSpec((tm, tn), lambda i, j: (i, j))
    wspec = pl.BlockSpec((tn, d), lambda i, j: (j, 0))
    return pl.pallas_call(
        body, name=name,
        out_shape=[jax.ShapeDtypeStruct((s, f), BF16), jax.ShapeDtypeStruct((s, f), BF16), jax.ShapeDtypeStruct((s, f), BF16)],
        grid=(s // tm, f // tn), in_specs=[pl.BlockSpec((tm, d), lambda i, j: (i, 0)), wspec, wspec],
        out_specs=[tile, tile, tile], compiler_params=_params(("parallel", "parallel")),
    )(h, wg, wu)


def _ffn_bwd_ds(df, wd, a, u, name, tm=SEQ, tn=256):
    s, d = df.shape
    f = wd.shape[0]

    def body(df_ref, wd_ref, a_ref, u_ref, da_ref, du_ref):
        ds = _dot(df_ref[...], wd_ref[...], 1, 1)
        av = a_ref[...].astype(F32)
        sg = _sigmoid(av)
        da_ref[...] = (ds * u_ref[...].astype(F32) * (sg * (1.0 + av * (1.0 - sg)))).astype(BF16)
        du_ref[...] = (ds * (av * sg)).astype(BF16)

    tile = pl.BlockSpec((tm, tn), lambda i, j: (i, j))
    return pl.pallas_call(
        body, name=name, out_shape=[jax.ShapeDtypeStruct((s, f), BF16), jax.ShapeDtypeStruct((s, f), BF16)],
        grid=(s // tm, f // tn),
        in_specs=[pl.BlockSpec((tm, d), lambda i, j: (i, 0)), pl.BlockSpec((tn, d), lambda i, j: (j, 0)), tile, tile],
        out_specs=[tile, tile], compiler_params=_params(("parallel", "parallel")),
    )(df, wd, a, u)


def _merge_fwd(o_sb, o_dil, o_swa, gates, wb_sb, wb_dil, wb_swa, name):
    s, d = SEQ, D_MODEL
    tm = 256

    def body(osb_ref, odl_ref, osw_ref, g_ref, wsb_ref, wdl_ref, wsw_ref, m_ref, tsb_ref, tdl_ref, tsw_ref):
        for h in range(osb_ref.shape[0]):
            tsb_ref[:, h * HEAD_DIM:(h + 1) * HEAD_DIM] = osb_ref[h].astype(BF16)
        for h in range(osw_ref.shape[0]):
            tsw_ref[:, h * HEAD_DIM:(h + 1) * HEAD_DIM] = osw_ref[h].astype(BF16)
        tdl_ref[...] = odl_ref[...].astype(BF16)
        acc = _sigmoid(g_ref[:, 0:d]) * _dot(tsb_ref[...], wsb_ref[...], 1, 0)
        acc += _sigmoid(g_ref[:, d:2 * d]) * _dot(tdl_ref[...], wdl_ref[...], 1, 0)
        acc += _sigmoid(g_ref[:, 2 * d:3 * d]) * _dot(tsw_ref[...], wsw_ref[...], 1, 0)
        m_ref[...] = acc.astype(BF16)

    def rows(w):
        return pl.BlockSpec((tm, w), lambda i: (i, 0))

    def heads(n):
        return pl.BlockSpec((n, tm, HEAD_DIM), lambda i: (0, i, 0))

    def whole(w):
        return pl.BlockSpec((w, d), lambda i: (0, 0))

    return pl.pallas_call(
        body, name=name, out_shape=[jax.ShapeDtypeStruct((s, w), BF16) for w in (d, 256, 128, 384)], grid=(s // tm,),
        in_specs=[heads(H_SB), rows(128), heads(H_SWA_Q), rows(3 * d), whole(256), whole(128), whole(384)],
        out_specs=[rows(d), rows(256), rows(128), rows(384)], compiler_params=_params(("parallel",)),
    )(o_sb, o_dil, o_swa, gates, wb_sb, wb_dil, wb_swa)


def _merge_bwd(dmerged, t_sb, t_dil, t_swa, gates, wb_sb, wb_dil, wb_swa, name):
    s, d = SEQ, D_MODEL
    tm = 256

    def body(dm_ref, tsb_ref, tdl_ref, tsw_ref, g_ref, wsb_ref, wdl_ref, wsw_ref,
             dg_ref, dosb_ref, dodl_ref, dosw_ref, dbsb_ref, dbdl_ref, dbsw_ref):
        dm = dm_ref[...]
        for idx, (t_ref, w_ref, do_ref, db_ref) in enumerate((
                (tsb_ref, wsb_ref, dosb_ref, dbsb_ref), (tdl_ref, wdl_ref, dodl_ref, dbdl_ref),
                (tsw_ref, wsw_ref, dosw_ref, dbsw_ref))):
            w = w_ref[...]
            br = _dot(t_ref[...], w, 1, 0)
            sg = _sigmoid(g_ref[:, idx * d:(idx + 1) * d])
            dbr = (dm * sg).astype(BF16)
            dg_ref[:, idx * d:(idx + 1) * d] = (dm * br * (sg * (1.0 - sg))).astype(BF16)
            db_ref[...] = dbr
            do = _dot(dbr, w, 1, 1)
            if len(do_ref.shape) == 2:
                do_ref[...] = do
            else:
                for h in range(do_ref.shape[0]):
                    do_ref[h] = do[:, h * HEAD_DIM:(h + 1) * HEAD_DIM]

    def rows(w):
        return pl.BlockSpec((tm, w), lambda i: (i, 0))

    def heads(n):
        return pl.BlockSpec((n, tm, HEAD_DIM), lambda i: (0, i, 0))

    def whole(w):
        return pl.BlockSpec((w, d), lambda i: (0, 0))

    def shp(w, dt):
        return jax.ShapeDtypeStruct((s, w), dt)

    def hshp(n):
        return jax.ShapeDtypeStruct((n, s, HEAD_DIM), F32)

    return pl.pallas_call(
        body, name=name,
        out_shape=[shp(3 * d, BF16), hshp(H_SB), shp(128, F32), hshp(H_SWA_Q), shp(d, BF16), shp(d, BF16), shp(d, BF16)],
        grid=(s // tm,),
        in_specs=[rows(d), rows(256), rows(128), rows(384), rows(3 * d), whole(256), whole(128), whole(384)],
        out_specs=[rows(3 * d), heads(H_SB), rows(128), heads(H_SWA_Q), rows(d), rows(d), rows(d)],
        compiler_params=_params(("parallel",)),
    )(dmerged, t_sb, t_dil, t_swa, gates, wb_sb, wb_dil, wb_swa)


def _final_loss(x, target, g, name):
    s, d = x.shape
    tr = 256

    def body(x_ref, t_ref, g_ref, loss_ref, dx_ref, dg_ref):
        @pl.when(pl.program_id(0) == 0)
        def _():
            loss_ref[...] = jnp.zeros_like(loss_ref)
            dg_ref[...] = jnp.zeros_like(dg_ref)

        xv = x_ref[...]
        gv = g_ref[...]
        rstd = lax.rsqrt(jnp.mean(xv * xv, axis=-1, keepdims=True) + RMS_EPS)
        xhat = xv * rstd
        err = xhat * gv - t_ref[...]
        loss_ref[...] += 0.5 * jnp.sum(jnp.mean(err * err, axis=-1, keepdims=True))
        dy = err * (1.0 / d)
        dxhat = dy * gv
        mean_term = jnp.mean(dxhat * xhat, axis=-1, keepdims=True)
        dx_ref[...] = rstd * (dxhat - xhat * mean_term)
        dg_ref[...] += jnp.sum(dy * xhat, axis=0, keepdims=True)

    rowspec = pl.BlockSpec((1, d), lambda i: (0, 0))
    tile = pl.BlockSpec((tr, d), lambda i: (i, 0))
    return pl.pallas_call(
        body, name=name,
        out_shape=[jax.ShapeDtypeStruct((1, LANES), F32), jax.ShapeDtypeStruct((s, d), F32), jax.ShapeDtypeStruct((1, d), F32)],
        grid=(s // tr,), in_specs=[tile, tile, rowspec],
        out_specs=[pl.BlockSpec((1, LANES), lambda i: (0, 0)), tile, rowspec],
        compiler_params=_params(("arbitrary",)),
    )(x, target, g)


def _adamw(w, g, m, v, name, after=None):
    shape = w.shape
    cols = shape[-1]
    rows = int(np.prod(shape[:-1])) if len(shape) > 1 else 1
    tr = rows
    for cand in (1024, 512, 256, 128, 64, 32, 16, 8):
        if rows % cand == 0 and rows > cand and cand * cols * 4 <= (1 << 21):
            tr = cand
            break

    def body(w_ref, g_ref, m_ref, v_ref, *rest):
        d_ref, nm_ref, nv_ref = rest[-3:]
        d_ref[...], nm_ref[...], nv_ref[...] = _adam_update(w_ref[...], g_ref[...], m_ref[...], v_ref[...])

    tile = pl.BlockSpec((tr, cols), lambda i: (i, 0))
    flat = [t.reshape(rows, cols) for t in (w, g, m, v)] + ([] if after is None else [after])
    out = pl.pallas_call(
        body, name=name, out_shape=[jax.ShapeDtypeStruct((rows, cols), F32)] * 3, grid=(rows // tr,),
        in_specs=[tile] * 4 + [ANY] * (len(flat) - 4), out_specs=[tile] * 3, compiler_params=_params(("parallel",)),
    )(*flat)
    return tuple(t.reshape(shape) for t in out)


def _adam_update(w, gv, m, v):
    nm = ADAM_B1 * m + (1.0 - ADAM_B1) * gv
    nv = ADAM_B2 * v + (1.0 - ADAM_B2) * (gv * gv)
    m_hat = nm / (1.0 - ADAM_B1 ** ADAM_STEP)
    v_hat = nv / (1.0 - ADAM_B2 ** ADAM_STEP)
    return -ADAM_LR * (m_hat / (jnp.sqrt(v_hat) + ADAM_EPS) + ADAM_WD * w), nm, nv


def _reduce_adamw(groups, w, m, v, row0, prev, name, after=None):
    n, r, cdim = groups[0].shape
    rows = w.shape[0]
    tr = _row_tile(r, max(16, (1 << 22) // (n * cdim * groups[0].dtype.itemsize)))
    steps = r // tr
    ng = len(groups)

    def body(*refs):
        w_ref, m_ref, v_ref = refs[ng:ng + 3]
        g_out, d_out, m_out, v_out = refs[-4:]
        gg = pl.program_id(0)
        for gi in range(ng):
            @pl.when(gg == gi)
            def _(gi=gi):
                acc = refs[gi][0].astype(F32)
                for k in range(1, n):
                    acc = acc + refs[gi][k].astype(F32)
                g_out[...] = acc
                d_out[...], m_out[...], v_out[...] = _adam_update(w_ref[...], acc, m_ref[...], v_ref[...])

    def part_spec(gi):
        return pl.BlockSpec((n, tr, cdim), lambda gg, i: (0, jnp.where(gg == gi, i, 0), 0))

    tile = pl.BlockSpec((tr, cdim), lambda gg, i: (row0 // tr + gg * steps + i, 0))
    extra = ([] if prev is None else list(prev)) + ([] if after is None else [after])
    return pl.pallas_call(
        body, name=name, out_shape=[jax.ShapeDtypeStruct((rows, cdim), F32)] * 4, grid=(ng, steps),
        in_specs=[part_spec(gi) for gi in range(ng)] + [tile] * 3 + [ANY] * len(extra), out_specs=[tile] * 4,
        input_output_aliases={} if prev is None else {ng + 3 + k: k for k in range(4)},
        compiler_params=_params(("parallel", "parallel")),
    )(*groups, w, m, v, *extra)


def _ada_fwd(c_all, w, name):
    n = w.shape[1]

    def body(c_ref, w_ref, o_ref):
        cv = c_ref[...]
        o_ref[...] = jnp.dot(cv * _sigmoid(cv), w_ref[...], preferred_element_type=F32, precision=lax.Precision.HIGHEST)

    return pl.pallas_call(body, name=name, out_shape=jax.ShapeDtypeStruct((N_DEV, n), F32), compiler_params=_params())(c_all, w)


def _ada_bwd(c_all_t, dmod, name):
    n = dmod.shape[1]

    def body(c_ref, d_ref, o_ref):
        cv = c_ref[...]
        o_ref[...] = jnp.dot(cv * _sigmoid(cv), d_ref[...], preferred_element_type=F32, precision=lax.Precision.HIGHEST)

    return pl.pallas_call(body, name=name, out_shape=jax.ShapeDtypeStruct((D_MODEL, n), F32), compiler_params=_params())(c_all_t, dmod)


def _bucket_tables():
    rel = np.arange(BLK)[:, None] + BLK - np.arange(2 * BLK)[None, :]
    max_exact = N_BUCKETS // 2

    def bucket(n):
        nf = np.maximum(n, 1).astype(np.float32)
        large = max_exact + (np.log(nf / np.float32(max_exact)) / np.float32(math.log(MAX_REL_DIST / max_exact))
                             * np.float32(N_BUCKETS - max_exact)).astype(np.int32)
        return np.where(n < max_exact, n, np.minimum(large, N_BUCKETS - 1))

    tabs = []
    for dil, max_dist in ((1, 128), (4, 128), (16, 128), (1, SWA_WINDOW - 1)):
        in_band = (rel >= 0) & (rel <= max_dist)
        tabs.append(np.where(in_band, bucket(np.maximum(rel, 0) * dil), -1))
    return np.stack(tabs).astype(np.int32)


N_SOFT = H_DIL + H_SWA_Q


def _table_of_head(h):
    return jnp.minimum(h // 2, 3)


def _bias_build(rel_bias, tables, name):
    def body(rel_ref, t_ref, o_ref):
        h = pl.program_id(0)
        tb = t_ref[0]
        out = jnp.full((BLK, 2 * BLK), NEG, F32)
        for b in range(N_BUCKETS):
            out = jnp.where(tb == b, rel_ref[b, h], out)
        o_ref[0] = out

    return pl.pallas_call(
        body, name=name, out_shape=jax.ShapeDtypeStruct((N_SOFT, BLK, 2 * BLK), F32), grid=(N_SOFT,),
        in_specs=[pl.BlockSpec(memory_space=pltpu.SMEM),
                  pl.BlockSpec((1, BLK, 2 * BLK), lambda h: (_table_of_head(h), 0, 0))],
        out_specs=pl.BlockSpec((1, BLK, 2 * BLK), lambda h: (h, 0, 0)),
        compiler_params=_params(("parallel",)),
    )(rel_bias, tables)


def _bias_grad(dbias, tables, name):
    def body(d_ref, t_ref, o_ref):
        tb = t_ref[0]
        dv = d_ref[0]
        lane = lax.broadcasted_iota(jnp.int32, (1, LANES), 1)
        out = jnp.zeros((1, LANES), F32)
        for b in range(N_BUCKETS):
            out = jnp.where(lane == b, jnp.sum(jnp.where(tb == b, dv, 0.0)), out)
        o_ref[0] = out

    return pl.pallas_call(
        body, name=name, out_shape=jax.ShapeDtypeStruct((N_SOFT, 1, LANES), F32), grid=(N_SOFT,),
        in_specs=[pl.BlockSpec((1, BLK, 2 * BLK), lambda h: (h, 0, 0)),
                  pl.BlockSpec((1, BLK, 2 * BLK), lambda h: (_table_of_head(h), 0, 0))],
        out_specs=pl.BlockSpec((1, 1, LANES), lambda h: (h, 0, 0)),
        compiler_params=_params(("parallel",)),
    )(dbias, tables)


def _band_layout(g, bias_div):
    assert g == 1 or bias_div == 1
    return bias_div if g == 1 else 1


def _band_specs(length, g, bias_div, offs):
    ns = _band_layout(g, bias_div)

    def seqs(off, div=1):
        return pl.BlockSpec((ns, length, HEAD_DIM), lambda s: (off // ns + s // div, 0, 0))

    xspecs = [seqs(offs[0]), seqs(offs[1], g), seqs(offs[2], g)]
    bspec = pl.BlockSpec((1, BLK, 2 * BLK), lambda s: (s, 0, 0))
    sspec = pl.BlockSpec((ns, 1, LANES), lambda s: (s, 0, 0))
    colspec = pl.BlockSpec((ns, length, 1), lambda s: (s, 0, 0))
    return xspecs, seqs(0), seqs(0, g), bspec, sspec, colspec


def _band_sweep(length, ns, one):
    nblk = length // BLK
    for qq in range(ns):
        if ns * nblk <= 16:
            for i in range(nblk):
                one(qq, i * BLK, max(i - 1, 0) * BLK, i == 0)
        else:
            def step(i, carry, qq=qq):
                one(qq, pl.multiple_of(i * BLK, BLK), pl.multiple_of(jnp.maximum(i - 1, 0) * BLK, BLK), i == 0)
                return carry

            lax.fori_loop(0, nblk, step, 0, unroll=2)


def _band_scores(q_ref, k_ref, b_ref, qq, kq, bq, cur, prv, first):
    qv = q_ref[qq, pl.ds(cur, BLK), :]
    bv = b_ref[bq]
    if first is True:
        sp = jnp.full((BLK, BLK), NEG, F32)
    else:
        sp = _dot(qv, k_ref[kq, pl.ds(prv, BLK), :], 1, 1) + bv[:, :BLK]
        sp = sp if first is False else jnp.where(first, NEG, sp)
    sc = _dot(qv, k_ref[kq, pl.ds(cur, BLK), :], 1, 1) + bv[:, BLK:]
    return qv, sp, sc


def _band_fwd(x, bias, sink, *, nq, offs, g, bias_div, has_sink, name):
    length = x.shape[1]
    ns = _band_layout(g, bias_div)

    def body(q_ref, k_ref, v_ref, b_ref, s_ref, o_ref, lse_ref):
        def one(qq, cur, prv, first):
            kq, bq = qq, 0
            _, sp, sc = _band_scores(q_ref, k_ref, b_ref, qq, kq, bq, cur, prv, first)
            m = jnp.maximum(jnp.max(sp, axis=1, keepdims=True), jnp.max(sc, axis=1, keepdims=True))
            if has_sink:
                sk = s_ref[qq][:, :1]
                m = jnp.maximum(m, sk)
            pp, pc = jnp.exp(sp - m), jnp.exp(sc - m)
            den = jnp.sum(pp, axis=1, keepdims=True) + jnp.sum(pc, axis=1, keepdims=True)
            if has_sink:
                den = den + jnp.exp(sk - m)
            acc = (_dot(pp.astype(BF16), v_ref[kq, pl.ds(prv, BLK), :], 1, 0)
                   + _dot(pc.astype(BF16), v_ref[kq, pl.ds(cur, BLK), :], 1, 0))
            o_ref[qq, pl.ds(cur, BLK), :] = acc / den
            lse_ref[qq, pl.ds(cur, BLK), :] = m + jnp.log(den)

        _band_sweep(length, ns, one)

    xspecs, qspec, _, bspec, sspec, colspec = _band_specs(length, g, bias_div, offs)
    return pl.pallas_call(
        body, name=name,
        out_shape=[jax.ShapeDtypeStruct((nq, length, HEAD_DIM), F32), jax.ShapeDtypeStruct((nq, length, 1), F32)],
        grid=(nq // ns,), in_specs=xspecs + [bspec, sspec],
        out_specs=[qspec, colspec], compiler_params=_params(("parallel",)),
    )(x, x, x, bias, sink)


def _band_bwd(x, bias, sink, o, lse, do, dlse, *, nq, offs, g, bias_div, has_sink, name):
    length = x.shape[1]
    ns = _band_layout(g, bias_div)
    nk, nbias = nq // g, nq // bias_div

    def body(q_ref, k_ref, v_ref, b_ref, s_ref, o_ref, lse_ref, do_ref, dlse_ref,
             dq_ref, dk_ref, dv_ref, db_ref, dsk_ref, dkp_ref, dvp_ref):
        for ref in (db_ref, dsk_ref, dkp_ref, dvp_ref):
            ref[...] = jnp.zeros_like(ref)

        @pl.when(pl.program_id(0) % g == 0)
        def _():
            dk_ref[...] = jnp.zeros_like(dk_ref)
            dv_ref[...] = jnp.zeros_like(dv_ref)

        def one(qq, cur, prv, first):
            kq, bq = qq, 0
            qv, sp, sc = _band_scores(q_ref, k_ref, b_ref, qq, kq, bq, cur, prv, first)
            rows, prow = pl.ds(cur, BLK), pl.ds(prv, BLK)
            lse_v = lse_ref[qq, rows, :]
            pp, pc = jnp.exp(sp - lse_v), jnp.exp(sc - lse_v)
            dov = do_ref[qq, rows, :]
            dob = dov.astype(BF16)
            coef = dlse_ref[qq, rows, :] - jnp.sum(dov * o_ref[qq, rows, :], axis=1, keepdims=True)
            dsp = pp * (_dot(dob, v_ref[kq, prow, :], 1, 1) + coef)
            dsc = pc * (_dot(dob, v_ref[kq, rows, :], 1, 1) + coef)
            dspb, dscb = dsp.astype(BF16), dsc.astype(BF16)
            dq_ref[qq, rows, :] = ((_dot(dspb, k_ref[kq, prow, :], 1, 0) + _dot(dscb, k_ref[kq, rows, :], 1, 0))
                                   * (HEAD_DIM ** -0.5))
            dk_ref[kq, rows, :] += _dot(dscb, qv, 0, 0)
            dkp_ref[kq, prow, :] += _dot(dspb, qv, 0, 0)
            dv_ref[kq, rows, :] += _dot(pc.astype(BF16), dob, 0, 0)
            dvp_ref[kq, prow, :] += _dot(pp.astype(BF16), dob, 0, 0)
            db_ref[bq, :, :BLK] += dsp
            db_ref[bq, :, BLK:] += dsc
            if has_sink:
                dsk_ref[qq] += jnp.sum(jnp.exp(s_ref[qq][:, :1] - lse_v) * coef)

        _band_sweep(length, ns, one)
        dk_ref[...] += dkp_ref[...]
        dv_ref[...] += dvp_ref[...]

    xspecs, qspec, kvspec, bspec, sspec, colspec = _band_specs(length, g, bias_div, offs)
    return pl.pallas_call(
        body, name=name,
        out_shape=[jax.ShapeDtypeStruct((nq, length, HEAD_DIM), F32), jax.ShapeDtypeStruct((nk, length, HEAD_DIM), F32),
                   jax.ShapeDtypeStruct((nk, length, HEAD_DIM), F32), jax.ShapeDtypeStruct((nbias, BLK, 2 * BLK), F32),
                   jax.ShapeDtypeStruct((nq, 1, LANES), F32)],
        grid=(nq // ns,),
        in_specs=xspecs + [bspec, sspec, qspec, colspec, qspec, colspec],
        out_specs=[qspec, kvspec, kvspec, bspec, sspec],
        scratch_shapes=[pltpu.VMEM((ns, length, HEAD_DIM), F32), pltpu.VMEM((ns, length, HEAD_DIM), F32)],
        compiler_params=_params(("arbitrary",)),
    )(x, x, x, bias, sink, o, lse, do, dlse)


TOK_TILE = 512


def _dil_merge(outs, lses, dout, name):
    tr = TOK_TILE
    dils = [d for _, d in DIL_PATTERNS]
    n = len(dils)
    o4 = [o.reshape(2, d, SEQ // d, HEAD_DIM) for o, d in zip(outs, dils)]
    l4 = [l.reshape(2, d, SEQ // d, 1) for l, d in zip(lses, dils)]
    o_specs = [pl.BlockSpec((2, d, tr // d, HEAD_DIM), lambda i: (0, 0, i, 0)) for d in dils]
    l_specs = [pl.BlockSpec((2, d, tr // d, 1), lambda i: (0, 0, i, 0)) for d in dils]
    tok = pl.BlockSpec((tr, 2 * HEAD_DIM), lambda i: (i, 0))
    scratch = ([pltpu.VMEM((tr, 2 * HEAD_DIM), F32) for _ in dils] + [pltpu.VMEM((tr, 1), F32) for _ in range(2 * n)]
               + [pltpu.VMEM((tr // d, 2 * HEAD_DIM), F32) for d in dils])

    def to_tokens(o_ref, l_ref, d, pair, cols, stage):
        for r in range(d):
            rows = pl.ds(r, tr // d, stride=d) if d > 1 else slice(None)
            stage[:, :HEAD_DIM] = o_ref[0, r]
            stage[:, HEAD_DIM:] = o_ref[1, r]
            pair[rows, :] = stage[...]
            for h in range(2):
                cols[h][rows, :] = l_ref[h, r]
        return pair[...], [cols[0][...], cols[1][...]]

    def weights(ls):
        left = lax.broadcasted_iota(jnp.int32, (tr, 2 * HEAD_DIM), 1) < HEAD_DIM
        per_head = []
        for h in range(2):
            m = ls[0][h]
            for g in range(1, n):
                m = jnp.maximum(m, ls[g][h])
            es = [jnp.exp(ls[g][h] - m) for g in range(n)]
            den = es[0]
            for e in es[1:]:
                den = den + e
            per_head.append([e / den for e in es])
        return per_head, [jnp.where(left, per_head[0][g], per_head[1][g]) for g in range(n)], left

    def load(refs):
        pairs, cols, stages = refs[:n], refs[n:3 * n], refs[3 * n:]
        return pairs, [cols[2 * g:2 * g + 2] for g in range(n)], stages

    if dout is None:
        def body(*refs):
            pairs, cols, stages = load(refs[2 * n + 1:])
            toks = [to_tokens(refs[g], refs[n + g], dils[g], pairs[g], cols[g], stages[g]) for g in range(n)]
            _, alphas, _ = weights([t[1] for t in toks])
            acc = alphas[0] * toks[0][0]
            for g in range(1, n):
                acc = acc + alphas[g] * toks[g][0]
            refs[2 * n][...] = acc

        return pl.pallas_call(
            body, name=name, out_shape=jax.ShapeDtypeStruct((SEQ, 2 * HEAD_DIM), F32), grid=(SEQ // tr,),
            in_specs=o_specs + l_specs, out_specs=tok, scratch_shapes=scratch, compiler_params=_params(("parallel",)),
        )(*o4, *l4)

    def body(*refs):
        do_refs, dl_refs = refs[2 * n + 1:3 * n + 1], refs[3 * n + 1:4 * n + 1]
        pairs, cols, stages = load(refs[4 * n + 1:])
        toks = [to_tokens(refs[g], refs[n + g], dils[g], pairs[g], cols[g], stages[g]) for g in range(n)]
        per_head, alphas, left = weights([t[1] for t in toks])
        dov = refs[2 * n][...]
        das = []
        for g in range(n):
            prod = dov * toks[g][0]
            das.append([jnp.sum(jnp.where(left, prod, 0.0), axis=1, keepdims=True),
                        jnp.sum(jnp.where(left, 0.0, prod), axis=1, keepdims=True)])
        dbar = [sum(per_head[h][g] * das[g][h] for g in range(n)) for h in range(2)]
        for g, d in enumerate(dils):
            pairs[g][...] = alphas[g] * dov
            for h in range(2):
                cols[g][h][...] = per_head[h][g] * (das[g][h] - dbar[h])
            for r in range(d):
                rows = pl.ds(r, tr // d, stride=d) if d > 1 else slice(None)
                v = pairs[g][rows, :]
                for h in range(2):
                    do_refs[g][h, r] = v[:, h * HEAD_DIM:(h + 1) * HEAD_DIM]
                    dl_refs[g][h, r] = cols[g][h][rows, :]

    out = pl.pallas_call(
        body, name=name,
        out_shape=[jax.ShapeDtypeStruct(o.shape, F32) for o in o4] + [jax.ShapeDtypeStruct(l.shape, F32) for l in l4],
        grid=(SEQ // tr,), in_specs=o_specs + l_specs + [tok], out_specs=o_specs + l_specs, scratch_shapes=scratch,
        compiler_params=_params(("parallel",)),
    )(*o4, *l4, dout)
    return [t.reshape(s.shape) for t, s in zip(out, list(outs) + list(lses))]


def _tri(cmp):
    r = lax.broadcasted_iota(jnp.int32, (SB_TILE, SB_TILE), 0)
    c = lax.broadcasted_iota(jnp.int32, (SB_TILE, SB_TILE), 1)
    return cmp(r, c).astype(BF16)


def _cum(x, tri, terms):
    acc, rest = None, x
    for _ in range(terms):
        part = rest.astype(BF16)
        rest = rest - part.astype(F32)
        d = _dot(part, tri, 1, 0)
        acc = d if acc is None else acc + d
    return acc


def _sb_logits(q, ks, diagonal):
    t = SB_TILE
    z = _dot(q, ks, 1, 1)
    e = jnp.exp(-jnp.abs(z))
    lf = -(jnp.maximum(z, 0.0) + jnp.log(1.0 + e))
    if not diagonal:
        return z, e, lf, None
    mask = lax.broadcasted_iota(jnp.int32, (t, t), 1) < lax.broadcasted_iota(jnp.int32, (t, t), 0)
    return z, e, jnp.where(mask, lf, 0.0), mask


def _sb_specs(h, s):
    t = SB_TILE
    tile = pl.BlockSpec((h, t, HEAD_DIM), lambda i: (0, i, 0))
    keys = pl.BlockSpec((h, s, HEAD_DIM), lambda i: (1, 0, 0))
    values = pl.BlockSpec((h, s, HEAD_DIM), lambda i: (2, 0, 0))
    return tile, keys, values, pl.BlockSpec((h, t, 1), lambda i: (0, i, 0))


def _sb_fwd(x, name):
    h, s = x.shape[0] // 3, x.shape[1]
    t = SB_TILE

    def body(q_ref, k_ref, v_ref, o_ref, tot_ref):
        i = pl.program_id(0)
        after = _tri(lambda r, c: r > c)

        def tile(j, carry, diagonal):
            rows = pl.ds(pl.multiple_of(j * t, t), t)
            out = []
            for hh, (right, acc) in enumerate(carry):
                z, _, lf, mask = _sb_logits(q_ref[hh], k_ref[hh, rows, :], diagonal)
                w = jnp.exp(z + lf + (right + _cum(lf, after, 2)))
                w = w if mask is None else jnp.where(mask, w, 0.0)
                out.append((right + jnp.sum(lf, axis=1, keepdims=True), acc + _dot(w.astype(BF16), v_ref[hh, rows, :], 1, 0)))
            return tuple(out)

        carry = tile(i, tuple((jnp.zeros((t, 1), F32), jnp.zeros((t, HEAD_DIM), F32)) for _ in range(h)), True)
        carry = lax.fori_loop(0, i, lambda jj, c: tile(i - 1 - jj, c, False), carry)
        for hh, (right, acc) in enumerate(carry):
            o_ref[hh] = acc
            tot_ref[hh] = right

    tile_spec, keys, values, col = _sb_specs(h, s)
    return pl.pallas_call(
        body, name=name, out_shape=[jax.ShapeDtypeStruct((h, s, HEAD_DIM), F32), jax.ShapeDtypeStruct((h, s, 1), F32)],
        grid=(s // t,), in_specs=[tile_spec, keys, values], out_specs=[tile_spec, col],
        compiler_params=_params(("parallel",)),
    )(x, x, x)


def _sb_bwd(x, tot, do, name):
    h, s = x.shape[0] // 3, x.shape[1]
    t = SB_TILE

    def body(q_ref, k_ref, v_ref, tot_ref, do_ref, dq_ref, dk_ref, dv_ref):
        i = pl.program_id(0)

        @pl.when(i == 0)
        def _():
            dk_ref[...] = jnp.zeros_like(dk_ref)
            dv_ref[...] = jnp.zeros_like(dv_ref)

        upto = _tri(lambda r, c: r <= c)
        before = _tri(lambda r, c: r < c)

        def tile(j, carry, diagonal):
            rows = pl.ds(pl.multiple_of(j * t, t), t)
            out = []
            for hh, (left, cleft, dq) in enumerate(carry):
                qv, ks, dob = q_ref[hh], k_ref[hh, rows, :], do_ref[hh].astype(BF16)
                z, e, lf, mask = _sb_logits(qv, ks, diagonal)
                between = tot_ref[hh] - (left + _cum(lf, upto, 2))
                w = jnp.exp(z + lf + between)
                w = w if mask is None else jnp.where(mask, w, 0.0)
                dlog = w * _dot(dob, v_ref[hh, rows, :], 1, 1)
                cfail = cleft + _cum(dlog, before, 2)
                sig = jnp.where(z >= 0.0, 1.0, e) / (1.0 + e)
                dz = dlog * (1.0 - sig) - sig * cfail
                dz = (dz if mask is None else jnp.where(mask, dz, 0.0)).astype(BF16)
                dk_ref[hh, rows, :] += _dot(dz, qv, 0, 0)
                dv_ref[hh, rows, :] += _dot(w.astype(BF16), dob, 0, 0)
                out.append((left + jnp.sum(lf, axis=1, keepdims=True), cleft + jnp.sum(dlog, axis=1, keepdims=True),
                            dq + _dot(dz, ks, 1, 0)))
            return tuple(out)

        zero = jnp.zeros((t, 1), F32)
        carry = lax.fori_loop(0, i, lambda j, c: tile(j, c, False),
                              tuple((zero, zero, jnp.zeros((t, HEAD_DIM), F32)) for _ in range(h)))
        for hh, (_, _, dq) in enumerate(tile(i, carry, True)):
            dq_ref[hh] = dq * (HEAD_DIM ** -0.5)

    tile_spec, keys, values, col = _sb_specs(h, s)
    full = pl.BlockSpec((h, s, HEAD_DIM), lambda i: (0, 0, 0))
    shp = jax.ShapeDtypeStruct((h, s, HEAD_DIM), F32)
    return pl.pallas_call(
        body, name=name, out_shape=[shp, shp, shp], grid=(s // t,),
        in_specs=[tile_spec, keys, values, col, tile_spec],
        out_specs=[tile_spec, full, full], compiler_params=_params(("arbitrary",)),
    )(x, x, x, tot, do)


COL_SB, COL_DIL, COL_SWA = 0, 3 * H_SB * HEAD_DIM, 3 * H_SB * HEAD_DIM + 3 * H_DIL * HEAD_DIM
N_SWA = H_SWA_Q + 2 * H_SWA_KV


def _dil_col(t, g):
    return COL_DIL + t * H_DIL * HEAD_DIM + g * 2 * HEAD_DIM


def _split_heads(qkv, name):
    tr = TOK_TILE
    scale = HEAD_DIM ** -0.5
    dils = [d for _, d in DIL_PATTERNS]

    def body(x_ref, sb_ref, d0_ref, d1_ref, d2_ref, swa_ref, pair):
        def head(col, scaled):
            v = x_ref[:, col:col + HEAD_DIM]
            return (v * scale if scaled else v).astype(BF16)

        for hh in range(3 * H_SB):
            sb_ref[hh] = head(COL_SB + hh * HEAD_DIM, hh < H_SB)
        for hh in range(N_SWA):
            swa_ref[hh] = head(COL_SWA + hh * HEAD_DIM, hh < H_SWA_Q)
        for t in range(3):
            for g, (d, out_ref) in enumerate(zip(dils, (d0_ref, d1_ref, d2_ref))):
                col = _dil_col(t, g)
                if d == 1:
                    for h in range(2):
                        out_ref[t * 2 + h] = head(col + h * HEAD_DIM, t == 0)
                    continue
                pair[...] = x_ref[:, col:col + 2 * HEAD_DIM]
                for r in range(d):
                    v = pair[pl.ds(r, tr // d, stride=d), :]
                    v = v * scale if t == 0 else v
                    for h in range(2):
                        out_ref[t * 2 * d + h * d + r] = v[:, h * HEAD_DIM:(h + 1) * HEAD_DIM].astype(BF16)

    def heads(n, length):
        return jax.ShapeDtypeStruct((n, length, HEAD_DIM), BF16)

    def spec(n, rows):
        return pl.BlockSpec((n, rows, HEAD_DIM), lambda i: (0, i, 0))

    return pl.pallas_call(
        body, name=name,
        out_shape=[heads(3 * H_SB, SEQ)] + [heads(6 * d, SEQ // d) for d in dils] + [heads(N_SWA, SEQ)],
        grid=(SEQ // tr,), in_specs=[pl.BlockSpec((tr, D_QKV), lambda i: (i, 0))],
        out_specs=[spec(3 * H_SB, tr)] + [spec(6 * d, tr // d) for d in dils] + [spec(N_SWA, tr)],
        scratch_shapes=[pltpu.VMEM((tr, 2 * HEAD_DIM), F32)], compiler_params=_params(("parallel",)),
    )(qkv)


def _join_heads(sb, dil, swa, name):
    tr = TOK_TILE
    dils = [d for _, d in DIL_PATTERNS]

    def body(*refs):
        sb_refs, dil_refs, swa_refs = refs[:3], [refs[3 + 3 * g:6 + 3 * g] for g in range(3)], refs[12:15]
        o_ref, pair, stages = refs[15], refs[16], refs[17:]

        def put(col, v):
            o_ref[:, col:col + v.shape[1]] = v.astype(BF16)

        for t in range(3):
            for h in range(H_SB):
                put(COL_SB + (t * H_SB + h) * HEAD_DIM, sb_refs[t][h])
        col = COL_SWA
        for ref in swa_refs:
            for h in range(ref.shape[0]):
                put(col, ref[h])
                col += HEAD_DIM
        for t in range(3):
            for g, d in enumerate(dils):
                ref, col = dil_refs[g][t], _dil_col(t, g)
                if d == 1:
                    for h in range(2):
                        put(col + h * HEAD_DIM, ref[h])
                    continue
                stage = stages[g - 1]
                for r in range(d):
                    stage[:, :HEAD_DIM] = ref[r]
                    stage[:, HEAD_DIM:] = ref[d + r]
                    pair[pl.ds(r, tr // d, stride=d), :] = stage[...]
                put(col, pair[...])

    def spec(n, rows):
        return pl.BlockSpec((n, rows, HEAD_DIM), lambda i: (0, i, 0))

    ins = list(sb) + [t for g in range(3) for t in dil[g]] + list(swa)
    in_specs = ([spec(H_SB, tr)] * 3 + [spec(2 * d, tr // d) for d in dils for _ in range(3)]
                + [spec(H_SWA_Q, tr), spec(H_SWA_KV, tr), spec(H_SWA_KV, tr)])
    return pl.pallas_call(
        body, name=name, out_shape=jax.ShapeDtypeStruct((SEQ, D_QKV), BF16), grid=(SEQ // tr,), in_specs=in_specs,
        out_specs=pl.BlockSpec((tr, D_QKV), lambda i: (i, 0)),
        scratch_shapes=[pltpu.VMEM((tr, 2 * HEAD_DIM), F32)] + [pltpu.VMEM((tr // d, 2 * HEAD_DIM), F32) for d in dils[1:]],
        compiler_params=_params(("parallel",)),
    )(*ins)


def _mixer_fwd(qkv, bias, sinks_l, tag):
    sb, d0, d1, d2, swa = _split_heads(qkv, name=f"split_heads_{tag}")
    st = {"sb": sb, "dil": (d0, d1, d2), "swa": swa}
    o_sb, st["sb_tot"] = _sb_fwd(sb, name=f"sb_fwd_{tag}")
    st["dil_out"], st["dil_lse"], st["dil_sink"] = [], [], []
    for gi, (_, d) in enumerate(DIL_PATTERNS):
        sink = jnp.zeros((2 * d, 1, LANES), F32)
        og, lg = _band_fwd(st["dil"][gi], bias[2 * gi:2 * gi + 2], sink, nq=2 * d, offs=(0, 2 * d, 4 * d), g=1, bias_div=d,
                           has_sink=False, name=f"dil{gi}_fwd_{tag}")
        st["dil_out"].append(og)
        st["dil_lse"].append(lg)
        st["dil_sink"].append(sink)
    o_dil = _dil_merge(st["dil_out"], st["dil_lse"], None, name=f"dil_merge_fwd_{tag}")
    st["swa_sink"] = jnp.broadcast_to(sinks_l.reshape(H_SWA_Q, 1, 1), (H_SWA_Q, 1, LANES))
    st["swa_out"] = _band_fwd(swa, bias[H_DIL:], st["swa_sink"], nq=H_SWA_Q, offs=(0, H_SWA_Q, H_SWA_Q + H_SWA_KV),
                              g=H_SWA_Q // H_SWA_KV, bias_div=1, has_sink=True, name=f"swa_fwd_{tag}")
    return (o_sb, o_dil, st["swa_out"][0]), st


def _mixer_bwd(st, bias, do_sb, do_dil, do_swa, tag):
    d_sb = _sb_bwd(st["sb"], st["sb_tot"], do_sb, name=f"sb_bwd_{tag}")
    dmerge = _dil_merge(st["dil_out"], st["dil_lse"], do_dil, name=f"dil_merge_bwd_{tag}")
    d_dil, dbs = [], []
    for gi, (_, d) in enumerate(DIL_PATTERNS):
        dq, dk, dv, db, _ = _band_bwd(st["dil"][gi], bias[2 * gi:2 * gi + 2], st["dil_sink"][gi], st["dil_out"][gi],
                                      st["dil_lse"][gi], dmerge[gi], dmerge[3 + gi], nq=2 * d, offs=(0, 2 * d, 4 * d),
                                      g=1, bias_div=d, has_sink=False, name=f"dil{gi}_bwd_{tag}")
        d_dil.append((dq, dk, dv))
        dbs.append(db)
    o_sw, l_sw = st["swa_out"]
    dq_sw, dk_sw, dv_sw, db_sw, dsink = _band_bwd(st["swa"], bias[H_DIL:], st["swa_sink"], o_sw, l_sw, do_swa,
                                                  jnp.zeros_like(l_sw), nq=H_SWA_Q, offs=(0, H_SWA_Q, H_SWA_Q + H_SWA_KV),
                                                  g=H_SWA_Q // H_SWA_KV, bias_div=1, has_sink=True, name=f"swa_bwd_{tag}")
    dqkv = _join_heads(d_sb, d_dil, (dq_sw, dk_sw, dv_sw), name=f"join_heads_{tag}")
    return dqkv, jnp.concatenate(dbs + [db_sw], 0), dsink[:, 0, 0]


PIECES = ("ffn0", "mix", "ffn1")


def _ffn_fwd(x_in, w, gain, mod_j, tag, after=None):
    st = {"x": x_in, "w": w}
    st["h"] = _norm_fwd(x_in, _row(gain), _row(mod_j[1]), _row(mod_j[0]), name=f"norm_fwd_{tag}", after=after)
    st["a"], st["u"], st["s"] = _ffn_up(st["h"], w["gate"], w["up"], name=f"up_{tag}")
    st["f"], x_out = _mm(st["s"], w["down"], res=x_in, colscale=_row(0.5 * mod_j[2]), emit_acc=True, tm=512, tn=1024,
                         name=f"down_{tag}")
    return x_out, st


def _ffn_bwd(dx_out, st, gain, mod_j, tag, done, pre, nxt):
    w = st["w"]

    def latest(new, old):
        return old if new is None else new

    df, dgate = pre or _gate_bwd(dx_out, st["f"], _row(0.5 * mod_j[2]), 0.5, name=f"gate_bwd_{tag}")
    dwd = _mm_tn(st["s"], df, tm=D_FF // 2, name=f"dwd_{tag}")
    token = latest(done({"down": dwd}), dwd)
    da, du = _ffn_bwd_ds(df, w["down"], st["a"], st["u"], name=f"ds_{tag}")
    dwg = _mm_tn(da, st["h"], after=token, tm=D_FF // 2, name=f"dwg_{tag}")
    token = latest(done({"gate": dwg}), dwg)
    dwu = _mm_tn(du, st["h"], after=token, tm=D_FF // 2, name=f"dwu_{tag}")
    token = latest(done({"up": dwu}), dwu)
    dx_in, sum_dh, sum_dhx, made = _dh_norm_bwd(da, w["gate"], du, w["up"], st["x"], dx_out, _row(gain), _row(mod_j[1]), nxt,
                                                after=token, name=f"dh_{tag}")
    dmod = jnp.concatenate([sum_dh, gain * sum_dhx, dgate], 0)
    return dx_in, dmod, (1.0 + mod_j[1]) * sum_dhx[0], made


def _mix_fwd(x_in, w, gain, mod_j, bias, sinks_l, tag, after=None):
    st = {"x": x_in, "w": w}
    st["h"] = _norm_fwd(x_in, _row(gain), _row(mod_j[1]), _row(mod_j[0]), name=f"norm_fwd_mix_{tag}", after=after)
    qkv = _mm(st["h"], w["in"], tb=True, tm=SEQ, b_rows=(0, D_QKV), name=f"qkv_{tag}")
    st["gates"] = _mm(st["h"], w["in"], tb=True, tm=SEQ, b_rows=(D_QKV, D_GATES), name=f"gates_{tag}")
    outs, st["mix"] = _mixer_fwd(qkv, bias, sinks_l, tag)
    st["merged"], *st["t"] = _merge_fwd(*outs, st["gates"], w["br_sb"], w["br_dil"], w["br_swa"], name=f"merge_fwd_{tag}")
    st["f"], x_out = _mm(st["merged"], w["out"], res=x_in, colscale=_row(mod_j[2]), emit_acc=True, name=f"out_{tag}")
    return x_out, st


def _mix_bwd(dx_out, st, gain, mod_j, bias, tag, done, pre, nxt):
    w = st["w"]
    df, dgate = pre or _gate_bwd(dx_out, st["f"], _row(mod_j[2]), 1.0, name=f"gate_bwd_mix_{tag}")
    g = {"out": _mm_tn(st["merged"], df, name=f"dw_out_{tag}")}
    dmerged = _mm(df, w["out"], tb=True, name=f"dmerged_{tag}")
    dgates, do_sb, do_dil, do_swa, dbr_sb, dbr_dil, dbr_swa = _merge_bwd(
        dmerged, *st["t"], st["gates"], w["br_sb"], w["br_dil"], w["br_swa"], name=f"merge_bwd_{tag}")
    g["br_sb"] = _mm_tn(st["t"][0], dbr_sb, name=f"dw_br_sb_{tag}")
    g["br_dil"] = _mm_tn(st["t"][1], dbr_dil, name=f"dw_br_dil_{tag}")
    g["br_swa"] = _mm_tn(st["t"][2], dbr_swa, name=f"dw_br_swa_{tag}")
    dqkv, dbias, dsinks = _mixer_bwd(st["mix"], bias, do_sb, do_dil, do_swa, tag)
    dw_qkv = _mm_tn(dqkv, st["h"], out_rows=D_QKV + D_GATES, name=f"dw_qkv_{tag}")
    g["in"] = _mm_tn(dgates, st["h"], out_rows=D_QKV + D_GATES, row0=D_QKV, prev=dw_qkv, name=f"dw_gates_{tag}")
    dx_in, sum_dh, sum_dhx, made = _dh_norm_bwd(dqkv, w["in"], dgates, w["in"], st["x"], dx_out, _row(gain), _row(mod_j[1]),
                                                nxt, after=done(g), b_rows=(0, D_QKV), name=f"dh_mix_{tag}")
    dmod = jnp.concatenate([sum_dh, gain * sum_dhx, dgate], 0)
    return dx_in, dmod, (1.0 + mod_j[1]) * sum_dhx[0], dbias, dsinks, made


def _local_step(x, target, mod, gains, weights_of, rel_bias, sinks, final_gain, grads_done):
    tables = jnp.asarray(_bucket_tables())
    bias = _bias_build(rel_bias, tables, name="bias_build")
    states, h = [], x
    for l in range(DEPTH):
        st = {}
        for j, piece in enumerate(PIECES):
            w, after = weights_of(l, piece, h)
            if piece == "mix":
                h, st[piece] = _mix_fwd(h, w, gains[l, j], mod[l, j], bias, sinks[l], f"l{l}", after)
            else:
                h, st[piece] = _ffn_fwd(h, w, gains[l, j], mod[l, j], f"{piece}_l{l}", after)
        states.append(st)
    loss, dx, dfinal = _final_loss(h, target, _row(final_gain), name="final_loss")
    dmods = [[None] * 3 for _ in range(DEPTH)]
    dgains = [[None] * 3 for _ in range(DEPTH)]
    dsinks = [None] * DEPTH
    dbias, made = None, None
    sweep = [(l, j) for l in reversed(range(DEPTH)) for j in reversed(range(3))]
    for k, (l, j) in enumerate(sweep):
        piece = PIECES[j]
        done = lambda grads, l=l, piece=piece: grads_done(l, piece, grads)
        nxt = None
        if k + 1 < len(sweep):
            nl, nj = sweep[k + 1]
            coef = 1.0 if PIECES[nj] == "mix" else 0.5
            nxt = (states[nl][PIECES[nj]]["f"], _row(coef * mod[nl, nj, 2]), coef)
        if piece == "mix":
            dx, dmods[l][j], dgains[l][j], db, dsinks[l], made = _mix_bwd(
                dx, states[l][piece], gains[l, j], mod[l, j], bias, f"l{l}", done, made, nxt)
            dbias = db if dbias is None else dbias + db
        else:
            dx, dmods[l][j], dgains[l][j], made = _ffn_bwd(
                dx, states[l][piece], gains[l, j], mod[l, j], f"{piece}_l{l}", done, made, nxt)
    drel = _bias_grad(dbias, tables, name="bias_grad")[:, 0, :N_BUCKETS].T
    dmod = jnp.stack([jnp.stack(m) for m in dmods])
    dgain = jnp.stack([jnp.stack(g) for g in dgains])
    return loss, dx, dmod, dgain, dfinal[0], drel, jnp.stack(dsinks)


BR_ROWS = (H_SB * HEAD_DIM, 2 * HEAD_DIM, H_SWA_Q * HEAD_DIM)


def _lanes_unshard(g, lead):
    _, rows, _ = g.shape
    r = rows // lead
    return g.reshape(N_DEV, lead, r, LANES).transpose(1, 2, 0, 3).reshape(lead, r, N_DEV * LANES)


def _lanes_shard(full):
    lead, r, _ = full.shape
    return full.reshape(lead, r, N_DEV, LANES).transpose(2, 0, 1, 3).reshape(N_DEV, lead * r, LANES)


def _pack_rows(parts, dtype):
    flat = jnp.concatenate([p.astype(dtype).reshape(-1) for p in parts])
    pad = (-flat.shape[0]) % (16 * LANES)
    if pad:
        flat = jnp.concatenate([flat, jnp.zeros((pad,), dtype)])
    return flat.reshape(-1, LANES)


def _unshard(gathered, axis):
    moved = jnp.moveaxis(gathered, 0, axis)
    shape = list(moved.shape)
    shape[axis:axis + 2] = [shape[axis] * shape[axis + 1]]
    return moved.reshape(shape)


def kernel(x, c, w_ada, b_ada, norm_gain, w_ffn_gate, w_ffn_up, w_ffn_down, w_in, w_br_sb, w_br_dil, w_br_swa, w_out, sinks, rel_bias, final_gain, loss_target, m_w_ada, m_b_ada, m_norm_gain, m_w_ffn_gate, m_w_ffn_up, m_w_ffn_down, m_w_in, m_w_br_sb, m_w_br_dil, m_w_br_swa, m_w_out, m_sinks, m_rel_bias, m_final_gain, v_w_ada, v_b_ada, v_norm_gain, v_w_ffn_gate, v_w_ffn_up, v_w_ffn_down, v_w_in, v_w_br_sb, v_w_br_dil, v_w_br_swa, v_w_out, v_sinks, v_rel_bias, v_final_gain):
    me = 4 * lax.axis_index("x") + 2 * lax.axis_index("y") + lax.axis_index("c")
    d = D_MODEL
    gate_t, up_t, in_t = jnp.swapaxes(w_ffn_gate, 2, 3), jnp.swapaxes(w_ffn_up, 2, 3), jnp.swapaxes(w_in, 1, 2)

    def piece_shards(l, piece):
        bf = lambda t: t.astype(BF16)
        if piece == "mix":
            return [bf(in_t[l]), jnp.concatenate([bf(w_br_sb[l]), bf(w_br_dil[l]), bf(w_br_swa[l])], 0), bf(w_out[l])]
        i = PIECES.index(piece) // 2
        return [bf(gate_t[l, i]), bf(up_t[l, i]), bf(w_ffn_down[l, i])]

    br_off = np.concatenate([[0], np.cumsum(BR_ROWS)])

    def piece_weights(gathered, piece):
        if piece == "mix":
            g_in, g_br, g_out = gathered
            f_br = [_lanes_unshard(g_br[:, br_off[k]:br_off[k + 1]], 1)[0] for k in range(3)]
            return {"in": g_in.reshape(D_QKV + D_GATES, d), "br_sb": f_br[0], "br_dil": f_br[1], "br_swa": f_br[2],
                    "out": g_out.reshape(d, d)}
        return {n: g.reshape(D_FF, d) for n, g in zip(("gate", "up", "down"), gathered)}

    order = [(l, piece) for l in range(DEPTH) for piece in PIECES]
    ahead = 3
    in_flight, passed = {}, {}

    def start_gather(k, after):
        l, piece = order[k]
        in_flight[k], token = _relay_start(piece_shards(l, piece), after, name=f"gather_{piece}_l{l}_start")
        return token

    small, = _all_gather([_pack_rows([c, norm_gain], F32)], after=start_gather(0, c), name="gather_cond")
    c_all = small[:, :d // LANES].reshape(N_DEV, d)
    gains = _unshard(small[:, d // LANES:d // LANES + 6].reshape(N_DEV, DEPTH, 3, LANES), 2)

    cols = w_ada.shape[2]
    mod_cols = jnp.stack([_ada_fwd(c_all, w_ada[l], name=f"ada_fwd_l{l}") for l in range(DEPTH)])
    mod_all, = _all_gather([_pack_rows([mod_cols], F32)], name="gather_mod")
    mod_all = mod_all.reshape(N_DEV, -1)[:, :DEPTH * N_DEV * cols].reshape(N_DEV, DEPTH, N_DEV, cols)
    mod_mine = lax.dynamic_index_in_dim(mod_all, me, axis=2, keepdims=False)
    mod = (mod_mine.transpose(1, 0, 2).reshape(DEPTH, N_DEV * cols) + b_ada).reshape(DEPTH, 3, 3, d)

    token = mod_all
    for k in range(1, 1 + ahead):
        token = start_gather(k, token)
    mod = mod + token[0, 0]

    def weights_of(l, piece, h):
        k = order.index((l, piece))
        token = start_gather(k + ahead, h) if k + ahead < len(order) and k + ahead not in in_flight else None
        for nxt in ([k] if k < 3 else []) + ([k + 1] if 3 <= k + 1 < len(order) else []):
            nl, npiece = order[nxt]
            passed[nxt], token = _relay_pass(in_flight[nxt], h if token is None else token,
                                             name=f"gather_{npiece}_l{nl}_pass")
        landed = _relay_wait(passed[k], h if token is None else token, name=f"gather_{piece}_l{l}_wait")
        return piece_weights(landed, piece), token

    exchanges, have, deferred = {}, {}, []

    def grads_done(l, piece, g):
        key = (l, piece)
        have.setdefault(key, {}).update(g)
        if piece == "mix":
            if len(have[key]) < 5:
                return None
            g = have[key]
            s_br = jnp.concatenate([_lanes_shard(g[n][None]) for n in ("br_sb", "br_dil", "br_swa")], 1)
            groups = [(("in", "br", "out"), [g["in"].reshape(N_DEV, -1, d), s_br, g["out"].reshape(N_DEV, -1, d)])]
        elif key == order[0]:
            deferred.extend(((n,), [t.reshape(N_DEV, -1, d)]) for n, t in g.items())
            return None
        elif len(have[key]) < 3:
            return None
        else:
            groups = [(("gate", "up", "down"), [have[key][n].reshape(N_DEV, -1, d) for n in ("gate", "up", "down")])]
        token = None
        for names, sg in groups:
            state, token = _exchange_start(sg, None, name=f"exchange_{piece}_l{l}_{names[0]}_start")
            exchanges.setdefault(key, []).append((names, state))
        return token

    loss, dx, dmod, dgains, dfinal, drel, dsinks = _local_step(
        x[0], loss_target[0], mod, gains, weights_of, rel_bias, sinks, final_gain, grads_done)

    flat = lambda t: t.reshape(-1, t.shape[-1])
    transposed = lambda ts: tuple(flat(jnp.swapaxes(t, -1, -2)) for t in ts)
    families = {
        "gate": transposed((w_ffn_gate, m_w_ffn_gate, v_w_ffn_gate)), "up": transposed((w_ffn_up, m_w_ffn_up, v_w_ffn_up)),
        "down": tuple(flat(t) for t in (w_ffn_down, m_w_ffn_down, v_w_ffn_down)),
        "in": transposed((w_in, m_w_in, v_w_in)),
        "br": tuple(flat(jnp.concatenate(ts, 1)) for ts in ((w_br_sb, w_br_dil, w_br_swa), (m_w_br_sb, m_w_br_dil, m_w_br_swa),
                                                            (v_w_br_sb, v_w_br_dil, v_w_br_swa))),
        "out": tuple(flat(t) for t in (w_out, m_w_out, v_w_out))}
    stepped = {}

    def step(keys, after):
        for l, piece in keys:
            for names, ex_state in exchanges[l, piece]:
                landed = _exchange_wait(ex_state, after, name=f"exchange_{piece}_l{l}_{names[0]}_wait")
                after = landed[0]
                for n, group in zip(names, landed):
                    w2, m2, v2 = families[n]
                    rows = group.shape[1]
                    row0 = (2 * l + PIECES.index(piece) // 2) * rows if piece != "mix" else l * rows
                    stepped[n] = _reduce_adamw([group], w2, m2, v2, row0, stepped.get(n), after=after,
                                               name=f"reduce_adamw_{n}_{piece}_l{l}")
                    after = stepped[n][1]
        return after

    small_parts = [dmod, dgains, dfinal, drel.T, dsinks, loss[0, :1]]
    small_sizes = [int(np.prod(p.shape)) for p in small_parts]
    small_all, = _all_gather([_pack_rows(small_parts, F32)], name="gather_small")
    token = small_all
    for names, sg in deferred:
        state, token = _exchange_start(sg, token, name=f"exchange_ffn0_l0_{names[0]}_start")
        exchanges.setdefault(order[0], []).append((names, state))

    after_l1 = step([key for key in reversed(order) if key[0] == 1], token)
    small_sum = _sum_parts(small_all, name="sum_small", after=token).reshape(-1)
    offs = np.concatenate([[0], np.cumsum(small_sizes)])
    g_b_ada = small_sum[offs[0]:offs[1]].reshape(DEPTH, 9 * d)
    g_gain_full = small_sum[offs[1]:offs[2]].reshape(DEPTH, 3, d)
    g_norm_gain = lax.dynamic_slice_in_dim(g_gain_full, me * LANES, LANES, axis=2)
    g_final = small_sum[offs[2]:offs[3]]
    g_rel = small_sum[offs[3]:offs[4]].reshape(N_SOFT, N_BUCKETS).T
    g_sinks = small_sum[offs[4]:offs[5]].reshape(DEPTH, H_SWA_Q)
    loss_total = small_sum[offs[5]]

    dmod_all = small_all.reshape(N_DEV, -1)[:, :DEPTH * 9 * d].reshape(N_DEV, DEPTH, 9 * d)
    dmod_cols = lax.dynamic_slice_in_dim(dmod_all, me * cols, cols, axis=2)
    g_w_ada = jnp.stack([_ada_bwd(c_all.T, dmod_cols[:, l], name=f"ada_bwd_l{l}") for l in range(DEPTH)])

    small_state = {"w_ada": (w_ada, m_w_ada, v_w_ada), "b_ada": (b_ada, m_b_ada, v_b_ada),
                   "norm_gain": (norm_gain, m_norm_gain, v_norm_gain), "sinks": (sinks, m_sinks, v_sinks),
                   "rel_bias": (rel_bias, m_rel_bias, v_rel_bias), "final_gain": (final_gain, m_final_gain, v_final_gain)}
    after = step([order[2], order[1]], after_l1)
    grad, update = {}, {}
    for n, g in (("w_ada", g_w_ada), ("b_ada", g_b_ada), ("norm_gain", g_norm_gain), ("sinks", g_sinks),
                 ("rel_bias", g_rel), ("final_gain", g_final)):
        w, m, v = small_state[n]
        grad[n] = g
        if w.ndim == 1:
            update[n] = tuple(t.reshape(w.shape)
                              for t in _adamw(_row(w), _row(g), _row(m), _row(v), name=f"adamw_{n}", after=after))
        else:
            update[n] = _adamw(w, g, m, v, name=f"adamw_{n}", after=after)
        after = update[n][0]

    step([order[0]], after)

    def unflat(n, like, swapped):
        shape = jnp.swapaxes(like, -1, -2).shape if swapped else like.shape
        out = [t.reshape(shape) for t in stepped[n]]
        return [jnp.swapaxes(t, -1, -2) for t in out] if swapped else out

    results = {"w_ffn_gate": unflat("gate", w_ffn_gate, True), "w_ffn_up": unflat("up", w_ffn_up, True),
               "w_ffn_down": unflat("down", w_ffn_down, False), "w_in": unflat("in", w_in, True),
               "w_out": unflat("out", w_out, False)}
    br = [t.reshape(DEPTH, -1, LANES) for t in stepped["br"]]
    for k, n in enumerate(("w_br_sb", "w_br_dil", "w_br_swa")):
        results[n] = [t[:, br_off[k]:br_off[k + 1]] for t in br]
    for n, (g, dl, nm, nv) in results.items():
        grad[n], update[n] = g, (dl, nm, nv)

    names = ["w_ada", "b_ada", "norm_gain", "w_ffn_gate", "w_ffn_up", "w_ffn_down", "w_in", "w_br_sb", "w_br_dil",
             "w_br_swa", "w_out", "sinks", "rel_bias", "final_gain"]
    return (loss_total, dx[None], *[grad[n] for n in names], *[update[n][0] for n in names],
            *[update[n][1] for n in names], *[update[n][2] for n in names])
```

```python
import math

import numpy as np
import jax
import jax.numpy as jnp
from jax import lax
from jax.experimental import pallas as pl
from jax.experimental.pallas import tpu as pltpu

F32, BF16 = jnp.float32, jnp.bfloat16

SEQ, D_MODEL, D_FF, HEAD_DIM = 2048, 1024, 2816, 64
DEPTH = 2
BLK = 128
H_SB, H_DIL, H_SWA_Q, H_SWA_KV = 4, 6, 6, 2
DIL_PATTERNS = ((128, 1), (512, 4), (2048, 16))
SWA_WINDOW = 128
N_BUCKETS, MAX_REL_DIST = 32, 2048
RMS_EPS = 1e-6
D_QKV = 2560
D_GATES = 3 * D_MODEL
ADAM_LR, ADAM_B1, ADAM_B2, ADAM_EPS, ADAM_WD, ADAM_STEP = 0.001, 0.9, 0.999, 1e-08, 0.01, 10

N_DEV = 8
LANES = 128
NEG = -1e30
SB_TILE = 512
VMEM_LIMIT_BYTES = 48 * 1024 * 1024
HBM = pl.BlockSpec(memory_space=pltpu.HBM)
MESH = pl.DeviceIdType.MESH


def _tile(n, target):
    t = (min(n, target) // LANES) * LANES
    while t >= LANES:
        if n % t == 0:
            return t
        t -= LANES
    return n


def _row_tile(r, cap):
    t = (min(r, cap) // 16) * 16
    while t > 16 and r % t:
        t -= 16
    return t


def _params(semantics=None):
    return pltpu.CompilerParams(dimension_semantics=semantics, vmem_limit_bytes=VMEM_LIMIT_BYTES)


def _dot(a, b, ca, cb):
    return lax.dot_general(a, b, (((ca,), (cb,)), ((), ())), preferred_element_type=F32)


def _sigmoid(a):
    return 1.0 / (1.0 + jnp.exp(-a))


def _row(v):
    return v.reshape(1, -1)


def _all_gather(arrs, name, after=None):
    n = len(arrs)
    ins = list(arrs) + ([] if after is None else [after])

    def body(*refs):
        x_refs, out_refs = refs[:n], refs[len(ins):len(ins) + n]
        send_sems, recv_sems, local_sems = refs[len(ins) + n:]
        x, y, c = lax.axis_index("x"), lax.axis_index("y"), lax.axis_index("c")
        me, sibling = (x, y, c), (x, y, 1 - c)
        chips = [(1 - x, y), (x, 1 - y), (1 - x, 1 - y)]

        def slot(t, px, py, pc):
            return out_refs[t].at[4 * px + 2 * py + pc]

        def copy(t, k, block, to, src=None):
            return pltpu.make_async_remote_copy(
                src_ref=slot(t, *block) if src is None else src, dst_ref=slot(t, *block),
                send_sem=send_sems.at[7 * t + k], recv_sem=recv_sems.at[7 * t + k], device_id=to, device_id_type=MESH)

        mine = [pltpu.make_async_copy(x_refs[t], slot(t, *me), local_sems.at[t]) for t in range(n)]
        for cp in mine:
            cp.start()
        first = []
        for t in range(n):
            first.append(copy(t, 0, me, sibling, src=x_refs[t]))
            first += [copy(t, 1 + j, me, (*chip, c), src=x_refs[t]) for j, chip in enumerate(chips)]
        for cp in first:
            cp.start()
        passed = []
        for j, chip in enumerate(chips):
            for t in range(n):
                copy(t, 1 + j, (*chip, c), me).wait_recv()
                passed.append(copy(t, 4 + j, (*chip, c), sibling))
                passed[-1].start()
        for t in range(n):
            copy(t, 0, sibling, me).wait_recv()
        for j, chip in enumerate(chips):
            for t in range(n):
                copy(t, 4 + j, (*chip, 1 - c), me).wait_recv()
        for cp in first + passed:
            cp.wait_send()
        for cp in mine:
            cp.wait()

    return pl.pallas_call(
        body, name=name, out_shape=[jax.ShapeDtypeStruct((N_DEV,) + a.shape, a.dtype) for a in arrs],
        in_specs=[HBM] * n + [pl.BlockSpec(memory_space=pl.ANY)] * (len(ins) - n), out_specs=[HBM] * n,
        scratch_shapes=[pltpu.SemaphoreType.DMA((7 * n,)), pltpu.SemaphoreType.DMA((7 * n,)), pltpu.SemaphoreType.DMA((n,))],
    )(*ins)


def _direct_copies(x_refs, land_refs, send_sems, recv_sems, local_sems):
    x, y, c = lax.axis_index("x"), lax.axis_index("y"), lax.axis_index("c")
    me = 4 * x + 2 * y + c
    sends, recvs = [], []
    for k in range(1, N_DEV):
        px = 1 - x if (k >> 2) & 1 else x
        py = 1 - y if (k >> 1) & 1 else y
        pc = 1 - c if k & 1 else c
        peer = 4 * px + 2 * py + pc
        for t, (x_ref, land_ref) in enumerate(zip(x_refs, land_refs)):
            sem = 7 * t + k - 1
            for out, src, slot in ((sends, peer, me), (recvs, me, peer)):
                out.append(pltpu.make_async_remote_copy(
                    src_ref=x_ref.at[src], dst_ref=land_ref.at[slot], send_sem=send_sems.at[sem],
                    recv_sem=recv_sems.at[sem], device_id=(px, py, pc), device_id_type=MESH))
    own = [pltpu.make_async_copy(x_ref.at[me], land_ref.at[me], local_sems.at[t])
           for t, (x_ref, land_ref) in enumerate(zip(x_refs, land_refs))]
    return sends, recvs, own


SEM =pl.BlockSpec(memory_space=pltpu.SEMAPHORE)
ANY = pl.BlockSpec(memory_space=pl.ANY)
SIDE_EFFECT = pltpu.SideEffectType.DATAFLOW_SIDE_EFFECTING


def _exchange_start(arrs, after, *, name):
    n = len(arrs)
    lands = [lax.empty(a.shape, a.dtype) for a in arrs]
    extra = [] if after is None else [after]

    def body(*refs):
        sems = refs[2 * n + len(extra):2 * n + len(extra) + 3]
        sends, _, own = _direct_copies(refs[:n], refs[n:2 * n], *sems)
        for cp in own + sends:
            cp.start()
        refs[-1][...] = jnp.zeros_like(refs[-1])

    ops = [pltpu.with_memory_space_constraint(a, pltpu.HBM) for a in list(arrs) + lands]
    out = pl.pallas_call(
        body, name=name,
        out_shape=(pltpu.SemaphoreType.DMA((7 * n,)), pltpu.SemaphoreType.DMA((7 * n,)), pltpu.SemaphoreType.DMA((n,)),
                   *[pltpu.HBM(a.shape, a.dtype) for a in ops], jax.ShapeDtypeStruct((8, LANES), F32)),
        in_specs=[HBM] * (2 * n) + [ANY] * len(extra),
        out_specs=(SEM, SEM, SEM, *[HBM] * (2 * n), pl.BlockSpec(memory_space=pltpu.VMEM)),
        input_output_aliases={t: 3 + t for t in range(2 * n)},
        compiler_params=pltpu.CompilerParams(has_side_effects=SIDE_EFFECT),
    )(*ops, *extra)
    return (out[:3], out[3:3 + n], out[3 + n:3 + 2 * n]), out[-1]


def _exchange_wait(state, after, *, name):
    sems, arrs, lands = state
    n = len(arrs)

    def body(*refs):
        sends, recvs, own = _direct_copies(refs[:n], refs[n:2 * n], *refs[2 * n:2 * n + 3])
        for cp in own:
            cp.wait()
        for cp in sends:
            cp.wait_send()
        for cp in recvs:
            cp.wait_recv()

    out = pl.pallas_call(
        body, name=name, out_shape=tuple(pltpu.HBM(a.shape, a.dtype) for a in list(arrs) + list(lands)),
        in_specs=[HBM] * (2 * n) + [SEM, SEM, SEM, ANY], out_specs=tuple([HBM] * (2 * n)),
        input_output_aliases={t: t for t in range(2 * n)},
        compiler_params=pltpu.CompilerParams(has_side_effects=SIDE_EFFECT),
    )(*arrs, *lands, *sems, after)
    return out[n:]


def _relay_copies(x_refs, land_refs, sems_a, sems_b):
    x, y, c = lax.axis_index("x"), lax.axis_index("y"), lax.axis_index("c")
    me = 4 * x + 2 * y + c
    sibling = (x, y, 1 - c)
    chips = [(1 - x, y), (x, 1 - y), (1 - x, 1 - y)]

    def slot(px, py, pc):
        return 4 * px + 2 * py + pc

    def copy(src, land_ref, dst_slot, send_sems, recv_sems, k, to):
        return pltpu.make_async_remote_copy(src_ref=src, dst_ref=land_ref.at[dst_slot], send_sem=send_sems.at[k],
                                            recv_sem=recv_sems.at[k], device_id=to, device_id_type=MESH)

    a_send, a_recv, a_own, b_send, b_recv = [], [], [], [], []
    for t, (x_ref, land_ref) in enumerate(zip(x_refs, land_refs)):
        peers = [sibling] + [(*chip, c) for chip in chips]
        if sems_a is not None:
            for k, peer in enumerate(peers):
                a_send.append(copy(x_ref, land_ref, me, sems_a[0], sems_a[1], 4 * t + k, peer))
                a_recv.append(copy(x_ref, land_ref, slot(*peer), sems_a[0], sems_a[1], 4 * t + k, peer))
            a_own.append(pltpu.make_async_copy(x_ref, land_ref.at[me], sems_a[2].at[t]))
        if sems_b is not None:
            for j, chip in enumerate(chips):
                b_send.append(copy(land_ref.at[slot(*chip, c)], land_ref, slot(*chip, c), sems_b[0], sems_b[1], 3 * t + j, sibling))
                b_recv.append(copy(land_ref.at[slot(*chip, c)], land_ref, slot(*chip, 1 - c), sems_b[0], sems_b[1], 3 * t + j,
                                   sibling))
    return (a_send, a_recv, a_own), (b_send, b_recv)


def _relay_start(arrs, after, name):
    n = len(arrs)
    lands = [lax.empty((N_DEV,) + a.shape, a.dtype) for a in arrs]

    def body(*refs):
        (sends, _, own), _ = _relay_copies(refs[:n], refs[n:2 * n], refs[2 * n + 1:2 * n + 4], None)
        for cp in own + sends:
            cp.start()
        refs[-1][...] = jnp.zeros_like(refs[-1])

    ops = [pltpu.with_memory_space_constraint(a, pltpu.HBM) for a in list(arrs) + lands]
    out = pl.pallas_call(
        body, name=name,
        out_shape=(pltpu.SemaphoreType.DMA((4 * n,)), pltpu.SemaphoreType.DMA((4 * n,)), pltpu.SemaphoreType.DMA((n,)),
                   *[pltpu.HBM(a.shape, a.dtype) for a in ops], jax.ShapeDtypeStruct((8, LANES), F32)),
        in_specs=[HBM] * (2 * n) + [ANY],
        out_specs=(SEM, SEM, SEM, *[HBM] * (2 * n), pl.BlockSpec(memory_space=pltpu.VMEM)),
        input_output_aliases={t: 3 + t for t in range(2 * n)},
        compiler_params=pltpu.CompilerParams(has_side_effects=SIDE_EFFECT),
    )(*ops, after)
    return (out[:3], out[3:3 + n], out[3 + n:3 + 2 * n]), out[-1]


def _relay_pass(state, after, name):
    sems_a, arrs, lands = state
    n = len(arrs)

    def body(*refs):
        sems_b = refs[2 * n + 4:2 * n + 6]
        (a_send, a_recv, a_own), (b_send, _) = _relay_copies(refs[:n], refs[n:2 * n], refs[2 * n:2 * n + 3], sems_b)
        for cp in a_own:
            cp.wait()
        for cp in a_send:
            cp.wait_send()
        for cp in a_recv:
            cp.wait_recv()
        for cp in b_send:
            cp.start()
        refs[-1][...] = jnp.zeros_like(refs[-1])

    out = pl.pallas_call(
        body, name=name,
        out_shape=(pltpu.SemaphoreType.DMA((3 * n,)), pltpu.SemaphoreType.DMA((3 * n,)),
                   *[pltpu.HBM(a.shape, a.dtype) for a in list(arrs) + list(lands)], jax.ShapeDtypeStruct((8, LANES), F32)),
        in_specs=[HBM] * (2 * n) + [SEM, SEM, SEM, ANY],
        out_specs=(SEM, SEM, *[HBM] * (2 * n), pl.BlockSpec(memory_space=pltpu.VMEM)),
        input_output_aliases={t: 2 + t for t in range(2 * n)},
        compiler_params=pltpu.CompilerParams(has_side_effects=SIDE_EFFECT),
    )(*arrs, *lands, *sems_a, after)
    return (out[:2], out[2:2 + n], out[2 + n:2 + 2 * n]), out[-1]


def _relay_wait(state, after, name):
    sems_b, arrs, lands = state
    n = len(arrs)

    def body(*refs):
        _, (b_send, b_recv) = _relay_copies(refs[:n], refs[n:2 * n], None, refs[2 * n:2 * n + 2])
        for cp in b_send:
            cp.wait_send()
        for cp in b_recv:
            cp.wait_recv()

    out = pl.pallas_call(
        body, name=name, out_shape=tuple(pltpu.HBM(a.shape, a.dtype) for a in list(arrs) + list(lands)),
        in_specs=[HBM] * (2 * n) + [SEM, SEM, ANY], out_specs=tuple([HBM] * (2 * n)),
        input_output_aliases={t: t for t in range(2 * n)},
        compiler_params=pltpu.CompilerParams(has_side_effects=SIDE_EFFECT),
    )(*arrs, *lands, *sems_b, after)
    return out[n:]


def _sum_parts(parts, name, after=None):
    n, r, cdim = parts.shape
    tr = _row_tile(r, max(16, (1 << 21) // (n * cdim * parts.dtype.itemsize)))

    def body(p_ref, *rest):
        acc = p_ref[0].astype(F32)
        for k in range(1, n):
            acc = acc + p_ref[k].astype(F32)
        rest[-1][...] = acc

    ins = [parts] + ([] if after is None else [after])
    return pl.pallas_call(
        body, name=name, out_shape=jax.ShapeDtypeStruct((r, cdim), F32), grid=(r // tr,),
        in_specs=[pl.BlockSpec((n, tr, cdim), lambda i: (0, i, 0))] + [ANY] * (len(ins) - 1),
        out_specs=pl.BlockSpec((tr, cdim), lambda i: (i, 0)), compiler_params=_params(("parallel",)),
    )(*ins)


def _mm_tn(a, b, *, name, after=None, tm=512, tn=1024, out_rows=None, row0=0, prev=None):
    k, m = a.shape
    n = b.shape[1]
    tm, tn = _tile(m, tm), _tile(n, tn)
    out_rows = m if out_rows is None else out_rows

    def body(a_ref, b_ref, *rest):
        o_ref, at_ref = rest[-2], rest[-1]

        @pl.when(pl.program_id(1) == 0)
        def _():
            at_ref[...] = a_ref[...].astype(BF16).T

        o_ref[...] = _dot(at_ref[...], b_ref[...].astype(BF16), 1, 0).astype(BF16)

    ins = [a, b] + [t for t in (after, prev) if t is not None]
    return pl.pallas_call(
        body, name=name, out_shape=jax.ShapeDtypeStruct((out_rows, n), BF16), grid=(m // tm, n // tn),
        in_specs=[pl.BlockSpec((k, tm), lambda i, j: (0, i)), pl.BlockSpec((k, tn), lambda i, j: (0, j))] + [ANY] * (len(ins) - 2),
        out_specs=pl.BlockSpec((tm, tn), lambda i, j: (row0 // tm + i, j)),
        input_output_aliases={} if prev is None else {len(ins) - 1: 0},
        scratch_shapes=[pltpu.VMEM((tm, k), BF16)], compiler_params=_params(("parallel", "arbitrary")),
    )(*ins)


def _mm(a, b, *, name, ta=False, tb=False, res=None, colscale=None, emit_acc=False,
        out_dtype=F32, tm=512, tn=512, b_rows=None):
    m, k = (a.shape[1], a.shape[0]) if ta else a.shape
    n = b.shape[0] if tb else b.shape[1]
    b_start = 0
    if b_rows is not None:
        b_start, n = b_rows
    tm, tn = _tile(m, tm), _tile(n, tn)
    ca, cb = (0 if ta else 1), (1 if tb else 0)
    a_spec = pl.BlockSpec((k, tm), lambda i, j: (0, i)) if ta else pl.BlockSpec((tm, k), lambda i, j: (i, 0))
    b_spec = (pl.BlockSpec((tn, k), lambda i, j: (b_start // tn + j, 0)) if tb
              else pl.BlockSpec((k, tn), lambda i, j: (0, j)))
    tile = pl.BlockSpec((tm, tn), lambda i, j: (i, j))
    ins, in_specs = [a, b], [a_spec, b_spec]
    if res is not None:
        ins.append(res)
        in_specs.append(tile)
    if colscale is not None:
        ins.append(colscale)
        in_specs.append(pl.BlockSpec((1, tn), lambda i, j: (0, j)))
    n_in = len(ins)

    def body(*refs):
        outs = refs[n_in:]
        acc = _dot(refs[0][...].astype(BF16), refs[1][...].astype(BF16), ca, cb)
        val, p = acc, 2
        if res is not None:
            r_val, p = refs[p][...], p + 1
        if colscale is not None:
            val = val * refs[p][...]
        if res is not None:
            val = r_val + val
        if emit_acc:
            outs[0][...] = acc
        outs[-1][...] = val.astype(out_dtype)

    out_shape = [jax.ShapeDtypeStruct((m, n), out_dtype)]
    out_specs = [tile]
    if emit_acc:
        out_shape.insert(0, jax.ShapeDtypeStruct((m, n), F32))
        out_specs.insert(0, tile)
    out = pl.pallas_call(
        body, name=name, out_shape=out_shape, grid=(m // tm, n // tn), in_specs=in_specs, out_specs=out_specs,
        compiler_params=_params(("parallel", "parallel")),
    )(*ins)
    return out if emit_acc else out[0]


def _norm_fwd(x, g, scale, shift, name, after=None):
    s, d = x.shape
    tr = TOK_TILE

    def body(x_ref, g_ref, sc_ref, sh_ref, *rest):
        xv = x_ref[...]
        rstd = lax.rsqrt(jnp.mean(xv * xv, axis=-1, keepdims=True) + RMS_EPS)
        rest[-1][...] = (xv * rstd * g_ref[...] * (1.0 + sc_ref[...]) + sh_ref[...]).astype(BF16)

    rowspec = pl.BlockSpec((1, d), lambda i: (0, 0))
    ins = [x, g, scale, shift] + ([] if after is None else [after])
    return pl.pallas_call(
        body, name=name, out_shape=jax.ShapeDtypeStruct((s, d), BF16), grid=(s // tr,),
        in_specs=[pl.BlockSpec((tr, d), lambda i: (i, 0)), rowspec, rowspec, rowspec] + [ANY] * (len(ins) - 4),
        out_specs=pl.BlockSpec((tr, d), lambda i: (i, 0)),
        compiler_params=_params(("parallel",)),
    )(*ins)


def _dh_norm_bwd(a1, b1, a2, b2, x, dres, g, scale, nxt, *, name, after=None, b_rows=None):
    s, d = x.shape
    tm = 256
    n_fixed = 8

    def body(a1_ref, b1_ref, a2_ref, b2_ref, x_ref, dr_ref, g_ref, sc_ref, *rest):
        rest = rest[(1 if after is not None else 0):]
        if nxt is not None:
            f_ref, cs_ref, dx_ref, sa_ref, sb_ref, df_ref, dg_ref = rest
        else:
            dx_ref, sa_ref, sb_ref = rest

        @pl.when(pl.program_id(0) == 0)
        def _():
            sa_ref[...] = jnp.zeros_like(sa_ref)
            sb_ref[...] = jnp.zeros_like(sb_ref)
            if nxt is not None:
                dg_ref[...] = jnp.zeros_like(dg_ref)

        dhv = (_dot(a1_ref[...].astype(BF16), b1_ref[...], 1, 0) + _dot(a2_ref[...].astype(BF16), b2_ref[...], 1, 0))
        xv = x_ref[...]
        rstd = lax.rsqrt(jnp.mean(xv * xv, axis=-1, keepdims=True) + RMS_EPS)
        xhat = xv * rstd
        dxhat = dhv * (g_ref[...] * (1.0 + sc_ref[...]))
        mean_term = jnp.mean(dxhat * xhat, axis=-1, keepdims=True)
        dxv = dr_ref[...] + rstd * (dxhat - xhat * mean_term)
        dx_ref[...] = dxv
        sa_ref[...] += jnp.sum(dhv, axis=0, keepdims=True)
        sb_ref[...] += jnp.sum(dhv * xhat, axis=0, keepdims=True)
        if nxt is not None:
            df_ref[...] = (dxv * cs_ref[...]).astype(BF16)
            dg_ref[...] += nxt[2] * jnp.sum(dxv * f_ref[...], axis=0, keepdims=True)

    def a_spec(t):
        return pl.BlockSpec((tm, t.shape[1]), lambda i: (i, 0))

    def b_spec(t, a, which):
        if b_rows is None:
            return pl.BlockSpec((t.shape[0], d), lambda i: (0, 0))
        start = b_rows[which]
        return pl.BlockSpec((pl.Element(a.shape[1]), pl.Element(d)), lambda i: (start, 0))

    rowspec = pl.BlockSpec((1, d), lambda i: (0, 0))
    tile = pl.BlockSpec((tm, d), lambda i: (i, 0))
    ins = [a1, b1, a2, b2, x, dres, g, scale] + ([] if after is None else [after])
    in_specs = [a_spec(a1), b_spec(b1, a1, 0), a_spec(a2), b_spec(b2, a2, 1), tile, tile, rowspec, rowspec]
    in_specs += [ANY] * (len(ins) - n_fixed)
    out_shape = [jax.ShapeDtypeStruct((s, d), F32), jax.ShapeDtypeStruct((1, d), F32), jax.ShapeDtypeStruct((1, d), F32)]
    out_specs = [tile, rowspec, rowspec]
    if nxt is not None:
        ins += [nxt[0], nxt[1]]
        in_specs += [tile, rowspec]
        out_shape += [jax.ShapeDtypeStruct((s, d), BF16), jax.ShapeDtypeStruct((1, d), F32)]
        out_specs += [tile, rowspec]
    out = pl.pallas_call(
        body, name=name, out_shape=out_shape, grid=(s // tm,), in_specs=in_specs, out_specs=out_specs,
        compiler_params=_params(("arbitrary",)),
    )(*ins)
    return out[0], out[1], out[2], (None if nxt is None else (out[3], out[4]))


def _gate_bwd(dxn, f, colscale, coef, name):
    s, d = dxn.shape
    tr = 256

    def body(dx_ref, f_ref, cs_ref, df_ref, dg_ref):
        @pl.when(pl.program_id(0) == 0)
        def _():
            dg_ref[...] = jnp.zeros_like(dg_ref)

        dxv = dx_ref[...]
        df_ref[...] = (dxv * cs_ref[...]).astype(BF16)
        dg_ref[...] += coef * jnp.sum(dxv * f_ref[...], axis=0, keepdims=True)

    rowspec = pl.BlockSpec((1, d), lambda i: (0, 0))
    tile = pl.BlockSpec((tr, d), lambda i: (i, 0))
    return pl.pallas_call(
        body, name=name, out_shape=[jax.ShapeDtypeStruct((s, d), BF16), jax.ShapeDtypeStruct((1, d), F32)],
        grid=(s // tr,), in_specs=[tile, tile, rowspec], out_specs=[tile, rowspec],
        compiler_params=_params(("arbitrary",)),
    )(dxn, f, colscale)


def _ffn_up(h, wg, wu, name, tm=SEQ, tn=256):
    s, d = h.shape
    f = wg.shape[0]

    def body(h_ref, wg_ref, wu_ref, a_ref, u_ref, s_ref):
        hv = h_ref[...]
        a = _dot(hv, wg_ref[...], 1, 1)
        u = _dot(hv, wu_ref[...], 1, 1)
        a_ref[...] = a.astype(BF16)
        u_ref[...] = u.astype(BF16)
        s_ref[...] = (a * _sigmoid(a) * u).astype(BF16)

    tile = pl.BlockSpec((tm, tn), lambda i, j: (i, j))
    wspec = pl.BlockSpec((tn, d), lambda i, j: (j, 0))
    return pl.pallas_call(
        body, name=name,
        out_shape=[jax.ShapeDtypeStruct((s, f), BF16), jax.ShapeDtypeStruct((s, f), BF16), jax.ShapeDtypeStruct((s, f), BF16)],
        grid=(s // tm, f // tn), in_specs=[pl.BlockSpec((tm, d), lambda i, j: (i, 0)), wspec, wspec],
        out_specs=[tile, tile, tile], compiler_params=_params(("parallel", "parallel")),
    )(h, wg, wu)


def _ffn_bwd_ds(df, wd, a, u, name, tm=SEQ, tn=256):
    s, d = df.shape
    f = wd.shape[0]

    def body(df_ref, wd_ref, a_ref, u_ref, da_ref, du_ref):
        ds = _dot(df_ref[...], wd_ref[...], 1, 1)
        av = a_ref[...].astype(F32)
        sg = _sigmoid(av)
        da_ref[...] = (ds * u_ref[...].astype(F32) * (sg * (1.0 + av * (1.0 - sg)))).astype(BF16)
        du_ref[...] = (ds * (av * sg)).astype(BF16)

    tile = pl.BlockSpec((tm, tn), lambda i, j: (i, j))
    return pl.pallas_call(
        body, name=name, out_shape=[jax.ShapeDtypeStruct((s, f), BF16), jax.ShapeDtypeStruct((s, f), BF16)],
        grid=(s // tm, f // tn),
        in_specs=[pl.BlockSpec((tm, d), lambda i, j: (i, 0)), pl.BlockSpec((tn, d), lambda i, j: (j, 0)), tile, tile],
        out_specs=[tile, tile], compiler_params=_params(("parallel", "parallel")),
    )(df, wd, a, u)


def _merge_fwd(o_sb, o_dil, o_swa, gates, wb_sb, wb_dil, wb_swa, name):
    s, d = SEQ, D_MODEL
    tm = 256

    def body(osb_ref, odl_ref, osw_ref, g_ref, wsb_ref, wdl_ref, wsw_ref, m_ref, tsb_ref, tdl_ref, tsw_ref):
        for h in range(osb_ref.shape[0]):
            tsb_ref[:, h * HEAD_DIM:(h + 1) * HEAD_DIM] = osb_ref[h].astype(BF16)
        for h in range(osw_ref.shape[0]):
            tsw_ref[:, h * HEAD_DIM:(h + 1) * HEAD_DIM] = osw_ref[h].astype(BF16)
        tdl_ref[...] = odl_ref[...].astype(BF16)
        acc = _sigmoid(g_ref[:, 0:d]) * _dot(tsb_ref[...], wsb_ref[...], 1, 0)
        acc += _sigmoid(g_ref[:, d:2 * d]) * _dot(tdl_ref[...], wdl_ref[...], 1, 0)
        acc += _sigmoid(g_ref[:, 2 * d:3 * d]) * _dot(tsw_ref[...], wsw_ref[...], 1, 0)
        m_ref[...] = acc.astype(BF16)

    def rows(w):
        return pl.BlockSpec((tm, w), lambda i: (i, 0))

    def heads(n):
        return pl.BlockSpec((n, tm, HEAD_DIM), lambda i: (0, i, 0))

    def whole(w):
        return pl.BlockSpec((w, d), lambda i: (0, 0))

    return pl.pallas_call(
        body, name=name, out_shape=[jax.ShapeDtypeStruct((s, w), BF16) for w in (d, 256, 128, 384)], grid=(s // tm,),
        in_specs=[heads(H_SB), rows(128), heads(H_SWA_Q), rows(3 * d), whole(256), whole(128), whole(384)],
        out_specs=[rows(d), rows(256), rows(128), rows(384)], compiler_params=_params(("parallel",)),
    )(o_sb, o_dil, o_swa, gates, wb_sb, wb_dil, wb_swa)


def _merge_bwd(dmerged, t_sb, t_dil, t_swa, gates, wb_sb, wb_dil, wb_swa, name):
    s, d = SEQ, D_MODEL
    tm = 256

    def body(dm_ref, tsb_ref, tdl_ref, tsw_ref, g_ref, wsb_ref, wdl_ref, wsw_ref,
             dg_ref, dosb_ref, dodl_ref, dosw_ref, dbsb_ref, dbdl_ref, dbsw_ref):
        dm = dm_ref[...]
        for idx, (t_ref, w_ref, do_ref, db_ref) in enumerate((
                (tsb_ref, wsb_ref, dosb_ref, dbsb_ref), (tdl_ref, wdl_ref, dodl_ref, dbdl_ref),
                (tsw_ref, wsw_ref, dosw_ref, dbsw_ref))):
            w = w_ref[...]
            br = _dot(t_ref[...], w, 1, 0)
            sg = _sigmoid(g_ref[:, idx * d:(idx + 1) * d])
            dbr = (dm * sg).astype(BF16)
            dg_ref[:, idx * d:(idx + 1) * d] = (dm * br * (sg * (1.0 - sg))).astype(BF16)
            db_ref[...] = dbr
            do = _dot(dbr, w, 1, 1)
            if len(do_ref.shape) == 2:
                do_ref[...] = do
            else:
                for h in range(do_ref.shape[0]):
                    do_ref[h] = do[:, h * HEAD_DIM:(h + 1) * HEAD_DIM]

    def rows(w):
        return pl.BlockSpec((tm, w), lambda i: (i, 0))

    def heads(n):
        return pl.BlockSpec((n, tm, HEAD_DIM), lambda i: (0, i, 0))

    def whole(w):
        return pl.BlockSpec((w, d), lambda i: (0, 0))

    def shp(w, dt):
        return jax.ShapeDtypeStruct((s, w), dt)

    def hshp(n):
        return jax.ShapeDtypeStruct((n, s, HEAD_DIM), F32)

    return pl.pallas_call(
        body, name=name,
        out_shape=[shp(3 * d, BF16), hshp(H_SB), shp(128, F32), hshp(H_SWA_Q), shp(d, BF16), shp(d, BF16), shp(d, BF16)],
        grid=(s // tm,),
        in_specs=[rows(d), rows(256), rows(128), rows(384), rows(3 * d), whole(256), whole(128), whole(384)],
        out_specs=[rows(3 * d), heads(H_SB), rows(128), heads(H_SWA_Q), rows(d), rows(d), rows(d)],
        compiler_params=_params(("parallel",)),
    )(dmerged, t_sb, t_dil, t_swa, gates, wb_sb, wb_dil, wb_swa)


def _final_loss(x, target, g, name):
    s, d = x.shape
    tr = 256

    def body(x_ref, t_ref, g_ref, loss_ref, dx_ref, dg_ref):
        @pl.when(pl.program_id(0) == 0)
        def _():
            loss_ref[...] = jnp.zeros_like(loss_ref)
            dg_ref[...] = jnp.zeros_like(dg_ref)

        xv = x_ref[...]
        gv = g_ref[...]
        rstd = lax.rsqrt(jnp.mean(xv * xv, axis=-1, keepdims=True) + RMS_EPS)
        xhat = xv * rstd
        err = xhat * gv - t_ref[...]
        loss_ref[...] += 0.5 * jnp.sum(jnp.mean(err * err, axis=-1, keepdims=True))
        dy = err * (1.0 / d)
        dxhat = dy * gv
        mean_term = jnp.mean(dxhat * xhat, axis=-1, keepdims=True)
        dx_ref[...] = rstd * (dxhat - xhat * mean_term)
        dg_ref[...] += jnp.sum(dy * xhat, axis=0, keepdims=True)

    rowspec = pl.BlockSpec((1, d), lambda i: (0, 0))
    tile = pl.BlockSpec((tr, d), lambda i: (i, 0))
    return pl.pallas_call(
        body, name=name,
        out_shape=[jax.ShapeDtypeStruct((1, LANES), F32), jax.ShapeDtypeStruct((s, d), F32), jax.ShapeDtypeStruct((1, d), F32)],
        grid=(s // tr,), in_specs=[tile, tile, rowspec],
        out_specs=[pl.BlockSpec((1, LANES), lambda i: (0, 0)), tile, rowspec],
        compiler_params=_params(("arbitrary",)),
    )(x, target, g)


def _adamw(w, g, m, v, name, after=None):
    shape = w.shape
    cols = shape[-1]
    rows = int(np.prod(shape[:-1])) if len(shape) > 1 else 1
    tr = rows
    for cand in (1024, 512, 256, 128, 64, 32, 16, 8):
        if rows % cand == 0 and rows > cand and cand * cols * 4 <= (1 << 21):
            tr = cand
            break

    def body(w_ref, g_ref, m_ref, v_ref, *rest):
        d_ref, nm_ref, nv_ref = rest[-3:]
        d_ref[...], nm_ref[...], nv_ref[...] = _adam_update(w_ref[...], g_ref[...], m_ref[...], v_ref[...])

    tile = pl.BlockSpec((tr, cols), lambda i: (i, 0))
    flat = [t.reshape(rows, cols) for t in (w, g, m, v)] + ([] if after is None else [after])
    out = pl.pallas_call(
        body, name=name, out_shape=[jax.ShapeDtypeStruct((rows, cols), F32)] * 3, grid=(rows // tr,),
        in_specs=[tile] * 4 + [ANY] * (len(flat) - 4), out_specs=[tile] * 3, compiler_params=_params(("parallel",)),
    )(*flat)
    return tuple(t.reshape(shape) for t in out)


def _adam_update(w, gv, m, v):
    nm = ADAM_B1 * m + (1.0 - ADAM_B1) * gv
    nv = ADAM_B2 * v + (1.0 - ADAM_B2) * (gv * gv)
    m_hat = nm / (1.0 - ADAM_B1 ** ADAM_STEP)
    v_hat = nv / (1.0 - ADAM_B2 ** ADAM_STEP)
    return -ADAM_LR * (m_hat / (jnp.sqrt(v_hat) + ADAM_EPS) + ADAM_WD * w), nm, nv


def _reduce_adamw(groups, w, m, v, row0, prev, name, after=None):
    n, r, cdim = groups[0].shape
    rows = w.shape[0]
    tr = _row_tile(r, max(16, (1 << 22) // (n * cdim * groups[0].dtype.itemsize)))
    steps = r // tr
    ng = len(groups)

    def body(*refs):
        w_ref, m_ref, v_ref = refs[ng:ng + 3]
        g_out, d_out, m_out, v_out = refs[-4:]
        gg = pl.program_id(0)
        for gi in range(ng):
            @pl.when(gg == gi)
            def _(gi=gi):
                acc = refs[gi][0].astype(F32)
                for k in range(1, n):
                    acc = acc + refs[gi][k].astype(F32)
                g_out[...] = acc
                d_out[...], m_out[...], v_out[...] = _adam_update(w_ref[...], acc, m_ref[...], v_ref[...])

    def part_spec(gi):
        return pl.BlockSpec((n, tr, cdim), lambda gg, i: (0, jnp.where(gg == gi, i, 0), 0))

    tile = pl.BlockSpec((tr, cdim), lambda gg, i: (row0 // tr + gg * steps + i, 0))
    extra = ([] if prev is None else list(prev)) + ([] if after is None else [after])
    return pl.pallas_call(
        body, name=name, out_shape=[jax.ShapeDtypeStruct((rows, cdim), F32)] * 4, grid=(ng, steps),
        in_specs=[part_spec(gi) for gi in range(ng)] + [tile] * 3 + [ANY] * len(extra), out_specs=[tile] * 4,
        input_output_aliases={} if prev is None else {ng + 3 + k: k for k in range(4)},
        compiler_params=_params(("parallel", "parallel")),
    )(*groups, w, m, v, *extra)


def _ada_fwd(c_all, w, name):
    n = w.shape[1]

    def body(c_ref, w_ref, o_ref):
        cv = c_ref[...]
        o_ref[...] = jnp.dot(cv * _sigmoid(cv), w_ref[...], preferred_element_type=F32, precision=lax.Precision.HIGHEST)

    return pl.pallas_call(body, name=name, out_shape=jax.ShapeDtypeStruct((N_DEV, n), F32), compiler_params=_params())(c_all, w)


def _ada_bwd(c_all_t, dmod, name):
    n = dmod.shape[1]

    def body(c_ref, d_ref, o_ref):
        cv = c_ref[...]
        o_ref[...] = jnp.dot(cv * _sigmoid(cv), d_ref[...], preferred_element_type=F32, precision=lax.Precision.HIGHEST)

    return pl.pallas_call(body, name=name, out_shape=jax.ShapeDtypeStruct((D_MODEL, n), F32), compiler_params=_params())(c_all_t, dmod)


def _bucket_tables():
    rel = np.arange(BLK)[:, None] + BLK - np.arange(2 * BLK)[None, :]
    max_exact = N_BUCKETS // 2

    def bucket(n):
        nf = np.maximum(n, 1).astype(np.float32)
        large = max_exact + (np.log(nf / np.float32(max_exact)) / np.float32(math.log(MAX_REL_DIST / max_exact))
                             * np.float32(N_BUCKETS - max_exact)).astype(np.int32)
        return np.where(n < max_exact, n, np.minimum(large, N_BUCKETS - 1))

    tabs = []
    for dil, max_dist in ((1, 128), (4, 128), (16, 128), (1, SWA_WINDOW - 1)):
        in_band = (rel >= 0) & (rel <= max_dist)
        tabs.append(np.where(in_band, bucket(np.maximum(rel, 0) * dil), -1))
    return np.stack(tabs).astype(np.int32)


N_SOFT = H_DIL + H_SWA_Q


def _table_of_head(h):
    return jnp.minimum(h // 2, 3)


def _bias_build(rel_bias, tables, name):
    def body(rel_ref, t_ref, o_ref):
        h = pl.program_id(0)
        tb = t_ref[0]
        out = jnp.full((BLK, 2 * BLK), NEG, F32)
        for b in range(N_BUCKETS):
            out = jnp.where(tb == b, rel_ref[b, h], out)
        o_ref[0] = out

    return pl.pallas_call(
        body, name=name, out_shape=jax.ShapeDtypeStruct((N_SOFT, BLK, 2 * BLK), F32), grid=(N_SOFT,),
        in_specs=[pl.BlockSpec(memory_space=pltpu.SMEM),
                  pl.BlockSpec((1, BLK, 2 * BLK), lambda h: (_table_of_head(h), 0, 0))],
        out_specs=pl.BlockSpec((1, BLK, 2 * BLK), lambda h: (h, 0, 0)),
        compiler_params=_params(("parallel",)),
    )(rel_bias, tables)


def _bias_grad(dbias, tables, name):
    def body(d_ref, t_ref, o_ref):
        tb = t_ref[0]
        dv = d_ref[0]
        lane = lax.broadcasted_iota(jnp.int32, (1, LANES), 1)
        out = jnp.zeros((1, LANES), F32)
        for b in range(N_BUCKETS):
            out = jnp.where(lane == b, jnp.sum(jnp.where(tb == b, dv, 0.0)), out)
        o_ref[0] = out

    return pl.pallas_call(
        body, name=name, out_shape=jax.ShapeDtypeStruct((N_SOFT, 1, LANES), F32), grid=(N_SOFT,),
        in_specs=[pl.BlockSpec((1, BLK, 2 * BLK), lambda h: (h, 0, 0)),
                  pl.BlockSpec((1, BLK, 2 * BLK), lambda h: (_table_of_head(h), 0, 0))],
        out_specs=pl.BlockSpec((1, 1, LANES), lambda h: (h, 0, 0)),
        compiler_params=_params(("parallel",)),
    )(dbias, tables)


def _band_layout(g, bias_div):
    assert g == 1 or bias_div == 1
    return bias_div if g == 1 else 1


def _band_specs(length, g, bias_div, offs):
    ns = _band_layout(g, bias_div)

    def seqs(off, div=1):
        return pl.BlockSpec((ns, length, HEAD_DIM), lambda s: (off // ns + s // div, 0, 0))

    xspecs = [seqs(offs[0]), seqs(offs[1], g), seqs(offs[2], g)]
    bspec = pl.BlockSpec((1, BLK, 2 * BLK), lambda s: (s, 0, 0))
    sspec = pl.BlockSpec((ns, 1, LANES), lambda s: (s, 0, 0))
    colspec = pl.BlockSpec((ns, length, 1), lambda s: (s, 0, 0))
    return xspecs, seqs(0), seqs(0, g), bspec, sspec, colspec


def _band_sweep(length, ns, one):
    nblk = length // BLK
    for qq in range(ns):
        if ns * nblk <= 16:
            for i in range(nblk):
                one(qq, i * BLK, max(i - 1, 0) * BLK, i == 0)
        else:
            def step(i, carry, qq=qq):
                one(qq, pl.multiple_of(i * BLK, BLK), pl.multiple_of(jnp.maximum(i - 1, 0) * BLK, BLK), i == 0)
                return carry

            lax.fori_loop(0, nblk, step, 0, unroll=2)


def _band_scores(q_ref, k_ref, b_ref, qq, kq, bq, cur, prv, first):
    qv = q_ref[qq, pl.ds(cur, BLK), :]
    bv = b_ref[bq]
    if first is True:
        sp = jnp.full((BLK, BLK), NEG, F32)
    else:
        sp = _dot(qv, k_ref[kq, pl.ds(prv, BLK), :], 1, 1) + bv[:, :BLK]
        sp = sp if first is False else jnp.where(first, NEG, sp)
    sc = _dot(qv, k_ref[kq, pl.ds(cur, BLK), :], 1, 1) + bv[:, BLK:]
    return qv, sp, sc


def _band_fwd(x, bias, sink, *, nq, offs, g, bias_div, has_sink, name):
    length = x.shape[1]
    ns = _band_layout(g, bias_div)

    def body(q_ref, k_ref, v_ref, b_ref, s_ref, o_ref, lse_ref):
        def one(qq, cur, prv, first):
            kq, bq = qq, 0
            _, sp, sc = _band_scores(q_ref, k_ref, b_ref, qq, kq, bq, cur, prv, first)
            m = jnp.maximum(jnp.max(sp, axis=1, keepdims=True), jnp.max(sc, axis=1, keepdims=True))
            if has_sink:
                sk = s_ref[qq][:, :1]
                m = jnp.maximum(m, sk)
            pp, pc = jnp.exp(sp - m), jnp.exp(sc - m)
            den = jnp.sum(pp, axis=1, keepdims=True) + jnp.sum(pc, axis=1, keepdims=True)
            if has_sink:
                den = den + jnp.exp(sk - m)
            acc = (_dot(pp.astype(BF16), v_ref[kq, pl.ds(prv, BLK), :], 1, 0)
                   + _dot(pc.astype(BF16), v_ref[kq, pl.ds(cur, BLK), :], 1, 0))
            o_ref[qq, pl.ds(cur, BLK), :] = acc / den
            lse_ref[qq, pl.ds(cur, BLK), :] = m + jnp.log(den)

        _band_sweep(length, ns, one)

    xspecs, qspec, _, bspec, sspec, colspec = _band_specs(length, g, bias_div, offs)
    return pl.pallas_call(
        body, name=name,
        out_shape=[jax.ShapeDtypeStruct((nq, length, HEAD_DIM), F32), jax.ShapeDtypeStruct((nq, length, 1), F32)],
        grid=(nq // ns,), in_specs=xspecs + [bspec, sspec],
        out_specs=[qspec, colspec], compiler_params=_params(("parallel",)),
    )(x, x, x, bias, sink)


def _band_bwd(x, bias, sink, o, lse, do, dlse, *, nq, offs, g, bias_div, has_sink, name):
    length = x.shape[1]
    ns = _band_layout(g, bias_div)
    nk, nbias = nq // g, nq // bias_div

    def body(q_ref, k_ref, v_ref, b_ref, s_ref, o_ref, lse_ref, do_ref, dlse_ref,
             dq_ref, dk_ref, dv_ref, db_ref, dsk_ref, dkp_ref, dvp_ref):
        for ref in (db_ref, dsk_ref, dkp_ref, dvp_ref):
            ref[...] = jnp.zeros_like(ref)

        @pl.when(pl.program_id(0) % g == 0)
        def _():
            dk_ref[...] = jnp.zeros_like(dk_ref)
            dv_ref[...] = jnp.zeros_like(dv_ref)

        def one(qq, cur, prv, first):
            kq, bq = qq, 0
            qv, sp, sc = _band_scores(q_ref, k_ref, b_ref, qq, kq, bq, cur, prv, first)
            rows, prow = pl.ds(cur, BLK), pl.ds(prv, BLK)
            lse_v = lse_ref[qq, rows, :]
            pp, pc = jnp.exp(sp - lse_v), jnp.exp(sc - lse_v)
            dov = do_ref[qq, rows, :]
            dob = dov.astype(BF16)
            coef = dlse_ref[qq, rows, :] - jnp.sum(dov * o_ref[qq, rows, :], axis=1, keepdims=True)
            dsp = pp * (_dot(dob, v_ref[kq, prow, :], 1, 1) + coef)
            dsc = pc * (_dot(dob, v_ref[kq, rows, :], 1, 1) + coef)
            dspb, dscb = dsp.astype(BF16), dsc.astype(BF16)
            dq_ref[qq, rows, :] = ((_dot(dspb, k_ref[kq, prow, :], 1, 0) + _dot(dscb, k_ref[kq, rows, :], 1, 0))
                                   * (HEAD_DIM ** -0.5))
            dk_ref[kq, rows, :] += _dot(dscb, qv, 0, 0)
            dkp_ref[kq, prow, :] += _dot(dspb, qv, 0, 0)
            dv_ref[kq, rows, :] += _dot(pc.astype(BF16), dob, 0, 0)
            dvp_ref[kq, prow, :] += _dot(pp.astype(BF16), dob, 0, 0)
            db_ref[bq, :, :BLK] += dsp
            db_ref[bq, :, BLK:] += dsc
            if has_sink:
                dsk_ref[qq] += jnp.sum(jnp.exp(s_ref[qq][:, :1] - lse_v) * coef)

        _band_sweep(length, ns, one)
        dk_ref[...] += dkp_ref[...]
        dv_ref[...] += dvp_ref[...]

    xspecs, qspec, kvspec, bspec, sspec, colspec = _band_specs(length, g, bias_div, offs)
    return pl.pallas_call(
        body, name=name,
        out_shape=[jax.ShapeDtypeStruct((nq, length, HEAD_DIM), F32), jax.ShapeDtypeStruct((nk, length, HEAD_DIM), F32),
                   jax.ShapeDtypeStruct((nk, length, HEAD_DIM), F32), jax.ShapeDtypeStruct((nbias, BLK, 2 * BLK), F32),
                   jax.ShapeDtypeStruct((nq, 1, LANES), F32)],
        grid=(nq // ns,),
        in_specs=xspecs + [bspec, sspec, qspec, colspec, qspec, colspec],
        out_specs=[qspec, kvspec, kvspec, bspec, sspec],
        scratch_shapes=[pltpu.VMEM((ns, length, HEAD_DIM), F32), pltpu.VMEM((ns, length, HEAD_DIM), F32)],
        compiler_params=_params(("arbitrary",)),
    )(x, x, x, bias, sink, o, lse, do, dlse)


TOK_TILE = 512


def _dil_merge(outs, lses, dout, name):
    tr = TOK_TILE
    dils = [d for _, d in DIL_PATTERNS]
    n = len(dils)
    o4 = [o.reshape(2, d, SEQ // d, HEAD_DIM) for o, d in zip(outs, dils)]
    l4 = [l.reshape(2, d, SEQ // d, 1) for l, d in zip(lses, dils)]
    o_specs = [pl.BlockSpec((2, d, tr // d, HEAD_DIM), lambda i: (0, 0, i, 0)) for d in dils]
    l_specs = [pl.BlockSpec((2, d, tr // d, 1), lambda i: (0, 0, i, 0)) for d in dils]
    tok = pl.BlockSpec((tr, 2 * HEAD_DIM), lambda i: (i, 0))
    scratch = ([pltpu.VMEM((tr, 2 * HEAD_DIM), F32) for _ in dils] + [pltpu.VMEM((tr, 1), F32) for _ in range(2 * n)]
               + [pltpu.VMEM((tr // d, 2 * HEAD_DIM), F32) for d in dils])

    def to_tokens(o_ref, l_ref, d, pair, cols, stage):
        for r in range(d):
            rows = pl.ds(r, tr // d, stride=d) if d > 1 else slice(None)
            stage[:, :HEAD_DIM] = o_ref[0, r]
            stage[:, HEAD_DIM:] = o_ref[1, r]
            pair[rows, :] = stage[...]
            for h in range(2):
                cols[h][rows, :] = l_ref[h, r]
        return pair[...], [cols[0][...], cols[1][...]]

    def weights(ls):
        left = lax.broadcasted_iota(jnp.int32, (tr, 2 * HEAD_DIM), 1) < HEAD_DIM
        per_head = []
        for h in range(2):
            m = ls[0][h]
            for g in range(1, n):
                m = jnp.maximum(m, ls[g][h])
            es = [jnp.exp(ls[g][h] - m) for g in range(n)]
            den = es[0]
            for e in es[1:]:
                den = den + e
            per_head.append([e / den for e in es])
        return per_head, [jnp.where(left, per_head[0][g], per_head[1][g]) for g in range(n)], left

    def load(refs):
        pairs, cols, stages = refs[:n], refs[n:3 * n], refs[3 * n:]
        return pairs, [cols[2 * g:2 * g + 2] for g in range(n)], stages

    if dout is None:
        def body(*refs):
            pairs, cols, stages = load(refs[2 * n + 1:])
            toks = [to_tokens(refs[g], refs[n + g], dils[g], pairs[g], cols[g], stages[g]) for g in range(n)]
            _, alphas, _ = weights([t[1] for t in toks])
            acc = alphas[0] * toks[0][0]
            for g in range(1, n):
                acc = acc + alphas[g] * toks[g][0]
            refs[2 * n][...] = acc

        return pl.pallas_call(
            body, name=name, out_shape=jax.ShapeDtypeStruct((SEQ, 2 * HEAD_DIM), F32), grid=(SEQ // tr,),
            in_specs=o_specs + l_specs, out_specs=tok, scratch_shapes=scratch, compiler_params=_params(("parallel",)),
        )(*o4, *l4)

    def body(*refs):
        do_refs, dl_refs = refs[2 * n + 1:3 * n + 1], refs[3 * n + 1:4 * n + 1]
        pairs, cols, stages = load(refs[4 * n + 1:])
        toks = [to_tokens(refs[g], refs[n + g], dils[g], pairs[g], cols[g], stages[g]) for g in range(n)]
        per_head, alphas, left = weights([t[1] for t in toks])
        dov = refs[2 * n][...]
        das = []
        for g in range(n):
            prod = dov * toks[g][0]
            das.append([jnp.sum(jnp.where(left, prod, 0.0), axis=1, keepdims=True),
                        jnp.sum(jnp.where(left, 0.0, prod), axis=1, keepdims=True)])
        dbar = [sum(per_head[h][g] * das[g][h] for g in range(n)) for h in range(2)]
        for g, d in enumerate(dils):
            pairs[g][...] = alphas[g] * dov
            for h in range(2):
                cols[g][h][...] = per_head[h][g] * (das[g][h] - dbar[h])
            for r in range(d):
                rows = pl.ds(r, tr // d, stride=d) if d > 1 else slice(None)
                v = pairs[g][rows, :]
                for h in range(2):
                    do_refs[g][h, r] = v[:, h * HEAD_DIM:(h + 1) * HEAD_DIM]
                    dl_refs[g][h, r] = cols[g][h][rows, :]

    out = pl.pallas_call(
        body, name=name,
        out_shape=[jax.ShapeDtypeStruct(o.shape, F32) for o in o4] + [jax.ShapeDtypeStruct(l.shape, F32) for l in l4],
        grid=(SEQ // tr,), in_specs=o_specs + l_specs + [tok], out_specs=o_specs + l_specs, scratch_shapes=scratch,
        compiler_params=_params(("parallel",)),
    )(*o4, *l4, dout)
    return [t.reshape(s.shape) for t, s in zip(out, list(outs) + list(lses))]


def _tri(cmp):
    r = lax.broadcasted_iota(jnp.int32, (SB_TILE, SB_TILE), 0)
    c = lax.broadcasted_iota(jnp.int32, (SB_TILE, SB_TILE), 1)
    return cmp(r, c).astype(BF16)


def _cum(x, tri, terms):
    acc, rest = None, x
    for _ in range(terms):
        part = rest.astype(BF16)
        rest = rest - part.astype(F32)
        d = _dot(part, tri, 1, 0)
        acc = d if acc is None else acc + d
    return acc


def _sb_logits(q, ks, diagonal):
    t = SB_TILE
    z = _dot(q, ks, 1, 1)
    e = jnp.exp(-jnp.abs(z))
    lf = -(jnp.maximum(z, 0.0) + jnp.log(1.0 + e))
    if not diagonal:
        return z, e, lf, None
    mask = lax.broadcasted_iota(jnp.int32, (t, t), 1) < lax.broadcasted_iota(jnp.int32, (t, t), 0)
    return z, e, jnp.where(mask, lf, 0.0), mask


def _sb_specs(h, s):
    t = SB_TILE
    tile = pl.BlockSpec((h, t, HEAD_DIM), lambda i: (0, i, 0))
    keys = pl.BlockSpec((h, s, HEAD_DIM), lambda i: (1, 0, 0))
    values = pl.BlockSpec((h, s, HEAD_DIM), lambda i: (2, 0, 0))
    return tile, keys, values, pl.BlockSpec((h, t, 1), lambda i: (0, i, 0))


def _sb_fwd(x, name):
    h, s = x.shape[0] // 3, x.shape[1]
    t = SB_TILE

    def body(q_ref, k_ref, v_ref, o_ref, tot_ref):
        i = pl.program_id(0)
        after = _tri(lambda r, c: r > c)

        def tile(j, carry, diagonal):
            rows = pl.ds(pl.multiple_of(j * t, t), t)
            out = []
            for hh, (right, acc) in enumerate(carry):
                z, _, lf, mask = _sb_logits(q_ref[hh], k_ref[hh, rows, :], diagonal)
                w = jnp.exp(z + lf + (right + _cum(lf, after, 2)))
                w = w if mask is None else jnp.where(mask, w, 0.0)
                out.append((right + jnp.sum(lf, axis=1, keepdims=True), acc + _dot(w.astype(BF16), v_ref[hh, rows, :], 1, 0)))
            return tuple(out)

        carry = tile(i, tuple((jnp.zeros((t, 1), F32), jnp.zeros((t, HEAD_DIM), F32)) for _ in range(h)), True)
        carry = lax.fori_loop(0, i, lambda jj, c: tile(i - 1 - jj, c, False), carry)
        for hh, (right, acc) in enumerate(carry):
            o_ref[hh] = acc
            tot_ref[hh] = right

    tile_spec, keys, values, col = _sb_specs(h, s)
    return pl.pallas_call(
        body, name=name, out_shape=[jax.ShapeDtypeStruct((h, s, HEAD_DIM), F32), jax.ShapeDtypeStruct((h, s, 1), F32)],
        grid=(s // t,), in_specs=[tile_spec, keys, values], out_specs=[tile_spec, col],
        compiler_params=_params(("parallel",)),
    )(x, x, x)


def _sb_bwd(x, tot, do, name):
    h, s = x.shape[0] // 3, x.shape[1]
    t = SB_TILE

    def body(q_ref, k_ref, v_ref, tot_ref, do_ref, dq_ref, dk_ref, dv_ref):
        i = pl.program_id(0)

        @pl.when(i == 0)
        def _():
            dk_ref[...] = jnp.zeros_like(dk_ref)
            dv_ref[...] = jnp.zeros_like(dv_ref)

        upto = _tri(lambda r, c: r <= c)
        before = _tri(lambda r, c: r < c)

        def tile(j, carry, diagonal):
            rows = pl.ds(pl.multiple_of(j * t, t), t)
            out = []
            for hh, (left, cleft, dq) in enumerate(carry):
                qv, ks, dob = q_ref[hh], k_ref[hh, rows, :], do_ref[hh].astype(BF16)
                z, e, lf, mask = _sb_logits(qv, ks, diagonal)
                between = tot_ref[hh] - (left + _cum(lf, upto, 2))
                w = jnp.exp(z + lf + between)
                w = w if mask is None else jnp.where(mask, w, 0.0)
                dlog = w * _dot(dob, v_ref[hh, rows, :], 1, 1)
                cfail = cleft + _cum(dlog, before, 2)
                sig = jnp.where(z >= 0.0, 1.0, e) / (1.0 + e)
                dz = dlog * (1.0 - sig) - sig * cfail
                dz = (dz if mask is None else jnp.where(mask, dz, 0.0)).astype(BF16)
                dk_ref[hh, rows, :] += _dot(dz, qv, 0, 0)
                dv_ref[hh, rows, :] += _dot(w.astype(BF16), dob, 0, 0)
                out.append((left + jnp.sum(lf, axis=1, keepdims=True), cleft + jnp.sum(dlog, axis=1, keepdims=True),
                            dq + _dot(dz, ks, 1, 0)))
            return tuple(out)

        zero = jnp.zeros((t, 1), F32)
        carry = lax.fori_loop(0, i, lambda j, c: tile(j, c, False),
                              tuple((zero, zero, jnp.zeros((t, HEAD_DIM), F32)) for _ in range(h)))
        for hh, (_, _, dq) in enumerate(tile(i, carry, True)):
            dq_ref[hh] = dq * (HEAD_DIM ** -0.5)

    tile_spec, keys, values, col = _sb_specs(h, s)
    full = pl.BlockSpec((h, s, HEAD_DIM), lambda i: (0, 0, 0))
    shp = jax.ShapeDtypeStruct((h, s, HEAD_DIM), F32)
    return pl.pallas_call(
        body, name=name, out_shape=[shp, shp, shp], grid=(s // t,),
        in_specs=[tile_spec, keys, values, col, tile_spec],
        out_specs=[tile_spec, full, full], compiler_params=_params(("arbitrary",)),
    )(x, x, x, tot, do)


COL_SB, COL_DIL, COL_SWA = 0, 3 * H_SB * HEAD_DIM, 3 * H_SB * HEAD_DIM + 3 * H_DIL * HEAD_DIM
N_SWA = H_SWA_Q + 2 * H_SWA_KV


def _dil_col(t, g):
    return COL_DIL + t * H_DIL * HEAD_DIM + g * 2 * HEAD_DIM


def _split_heads(qkv, name):
    tr = TOK_TILE
    scale = HEAD_DIM ** -0.5
    dils = [d for _, d in DIL_PATTERNS]

    def body(x_ref, sb_ref, d0_ref, d1_ref, d2_ref, swa_ref, pair):
        def head(col, scaled):
            v = x_ref[:, col:col + HEAD_DIM]
            return (v * scale if scaled else v).astype(BF16)

        for hh in range(3 * H_SB):
            sb_ref[hh] = head(COL_SB + hh * HEAD_DIM, hh < H_SB)
        for hh in range(N_SWA):
            swa_ref[hh] = head(COL_SWA + hh * HEAD_DIM, hh < H_SWA_Q)
        for t in range(3):
            for g, (d, out_ref) in enumerate(zip(dils, (d0_ref, d1_ref, d2_ref))):
                col = _dil_col(t, g)
                if d == 1:
                    for h in range(2):
                        out_ref[t * 2 + h] = head(col + h * HEAD_DIM, t == 0)
                    continue
                pair[...] = x_ref[:, col:col + 2 * HEAD_DIM]
                for r in range(d):
                    v = pair[pl.ds(r, tr // d, stride=d), :]
                    v = v * scale if t == 0 else v
                    for h in range(2):
                        out_ref[t * 2 * d + h * d + r] = v[:, h * HEAD_DIM:(h + 1) * HEAD_DIM].astype(BF16)

    def heads(n, length):
        return jax.ShapeDtypeStruct((n, length, HEAD_DIM), BF16)

    def spec(n, rows):
        return pl.BlockSpec((n, rows, HEAD_DIM), lambda i: (0, i, 0))

    return pl.pallas_call(
        body, name=name,
        out_shape=[heads(3 * H_SB, SEQ)] + [heads(6 * d, SEQ // d) for d in dils] + [heads(N_SWA, SEQ)],
        grid=(SEQ // tr,), in_specs=[pl.BlockSpec((tr, D_QKV), lambda i: (i, 0))],
        out_specs=[spec(3 * H_SB, tr)] + [spec(6 * d, tr // d) for d in dils] + [spec(N_SWA, tr)],
        scratch_shapes=[pltpu.VMEM((tr, 2 * HEAD_DIM), F32)], compiler_params=_params(("parallel",)),
    )(qkv)


def _join_heads(sb, dil, swa, name):
    tr = TOK_TILE
    dils = [d for _, d in DIL_PATTERNS]

    def body(*refs):
        sb_refs, dil_refs, swa_refs = refs[:3], [refs[3 + 3 * g:6 + 3 * g] for g in range(3)], refs[12:15]
        o_ref, pair, stages = refs[15], refs[16], refs[17:]

        def put(col, v):
            o_ref[:, col:col + v.shape[1]] = v.astype(BF16)

        for t in range(3):
            for h in range(H_SB):
                put(COL_SB + (t * H_SB + h) * HEAD_DIM, sb_refs[t][h])
        col = COL_SWA
        for ref in swa_refs:
            for h in range(ref.shape[0]):
                put(col, ref[h])
                col += HEAD_DIM
        for t in range(3):
            for g, d in enumerate(dils):
                ref, col = dil_refs[g][t], _dil_col(t, g)
                if d == 1:
                    for h in range(2):
                        put(col + h * HEAD_DIM, ref[h])
                    continue
                stage = stages[g - 1]
                for r in range(d):
                    stage[:, :HEAD_DIM] = ref[r]
                    stage[:, HEAD_DIM:] = ref[d + r]
                    pair[pl.ds(r, tr // d, stride=d), :] = stage[...]
                put(col, pair[...])

    def spec(n, rows):
        return pl.BlockSpec((n, rows, HEAD_DIM), lambda i: (0, i, 0))

    ins = list(sb) + [t for g in range(3) for t in dil[g]] + list(swa)
    in_specs = ([spec(H_SB, tr)] * 3 + [spec(2 * d, tr // d) for d in dils for _ in range(3)]
                + [spec(H_SWA_Q, tr), spec(H_SWA_KV, tr), spec(H_SWA_KV, tr)])
    return pl.pallas_call(
        body, name=name, out_shape=jax.ShapeDtypeStruct((SEQ, D_QKV), BF16), grid=(SEQ // tr,), in_specs=in_specs,
        out_specs=pl.BlockSpec((tr, D_QKV), lambda i: (i, 0)),
        scratch_shapes=[pltpu.VMEM((tr, 2 * HEAD_DIM), F32)] + [pltpu.VMEM((tr // d, 2 * HEAD_DIM), F32) for d in dils[1:]],
        compiler_params=_params(("parallel",)),
    )(*ins)


def _mixer_fwd(qkv, bias, sinks_l, tag):
    sb, d0, d1, d2, swa = _split_heads(qkv, name=f"split_heads_{tag}")
    st = {"sb": sb, "dil": (d0, d1, d2), "swa": swa}
    o_sb, st["sb_tot"] = _sb_fwd(sb, name=f"sb_fwd_{tag}")
    st["dil_out"], st["dil_lse"], st["dil_sink"] = [], [], []
    for gi, (_, d) in enumerate(DIL_PATTERNS):
        sink = jnp.zeros((2 * d, 1, LANES), F32)
        og, lg = _band_fwd(st["dil"][gi], bias[2 * gi:2 * gi + 2], sink, nq=2 * d, offs=(0, 2 * d, 4 * d), g=1, bias_div=d,
                           has_sink=False, name=f"dil{gi}_fwd_{tag}")
        st["dil_out"].append(og)
        st["dil_lse"].append(lg)
        st["dil_sink"].append(sink)
    o_dil = _dil_merge(st["dil_out"], st["dil_lse"], None, name=f"dil_merge_fwd_{tag}")
    st["swa_sink"] = jnp.broadcast_to(sinks_l.reshape(H_SWA_Q, 1, 1), (H_SWA_Q, 1, LANES))
    st["swa_out"] = _band_fwd(swa, bias[H_DIL:], st["swa_sink"], nq=H_SWA_Q, offs=(0, H_SWA_Q, H_SWA_Q + H_SWA_KV),
                              g=H_SWA_Q // H_SWA_KV, bias_div=1, has_sink=True, name=f"swa_fwd_{tag}")
    return (o_sb, o_dil, st["swa_out"][0]), st


def _mixer_bwd(st, bias, do_sb, do_dil, do_swa, tag):
    d_sb = _sb_bwd(st["sb"], st["sb_tot"], do_sb, name=f"sb_bwd_{tag}")
    dmerge = _dil_merge(st["dil_out"], st["dil_lse"], do_dil, name=f"dil_merge_bwd_{tag}")
    d_dil, dbs = [], []
    for gi, (_, d) in enumerate(DIL_PATTERNS):
        dq, dk, dv, db, _ = _band_bwd(st["dil"][gi], bias[2 * gi:2 * gi + 2], st["dil_sink"][gi], st["dil_out"][gi],
                                      st["dil_lse"][gi], dmerge[gi], dmerge[3 + gi], nq=2 * d, offs=(0, 2 * d, 4 * d),
                                      g=1, bias_div=d, has_sink=False, name=f"dil{gi}_bwd_{tag}")
        d_dil.append((dq, dk, dv))
        dbs.append(db)
    o_sw, l_sw = st["swa_out"]
    dq_sw, dk_sw, dv_sw, db_sw, dsink = _band_bwd(st["swa"], bias[H_DIL:], st["swa_sink"], o_sw, l_sw, do_swa,
                                                  jnp.zeros_like(l_sw), nq=H_SWA_Q, offs=(0, H_SWA_Q, H_SWA_Q + H_SWA_KV),
                                                  g=H_SWA_Q // H_SWA_KV, bias_div=1, has_sink=True, name=f"swa_bwd_{tag}")
    dqkv = _join_heads(d_sb, d_dil, (dq_sw, dk_sw, dv_sw), name=f"join_heads_{tag}")
    return dqkv, jnp.concatenate(dbs + [db_sw], 0), dsink[:, 0, 0]


PIECES = ("ffn0", "mix", "ffn1")


def _ffn_fwd(x_in, w, gain, mod_j, tag, after=None):
    st = {"x": x_in, "w": w}
    st["h"] = _norm_fwd(x_in, _row(gain), _row(mod_j[1]), _row(mod_j[0]), name=f"norm_fwd_{tag}", after=after)
    st["a"], st["u"], st["s"] = _ffn_up(st["h"], w["gate"], w["up"], name=f"up_{tag}")
    st["f"], x_out = _mm(st["s"], w["down"], res=x_in, colscale=_row(0.5 * mod_j[2]), emit_acc=True, tm=512, tn=1024,
                         name=f"down_{tag}")
    return x_out, st


def _ffn_bwd(dx_out, st, gain, mod_j, tag, done, pre, nxt):
    w = st["w"]

    def latest(new, old):
        return old if new is None else new

    df, dgate = pre or _gate_bwd(dx_out, st["f"], _row(0.5 * mod_j[2]), 0.5, name=f"gate_bwd_{tag}")
    dwd = _mm_tn(st["s"], df, tm=D_FF // 2, name=f"dwd_{tag}")
    token = latest(done({"down": dwd}), dwd)
    da, du = _ffn_bwd_ds(df, w["down"], st["a"], st["u"], name=f"ds_{tag}")
    dwg = _mm_tn(da, st["h"], after=token, tm=D_FF // 2, name=f"dwg_{tag}")
    token = latest(done({"gate": dwg}), dwg)
    dwu = _mm_tn(du, st["h"], after=token, tm=D_FF // 2, name=f"dwu_{tag}")
    token = latest(done({"up": dwu}), dwu)
    dx_in, sum_dh, sum_dhx, made = _dh_norm_bwd(da, w["gate"], du, w["up"], st["x"], dx_out, _row(gain), _row(mod_j[1]), nxt,
                                                after=token, name=f"dh_{tag}")
    dmod = jnp.concatenate([sum_dh, gain * sum_dhx, dgate], 0)
    return dx_in, dmod, (1.0 + mod_j[1]) * sum_dhx[0], made


def _mix_fwd(x_in, w, gain, mod_j, bias, sinks_l, tag, after=None):
    st = {"x": x_in, "w": w}
    st["h"] = _norm_fwd(x_in, _row(gain), _row(mod_j[1]), _row(mod_j[0]), name=f"norm_fwd_mix_{tag}", after=after)
    qkv = _mm(st["h"], w["in"], tb=True, tm=SEQ, b_rows=(0, D_QKV), name=f"qkv_{tag}")
    st["gates"] = _mm(st["h"], w["in"], tb=True, tm=SEQ, b_rows=(D_QKV, D_GATES), name=f"gates_{tag}")
    outs, st["mix"] = _mixer_fwd(qkv, bias, sinks_l, tag)
    st["merged"], *st["t"] = _merge_fwd(*outs, st["gates"], w["br_sb"], w["br_dil"], w["br_swa"], name=f"merge_fwd_{tag}")
    st["f"], x_out = _mm(st["merged"], w["out"], res=x_in, colscale=_row(mod_j[2]), emit_acc=True, name=f"out_{tag}")
    return x_out, st


def _mix_bwd(dx_out, st, gain, mod_j, bias, tag, done, pre, nxt):
    w = st["w"]
    df, dgate = pre or _gate_bwd(dx_out, st["f"], _row(mod_j[2]), 1.0, name=f"gate_bwd_mix_{tag}")
    g = {"out": _mm_tn(st["merged"], df, name=f"dw_out_{tag}")}
    dmerged = _mm(df, w["out"], tb=True, name=f"dmerged_{tag}")
    dgates, do_sb, do_dil, do_swa, dbr_sb, dbr_dil, dbr_swa = _merge_bwd(
        dmerged, *st["t"], st["gates"], w["br_sb"], w["br_dil"], w["br_swa"], name=f"merge_bwd_{tag}")
    g["br_sb"] = _mm_tn(st["t"][0], dbr_sb, name=f"dw_br_sb_{tag}")
    g["br_dil"] = _mm_tn(st["t"][1], dbr_dil, name=f"dw_br_dil_{tag}")
    g["br_swa"] = _mm_tn(st["t"][2], dbr_swa, name=f"dw_br_swa_{tag}")
    dqkv, dbias, dsinks = _mixer_bwd(st["mix"], bias, do_sb, do_dil, do_swa, tag)
    dw_qkv = _mm_tn(dqkv, st["h"], out_rows=D_QKV + D_GATES, name=f"dw_qkv_{tag}")
    g["in"] = _mm_tn(dgates, st["h"], out_rows=D_QKV + D_GATES, row0=D_QKV, prev=dw_qkv, name=f"dw_gates_{tag}")
    dx_in, sum_dh, sum_dhx, made = _dh_norm_bwd(dqkv, w["in"], dgates, w["in"], st["x"], dx_out, _row(gain), _row(mod_j[1]),
                                                nxt, after=done(g), b_rows=(0, D_QKV), name=f"dh_mix_{tag}")
    dmod = jnp.concatenate([sum_dh, gain * sum_dhx, dgate], 0)
    return dx_in, dmod, (1.0 + mod_j[1]) * sum_dhx[0], dbias, dsinks, made


def _local_step(x, target, mod, gains, weights_of, rel_bias, sinks, final_gain, grads_done):
    tables = jnp.asarray(_bucket_tables())
    bias = _bias_build(rel_bias, tables, name="bias_build")
    states, h = [], x
    for l in range(DEPTH):
        st = {}
        for j, piece in enumerate(PIECES):
            w, after = weights_of(l, piece, h)
            if piece == "mix":
                h, st[piece] = _mix_fwd(h, w, gains[l, j], mod[l, j], bias, sinks[l], f"l{l}", after)
            else:
                h, st[piece] = _ffn_fwd(h, w, gains[l, j], mod[l, j], f"{piece}_l{l}", after)
        states.append(st)
    loss, dx, dfinal = _final_loss(h, target, _row(final_gain), name="final_loss")
    dmods = [[None] * 3 for _ in range(DEPTH)]
    dgains = [[None] * 3 for _ in range(DEPTH)]
    dsinks = [None] * DEPTH
    dbias, made = None, None
    sweep = [(l, j) for l in reversed(range(DEPTH)) for j in reversed(range(3))]
    for k, (l, j) in enumerate(sweep):
        piece = PIECES[j]
        done = lambda grads, l=l, piece=piece: grads_done(l, piece, grads)
        nxt = None
        if k + 1 < len(sweep):
            nl, nj = sweep[k + 1]
            coef = 1.0 if PIECES[nj] == "mix" else 0.5
            nxt = (states[nl][PIECES[nj]]["f"], _row(coef * mod[nl, nj, 2]), coef)
        if piece == "mix":
            dx, dmods[l][j], dgains[l][j], db, dsinks[l], made = _mix_bwd(
                dx, states[l][piece], gains[l, j], mod[l, j], bias, f"l{l}", done, made, nxt)
            dbias = db if dbias is None else dbias + db
        else:
            dx, dmods[l][j], dgains[l][j], made = _ffn_bwd(
                dx, states[l][piece], gains[l, j], mod[l, j], f"{piece}_l{l}", done, made, nxt)
    drel = _bias_grad(dbias, tables, name="bias_grad")[:, 0, :N_BUCKETS].T
    dmod = jnp.stack([jnp.stack(m) for m in dmods])
    dgain = jnp.stack([jnp.stack(g) for g in dgains])
    return loss, dx, dmod, dgain, dfinal[0], drel, jnp.stack(dsinks)


BR_ROWS = (H_SB * HEAD_DIM, 2 * HEAD_DIM, H_SWA_Q * HEAD_DIM)


def _lanes_unshard(g, lead):
    _, rows, _ = g.shape
    r = rows // lead
    return g.reshape(N_DEV, lead, r, LANES).transpose(1, 2, 0, 3).reshape(lead, r, N_DEV * LANES)


def _lanes_shard(full):
    lead, r, _ = full.shape
    return full.reshape(lead, r, N_DEV, LANES).transpose(2, 0, 1, 3).reshape(N_DEV, lead * r, LANES)


def _pack_rows(parts, dtype):
    flat = jnp.concatenate([p.astype(dtype).reshape(-1) for p in parts])
    pad = (-flat.shape[0]) % (16 * LANES)
    if pad:
        flat = jnp.concatenate([flat, jnp.zeros((pad,), dtype)])
    return flat.reshape(-1, LANES)


def _unshard(gathered, axis):
    moved = jnp.moveaxis(gathered, 0, axis)
    shape = list(moved.shape)
    shape[axis:axis + 2] = [shape[axis] * shape[axis + 1]]
    return moved.reshape(shape)


def kernel(x, c, w_ada, b_ada, norm_gain, w_ffn_gate, w_ffn_up, w_ffn_down, w_in, w_br_sb, w_br_dil, w_br_swa, w_out, sinks, rel_bias, final_gain, loss_target, m_w_ada, m_b_ada, m_norm_gain, m_w_ffn_gate, m_w_ffn_up, m_w_ffn_down, m_w_in, m_w_br_sb, m_w_br_dil, m_w_br_swa, m_w_out, m_sinks, m_rel_bias, m_final_gain, v_w_ada, v_b_ada, v_norm_gain, v_w_ffn_gate, v_w_ffn_up, v_w_ffn_down, v_w_in, v_w_br_sb, v_w_br_dil, v_w_br_swa, v_w_out, v_sinks, v_rel_bias, v_final_gain):
    me = 4 * lax.axis_index("x") + 2 * lax.axis_index("y") + lax.axis_index("c")
    d = D_MODEL
    gate_t, up_t, in_t = jnp.swapaxes(w_ffn_gate, 2, 3), jnp.swapaxes(w_ffn_up, 2, 3), jnp.swapaxes(w_in, 1, 2)

    def piece_shards(l, piece):
        bf = lambda t: t.astype(BF16)
        if piece == "mix":
            return [bf(in_t[l]), jnp.concatenate([bf(w_br_sb[l]), bf(w_br_dil[l]), bf(w_br_swa[l])], 0), bf(w_out[l])]
        i = PIECES.index(piece) // 2
        return [bf(gate_t[l, i]), bf(up_t[l, i]), bf(w_ffn_down[l, i])]

    br_off = np.concatenate([[0], np.cumsum(BR_ROWS)])

    def piece_weights(gathered, piece):
        if piece == "mix":
            g_in, g_br, g_out = gathered
            f_br = [_lanes_unshard(g_br[:, br_off[k]:br_off[k + 1]], 1)[0] for k in range(3)]
            return {"in": g_in.reshape(D_QKV + D_GATES, d), "br_sb": f_br[0], "br_dil": f_br[1], "br_swa": f_br[2],
                    "out": g_out.reshape(d, d)}
        return {n: g.reshape(D_FF, d) for n, g in zip(("gate", "up", "down"), gathered)}

    order = [(l, piece) for l in range(DEPTH) for piece in PIECES]
    ahead = 3
    in_flight, passed = {}, {}

    def start_gather(k, after):
        l, piece = order[k]
        in_flight[k], token = _relay_start(piece_shards(l, piece), after, name=f"gather_{piece}_l{l}_start")
        return token

    small, = _all_gather([_pack_rows([c, norm_gain], F32)], after=start_gather(0, c), name="gather_cond")
    c_all = small[:, :d // LANES].reshape(N_DEV, d)
    gains = _unshard(small[:, d // LANES:d // LANES + 6].reshape(N_DEV, DEPTH, 3, LANES), 2)

    cols = w_ada.shape[2]
    mod_cols = jnp.stack([_ada_fwd(c_all, w_ada[l], name=f"ada_fwd_l{l}") for l in range(DEPTH)])
    mod_all, = _all_gather([_pack_rows([mod_cols], F32)], name="gather_mod")
    mod_all = mod_all.reshape(N_DEV, -1)[:, :DEPTH * N_DEV * cols].reshape(N_DEV, DEPTH, N_DEV, cols)
    mod_mine = lax.dynamic_index_in_dim(mod_all, me, axis=2, keepdims=False)
    mod = (mod_mine.transpose(1, 0, 2).reshape(DEPTH, N_DEV * cols) + b_ada).reshape(DEPTH, 3, 3, d)

    token = mod_all
    for k in range(1, 1 + ahead):
        token = start_gather(k, token)
    mod = mod + token[0, 0]

    def weights_of(l, piece, h):
        k = order.index((l, piece))
        token = start_gather(k + ahead, h) if k + ahead < len(order) and k + ahead not in in_flight else None
        for nxt in ([k] if k < 3 else []) + ([k + 1] if 3 <= k + 1 < len(order) else []):
            nl, npiece = order[nxt]
            passed[nxt], token = _relay_pass(in_flight[nxt], h if token is None else token,
                                             name=f"gather_{npiece}_l{nl}_pass")
        landed = _relay_wait(passed[k], h if token is None else token, name=f"gather_{piece}_l{l}_wait")
        return piece_weights(landed, piece), token

    exchanges, have, deferred = {}, {}, []

    def grads_done(l, piece, g):
        key = (l, piece)
        have.setdefault(key, {}).update(g)
        if piece == "mix":
            if len(have[key]) < 5:
                return None
            g = have[key]
            s_br = jnp.concatenate([_lanes_shard(g[n][None]) for n in ("br_sb", "br_dil", "br_swa")], 1)
            groups = [(("in", "br", "out"), [g["in"].reshape(N_DEV, -1, d), s_br, g["out"].reshape(N_DEV, -1, d)])]
        elif key == order[0]:
            deferred.extend(((n,), [t.reshape(N_DEV, -1, d)]) for n, t in g.items())
            return None
        elif len(have[key]) < 3:
            return None
        else:
            groups = [(("gate", "up", "down"), [have[key][n].reshape(N_DEV, -1, d) for n in ("gate", "up", "down")])]
        token = None
        for names, sg in groups:
            state, token = _exchange_start(sg, None, name=f"exchange_{piece}_l{l}_{names[0]}_start")
            exchanges.setdefault(key, []).append((names, state))
        return token

    loss, dx, dmod, dgains, dfinal, drel, dsinks = _local_step(
        x[0], loss_target[0], mod, gains, weights_of, rel_bias, sinks, final_gain, grads_done)

    flat = lambda t: t.reshape(-1, t.shape[-1])
    transposed = lambda ts: tuple(flat(jnp.swapaxes(t, -1, -2)) for t in ts)
    families = {
        "gate": transposed((w_ffn_gate, m_w_ffn_gate, v_w_ffn_gate)), "up": transposed((w_ffn_up, m_w_ffn_up, v_w_ffn_up)),
        "down": tuple(flat(t) for t in (w_ffn_down, m_w_ffn_down, v_w_ffn_down)),
        "in": transposed((w_in, m_w_in, v_w_in)),
        "br": tuple(flat(jnp.concatenate(ts, 1)) for ts in ((w_br_sb, w_br_dil, w_br_swa), (m_w_br_sb, m_w_br_dil, m_w_br_swa),
                                                            (v_w_br_sb, v_w_br_dil, v_w_br_swa))),
        "out": tuple(flat(t) for t in (w_out, m_w_out, v_w_out))}
    stepped = {}

    def step(keys, after):
        for l, piece in keys:
            for names, ex_state in exchanges[l, piece]:
                landed = _exchange_wait(ex_state, after, name=f"exchange_{piece}_l{l}_{names[0]}_wait")
                after = landed[0]
                for n, group in zip(names, landed):
                    w2, m2, v2 = families[n]
                    rows = group.shape[1]
                    row0 = (2 * l + PIECES.index(piece) // 2) * rows if piece != "mix" else l * rows
                    stepped[n] = _reduce_adamw([group], w2, m2, v2, row0, stepped.get(n), after=after,
                                               name=f"reduce_adamw_{n}_{piece}_l{l}")
                    after = stepped[n][1]
        return after

    small_parts = [dmod, dgains, dfinal, drel.T, dsinks, loss[0, :1]]
    small_sizes = [int(np.prod(p.shape)) for p in small_parts]
    small_all, = _all_gather([_pack_rows(small_parts, F32)], name="gather_small")
    token = small_all
    for names, sg in deferred:
        state, token = _exchange_start(sg, token, name=f"exchange_ffn0_l0_{names[0]}_start")
        exchanges.setdefault(order[0], []).append((names, state))

    after_l1 = step([key for key in reversed(order) if key[0] == 1], token)
    small_sum = _sum_parts(small_all, name="sum_small", after=token).reshape(-1)
    offs = np.concatenate([[0], np.cumsum(small_sizes)])
    g_b_ada = small_sum[offs[0]:offs[1]].reshape(DEPTH, 9 * d)
    g_gain_full = small_sum[offs[1]:offs[2]].reshape(DEPTH, 3, d)
    g_norm_gain = lax.dynamic_slice_in_dim(g_gain_full, me * LANES, LANES, axis=2)
    g_final = small_sum[offs[2]:offs[3]]
    g_rel = small_sum[offs[3]:offs[4]].reshape(N_SOFT, N_BUCKETS).T
    g_sinks = small_sum[offs[4]:offs[5]].reshape(DEPTH, H_SWA_Q)
    loss_total = small_sum[offs[5]]

    dmod_all = small_all.reshape(N_DEV, -1)[:, :DEPTH * 9 * d].reshape(N_DEV, DEPTH, 9 * d)
    dmod_cols = lax.dynamic_slice_in_dim(dmod_all, me * cols, cols, axis=2)
    g_w_ada = jnp.stack([_ada_bwd(c_all.T, dmod_cols[:, l], name=f"ada_bwd_l{l}") for l in range(DEPTH)])

    small_state = {"w_ada": (w_ada, m_w_ada, v_w_ada), "b_ada": (b_ada, m_b_ada, v_b_ada),
                   "norm_gain": (norm_gain, m_norm_gain, v_norm_gain), "sinks": (sinks, m_sinks, v_sinks),
                   "rel_bias": (rel_bias, m_rel_bias, v_rel_bias), "final_gain": (final_gain, m_final_gain, v_final_gain)}
    after = step([order[2], order[1]], after_l1)
    grad, update = {}, {}
    for n, g in (("w_ada", g_w_ada), ("b_ada", g_b_ada), ("norm_gain", g_norm_gain), ("sinks", g_sinks),
                 ("rel_bias", g_rel), ("final_gain", g_final)):
        w, m, v = small_state[n]
        grad[n] = g
        if w.ndim == 1:
            update[n] = tuple(t.reshape(w.shape)
                              for t in _adamw(_row(w), _row(g), _row(m), _row(v), name=f"adamw_{n}", after=after))
        else:
            update[n] = _adamw(w, g, m, v, name=f"adamw_{n}", after=after)
        after = update[n][0]

    step([order[0]], after)

    def unflat(n, like, swapped):
        shape = jnp.swapaxes(like, -1, -2).shape if swapped else like.shape
        out = [t.reshape(shape) for t in stepped[n]]
        return [jnp.swapaxes(t, -1, -2) for t in out] if swapped else out

    results = {"w_ffn_gate": unflat("gate", w_ffn_gate, True), "w_ffn_up": unflat("up", w_ffn_up, True),
               "w_ffn_down": unflat("down", w_ffn_down, False), "w_in": unflat("in", w_in, True),
               "w_out": unflat("out", w_out, False)}
    br = [t.reshape(DEPTH, -1, LANES) for t in stepped["br"]]
    for k, n in enumerate(("w_br_sb", "w_br_dil", "w_br_swa")):
        results[n] = [t[:, br_off[k]:br_off[k + 1]] for t in br]
    for n, (g, dl, nm, nv) in results.items():
        grad[n], update[n] = g, (dl, nm, nv)

    names = ["w_ada", "b_ada", "norm_gain", "w_ffn_gate", "w_ffn_up", "w_ffn_down", "w_in", "w_br_sb", "w_br_dil",
             "w_br_swa", "w_out", "sinks", "rel_bias", "final_gain"]
    return (loss_total, dx[None], *[grad[n] for n in names], *[update[n][0] for n in names],
            *[update[n][1] for n in names], *[update[n][2] for n in names])
```

```python
import math

import numpy as np
import jax
import jax.numpy as jnp
from jax import lax
from jax.experimental import pallas as pl
from jax.experimental.pallas import tpu as pltpu

F32, BF16 = jnp.float32, jnp.bfloat16

SEQ, D_MODEL, D_FF, HEAD_DIM = 2048, 1024, 2816, 64
DEPTH = 2
BLK = 128
H_SB, H_DIL, H_SWA_Q, H_SWA_KV = 4, 6, 6, 2
DIL_PATTERNS = ((128, 1), (512, 4), (2048, 16))
SWA_WINDOW = 128
N_BUCKETS, MAX_REL_DIST = 32, 2048
RMS_EPS = 1e-6
D_QKV = 2560
D_GATES = 3 * D_MODEL
ADAM_LR, ADAM_B1, ADAM_B2, ADAM_EPS, ADAM_WD, ADAM_STEP = 0.001, 0.9, 0.999, 1e-08, 0.01, 10

N_DEV = 8
LANES = 128
NEG = -1e30
SB_TILE = 512
VMEM_LIMIT_BYTES = 48 * 1024 * 1024
HBM = pl.BlockSpec(memory_space=pltpu.HBM)
MESH = pl.DeviceIdType.MESH


def _tile(n, target):
    t = (min(n, target) // LANES) * LANES
    while t >= LANES:
        if n % t == 0:
            return t
        t -= LANES
    return n


def _row_tile(r, cap):
    t = (min(r, cap) // 16) * 16
    while t > 16 and r % t:
        t -= 16
    return t


def _params(semantics=None):
    return pltpu.CompilerParams(dimension_semantics=semantics, vmem_limit_bytes=VMEM_LIMIT_BYTES)


def _dot(a, b, ca, cb):
    return lax.dot_general(a, b, (((ca,), (cb,)), ((), ())), preferred_element_type=F32)


def _sigmoid(a):
    return 1.0 / (1.0 + jnp.exp(-a))


def _row(v):
    return v.reshape(1, -1)


def _all_gather(arrs, name, after=None):
    n = len(arrs)
    ins = list(arrs) + ([] if after is None else [after])

    def body(*refs):
        x_refs, out_refs = refs[:n], refs[len(ins):len(ins) + n]
        send_sems, recv_sems, local_sems = refs[len(ins) + n:]
        x, y, c = lax.axis_index("x"), lax.axis_index("y"), lax.axis_index("c")
        me, sibling = (x, y, c), (x, y, 1 - c)
        chips = [(1 - x, y), (x, 1 - y), (1 - x, 1 - y)]

        def slot(t, px, py, pc):
            return out_refs[t].at[4 * px + 2 * py + pc]

        def copy(t, k, block, to, src=None):
            return pltpu.make_async_remote_copy(
                src_ref=slot(t, *block) if src is None else src, dst_ref=slot(t, *block),
                send_sem=send_sems.at[7 * t + k], recv_sem=recv_sems.at[7 * t + k], device_id=to, device_id_type=MESH)

        mine = [pltpu.make_async_copy(x_refs[t], slot(t, *me), local_sems.at[t]) for t in range(n)]
        for cp in mine:
            cp.start()
        first = []
        for t in range(n):
            first.append(copy(t, 0, me, sibling, src=x_refs[t]))
            first += [copy(t, 1 + j, me, (*chip, c), src=x_refs[t]) for j, chip in enumerate(chips)]
        for cp in first:
            cp.start()
        passed = []
        for j, chip in enumerate(chips):
            for t in range(n):
                copy(t, 1 + j, (*chip, c), me).wait_recv()
                passed.append(copy(t, 4 + j, (*chip, c), sibling))
                passed[-1].start()
        for t in range(n):
            copy(t, 0, sibling, me).wait_recv()
        for j, chip in enumerate(chips):
            for t in range(n):
                copy(t, 4 + j, (*chip, 1 - c), me).wait_recv()
        for cp in first + passed:
            cp.wait_send()
        for cp in mine:
            cp.wait()

    return pl.pallas_call(
        body, name=name, out_shape=[jax.ShapeDtypeStruct((N_DEV,) + a.shape, a.dtype) for a in arrs],
        in_specs=[HBM] * n + [pl.BlockSpec(memory_space=pl.ANY)] * (len(ins) - n), out_specs=[HBM] * n,
        scratch_shapes=[pltpu.SemaphoreType.DMA((7 * n,)), pltpu.SemaphoreType.DMA((7 * n,)), pltpu.SemaphoreType.DMA((n,))],
    )(*ins)


def _direct_copies(x_refs, land_refs, send_sems, recv_sems, local_sems):
    x, y, c = lax.axis_index("x"), lax.axis_index("y"), lax.axis_index("c")
    me = 4 * x + 2 * y + c
    sends, recvs = [], []
    for k in range(1, N_DEV):
        px = 1 - x if (k >> 2) & 1 else x
        py = 1 - y if (k >> 1) & 1 else y
        pc = 1 - c if k & 1 else c
        peer = 4 * px + 2 * py + pc
        for t, (x_ref, land_ref) in enumerate(zip(x_refs, land_refs)):
            sem = 7 * t + k - 1
            for out, src, slot in ((sends, peer, me), (recvs, me, peer)):
                out.append(pltpu.make_async_remote_copy(
                    src_ref=x_ref.at[src], dst_ref=land_ref.at[slot], send_sem=send_sems.at[sem],
                    recv_sem=recv_sems.at[sem], device_id=(px, py, pc), device_id_type=MESH))
    own = [pltpu.make_async_copy(x_ref.at[me], land_ref.at[me], local_sems.at[t])
           for t, (x_ref, land_ref) in enumerate(zip(x_refs, land_refs))]
    return sends, recvs, own


SEM =pl.BlockSpec(memory_space=pltpu.SEMAPHORE)
ANY = pl.BlockSpec(memory_space=pl.ANY)
SIDE_EFFECT = pltpu.SideEffectType.DATAFLOW_SIDE_EFFECTING


def _exchange_start(arrs, after, *, name):
    n = len(arrs)
    lands = [lax.empty(a.shape, a.dtype) for a in arrs]
    extra = [] if after is None else [after]

    def body(*refs):
        sems = refs[2 * n + len(extra):2 * n + len(extra) + 3]
        sends, _, own = _direct_copies(refs[:n], refs[n:2 * n], *sems)
        for cp in own + sends:
            cp.start()
        refs[-1][...] = jnp.zeros_like(refs[-1])

    ops = [pltpu.with_memory_space_constraint(a, pltpu.HBM) for a in list(arrs) + lands]
    out = pl.pallas_call(
        body, name=name,
        out_shape=(pltpu.SemaphoreType.DMA((7 * n,)), pltpu.SemaphoreType.DMA((7 * n,)), pltpu.SemaphoreType.DMA((n,)),
                   *[pltpu.HBM(a.shape, a.dtype) for a in ops], jax.ShapeDtypeStruct((8, LANES), F32)),
        in_specs=[HBM] * (2 * n) + [ANY] * len(extra),
        out_specs=(SEM, SEM, SEM, *[HBM] * (2 * n), pl.BlockSpec(memory_space=pltpu.VMEM)),
        input_output_aliases={t: 3 + t for t in range(2 * n)},
        compiler_params=pltpu.CompilerParams(has_side_effects=SIDE_EFFECT),
    )(*ops, *extra)
    return (out[:3], out[3:3 + n], out[3 + n:3 + 2 * n]), out[-1]


def _exchange_wait(state, after, *, name):
    sems, arrs, lands = state
    n = len(arrs)

    def body(*refs):
        sends, recvs, own = _direct_copies(refs[:n], refs[n:2 * n], *refs[2 * n:2 * n + 3])
        for cp in own:
            cp.wait()
        for cp in sends:
            cp.wait_send()
        for cp in recvs:
            cp.wait_recv()

    out = pl.pallas_call(
        body, name=name, out_shape=tuple(pltpu.HBM(a.shape, a.dtype) for a in list(arrs) + list(lands)),
        in_specs=[HBM] * (2 * n) + [SEM, SEM, SEM, ANY], out_specs=tuple([HBM] * (2 * n)),
        input_output_aliases={t: t for t in range(2 * n)},
        compiler_params=pltpu.CompilerParams(has_side_effects=SIDE_EFFECT),
    )(*arrs, *lands, *sems, after)
    return out[n:]


def _relay_copies(x_refs, land_refs, sems_a, sems_b):
    x, y, c = lax.axis_index("x"), lax.axis_index("y"), lax.axis_index("c")
    me = 4 * x + 2 * y + c
    sibling = (x, y, 1 - c)
    chips = [(1 - x, y), (x, 1 - y), (1 - x, 1 - y)]

    def slot(px, py, pc):
        return 4 * px + 2 * py + pc

    def copy(src, land_ref, dst_slot, send_sems, recv_sems, k, to):
        return pltpu.make_async_remote_copy(src_ref=src, dst_ref=land_ref.at[dst_slot], send_sem=send_sems.at[k],
                                            recv_sem=recv_sems.at[k], device_id=to, device_id_type=MESH)

    a_send, a_recv, a_own, b_send, b_recv = [], [], [], [], []
    for t, (x_ref, land_ref) in enumerate(zip(x_refs, land_refs)):
        peers = [sibling] + [(*chip, c) for chip in chips]
        if sems_a is not None:
            for k, peer in enumerate(peers):
                a_send.append(copy(x_ref, land_ref, me, sems_a[0], sems_a[1], 4 * t + k, peer))
                a_recv.append(copy(x_ref, land_ref, slot(*peer), sems_a[0], sems_a[1], 4 * t + k, peer))
            a_own.append(pltpu.make_async_copy(x_ref, land_ref.at[me], sems_a[2].at[t]))
        if sems_b is not None:
            for j, chip in enumerate(chips):
                b_send.append(copy(land_ref.at[slot(*chip, c)], land_ref, slot(*chip, c), sems_b[0], sems_b[1], 3 * t + j, sibling))
                b_recv.append(copy(land_ref.at[slot(*chip, c)], land_ref, slot(*chip, 1 - c), sems_b[0], sems_b[1], 3 * t + j,
                                   sibling))
    return (a_send, a_recv, a_own), (b_send, b_recv)


def _relay_start(arrs, after, name):
    n = len(arrs)
    lands = [lax.empty((N_DEV,) + a.shape, a.dtype) for a in arrs]

    def body(*refs):
        (sends, _, own), _ = _relay_copies(refs[:n], refs[n:2 * n], refs[2 * n + 1:2 * n + 4], None)
        for cp in own + sends:
            cp.start()
        refs[-1][...] = jnp.zeros_like(refs[-1])

    ops = [pltpu.with_memory_space_constraint(a, pltpu.HBM) for a in list(arrs) + lands]
    out = pl.pallas_call(
        body, name=name,
        out_shape=(pltpu.SemaphoreType.DMA((4 * n,)), pltpu.SemaphoreType.DMA((4 * n,)), pltpu.SemaphoreType.DMA((n,)),
                   *[pltpu.HBM(a.shape, a.dtype) for a in ops], jax.ShapeDtypeStruct((8, LANES), F32)),
        in_specs=[HBM] * (2 * n) + [ANY],
        out_specs=(SEM, SEM, SEM, *[HBM] * (2 * n), pl.BlockSpec(memory_space=pltpu.VMEM)),
        input_output_aliases={t: 3 + t for t in range(2 * n)},
        compiler_params=pltpu.CompilerParams(has_side_effects=SIDE_EFFECT),
    )(*ops, after)
    return (out[:3], out[3:3 + n], out[3 + n:3 + 2 * n]), out[-1]


def _relay_pass(state, after, name):
    sems_a, arrs, lands = state
    n = len(arrs)

    def body(*refs):
        sems_b = refs[2 * n + 4:2 * n + 6]
        (a_send, a_recv, a_own), (b_send, _) = _relay_copies(refs[:n], refs[n:2 * n], refs[2 * n:2 * n + 3], sems_b)
        for cp in a_own:
            cp.wait()
        for cp in a_send:
            cp.wait_send()
        for cp in a_recv:
            cp.wait_recv()
        for cp in b_send:
            cp.start()
        refs[-1][...] = jnp.zeros_like(refs[-1])

    out = pl.pallas_call(
        body, name=name,
        out_shape=(pltpu.SemaphoreType.DMA((3 * n,)), pltpu.SemaphoreType.DMA((3 * n,)),
                   *[pltpu.HBM(a.shape, a.dtype) for a in list(arrs) + list(lands)], jax.ShapeDtypeStruct((8, LANES), F32)),
        in_specs=[HBM] * (2 * n) + [SEM, SEM, SEM, ANY],
        out_specs=(SEM, SEM, *[HBM] * (2 * n), pl.BlockSpec(memory_space=pltpu.VMEM)),
        input_output_aliases={t: 2 + t for t in range(2 * n)},
        compiler_params=pltpu.CompilerParams(has_side_effects=SIDE_EFFECT),
    )(*arrs, *lands, *sems_a, after)
    return (out[:2], out[2:2 + n], out[2 + n:2 + 2 * n]), out[-1]


def _relay_wait(state, after, name):
    sems_b, arrs, lands = state
    n = len(arrs)

    def body(*refs):
        _, (b_send, b_recv) = _relay_copies(refs[:n], refs[n:2 * n], None, refs[2 * n:2 * n + 2])
        for cp in b_send:
            cp.wait_send()
        for cp in b_recv:
            cp.wait_recv()

    out = pl.pallas_call(
        body, name=name, out_shape=tuple(pltpu.HBM(a.shape, a.dtype) for a in list(arrs) + list(lands)),
        in_specs=[HBM] * (2 * n) + [SEM, SEM, ANY], out_specs=tuple([HBM] * (2 * n)),
        input_output_aliases={t: t for t in range(2 * n)},
        compiler_params=pltpu.CompilerParams(has_side_effects=SIDE_EFFECT),
    )(*arrs, *lands, *sems_b, after)
    return out[n:]


def _sum_parts(parts, name, after=None):
    n, r, cdim = parts.shape
    tr = _row_tile(r, max(16, (1 << 21) // (n * cdim * parts.dtype.itemsize)))

    def body(p_ref, *rest):
        acc = p_ref[0].astype(F32)
        for k in range(1, n):
            acc = acc + p_ref[k].astype(F32)
        rest[-1][...] = acc

    ins = [parts] + ([] if after is None else [after])
    return pl.pallas_call(
        body, name=name, out_shape=jax.ShapeDtypeStruct((r, cdim), F32), grid=(r // tr,),
        in_specs=[pl.BlockSpec((n, tr, cdim), lambda i: (0, i, 0))] + [ANY] * (len(ins) - 1),
        out_specs=pl.BlockSpec((tr, cdim), lambda i: (i, 0)), compiler_params=_params(("parallel",)),
    )(*ins)


def _mm_tn(a, b, *, name, after=None, tm=512, tn=1024, out_rows=None, row0=0, prev=None):
    k, m = a.shape
    n = b.shape[1]
    tm, tn = _tile(m, tm), _tile(n, tn)
    out_rows = m if out_rows is None else out_rows

    def body(a_ref, b_ref, *rest):
        o_ref, at_ref = rest[-2], rest[-1]

        @pl.when(pl.program_id(1) == 0)
        def _():
            at_ref[...] = a_ref[...].astype(BF16).T

        o_ref[...] = _dot(at_ref[...], b_ref[...].astype(BF16), 1, 0).astype(BF16)

    ins = [a, b] + [t for t in (after, prev) if t is not None]
    return pl.pallas_call(
        body, name=name, out_shape=jax.ShapeDtypeStruct((out_rows, n), BF16), grid=(m // tm, n // tn),
        in_specs=[pl.BlockSpec((k, tm), lambda i, j: (0, i)), pl.BlockSpec((k, tn), lambda i, j: (0, j))] + [ANY] * (len(ins) - 2),
        out_specs=pl.BlockSpec((tm, tn), lambda i, j: (row0 // tm + i, j)),
        input_output_aliases={} if prev is None else {len(ins) - 1: 0},
        scratch_shapes=[pltpu.VMEM((tm, k), BF16)], compiler_params=_params(("parallel", "arbitrary")),
    )(*ins)


def _mm(a, b, *, name, ta=False, tb=False, res=None, colscale=None, emit_acc=False,
        out_dtype=F32, tm=512, tn=512, b_rows=None):
    m, k = (a.shape[1], a.shape[0]) if ta else a.shape
    n = b.shape[0] if tb else b.shape[1]
    b_start = 0
    if b_rows is not None:
        b_start, n = b_rows
    tm, tn = _tile(m, tm), _tile(n, tn)
    ca, cb = (0 if ta else 1), (1 if tb else 0)
    a_spec = pl.BlockSpec((k, tm), lambda i, j: (0, i)) if ta else pl.BlockSpec((tm, k), lambda i, j: (i, 0))
    b_spec = (pl.BlockSpec((tn, k), lambda i, j: (b_start // tn + j, 0)) if tb
              else pl.BlockSpec((k, tn), lambda i, j: (0, j)))
    tile = pl.BlockSpec((tm, tn), lambda i, j: (i, j))
    ins, in_specs = [a, b], [a_spec, b_spec]
    if res is not None:
        ins.append(res)
        in_specs.append(tile)
    if colscale is not None:
        ins.append(colscale)
        in_specs.append(pl.BlockSpec((1, tn), lambda i, j: (0, j)))
    n_in = len(ins)

    def body(*refs):
        outs = refs[n_in:]
        acc = _dot(refs[0][...].astype(BF16), refs[1][...].astype(BF16), ca, cb)
        val, p = acc, 2
        if res is not None:
            r_val, p = refs[p][...], p + 1
        if colscale is not None:
            val = val * refs[p][...]
        if res is not None:
            val = r_val + val
        if emit_acc:
            outs[0][...] = acc
        outs[-1][...] = val.astype(out_dtype)

    out_shape = [jax.ShapeDtypeStruct((m, n), out_dtype)]
    out_specs = [tile]
    if emit_acc:
        out_shape.insert(0, jax.ShapeDtypeStruct((m, n), F32))
        out_specs.insert(0, tile)
    out = pl.pallas_call(
        body, name=name, out_shape=out_shape, grid=(m // tm, n // tn), in_specs=in_specs, out_specs=out_specs,
        compiler_params=_params(("parallel", "parallel")),
    )(*ins)
    return out if emit_acc else out[0]


def _norm_fwd(x, g, scale, shift, name, after=None):
    s, d = x.shape
    tr = TOK_TILE

    def body(x_ref, g_ref, sc_ref, sh_ref, *rest):
        xv = x_ref[...]
        rstd = lax.rsqrt(jnp.mean(xv * xv, axis=-1, keepdims=True) + RMS_EPS)
        rest[-1][...] = (xv * rstd * g_ref[...] * (1.0 + sc_ref[...]) + sh_ref[...]).astype(BF16)

    rowspec = pl.BlockSpec((1, d), lambda i: (0, 0))
    ins = [x, g, scale, shift] + ([] if after is None else [after])
    return pl.pallas_call(
        body, name=name, out_shape=jax.ShapeDtypeStruct((s, d), BF16), grid=(s // tr,),
        in_specs=[pl.BlockSpec((tr, d), lambda i: (i, 0)), rowspec, rowspec, rowspec] + [ANY] * (len(ins) - 4),
        out_specs=pl.BlockSpec((tr, d), lambda i: (i, 0)),
        compiler_params=_params(("parallel",)),
    )(*ins)


def _dh_norm_bwd(a1, b1, a2, b2, x, dres, g, scale, nxt, *, name, after=None, b_rows=None):
    s, d = x.shape
    tm = TOK_TILE
    n_fixed = 8
    resident = pl.Buffered(1)

    def body(a1_ref, b1_ref, a2_ref, b2_ref, x_ref, dr_ref, g_ref, sc_ref, *rest):
        rest = rest[(1 if after is not None else 0):]
        if nxt is not None:
            f_ref, cs_ref, dx_ref, sa_ref, sb_ref, df_ref, dg_ref = rest
        else:
            dx_ref, sa_ref, sb_ref = rest

        @pl.when(pl.program_id(0) == 0)
        def _():
            sa_ref[...] = jnp.zeros_like(sa_ref)
            sb_ref[...] = jnp.zeros_like(sb_ref)
            if nxt is not None:
                dg_ref[...] = jnp.zeros_like(dg_ref)

        dhv = (_dot(a1_ref[...].astype(BF16), b1_ref[...], 1, 0) + _dot(a2_ref[...].astype(BF16), b2_ref[...], 1, 0))
        xv = x_ref[...]
        rstd = lax.rsqrt(jnp.mean(xv * xv, axis=-1, keepdims=True) + RMS_EPS)
        xhat = xv * rstd
        dxhat = dhv * (g_ref[...] * (1.0 + sc_ref[...]))
        mean_term = jnp.mean(dxhat * xhat, axis=-1, keepdims=True)
        dxv = dr_ref[...] + rstd * (dxhat - xhat * mean_term)
        dx_ref[...] = dxv
        sa_ref[...] += jnp.sum(dhv, axis=0, keepdims=True)
        sb_ref[...] += jnp.sum(dhv * xhat, axis=0, keepdims=True)
        if nxt is not None:
            df_ref[...] = (dxv * cs_ref[...]).astype(BF16)
            dg_ref[...] += nxt[2] * jnp.sum(dxv * f_ref[...], axis=0, keepdims=True)

    def a_spec(t):
        return pl.BlockSpec((tm, t.shape[1]), lambda i: (i, 0))

    def b_spec(t, a, which):
        if b_rows is None:
            return pl.BlockSpec((t.shape[0], d), lambda i: (0, 0), pipeline_mode=resident)
        start = b_rows[which]
        return pl.BlockSpec((pl.Element(a.shape[1]), pl.Element(d)), lambda i: (start, 0), pipeline_mode=resident)

    rowspec = pl.BlockSpec((1, d), lambda i: (0, 0))
    tile = pl.BlockSpec((tm, d), lambda i: (i, 0))
    ins = [a1, b1, a2, b2, x, dres, g, scale] + ([] if after is None else [after])
    in_specs = [a_spec(a1), b_spec(b1, a1, 0), a_spec(a2), b_spec(b2, a2, 1), tile, tile, rowspec, rowspec]
    in_specs += [ANY] * (len(ins) - n_fixed)
    out_shape = [jax.ShapeDtypeStruct((s, d), F32), jax.ShapeDtypeStruct((1, d), F32), jax.ShapeDtypeStruct((1, d), F32)]
    out_specs = [tile, rowspec, rowspec]
    if nxt is not None:
        ins += [nxt[0], nxt[1]]
        in_specs += [tile, rowspec]
        out_shape += [jax.ShapeDtypeStruct((s, d), BF16), jax.ShapeDtypeStruct((1, d), F32)]
        out_specs += [tile, rowspec]
    out = pl.pallas_call(
        body, name=name, out_shape=out_shape, grid=(s // tm,), in_specs=in_specs, out_specs=out_specs,
        compiler_params=_params(("arbitrary",)),
    )(*ins)
    return out[0], out[1], out[2], (None if nxt is None else (out[3], out[4]))


def _gate_bwd(dxn, f, colscale, coef, name):
    s, d = dxn.shape
    tr = 256

    def body(dx_ref, f_ref, cs_ref, df_ref, dg_ref):
        @pl.when(pl.program_id(0) == 0)
        def _():
            dg_ref[...] = jnp.zeros_like(dg_ref)

        dxv = dx_ref[...]
        df_ref[...] = (dxv * cs_ref[...]).astype(BF16)
        dg_ref[...] += coef * jnp.sum(dxv * f_ref[...], axis=0, keepdims=True)

    rowspec = pl.BlockSpec((1, d), lambda i: (0, 0))
    tile = pl.BlockSpec((tr, d), lambda i: (i, 0))
    return pl.pallas_call(
        body, name=name, out_shape=[jax.ShapeDtypeStruct((s, d), BF16), jax.ShapeDtypeStruct((1, d), F32)],
        grid=(s // tr,), in_specs=[tile, tile, rowspec], out_specs=[tile, rowspec],
        compiler_params=_params(("arbitrary",)),
    )(dxn, f, colscale)


def _ffn_up(h, wg, wu, name, tm=SEQ, tn=256):
    s, d = h.shape
    f = wg.shape[0]

    def body(h_ref, wg_ref, wu_ref, a_ref, u_ref, s_ref):
        hv = h_ref[...]
        a = _dot(hv, wg_ref[...], 1, 1)
        u = _dot(hv, wu_ref[...], 1, 1)
        a_ref[...] = a.astype(BF16)
        u_ref[...] = u.astype(BF16)
        s_ref[...] = (a * _sigmoid(a) * u).astype(BF16)

    tile = pl.BlockSpec((tm, tn), lambda i, j: (i, j))
    wspec = pl.BlockSpec((tn, d), lambda i, j: (j, 0))
    return pl.pallas_call(
        body, name=name,
        out_shape=[jax.ShapeDtypeStruct((s, f), BF16), jax.ShapeDtypeStruct((s, f), BF16), jax.ShapeDtypeStruct((s, f), BF16)],
        grid=(s // tm, f // tn), in_specs=[pl.BlockSpec((tm, d), lambda i, j: (i, 0)), wspec, wspec],
        out_specs=[tile, tile, tile], compiler_params=_params(("parallel", "parallel")),
    )(h, wg, wu)


def _ffn_bwd_ds(df, wd, a, u, name, tm=SEQ, tn=256):
    s, d = df.shape
    f = wd.shape[0]

    def body(df_ref, wd_ref, a_ref, u_ref, da_ref, du_ref):
        ds = _dot(df_ref[...], wd_ref[...], 1, 1)
        av = a_ref[...].astype(F32)
        sg = _sigmoid(av)
        da_ref[...] = (ds * u_ref[...].astype(F32) * (sg * (1.0 + av * (1.0 - sg)))).astype(BF16)
        du_ref[...] = (ds * (av * sg)).astype(BF16)

    tile = pl.BlockSpec((tm, tn), lambda i, j: (i, j))
    return pl.pallas_call(
        body, name=name, out_shape=[jax.ShapeDtypeStruct((s, f), BF16), jax.ShapeDtypeStruct((s, f), BF16)],
        grid=(s // tm, f // tn),
        in_specs=[pl.BlockSpec((tm, d), lambda i, j: (i, 0)), pl.BlockSpec((tn, d), lambda i, j: (j, 0)), tile, tile],
        out_specs=[tile, tile], compiler_params=_params(("parallel", "parallel")),
    )(df, wd, a, u)


def _merge_fwd(o_sb, o_dil, o_swa, gates, wb_sb, wb_dil, wb_swa, name):
    s, d = SEQ, D_MODEL
    tm = 256

    def body(osb_ref, odl_ref, osw_ref, g_ref, wsb_ref, wdl_ref, wsw_ref, m_ref, tsb_ref, tdl_ref, tsw_ref):
        for h in range(osb_ref.shape[0]):
            tsb_ref[:, h * HEAD_DIM:(h + 1) * HEAD_DIM] = osb_ref[h].astype(BF16)
        for h in range(osw_ref.shape[0]):
            tsw_ref[:, h * HEAD_DIM:(h + 1) * HEAD_DIM] = osw_ref[h].astype(BF16)
        tdl_ref[...] = odl_ref[...].astype(BF16)
        acc = _sigmoid(g_ref[:, 0:d]) * _dot(tsb_ref[...], wsb_ref[...], 1, 0)
        acc += _sigmoid(g_ref[:, d:2 * d]) * _dot(tdl_ref[...], wdl_ref[...], 1, 0)
        acc += _sigmoid(g_ref[:, 2 * d:3 * d]) * _dot(tsw_ref[...], wsw_ref[...], 1, 0)
        m_ref[...] = acc.astype(BF16)

    def rows(w):
        return pl.BlockSpec((tm, w), lambda i: (i, 0))

    def heads(n):
        return pl.BlockSpec((n, tm, HEAD_DIM), lambda i: (0, i, 0))

    def whole(w):
        return pl.BlockSpec((w, d), lambda i: (0, 0))

    return pl.pallas_call(
        body, name=name, out_shape=[jax.ShapeDtypeStruct((s, w), BF16) for w in (d, 256, 128, 384)], grid=(s // tm,),
        in_specs=[heads(H_SB), rows(128), heads(H_SWA_Q), rows(3 * d), whole(256), whole(128), whole(384)],
        out_specs=[rows(d), rows(256), rows(128), rows(384)], compiler_params=_params(("parallel",)),
    )(o_sb, o_dil, o_swa, gates, wb_sb, wb_dil, wb_swa)


def _merge_bwd(dmerged, t_sb, t_dil, t_swa, gates, wb_sb, wb_dil, wb_swa, name):
    s, d = SEQ, D_MODEL
    tm = 256

    def body(dm_ref, tsb_ref, tdl_ref, tsw_ref, g_ref, wsb_ref, wdl_ref, wsw_ref,
             dg_ref, dosb_ref, dodl_ref, dosw_ref, dbsb_ref, dbdl_ref, dbsw_ref):
        dm = dm_ref[...]
        for idx, (t_ref, w_ref, do_ref, db_ref) in enumerate((
                (tsb_ref, wsb_ref, dosb_ref, dbsb_ref), (tdl_ref, wdl_ref, dodl_ref, dbdl_ref),
                (tsw_ref, wsw_ref, dosw_ref, dbsw_ref))):
            w = w_ref[...]
            br = _dot(t_ref[...], w, 1, 0)
            sg = _sigmoid(g_ref[:, idx * d:(idx + 1) * d])
            dbr = (dm * sg).astype(BF16)
            dg_ref[:, idx * d:(idx + 1) * d] = (dm * br * (sg * (1.0 - sg))).astype(BF16)
            db_ref[...] = dbr
            do = _dot(dbr, w, 1, 1)
            if len(do_ref.shape) == 2:
                do_ref[...] = do
            else:
                for h in range(do_ref.shape[0]):
                    do_ref[h] = do[:, h * HEAD_DIM:(h + 1) * HEAD_DIM]

    def rows(w):
        return pl.BlockSpec((tm, w), lambda i: (i, 0))

    def heads(n):
        return pl.BlockSpec((n, tm, HEAD_DIM), lambda i: (0, i, 0))

    def whole(w):
        return pl.BlockSpec((w, d), lambda i: (0, 0))

    def shp(w, dt):
        return jax.ShapeDtypeStruct((s, w), dt)

    def hshp(n):
        return jax.ShapeDtypeStruct((n, s, HEAD_DIM), F32)

    return pl.pallas_call(
        body, name=name,
        out_shape=[shp(3 * d, BF16), hshp(H_SB), shp(128, F32), hshp(H_SWA_Q), shp(d, BF16), shp(d, BF16), shp(d, BF16)],
        grid=(s // tm,),
        in_specs=[rows(d), rows(256), rows(128), rows(384), rows(3 * d), whole(256), whole(128), whole(384)],
        out_specs=[rows(3 * d), heads(H_SB), rows(128), heads(H_SWA_Q), rows(d), rows(d), rows(d)],
        compiler_params=_params(("parallel",)),
    )(dmerged, t_sb, t_dil, t_swa, gates, wb_sb, wb_dil, wb_swa)


def _final_loss(x, target, g, name):
    s, d = x.shape
    tr = 256

    def body(x_ref, t_ref, g_ref, loss_ref, dx_ref, dg_ref):
        @pl.when(pl.program_id(0) == 0)
        def _():
            loss_ref[...] = jnp.zeros_like(loss_ref)
            dg_ref[...] = jnp.zeros_like(dg_ref)

        xv = x_ref[...]
        gv = g_ref[...]
        rstd = lax.rsqrt(jnp.mean(xv * xv, axis=-1, keepdims=True) + RMS_EPS)
        xhat = xv * rstd
        err = xhat * gv - t_ref[...]
        loss_ref[...] += 0.5 * jnp.sum(jnp.mean(err * err, axis=-1, keepdims=True))
        dy = err * (1.0 / d)
        dxhat = dy * gv
        mean_term = jnp.mean(dxhat * xhat, axis=-1, keepdims=True)
        dx_ref[...] = rstd * (dxhat - xhat * mean_term)
        dg_ref[...] += jnp.sum(dy * xhat, axis=0, keepdims=True)

    rowspec = pl.BlockSpec((1, d), lambda i: (0, 0))
    tile = pl.BlockSpec((tr, d), lambda i: (i, 0))
    return pl.pallas_call(
        body, name=name,
        out_shape=[jax.ShapeDtypeStruct((1, LANES), F32), jax.ShapeDtypeStruct((s, d), F32), jax.ShapeDtypeStruct((1, d), F32)],
        grid=(s // tr,), in_specs=[tile, tile, rowspec],
        out_specs=[pl.BlockSpec((1, LANES), lambda i: (0, 0)), tile, rowspec],
        compiler_params=_params(("arbitrary",)),
    )(x, target, g)


def _adamw(w, g, m, v, name, after=None):
    shape = w.shape
    cols = shape[-1]
    rows = int(np.prod(shape[:-1])) if len(shape) > 1 else 1
    tr = rows
    for cand in (1024, 512, 256, 128, 64, 32, 16, 8):
        if rows % cand == 0 and rows > cand and cand * cols * 4 <= (1 << 21):
            tr = cand
            break

    def body(w_ref, g_ref, m_ref, v_ref, *rest):
        d_ref, nm_ref, nv_ref = rest[-3:]
        d_ref[...], nm_ref[...], nv_ref[...] = _adam_update(w_ref[...], g_ref[...], m_ref[...], v_ref[...])

    tile = pl.BlockSpec((tr, cols), lambda i: (i, 0))
    flat = [t.reshape(rows, cols) for t in (w, g, m, v)] + ([] if after is None else [after])
    out = pl.pallas_call(
        body, name=name, out_shape=[jax.ShapeDtypeStruct((rows, cols), F32)] * 3, grid=(rows // tr,),
        in_specs=[tile] * 4 + [ANY] * (len(flat) - 4), out_specs=[tile] * 3, compiler_params=_params(("parallel",)),
    )(*flat)
    return tuple(t.reshape(shape) for t in out)


def _adam_update(w, gv, m, v):
    nm = ADAM_B1 * m + (1.0 - ADAM_B1) * gv
    nv = ADAM_B2 * v + (1.0 - ADAM_B2) * (gv * gv)
    m_hat = nm / (1.0 - ADAM_B1 ** ADAM_STEP)
    v_hat = nv / (1.0 - ADAM_B2 ** ADAM_STEP)
    return -ADAM_LR * (m_hat / (jnp.sqrt(v_hat) + ADAM_EPS) + ADAM_WD * w), nm, nv


def _reduce_adamw(groups, w, m, v, row0, prev, name, after=None):
    n, r, cdim = groups[0].shape
    rows = w.shape[0]
    tr = _row_tile(r, max(16, (1 << 22) // (n * cdim * groups[0].dtype.itemsize)))
    steps = r // tr
    ng = len(groups)

    def body(*refs):
        w_ref, m_ref, v_ref = refs[ng:ng + 3]
        g_out, d_out, m_out, v_out = refs[-4:]
        gg = pl.program_id(0)
        for gi in range(ng):
            @pl.when(gg == gi)
            def _(gi=gi):
                acc = refs[gi][0].astype(F32)
                for k in range(1, n):
                    acc = acc + refs[gi][k].astype(F32)
                g_out[...] = acc
                d_out[...], m_out[...], v_out[...] = _adam_update(w_ref[...], acc, m_ref[...], v_ref[...])

    def part_spec(gi):
        return pl.BlockSpec((n, tr, cdim), lambda gg, i: (0, jnp.where(gg == gi, i, 0), 0))

    tile = pl.BlockSpec((tr, cdim), lambda gg, i: (row0 // tr + gg * steps + i, 0))
    extra = ([] if prev is None else list(prev)) + ([] if after is None else [after])
    return pl.pallas_call(
        body, name=name, out_shape=[jax.ShapeDtypeStruct((rows, cdim), F32)] * 4, grid=(ng, steps),
        in_specs=[part_spec(gi) for gi in range(ng)] + [tile] * 3 + [ANY] * len(extra), out_specs=[tile] * 4,
        input_output_aliases={} if prev is None else {ng + 3 + k: k for k in range(4)},
        compiler_params=_params(("parallel", "parallel")),
    )(*groups, w, m, v, *extra)


def _ada_fwd(c_all, w, name):
    n = w.shape[1]

    def body(c_ref, w_ref, o_ref):
        cv = c_ref[...]
        o_ref[...] = jnp.dot(cv * _sigmoid(cv), w_ref[...], preferred_element_type=F32, precision=lax.Precision.HIGHEST)

    return pl.pallas_call(body, name=name, out_shape=jax.ShapeDtypeStruct((N_DEV, n), F32), compiler_params=_params())(c_all, w)


def _ada_bwd(c_all_t, dmod, name):
    n = dmod.shape[1]

    def body(c_ref, d_ref, o_ref):
        cv = c_ref[...]
        o_ref[...] = jnp.dot(cv * _sigmoid(cv), d_ref[...], preferred_element_type=F32, precision=lax.Precision.HIGHEST)

    return pl.pallas_call(body, name=name, out_shape=jax.ShapeDtypeStruct((D_MODEL, n), F32), compiler_params=_params())(c_all_t, dmod)


def _bucket_tables():
    rel = np.arange(BLK)[:, None] + BLK - np.arange(2 * BLK)[None, :]
    max_exact = N_BUCKETS // 2

    def bucket(n):
        nf = np.maximum(n, 1).astype(np.float32)
        large = max_exact + (np.log(nf / np.float32(max_exact)) / np.float32(math.log(MAX_REL_DIST / max_exact))
                             * np.float32(N_BUCKETS - max_exact)).astype(np.int32)
        return np.where(n < max_exact, n, np.minimum(large, N_BUCKETS - 1))

    tabs = []
    for dil, max_dist in ((1, 128), (4, 128), (16, 128), (1, SWA_WINDOW - 1)):
        in_band = (rel >= 0) & (rel <= max_dist)
        tabs.append(np.where(in_band, bucket(np.maximum(rel, 0) * dil), -1))
    return np.stack(tabs).astype(np.int32)


N_SOFT = H_DIL + H_SWA_Q


def _table_of_head(h):
    return jnp.minimum(h // 2, 3)


def _bias_build(rel_bias, tables, name):
    def body(rel_ref, t_ref, o_ref):
        h = pl.program_id(0)
        tb = t_ref[0]
        out = jnp.full((BLK, 2 * BLK), NEG, F32)
        for b in range(N_BUCKETS):
            out = jnp.where(tb == b, rel_ref[b, h], out)
        o_ref[0] = out

    return pl.pallas_call(
        body, name=name, out_shape=jax.ShapeDtypeStruct((N_SOFT, BLK, 2 * BLK), F32), grid=(N_SOFT,),
        in_specs=[pl.BlockSpec(memory_space=pltpu.SMEM),
                  pl.BlockSpec((1, BLK, 2 * BLK), lambda h: (_table_of_head(h), 0, 0))],
        out_specs=pl.BlockSpec((1, BLK, 2 * BLK), lambda h: (h, 0, 0)),
        compiler_params=_params(("parallel",)),
    )(rel_bias, tables)


def _bias_grad(dbias, tables, name):
    def body(d_ref, t_ref, o_ref):
        tb = t_ref[0]
        dv = d_ref[0]
        lane = lax.broadcasted_iota(jnp.int32, (1, LANES), 1)
        out = jnp.zeros((1, LANES), F32)
        for b in range(N_BUCKETS):
            out = jnp.where(lane == b, jnp.sum(jnp.where(tb == b, dv, 0.0)), out)
        o_ref[0] = out

    return pl.pallas_call(
        body, name=name, out_shape=jax.ShapeDtypeStruct((N_SOFT, 1, LANES), F32), grid=(N_SOFT,),
        in_specs=[pl.BlockSpec((1, BLK, 2 * BLK), lambda h: (h, 0, 0)),
                  pl.BlockSpec((1, BLK, 2 * BLK), lambda h: (_table_of_head(h), 0, 0))],
        out_specs=pl.BlockSpec((1, 1, LANES), lambda h: (h, 0, 0)),
        compiler_params=_params(("parallel",)),
    )(dbias, tables)


def _band_layout(g, bias_div):
    assert g == 1 or bias_div == 1
    return bias_div if g == 1 else 1


def _band_specs(length, g, bias_div, offs):
    ns = _band_layout(g, bias_div)

    def seqs(off, div=1):
        return pl.BlockSpec((ns, length, HEAD_DIM), lambda s: (off // ns + s // div, 0, 0))

    xspecs = [seqs(offs[0]), seqs(offs[1], g), seqs(offs[2], g)]
    bspec = pl.BlockSpec((1, BLK, 2 * BLK), lambda s: (s, 0, 0))
    sspec = pl.BlockSpec((ns, 1, LANES), lambda s: (s, 0, 0))
    colspec = pl.BlockSpec((ns, length, 1), lambda s: (s, 0, 0))
    return xspecs, seqs(0), seqs(0, g), bspec, sspec, colspec


def _band_sweep(length, ns, one):
    nblk = length // BLK
    for qq in range(ns):
        if ns * nblk <= 16:
            for i in range(nblk):
                one(qq, i * BLK, max(i - 1, 0) * BLK, i == 0)
        else:
            def step(i, carry, qq=qq):
                one(qq, pl.multiple_of(i * BLK, BLK), pl.multiple_of(jnp.maximum(i - 1, 0) * BLK, BLK), i == 0)
                return carry

            lax.fori_loop(0, nblk, step, 0, unroll=2)


def _band_scores(q_ref, k_ref, b_ref, qq, kq, bq, cur, prv, first):
    qv = q_ref[qq, pl.ds(cur, BLK), :]
    bv = b_ref[bq]
    if first is True:
        sp = jnp.full((BLK, BLK), NEG, F32)
    else:
        sp = _dot(qv, k_ref[kq, pl.ds(prv, BLK), :], 1, 1) + bv[:, :BLK]
        sp = sp if first is False else jnp.where(first, NEG, sp)
    sc = _dot(qv, k_ref[kq, pl.ds(cur, BLK), :], 1, 1) + bv[:, BLK:]
    return qv, sp, sc


def _band_fwd(x, bias, sink, *, nq, offs, g, bias_div, has_sink, name):
    length = x.shape[1]
    ns = _band_layout(g, bias_div)

    def body(q_ref, k_ref, v_ref, b_ref, s_ref, o_ref, lse_ref):
        def one(qq, cur, prv, first):
            kq, bq = qq, 0
            _, sp, sc = _band_scores(q_ref, k_ref, b_ref, qq, kq, bq, cur, prv, first)
            m = jnp.maximum(jnp.max(sp, axis=1, keepdims=True), jnp.max(sc, axis=1, keepdims=True))
            if has_sink:
                sk = s_ref[qq][:, :1]
                m = jnp.maximum(m, sk)
            pp, pc = jnp.exp(sp - m), jnp.exp(sc - m)
            den = jnp.sum(pp, axis=1, keepdims=True) + jnp.sum(pc, axis=1, keepdims=True)
            if has_sink:
                den = den + jnp.exp(sk - m)
            acc = (_dot(pp.astype(BF16), v_ref[kq, pl.ds(prv, BLK), :], 1, 0)
                   + _dot(pc.astype(BF16), v_ref[kq, pl.ds(cur, BLK), :], 1, 0))
            o_ref[qq, pl.ds(cur, BLK), :] = acc / den
            lse_ref[qq, pl.ds(cur, BLK), :] = m + jnp.log(den)

        _band_sweep(length, ns, one)

    xspecs, qspec, _, bspec, sspec, colspec = _band_specs(length, g, bias_div, offs)
    return pl.pallas_call(
        body, name=name,
        out_shape=[jax.ShapeDtypeStruct((nq, length, HEAD_DIM), F32), jax.ShapeDtypeStruct((nq, length, 1), F32)],
        grid=(nq // ns,), in_specs=xspecs + [bspec, sspec],
        out_specs=[qspec, colspec], compiler_params=_params(("parallel",)),
    )(x, x, x, bias, sink)


def _band_bwd(x, bias, sink, o, lse, do, dlse, *, nq, offs, g, bias_div, has_sink, name):
    length = x.shape[1]
    ns = _band_layout(g, bias_div)
    nk, nbias = nq // g, nq // bias_div

    def body(q_ref, k_ref, v_ref, b_ref, s_ref, o_ref, lse_ref, do_ref, dlse_ref,
             dq_ref, dk_ref, dv_ref, db_ref, dsk_ref, dkp_ref, dvp_ref):
        for ref in (db_ref, dsk_ref, dkp_ref, dvp_ref):
            ref[...] = jnp.zeros_like(ref)

        @pl.when(pl.program_id(0) % g == 0)
        def _():
            dk_ref[...] = jnp.zeros_like(dk_ref)
            dv_ref[...] = jnp.zeros_like(dv_ref)

        def one(qq, cur, prv, first):
            kq, bq = qq, 0
            qv, sp, sc = _band_scores(q_ref, k_ref, b_ref, qq, kq, bq, cur, prv, first)
            rows, prow = pl.ds(cur, BLK), pl.ds(prv, BLK)
            lse_v = lse_ref[qq, rows, :]
            pp, pc = jnp.exp(sp - lse_v), jnp.exp(sc - lse_v)
            dov = do_ref[qq, rows, :]
            dob = dov.astype(BF16)
            coef = dlse_ref[qq, rows, :] - jnp.sum(dov * o_ref[qq, rows, :], axis=1, keepdims=True)
            dsp = pp * (_dot(dob, v_ref[kq, prow, :], 1, 1) + coef)
            dsc = pc * (_dot(dob, v_ref[kq, rows, :], 1, 1) + coef)
            dspb, dscb = dsp.astype(BF16), dsc.astype(BF16)
            dq_ref[qq, rows, :] = ((_dot(dspb, k_ref[kq, prow, :], 1, 0) + _dot(dscb, k_ref[kq, rows, :], 1, 0))
                                   * (HEAD_DIM ** -0.5))
            dk_ref[kq, rows, :] += _dot(dscb, qv, 0, 0)
            dkp_ref[kq, prow, :] += _dot(dspb, qv, 0, 0)
            dv_ref[kq, rows, :] += _dot(pc.astype(BF16), dob, 0, 0)
            dvp_ref[kq, prow, :] += _dot(pp.astype(BF16), dob, 0, 0)
            db_ref[bq, :, :BLK] += dsp
            db_ref[bq, :, BLK:] += dsc
            if has_sink:
                dsk_ref[qq] += jnp.sum(jnp.exp(s_ref[qq][:, :1] - lse_v) * coef)

        _band_sweep(length, ns, one)
        dk_ref[...] += dkp_ref[...]
        dv_ref[...] += dvp_ref[...]

    xspecs, qspec, kvspec, bspec, sspec, colspec = _band_specs(length, g, bias_div, offs)
    return pl.pallas_call(
        body, name=name,
        out_shape=[jax.ShapeDtypeStruct((nq, length, HEAD_DIM), F32), jax.ShapeDtypeStruct((nk, length, HEAD_DIM), F32),
                   jax.ShapeDtypeStruct((nk, length, HEAD_DIM), F32), jax.ShapeDtypeStruct((nbias, BLK, 2 * BLK), F32),
                   jax.ShapeDtypeStruct((nq, 1, LANES), F32)],
        grid=(nq // ns,),
        in_specs=xspecs + [bspec, sspec, qspec, colspec, qspec, colspec],
        out_specs=[qspec, kvspec, kvspec, bspec, sspec],
        scratch_shapes=[pltpu.VMEM((ns, length, HEAD_DIM), F32), pltpu.VMEM((ns, length, HEAD_DIM), F32)],
        compiler_params=_params(("arbitrary",)),
    )(x, x, x, bias, sink, o, lse, do, dlse)


TOK_TILE = 512


def _dil_merge(outs, lses, dout, name):
    tr = TOK_TILE
    dils = [d for _, d in DIL_PATTERNS]
    n = len(dils)
    o4 = [o.reshape(2, d, SEQ // d, HEAD_DIM) for o, d in zip(outs, dils)]
    l4 = [l.reshape(2, d, SEQ // d, 1) for l, d in zip(lses, dils)]
    o_specs = [pl.BlockSpec((2, d, tr // d, HEAD_DIM), lambda i: (0, 0, i, 0)) for d in dils]
    l_specs = [pl.BlockSpec((2, d, tr // d, 1), lambda i: (0, 0, i, 0)) for d in dils]
    tok = pl.BlockSpec((tr, 2 * HEAD_DIM), lambda i: (i, 0))
    scratch = ([pltpu.VMEM((tr, 2 * HEAD_DIM), F32) for _ in dils] + [pltpu.VMEM((tr, 1), F32) for _ in range(2 * n)]
               + [pltpu.VMEM((tr // d, 2 * HEAD_DIM), F32) for d in dils])

    def to_tokens(o_ref, l_ref, d, pair, cols, stage):
        for r in range(d):
            rows = pl.ds(r, tr // d, stride=d) if d > 1 else slice(None)
            stage[:, :HEAD_DIM] = o_ref[0, r]
            stage[:, HEAD_DIM:] = o_ref[1, r]
            pair[rows, :] = stage[...]
            for h in range(2):
                cols[h][rows, :] = l_ref[h, r]
        return pair[...], [cols[0][...], cols[1][...]]

    def weights(ls):
        left = lax.broadcasted_iota(jnp.int32, (tr, 2 * HEAD_DIM), 1) < HEAD_DIM
        per_head = []
        for h in range(2):
            m = ls[0][h]
            for g in range(1, n):
                m = jnp.maximum(m, ls[g][h])
            es = [jnp.exp(ls[g][h] - m) for g in range(n)]
            den = es[0]
            for e in es[1:]:
                den = den + e
            per_head.append([e / den for e in es])
        return per_head, [jnp.where(left, per_head[0][g], per_head[1][g]) for g in range(n)], left

    def load(refs):
        pairs, cols, stages = refs[:n], refs[n:3 * n], refs[3 * n:]
        return pairs, [cols[2 * g:2 * g + 2] for g in range(n)], stages

    if dout is None:
        def body(*refs):
            pairs, cols, stages = load(refs[2 * n + 1:])
            toks = [to_tokens(refs[g], refs[n + g], dils[g], pairs[g], cols[g], stages[g]) for g in range(n)]
            _, alphas, _ = weights([t[1] for t in toks])
            acc = alphas[0] * toks[0][0]
            for g in range(1, n):
                acc = acc + alphas[g] * toks[g][0]
            refs[2 * n][...] = acc

        return pl.pallas_call(
            body, name=name, out_shape=jax.ShapeDtypeStruct((SEQ, 2 * HEAD_DIM), F32), grid=(SEQ // tr,),
            in_specs=o_specs + l_specs, out_specs=tok, scratch_shapes=scratch, compiler_params=_params(("parallel",)),
        )(*o4, *l4)

    def body(*refs):
        do_refs, dl_refs = refs[2 * n + 1:3 * n + 1], refs[3 * n + 1:4 * n + 1]
        pairs, cols, stages = load(refs[4 * n + 1:])
        toks = [to_tokens(refs[g], refs[n + g], dils[g], pairs[g], cols[g], stages[g]) for g in range(n)]
        per_head, alphas, left = weights([t[1] for t in toks])
        dov = refs[2 * n][...]
        das = []
        for g in range(n):
            prod = dov * toks[g][0]
            das.append([jnp.sum(jnp.where(left, prod, 0.0), axis=1, keepdims=True),
                        jnp.sum(jnp.where(left, 0.0, prod), axis=1, keepdims=True)])
        dbar = [sum(per_head[h][g] * das[g][h] for g in range(n)) for h in range(2)]
        for g, d in enumerate(dils):
            pairs[g][...] = alphas[g] * dov
            for h in range(2):
                cols[g][h][...] = per_head[h][g] * (das[g][h] - dbar[h])
            for r in range(d):
                rows = pl.ds(r, tr // d, stride=d) if d > 1 else slice(None)
                v = pairs[g][rows, :]
                for h in range(2):
                    do_refs[g][h, r] = v[:, h * HEAD_DIM:(h + 1) * HEAD_DIM]
                    dl_refs[g][h, r] = cols[g][h][rows, :]

    out = pl.pallas_call(
        body, name=name,
        out_shape=[jax.ShapeDtypeStruct(o.shape, F32) for o in o4] + [jax.ShapeDtypeStruct(l.shape, F32) for l in l4],
        grid=(SEQ // tr,), in_specs=o_specs + l_specs + [tok], out_specs=o_specs + l_specs, scratch_shapes=scratch,
        compiler_params=_params(("parallel",)),
    )(*o4, *l4, dout)
    return [t.reshape(s.shape) for t, s in zip(out, list(outs) + list(lses))]


def _tri(cmp):
    r = lax.broadcasted_iota(jnp.int32, (SB_TILE, SB_TILE), 0)
    c = lax.broadcasted_iota(jnp.int32, (SB_TILE, SB_TILE), 1)
    return cmp(r, c).astype(BF16)


def _cum(x, tri, terms):
    acc, rest = None, x
    for _ in range(terms):
        part = rest.astype(BF16)
        rest = rest - part.astype(F32)
        d = _dot(part, tri, 1, 0)
        acc = d if acc is None else acc + d
    return acc


def _sb_logits(q, ks, diagonal):
    t = SB_TILE
    z = _dot(q, ks, 1, 1)
    e = jnp.exp(-jnp.abs(z))
    lf = -(jnp.maximum(z, 0.0) + jnp.log(1.0 + e))
    if not diagonal:
        return z, e, lf, None
    mask = lax.broadcasted_iota(jnp.int32, (t, t), 1) < lax.broadcasted_iota(jnp.int32, (t, t), 0)
    return z, e, jnp.where(mask, lf, 0.0), mask


def _sb_specs(h, s):
    t = SB_TILE
    tile = pl.BlockSpec((h, t, HEAD_DIM), lambda i: (0, i, 0))
    keys = pl.BlockSpec((h, s, HEAD_DIM), lambda i: (1, 0, 0))
    values = pl.BlockSpec((h, s, HEAD_DIM), lambda i: (2, 0, 0))
    return tile, keys, values, pl.BlockSpec((h, t, 1), lambda i: (0, i, 0))


def _sb_fwd(x, name):
    h, s = x.shape[0] // 3, x.shape[1]
    t = SB_TILE

    def body(q_ref, k_ref, v_ref, o_ref, tot_ref):
        i = pl.program_id(0)
        after = _tri(lambda r, c: r > c)

        def tile(j, carry, diagonal):
            rows = pl.ds(pl.multiple_of(j * t, t), t)
            out = []
            for hh, (right, acc) in enumerate(carry):
                z, _, lf, mask = _sb_logits(q_ref[hh], k_ref[hh, rows, :], diagonal)
                w = jnp.exp(z + lf + (right + _cum(lf, after, 2)))
                w = w if mask is None else jnp.where(mask, w, 0.0)
                out.append((right + jnp.sum(lf, axis=1, keepdims=True), acc + _dot(w.astype(BF16), v_ref[hh, rows, :], 1, 0)))
            return tuple(out)

        carry = tile(i, tuple((jnp.zeros((t, 1), F32), jnp.zeros((t, HEAD_DIM), F32)) for _ in range(h)), True)
        carry = lax.fori_loop(0, i, lambda jj, c: tile(i - 1 - jj, c, False), carry)
        for hh, (right, acc) in enumerate(carry):
            o_ref[hh] = acc
            tot_ref[hh] = right

    tile_spec, keys, values, col = _sb_specs(h, s)
    return pl.pallas_call(
        body, name=name, out_shape=[jax.ShapeDtypeStruct((h, s, HEAD_DIM), F32), jax.ShapeDtypeStruct((h, s, 1), F32)],
        grid=(s // t,), in_specs=[tile_spec, keys, values], out_specs=[tile_spec, col],
        compiler_params=_params(("parallel",)),
    )(x, x, x)


def _sb_bwd(x, tot, do, name):
    h, s = x.shape[0] // 3, x.shape[1]
    t = SB_TILE

    def body(q_ref, k_ref, v_ref, tot_ref, do_ref, dq_ref, dk_ref, dv_ref):
        i = pl.program_id(0)

        @pl.when(i == 0)
        def _():
            dk_ref[...] = jnp.zeros_like(dk_ref)
            dv_ref[...] = jnp.zeros_like(dv_ref)

        upto = _tri(lambda r, c: r <= c)
        before = _tri(lambda r, c: r < c)

        def tile(j, carry, diagonal):
            rows = pl.ds(pl.multiple_of(j * t, t), t)
            out = []
            for hh, (left, cleft, dq) in enumerate(carry):
                qv, ks, dob = q_ref[hh], k_ref[hh, rows, :], do_ref[hh].astype(BF16)
                z, e, lf, mask = _sb_logits(qv, ks, diagonal)
                between = tot_ref[hh] - (left + _cum(lf, upto, 2))
                w = jnp.exp(z + lf + between)
                w = w if mask is None else jnp.where(mask, w, 0.0)
                dlog = w * _dot(dob, v_ref[hh, rows, :], 1, 1)
                cfail = cleft + _cum(dlog, before, 2)
                sig = jnp.where(z >= 0.0, 1.0, e) / (1.0 + e)
                dz = dlog * (1.0 - sig) - sig * cfail
                dz = (dz if mask is None else jnp.where(mask, dz, 0.0)).astype(BF16)
                dk_ref[hh, rows, :] += _dot(dz, qv, 0, 0)
                dv_ref[hh, rows, :] += _dot(w.astype(BF16), dob, 0, 0)
                out.append((left + jnp.sum(lf, axis=1, keepdims=True), cleft + jnp.sum(dlog, axis=1, keepdims=True),
                            dq + _dot(dz, ks, 1, 0)))
            return tuple(out)

        zero = jnp.zeros((t, 1), F32)
        carry = lax.fori_loop(0, i, lambda j, c: tile(j, c, False),
                              tuple((zero, zero, jnp.zeros((t, HEAD_DIM), F32)) for _ in range(h)))
        for hh, (_, _, dq) in enumerate(tile(i, carry, True)):
            dq_ref[hh] = dq * (HEAD_DIM ** -0.5)

    tile_spec, keys, values, col = _sb_specs(h, s)
    full = pl.BlockSpec((h, s, HEAD_DIM), lambda i: (0, 0, 0))
    shp = jax.ShapeDtypeStruct((h, s, HEAD_DIM), F32)
    return pl.pallas_call(
        body, name=name, out_shape=[shp, shp, shp], grid=(s // t,),
        in_specs=[tile_spec, keys, values, col, tile_spec],
        out_specs=[tile_spec, full, full], compiler_params=_params(("arbitrary",)),
    )(x, x, x, tot, do)


COL_SB, COL_DIL, COL_SWA = 0, 3 * H_SB * HEAD_DIM, 3 * H_SB * HEAD_DIM + 3 * H_DIL * HEAD_DIM
N_SWA = H_SWA_Q + 2 * H_SWA_KV


def _dil_col(t, g):
    return COL_DIL + t * H_DIL * HEAD_DIM + g * 2 * HEAD_DIM


def _split_heads(qkv, name):
    tr = TOK_TILE
    scale = HEAD_DIM ** -0.5
    dils = [d for _, d in DIL_PATTERNS]

    def body(x_ref, sb_ref, d0_ref, d1_ref, d2_ref, swa_ref, pair):
        def head(col, scaled):
            v = x_ref[:, col:col + HEAD_DIM]
            return (v * scale if scaled else v).astype(BF16)

        for hh in range(3 * H_SB):
            sb_ref[hh] = head(COL_SB + hh * HEAD_DIM, hh < H_SB)
        for hh in range(N_SWA):
            swa_ref[hh] = head(COL_SWA + hh * HEAD_DIM, hh < H_SWA_Q)
        for t in range(3):
            for g, (d, out_ref) in enumerate(zip(dils, (d0_ref, d1_ref, d2_ref))):
                col = _dil_col(t, g)
                if d == 1:
                    for h in range(2):
                        out_ref[t * 2 + h] = head(col + h * HEAD_DIM, t == 0)
                    continue
                pair[...] = x_ref[:, col:col + 2 * HEAD_DIM]
                for r in range(d):
                    v = pair[pl.ds(r, tr // d, stride=d), :]
                    v = v * scale if t == 0 else v
                    for h in range(2):
                        out_ref[t * 2 * d + h * d + r] = v[:, h * HEAD_DIM:(h + 1) * HEAD_DIM].astype(BF16)

    def heads(n, length):
        return jax.ShapeDtypeStruct((n, length, HEAD_DIM), BF16)

    def spec(n, rows):
        return pl.BlockSpec((n, rows, HEAD_DIM), lambda i: (0, i, 0))

    return pl.pallas_call(
        body, name=name,
        out_shape=[heads(3 * H_SB, SEQ)] + [heads(6 * d, SEQ // d) for d in dils] + [heads(N_SWA, SEQ)],
        grid=(SEQ // tr,), in_specs=[pl.BlockSpec((tr, D_QKV), lambda i: (i, 0))],
        out_specs=[spec(3 * H_SB, tr)] + [spec(6 * d, tr // d) for d in dils] + [spec(N_SWA, tr)],
        scratch_shapes=[pltpu.VMEM((tr, 2 * HEAD_DIM), F32)], compiler_params=_params(("parallel",)),
    )(qkv)


def _join_heads(sb, dil, swa, name):
    tr = TOK_TILE
    dils = [d for _, d in DIL_PATTERNS]

    def body(*refs):
        sb_refs, dil_refs, swa_refs = refs[:3], [refs[3 + 3 * g:6 + 3 * g] for g in range(3)], refs[12:15]
        o_ref, pair, stages = refs[15], refs[16], refs[17:]

        def put(col, v):
            o_ref[:, col:col + v.shape[1]] = v.astype(BF16)

        for t in range(3):
            for h in range(H_SB):
                put(COL_SB + (t * H_SB + h) * HEAD_DIM, sb_refs[t][h])
        col = COL_SWA
        for ref in swa_refs:
            for h in range(ref.shape[0]):
                put(col, ref[h])
                col += HEAD_DIM
        for t in range(3):
            for g, d in enumerate(dils):
                ref, col = dil_refs[g][t], _dil_col(t, g)
                if d == 1:
                    for h in range(2):
                        put(col + h * HEAD_DIM, ref[h])
                    continue
                stage = stages[g - 1]
                for r in range(d):
                    stage[:, :HEAD_DIM] = ref[r]
                    stage[:, HEAD_DIM:] = ref[d + r]
                    pair[pl.ds(r, tr // d, stride=d), :] = stage[...]
                put(col, pair[...])

    def spec(n, rows):
        return pl.BlockSpec((n, rows, HEAD_DIM), lambda i: (0, i, 0))

    ins = list(sb) + [t for g in range(3) for t in dil[g]] + list(swa)
    in_specs = ([spec(H_SB, tr)] * 3 + [spec(2 * d, tr // d) for d in dils for _ in range(3)]
                + [spec(H_SWA_Q, tr), spec(H_SWA_KV, tr), spec(H_SWA_KV, tr)])
    return pl.pallas_call(
        body, name=name, out_shape=jax.ShapeDtypeStruct((SEQ, D_QKV), BF16), grid=(SEQ // tr,), in_specs=in_specs,
        out_specs=pl.BlockSpec((tr, D_QKV), lambda i: (i, 0)),
        scratch_shapes=[pltpu.VMEM((tr, 2 * HEAD_DIM), F32)] + [pltpu.VMEM((tr // d, 2 * HEAD_DIM), F32) for d in dils[1:]],
        compiler_params=_params(("parallel",)),
    )(*ins)


def _mixer_fwd(qkv, bias, sinks_l, tag):
    sb, d0, d1, d2, swa = _split_heads(qkv, name=f"split_heads_{tag}")
    st = {"sb": sb, "dil": (d0, d1, d2), "swa": swa}
    o_sb, st["sb_tot"] = _sb_fwd(sb, name=f"sb_fwd_{tag}")
    st["dil_out"], st["dil_lse"], st["dil_sink"] = [], [], []
    for gi, (_, d) in enumerate(DIL_PATTERNS):
        sink = jnp.zeros((2 * d, 1, LANES), F32)
        og, lg = _band_fwd(st["dil"][gi], bias[2 * gi:2 * gi + 2], sink, nq=2 * d, offs=(0, 2 * d, 4 * d), g=1, bias_div=d,
                           has_sink=False, name=f"dil{gi}_fwd_{tag}")
        st["dil_out"].append(og)
        st["dil_lse"].append(lg)
        st["dil_sink"].append(sink)
    o_dil = _dil_merge(st["dil_out"], st["dil_lse"], None, name=f"dil_merge_fwd_{tag}")
    st["swa_sink"] = jnp.broadcast_to(sinks_l.reshape(H_SWA_Q, 1, 1), (H_SWA_Q, 1, LANES))
    st["swa_out"] = _band_fwd(swa, bias[H_DIL:], st["swa_sink"], nq=H_SWA_Q, offs=(0, H_SWA_Q, H_SWA_Q + H_SWA_KV),
                              g=H_SWA_Q // H_SWA_KV, bias_div=1, has_sink=True, name=f"swa_fwd_{tag}")
    return (o_sb, o_dil, st["swa_out"][0]), st


def _mixer_bwd(st, bias, do_sb, do_dil, do_swa, tag):
    d_sb = _sb_bwd(st["sb"], st["sb_tot"], do_sb, name=f"sb_bwd_{tag}")
    dmerge = _dil_merge(st["dil_out"], st["dil_lse"], do_dil, name=f"dil_merge_bwd_{tag}")
    d_dil, dbs = [], []
    for gi, (_, d) in enumerate(DIL_PATTERNS):
        dq, dk, dv, db, _ = _band_bwd(st["dil"][gi], bias[2 * gi:2 * gi + 2], st["dil_sink"][gi], st["dil_out"][gi],
                                      st["dil_lse"][gi], dmerge[gi], dmerge[3 + gi], nq=2 * d, offs=(0, 2 * d, 4 * d),
                                      g=1, bias_div=d, has_sink=False, name=f"dil{gi}_bwd_{tag}")
        d_dil.append((dq, dk, dv))
        dbs.append(db)
    o_sw, l_sw = st["swa_out"]
    dq_sw, dk_sw, dv_sw, db_sw, dsink = _band_bwd(st["swa"], bias[H_DIL:], st["swa_sink"], o_sw, l_sw, do_swa,
                                                  jnp.zeros_like(l_sw), nq=H_SWA_Q, offs=(0, H_SWA_Q, H_SWA_Q + H_SWA_KV),
                                                  g=H_SWA_Q // H_SWA_KV, bias_div=1, has_sink=True, name=f"swa_bwd_{tag}")
    dqkv = _join_heads(d_sb, d_dil, (dq_sw, dk_sw, dv_sw), name=f"join_heads_{tag}")
    return dqkv, jnp.concatenate(dbs + [db_sw], 0), dsink[:, 0, 0]


PIECES = ("ffn0", "mix", "ffn1")


def _ffn_fwd(x_in, w, gain, mod_j, tag, after=None):
    st = {"x": x_in, "w": w}
    st["h"] = _norm_fwd(x_in, _row(gain), _row(mod_j[1]), _row(mod_j[0]), name=f"norm_fwd_{tag}", after=after)
    st["a"], st["u"], st["s"] = _ffn_up(st["h"], w["gate"], w["up"], name=f"up_{tag}")
    st["f"], x_out = _mm(st["s"], w["down"], res=x_in, colscale=_row(0.5 * mod_j[2]), emit_acc=True, tm=512, tn=1024,
                         name=f"down_{tag}")
    return x_out, st


def _ffn_bwd(dx_out, st, gain, mod_j, tag, done, pre, nxt):
    w = st["w"]

    def latest(new, old):
        return old if new is None else new

    df, dgate = pre or _gate_bwd(dx_out, st["f"], _row(0.5 * mod_j[2]), 0.5, name=f"gate_bwd_{tag}")
    dwd = _mm_tn(st["s"], df, tm=D_FF // 2, name=f"dwd_{tag}")
    token = latest(done({"down": dwd}), dwd)
    da, du = _ffn_bwd_ds(df, w["down"], st["a"], st["u"], name=f"ds_{tag}")
    dwg = _mm_tn(da, st["h"], after=token, tm=D_FF // 2, name=f"dwg_{tag}")
    token = latest(done({"gate": dwg}), dwg)
    dwu = _mm_tn(du, st["h"], after=token, tm=D_FF // 2, name=f"dwu_{tag}")
    token = latest(done({"up": dwu}), dwu)
    dx_in, sum_dh, sum_dhx, made = _dh_norm_bwd(da, w["gate"], du, w["up"], st["x"], dx_out, _row(gain), _row(mod_j[1]), nxt,
                                                after=token, name=f"dh_{tag}")
    dmod = jnp.concatenate([sum_dh, gain * sum_dhx, dgate], 0)
    return dx_in, dmod, (1.0 + mod_j[1]) * sum_dhx[0], made


def _mix_fwd(x_in, w, gain, mod_j, bias, sinks_l, tag, after=None):
    st = {"x": x_in, "w": w}
    st["h"] = _norm_fwd(x_in, _row(gain), _row(mod_j[1]), _row(mod_j[0]), name=f"norm_fwd_mix_{tag}", after=after)
    qkv = _mm(st["h"], w["in"], tb=True, tm=SEQ, b_rows=(0, D_QKV), name=f"qkv_{tag}")
    st["gates"] = _mm(st["h"], w["in"], tb=True, tm=SEQ, b_rows=(D_QKV, D_GATES), name=f"gates_{tag}")
    outs, st["mix"] = _mixer_fwd(qkv, bias, sinks_l, tag)
    st["merged"], *st["t"] = _merge_fwd(*outs, st["gates"], w["br_sb"], w["br_dil"], w["br_swa"], name=f"merge_fwd_{tag}")
    st["f"], x_out = _mm(st["merged"], w["out"], res=x_in, colscale=_row(mod_j[2]), emit_acc=True, name=f"out_{tag}")
    return x_out, st


def _mix_bwd(dx_out, st, gain, mod_j, bias, tag, done, pre, nxt):
    w = st["w"]
    df, dgate = pre or _gate_bwd(dx_out, st["f"], _row(mod_j[2]), 1.0, name=f"gate_bwd_mix_{tag}")
    g = {"out": _mm_tn(st["merged"], df, name=f"dw_out_{tag}")}
    dmerged = _mm(df, w["out"], tb=True, name=f"dmerged_{tag}")
    dgates, do_sb, do_dil, do_swa, dbr_sb, dbr_dil, dbr_swa = _merge_bwd(
        dmerged, *st["t"], st["gates"], w["br_sb"], w["br_dil"], w["br_swa"], name=f"merge_bwd_{tag}")
    g["br_sb"] = _mm_tn(st["t"][0], dbr_sb, name=f"dw_br_sb_{tag}")
    g["br_dil"] = _mm_tn(st["t"][1], dbr_dil, name=f"dw_br_dil_{tag}")
    g["br_swa"] = _mm_tn(st["t"][2], dbr_swa, name=f"dw_br_swa_{tag}")
    dqkv, dbias, dsinks = _mixer_bwd(st["mix"], bias, do_sb, do_dil, do_swa, tag)
    dw_qkv = _mm_tn(dqkv, st["h"], out_rows=D_QKV + D_GATES, name=f"dw_qkv_{tag}")
    g["in"] = _mm_tn(dgates, st["h"], out_rows=D_QKV + D_GATES, row0=D_QKV, prev=dw_qkv, name=f"dw_gates_{tag}")
    dx_in, sum_dh, sum_dhx, made = _dh_norm_bwd(dqkv, w["in"], dgates, w["in"], st["x"], dx_out, _row(gain), _row(mod_j[1]),
                                                nxt, after=done(g), b_rows=(0, D_QKV), name=f"dh_mix_{tag}")
    dmod = jnp.concatenate([sum_dh, gain * sum_dhx, dgate], 0)
    return dx_in, dmod, (1.0 + mod_j[1]) * sum_dhx[0], dbias, dsinks, made


def _local_step(x, target, mod, gains, weights_of, rel_bias, sinks, final_gain, grads_done):
    tables = jnp.asarray(_bucket_tables())
    bias = _bias_build(rel_bias, tables, name="bias_build")
    states, h = [], x
    for l in range(DEPTH):
        st = {}
        for j, piece in enumerate(PIECES):
            w, after = weights_of(l, piece, h)
            if piece == "mix":
                h, st[piece] = _mix_fwd(h, w, gains[l, j], mod[l, j], bias, sinks[l], f"l{l}", after)
            else:
                h, st[piece] = _ffn_fwd(h, w, gains[l, j], mod[l, j], f"{piece}_l{l}", after)
        states.append(st)
    loss, dx, dfinal = _final_loss(h, target, _row(final_gain), name="final_loss")
    dmods = [[None] * 3 for _ in range(DEPTH)]
    dgains = [[None] * 3 for _ in range(DEPTH)]
    dsinks = [None] * DEPTH
    dbias, made = None, None
    sweep = [(l, j) for l in reversed(range(DEPTH)) for j in reversed(range(3))]
    for k, (l, j) in enumerate(sweep):
        piece = PIECES[j]
        done = lambda grads, l=l, piece=piece: grads_done(l, piece, grads)
        nxt = None
        if k + 1 < len(sweep):
            nl, nj = sweep[k + 1]
            coef = 1.0 if PIECES[nj] == "mix" else 0.5
            nxt = (states[nl][PIECES[nj]]["f"], _row(coef * mod[nl, nj, 2]), coef)
        if piece == "mix":
            dx, dmods[l][j], dgains[l][j], db, dsinks[l], made = _mix_bwd(
                dx, states[l][piece], gains[l, j], mod[l, j], bias, f"l{l}", done, made, nxt)
            dbias = db if dbias is None else dbias + db
        else:
            dx, dmods[l][j], dgains[l][j], made = _ffn_bwd(
                dx, states[l][piece], gains[l, j], mod[l, j], f"{piece}_l{l}", done, made, nxt)
    drel = _bias_grad(dbias, tables, name="bias_grad")[:, 0, :N_BUCKETS].T
    dmod = jnp.stack([jnp.stack(m) for m in dmods])
    dgain = jnp.stack([jnp.stack(g) for g in dgains])
    return loss, dx, dmod, dgain, dfinal[0], drel, jnp.stack(dsinks)


BR_ROWS = (H_SB * HEAD_DIM, 2 * HEAD_DIM, H_SWA_Q * HEAD_DIM)


def _lanes_unshard(g, lead):
    _, rows, _ = g.shape
    r = rows // lead
    return g.reshape(N_DEV, lead, r, LANES).transpose(1, 2, 0, 3).reshape(lead, r, N_DEV * LANES)


def _lanes_shard(full):
    lead, r, _ = full.shape
    return full.reshape(lead, r, N_DEV, LANES).transpose(2, 0, 1, 3).reshape(N_DEV, lead * r, LANES)


def _pack_rows(parts, dtype):
    flat = jnp.concatenate([p.astype(dtype).reshape(-1) for p in parts])
    pad = (-flat.shape[0]) % (16 * LANES)
    if pad:
        flat = jnp.concatenate([flat, jnp.zeros((pad,), dtype)])
    return flat.reshape(-1, LANES)


def _unshard(gathered, axis):
    moved = jnp.moveaxis(gathered, 0, axis)
    shape = list(moved.shape)
    shape[axis:axis + 2] = [shape[axis] * shape[axis + 1]]
    return moved.reshape(shape)


def kernel(x, c, w_ada, b_ada, norm_gain, w_ffn_gate, w_ffn_up, w_ffn_down, w_in, w_br_sb, w_br_dil, w_br_swa, w_out, sinks, rel_bias, final_gain, loss_target, m_w_ada, m_b_ada, m_norm_gain, m_w_ffn_gate, m_w_ffn_up, m_w_ffn_down, m_w_in, m_w_br_sb, m_w_br_dil, m_w_br_swa, m_w_out, m_sinks, m_rel_bias, m_final_gain, v_w_ada, v_b_ada, v_norm_gain, v_w_ffn_gate, v_w_ffn_up, v_w_ffn_down, v_w_in, v_w_br_sb, v_w_br_dil, v_w_br_swa, v_w_out, v_sinks, v_rel_bias, v_final_gain):
    me = 4 * lax.axis_index("x") + 2 * lax.axis_index("y") + lax.axis_index("c")
    d = D_MODEL
    gate_t, up_t, in_t = jnp.swapaxes(w_ffn_gate, 2, 3), jnp.swapaxes(w_ffn_up, 2, 3), jnp.swapaxes(w_in, 1, 2)

    def piece_shards(l, piece):
        bf = lambda t: t.astype(BF16)
        if piece == "mix":
            return [bf(in_t[l]), jnp.concatenate([bf(w_br_sb[l]), bf(w_br_dil[l]), bf(w_br_swa[l])], 0), bf(w_out[l])]
        i = PIECES.index(piece) // 2
        return [bf(gate_t[l, i]), bf(up_t[l, i]), bf(w_ffn_down[l, i])]

    br_off = np.concatenate([[0], np.cumsum(BR_ROWS)])

    def piece_weights(gathered, piece):
        if piece == "mix":
            g_in, g_br, g_out = gathered
            f_br = [_lanes_unshard(g_br[:, br_off[k]:br_off[k + 1]], 1)[0] for k in range(3)]
            return {"in": g_in.reshape(D_QKV + D_GATES, d), "br_sb": f_br[0], "br_dil": f_br[1], "br_swa": f_br[2],
                    "out": g_out.reshape(d, d)}
        return {n: g.reshape(D_FF, d) for n, g in zip(("gate", "up", "down"), gathered)}

    order = [(l, piece) for l in range(DEPTH) for piece in PIECES]
    ahead = 3
    in_flight, passed = {}, {}

    def start_gather(k, after):
        l, piece = order[k]
        in_flight[k], token = _relay_start(piece_shards(l, piece), after, name=f"gather_{piece}_l{l}_start")
        return token

    small, = _all_gather([_pack_rows([c, norm_gain], F32)], after=start_gather(0, c), name="gather_cond")
    c_all = small[:, :d // LANES].reshape(N_DEV, d)
    gains = _unshard(small[:, d // LANES:d // LANES + 6].reshape(N_DEV, DEPTH, 3, LANES), 2)

    cols = w_ada.shape[2]
    mod_cols = jnp.stack([_ada_fwd(c_all, w_ada[l], name=f"ada_fwd_l{l}") for l in range(DEPTH)])
    mod_all, = _all_gather([_pack_rows([mod_cols], F32)], name="gather_mod")
    mod_all = mod_all.reshape(N_DEV, -1)[:, :DEPTH * N_DEV * cols].reshape(N_DEV, DEPTH, N_DEV, cols)
    mod_mine = lax.dynamic_index_in_dim(mod_all, me, axis=2, keepdims=False)
    mod = (mod_mine.transpose(1, 0, 2).reshape(DEPTH, N_DEV * cols) + b_ada).reshape(DEPTH, 3, 3, d)

    token = mod_all
    for k in range(1, 1 + ahead):
        token = start_gather(k, token)
    mod = mod + token[0, 0]

    def weights_of(l, piece, h):
        k = order.index((l, piece))
        token = start_gather(k + ahead, h) if k + ahead < len(order) and k + ahead not in in_flight else None
        for nxt in ([k] if k < 3 else []) + ([k + 1] if 3 <= k + 1 < len(order) else []):
            nl, npiece = order[nxt]
            passed[nxt], token = _relay_pass(in_flight[nxt], h if token is None else token,
                                             name=f"gather_{npiece}_l{nl}_pass")
        landed = _relay_wait(passed[k], h if token is None else token, name=f"gather_{piece}_l{l}_wait")
        return piece_weights(landed, piece), token

    exchanges, have, deferred = {}, {}, []

    def grads_done(l, piece, g):
        key = (l, piece)
        have.setdefault(key, {}).update(g)
        if piece == "mix":
            if len(have[key]) < 5:
                return None
            g = have[key]
            s_br = jnp.concatenate([_lanes_shard(g[n][None]) for n in ("br_sb", "br_dil", "br_swa")], 1)
            groups = [(("in", "br", "out"), [g["in"].reshape(N_DEV, -1, d), s_br, g["out"].reshape(N_DEV, -1, d)])]
        elif key == order[0]:
            deferred.extend(((n,), [t.reshape(N_DEV, -1, d)]) for n, t in g.items())
            return None
        elif len(have[key]) < 3:
            return None
        else:
            groups = [(("gate", "up", "down"), [have[key][n].reshape(N_DEV, -1, d) for n in ("gate", "up", "down")])]
        token = None
        for names, sg in groups:
            state, token = _exchange_start(sg, None, name=f"exchange_{piece}_l{l}_{names[0]}_start")
            exchanges.setdefault(key, []).append((names, state))
        return token

    loss, dx, dmod, dgains, dfinal, drel, dsinks = _local_step(
        x[0], loss_target[0], mod, gains, weights_of, rel_bias, sinks, final_gain, grads_done)

    flat = lambda t: t.reshape(-1, t.shape[-1])
    transposed = lambda ts: tuple(flat(jnp.swapaxes(t, -1, -2)) for t in ts)
    families = {
        "gate": transposed((w_ffn_gate, m_w_ffn_gate, v_w_ffn_gate)), "up": transposed((w_ffn_up, m_w_ffn_up, v_w_ffn_up)),
        "down": tuple(flat(t) for t in (w_ffn_down, m_w_ffn_down, v_w_ffn_down)),
        "in": transposed((w_in, m_w_in, v_w_in)),
        "br": tuple(flat(jnp.concatenate(ts, 1)) for ts in ((w_br_sb, w_br_dil, w_br_swa), (m_w_br_sb, m_w_br_dil, m_w_br_swa),
                                                            (v_w_br_sb, v_w_br_dil, v_w_br_swa))),
        "out": tuple(flat(t) for t in (w_out, m_w_out, v_w_out))}
    stepped = {}

    def step(keys, after):
        for l, piece in keys:
            for names, ex_state in exchanges[l, piece]:
                landed = _exchange_wait(ex_state, after, name=f"exchange_{piece}_l{l}_{names[0]}_wait")
                after = landed[0]
                for n, group in zip(names, landed):
                    w2, m2, v2 = families[n]
                    rows = group.shape[1]
                    row0 = (2 * l + PIECES.index(piece) // 2) * rows if piece != "mix" else l * rows
                    stepped[n] = _reduce_adamw([group], w2, m2, v2, row0, stepped.get(n), after=after,
                                               name=f"reduce_adamw_{n}_{piece}_l{l}")
                    after = stepped[n][1]
        return after

    small_parts = [dmod, dgains, dfinal, drel.T, dsinks, loss[0, :1]]
    small_sizes = [int(np.prod(p.shape)) for p in small_parts]
    small_all, = _all_gather([_pack_rows(small_parts, F32)], name="gather_small")
    token = small_all
    for names, sg in deferred:
        state, token = _exchange_start(sg, token, name=f"exchange_ffn0_l0_{names[0]}_start")
        exchanges.setdefault(order[0], []).append((names, state))

    after_l1 = step([key for key in reversed(order) if key[0] == 1], token)
    small_sum = _sum_parts(small_all, name="sum_small", after=token).reshape(-1)
    offs = np.concatenate([[0], np.cumsum(small_sizes)])
    g_b_ada = small_sum[offs[0]:offs[1]].reshape(DEPTH, 9 * d)
    g_gain_full = small_sum[offs[1]:offs[2]].reshape(DEPTH, 3, d)
    g_norm_gain = lax.dynamic_slice_in_dim(g_gain_full, me * LANES, LANES, axis=2)
    g_final = small_sum[offs[2]:offs[3]]
    g_rel = small_sum[offs[3]:offs[4]].reshape(N_SOFT, N_BUCKETS).T
    g_sinks = small_sum[offs[4]:offs[5]].reshape(DEPTH, H_SWA_Q)
    loss_total = small_sum[offs[5]]

    dmod_all = small_all.reshape(N_DEV, -1)[:, :DEPTH * 9 * d].reshape(N_DEV, DEPTH, 9 * d)
    dmod_cols = lax.dynamic_slice_in_dim(dmod_all, me * cols, cols, axis=2)
    g_w_ada = jnp.stack([_ada_bwd(c_all.T, dmod_cols[:, l], name=f"ada_bwd_l{l}") for l in range(DEPTH)])

    small_state = {"w_ada": (w_ada, m_w_ada, v_w_ada), "b_ada": (b_ada, m_b_ada, v_b_ada),
                   "norm_gain": (norm_gain, m_norm_gain, v_norm_gain), "sinks": (sinks, m_sinks, v_sinks),
                   "rel_bias": (rel_bias, m_rel_bias, v_rel_bias), "final_gain": (final_gain, m_final_gain, v_final_gain)}
    after = step([order[2], order[1]], after_l1)
    grad, update = {}, {}
    for n, g in (("w_ada", g_w_ada), ("b_ada", g_b_ada), ("norm_gain", g_norm_gain), ("sinks", g_sinks),
                 ("rel_bias", g_rel), ("final_gain", g_final)):
        w, m, v = small_state[n]
        grad[n] = g
        if w.ndim == 1:
            update[n] = tuple(t.reshape(w.shape)
                              for t in _adamw(_row(w), _row(g), _row(m), _row(v), name=f"adamw_{n}", after=after))
        else:
            update[n] = _adamw(w, g, m, v, name=f"adamw_{n}", after=after)
        after = update[n][0]

    step([order[0]], after)

    def unflat(n, like, swapped):
        shape = jnp.swapaxes(like, -1, -2).shape if swapped else like.shape
        out = [t.reshape(shape) for t in stepped[n]]
        return [jnp.swapaxes(t, -1, -2) for t in out] if swapped else out

    results = {"w_ffn_gate": unflat("gate", w_ffn_gate, True), "w_ffn_up": unflat("up", w_ffn_up, True),
               "w_ffn_down": unflat("down", w_ffn_down, False), "w_in": unflat("in", w_in, True),
               "w_out": unflat("out", w_out, False)}
    br = [t.reshape(DEPTH, -1, LANES) for t in stepped["br"]]
    for k, n in enumerate(("w_br_sb", "w_br_dil", "w_br_swa")):
        results[n] = [t[:, br_off[k]:br_off[k + 1]] for t in br]
    for n, (g, dl, nm, nv) in results.items():
        grad[n], update[n] = g, (dl, nm, nv)

    names = ["w_ada", "b_ada", "norm_gain", "w_ffn_gate", "w_ffn_up", "w_ffn_down", "w_in", "w_br_sb", "w_br_dil",
             "w_br_swa", "w_out", "sinks", "rel_bias", "final_gain"]
    return (loss_total, dx[None], *[grad[n] for n in names], *[update[n][0] for n in names],
            *[update[n][1] for n in names], *[update[n][2] for n in names])
```

```python
import math

import numpy as np
import jax
import jax.numpy as jnp
from jax import lax
from jax.experimental import pallas as pl
from jax.experimental.pallas import tpu as pltpu

F32, BF16 = jnp.float32, jnp.bfloat16

SEQ, D_MODEL, D_FF, HEAD_DIM = 2048, 1024, 2816, 64
DEPTH = 2
BLK = 128
H_SB, H_DIL, H_SWA_Q, H_SWA_KV = 4, 6, 6, 2
DIL_PATTERNS = ((128, 1), (512, 4), (2048, 16))
SWA_WINDOW = 128
N_BUCKETS, MAX_REL_DIST = 32, 2048
RMS_EPS = 1e-6
D_QKV = 2560
D_GATES = 3 * D_MODEL
ADAM_LR, ADAM_B1, ADAM_B2, ADAM_EPS, ADAM_WD, ADAM_STEP = 0.001, 0.9, 0.999, 1e-08, 0.01, 10

N_DEV = 8
LANES = 128
NEG = -1e30
SB_TILE = 512
VMEM_LIMIT_BYTES = 48 * 1024 * 1024
HBM = pl.BlockSpec(memory_space=pltpu.HBM)
MESH = pl.DeviceIdType.MESH


def _tile(n, target):
    t = (min(n, target) // LANES) * LANES
    while t >= LANES:
        if n % t == 0:
            return t
        t -= LANES
    return n


def _row_tile(r, cap):
    t = (min(r, cap) // 16) * 16
    while t > 16 and r % t:
        t -= 16
    return t


def _params(semantics=None):
    return pltpu.CompilerParams(dimension_semantics=semantics, vmem_limit_bytes=VMEM_LIMIT_BYTES)


def _dot(a, b, ca, cb):
    return lax.dot_general(a, b, (((ca,), (cb,)), ((), ())), preferred_element_type=F32)


def _sigmoid(a):
    return 1.0 / (1.0 + jnp.exp(-a))


def _row(v):
    return v.reshape(1, -1)


def _all_gather(arrs, name, after=None):
    n = len(arrs)
    ins = list(arrs) + ([] if after is None else [after])

    def body(*refs):
        x_refs, out_refs = refs[:n], refs[len(ins):len(ins) + n]
        send_sems, recv_sems, local_sems = refs[len(ins) + n:]
        x, y, c = lax.axis_index("x"), lax.axis_index("y"), lax.axis_index("c")
        me, sibling = (x, y, c), (x, y, 1 - c)
        chips = [(1 - x, y), (x, 1 - y), (1 - x, 1 - y)]

        def slot(t, px, py, pc):
            return out_refs[t].at[4 * px + 2 * py + pc]

        def copy(t, k, block, to, src=None):
            return pltpu.make_async_remote_copy(
                src_ref=slot(t, *block) if src is None else src, dst_ref=slot(t, *block),
                send_sem=send_sems.at[7 * t + k], recv_sem=recv_sems.at[7 * t + k], device_id=to, device_id_type=MESH)

        mine = [pltpu.make_async_copy(x_refs[t], slot(t, *me), local_sems.at[t]) for t in range(n)]
        for cp in mine:
            cp.start()
        first = []
        for t in range(n):
            first.append(copy(t, 0, me, sibling, src=x_refs[t]))
            first += [copy(t, 1 + j, me, (*chip, c), src=x_refs[t]) for j, chip in enumerate(chips)]
        for cp in first:
            cp.start()
        passed = []
        for j, chip in enumerate(chips):
            for t in range(n):
                copy(t, 1 + j, (*chip, c), me).wait_recv()
                passed.append(copy(t, 4 + j, (*chip, c), sibling))
                passed[-1].start()
        for t in range(n):
            copy(t, 0, sibling, me).wait_recv()
        for j, chip in enumerate(chips):
            for t in range(n):
                copy(t, 4 + j, (*chip, 1 - c), me).wait_recv()
        for cp in first + passed:
            cp.wait_send()
        for cp in mine:
            cp.wait()

    return pl.pallas_call(
        body, name=name, out_shape=[jax.ShapeDtypeStruct((N_DEV,) + a.shape, a.dtype) for a in arrs],
        in_specs=[HBM] * n + [pl.BlockSpec(memory_space=pl.ANY)] * (len(ins) - n), out_specs=[HBM] * n,
        scratch_shapes=[pltpu.SemaphoreType.DMA((7 * n,)), pltpu.SemaphoreType.DMA((7 * n,)), pltpu.SemaphoreType.DMA((n,))],
    )(*ins)


def _direct_copies(x_refs, land_refs, send_sems, recv_sems, local_sems):
    x, y, c = lax.axis_index("x"), lax.axis_index("y"), lax.axis_index("c")
    me = 4 * x + 2 * y + c
    sends, recvs = [], []
    for k in range(1, N_DEV):
        px = 1 - x if (k >> 2) & 1 else x
        py = 1 - y if (k >> 1) & 1 else y
        pc = 1 - c if k & 1 else c
        peer = 4 * px + 2 * py + pc
        for t, (x_ref, land_ref) in enumerate(zip(x_refs, land_refs)):
            sem = 7 * t + k - 1
            for out, src, slot in ((sends, peer, me), (recvs, me, peer)):
                out.append(pltpu.make_async_remote_copy(
                    src_ref=x_ref.at[src], dst_ref=land_ref.at[slot], send_sem=send_sems.at[sem],
                    recv_sem=recv_sems.at[sem], device_id=(px, py, pc), device_id_type=MESH))
    own = [pltpu.make_async_copy(x_ref.at[me], land_ref.at[me], local_sems.at[t])
           for t, (x_ref, land_ref) in enumerate(zip(x_refs, land_refs))]
    return sends, recvs, own


SEM =pl.BlockSpec(memory_space=pltpu.SEMAPHORE)
ANY = pl.BlockSpec(memory_space=pl.ANY)
SIDE_EFFECT = pltpu.SideEffectType.DATAFLOW_SIDE_EFFECTING


def _exchange_start(arrs, after, *, name):
    n = len(arrs)
    lands = [lax.empty(a.shape, a.dtype) for a in arrs]
    extra = [] if after is None else [after]

    def body(*refs):
        sems = refs[2 * n + len(extra):2 * n + len(extra) + 3]
        sends, _, own = _direct_copies(refs[:n], refs[n:2 * n], *sems)
        for cp in own + sends:
            cp.start()
        refs[-1][...] = jnp.zeros_like(refs[-1])

    ops = [pltpu.with_memory_space_constraint(a, pltpu.HBM) for a in list(arrs) + lands]
    out = pl.pallas_call(
        body, name=name,
        out_shape=(pltpu.SemaphoreType.DMA((7 * n,)), pltpu.SemaphoreType.DMA((7 * n,)), pltpu.SemaphoreType.DMA((n,)),
                   *[pltpu.HBM(a.shape, a.dtype) for a in ops], jax.ShapeDtypeStruct((8, LANES), F32)),
        in_specs=[HBM] * (2 * n) + [ANY] * len(extra),
        out_specs=(SEM, SEM, SEM, *[HBM] * (2 * n), pl.BlockSpec(memory_space=pltpu.VMEM)),
        input_output_aliases={t: 3 + t for t in range(2 * n)},
        compiler_params=pltpu.CompilerParams(has_side_effects=SIDE_EFFECT),
    )(*ops, *extra)
    return (out[:3], out[3:3 + n], out[3 + n:3 + 2 * n]), out[-1]


def _exchange_wait(state, after, *, name):
    sems, arrs, lands = state
    n = len(arrs)

    def body(*refs):
        sends, recvs, own = _direct_copies(refs[:n], refs[n:2 * n], *refs[2 * n:2 * n + 3])
        for cp in own:
            cp.wait()
        for cp in sends:
            cp.wait_send()
        for cp in recvs:
            cp.wait_recv()

    out = pl.pallas_call(
        body, name=name, out_shape=tuple(pltpu.HBM(a.shape, a.dtype) for a in list(arrs) + list(lands)),
        in_specs=[HBM] * (2 * n) + [SEM, SEM, SEM, ANY], out_specs=tuple([HBM] * (2 * n)),
        input_output_aliases={t: t for t in range(2 * n)},
        compiler_params=pltpu.CompilerParams(has_side_effects=SIDE_EFFECT),
    )(*arrs, *lands, *sems, after)
    return out[n:]


def _relay_copies(x_refs, land_refs, sems_a, sems_b):
    x, y, c = lax.axis_index("x"), lax.axis_index("y"), lax.axis_index("c")
    me = 4 * x + 2 * y + c
    sibling = (x, y, 1 - c)
    chips = [(1 - x, y), (x, 1 - y), (1 - x, 1 - y)]

    def slot(px, py, pc):
        return 4 * px + 2 * py + pc

    def copy(src, land_ref, dst_slot, send_sems, recv_sems, k, to):
        return pltpu.make_async_remote_copy(src_ref=src, dst_ref=land_ref.at[dst_slot], send_sem=send_sems.at[k],
                                            recv_sem=recv_sems.at[k], device_id=to, device_id_type=MESH)

    a_send, a_recv, a_own, b_send, b_recv = [], [], [], [], []
    for t, (x_ref, land_ref) in enumerate(zip(x_refs, land_refs)):
        peers = [sibling] + [(*chip, c) for chip in chips]
        if sems_a is not None:
            for k, peer in enumerate(peers):
                a_send.append(copy(x_ref, land_ref, me, sems_a[0], sems_a[1], 4 * t + k, peer))
                a_recv.append(copy(x_ref, land_ref, slot(*peer), sems_a[0], sems_a[1], 4 * t + k, peer))
            a_own.append(pltpu.make_async_copy(x_ref, land_ref.at[me], sems_a[2].at[t]))
        if sems_b is not None:
            for j, chip in enumerate(chips):
                b_send.append(copy(land_ref.at[slot(*chip, c)], land_ref, slot(*chip, c), sems_b[0], sems_b[1], 3 * t + j, sibling))
                b_recv.append(copy(land_ref.at[slot(*chip, c)], land_ref, slot(*chip, 1 - c), sems_b[0], sems_b[1], 3 * t + j,
                                   sibling))
    return (a_send, a_recv, a_own), (b_send, b_recv)


def _relay_start(arrs, after, name):
    n = len(arrs)
    lands = [lax.empty((N_DEV,) + a.shape, a.dtype) for a in arrs]

    def body(*refs):
        (sends, _, own), _ = _relay_copies(refs[:n], refs[n:2 * n], refs[2 * n + 1:2 * n + 4], None)
        for cp in own + sends:
            cp.start()
        refs[-1][...] = jnp.zeros_like(refs[-1])

    ops = [pltpu.with_memory_space_constraint(a, pltpu.HBM) for a in list(arrs) + lands]
    out = pl.pallas_call(
        body, name=name,
        out_shape=(pltpu.SemaphoreType.DMA((4 * n,)), pltpu.SemaphoreType.DMA((4 * n,)), pltpu.SemaphoreType.DMA((n,)),
                   *[pltpu.HBM(a.shape, a.dtype) for a in ops], jax.ShapeDtypeStruct((8, LANES), F32)),
        in_specs=[HBM] * (2 * n) + [ANY],
        out_specs=(SEM, SEM, SEM, *[HBM] * (2 * n), pl.BlockSpec(memory_space=pltpu.VMEM)),
        input_output_aliases={t: 3 + t for t in range(2 * n)},
        compiler_params=pltpu.CompilerParams(has_side_effects=SIDE_EFFECT),
    )(*ops, after)
    return (out[:3], out[3:3 + n], out[3 + n:3 + 2 * n]), out[-1]


def _relay_pass(state, after, name):
    sems_a, arrs, lands = state
    n = len(arrs)

    def body(*refs):
        sems_b = refs[2 * n + 4:2 * n + 6]
        (a_send, a_recv, a_own), (b_send, _) = _relay_copies(refs[:n], refs[n:2 * n], refs[2 * n:2 * n + 3], sems_b)
        for cp in a_own:
            cp.wait()
        for cp in a_send:
            cp.wait_send()
        for cp in a_recv:
            cp.wait_recv()
        for cp in b_send:
            cp.start()
        refs[-1][...] = jnp.zeros_like(refs[-1])

    out = pl.pallas_call(
        body, name=name,
        out_shape=(pltpu.SemaphoreType.DMA((3 * n,)), pltpu.SemaphoreType.DMA((3 * n,)),
                   *[pltpu.HBM(a.shape, a.dtype) for a in list(arrs) + list(lands)], jax.ShapeDtypeStruct((8, LANES), F32)),
        in_specs=[HBM] * (2 * n) + [SEM, SEM, SEM, ANY],
        out_specs=(SEM, SEM, *[HBM] * (2 * n), pl.BlockSpec(memory_space=pltpu.VMEM)),
        input_output_aliases={t: 2 + t for t in range(2 * n)},
        compiler_params=pltpu.CompilerParams(has_side_effects=SIDE_EFFECT),
    )(*arrs, *lands, *sems_a, after)
    return (out[:2], out[2:2 + n], out[2 + n:2 + 2 * n]), out[-1]


def _relay_wait(state, after, name):
    sems_b, arrs, lands = state
    n = len(arrs)

    def body(*refs):
        _, (b_send, b_recv) = _relay_copies(refs[:n], refs[n:2 * n], None, refs[2 * n:2 * n + 2])
        for cp in b_send:
            cp.wait_send()
        for cp in b_recv:
            cp.wait_recv()

    out = pl.pallas_call(
        body, name=name, out_shape=tuple(pltpu.HBM(a.shape, a.dtype) for a in list(arrs) + list(lands)),
        in_specs=[HBM] * (2 * n) + [SEM, SEM, ANY], out_specs=tuple([HBM] * (2 * n)),
        input_output_aliases={t: t for t in range(2 * n)},
        compiler_params=pltpu.CompilerParams(has_side_effects=SIDE_EFFECT),
    )(*arrs, *lands, *sems_b, after)
    return out[n:]


def _sum_parts(parts, name, after=None):
    n, r, cdim = parts.shape
    tr = _row_tile(r, max(16, (1 << 21) // (n * cdim * parts.dtype.itemsize)))

    def body(p_ref, *rest):
        acc = p_ref[0].astype(F32)
        for k in range(1, n):
            acc = acc + p_ref[k].astype(F32)
        rest[-1][...] = acc

    ins = [parts] + ([] if after is None else [after])
    return pl.pallas_call(
        body, name=name, out_shape=jax.ShapeDtypeStruct((r, cdim), F32), grid=(r // tr,),
        in_specs=[pl.BlockSpec((n, tr, cdim), lambda i: (0, i, 0))] + [ANY] * (len(ins) - 1),
        out_specs=pl.BlockSpec((tr, cdim), lambda i: (i, 0)), compiler_params=_params(("parallel",)),
    )(*ins)


def _mm_tn(a, b, *, name, after=None, tm=512, tn=1024, out_rows=None, row0=0, prev=None):
    k, m = a.shape
    n = b.shape[1]
    tm, tn = _tile(m, tm), _tile(n, tn)
    out_rows = m if out_rows is None else out_rows

    def body(a_ref, b_ref, *rest):
        o_ref, at_ref = rest[-2], rest[-1]

        @pl.when(pl.program_id(1) == 0)
        def _():
            at_ref[...] = a_ref[...].astype(BF16).T

        o_ref[...] = _dot(at_ref[...], b_ref[...].astype(BF16), 1, 0).astype(BF16)

    ins = [a, b] + [t for t in (after, prev) if t is not None]
    return pl.pallas_call(
        body, name=name, out_shape=jax.ShapeDtypeStruct((out_rows, n), BF16), grid=(m // tm, n // tn),
        in_specs=[pl.BlockSpec((k, tm), lambda i, j: (0, i)), pl.BlockSpec((k, tn), lambda i, j: (0, j))] + [ANY] * (len(ins) - 2),
        out_specs=pl.BlockSpec((tm, tn), lambda i, j: (row0 // tm + i, j)),
        input_output_aliases={} if prev is None else {len(ins) - 1: 0},
        scratch_shapes=[pltpu.VMEM((tm, k), BF16)], compiler_params=_params(("parallel", "arbitrary")),
    )(*ins)


def _mm(a, b, *, name, ta=False, tb=False, res=None, colscale=None, emit_acc=False,
        out_dtype=F32, tm=512, tn=512, b_rows=None):
    m, k = (a.shape[1], a.shape[0]) if ta else a.shape
    n = b.shape[0] if tb else b.shape[1]
    b_start = 0
    if b_rows is not None:
        b_start, n = b_rows
    tm, tn = _tile(m, tm), _tile(n, tn)
    ca, cb = (0 if ta else 1), (1 if tb else 0)
    a_spec = pl.BlockSpec((k, tm), lambda i, j: (0, i)) if ta else pl.BlockSpec((tm, k), lambda i, j: (i, 0))
    b_spec = (pl.BlockSpec((tn, k), lambda i, j: (b_start // tn + j, 0)) if tb
              else pl.BlockSpec((k, tn), lambda i, j: (0, j)))
    tile = pl.BlockSpec((tm, tn), lambda i, j: (i, j))
    ins, in_specs = [a, b], [a_spec, b_spec]
    if res is not None:
        ins.append(res)
        in_specs.append(tile)
    if colscale is not None:
        ins.append(colscale)
        in_specs.append(pl.BlockSpec((1, tn), lambda i, j: (0, j)))
    n_in = len(ins)

    def body(*refs):
        outs = refs[n_in:]
        acc = _dot(refs[0][...].astype(BF16), refs[1][...].astype(BF16), ca, cb)
        val, p = acc, 2
        if res is not None:
            r_val, p = refs[p][...], p + 1
        if colscale is not None:
            val = val * refs[p][...]
        if res is not None:
            val = r_val + val
        if emit_acc:
            outs[0][...] = acc
        outs[-1][...] = val.astype(out_dtype)

    out_shape = [jax.ShapeDtypeStruct((m, n), out_dtype)]
    out_specs = [tile]
    if emit_acc:
        out_shape.insert(0, jax.ShapeDtypeStruct((m, n), F32))
        out_specs.insert(0, tile)
    out = pl.pallas_call(
        body, name=name, out_shape=out_shape, grid=(m // tm, n // tn), in_specs=in_specs, out_specs=out_specs,
        compiler_params=_params(("parallel", "parallel")),
    )(*ins)
    return out if emit_acc else out[0]


def _norm_fwd(x, g, scale, shift, name, after=None):
    s, d = x.shape
    tr = TOK_TILE

    def body(x_ref, g_ref, sc_ref, sh_ref, *rest):
        xv = x_ref[...]
        rstd = lax.rsqrt(jnp.mean(xv * xv, axis=-1, keepdims=True) + RMS_EPS)
        rest[-1][...] = (xv * rstd * g_ref[...] * (1.0 + sc_ref[...]) + sh_ref[...]).astype(BF16)

    rowspec = pl.BlockSpec((1, d), lambda i: (0, 0))
    ins = [x, g, scale, shift] + ([] if after is None else [after])
    return pl.pallas_call(
        body, name=name, out_shape=jax.ShapeDtypeStruct((s, d), BF16), grid=(s // tr,),
        in_specs=[pl.BlockSpec((tr, d), lambda i: (i, 0)), rowspec, rowspec, rowspec] + [ANY] * (len(ins) - 4),
        out_specs=pl.BlockSpec((tr, d), lambda i: (i, 0)),
        compiler_params=_params(("parallel",)),
    )(*ins)


def _dh_norm_bwd(a1, b1, a2, b2, x, dres, g, scale, nxt, *, name, after=None, b_rows=None):
    s, d = x.shape
    tm = 256
    n_fixed = 8

    def body(a1_ref, b1_ref, a2_ref, b2_ref, x_ref, dr_ref, g_ref, sc_ref, *rest):
        rest = rest[(1 if after is not None else 0):]
        if nxt is not None:
            f_ref, cs_ref, dx_ref, sa_ref, sb_ref, df_ref, dg_ref = rest
        else:
            dx_ref, sa_ref, sb_ref = rest

        @pl.when(pl.program_id(0) == 0)
        def _():
            sa_ref[...] = jnp.zeros_like(sa_ref)
            sb_ref[...] = jnp.zeros_like(sb_ref)
            if nxt is not None:
                dg_ref[...] = jnp.zeros_like(dg_ref)

        dhv = (_dot(a1_ref[...].astype(BF16), b1_ref[...], 1, 0) + _dot(a2_ref[...].astype(BF16), b2_ref[...], 1, 0))
        xv = x_ref[...]
        rstd = lax.rsqrt(jnp.mean(xv * xv, axis=-1, keepdims=True) + RMS_EPS)
        xhat = xv * rstd
        dxhat = dhv * (g_ref[...] * (1.0 + sc_ref[...]))
        mean_term = jnp.mean(dxhat * xhat, axis=-1, keepdims=True)
        dxv = dr_ref[...] + rstd * (dxhat - xhat * mean_term)
        dx_ref[...] = dxv
        sa_ref[...] += jnp.sum(dhv, axis=0, keepdims=True)
        sb_ref[...] += jnp.sum(dhv * xhat, axis=0, keepdims=True)
        if nxt is not None:
            df_ref[...] = (dxv * cs_ref[...]).astype(BF16)
            dg_ref[...] += nxt[2] * jnp.sum(dxv * f_ref[...], axis=0, keepdims=True)

    def a_spec(t):
        return pl.BlockSpec((tm, t.shape[1]), lambda i: (i, 0))

    def b_spec(t, a, which):
        if b_rows is None:
            return pl.BlockSpec((t.shape[0], d), lambda i: (0, 0))
        start = b_rows[which]
        return pl.BlockSpec((pl.Element(a.shape[1]), pl.Element(d)), lambda i: (start, 0))

    rowspec = pl.BlockSpec((1, d), lambda i: (0, 0))
    tile = pl.BlockSpec((tm, d), lambda i: (i, 0))
    ins = [a1, b1, a2, b2, x, dres, g, scale] + ([] if after is None else [after])
    in_specs = [a_spec(a1), b_spec(b1, a1, 0), a_spec(a2), b_spec(b2, a2, 1), tile, tile, rowspec, rowspec]
    in_specs += [ANY] * (len(ins) - n_fixed)
    out_shape = [jax.ShapeDtypeStruct((s, d), F32), jax.ShapeDtypeStruct((1, d), F32), jax.ShapeDtypeStruct((1, d), F32)]
    out_specs = [tile, rowspec, rowspec]
    if nxt is not None:
        ins += [nxt[0], nxt[1]]
        in_specs += [tile, rowspec]
        out_shape += [jax.ShapeDtypeStruct((s, d), BF16), jax.ShapeDtypeStruct((1, d), F32)]
        out_specs += [tile, rowspec]
    out = pl.pallas_call(
        body, name=name, out_shape=out_shape, grid=(s // tm,), in_specs=in_specs, out_specs=out_specs,
        compiler_params=_params(("arbitrary",)),
    )(*ins)
    return out[0], out[1], out[2], (None if nxt is None else (out[3], out[4]))


def _gate_bwd(dxn, f, colscale, coef, name):
    s, d = dxn.shape
    tr = TOK_TILE

    def body(dx_ref, f_ref, cs_ref, df_ref, dg_ref):
        @pl.when(pl.program_id(0) == 0)
        def _():
            dg_ref[...] = jnp.zeros_like(dg_ref)

        dxv = dx_ref[...]
        df_ref[...] = (dxv * cs_ref[...]).astype(BF16)
        dg_ref[...] += coef * jnp.sum(dxv * f_ref[...], axis=0, keepdims=True)

    rowspec = pl.BlockSpec((1, d), lambda i: (0, 0))
    tile = pl.BlockSpec((tr, d), lambda i: (i, 0))
    return pl.pallas_call(
        body, name=name, out_shape=[jax.ShapeDtypeStruct((s, d), BF16), jax.ShapeDtypeStruct((1, d), F32)],
        grid=(s // tr,), in_specs=[tile, tile, rowspec], out_specs=[tile, rowspec],
        compiler_params=_params(("arbitrary",)),
    )(dxn, f, colscale)


def _ffn_up(h, wg, wu, name, tm=SEQ, tn=256):
    s, d = h.shape
    f = wg.shape[0]

    def body(h_ref, wg_ref, wu_ref, a_ref, u_ref, s_ref):
        hv = h_ref[...]
        a = _dot(hv, wg_ref[...], 1, 1)
        u = _dot(hv, wu_ref[...], 1, 1)
        a_ref[...] = a.astype(BF16)
        u_ref[...] = u.astype(BF16)
        s_ref[...] = (a * _sigmoid(a) * u).astype(BF16)

    tile = pl.BlockSpec((tm, tn), lambda i, j: (i, j))
    wspec = pl.BlockSpec((tn, d), lambda i, j: (j, 0))
    return pl.pallas_call(
        body, name=name,
        out_shape=[jax.ShapeDtypeStruct((s, f), BF16), jax.ShapeDtypeStruct((s, f), BF16), jax.ShapeDtypeStruct((s, f), BF16)],
        grid=(s // tm, f // tn), in_specs=[pl.BlockSpec((tm, d), lambda i, j: (i, 0)), wspec, wspec],
        out_specs=[tile, tile, tile], compiler_params=_params(("parallel", "parallel")),
    )(h, wg, wu)


def _ffn_bwd_ds(df, wd, a, u, name, tm=SEQ, tn=256):
    s, d = df.shape
    f = wd.shape[0]

    def body(df_ref, wd_ref, a_ref, u_ref, da_ref, du_ref):
        ds = _dot(df_ref[...], wd_ref[...], 1, 1)
        av = a_ref[...].astype(F32)
        sg = _sigmoid(av)
        da_ref[...] = (ds * u_ref[...].astype(F32) * (sg * (1.0 + av * (1.0 - sg)))).astype(BF16)
        du_ref[...] = (ds * (av * sg)).astype(BF16)

    tile = pl.BlockSpec((tm, tn), lambda i, j: (i, j))
    return pl.pallas_call(
        body, name=name, out_shape=[jax.ShapeDtypeStruct((s, f), BF16), jax.ShapeDtypeStruct((s, f), BF16)],
        grid=(s // tm, f // tn),
        in_specs=[pl.BlockSpec((tm, d), lambda i, j: (i, 0)), pl.BlockSpec((tn, d), lambda i, j: (j, 0)), tile, tile],
        out_specs=[tile, tile], compiler_params=_params(("parallel", "parallel")),
    )(df, wd, a, u)


def _merge_fwd(o_sb, o_dil, o_swa, gates, wb_sb, wb_dil, wb_swa, name):
    s, d = SEQ, D_MODEL
    tm = TOK_TILE

    def body(osb_ref, odl_ref, osw_ref, g_ref, wsb_ref, wdl_ref, wsw_ref, m_ref, tsb_ref, tdl_ref, tsw_ref):
        for h in range(osb_ref.shape[0]):
            tsb_ref[:, h * HEAD_DIM:(h + 1) * HEAD_DIM] = osb_ref[h].astype(BF16)
        for h in range(osw_ref.shape[0]):
            tsw_ref[:, h * HEAD_DIM:(h + 1) * HEAD_DIM] = osw_ref[h].astype(BF16)
        tdl_ref[...] = odl_ref[...].astype(BF16)
        acc = _sigmoid(g_ref[:, 0:d]) * _dot(tsb_ref[...], wsb_ref[...], 1, 0)
        acc += _sigmoid(g_ref[:, d:2 * d]) * _dot(tdl_ref[...], wdl_ref[...], 1, 0)
        acc += _sigmoid(g_ref[:, 2 * d:3 * d]) * _dot(tsw_ref[...], wsw_ref[...], 1, 0)
        m_ref[...] = acc.astype(BF16)

    def rows(w):
        return pl.BlockSpec((tm, w), lambda i: (i, 0))

    def heads(n):
        return pl.BlockSpec((n, tm, HEAD_DIM), lambda i: (0, i, 0))

    def whole(w):
        return pl.BlockSpec((w, d), lambda i: (0, 0))

    return pl.pallas_call(
        body, name=name, out_shape=[jax.ShapeDtypeStruct((s, w), BF16) for w in (d, 256, 128, 384)], grid=(s // tm,),
        in_specs=[heads(H_SB), rows(128), heads(H_SWA_Q), rows(3 * d), whole(256), whole(128), whole(384)],
        out_specs=[rows(d), rows(256), rows(128), rows(384)], compiler_params=_params(("parallel",)),
    )(o_sb, o_dil, o_swa, gates, wb_sb, wb_dil, wb_swa)


def _merge_bwd(dmerged, t_sb, t_dil, t_swa, gates, wb_sb, wb_dil, wb_swa, name):
    s, d = SEQ, D_MODEL
    tm = 256

    def body(dm_ref, tsb_ref, tdl_ref, tsw_ref, g_ref, wsb_ref, wdl_ref, wsw_ref,
             dg_ref, dosb_ref, dodl_ref, dosw_ref, dbsb_ref, dbdl_ref, dbsw_ref):
        dm = dm_ref[...]
        for idx, (t_ref, w_ref, do_ref, db_ref) in enumerate((
                (tsb_ref, wsb_ref, dosb_ref, dbsb_ref), (tdl_ref, wdl_ref, dodl_ref, dbdl_ref),
                (tsw_ref, wsw_ref, dosw_ref, dbsw_ref))):
            w = w_ref[...]
            br = _dot(t_ref[...], w, 1, 0)
            sg = _sigmoid(g_ref[:, idx * d:(idx + 1) * d])
            dbr = (dm * sg).astype(BF16)
            dg_ref[:, idx * d:(idx + 1) * d] = (dm * br * (sg * (1.0 - sg))).astype(BF16)
            db_ref[...] = dbr
            do = _dot(dbr, w, 1, 1)
            if len(do_ref.shape) == 2:
                do_ref[...] = do
            else:
                for h in range(do_ref.shape[0]):
                    do_ref[h] = do[:, h * HEAD_DIM:(h + 1) * HEAD_DIM]

    def rows(w):
        return pl.BlockSpec((tm, w), lambda i: (i, 0))

    def heads(n):
        return pl.BlockSpec((n, tm, HEAD_DIM), lambda i: (0, i, 0))

    def whole(w):
        return pl.BlockSpec((w, d), lambda i: (0, 0))

    def shp(w, dt):
        return jax.ShapeDtypeStruct((s, w), dt)

    def hshp(n):
        return jax.ShapeDtypeStruct((n, s, HEAD_DIM), F32)

    return pl.pallas_call(
        body, name=name,
        out_shape=[shp(3 * d, BF16), hshp(H_SB), shp(128, F32), hshp(H_SWA_Q), shp(d, BF16), shp(d, BF16), shp(d, BF16)],
        grid=(s // tm,),
        in_specs=[rows(d), rows(256), rows(128), rows(384), rows(3 * d), whole(256), whole(128), whole(384)],
        out_specs=[rows(3 * d), heads(H_SB), rows(128), heads(H_SWA_Q), rows(d), rows(d), rows(d)],
        compiler_params=_params(("parallel",)),
    )(dmerged, t_sb, t_dil, t_swa, gates, wb_sb, wb_dil, wb_swa)


def _final_loss(x, target, g, name):
    s, d = x.shape
    tr = TOK_TILE

    def body(x_ref, t_ref, g_ref, loss_ref, dx_ref, dg_ref):
        @pl.when(pl.program_id(0) == 0)
        def _():
            loss_ref[...] = jnp.zeros_like(loss_ref)
            dg_ref[...] = jnp.zeros_like(dg_ref)

        xv = x_ref[...]
        gv = g_ref[...]
        rstd = lax.rsqrt(jnp.mean(xv * xv, axis=-1, keepdims=True) + RMS_EPS)
        xhat = xv * rstd
        err = xhat * gv - t_ref[...]
        loss_ref[...] += 0.5 * jnp.sum(jnp.mean(err * err, axis=-1, keepdims=True))
        dy = err * (1.0 / d)
        dxhat = dy * gv
        mean_term = jnp.mean(dxhat * xhat, axis=-1, keepdims=True)
        dx_ref[...] = rstd * (dxhat - xhat * mean_term)
        dg_ref[...] += jnp.sum(dy * xhat, axis=0, keepdims=True)

    rowspec = pl.BlockSpec((1, d), lambda i: (0, 0))
    tile = pl.BlockSpec((tr, d), lambda i: (i, 0))
    return pl.pallas_call(
        body, name=name,
        out_shape=[jax.ShapeDtypeStruct((1, LANES), F32), jax.ShapeDtypeStruct((s, d), F32), jax.ShapeDtypeStruct((1, d), F32)],
        grid=(s // tr,), in_specs=[tile, tile, rowspec],
        out_specs=[pl.BlockSpec((1, LANES), lambda i: (0, 0)), tile, rowspec],
        compiler_params=_params(("arbitrary",)),
    )(x, target, g)


def _adamw(w, g, m, v, name, after=None):
    shape = w.shape
    cols = shape[-1]
    rows = int(np.prod(shape[:-1])) if len(shape) > 1 else 1
    tr = rows
    for cand in (1024, 512, 256, 128, 64, 32, 16, 8):
        if rows % cand == 0 and rows > cand and cand * cols * 4 <= (1 << 21):
            tr = cand
            break

    def body(w_ref, g_ref, m_ref, v_ref, *rest):
        d_ref, nm_ref, nv_ref = rest[-3:]
        d_ref[...], nm_ref[...], nv_ref[...] = _adam_update(w_ref[...], g_ref[...], m_ref[...], v_ref[...])

    tile = pl.BlockSpec((tr, cols), lambda i: (i, 0))
    flat = [t.reshape(rows, cols) for t in (w, g, m, v)] + ([] if after is None else [after])
    out = pl.pallas_call(
        body, name=name, out_shape=[jax.ShapeDtypeStruct((rows, cols), F32)] * 3, grid=(rows // tr,),
        in_specs=[tile] * 4 + [ANY] * (len(flat) - 4), out_specs=[tile] * 3, compiler_params=_params(("parallel",)),
    )(*flat)
    return tuple(t.reshape(shape) for t in out)


def _adam_update(w, gv, m, v):
    nm = ADAM_B1 * m + (1.0 - ADAM_B1) * gv
    nv = ADAM_B2 * v + (1.0 - ADAM_B2) * (gv * gv)
    m_hat = nm / (1.0 - ADAM_B1 ** ADAM_STEP)
    v_hat = nv / (1.0 - ADAM_B2 ** ADAM_STEP)
    return -ADAM_LR * (m_hat / (jnp.sqrt(v_hat) + ADAM_EPS) + ADAM_WD * w), nm, nv


def _reduce_adamw(groups, w, m, v, row0, prev, name, after=None):
    n, r, cdim = groups[0].shape
    rows = w.shape[0]
    tr = _row_tile(r, max(16, (1 << 22) // (n * cdim * groups[0].dtype.itemsize)))
    steps = r // tr
    ng = len(groups)

    def body(*refs):
        w_ref, m_ref, v_ref = refs[ng:ng + 3]
        g_out, d_out, m_out, v_out = refs[-4:]
        gg = pl.program_id(0)
        for gi in range(ng):
            @pl.when(gg == gi)
            def _(gi=gi):
                acc = refs[gi][0].astype(F32)
                for k in range(1, n):
                    acc = acc + refs[gi][k].astype(F32)
                g_out[...] = acc
                d_out[...], m_out[...], v_out[...] = _adam_update(w_ref[...], acc, m_ref[...], v_ref[...])

    def part_spec(gi):
        return pl.BlockSpec((n, tr, cdim), lambda gg, i: (0, jnp.where(gg == gi, i, 0), 0))

    tile = pl.BlockSpec((tr, cdim), lambda gg, i: (row0 // tr + gg * steps + i, 0))
    extra = ([] if prev is None else list(prev)) + ([] if after is None else [after])
    return pl.pallas_call(
        body, name=name, out_shape=[jax.ShapeDtypeStruct((rows, cdim), F32)] * 4, grid=(ng, steps),
        in_specs=[part_spec(gi) for gi in range(ng)] + [tile] * 3 + [ANY] * len(extra), out_specs=[tile] * 4,
        input_output_aliases={} if prev is None else {ng + 3 + k: k for k in range(4)},
        compiler_params=_params(("parallel", "parallel")),
    )(*groups, w, m, v, *extra)


def _ada_fwd(c_all, w, name):
    n = w.shape[1]

    def body(c_ref, w_ref, o_ref):
        cv = c_ref[...]
        o_ref[...] = jnp.dot(cv * _sigmoid(cv), w_ref[...], preferred_element_type=F32, precision=lax.Precision.HIGHEST)

    return pl.pallas_call(body, name=name, out_shape=jax.ShapeDtypeStruct((N_DEV, n), F32), compiler_params=_params())(c_all, w)


def _ada_bwd(c_all_t, dmod, name):
    n = dmod.shape[1]

    def body(c_ref, d_ref, o_ref):
        cv = c_ref[...]
        o_ref[...] = jnp.dot(cv * _sigmoid(cv), d_ref[...], preferred_element_type=F32, precision=lax.Precision.HIGHEST)

    return pl.pallas_call(body, name=name, out_shape=jax.ShapeDtypeStruct((D_MODEL, n), F32), compiler_params=_params())(c_all_t, dmod)


def _bucket_tables():
    rel = np.arange(BLK)[:, None] + BLK - np.arange(2 * BLK)[None, :]
    max_exact = N_BUCKETS // 2

    def bucket(n):
        nf = np.maximum(n, 1).astype(np.float32)
        large = max_exact + (np.log(nf / np.float32(max_exact)) / np.float32(math.log(MAX_REL_DIST / max_exact))
                             * np.float32(N_BUCKETS - max_exact)).astype(np.int32)
        return np.where(n < max_exact, n, np.minimum(large, N_BUCKETS - 1))

    tabs = []
    for dil, max_dist in ((1, 128), (4, 128), (16, 128), (1, SWA_WINDOW - 1)):
        in_band = (rel >= 0) & (rel <= max_dist)
        tabs.append(np.where(in_band, bucket(np.maximum(rel, 0) * dil), -1))
    return np.stack(tabs).astype(np.int32)


N_SOFT = H_DIL + H_SWA_Q


def _table_of_head(h):
    return jnp.minimum(h // 2, 3)


def _bias_build(rel_bias, tables, name):
    def body(rel_ref, t_ref, o_ref):
        h = pl.program_id(0)
        tb = t_ref[0]
        out = jnp.full((BLK, 2 * BLK), NEG, F32)
        for b in range(N_BUCKETS):
            out = jnp.where(tb == b, rel_ref[b, h], out)
        o_ref[0] = out

    return pl.pallas_call(
        body, name=name, out_shape=jax.ShapeDtypeStruct((N_SOFT, BLK, 2 * BLK), F32), grid=(N_SOFT,),
        in_specs=[pl.BlockSpec(memory_space=pltpu.SMEM),
                  pl.BlockSpec((1, BLK, 2 * BLK), lambda h: (_table_of_head(h), 0, 0))],
        out_specs=pl.BlockSpec((1, BLK, 2 * BLK), lambda h: (h, 0, 0)),
        compiler_params=_params(("parallel",)),
    )(rel_bias, tables)


def _bias_grad(dbias, tables, name):
    def body(d_ref, t_ref, o_ref):
        tb = t_ref[0]
        dv = d_ref[0]
        lane = lax.broadcasted_iota(jnp.int32, (1, LANES), 1)
        out = jnp.zeros((1, LANES), F32)
        for b in range(N_BUCKETS):
            out = jnp.where(lane == b, jnp.sum(jnp.where(tb == b, dv, 0.0)), out)
        o_ref[0] = out

    return pl.pallas_call(
        body, name=name, out_shape=jax.ShapeDtypeStruct((N_SOFT, 1, LANES), F32), grid=(N_SOFT,),
        in_specs=[pl.BlockSpec((1, BLK, 2 * BLK), lambda h: (h, 0, 0)),
                  pl.BlockSpec((1, BLK, 2 * BLK), lambda h: (_table_of_head(h), 0, 0))],
        out_specs=pl.BlockSpec((1, 1, LANES), lambda h: (h, 0, 0)),
        compiler_params=_params(("parallel",)),
    )(dbias, tables)


def _band_layout(g, bias_div):
    assert g == 1 or bias_div == 1
    return bias_div if g == 1 else 1


def _band_specs(length, g, bias_div, offs):
    ns = _band_layout(g, bias_div)

    def seqs(off, div=1):
        return pl.BlockSpec((ns, length, HEAD_DIM), lambda s: (off // ns + s // div, 0, 0))

    xspecs = [seqs(offs[0]), seqs(offs[1], g), seqs(offs[2], g)]
    bspec = pl.BlockSpec((1, BLK, 2 * BLK), lambda s: (s, 0, 0))
    sspec = pl.BlockSpec((ns, 1, LANES), lambda s: (s, 0, 0))
    colspec = pl.BlockSpec((ns, length, 1), lambda s: (s, 0, 0))
    return xspecs, seqs(0), seqs(0, g), bspec, sspec, colspec


def _band_sweep(length, ns, one):
    nblk = length // BLK
    for qq in range(ns):
        if ns * nblk <= 16:
            for i in range(nblk):
                one(qq, i * BLK, max(i - 1, 0) * BLK, i == 0)
        else:
            def step(i, carry, qq=qq):
                one(qq, pl.multiple_of(i * BLK, BLK), pl.multiple_of(jnp.maximum(i - 1, 0) * BLK, BLK), i == 0)
                return carry

            lax.fori_loop(0, nblk, step, 0, unroll=2)


def _band_scores(q_ref, k_ref, b_ref, qq, kq, bq, cur, prv, first):
    qv = q_ref[qq, pl.ds(cur, BLK), :]
    bv = b_ref[bq]
    if first is True:
        sp = jnp.full((BLK, BLK), NEG, F32)
    else:
        sp = _dot(qv, k_ref[kq, pl.ds(prv, BLK), :], 1, 1) + bv[:, :BLK]
        sp = sp if first is False else jnp.where(first, NEG, sp)
    sc = _dot(qv, k_ref[kq, pl.ds(cur, BLK), :], 1, 1) + bv[:, BLK:]
    return qv, sp, sc


def _band_fwd(x, bias, sink, *, nq, offs, g, bias_div, has_sink, name):
    length = x.shape[1]
    ns = _band_layout(g, bias_div)

    def body(q_ref, k_ref, v_ref, b_ref, s_ref, o_ref, lse_ref):
        def one(qq, cur, prv, first):
            kq, bq = qq, 0
            _, sp, sc = _band_scores(q_ref, k_ref, b_ref, qq, kq, bq, cur, prv, first)
            m = jnp.maximum(jnp.max(sp, axis=1, keepdims=True), jnp.max(sc, axis=1, keepdims=True))
            if has_sink:
                sk = s_ref[qq][:, :1]
                m = jnp.maximum(m, sk)
            pp, pc = jnp.exp(sp - m), jnp.exp(sc - m)
            den = jnp.sum(pp, axis=1, keepdims=True) + jnp.sum(pc, axis=1, keepdims=True)
            if has_sink:
                den = den + jnp.exp(sk - m)
            acc = (_dot(pp.astype(BF16), v_ref[kq, pl.ds(prv, BLK), :], 1, 0)
                   + _dot(pc.astype(BF16), v_ref[kq, pl.ds(cur, BLK), :], 1, 0))
            o_ref[qq, pl.ds(cur, BLK), :] = acc / den
            lse_ref[qq, pl.ds(cur, BLK), :] = m + jnp.log(den)

        _band_sweep(length, ns, one)

    xspecs, qspec, _, bspec, sspec, colspec = _band_specs(length, g, bias_div, offs)
    return pl.pallas_call(
        body, name=name,
        out_shape=[jax.ShapeDtypeStruct((nq, length, HEAD_DIM), F32), jax.ShapeDtypeStruct((nq, length, 1), F32)],
        grid=(nq // ns,), in_specs=xspecs + [bspec, sspec],
        out_specs=[qspec, colspec], compiler_params=_params(("parallel",)),
    )(x, x, x, bias, sink)


def _band_bwd(x, bias, sink, o, lse, do, dlse, *, nq, offs, g, bias_div, has_sink, name):
    length = x.shape[1]
    ns = _band_layout(g, bias_div)
    nk, nbias = nq // g, nq // bias_div

    def body(q_ref, k_ref, v_ref, b_ref, s_ref, o_ref, lse_ref, do_ref, dlse_ref,
             dq_ref, dk_ref, dv_ref, db_ref, dsk_ref, dkp_ref, dvp_ref):
        for ref in (db_ref, dsk_ref, dkp_ref, dvp_ref):
            ref[...] = jnp.zeros_like(ref)

        @pl.when(pl.program_id(0) % g == 0)
        def _():
            dk_ref[...] = jnp.zeros_like(dk_ref)
            dv_ref[...] = jnp.zeros_like(dv_ref)

        def one(qq, cur, prv, first):
            kq, bq = qq, 0
            qv, sp, sc = _band_scores(q_ref, k_ref, b_ref, qq, kq, bq, cur, prv, first)
            rows, prow = pl.ds(cur, BLK), pl.ds(prv, BLK)
            lse_v = lse_ref[qq, rows, :]
            pp, pc = jnp.exp(sp - lse_v), jnp.exp(sc - lse_v)
            dov = do_ref[qq, rows, :]
            dob = dov.astype(BF16)
            coef = dlse_ref[qq, rows, :] - jnp.sum(dov * o_ref[qq, rows, :], axis=1, keepdims=True)
            dsp = pp * (_dot(dob, v_ref[kq, prow, :], 1, 1) + coef)
            dsc = pc * (_dot(dob, v_ref[kq, rows, :], 1, 1) + coef)
            dspb, dscb = dsp.astype(BF16), dsc.astype(BF16)
            dq_ref[qq, rows, :] = ((_dot(dspb, k_ref[kq, prow, :], 1, 0) + _dot(dscb, k_ref[kq, rows, :], 1, 0))
                                   * (HEAD_DIM ** -0.5))
            dk_ref[kq, rows, :] += _dot(dscb, qv, 0, 0)
            dkp_ref[kq, prow, :] += _dot(dspb, qv, 0, 0)
            dv_ref[kq, rows, :] += _dot(pc.astype(BF16), dob, 0, 0)
            dvp_ref[kq, prow, :] += _dot(pp.astype(BF16), dob, 0, 0)
            db_ref[bq, :, :BLK] += dsp
            db_ref[bq, :, BLK:] += dsc
            if has_sink:
                dsk_ref[qq] += jnp.sum(jnp.exp(s_ref[qq][:, :1] - lse_v) * coef)

        _band_sweep(length, ns, one)
        dk_ref[...] += dkp_ref[...]
        dv_ref[...] += dvp_ref[...]

    xspecs, qspec, kvspec, bspec, sspec, colspec = _band_specs(length, g, bias_div, offs)
    return pl.pallas_call(
        body, name=name,
        out_shape=[jax.ShapeDtypeStruct((nq, length, HEAD_DIM), F32), jax.ShapeDtypeStruct((nk, length, HEAD_DIM), F32),
                   jax.ShapeDtypeStruct((nk, length, HEAD_DIM), F32), jax.ShapeDtypeStruct((nbias, BLK, 2 * BLK), F32),
                   jax.ShapeDtypeStruct((nq, 1, LANES), F32)],
        grid=(nq // ns,),
        in_specs=xspecs + [bspec, sspec, qspec, colspec, qspec, colspec],
        out_specs=[qspec, kvspec, kvspec, bspec, sspec],
        scratch_shapes=[pltpu.VMEM((ns, length, HEAD_DIM), F32), pltpu.VMEM((ns, length, HEAD_DIM), F32)],
        compiler_params=_params(("arbitrary",)),
    )(x, x, x, bias, sink, o, lse, do, dlse)


TOK_TILE = 512


def _dil_merge(outs, lses, dout, name):
    tr = TOK_TILE
    dils = [d for _, d in DIL_PATTERNS]
    n = len(dils)
    o4 = [o.reshape(2, d, SEQ // d, HEAD_DIM) for o, d in zip(outs, dils)]
    l4 = [l.reshape(2, d, SEQ // d, 1) for l, d in zip(lses, dils)]
    o_specs = [pl.BlockSpec((2, d, tr // d, HEAD_DIM), lambda i: (0, 0, i, 0)) for d in dils]
    l_specs = [pl.BlockSpec((2, d, tr // d, 1), lambda i: (0, 0, i, 0)) for d in dils]
    tok = pl.BlockSpec((tr, 2 * HEAD_DIM), lambda i: (i, 0))
    scratch = ([pltpu.VMEM((tr, 2 * HEAD_DIM), F32) for _ in dils] + [pltpu.VMEM((tr, 1), F32) for _ in range(2 * n)]
               + [pltpu.VMEM((tr // d, 2 * HEAD_DIM), F32) for d in dils])

    def to_tokens(o_ref, l_ref, d, pair, cols, stage):
        for r in range(d):
            rows = pl.ds(r, tr // d, stride=d) if d > 1 else slice(None)
            stage[:, :HEAD_DIM] = o_ref[0, r]
            stage[:, HEAD_DIM:] = o_ref[1, r]
            pair[rows, :] = stage[...]
            for h in range(2):
                cols[h][rows, :] = l_ref[h, r]
        return pair[...], [cols[0][...], cols[1][...]]

    def weights(ls):
        left = lax.broadcasted_iota(jnp.int32, (tr, 2 * HEAD_DIM), 1) < HEAD_DIM
        per_head = []
        for h in range(2):
            m = ls[0][h]
            for g in range(1, n):
                m = jnp.maximum(m, ls[g][h])
            es = [jnp.exp(ls[g][h] - m) for g in range(n)]
            den = es[0]
            for e in es[1:]:
                den = den + e
            per_head.append([e / den for e in es])
        return per_head, [jnp.where(left, per_head[0][g], per_head[1][g]) for g in range(n)], left

    def load(refs):
        pairs, cols, stages = refs[:n], refs[n:3 * n], refs[3 * n:]
        return pairs, [cols[2 * g:2 * g + 2] for g in range(n)], stages

    if dout is None:
        def body(*refs):
            pairs, cols, stages = load(refs[2 * n + 1:])
            toks = [to_tokens(refs[g], refs[n + g], dils[g], pairs[g], cols[g], stages[g]) for g in range(n)]
            _, alphas, _ = weights([t[1] for t in toks])
            acc = alphas[0] * toks[0][0]
            for g in range(1, n):
                acc = acc + alphas[g] * toks[g][0]
            refs[2 * n][...] = acc

        return pl.pallas_call(
            body, name=name, out_shape=jax.ShapeDtypeStruct((SEQ, 2 * HEAD_DIM), F32), grid=(SEQ // tr,),
            in_specs=o_specs + l_specs, out_specs=tok, scratch_shapes=scratch, compiler_params=_params(("parallel",)),
        )(*o4, *l4)

    def body(*refs):
        do_refs, dl_refs = refs[2 * n + 1:3 * n + 1], refs[3 * n + 1:4 * n + 1]
        pairs, cols, stages = load(refs[4 * n + 1:])
        toks = [to_tokens(refs[g], refs[n + g], dils[g], pairs[g], cols[g], stages[g]) for g in range(n)]
        per_head, alphas, left = weights([t[1] for t in toks])
        dov = refs[2 * n][...]
        das = []
        for g in range(n):
            prod = dov * toks[g][0]
            das.append([jnp.sum(jnp.where(left, prod, 0.0), axis=1, keepdims=True),
                        jnp.sum(jnp.where(left, 0.0, prod), axis=1, keepdims=True)])
        dbar = [sum(per_head[h][g] * das[g][h] for g in range(n)) for h in range(2)]
        for g, d in enumerate(dils):
            pairs[g][...] = alphas[g] * dov
            for h in range(2):
                cols[g][h][...] = per_head[h][g] * (das[g][h] - dbar[h])
            for r in range(d):
                rows = pl.ds(r, tr // d, stride=d) if d > 1 else slice(None)
                v = pairs[g][rows, :]
                for h in range(2):
                    do_refs[g][h, r] = v[:, h * HEAD_DIM:(h + 1) * HEAD_DIM]
                    dl_refs[g][h, r] = cols[g][h][rows, :]

    out = pl.pallas_call(
        body, name=name,
        out_shape=[jax.ShapeDtypeStruct(o.shape, F32) for o in o4] + [jax.ShapeDtypeStruct(l.shape, F32) for l in l4],
        grid=(SEQ // tr,), in_specs=o_specs + l_specs + [tok], out_specs=o_specs + l_specs, scratch_shapes=scratch,
        compiler_params=_params(("parallel",)),
    )(*o4, *l4, dout)
    return [t.reshape(s.shape) for t, s in zip(out, list(outs) + list(lses))]


def _tri(cmp):
    r = lax.broadcasted_iota(jnp.int32, (SB_TILE, SB_TILE), 0)
    c = lax.broadcasted_iota(jnp.int32, (SB_TILE, SB_TILE), 1)
    return cmp(r, c).astype(BF16)


def _cum(x, tri, terms):
    acc, rest = None, x
    for _ in range(terms):
        part = rest.astype(BF16)
        rest = rest - part.astype(F32)
        d = _dot(part, tri, 1, 0)
        acc = d if acc is None else acc + d
    return acc


def _sb_logits(q, ks, diagonal):
    t = SB_TILE
    z = _dot(q, ks, 1, 1)
    e = jnp.exp(-jnp.abs(z))
    lf = -(jnp.maximum(z, 0.0) + jnp.log(1.0 + e))
    if not diagonal:
        return z, e, lf, None
    mask = lax.broadcasted_iota(jnp.int32, (t, t), 1) < lax.broadcasted_iota(jnp.int32, (t, t), 0)
    return z, e, jnp.where(mask, lf, 0.0), mask


def _sb_specs(h, s):
    t = SB_TILE
    tile = pl.BlockSpec((h, t, HEAD_DIM), lambda i: (0, i, 0))
    keys = pl.BlockSpec((h, s, HEAD_DIM), lambda i: (1, 0, 0))
    values = pl.BlockSpec((h, s, HEAD_DIM), lambda i: (2, 0, 0))
    return tile, keys, values, pl.BlockSpec((h, t, 1), lambda i: (0, i, 0))


def _sb_fwd(x, name):
    h, s = x.shape[0] // 3, x.shape[1]
    t = SB_TILE

    def body(q_ref, k_ref, v_ref, o_ref, tot_ref):
        i = pl.program_id(0)
        after = _tri(lambda r, c: r > c)

        def tile(j, carry, diagonal):
            rows = pl.ds(pl.multiple_of(j * t, t), t)
            out = []
            for hh, (right, acc) in enumerate(carry):
                z, _, lf, mask = _sb_logits(q_ref[hh], k_ref[hh, rows, :], diagonal)
                w = jnp.exp(z + lf + (right + _cum(lf, after, 2)))
                w = w if mask is None else jnp.where(mask, w, 0.0)
                out.append((right + jnp.sum(lf, axis=1, keepdims=True), acc + _dot(w.astype(BF16), v_ref[hh, rows, :], 1, 0)))
            return tuple(out)

        carry = tile(i, tuple((jnp.zeros((t, 1), F32), jnp.zeros((t, HEAD_DIM), F32)) for _ in range(h)), True)
        carry = lax.fori_loop(0, i, lambda jj, c: tile(i - 1 - jj, c, False), carry)
        for hh, (right, acc) in enumerate(carry):
            o_ref[hh] = acc
            tot_ref[hh] = right

    tile_spec, keys, values, col = _sb_specs(h, s)
    return pl.pallas_call(
        body, name=name, out_shape=[jax.ShapeDtypeStruct((h, s, HEAD_DIM), F32), jax.ShapeDtypeStruct((h, s, 1), F32)],
        grid=(s // t,), in_specs=[tile_spec, keys, values], out_specs=[tile_spec, col],
        compiler_params=_params(("parallel",)),
    )(x, x, x)


def _sb_bwd(x, tot, do, name):
    h, s = x.shape[0] // 3, x.shape[1]
    t = SB_TILE

    def body(q_ref, k_ref, v_ref, tot_ref, do_ref, dq_ref, dk_ref, dv_ref):
        i = pl.program_id(0)

        @pl.when(i == 0)
        def _():
            dk_ref[...] = jnp.zeros_like(dk_ref)
            dv_ref[...] = jnp.zeros_like(dv_ref)

        upto = _tri(lambda r, c: r <= c)
        before = _tri(lambda r, c: r < c)

        def tile(j, carry, diagonal):
            rows = pl.ds(pl.multiple_of(j * t, t), t)
            out = []
            for hh, (left, cleft, dq) in enumerate(carry):
                qv, ks, dob = q_ref[hh], k_ref[hh, rows, :], do_ref[hh].astype(BF16)
                z, e, lf, mask = _sb_logits(qv, ks, diagonal)
                between = tot_ref[hh] - (left + _cum(lf, upto, 2))
                w = jnp.exp(z + lf + between)
                w = w if mask is None else jnp.where(mask, w, 0.0)
                dlog = w * _dot(dob, v_ref[hh, rows, :], 1, 1)
                cfail = cleft + _cum(dlog, before, 2)
                sig = jnp.where(z >= 0.0, 1.0, e) / (1.0 + e)
                dz = dlog * (1.0 - sig) - sig * cfail
                dz = (dz if mask is None else jnp.where(mask, dz, 0.0)).astype(BF16)
                dk_ref[hh, rows, :] += _dot(dz, qv, 0, 0)
                dv_ref[hh, rows, :] += _dot(w.astype(BF16), dob, 0, 0)
                out.append((left + jnp.sum(lf, axis=1, keepdims=True), cleft + jnp.sum(dlog, axis=1, keepdims=True),
                            dq + _dot(dz, ks, 1, 0)))
            return tuple(out)

        zero = jnp.zeros((t, 1), F32)
        carry = lax.fori_loop(0, i, lambda j, c: tile(j, c, False),
                              tuple((zero, zero, jnp.zeros((t, HEAD_DIM), F32)) for _ in range(h)))
        for hh, (_, _, dq) in enumerate(tile(i, carry, True)):
            dq_ref[hh] = dq * (HEAD_DIM ** -0.5)

    tile_spec, keys, values, col = _sb_specs(h, s)
    full = pl.BlockSpec((h, s, HEAD_DIM), lambda i: (0, 0, 0))
    shp = jax.ShapeDtypeStruct((h, s, HEAD_DIM), F32)
    return pl.pallas_call(
        body, name=name, out_shape=[shp, shp, shp], grid=(s // t,),
        in_specs=[tile_spec, keys, values, col, tile_spec],
        out_specs=[tile_spec, full, full], compiler_params=_params(("arbitrary",)),
    )(x, x, x, tot, do)


COL_SB, COL_DIL, COL_SWA = 0, 3 * H_SB * HEAD_DIM, 3 * H_SB * HEAD_DIM + 3 * H_DIL * HEAD_DIM
N_SWA = H_SWA_Q + 2 * H_SWA_KV


def _dil_col(t, g):
    return COL_DIL + t * H_DIL * HEAD_DIM + g * 2 * HEAD_DIM


def _split_heads(qkv, name):
    tr = TOK_TILE
    scale = HEAD_DIM ** -0.5
    dils = [d for _, d in DIL_PATTERNS]

    def body(x_ref, sb_ref, d0_ref, d1_ref, d2_ref, swa_ref, pair):
        def head(col, scaled):
            v = x_ref[:, col:col + HEAD_DIM]
            return (v * scale if scaled else v).astype(BF16)

        for hh in range(3 * H_SB):
            sb_ref[hh] = head(COL_SB + hh * HEAD_DIM, hh < H_SB)
        for hh in range(N_SWA):
            swa_ref[hh] = head(COL_SWA + hh * HEAD_DIM, hh < H_SWA_Q)
        for t in range(3):
            for g, (d, out_ref) in enumerate(zip(dils, (d0_ref, d1_ref, d2_ref))):
                col = _dil_col(t, g)
                if d == 1:
                    for h in range(2):
                        out_ref[t * 2 + h] = head(col + h * HEAD_DIM, t == 0)
                    continue
                pair[...] = x_ref[:, col:col + 2 * HEAD_DIM]
                for r in range(d):
                    v = pair[pl.ds(r, tr // d, stride=d), :]
                    v = v * scale if t == 0 else v
                    for h in range(2):
                        out_ref[t * 2 * d + h * d + r] = v[:, h * HEAD_DIM:(h + 1) * HEAD_DIM].astype(BF16)

    def heads(n, length):
        return jax.ShapeDtypeStruct((n, length, HEAD_DIM), BF16)

    def spec(n, rows):
        return pl.BlockSpec((n, rows, HEAD_DIM), lambda i: (0, i, 0))

    return pl.pallas_call(
        body, name=name,
        out_shape=[heads(3 * H_SB, SEQ)] + [heads(6 * d, SEQ // d) for d in dils] + [heads(N_SWA, SEQ)],
        grid=(SEQ // tr,), in_specs=[pl.BlockSpec((tr, D_QKV), lambda i: (i, 0))],
        out_specs=[spec(3 * H_SB, tr)] + [spec(6 * d, tr // d) for d in dils] + [spec(N_SWA, tr)],
        scratch_shapes=[pltpu.VMEM((tr, 2 * HEAD_DIM), F32)], compiler_params=_params(("parallel",)),
    )(qkv)


def _join_heads(sb, dil, swa, name):
    tr = TOK_TILE
    dils = [d for _, d in DIL_PATTERNS]

    def body(*refs):
        sb_refs, dil_refs, swa_refs = refs[:3], [refs[3 + 3 * g:6 + 3 * g] for g in range(3)], refs[12:15]
        o_ref, pair, stages = refs[15], refs[16], refs[17:]

        def put(col, v):
            o_ref[:, col:col + v.shape[1]] = v.astype(BF16)

        for t in range(3):
            for h in range(H_SB):
                put(COL_SB + (t * H_SB + h) * HEAD_DIM, sb_refs[t][h])
        col = COL_SWA
        for ref in swa_refs:
            for h in range(ref.shape[0]):
                put(col, ref[h])
                col += HEAD_DIM
        for t in range(3):
            for g, d in enumerate(dils):
                ref, col = dil_refs[g][t], _dil_col(t, g)
                if d == 1:
                    for h in range(2):
                        put(col + h * HEAD_DIM, ref[h])
                    continue
                stage = stages[g - 1]
                for r in range(d):
                    stage[:, :HEAD_DIM] = ref[r]
                    stage[:, HEAD_DIM:] = ref[d + r]
                    pair[pl.ds(r, tr // d, stride=d), :] = stage[...]
                put(col, pair[...])

    def spec(n, rows):
        return pl.BlockSpec((n, rows, HEAD_DIM), lambda i: (0, i, 0))

    ins = list(sb) + [t for g in range(3) for t in dil[g]] + list(swa)
    in_specs = ([spec(H_SB, tr)] * 3 + [spec(2 * d, tr // d) for d in dils for _ in range(3)]
                + [spec(H_SWA_Q, tr), spec(H_SWA_KV, tr), spec(H_SWA_KV, tr)])
    return pl.pallas_call(
        body, name=name, out_shape=jax.ShapeDtypeStruct((SEQ, D_QKV), BF16), grid=(SEQ // tr,), in_specs=in_specs,
        out_specs=pl.BlockSpec((tr, D_QKV), lambda i: (i, 0)),
        scratch_shapes=[pltpu.VMEM((tr, 2 * HEAD_DIM), F32)] + [pltpu.VMEM((tr // d, 2 * HEAD_DIM), F32) for d in dils[1:]],
        compiler_params=_params(("parallel",)),
    )(*ins)


def _mixer_fwd(qkv, bias, sinks_l, tag):
    sb, d0, d1, d2, swa = _split_heads(qkv, name=f"split_heads_{tag}")
    st = {"sb": sb, "dil": (d0, d1, d2), "swa": swa}
    o_sb, st["sb_tot"] = _sb_fwd(sb, name=f"sb_fwd_{tag}")
    st["dil_out"], st["dil_lse"], st["dil_sink"] = [], [], []
    for gi, (_, d) in enumerate(DIL_PATTERNS):
        sink = jnp.zeros((2 * d, 1, LANES), F32)
        og, lg = _band_fwd(st["dil"][gi], bias[2 * gi:2 * gi + 2], sink, nq=2 * d, offs=(0, 2 * d, 4 * d), g=1, bias_div=d,
                           has_sink=False, name=f"dil{gi}_fwd_{tag}")
        st["dil_out"].append(og)
        st["dil_lse"].append(lg)
        st["dil_sink"].append(sink)
    o_dil = _dil_merge(st["dil_out"], st["dil_lse"], None, name=f"dil_merge_fwd_{tag}")
    st["swa_sink"] = jnp.broadcast_to(sinks_l.reshape(H_SWA_Q, 1, 1), (H_SWA_Q, 1, LANES))
    st["swa_out"] = _band_fwd(swa, bias[H_DIL:], st["swa_sink"], nq=H_SWA_Q, offs=(0, H_SWA_Q, H_SWA_Q + H_SWA_KV),
                              g=H_SWA_Q // H_SWA_KV, bias_div=1, has_sink=True, name=f"swa_fwd_{tag}")
    return (o_sb, o_dil, st["swa_out"][0]), st


def _mixer_bwd(st, bias, do_sb, do_dil, do_swa, tag):
    d_sb = _sb_bwd(st["sb"], st["sb_tot"], do_sb, name=f"sb_bwd_{tag}")
    dmerge = _dil_merge(st["dil_out"], st["dil_lse"], do_dil, name=f"dil_merge_bwd_{tag}")
    d_dil, dbs = [], []
    for gi, (_, d) in enumerate(DIL_PATTERNS):
        dq, dk, dv, db, _ = _band_bwd(st["dil"][gi], bias[2 * gi:2 * gi + 2], st["dil_sink"][gi], st["dil_out"][gi],
                                      st["dil_lse"][gi], dmerge[gi], dmerge[3 + gi], nq=2 * d, offs=(0, 2 * d, 4 * d),
                                      g=1, bias_div=d, has_sink=False, name=f"dil{gi}_bwd_{tag}")
        d_dil.append((dq, dk, dv))
        dbs.append(db)
    o_sw, l_sw = st["swa_out"]
    dq_sw, dk_sw, dv_sw, db_sw, dsink = _band_bwd(st["swa"], bias[H_DIL:], st["swa_sink"], o_sw, l_sw, do_swa,
                                                  jnp.zeros_like(l_sw), nq=H_SWA_Q, offs=(0, H_SWA_Q, H_SWA_Q + H_SWA_KV),
                                                  g=H_SWA_Q // H_SWA_KV, bias_div=1, has_sink=True, name=f"swa_bwd_{tag}")
    dqkv = _join_heads(d_sb, d_dil, (dq_sw, dk_sw, dv_sw), name=f"join_heads_{tag}")
    return dqkv, jnp.concatenate(dbs + [db_sw], 0), dsink[:, 0, 0]


PIECES = ("ffn0", "mix", "ffn1")


def _ffn_fwd(x_in, w, gain, mod_j, tag, after=None):
    st = {"x": x_in, "w": w}
    st["h"] = _norm_fwd(x_in, _row(gain), _row(mod_j[1]), _row(mod_j[0]), name=f"norm_fwd_{tag}", after=after)
    st["a"], st["u"], st["s"] = _ffn_up(st["h"], w["gate"], w["up"], name=f"up_{tag}")
    st["f"], x_out = _mm(st["s"], w["down"], res=x_in, colscale=_row(0.5 * mod_j[2]), emit_acc=True, tm=512, tn=1024,
                         name=f"down_{tag}")
    return x_out, st


def _ffn_bwd(dx_out, st, gain, mod_j, tag, done, pre, nxt):
    w = st["w"]

    def latest(new, old):
        return old if new is None else new

    df, dgate = pre or _gate_bwd(dx_out, st["f"], _row(0.5 * mod_j[2]), 0.5, name=f"gate_bwd_{tag}")
    dwd = _mm_tn(st["s"], df, tm=D_FF // 2, name=f"dwd_{tag}")
    token = latest(done({"down": dwd}), dwd)
    da, du = _ffn_bwd_ds(df, w["down"], st["a"], st["u"], name=f"ds_{tag}")
    dwg = _mm_tn(da, st["h"], after=token, tm=D_FF // 2, name=f"dwg_{tag}")
    token = latest(done({"gate": dwg}), dwg)
    dwu = _mm_tn(du, st["h"], after=token, tm=D_FF // 2, name=f"dwu_{tag}")
    token = latest(done({"up": dwu}), dwu)
    dx_in, sum_dh, sum_dhx, made = _dh_norm_bwd(da, w["gate"], du, w["up"], st["x"], dx_out, _row(gain), _row(mod_j[1]), nxt,
                                                after=token, name=f"dh_{tag}")
    dmod = jnp.concatenate([sum_dh, gain * sum_dhx, dgate], 0)
    return dx_in, dmod, (1.0 + mod_j[1]) * sum_dhx[0], made


def _mix_fwd(x_in, w, gain, mod_j, bias, sinks_l, tag, after=None):
    st = {"x": x_in, "w": w}
    st["h"] = _norm_fwd(x_in, _row(gain), _row(mod_j[1]), _row(mod_j[0]), name=f"norm_fwd_mix_{tag}", after=after)
    qkv = _mm(st["h"], w["in"], tb=True, tm=SEQ, b_rows=(0, D_QKV), name=f"qkv_{tag}")
    st["gates"] = _mm(st["h"], w["in"], tb=True, tm=SEQ, b_rows=(D_QKV, D_GATES), name=f"gates_{tag}")
    outs, st["mix"] = _mixer_fwd(qkv, bias, sinks_l, tag)
    st["merged"], *st["t"] = _merge_fwd(*outs, st["gates"], w["br_sb"], w["br_dil"], w["br_swa"], name=f"merge_fwd_{tag}")
    st["f"], x_out = _mm(st["merged"], w["out"], res=x_in, colscale=_row(mod_j[2]), emit_acc=True, name=f"out_{tag}")
    return x_out, st


def _mix_bwd(dx_out, st, gain, mod_j, bias, tag, done, pre, nxt):
    w = st["w"]
    df, dgate = pre or _gate_bwd(dx_out, st["f"], _row(mod_j[2]), 1.0, name=f"gate_bwd_mix_{tag}")
    g = {"out": _mm_tn(st["merged"], df, name=f"dw_out_{tag}")}
    dmerged = _mm(df, w["out"], tb=True, name=f"dmerged_{tag}")
    dgates, do_sb, do_dil, do_swa, dbr_sb, dbr_dil, dbr_swa = _merge_bwd(
        dmerged, *st["t"], st["gates"], w["br_sb"], w["br_dil"], w["br_swa"], name=f"merge_bwd_{tag}")
    g["br_sb"] = _mm_tn(st["t"][0], dbr_sb, name=f"dw_br_sb_{tag}")
    g["br_dil"] = _mm_tn(st["t"][1], dbr_dil, name=f"dw_br_dil_{tag}")
    g["br_swa"] = _mm_tn(st["t"][2], dbr_swa, name=f"dw_br_swa_{tag}")
    dqkv, dbias, dsinks = _mixer_bwd(st["mix"], bias, do_sb, do_dil, do_swa, tag)
    dw_qkv = _mm_tn(dqkv, st["h"], out_rows=D_QKV + D_GATES, name=f"dw_qkv_{tag}")
    g["in"] = _mm_tn(dgates, st["h"], out_rows=D_QKV + D_GATES, row0=D_QKV, prev=dw_qkv, name=f"dw_gates_{tag}")
    dx_in, sum_dh, sum_dhx, made = _dh_norm_bwd(dqkv, w["in"], dgates, w["in"], st["x"], dx_out, _row(gain), _row(mod_j[1]),
                                                nxt, after=done(g), b_rows=(0, D_QKV), name=f"dh_mix_{tag}")
    dmod = jnp.concatenate([sum_dh, gain * sum_dhx, dgate], 0)
    return dx_in, dmod, (1.0 + mod_j[1]) * sum_dhx[0], dbias, dsinks, made


def _local_step(x, target, mod, gains, weights_of, rel_bias, sinks, final_gain, grads_done):
    tables = jnp.asarray(_bucket_tables())
    bias = _bias_build(rel_bias, tables, name="bias_build")
    states, h = [], x
    for l in range(DEPTH):
        st = {}
        for j, piece in enumerate(PIECES):
            w, after = weights_of(l, piece, h)
            if piece == "mix":
                h, st[piece] = _mix_fwd(h, w, gains[l, j], mod[l, j], bias, sinks[l], f"l{l}", after)
            else:
                h, st[piece] = _ffn_fwd(h, w, gains[l, j], mod[l, j], f"{piece}_l{l}", after)
        states.append(st)
    loss, dx, dfinal = _final_loss(h, target, _row(final_gain), name="final_loss")
    dmods = [[None] * 3 for _ in range(DEPTH)]
    dgains = [[None] * 3 for _ in range(DEPTH)]
    dsinks = [None] * DEPTH
    dbias, made = None, None
    sweep = [(l, j) for l in reversed(range(DEPTH)) for j in reversed(range(3))]
    for k, (l, j) in enumerate(sweep):
        piece = PIECES[j]
        done = lambda grads, l=l, piece=piece: grads_done(l, piece, grads)
        nxt = None
        if k + 1 < len(sweep):
            nl, nj = sweep[k + 1]
            coef = 1.0 if PIECES[nj] == "mix" else 0.5
            nxt = (states[nl][PIECES[nj]]["f"], _row(coef * mod[nl, nj, 2]), coef)
        if piece == "mix":
            dx, dmods[l][j], dgains[l][j], db, dsinks[l], made = _mix_bwd(
                dx, states[l][piece], gains[l, j], mod[l, j], bias, f"l{l}", done, made, nxt)
            dbias = db if dbias is None else dbias + db
        else:
            dx, dmods[l][j], dgains[l][j], made = _ffn_bwd(
                dx, states[l][piece], gains[l, j], mod[l, j], f"{piece}_l{l}", done, made, nxt)
    drel = _bias_grad(dbias, tables, name="bias_grad")[:, 0, :N_BUCKETS].T
    dmod = jnp.stack([jnp.stack(m) for m in dmods])
    dgain = jnp.stack([jnp.stack(g) for g in dgains])
    return loss, dx, dmod, dgain, dfinal[0], drel, jnp.stack(dsinks)


BR_ROWS = (H_SB * HEAD_DIM, 2 * HEAD_DIM, H_SWA_Q * HEAD_DIM)


def _lanes_unshard(g, lead):
    _, rows, _ = g.shape
    r = rows // lead
    return g.reshape(N_DEV, lead, r, LANES).transpose(1, 2, 0, 3).reshape(lead, r, N_DEV * LANES)


def _lanes_shard(full):
    lead, r, _ = full.shape
    return full.reshape(lead, r, N_DEV, LANES).transpose(2, 0, 1, 3).reshape(N_DEV, lead * r, LANES)


def _pack_rows(parts, dtype):
    flat = jnp.concatenate([p.astype(dtype).reshape(-1) for p in parts])
    pad = (-flat.shape[0]) % (16 * LANES)
    if pad:
        flat = jnp.concatenate([flat, jnp.zeros((pad,), dtype)])
    return flat.reshape(-1, LANES)


def _unshard(gathered, axis):
    moved = jnp.moveaxis(gathered, 0, axis)
    shape = list(moved.shape)
    shape[axis:axis + 2] = [shape[axis] * shape[axis + 1]]
    return moved.reshape(shape)


def kernel(x, c, w_ada, b_ada, norm_gain, w_ffn_gate, w_ffn_up, w_ffn_down, w_in, w_br_sb, w_br_dil, w_br_swa, w_out, sinks, rel_bias, final_gain, loss_target, m_w_ada, m_b_ada, m_norm_gain, m_w_ffn_gate, m_w_ffn_up, m_w_ffn_down, m_w_in, m_w_br_sb, m_w_br_dil, m_w_br_swa, m_w_out, m_sinks, m_rel_bias, m_final_gain, v_w_ada, v_b_ada, v_norm_gain, v_w_ffn_gate, v_w_ffn_up, v_w_ffn_down, v_w_in, v_w_br_sb, v_w_br_dil, v_w_br_swa, v_w_out, v_sinks, v_rel_bias, v_final_gain):
    me = 4 * lax.axis_index("x") + 2 * lax.axis_index("y") + lax.axis_index("c")
    d = D_MODEL
    gate_t, up_t, in_t = jnp.swapaxes(w_ffn_gate, 2, 3), jnp.swapaxes(w_ffn_up, 2, 3), jnp.swapaxes(w_in, 1, 2)

    def piece_shards(l, piece):
        bf = lambda t: t.astype(BF16)
        if piece == "mix":
            return [bf(in_t[l]), jnp.concatenate([bf(w_br_sb[l]), bf(w_br_dil[l]), bf(w_br_swa[l])], 0), bf(w_out[l])]
        i = PIECES.index(piece) // 2
        return [bf(gate_t[l, i]), bf(up_t[l, i]), bf(w_ffn_down[l, i])]

    br_off = np.concatenate([[0], np.cumsum(BR_ROWS)])

    def piece_weights(gathered, piece):
        if piece == "mix":
            g_in, g_br, g_out = gathered
            f_br = [_lanes_unshard(g_br[:, br_off[k]:br_off[k + 1]], 1)[0] for k in range(3)]
            return {"in": g_in.reshape(D_QKV + D_GATES, d), "br_sb": f_br[0], "br_dil": f_br[1], "br_swa": f_br[2],
                    "out": g_out.reshape(d, d)}
        return {n: g.reshape(D_FF, d) for n, g in zip(("gate", "up", "down"), gathered)}

    order = [(l, piece) for l in range(DEPTH) for piece in PIECES]
    ahead = 3
    in_flight, passed = {}, {}

    def start_gather(k, after):
        l, piece = order[k]
        in_flight[k], token = _relay_start(piece_shards(l, piece), after, name=f"gather_{piece}_l{l}_start")
        return token

    small, = _all_gather([_pack_rows([c, norm_gain], F32)], after=start_gather(0, c), name="gather_cond")
    c_all = small[:, :d // LANES].reshape(N_DEV, d)
    gains = _unshard(small[:, d // LANES:d // LANES + 6].reshape(N_DEV, DEPTH, 3, LANES), 2)

    cols = w_ada.shape[2]
    mod_cols = jnp.stack([_ada_fwd(c_all, w_ada[l], name=f"ada_fwd_l{l}") for l in range(DEPTH)])
    mod_all, = _all_gather([_pack_rows([mod_cols], F32)], name="gather_mod")
    mod_all = mod_all.reshape(N_DEV, -1)[:, :DEPTH * N_DEV * cols].reshape(N_DEV, DEPTH, N_DEV, cols)
    mod_mine = lax.dynamic_index_in_dim(mod_all, me, axis=2, keepdims=False)
    mod = (mod_mine.transpose(1, 0, 2).reshape(DEPTH, N_DEV * cols) + b_ada).reshape(DEPTH, 3, 3, d)

    token = mod_all
    for k in range(1, 1 + ahead):
        token = start_gather(k, token)
    mod = mod + token[0, 0]

    def weights_of(l, piece, h):
        k = order.index((l, piece))
        token = start_gather(k + ahead, h) if k + ahead < len(order) and k + ahead not in in_flight else None
        for nxt in ([k] if k < 3 else []) + ([k + 1] if 3 <= k + 1 < len(order) else []):
            nl, npiece = order[nxt]
            passed[nxt], token = _relay_pass(in_flight[nxt], h if token is None else token,
                                             name=f"gather_{npiece}_l{nl}_pass")
        landed = _relay_wait(passed[k], h if token is None else token, name=f"gather_{piece}_l{l}_wait")
        return piece_weights(landed, piece), token

    exchanges, have, deferred = {}, {}, []

    def grads_done(l, piece, g):
        key = (l, piece)
        have.setdefault(key, {}).update(g)
        if piece == "mix":
            if len(have[key]) < 5:
                return None
            g = have[key]
            s_br = jnp.concatenate([_lanes_shard(g[n][None]) for n in ("br_sb", "br_dil", "br_swa")], 1)
            groups = [(("in", "br", "out"), [g["in"].reshape(N_DEV, -1, d), s_br, g["out"].reshape(N_DEV, -1, d)])]
        elif key == order[0]:
            deferred.extend(((n,), [t.reshape(N_DEV, -1, d)]) for n, t in g.items())
            return None
        elif len(have[key]) < 3:
            return None
        else:
            groups = [(("gate", "up", "down"), [have[key][n].reshape(N_DEV, -1, d) for n in ("gate", "up", "down")])]
        token = None
        for names, sg in groups:
            state, token = _exchange_start(sg, None, name=f"exchange_{piece}_l{l}_{names[0]}_start")
            exchanges.setdefault(key, []).append((names, state))
        return token

    loss, dx, dmod, dgains, dfinal, drel, dsinks = _local_step(
        x[0], loss_target[0], mod, gains, weights_of, rel_bias, sinks, final_gain, grads_done)

    flat = lambda t: t.reshape(-1, t.shape[-1])
    transposed = lambda ts: tuple(flat(jnp.swapaxes(t, -1, -2)) for t in ts)
    families = {
        "gate": transposed((w_ffn_gate, m_w_ffn_gate, v_w_ffn_gate)), "up": transposed((w_ffn_up, m_w_ffn_up, v_w_ffn_up)),
        "down": tuple(flat(t) for t in (w_ffn_down, m_w_ffn_down, v_w_ffn_down)),
        "in": transposed((w_in, m_w_in, v_w_in)),
        "br": tuple(flat(jnp.concatenate(ts, 1)) for ts in ((w_br_sb, w_br_dil, w_br_swa), (m_w_br_sb, m_w_br_dil, m_w_br_swa),
                                                            (v_w_br_sb, v_w_br_dil, v_w_br_swa))),
        "out": tuple(flat(t) for t in (w_out, m_w_out, v_w_out))}
    stepped = {}

    def step(keys, after):
        for l, piece in keys:
            for names, ex_state in exchanges[l, piece]:
                landed = _exchange_wait(ex_state, after, name=f"exchange_{piece}_l{l}_{names[0]}_wait")
                after = landed[0]
                for n, group in zip(names, landed):
                    w2, m2, v2 = families[n]
                    rows = group.shape[1]
                    row0 = (2 * l + PIECES.index(piece) // 2) * rows if piece != "mix" else l * rows
                    stepped[n] = _reduce_adamw([group], w2, m2, v2, row0, stepped.get(n), after=after,
                                               name=f"reduce_adamw_{n}_{piece}_l{l}")
                    after = stepped[n][1]
        return after

    small_parts = [dmod, dgains, dfinal, drel.T, dsinks, loss[0, :1]]
    small_sizes = [int(np.prod(p.shape)) for p in small_parts]
    small_all, = _all_gather([_pack_rows(small_parts, F32)], name="gather_small")
    token = small_all
    for names, sg in deferred:
        state, token = _exchange_start(sg, token, name=f"exchange_ffn0_l0_{names[0]}_start")
        exchanges.setdefault(order[0], []).append((names, state))

    after_l1 = step([key for key in reversed(order) if key[0] == 1], token)
    small_sum = _sum_parts(small_all, name="sum_small", after=token).reshape(-1)
    offs = np.concatenate([[0], np.cumsum(small_sizes)])
    g_b_ada = small_sum[offs[0]:offs[1]].reshape(DEPTH, 9 * d)
    g_gain_full = small_sum[offs[1]:offs[2]].reshape(DEPTH, 3, d)
    g_norm_gain = lax.dynamic_slice_in_dim(g_gain_full, me * LANES, LANES, axis=2)
    g_final = small_sum[offs[2]:offs[3]]
    g_rel = small_sum[offs[3]:offs[4]].reshape(N_SOFT, N_BUCKETS).T
    g_sinks = small_sum[offs[4]:offs[5]].reshape(DEPTH, H_SWA_Q)
    loss_total = small_sum[offs[5]]

    dmod_all = small_all.reshape(N_DEV, -1)[:, :DEPTH * 9 * d].reshape(N_DEV, DEPTH, 9 * d)
    dmod_cols = lax.dynamic_slice_in_dim(dmod_all, me * cols, cols, axis=2)
    g_w_ada = jnp.stack([_ada_bwd(c_all.T, dmod_cols[:, l], name=f"ada_bwd_l{l}") for l in range(DEPTH)])

    small_state = {"w_ada": (w_ada, m_w_ada, v_w_ada), "b_ada": (b_ada, m_b_ada, v_b_ada),
                   "norm_gain": (norm_gain, m_norm_gain, v_norm_gain), "sinks": (sinks, m_sinks, v_sinks),
                   "rel_bias": (rel_bias, m_rel_bias, v_rel_bias), "final_gain": (final_gain, m_final_gain, v_final_gain)}
    after = step([order[2], order[1]], after_l1)
    grad, update = {}, {}
    for n, g in (("w_ada", g_w_ada), ("b_ada", g_b_ada), ("norm_gain", g_norm_gain), ("sinks", g_sinks),
                 ("rel_bias", g_rel), ("final_gain", g_final)):
        w, m, v = small_state[n]
        grad[n] = g
        if w.ndim == 1:
            update[n] = tuple(t.reshape(w.shape)
                              for t in _adamw(_row(w), _row(g), _row(m), _row(v), name=f"adamw_{n}", after=after))
        else:
            update[n] = _adamw(w, g, m, v, name=f"adamw_{n}", after=after)
        after = update[n][0]

    step([order[0]], after)

    def unflat(n, like, swapped):
        shape = jnp.swapaxes(like, -1, -2).shape if swapped else like.shape
        out = [t.reshape(shape) for t in stepped[n]]
        return [jnp.swapaxes(t, -1, -2) for t in out] if swapped else out

    results = {"w_ffn_gate": unflat("gate", w_ffn_gate, True), "w_ffn_up": unflat("up", w_ffn_up, True),
               "w_ffn_down": unflat("down", w_ffn_down, False), "w_in": unflat("in", w_in, True),
               "w_out": unflat("out", w_out, False)}
    br = [t.reshape(DEPTH, -1, LANES) for t in stepped["br"]]
    for k, n in enumerate(("w_br_sb", "w_br_dil", "w_br_swa")):
        results[n] = [t[:, br_off[k]:br_off[k + 1]] for t in br]
    for n, (g, dl, nm, nv) in results.items():
        grad[n], update[n] = g, (dl, nm, nv)

    names = ["w_ada", "b_ada", "norm_gain", "w_ffn_gate", "w_ffn_up", "w_ffn_down", "w_in", "w_br_sb", "w_br_dil",
             "w_br_swa", "w_out", "sinks", "rel_bias", "final_gain"]
    return (loss_total, dx[None], *[grad[n] for n in names], *[update[n][0] for n in names],
            *[update[n][1] for n in names], *[update[n][2] for n in names])
```
